```python
import jax, jax.numpy as jnp
from jax import lax
import numpy as np

D_MODEL = 1024
BATCH = 16
SEQ = 2048
DEPTH = 1

HEAD_DIM = 64
ATTN_WIDTH = D_MODEL // 2
CONV_WIDTH = D_MODEL - ATTN_WIDTH
N_Q_HEADS = ATTN_WIDTH // HEAD_DIM
N_KV_HEADS = 2
GQA_GROUP = N_Q_HEADS // N_KV_HEADS
KV_WIDTH = N_KV_HEADS * HEAD_DIM
WINDOW = 128
BLOCK = 128
ROT_DIM = HEAD_DIM // 4
ROPE_THETA = 500000.0
CONV_W = 3
D_FF = 2816
N_MOD = 9
LN_EPS = 1e-5
DN_ALPHA = (2.0 * DEPTH) ** 0.25
DN_BETA = (8.0 * DEPTH) ** -0.25
IN_WIDTH = ATTN_WIDTH + 2 * KV_WIDTH + 3 * CONV_WIDTH

kernel_name = "hybrid_swa_sink_shortconv_macaron_deepnorm_adaln"


def layer_norm(x, g, b):
    xf = x.astype(jnp.float32)
    mu = jnp.mean(xf, axis=-1, keepdims=True)
    var = jnp.mean(jnp.square(xf - mu), axis=-1, keepdims=True)
    y = (xf - mu) * lax.rsqrt(var + LN_EPS) * g.astype(jnp.float32) + b.astype(jnp.float32)
    return y.astype(x.dtype)


def swiglu(h, w_gate_up, w_down):
    gu = h @ w_gate_up
    g, u = jnp.split(gu, 2, axis=-1)
    return (jax.nn.silu(g) * u) @ w_down


def partial_rope(x, positions):
    half = ROT_DIM // 2
    inv_freq = jnp.power(jnp.float32(ROPE_THETA), -jnp.arange(0, ROT_DIM, 2, dtype=jnp.float32) / ROT_DIM)
    ang = positions.astype(jnp.float32)[..., None] * inv_freq
    cos = jnp.cos(ang)[:, :, None, :].astype(x.dtype)
    sin = jnp.sin(ang)[:, :, None, :].astype(x.dtype)
    x1 = x[..., :half]
    x2 = x[..., half:ROT_DIM]
    rest = x[..., ROT_DIM:]
    return jnp.concatenate([x1 * cos - x2 * sin, x2 * cos + x1 * sin, rest], axis=-1)


def sliding_window_sink_attention(q, k, v, sinks):
    bsz, seq = q.shape[0], q.shape[1]
    nb = seq // BLOCK
    qb = q.reshape(bsz, nb, BLOCK, N_KV_HEADS, GQA_GROUP, HEAD_DIM)
    pad = ((0, 0), (BLOCK, 0), (0, 0), (0, 0))
    kp = jnp.pad(k, pad).reshape(bsz, nb + 1, BLOCK, N_KV_HEADS, HEAD_DIM)
    vp = jnp.pad(v, pad).reshape(bsz, nb + 1, BLOCK, N_KV_HEADS, HEAD_DIM)
    kb = jnp.concatenate([kp[:, :-1], kp[:, 1:]], axis=2)
    vb = jnp.concatenate([vp[:, :-1], vp[:, 1:]], axis=2)
    scale = HEAD_DIM ** -0.5
    scores = jnp.einsum('bnqhgd,bnkhd->bnhgqk', qb, kb).astype(jnp.float32) * scale
    blk = jnp.arange(nb)[:, None, None]
    qi = jnp.arange(BLOCK)[None, :, None]
    ki = jnp.arange(2 * BLOCK)[None, None, :]
    diff = qi + BLOCK - ki
    key_pos = (blk - 1) * BLOCK + ki
    valid = (diff >= 0) & (diff < WINDOW) & (key_pos >= 0)
    scores = jnp.where(valid[None, :, None, None], scores, -jnp.inf)
    sink = jnp.broadcast_to(
        sinks.astype(jnp.float32).reshape(N_KV_HEADS, GQA_GROUP)[None, None, :, :, None, None],
        scores.shape[:-1] + (1,))
    probs = jax.nn.softmax(jnp.concatenate([scores, sink], axis=-1), axis=-1)[..., :-1]
    out = jnp.einsum('bnhgqk,bnkhd->bnqhgd', probs.astype(v.dtype), vb)
    return out.reshape(bsz, seq, N_Q_HEADS * HEAD_DIM)


def short_gated_conv(u, b_gate, c_gate, conv_w):
    seq = u.shape[1]
    z = c_gate * u
    zp = jnp.pad(z, ((0, 0), (CONV_W - 1, 0), (0, 0)))
    y = conv_w[0] * zp[:, 0:seq]
    for tap in range(1, CONV_W):
        y = y + conv_w[tap] * zp[:, tap:tap + seq]
    return b_gate * y


def _fwd_setup_inputs(seed: int = 0) -> dict:
    key = jax.random.key(seed)
    ks = jax.random.split(key, 20)
    nrm = lambda k, shape, s: jax.random.normal(k, shape, jnp.float32) * s
    L, D = DEPTH, D_MODEL
    x = jax.random.normal(ks[0], (BATCH, SEQ, D), jnp.float32)
    c = jax.random.normal(ks[1], (BATCH, D), jnp.float32)
    offsets = jax.random.randint(ks[2], (BATCH, 1), 0, 1024, dtype=jnp.int32)
    positions = offsets + jnp.arange(SEQ, dtype=jnp.int32)[None, :]
    return {
        "x": x,
        "c": c,
        "positions": positions,
        "w_ada": nrm(ks[3], (L, D, N_MOD * D), 0.1 * D ** -0.5),
        "b_ada": nrm(ks[4], (L, N_MOD * D), 0.01),
        "ffn1_w_gate_up": nrm(ks[5], (L, D, 2 * D_FF), D ** -0.5),
        "ffn1_w_down": nrm(ks[6], (L, D_FF, D), DN_BETA * D_FF ** -0.5),
        "ln1_g": 1.0 + nrm(ks[7], (L, D), 0.02),
        "ln1_b": nrm(ks[8], (L, D), 0.02),
        "w_in": nrm(ks[9], (L, D, IN_WIDTH), D ** -0.5),
        "conv_w": nrm(ks[10], (L, CONV_W, CONV_WIDTH), CONV_W ** -0.5),
        "attn_sinks": nrm(ks[11], (L, N_Q_HEADS), 1.0),
        "w_out": nrm(ks[12], (L, D, D), DN_BETA * D ** -0.5),
        "ln2_g": 1.0 + nrm(ks[13], (L, D), 0.02),
        "ln2_b": nrm(ks[14], (L, D), 0.02),
        "ffn2_w_gate_up": nrm(ks[15], (L, D, 2 * D_FF), D ** -0.5),
        "ffn2_w_down": nrm(ks[16], (L, D_FF, D), DN_BETA * D_FF ** -0.5),
        "ln3_g": 1.0 + nrm(ks[17], (L, D), 0.02),
        "ln3_b": nrm(ks[18], (L, D), 0.02),
    }


def _fwd_reference(x, c, positions, w_ada, b_ada, ffn1_w_gate_up, ffn1_w_down, ln1_g, ln1_b,
              w_in, conv_w, attn_sinks, w_out, ln2_g, ln2_b,
              ffn2_w_gate_up, ffn2_w_down, ln3_g, ln3_b):
    bsz, seq = x.shape[0], x.shape[1]
    split_at = [ATTN_WIDTH, ATTN_WIDTH + KV_WIDTH, ATTN_WIDTH + 2 * KV_WIDTH,
                ATTN_WIDTH + 2 * KV_WIDTH + CONV_WIDTH, ATTN_WIDTH + 2 * KV_WIDTH + 2 * CONV_WIDTH]
    cond = jax.nn.silu(c)
    for l in range(DEPTH):
        mod = (cond @ w_ada[l] + b_ada[l])[:, None, :]
        sh1, sc1, g1, sh2, sc2, g2, sh3, sc3, g3 = jnp.split(mod, N_MOD, axis=-1)

        h = x * (1 + sc1) + sh1
        x = layer_norm(DN_ALPHA * x + 0.5 * (1 + g1) * swiglu(h, ffn1_w_gate_up[l], ffn1_w_down[l]),
                       ln1_g[l], ln1_b[l])

        h = x * (1 + sc2) + sh2
        proj = h @ w_in[l]
        q, k, v, u, b_gate, c_gate = jnp.split(proj, split_at, axis=-1)
        q = partial_rope(q.reshape(bsz, seq, N_Q_HEADS, HEAD_DIM), positions)
        k = partial_rope(k.reshape(bsz, seq, N_KV_HEADS, HEAD_DIM), positions)
        v = v.reshape(bsz, seq, N_KV_HEADS, HEAD_DIM)
        attn_out = sliding_window_sink_attention(q, k, v, attn_sinks[l])
        conv_out = short_gated_conv(u, b_gate, c_gate, conv_w[l])
        mix = jnp.concatenate([attn_out, conv_out], axis=-1) @ w_out[l]
        x = layer_norm(DN_ALPHA * x + (1 + g2) * mix, ln2_g[l], ln2_b[l])

        h = x * (1 + sc3) + sh3
        x = layer_norm(DN_ALPHA * x + 0.5 * (1 + g3) * swiglu(h, ffn2_w_gate_up[l], ffn2_w_down[l]),
                       ln3_g[l], ln3_b[l])
    return x


import jax as _jax
import jax.numpy as _jnp

TWIN_FORMAT = 'train_step'
FWD_PARAMS = ['x', 'c', 'positions', 'w_ada', 'b_ada', 'ffn1_w_gate_up', 'ffn1_w_down', 'ln1_g', 'ln1_b', 'w_in', 'conv_w', 'attn_sinks', 'w_out', 'ln2_g', 'ln2_b', 'ffn2_w_gate_up', 'ffn2_w_down', 'ln3_g', 'ln3_b']
TWIN_WEIGHTS = ['w_ada', 'b_ada', 'ffn1_w_gate_up', 'ffn1_w_down', 'ln1_g', 'ln1_b', 'w_in', 'conv_w', 'attn_sinks', 'w_out', 'ln2_g', 'ln2_b', 'ffn2_w_gate_up', 'ffn2_w_down', 'ln3_g', 'ln3_b']
TWIN_DIFF_INPUT = 'x'
TWIN_INPUTS = ['x', 'c', 'positions', 'w_ada', 'b_ada', 'ffn1_w_gate_up', 'ffn1_w_down', 'ln1_g', 'ln1_b', 'w_in', 'conv_w', 'attn_sinks', 'w_out', 'ln2_g', 'ln2_b', 'ffn2_w_gate_up', 'ffn2_w_down', 'ln3_g', 'ln3_b', 'loss_target', 'm_w_ada', 'm_b_ada', 'm_ffn1_w_gate_up', 'm_ffn1_w_down', 'm_ln1_g', 'm_ln1_b', 'm_w_in', 'm_conv_w', 'm_attn_sinks', 'm_w_out', 'm_ln2_g', 'm_ln2_b', 'm_ffn2_w_gate_up', 'm_ffn2_w_down', 'm_ln3_g', 'm_ln3_b', 'v_w_ada', 'v_b_ada', 'v_ffn1_w_gate_up', 'v_ffn1_w_down', 'v_ln1_g', 'v_ln1_b', 'v_w_in', 'v_conv_w', 'v_attn_sinks', 'v_w_out', 'v_ln2_g', 'v_ln2_b', 'v_ffn2_w_gate_up', 'v_ffn2_w_down', 'v_ln3_g', 'v_ln3_b']
TWIN_OUTPUTS = ['loss', 'grad_x', 'grad_w_ada', 'grad_b_ada', 'grad_ffn1_w_gate_up', 'grad_ffn1_w_down', 'grad_ln1_g', 'grad_ln1_b', 'grad_w_in', 'grad_conv_w', 'grad_attn_sinks', 'grad_w_out', 'grad_ln2_g', 'grad_ln2_b', 'grad_ffn2_w_gate_up', 'grad_ffn2_w_down', 'grad_ln3_g', 'grad_ln3_b', 'delta_w_ada', 'delta_b_ada', 'delta_ffn1_w_gate_up', 'delta_ffn1_w_down', 'delta_ln1_g', 'delta_ln1_b', 'delta_w_in', 'delta_conv_w', 'delta_attn_sinks', 'delta_w_out', 'delta_ln2_g', 'delta_ln2_b', 'delta_ffn2_w_gate_up', 'delta_ffn2_w_down', 'delta_ln3_g', 'delta_ln3_b', 'new_m_w_ada', 'new_m_b_ada', 'new_m_ffn1_w_gate_up', 'new_m_ffn1_w_down', 'new_m_ln1_g', 'new_m_ln1_b', 'new_m_w_in', 'new_m_conv_w', 'new_m_attn_sinks', 'new_m_w_out', 'new_m_ln2_g', 'new_m_ln2_b', 'new_m_ffn2_w_gate_up', 'new_m_ffn2_w_down', 'new_m_ln3_g', 'new_m_ln3_b', 'new_v_w_ada', 'new_v_b_ada', 'new_v_ffn1_w_gate_up', 'new_v_ffn1_w_down', 'new_v_ln1_g', 'new_v_ln1_b', 'new_v_w_in', 'new_v_conv_w', 'new_v_attn_sinks', 'new_v_w_out', 'new_v_ln2_g', 'new_v_ln2_b', 'new_v_ffn2_w_gate_up', 'new_v_ffn2_w_down', 'new_v_ln3_g', 'new_v_ln3_b']
TWIN_LEAF_KINDS = {'loss': 'loss', 'grad_x': 'grad_x', 'grad_w_ada': 'grad_w', 'grad_b_ada': 'grad_w', 'grad_ffn1_w_gate_up': 'grad_w', 'grad_ffn1_w_down': 'grad_w', 'grad_ln1_g': 'grad_w', 'grad_ln1_b': 'grad_w', 'grad_w_in': 'grad_w', 'grad_conv_w': 'grad_w', 'grad_attn_sinks': 'grad_w', 'grad_w_out': 'grad_w', 'grad_ln2_g': 'grad_w', 'grad_ln2_b': 'grad_w', 'grad_ffn2_w_gate_up': 'grad_w', 'grad_ffn2_w_down': 'grad_w', 'grad_ln3_g': 'grad_w', 'grad_ln3_b': 'grad_w', 'delta_w_ada': 'delta_w', 'delta_b_ada': 'delta_w', 'delta_ffn1_w_gate_up': 'delta_w', 'delta_ffn1_w_down': 'delta_w', 'delta_ln1_g': 'delta_w', 'delta_ln1_b': 'delta_w', 'delta_w_in': 'delta_w', 'delta_conv_w': 'delta_w', 'delta_attn_sinks': 'delta_w', 'delta_w_out': 'delta_w', 'delta_ln2_g': 'delta_w', 'delta_ln2_b': 'delta_w', 'delta_ffn2_w_gate_up': 'delta_w', 'delta_ffn2_w_down': 'delta_w', 'delta_ln3_g': 'delta_w', 'delta_ln3_b': 'delta_w', 'new_m_w_ada': 'new_m', 'new_m_b_ada': 'new_m', 'new_m_ffn1_w_gate_up': 'new_m', 'new_m_ffn1_w_down': 'new_m', 'new_m_ln1_g': 'new_m', 'new_m_ln1_b': 'new_m', 'new_m_w_in': 'new_m', 'new_m_conv_w': 'new_m', 'new_m_attn_sinks': 'new_m', 'new_m_w_out': 'new_m', 'new_m_ln2_g': 'new_m', 'new_m_ln2_b': 'new_m', 'new_m_ffn2_w_gate_up': 'new_m', 'new_m_ffn2_w_down': 'new_m', 'new_m_ln3_g': 'new_m', 'new_m_ln3_b': 'new_m', 'new_v_w_ada': 'new_v', 'new_v_b_ada': 'new_v', 'new_v_ffn1_w_gate_up': 'new_v', 'new_v_ffn1_w_down': 'new_v', 'new_v_ln1_g': 'new_v', 'new_v_ln1_b': 'new_v', 'new_v_w_in': 'new_v', 'new_v_conv_w': 'new_v', 'new_v_attn_sinks': 'new_v', 'new_v_w_out': 'new_v', 'new_v_ln2_g': 'new_v', 'new_v_ln2_b': 'new_v', 'new_v_ffn2_w_gate_up': 'new_v', 'new_v_ffn2_w_down': 'new_v', 'new_v_ln3_g': 'new_v', 'new_v_ln3_b': 'new_v'}


def _forward(args):
    return _fwd_reference(*[args[k] for k in FWD_PARAMS])


def _output_shape():
    out = _jax.eval_shape(lambda: _forward(_fwd_setup_inputs(0)))
    return out.shape, out.dtype

N_MICROBATCH = 1
ADAM_LR = 0.001
ADAM_B1 = 0.9
ADAM_B2 = 0.999
ADAM_EPS = 1e-08
ADAM_WD = 0.01
ADAM_STEP = 10
PER_EXAMPLE_BATCH_AXIS = {'x': 0, 'c': 0, 'positions': 0, 'loss_target': 0}
SHARED_INPUTS = []
_WEIGHT_DTYPES = {'w_ada': _jnp.float32, 'b_ada': _jnp.float32, 'ffn1_w_gate_up': _jnp.float32, 'ffn1_w_down': _jnp.float32, 'ln1_g': _jnp.float32, 'ln1_b': _jnp.float32, 'w_in': _jnp.float32, 'conv_w': _jnp.float32, 'attn_sinks': _jnp.float32, 'w_out': _jnp.float32, 'ln2_g': _jnp.float32, 'ln2_b': _jnp.float32, 'ffn2_w_gate_up': _jnp.float32, 'ffn2_w_down': _jnp.float32, 'ln3_g': _jnp.float32, 'ln3_b': _jnp.float32}
MOMENT_SCALE = {'w_ada': 4.328922e-02, 'b_ada': 7.845390e-02, 'ffn1_w_gate_up': 1.802523e-02, 'ffn1_w_down': 4.950756e-02, 'ln1_g': 8.372484e-01, 'ln1_b': 4.238668e-01, 'w_in': 7.010049e-02, 'conv_w': 8.652634e-02, 'attn_sinks': 1.502222e-02, 'w_out': 1.030868e-01, 'ln2_g': 1.014116e+00, 'ln2_b': 4.497578e-01, 'ffn2_w_gate_up': 1.619725e-02, 'ffn2_w_down': 4.448512e-02, 'ln3_g': 3.204666e+01, 'ln3_b': 8.520681e-01}


def _to_microbatches(a, axis):
    t = _jnp.moveaxis(a, axis, 0)
    t = t.reshape((N_MICROBATCH, t.shape[0] // N_MICROBATCH) + t.shape[1:])
    return _jnp.moveaxis(t, 1, axis + 1)


def setup_inputs(seed: int = 0) -> dict:
    inp = _fwd_setup_inputs(seed)
    key = _jax.random.fold_in(_jax.random.key(seed), 7919)
    shape, _ = _output_shape()
    out = dict(inp)
    out["loss_target"] = _jax.random.normal(_jax.random.fold_in(key, 0), shape, _jnp.float32)
    for i, name in enumerate(TWIN_WEIGHTS):
        w = inp[name].astype(_jnp.float32)
        if MOMENT_SCALE is None:
            s = _jnp.sqrt(_jnp.mean(_jnp.square(w)) + 1e-30)
        else:
            s = MOMENT_SCALE[name]
        km, kv = _jax.random.split(_jax.random.fold_in(key, i + 1))
        out[name] = w
        out["m_" + name] = s * _jax.random.normal(km, w.shape, _jnp.float32)
        out["v_" + name] = (s * s) * _jax.random.uniform(kv, w.shape, _jnp.float32, 0.5, 1.5)
    if N_MICROBATCH > 1:
        for name, axis in PER_EXAMPLE_BATCH_AXIS.items():
            out[name] = _to_microbatches(out[name], axis)
    return {'x': out['x'], 'c': out['c'], 'positions': out['positions'], 'w_ada': out['w_ada'], 'b_ada': out['b_ada'], 'ffn1_w_gate_up': out['ffn1_w_gate_up'], 'ffn1_w_down': out['ffn1_w_down'], 'ln1_g': out['ln1_g'], 'ln1_b': out['ln1_b'], 'w_in': out['w_in'], 'conv_w': out['conv_w'], 'attn_sinks': out['attn_sinks'], 'w_out': out['w_out'], 'ln2_g': out['ln2_g'], 'ln2_b': out['ln2_b'], 'ffn2_w_gate_up': out['ffn2_w_gate_up'], 'ffn2_w_down': out['ffn2_w_down'], 'ln3_g': out['ln3_g'], 'ln3_b': out['ln3_b'], 'loss_target': out['loss_target'], 'm_w_ada': out['m_w_ada'], 'm_b_ada': out['m_b_ada'], 'm_ffn1_w_gate_up': out['m_ffn1_w_gate_up'], 'm_ffn1_w_down': out['m_ffn1_w_down'], 'm_ln1_g': out['m_ln1_g'], 'm_ln1_b': out['m_ln1_b'], 'm_w_in': out['m_w_in'], 'm_conv_w': out['m_conv_w'], 'm_attn_sinks': out['m_attn_sinks'], 'm_w_out': out['m_w_out'], 'm_ln2_g': out['m_ln2_g'], 'm_ln2_b': out['m_ln2_b'], 'm_ffn2_w_gate_up': out['m_ffn2_w_gate_up'], 'm_ffn2_w_down': out['m_ffn2_w_down'], 'm_ln3_g': out['m_ln3_g'], 'm_ln3_b': out['m_ln3_b'], 'v_w_ada': out['v_w_ada'], 'v_b_ada': out['v_b_ada'], 'v_ffn1_w_gate_up': out['v_ffn1_w_gate_up'], 'v_ffn1_w_down': out['v_ffn1_w_down'], 'v_ln1_g': out['v_ln1_g'], 'v_ln1_b': out['v_ln1_b'], 'v_w_in': out['v_w_in'], 'v_conv_w': out['v_conv_w'], 'v_attn_sinks': out['v_attn_sinks'], 'v_w_out': out['v_w_out'], 'v_ln2_g': out['v_ln2_g'], 'v_ln2_b': out['v_ln2_b'], 'v_ffn2_w_gate_up': out['v_ffn2_w_gate_up'], 'v_ffn2_w_down': out['v_ffn2_w_down'], 'v_ln3_g': out['v_ln3_g'], 'v_ln3_b': out['v_ln3_b']}


def _loss(weights, diff, rest, loss_target):
    with _jax.named_scope("forward"):
        args = {**rest, TWIN_DIFF_INPUT: diff, **{k: w.astype(_WEIGHT_DTYPES[k]) for k, w in weights.items()}}
        y = _forward(args)
    with _jax.named_scope("loss_head"):
        err = _jnp.square(y.astype(_jnp.float32) - loss_target)
        return 0.5 * _jnp.sum(_jnp.mean(err, axis=-1)) if err.ndim else 0.5 * err


def _adamw(w, g, m, v):
    m = ADAM_B1 * m + (1.0 - ADAM_B1) * g
    v = ADAM_B2 * v + (1.0 - ADAM_B2) * _jnp.square(g)
    m_hat = m / (1.0 - ADAM_B1 ** ADAM_STEP)
    v_hat = v / (1.0 - ADAM_B2 ** ADAM_STEP)
    delta = -ADAM_LR * (m_hat / (_jnp.sqrt(v_hat) + ADAM_EPS) + ADAM_WD * w)
    return delta, m, v


def reference(x, c, positions, w_ada, b_ada, ffn1_w_gate_up, ffn1_w_down, ln1_g, ln1_b, w_in, conv_w, attn_sinks, w_out, ln2_g, ln2_b, ffn2_w_gate_up, ffn2_w_down, ln3_g, ln3_b, loss_target, m_w_ada, m_b_ada, m_ffn1_w_gate_up, m_ffn1_w_down, m_ln1_g, m_ln1_b, m_w_in, m_conv_w, m_attn_sinks, m_w_out, m_ln2_g, m_ln2_b, m_ffn2_w_gate_up, m_ffn2_w_down, m_ln3_g, m_ln3_b, v_w_ada, v_b_ada, v_ffn1_w_gate_up, v_ffn1_w_down, v_ln1_g, v_ln1_b, v_w_in, v_conv_w, v_attn_sinks, v_w_out, v_ln2_g, v_ln2_b, v_ffn2_w_gate_up, v_ffn2_w_down, v_ln3_g, v_ln3_b):
    given = dict(x=x, c=c, positions=positions, w_ada=w_ada, b_ada=b_ada, ffn1_w_gate_up=ffn1_w_gate_up, ffn1_w_down=ffn1_w_down, ln1_g=ln1_g, ln1_b=ln1_b, w_in=w_in, conv_w=conv_w, attn_sinks=attn_sinks, w_out=w_out, ln2_g=ln2_g, ln2_b=ln2_b, ffn2_w_gate_up=ffn2_w_gate_up, ffn2_w_down=ffn2_w_down, ln3_g=ln3_g, ln3_b=ln3_b, loss_target=loss_target, m_w_ada=m_w_ada, m_b_ada=m_b_ada, m_ffn1_w_gate_up=m_ffn1_w_gate_up, m_ffn1_w_down=m_ffn1_w_down, m_ln1_g=m_ln1_g, m_ln1_b=m_ln1_b, m_w_in=m_w_in, m_conv_w=m_conv_w, m_attn_sinks=m_attn_sinks, m_w_out=m_w_out, m_ln2_g=m_ln2_g, m_ln2_b=m_ln2_b, m_ffn2_w_gate_up=m_ffn2_w_gate_up, m_ffn2_w_down=m_ffn2_w_down, m_ln3_g=m_ln3_g, m_ln3_b=m_ln3_b, v_w_ada=v_w_ada, v_b_ada=v_b_ada, v_ffn1_w_gate_up=v_ffn1_w_gate_up, v_ffn1_w_down=v_ffn1_w_down, v_ln1_g=v_ln1_g, v_ln1_b=v_ln1_b, v_w_in=v_w_in, v_conv_w=v_conv_w, v_attn_sinks=v_attn_sinks, v_w_out=v_w_out, v_ln2_g=v_ln2_g, v_ln2_b=v_ln2_b, v_ffn2_w_gate_up=v_ffn2_w_gate_up, v_ffn2_w_down=v_ffn2_w_down, v_ln3_g=v_ln3_g, v_ln3_b=v_ln3_b)
    weights = {n: given[n] for n in TWIN_WEIGHTS}
    shared = {n: given[n] for n in SHARED_INPUTS}
    per_example = {n: given[n] for n in ['x', 'c', 'positions']}
    grad_fn = _jax.value_and_grad(_loss, argnums=(0, 1))

    def one_microbatch(ex, loss_target):
        ex = dict(ex)
        diff = ex.pop(TWIN_DIFF_INPUT)
        return grad_fn(weights, diff, {**shared, **ex}, loss_target)

    if N_MICROBATCH == 1:
        loss, (grad_w, grad_x) = one_microbatch(per_example, given["loss_target"])
    else:
        def body(carry, xs):
            loss_sum, grad_sum = carry
            l_k, (gw_k, gx_k) = one_microbatch(xs[0], xs[1])
            with _jax.named_scope("update"):
                return (loss_sum + l_k, _jax.tree.map(_jnp.add, grad_sum, gw_k)), gx_k

        init = (_jnp.zeros((), _jnp.float32), _jax.tree.map(_jnp.zeros_like, weights))
        (loss, grad_w), grad_x = _jax.lax.scan(body, init, (per_example, given["loss_target"]))
    with _jax.named_scope("update"):
        delta_w, new_m, new_v = {}, {}, {}
        for n in TWIN_WEIGHTS:
            delta_w[n], new_m[n], new_v[n] = _adamw(weights[n], grad_w[n], given["m_" + n], given["v_" + n])
    return (loss, grad_x, *[grad_w[n] for n in TWIN_WEIGHTS], *[delta_w[n] for n in TWIN_WEIGHTS],
            *[new_m[n] for n in TWIN_WEIGHTS], *[new_v[n] for n in TWIN_WEIGHTS])
```

```python
import functools

import jax
import jax.numpy as jnp
from jax import lax
from jax.experimental import pallas as pl
from jax.experimental.pallas import tpu as pltpu

F32 = jnp.float32
BF16 = jnp.bfloat16
MESH = pl.DeviceIdType.MESH

D_MODEL = 1024
HEAD_DIM = 64
ATTN_WIDTH = 512
CONV_WIDTH = 512
N_Q_HEADS = 8
N_KV_HEADS = 2
GQA_GROUP = 4
KV_WIDTH = 128
WINDOW = 128
BLOCK = 128
ROT_DIM = 16
ROPE_THETA = 500000.0
N_MOD = 9
LN_EPS = 1e-5
DN_ALPHA = 2.0 ** 0.25
IN_WIDTH = 2304
N_CHIPS = 4
N_DEV = 8
SMALL_ROWS = 32

ADAM_LR = 0.001
ADAM_B1 = 0.9
ADAM_B2 = 0.999
ADAM_EPS = 1e-08
ADAM_WD = 0.01
ADAM_STEP = 10

LANE = 128
COL_CHUNK = 256
VMEM_LIMIT = 56 * 1024 * 1024


def _params(sem=None, vmem=True):
    return pltpu.CompilerParams(dimension_semantics=sem, vmem_limit_bytes=VMEM_LIMIT if vmem else None)


def _sigmoid(g):
    return 1.0 / (1.0 + jnp.exp(-g))


def _row_sum(v):
    return jnp.sum(v, axis=0, keepdims=True)


def _ln_stats(r):
    mu = jnp.mean(r, axis=-1, keepdims=True)
    rc = r - mu
    var = jnp.mean(rc * rc, axis=-1, keepdims=True)
    rstd = lax.rsqrt(var + LN_EPS)
    return rc * rstd, rstd


def _ln_bwd(dxo, xhat, rstd, g):
    dxhat = dxo * g
    m1 = jnp.mean(dxhat, axis=-1, keepdims=True)
    m2 = jnp.mean(dxhat * xhat, axis=-1, keepdims=True)
    return rstd * (dxhat - m1 - xhat * m2)


def _dot_nt(a, b):
    return lax.dot_general(a, b, (((1,), (1,)), ((), ())), preferred_element_type=F32)


def _dot_tn(a, b):
    return lax.dot_general(a, b, (((0,), (0,)), ((), ())), preferred_element_type=F32)


def _full(shape):
    nd = len(shape)
    return pl.BlockSpec(shape, lambda *_: (0,) * nd)


def _resident(shape):
    nd = len(shape)
    return pl.BlockSpec(shape, lambda *_: (0,) * nd, pipeline_mode=pl.Buffered(1))


def _ffn_up(xin, lnp, mod, w, *, seq, sc_idx, sh_idx, use_ln, name):
    t, d = xin.shape
    f = w.shape[1] // 2
    tm = min(512, seq)
    tpb = seq // tm
    ch = min(COL_CHUNK, f)

    def body(x_ref, ln_ref, mod_ref, w_ref, h_ref, a_ref, gu_ref):
        x = x_ref[...]
        if use_ln:
            x = x * ln_ref[0:1, :] + ln_ref[1:2, :]
        h = x * (1.0 + mod_ref[0, sc_idx:sc_idx + 1, :]) + mod_ref[0, sh_idx:sh_idx + 1, :]
        hb = h.astype(BF16)
        h_ref[...] = hb
        for j in range(f // ch):
            g = jnp.dot(hb, w_ref[:, j * ch:(j + 1) * ch], preferred_element_type=F32)
            u = jnp.dot(hb, w_ref[:, f + j * ch:f + (j + 1) * ch], preferred_element_type=F32)
            a_ref[:, j * ch:(j + 1) * ch] = (g * _sigmoid(g) * u).astype(BF16)
            gu_ref[:, j * ch:(j + 1) * ch] = g.astype(BF16)
            gu_ref[:, f + j * ch:f + (j + 1) * ch] = u.astype(BF16)

    return pl.pallas_call(
        body, name=name, grid=(t // tm,),
        in_specs=[pl.BlockSpec((tm, d), lambda i: (i, 0)), _full((2, d)),
                  pl.BlockSpec((1, N_MOD, d), lambda i: (i // tpb, 0, 0)), _resident((d, 2 * f))],
        out_specs=[pl.BlockSpec((tm, d), lambda i: (i, 0)), pl.BlockSpec((tm, f), lambda i: (i, 0)),
                   pl.BlockSpec((tm, 2 * f), lambda i: (i, 0))],
        out_shape=[jax.ShapeDtypeStruct((t, d), BF16), jax.ShapeDtypeStruct((t, f), BF16),
                   jax.ShapeDtypeStruct((t, 2 * f), BF16)],
        compiler_params=_params(("arbitrary",)),
    )(xin, lnp, mod, w)


def _ffn_down_ln(a, wd, xin, lnp_in, mod, *, seq, gate_idx, use_ln, name):
    t, f = a.shape
    d = wd.shape[1]
    tm = min(512, seq)
    tpb = seq // tm

    def body(a_ref, wd_ref, x_ref, ln_ref, mod_ref, f_ref, xhat_ref, rstd_ref):
        fo = jnp.dot(a_ref[...], wd_ref[...], preferred_element_type=F32)
        x = x_ref[...]
        if use_ln:
            x = x * ln_ref[0:1, :] + ln_ref[1:2, :]
        r = DN_ALPHA * x + 0.5 * (1.0 + mod_ref[0, gate_idx:gate_idx + 1, :]) * fo
        xhat, rstd = _ln_stats(r)
        f_ref[...] = fo.astype(BF16)
        xhat_ref[...] = xhat
        rstd_ref[...] = rstd

    return pl.pallas_call(
        body, name=name, grid=(t // tm,),
        in_specs=[pl.BlockSpec((tm, f), lambda i: (i, 0)), _resident((f, d)),
                  pl.BlockSpec((tm, d), lambda i: (i, 0)), _full((2, d)),
                  pl.BlockSpec((1, N_MOD, d), lambda i: (i // tpb, 0, 0))],
        out_specs=[pl.BlockSpec((tm, d), lambda i: (i, 0)), pl.BlockSpec((tm, d), lambda i: (i, 0)),
                   pl.BlockSpec((tm, 1), lambda i: (i, 0))],
        out_shape=[jax.ShapeDtypeStruct((t, d), BF16), jax.ShapeDtypeStruct((t, d), F32),
                   jax.ShapeDtypeStruct((t, 1), F32)],
        compiler_params=_params(("arbitrary",)),
    )(a, wd, xin, lnp_in, mod)


def _ffn_down_loss(a, wd, xhat_in, lnp_in, mod, lnp_out, tgt, *, seq, gate_idx, name):
    t, f = a.shape
    d = wd.shape[1]
    nb = t // seq
    tm = min(512, seq)
    tpb = seq // tm

    def body(a_ref, wd_ref, x_ref, lnin_ref, mod_ref, lnout_ref, tgt_ref,
             dr_ref, df_ref, loss_ref, dg_ref, db_ref, dgate_ref):
        i = pl.program_id(0)
        fo = jnp.dot(a_ref[...], wd_ref[...], preferred_element_type=F32)
        x = x_ref[...] * lnin_ref[0:1, :] + lnin_ref[1:2, :]
        gate = mod_ref[0, gate_idx:gate_idx + 1, :]
        r = DN_ALPHA * x + 0.5 * (1.0 + gate) * fo
        xhat, rstd = _ln_stats(r)
        g_out = lnout_ref[0:1, :]
        y = xhat * g_out + lnout_ref[1:2, :]
        e = y - tgt_ref[...]
        dy = e * (1.0 / d)

        @pl.when(i == 0)
        def _():
            loss_ref[...] = jnp.zeros_like(loss_ref)
            dg_ref[...] = jnp.zeros_like(dg_ref)
            db_ref[...] = jnp.zeros_like(db_ref)

        @pl.when(i % tpb == 0)
        def _():
            dgate_ref[...] = jnp.zeros_like(dgate_ref)

        loss_ref[...] += _row_sum(e * e)
        dg_ref[...] += _row_sum(dy * xhat)
        db_ref[...] += _row_sum(dy)
        dr = _ln_bwd(dy, xhat, rstd, g_out)
        dgate_ref[0] += _row_sum(0.5 * fo * dr)
        dr_ref[...] = dr
        df_ref[...] = (0.5 * (1.0 + gate) * dr).astype(BF16)

    return pl.pallas_call(
        body, name=name, grid=(t // tm,),
        in_specs=[pl.BlockSpec((tm, f), lambda i: (i, 0)), _resident((f, d)),
                  pl.BlockSpec((tm, d), lambda i: (i, 0)), _full((2, d)),
                  pl.BlockSpec((1, N_MOD, d), lambda i: (i // tpb, 0, 0)), _full((2, d)),
                  pl.BlockSpec((tm, d), lambda i: (i, 0))],
        out_specs=[pl.BlockSpec((tm, d), lambda i: (i, 0)), pl.BlockSpec((tm, d), lambda i: (i, 0)),
                   _full((1, d)), _full((1, d)), _full((1, d)),
                   pl.BlockSpec((1, 1, d), lambda i: (i // tpb, 0, 0))],
        out_shape=[jax.ShapeDtypeStruct((t, d), F32), jax.ShapeDtypeStruct((t, d), BF16),
                   jax.ShapeDtypeStruct((1, d), F32), jax.ShapeDtypeStruct((1, d), F32),
                   jax.ShapeDtypeStruct((1, d), F32), jax.ShapeDtypeStruct((nb, 1, d), F32)],
        compiler_params=_params(("arbitrary",)),
    )(a, wd, xhat_in, lnp_in, mod, lnp_out, tgt)


def _rope(v, cos, sa, sb):
    return v * cos + pltpu.roll(v, LANE - ROT_DIM // 2, 1) * sa + pltpu.roll(v, ROT_DIM // 2, 1) * sb


def _rope_t(dy, cos, sa, sb):
    return dy * cos + pltpu.roll(dy * sa, ROT_DIM // 2, 1) + pltpu.roll(dy * sb, LANE - ROT_DIM // 2, 1)


def _in_proj(xhat, lnp, mod, w_t, cos, sa, sb, *, seq, sc_idx, sh_idx, name):
    t, d = xhat.shape
    tm = min(512, seq)
    tpb = seq // tm
    n_conv = 3 * CONV_WIDTH

    def body(x_ref, ln_ref, mod_ref, w_ref, cos_ref, sa_ref, sb_ref, h_ref, q_ref, k_ref, v_ref, ubc_ref):
        x = x_ref[...] * ln_ref[0:1, :] + ln_ref[1:2, :]
        h = x * (1.0 + mod_ref[0, sc_idx:sc_idx + 1, :]) + mod_ref[0, sh_idx:sh_idx + 1, :]
        hb = h.astype(BF16)
        h_ref[...] = hb
        cos_t, sa_t, sb_t = cos_ref[...], sa_ref[...], sb_ref[...]
        for j in range(ATTN_WIDTH // LANE):
            p = _dot_nt(hb, w_ref[j * LANE:(j + 1) * LANE, :])
            q_ref[:, j * LANE:(j + 1) * LANE] = _rope(p, cos_t, sa_t, sb_t).astype(BF16)
        p = _dot_nt(hb, w_ref[ATTN_WIDTH:ATTN_WIDTH + KV_WIDTH, :])
        k_ref[...] = _rope(p, cos_t, sa_t, sb_t).astype(BF16)
        p = _dot_nt(hb, w_ref[ATTN_WIDTH + KV_WIDTH:ATTN_WIDTH + 2 * KV_WIDTH, :])
        v_ref[...] = p.astype(BF16)
        base = ATTN_WIDTH + 2 * KV_WIDTH
        for j in range(n_conv // COL_CHUNK):
            ubc_ref[:, j * COL_CHUNK:(j + 1) * COL_CHUNK] = _dot_nt(
                hb, w_ref[base + j * COL_CHUNK:base + (j + 1) * COL_CHUNK, :])

    row = lambda w: pl.BlockSpec((tm, w), lambda i: (i, 0))
    return pl.pallas_call(
        body, name=name, grid=(t // tm,),
        in_specs=[row(d), _full((2, d)), pl.BlockSpec((1, N_MOD, d), lambda i: (i // tpb, 0, 0)),
                  _resident((IN_WIDTH, d)), row(LANE), row(LANE), row(LANE)],
        out_specs=[row(d), row(ATTN_WIDTH), row(KV_WIDTH), row(KV_WIDTH), row(n_conv)],
        out_shape=[jax.ShapeDtypeStruct((t, d), BF16), jax.ShapeDtypeStruct((t, ATTN_WIDTH), BF16),
                   jax.ShapeDtypeStruct((t, KV_WIDTH), BF16), jax.ShapeDtypeStruct((t, KV_WIDTH), BF16),
                   jax.ShapeDtypeStruct((t, n_conv), F32)],
        compiler_params=_params(("arbitrary",)),
    )(xhat, lnp, mod, w_t, cos, sa, sb)


def _attn_group(q_ref, kp_ref, kc_ref, vp_ref, vc_ref, sink_ref, g, first):
    lo, hi = g * HEAD_DIM, (g + 1) * HEAD_DIM
    kk = jnp.concatenate([kp_ref[:, lo:hi], kc_ref[:, lo:hi]], axis=0)
    vv = jnp.concatenate([vp_ref[:, lo:hi], vc_ref[:, lo:hi]], axis=0)
    qs = jnp.concatenate([q_ref[:, (GQA_GROUP * g + j) * HEAD_DIM:(GQA_GROUP * g + j + 1) * HEAD_DIM]
                          for j in range(GQA_GROUP)], axis=0)
    rows = GQA_GROUP * BLOCK
    row = lax.broadcasted_iota(jnp.int32, (rows, 2 * BLOCK), 0)
    ki = lax.broadcasted_iota(jnp.int32, (rows, 2 * BLOCK), 1)
    diff = (row & (BLOCK - 1)) + BLOCK - ki
    valid = (diff >= 0) & (diff < WINDOW) & ((ki >= BLOCK) | jnp.logical_not(first))
    s = _dot_nt(qs, kk) * (HEAD_DIM ** -0.5)
    s = jnp.where(valid, s, -1e30)
    rcol = lax.broadcasted_iota(jnp.int32, (rows, 1), 0)
    sink = jnp.zeros((rows, 1), F32)
    for j in range(GQA_GROUP):
        sink = jnp.where(rcol // BLOCK == j, sink_ref[GQA_GROUP * g + j], sink)
    m = jnp.maximum(jnp.max(s, axis=1, keepdims=True), sink)
    p = jnp.exp(s - m)
    ps = jnp.exp(sink - m)
    inv = 1.0 / (jnp.sum(p, axis=1, keepdims=True) + ps)
    return qs, kk, vv, p * inv, ps * inv


def _attention(q, k, v, sinks, *, seq, name):
    t = q.shape[0]
    nblk = seq // BLOCK

    def body(q_ref, kp_ref, kc_ref, vp_ref, vc_ref, sink_ref, o_ref):
        first = (pl.program_id(0) % nblk) == 0
        outs = []
        for g in range(N_KV_HEADS):
            _, _, vv, pn, _ = _attn_group(q_ref, kp_ref, kc_ref, vp_ref, vc_ref, sink_ref, g, first)
            o = jnp.dot(pn.astype(BF16), vv, preferred_element_type=F32)
            outs += [o[j * BLOCK:(j + 1) * BLOCK, :] for j in range(GQA_GROUP)]
        o_ref[...] = jnp.concatenate(outs, axis=1).astype(BF16)

    cur = lambda w: pl.BlockSpec((BLOCK, w), lambda n: (n, 0))
    prev = lambda w: pl.BlockSpec((BLOCK, w), lambda n: (jnp.maximum(n - 1, 0), 0))
    return pl.pallas_call(
        body, name=name, grid=(t // BLOCK,),
        in_specs=[cur(ATTN_WIDTH), prev(KV_WIDTH), cur(KV_WIDTH), prev(KV_WIDTH), cur(KV_WIDTH),
                  pl.BlockSpec(memory_space=pltpu.SMEM)],
        out_specs=cur(ATTN_WIDTH),
        out_shape=jax.ShapeDtypeStruct((t, ATTN_WIDTH), BF16),
        compiler_params=_params(("arbitrary",)),
    )(q, k, k, v, v, sinks)


def _out_proj(attn, ubc, cw, wout, xhat_in, lnp_in, mod, *, seq, gate_idx, name):
    t, d = xhat_in.shape
    tm = min(512, seq)
    tpb = seq // tm
    cwid = CONV_WIDTH

    def body(attn_ref, ubc_ref, halo_ref, cw_ref, w_ref, x_ref, ln_ref, mod_ref,
             mixin_ref, mix_ref, xhat_ref, rstd_ref, zbuf):
        first = (pl.program_id(0) % tpb) == 0
        u, bg, cg = ubc_ref[:, 0:cwid], ubc_ref[:, cwid:2 * cwid], ubc_ref[:, 2 * cwid:3 * cwid]
        z = cg * u
        hz = halo_ref[:, 2 * cwid:3 * cwid] * halo_ref[:, 0:cwid]
        zbuf[0:8, :] = jnp.where(first, 0.0, hz)
        zbuf[8:8 + tm, :] = z
        y = cw_ref[0:1, :] * zbuf[6:6 + tm, :] + cw_ref[1:2, :] * zbuf[7:7 + tm, :] + cw_ref[2:3, :] * z
        co = (bg * y).astype(BF16)
        at = attn_ref[...]
        mixin_ref[:, 0:ATTN_WIDTH] = at
        mixin_ref[:, ATTN_WIDTH:] = co
        mix = (jnp.dot(at, w_ref[0:ATTN_WIDTH, :], preferred_element_type=F32)
               + jnp.dot(co, w_ref[ATTN_WIDTH:, :], preferred_element_type=F32))
        x = x_ref[...] * ln_ref[0:1, :] + ln_ref[1:2, :]
        r = DN_ALPHA * x + (1.0 + mod_ref[0, gate_idx:gate_idx + 1, :]) * mix
        xhat, rstd = _ln_stats(r)
        mix_ref[...] = mix.astype(BF16)
        xhat_ref[...] = xhat
        rstd_ref[...] = rstd

    row = lambda w: pl.BlockSpec((tm, w), lambda i: (i, 0))
    return pl.pallas_call(
        body, name=name, grid=(t // tm,),
        in_specs=[row(ATTN_WIDTH), row(3 * cwid),
                  pl.BlockSpec((8, 3 * cwid), lambda i: (jnp.maximum(i * (tm // 8) - 1, 0), 0)),
                  _full((8, cwid)), _resident((d, d)), row(d), _full((2, d)),
                  pl.BlockSpec((1, N_MOD, d), lambda i: (i // tpb, 0, 0))],
        out_specs=[row(d), row(d), row(d), row(1)],
        out_shape=[jax.ShapeDtypeStruct((t, d), BF16), jax.ShapeDtypeStruct((t, d), BF16),
                   jax.ShapeDtypeStruct((t, d), F32), jax.ShapeDtypeStruct((t, 1), F32)],
        scratch_shapes=[pltpu.VMEM((tm + 8, cwid), F32)],
        compiler_params=_params(("arbitrary",)),
    )(attn, ubc, ubc, cw, wout, xhat_in, lnp_in, mod)


def _ffn_bwd_act(df, wd, gu, *, seq, name):
    t, d = df.shape
    f = wd.shape[0]
    tm = min(512, seq)
    ch = min(COL_CHUNK, f)

    def body(df_ref, wd_ref, gu_ref, dgu_ref):
        dfv = df_ref[...]
        for j in range(f // ch):
            da = _dot_nt(dfv, wd_ref[j * ch:(j + 1) * ch, :])
            g = gu_ref[:, j * ch:(j + 1) * ch].astype(F32)
            u = gu_ref[:, f + j * ch:f + (j + 1) * ch].astype(F32)
            s = _sigmoid(g)
            dgu_ref[:, j * ch:(j + 1) * ch] = (da * u * (s * (1.0 + g * (1.0 - s)))).astype(BF16)
            dgu_ref[:, f + j * ch:f + (j + 1) * ch] = (da * (g * s)).astype(BF16)

    return pl.pallas_call(
        body, name=name, grid=(t // tm,),
        in_specs=[pl.BlockSpec((tm, d), lambda i: (i, 0)), _resident((f, d)),
                  pl.BlockSpec((tm, 2 * f), lambda i: (i, 0))],
        out_specs=pl.BlockSpec((tm, 2 * f), lambda i: (i, 0)),
        out_shape=jax.ShapeDtypeStruct((t, 2 * f), BF16),
        compiler_params=_params(("arbitrary",)),
    )(df, wd, gu)


def _bwd_in(a, w, dr, xin, rstd_prev, lnp_prev, mod, branch_prev, *, seq, w_is_nt, tk, sc_idx, gate_idx,
            branch_scale, final, name):
    t, kdim = a.shape
    d = dr.shape[1]
    nb = t // seq
    tm = min(512, seq)
    tpb = seq // tm
    nk = kdim // tk

    def body(*refs):
        if final:
            a_ref, w_ref, dr_ref, x_ref, mod_ref, dx_ref, dsc_ref, dsh_ref, acc = refs
        else:
            (a_ref, w_ref, dr_ref, x_ref, rstd_ref, ln_ref, mod_ref, br_ref,
             drp_ref, dbr_ref, dsc_ref, dsh_ref, dgate_ref, dg_ref, db_ref, acc) = refs
        i, k = pl.program_id(0), pl.program_id(1)
        part = _dot_nt(a_ref[...], w_ref[...]) if w_is_nt else jnp.dot(
            a_ref[...], w_ref[...], preferred_element_type=F32)

        @pl.when(k == 0)
        def _():
            acc[...] = part

        @pl.when(k > 0)
        def _():
            acc[...] += part

        @pl.when(k == nk - 1)
        def _():
            dh = acc[...]
            first_of_batch = (i % tpb) == 0

            @pl.when(first_of_batch)
            def _():
                dsc_ref[...] = jnp.zeros_like(dsc_ref)
                dsh_ref[...] = jnp.zeros_like(dsh_ref)
                if not final:
                    dgate_ref[...] = jnp.zeros_like(dgate_ref)

            if final:
                x = x_ref[...]
            else:
                @pl.when(i == 0)
                def _():
                    dg_ref[...] = jnp.zeros_like(dg_ref)
                    db_ref[...] = jnp.zeros_like(db_ref)
                xhat = x_ref[...]
                g_prev = ln_ref[0:1, :]
                x = xhat * g_prev + ln_ref[1:2, :]
            dsc_ref[0] += _row_sum(dh * x)
            dsh_ref[0] += _row_sum(dh)
            dx = DN_ALPHA * dr_ref[...] + dh * (1.0 + mod_ref[0, sc_idx:sc_idx + 1, :])
            if final:
                dx_ref[...] = dx
            else:
                dg_ref[...] += _row_sum(dx * xhat)
                db_ref[...] += _row_sum(dx)
                drp = _ln_bwd(dx, xhat, rstd_ref[...], g_prev)
                dgate_ref[0] += _row_sum(branch_scale * br_ref[...].astype(F32) * drp)
                drp_ref[...] = drp
                dbr_ref[...] = (branch_scale * (1.0 + mod_ref[0, gate_idx:gate_idx + 1, :]) * drp).astype(BF16)

    row = lambda w_: pl.BlockSpec((tm, w_), lambda i, k: (i, 0))
    vec = pl.BlockSpec((1, 1, d), lambda i, k: (i // tpb, 0, 0))
    w_spec = (pl.BlockSpec((d, tk), lambda i, k: (0, k)) if w_is_nt else pl.BlockSpec((tk, d), lambda i, k: (k, 0)))
    mod_spec = pl.BlockSpec((1, N_MOD, d), lambda i, k: (i // tpb, 0, 0))
    a_spec = pl.BlockSpec((tm, tk), lambda i, k: (i, k))
    vshape = jax.ShapeDtypeStruct((nb, 1, d), F32)
    if final:
        in_specs = [a_spec, w_spec, row(d), row(d), mod_spec]
        args = (a, w, dr, xin, mod)
        out_specs = [row(d), vec, vec]
        out_shape = [jax.ShapeDtypeStruct((t, d), F32), vshape, vshape]
    else:
        in_specs = [a_spec, w_spec, row(d), row(d), row(1), _full((2, d)), mod_spec, row(d)]
        args = (a, w, dr, xin, rstd_prev, lnp_prev, mod, branch_prev)
        out_specs = [row(d), row(d), vec, vec, vec, _full((1, d)), _full((1, d))]
        out_shape = [jax.ShapeDtypeStruct((t, d), F32), jax.ShapeDtypeStruct((t, d), BF16), vshape, vshape, vshape,
                     jax.ShapeDtypeStruct((1, d), F32), jax.ShapeDtypeStruct((1, d), F32)]
    return pl.pallas_call(
        body, name=name, grid=(t // tm, nk), in_specs=in_specs, out_specs=out_specs, out_shape=out_shape,
        scratch_shapes=[pltpu.VMEM((tm, d), F32)],
        compiler_params=_params(("arbitrary", "arbitrary")),
    )(*args)


def _matmul_tn(a, b, *, tmm, tnn, name):
    t, m = a.shape
    n = b.shape[1]
    tk = min(512, t)

    def body(a_ref, b_ref, o_ref):
        @pl.when(pl.program_id(2) == 0)
        def _():
            o_ref[...] = jnp.zeros_like(o_ref)
        o_ref[...] += _dot_tn(a_ref[...], b_ref[...])

    return pl.pallas_call(
        body, name=name, grid=(m // tmm, n // tnn, t // tk),
        in_specs=[pl.BlockSpec((tk, tmm), lambda i, j, k: (k, i)), pl.BlockSpec((tk, tnn), lambda i, j, k: (k, j))],
        out_specs=pl.BlockSpec((tmm, tnn), lambda i, j, k: (i, j)),
        out_shape=jax.ShapeDtypeStruct((m, n), F32),
        compiler_params=_params(("arbitrary", "arbitrary", "arbitrary")),
    )(a, b)


def _matmul_nt_bf16(a, w, *, seq, name):
    t, kdim = a.shape
    n = w.shape[0]
    tm = min(512, seq)

    def body(a_ref, w_ref, o_ref):
        av = a_ref[...]
        for j in range(n // COL_CHUNK):
            o_ref[:, j * COL_CHUNK:(j + 1) * COL_CHUNK] = _dot_nt(
                av, w_ref[j * COL_CHUNK:(j + 1) * COL_CHUNK, :]).astype(BF16)

    return pl.pallas_call(
        body, name=name, grid=(t // tm,),
        in_specs=[pl.BlockSpec((tm, kdim), lambda i: (i, 0)), _resident((n, kdim))],
        out_specs=pl.BlockSpec((tm, n), lambda i: (i, 0)),
        out_shape=jax.ShapeDtypeStruct((t, n), BF16),
        compiler_params=_params(("arbitrary",)),
    )(a, w)


def _attention_bwd(q, k, v, dmixin, sinks, *, seq, name):
    t = q.shape[0]
    nblk = seq // BLOCK

    def body(q_ref, kp_ref, kc_ref, vp_ref, vc_ref, do_ref, sink_ref,
             dq_ref, dkp_ref, dkc_ref, dvp_ref, dvc_ref, dsink_ref):
        n = pl.program_id(0)
        first = (n % nblk) == 0

        @pl.when(n == 0)
        def _():
            dsink_ref[...] = jnp.zeros_like(dsink_ref)

        dqs, dks, dvs = [], [], []
        srow = lax.broadcasted_iota(jnp.int32, (8, LANE), 0)
        dsink = jnp.zeros((8, LANE), F32)
        for g in range(N_KV_HEADS):
            qs, kk, vv, pn, psn = _attn_group(q_ref, kp_ref, kc_ref, vp_ref, vc_ref, sink_ref, g, first)
            dos = jnp.concatenate([do_ref[:, (GQA_GROUP * g + j) * HEAD_DIM:(GQA_GROUP * g + j + 1) * HEAD_DIM]
                                   for j in range(GQA_GROUP)], axis=0)
            dp = _dot_nt(dos, vv)
            delta = jnp.sum(pn * dp, axis=1, keepdims=True)
            ds = pn * (dp - delta)
            dsk = psn * delta
            for j in range(GQA_GROUP):
                tot = jnp.sum(dsk[j * BLOCK:(j + 1) * BLOCK, :], axis=0, keepdims=True)
                dsink = dsink - jnp.where(srow == GQA_GROUP * g + j, tot, 0.0)
            dsb = (ds * (HEAD_DIM ** -0.5)).astype(BF16)
            dqg = jnp.dot(dsb, kk, preferred_element_type=F32)
            dqs += [dqg[j * BLOCK:(j + 1) * BLOCK, :] for j in range(GQA_GROUP)]
            dks.append(_dot_tn(dsb, qs))
            dvs.append(_dot_tn(pn.astype(BF16), dos))
        dsink_ref[...] += dsink
        dq_ref[...] = jnp.concatenate(dqs, axis=1)
        dkp_ref[...] = jnp.concatenate([x[0:BLOCK, :] for x in dks], axis=1)
        dkc_ref[...] = jnp.concatenate([x[BLOCK:, :] for x in dks], axis=1)
        dvp_ref[...] = jnp.concatenate([x[0:BLOCK, :] for x in dvs], axis=1)
        dvc_ref[...] = jnp.concatenate([x[BLOCK:, :] for x in dvs], axis=1)

    cur = lambda w: pl.BlockSpec((BLOCK, w), lambda n: (n, 0))
    prev = lambda w: pl.BlockSpec((BLOCK, w), lambda n: (jnp.maximum(n - 1, 0), 0))
    kv = jax.ShapeDtypeStruct((t, KV_WIDTH), F32)
    return pl.pallas_call(
        body, name=name, grid=(t // BLOCK,),
        in_specs=[cur(ATTN_WIDTH), prev(KV_WIDTH), cur(KV_WIDTH), prev(KV_WIDTH), cur(KV_WIDTH), cur(ATTN_WIDTH),
                  pl.BlockSpec(memory_space=pltpu.SMEM)],
        out_specs=[cur(ATTN_WIDTH), cur(KV_WIDTH), cur(KV_WIDTH), cur(KV_WIDTH), cur(KV_WIDTH), _full((8, LANE))],
        out_shape=[jax.ShapeDtypeStruct((t, ATTN_WIDTH), F32), kv, kv, kv, kv, jax.ShapeDtypeStruct((8, LANE), F32)],
        compiler_params=_params(("arbitrary",)),
    )(q, k, k, v, v, dmixin, sinks)


def _mix_bwd_assemble(dq, dkp, dkc, dvp, dvc, cos, sa, sb, dmixin, ubc, cw, *, seq, name):
    t = dq.shape[0]
    nblk = seq // BLOCK
    ntile = t // BLOCK
    cwid = CONV_WIDTH
    tm = BLOCK

    def body(dq_ref, dkc_ref, dkp_ref, dvc_ref, dvp_ref, cos_ref, sa_ref, sb_ref, dco_ref, dcon_ref,
             ubc_ref, hprev_ref, hnext_ref, cw_ref, dproj_ref, dcw_ref, zbuf, dybuf):
        i = pl.program_id(0)
        first = (i % nblk) == 0
        last = (i % nblk) == nblk - 1
        glast = i == ntile - 1

        @pl.when(i == 0)
        def _():
            dcw_ref[...] = jnp.zeros_like(dcw_ref)

        cos_t, sa_t, sb_t = cos_ref[...], sa_ref[...], sb_ref[...]
        for j in range(ATTN_WIDTH // LANE):
            dproj_ref[:, j * LANE:(j + 1) * LANE] = _rope_t(
                dq_ref[:, j * LANE:(j + 1) * LANE], cos_t, sa_t, sb_t).astype(BF16)
        dk = dkc_ref[...] + jnp.where(glast, 0.0, dkp_ref[...])
        dproj_ref[:, ATTN_WIDTH:ATTN_WIDTH + KV_WIDTH] = _rope_t(dk, cos_t, sa_t, sb_t).astype(BF16)
        dv = dvc_ref[...] + jnp.where(glast, 0.0, dvp_ref[...])
        dproj_ref[:, ATTN_WIDTH + KV_WIDTH:ATTN_WIDTH + 2 * KV_WIDTH] = dv.astype(BF16)

        u, bg, cg = ubc_ref[:, 0:cwid], ubc_ref[:, cwid:2 * cwid], ubc_ref[:, 2 * cwid:3 * cwid]
        z = cg * u
        hz = hprev_ref[:, 2 * cwid:3 * cwid] * hprev_ref[:, 0:cwid]
        zbuf[0:8, :] = jnp.where(first, 0.0, hz)
        zbuf[8:8 + tm, :] = z
        z2, z1 = zbuf[6:6 + tm, :], zbuf[7:7 + tm, :]
        w0, w1, w2 = cw_ref[0:1, :], cw_ref[1:2, :], cw_ref[2:3, :]
        y = w0 * z2 + w1 * z1 + w2 * z
        dco = dco_ref[...].astype(F32)
        dyc = dco * bg
        dyn = dcon_ref[0:8, :].astype(F32) * hnext_ref[:, cwid:2 * cwid]
        dybuf[0:tm, :] = dyc
        dybuf[tm:tm + 8, :] = jnp.where(last, 0.0, dyn)
        dz = w2 * dyc + w1 * dybuf[1:1 + tm, :] + w0 * dybuf[2:2 + tm, :]
        srow = lax.broadcasted_iota(jnp.int32, (8, cwid), 0)
        dcw_ref[...] += (jnp.where(srow == 0, _row_sum(dyc * z2), 0.0) + jnp.where(srow == 1, _row_sum(dyc * z1), 0.0)
                         + jnp.where(srow == 2, _row_sum(dyc * z), 0.0))
        base = ATTN_WIDTH + 2 * KV_WIDTH
        dproj_ref[:, base:base + cwid] = (dz * cg).astype(BF16)
        dproj_ref[:, base + cwid:base + 2 * cwid] = (dco * y).astype(BF16)
        dproj_ref[:, base + 2 * cwid:base + 3 * cwid] = (dz * u).astype(BF16)

    cur = lambda w: pl.BlockSpec((tm, w), lambda i: (i, 0))
    nxt = lambda w: pl.BlockSpec((tm, w), lambda i: (jnp.minimum(i + 1, ntile - 1), 0))
    return pl.pallas_call(
        body, name=name, grid=(ntile,),
        in_specs=[cur(ATTN_WIDTH), cur(KV_WIDTH), nxt(KV_WIDTH), cur(KV_WIDTH), nxt(KV_WIDTH),
                  cur(LANE), cur(LANE), cur(LANE),
                  pl.BlockSpec((tm, cwid), lambda i: (i, 1)),
                  pl.BlockSpec((16, cwid), lambda i: (jnp.minimum((i + 1) * (tm // 16), t // 16 - 1), 1)),
                  cur(3 * cwid),
                  pl.BlockSpec((8, 3 * cwid), lambda i: (jnp.maximum(i * (tm // 8) - 1, 0), 0)),
                  pl.BlockSpec((8, 3 * cwid), lambda i: (jnp.minimum((i + 1) * (tm // 8), t // 8 - 1), 0)),
                  _full((8, cwid))],
        out_specs=[cur(IN_WIDTH), _full((8, cwid))],
        out_shape=[jax.ShapeDtypeStruct((t, IN_WIDTH), BF16), jax.ShapeDtypeStruct((8, cwid), F32)],
        scratch_shapes=[pltpu.VMEM((tm + 8, cwid), F32), pltpu.VMEM((tm + 8, cwid), F32)],
        compiler_params=_params(("arbitrary",)),
    )(dq, dkc, dkp, dvc, dvp, cos, sa, sb, dmixin, dmixin, ubc, ubc, ubc, cw)


def _ada_fwd(c_all, w_ada, b_ada_shard, *, name):
    nb, d = c_all.shape
    n = w_ada.shape[1]
    tn = n // 2

    def body(c_ref, w_ref, b_ref, o_ref):
        cv = c_ref[...]
        cond = cv * _sigmoid(cv)
        o_ref[...] = jnp.dot(cond, w_ref[...], preferred_element_type=F32,
                             precision=lax.Precision.HIGHEST) + b_ref[...]

    return pl.pallas_call(
        body, name=name, grid=(n // tn,),
        in_specs=[_full((nb, d)), pl.BlockSpec((d, tn), lambda j: (0, j)), pl.BlockSpec((1, tn), lambda j: (0, j))],
        out_specs=pl.BlockSpec((nb, tn), lambda j: (0, j)),
        out_shape=jax.ShapeDtypeStruct((nb, n), F32),
        compiler_params=_params(("arbitrary",)),
    )(c_all, w_ada, b_ada_shard)


def _small_finish(gathered, dmod_all, dmod_shard, c_all_t, *, name):
    d = D_MODEL
    nb, n = dmod_shard.shape

    def body(g_ref, dm_ref, dms_ref, ct_ref, sum_ref, gw_ref, gb_ref):
        total = g_ref[0]
        for dev in range(1, N_DEV):
            total = total + g_ref[dev]
        sum_ref[...] = total
        gb_ref[...] = _row_sum(dm_ref[...])
        ctv = ct_ref[...]
        cond_t = ctv * _sigmoid(ctv)
        for jb in range(n // COL_CHUNK):
            gw_ref[:, jb * COL_CHUNK:(jb + 1) * COL_CHUNK] = jnp.dot(
                cond_t, dms_ref[:, jb * COL_CHUNK:(jb + 1) * COL_CHUNK], preferred_element_type=F32,
                precision=lax.Precision.HIGHEST)

    return pl.pallas_call(
        body, name=name, grid=(1,),
        in_specs=[_full((N_DEV, SMALL_ROWS, d)), _full((nb, N_MOD * d)), _full((nb, n)), _full((d, nb))],
        out_specs=[_full((SMALL_ROWS, d)), _full((d, n)), _full((1, N_MOD * d))],
        out_shape=[jax.ShapeDtypeStruct((SMALL_ROWS, d), F32), jax.ShapeDtypeStruct((d, n), F32),
                   jax.ShapeDtypeStruct((1, N_MOD * d), F32)],
        compiler_params=_params(("arbitrary",)),
    )(gathered, dmod_all, dmod_shard, c_all_t)


def _row_tile(r, c, budget=1 << 20):
    if r * c * 4 <= budget or r % 16:
        return r
    best = 16
    for tr in range(16, r + 1, 16):
        if r % tr == 0 and tr * c * 4 <= budget:
            best = tr
    return best


def _cast_bf16(w, *, name):
    r, c = w.shape
    tr = _row_tile(r, c)

    def body(w_ref, o_ref):
        o_ref[...] = w_ref[...].astype(BF16)

    return pl.pallas_call(
        body, name=name, grid=(r // tr,),
        in_specs=[pl.BlockSpec((tr, c), lambda i: (i, 0))], out_specs=pl.BlockSpec((tr, c), lambda i: (i, 0)),
        out_shape=jax.ShapeDtypeStruct((r, c), BF16), compiler_params=_params(("arbitrary",)),
    )(w)


def _adamw(w, g, m, v, *, name):
    r, c = w.shape
    tr = _row_tile(r, c)
    c1 = 1.0 - ADAM_B1 ** ADAM_STEP
    c2 = 1.0 - ADAM_B2 ** ADAM_STEP

    def body(w_ref, g_ref, m_ref, v_ref, d_ref, nm_ref, nv_ref):
        gv = g_ref[...]
        m2 = ADAM_B1 * m_ref[...] + (1.0 - ADAM_B1) * gv
        v2 = ADAM_B2 * v_ref[...] + (1.0 - ADAM_B2) * (gv * gv)
        d_ref[...] = -ADAM_LR * ((m2 / c1) / (jnp.sqrt(v2 / c2) + ADAM_EPS) + ADAM_WD * w_ref[...])
        nm_ref[...] = m2
        nv_ref[...] = v2

    spec = pl.BlockSpec((tr, c), lambda i: (i, 0))
    sh = jax.ShapeDtypeStruct((r, c), F32)
    return pl.pallas_call(
        body, name=name, grid=(r // tr,), in_specs=[spec] * 4, out_specs=[spec] * 3, out_shape=[sh] * 3,
        compiler_params=_params(("arbitrary",)),
    )(w, g, m, v)


def _sum_pair(pos, g3, r3, blk_of, *, name):
    n, rows, cols = r3.shape
    tr = _row_tile(rows, cols)

    def body(pos_ref, g_ref, r_ref, s32_ref, s16_ref):
        s = g_ref[0] + r_ref[0]
        s32_ref[0] = s
        s16_ref[0] = s.astype(BF16)

    own = pl.BlockSpec((1, tr, cols), lambda p, i, pos: (blk_of(p, pos), i, 0))
    plain = pl.BlockSpec((1, tr, cols), lambda p, i, pos: (p, i, 0))
    grid_spec = pltpu.PrefetchScalarGridSpec(num_scalar_prefetch=1, grid=(n, rows // tr), in_specs=[own, plain],
                                             out_specs=[plain, plain])
    return pl.pallas_call(
        body, name=name, grid_spec=grid_spec,
        out_shape=[jax.ShapeDtypeStruct((n, rows, cols), F32), jax.ShapeDtypeStruct((n, rows, cols), BF16)],
        compiler_params=_params(("arbitrary", "arbitrary")),
    )(pos, g3, r3)


def _sum_final(pos, s32, recv, *, col_kind, n_shard, name):
    if col_kind:
        rows, cols = s32.shape[1], n_shard
        own = lambda tr: pl.BlockSpec((1, tr, cols), lambda i, pos: (0, i, 2 * pos[0] + pos[1]))
    else:
        rows, cols = s32.shape[1], s32.shape[2]
        own = lambda tr: pl.BlockSpec((1, tr, cols), lambda i, pos: (2 * pos[0] + pos[1], i, 0))
    tr = _row_tile(rows, cols)

    def body(pos_ref, s_ref, r_ref, o_ref):
        o_ref[0] = ((s_ref[0] + r_ref[0].astype(F32)) + r_ref[1].astype(F32)) + r_ref[2].astype(F32)

    grid_spec = pltpu.PrefetchScalarGridSpec(
        num_scalar_prefetch=1, grid=(rows // tr,),
        in_specs=[own(tr), pl.BlockSpec((3, tr, cols), lambda i, pos: (0, i, 0))],
        out_specs=pl.BlockSpec((1, tr, cols), lambda i, pos: (pos[2], i, 0)))
    return pl.pallas_call(
        body, name=name, grid_spec=grid_spec, out_shape=jax.ShapeDtypeStruct((2, rows, cols), F32),
        compiler_params=_params(("arbitrary",)),
    )(pos, s32, recv)


def _position():
    return lax.axis_index("x"), lax.axis_index("y"), lax.axis_index("c")


def _allgather8(x_shard, *, name):
    m_per, n = x_shard.shape

    def body(x_ref, out_ref, send_sems, recv_sems, local_sem):
        x, y, c = _position()
        me, sibling = (x, y, c), (x, y, 1 - c)
        chips = [(1 - x, y), (x, 1 - y), (1 - x, 1 - y)]

        def rows(px, py, pc):
            return out_ref.at[pl.ds((4 * px + 2 * py + pc) * m_per, m_per), :]

        def copy(k, block, to, src=None):
            return pltpu.make_async_remote_copy(
                src_ref=rows(*block) if src is None else src, dst_ref=rows(*block),
                send_sem=send_sems.at[k], recv_sem=recv_sems.at[k], device_id=to, device_id_type=MESH)

        mine = pltpu.make_async_copy(x_ref, rows(*me), local_sem)
        mine.start()
        first = [copy(0, me, sibling, src=x_ref)]
        first += [copy(1 + j, me, (*chip, c), src=x_ref) for j, chip in enumerate(chips)]
        for cp in first:
            cp.start()
        passed = [copy(4 + j, (*chip, c), sibling) for j, chip in enumerate(chips)]
        for j, chip in enumerate(chips):
            copy(1 + j, (*chip, c), me).wait_recv()
            passed[j].start()
        copy(0, sibling, me).wait_recv()
        for j, chip in enumerate(chips):
            copy(4 + j, (*chip, 1 - c), me).wait_recv()
        for cp in first + passed:
            cp.wait_send()
        mine.wait()

    return pl.pallas_call(
        body, name=name,
        out_shape=jax.ShapeDtypeStruct((N_DEV * m_per, n), x_shard.dtype),
        in_specs=[pl.BlockSpec(memory_space=pltpu.VMEM)], out_specs=pl.BlockSpec(memory_space=pltpu.VMEM),
        scratch_shapes=[pltpu.SemaphoreType.DMA((7,)), pltpu.SemaphoreType.DMA((7,)), pltpu.SemaphoreType.DMA],
    )(x_shard)


def _shard_region(ref, col_kind, n, chip_idx, half=None):
    if col_kind:
        rows = ref.shape[0]
        if half is None:
            return ref.at[:, pl.ds(chip_idx * n, n)]
        return ref.at[pl.ds(half * (rows // 2), rows // 2), pl.ds(chip_idx * n, n)]
    if half is None:
        return ref.at[pl.ds(chip_idx * n, n), :]
    return ref.at[pl.ds(chip_idx * n + half * (n // 2), n // 2), :]


def _half_rows(ref, half):
    rows = ref.shape[0]
    return ref.at[pl.ds(half * (rows // 2), rows // 2), :]


def _gather_weights(shards, kinds, *, name):
    n_p = len(shards)
    sizes = [s.shape[1] if kd else s.shape[0] for s, kd in zip(shards, kinds)]
    out_shape = [jax.ShapeDtypeStruct((s.shape[0], s.shape[1] * N_CHIPS) if kd else (s.shape[0] * N_CHIPS, s.shape[1]),
                                      s.dtype) for s, kd in zip(shards, kinds)]

    def body(*refs):
        srcs, outs = refs[:n_p], refs[n_p:2 * n_p]
        send_sems, recv_sems, fsend_sems, frecv_sems, local_sems = refs[2 * n_p:]
        x, y, c = _position()
        k = 2 * x + y
        chips = [(1 - x, y), (x, 1 - y), (1 - x, 1 - y)]
        local, sends, fwds = [], [], []
        for p in range(n_p):
            cp = pltpu.make_async_copy(srcs[p], _shard_region(outs[p], kinds[p], sizes[p], k), local_sems.at[p])
            cp.start()
            local.append(cp)
        for p in range(n_p):
            for j, chip in enumerate(chips):
                cp = pltpu.make_async_remote_copy(
                    src_ref=_half_rows(srcs[p], c), dst_ref=_shard_region(outs[p], kinds[p], sizes[p], k, c),
                    send_sem=send_sems.at[3 * p + j], recv_sem=recv_sems.at[3 * p + j],
                    device_id=(*chip, c), device_id_type=MESH)
                cp.start()
                sends.append(cp)
        for p in range(n_p):
            for j, chip in enumerate(chips):
                region = _shard_region(outs[p], kinds[p], sizes[p], 2 * chip[0] + chip[1], c)
                pltpu.make_async_remote_copy(
                    src_ref=region, dst_ref=region, send_sem=send_sems.at[3 * p + j], recv_sem=recv_sems.at[3 * p + j],
                    device_id=(*chip, c), device_id_type=MESH).wait_recv()
                fw = pltpu.make_async_remote_copy(
                    src_ref=region, dst_ref=region, send_sem=fsend_sems.at[3 * p + j], recv_sem=frecv_sems.at[3 * p + j],
                    device_id=(x, y, 1 - c), device_id_type=MESH)
                fw.start()
                fwds.append(fw)
        for p in range(n_p):
            for j, chip in enumerate(chips):
                region = _shard_region(outs[p], kinds[p], sizes[p], 2 * chip[0] + chip[1], 1 - c)
                pltpu.make_async_remote_copy(
                    src_ref=region, dst_ref=region, send_sem=fsend_sems.at[3 * p + j], recv_sem=frecv_sems.at[3 * p + j],
                    device_id=(x, y, 1 - c), device_id_type=MESH).wait_recv()
        for cp in sends + fwds:
            cp.wait_send()
        for cp in local:
            cp.wait()

    any_spec = pl.BlockSpec(memory_space=pl.ANY)
    return pl.pallas_call(
        body, name=name, out_shape=out_shape, in_specs=[any_spec] * n_p, out_specs=[any_spec] * n_p,
        scratch_shapes=[pltpu.SemaphoreType.DMA((3 * n_p,))] * 4 + [pltpu.SemaphoreType.DMA((n_p,))],
    )(*shards)


def _swap_halves(grads, kinds, *, name):
    n_p = len(grads)
    out_shape, n_copies = [], []
    for g, kd in zip(grads, kinds):
        if kd:
            out_shape.append(jax.ShapeDtypeStruct((1, g.shape[0] // 2, g.shape[1]), g.dtype))
            n_copies.append(1)
        else:
            n = g.shape[0] // N_CHIPS
            out_shape.append(jax.ShapeDtypeStruct((N_CHIPS, n // 2, g.shape[1]), g.dtype))
            n_copies.append(N_CHIPS)
    total = sum(n_copies)

    def body(*refs):
        srcs, outs = refs[:n_p], refs[n_p:2 * n_p]
        send_sems, recv_sems = refs[2 * n_p:]
        x, y, c = _position()
        copies = []
        idx = 0
        for p in range(n_p):
            for kk in range(n_copies[p]):
                if kinds[p]:
                    src = _half_rows(srcs[p], 1 - c)
                else:
                    n = srcs[p].shape[0] // N_CHIPS
                    src = srcs[p].at[pl.ds(kk * n + (1 - c) * (n // 2), n // 2), :]
                cp = pltpu.make_async_remote_copy(
                    src_ref=src, dst_ref=outs[p].at[kk], send_sem=send_sems.at[idx], recv_sem=recv_sems.at[idx],
                    device_id=(x, y, 1 - c), device_id_type=MESH)
                cp.start()
                copies.append(cp)
                idx += 1
        for cp in copies:
            cp.wait_recv()
        for cp in copies:
            cp.wait_send()

    any_spec = pl.BlockSpec(memory_space=pl.ANY)
    return pl.pallas_call(
        body, name=name, out_shape=out_shape, in_specs=[any_spec] * n_p, out_specs=[any_spec] * n_p,
        scratch_shapes=[pltpu.SemaphoreType.DMA((total,)), pltpu.SemaphoreType.DMA((total,))],
    )(*grads)


def _exchange_chip_sums(s16, kinds, sizes, *, name):
    n_p = len(s16)
    out_shape = []
    for s, kd, n in zip(s16, kinds, sizes):
        out_shape.append(jax.ShapeDtypeStruct((3, s.shape[1], n if kd else s.shape[2]), s.dtype))

    def body(*refs):
        srcs, outs = refs[:n_p], refs[n_p:2 * n_p]
        send_sems, recv_sems = refs[2 * n_p:]
        x, y, c = _position()
        chips = [(1 - x, y), (x, 1 - y), (1 - x, 1 - y)]
        copies = []
        for p in range(n_p):
            for j, chip in enumerate(chips):
                kk = 2 * chip[0] + chip[1]
                src = srcs[p].at[0, :, pl.ds(kk * sizes[p], sizes[p])] if kinds[p] else srcs[p].at[kk]
                cp = pltpu.make_async_remote_copy(
                    src_ref=src, dst_ref=outs[p].at[j], send_sem=send_sems.at[3 * p + j],
                    recv_sem=recv_sems.at[3 * p + j], device_id=(*chip, c), device_id_type=MESH)
                cp.start()
                copies.append(cp)
        for cp in copies:
            cp.wait_recv()
        for cp in copies:
            cp.wait_send()

    any_spec = pl.BlockSpec(memory_space=pl.ANY)
    return pl.pallas_call(
        body, name=name, out_shape=out_shape, in_specs=[any_spec] * n_p, out_specs=[any_spec] * n_p,
        scratch_shapes=[pltpu.SemaphoreType.DMA((3 * n_p,)), pltpu.SemaphoreType.DMA((3 * n_p,))],
    )(*s16)


def _share_finished(halves, *, name):
    n_p = len(halves)

    def body(*refs):
        srcs, outs = refs[:n_p], refs[n_p:2 * n_p]
        send_sems, recv_sems = refs[2 * n_p:]
        x, y, c = _position()
        copies = []
        for p in range(n_p):
            cp = pltpu.make_async_remote_copy(
                src_ref=outs[p].at[c], dst_ref=outs[p].at[c], send_sem=send_sems.at[p], recv_sem=recv_sems.at[p],
                device_id=(x, y, 1 - c), device_id_type=MESH)
            cp.start()
            copies.append(cp)
        for p in range(n_p):
            other = outs[p].at[1 - c]
            pltpu.make_async_remote_copy(
                src_ref=other, dst_ref=other, send_sem=send_sems.at[p], recv_sem=recv_sems.at[p],
                device_id=(x, y, 1 - c), device_id_type=MESH).wait_recv()
        for cp in copies:
            cp.wait_send()

    any_spec = pl.BlockSpec(memory_space=pl.ANY)
    return pl.pallas_call(
        body, name=name, out_shape=[jax.ShapeDtypeStruct(h.shape, h.dtype) for h in halves],
        in_specs=[any_spec] * n_p, out_specs=[any_spec] * n_p, input_output_aliases={p: p for p in range(n_p)},
        scratch_shapes=[pltpu.SemaphoreType.DMA((n_p,)), pltpu.SemaphoreType.DMA((n_p,))],
    )(*halves)


def _rope_tables(positions):
    half = ROT_DIM // 2
    inv_freq = jnp.power(jnp.float32(ROPE_THETA), -jnp.arange(0, ROT_DIM, 2, dtype=F32) / ROT_DIM)
    ang = positions.astype(F32).reshape(-1)[:, None] * inv_freq
    cos, sin = jnp.cos(ang), jnp.sin(ang)
    t = cos.shape[0]
    ones = jnp.ones((t, HEAD_DIM - ROT_DIM), F32)
    zeros = jnp.zeros((t, HEAD_DIM - ROT_DIM), F32)
    zh = jnp.zeros((t, half), F32)
    cos_h = jnp.concatenate([cos, cos, ones], axis=1)
    sa_h = jnp.concatenate([-sin, zh, zeros], axis=1)
    sb_h = jnp.concatenate([zh, sin, zeros], axis=1)
    two = lambda a: jnp.concatenate([a, a], axis=1)
    return two(cos_h), two(sa_h), two(sb_h)


def kernel(x, c, positions, w_ada, b_ada, ffn1_w_gate_up, ffn1_w_down, ln1_g, ln1_b, w_in, conv_w, attn_sinks, w_out, ln2_g, ln2_b, ffn2_w_gate_up, ffn2_w_down, ln3_g, ln3_b, loss_target, m_w_ada, m_b_ada, m_ffn1_w_gate_up, m_ffn1_w_down, m_ln1_g, m_ln1_b, m_w_in, m_conv_w, m_attn_sinks, m_w_out, m_ln2_g, m_ln2_b, m_ffn2_w_gate_up, m_ffn2_w_down, m_ln3_g, m_ln3_b, v_w_ada, v_b_ada, v_ffn1_w_gate_up, v_ffn1_w_down, v_ln1_g, v_ln1_b, v_w_in, v_conv_w, v_attn_sinks, v_w_out, v_ln2_g, v_ln2_b, v_ffn2_w_gate_up, v_ffn2_w_down, v_ln3_g, v_ln3_b):
    d = D_MODEL
    nb, seq, _ = x.shape
    t = nb * seq
    f = ffn1_w_down.shape[1] * N_CHIPS
    ax, ay, ac = _position()
    chip = 2 * ax + ay
    dev = 2 * chip + ac
    pos = jnp.stack([ax, ay, ac]).astype(jnp.int32)

    x2 = x.reshape(t, d)
    tgt2 = loss_target.reshape(t, d)
    ln1 = jnp.concatenate([ln1_g, ln1_b], axis=0)
    ln2 = jnp.concatenate([ln2_g, ln2_b], axis=0)
    ln3 = jnp.concatenate([ln3_g, ln3_b], axis=0)
    sinks = attn_sinks.reshape(N_Q_HEADS)
    cos_t, sa_t, sb_t = _rope_tables(positions)

    n_ada = w_ada.shape[2]
    c_all = _allgather8(c.reshape(nb * d // LANE, LANE), name="gather_c").reshape(N_DEV * nb, d)
    b_shard = lax.dynamic_slice(b_ada, (0, chip * n_ada), (1, n_ada))
    mod_part = _ada_fwd(c_all, w_ada[0], b_shard, name="ada_fwd")
    conv_rows = jnp.pad(conv_w[0], ((0, 5), (0, n_ada - conv_w.shape[2])))
    part = jnp.concatenate([mod_part, conv_rows], axis=0)
    parts = _allgather8(part, name="gather_mod").reshape(N_DEV, N_DEV * nb + 8, n_ada)
    mod_all = jnp.concatenate([parts[2 * k, :N_DEV * nb, :] for k in range(N_CHIPS)], axis=1)
    mod = lax.dynamic_slice(mod_all, (dev * nb, 0), (nb, N_MOD * d)).reshape(nb, N_MOD, d)
    cw_full = jnp.concatenate([parts[2 * k, N_DEV * nb:, :conv_w.shape[2]] for k in range(N_CHIPS)], axis=1)

    shards = [_cast_bf16(ffn1_w_gate_up[0], name="cast_gu1"), _cast_bf16(ffn1_w_down[0], name="cast_d1"),
              _cast_bf16(w_in[0].T, name="cast_in"), _cast_bf16(w_out[0], name="cast_out"),
              _cast_bf16(ffn2_w_gate_up[0], name="cast_gu2"), _cast_bf16(ffn2_w_down[0], name="cast_d2")]
    kinds = [True, False, False, False, True, False]
    sizes = [s.shape[1] if kd else s.shape[0] for s, kd in zip(shards, kinds)]
    wgu1, wd1, win_t, wout, wgu2, wd2 = _gather_weights(shards, kinds, name="gather_weights")

    h1, a1, gu1 = _ffn_up(x2, ln1, mod, wgu1, seq=seq, sc_idx=1, sh_idx=0, use_ln=False, name="ffn1_up")
    f1, xhat1, rstd1 = _ffn_down_ln(a1, wd1, x2, ln1, mod, seq=seq, gate_idx=2, use_ln=False, name="ffn1_down")
    h2, q, k, v, ubc = _in_proj(xhat1, ln1, mod, win_t, cos_t, sa_t, sb_t, seq=seq, sc_idx=4, sh_idx=3, name="in_proj")
    attn = _attention(q, k, v, sinks, seq=seq, name="attention")
    mixin, mix, xhat2, rstd2 = _out_proj(attn, ubc, cw_full, wout, xhat1, ln1, mod, seq=seq, gate_idx=5, name="out_proj")
    h3, a3, gu3 = _ffn_up(xhat2, ln2, mod, wgu2, seq=seq, sc_idx=7, sh_idx=6, use_ln=True, name="ffn2_up")
    dr3, df3, loss_cols, dln3g, dln3b, dgate3 = _ffn_down_loss(
        a3, wd2, xhat2, ln2, mod, ln3, tgt2, seq=seq, gate_idx=8, name="ffn2_down_loss")

    dgu3 = _ffn_bwd_act(df3, wd2, gu3, seq=seq, name="ffn2_bwd_act")
    g_wd2 = _matmul_tn(a3, df3, tmm=f // 2, tnn=d, name="grad_wd2")
    g_wgu2 = _matmul_tn(h3, dgu3, tmm=d, tnn=(2 * f) // 4, name="grad_wgu2")
    dr2, dmix, dsc3, dsh3, dgate2, dln2g, dln2b = _bwd_in(
        dgu3, wgu2, dr3, xhat2, rstd2, ln2, mod, mix, seq=seq, w_is_nt=True, tk=(2 * f) // 4, sc_idx=7, gate_idx=5,
        branch_scale=1.0, final=False, name="ffn2_bwd_in")
    g_wout = _matmul_tn(mixin, dmix, tmm=d, tnn=d, name="grad_wout")
    dmixin = _matmul_nt_bf16(dmix, wout, seq=seq, name="out_proj_bwd")
    dq, dkp, dkc, dvp, dvc, dsink = _attention_bwd(q, k, v, dmixin, sinks, seq=seq, name="attention_bwd")
    dproj, dcw = _mix_bwd_assemble(dq, dkp, dkc, dvp, dvc, cos_t, sa_t, sb_t, dmixin, ubc, cw_full, seq=seq,
                                   name="mix_bwd")
    g_win_t = _matmul_tn(dproj, h2, tmm=IN_WIDTH // 2, tnn=d, name="grad_win")
    dr1, df1, dsc2, dsh2, dgate1, dln1g, dln1b = _bwd_in(
        dproj, win_t, dr2, xhat1, rstd1, ln1, mod, f1, seq=seq, w_is_nt=False, tk=IN_WIDTH // 3, sc_idx=4, gate_idx=2,
        branch_scale=0.5, final=False, name="in_proj_bwd")
    dgu1 = _ffn_bwd_act(df1, wd1, gu1, seq=seq, name="ffn1_bwd_act")
    g_wd1 = _matmul_tn(a1, df1, tmm=f // 2, tnn=d, name="grad_wd1")
    g_wgu1 = _matmul_tn(h1, dgu1, tmm=d, tnn=(2 * f) // 4, name="grad_wgu1")
    grad_x, dsc1, dsh1 = _bwd_in(
        dgu1, wgu1, dr1, x2, None, None, mod, None, seq=seq, w_is_nt=True, tk=(2 * f) // 4, sc_idx=1, gate_idx=None,
        branch_scale=None, final=True, name="ffn1_bwd_in")

    dmod = jnp.concatenate([dsh1, dsc1, dgate1, dsh2, dsc2, dgate2, dsh3, dsc3, dgate3], axis=1)
    loss_row = jnp.sum(loss_cols, axis=1, keepdims=True) * (0.5 / d)
    lane_row = lambda a: jnp.pad(a, ((0, 0), (0, d - a.shape[1])))
    block = jnp.concatenate(
        [dmod.reshape(nb * N_MOD, d), dln1g, dln1b, dln2g, dln2b, dln3g, dln3b,
         lane_row(dcw[0:3, :]), lane_row(dsink[:, 0:1].reshape(1, N_Q_HEADS)), lane_row(loss_row)], axis=0)
    block = jnp.pad(block, ((0, SMALL_ROWS - block.shape[0]), (0, 0)))
    gathered = _allgather8(block, name="gather_small").reshape(N_DEV, SMALL_ROWS, d)
    dmod_all = gathered[:, :nb * N_MOD, :].reshape(N_DEV * nb, N_MOD * d)
    dmod_shard = lax.dynamic_slice(dmod_all, (0, chip * n_ada), (N_DEV * nb, n_ada))
    small, g_w_ada, g_b_ada = _small_finish(gathered, dmod_all, dmod_shard, c_all.T, name="small_finish")
    r0 = nb * N_MOD
    loss = small[r0 + 10, 0]
    g_ln = [small[r0 + i:r0 + i + 1, :] for i in range(6)]
    g_cw_full = small[r0 + 6:r0 + 9, :CONV_WIDTH]
    g_conv = lax.dynamic_slice(g_cw_full, (0, chip * conv_w.shape[2]), (3, conv_w.shape[2]))
    g_sinks = small[r0 + 9:r0 + 10, :N_Q_HEADS]

    grads = [g_wgu1, g_wd1, g_win_t, g_wout, g_wgu2, g_wd2]
    sib = _swap_halves(grads, kinds, name="swap_halves")
    s32, s16 = [], []
    for p, (g, r3, kd) in enumerate(zip(grads, sib, kinds)):
        if kd:
            g3 = g.reshape(2, g.shape[0] // 2, g.shape[1])
            blk_of = lambda p_, pos_: pos_[2]
        else:
            g3 = g.reshape(2 * N_CHIPS, g.shape[0] // (2 * N_CHIPS), g.shape[1])
            blk_of = lambda p_, pos_: 2 * p_ + pos_[2]
        a32, a16 = _sum_pair(pos, g3, r3, blk_of, name=f"sum_pair{p}")
        s32.append(a32)
        s16.append(a16)
    recv = _exchange_chip_sums(s16, kinds, sizes, name="exchange_chip_sums")
    halves = [_sum_final(pos, s32[p], recv[p], col_kind=kinds[p], n_shard=sizes[p], name=f"sum_final{p}")
              for p in range(len(grads))]
    full = _share_finished(halves, name="share_finished")
    g_gu1, g_d1, g_in_t, g_out, g_gu2, g_d2 = [a.reshape(2 * a.shape[1], a.shape[2]) for a in full]
    g_in = g_in_t.T

    def flat2(a):
        return a.reshape(-1, a.shape[-1])

    weights = dict(
        w_ada=(w_ada, g_w_ada, m_w_ada, v_w_ada), b_ada=(b_ada, g_b_ada.reshape(1, N_MOD * d), m_b_ada, v_b_ada),
        ffn1_w_gate_up=(ffn1_w_gate_up, g_gu1, m_ffn1_w_gate_up, v_ffn1_w_gate_up),
        ffn1_w_down=(ffn1_w_down, g_d1, m_ffn1_w_down, v_ffn1_w_down),
        ln1_g=(ln1_g, g_ln[0], m_ln1_g, v_ln1_g), ln1_b=(ln1_b, g_ln[1], m_ln1_b, v_ln1_b),
        w_in=(w_in, g_in, m_w_in, v_w_in), conv_w=(conv_w, g_conv, m_conv_w, v_conv_w),
        attn_sinks=(attn_sinks, g_sinks, m_attn_sinks, v_attn_sinks), w_out=(w_out, g_out, m_w_out, v_w_out),
        ln2_g=(ln2_g, g_ln[2], m_ln2_g, v_ln2_g), ln2_b=(ln2_b, g_ln[3], m_ln2_b, v_ln2_b),
        ffn2_w_gate_up=(ffn2_w_gate_up, g_gu2, m_ffn2_w_gate_up, v_ffn2_w_gate_up),
        ffn2_w_down=(ffn2_w_down, g_d2, m_ffn2_w_down, v_ffn2_w_down),
        ln3_g=(ln3_g, g_ln[4], m_ln3_g, v_ln3_g), ln3_b=(ln3_b, g_ln[5], m_ln3_b, v_ln3_b))
    grads_out, deltas, new_m, new_v = [], [], [], []
    for name_, (w_, g_, m_, v_) in weights.items():
        g2 = flat2(g_)
        dl, nm, nv = _adamw(flat2(w_), g2, flat2(m_), flat2(v_), name="adamw_" + name_)
        grads_out.append(g2.reshape(w_.shape))
        deltas.append(dl.reshape(w_.shape))
        new_m.append(nm.reshape(w_.shape))
        new_v.append(nv.reshape(w_.shape))
    return (loss, grad_x.reshape(x.shape), *grads_out, *deltas, *new_m, *new_v)
```

```python
import jax
import jax.numpy as jnp
from jax import lax
from jax.experimental import pallas as pl
from jax.experimental.pallas import tpu as pltpu

F32 = jnp.float32
BF16 = jnp.bfloat16
MESH = pl.DeviceIdType.MESH

D_MODEL = 1024
HEAD_DIM = 64
ATTN_WIDTH = 512
CONV_WIDTH = 512
N_Q_HEADS = 8
N_KV_HEADS = 2
GQA_GROUP = 4
KV_WIDTH = 128
WINDOW = 128
BLOCK = 128
ROT_DIM = 16
ROPE_THETA = 500000.0
N_MOD = 9
LN_EPS = 1e-5
DN_ALPHA = 2.0 ** 0.25
IN_WIDTH = 2304
N_CHIPS = 4
N_DEV = 8
SMALL_ROWS = 32

ADAM_LR = 0.001
ADAM_B1 = 0.9
ADAM_B2 = 0.999
ADAM_EPS = 1e-08
ADAM_WD = 0.01
ADAM_STEP = 10

LANE = 128
COL_CHUNK = 256
VMEM_LIMIT = 56 * 1024 * 1024


def _params(sem=None, vmem=True):
    return pltpu.CompilerParams(dimension_semantics=sem, vmem_limit_bytes=VMEM_LIMIT if vmem else None)


def _sigmoid(g):
    return 1.0 / (1.0 + jnp.exp(-g))


def _row_sum(v):
    return jnp.sum(v, axis=0, keepdims=True)


def _ln_stats(r):
    mu = jnp.mean(r, axis=-1, keepdims=True)
    rc = r - mu
    var = jnp.mean(rc * rc, axis=-1, keepdims=True)
    rstd = lax.rsqrt(var + LN_EPS)
    return rc * rstd, rstd


def _ln_bwd(dxo, xhat, rstd, g):
    dxhat = dxo * g
    m1 = jnp.mean(dxhat, axis=-1, keepdims=True)
    m2 = jnp.mean(dxhat * xhat, axis=-1, keepdims=True)
    return rstd * (dxhat - m1 - xhat * m2)


def _dot_nt(a, b):
    return lax.dot_general(a, b, (((1,), (1,)), ((), ())), preferred_element_type=F32)


def _dot_tn(a, b):
    return lax.dot_general(a, b, (((0,), (0,)), ((), ())), preferred_element_type=F32)


def _full(shape):
    nd = len(shape)
    return pl.BlockSpec(shape, lambda *_: (0,) * nd)


def _resident(shape):
    nd = len(shape)
    return pl.BlockSpec(shape, lambda *_: (0,) * nd, pipeline_mode=pl.Buffered(1))


ANY_SPEC = pl.BlockSpec(memory_space=pl.ANY)


def _pcall(body, *, name, grid, in_specs, out_specs, out_shape, args, scratch_shapes=(), comm=None):
    single = not isinstance(out_shape, (list, tuple))
    out_specs = [out_specs] if single else list(out_specs)
    out_shape = [out_shape] if single else list(out_shape)
    in_specs = list(in_specs)
    scratch_shapes = list(scratch_shapes)
    sem = ("arbitrary",) * len(grid)
    if comm is None:
        res = pl.pallas_call(body, name=name, grid=grid, in_specs=in_specs, out_specs=out_specs, out_shape=out_shape,
                             scratch_shapes=scratch_shapes, compiler_params=_params(sem))(*args)
        return res[0] if single else res
    n_in, n_out, n_scr = len(in_specs), len(out_specs), len(scratch_shapes)
    nci, nco = len(comm.inputs), len(comm.out_shapes)

    def wrapped(*refs):
        ins, refs = refs[:n_in], refs[n_in:]
        cin, refs = refs[:nci], refs[nci:]
        outs, refs = refs[:n_out], refs[n_out:]
        cout, refs = refs[:nco], refs[nco:]
        scr, csems = refs[:n_scr], refs[n_scr:]
        first = pl.program_id(0) == 0
        last = pl.program_id(0) == grid[0] - 1
        for ax in range(1, len(grid)):
            first = jnp.logical_and(first, pl.program_id(ax) == 0)
            last = jnp.logical_and(last, pl.program_id(ax) == grid[ax] - 1)

        @pl.when(first)
        def _():
            comm.start(cin, cout, csems)

        body(*ins, *outs, *scr)

        @pl.when(last)
        def _():
            comm.finish(cin, cout, csems)

    res = pl.pallas_call(
        wrapped, name=name, grid=grid, in_specs=in_specs + [ANY_SPEC] * nci, out_specs=out_specs + [ANY_SPEC] * nco,
        out_shape=out_shape + list(comm.out_shapes), scratch_shapes=scratch_shapes + list(comm.sems),
        input_output_aliases={n_in + i: n_out + o for i, o in comm.aliases.items()},
        compiler_params=_params(sem))(*args, *comm.inputs)
    main = res[:n_out]
    return (main[0] if single else main), list(res[n_out:])


def _comm_call(job, *, name):
    nci, nco = len(job.inputs), len(job.out_shapes)

    def body(*refs):
        cin, refs = refs[:nci], refs[nci:]
        cout, csems = refs[:nco], refs[nco:]
        job.start(cin, cout, csems)
        job.finish(cin, cout, csems)

    return pl.pallas_call(
        body, name=name, out_shape=list(job.out_shapes), in_specs=[ANY_SPEC] * nci, out_specs=[ANY_SPEC] * nco,
        scratch_shapes=list(job.sems), input_output_aliases=dict(job.aliases))(*job.inputs)


def _ffn_up(xin, lnp, mod, w, *, seq, sc_idx, sh_idx, use_ln, name, comm=None):
    t, d = xin.shape
    f = w.shape[1] // 2
    tm = min(512, seq)
    tpb = seq // tm
    ch = min(COL_CHUNK, f)

    def body(x_ref, ln_ref, mod_ref, w_ref, h_ref, a_ref, gu_ref):
        x = x_ref[...]
        if use_ln:
            x = x * ln_ref[0:1, :] + ln_ref[1:2, :]
        h = x * (1.0 + mod_ref[0, sc_idx:sc_idx + 1, :]) + mod_ref[0, sh_idx:sh_idx + 1, :]
        hb = h.astype(BF16)
        h_ref[...] = hb
        for j in range(f // ch):
            g = jnp.dot(hb, w_ref[:, j * ch:(j + 1) * ch], preferred_element_type=F32)
            u = jnp.dot(hb, w_ref[:, f + j * ch:f + (j + 1) * ch], preferred_element_type=F32)
            a_ref[:, j * ch:(j + 1) * ch] = (g * _sigmoid(g) * u).astype(BF16)
            gu_ref[:, j * ch:(j + 1) * ch] = g.astype(BF16)
            gu_ref[:, f + j * ch:f + (j + 1) * ch] = u.astype(BF16)

    return _pcall(
        body, name=name, grid=(t // tm,),
        in_specs=[pl.BlockSpec((tm, d), lambda i: (i, 0)), _full((2, d)),
                  pl.BlockSpec((1, N_MOD, d), lambda i: (i // tpb, 0, 0)), _resident((d, 2 * f))],
        out_specs=[pl.BlockSpec((tm, d), lambda i: (i, 0)), pl.BlockSpec((tm, f), lambda i: (i, 0)),
                   pl.BlockSpec((tm, 2 * f), lambda i: (i, 0))],
        out_shape=[jax.ShapeDtypeStruct((t, d), BF16), jax.ShapeDtypeStruct((t, f), BF16),
                   jax.ShapeDtypeStruct((t, 2 * f), BF16)],
        args=(xin, lnp, mod, w), comm=comm)


def _ffn_down_ln(a, wd, xin, lnp_in, mod, *, seq, gate_idx, use_ln, name, comm=None):
    t, f = a.shape
    d = wd.shape[1]
    tm = min(512, seq)
    tpb = seq // tm

    def body(a_ref, wd_ref, x_ref, ln_ref, mod_ref, f_ref, xhat_ref, rstd_ref):
        fo = jnp.dot(a_ref[...], wd_ref[...], preferred_element_type=F32)
        x = x_ref[...]
        if use_ln:
            x = x * ln_ref[0:1, :] + ln_ref[1:2, :]
        r = DN_ALPHA * x + 0.5 * (1.0 + mod_ref[0, gate_idx:gate_idx + 1, :]) * fo
        xhat, rstd = _ln_stats(r)
        f_ref[...] = fo.astype(BF16)
        xhat_ref[...] = xhat
        rstd_ref[...] = rstd

    return _pcall(
        body, name=name, grid=(t // tm,),
        in_specs=[pl.BlockSpec((tm, f), lambda i: (i, 0)), _resident((f, d)),
                  pl.BlockSpec((tm, d), lambda i: (i, 0)), _full((2, d)),
                  pl.BlockSpec((1, N_MOD, d), lambda i: (i // tpb, 0, 0))],
        out_specs=[pl.BlockSpec((tm, d), lambda i: (i, 0)), pl.BlockSpec((tm, d), lambda i: (i, 0)),
                   pl.BlockSpec((tm, 1), lambda i: (i, 0))],
        out_shape=[jax.ShapeDtypeStruct((t, d), BF16), jax.ShapeDtypeStruct((t, d), F32),
                   jax.ShapeDtypeStruct((t, 1), F32)],
        args=(a, wd, xin, lnp_in, mod), comm=comm)


def _ffn_down_loss(a, wd, xhat_in, lnp_in, mod, lnp_out, tgt, *, seq, gate_idx, name):
    t, f = a.shape
    d = wd.shape[1]
    nb = t // seq
    tm = min(512, seq)
    tpb = seq // tm

    def body(a_ref, wd_ref, x_ref, lnin_ref, mod_ref, lnout_ref, tgt_ref,
             dr_ref, df_ref, loss_ref, dg_ref, db_ref, dgate_ref):
        i = pl.program_id(0)
        fo = jnp.dot(a_ref[...], wd_ref[...], preferred_element_type=F32)
        x = x_ref[...] * lnin_ref[0:1, :] + lnin_ref[1:2, :]
        gate = mod_ref[0, gate_idx:gate_idx + 1, :]
        r = DN_ALPHA * x + 0.5 * (1.0 + gate) * fo
        xhat, rstd = _ln_stats(r)
        g_out = lnout_ref[0:1, :]
        y = xhat * g_out + lnout_ref[1:2, :]
        e = y - tgt_ref[...]
        dy = e * (1.0 / d)

        @pl.when(i == 0)
        def _():
            loss_ref[...] = jnp.zeros_like(loss_ref)
            dg_ref[...] = jnp.zeros_like(dg_ref)
            db_ref[...] = jnp.zeros_like(db_ref)

        @pl.when(i % tpb == 0)
        def _():
            dgate_ref[...] = jnp.zeros_like(dgate_ref)

        loss_ref[...] += _row_sum(e * e)
        dg_ref[...] += _row_sum(dy * xhat)
        db_ref[...] += _row_sum(dy)
        dr = _ln_bwd(dy, xhat, rstd, g_out)
        dgate_ref[0] += _row_sum(0.5 * fo * dr)
        dr_ref[...] = dr
        df_ref[...] = (0.5 * (1.0 + gate) * dr).astype(BF16)

    return pl.pallas_call(
        body, name=name, grid=(t // tm,),
        in_specs=[pl.BlockSpec((tm, f), lambda i: (i, 0)), _resident((f, d)),
                  pl.BlockSpec((tm, d), lambda i: (i, 0)), _full((2, d)),
                  pl.BlockSpec((1, N_MOD, d), lambda i: (i // tpb, 0, 0)), _full((2, d)),
                  pl.BlockSpec((tm, d), lambda i: (i, 0))],
        out_specs=[pl.BlockSpec((tm, d), lambda i: (i, 0)), pl.BlockSpec((tm, d), lambda i: (i, 0)),
                   _full((1, d)), _full((1, d)), _full((1, d)),
                   pl.BlockSpec((1, 1, d), lambda i: (i // tpb, 0, 0))],
        out_shape=[jax.ShapeDtypeStruct((t, d), F32), jax.ShapeDtypeStruct((t, d), BF16),
                   jax.ShapeDtypeStruct((1, d), F32), jax.ShapeDtypeStruct((1, d), F32),
                   jax.ShapeDtypeStruct((1, d), F32), jax.ShapeDtypeStruct((nb, 1, d), F32)],
        compiler_params=_params(("arbitrary",)),
    )(a, wd, xhat_in, lnp_in, mod, lnp_out, tgt)


def _rope(v, cos, sa, sb):
    return v * cos + pltpu.roll(v, LANE - ROT_DIM // 2, 1) * sa + pltpu.roll(v, ROT_DIM // 2, 1) * sb


def _rope_t(dy, cos, sa, sb):
    return dy * cos + pltpu.roll(dy * sa, ROT_DIM // 2, 1) + pltpu.roll(dy * sb, LANE - ROT_DIM // 2, 1)


def _in_proj(xhat, lnp, mod, w_t, cos, sa, sb, *, seq, sc_idx, sh_idx, name, comm=None):
    t, d = xhat.shape
    tm = min(512, seq)
    tpb = seq // tm
    n_conv = 3 * CONV_WIDTH

    def body(x_ref, ln_ref, mod_ref, w_ref, cos_ref, sa_ref, sb_ref, h_ref, q_ref, k_ref, v_ref, ubc_ref):
        x = x_ref[...] * ln_ref[0:1, :] + ln_ref[1:2, :]
        h = x * (1.0 + mod_ref[0, sc_idx:sc_idx + 1, :]) + mod_ref[0, sh_idx:sh_idx + 1, :]
        hb = h.astype(BF16)
        h_ref[...] = hb
        cos_t, sa_t, sb_t = cos_ref[...], sa_ref[...], sb_ref[...]
        for j in range(ATTN_WIDTH // LANE):
            p = _dot_nt(hb, w_ref[j * LANE:(j + 1) * LANE, :])
            q_ref[:, j * LANE:(j + 1) * LANE] = _rope(p, cos_t, sa_t, sb_t).astype(BF16)
        p = _dot_nt(hb, w_ref[ATTN_WIDTH:ATTN_WIDTH + KV_WIDTH, :])
        k_ref[...] = _rope(p, cos_t, sa_t, sb_t).astype(BF16)
        p = _dot_nt(hb, w_ref[ATTN_WIDTH + KV_WIDTH:ATTN_WIDTH + 2 * KV_WIDTH, :])
        v_ref[...] = p.astype(BF16)
        base = ATTN_WIDTH + 2 * KV_WIDTH
        for j in range(n_conv // COL_CHUNK):
            ubc_ref[:, j * COL_CHUNK:(j + 1) * COL_CHUNK] = _dot_nt(
                hb, w_ref[base + j * COL_CHUNK:base + (j + 1) * COL_CHUNK, :])

    row = lambda w: pl.BlockSpec((tm, w), lambda i: (i, 0))
    return _pcall(
        body, name=name, grid=(t // tm,),
        in_specs=[row(d), _full((2, d)), pl.BlockSpec((1, N_MOD, d), lambda i: (i // tpb, 0, 0)),
                  _resident((IN_WIDTH, d)), row(LANE), row(LANE), row(LANE)],
        out_specs=[row(d), row(ATTN_WIDTH), row(KV_WIDTH), row(KV_WIDTH), row(n_conv)],
        out_shape=[jax.ShapeDtypeStruct((t, d), BF16), jax.ShapeDtypeStruct((t, ATTN_WIDTH), BF16),
                   jax.ShapeDtypeStruct((t, KV_WIDTH), BF16), jax.ShapeDtypeStruct((t, KV_WIDTH), BF16),
                   jax.ShapeDtypeStruct((t, n_conv), F32)],
        args=(xhat, lnp, mod, w_t, cos, sa, sb), comm=comm)


def _attn_group(q_ref, kp_ref, kc_ref, vp_ref, vc_ref, sink_ref, g, first):
    lo, hi = g * HEAD_DIM, (g + 1) * HEAD_DIM
    kk = jnp.concatenate([kp_ref[:, lo:hi], kc_ref[:, lo:hi]], axis=0)
    vv = jnp.concatenate([vp_ref[:, lo:hi], vc_ref[:, lo:hi]], axis=0)
    qs = jnp.concatenate([q_ref[:, (GQA_GROUP * g + j) * HEAD_DIM:(GQA_GROUP * g + j + 1) * HEAD_DIM]
                          for j in range(GQA_GROUP)], axis=0)
    rows = GQA_GROUP * BLOCK
    row = lax.broadcasted_iota(jnp.int32, (rows, 2 * BLOCK), 0)
    ki = lax.broadcasted_iota(jnp.int32, (rows, 2 * BLOCK), 1)
    diff = (row & (BLOCK - 1)) + BLOCK - ki
    valid = (diff >= 0) & (diff < WINDOW) & ((ki >= BLOCK) | jnp.logical_not(first))
    s = _dot_nt(qs, kk) * (HEAD_DIM ** -0.5)
    s = jnp.where(valid, s, -1e30)
    rcol = lax.broadcasted_iota(jnp.int32, (rows, 1), 0)
    sink = jnp.zeros((rows, 1), F32)
    for j in range(GQA_GROUP):
        sink = jnp.where(rcol // BLOCK == j, sink_ref[GQA_GROUP * g + j], sink)
    m = jnp.maximum(jnp.max(s, axis=1, keepdims=True), sink)
    p = jnp.exp(s - m)
    ps = jnp.exp(sink - m)
    inv = 1.0 / (jnp.sum(p, axis=1, keepdims=True) + ps)
    return qs, kk, vv, p * inv, ps * inv


def _attention(q, k, v, sinks, *, seq, name, comm=None):
    t = q.shape[0]
    nblk = seq // BLOCK

    def body(q_ref, kp_ref, kc_ref, vp_ref, vc_ref, sink_ref, o_ref):
        first = (pl.program_id(0) % nblk) == 0
        outs = []
        for g in range(N_KV_HEADS):
            _, _, vv, pn, _ = _attn_group(q_ref, kp_ref, kc_ref, vp_ref, vc_ref, sink_ref, g, first)
            o = jnp.dot(pn.astype(BF16), vv, preferred_element_type=F32)
            outs += [o[j * BLOCK:(j + 1) * BLOCK, :] for j in range(GQA_GROUP)]
        o_ref[...] = jnp.concatenate(outs, axis=1).astype(BF16)

    cur = lambda w: pl.BlockSpec((BLOCK, w), lambda n: (n, 0))
    prev = lambda w: pl.BlockSpec((BLOCK, w), lambda n: (jnp.maximum(n - 1, 0), 0))
    return _pcall(
        body, name=name, grid=(t // BLOCK,),
        in_specs=[cur(ATTN_WIDTH), prev(KV_WIDTH), cur(KV_WIDTH), prev(KV_WIDTH), cur(KV_WIDTH),
                  pl.BlockSpec(memory_space=pltpu.SMEM)],
        out_specs=cur(ATTN_WIDTH),
        out_shape=jax.ShapeDtypeStruct((t, ATTN_WIDTH), BF16),
        args=(q, k, k, v, v, sinks), comm=comm)


def _out_proj(attn, ubc, cw, wout, xhat_in, lnp_in, mod, *, seq, gate_idx, name, comm=None):
    t, d = xhat_in.shape
    tm = min(512, seq)
    tpb = seq // tm
    cwid = CONV_WIDTH

    def body(attn_ref, ubc_ref, halo_ref, cw_ref, w_ref, x_ref, ln_ref, mod_ref,
             mixin_ref, mix_ref, xhat_ref, rstd_ref, zbuf):
        first = (pl.program_id(0) % tpb) == 0
        u, bg, cg = ubc_ref[:, 0:cwid], ubc_ref[:, cwid:2 * cwid], ubc_ref[:, 2 * cwid:3 * cwid]
        z = cg * u
        hz = halo_ref[:, 2 * cwid:3 * cwid] * halo_ref[:, 0:cwid]
        zbuf[0:8, :] = jnp.where(first, 0.0, hz)
        zbuf[8:8 + tm, :] = z
        y = cw_ref[0:1, :] * zbuf[6:6 + tm, :] + cw_ref[1:2, :] * zbuf[7:7 + tm, :] + cw_ref[2:3, :] * z
        co = (bg * y).astype(BF16)
        at = attn_ref[...]
        mixin_ref[:, 0:ATTN_WIDTH] = at
        mixin_ref[:, ATTN_WIDTH:] = co
        mix = (jnp.dot(at, w_ref[0:ATTN_WIDTH, :], preferred_element_type=F32)
               + jnp.dot(co, w_ref[ATTN_WIDTH:, :], preferred_element_type=F32))
        x = x_ref[...] * ln_ref[0:1, :] + ln_ref[1:2, :]
        r = DN_ALPHA * x + (1.0 + mod_ref[0, gate_idx:gate_idx + 1, :]) * mix
        xhat, rstd = _ln_stats(r)
        mix_ref[...] = mix.astype(BF16)
        xhat_ref[...] = xhat
        rstd_ref[...] = rstd

    row = lambda w: pl.BlockSpec((tm, w), lambda i: (i, 0))
    return _pcall(
        body, name=name, grid=(t // tm,),
        in_specs=[row(ATTN_WIDTH), row(3 * cwid),
                  pl.BlockSpec((8, 3 * cwid), lambda i: (jnp.maximum(i * (tm // 8) - 1, 0), 0)),
                  _full((8, cwid)), _resident((d, d)), row(d), _full((2, d)),
                  pl.BlockSpec((1, N_MOD, d), lambda i: (i // tpb, 0, 0))],
        out_specs=[row(d), row(d), row(d), row(1)],
        out_shape=[jax.ShapeDtypeStruct((t, d), BF16), jax.ShapeDtypeStruct((t, d), BF16),
                   jax.ShapeDtypeStruct((t, d), F32), jax.ShapeDtypeStruct((t, 1), F32)],
        scratch_shapes=[pltpu.VMEM((tm + 8, cwid), F32)],
        args=(attn, ubc, ubc, cw, wout, xhat_in, lnp_in, mod), comm=comm)


def _ffn_bwd_act(df, wd, gu, *, seq, name, comm=None):
    t, d = df.shape
    f = wd.shape[0]
    tm = min(512, seq)
    ch = min(COL_CHUNK, f)

    def body(df_ref, wd_ref, gu_ref, dgu_ref):
        dfv = df_ref[...]
        for j in range(f // ch):
            da = _dot_nt(dfv, wd_ref[j * ch:(j + 1) * ch, :])
            g = gu_ref[:, j * ch:(j + 1) * ch].astype(F32)
            u = gu_ref[:, f + j * ch:f + (j + 1) * ch].astype(F32)
            s = _sigmoid(g)
            dgu_ref[:, j * ch:(j + 1) * ch] = (da * u * (s * (1.0 + g * (1.0 - s)))).astype(BF16)
            dgu_ref[:, f + j * ch:f + (j + 1) * ch] = (da * (g * s)).astype(BF16)

    return _pcall(
        body, name=name, grid=(t // tm,),
        in_specs=[pl.BlockSpec((tm, d), lambda i: (i, 0)), _resident((f, d)),
                  pl.BlockSpec((tm, 2 * f), lambda i: (i, 0))],
        out_specs=pl.BlockSpec((tm, 2 * f), lambda i: (i, 0)),
        out_shape=jax.ShapeDtypeStruct((t, 2 * f), BF16),
        args=(df, wd, gu), comm=comm)


def _bwd_in(a, w, dr, xin, rstd_prev, lnp_prev, mod, branch_prev, *, seq, w_is_nt, tk, sc_idx, gate_idx,
            branch_scale, final, name, comm=None):
    t, kdim = a.shape
    d = dr.shape[1]
    nb = t // seq
    tm = min(512, seq)
    tpb = seq // tm
    nk = kdim // tk

    def body(*refs):
        if final:
            a_ref, w_ref, dr_ref, x_ref, mod_ref, dx_ref, dsc_ref, dsh_ref, acc = refs
        else:
            (a_ref, w_ref, dr_ref, x_ref, rstd_ref, ln_ref, mod_ref, br_ref,
             drp_ref, dbr_ref, dsc_ref, dsh_ref, dgate_ref, dg_ref, db_ref, acc) = refs
        i, k = pl.program_id(0), pl.program_id(1)
        part = _dot_nt(a_ref[...], w_ref[...]) if w_is_nt else jnp.dot(
            a_ref[...], w_ref[...], preferred_element_type=F32)

        @pl.when(k == 0)
        def _():
            acc[...] = part

        @pl.when(k > 0)
        def _():
            acc[...] += part

        @pl.when(k == nk - 1)
        def _():
            dh = acc[...]
            first_of_batch = (i % tpb) == 0

            @pl.when(first_of_batch)
            def _():
                dsc_ref[...] = jnp.zeros_like(dsc_ref)
                dsh_ref[...] = jnp.zeros_like(dsh_ref)
                if not final:
                    dgate_ref[...] = jnp.zeros_like(dgate_ref)

            if final:
                x = x_ref[...]
            else:
                @pl.when(i == 0)
                def _():
                    dg_ref[...] = jnp.zeros_like(dg_ref)
                    db_ref[...] = jnp.zeros_like(db_ref)
                xhat = x_ref[...]
                g_prev = ln_ref[0:1, :]
                x = xhat * g_prev + ln_ref[1:2, :]
            dsc_ref[0] += _row_sum(dh * x)
            dsh_ref[0] += _row_sum(dh)
            dx = DN_ALPHA * dr_ref[...] + dh * (1.0 + mod_ref[0, sc_idx:sc_idx + 1, :])
            if final:
                dx_ref[...] = dx
            else:
                dg_ref[...] += _row_sum(dx * xhat)
                db_ref[...] += _row_sum(dx)
                drp = _ln_bwd(dx, xhat, rstd_ref[...], g_prev)
                dgate_ref[0] += _row_sum(branch_scale * br_ref[...].astype(F32) * drp)
                drp_ref[...] = drp
                dbr_ref[...] = (branch_scale * (1.0 + mod_ref[0, gate_idx:gate_idx + 1, :]) * drp).astype(BF16)

    row = lambda w_: pl.BlockSpec((tm, w_), lambda i, k: (i, 0))
    vec = pl.BlockSpec((1, 1, d), lambda i, k: (i // tpb, 0, 0))
    w_spec = (pl.BlockSpec((d, tk), lambda i, k: (0, k)) if w_is_nt else pl.BlockSpec((tk, d), lambda i, k: (k, 0)))
    mod_spec = pl.BlockSpec((1, N_MOD, d), lambda i, k: (i // tpb, 0, 0))
    a_spec = pl.BlockSpec((tm, tk), lambda i, k: (i, k))
    vshape = jax.ShapeDtypeStruct((nb, 1, d), F32)
    if final:
        in_specs = [a_spec, w_spec, row(d), row(d), mod_spec]
        args = (a, w, dr, xin, mod)
        out_specs = [row(d), vec, vec]
        out_shape = [jax.ShapeDtypeStruct((t, d), F32), vshape, vshape]
    else:
        in_specs = [a_spec, w_spec, row(d), row(d), row(1), _full((2, d)), mod_spec, row(d)]
        args = (a, w, dr, xin, rstd_prev, lnp_prev, mod, branch_prev)
        out_specs = [row(d), row(d), vec, vec, vec, _full((1, d)), _full((1, d))]
        out_shape = [jax.ShapeDtypeStruct((t, d), F32), jax.ShapeDtypeStruct((t, d), BF16), vshape, vshape, vshape,
                     jax.ShapeDtypeStruct((1, d), F32), jax.ShapeDtypeStruct((1, d), F32)]
    return _pcall(
        body, name=name, grid=(t // tm, nk), in_specs=in_specs, out_specs=out_specs, out_shape=out_shape,
        scratch_shapes=[pltpu.VMEM((tm, d), F32)], args=args, comm=comm)


def _matmul_tn(a, b, *, tmm, tnn, name, comm=None):
    t, m = a.shape
    n = b.shape[1]
    tk = min(512, t)

    def body(a_ref, b_ref, o_ref):
        @pl.when(pl.program_id(2) == 0)
        def _():
            o_ref[...] = jnp.zeros_like(o_ref)
        o_ref[...] += _dot_tn(a_ref[...], b_ref[...])

    return _pcall(
        body, name=name, grid=(m // tmm, n // tnn, t // tk),
        in_specs=[pl.BlockSpec((tk, tmm), lambda i, j, k: (k, i)), pl.BlockSpec((tk, tnn), lambda i, j, k: (k, j))],
        out_specs=pl.BlockSpec((tmm, tnn), lambda i, j, k: (i, j)),
        out_shape=jax.ShapeDtypeStruct((m, n), F32),
        args=(a, b), comm=comm)


def _matmul_nt_bf16(a, w, *, seq, name):
    t, kdim = a.shape
    n = w.shape[0]
    tm = min(512, seq)

    def body(a_ref, w_ref, o_ref):
        av = a_ref[...]
        for j in range(n // COL_CHUNK):
            o_ref[:, j * COL_CHUNK:(j + 1) * COL_CHUNK] = _dot_nt(
                av, w_ref[j * COL_CHUNK:(j + 1) * COL_CHUNK, :]).astype(BF16)

    return pl.pallas_call(
        body, name=name, grid=(t // tm,),
        in_specs=[pl.BlockSpec((tm, kdim), lambda i: (i, 0)), _resident((n, kdim))],
        out_specs=pl.BlockSpec((tm, n), lambda i: (i, 0)),
        out_shape=jax.ShapeDtypeStruct((t, n), BF16),
        compiler_params=_params(("arbitrary",)),
    )(a, w)


def _attention_bwd(q, k, v, dmixin, sinks, *, seq, name, comm=None):
    t = q.shape[0]
    nblk = seq // BLOCK

    def body(q_ref, kp_ref, kc_ref, vp_ref, vc_ref, do_ref, sink_ref,
             dq_ref, dkp_ref, dkc_ref, dvp_ref, dvc_ref, dsink_ref):
        n = pl.program_id(0)
        first = (n % nblk) == 0

        @pl.when(n == 0)
        def _():
            dsink_ref[...] = jnp.zeros_like(dsink_ref)

        dqs, dks, dvs = [], [], []
        srow = lax.broadcasted_iota(jnp.int32, (8, LANE), 0)
        dsink = jnp.zeros((8, LANE), F32)
        for g in range(N_KV_HEADS):
            qs, kk, vv, pn, psn = _attn_group(q_ref, kp_ref, kc_ref, vp_ref, vc_ref, sink_ref, g, first)
            dos = jnp.concatenate([do_ref[:, (GQA_GROUP * g + j) * HEAD_DIM:(GQA_GROUP * g + j + 1) * HEAD_DIM]
                                   for j in range(GQA_GROUP)], axis=0)
            dp = _dot_nt(dos, vv)
            delta = jnp.sum(pn * dp, axis=1, keepdims=True)
            ds = pn * (dp - delta)
            dsk = psn * delta
            for j in range(GQA_GROUP):
                tot = jnp.sum(dsk[j * BLOCK:(j + 1) * BLOCK, :], axis=0, keepdims=True)
                dsink = dsink - jnp.where(srow == GQA_GROUP * g + j, tot, 0.0)
            dsb = (ds * (HEAD_DIM ** -0.5)).astype(BF16)
            dqg = jnp.dot(dsb, kk, preferred_element_type=F32)
            dqs += [dqg[j * BLOCK:(j + 1) * BLOCK, :] for j in range(GQA_GROUP)]
            dks.append(_dot_tn(dsb, qs))
            dvs.append(_dot_tn(pn.astype(BF16), dos))
        dsink_ref[...] += dsink
        dq_ref[...] = jnp.concatenate(dqs, axis=1)
        dkp_ref[...] = jnp.concatenate([x[0:BLOCK, :] for x in dks], axis=1)
        dkc_ref[...] = jnp.concatenate([x[BLOCK:, :] for x in dks], axis=1)
        dvp_ref[...] = jnp.concatenate([x[0:BLOCK, :] for x in dvs], axis=1)
        dvc_ref[...] = jnp.concatenate([x[BLOCK:, :] for x in dvs], axis=1)

    cur = lambda w: pl.BlockSpec((BLOCK, w), lambda n: (n, 0))
    prev = lambda w: pl.BlockSpec((BLOCK, w), lambda n: (jnp.maximum(n - 1, 0), 0))
    kv = jax.ShapeDtypeStruct((t, KV_WIDTH), F32)
    return _pcall(
        body, name=name, grid=(t // BLOCK,),
        in_specs=[cur(ATTN_WIDTH), prev(KV_WIDTH), cur(KV_WIDTH), prev(KV_WIDTH), cur(KV_WIDTH), cur(ATTN_WIDTH),
                  pl.BlockSpec(memory_space=pltpu.SMEM)],
        out_specs=[cur(ATTN_WIDTH), cur(KV_WIDTH), cur(KV_WIDTH), cur(KV_WIDTH), cur(KV_WIDTH), _full((8, LANE))],
        out_shape=[jax.ShapeDtypeStruct((t, ATTN_WIDTH), F32), kv, kv, kv, kv, jax.ShapeDtypeStruct((8, LANE), F32)],
        args=(q, k, k, v, v, dmixin, sinks), comm=comm)


def _mix_bwd_assemble(dq, dkp, dkc, dvp, dvc, cos, sa, sb, dmixin, ubc, cw, *, seq, name, comm=None):
    t = dq.shape[0]
    nblk = seq // BLOCK
    ntile = t // BLOCK
    cwid = CONV_WIDTH
    tm = BLOCK

    def body(dq_ref, dkc_ref, dkp_ref, dvc_ref, dvp_ref, cos_ref, sa_ref, sb_ref, dco_ref, dcon_ref,
             ubc_ref, hprev_ref, hnext_ref, cw_ref, dproj_ref, dcw_ref, zbuf, dybuf):
        i = pl.program_id(0)
        first = (i % nblk) == 0
        last = (i % nblk) == nblk - 1
        glast = i == ntile - 1

        @pl.when(i == 0)
        def _():
            dcw_ref[...] = jnp.zeros_like(dcw_ref)

        cos_t, sa_t, sb_t = cos_ref[...], sa_ref[...], sb_ref[...]
        for j in range(ATTN_WIDTH // LANE):
            dproj_ref[:, j * LANE:(j + 1) * LANE] = _rope_t(
                dq_ref[:, j * LANE:(j + 1) * LANE], cos_t, sa_t, sb_t).astype(BF16)
        dk = dkc_ref[...] + jnp.where(glast, 0.0, dkp_ref[...])
        dproj_ref[:, ATTN_WIDTH:ATTN_WIDTH + KV_WIDTH] = _rope_t(dk, cos_t, sa_t, sb_t).astype(BF16)
        dv = dvc_ref[...] + jnp.where(glast, 0.0, dvp_ref[...])
        dproj_ref[:, ATTN_WIDTH + KV_WIDTH:ATTN_WIDTH + 2 * KV_WIDTH] = dv.astype(BF16)

        u, bg, cg = ubc_ref[:, 0:cwid], ubc_ref[:, cwid:2 * cwid], ubc_ref[:, 2 * cwid:3 * cwid]
        z = cg * u
        hz = hprev_ref[:, 2 * cwid:3 * cwid] * hprev_ref[:, 0:cwid]
        zbuf[0:8, :] = jnp.where(first, 0.0, hz)
        zbuf[8:8 + tm, :] = z
        z2, z1 = zbuf[6:6 + tm, :], zbuf[7:7 + tm, :]
        w0, w1, w2 = cw_ref[0:1, :], cw_ref[1:2, :], cw_ref[2:3, :]
        y = w0 * z2 + w1 * z1 + w2 * z
        dco = dco_ref[...].astype(F32)
        dyc = dco * bg
        dyn = dcon_ref[0:8, :].astype(F32) * hnext_ref[:, cwid:2 * cwid]
        dybuf[0:tm, :] = dyc
        dybuf[tm:tm + 8, :] = jnp.where(last, 0.0, dyn)
        dz = w2 * dyc + w1 * dybuf[1:1 + tm, :] + w0 * dybuf[2:2 + tm, :]
        srow = lax.broadcasted_iota(jnp.int32, (8, cwid), 0)
        dcw_ref[...] += (jnp.where(srow == 0, _row_sum(dyc * z2), 0.0) + jnp.where(srow == 1, _row_sum(dyc * z1), 0.0)
                         + jnp.where(srow == 2, _row_sum(dyc * z), 0.0))
        base = ATTN_WIDTH + 2 * KV_WIDTH
        dproj_ref[:, base:base + cwid] = (dz * cg).astype(BF16)
        dproj_ref[:, base + cwid:base + 2 * cwid] = (dco * y).astype(BF16)
        dproj_ref[:, base + 2 * cwid:base + 3 * cwid] = (dz * u).astype(BF16)

    cur = lambda w: pl.BlockSpec((tm, w), lambda i: (i, 0))
    nxt = lambda w: pl.BlockSpec((tm, w), lambda i: (jnp.minimum(i + 1, ntile - 1), 0))
    return _pcall(
        body, name=name, grid=(ntile,),
        in_specs=[cur(ATTN_WIDTH), cur(KV_WIDTH), nxt(KV_WIDTH), cur(KV_WIDTH), nxt(KV_WIDTH),
                  cur(LANE), cur(LANE), cur(LANE),
                  pl.BlockSpec((tm, cwid), lambda i: (i, 1)),
                  pl.BlockSpec((16, cwid), lambda i: (jnp.minimum((i + 1) * (tm // 16), t // 16 - 1), 1)),
                  cur(3 * cwid),
                  pl.BlockSpec((8, 3 * cwid), lambda i: (jnp.maximum(i * (tm // 8) - 1, 0), 0)),
                  pl.BlockSpec((8, 3 * cwid), lambda i: (jnp.minimum((i + 1) * (tm // 8), t // 8 - 1), 0)),
                  _full((8, cwid))],
        out_specs=[cur(IN_WIDTH), _full((8, cwid))],
        out_shape=[jax.ShapeDtypeStruct((t, IN_WIDTH), BF16), jax.ShapeDtypeStruct((8, cwid), F32)],
        scratch_shapes=[pltpu.VMEM((tm + 8, cwid), F32), pltpu.VMEM((tm + 8, cwid), F32)],
        args=(dq, dkc, dkp, dvc, dvp, cos, sa, sb, dmixin, dmixin, ubc, ubc, ubc, cw), comm=comm)


def _ada_fwd(c_all, w_ada, b_ada_shard, *, name):
    nb, d = c_all.shape
    n = w_ada.shape[1]
    tn = n // 2

    def body(c_ref, w_ref, b_ref, o_ref):
        cv = c_ref[...]
        cond = cv * _sigmoid(cv)
        o_ref[...] = jnp.dot(cond, w_ref[...], preferred_element_type=F32,
                             precision=lax.Precision.HIGHEST) + b_ref[...]

    return pl.pallas_call(
        body, name=name, grid=(n // tn,),
        in_specs=[_full((nb, d)), pl.BlockSpec((d, tn), lambda j: (0, j)), pl.BlockSpec((1, tn), lambda j: (0, j))],
        out_specs=pl.BlockSpec((nb, tn), lambda j: (0, j)),
        out_shape=jax.ShapeDtypeStruct((nb, n), F32),
        compiler_params=_params(("arbitrary",)),
    )(c_all, w_ada, b_ada_shard)


def _small_finish(gathered, dmod_all, dmod_shard, c_all_t, *, name):
    d = D_MODEL
    nb, n = dmod_shard.shape

    def body(g_ref, dm_ref, dms_ref, ct_ref, sum_ref, gw_ref, gb_ref):
        total = g_ref[0]
        for dev in range(1, N_DEV):
            total = total + g_ref[dev]
        sum_ref[...] = total
        gb_ref[...] = _row_sum(dm_ref[...])
        ctv = ct_ref[...]
        cond_t = ctv * _sigmoid(ctv)
        for jb in range(n // COL_CHUNK):
            gw_ref[:, jb * COL_CHUNK:(jb + 1) * COL_CHUNK] = jnp.dot(
                cond_t, dms_ref[:, jb * COL_CHUNK:(jb + 1) * COL_CHUNK], preferred_element_type=F32,
                precision=lax.Precision.HIGHEST)

    return pl.pallas_call(
        body, name=name, grid=(1,),
        in_specs=[_full((N_DEV, SMALL_ROWS, d)), _full((nb, N_MOD * d)), _full((nb, n)), _full((d, nb))],
        out_specs=[_full((SMALL_ROWS, d)), _full((d, n)), _full((1, N_MOD * d))],
        out_shape=[jax.ShapeDtypeStruct((SMALL_ROWS, d), F32), jax.ShapeDtypeStruct((d, n), F32),
                   jax.ShapeDtypeStruct((1, N_MOD * d), F32)],
        compiler_params=_params(("arbitrary",)),
    )(gathered, dmod_all, dmod_shard, c_all_t)


def _row_tile(r, c, budget=1 << 20):
    if r * c * 4 <= budget or r % 16:
        return r
    best = 16
    for tr in range(16, r + 1, 16):
        if r % tr == 0 and tr * c * 4 <= budget:
            best = tr
    return best


def _cast_bf16(w, *, name):
    r, c = w.shape
    tr = _row_tile(r, c)

    def body(w_ref, o_ref):
        o_ref[...] = w_ref[...].astype(BF16)

    return pl.pallas_call(
        body, name=name, grid=(r // tr,),
        in_specs=[pl.BlockSpec((tr, c), lambda i: (i, 0))], out_specs=pl.BlockSpec((tr, c), lambda i: (i, 0)),
        out_shape=jax.ShapeDtypeStruct((r, c), BF16), compiler_params=_params(("arbitrary",)),
    )(w)


def _adamw(w, g, m, v, *, name, comm=None):
    r, c = w.shape
    tr = _row_tile(r, c)
    c1 = 1.0 - ADAM_B1 ** ADAM_STEP
    c2 = 1.0 - ADAM_B2 ** ADAM_STEP

    def body(w_ref, g_ref, m_ref, v_ref, d_ref, nm_ref, nv_ref):
        gv = g_ref[...]
        m2 = ADAM_B1 * m_ref[...] + (1.0 - ADAM_B1) * gv
        v2 = ADAM_B2 * v_ref[...] + (1.0 - ADAM_B2) * (gv * gv)
        d_ref[...] = -ADAM_LR * ((m2 / c1) / (jnp.sqrt(v2 / c2) + ADAM_EPS) + ADAM_WD * w_ref[...])
        nm_ref[...] = m2
        nv_ref[...] = v2

    spec = pl.BlockSpec((tr, c), lambda i: (i, 0))
    sh = jax.ShapeDtypeStruct((r, c), F32)
    return _pcall(body, name=name, grid=(r // tr,), in_specs=[spec] * 4, out_specs=[spec] * 3, out_shape=[sh] * 3,
                  args=(w, g, m, v), comm=comm)


def _sum_pair(pos, g3, r3, blk_of, *, name):
    n, rows, cols = r3.shape
    tr = _row_tile(rows, cols)

    def body(pos_ref, g_ref, r_ref, s32_ref, s16_ref):
        s = g_ref[0] + r_ref[0]
        s32_ref[0] = s
        s16_ref[0] = s.astype(BF16)

    own = pl.BlockSpec((1, tr, cols), lambda p, i, pos: (blk_of(p, pos), i, 0))
    plain = pl.BlockSpec((1, tr, cols), lambda p, i, pos: (p, i, 0))
    grid_spec = pltpu.PrefetchScalarGridSpec(num_scalar_prefetch=1, grid=(n, rows // tr), in_specs=[own, plain],
                                             out_specs=[plain, plain])
    return pl.pallas_call(
        body, name=name, grid_spec=grid_spec,
        out_shape=[jax.ShapeDtypeStruct((n, rows, cols), F32), jax.ShapeDtypeStruct((n, rows, cols), BF16)],
        compiler_params=_params(("arbitrary", "arbitrary")),
    )(pos, g3, r3)


def _sum_final(pos, s32, recv, *, col_kind, n_shard, name):
    if col_kind:
        rows, cols = s32.shape[1], n_shard
        own = lambda tr: pl.BlockSpec((1, tr, cols), lambda i, pos: (0, i, 2 * pos[0] + pos[1]))
    else:
        rows, cols = s32.shape[1], s32.shape[2]
        own = lambda tr: pl.BlockSpec((1, tr, cols), lambda i, pos: (2 * pos[0] + pos[1], i, 0))
    tr = _row_tile(rows, cols)

    def body(pos_ref, s_ref, r_ref, o_ref):
        o_ref[0] = ((s_ref[0] + r_ref[0].astype(F32)) + r_ref[1].astype(F32)) + r_ref[2].astype(F32)

    grid_spec = pltpu.PrefetchScalarGridSpec(
        num_scalar_prefetch=1, grid=(rows // tr,),
        in_specs=[own(tr), pl.BlockSpec((3, tr, cols), lambda i, pos: (0, i, 0))],
        out_specs=pl.BlockSpec((1, tr, cols), lambda i, pos: (pos[2], i, 0)))
    return pl.pallas_call(
        body, name=name, grid_spec=grid_spec, out_shape=jax.ShapeDtypeStruct((2, rows, cols), F32),
        compiler_params=_params(("arbitrary",)),
    )(pos, s32, recv)


def _position():
    return lax.axis_index("x"), lax.axis_index("y"), lax.axis_index("c")


def _allgather8(x_shard, *, name):
    m_per, n = x_shard.shape

    def body(x_ref, out_ref, send_sems, recv_sems, local_sem):
        x, y, c = _position()
        me, sibling = (x, y, c), (x, y, 1 - c)
        chips = [(1 - x, y), (x, 1 - y), (1 - x, 1 - y)]

        def rows(px, py, pc):
            return out_ref.at[pl.ds((4 * px + 2 * py + pc) * m_per, m_per), :]

        def copy(k, block, to, src=None):
            return pltpu.make_async_remote_copy(
                src_ref=rows(*block) if src is None else src, dst_ref=rows(*block),
                send_sem=send_sems.at[k], recv_sem=recv_sems.at[k], device_id=to, device_id_type=MESH)

        mine = pltpu.make_async_copy(x_ref, rows(*me), local_sem)
        mine.start()
        first = [copy(0, me, sibling, src=x_ref)]
        first += [copy(1 + j, me, (*chip, c), src=x_ref) for j, chip in enumerate(chips)]
        for cp in first:
            cp.start()
        passed = [copy(4 + j, (*chip, c), sibling) for j, chip in enumerate(chips)]
        for j, chip in enumerate(chips):
            copy(1 + j, (*chip, c), me).wait_recv()
            passed[j].start()
        copy(0, sibling, me).wait_recv()
        for j, chip in enumerate(chips):
            copy(4 + j, (*chip, 1 - c), me).wait_recv()
        for cp in first + passed:
            cp.wait_send()
        mine.wait()

    return pl.pallas_call(
        body, name=name,
        out_shape=jax.ShapeDtypeStruct((N_DEV * m_per, n), x_shard.dtype),
        in_specs=[pl.BlockSpec(memory_space=pltpu.VMEM)], out_specs=pl.BlockSpec(memory_space=pltpu.VMEM),
        scratch_shapes=[pltpu.SemaphoreType.DMA((7,)), pltpu.SemaphoreType.DMA((7,)), pltpu.SemaphoreType.DMA],
    )(x_shard)


def _peer_chips(x, y):
    return [(1 - x, y), (x, 1 - y), (1 - x, 1 - y)]


class _GatherJob:
    def __init__(self, pieces):
        self.pieces = pieces
        n_p = len(pieces)
        self.inputs = [p[0] for p in pieces] + [p[4] for p in pieces if p[4] is not None]
        self.out_shapes = [
            jax.ShapeDtypeStruct((s.shape[0], s.shape[1] * N_CHIPS) if kd else (s.shape[0] * N_CHIPS, s.shape[1]), s.dtype)
            for s, kd, _, _, _ in pieces]
        self.aliases = {}
        nxt = n_p
        for o, p in enumerate(pieces):
            if p[4] is not None:
                self.aliases[nxt] = o
                nxt += 1
        self.sems = [pltpu.SemaphoreType.DMA((3 * n_p,))] * 4 + [pltpu.SemaphoreType.DMA((n_p,))]

    def _rows(self, p, half):
        shard, _, sub, nsub, _ = self.pieces[p]
        hr = shard.shape[0] // 2
        sr = hr // nsub
        return half * hr + sub * sr, sr

    def _dst(self, cout, p, chip_idx, half):
        shard, col_kind = self.pieces[p][0], self.pieces[p][1]
        r0, nr = self._rows(p, half)
        if col_kind:
            n = shard.shape[1]
            return cout[p].at[pl.ds(r0, nr), pl.ds(chip_idx * n, n)]
        n = shard.shape[0]
        return cout[p].at[pl.ds(chip_idx * n + r0, nr), :]

    def _copies(self, cin, cout, sems):
        send_sems, recv_sems, fsend_sems, frecv_sems, local_sems = sems
        x, y, c = _position()
        k = 2 * x + y
        sibling = (x, y, 1 - c)
        local, sends, arrivals, fwds, fwd_arrivals = [], [], [], [], []
        for p, (shard, col_kind, sub, _, _) in enumerate(self.pieces):
            if sub == 0:
                n = shard.shape[1] if col_kind else shard.shape[0]
                own = cout[p].at[:, pl.ds(k * n, n)] if col_kind else cout[p].at[pl.ds(k * n, n), :]
                local.append(pltpu.make_async_copy(cin[p], own, local_sems.at[p]))
            r0, nr = self._rows(p, c)
            for j, chip in enumerate(_peer_chips(x, y)):
                idx = 3 * p + j

                def remote(src, dst, ssem, rsem, to):
                    return pltpu.make_async_remote_copy(src_ref=src, dst_ref=dst, send_sem=ssem, recv_sem=rsem,
                                                        device_id=to, device_id_type=MESH)

                sends.append(remote(cin[p].at[pl.ds(r0, nr), :], self._dst(cout, p, k, c),
                                    send_sems.at[idx], recv_sems.at[idx], (*chip, c)))
                landed = self._dst(cout, p, 2 * chip[0] + chip[1], c)
                arrivals.append(remote(landed, landed, send_sems.at[idx], recv_sems.at[idx], (*chip, c)))
                fwds.append(remote(landed, landed, fsend_sems.at[idx], frecv_sems.at[idx], sibling))
                other = self._dst(cout, p, 2 * chip[0] + chip[1], 1 - c)
                fwd_arrivals.append(remote(other, other, fsend_sems.at[idx], frecv_sems.at[idx], sibling))
        return local, sends, arrivals, fwds, fwd_arrivals

    def start(self, cin, cout, sems):
        local, sends, _, _, _ = self._copies(cin, cout, sems)
        for cp in local + sends:
            cp.start()

    def finish(self, cin, cout, sems):
        local, sends, arrivals, fwds, fwd_arrivals = self._copies(cin, cout, sems)
        for arrived, fw in zip(arrivals, fwds):
            arrived.wait_recv()
            fw.start()
        for arrived in fwd_arrivals:
            arrived.wait_recv()
        for cp in sends + fwds:
            cp.wait_send()
        for cp in local:
            cp.wait()


class _PairedJob:
    aliases = {}

    def start(self, cin, cout, sems):
        for cp in self._copies(cin, cout, sems):
            cp.start()

    def finish(self, cin, cout, sems):
        copies = self._copies(cin, cout, sems)
        for cp in copies:
            cp.wait_recv()
        for cp in copies:
            cp.wait_send()


class _SwapJob(_PairedJob):
    def __init__(self, grads, kinds):
        self.inputs, self.kinds = list(grads), list(kinds)
        self.out_shapes, self.n_copies = [], []
        for g, kd in zip(grads, kinds):
            if kd:
                self.out_shapes.append(jax.ShapeDtypeStruct((1, g.shape[0] // 2, g.shape[1]), g.dtype))
                self.n_copies.append(1)
            else:
                n = g.shape[0] // N_CHIPS
                self.out_shapes.append(jax.ShapeDtypeStruct((N_CHIPS, n // 2, g.shape[1]), g.dtype))
                self.n_copies.append(N_CHIPS)
        total = sum(self.n_copies)
        self.sems = [pltpu.SemaphoreType.DMA((total,)), pltpu.SemaphoreType.DMA((total,))]

    def _copies(self, cin, cout, sems):
        send_sems, recv_sems = sems
        x, y, c = _position()
        copies = []
        for p, src_ref in enumerate(cin):
            for kk in range(self.n_copies[p]):
                if self.kinds[p]:
                    hr = src_ref.shape[0] // 2
                    src = src_ref.at[pl.ds((1 - c) * hr, hr), :]
                else:
                    n = src_ref.shape[0] // N_CHIPS
                    src = src_ref.at[pl.ds(kk * n + (1 - c) * (n // 2), n // 2), :]
                idx = len(copies)
                copies.append(pltpu.make_async_remote_copy(
                    src_ref=src, dst_ref=cout[p].at[kk], send_sem=send_sems.at[idx], recv_sem=recv_sems.at[idx],
                    device_id=(x, y, 1 - c), device_id_type=MESH))
        return copies


class _ExchangeJob(_PairedJob):
    def __init__(self, s16, kinds, sizes):
        self.inputs, self.kinds, self.sizes = list(s16), list(kinds), list(sizes)
        self.out_shapes = [jax.ShapeDtypeStruct((3, s.shape[1], n if kd else s.shape[2]), s.dtype)
                           for s, kd, n in zip(s16, kinds, sizes)]
        self.sems = [pltpu.SemaphoreType.DMA((3 * len(s16),)), pltpu.SemaphoreType.DMA((3 * len(s16),))]

    def _copies(self, cin, cout, sems):
        send_sems, recv_sems = sems
        x, y, c = _position()
        copies = []
        for p, src_ref in enumerate(cin):
            for j, chip in enumerate(_peer_chips(x, y)):
                kk = 2 * chip[0] + chip[1]
                n = self.sizes[p]
                src = src_ref.at[0, :, pl.ds(kk * n, n)] if self.kinds[p] else src_ref.at[kk]
                copies.append(pltpu.make_async_remote_copy(
                    src_ref=src, dst_ref=cout[p].at[j], send_sem=send_sems.at[3 * p + j],
                    recv_sem=recv_sems.at[3 * p + j], device_id=(*chip, c), device_id_type=MESH))
        return copies


class _ShareJob:
    def __init__(self, halves):
        self.inputs = list(halves)
        self.out_shapes = [jax.ShapeDtypeStruct(h.shape, h.dtype) for h in halves]
        self.aliases = {p: p for p in range(len(halves))}
        self.sems = [pltpu.SemaphoreType.DMA((len(halves),)), pltpu.SemaphoreType.DMA((len(halves),))]

    def _copies(self, cout, sems, half):
        send_sems, recv_sems = sems
        x, y, c = _position()
        h = c if half == "mine" else 1 - c
        return [pltpu.make_async_remote_copy(
            src_ref=o.at[h], dst_ref=o.at[h], send_sem=send_sems.at[p], recv_sem=recv_sems.at[p],
            device_id=(x, y, 1 - c), device_id_type=MESH) for p, o in enumerate(cout)]

    def start(self, cin, cout, sems):
        for cp in self._copies(cout, sems, "mine"):
            cp.start()

    def finish(self, cin, cout, sems):
        for cp in self._copies(cout, sems, "theirs"):
            cp.wait_recv()
        for cp in self._copies(cout, sems, "mine"):
            cp.wait_send()


class _MultiJob:
    def __init__(self, jobs):
        self.jobs = jobs
        self.inputs = [a for j in jobs for a in j.inputs]
        self.out_shapes = [s for j in jobs for s in j.out_shapes]
        self.sems = [s for j in jobs for s in j.sems]
        self.aliases = {}
        i0 = o0 = 0
        for j in jobs:
            for i, o in j.aliases.items():
                self.aliases[i0 + i] = o0 + o
            i0 += len(j.inputs)
            o0 += len(j.out_shapes)

    def _parts(self, cin, cout, sems):
        i0 = o0 = s0 = 0
        for j in self.jobs:
            ni, no, ns = len(j.inputs), len(j.out_shapes), len(j.sems)
            yield j, cin[i0:i0 + ni], cout[o0:o0 + no], sems[s0:s0 + ns]
            i0, o0, s0 = i0 + ni, o0 + no, s0 + ns

    def start(self, cin, cout, sems):
        for j, a, b, s in self._parts(cin, cout, sems):
            j.start(a, b, s)

    def finish(self, cin, cout, sems):
        for j, a, b, s in self._parts(cin, cout, sems):
            j.finish(a, b, s)


def _rope_tables(positions):
    half = ROT_DIM // 2
    inv_freq = jnp.power(jnp.float32(ROPE_THETA), -jnp.arange(0, ROT_DIM, 2, dtype=F32) / ROT_DIM)
    ang = positions.astype(F32).reshape(-1)[:, None] * inv_freq
    cos, sin = jnp.cos(ang), jnp.sin(ang)
    t = cos.shape[0]
    ones = jnp.ones((t, HEAD_DIM - ROT_DIM), F32)
    zeros = jnp.zeros((t, HEAD_DIM - ROT_DIM), F32)
    zh = jnp.zeros((t, half), F32)
    cos_h = jnp.concatenate([cos, cos, ones], axis=1)
    sa_h = jnp.concatenate([-sin, zh, zeros], axis=1)
    sb_h = jnp.concatenate([zh, sin, zeros], axis=1)
    two = lambda a: jnp.concatenate([a, a], axis=1)
    return two(cos_h), two(sa_h), two(sb_h)


def kernel(x, c, positions, w_ada, b_ada, ffn1_w_gate_up, ffn1_w_down, ln1_g, ln1_b, w_in, conv_w, attn_sinks, w_out, ln2_g, ln2_b, ffn2_w_gate_up, ffn2_w_down, ln3_g, ln3_b, loss_target, m_w_ada, m_b_ada, m_ffn1_w_gate_up, m_ffn1_w_down, m_ln1_g, m_ln1_b, m_w_in, m_conv_w, m_attn_sinks, m_w_out, m_ln2_g, m_ln2_b, m_ffn2_w_gate_up, m_ffn2_w_down, m_ln3_g, m_ln3_b, v_w_ada, v_b_ada, v_ffn1_w_gate_up, v_ffn1_w_down, v_ln1_g, v_ln1_b, v_w_in, v_conv_w, v_attn_sinks, v_w_out, v_ln2_g, v_ln2_b, v_ffn2_w_gate_up, v_ffn2_w_down, v_ln3_g, v_ln3_b):
    d = D_MODEL
    nb, seq, _ = x.shape
    t = nb * seq
    f = ffn1_w_down.shape[1] * N_CHIPS
    ax, ay, ac = _position()
    chip = 2 * ax + ay
    dev = 2 * chip + ac
    pos = jnp.stack([ax, ay, ac]).astype(jnp.int32)

    x2 = x.reshape(t, d)
    tgt2 = loss_target.reshape(t, d)
    ln1 = jnp.concatenate([ln1_g, ln1_b], axis=0)
    ln2 = jnp.concatenate([ln2_g, ln2_b], axis=0)
    ln3 = jnp.concatenate([ln3_g, ln3_b], axis=0)
    sinks = attn_sinks.reshape(N_Q_HEADS)
    cos_t, sa_t, sb_t = _rope_tables(positions)

    n_ada = w_ada.shape[2]
    c_all = _allgather8(c.reshape(nb * d // LANE, LANE), name="gather_c").reshape(N_DEV * nb, d)
    b_shard = lax.dynamic_slice(b_ada, (0, chip * n_ada), (1, n_ada))
    mod_part = _ada_fwd(c_all, w_ada[0], b_shard, name="ada_fwd")
    conv_rows = jnp.pad(conv_w[0], ((0, 5), (0, n_ada - conv_w.shape[2])))
    part = jnp.concatenate([mod_part, conv_rows], axis=0)
    parts = _allgather8(part, name="gather_mod").reshape(N_DEV, N_DEV * nb + 8, n_ada)
    mod_all = jnp.concatenate([parts[2 * k, :N_DEV * nb, :] for k in range(N_CHIPS)], axis=1)
    mod = lax.dynamic_slice(mod_all, (dev * nb, 0), (nb, N_MOD * d)).reshape(nb, N_MOD, d)
    cw_full = jnp.concatenate([parts[2 * k, N_DEV * nb:, :conv_w.shape[2]] for k in range(N_CHIPS)], axis=1)

    shards = [_cast_bf16(ffn1_w_gate_up[0], name="cast_gu1"), _cast_bf16(ffn1_w_down[0], name="cast_d1"),
              _cast_bf16(w_in[0].T, name="cast_in"), _cast_bf16(w_out[0], name="cast_out"),
              _cast_bf16(ffn2_w_gate_up[0], name="cast_gu2"), _cast_bf16(ffn2_w_down[0], name="cast_d2")]
    sh_gu1, sh_d1, sh_in, sh_out, sh_gu2, sh_d2 = shards
    n_gu, n_d, n_in, n_out = sh_gu1.shape[1], sh_d1.shape[0], sh_in.shape[0], sh_out.shape[0]
    whole = lambda shard, col_kind: _GatherJob([(shard, col_kind, 0, 1, None)])

    (wgu1,) = _comm_call(whole(sh_gu1, True), name="gather_gu1")
    (h1, a1, gu1), (wd1,) = _ffn_up(x2, ln1, mod, wgu1, seq=seq, sc_idx=1, sh_idx=0, use_ln=False, name="ffn1_up",
                                    comm=whole(sh_d1, False))
    (f1, xhat1, rstd1), (win_t,) = _ffn_down_ln(a1, wd1, x2, ln1, mod, seq=seq, gate_idx=2, use_ln=False,
                                                name="ffn1_down", comm=whole(sh_in, False))
    (h2, q, k, v, ubc), (wout,) = _in_proj(xhat1, ln1, mod, win_t, cos_t, sa_t, sb_t, seq=seq, sc_idx=4, sh_idx=3,
                                           name="in_proj", comm=whole(sh_out, False))
    attn, (wgu2_part,) = _attention(q, k, v, sinks, seq=seq, name="attention",
                                    comm=_GatherJob([(sh_gu2, True, 0, 2, None)]))
    (mixin, mix, xhat2, rstd2), (wgu2,) = _out_proj(attn, ubc, cw_full, wout, xhat1, ln1, mod, seq=seq, gate_idx=5,
                                                    name="out_proj", comm=_GatherJob([(sh_gu2, True, 1, 2, wgu2_part)]))
    (h3, a3, gu3), (wd2,) = _ffn_up(xhat2, ln2, mod, wgu2, seq=seq, sc_idx=7, sh_idx=6, use_ln=True, name="ffn2_up",
                                    comm=whole(sh_d2, False))
    dr3, df3, loss_cols, dln3g, dln3b, dgate3 = _ffn_down_loss(
        a3, wd2, xhat2, ln2, mod, ln3, tgt2, seq=seq, gate_idx=8, name="ffn2_down_loss")

    def pair_sum(g, r3, col_kind, name_):
        if col_kind:
            g3 = g.reshape(2, g.shape[0] // 2, g.shape[1])
            blk_of = lambda p_, pos_: pos_[2]
        else:
            g3 = g.reshape(2 * N_CHIPS, g.shape[0] // (2 * N_CHIPS), g.shape[1])
            blk_of = lambda p_, pos_: 2 * p_ + pos_[2]
        return _sum_pair(pos, g3, r3, blk_of, name=name_)

    dgu3 = _ffn_bwd_act(df3, wd2, gu3, seq=seq, name="ffn2_bwd_act")
    g_wd2 = _matmul_tn(a3, df3, tmm=f // 2, tnn=d, name="grad_wd2")
    g_wgu2, (sib_d2,) = _matmul_tn(h3, dgu3, tmm=d, tnn=(2 * f) // 4, name="grad_wgu2",
                                   comm=_SwapJob([g_wd2], [False]))
    s32_d2, s16_d2 = pair_sum(g_wd2, sib_d2, False, "sum_pair_d2")
    (dr2, dmix, dsc3, dsh3, dgate2, dln2g, dln2b), (sib_gu2, recv_d2) = _bwd_in(
        dgu3, wgu2, dr3, xhat2, rstd2, ln2, mod, mix, seq=seq, w_is_nt=True, tk=(2 * f) // 4, sc_idx=7, gate_idx=5,
        branch_scale=1.0, final=False, name="ffn2_bwd_in",
        comm=_MultiJob([_SwapJob([g_wgu2], [True]), _ExchangeJob([s16_d2], [False], [n_d])]))
    s32_gu2, s16_gu2 = pair_sum(g_wgu2, sib_gu2, True, "sum_pair_gu2")
    g_wout = _matmul_tn(mixin, dmix, tmm=d, tnn=d, name="grad_wout")
    dmixin = _matmul_nt_bf16(dmix, wout, seq=seq, name="out_proj_bwd")
    (dq, dkp, dkc, dvp, dvc, dsink), (recv_gu2, sib_out) = _attention_bwd(
        q, k, v, dmixin, sinks, seq=seq, name="attention_bwd",
        comm=_MultiJob([_ExchangeJob([s16_gu2], [True], [n_gu]), _SwapJob([g_wout], [False])]))
    s32_out, s16_out = pair_sum(g_wout, sib_out, False, "sum_pair_out")
    (dproj, dcw), (recv_out,) = _mix_bwd_assemble(
        dq, dkp, dkc, dvp, dvc, cos_t, sa_t, sb_t, dmixin, ubc, cw_full, seq=seq, name="mix_bwd",
        comm=_ExchangeJob([s16_out], [False], [n_out]))
    g_win_t = _matmul_tn(dproj, h2, tmm=IN_WIDTH // 2, tnn=d, name="grad_win")
    (dr1, df1, dsc2, dsh2, dgate1, dln1g, dln1b), (sib_in,) = _bwd_in(
        dproj, win_t, dr2, xhat1, rstd1, ln1, mod, f1, seq=seq, w_is_nt=False, tk=IN_WIDTH // 3, sc_idx=4, gate_idx=2,
        branch_scale=0.5, final=False, name="in_proj_bwd", comm=_SwapJob([g_win_t], [False]))
    s32_in, s16_in = pair_sum(g_win_t, sib_in, False, "sum_pair_in")
    dgu1, (recv_in,) = _ffn_bwd_act(df1, wd1, gu1, seq=seq, name="ffn1_bwd_act",
                                    comm=_ExchangeJob([s16_in], [False], [n_in]))
    g_wgu1 = _matmul_tn(h1, dgu1, tmm=d, tnn=(2 * f) // 4, name="grad_wgu1")
    g_wd1, (sib_gu1,) = _matmul_tn(a1, df1, tmm=f // 2, tnn=d, name="grad_wd1", comm=_SwapJob([g_wgu1], [True]))
    s32_gu1, s16_gu1 = pair_sum(g_wgu1, sib_gu1, True, "sum_pair_gu1")
    (grad_x, dsc1, dsh1), (recv_gu1, sib_d1) = _bwd_in(
        dgu1, wgu1, dr1, x2, None, None, mod, None, seq=seq, w_is_nt=True, tk=(2 * f) // 4, sc_idx=1, gate_idx=None,
        branch_scale=None, final=True, name="ffn1_bwd_in",
        comm=_MultiJob([_ExchangeJob([s16_gu1], [True], [n_gu]), _SwapJob([g_wd1], [False])]))
    s32_d1, s16_d1 = pair_sum(g_wd1, sib_d1, False, "sum_pair_d1")

    dmod = jnp.concatenate([dsh1, dsc1, dgate1, dsh2, dsc2, dgate2, dsh3, dsc3, dgate3], axis=1)
    loss_row = jnp.sum(loss_cols, axis=1, keepdims=True) * (0.5 / d)
    lane_row = lambda a: jnp.pad(a, ((0, 0), (0, d - a.shape[1])))
    block = jnp.concatenate(
        [dmod.reshape(nb * N_MOD, d), dln1g, dln1b, dln2g, dln2b, dln3g, dln3b,
         lane_row(dcw[0:3, :]), lane_row(dsink[:, 0:1].reshape(1, N_Q_HEADS)), lane_row(loss_row)], axis=0)
    block = jnp.pad(block, ((0, SMALL_ROWS - block.shape[0]), (0, 0)))
    gathered = _allgather8(block, name="gather_small").reshape(N_DEV, SMALL_ROWS, d)
    dmod_all = gathered[:, :nb * N_MOD, :].reshape(N_DEV * nb, N_MOD * d)
    dmod_shard = lax.dynamic_slice(dmod_all, (0, chip * n_ada), (N_DEV * nb, n_ada))
    small, g_w_ada, g_b_ada = _small_finish(gathered, dmod_all, dmod_shard, c_all.T, name="small_finish")
    r0 = nb * N_MOD
    loss = small[r0 + 10, 0]
    g_ln = [small[r0 + i:r0 + i + 1, :] for i in range(6)]
    g_cw_full = small[r0 + 6:r0 + 9, :CONV_WIDTH]
    g_conv = lax.dynamic_slice(g_cw_full, (0, chip * conv_w.shape[2]), (3, conv_w.shape[2]))
    g_sinks = small[r0 + 9:r0 + 10, :N_Q_HEADS]

    def flat2(a):
        return a.reshape(-1, a.shape[-1])

    def unhalve(a):
        return a.reshape(2 * a.shape[1], a.shape[2])

    def final_half(s32_, recv_, col_kind, n_shard, name_):
        return _sum_final(pos, s32_, recv_, col_kind=col_kind, n_shard=n_shard, name=name_)

    results = {}

    def adamw(name_, w_, g_, m_, v_, comm=None):
        g2 = flat2(g_)
        out = _adamw(flat2(w_), g2, flat2(m_), flat2(v_), name="adamw_" + name_, comm=comm)
        (dl, nm, nv), extra = out if comm is not None else (out, None)
        results[name_] = tuple(a.reshape(w_.shape) for a in (g2, dl, nm, nv))
        return extra

    early = [final_half(s32_gu2, recv_gu2, True, n_gu, "sum_final_gu2"),
             final_half(s32_d2, recv_d2, False, n_d, "sum_final_d2"),
             final_half(s32_out, recv_out, False, n_out, "sum_final_out"),
             final_half(s32_in, recv_in, False, n_in, "sum_final_in")]
    recv_d1, full_gu2, full_d2, full_out, full_in = adamw(
        "w_ada", w_ada, g_w_ada, m_w_ada, v_w_ada,
        comm=_MultiJob([_ExchangeJob([s16_d1], [False], [n_d]), _ShareJob(early)]))
    late = [final_half(s32_gu1, recv_gu1, True, n_gu, "sum_final_gu1"),
            final_half(s32_d1, recv_d1, False, n_d, "sum_final_d1")]
    full_gu1, full_d1 = adamw("ffn2_w_gate_up", ffn2_w_gate_up, unhalve(full_gu2), m_ffn2_w_gate_up, v_ffn2_w_gate_up,
                              comm=_ShareJob(late))
    adamw("ffn2_w_down", ffn2_w_down, unhalve(full_d2), m_ffn2_w_down, v_ffn2_w_down)
    adamw("w_out", w_out, unhalve(full_out), m_w_out, v_w_out)
    adamw("w_in", w_in, unhalve(full_in).T, m_w_in, v_w_in)
    adamw("ffn1_w_gate_up", ffn1_w_gate_up, unhalve(full_gu1), m_ffn1_w_gate_up, v_ffn1_w_gate_up)
    adamw("ffn1_w_down", ffn1_w_down, unhalve(full_d1), m_ffn1_w_down, v_ffn1_w_down)
    adamw("b_ada", b_ada, g_b_ada, m_b_ada, v_b_ada)
    adamw("ln1_g", ln1_g, g_ln[0], m_ln1_g, v_ln1_g)
    adamw("ln1_b", ln1_b, g_ln[1], m_ln1_b, v_ln1_b)
    adamw("ln2_g", ln2_g, g_ln[2], m_ln2_g, v_ln2_g)
    adamw("ln2_b", ln2_b, g_ln[3], m_ln2_b, v_ln2_b)
    adamw("ln3_g", ln3_g, g_ln[4], m_ln3_g, v_ln3_g)
    adamw("ln3_b", ln3_b, g_ln[5], m_ln3_b, v_ln3_b)
    adamw("conv_w", conv_w, g_conv, m_conv_w, v_conv_w)
    adamw("attn_sinks", attn_sinks, g_sinks, m_attn_sinks, v_attn_sinks)
    order = ["w_ada", "b_ada", "ffn1_w_gate_up", "ffn1_w_down", "ln1_g", "ln1_b", "w_in", "conv_w", "attn_sinks",
             "w_out", "ln2_g", "ln2_b", "ffn2_w_gate_up", "ffn2_w_down", "ln3_g", "ln3_b"]
    return (loss, grad_x.reshape(x.shape), *[results[n_][0] for n_ in order], *[results[n_][1] for n_ in order],
            *[results[n_][2] for n_ in order], *[results[n_][3] for n_ in order])
```

```python
import jax
import jax.numpy as jnp
from jax import lax
from jax.experimental import pallas as pl
from jax.experimental.pallas import tpu as pltpu

F32 = jnp.float32
BF16 = jnp.bfloat16
MESH = pl.DeviceIdType.MESH

D_MODEL = 1024
HEAD_DIM = 64
ATTN_WIDTH = 512
CONV_WIDTH = 512
N_Q_HEADS = 8
N_KV_HEADS = 2
GQA_GROUP = 4
KV_WIDTH = 128
WINDOW = 128
BLOCK = 128
ROT_DIM = 16
ROPE_THETA = 500000.0
N_MOD = 9
LN_EPS = 1e-5
DN_ALPHA = 2.0 ** 0.25
IN_WIDTH = 2304
N_CHIPS = 4
N_DEV = 8
SMALL_ROWS = 32

ADAM_LR = 0.001
ADAM_B1 = 0.9
ADAM_B2 = 0.999
ADAM_EPS = 1e-08
ADAM_WD = 0.01
ADAM_STEP = 10

LANE = 128
COL_CHUNK = 256
VMEM_LIMIT = 56 * 1024 * 1024


def _params(sem=None, vmem=True):
    return pltpu.CompilerParams(dimension_semantics=sem, vmem_limit_bytes=VMEM_LIMIT if vmem else None)


def _sigmoid(g):
    return 0.5 * jnp.tanh(0.5 * g) + 0.5


def _row_sum(v):
    return jnp.sum(v, axis=0, keepdims=True)


ROW_CHUNK = 16


def _fold8(v):
    return v[0:8, :] + v[8:16, :]


def _chunk_rows(r):
    return pl.ds(pl.multiple_of(r * ROW_CHUNK, ROW_CHUNK), ROW_CHUNK)


def _ln_stats(r):
    mu = jnp.mean(r, axis=-1, keepdims=True)
    rc = r - mu
    var = jnp.mean(rc * rc, axis=-1, keepdims=True)
    rstd = lax.rsqrt(var + LN_EPS)
    return rc * rstd, rstd


def _ln_bwd(dxo, xhat, rstd, g):
    dxhat = dxo * g
    m1 = jnp.mean(dxhat, axis=-1, keepdims=True)
    m2 = jnp.mean(dxhat * xhat, axis=-1, keepdims=True)
    return rstd * (dxhat - m1 - xhat * m2)


def _dot_nt(a, b):
    return lax.dot_general(a, b, (((1,), (1,)), ((), ())), preferred_element_type=F32)


def _dot_tn(a, b):
    return lax.dot_general(a, b, (((0,), (0,)), ((), ())), preferred_element_type=F32)


def _full(shape):
    nd = len(shape)
    return pl.BlockSpec(shape, lambda *_: (0,) * nd)


def _resident(shape):
    nd = len(shape)
    return pl.BlockSpec(shape, lambda *_: (0,) * nd, pipeline_mode=pl.Buffered(1))


ANY_SPEC = pl.BlockSpec(memory_space=pl.ANY)


def _pcall(body, *, name, grid, in_specs, out_specs, out_shape, args, scratch_shapes=(), comm=None, prefetch=None):
    single = not isinstance(out_shape, (list, tuple))
    out_specs = [out_specs] if single else list(out_specs)
    out_shape = [out_shape] if single else list(out_shape)
    in_specs = list(in_specs)
    scratch_shapes = list(scratch_shapes)
    sem = ("arbitrary",) * len(grid)
    n_pre = 0 if prefetch is None else 1
    pre_args = () if prefetch is None else (prefetch,)

    def call(fn, ins_, outs_, shapes_, scratch_, aliases_, operands):
        if prefetch is None:
            return pl.pallas_call(fn, name=name, grid=grid, in_specs=ins_, out_specs=outs_, out_shape=shapes_,
                                  scratch_shapes=scratch_, input_output_aliases=aliases_,
                                  compiler_params=_params(sem))(*operands)
        spec = pltpu.PrefetchScalarGridSpec(num_scalar_prefetch=1, grid=grid, in_specs=ins_, out_specs=outs_,
                                            scratch_shapes=scratch_)
        return pl.pallas_call(fn, name=name, grid_spec=spec, out_shape=shapes_,
                              input_output_aliases={n_pre + i: o for i, o in aliases_.items()},
                              compiler_params=_params(sem))(*pre_args, *operands)

    if comm is None:
        res = call(body, in_specs, out_specs, out_shape, scratch_shapes, {}, args)
        return res[0] if single else res
    n_in, n_out, n_scr = len(in_specs), len(out_specs), len(scratch_shapes)
    nci, nco = len(comm.inputs), len(comm.out_shapes)

    def wrapped(*refs):
        pre, refs = refs[:n_pre], refs[n_pre:]
        ins, refs = refs[:n_in], refs[n_in:]
        cin, refs = refs[:nci], refs[nci:]
        outs, refs = refs[:n_out], refs[n_out:]
        cout, refs = refs[:nco], refs[nco:]
        scr, csems = refs[:n_scr], refs[n_scr:]
        first = pl.program_id(0) == 0
        last = pl.program_id(0) == grid[0] - 1
        for ax in range(1, len(grid)):
            first = jnp.logical_and(first, pl.program_id(ax) == 0)
            last = jnp.logical_and(last, pl.program_id(ax) == grid[ax] - 1)

        @pl.when(first)
        def _():
            comm.start(cin, cout, csems)

        body(*pre, *ins, *outs, *scr)

        @pl.when(last)
        def _():
            comm.finish(cin, cout, csems)

    res = call(wrapped, in_specs + [ANY_SPEC] * nci, out_specs + [ANY_SPEC] * nco,
               out_shape + list(comm.out_shapes), scratch_shapes + list(comm.sems),
               {n_in + i: n_out + o for i, o in comm.aliases.items()}, (*args, *comm.inputs))
    main = res[:n_out]
    return (main[0] if single else main), list(res[n_out:])


def _comm_call(job, *, name):
    nci, nco = len(job.inputs), len(job.out_shapes)

    def body(*refs):
        cin, refs = refs[:nci], refs[nci:]
        cout, csems = refs[:nco], refs[nco:]
        job.start(cin, cout, csems)
        job.finish(cin, cout, csems)

    return pl.pallas_call(
        body, name=name, out_shape=list(job.out_shapes), in_specs=[ANY_SPEC] * nci, out_specs=[ANY_SPEC] * nco,
        scratch_shapes=list(job.sems), input_output_aliases=dict(job.aliases))(*job.inputs)


def _ffn_up(xin, lnp, mod, w, *, seq, sc_idx, sh_idx, use_ln, name, comm=None):
    t, d = xin.shape
    f = w.shape[1] // 2
    tm = min(512, seq)
    tpb = seq // tm
    ch = min(COL_CHUNK, f)

    def body(x_ref, ln_ref, mod_ref, w_ref, h_ref, a_ref, gu_ref):
        x = x_ref[...]
        if use_ln:
            x = x * ln_ref[0:1, :] + ln_ref[1:2, :]
        h = x * (1.0 + mod_ref[0, sc_idx:sc_idx + 1, :]) + mod_ref[0, sh_idx:sh_idx + 1, :]
        hb = h.astype(BF16)
        h_ref[...] = hb
        for j in range(f // ch):
            g = jnp.dot(hb, w_ref[:, j * ch:(j + 1) * ch], preferred_element_type=F32)
            u = jnp.dot(hb, w_ref[:, f + j * ch:f + (j + 1) * ch], preferred_element_type=F32)
            a_ref[:, j * ch:(j + 1) * ch] = (g * _sigmoid(g) * u).astype(BF16)
            gu_ref[:, j * ch:(j + 1) * ch] = g.astype(BF16)
            gu_ref[:, f + j * ch:f + (j + 1) * ch] = u.astype(BF16)

    return _pcall(
        body, name=name, grid=(t // tm,),
        in_specs=[pl.BlockSpec((tm, d), lambda i: (i, 0)), _full((2, d)),
                  pl.BlockSpec((1, N_MOD, d), lambda i: (i // tpb, 0, 0)), _resident((d, 2 * f))],
        out_specs=[pl.BlockSpec((tm, d), lambda i: (i, 0)), pl.BlockSpec((tm, f), lambda i: (i, 0)),
                   pl.BlockSpec((tm, 2 * f), lambda i: (i, 0))],
        out_shape=[jax.ShapeDtypeStruct((t, d), BF16), jax.ShapeDtypeStruct((t, f), BF16),
                   jax.ShapeDtypeStruct((t, 2 * f), BF16)],
        args=(xin, lnp, mod, w), comm=comm)


def _ffn_down_ln(a, wd, xin, lnp_in, mod, *, seq, gate_idx, use_ln, name, comm=None):
    t, f = a.shape
    d = wd.shape[1]
    tm = min(512, seq)
    tpb = seq // tm

    def body(a_ref, wd_ref, x_ref, ln_ref, mod_ref, f_ref, xhat_ref, rstd_ref, acc):
        av = a_ref[...]
        for j in range(d // COL_CHUNK):
            acc[:, j * COL_CHUNK:(j + 1) * COL_CHUNK] = jnp.dot(
                av, wd_ref[:, j * COL_CHUNK:(j + 1) * COL_CHUNK], preferred_element_type=F32)
        scale = 0.5 * (1.0 + mod_ref[0, gate_idx:gate_idx + 1, :])

        def chunk(r, carry):
            rows = _chunk_rows(r)
            fo = acc[rows, :]
            x = x_ref[rows, :]
            if use_ln:
                x = x * ln_ref[0:1, :] + ln_ref[1:2, :]
            xhat, rstd = _ln_stats(DN_ALPHA * x + scale * fo)
            f_ref[rows, :] = fo.astype(BF16)
            xhat_ref[rows, :] = xhat
            rstd_ref[rows, :] = rstd
            return carry

        lax.fori_loop(0, tm // ROW_CHUNK, chunk, 0)

    return _pcall(
        body, name=name, grid=(t // tm,),
        in_specs=[pl.BlockSpec((tm, f), lambda i: (i, 0)), _resident((f, d)),
                  pl.BlockSpec((tm, d), lambda i: (i, 0)), _full((2, d)),
                  pl.BlockSpec((1, N_MOD, d), lambda i: (i // tpb, 0, 0))],
        out_specs=[pl.BlockSpec((tm, d), lambda i: (i, 0)), pl.BlockSpec((tm, d), lambda i: (i, 0)),
                   pl.BlockSpec((tm, 1), lambda i: (i, 0))],
        out_shape=[jax.ShapeDtypeStruct((t, d), BF16), jax.ShapeDtypeStruct((t, d), F32),
                   jax.ShapeDtypeStruct((t, 1), F32)],
        scratch_shapes=[pltpu.VMEM((tm, d), F32)],
        args=(a, wd, xin, lnp_in, mod), comm=comm)


def _ffn_down_loss(a, wd, xhat_in, lnp_in, mod, lnp_out, tgt, *, seq, gate_idx, name):
    t, f = a.shape
    d = wd.shape[1]
    nb = t // seq
    tm = min(512, seq)
    tpb = seq // tm

    def body(a_ref, wd_ref, x_ref, lnin_ref, mod_ref, lnout_ref, tgt_ref,
             dr_ref, df_ref, loss_ref, dg_ref, db_ref, dgate_ref, acc):
        i = pl.program_id(0)
        av = a_ref[...]
        for j in range(d // COL_CHUNK):
            acc[:, j * COL_CHUNK:(j + 1) * COL_CHUNK] = jnp.dot(
                av, wd_ref[:, j * COL_CHUNK:(j + 1) * COL_CHUNK], preferred_element_type=F32)
        scale = 0.5 * (1.0 + mod_ref[0, gate_idx:gate_idx + 1, :])
        g_in, b_in = lnin_ref[0:1, :], lnin_ref[1:2, :]
        g_out, b_out = lnout_ref[0:1, :], lnout_ref[1:2, :]

        def chunk(r, carry):
            s_loss, s_dg, s_db, s_gate = carry
            rows = _chunk_rows(r)
            fo = acc[rows, :]
            xhat, rstd = _ln_stats(DN_ALPHA * (x_ref[rows, :] * g_in + b_in) + scale * fo)
            e = xhat * g_out + b_out - tgt_ref[rows, :]
            dy = e * (1.0 / d)
            dr = _ln_bwd(dy, xhat, rstd, g_out)
            dr_ref[rows, :] = dr
            df_ref[rows, :] = (scale * dr).astype(BF16)
            return (s_loss + _fold8(e * e), s_dg + _fold8(dy * xhat), s_db + _fold8(dy),
                    s_gate + _fold8(0.5 * fo * dr))

        zero = jnp.zeros((8, d), F32)
        s_loss, s_dg, s_db, s_gate = lax.fori_loop(0, tm // ROW_CHUNK, chunk, (zero, zero, zero, zero))

        @pl.when(i == 0)
        def _():
            loss_ref[...] = jnp.zeros_like(loss_ref)
            dg_ref[...] = jnp.zeros_like(dg_ref)
            db_ref[...] = jnp.zeros_like(db_ref)

        @pl.when(i % tpb == 0)
        def _():
            dgate_ref[...] = jnp.zeros_like(dgate_ref)

        loss_ref[...] += _row_sum(s_loss)
        dg_ref[...] += _row_sum(s_dg)
        db_ref[...] += _row_sum(s_db)
        dgate_ref[0] += _row_sum(s_gate)

    return pl.pallas_call(
        body, name=name, grid=(t // tm,), scratch_shapes=[pltpu.VMEM((tm, d), F32)],
        in_specs=[pl.BlockSpec((tm, f), lambda i: (i, 0)), _resident((f, d)),
                  pl.BlockSpec((tm, d), lambda i: (i, 0)), _full((2, d)),
                  pl.BlockSpec((1, N_MOD, d), lambda i: (i // tpb, 0, 0)), _full((2, d)),
                  pl.BlockSpec((tm, d), lambda i: (i, 0))],
        out_specs=[pl.BlockSpec((tm, d), lambda i: (i, 0)), pl.BlockSpec((tm, d), lambda i: (i, 0)),
                   _full((1, d)), _full((1, d)), _full((1, d)),
                   pl.BlockSpec((1, 1, d), lambda i: (i // tpb, 0, 0))],
        out_shape=[jax.ShapeDtypeStruct((t, d), F32), jax.ShapeDtypeStruct((t, d), BF16),
                   jax.ShapeDtypeStruct((1, d), F32), jax.ShapeDtypeStruct((1, d), F32),
                   jax.ShapeDtypeStruct((1, d), F32), jax.ShapeDtypeStruct((nb, 1, d), F32)],
        compiler_params=_params(("arbitrary",)),
    )(a, wd, xhat_in, lnp_in, mod, lnp_out, tgt)


def _rope(v, cos, sa, sb):
    return v * cos + pltpu.roll(v, LANE - ROT_DIM // 2, 1) * sa + pltpu.roll(v, ROT_DIM // 2, 1) * sb


def _rope_t(dy, cos, sa, sb):
    return dy * cos + pltpu.roll(dy * sa, ROT_DIM // 2, 1) + pltpu.roll(dy * sb, LANE - ROT_DIM // 2, 1)


def _in_proj(xhat, lnp, mod, w_t, cos, sa, sb, *, seq, sc_idx, sh_idx, name, comm=None):
    t, d = xhat.shape
    tm = min(512, seq)
    tpb = seq // tm
    n_conv = 3 * CONV_WIDTH

    def body(x_ref, ln_ref, mod_ref, w_ref, cos_ref, sa_ref, sb_ref, h_ref, q_ref, k_ref, v_ref, ubc_ref):
        x = x_ref[...] * ln_ref[0:1, :] + ln_ref[1:2, :]
        h = x * (1.0 + mod_ref[0, sc_idx:sc_idx + 1, :]) + mod_ref[0, sh_idx:sh_idx + 1, :]
        hb = h.astype(BF16)
        h_ref[...] = hb
        cos_t, sa_t, sb_t = cos_ref[...], sa_ref[...], sb_ref[...]
        for j in range(ATTN_WIDTH // COL_CHUNK):
            p = _dot_nt(hb, w_ref[j * COL_CHUNK:(j + 1) * COL_CHUNK, :])
            for s in range(COL_CHUNK // LANE):
                q_ref[:, j * COL_CHUNK + s * LANE:j * COL_CHUNK + (s + 1) * LANE] = _rope(
                    p[:, s * LANE:(s + 1) * LANE], cos_t, sa_t, sb_t).astype(BF16)
        p = _dot_nt(hb, w_ref[ATTN_WIDTH:ATTN_WIDTH + 2 * KV_WIDTH, :])
        k_ref[...] = _rope(p[:, 0:KV_WIDTH], cos_t, sa_t, sb_t).astype(BF16)
        v_ref[...] = p[:, KV_WIDTH:].astype(BF16)
        base = ATTN_WIDTH + 2 * KV_WIDTH
        for j in range(n_conv // COL_CHUNK):
            ubc_ref[:, j * COL_CHUNK:(j + 1) * COL_CHUNK] = _dot_nt(
                hb, w_ref[base + j * COL_CHUNK:base + (j + 1) * COL_CHUNK, :])

    row = lambda w: pl.BlockSpec((tm, w), lambda i: (i, 0))
    return _pcall(
        body, name=name, grid=(t // tm,),
        in_specs=[row(d), _full((2, d)), pl.BlockSpec((1, N_MOD, d), lambda i: (i // tpb, 0, 0)),
                  _resident((IN_WIDTH, d)), row(LANE), row(LANE), row(LANE)],
        out_specs=[row(d), row(ATTN_WIDTH), row(KV_WIDTH), row(KV_WIDTH), row(n_conv)],
        out_shape=[jax.ShapeDtypeStruct((t, d), BF16), jax.ShapeDtypeStruct((t, ATTN_WIDTH), BF16),
                   jax.ShapeDtypeStruct((t, KV_WIDTH), BF16), jax.ShapeDtypeStruct((t, KV_WIDTH), BF16),
                   jax.ShapeDtypeStruct((t, n_conv), F32)],
        args=(xhat, lnp, mod, w_t, cos, sa, sb), comm=comm)


def _attn_group(q_ref, kp_ref, kc_ref, vp_ref, vc_ref, sink_ref, g, first):
    lo, hi = g * HEAD_DIM, (g + 1) * HEAD_DIM
    kk = jnp.concatenate([kp_ref[:, lo:hi], kc_ref[:, lo:hi]], axis=0)
    vv = jnp.concatenate([vp_ref[:, lo:hi], vc_ref[:, lo:hi]], axis=0)
    qs = jnp.concatenate([q_ref[:, (GQA_GROUP * g + j) * HEAD_DIM:(GQA_GROUP * g + j + 1) * HEAD_DIM]
                          for j in range(GQA_GROUP)], axis=0)
    rows = GQA_GROUP * BLOCK
    row = lax.broadcasted_iota(jnp.int32, (rows, 2 * BLOCK), 0)
    ki = lax.broadcasted_iota(jnp.int32, (rows, 2 * BLOCK), 1)
    diff = (row & (BLOCK - 1)) + BLOCK - ki
    valid = (diff >= 0) & (diff < WINDOW) & ((ki >= BLOCK) | jnp.logical_not(first))
    s = _dot_nt(qs, kk) * (HEAD_DIM ** -0.5)
    s = jnp.where(valid, s, -1e30)
    rcol = lax.broadcasted_iota(jnp.int32, (rows, 1), 0)
    sink = jnp.zeros((rows, 1), F32)
    for j in range(GQA_GROUP):
        sink = jnp.where(rcol // BLOCK == j, sink_ref[GQA_GROUP * g + j], sink)
    m = jnp.maximum(jnp.max(s, axis=1, keepdims=True), sink)
    p = jnp.exp(s - m)
    ps = jnp.exp(sink - m)
    inv = 1.0 / (jnp.sum(p, axis=1, keepdims=True) + ps)
    return qs, kk, vv, p * inv, ps * inv


def _attention(q, k, v, sinks, *, seq, name, comm=None):
    t = q.shape[0]
    nblk = seq // BLOCK

    def body(q_ref, kp_ref, kc_ref, vp_ref, vc_ref, sink_ref, o_ref):
        first = (pl.program_id(0) % nblk) == 0
        outs = []
        for g in range(N_KV_HEADS):
            _, _, vv, pn, _ = _attn_group(q_ref, kp_ref, kc_ref, vp_ref, vc_ref, sink_ref, g, first)
            o = jnp.dot(pn.astype(BF16), vv, preferred_element_type=F32)
            outs += [o[j * BLOCK:(j + 1) * BLOCK, :] for j in range(GQA_GROUP)]
        o_ref[...] = jnp.concatenate(outs, axis=1).astype(BF16)

    cur = lambda w: pl.BlockSpec((BLOCK, w), lambda n: (n, 0))
    prev = lambda w: pl.BlockSpec((BLOCK, w), lambda n: (jnp.maximum(n - 1, 0), 0))
    return _pcall(
        body, name=name, grid=(t // BLOCK,),
        in_specs=[cur(ATTN_WIDTH), prev(KV_WIDTH), cur(KV_WIDTH), prev(KV_WIDTH), cur(KV_WIDTH),
                  pl.BlockSpec(memory_space=pltpu.SMEM)],
        out_specs=cur(ATTN_WIDTH),
        out_shape=jax.ShapeDtypeStruct((t, ATTN_WIDTH), BF16),
        args=(q, k, k, v, v, sinks), comm=comm)


def _out_proj(attn, ubc, cw, wout, xhat_in, lnp_in, mod, *, seq, gate_idx, name, comm=None):
    t, d = xhat_in.shape
    tm = min(512, seq)
    tpb = seq // tm
    cwid = CONV_WIDTH

    def body(attn_ref, ubc_ref, halo_ref, cw_ref, w_ref, x_ref, ln_ref, mod_ref,
             mixin_ref, mix_ref, xhat_ref, rstd_ref, zbuf, acc):
        first = (pl.program_id(0) % tpb) == 0
        u, bg, cg = ubc_ref[:, 0:cwid], ubc_ref[:, cwid:2 * cwid], ubc_ref[:, 2 * cwid:3 * cwid]
        z = cg * u
        hz = halo_ref[:, 2 * cwid:3 * cwid] * halo_ref[:, 0:cwid]
        zbuf[0:8, :] = jnp.where(first, 0.0, hz)
        zbuf[8:8 + tm, :] = z
        y = cw_ref[0:1, :] * zbuf[6:6 + tm, :] + cw_ref[1:2, :] * zbuf[7:7 + tm, :] + cw_ref[2:3, :] * z
        mixin_ref[:, 0:ATTN_WIDTH] = attn_ref[...]
        mixin_ref[:, ATTN_WIDTH:] = (bg * y).astype(BF16)
        mv = mixin_ref[...]
        for j in range(d // COL_CHUNK):
            acc[:, j * COL_CHUNK:(j + 1) * COL_CHUNK] = jnp.dot(
                mv, w_ref[:, j * COL_CHUNK:(j + 1) * COL_CHUNK], preferred_element_type=F32)
        scale = 1.0 + mod_ref[0, gate_idx:gate_idx + 1, :]

        def chunk(r, carry):
            rows = _chunk_rows(r)
            mix = acc[rows, :]
            xhat, rstd = _ln_stats(DN_ALPHA * (x_ref[rows, :] * ln_ref[0:1, :] + ln_ref[1:2, :]) + scale * mix)
            mix_ref[rows, :] = mix.astype(BF16)
            xhat_ref[rows, :] = xhat
            rstd_ref[rows, :] = rstd
            return carry

        lax.fori_loop(0, tm // ROW_CHUNK, chunk, 0)

    row = lambda w: pl.BlockSpec((tm, w), lambda i: (i, 0))
    return _pcall(
        body, name=name, grid=(t // tm,),
        in_specs=[row(ATTN_WIDTH), row(3 * cwid),
                  pl.BlockSpec((8, 3 * cwid), lambda i: (jnp.maximum(i * (tm // 8) - 1, 0), 0)),
                  _full((8, cwid)), _resident((d, d)), row(d), _full((2, d)),
                  pl.BlockSpec((1, N_MOD, d), lambda i: (i // tpb, 0, 0))],
        out_specs=[row(d), row(d), row(d), row(1)],
        out_shape=[jax.ShapeDtypeStruct((t, d), BF16), jax.ShapeDtypeStruct((t, d), BF16),
                   jax.ShapeDtypeStruct((t, d), F32), jax.ShapeDtypeStruct((t, 1), F32)],
        scratch_shapes=[pltpu.VMEM((tm + 8, cwid), F32), pltpu.VMEM((tm, d), F32)],
        args=(attn, ubc, ubc, cw, wout, xhat_in, lnp_in, mod), comm=comm)


def _ffn_bwd_act(df, wd, gu, *, seq, name, comm=None):
    t, d = df.shape
    f = wd.shape[0]
    tm = min(512, seq)
    ch = min(COL_CHUNK, f)

    def body(df_ref, wd_ref, gu_ref, dgu_ref):
        dfv = df_ref[...]
        for j in range(f // ch):
            da = _dot_nt(dfv, wd_ref[j * ch:(j + 1) * ch, :])
            g = gu_ref[:, j * ch:(j + 1) * ch].astype(F32)
            u = gu_ref[:, f + j * ch:f + (j + 1) * ch].astype(F32)
            s = _sigmoid(g)
            dgu_ref[:, j * ch:(j + 1) * ch] = (da * u * (s * (1.0 + g * (1.0 - s)))).astype(BF16)
            dgu_ref[:, f + j * ch:f + (j + 1) * ch] = (da * (g * s)).astype(BF16)

    return _pcall(
        body, name=name, grid=(t // tm,),
        in_specs=[pl.BlockSpec((tm, d), lambda i: (i, 0)), _resident((f, d)),
                  pl.BlockSpec((tm, 2 * f), lambda i: (i, 0))],
        out_specs=pl.BlockSpec((tm, 2 * f), lambda i: (i, 0)),
        out_shape=jax.ShapeDtypeStruct((t, 2 * f), BF16),
        args=(df, wd, gu), comm=comm)


def _bwd_in(a, w, dr, xin, rstd_prev, lnp_prev, mod, branch_prev, *, seq, w_is_nt, sc_idx, gate_idx,
            branch_scale, final, name, comm=None):
    t, kdim = a.shape
    d = dr.shape[1]
    nb = t // seq
    tm = min(512, seq)
    tpb = seq // tm

    def body(*refs):
        if final:
            a_ref, w_ref, dr_ref, x_ref, mod_ref, dx_ref, dsc_ref, dsh_ref, acc = refs
        else:
            (a_ref, w_ref, dr_ref, x_ref, rstd_ref, ln_ref, mod_ref, br_ref,
             drp_ref, dbr_ref, dsc_ref, dsh_ref, dgate_ref, dg_ref, db_ref, acc) = refs
        i = pl.program_id(0)
        av = a_ref[...]
        for j in range(d // COL_CHUNK):
            cols = slice(j * COL_CHUNK, (j + 1) * COL_CHUNK)
            acc[:, cols] = (_dot_nt(av, w_ref[cols, :]) if w_is_nt
                            else jnp.dot(av, w_ref[:, cols], preferred_element_type=F32))
        sc1 = 1.0 + mod_ref[0, sc_idx:sc_idx + 1, :]
        if not final:
            g_prev, b_prev = ln_ref[0:1, :], ln_ref[1:2, :]
            bscale = branch_scale * (1.0 + mod_ref[0, gate_idx:gate_idx + 1, :])

        def chunk(r, carry):
            rows = _chunk_rows(r)
            dh = acc[rows, :]
            dx = DN_ALPHA * dr_ref[rows, :] + dh * sc1
            if final:
                dx_ref[rows, :] = dx
                return carry[0] + _fold8(dh * x_ref[rows, :]), carry[1] + _fold8(dh)
            xhat = x_ref[rows, :]
            drp = _ln_bwd(dx, xhat, rstd_ref[rows, :], g_prev)
            drp_ref[rows, :] = drp
            dbr_ref[rows, :] = (bscale * drp).astype(BF16)
            return (carry[0] + _fold8(dh * (xhat * g_prev + b_prev)), carry[1] + _fold8(dh),
                    carry[2] + _fold8(branch_scale * br_ref[rows, :].astype(F32) * drp),
                    carry[3] + _fold8(dx * xhat), carry[4] + _fold8(dx))

        zero = jnp.zeros((8, d), F32)
        sums = lax.fori_loop(0, tm // ROW_CHUNK, chunk, (zero,) * (2 if final else 5))

        @pl.when((i % tpb) == 0)
        def _():
            dsc_ref[...] = jnp.zeros_like(dsc_ref)
            dsh_ref[...] = jnp.zeros_like(dsh_ref)
            if not final:
                dgate_ref[...] = jnp.zeros_like(dgate_ref)

        dsc_ref[0] += _row_sum(sums[0])
        dsh_ref[0] += _row_sum(sums[1])
        if not final:
            @pl.when(i == 0)
            def _():
                dg_ref[...] = jnp.zeros_like(dg_ref)
                db_ref[...] = jnp.zeros_like(db_ref)

            dgate_ref[0] += _row_sum(sums[2])
            dg_ref[...] += _row_sum(sums[3])
            db_ref[...] += _row_sum(sums[4])

    row = lambda w_: pl.BlockSpec((tm, w_), lambda i: (i, 0))
    vec = pl.BlockSpec((1, 1, d), lambda i: (i // tpb, 0, 0))
    mod_spec = pl.BlockSpec((1, N_MOD, d), lambda i: (i // tpb, 0, 0))
    vshape = jax.ShapeDtypeStruct((nb, 1, d), F32)
    if final:
        in_specs = [row(kdim), _resident(w.shape), row(d), row(d), mod_spec]
        args = (a, w, dr, xin, mod)
        out_specs = [row(d), vec, vec]
        out_shape = [jax.ShapeDtypeStruct((t, d), F32), vshape, vshape]
    else:
        in_specs = [row(kdim), _resident(w.shape), row(d), row(d), row(1), _full((2, d)), mod_spec, row(d)]
        args = (a, w, dr, xin, rstd_prev, lnp_prev, mod, branch_prev)
        out_specs = [row(d), row(d), vec, vec, vec, _full((1, d)), _full((1, d))]
        out_shape = [jax.ShapeDtypeStruct((t, d), F32), jax.ShapeDtypeStruct((t, d), BF16), vshape, vshape, vshape,
                     jax.ShapeDtypeStruct((1, d), F32), jax.ShapeDtypeStruct((1, d), F32)]
    return _pcall(
        body, name=name, grid=(t // tm,), in_specs=in_specs, out_specs=out_specs, out_shape=out_shape,
        scratch_shapes=[pltpu.VMEM((tm, d), F32)], args=args, comm=comm)


def _matmul_tn(a, b, *, tmm, tnn, name, comm=None):
    t, m = a.shape
    n = b.shape[1]
    tk = min(512, t)

    def body(a_ref, b_ref, o_ref):
        @pl.when(pl.program_id(2) == 0)
        def _():
            o_ref[...] = jnp.zeros_like(o_ref)
        o_ref[...] += _dot_tn(a_ref[...], b_ref[...])

    return _pcall(
        body, name=name, grid=(m // tmm, n // tnn, t // tk),
        in_specs=[pl.BlockSpec((tk, tmm), lambda i, j, k: (k, i)), pl.BlockSpec((tk, tnn), lambda i, j, k: (k, j))],
        out_specs=pl.BlockSpec((tmm, tnn), lambda i, j, k: (i, j)),
        out_shape=jax.ShapeDtypeStruct((m, n), F32),
        args=(a, b), comm=comm)


def _matmul_nt_bf16(a, w, *, seq, name):
    t, kdim = a.shape
    n = w.shape[0]
    tm = min(512, seq)

    def body(a_ref, w_ref, o_ref):
        av = a_ref[...]
        for j in range(n // COL_CHUNK):
            o_ref[:, j * COL_CHUNK:(j + 1) * COL_CHUNK] = _dot_nt(
                av, w_ref[j * COL_CHUNK:(j + 1) * COL_CHUNK, :]).astype(BF16)

    return pl.pallas_call(
        body, name=name, grid=(t // tm,),
        in_specs=[pl.BlockSpec((tm, kdim), lambda i: (i, 0)), _resident((n, kdim))],
        out_specs=pl.BlockSpec((tm, n), lambda i: (i, 0)),
        out_shape=jax.ShapeDtypeStruct((t, n), BF16),
        compiler_params=_params(("arbitrary",)),
    )(a, w)


def _attention_bwd(q, k, v, dmixin, sinks, *, seq, name, comm=None):
    t = q.shape[0]
    nblk = seq // BLOCK

    def body(q_ref, kp_ref, kc_ref, vp_ref, vc_ref, do_ref, sink_ref,
             dq_ref, dkp_ref, dkc_ref, dvp_ref, dvc_ref, dsink_ref):
        n = pl.program_id(0)
        first = (n % nblk) == 0

        @pl.when(n == 0)
        def _():
            dsink_ref[...] = jnp.zeros_like(dsink_ref)

        dqs, dks, dvs = [], [], []
        srow = lax.broadcasted_iota(jnp.int32, (8, LANE), 0)
        dsink = jnp.zeros((8, LANE), F32)
        for g in range(N_KV_HEADS):
            qs, kk, vv, pn, psn = _attn_group(q_ref, kp_ref, kc_ref, vp_ref, vc_ref, sink_ref, g, first)
            dos = jnp.concatenate([do_ref[:, (GQA_GROUP * g + j) * HEAD_DIM:(GQA_GROUP * g + j + 1) * HEAD_DIM]
                                   for j in range(GQA_GROUP)], axis=0)
            dp = _dot_nt(dos, vv)
            delta = jnp.sum(pn * dp, axis=1, keepdims=True)
            ds = pn * (dp - delta)
            dsk = psn * delta
            for j in range(GQA_GROUP):
                tot = jnp.sum(dsk[j * BLOCK:(j + 1) * BLOCK, :], axis=0, keepdims=True)
                dsink = dsink - jnp.where(srow == GQA_GROUP * g + j, tot, 0.0)
            dsb = (ds * (HEAD_DIM ** -0.5)).astype(BF16)
            dqg = jnp.dot(dsb, kk, preferred_element_type=F32)
            dqs += [dqg[j * BLOCK:(j + 1) * BLOCK, :] for j in range(GQA_GROUP)]
            dks.append(_dot_tn(dsb, qs))
            dvs.append(_dot_tn(pn.astype(BF16), dos))
        dsink_ref[...] += dsink
        dq_ref[...] = jnp.concatenate(dqs, axis=1)
        dkp_ref[...] = jnp.concatenate([x[0:BLOCK, :] for x in dks], axis=1)
        dkc_ref[...] = jnp.concatenate([x[BLOCK:, :] for x in dks], axis=1)
        dvp_ref[...] = jnp.concatenate([x[0:BLOCK, :] for x in dvs], axis=1)
        dvc_ref[...] = jnp.concatenate([x[BLOCK:, :] for x in dvs], axis=1)

    cur = lambda w: pl.BlockSpec((BLOCK, w), lambda n: (n, 0))
    prev = lambda w: pl.BlockSpec((BLOCK, w), lambda n: (jnp.maximum(n - 1, 0), 0))
    kv = jax.ShapeDtypeStruct((t, KV_WIDTH), F32)
    return _pcall(
        body, name=name, grid=(t // BLOCK,),
        in_specs=[cur(ATTN_WIDTH), prev(KV_WIDTH), cur(KV_WIDTH), prev(KV_WIDTH), cur(KV_WIDTH), cur(ATTN_WIDTH),
                  pl.BlockSpec(memory_space=pltpu.SMEM)],
        out_specs=[cur(ATTN_WIDTH), cur(KV_WIDTH), cur(KV_WIDTH), cur(KV_WIDTH), cur(KV_WIDTH), _full((8, LANE))],
        out_shape=[jax.ShapeDtypeStruct((t, ATTN_WIDTH), F32), kv, kv, kv, kv, jax.ShapeDtypeStruct((8, LANE), F32)],
        args=(q, k, k, v, v, dmixin, sinks), comm=comm)


def _mix_bwd_assemble(dq, dkp, dkc, dvp, dvc, cos, sa, sb, dmixin, ubc, cw, *, seq, name, comm=None):
    t = dq.shape[0]
    nblk = seq // BLOCK
    ntile = t // BLOCK
    cwid = CONV_WIDTH
    tm = BLOCK

    def body(dq_ref, dkc_ref, dkp_ref, dvc_ref, dvp_ref, cos_ref, sa_ref, sb_ref, dco_ref, dcon_ref,
             ubc_ref, hprev_ref, hnext_ref, cw_ref, dproj_ref, dcw_ref, zbuf, dybuf):
        i = pl.program_id(0)
        first = (i % nblk) == 0
        last = (i % nblk) == nblk - 1
        glast = i == ntile - 1

        @pl.when(i == 0)
        def _():
            dcw_ref[...] = jnp.zeros_like(dcw_ref)

        cos_t, sa_t, sb_t = cos_ref[...], sa_ref[...], sb_ref[...]
        for j in range(ATTN_WIDTH // LANE):
            dproj_ref[:, j * LANE:(j + 1) * LANE] = _rope_t(
                dq_ref[:, j * LANE:(j + 1) * LANE], cos_t, sa_t, sb_t).astype(BF16)
        dk = dkc_ref[...] + jnp.where(glast, 0.0, dkp_ref[...])
        dproj_ref[:, ATTN_WIDTH:ATTN_WIDTH + KV_WIDTH] = _rope_t(dk, cos_t, sa_t, sb_t).astype(BF16)
        dv = dvc_ref[...] + jnp.where(glast, 0.0, dvp_ref[...])
        dproj_ref[:, ATTN_WIDTH + KV_WIDTH:ATTN_WIDTH + 2 * KV_WIDTH] = dv.astype(BF16)

        u, bg, cg = ubc_ref[:, 0:cwid], ubc_ref[:, cwid:2 * cwid], ubc_ref[:, 2 * cwid:3 * cwid]
        z = cg * u
        hz = hprev_ref[:, 2 * cwid:3 * cwid] * hprev_ref[:, 0:cwid]
        zbuf[0:8, :] = jnp.where(first, 0.0, hz)
        zbuf[8:8 + tm, :] = z
        z2, z1 = zbuf[6:6 + tm, :], zbuf[7:7 + tm, :]
        w0, w1, w2 = cw_ref[0:1, :], cw_ref[1:2, :], cw_ref[2:3, :]
        y = w0 * z2 + w1 * z1 + w2 * z
        dco = dco_ref[...].astype(F32)
        dyc = dco * bg
        dyn = dcon_ref[0:8, :].astype(F32) * hnext_ref[:, cwid:2 * cwid]
        dybuf[0:tm, :] = dyc
        dybuf[tm:tm + 8, :] = jnp.where(last, 0.0, dyn)
        dz = w2 * dyc + w1 * dybuf[1:1 + tm, :] + w0 * dybuf[2:2 + tm, :]
        srow = lax.broadcasted_iota(jnp.int32, (8, cwid), 0)
        dcw_ref[...] += (jnp.where(srow == 0, _row_sum(dyc * z2), 0.0) + jnp.where(srow == 1, _row_sum(dyc * z1), 0.0)
                         + jnp.where(srow == 2, _row_sum(dyc * z), 0.0))
        base = ATTN_WIDTH + 2 * KV_WIDTH
        dproj_ref[:, base:base + cwid] = (dz * cg).astype(BF16)
        dproj_ref[:, base + cwid:base + 2 * cwid] = (dco * y).astype(BF16)
        dproj_ref[:, base + 2 * cwid:base + 3 * cwid] = (dz * u).astype(BF16)

    cur = lambda w: pl.BlockSpec((tm, w), lambda i: (i, 0))
    nxt = lambda w: pl.BlockSpec((tm, w), lambda i: (jnp.minimum(i + 1, ntile - 1), 0))
    return _pcall(
        body, name=name, grid=(ntile,),
        in_specs=[cur(ATTN_WIDTH), cur(KV_WIDTH), nxt(KV_WIDTH), cur(KV_WIDTH), nxt(KV_WIDTH),
                  cur(LANE), cur(LANE), cur(LANE),
                  pl.BlockSpec((tm, cwid), lambda i: (i, 1)),
                  pl.BlockSpec((16, cwid), lambda i: (jnp.minimum((i + 1) * (tm // 16), t // 16 - 1), 1)),
                  cur(3 * cwid),
                  pl.BlockSpec((8, 3 * cwid), lambda i: (jnp.maximum(i * (tm // 8) - 1, 0), 0)),
                  pl.BlockSpec((8, 3 * cwid), lambda i: (jnp.minimum((i + 1) * (tm // 8), t // 8 - 1), 0)),
                  _full((8, cwid))],
        out_specs=[cur(IN_WIDTH), _full((8, cwid))],
        out_shape=[jax.ShapeDtypeStruct((t, IN_WIDTH), BF16), jax.ShapeDtypeStruct((8, cwid), F32)],
        scratch_shapes=[pltpu.VMEM((tm + 8, cwid), F32), pltpu.VMEM((tm + 8, cwid), F32)],
        args=(dq, dkc, dkp, dvc, dvp, cos, sa, sb, dmixin, dmixin, ubc, ubc, ubc, cw), comm=comm)


def _ada_fwd(c_all, w_ada, b_ada_shard, *, name):
    nb, d = c_all.shape
    n = w_ada.shape[1]
    tn = n // 2

    def body(c_ref, w_ref, b_ref, o_ref):
        cv = c_ref[...]
        cond = cv * _sigmoid(cv)
        o_ref[...] = jnp.dot(cond, w_ref[...], preferred_element_type=F32,
                             precision=lax.Precision.HIGHEST) + b_ref[...]

    return pl.pallas_call(
        body, name=name, grid=(n // tn,),
        in_specs=[_full((nb, d)), pl.BlockSpec((d, tn), lambda j: (0, j)), pl.BlockSpec((1, tn), lambda j: (0, j))],
        out_specs=pl.BlockSpec((nb, tn), lambda j: (0, j)),
        out_shape=jax.ShapeDtypeStruct((nb, n), F32),
        compiler_params=_params(("arbitrary",)),
    )(c_all, w_ada, b_ada_shard)


def _small_finish(gathered, dmod_all, dmod_shard, c_all_t, *, name):
    d = D_MODEL
    nb, n = dmod_shard.shape

    def body(g_ref, dm_ref, dms_ref, ct_ref, sum_ref, gw_ref, gb_ref):
        total = g_ref[0]
        for dev in range(1, N_DEV):
            total = total + g_ref[dev]
        sum_ref[...] = total
        gb_ref[...] = _row_sum(dm_ref[...])
        ctv = ct_ref[...]
        cond_t = ctv * _sigmoid(ctv)
        for jb in range(n // COL_CHUNK):
            gw_ref[:, jb * COL_CHUNK:(jb + 1) * COL_CHUNK] = jnp.dot(
                cond_t, dms_ref[:, jb * COL_CHUNK:(jb + 1) * COL_CHUNK], preferred_element_type=F32,
                precision=lax.Precision.HIGHEST)

    return pl.pallas_call(
        body, name=name, grid=(1,),
        in_specs=[_full((N_DEV, SMALL_ROWS, d)), _full((nb, N_MOD * d)), _full((nb, n)), _full((d, nb))],
        out_specs=[_full((SMALL_ROWS, d)), _full((d, n)), _full((1, N_MOD * d))],
        out_shape=[jax.ShapeDtypeStruct((SMALL_ROWS, d), F32), jax.ShapeDtypeStruct((d, n), F32),
                   jax.ShapeDtypeStruct((1, N_MOD * d), F32)],
        compiler_params=_params(("arbitrary",)),
    )(gathered, dmod_all, dmod_shard, c_all_t)


def _row_tile(r, c, budget=1 << 20):
    if r * c * 4 <= budget or r % 16:
        return r
    best = 16
    for tr in range(16, r + 1, 16):
        if r % tr == 0 and tr * c * 4 <= budget:
            best = tr
    return best


def _cast_bf16(w, *, name):
    r, c = w.shape
    tr = _row_tile(r, c)

    def body(w_ref, o_ref):
        o_ref[...] = w_ref[...].astype(BF16)

    return pl.pallas_call(
        body, name=name, grid=(r // tr,),
        in_specs=[pl.BlockSpec((tr, c), lambda i: (i, 0))], out_specs=pl.BlockSpec((tr, c), lambda i: (i, 0)),
        out_shape=jax.ShapeDtypeStruct((r, c), BF16), compiler_params=_params(("arbitrary",)),
    )(w)


def _adamw(w, g, m, v, *, name, comm=None):
    r, c = w.shape
    tr = _row_tile(r, c)
    c1 = 1.0 - ADAM_B1 ** ADAM_STEP
    c2 = 1.0 - ADAM_B2 ** ADAM_STEP

    def body(w_ref, g_ref, m_ref, v_ref, d_ref, nm_ref, nv_ref):
        gv = g_ref[...]
        m2 = ADAM_B1 * m_ref[...] + (1.0 - ADAM_B1) * gv
        v2 = ADAM_B2 * v_ref[...] + (1.0 - ADAM_B2) * (gv * gv)
        d_ref[...] = -ADAM_LR * ((m2 / c1) / (jnp.sqrt(v2 / c2) + ADAM_EPS) + ADAM_WD * w_ref[...])
        nm_ref[...] = m2
        nv_ref[...] = v2

    spec = pl.BlockSpec((tr, c), lambda i: (i, 0))
    sh = jax.ShapeDtypeStruct((r, c), F32)
    return _pcall(body, name=name, grid=(r // tr,), in_specs=[spec] * 4, out_specs=[spec] * 3, out_shape=[sh] * 3,
                  args=(w, g, m, v), comm=comm)


def _sum_pair(pos, g3, r3, blk_of, *, name, comm=None):
    n, rows, cols = r3.shape
    tr = _row_tile(rows, cols)

    def body(pos_ref, g_ref, r_ref, s32_ref, s16_ref):
        s = g_ref[0] + r_ref[0]
        s32_ref[0] = s
        s16_ref[0] = s.astype(BF16)

    own = pl.BlockSpec((1, tr, cols), lambda p, i, pos: (blk_of(p, pos), i, 0))
    plain = pl.BlockSpec((1, tr, cols), lambda p, i, pos: (p, i, 0))
    return _pcall(
        body, name=name, grid=(n, rows // tr), in_specs=[own, plain], out_specs=[plain, plain],
        out_shape=[jax.ShapeDtypeStruct((n, rows, cols), F32), jax.ShapeDtypeStruct((n, rows, cols), BF16)],
        args=(g3, r3), prefetch=pos, comm=comm)


def _sum_final(pos, s32, recv, *, col_kind, n_shard, name):
    if col_kind:
        rows, cols = s32.shape[1], n_shard
        own = lambda tr: pl.BlockSpec((1, tr, cols), lambda i, pos: (0, i, 2 * pos[0] + pos[1]))
    else:
        rows, cols = s32.shape[1], s32.shape[2]
        own = lambda tr: pl.BlockSpec((1, tr, cols), lambda i, pos: (2 * pos[0] + pos[1], i, 0))
    tr = _row_tile(rows, cols)

    def body(pos_ref, s_ref, r_ref, o_ref):
        o_ref[0] = ((s_ref[0] + r_ref[0].astype(F32)) + r_ref[1].astype(F32)) + r_ref[2].astype(F32)

    grid_spec = pltpu.PrefetchScalarGridSpec(
        num_scalar_prefetch=1, grid=(rows // tr,),
        in_specs=[own(tr), pl.BlockSpec((3, tr, cols), lambda i, pos: (0, i, 0))],
        out_specs=pl.BlockSpec((1, tr, cols), lambda i, pos: (pos[2], i, 0)))
    return pl.pallas_call(
        body, name=name, grid_spec=grid_spec, out_shape=jax.ShapeDtypeStruct((2, rows, cols), F32),
        compiler_params=_params(("arbitrary",)),
    )(pos, s32, recv)


def _position():
    return lax.axis_index("x"), lax.axis_index("y"), lax.axis_index("c")


def _allgather8(x_shard, *, name, comm=None):
    m_per, n = x_shard.shape
    nci, nco = (0, 0) if comm is None else (len(comm.inputs), len(comm.out_shapes))

    def body(*refs):
        x_ref, refs = refs[0], refs[1:]
        cin, refs = refs[:nci], refs[nci:]
        out_ref, refs = refs[0], refs[1:]
        cout, refs = refs[:nco], refs[nco:]
        (send_sems, recv_sems, local_sem), csems = refs[:3], refs[3:]
        x, y, c = _position()
        me, sibling = (x, y, c), (x, y, 1 - c)
        chips = [(1 - x, y), (x, 1 - y), (1 - x, 1 - y)]

        def rows(px, py, pc):
            return out_ref.at[pl.ds((4 * px + 2 * py + pc) * m_per, m_per), :]

        def copy(k, block, to, src=None):
            return pltpu.make_async_remote_copy(
                src_ref=rows(*block) if src is None else src, dst_ref=rows(*block),
                send_sem=send_sems.at[k], recv_sem=recv_sems.at[k], device_id=to, device_id_type=MESH)

        mine = pltpu.make_async_copy(x_ref, rows(*me), local_sem)
        mine.start()
        first = [copy(0, me, sibling, src=x_ref)]
        first += [copy(1 + j, me, (*chip, c), src=x_ref) for j, chip in enumerate(chips)]
        for cp in first:
            cp.start()
        if comm is not None:
            comm.start(cin, cout, csems)
        passed = [copy(4 + j, (*chip, c), sibling) for j, chip in enumerate(chips)]
        for j, chip in enumerate(chips):
            copy(1 + j, (*chip, c), me).wait_recv()
            passed[j].start()
        copy(0, sibling, me).wait_recv()
        for j, chip in enumerate(chips):
            copy(4 + j, (*chip, 1 - c), me).wait_recv()
        for cp in first + passed:
            cp.wait_send()
        mine.wait()
        if comm is not None:
            comm.finish(cin, cout, csems)

    vmem = pl.BlockSpec(memory_space=pltpu.VMEM)
    sems = [pltpu.SemaphoreType.DMA((7,)), pltpu.SemaphoreType.DMA((7,)), pltpu.SemaphoreType.DMA]
    out = jax.ShapeDtypeStruct((N_DEV * m_per, n), x_shard.dtype)
    if comm is None:
        return pl.pallas_call(body, name=name, out_shape=out, in_specs=[vmem], out_specs=vmem,
                              scratch_shapes=sems)(x_shard)
    res = pl.pallas_call(
        body, name=name, out_shape=[out] + list(comm.out_shapes), in_specs=[vmem] + [ANY_SPEC] * nci,
        out_specs=[vmem] + [ANY_SPEC] * nco, scratch_shapes=sems + list(comm.sems),
        input_output_aliases={1 + i: 1 + o for i, o in comm.aliases.items()})(x_shard, *comm.inputs)
    return res[0], list(res[1:])


def _peer_chips(x, y):
    return [(1 - x, y), (x, 1 - y), (1 - x, 1 - y)]


class _GatherJob:
    def __init__(self, pieces):
        self.pieces = pieces
        n_p = len(pieces)
        self.inputs = [p[0] for p in pieces] + [p[4] for p in pieces if p[4] is not None]
        self.out_shapes = [
            jax.ShapeDtypeStruct((s.shape[0], s.shape[1] * N_CHIPS) if kd else (s.shape[0] * N_CHIPS, s.shape[1]), s.dtype)
            for s, kd, _, _, _ in pieces]
        for s, _, r0, nr, _ in pieces:
            assert r0 % 16 == 0 and nr % 16 == 0 and r0 + nr <= s.shape[0] // 2, (s.shape, r0, nr)
        self.aliases = {}
        nxt = n_p
        for o, p in enumerate(pieces):
            if p[4] is not None:
                self.aliases[nxt] = o
                nxt += 1
        self.sems = [pltpu.SemaphoreType.DMA((3 * n_p,))] * 4 + [pltpu.SemaphoreType.DMA((n_p,))]

    def _rows(self, p, half):
        shard, _, r0, nr, _ = self.pieces[p]
        return half * (shard.shape[0] // 2) + r0, nr

    def _dst(self, cout, p, chip_idx, half):
        shard, col_kind = self.pieces[p][0], self.pieces[p][1]
        r0, nr = self._rows(p, half)
        if col_kind:
            n = shard.shape[1]
            return cout[p].at[pl.ds(r0, nr), pl.ds(chip_idx * n, n)]
        n = shard.shape[0]
        return cout[p].at[pl.ds(chip_idx * n + r0, nr), :]

    def _copies(self, cin, cout, sems):
        send_sems, recv_sems, fsend_sems, frecv_sems, local_sems = sems
        x, y, c = _position()
        k = 2 * x + y
        sibling = (x, y, 1 - c)
        local, sends, arrivals, fwds, fwd_arrivals = [], [], [], [], []
        for p, (shard, col_kind, _, _, carry) in enumerate(self.pieces):
            if carry is None:
                n = shard.shape[1] if col_kind else shard.shape[0]
                own = cout[p].at[:, pl.ds(k * n, n)] if col_kind else cout[p].at[pl.ds(k * n, n), :]
                local.append(pltpu.make_async_copy(cin[p], own, local_sems.at[p]))
            r0, nr = self._rows(p, c)
            for j, chip in enumerate(_peer_chips(x, y)):
                idx = 3 * p + j

                def remote(src, dst, ssem, rsem, to):
                    return pltpu.make_async_remote_copy(src_ref=src, dst_ref=dst, send_sem=ssem, recv_sem=rsem,
                                                        device_id=to, device_id_type=MESH)

                sends.append(remote(cin[p].at[pl.ds(r0, nr), :], self._dst(cout, p, k, c),
                                    send_sems.at[idx], recv_sems.at[idx], (*chip, c)))
                landed = self._dst(cout, p, 2 * chip[0] + chip[1], c)
                arrivals.append(remote(landed, landed, send_sems.at[idx], recv_sems.at[idx], (*chip, c)))
                fwds.append(remote(landed, landed, fsend_sems.at[idx], frecv_sems.at[idx], sibling))
                other = self._dst(cout, p, 2 * chip[0] + chip[1], 1 - c)
                fwd_arrivals.append(remote(other, other, fsend_sems.at[idx], frecv_sems.at[idx], sibling))
        return local, sends, arrivals, fwds, fwd_arrivals

    def start(self, cin, cout, sems):
        local, sends, _, _, _ = self._copies(cin, cout, sems)
        for cp in local + sends:
            cp.start()

    def finish(self, cin, cout, sems):
        local, sends, arrivals, fwds, fwd_arrivals = self._copies(cin, cout, sems)
        for arrived, fw in zip(arrivals, fwds):
            arrived.wait_recv()
            fw.start()
        for arrived in fwd_arrivals:
            arrived.wait_recv()
        for cp in sends + fwds:
            cp.wait_send()
        for cp in local:
            cp.wait()


class _PairedJob:
    aliases = {}

    def start(self, cin, cout, sems):
        for cp in self._copies(cin, cout, sems):
            cp.start()

    def finish(self, cin, cout, sems):
        copies = self._copies(cin, cout, sems)
        for cp in copies:
            cp.wait_recv()
        for cp in copies:
            cp.wait_send()


class _SwapJob(_PairedJob):
    def __init__(self, grads, kinds):
        self.inputs, self.kinds = list(grads), list(kinds)
        self.out_shapes, self.n_copies = [], []
        for g, kd in zip(grads, kinds):
            if kd:
                self.out_shapes.append(jax.ShapeDtypeStruct((1, g.shape[0] // 2, g.shape[1]), g.dtype))
                self.n_copies.append(1)
            else:
                n = g.shape[0] // N_CHIPS
                self.out_shapes.append(jax.ShapeDtypeStruct((N_CHIPS, n // 2, g.shape[1]), g.dtype))
                self.n_copies.append(N_CHIPS)
        total = sum(self.n_copies)
        self.sems = [pltpu.SemaphoreType.DMA((total,)), pltpu.SemaphoreType.DMA((total,))]

    def _copies(self, cin, cout, sems):
        send_sems, recv_sems = sems
        x, y, c = _position()
        copies = []
        for p, src_ref in enumerate(cin):
            for kk in range(self.n_copies[p]):
                if self.kinds[p]:
                    hr = src_ref.shape[0] // 2
                    src = src_ref.at[pl.ds((1 - c) * hr, hr), :]
                else:
                    n = src_ref.shape[0] // N_CHIPS
                    src = src_ref.at[pl.ds(kk * n + (1 - c) * (n // 2), n // 2), :]
                idx = len(copies)
                copies.append(pltpu.make_async_remote_copy(
                    src_ref=src, dst_ref=cout[p].at[kk], send_sem=send_sems.at[idx], recv_sem=recv_sems.at[idx],
                    device_id=(x, y, 1 - c), device_id_type=MESH))
        return copies


class _ExchangeJob(_PairedJob):
    def __init__(self, s16, kinds, sizes):
        self.inputs, self.kinds, self.sizes = list(s16), list(kinds), list(sizes)
        self.out_shapes = [jax.ShapeDtypeStruct((3, s.shape[1], n if kd else s.shape[2]), s.dtype)
                           for s, kd, n in zip(s16, kinds, sizes)]
        self.sems = [pltpu.SemaphoreType.DMA((3 * len(s16),)), pltpu.SemaphoreType.DMA((3 * len(s16),))]

    def _copies(self, cin, cout, sems):
        send_sems, recv_sems = sems
        x, y, c = _position()
        copies = []
        for p, src_ref in enumerate(cin):
            for j, chip in enumerate(_peer_chips(x, y)):
                kk = 2 * chip[0] + chip[1]
                n = self.sizes[p]
                src = src_ref.at[0, :, pl.ds(kk * n, n)] if self.kinds[p] else src_ref.at[kk]
                copies.append(pltpu.make_async_remote_copy(
                    src_ref=src, dst_ref=cout[p].at[j], send_sem=send_sems.at[3 * p + j],
                    recv_sem=recv_sems.at[3 * p + j], device_id=(*chip, c), device_id_type=MESH))
        return copies


class _ShareJob:
    def __init__(self, halves):
        self.inputs = list(halves)
        self.out_shapes = [jax.ShapeDtypeStruct(h.shape, h.dtype) for h in halves]
        self.aliases = {p: p for p in range(len(halves))}
        self.sems = [pltpu.SemaphoreType.DMA((len(halves),)), pltpu.SemaphoreType.DMA((len(halves),))]

    def _copies(self, cout, sems, half):
        send_sems, recv_sems = sems
        x, y, c = _position()
        h = c if half == "mine" else 1 - c
        return [pltpu.make_async_remote_copy(
            src_ref=o.at[h], dst_ref=o.at[h], send_sem=send_sems.at[p], recv_sem=recv_sems.at[p],
            device_id=(x, y, 1 - c), device_id_type=MESH) for p, o in enumerate(cout)]

    def start(self, cin, cout, sems):
        for cp in self._copies(cout, sems, "mine"):
            cp.start()

    def finish(self, cin, cout, sems):
        for cp in self._copies(cout, sems, "theirs"):
            cp.wait_recv()
        for cp in self._copies(cout, sems, "mine"):
            cp.wait_send()


class _MultiJob:
    def __init__(self, jobs):
        self.jobs = jobs
        self.inputs = [a for j in jobs for a in j.inputs]
        self.out_shapes = [s for j in jobs for s in j.out_shapes]
        self.sems = [s for j in jobs for s in j.sems]
        self.aliases = {}
        i0 = o0 = 0
        for j in jobs:
            for i, o in j.aliases.items():
                self.aliases[i0 + i] = o0 + o
            i0 += len(j.inputs)
            o0 += len(j.out_shapes)

    def _parts(self, cin, cout, sems):
        i0 = o0 = s0 = 0
        for j in self.jobs:
            ni, no, ns = len(j.inputs), len(j.out_shapes), len(j.sems)
            yield j, cin[i0:i0 + ni], cout[o0:o0 + no], sems[s0:s0 + ns]
            i0, o0, s0 = i0 + ni, o0 + no, s0 + ns

    def start(self, cin, cout, sems):
        for j, a, b, s in self._parts(cin, cout, sems):
            j.start(a, b, s)

    def finish(self, cin, cout, sems):
        for j, a, b, s in self._parts(cin, cout, sems):
            j.finish(a, b, s)


def _rope_tables(positions):
    half = ROT_DIM // 2
    inv_freq = jnp.power(jnp.float32(ROPE_THETA), -jnp.arange(0, ROT_DIM, 2, dtype=F32) / ROT_DIM)
    inv_head = jnp.concatenate([inv_freq, inv_freq, jnp.zeros((HEAD_DIM - ROT_DIM,), F32)])
    inv_lane = jnp.concatenate([inv_head] * (LANE // HEAD_DIM))
    ang = positions.astype(F32).reshape(-1)[:, None] * inv_lane[None, :]
    sin = jnp.sin(ang)
    dim = jnp.arange(LANE) % HEAD_DIM
    return jnp.cos(ang), jnp.where(dim < half, -sin, 0.0), jnp.where(dim >= half, sin, 0.0)


def kernel(x, c, positions, w_ada, b_ada, ffn1_w_gate_up, ffn1_w_down, ln1_g, ln1_b, w_in, conv_w, attn_sinks, w_out, ln2_g, ln2_b, ffn2_w_gate_up, ffn2_w_down, ln3_g, ln3_b, loss_target, m_w_ada, m_b_ada, m_ffn1_w_gate_up, m_ffn1_w_down, m_ln1_g, m_ln1_b, m_w_in, m_conv_w, m_attn_sinks, m_w_out, m_ln2_g, m_ln2_b, m_ffn2_w_gate_up, m_ffn2_w_down, m_ln3_g, m_ln3_b, v_w_ada, v_b_ada, v_ffn1_w_gate_up, v_ffn1_w_down, v_ln1_g, v_ln1_b, v_w_in, v_conv_w, v_attn_sinks, v_w_out, v_ln2_g, v_ln2_b, v_ffn2_w_gate_up, v_ffn2_w_down, v_ln3_g, v_ln3_b):
    d = D_MODEL
    nb, seq, _ = x.shape
    t = nb * seq
    f = ffn1_w_down.shape[1] * N_CHIPS
    ax, ay, ac = _position()
    chip = 2 * ax + ay
    dev = 2 * chip + ac
    pos = jnp.stack([ax, ay, ac]).astype(jnp.int32)

    x2 = x.reshape(t, d)
    tgt2 = loss_target.reshape(t, d)
    ln1 = jnp.concatenate([ln1_g, ln1_b], axis=0)
    ln2 = jnp.concatenate([ln2_g, ln2_b], axis=0)
    ln3 = jnp.concatenate([ln3_g, ln3_b], axis=0)
    sinks = attn_sinks.reshape(N_Q_HEADS)
    cos_t, sa_t, sb_t = _rope_tables(positions)

    n_ada = w_ada.shape[2]
    c_all = _allgather8(c.reshape(nb * d // LANE, LANE), name="gather_c").reshape(N_DEV * nb, d)
    b_shard = lax.dynamic_slice(b_ada, (0, chip * n_ada), (1, n_ada))
    mod_part = _ada_fwd(c_all, w_ada[0], b_shard, name="ada_fwd")
    conv_rows = jnp.pad(conv_w[0], ((0, 5), (0, n_ada - conv_w.shape[2])))
    part = jnp.concatenate([mod_part, conv_rows], axis=0)
    parts = _allgather8(part, name="gather_mod").reshape(N_DEV, N_DEV * nb + 8, n_ada)
    mod_all = jnp.concatenate([parts[2 * k, :N_DEV * nb, :] for k in range(N_CHIPS)], axis=1)
    mod = lax.dynamic_slice(mod_all, (dev * nb, 0), (nb, N_MOD * d)).reshape(nb, N_MOD, d)
    cw_full = jnp.concatenate([parts[2 * k, N_DEV * nb:, :conv_w.shape[2]] for k in range(N_CHIPS)], axis=1)

    shards = [_cast_bf16(ffn1_w_gate_up[0], name="cast_gu1"), _cast_bf16(ffn1_w_down[0], name="cast_d1"),
              _cast_bf16(w_in[0].T, name="cast_in"), _cast_bf16(w_out[0], name="cast_out"),
              _cast_bf16(ffn2_w_gate_up[0], name="cast_gu2"), _cast_bf16(ffn2_w_down[0], name="cast_d2")]
    sh_gu1, sh_d1, sh_in, sh_out, sh_gu2, sh_d2 = shards
    n_gu, n_d, n_in, n_out = sh_gu1.shape[1], sh_d1.shape[0], sh_in.shape[0], sh_out.shape[0]
    whole = lambda shard, col_kind: (shard, col_kind, 0, shard.shape[0] // 2, None)
    gu_cuts = [0, 176, 352, sh_gu2.shape[0] // 2]

    (wgu1,) = _comm_call(_GatherJob([whole(sh_gu1, True)]), name="gather_gu1")
    (h1, a1, gu1), (wd1, wout) = _ffn_up(x2, ln1, mod, wgu1, seq=seq, sc_idx=1, sh_idx=0, use_ln=False,
                                         name="ffn1_up", comm=_GatherJob([whole(sh_d1, False), whole(sh_out, False)]))
    (f1, xhat1, rstd1), (win_t,) = _ffn_down_ln(a1, wd1, x2, ln1, mod, seq=seq, gate_idx=2, use_ln=False,
                                                name="ffn1_down", comm=_GatherJob([whole(sh_in, False)]))
    (h2, q, k, v, ubc), (wgu2_a,) = _in_proj(
        xhat1, ln1, mod, win_t, cos_t, sa_t, sb_t, seq=seq, sc_idx=4, sh_idx=3, name="in_proj",
        comm=_GatherJob([(sh_gu2, True, gu_cuts[0], gu_cuts[1] - gu_cuts[0], None)]))
    attn, (wgu2_b,) = _attention(q, k, v, sinks, seq=seq, name="attention",
                                 comm=_GatherJob([(sh_gu2, True, gu_cuts[1], gu_cuts[2] - gu_cuts[1], wgu2_a)]))
    (mixin, mix, xhat2, rstd2), (wgu2,) = _out_proj(
        attn, ubc, cw_full, wout, xhat1, ln1, mod, seq=seq, gate_idx=5, name="out_proj",
        comm=_GatherJob([(sh_gu2, True, gu_cuts[2], gu_cuts[3] - gu_cuts[2], wgu2_b)]))
    (h3, a3, gu3), (wd2,) = _ffn_up(xhat2, ln2, mod, wgu2, seq=seq, sc_idx=7, sh_idx=6, use_ln=True, name="ffn2_up",
                                    comm=_GatherJob([whole(sh_d2, False)]))
    dr3, df3, loss_cols, dln3g, dln3b, dgate3 = _ffn_down_loss(
        a3, wd2, xhat2, ln2, mod, ln3, tgt2, seq=seq, gate_idx=8, name="ffn2_down_loss")

    def pair_sum(g, r3, col_kind, name_, comm=None):
        if col_kind:
            g3 = g.reshape(2, g.shape[0] // 2, g.shape[1])
            blk_of = lambda p_, pos_: pos_[2]
        else:
            g3 = g.reshape(2 * N_CHIPS, g.shape[0] // (2 * N_CHIPS), g.shape[1])
            blk_of = lambda p_, pos_: 2 * p_ + pos_[2]
        return _sum_pair(pos, g3, r3, blk_of, name=name_, comm=comm)

    dgu3 = _ffn_bwd_act(df3, wd2, gu3, seq=seq, name="ffn2_bwd_act")
    g_wd2 = _matmul_tn(a3, df3, tmm=f // 2, tnn=d, name="grad_wd2")
    g_wgu2, (sib_d2,) = _matmul_tn(h3, dgu3, tmm=d, tnn=(2 * f) // 4, name="grad_wgu2",
                                   comm=_SwapJob([g_wd2], [False]))
    s32_d2, s16_d2 = pair_sum(g_wd2, sib_d2, False, "sum_pair_d2")
    (dr2, dmix, dsc3, dsh3, dgate2, dln2g, dln2b), (sib_gu2, recv_d2) = _bwd_in(
        dgu3, wgu2, dr3, xhat2, rstd2, ln2, mod, mix, seq=seq, w_is_nt=True, sc_idx=7, gate_idx=5,
        branch_scale=1.0, final=False, name="ffn2_bwd_in",
        comm=_MultiJob([_SwapJob([g_wgu2], [True]), _ExchangeJob([s16_d2], [False], [n_d])]))
    s32_gu2, s16_gu2 = pair_sum(g_wgu2, sib_gu2, True, "sum_pair_gu2")
    g_wout = _matmul_tn(mixin, dmix, tmm=d, tnn=d, name="grad_wout")
    dmixin = _matmul_nt_bf16(dmix, wout, seq=seq, name="out_proj_bwd")
    (dq, dkp, dkc, dvp, dvc, dsink), (recv_gu2, sib_out) = _attention_bwd(
        q, k, v, dmixin, sinks, seq=seq, name="attention_bwd",
        comm=_MultiJob([_ExchangeJob([s16_gu2], [True], [n_gu]), _SwapJob([g_wout], [False])]))
    s32_out, s16_out = pair_sum(g_wout, sib_out, False, "sum_pair_out")
    (dproj, dcw), (recv_out,) = _mix_bwd_assemble(
        dq, dkp, dkc, dvp, dvc, cos_t, sa_t, sb_t, dmixin, ubc, cw_full, seq=seq, name="mix_bwd",
        comm=_ExchangeJob([s16_out], [False], [n_out]))
    g_win_t = _matmul_tn(dproj, h2, tmm=IN_WIDTH // 2, tnn=d, name="grad_win")
    (dr1, df1, dsc2, dsh2, dgate1, dln1g, dln1b), (sib_in,) = _bwd_in(
        dproj, win_t, dr2, xhat1, rstd1, ln1, mod, f1, seq=seq, w_is_nt=False, sc_idx=4, gate_idx=2,
        branch_scale=0.5, final=False, name="in_proj_bwd", comm=_SwapJob([g_win_t], [False]))
    s32_in, s16_in = pair_sum(g_win_t, sib_in, False, "sum_pair_in")
    dgu1, (recv_in,) = _ffn_bwd_act(df1, wd1, gu1, seq=seq, name="ffn1_bwd_act",
                                    comm=_ExchangeJob([s16_in], [False], [n_in]))
    g_wgu1 = _matmul_tn(h1, dgu1, tmm=d, tnn=(2 * f) // 4, name="grad_wgu1")
    g_wd1, (sib_gu1,) = _matmul_tn(a1, df1, tmm=f // 2, tnn=d, name="grad_wd1", comm=_SwapJob([g_wgu1], [True]))
    (s32_gu1, s16_gu1), (sib_d1,) = pair_sum(g_wgu1, sib_gu1, True, "sum_pair_gu1", comm=_SwapJob([g_wd1], [False]))
    s32_d1, s16_d1 = pair_sum(g_wd1, sib_d1, False, "sum_pair_d1")

    def final_half(s32_, recv_, col_kind, n_shard, name_):
        return _sum_final(pos, s32_, recv_, col_kind=col_kind, n_shard=n_shard, name=name_)

    early = [final_half(s32_gu2, recv_gu2, True, n_gu, "sum_final_gu2"),
             final_half(s32_d2, recv_d2, False, n_d, "sum_final_d2"),
             final_half(s32_out, recv_out, False, n_out, "sum_final_out"),
             final_half(s32_in, recv_in, False, n_in, "sum_final_in")]
    (grad_x, dsc1, dsh1), (recv_gu1, recv_d1, full_gu2, full_d2, full_out, full_in) = _bwd_in(
        dgu1, wgu1, dr1, x2, None, None, mod, None, seq=seq, w_is_nt=True, sc_idx=1, gate_idx=None,
        branch_scale=None, final=True, name="ffn1_bwd_in",
        comm=_MultiJob([_ExchangeJob([s16_gu1, s16_d1], [True, False], [n_gu, n_d]), _ShareJob(early)]))
    late = [final_half(s32_gu1, recv_gu1, True, n_gu, "sum_final_gu1"),
            final_half(s32_d1, recv_d1, False, n_d, "sum_final_d1")]

    dmod = jnp.concatenate([dsh1, dsc1, dgate1, dsh2, dsc2, dgate2, dsh3, dsc3, dgate3], axis=1)
    loss_row = jnp.sum(loss_cols, axis=1, keepdims=True) * (0.5 / d)
    lane_row = lambda a: jnp.pad(a, ((0, 0), (0, d - a.shape[1])))
    block = jnp.concatenate(
        [dmod.reshape(nb * N_MOD, d), dln1g, dln1b, dln2g, dln2b, dln3g, dln3b,
         lane_row(dcw[0:3, :]), lane_row(dsink[:, 0:1].reshape(1, N_Q_HEADS)), lane_row(loss_row)], axis=0)
    block = jnp.pad(block, ((0, SMALL_ROWS - block.shape[0]), (0, 0)))
    gathered, (full_gu1, full_d1) = _allgather8(block, name="gather_small", comm=_ShareJob(late))
    gathered = gathered.reshape(N_DEV, SMALL_ROWS, d)
    dmod_all = gathered[:, :nb * N_MOD, :].reshape(N_DEV * nb, N_MOD * d)
    dmod_shard = lax.dynamic_slice(dmod_all, (0, chip * n_ada), (N_DEV * nb, n_ada))
    small, g_w_ada, g_b_ada = _small_finish(gathered, dmod_all, dmod_shard, c_all.T, name="small_finish")
    r0 = nb * N_MOD
    loss = small[r0 + 10, 0]
    g_ln = [small[r0 + i:r0 + i + 1, :] for i in range(6)]
    g_cw_full = small[r0 + 6:r0 + 9, :CONV_WIDTH]
    g_conv = lax.dynamic_slice(g_cw_full, (0, chip * conv_w.shape[2]), (3, conv_w.shape[2]))
    g_sinks = small[r0 + 9:r0 + 10, :N_Q_HEADS]

    def flat2(a):
        return a.reshape(-1, a.shape[-1])

    def unhalve(a):
        return a.reshape(2 * a.shape[1], a.shape[2])

    results = {}

    def adamw(name_, w_, g_, m_, v_):
        g2 = flat2(g_)
        dl, nm, nv = _adamw(flat2(w_), g2, flat2(m_), flat2(v_), name="adamw_" + name_)
        results[name_] = tuple(a.reshape(w_.shape) for a in (g2, dl, nm, nv))

    adamw("w_ada", w_ada, g_w_ada, m_w_ada, v_w_ada)
    adamw("ffn2_w_gate_up", ffn2_w_gate_up, unhalve(full_gu2), m_ffn2_w_gate_up, v_ffn2_w_gate_up)
    adamw("ffn2_w_down", ffn2_w_down, unhalve(full_d2), m_ffn2_w_down, v_ffn2_w_down)
    adamw("w_out", w_out, unhalve(full_out), m_w_out, v_w_out)
    adamw("w_in", w_in, unhalve(full_in).T, m_w_in, v_w_in)
    adamw("ffn1_w_gate_up", ffn1_w_gate_up, unhalve(full_gu1), m_ffn1_w_gate_up, v_ffn1_w_gate_up)
    adamw("ffn1_w_down", ffn1_w_down, unhalve(full_d1), m_ffn1_w_down, v_ffn1_w_down)
    adamw("b_ada", b_ada, g_b_ada, m_b_ada, v_b_ada)
    adamw("ln1_g", ln1_g, g_ln[0], m_ln1_g, v_ln1_g)
    adamw("ln1_b", ln1_b, g_ln[1], m_ln1_b, v_ln1_b)
    adamw("ln2_g", ln2_g, g_ln[2], m_ln2_g, v_ln2_g)
    adamw("ln2_b", ln2_b, g_ln[3], m_ln2_b, v_ln2_b)
    adamw("ln3_g", ln3_g, g_ln[4], m_ln3_g, v_ln3_g)
    adamw("ln3_b", ln3_b, g_ln[5], m_ln3_b, v_ln3_b)
    adamw("conv_w", conv_w, g_conv, m_conv_w, v_conv_w)
    adamw("attn_sinks", attn_sinks, g_sinks, m_attn_sinks, v_attn_sinks)
    order = ["w_ada", "b_ada", "ffn1_w_gate_up", "ffn1_w_down", "ln1_g", "ln1_b", "w_in", "conv_w", "attn_sinks",
             "w_out", "ln2_g", "ln2_b", "ffn2_w_gate_up", "ffn2_w_down", "ln3_g", "ln3_b"]
    return (loss, grad_x.reshape(x.shape), *[results[n_][0] for n_ in order], *[results[n_][1] for n_ in order],
            *[results[n_][2] for n_ in order], *[results[n_][3] for n_ in order])
```

```python
import jax
import jax.numpy as jnp
from jax import lax
from jax.experimental import pallas as pl
from jax.experimental.pallas import tpu as pltpu

F32 = jnp.float32
BF16 = jnp.bfloat16
MESH = pl.DeviceIdType.MESH

D_MODEL = 1024
HEAD_DIM = 64
ATTN_WIDTH = 512
CONV_WIDTH = 512
N_Q_HEADS = 8
N_KV_HEADS = 2
GQA_GROUP = 4
KV_WIDTH = 128
WINDOW = 128
BLOCK = 128
ROT_DIM = 16
ROPE_THETA = 500000.0
N_MOD = 9
LN_EPS = 1e-5
DN_ALPHA = 2.0 ** 0.25
IN_WIDTH = 2304
N_CHIPS = 4
N_DEV = 8
SMALL_ROWS = 32

ADAM_LR = 0.001
ADAM_B1 = 0.9
ADAM_B2 = 0.999
ADAM_EPS = 1e-08
ADAM_WD = 0.01
ADAM_STEP = 10

LANE = 128
COL_CHUNK = 256
VMEM_LIMIT = 56 * 1024 * 1024


def _params(sem=None, vmem=True):
    return pltpu.CompilerParams(dimension_semantics=sem, vmem_limit_bytes=VMEM_LIMIT if vmem else None)


def _sigmoid(g):
    return 0.5 * jnp.tanh(0.5 * g) + 0.5


def _row_sum(v):
    return jnp.sum(v, axis=0, keepdims=True)


ROW_CHUNK = 16
EPILOGUE_UNROLL = 8


def _fold8(v):
    return v[0:8, :] + v[8:16, :]


def _row_chunk_loop(n_rows, step, init):
    per_iter = ROW_CHUNK * EPILOGUE_UNROLL
    assert n_rows % per_iter == 0, n_rows

    def body(it, carry):
        for s in range(EPILOGUE_UNROLL):
            start = pl.multiple_of(it * per_iter + s * ROW_CHUNK, ROW_CHUNK)
            carry = step(pl.ds(start, ROW_CHUNK), carry)
        return carry

    return lax.fori_loop(0, n_rows // per_iter, body, init)


def _ln_stats(r):
    mu = jnp.mean(r, axis=-1, keepdims=True)
    rc = r - mu
    var = jnp.mean(rc * rc, axis=-1, keepdims=True)
    rstd = lax.rsqrt(var + LN_EPS)
    return rc * rstd, rstd


def _ln_bwd(dxo, xhat, rstd, g):
    dxhat = dxo * g
    m1 = jnp.mean(dxhat, axis=-1, keepdims=True)
    m2 = jnp.mean(dxhat * xhat, axis=-1, keepdims=True)
    return rstd * (dxhat - m1 - xhat * m2)


def _dot_nt(a, b):
    return lax.dot_general(a, b, (((1,), (1,)), ((), ())), preferred_element_type=F32)


def _dot_tn(a, b):
    return lax.dot_general(a, b, (((0,), (0,)), ((), ())), preferred_element_type=F32)


def _full(shape):
    nd = len(shape)
    return pl.BlockSpec(shape, lambda *_: (0,) * nd)


def _resident(shape):
    nd = len(shape)
    return pl.BlockSpec(shape, lambda *_: (0,) * nd, pipeline_mode=pl.Buffered(1))


ANY_SPEC = pl.BlockSpec(memory_space=pl.ANY)


def _pcall(body, *, name, grid, in_specs, out_specs, out_shape, args, scratch_shapes=(), comm=None, prefetch=None):
    single = not isinstance(out_shape, (list, tuple))
    out_specs = [out_specs] if single else list(out_specs)
    out_shape = [out_shape] if single else list(out_shape)
    in_specs = list(in_specs)
    scratch_shapes = list(scratch_shapes)
    sem = ("arbitrary",) * len(grid)
    n_pre = 0 if prefetch is None else 1
    pre_args = () if prefetch is None else (prefetch,)

    def call(fn, ins_, outs_, shapes_, scratch_, aliases_, operands):
        if prefetch is None:
            return pl.pallas_call(fn, name=name, grid=grid, in_specs=ins_, out_specs=outs_, out_shape=shapes_,
                                  scratch_shapes=scratch_, input_output_aliases=aliases_,
                                  compiler_params=_params(sem))(*operands)
        spec = pltpu.PrefetchScalarGridSpec(num_scalar_prefetch=1, grid=grid, in_specs=ins_, out_specs=outs_,
                                            scratch_shapes=scratch_)
        return pl.pallas_call(fn, name=name, grid_spec=spec, out_shape=shapes_,
                              input_output_aliases={n_pre + i: o for i, o in aliases_.items()},
                              compiler_params=_params(sem))(*pre_args, *operands)

    if comm is None:
        res = call(body, in_specs, out_specs, out_shape, scratch_shapes, {}, args)
        return res[0] if single else res
    n_in, n_out, n_scr = len(in_specs), len(out_specs), len(scratch_shapes)
    nci, nco = len(comm.inputs), len(comm.out_shapes)

    def wrapped(*refs):
        pre, refs = refs[:n_pre], refs[n_pre:]
        ins, refs = refs[:n_in], refs[n_in:]
        cin, refs = refs[:nci], refs[nci:]
        outs, refs = refs[:n_out], refs[n_out:]
        cout, refs = refs[:nco], refs[nco:]
        scr, csems = refs[:n_scr], refs[n_scr:]
        first = pl.program_id(0) == 0
        last = pl.program_id(0) == grid[0] - 1
        for ax in range(1, len(grid)):
            first = jnp.logical_and(first, pl.program_id(ax) == 0)
            last = jnp.logical_and(last, pl.program_id(ax) == grid[ax] - 1)

        @pl.when(first)
        def _():
            comm.start(cin, cout, csems)

        body(*pre, *ins, *outs, *scr)

        @pl.when(last)
        def _():
            comm.finish(cin, cout, csems)

    res = call(wrapped, in_specs + [ANY_SPEC] * nci, out_specs + [ANY_SPEC] * nco,
               out_shape + list(comm.out_shapes), scratch_shapes + list(comm.sems),
               {n_in + i: n_out + o for i, o in comm.aliases.items()}, (*args, *comm.inputs))
    main = res[:n_out]
    return (main[0] if single else main), list(res[n_out:])


def _comm_call(job, *, name):
    nci, nco = len(job.inputs), len(job.out_shapes)

    def body(*refs):
        cin, refs = refs[:nci], refs[nci:]
        cout, csems = refs[:nco], refs[nco:]
        job.start(cin, cout, csems)
        job.finish(cin, cout, csems)

    return pl.pallas_call(
        body, name=name, out_shape=list(job.out_shapes), in_specs=[ANY_SPEC] * nci, out_specs=[ANY_SPEC] * nco,
        scratch_shapes=list(job.sems), input_output_aliases=dict(job.aliases))(*job.inputs)


def _ffn_up(xin, lnp, mod, w, *, seq, sc_idx, sh_idx, use_ln, name, comm=None):
    t, d = xin.shape
    f = w.shape[1] // 2
    tm = min(512, seq)
    tpb = seq // tm
    ch = min(COL_CHUNK, f)

    def body(x_ref, ln_ref, mod_ref, w_ref, h_ref, a_ref, gu_ref):
        x = x_ref[...]
        if use_ln:
            x = x * ln_ref[0:1, :] + ln_ref[1:2, :]
        h = x * (1.0 + mod_ref[0, sc_idx:sc_idx + 1, :]) + mod_ref[0, sh_idx:sh_idx + 1, :]
        hb = h.astype(BF16)
        h_ref[...] = hb
        for j in range(f // ch):
            g = jnp.dot(hb, w_ref[:, j * ch:(j + 1) * ch], preferred_element_type=F32)
            u = jnp.dot(hb, w_ref[:, f + j * ch:f + (j + 1) * ch], preferred_element_type=F32)
            a_ref[:, j * ch:(j + 1) * ch] = (g * _sigmoid(g) * u).astype(BF16)
            gu_ref[:, j * ch:(j + 1) * ch] = g.astype(BF16)
            gu_ref[:, f + j * ch:f + (j + 1) * ch] = u.astype(BF16)

    return _pcall(
        body, name=name, grid=(t // tm,),
        in_specs=[pl.BlockSpec((tm, d), lambda i: (i, 0)), _full((2, d)),
                  pl.BlockSpec((1, N_MOD, d), lambda i: (i // tpb, 0, 0)), _resident((d, 2 * f))],
        out_specs=[pl.BlockSpec((tm, d), lambda i: (i, 0)), pl.BlockSpec((tm, f), lambda i: (i, 0)),
                   pl.BlockSpec((tm, 2 * f), lambda i: (i, 0))],
        out_shape=[jax.ShapeDtypeStruct((t, d), BF16), jax.ShapeDtypeStruct((t, f), BF16),
                   jax.ShapeDtypeStruct((t, 2 * f), BF16)],
        args=(xin, lnp, mod, w), comm=comm)


def _ffn_down_ln(a, wd, xin, lnp_in, mod, *, seq, gate_idx, use_ln, name, comm=None):
    t, f = a.shape
    d = wd.shape[1]
    tm = min(512, seq)
    tpb = seq // tm

    def body(a_ref, wd_ref, x_ref, ln_ref, mod_ref, f_ref, xhat_ref, rstd_ref, acc):
        av = a_ref[...]
        for j in range(d // COL_CHUNK):
            acc[:, j * COL_CHUNK:(j + 1) * COL_CHUNK] = jnp.dot(
                av, wd_ref[:, j * COL_CHUNK:(j + 1) * COL_CHUNK], preferred_element_type=F32)
        scale = 0.5 * (1.0 + mod_ref[0, gate_idx:gate_idx + 1, :])

        fo = acc[...]
        x = x_ref[...]
        if use_ln:
            x = x * ln_ref[0:1, :] + ln_ref[1:2, :]
        xhat, rstd = _ln_stats(DN_ALPHA * x + scale * fo)
        f_ref[...] = fo.astype(BF16)
        xhat_ref[...] = xhat
        rstd_ref[...] = rstd

    return _pcall(
        body, name=name, grid=(t // tm,),
        in_specs=[pl.BlockSpec((tm, f), lambda i: (i, 0)), _resident((f, d)),
                  pl.BlockSpec((tm, d), lambda i: (i, 0)), _full((2, d)),
                  pl.BlockSpec((1, N_MOD, d), lambda i: (i // tpb, 0, 0))],
        out_specs=[pl.BlockSpec((tm, d), lambda i: (i, 0)), pl.BlockSpec((tm, d), lambda i: (i, 0)),
                   pl.BlockSpec((tm, 1), lambda i: (i, 0))],
        out_shape=[jax.ShapeDtypeStruct((t, d), BF16), jax.ShapeDtypeStruct((t, d), F32),
                   jax.ShapeDtypeStruct((t, 1), F32)],
        scratch_shapes=[pltpu.VMEM((tm, d), F32)],
        args=(a, wd, xin, lnp_in, mod), comm=comm)


def _ffn_down_loss(a, wd, xhat_in, lnp_in, mod, lnp_out, tgt, *, seq, gate_idx, name):
    t, f = a.shape
    d = wd.shape[1]
    nb = t // seq
    tm = min(512, seq)
    tpb = seq // tm

    def body(a_ref, wd_ref, x_ref, lnin_ref, mod_ref, lnout_ref, tgt_ref,
             dr_ref, df_ref, loss_ref, dg_ref, db_ref, dgate_ref, acc):
        i = pl.program_id(0)
        av = a_ref[...]
        for j in range(d // COL_CHUNK):
            acc[:, j * COL_CHUNK:(j + 1) * COL_CHUNK] = jnp.dot(
                av, wd_ref[:, j * COL_CHUNK:(j + 1) * COL_CHUNK], preferred_element_type=F32)
        scale = 0.5 * (1.0 + mod_ref[0, gate_idx:gate_idx + 1, :])
        g_in, b_in = lnin_ref[0:1, :], lnin_ref[1:2, :]
        g_out, b_out = lnout_ref[0:1, :], lnout_ref[1:2, :]

        def chunk(rows, carry):
            s_loss, s_dg, s_db, s_gate = carry
            fo = acc[rows, :]
            xhat, rstd = _ln_stats(DN_ALPHA * (x_ref[rows, :] * g_in + b_in) + scale * fo)
            e = xhat * g_out + b_out - tgt_ref[rows, :]
            dy = e * (1.0 / d)
            dr = _ln_bwd(dy, xhat, rstd, g_out)
            dr_ref[rows, :] = dr
            df_ref[rows, :] = (scale * dr).astype(BF16)
            return (s_loss + _fold8(e * e), s_dg + _fold8(dy * xhat), s_db + _fold8(dy),
                    s_gate + _fold8(0.5 * fo * dr))

        zero = jnp.zeros((8, d), F32)
        s_loss, s_dg, s_db, s_gate = _row_chunk_loop(tm, chunk, (zero, zero, zero, zero))

        @pl.when(i == 0)
        def _():
            loss_ref[...] = jnp.zeros_like(loss_ref)
            dg_ref[...] = jnp.zeros_like(dg_ref)
            db_ref[...] = jnp.zeros_like(db_ref)

        @pl.when(i % tpb == 0)
        def _():
            dgate_ref[...] = jnp.zeros_like(dgate_ref)

        loss_ref[...] += _row_sum(s_loss)
        dg_ref[...] += _row_sum(s_dg)
        db_ref[...] += _row_sum(s_db)
        dgate_ref[0] += _row_sum(s_gate)

    return pl.pallas_call(
        body, name=name, grid=(t // tm,), scratch_shapes=[pltpu.VMEM((tm, d), F32)],
        in_specs=[pl.BlockSpec((tm, f), lambda i: (i, 0)), _resident((f, d)),
                  pl.BlockSpec((tm, d), lambda i: (i, 0)), _full((2, d)),
                  pl.BlockSpec((1, N_MOD, d), lambda i: (i // tpb, 0, 0)), _full((2, d)),
                  pl.BlockSpec((tm, d), lambda i: (i, 0))],
        out_specs=[pl.BlockSpec((tm, d), lambda i: (i, 0)), pl.BlockSpec((tm, d), lambda i: (i, 0)),
                   _full((1, d)), _full((1, d)), _full((1, d)),
                   pl.BlockSpec((1, 1, d), lambda i: (i // tpb, 0, 0))],
        out_shape=[jax.ShapeDtypeStruct((t, d), F32), jax.ShapeDtypeStruct((t, d), BF16),
                   jax.ShapeDtypeStruct((1, d), F32), jax.ShapeDtypeStruct((1, d), F32),
                   jax.ShapeDtypeStruct((1, d), F32), jax.ShapeDtypeStruct((nb, 1, d), F32)],
        compiler_params=_params(("arbitrary",)),
    )(a, wd, xhat_in, lnp_in, mod, lnp_out, tgt)


def _rope(v, cos, sa, sb):
    return v * cos + pltpu.roll(v, LANE - ROT_DIM // 2, 1) * sa + pltpu.roll(v, ROT_DIM // 2, 1) * sb


def _rope_t(dy, cos, sa, sb):
    return dy * cos + pltpu.roll(dy * sa, ROT_DIM // 2, 1) + pltpu.roll(dy * sb, LANE - ROT_DIM // 2, 1)


def _in_proj(xhat, lnp, mod, w_t, cos, sa, sb, *, seq, sc_idx, sh_idx, name, comm=None):
    t, d = xhat.shape
    tm = min(512, seq)
    tpb = seq // tm
    n_conv = 3 * CONV_WIDTH

    def body(x_ref, ln_ref, mod_ref, w_ref, cos_ref, sa_ref, sb_ref, h_ref, q_ref, k_ref, v_ref, ubc_ref):
        x = x_ref[...] * ln_ref[0:1, :] + ln_ref[1:2, :]
        h = x * (1.0 + mod_ref[0, sc_idx:sc_idx + 1, :]) + mod_ref[0, sh_idx:sh_idx + 1, :]
        hb = h.astype(BF16)
        h_ref[...] = hb
        cos_t, sa_t, sb_t = cos_ref[...], sa_ref[...], sb_ref[...]
        for j in range(ATTN_WIDTH // COL_CHUNK):
            p = _dot_nt(hb, w_ref[j * COL_CHUNK:(j + 1) * COL_CHUNK, :])
            for s in range(COL_CHUNK // LANE):
                q_ref[:, j * COL_CHUNK + s * LANE:j * COL_CHUNK + (s + 1) * LANE] = _rope(
                    p[:, s * LANE:(s + 1) * LANE], cos_t, sa_t, sb_t).astype(BF16)
        p = _dot_nt(hb, w_ref[ATTN_WIDTH:ATTN_WIDTH + 2 * KV_WIDTH, :])
        k_ref[...] = _rope(p[:, 0:KV_WIDTH], cos_t, sa_t, sb_t).astype(BF16)
        v_ref[...] = p[:, KV_WIDTH:].astype(BF16)
        base = ATTN_WIDTH + 2 * KV_WIDTH
        for j in range(n_conv // COL_CHUNK):
            ubc_ref[:, j * COL_CHUNK:(j + 1) * COL_CHUNK] = _dot_nt(
                hb, w_ref[base + j * COL_CHUNK:base + (j + 1) * COL_CHUNK, :])

    row = lambda w: pl.BlockSpec((tm, w), lambda i: (i, 0))
    return _pcall(
        body, name=name, grid=(t // tm,),
        in_specs=[row(d), _full((2, d)), pl.BlockSpec((1, N_MOD, d), lambda i: (i // tpb, 0, 0)),
                  _resident((IN_WIDTH, d)), row(LANE), row(LANE), row(LANE)],
        out_specs=[row(d), row(ATTN_WIDTH), row(KV_WIDTH), row(KV_WIDTH), row(n_conv)],
        out_shape=[jax.ShapeDtypeStruct((t, d), BF16), jax.ShapeDtypeStruct((t, ATTN_WIDTH), BF16),
                   jax.ShapeDtypeStruct((t, KV_WIDTH), BF16), jax.ShapeDtypeStruct((t, KV_WIDTH), BF16),
                   jax.ShapeDtypeStruct((t, n_conv), F32)],
        args=(xhat, lnp, mod, w_t, cos, sa, sb), comm=comm)


def _attn_group(q_ref, kp_ref, kc_ref, vp_ref, vc_ref, sink_ref, g, first):
    lo, hi = g * HEAD_DIM, (g + 1) * HEAD_DIM
    kk = jnp.concatenate([kp_ref[:, lo:hi], kc_ref[:, lo:hi]], axis=0)
    vv = jnp.concatenate([vp_ref[:, lo:hi], vc_ref[:, lo:hi]], axis=0)
    qs = jnp.concatenate([q_ref[:, (GQA_GROUP * g + j) * HEAD_DIM:(GQA_GROUP * g + j + 1) * HEAD_DIM]
                          for j in range(GQA_GROUP)], axis=0)
    rows = GQA_GROUP * BLOCK
    row = lax.broadcasted_iota(jnp.int32, (rows, 2 * BLOCK), 0)
    ki = lax.broadcasted_iota(jnp.int32, (rows, 2 * BLOCK), 1)
    diff = (row & (BLOCK - 1)) + BLOCK - ki
    valid = (diff >= 0) & (diff < WINDOW) & ((ki >= BLOCK) | jnp.logical_not(first))
    s = _dot_nt(qs, kk) * (HEAD_DIM ** -0.5)
    s = jnp.where(valid, s, -1e30)
    rcol = lax.broadcasted_iota(jnp.int32, (rows, 1), 0)
    sink = jnp.zeros((rows, 1), F32)
    for j in range(GQA_GROUP):
        sink = jnp.where(rcol // BLOCK == j, sink_ref[GQA_GROUP * g + j], sink)
    m = jnp.maximum(jnp.max(s, axis=1, keepdims=True), sink)
    p = jnp.exp(s - m)
    ps = jnp.exp(sink - m)
    inv = 1.0 / (jnp.sum(p, axis=1, keepdims=True) + ps)
    return qs, kk, vv, p * inv, ps * inv


def _attention(q, k, v, sinks, *, seq, name, comm=None):
    t = q.shape[0]
    nblk = seq // BLOCK

    def body(q_ref, kp_ref, kc_ref, vp_ref, vc_ref, sink_ref, o_ref):
        first = (pl.program_id(0) % nblk) == 0
        outs = []
        for g in range(N_KV_HEADS):
            _, _, vv, pn, _ = _attn_group(q_ref, kp_ref, kc_ref, vp_ref, vc_ref, sink_ref, g, first)
            o = jnp.dot(pn.astype(BF16), vv, preferred_element_type=F32)
            outs += [o[j * BLOCK:(j + 1) * BLOCK, :] for j in range(GQA_GROUP)]
        o_ref[...] = jnp.concatenate(outs, axis=1).astype(BF16)

    cur = lambda w: pl.BlockSpec((BLOCK, w), lambda n: (n, 0))
    prev = lambda w: pl.BlockSpec((BLOCK, w), lambda n: (jnp.maximum(n - 1, 0), 0))
    return _pcall(
        body, name=name, grid=(t // BLOCK,),
        in_specs=[cur(ATTN_WIDTH), prev(KV_WIDTH), cur(KV_WIDTH), prev(KV_WIDTH), cur(KV_WIDTH),
                  pl.BlockSpec(memory_space=pltpu.SMEM)],
        out_specs=cur(ATTN_WIDTH),
        out_shape=jax.ShapeDtypeStruct((t, ATTN_WIDTH), BF16),
        args=(q, k, k, v, v, sinks), comm=comm)


def _out_proj(attn, ubc, cw, wout, xhat_in, lnp_in, mod, *, seq, gate_idx, name, comm=None):
    t, d = xhat_in.shape
    tm = min(512, seq)
    tpb = seq // tm
    cwid = CONV_WIDTH

    def body(attn_ref, ubc_ref, halo_ref, cw_ref, w_ref, x_ref, ln_ref, mod_ref,
             mixin_ref, mix_ref, xhat_ref, rstd_ref, zbuf, acc):
        first = (pl.program_id(0) % tpb) == 0
        u, bg, cg = ubc_ref[:, 0:cwid], ubc_ref[:, cwid:2 * cwid], ubc_ref[:, 2 * cwid:3 * cwid]
        z = cg * u
        hz = halo_ref[:, 2 * cwid:3 * cwid] * halo_ref[:, 0:cwid]
        zbuf[0:8, :] = jnp.where(first, 0.0, hz)
        zbuf[8:8 + tm, :] = z
        y = cw_ref[0:1, :] * zbuf[6:6 + tm, :] + cw_ref[1:2, :] * zbuf[7:7 + tm, :] + cw_ref[2:3, :] * z
        mixin_ref[:, 0:ATTN_WIDTH] = attn_ref[...]
        mixin_ref[:, ATTN_WIDTH:] = (bg * y).astype(BF16)
        mv = mixin_ref[...]
        for j in range(d // COL_CHUNK):
            acc[:, j * COL_CHUNK:(j + 1) * COL_CHUNK] = jnp.dot(
                mv, w_ref[:, j * COL_CHUNK:(j + 1) * COL_CHUNK], preferred_element_type=F32)
        scale = 1.0 + mod_ref[0, gate_idx:gate_idx + 1, :]

        mix = acc[...]
        xhat, rstd = _ln_stats(DN_ALPHA * (x_ref[...] * ln_ref[0:1, :] + ln_ref[1:2, :]) + scale * mix)
        mix_ref[...] = mix.astype(BF16)
        xhat_ref[...] = xhat
        rstd_ref[...] = rstd

    row = lambda w: pl.BlockSpec((tm, w), lambda i: (i, 0))
    return _pcall(
        body, name=name, grid=(t // tm,),
        in_specs=[row(ATTN_WIDTH), row(3 * cwid),
                  pl.BlockSpec((8, 3 * cwid), lambda i: (jnp.maximum(i * (tm // 8) - 1, 0), 0)),
                  _full((8, cwid)), _resident((d, d)), row(d), _full((2, d)),
                  pl.BlockSpec((1, N_MOD, d), lambda i: (i // tpb, 0, 0))],
        out_specs=[row(d), row(d), row(d), row(1)],
        out_shape=[jax.ShapeDtypeStruct((t, d), BF16), jax.ShapeDtypeStruct((t, d), BF16),
                   jax.ShapeDtypeStruct((t, d), F32), jax.ShapeDtypeStruct((t, 1), F32)],
        scratch_shapes=[pltpu.VMEM((tm + 8, cwid), F32), pltpu.VMEM((tm, d), F32)],
        args=(attn, ubc, ubc, cw, wout, xhat_in, lnp_in, mod), comm=comm)


def _ffn_bwd_act(df, wd, gu, *, seq, name, comm=None):
    t, d = df.shape
    f = wd.shape[0]
    tm = min(512, seq)
    ch = min(COL_CHUNK, f)

    def body(df_ref, wd_ref, gu_ref, dgu_ref):
        dfv = df_ref[...]
        for j in range(f // ch):
            da = _dot_nt(dfv, wd_ref[j * ch:(j + 1) * ch, :])
            g = gu_ref[:, j * ch:(j + 1) * ch].astype(F32)
            u = gu_ref[:, f + j * ch:f + (j + 1) * ch].astype(F32)
            s = _sigmoid(g)
            dgu_ref[:, j * ch:(j + 1) * ch] = (da * u * (s * (1.0 + g * (1.0 - s)))).astype(BF16)
            dgu_ref[:, f + j * ch:f + (j + 1) * ch] = (da * (g * s)).astype(BF16)

    return _pcall(
        body, name=name, grid=(t // tm,),
        in_specs=[pl.BlockSpec((tm, d), lambda i: (i, 0)), _resident((f, d)),
                  pl.BlockSpec((tm, 2 * f), lambda i: (i, 0))],
        out_specs=pl.BlockSpec((tm, 2 * f), lambda i: (i, 0)),
        out_shape=jax.ShapeDtypeStruct((t, 2 * f), BF16),
        args=(df, wd, gu), comm=comm)


def _bwd_in(a, w, dr, xin, rstd_prev, lnp_prev, mod, branch_prev, *, seq, w_is_nt, sc_idx, gate_idx,
            branch_scale, final, name, comm=None):
    t, kdim = a.shape
    d = dr.shape[1]
    nb = t // seq
    tm = min(512, seq)
    tpb = seq // tm

    def body(*refs):
        if final:
            a_ref, w_ref, dr_ref, x_ref, mod_ref, dx_ref, dsc_ref, dsh_ref, acc = refs
        else:
            (a_ref, w_ref, dr_ref, x_ref, rstd_ref, ln_ref, mod_ref, br_ref,
             drp_ref, dbr_ref, dsc_ref, dsh_ref, dgate_ref, dg_ref, db_ref, acc) = refs
        i = pl.program_id(0)
        av = a_ref[...]
        for j in range(d // COL_CHUNK):
            cols = slice(j * COL_CHUNK, (j + 1) * COL_CHUNK)
            acc[:, cols] = (_dot_nt(av, w_ref[cols, :]) if w_is_nt
                            else jnp.dot(av, w_ref[:, cols], preferred_element_type=F32))
        sc1 = 1.0 + mod_ref[0, sc_idx:sc_idx + 1, :]
        if not final:
            g_prev, b_prev = ln_ref[0:1, :], ln_ref[1:2, :]
            bscale = branch_scale * (1.0 + mod_ref[0, gate_idx:gate_idx + 1, :])

        def chunk(rows, carry):
            dh = acc[rows, :]
            dx = DN_ALPHA * dr_ref[rows, :] + dh * sc1
            if final:
                dx_ref[rows, :] = dx
                return carry[0] + _fold8(dh * x_ref[rows, :]), carry[1] + _fold8(dh)
            xhat = x_ref[rows, :]
            drp = _ln_bwd(dx, xhat, rstd_ref[rows, :], g_prev)
            drp_ref[rows, :] = drp
            dbr_ref[rows, :] = (bscale * drp).astype(BF16)
            return (carry[0] + _fold8(dh * (xhat * g_prev + b_prev)), carry[1] + _fold8(dh),
                    carry[2] + _fold8(branch_scale * br_ref[rows, :].astype(F32) * drp),
                    carry[3] + _fold8(dx * xhat), carry[4] + _fold8(dx))

        zero = jnp.zeros((8, d), F32)
        sums = _row_chunk_loop(tm, chunk, (zero,) * (2 if final else 5))

        @pl.when((i % tpb) == 0)
        def _():
            dsc_ref[...] = jnp.zeros_like(dsc_ref)
            dsh_ref[...] = jnp.zeros_like(dsh_ref)
            if not final:
                dgate_ref[...] = jnp.zeros_like(dgate_ref)

        dsc_ref[0] += _row_sum(sums[0])
        dsh_ref[0] += _row_sum(sums[1])
        if not final:
            @pl.when(i == 0)
            def _():
                dg_ref[...] = jnp.zeros_like(dg_ref)
                db_ref[...] = jnp.zeros_like(db_ref)

            dgate_ref[0] += _row_sum(sums[2])
            dg_ref[...] += _row_sum(sums[3])
            db_ref[...] += _row_sum(sums[4])

    row = lambda w_: pl.BlockSpec((tm, w_), lambda i: (i, 0))
    vec = pl.BlockSpec((1, 1, d), lambda i: (i // tpb, 0, 0))
    mod_spec = pl.BlockSpec((1, N_MOD, d), lambda i: (i // tpb, 0, 0))
    vshape = jax.ShapeDtypeStruct((nb, 1, d), F32)
    if final:
        in_specs = [row(kdim), _resident(w.shape), row(d), row(d), mod_spec]
        args = (a, w, dr, xin, mod)
        out_specs = [row(d), vec, vec]
        out_shape = [jax.ShapeDtypeStruct((t, d), F32), vshape, vshape]
    else:
        in_specs = [row(kdim), _resident(w.shape), row(d), row(d), row(1), _full((2, d)), mod_spec, row(d)]
        args = (a, w, dr, xin, rstd_prev, lnp_prev, mod, branch_prev)
        out_specs = [row(d), row(d), vec, vec, vec, _full((1, d)), _full((1, d))]
        out_shape = [jax.ShapeDtypeStruct((t, d), F32), jax.ShapeDtypeStruct((t, d), BF16), vshape, vshape, vshape,
                     jax.ShapeDtypeStruct((1, d), F32), jax.ShapeDtypeStruct((1, d), F32)]
    return _pcall(
        body, name=name, grid=(t // tm,), in_specs=in_specs, out_specs=out_specs, out_shape=out_shape,
        scratch_shapes=[pltpu.VMEM((tm, d), F32)], args=args, comm=comm)


def _matmul_tn(a, b, *, tmm, tnn, name, comm=None):
    t, m = a.shape
    n = b.shape[1]
    tk = min(512, t)

    def body(a_ref, b_ref, o_ref):
        @pl.when(pl.program_id(2) == 0)
        def _():
            o_ref[...] = jnp.zeros_like(o_ref)
        o_ref[...] += _dot_tn(a_ref[...], b_ref[...])

    return _pcall(
        body, name=name, grid=(m // tmm, n // tnn, t // tk),
        in_specs=[pl.BlockSpec((tk, tmm), lambda i, j, k: (k, i)), pl.BlockSpec((tk, tnn), lambda i, j, k: (k, j))],
        out_specs=pl.BlockSpec((tmm, tnn), lambda i, j, k: (i, j)),
        out_shape=jax.ShapeDtypeStruct((m, n), F32),
        args=(a, b), comm=comm)


def _matmul_nt_bf16(a, w, *, seq, name):
    t, kdim = a.shape
    n = w.shape[0]
    tm = min(512, seq)

    def body(a_ref, w_ref, o_ref):
        av = a_ref[...]
        for j in range(n // COL_CHUNK):
            o_ref[:, j * COL_CHUNK:(j + 1) * COL_CHUNK] = _dot_nt(
                av, w_ref[j * COL_CHUNK:(j + 1) * COL_CHUNK, :]).astype(BF16)

    return pl.pallas_call(
        body, name=name, grid=(t // tm,),
        in_specs=[pl.BlockSpec((tm, kdim), lambda i: (i, 0)), _resident((n, kdim))],
        out_specs=pl.BlockSpec((tm, n), lambda i: (i, 0)),
        out_shape=jax.ShapeDtypeStruct((t, n), BF16),
        compiler_params=_params(("arbitrary",)),
    )(a, w)


def _attention_bwd(q, k, v, dmixin, sinks, *, seq, name, comm=None):
    t = q.shape[0]
    nblk = seq // BLOCK

    def body(q_ref, kp_ref, kc_ref, vp_ref, vc_ref, do_ref, sink_ref,
             dq_ref, dkp_ref, dkc_ref, dvp_ref, dvc_ref, dsink_ref):
        n = pl.program_id(0)
        first = (n % nblk) == 0

        @pl.when(n == 0)
        def _():
            dsink_ref[...] = jnp.zeros_like(dsink_ref)

        dqs, dks, dvs = [], [], []
        srow = lax.broadcasted_iota(jnp.int32, (8, LANE), 0)
        dsink = jnp.zeros((8, LANE), F32)
        for g in range(N_KV_HEADS):
            qs, kk, vv, pn, psn = _attn_group(q_ref, kp_ref, kc_ref, vp_ref, vc_ref, sink_ref, g, first)
            dos = jnp.concatenate([do_ref[:, (GQA_GROUP * g + j) * HEAD_DIM:(GQA_GROUP * g + j + 1) * HEAD_DIM]
                                   for j in range(GQA_GROUP)], axis=0)
            dp = _dot_nt(dos, vv)
            delta = jnp.sum(pn * dp, axis=1, keepdims=True)
            ds = pn * (dp - delta)
            dsk = psn * delta
            for j in range(GQA_GROUP):
                tot = jnp.sum(dsk[j * BLOCK:(j + 1) * BLOCK, :], axis=0, keepdims=True)
                dsink = dsink - jnp.where(srow == GQA_GROUP * g + j, tot, 0.0)
            dsb = (ds * (HEAD_DIM ** -0.5)).astype(BF16)
            dqg = jnp.dot(dsb, kk, preferred_element_type=F32)
            dqs += [dqg[j * BLOCK:(j + 1) * BLOCK, :] for j in range(GQA_GROUP)]
            dks.append(_dot_tn(dsb, qs))
            dvs.append(_dot_tn(pn.astype(BF16), dos))
        dsink_ref[...] += dsink
        dq_ref[...] = jnp.concatenate(dqs, axis=1)
        dkp_ref[...] = jnp.concatenate([x[0:BLOCK, :] for x in dks], axis=1)
        dkc_ref[...] = jnp.concatenate([x[BLOCK:, :] for x in dks], axis=1)
        dvp_ref[...] = jnp.concatenate([x[0:BLOCK, :] for x in dvs], axis=1)
        dvc_ref[...] = jnp.concatenate([x[BLOCK:, :] for x in dvs], axis=1)

    cur = lambda w: pl.BlockSpec((BLOCK, w), lambda n: (n, 0))
    prev = lambda w: pl.BlockSpec((BLOCK, w), lambda n: (jnp.maximum(n - 1, 0), 0))
    kv = jax.ShapeDtypeStruct((t, KV_WIDTH), F32)
    return _pcall(
        body, name=name, grid=(t // BLOCK,),
        in_specs=[cur(ATTN_WIDTH), prev(KV_WIDTH), cur(KV_WIDTH), prev(KV_WIDTH), cur(KV_WIDTH), cur(ATTN_WIDTH),
                  pl.BlockSpec(memory_space=pltpu.SMEM)],
        out_specs=[cur(ATTN_WIDTH), cur(KV_WIDTH), cur(KV_WIDTH), cur(KV_WIDTH), cur(KV_WIDTH), _full((8, LANE))],
        out_shape=[jax.ShapeDtypeStruct((t, ATTN_WIDTH), F32), kv, kv, kv, kv, jax.ShapeDtypeStruct((8, LANE), F32)],
        args=(q, k, k, v, v, dmixin, sinks), comm=comm)


def _mix_bwd_assemble(dq, dkp, dkc, dvp, dvc, cos, sa, sb, dmixin, ubc, cw, *, seq, name, comm=None):
    t = dq.shape[0]
    nblk = seq // BLOCK
    ntile = t // BLOCK
    cwid = CONV_WIDTH
    tm = BLOCK

    def body(dq_ref, dkc_ref, dkp_ref, dvc_ref, dvp_ref, cos_ref, sa_ref, sb_ref, dco_ref, dcon_ref,
             ubc_ref, hprev_ref, hnext_ref, cw_ref, dproj_ref, dcw_ref, zbuf, dybuf):
        i = pl.program_id(0)
        first = (i % nblk) == 0
        last = (i % nblk) == nblk - 1
        glast = i == ntile - 1

        @pl.when(i == 0)
        def _():
            dcw_ref[...] = jnp.zeros_like(dcw_ref)

        cos_t, sa_t, sb_t = cos_ref[...], sa_ref[...], sb_ref[...]
        for j in range(ATTN_WIDTH // LANE):
            dproj_ref[:, j * LANE:(j + 1) * LANE] = _rope_t(
                dq_ref[:, j * LANE:(j + 1) * LANE], cos_t, sa_t, sb_t).astype(BF16)
        dk = dkc_ref[...] + jnp.where(glast, 0.0, dkp_ref[...])
        dproj_ref[:, ATTN_WIDTH:ATTN_WIDTH + KV_WIDTH] = _rope_t(dk, cos_t, sa_t, sb_t).astype(BF16)
        dv = dvc_ref[...] + jnp.where(glast, 0.0, dvp_ref[...])
        dproj_ref[:, ATTN_WIDTH + KV_WIDTH:ATTN_WIDTH + 2 * KV_WIDTH] = dv.astype(BF16)

        u, bg, cg = ubc_ref[:, 0:cwid], ubc_ref[:, cwid:2 * cwid], ubc_ref[:, 2 * cwid:3 * cwid]
        z = cg * u
        hz = hprev_ref[:, 2 * cwid:3 * cwid] * hprev_ref[:, 0:cwid]
        zbuf[0:8, :] = jnp.where(first, 0.0, hz)
        zbuf[8:8 + tm, :] = z
        z2, z1 = zbuf[6:6 + tm, :], zbuf[7:7 + tm, :]
        w0, w1, w2 = cw_ref[0:1, :], cw_ref[1:2, :], cw_ref[2:3, :]
        y = w0 * z2 + w1 * z1 + w2 * z
        dco = dco_ref[...].astype(F32)
        dyc = dco * bg
        dyn = dcon_ref[0:8, :].astype(F32) * hnext_ref[:, cwid:2 * cwid]
        dybuf[0:tm, :] = dyc
        dybuf[tm:tm + 8, :] = jnp.where(last, 0.0, dyn)
        dz = w2 * dyc + w1 * dybuf[1:1 + tm, :] + w0 * dybuf[2:2 + tm, :]
        srow = lax.broadcasted_iota(jnp.int32, (8, cwid), 0)
        dcw_ref[...] += (jnp.where(srow == 0, _row_sum(dyc * z2), 0.0) + jnp.where(srow == 1, _row_sum(dyc * z1), 0.0)
                         + jnp.where(srow == 2, _row_sum(dyc * z), 0.0))
        base = ATTN_WIDTH + 2 * KV_WIDTH
        dproj_ref[:, base:base + cwid] = (dz * cg).astype(BF16)
        dproj_ref[:, base + cwid:base + 2 * cwid] = (dco * y).astype(BF16)
        dproj_ref[:, base + 2 * cwid:base + 3 * cwid] = (dz * u).astype(BF16)

    cur = lambda w: pl.BlockSpec((tm, w), lambda i: (i, 0))
    nxt = lambda w: pl.BlockSpec((tm, w), lambda i: (jnp.minimum(i + 1, ntile - 1), 0))
    return _pcall(
        body, name=name, grid=(ntile,),
        in_specs=[cur(ATTN_WIDTH), cur(KV_WIDTH), nxt(KV_WIDTH), cur(KV_WIDTH), nxt(KV_WIDTH),
                  cur(LANE), cur(LANE), cur(LANE),
                  pl.BlockSpec((tm, cwid), lambda i: (i, 1)),
                  pl.BlockSpec((16, cwid), lambda i: (jnp.minimum((i + 1) * (tm // 16), t // 16 - 1), 1)),
                  cur(3 * cwid),
                  pl.BlockSpec((8, 3 * cwid), lambda i: (jnp.maximum(i * (tm // 8) - 1, 0), 0)),
                  pl.BlockSpec((8, 3 * cwid), lambda i: (jnp.minimum((i + 1) * (tm // 8), t // 8 - 1), 0)),
                  _full((8, cwid))],
        out_specs=[cur(IN_WIDTH), _full((8, cwid))],
        out_shape=[jax.ShapeDtypeStruct((t, IN_WIDTH), BF16), jax.ShapeDtypeStruct((8, cwid), F32)],
        scratch_shapes=[pltpu.VMEM((tm + 8, cwid), F32), pltpu.VMEM((tm + 8, cwid), F32)],
        args=(dq, dkc, dkp, dvc, dvp, cos, sa, sb, dmixin, dmixin, ubc, ubc, ubc, cw), comm=comm)


def _ada_fwd(c_all, w_ada, b_ada_shard, *, name):
    nb, d = c_all.shape
    n = w_ada.shape[1]
    tn = n // 2

    def body(c_ref, w_ref, b_ref, o_ref):
        cv = c_ref[...]
        cond = cv * _sigmoid(cv)
        o_ref[...] = jnp.dot(cond, w_ref[...], preferred_element_type=F32,
                             precision=lax.Precision.HIGHEST) + b_ref[...]

    return pl.pallas_call(
        body, name=name, grid=(n // tn,),
        in_specs=[_full((nb, d)), pl.BlockSpec((d, tn), lambda j: (0, j)), pl.BlockSpec((1, tn), lambda j: (0, j))],
        out_specs=pl.BlockSpec((nb, tn), lambda j: (0, j)),
        out_shape=jax.ShapeDtypeStruct((nb, n), F32),
        compiler_params=_params(("arbitrary",)),
    )(c_all, w_ada, b_ada_shard)


def _small_finish(gathered, dmod_all, dmod_shard, c_all_t, *, name):
    d = D_MODEL
    nb, n = dmod_shard.shape

    def body(g_ref, dm_ref, dms_ref, ct_ref, sum_ref, gw_ref, gb_ref):
        total = g_ref[0]
        for dev in range(1, N_DEV):
            total = total + g_ref[dev]
        sum_ref[...] = total
        gb_ref[...] = _row_sum(dm_ref[...])
        ctv = ct_ref[...]
        cond_t = ctv * _sigmoid(ctv)
        for jb in range(n // COL_CHUNK):
            gw_ref[:, jb * COL_CHUNK:(jb + 1) * COL_CHUNK] = jnp.dot(
                cond_t, dms_ref[:, jb * COL_CHUNK:(jb + 1) * COL_CHUNK], preferred_element_type=F32,
                precision=lax.Precision.HIGHEST)

    return pl.pallas_call(
        body, name=name, grid=(1,),
        in_specs=[_full((N_DEV, SMALL_ROWS, d)), _full((nb, N_MOD * d)), _full((nb, n)), _full((d, nb))],
        out_specs=[_full((SMALL_ROWS, d)), _full((d, n)), _full((1, N_MOD * d))],
        out_shape=[jax.ShapeDtypeStruct((SMALL_ROWS, d), F32), jax.ShapeDtypeStruct((d, n), F32),
                   jax.ShapeDtypeStruct((1, N_MOD * d), F32)],
        compiler_params=_params(("arbitrary",)),
    )(gathered, dmod_all, dmod_shard, c_all_t)


def _row_tile(r, c, budget=1 << 20):
    if r * c * 4 <= budget or r % 16:
        return r
    best = 16
    for tr in range(16, r + 1, 16):
        if r % tr == 0 and tr * c * 4 <= budget:
            best = tr
    return best


def _cast_into(w, chip, col_kind, *, name):
    r, c = w.shape
    tr = _row_tile(r, c)

    def body(chip_ref, w_ref, o_ref):
        o_ref[...] = w_ref[...].astype(BF16)

    if col_kind:
        out_spec = pl.BlockSpec((tr, c), lambda i, chip_ref: (i, chip_ref[0]))
        out_shape = jax.ShapeDtypeStruct((r, c * N_CHIPS), BF16)
    else:
        out_spec = pl.BlockSpec((tr, c), lambda i, chip_ref: (chip_ref[0] * (r // tr) + i, 0))
        out_shape = jax.ShapeDtypeStruct((r * N_CHIPS, c), BF16)
    return _pcall(body, name=name, grid=(r // tr,), in_specs=[pl.BlockSpec((tr, c), lambda i, chip_ref: (i, 0))],
                  out_specs=out_spec, out_shape=out_shape, args=(w,), prefetch=chip)


def _adamw(w, g, m, v, *, name, comm=None):
    r, c = w.shape
    tr = _row_tile(r, c)
    c1 = 1.0 - ADAM_B1 ** ADAM_STEP
    c2 = 1.0 - ADAM_B2 ** ADAM_STEP

    def body(w_ref, g_ref, m_ref, v_ref, d_ref, nm_ref, nv_ref):
        gv = g_ref[...]
        m2 = ADAM_B1 * m_ref[...] + (1.0 - ADAM_B1) * gv
        v2 = ADAM_B2 * v_ref[...] + (1.0 - ADAM_B2) * (gv * gv)
        d_ref[...] = -ADAM_LR * ((m2 / c1) / (jnp.sqrt(v2 / c2) + ADAM_EPS) + ADAM_WD * w_ref[...])
        nm_ref[...] = m2
        nv_ref[...] = v2

    spec = pl.BlockSpec((tr, c), lambda i: (i, 0))
    sh = jax.ShapeDtypeStruct((r, c), F32)
    return _pcall(body, name=name, grid=(r // tr,), in_specs=[spec] * 4, out_specs=[spec] * 3, out_shape=[sh] * 3,
                  args=(w, g, m, v), comm=comm)


def _sum_pair(pos, g3, r3, blk_of, *, name, comm=None):
    n, rows, cols = r3.shape
    tr = _row_tile(rows, cols)

    def body(pos_ref, g_ref, r_ref, s32_ref, s16_ref):
        s = g_ref[0] + r_ref[0]
        s32_ref[0] = s
        s16_ref[0] = s.astype(BF16)

    own = pl.BlockSpec((1, tr, cols), lambda p, i, pos: (blk_of(p, pos), i, 0))
    plain = pl.BlockSpec((1, tr, cols), lambda p, i, pos: (p, i, 0))
    return _pcall(
        body, name=name, grid=(n, rows // tr), in_specs=[own, plain], out_specs=[plain, plain],
        out_shape=[jax.ShapeDtypeStruct((n, rows, cols), F32), jax.ShapeDtypeStruct((n, rows, cols), BF16)],
        args=(g3, r3), prefetch=pos, comm=comm)


def _sum_final(pos, s32, recv, *, col_kind, n_shard, name):
    if col_kind:
        rows, cols = s32.shape[1], n_shard
        own = lambda tr: pl.BlockSpec((1, tr, cols), lambda i, pos: (0, i, 2 * pos[0] + pos[1]))
    else:
        rows, cols = s32.shape[1], s32.shape[2]
        own = lambda tr: pl.BlockSpec((1, tr, cols), lambda i, pos: (2 * pos[0] + pos[1], i, 0))
    tr = _row_tile(rows, cols)

    def body(pos_ref, s_ref, r_ref, o_ref):
        o_ref[0] = ((s_ref[0] + r_ref[0].astype(F32)) + r_ref[1].astype(F32)) + r_ref[2].astype(F32)

    grid_spec = pltpu.PrefetchScalarGridSpec(
        num_scalar_prefetch=1, grid=(rows // tr,),
        in_specs=[own(tr), pl.BlockSpec((3, tr, cols), lambda i, pos: (0, i, 0))],
        out_specs=pl.BlockSpec((1, tr, cols), lambda i, pos: (pos[2], i, 0)))
    return pl.pallas_call(
        body, name=name, grid_spec=grid_spec, out_shape=jax.ShapeDtypeStruct((2, rows, cols), F32),
        compiler_params=_params(("arbitrary",)),
    )(pos, s32, recv)


def _position():
    return lax.axis_index("x"), lax.axis_index("y"), lax.axis_index("c")


def _allgather8(x_shard, *, name, comm=None):
    m_per, n = x_shard.shape
    nci, nco = (0, 0) if comm is None else (len(comm.inputs), len(comm.out_shapes))

    def body(*refs):
        x_ref, refs = refs[0], refs[1:]
        cin, refs = refs[:nci], refs[nci:]
        out_ref, refs = refs[0], refs[1:]
        cout, refs = refs[:nco], refs[nco:]
        (send_sems, recv_sems, local_sem), csems = refs[:3], refs[3:]
        x, y, c = _position()
        me, sibling = (x, y, c), (x, y, 1 - c)
        chips = [(1 - x, y), (x, 1 - y), (1 - x, 1 - y)]

        def rows(px, py, pc):
            return out_ref.at[pl.ds((4 * px + 2 * py + pc) * m_per, m_per), :]

        def copy(k, block, to, src=None):
            return pltpu.make_async_remote_copy(
                src_ref=rows(*block) if src is None else src, dst_ref=rows(*block),
                send_sem=send_sems.at[k], recv_sem=recv_sems.at[k], device_id=to, device_id_type=MESH)

        mine = pltpu.make_async_copy(x_ref, rows(*me), local_sem)
        mine.start()
        first = [copy(0, me, sibling, src=x_ref)]
        first += [copy(1 + j, me, (*chip, c), src=x_ref) for j, chip in enumerate(chips)]
        for cp in first:
            cp.start()
        if comm is not None:
            comm.start(cin, cout, csems)
        passed = [copy(4 + j, (*chip, c), sibling) for j, chip in enumerate(chips)]
        for j, chip in enumerate(chips):
            copy(1 + j, (*chip, c), me).wait_recv()
            passed[j].start()
        copy(0, sibling, me).wait_recv()
        for j, chip in enumerate(chips):
            copy(4 + j, (*chip, 1 - c), me).wait_recv()
        for cp in first + passed:
            cp.wait_send()
        mine.wait()
        if comm is not None:
            comm.finish(cin, cout, csems)

    vmem = pl.BlockSpec(memory_space=pltpu.VMEM)
    sems = [pltpu.SemaphoreType.DMA((7,)), pltpu.SemaphoreType.DMA((7,)), pltpu.SemaphoreType.DMA]
    out = jax.ShapeDtypeStruct((N_DEV * m_per, n), x_shard.dtype)
    if comm is None:
        return pl.pallas_call(body, name=name, out_shape=out, in_specs=[vmem], out_specs=vmem,
                              scratch_shapes=sems)(x_shard)
    res = pl.pallas_call(
        body, name=name, out_shape=[out] + list(comm.out_shapes), in_specs=[vmem] + [ANY_SPEC] * nci,
        out_specs=[vmem] + [ANY_SPEC] * nco, scratch_shapes=sems + list(comm.sems),
        input_output_aliases={1 + i: 1 + o for i, o in comm.aliases.items()})(x_shard, *comm.inputs)
    return res[0], list(res[1:])


def _peer_chips(x, y):
    return [(1 - x, y), (x, 1 - y), (1 - x, 1 - y)]


class _GatherJob:
    def __init__(self, pieces):
        self.pieces = pieces
        n_p = len(pieces)
        self.inputs = [p[0] for p in pieces]
        self.out_shapes = [jax.ShapeDtypeStruct(p[0].shape, p[0].dtype) for p in pieces]
        for buf, col_kind, r0, nr in pieces:
            half_rows = buf.shape[0] // (2 if col_kind else 2 * N_CHIPS)
            assert r0 % 16 == 0 and nr % 16 == 0 and r0 + nr <= half_rows, (buf.shape, r0, nr)
        self.aliases = {p: p for p in range(n_p)}
        self.sems = [pltpu.SemaphoreType.DMA((3 * n_p,))] * 4

    def _region(self, cout, p, chip_idx, half):
        buf, col_kind, r0, nr = self.pieces[p]
        if col_kind:
            n = buf.shape[1] // N_CHIPS
            return cout[p].at[pl.ds(half * (buf.shape[0] // 2) + r0, nr), pl.ds(chip_idx * n, n)]
        n = buf.shape[0] // N_CHIPS
        return cout[p].at[pl.ds(chip_idx * n + half * (n // 2) + r0, nr), :]

    def _copies(self, cout, sems):
        send_sems, recv_sems, fsend_sems, frecv_sems = sems
        x, y, c = _position()
        k = 2 * x + y
        sibling = (x, y, 1 - c)
        sends, arrivals, fwds, fwd_arrivals = [], [], [], []

        def remote(region, ssem, rsem, to):
            return pltpu.make_async_remote_copy(src_ref=region, dst_ref=region, send_sem=ssem, recv_sem=rsem,
                                                device_id=to, device_id_type=MESH)

        for p in range(len(self.pieces)):
            for j, chip in enumerate(_peer_chips(x, y)):
                idx = 3 * p + j
                theirs = 2 * chip[0] + chip[1]
                sends.append(remote(self._region(cout, p, k, c), send_sems.at[idx], recv_sems.at[idx], (*chip, c)))
                arrivals.append(remote(self._region(cout, p, theirs, c), send_sems.at[idx], recv_sems.at[idx],
                                       (*chip, c)))
                fwds.append(remote(self._region(cout, p, theirs, c), fsend_sems.at[idx], frecv_sems.at[idx], sibling))
                fwd_arrivals.append(remote(self._region(cout, p, theirs, 1 - c), fsend_sems.at[idx],
                                           frecv_sems.at[idx], sibling))
        return sends, arrivals, fwds, fwd_arrivals

    def start(self, cin, cout, sems):
        for cp in self._copies(cout, sems)[0]:
            cp.start()

    def finish(self, cin, cout, sems):
        sends, arrivals, fwds, fwd_arrivals = self._copies(cout, sems)
        for arrived, fw in zip(arrivals, fwds):
            arrived.wait_recv()
            fw.start()
        for arrived in fwd_arrivals:
            arrived.wait_recv()
        for cp in sends + fwds:
            cp.wait_send()


class _PairedJob:
    aliases = {}

    def start(self, cin, cout, sems):
        for cp in self._copies(cin, cout, sems):
            cp.start()

    def finish(self, cin, cout, sems):
        copies = self._copies(cin, cout, sems)
        for cp in copies:
            cp.wait_recv()
        for cp in copies:
            cp.wait_send()


class _SwapJob(_PairedJob):
    def __init__(self, grads, kinds):
        self.inputs, self.kinds = list(grads), list(kinds)
        self.out_shapes, self.n_copies = [], []
        for g, kd in zip(grads, kinds):
            if kd:
                self.out_shapes.append(jax.ShapeDtypeStruct((1, g.shape[0] // 2, g.shape[1]), g.dtype))
                self.n_copies.append(1)
            else:
                n = g.shape[0] // N_CHIPS
                self.out_shapes.append(jax.ShapeDtypeStruct((N_CHIPS, n // 2, g.shape[1]), g.dtype))
                self.n_copies.append(N_CHIPS)
        total = sum(self.n_copies)
        self.sems = [pltpu.SemaphoreType.DMA((total,)), pltpu.SemaphoreType.DMA((total,))]

    def _copies(self, cin, cout, sems):
        send_sems, recv_sems = sems
        x, y, c = _position()
        copies = []
        for p, src_ref in enumerate(cin):
            for kk in range(self.n_copies[p]):
                if self.kinds[p]:
                    hr = src_ref.shape[0] // 2
                    src = src_ref.at[pl.ds((1 - c) * hr, hr), :]
                else:
                    n = src_ref.shape[0] // N_CHIPS
                    src = src_ref.at[pl.ds(kk * n + (1 - c) * (n // 2), n // 2), :]
                idx = len(copies)
                copies.append(pltpu.make_async_remote_copy(
                    src_ref=src, dst_ref=cout[p].at[kk], send_sem=send_sems.at[idx], recv_sem=recv_sems.at[idx],
                    device_id=(x, y, 1 - c), device_id_type=MESH))
        return copies


class _ExchangeJob(_PairedJob):
    def __init__(self, s16, kinds, sizes):
        self.inputs, self.kinds, self.sizes = list(s16), list(kinds), list(sizes)
        self.out_shapes = [jax.ShapeDtypeStruct((3, s.shape[1], n if kd else s.shape[2]), s.dtype)
                           for s, kd, n in zip(s16, kinds, sizes)]
        self.sems = [pltpu.SemaphoreType.DMA((3 * len(s16),)), pltpu.SemaphoreType.DMA((3 * len(s16),))]

    def _copies(self, cin, cout, sems):
        send_sems, recv_sems = sems
        x, y, c = _position()
        copies = []
        for p, src_ref in enumerate(cin):
            for j, chip in enumerate(_peer_chips(x, y)):
                kk = 2 * chip[0] + chip[1]
                n = self.sizes[p]
                src = src_ref.at[0, :, pl.ds(kk * n, n)] if self.kinds[p] else src_ref.at[kk]
                copies.append(pltpu.make_async_remote_copy(
                    src_ref=src, dst_ref=cout[p].at[j], send_sem=send_sems.at[3 * p + j],
                    recv_sem=recv_sems.at[3 * p + j], device_id=(*chip, c), device_id_type=MESH))
        return copies


class _ShareJob:
    def __init__(self, halves):
        self.inputs = list(halves)
        self.out_shapes = [jax.ShapeDtypeStruct(h.shape, h.dtype) for h in halves]
        self.aliases = {p: p for p in range(len(halves))}
        self.sems = [pltpu.SemaphoreType.DMA((len(halves),)), pltpu.SemaphoreType.DMA((len(halves),))]

    def _copies(self, cout, sems, half):
        send_sems, recv_sems = sems
        x, y, c = _position()
        h = c if half == "mine" else 1 - c
        return [pltpu.make_async_remote_copy(
            src_ref=o.at[h], dst_ref=o.at[h], send_sem=send_sems.at[p], recv_sem=recv_sems.at[p],
            device_id=(x, y, 1 - c), device_id_type=MESH) for p, o in enumerate(cout)]

    def start(self, cin, cout, sems):
        for cp in self._copies(cout, sems, "mine"):
            cp.start()

    def finish(self, cin, cout, sems):
        for cp in self._copies(cout, sems, "theirs"):
            cp.wait_recv()
        for cp in self._copies(cout, sems, "mine"):
            cp.wait_send()


class _MultiJob:
    def __init__(self, jobs):
        self.jobs = jobs
        self.inputs = [a for j in jobs for a in j.inputs]
        self.out_shapes = [s for j in jobs for s in j.out_shapes]
        self.sems = [s for j in jobs for s in j.sems]
        self.aliases = {}
        i0 = o0 = 0
        for j in jobs:
            for i, o in j.aliases.items():
                self.aliases[i0 + i] = o0 + o
            i0 += len(j.inputs)
            o0 += len(j.out_shapes)

    def _parts(self, cin, cout, sems):
        i0 = o0 = s0 = 0
        for j in self.jobs:
            ni, no, ns = len(j.inputs), len(j.out_shapes), len(j.sems)
            yield j, cin[i0:i0 + ni], cout[o0:o0 + no], sems[s0:s0 + ns]
            i0, o0, s0 = i0 + ni, o0 + no, s0 + ns

    def start(self, cin, cout, sems):
        for j, a, b, s in self._parts(cin, cout, sems):
            j.start(a, b, s)

    def finish(self, cin, cout, sems):
        for j, a, b, s in self._parts(cin, cout, sems):
            j.finish(a, b, s)


def _rope_tables(positions):
    half = ROT_DIM // 2
    inv_freq = jnp.power(jnp.float32(ROPE_THETA), -jnp.arange(0, ROT_DIM, 2, dtype=F32) / ROT_DIM)
    inv_head = jnp.concatenate([inv_freq, inv_freq, jnp.zeros((HEAD_DIM - ROT_DIM,), F32)])
    inv_lane = jnp.concatenate([inv_head] * (LANE // HEAD_DIM))
    ang = positions.astype(F32).reshape(-1)[:, None] * inv_lane[None, :]
    sin = jnp.sin(ang)
    dim = jnp.arange(LANE) % HEAD_DIM
    return jnp.cos(ang), jnp.where(dim < half, -sin, 0.0), jnp.where(dim >= half, sin, 0.0)


def kernel(x, c, positions, w_ada, b_ada, ffn1_w_gate_up, ffn1_w_down, ln1_g, ln1_b, w_in, conv_w, attn_sinks, w_out, ln2_g, ln2_b, ffn2_w_gate_up, ffn2_w_down, ln3_g, ln3_b, loss_target, m_w_ada, m_b_ada, m_ffn1_w_gate_up, m_ffn1_w_down, m_ln1_g, m_ln1_b, m_w_in, m_conv_w, m_attn_sinks, m_w_out, m_ln2_g, m_ln2_b, m_ffn2_w_gate_up, m_ffn2_w_down, m_ln3_g, m_ln3_b, v_w_ada, v_b_ada, v_ffn1_w_gate_up, v_ffn1_w_down, v_ln1_g, v_ln1_b, v_w_in, v_conv_w, v_attn_sinks, v_w_out, v_ln2_g, v_ln2_b, v_ffn2_w_gate_up, v_ffn2_w_down, v_ln3_g, v_ln3_b):
    d = D_MODEL
    nb, seq, _ = x.shape
    t = nb * seq
    f = ffn1_w_down.shape[1] * N_CHIPS
    ax, ay, ac = _position()
    chip = 2 * ax + ay
    dev = 2 * chip + ac
    pos = jnp.stack([ax, ay, ac]).astype(jnp.int32)

    x2 = x.reshape(t, d)
    tgt2 = loss_target.reshape(t, d)
    ln1 = jnp.concatenate([ln1_g, ln1_b], axis=0)
    ln2 = jnp.concatenate([ln2_g, ln2_b], axis=0)
    ln3 = jnp.concatenate([ln3_g, ln3_b], axis=0)
    sinks = attn_sinks.reshape(N_Q_HEADS)
    cos_t, sa_t, sb_t = _rope_tables(positions)

    n_ada = w_ada.shape[2]
    c_all = _allgather8(c.reshape(nb * d // LANE, LANE), name="gather_c").reshape(N_DEV * nb, d)
    b_shard = lax.dynamic_slice(b_ada, (0, chip * n_ada), (1, n_ada))
    mod_part = _ada_fwd(c_all, w_ada[0], b_shard, name="ada_fwd")
    conv_rows = jnp.pad(conv_w[0], ((0, 5), (0, n_ada - conv_w.shape[2])))
    part = jnp.concatenate([mod_part, conv_rows], axis=0)
    parts = _allgather8(part, name="gather_mod").reshape(N_DEV, N_DEV * nb + 8, n_ada)
    mod_all = jnp.concatenate([parts[2 * k, :N_DEV * nb, :] for k in range(N_CHIPS)], axis=1)
    mod = lax.dynamic_slice(mod_all, (dev * nb, 0), (nb, N_MOD * d)).reshape(nb, N_MOD, d)
    cw_full = jnp.concatenate([parts[2 * k, N_DEV * nb:, :conv_w.shape[2]] for k in range(N_CHIPS)], axis=1)

    chip_arr = jnp.reshape(chip, (1,)).astype(jnp.int32)
    b_gu1 = _cast_into(ffn1_w_gate_up[0], chip_arr, True, name="cast_gu1")
    b_d1 = _cast_into(ffn1_w_down[0], chip_arr, False, name="cast_d1")
    b_in = _cast_into(w_in[0].T, chip_arr, False, name="cast_in")
    b_out = _cast_into(w_out[0], chip_arr, False, name="cast_out")
    b_gu2 = _cast_into(ffn2_w_gate_up[0], chip_arr, True, name="cast_gu2")
    b_d2 = _cast_into(ffn2_w_down[0], chip_arr, False, name="cast_d2")
    n_gu, n_d, n_in, n_out = (ffn1_w_gate_up.shape[2], ffn1_w_down.shape[1], w_in.shape[2], w_out.shape[1])

    def whole(buf, col_kind):
        return (buf, col_kind, 0, buf.shape[0] // (2 if col_kind else 2 * N_CHIPS))

    gu_cuts = [0, 176, 352, d // 2]
    gu_part = lambda buf, s: (buf, True, gu_cuts[s], gu_cuts[s + 1] - gu_cuts[s])

    (wgu1,) = _comm_call(_GatherJob([whole(b_gu1, True)]), name="gather_gu1")
    (h1, a1, gu1), (wd1, wout) = _ffn_up(x2, ln1, mod, wgu1, seq=seq, sc_idx=1, sh_idx=0, use_ln=False,
                                         name="ffn1_up", comm=_GatherJob([whole(b_d1, False), whole(b_out, False)]))
    (f1, xhat1, rstd1), (win_t,) = _ffn_down_ln(a1, wd1, x2, ln1, mod, seq=seq, gate_idx=2, use_ln=False,
                                                name="ffn1_down", comm=_GatherJob([whole(b_in, False)]))
    (h2, q, k, v, ubc), (b_gu2,) = _in_proj(
        xhat1, ln1, mod, win_t, cos_t, sa_t, sb_t, seq=seq, sc_idx=4, sh_idx=3, name="in_proj",
        comm=_GatherJob([gu_part(b_gu2, 0)]))
    attn, (b_gu2,) = _attention(q, k, v, sinks, seq=seq, name="attention", comm=_GatherJob([gu_part(b_gu2, 1)]))
    (mixin, mix, xhat2, rstd2), (wgu2,) = _out_proj(
        attn, ubc, cw_full, wout, xhat1, ln1, mod, seq=seq, gate_idx=5, name="out_proj",
        comm=_GatherJob([gu_part(b_gu2, 2)]))
    (h3, a3, gu3), (wd2,) = _ffn_up(xhat2, ln2, mod, wgu2, seq=seq, sc_idx=7, sh_idx=6, use_ln=True, name="ffn2_up",
                                    comm=_GatherJob([whole(b_d2, False)]))
    dr3, df3, loss_cols, dln3g, dln3b, dgate3 = _ffn_down_loss(
        a3, wd2, xhat2, ln2, mod, ln3, tgt2, seq=seq, gate_idx=8, name="ffn2_down_loss")

    def pair_sum(g, r3, col_kind, name_, comm=None):
        if col_kind:
            g3 = g.reshape(2, g.shape[0] // 2, g.shape[1])
            blk_of = lambda p_, pos_: pos_[2]
        else:
            g3 = g.reshape(2 * N_CHIPS, g.shape[0] // (2 * N_CHIPS), g.shape[1])
            blk_of = lambda p_, pos_: 2 * p_ + pos_[2]
        return _sum_pair(pos, g3, r3, blk_of, name=name_, comm=comm)

    dgu3 = _ffn_bwd_act(df3, wd2, gu3, seq=seq, name="ffn2_bwd_act")
    g_wd2 = _matmul_tn(a3, df3, tmm=f // 2, tnn=d, name="grad_wd2")
    g_wgu2, (sib_d2,) = _matmul_tn(h3, dgu3, tmm=d, tnn=(2 * f) // 4, name="grad_wgu2",
                                   comm=_SwapJob([g_wd2], [False]))
    s32_d2, s16_d2 = pair_sum(g_wd2, sib_d2, False, "sum_pair_d2")
    (dr2, dmix, dsc3, dsh3, dgate2, dln2g, dln2b), (sib_gu2, recv_d2) = _bwd_in(
        dgu3, wgu2, dr3, xhat2, rstd2, ln2, mod, mix, seq=seq, w_is_nt=True, sc_idx=7, gate_idx=5,
        branch_scale=1.0, final=False, name="ffn2_bwd_in",
        comm=_MultiJob([_SwapJob([g_wgu2], [True]), _ExchangeJob([s16_d2], [False], [n_d])]))
    s32_gu2, s16_gu2 = pair_sum(g_wgu2, sib_gu2, True, "sum_pair_gu2")
    g_wout = _matmul_tn(mixin, dmix, tmm=d, tnn=d, name="grad_wout")
    dmixin = _matmul_nt_bf16(dmix, wout, seq=seq, name="out_proj_bwd")
    (dq, dkp, dkc, dvp, dvc, dsink), (recv_gu2, sib_out) = _attention_bwd(
        q, k, v, dmixin, sinks, seq=seq, name="attention_bwd",
        comm=_MultiJob([_ExchangeJob([s16_gu2], [True], [n_gu]), _SwapJob([g_wout], [False])]))
    s32_out, s16_out = pair_sum(g_wout, sib_out, False, "sum_pair_out")
    (dproj, dcw), (recv_out,) = _mix_bwd_assemble(
        dq, dkp, dkc, dvp, dvc, cos_t, sa_t, sb_t, dmixin, ubc, cw_full, seq=seq, name="mix_bwd",
        comm=_ExchangeJob([s16_out], [False], [n_out]))
    g_win_t = _matmul_tn(dproj, h2, tmm=IN_WIDTH // 2, tnn=d, name="grad_win")
    (dr1, df1, dsc2, dsh2, dgate1, dln1g, dln1b), (sib_in,) = _bwd_in(
        dproj, win_t, dr2, xhat1, rstd1, ln1, mod, f1, seq=seq, w_is_nt=False, sc_idx=4, gate_idx=2,
        branch_scale=0.5, final=False, name="in_proj_bwd", comm=_SwapJob([g_win_t], [False]))
    s32_in, s16_in = pair_sum(g_win_t, sib_in, False, "sum_pair_in")
    dgu1, (recv_in,) = _ffn_bwd_act(df1, wd1, gu1, seq=seq, name="ffn1_bwd_act",
                                    comm=_ExchangeJob([s16_in], [False], [n_in]))
    g_wgu1 = _matmul_tn(h1, dgu1, tmm=d, tnn=(2 * f) // 4, name="grad_wgu1")
    g_wd1, (sib_gu1,) = _matmul_tn(a1, df1, tmm=f // 2, tnn=d, name="grad_wd1", comm=_SwapJob([g_wgu1], [True]))
    (s32_gu1, s16_gu1), (sib_d1,) = pair_sum(g_wgu1, sib_gu1, True, "sum_pair_gu1", comm=_SwapJob([g_wd1], [False]))
    s32_d1, s16_d1 = pair_sum(g_wd1, sib_d1, False, "sum_pair_d1")

    def final_half(s32_, recv_, col_kind, n_shard, name_):
        return _sum_final(pos, s32_, recv_, col_kind=col_kind, n_shard=n_shard, name=name_)

    early = [final_half(s32_gu2, recv_gu2, True, n_gu, "sum_final_gu2"),
             final_half(s32_d2, recv_d2, False, n_d, "sum_final_d2"),
             final_half(s32_out, recv_out, False, n_out, "sum_final_out"),
             final_half(s32_in, recv_in, False, n_in, "sum_final_in")]
    (grad_x, dsc1, dsh1), (recv_gu1, recv_d1, full_gu2, full_d2, full_out, full_in) = _bwd_in(
        dgu1, wgu1, dr1, x2, None, None, mod, None, seq=seq, w_is_nt=True, sc_idx=1, gate_idx=None,
        branch_scale=None, final=True, name="ffn1_bwd_in",
        comm=_MultiJob([_ExchangeJob([s16_gu1, s16_d1], [True, False], [n_gu, n_d]), _ShareJob(early)]))
    late = [final_half(s32_gu1, recv_gu1, True, n_gu, "sum_final_gu1"),
            final_half(s32_d1, recv_d1, False, n_d, "sum_final_d1")]

    dmod = jnp.concatenate([dsh1, dsc1, dgate1, dsh2, dsc2, dgate2, dsh3, dsc3, dgate3], axis=1)
    loss_row = jnp.sum(loss_cols, axis=1, keepdims=True) * (0.5 / d)
    lane_row = lambda a: jnp.pad(a, ((0, 0), (0, d - a.shape[1])))
    block = jnp.concatenate(
        [dmod.reshape(nb * N_MOD, d), dln1g, dln1b, dln2g, dln2b, dln3g, dln3b,
         lane_row(dcw[0:3, :]), lane_row(dsink[:, 0:1].reshape(1, N_Q_HEADS)), lane_row(loss_row)], axis=0)
    block = jnp.pad(block, ((0, SMALL_ROWS - block.shape[0]), (0, 0)))
    gathered, (full_gu1, full_d1) = _allgather8(block, name="gather_small", comm=_ShareJob(late))
    gathered = gathered.reshape(N_DEV, SMALL_ROWS, d)
    dmod_all = gathered[:, :nb * N_MOD, :].reshape(N_DEV * nb, N_MOD * d)
    dmod_shard = lax.dynamic_slice(dmod_all, (0, chip * n_ada), (N_DEV * nb, n_ada))
    small, g_w_ada, g_b_ada = _small_finish(gathered, dmod_all, dmod_shard, c_all.T, name="small_finish")
    r0 = nb * N_MOD
    loss = small[r0 + 10, 0]
    g_ln = [small[r0 + i:r0 + i + 1, :] for i in range(6)]
    g_cw_full = small[r0 + 6:r0 + 9, :CONV_WIDTH]
    g_conv = lax.dynamic_slice(g_cw_full, (0, chip * conv_w.shape[2]), (3, conv_w.shape[2]))
    g_sinks = small[r0 + 9:r0 + 10, :N_Q_HEADS]

    def flat2(a):
        return a.reshape(-1, a.shape[-1])

    def unhalve(a):
        return a.reshape(2 * a.shape[1], a.shape[2])

    results = {}

    def adamw(name_, w_, g_, m_, v_):
        g2 = flat2(g_)
        dl, nm, nv = _adamw(flat2(w_), g2, flat2(m_), flat2(v_), name="adamw_" + name_)
        results[name_] = tuple(a.reshape(w_.shape) for a in (g2, dl, nm, nv))

    adamw("w_ada", w_ada, g_w_ada, m_w_ada, v_w_ada)
    adamw("ffn2_w_gate_up", ffn2_w_gate_up, unhalve(full_gu2), m_ffn2_w_gate_up, v_ffn2_w_gate_up)
    adamw("ffn2_w_down", ffn2_w_down, unhalve(full_d2), m_ffn2_w_down, v_ffn2_w_down)
    adamw("w_out", w_out, unhalve(full_out), m_w_out, v_w_out)
    adamw("w_in", w_in, unhalve(full_in).T, m_w_in, v_w_in)
    adamw("ffn1_w_gate_up", ffn1_w_gate_up, unhalve(full_gu1), m_ffn1_w_gate_up, v_ffn1_w_gate_up)
    adamw("ffn1_w_down", ffn1_w_down, unhalve(full_d1), m_ffn1_w_down, v_ffn1_w_down)
    adamw("b_ada", b_ada, g_b_ada, m_b_ada, v_b_ada)
    adamw("ln1_g", ln1_g, g_ln[0], m_ln1_g, v_ln1_g)
    adamw("ln1_b", ln1_b, g_ln[1], m_ln1_b, v_ln1_b)
    adamw("ln2_g", ln2_g, g_ln[2], m_ln2_g, v_ln2_g)
    adamw("ln2_b", ln2_b, g_ln[3], m_ln2_b, v_ln2_b)
    adamw("ln3_g", ln3_g, g_ln[4], m_ln3_g, v_ln3_g)
    adamw("ln3_b", ln3_b, g_ln[5], m_ln3_b, v_ln3_b)
    adamw("conv_w", conv_w, g_conv, m_conv_w, v_conv_w)
    adamw("attn_sinks", attn_sinks, g_sinks, m_attn_sinks, v_attn_sinks)
    order = ["w_ada", "b_ada", "ffn1_w_gate_up", "ffn1_w_down", "ln1_g", "ln1_b", "w_in", "conv_w", "attn_sinks",
             "w_out", "ln2_g", "ln2_b", "ffn2_w_gate_up", "ffn2_w_down", "ln3_g", "ln3_b"]
    return (loss, grad_x.reshape(x.shape), *[results[n_][0] for n_ in order], *[results[n_][1] for n_ in order],
            *[results[n_][2] for n_ in order], *[results[n_][3] for n_ in order])
```

```python
import jax
import jax.numpy as jnp
from jax import lax
from jax.experimental import pallas as pl
from jax.experimental.pallas import tpu as pltpu

F32 = jnp.float32
BF16 = jnp.bfloat16
MESH = pl.DeviceIdType.MESH

D_MODEL = 1024
HEAD_DIM = 64
ATTN_WIDTH = 512
CONV_WIDTH = 512
N_Q_HEADS = 8
N_KV_HEADS = 2
GQA_GROUP = 4
KV_WIDTH = 128
WINDOW = 128
BLOCK = 128
ROT_DIM = 16
ROPE_THETA = 500000.0
N_MOD = 9
LN_EPS = 1e-5
DN_ALPHA = 2.0 ** 0.25
IN_WIDTH = 2304
N_CHIPS = 4
N_DEV = 8
SMALL_ROWS = 32

ADAM_LR = 0.001
ADAM_B1 = 0.9
ADAM_B2 = 0.999
ADAM_EPS = 1e-08
ADAM_WD = 0.01
ADAM_STEP = 10

LANE = 128
COL_CHUNK = 256
VMEM_LIMIT = 56 * 1024 * 1024


def _params(sem=None, vmem=True):
    return pltpu.CompilerParams(dimension_semantics=sem, vmem_limit_bytes=VMEM_LIMIT if vmem else None)


def _sigmoid(g):
    return 0.5 * jnp.tanh(0.5 * g) + 0.5


def _row_sum(v):
    return jnp.sum(v, axis=0, keepdims=True)


ROW_CHUNK = 16
EPILOGUE_UNROLL = 8


def _fold8(v):
    return v[0:8, :] + v[8:16, :]


def _row_chunk_loop(n_rows, step, init):
    per_iter = ROW_CHUNK * EPILOGUE_UNROLL
    assert n_rows % per_iter == 0, n_rows

    def body(it, carry):
        for s in range(EPILOGUE_UNROLL):
            start = pl.multiple_of(it * per_iter + s * ROW_CHUNK, ROW_CHUNK)
            carry = step(pl.ds(start, ROW_CHUNK), carry)
        return carry

    return lax.fori_loop(0, n_rows // per_iter, body, init)


def _ln_stats(r):
    mu = jnp.mean(r, axis=-1, keepdims=True)
    rc = r - mu
    var = jnp.mean(rc * rc, axis=-1, keepdims=True)
    rstd = lax.rsqrt(var + LN_EPS)
    return rc * rstd, rstd


def _ln_bwd(dxo, xhat, rstd, g):
    dxhat = dxo * g
    m1 = jnp.mean(dxhat, axis=-1, keepdims=True)
    m2 = jnp.mean(dxhat * xhat, axis=-1, keepdims=True)
    return rstd * (dxhat - m1 - xhat * m2)


def _dot_nt(a, b):
    return lax.dot_general(a, b, (((1,), (1,)), ((), ())), preferred_element_type=F32)


def _dot_tn(a, b):
    return lax.dot_general(a, b, (((0,), (0,)), ((), ())), preferred_element_type=F32)


def _full(shape):
    nd = len(shape)
    return pl.BlockSpec(shape, lambda *_: (0,) * nd)


def _resident(shape):
    nd = len(shape)
    return pl.BlockSpec(shape, lambda *_: (0,) * nd, pipeline_mode=pl.Buffered(1))


ANY_SPEC = pl.BlockSpec(memory_space=pl.ANY)


def _pcall(body, *, name, grid, in_specs, out_specs, out_shape, args, scratch_shapes=(), comm=None, prefetch=None):
    single = not isinstance(out_shape, (list, tuple))
    out_specs = [out_specs] if single else list(out_specs)
    out_shape = [out_shape] if single else list(out_shape)
    in_specs = list(in_specs)
    scratch_shapes = list(scratch_shapes)
    sem = ("arbitrary",) * len(grid)
    n_pre = 0 if prefetch is None else 1
    pre_args = () if prefetch is None else (prefetch,)

    def call(fn, ins_, outs_, shapes_, scratch_, aliases_, operands):
        if prefetch is None:
            return pl.pallas_call(fn, name=name, grid=grid, in_specs=ins_, out_specs=outs_, out_shape=shapes_,
                                  scratch_shapes=scratch_, input_output_aliases=aliases_,
                                  compiler_params=_params(sem))(*operands)
        spec = pltpu.PrefetchScalarGridSpec(num_scalar_prefetch=1, grid=grid, in_specs=ins_, out_specs=outs_,
                                            scratch_shapes=scratch_)
        return pl.pallas_call(fn, name=name, grid_spec=spec, out_shape=shapes_,
                              input_output_aliases={n_pre + i: o for i, o in aliases_.items()},
                              compiler_params=_params(sem))(*pre_args, *operands)

    if comm is None:
        res = call(body, in_specs, out_specs, out_shape, scratch_shapes, {}, args)
        return res[0] if single else res
    n_in, n_out, n_scr = len(in_specs), len(out_specs), len(scratch_shapes)
    nci, nco = len(comm.inputs), len(comm.out_shapes)
    n_steps = 1
    for g in grid:
        n_steps *= g
    staged = n_steps >= 4
    middle_step = n_steps - 1 - max(1, n_steps // 8)

    def wrapped(*refs):
        pre, refs = refs[:n_pre], refs[n_pre:]
        ins, refs = refs[:n_in], refs[n_in:]
        cin, refs = refs[:nci], refs[nci:]
        outs, refs = refs[:n_out], refs[n_out:]
        cout, refs = refs[:nco], refs[nco:]
        scr, csems = refs[:n_scr], refs[n_scr:]
        step = pl.program_id(0)
        for ax in range(1, len(grid)):
            step = step * grid[ax] + pl.program_id(ax)

        @pl.when(step == 0)
        def _():
            comm.start(cin, cout, csems)

        body(*pre, *ins, *outs, *scr)

        if staged:
            @pl.when(step == middle_step)
            def _():
                comm.middle(cin, cout, csems)

        @pl.when(step == n_steps - 1)
        def _():
            if not staged:
                comm.middle(cin, cout, csems)
            comm.finish(cin, cout, csems)

    res = call(wrapped, in_specs + [ANY_SPEC] * nci, out_specs + [ANY_SPEC] * nco,
               out_shape + list(comm.out_shapes), scratch_shapes + list(comm.sems),
               {n_in + i: n_out + o for i, o in comm.aliases.items()}, (*args, *comm.inputs))
    main = res[:n_out]
    return (main[0] if single else main), list(res[n_out:])


def _comm_call(job, *, name):
    nci, nco = len(job.inputs), len(job.out_shapes)

    def body(*refs):
        cin, refs = refs[:nci], refs[nci:]
        cout, csems = refs[:nco], refs[nco:]
        job.start(cin, cout, csems)
        job.middle(cin, cout, csems)
        job.finish(cin, cout, csems)

    return pl.pallas_call(
        body, name=name, out_shape=list(job.out_shapes), in_specs=[ANY_SPEC] * nci, out_specs=[ANY_SPEC] * nco,
        scratch_shapes=list(job.sems), input_output_aliases=dict(job.aliases))(*job.inputs)


def _ffn_up(xin, lnp, mod, w, *, seq, sc_idx, sh_idx, use_ln, name, comm=None):
    t, d = xin.shape
    f = w.shape[1] // 2
    tm = min(512, seq)
    tpb = seq // tm
    ch = min(COL_CHUNK, f)

    def body(x_ref, ln_ref, mod_ref, w_ref, h_ref, a_ref, gu_ref):
        x = x_ref[...]
        if use_ln:
            x = x * ln_ref[0:1, :] + ln_ref[1:2, :]
        h = x * (1.0 + mod_ref[0, sc_idx:sc_idx + 1, :]) + mod_ref[0, sh_idx:sh_idx + 1, :]
        hb = h.astype(BF16)
        h_ref[...] = hb
        for j in range(f // ch):
            g = jnp.dot(hb, w_ref[:, j * ch:(j + 1) * ch], preferred_element_type=F32)
            u = jnp.dot(hb, w_ref[:, f + j * ch:f + (j + 1) * ch], preferred_element_type=F32)
            a_ref[:, j * ch:(j + 1) * ch] = (g * _sigmoid(g) * u).astype(BF16)
            gu_ref[:, j * ch:(j + 1) * ch] = g.astype(BF16)
            gu_ref[:, f + j * ch:f + (j + 1) * ch] = u.astype(BF16)

    return _pcall(
        body, name=name, grid=(t // tm,),
        in_specs=[pl.BlockSpec((tm, d), lambda i: (i, 0)), _full((2, d)),
                  pl.BlockSpec((1, N_MOD, d), lambda i: (i // tpb, 0, 0)), _resident((d, 2 * f))],
        out_specs=[pl.BlockSpec((tm, d), lambda i: (i, 0)), pl.BlockSpec((tm, f), lambda i: (i, 0)),
                   pl.BlockSpec((tm, 2 * f), lambda i: (i, 0))],
        out_shape=[jax.ShapeDtypeStruct((t, d), BF16), jax.ShapeDtypeStruct((t, f), BF16),
                   jax.ShapeDtypeStruct((t, 2 * f), BF16)],
        args=(xin, lnp, mod, w), comm=comm)


def _ffn_down_ln(a, wd, xin, lnp_in, mod, *, seq, gate_idx, use_ln, name, comm=None):
    t, f = a.shape
    d = wd.shape[1]
    tm = min(512, seq)
    tpb = seq // tm

    def body(a_ref, wd_ref, x_ref, ln_ref, mod_ref, f_ref, xhat_ref, rstd_ref, acc):
        av = a_ref[...]
        for j in range(d // COL_CHUNK):
            acc[:, j * COL_CHUNK:(j + 1) * COL_CHUNK] = jnp.dot(
                av, wd_ref[:, j * COL_CHUNK:(j + 1) * COL_CHUNK], preferred_element_type=F32)
        scale = 0.5 * (1.0 + mod_ref[0, gate_idx:gate_idx + 1, :])

        fo = acc[...]
        x = x_ref[...]
        if use_ln:
            x = x * ln_ref[0:1, :] + ln_ref[1:2, :]
        xhat, rstd = _ln_stats(DN_ALPHA * x + scale * fo)
        f_ref[...] = fo.astype(BF16)
        xhat_ref[...] = xhat
        rstd_ref[...] = rstd

    return _pcall(
        body, name=name, grid=(t // tm,),
        in_specs=[pl.BlockSpec((tm, f), lambda i: (i, 0)), _resident((f, d)),
                  pl.BlockSpec((tm, d), lambda i: (i, 0)), _full((2, d)),
                  pl.BlockSpec((1, N_MOD, d), lambda i: (i // tpb, 0, 0))],
        out_specs=[pl.BlockSpec((tm, d), lambda i: (i, 0)), pl.BlockSpec((tm, d), lambda i: (i, 0)),
                   pl.BlockSpec((tm, 1), lambda i: (i, 0))],
        out_shape=[jax.ShapeDtypeStruct((t, d), BF16), jax.ShapeDtypeStruct((t, d), F32),
                   jax.ShapeDtypeStruct((t, 1), F32)],
        scratch_shapes=[pltpu.VMEM((tm, d), F32)],
        args=(a, wd, xin, lnp_in, mod), comm=comm)


def _ffn_down_loss(a, wd, xhat_in, lnp_in, mod, lnp_out, tgt, *, seq, gate_idx, name):
    t, f = a.shape
    d = wd.shape[1]
    nb = t // seq
    tm = min(512, seq)
    tpb = seq // tm

    def body(a_ref, wd_ref, x_ref, lnin_ref, mod_ref, lnout_ref, tgt_ref,
             dr_ref, df_ref, loss_ref, dg_ref, db_ref, dgate_ref, acc):
        i = pl.program_id(0)
        av = a_ref[...]
        for j in range(d // COL_CHUNK):
            acc[:, j * COL_CHUNK:(j + 1) * COL_CHUNK] = jnp.dot(
                av, wd_ref[:, j * COL_CHUNK:(j + 1) * COL_CHUNK], preferred_element_type=F32)
        scale = 0.5 * (1.0 + mod_ref[0, gate_idx:gate_idx + 1, :])
        g_in, b_in = lnin_ref[0:1, :], lnin_ref[1:2, :]
        g_out, b_out = lnout_ref[0:1, :], lnout_ref[1:2, :]

        def chunk(rows, carry):
            s_loss, s_dg, s_db, s_gate = carry
            fo = acc[rows, :]
            xhat, rstd = _ln_stats(DN_ALPHA * (x_ref[rows, :] * g_in + b_in) + scale * fo)
            e = xhat * g_out + b_out - tgt_ref[rows, :]
            dy = e * (1.0 / d)
            dr = _ln_bwd(dy, xhat, rstd, g_out)
            dr_ref[rows, :] = dr
            df_ref[rows, :] = (scale * dr).astype(BF16)
            return (s_loss + _fold8(e * e), s_dg + _fold8(dy * xhat), s_db + _fold8(dy),
                    s_gate + _fold8(0.5 * fo * dr))

        zero = jnp.zeros((8, d), F32)
        s_loss, s_dg, s_db, s_gate = _row_chunk_loop(tm, chunk, (zero, zero, zero, zero))

        @pl.when(i == 0)
        def _():
            loss_ref[...] = jnp.zeros_like(loss_ref)
            dg_ref[...] = jnp.zeros_like(dg_ref)
            db_ref[...] = jnp.zeros_like(db_ref)

        @pl.when(i % tpb == 0)
        def _():
            dgate_ref[...] = jnp.zeros_like(dgate_ref)

        loss_ref[...] += _row_sum(s_loss)
        dg_ref[...] += _row_sum(s_dg)
        db_ref[...] += _row_sum(s_db)
        dgate_ref[0] += _row_sum(s_gate)

    return pl.pallas_call(
        body, name=name, grid=(t // tm,), scratch_shapes=[pltpu.VMEM((tm, d), F32)],
        in_specs=[pl.BlockSpec((tm, f), lambda i: (i, 0)), _resident((f, d)),
                  pl.BlockSpec((tm, d), lambda i: (i, 0)), _full((2, d)),
                  pl.BlockSpec((1, N_MOD, d), lambda i: (i // tpb, 0, 0)), _full((2, d)),
                  pl.BlockSpec((tm, d), lambda i: (i, 0))],
        out_specs=[pl.BlockSpec((tm, d), lambda i: (i, 0)), pl.BlockSpec((tm, d), lambda i: (i, 0)),
                   _full((1, d)), _full((1, d)), _full((1, d)),
                   pl.BlockSpec((1, 1, d), lambda i: (i // tpb, 0, 0))],
        out_shape=[jax.ShapeDtypeStruct((t, d), F32), jax.ShapeDtypeStruct((t, d), BF16),
                   jax.ShapeDtypeStruct((1, d), F32), jax.ShapeDtypeStruct((1, d), F32),
                   jax.ShapeDtypeStruct((1, d), F32), jax.ShapeDtypeStruct((nb, 1, d), F32)],
        compiler_params=_params(("arbitrary",)),
    )(a, wd, xhat_in, lnp_in, mod, lnp_out, tgt)


def _rope(v, cos, sa, sb):
    return v * cos + pltpu.roll(v, LANE - ROT_DIM // 2, 1) * sa + pltpu.roll(v, ROT_DIM // 2, 1) * sb


def _rope_t(dy, cos, sa, sb):
    return dy * cos + pltpu.roll(dy * sa, ROT_DIM // 2, 1) + pltpu.roll(dy * sb, LANE - ROT_DIM // 2, 1)


def _in_proj(xhat, lnp, mod, w_t, cos, sa, sb, *, seq, sc_idx, sh_idx, name, comm=None):
    t, d = xhat.shape
    tm = min(512, seq)
    tpb = seq // tm
    n_conv = 3 * CONV_WIDTH

    def body(x_ref, ln_ref, mod_ref, w_ref, cos_ref, sa_ref, sb_ref, h_ref, q_ref, k_ref, v_ref, ubc_ref):
        x = x_ref[...] * ln_ref[0:1, :] + ln_ref[1:2, :]
        h = x * (1.0 + mod_ref[0, sc_idx:sc_idx + 1, :]) + mod_ref[0, sh_idx:sh_idx + 1, :]
        hb = h.astype(BF16)
        h_ref[...] = hb
        cos_t, sa_t, sb_t = cos_ref[...], sa_ref[...], sb_ref[...]
        for j in range(ATTN_WIDTH // COL_CHUNK):
            p = _dot_nt(hb, w_ref[j * COL_CHUNK:(j + 1) * COL_CHUNK, :])
            for s in range(COL_CHUNK // LANE):
                q_ref[:, j * COL_CHUNK + s * LANE:j * COL_CHUNK + (s + 1) * LANE] = _rope(
                    p[:, s * LANE:(s + 1) * LANE], cos_t, sa_t, sb_t).astype(BF16)
        p = _dot_nt(hb, w_ref[ATTN_WIDTH:ATTN_WIDTH + 2 * KV_WIDTH, :])
        k_ref[...] = _rope(p[:, 0:KV_WIDTH], cos_t, sa_t, sb_t).astype(BF16)
        v_ref[...] = p[:, KV_WIDTH:].astype(BF16)
        base = ATTN_WIDTH + 2 * KV_WIDTH
        for j in range(n_conv // COL_CHUNK):
            ubc_ref[:, j * COL_CHUNK:(j + 1) * COL_CHUNK] = _dot_nt(
                hb, w_ref[base + j * COL_CHUNK:base + (j + 1) * COL_CHUNK, :])

    row = lambda w: pl.BlockSpec((tm, w), lambda i: (i, 0))
    return _pcall(
        body, name=name, grid=(t // tm,),
        in_specs=[row(d), _full((2, d)), pl.BlockSpec((1, N_MOD, d), lambda i: (i // tpb, 0, 0)),
                  _resident((IN_WIDTH, d)), row(LANE), row(LANE), row(LANE)],
        out_specs=[row(d), row(ATTN_WIDTH), row(KV_WIDTH), row(KV_WIDTH), row(n_conv)],
        out_shape=[jax.ShapeDtypeStruct((t, d), BF16), jax.ShapeDtypeStruct((t, ATTN_WIDTH), BF16),
                   jax.ShapeDtypeStruct((t, KV_WIDTH), BF16), jax.ShapeDtypeStruct((t, KV_WIDTH), BF16),
                   jax.ShapeDtypeStruct((t, n_conv), F32)],
        args=(xhat, lnp, mod, w_t, cos, sa, sb), comm=comm)


def _attn_group(q_ref, kp_ref, kc_ref, vp_ref, vc_ref, sink_ref, g, first):
    lo, hi = g * HEAD_DIM, (g + 1) * HEAD_DIM
    kk = jnp.concatenate([kp_ref[:, lo:hi], kc_ref[:, lo:hi]], axis=0)
    vv = jnp.concatenate([vp_ref[:, lo:hi], vc_ref[:, lo:hi]], axis=0)
    qs = jnp.concatenate([q_ref[:, (GQA_GROUP * g + j) * HEAD_DIM:(GQA_GROUP * g + j + 1) * HEAD_DIM]
                          for j in range(GQA_GROUP)], axis=0)
    rows = GQA_GROUP * BLOCK
    row = lax.broadcasted_iota(jnp.int32, (rows, 2 * BLOCK), 0)
    ki = lax.broadcasted_iota(jnp.int32, (rows, 2 * BLOCK), 1)
    diff = (row & (BLOCK - 1)) + BLOCK - ki
    valid = (diff >= 0) & (diff < WINDOW) & ((ki >= BLOCK) | jnp.logical_not(first))
    s = _dot_nt(qs, kk) * (HEAD_DIM ** -0.5)
    s = jnp.where(valid, s, -1e30)
    rcol = lax.broadcasted_iota(jnp.int32, (rows, 1), 0)
    sink = jnp.zeros((rows, 1), F32)
    for j in range(GQA_GROUP):
        sink = jnp.where(rcol // BLOCK == j, sink_ref[GQA_GROUP * g + j], sink)
    m = jnp.maximum(jnp.max(s, axis=1, keepdims=True), sink)
    p = jnp.exp(s - m)
    ps = jnp.exp(sink - m)
    inv = 1.0 / (jnp.sum(p, axis=1, keepdims=True) + ps)
    return qs, kk, vv, p * inv, ps * inv


def _attention(q, k, v, sinks, *, seq, name, comm=None):
    t = q.shape[0]
    nblk = seq // BLOCK

    def body(q_ref, kp_ref, kc_ref, vp_ref, vc_ref, sink_ref, o_ref):
        first = (pl.program_id(0) % nblk) == 0
        outs = []
        for g in range(N_KV_HEADS):
            _, _, vv, pn, _ = _attn_group(q_ref, kp_ref, kc_ref, vp_ref, vc_ref, sink_ref, g, first)
            o = jnp.dot(pn.astype(BF16), vv, preferred_element_type=F32)
            outs += [o[j * BLOCK:(j + 1) * BLOCK, :] for j in range(GQA_GROUP)]
        o_ref[...] = jnp.concatenate(outs, axis=1).astype(BF16)

    cur = lambda w: pl.BlockSpec((BLOCK, w), lambda n: (n, 0))
    prev = lambda w: pl.BlockSpec((BLOCK, w), lambda n: (jnp.maximum(n - 1, 0), 0))
    return _pcall(
        body, name=name, grid=(t // BLOCK,),
        in_specs=[cur(ATTN_WIDTH), prev(KV_WIDTH), cur(KV_WIDTH), prev(KV_WIDTH), cur(KV_WIDTH),
                  pl.BlockSpec(memory_space=pltpu.SMEM)],
        out_specs=cur(ATTN_WIDTH),
        out_shape=jax.ShapeDtypeStruct((t, ATTN_WIDTH), BF16),
        args=(q, k, k, v, v, sinks), comm=comm)


def _out_proj(attn, ubc, cw, wout, xhat_in, lnp_in, mod, *, seq, gate_idx, name, comm=None):
    t, d = xhat_in.shape
    tm = min(512, seq)
    tpb = seq // tm
    cwid = CONV_WIDTH

    def body(attn_ref, ubc_ref, halo_ref, cw_ref, w_ref, x_ref, ln_ref, mod_ref,
             mixin_ref, mix_ref, xhat_ref, rstd_ref, zbuf, acc):
        first = (pl.program_id(0) % tpb) == 0
        u, bg, cg = ubc_ref[:, 0:cwid], ubc_ref[:, cwid:2 * cwid], ubc_ref[:, 2 * cwid:3 * cwid]
        z = cg * u
        hz = halo_ref[:, 2 * cwid:3 * cwid] * halo_ref[:, 0:cwid]
        zbuf[0:8, :] = jnp.where(first, 0.0, hz)
        zbuf[8:8 + tm, :] = z
        y = cw_ref[0:1, :] * zbuf[6:6 + tm, :] + cw_ref[1:2, :] * zbuf[7:7 + tm, :] + cw_ref[2:3, :] * z
        mixin_ref[:, 0:ATTN_WIDTH] = attn_ref[...]
        mixin_ref[:, ATTN_WIDTH:] = (bg * y).astype(BF16)
        mv = mixin_ref[...]
        for j in range(d // COL_CHUNK):
            acc[:, j * COL_CHUNK:(j + 1) * COL_CHUNK] = jnp.dot(
                mv, w_ref[:, j * COL_CHUNK:(j + 1) * COL_CHUNK], preferred_element_type=F32)
        scale = 1.0 + mod_ref[0, gate_idx:gate_idx + 1, :]

        mix = acc[...]
        xhat, rstd = _ln_stats(DN_ALPHA * (x_ref[...] * ln_ref[0:1, :] + ln_ref[1:2, :]) + scale * mix)
        mix_ref[...] = mix.astype(BF16)
        xhat_ref[...] = xhat
        rstd_ref[...] = rstd

    row = lambda w: pl.BlockSpec((tm, w), lambda i: (i, 0))
    return _pcall(
        body, name=name, grid=(t // tm,),
        in_specs=[row(ATTN_WIDTH), row(3 * cwid),
                  pl.BlockSpec((8, 3 * cwid), lambda i: (jnp.maximum(i * (tm // 8) - 1, 0), 0)),
                  _full((8, cwid)), _resident((d, d)), row(d), _full((2, d)),
                  pl.BlockSpec((1, N_MOD, d), lambda i: (i // tpb, 0, 0))],
        out_specs=[row(d), row(d), row(d), row(1)],
        out_shape=[jax.ShapeDtypeStruct((t, d), BF16), jax.ShapeDtypeStruct((t, d), BF16),
                   jax.ShapeDtypeStruct((t, d), F32), jax.ShapeDtypeStruct((t, 1), F32)],
        scratch_shapes=[pltpu.VMEM((tm + 8, cwid), F32), pltpu.VMEM((tm, d), F32)],
        args=(attn, ubc, ubc, cw, wout, xhat_in, lnp_in, mod), comm=comm)


def _ffn_bwd_act(df, wd, gu, *, seq, name, comm=None):
    t, d = df.shape
    f = wd.shape[0]
    tm = min(512, seq)
    ch = min(COL_CHUNK, f)

    def body(df_ref, wd_ref, gu_ref, dgu_ref):
        dfv = df_ref[...]
        for j in range(f // ch):
            da = _dot_nt(dfv, wd_ref[j * ch:(j + 1) * ch, :])
            g = gu_ref[:, j * ch:(j + 1) * ch].astype(F32)
            u = gu_ref[:, f + j * ch:f + (j + 1) * ch].astype(F32)
            s = _sigmoid(g)
            dgu_ref[:, j * ch:(j + 1) * ch] = (da * u * (s * (1.0 + g * (1.0 - s)))).astype(BF16)
            dgu_ref[:, f + j * ch:f + (j + 1) * ch] = (da * (g * s)).astype(BF16)

    return _pcall(
        body, name=name, grid=(t // tm,),
        in_specs=[pl.BlockSpec((tm, d), lambda i: (i, 0)), _resident((f, d)),
                  pl.BlockSpec((tm, 2 * f), lambda i: (i, 0))],
        out_specs=pl.BlockSpec((tm, 2 * f), lambda i: (i, 0)),
        out_shape=jax.ShapeDtypeStruct((t, 2 * f), BF16),
        args=(df, wd, gu), comm=comm)


def _bwd_in(a, w, dr, xin, rstd_prev, lnp_prev, mod, branch_prev, *, seq, w_is_nt, sc_idx, gate_idx,
            branch_scale, final, name, comm=None):
    t, kdim = a.shape
    d = dr.shape[1]
    nb = t // seq
    tm = min(512, seq)
    tpb = seq // tm

    def body(*refs):
        if final:
            a_ref, w_ref, dr_ref, x_ref, mod_ref, dx_ref, dsc_ref, dsh_ref, acc = refs
        else:
            (a_ref, w_ref, dr_ref, x_ref, rstd_ref, ln_ref, mod_ref, br_ref,
             drp_ref, dbr_ref, dsc_ref, dsh_ref, dgate_ref, dg_ref, db_ref, acc) = refs
        i = pl.program_id(0)
        av = a_ref[...]
        for j in range(d // COL_CHUNK):
            cols = slice(j * COL_CHUNK, (j + 1) * COL_CHUNK)
            acc[:, cols] = (_dot_nt(av, w_ref[cols, :]) if w_is_nt
                            else jnp.dot(av, w_ref[:, cols], preferred_element_type=F32))
        sc1 = 1.0 + mod_ref[0, sc_idx:sc_idx + 1, :]
        if not final:
            g_prev, b_prev = ln_ref[0:1, :], ln_ref[1:2, :]
            bscale = branch_scale * (1.0 + mod_ref[0, gate_idx:gate_idx + 1, :])

        def chunk(rows, carry):
            dh = acc[rows, :]
            dx = DN_ALPHA * dr_ref[rows, :] + dh * sc1
            if final:
                dx_ref[rows, :] = dx
                return carry[0] + _fold8(dh * x_ref[rows, :]), carry[1] + _fold8(dh)
            xhat = x_ref[rows, :]
            drp = _ln_bwd(dx, xhat, rstd_ref[rows, :], g_prev)
            drp_ref[rows, :] = drp
            dbr_ref[rows, :] = (bscale * drp).astype(BF16)
            return (carry[0] + _fold8(dh * (xhat * g_prev + b_prev)), carry[1] + _fold8(dh),
                    carry[2] + _fold8(branch_scale * br_ref[rows, :].astype(F32) * drp),
                    carry[3] + _fold8(dx * xhat), carry[4] + _fold8(dx))

        zero = jnp.zeros((8, d), F32)
        sums = _row_chunk_loop(tm, chunk, (zero,) * (2 if final else 5))

        @pl.when((i % tpb) == 0)
        def _():
            dsc_ref[...] = jnp.zeros_like(dsc_ref)
            dsh_ref[...] = jnp.zeros_like(dsh_ref)
            if not final:
                dgate_ref[...] = jnp.zeros_like(dgate_ref)

        dsc_ref[0] += _row_sum(sums[0])
        dsh_ref[0] += _row_sum(sums[1])
        if not final:
            @pl.when(i == 0)
            def _():
                dg_ref[...] = jnp.zeros_like(dg_ref)
                db_ref[...] = jnp.zeros_like(db_ref)

            dgate_ref[0] += _row_sum(sums[2])
            dg_ref[...] += _row_sum(sums[3])
            db_ref[...] += _row_sum(sums[4])

    row = lambda w_: pl.BlockSpec((tm, w_), lambda i: (i, 0))
    vec = pl.BlockSpec((1, 1, d), lambda i: (i // tpb, 0, 0))
    mod_spec = pl.BlockSpec((1, N_MOD, d), lambda i: (i // tpb, 0, 0))
    vshape = jax.ShapeDtypeStruct((nb, 1, d), F32)
    if final:
        in_specs = [row(kdim), _resident(w.shape), row(d), row(d), mod_spec]
        args = (a, w, dr, xin, mod)
        out_specs = [row(d), vec, vec]
        out_shape = [jax.ShapeDtypeStruct((t, d), F32), vshape, vshape]
    else:
        in_specs = [row(kdim), _resident(w.shape), row(d), row(d), row(1), _full((2, d)), mod_spec, row(d)]
        args = (a, w, dr, xin, rstd_prev, lnp_prev, mod, branch_prev)
        out_specs = [row(d), row(d), vec, vec, vec, _full((1, d)), _full((1, d))]
        out_shape = [jax.ShapeDtypeStruct((t, d), F32), jax.ShapeDtypeStruct((t, d), BF16), vshape, vshape, vshape,
                     jax.ShapeDtypeStruct((1, d), F32), jax.ShapeDtypeStruct((1, d), F32)]
    return _pcall(
        body, name=name, grid=(t // tm,), in_specs=in_specs, out_specs=out_specs, out_shape=out_shape,
        scratch_shapes=[pltpu.VMEM((tm, d), F32)], args=args, comm=comm)


def _matmul_tn(a, b, *, tmm, tnn, name, comm=None):
    t, m = a.shape
    n = b.shape[1]
    tk = min(2048, t)

    def body(a_ref, b_ref, o_ref):
        @pl.when(pl.program_id(2) == 0)
        def _():
            o_ref[...] = jnp.zeros_like(o_ref)
        o_ref[...] += _dot_tn(a_ref[...], b_ref[...])

    return _pcall(
        body, name=name, grid=(m // tmm, n // tnn, t // tk),
        in_specs=[pl.BlockSpec((tk, tmm), lambda i, j, k: (k, i)), pl.BlockSpec((tk, tnn), lambda i, j, k: (k, j))],
        out_specs=pl.BlockSpec((tmm, tnn), lambda i, j, k: (i, j)),
        out_shape=jax.ShapeDtypeStruct((m, n), F32),
        args=(a, b), comm=comm)


def _matmul_nt_bf16(a, w, *, seq, name):
    t, kdim = a.shape
    n = w.shape[0]
    tm = min(512, seq)

    def body(a_ref, w_ref, o_ref):
        av = a_ref[...]
        for j in range(n // COL_CHUNK):
            o_ref[:, j * COL_CHUNK:(j + 1) * COL_CHUNK] = _dot_nt(
                av, w_ref[j * COL_CHUNK:(j + 1) * COL_CHUNK, :]).astype(BF16)

    return pl.pallas_call(
        body, name=name, grid=(t // tm,),
        in_specs=[pl.BlockSpec((tm, kdim), lambda i: (i, 0)), _resident((n, kdim))],
        out_specs=pl.BlockSpec((tm, n), lambda i: (i, 0)),
        out_shape=jax.ShapeDtypeStruct((t, n), BF16),
        compiler_params=_params(("arbitrary",)),
    )(a, w)


def _attention_bwd(q, k, v, dmixin, sinks, *, seq, name, comm=None):
    t = q.shape[0]
    nblk = seq // BLOCK

    def body(q_ref, kp_ref, kc_ref, vp_ref, vc_ref, do_ref, sink_ref,
             dq_ref, dkp_ref, dkc_ref, dvp_ref, dvc_ref, dsink_ref):
        n = pl.program_id(0)
        first = (n % nblk) == 0

        @pl.when(n == 0)
        def _():
            dsink_ref[...] = jnp.zeros_like(dsink_ref)

        dqs, dks, dvs = [], [], []
        srow = lax.broadcasted_iota(jnp.int32, (8, LANE), 0)
        dsink = jnp.zeros((8, LANE), F32)
        for g in range(N_KV_HEADS):
            qs, kk, vv, pn, psn = _attn_group(q_ref, kp_ref, kc_ref, vp_ref, vc_ref, sink_ref, g, first)
            dos = jnp.concatenate([do_ref[:, (GQA_GROUP * g + j) * HEAD_DIM:(GQA_GROUP * g + j + 1) * HEAD_DIM]
                                   for j in range(GQA_GROUP)], axis=0)
            dp = _dot_nt(dos, vv)
            delta = jnp.sum(pn * dp, axis=1, keepdims=True)
            ds = pn * (dp - delta)
            dsk = psn * delta
            for j in range(GQA_GROUP):
                tot = jnp.sum(dsk[j * BLOCK:(j + 1) * BLOCK, :], axis=0, keepdims=True)
                dsink = dsink - jnp.where(srow == GQA_GROUP * g + j, tot, 0.0)
            dsb = (ds * (HEAD_DIM ** -0.5)).astype(BF16)
            dqg = jnp.dot(dsb, kk, preferred_element_type=F32)
            dqs += [dqg[j * BLOCK:(j + 1) * BLOCK, :] for j in range(GQA_GROUP)]
            dks.append(_dot_tn(dsb, qs))
            dvs.append(_dot_tn(pn.astype(BF16), dos))
        dsink_ref[...] += dsink
        dq_ref[...] = jnp.concatenate(dqs, axis=1)
        dkp_ref[...] = jnp.concatenate([x[0:BLOCK, :] for x in dks], axis=1)
        dkc_ref[...] = jnp.concatenate([x[BLOCK:, :] for x in dks], axis=1)
        dvp_ref[...] = jnp.concatenate([x[0:BLOCK, :] for x in dvs], axis=1)
        dvc_ref[...] = jnp.concatenate([x[BLOCK:, :] for x in dvs], axis=1)

    cur = lambda w: pl.BlockSpec((BLOCK, w), lambda n: (n, 0))
    prev = lambda w: pl.BlockSpec((BLOCK, w), lambda n: (jnp.maximum(n - 1, 0), 0))
    kv = jax.ShapeDtypeStruct((t, KV_WIDTH), F32)
    return _pcall(
        body, name=name, grid=(t // BLOCK,),
        in_specs=[cur(ATTN_WIDTH), prev(KV_WIDTH), cur(KV_WIDTH), prev(KV_WIDTH), cur(KV_WIDTH), cur(ATTN_WIDTH),
                  pl.BlockSpec(memory_space=pltpu.SMEM)],
        out_specs=[cur(ATTN_WIDTH), cur(KV_WIDTH), cur(KV_WIDTH), cur(KV_WIDTH), cur(KV_WIDTH), _full((8, LANE))],
        out_shape=[jax.ShapeDtypeStruct((t, ATTN_WIDTH), F32), kv, kv, kv, kv, jax.ShapeDtypeStruct((8, LANE), F32)],
        args=(q, k, k, v, v, dmixin, sinks), comm=comm)


def _mix_bwd_assemble(dq, dkp, dkc, dvp, dvc, cos, sa, sb, dmixin, ubc, cw, *, seq, name, comm=None):
    t = dq.shape[0]
    nblk = seq // BLOCK
    ntile = t // BLOCK
    cwid = CONV_WIDTH
    tm = BLOCK

    def body(dq_ref, dkc_ref, dkp_ref, dvc_ref, dvp_ref, cos_ref, sa_ref, sb_ref, dco_ref, dcon_ref,
             ubc_ref, hprev_ref, hnext_ref, cw_ref, dproj_ref, dcw_ref, zbuf, dybuf):
        i = pl.program_id(0)
        first = (i % nblk) == 0
        last = (i % nblk) == nblk - 1
        glast = i == ntile - 1

        @pl.when(i == 0)
        def _():
            dcw_ref[...] = jnp.zeros_like(dcw_ref)

        cos_t, sa_t, sb_t = cos_ref[...], sa_ref[...], sb_ref[...]
        for j in range(ATTN_WIDTH // LANE):
            dproj_ref[:, j * LANE:(j + 1) * LANE] = _rope_t(
                dq_ref[:, j * LANE:(j + 1) * LANE], cos_t, sa_t, sb_t).astype(BF16)
        dk = dkc_ref[...] + jnp.where(glast, 0.0, dkp_ref[...])
        dproj_ref[:, ATTN_WIDTH:ATTN_WIDTH + KV_WIDTH] = _rope_t(dk, cos_t, sa_t, sb_t).astype(BF16)
        dv = dvc_ref[...] + jnp.where(glast, 0.0, dvp_ref[...])
        dproj_ref[:, ATTN_WIDTH + KV_WIDTH:ATTN_WIDTH + 2 * KV_WIDTH] = dv.astype(BF16)

        u, bg, cg = ubc_ref[:, 0:cwid], ubc_ref[:, cwid:2 * cwid], ubc_ref[:, 2 * cwid:3 * cwid]
        z = cg * u
        hz = hprev_ref[:, 2 * cwid:3 * cwid] * hprev_ref[:, 0:cwid]
        zbuf[0:8, :] = jnp.where(first, 0.0, hz)
        zbuf[8:8 + tm, :] = z
        z2, z1 = zbuf[6:6 + tm, :], zbuf[7:7 + tm, :]
        w0, w1, w2 = cw_ref[0:1, :], cw_ref[1:2, :], cw_ref[2:3, :]
        y = w0 * z2 + w1 * z1 + w2 * z
        dco = dco_ref[...].astype(F32)
        dyc = dco * bg
        dyn = dcon_ref[0:8, :].astype(F32) * hnext_ref[:, cwid:2 * cwid]
        dybuf[0:tm, :] = dyc
        dybuf[tm:tm + 8, :] = jnp.where(last, 0.0, dyn)
        dz = w2 * dyc + w1 * dybuf[1:1 + tm, :] + w0 * dybuf[2:2 + tm, :]
        srow = lax.broadcasted_iota(jnp.int32, (8, cwid), 0)
        dcw_ref[...] += (jnp.where(srow == 0, _row_sum(dyc * z2), 0.0) + jnp.where(srow == 1, _row_sum(dyc * z1), 0.0)
                         + jnp.where(srow == 2, _row_sum(dyc * z), 0.0))
        base = ATTN_WIDTH + 2 * KV_WIDTH
        dproj_ref[:, base:base + cwid] = (dz * cg).astype(BF16)
        dproj_ref[:, base + cwid:base + 2 * cwid] = (dco * y).astype(BF16)
        dproj_ref[:, base + 2 * cwid:base + 3 * cwid] = (dz * u).astype(BF16)

    cur = lambda w: pl.BlockSpec((tm, w), lambda i: (i, 0))
    nxt = lambda w: pl.BlockSpec((tm, w), lambda i: (jnp.minimum(i + 1, ntile - 1), 0))
    return _pcall(
        body, name=name, grid=(ntile,),
        in_specs=[cur(ATTN_WIDTH), cur(KV_WIDTH), nxt(KV_WIDTH), cur(KV_WIDTH), nxt(KV_WIDTH),
                  cur(LANE), cur(LANE), cur(LANE),
                  pl.BlockSpec((tm, cwid), lambda i: (i, 1)),
                  pl.BlockSpec((16, cwid), lambda i: (jnp.minimum((i + 1) * (tm // 16), t // 16 - 1), 1)),
                  cur(3 * cwid),
                  pl.BlockSpec((8, 3 * cwid), lambda i: (jnp.maximum(i * (tm // 8) - 1, 0), 0)),
                  pl.BlockSpec((8, 3 * cwid), lambda i: (jnp.minimum((i + 1) * (tm // 8), t // 8 - 1), 0)),
                  _full((8, cwid))],
        out_specs=[cur(IN_WIDTH), _full((8, cwid))],
        out_shape=[jax.ShapeDtypeStruct((t, IN_WIDTH), BF16), jax.ShapeDtypeStruct((8, cwid), F32)],
        scratch_shapes=[pltpu.VMEM((tm + 8, cwid), F32), pltpu.VMEM((tm + 8, cwid), F32)],
        args=(dq, dkc, dkp, dvc, dvp, cos, sa, sb, dmixin, dmixin, ubc, ubc, ubc, cw), comm=comm)


def _ada_fwd(c_all, w_ada, b_ada_shard, *, name):
    nb, d = c_all.shape
    n = w_ada.shape[1]
    tn = n // 2

    def body(c_ref, w_ref, b_ref, o_ref):
        cv = c_ref[...]
        cond = cv * _sigmoid(cv)
        o_ref[...] = jnp.dot(cond, w_ref[...], preferred_element_type=F32,
                             precision=lax.Precision.HIGHEST) + b_ref[...]

    return pl.pallas_call(
        body, name=name, grid=(n // tn,),
        in_specs=[_full((nb, d)), pl.BlockSpec((d, tn), lambda j: (0, j)), pl.BlockSpec((1, tn), lambda j: (0, j))],
        out_specs=pl.BlockSpec((nb, tn), lambda j: (0, j)),
        out_shape=jax.ShapeDtypeStruct((nb, n), F32),
        compiler_params=_params(("arbitrary",)),
    )(c_all, w_ada, b_ada_shard)


def _small_finish(gathered, dmod_all, dmod_shard, c_all_t, *, name):
    d = D_MODEL
    nb, n = dmod_shard.shape

    def body(g_ref, dm_ref, dms_ref, ct_ref, sum_ref, gw_ref, gb_ref):
        total = g_ref[0]
        for dev in range(1, N_DEV):
            total = total + g_ref[dev]
        sum_ref[...] = total
        gb_ref[...] = _row_sum(dm_ref[...])
        ctv = ct_ref[...]
        cond_t = ctv * _sigmoid(ctv)
        for jb in range(n // COL_CHUNK):
            gw_ref[:, jb * COL_CHUNK:(jb + 1) * COL_CHUNK] = jnp.dot(
                cond_t, dms_ref[:, jb * COL_CHUNK:(jb + 1) * COL_CHUNK], preferred_element_type=F32,
                precision=lax.Precision.HIGHEST)

    return pl.pallas_call(
        body, name=name, grid=(1,),
        in_specs=[_full((N_DEV, SMALL_ROWS, d)), _full((nb, N_MOD * d)), _full((nb, n)), _full((d, nb))],
        out_specs=[_full((SMALL_ROWS, d)), _full((d, n)), _full((1, N_MOD * d))],
        out_shape=[jax.ShapeDtypeStruct((SMALL_ROWS, d), F32), jax.ShapeDtypeStruct((d, n), F32),
                   jax.ShapeDtypeStruct((1, N_MOD * d), F32)],
        compiler_params=_params(("arbitrary",)),
    )(gathered, dmod_all, dmod_shard, c_all_t)


def _row_tile(r, c, budget=1 << 20):
    if r * c * 4 <= budget or r % 16:
        return r
    best = 16
    for tr in range(16, r + 1, 16):
        if r % tr == 0 and tr * c * 4 <= budget:
            best = tr
    return best


def _cast_into(w, chip, col_kind, *, name):
    r, c = w.shape
    tr = _row_tile(r, c)

    def body(chip_ref, w_ref, o_ref):
        o_ref[...] = w_ref[...].astype(BF16)

    if col_kind:
        out_spec = pl.BlockSpec((tr, c), lambda i, chip_ref: (i, chip_ref[0]))
        out_shape = jax.ShapeDtypeStruct((r, c * N_CHIPS), BF16)
    else:
        out_spec = pl.BlockSpec((tr, c), lambda i, chip_ref: (chip_ref[0] * (r // tr) + i, 0))
        out_shape = jax.ShapeDtypeStruct((r * N_CHIPS, c), BF16)
    return _pcall(body, name=name, grid=(r // tr,), in_specs=[pl.BlockSpec((tr, c), lambda i, chip_ref: (i, 0))],
                  out_specs=out_spec, out_shape=out_shape, args=(w,), prefetch=chip)


def _adamw(w, g, m, v, *, name, comm=None):
    r, c = w.shape
    tr = _row_tile(r, c)
    c1 = 1.0 - ADAM_B1 ** ADAM_STEP
    c2 = 1.0 - ADAM_B2 ** ADAM_STEP

    def body(w_ref, g_ref, m_ref, v_ref, d_ref, nm_ref, nv_ref):
        gv = g_ref[...]
        m2 = ADAM_B1 * m_ref[...] + (1.0 - ADAM_B1) * gv
        v2 = ADAM_B2 * v_ref[...] + (1.0 - ADAM_B2) * (gv * gv)
        d_ref[...] = -ADAM_LR * ((m2 / c1) / (jnp.sqrt(v2 / c2) + ADAM_EPS) + ADAM_WD * w_ref[...])
        nm_ref[...] = m2
        nv_ref[...] = v2

    spec = pl.BlockSpec((tr, c), lambda i: (i, 0))
    sh = jax.ShapeDtypeStruct((r, c), F32)
    return _pcall(body, name=name, grid=(r // tr,), in_specs=[spec] * 4, out_specs=[spec] * 3, out_shape=[sh] * 3,
                  args=(w, g, m, v), comm=comm)


def _sum_pair(pos, g3, r3, blk_of, *, name, comm=None):
    n, rows, cols = r3.shape
    tr = _row_tile(rows, cols)

    def body(pos_ref, g_ref, r_ref, s32_ref, s16_ref):
        s = g_ref[0] + r_ref[0]
        s32_ref[0] = s
        s16_ref[0] = s.astype(BF16)

    own = pl.BlockSpec((1, tr, cols), lambda p, i, pos: (blk_of(p, pos), i, 0))
    plain = pl.BlockSpec((1, tr, cols), lambda p, i, pos: (p, i, 0))
    return _pcall(
        body, name=name, grid=(n, rows // tr), in_specs=[own, plain], out_specs=[plain, plain],
        out_shape=[jax.ShapeDtypeStruct((n, rows, cols), F32), jax.ShapeDtypeStruct((n, rows, cols), BF16)],
        args=(g3, r3), prefetch=pos, comm=comm)


def _sum_final(pos, s32, recv, *, col_kind, n_shard, name, comm=None):
    if col_kind:
        rows, cols = s32.shape[1], n_shard
        own = lambda tr: pl.BlockSpec((1, tr, cols), lambda i, pos: (0, i, 2 * pos[0] + pos[1]))
    else:
        rows, cols = s32.shape[1], s32.shape[2]
        own = lambda tr: pl.BlockSpec((1, tr, cols), lambda i, pos: (2 * pos[0] + pos[1], i, 0))
    tr = _row_tile(rows, cols)

    def body(pos_ref, s_ref, r_ref, o_ref):
        o_ref[0] = ((s_ref[0] + r_ref[0].astype(F32)) + r_ref[1].astype(F32)) + r_ref[2].astype(F32)

    return _pcall(
        body, name=name, grid=(rows // tr,),
        in_specs=[own(tr), pl.BlockSpec((3, tr, cols), lambda i, pos: (0, i, 0))],
        out_specs=pl.BlockSpec((1, tr, cols), lambda i, pos: (pos[2], i, 0)),
        out_shape=jax.ShapeDtypeStruct((2, rows, cols), F32), args=(s32, recv), prefetch=pos, comm=comm)


def _position():
    return lax.axis_index("x"), lax.axis_index("y"), lax.axis_index("c")


def _allgather8(x_shard, *, name, comm=None):
    m_per, n = x_shard.shape
    nci, nco = (0, 0) if comm is None else (len(comm.inputs), len(comm.out_shapes))

    def body(*refs):
        x_ref, refs = refs[0], refs[1:]
        cin, refs = refs[:nci], refs[nci:]
        out_ref, refs = refs[0], refs[1:]
        cout, refs = refs[:nco], refs[nco:]
        (send_sems, recv_sems, local_sem), csems = refs[:3], refs[3:]
        x, y, c = _position()
        me, sibling = (x, y, c), (x, y, 1 - c)
        chips = [(1 - x, y), (x, 1 - y), (1 - x, 1 - y)]

        def rows(px, py, pc):
            return out_ref.at[pl.ds((4 * px + 2 * py + pc) * m_per, m_per), :]

        def copy(k, block, to, src=None):
            return pltpu.make_async_remote_copy(
                src_ref=rows(*block) if src is None else src, dst_ref=rows(*block),
                send_sem=send_sems.at[k], recv_sem=recv_sems.at[k], device_id=to, device_id_type=MESH)

        mine = pltpu.make_async_copy(x_ref, rows(*me), local_sem)
        mine.start()
        first = [copy(0, me, sibling, src=x_ref)]
        first += [copy(1 + j, me, (*chip, c), src=x_ref) for j, chip in enumerate(chips)]
        for cp in first:
            cp.start()
        if comm is not None:
            comm.start(cin, cout, csems)
        passed = [copy(4 + j, (*chip, c), sibling) for j, chip in enumerate(chips)]
        for j, chip in enumerate(chips):
            copy(1 + j, (*chip, c), me).wait_recv()
            passed[j].start()
        copy(0, sibling, me).wait_recv()
        for j, chip in enumerate(chips):
            copy(4 + j, (*chip, 1 - c), me).wait_recv()
        for cp in first + passed:
            cp.wait_send()
        mine.wait()
        if comm is not None:
            comm.middle(cin, cout, csems)
            comm.finish(cin, cout, csems)

    vmem = pl.BlockSpec(memory_space=pltpu.VMEM)
    sems = [pltpu.SemaphoreType.DMA((7,)), pltpu.SemaphoreType.DMA((7,)), pltpu.SemaphoreType.DMA]
    out = jax.ShapeDtypeStruct((N_DEV * m_per, n), x_shard.dtype)
    if comm is None:
        return pl.pallas_call(body, name=name, out_shape=out, in_specs=[vmem], out_specs=vmem,
                              scratch_shapes=sems)(x_shard)
    res = pl.pallas_call(
        body, name=name, out_shape=[out] + list(comm.out_shapes), in_specs=[vmem] + [ANY_SPEC] * nci,
        out_specs=[vmem] + [ANY_SPEC] * nco, scratch_shapes=sems + list(comm.sems),
        input_output_aliases={1 + i: 1 + o for i, o in comm.aliases.items()})(x_shard, *comm.inputs)
    return res[0], list(res[1:])


def _peer_chips(x, y):
    return [(1 - x, y), (x, 1 - y), (1 - x, 1 - y)]


class _GatherJob:
    def __init__(self, pieces):
        self.pieces = pieces
        n_p = len(pieces)
        self.inputs = [p[0] for p in pieces]
        self.out_shapes = [jax.ShapeDtypeStruct(p[0].shape, p[0].dtype) for p in pieces]
        for buf, col_kind, r0, nr in pieces:
            half_rows = buf.shape[0] // (2 if col_kind else 2 * N_CHIPS)
            assert r0 % 16 == 0 and nr % 16 == 0 and r0 + nr <= half_rows, (buf.shape, r0, nr)
        self.aliases = {p: p for p in range(n_p)}
        self.sems = [pltpu.SemaphoreType.DMA((3 * n_p,))] * 4

    def _region(self, cout, p, chip_idx, half):
        buf, col_kind, r0, nr = self.pieces[p]
        if col_kind:
            n = buf.shape[1] // N_CHIPS
            return cout[p].at[pl.ds(half * (buf.shape[0] // 2) + r0, nr), pl.ds(chip_idx * n, n)]
        n = buf.shape[0] // N_CHIPS
        return cout[p].at[pl.ds(chip_idx * n + half * (n // 2) + r0, nr), :]

    def _copies(self, cout, sems):
        send_sems, recv_sems, fsend_sems, frecv_sems = sems
        x, y, c = _position()
        k = 2 * x + y
        sibling = (x, y, 1 - c)
        sends, arrivals, fwds, fwd_arrivals = [], [], [], []

        def remote(region, ssem, rsem, to):
            return pltpu.make_async_remote_copy(src_ref=region, dst_ref=region, send_sem=ssem, recv_sem=rsem,
                                                device_id=to, device_id_type=MESH)

        for p in range(len(self.pieces)):
            for j, chip in enumerate(_peer_chips(x, y)):
                idx = 3 * p + j
                theirs = 2 * chip[0] + chip[1]
                sends.append(remote(self._region(cout, p, k, c), send_sems.at[idx], recv_sems.at[idx], (*chip, c)))
                arrivals.append(remote(self._region(cout, p, theirs, c), send_sems.at[idx], recv_sems.at[idx],
                                       (*chip, c)))
                fwds.append(remote(self._region(cout, p, theirs, c), fsend_sems.at[idx], frecv_sems.at[idx], sibling))
                fwd_arrivals.append(remote(self._region(cout, p, theirs, 1 - c), fsend_sems.at[idx],
                                           frecv_sems.at[idx], sibling))
        return sends, arrivals, fwds, fwd_arrivals

    def start(self, cin, cout, sems):
        for cp in self._copies(cout, sems)[0]:
            cp.start()

    def middle(self, cin, cout, sems):
        _, arrivals, fwds, _ = self._copies(cout, sems)
        for arrived, fw in zip(arrivals, fwds):
            arrived.wait_recv()
            fw.start()

    def finish(self, cin, cout, sems):
        sends, _, fwds, fwd_arrivals = self._copies(cout, sems)
        for arrived in fwd_arrivals:
            arrived.wait_recv()
        for cp in sends + fwds:
            cp.wait_send()


class _PairedJob:
    aliases = {}

    def start(self, cin, cout, sems):
        for cp in self._copies(cin, cout, sems):
            cp.start()

    def middle(self, cin, cout, sems):
        pass

    def finish(self, cin, cout, sems):
        copies = self._copies(cin, cout, sems)
        for cp in copies:
            cp.wait_recv()
        for cp in copies:
            cp.wait_send()


class _SwapJob(_PairedJob):
    def __init__(self, grads, kinds):
        self.inputs, self.kinds = list(grads), list(kinds)
        self.out_shapes, self.n_copies = [], []
        for g, kd in zip(grads, kinds):
            if kd:
                self.out_shapes.append(jax.ShapeDtypeStruct((1, g.shape[0] // 2, g.shape[1]), g.dtype))
                self.n_copies.append(1)
            else:
                n = g.shape[0] // N_CHIPS
                self.out_shapes.append(jax.ShapeDtypeStruct((N_CHIPS, n // 2, g.shape[1]), g.dtype))
                self.n_copies.append(N_CHIPS)
        total = sum(self.n_copies)
        self.sems = [pltpu.SemaphoreType.DMA((total,)), pltpu.SemaphoreType.DMA((total,))]

    def _copies(self, cin, cout, sems):
        send_sems, recv_sems = sems
        x, y, c = _position()
        copies = []
        for p, src_ref in enumerate(cin):
            for kk in range(self.n_copies[p]):
                if self.kinds[p]:
                    hr = src_ref.shape[0] // 2
                    src = src_ref.at[pl.ds((1 - c) * hr, hr), :]
                else:
                    n = src_ref.shape[0] // N_CHIPS
                    src = src_ref.at[pl.ds(kk * n + (1 - c) * (n // 2), n // 2), :]
                idx = len(copies)
                copies.append(pltpu.make_async_remote_copy(
                    src_ref=src, dst_ref=cout[p].at[kk], send_sem=send_sems.at[idx], recv_sem=recv_sems.at[idx],
                    device_id=(x, y, 1 - c), device_id_type=MESH))
        return copies


class _ExchangeJob(_PairedJob):
    def __init__(self, s16, kinds, sizes):
        self.inputs, self.kinds, self.sizes = list(s16), list(kinds), list(sizes)
        self.out_shapes = [jax.ShapeDtypeStruct((3, s.shape[1], n if kd else s.shape[2]), s.dtype)
                           for s, kd, n in zip(s16, kinds, sizes)]
        self.sems = [pltpu.SemaphoreType.DMA((3 * len(s16),)), pltpu.SemaphoreType.DMA((3 * len(s16),))]

    def _copies(self, cin, cout, sems):
        send_sems, recv_sems = sems
        x, y, c = _position()
        copies = []
        for p, src_ref in enumerate(cin):
            for j, chip in enumerate(_peer_chips(x, y)):
                kk = 2 * chip[0] + chip[1]
                n = self.sizes[p]
                src = src_ref.at[0, :, pl.ds(kk * n, n)] if self.kinds[p] else src_ref.at[kk]
                copies.append(pltpu.make_async_remote_copy(
                    src_ref=src, dst_ref=cout[p].at[j], send_sem=send_sems.at[3 * p + j],
                    recv_sem=recv_sems.at[3 * p + j], device_id=(*chip, c), device_id_type=MESH))
        return copies


class _ShareJob:
    def __init__(self, halves):
        self.inputs = list(halves)
        self.out_shapes = [jax.ShapeDtypeStruct(h.shape, h.dtype) for h in halves]
        self.aliases = {p: p for p in range(len(halves))}
        self.sems = [pltpu.SemaphoreType.DMA((len(halves),)), pltpu.SemaphoreType.DMA((len(halves),))]

    def _copies(self, cout, sems, half):
        send_sems, recv_sems = sems
        x, y, c = _position()
        h = c if half == "mine" else 1 - c
        return [pltpu.make_async_remote_copy(
            src_ref=o.at[h], dst_ref=o.at[h], send_sem=send_sems.at[p], recv_sem=recv_sems.at[p],
            device_id=(x, y, 1 - c), device_id_type=MESH) for p, o in enumerate(cout)]

    def start(self, cin, cout, sems):
        for cp in self._copies(cout, sems, "mine"):
            cp.start()

    def middle(self, cin, cout, sems):
        pass

    def finish(self, cin, cout, sems):
        for cp in self._copies(cout, sems, "theirs"):
            cp.wait_recv()
        for cp in self._copies(cout, sems, "mine"):
            cp.wait_send()


class _MultiJob:
    def __init__(self, jobs):
        self.jobs = jobs
        self.inputs = [a for j in jobs for a in j.inputs]
        self.out_shapes = [s for j in jobs for s in j.out_shapes]
        self.sems = [s for j in jobs for s in j.sems]
        self.aliases = {}
        i0 = o0 = 0
        for j in jobs:
            for i, o in j.aliases.items():
                self.aliases[i0 + i] = o0 + o
            i0 += len(j.inputs)
            o0 += len(j.out_shapes)

    def _parts(self, cin, cout, sems):
        i0 = o0 = s0 = 0
        for j in self.jobs:
            ni, no, ns = len(j.inputs), len(j.out_shapes), len(j.sems)
            yield j, cin[i0:i0 + ni], cout[o0:o0 + no], sems[s0:s0 + ns]
            i0, o0, s0 = i0 + ni, o0 + no, s0 + ns

    def start(self, cin, cout, sems):
        for j, a, b, s in self._parts(cin, cout, sems):
            j.start(a, b, s)

    def middle(self, cin, cout, sems):
        for j, a, b, s in self._parts(cin, cout, sems):
            j.middle(a, b, s)

    def finish(self, cin, cout, sems):
        for j, a, b, s in self._parts(cin, cout, sems):
            j.finish(a, b, s)


def _rope_tables(positions):
    half = ROT_DIM // 2
    inv_freq = jnp.power(jnp.float32(ROPE_THETA), -jnp.arange(0, ROT_DIM, 2, dtype=F32) / ROT_DIM)
    inv_head = jnp.concatenate([inv_freq, inv_freq, jnp.zeros((HEAD_DIM - ROT_DIM,), F32)])
    inv_lane = jnp.concatenate([inv_head] * (LANE // HEAD_DIM))
    ang = positions.astype(F32).reshape(-1)[:, None] * inv_lane[None, :]
    sin = jnp.sin(ang)
    dim = jnp.arange(LANE) % HEAD_DIM
    return jnp.cos(ang), jnp.where(dim < half, -sin, 0.0), jnp.where(dim >= half, sin, 0.0)


def kernel(x, c, positions, w_ada, b_ada, ffn1_w_gate_up, ffn1_w_down, ln1_g, ln1_b, w_in, conv_w, attn_sinks, w_out, ln2_g, ln2_b, ffn2_w_gate_up, ffn2_w_down, ln3_g, ln3_b, loss_target, m_w_ada, m_b_ada, m_ffn1_w_gate_up, m_ffn1_w_down, m_ln1_g, m_ln1_b, m_w_in, m_conv_w, m_attn_sinks, m_w_out, m_ln2_g, m_ln2_b, m_ffn2_w_gate_up, m_ffn2_w_down, m_ln3_g, m_ln3_b, v_w_ada, v_b_ada, v_ffn1_w_gate_up, v_ffn1_w_down, v_ln1_g, v_ln1_b, v_w_in, v_conv_w, v_attn_sinks, v_w_out, v_ln2_g, v_ln2_b, v_ffn2_w_gate_up, v_ffn2_w_down, v_ln3_g, v_ln3_b):
    d = D_MODEL
    nb, seq, _ = x.shape
    t = nb * seq
    f = ffn1_w_down.shape[1] * N_CHIPS
    ax, ay, ac = _position()
    chip = 2 * ax + ay
    dev = 2 * chip + ac
    pos = jnp.stack([ax, ay, ac]).astype(jnp.int32)

    x2 = x.reshape(t, d)
    tgt2 = loss_target.reshape(t, d)
    ln1 = jnp.concatenate([ln1_g, ln1_b], axis=0)
    ln2 = jnp.concatenate([ln2_g, ln2_b], axis=0)
    ln3 = jnp.concatenate([ln3_g, ln3_b], axis=0)
    sinks = attn_sinks.reshape(N_Q_HEADS)
    cos_t, sa_t, sb_t = _rope_tables(positions)

    n_ada = w_ada.shape[2]
    c_all = _allgather8(c.reshape(nb * d // LANE, LANE), name="gather_c").reshape(N_DEV * nb, d)
    b_shard = lax.dynamic_slice(b_ada, (0, chip * n_ada), (1, n_ada))
    mod_part = _ada_fwd(c_all, w_ada[0], b_shard, name="ada_fwd")
    conv_rows = jnp.pad(conv_w[0], ((0, 5), (0, n_ada - conv_w.shape[2])))
    part = jnp.concatenate([mod_part, conv_rows], axis=0)
    parts = _allgather8(part, name="gather_mod").reshape(N_DEV, N_DEV * nb + 8, n_ada)
    mod_all = jnp.concatenate([parts[2 * k, :N_DEV * nb, :] for k in range(N_CHIPS)], axis=1)
    mod = lax.dynamic_slice(mod_all, (dev * nb, 0), (nb, N_MOD * d)).reshape(nb, N_MOD, d)
    cw_full = jnp.concatenate([parts[2 * k, N_DEV * nb:, :conv_w.shape[2]] for k in range(N_CHIPS)], axis=1)

    chip_arr = jnp.reshape(chip, (1,)).astype(jnp.int32)
    b_gu1 = _cast_into(ffn1_w_gate_up[0], chip_arr, True, name="cast_gu1")
    b_d1 = _cast_into(ffn1_w_down[0], chip_arr, False, name="cast_d1")
    b_in = _cast_into(w_in[0].T, chip_arr, False, name="cast_in")
    b_out = _cast_into(w_out[0], chip_arr, False, name="cast_out")
    b_gu2 = _cast_into(ffn2_w_gate_up[0], chip_arr, True, name="cast_gu2")
    b_d2 = _cast_into(ffn2_w_down[0], chip_arr, False, name="cast_d2")
    n_gu, n_d, n_in, n_out = (ffn1_w_gate_up.shape[2], ffn1_w_down.shape[1], w_in.shape[2], w_out.shape[1])

    def whole(buf, col_kind):
        return (buf, col_kind, 0, buf.shape[0] // (2 if col_kind else 2 * N_CHIPS))

    gu_cuts = [0, 176, 352, d // 2]
    gu_part = lambda buf, s: (buf, True, gu_cuts[s], gu_cuts[s + 1] - gu_cuts[s])

    (wgu1,) = _comm_call(_GatherJob([whole(b_gu1, True)]), name="gather_gu1")
    (h1, a1, gu1), (wd1, wout) = _ffn_up(x2, ln1, mod, wgu1, seq=seq, sc_idx=1, sh_idx=0, use_ln=False,
                                         name="ffn1_up", comm=_GatherJob([whole(b_d1, False), whole(b_out, False)]))
    (f1, xhat1, rstd1), (win_t,) = _ffn_down_ln(a1, wd1, x2, ln1, mod, seq=seq, gate_idx=2, use_ln=False,
                                                name="ffn1_down", comm=_GatherJob([whole(b_in, False)]))
    (h2, q, k, v, ubc), (b_gu2,) = _in_proj(
        xhat1, ln1, mod, win_t, cos_t, sa_t, sb_t, seq=seq, sc_idx=4, sh_idx=3, name="in_proj",
        comm=_GatherJob([gu_part(b_gu2, 0)]))
    attn, (b_gu2,) = _attention(q, k, v, sinks, seq=seq, name="attention", comm=_GatherJob([gu_part(b_gu2, 1)]))
    (mixin, mix, xhat2, rstd2), (wgu2,) = _out_proj(
        attn, ubc, cw_full, wout, xhat1, ln1, mod, seq=seq, gate_idx=5, name="out_proj",
        comm=_GatherJob([gu_part(b_gu2, 2)]))
    (h3, a3, gu3), (wd2,) = _ffn_up(xhat2, ln2, mod, wgu2, seq=seq, sc_idx=7, sh_idx=6, use_ln=True, name="ffn2_up",
                                    comm=_GatherJob([whole(b_d2, False)]))
    dr3, df3, loss_cols, dln3g, dln3b, dgate3 = _ffn_down_loss(
        a3, wd2, xhat2, ln2, mod, ln3, tgt2, seq=seq, gate_idx=8, name="ffn2_down_loss")

    def pair_sum(g, r3, col_kind, name_, comm=None):
        if col_kind:
            g3 = g.reshape(2, g.shape[0] // 2, g.shape[1])
            blk_of = lambda p_, pos_: pos_[2]
        else:
            g3 = g.reshape(2 * N_CHIPS, g.shape[0] // (2 * N_CHIPS), g.shape[1])
            blk_of = lambda p_, pos_: 2 * p_ + pos_[2]
        return _sum_pair(pos, g3, r3, blk_of, name=name_, comm=comm)

    dgu3 = _ffn_bwd_act(df3, wd2, gu3, seq=seq, name="ffn2_bwd_act")
    g_wd2 = _matmul_tn(a3, df3, tmm=f // 2, tnn=d, name="grad_wd2")
    g_wgu2, (sib_d2,) = _matmul_tn(h3, dgu3, tmm=d, tnn=(2 * f) // 4, name="grad_wgu2",
                                   comm=_SwapJob([g_wd2], [False]))
    s32_d2, s16_d2 = pair_sum(g_wd2, sib_d2, False, "sum_pair_d2")
    (dr2, dmix, dsc3, dsh3, dgate2, dln2g, dln2b), (sib_gu2, recv_d2) = _bwd_in(
        dgu3, wgu2, dr3, xhat2, rstd2, ln2, mod, mix, seq=seq, w_is_nt=True, sc_idx=7, gate_idx=5,
        branch_scale=1.0, final=False, name="ffn2_bwd_in",
        comm=_MultiJob([_SwapJob([g_wgu2], [True]), _ExchangeJob([s16_d2], [False], [n_d])]))
    s32_gu2, s16_gu2 = pair_sum(g_wgu2, sib_gu2, True, "sum_pair_gu2")
    g_wout = _matmul_tn(mixin, dmix, tmm=d, tnn=d, name="grad_wout")
    dmixin = _matmul_nt_bf16(dmix, wout, seq=seq, name="out_proj_bwd")
    (dq, dkp, dkc, dvp, dvc, dsink), (recv_gu2, sib_out) = _attention_bwd(
        q, k, v, dmixin, sinks, seq=seq, name="attention_bwd",
        comm=_MultiJob([_ExchangeJob([s16_gu2], [True], [n_gu]), _SwapJob([g_wout], [False])]))
    s32_out, s16_out = pair_sum(g_wout, sib_out, False, "sum_pair_out")
    (dproj, dcw), (recv_out,) = _mix_bwd_assemble(
        dq, dkp, dkc, dvp, dvc, cos_t, sa_t, sb_t, dmixin, ubc, cw_full, seq=seq, name="mix_bwd",
        comm=_ExchangeJob([s16_out], [False], [n_out]))
    g_win_t = _matmul_tn(dproj, h2, tmm=IN_WIDTH // 2, tnn=d, name="grad_win")
    (dr1, df1, dsc2, dsh2, dgate1, dln1g, dln1b), (sib_in,) = _bwd_in(
        dproj, win_t, dr2, xhat1, rstd1, ln1, mod, f1, seq=seq, w_is_nt=False, sc_idx=4, gate_idx=2,
        branch_scale=0.5, final=False, name="in_proj_bwd", comm=_SwapJob([g_win_t], [False]))
    s32_in, s16_in = pair_sum(g_win_t, sib_in, False, "sum_pair_in")
    g_wd1, (recv_in,) = _matmul_tn(a1, df1, tmm=f // 2, tnn=d, name="grad_wd1",
                                   comm=_ExchangeJob([s16_in], [False], [n_in]))
    dgu1, (sib_d1,) = _ffn_bwd_act(df1, wd1, gu1, seq=seq, name="ffn1_bwd_act", comm=_SwapJob([g_wd1], [False]))
    s32_d1, s16_d1 = pair_sum(g_wd1, sib_d1, False, "sum_pair_d1")
    g_wgu1, (recv_d1,) = _matmul_tn(h1, dgu1, tmm=d, tnn=(2 * f) // 4, name="grad_wgu1",
                                    comm=_ExchangeJob([s16_d1], [False], [n_d]))

    def final_half(s32_, recv_, col_kind, n_shard, name_, comm=None):
        return _sum_final(pos, s32_, recv_, col_kind=col_kind, n_shard=n_shard, name=name_, comm=comm)

    half_gu2, (sib_gu1,) = final_half(s32_gu2, recv_gu2, True, n_gu, "sum_final_gu2", comm=_SwapJob([g_wgu1], [True]))
    early = [half_gu2,
             final_half(s32_d2, recv_d2, False, n_d, "sum_final_d2"),
             final_half(s32_out, recv_out, False, n_out, "sum_final_out"),
             final_half(s32_in, recv_in, False, n_in, "sum_final_in"),
             final_half(s32_d1, recv_d1, False, n_d, "sum_final_d1")]
    s32_gu1, s16_gu1 = pair_sum(g_wgu1, sib_gu1, True, "sum_pair_gu1")
    (grad_x, dsc1, dsh1), (recv_gu1, full_gu2, full_d2, full_out, full_in, full_d1) = _bwd_in(
        dgu1, wgu1, dr1, x2, None, None, mod, None, seq=seq, w_is_nt=True, sc_idx=1, gate_idx=None,
        branch_scale=None, final=True, name="ffn1_bwd_in",
        comm=_MultiJob([_ExchangeJob([s16_gu1], [True], [n_gu]), _ShareJob(early)]))
    late = [final_half(s32_gu1, recv_gu1, True, n_gu, "sum_final_gu1")]

    dmod = jnp.concatenate([dsh1, dsc1, dgate1, dsh2, dsc2, dgate2, dsh3, dsc3, dgate3], axis=1)
    loss_row = jnp.sum(loss_cols, axis=1, keepdims=True) * (0.5 / d)
    lane_row = lambda a: jnp.pad(a, ((0, 0), (0, d - a.shape[1])))
    block = jnp.concatenate(
        [dmod.reshape(nb * N_MOD, d), dln1g, dln1b, dln2g, dln2b, dln3g, dln3b,
         lane_row(dcw[0:3, :]), lane_row(dsink[:, 0:1].reshape(1, N_Q_HEADS)), lane_row(loss_row)], axis=0)
    block = jnp.pad(block, ((0, SMALL_ROWS - block.shape[0]), (0, 0)))
    gathered, (full_gu1,) = _allgather8(block, name="gather_small", comm=_ShareJob(late))
    gathered = gathered.reshape(N_DEV, SMALL_ROWS, d)
    dmod_all = gathered[:, :nb * N_MOD, :].reshape(N_DEV * nb, N_MOD * d)
    dmod_shard = lax.dynamic_slice(dmod_all, (0, chip * n_ada), (N_DEV * nb, n_ada))
    small, g_w_ada, g_b_ada = _small_finish(gathered, dmod_all, dmod_shard, c_all.T, name="small_finish")
    r0 = nb * N_MOD
    loss = small[r0 + 10, 0]
    g_ln = [small[r0 + i:r0 + i + 1, :] for i in range(6)]
    g_cw_full = small[r0 + 6:r0 + 9, :CONV_WIDTH]
    g_conv = lax.dynamic_slice(g_cw_full, (0, chip * conv_w.shape[2]), (3, conv_w.shape[2]))
    g_sinks = small[r0 + 9:r0 + 10, :N_Q_HEADS]

    def flat2(a):
        return a.reshape(-1, a.shape[-1])

    def unhalve(a):
        return a.reshape(2 * a.shape[1], a.shape[2])

    results = {}

    def adamw(name_, w_, g_, m_, v_):
        g2 = flat2(g_)
        dl, nm, nv = _adamw(flat2(w_), g2, flat2(m_), flat2(v_), name="adamw_" + name_)
        results[name_] = tuple(a.reshape(w_.shape) for a in (g2, dl, nm, nv))

    adamw("w_ada", w_ada, g_w_ada, m_w_ada, v_w_ada)
    adamw("ffn2_w_gate_up", ffn2_w_gate_up, unhalve(full_gu2), m_ffn2_w_gate_up, v_ffn2_w_gate_up)
    adamw("ffn2_w_down", ffn2_w_down, unhalve(full_d2), m_ffn2_w_down, v_ffn2_w_down)
    adamw("w_out", w_out, unhalve(full_out), m_w_out, v_w_out)
    adamw("w_in", w_in, unhalve(full_in).T, m_w_in, v_w_in)
    adamw("ffn1_w_gate_up", ffn1_w_gate_up, unhalve(full_gu1), m_ffn1_w_gate_up, v_ffn1_w_gate_up)
    adamw("ffn1_w_down", ffn1_w_down, unhalve(full_d1), m_ffn1_w_down, v_ffn1_w_down)
    adamw("b_ada", b_ada, g_b_ada, m_b_ada, v_b_ada)
    adamw("ln1_g", ln1_g, g_ln[0], m_ln1_g, v_ln1_g)
    adamw("ln1_b", ln1_b, g_ln[1], m_ln1_b, v_ln1_b)
    adamw("ln2_g", ln2_g, g_ln[2], m_ln2_g, v_ln2_g)
    adamw("ln2_b", ln2_b, g_ln[3], m_ln2_b, v_ln2_b)
    adamw("ln3_g", ln3_g, g_ln[4], m_ln3_g, v_ln3_g)
    adamw("ln3_b", ln3_b, g_ln[5], m_ln3_b, v_ln3_b)
    adamw("conv_w", conv_w, g_conv, m_conv_w, v_conv_w)
    adamw("attn_sinks", attn_sinks, g_sinks, m_attn_sinks, v_attn_sinks)
    order = ["w_ada", "b_ada", "ffn1_w_gate_up", "ffn1_w_down", "ln1_g", "ln1_b", "w_in", "conv_w", "attn_sinks",
             "w_out", "ln2_g", "ln2_b", "ffn2_w_gate_up", "ffn2_w_down", "ln3_g", "ln3_b"]
    return (loss, grad_x.reshape(x.shape), *[results[n_][0] for n_ in order], *[results[n_][1] for n_ in order],
            *[results[n_][2] for n_ in order], *[results[n_][3] for n_ in order])
```

```python
import jax
import jax.numpy as jnp
from jax import lax
from jax.experimental import pallas as pl
from jax.experimental.pallas import tpu as pltpu

F32 = jnp.float32
BF16 = jnp.bfloat16
MESH = pl.DeviceIdType.MESH

D_MODEL = 1024
HEAD_DIM = 64
ATTN_WIDTH = 512
CONV_WIDTH = 512
N_Q_HEADS = 8
N_KV_HEADS = 2
GQA_GROUP = 4
KV_WIDTH = 128
WINDOW = 128
BLOCK = 128
ROT_DIM = 16
ROPE_THETA = 500000.0
N_MOD = 9
LN_EPS = 1e-5
DN_ALPHA = 2.0 ** 0.25
IN_WIDTH = 2304
N_CHIPS = 4
N_DEV = 8
SMALL_ROWS = 32

ADAM_LR = 0.001
ADAM_B1 = 0.9
ADAM_B2 = 0.999
ADAM_EPS = 1e-08
ADAM_WD = 0.01
ADAM_STEP = 10

LANE = 128
COL_CHUNK = 256
VMEM_LIMIT = 56 * 1024 * 1024


def _params(sem=None, vmem=True):
    return pltpu.CompilerParams(dimension_semantics=sem, vmem_limit_bytes=VMEM_LIMIT if vmem else None)


def _sigmoid(g):
    return 0.5 * jnp.tanh(0.5 * g) + 0.5


def _row_sum(v):
    return jnp.sum(v, axis=0, keepdims=True)


ROW_CHUNK = 16
EPILOGUE_UNROLL = 8


def _fold8(v):
    return v[0:8, :] + v[8:16, :]


def _row_chunk_loop(n_rows, step, init):
    per_iter = ROW_CHUNK * EPILOGUE_UNROLL
    assert n_rows % per_iter == 0, n_rows

    def body(it, carry):
        for s in range(EPILOGUE_UNROLL):
            start = pl.multiple_of(it * per_iter + s * ROW_CHUNK, ROW_CHUNK)
            carry = step(pl.ds(start, ROW_CHUNK), carry)
        return carry

    return lax.fori_loop(0, n_rows // per_iter, body, init)


def _ln_stats(r):
    mu = jnp.mean(r, axis=-1, keepdims=True)
    rc = r - mu
    var = jnp.mean(rc * rc, axis=-1, keepdims=True)
    rstd = lax.rsqrt(var + LN_EPS)
    return rc * rstd, rstd


def _ln_bwd(dxo, xhat, rstd, g):
    dxhat = dxo * g
    m1 = jnp.mean(dxhat, axis=-1, keepdims=True)
    m2 = jnp.mean(dxhat * xhat, axis=-1, keepdims=True)
    return rstd * (dxhat - m1 - xhat * m2)


def _dot_nt(a, b):
    return lax.dot_general(a, b, (((1,), (1,)), ((), ())), preferred_element_type=F32)


def _dot_tn(a, b):
    return lax.dot_general(a, b, (((0,), (0,)), ((), ())), preferred_element_type=F32)


def _full(shape):
    nd = len(shape)
    return pl.BlockSpec(shape, lambda *_: (0,) * nd)


def _resident(shape):
    nd = len(shape)
    return pl.BlockSpec(shape, lambda *_: (0,) * nd, pipeline_mode=pl.Buffered(1))


ANY_SPEC = pl.BlockSpec(memory_space=pl.ANY)


def _pcall(body, *, name, grid, in_specs, out_specs, out_shape, args, scratch_shapes=(), comm=None, prefetch=None):
    single = not isinstance(out_shape, (list, tuple))
    out_specs = [out_specs] if single else list(out_specs)
    out_shape = [out_shape] if single else list(out_shape)
    in_specs = list(in_specs)
    scratch_shapes = list(scratch_shapes)
    sem = ("arbitrary",) * len(grid)
    n_pre = 0 if prefetch is None else 1
    pre_args = () if prefetch is None else (prefetch,)

    def call(fn, ins_, outs_, shapes_, scratch_, aliases_, operands):
        if prefetch is None:
            return pl.pallas_call(fn, name=name, grid=grid, in_specs=ins_, out_specs=outs_, out_shape=shapes_,
                                  scratch_shapes=scratch_, input_output_aliases=aliases_,
                                  compiler_params=_params(sem))(*operands)
        spec = pltpu.PrefetchScalarGridSpec(num_scalar_prefetch=1, grid=grid, in_specs=ins_, out_specs=outs_,
                                            scratch_shapes=scratch_)
        return pl.pallas_call(fn, name=name, grid_spec=spec, out_shape=shapes_,
                              input_output_aliases={n_pre + i: o for i, o in aliases_.items()},
                              compiler_params=_params(sem))(*pre_args, *operands)

    if comm is None:
        res = call(body, in_specs, out_specs, out_shape, scratch_shapes, {}, args)
        return res[0] if single else res
    n_in, n_out, n_scr = len(in_specs), len(out_specs), len(scratch_shapes)
    nci, nco = len(comm.inputs), len(comm.out_shapes)
    n_steps = 1
    for g in grid:
        n_steps *= g
    staged = n_steps >= 4
    middle_step = n_steps - 1 - max(1, n_steps // 8)

    def wrapped(*refs):
        pre, refs = refs[:n_pre], refs[n_pre:]
        ins, refs = refs[:n_in], refs[n_in:]
        cin, refs = refs[:nci], refs[nci:]
        outs, refs = refs[:n_out], refs[n_out:]
        cout, refs = refs[:nco], refs[nco:]
        scr, csems = refs[:n_scr], refs[n_scr:]
        step = pl.program_id(0)
        for ax in range(1, len(grid)):
            step = step * grid[ax] + pl.program_id(ax)

        @pl.when(step == 0)
        def _():
            comm.start(cin, cout, csems)

        body(*pre, *ins, *outs, *scr)

        if staged:
            @pl.when(step == middle_step)
            def _():
                comm.middle(cin, cout, csems)

        @pl.when(step == n_steps - 1)
        def _():
            if not staged:
                comm.middle(cin, cout, csems)
            comm.finish(cin, cout, csems)

    res = call(wrapped, in_specs + [ANY_SPEC] * nci, out_specs + [ANY_SPEC] * nco,
               out_shape + list(comm.out_shapes), scratch_shapes + list(comm.sems),
               {n_in + i: n_out + o for i, o in comm.aliases.items()}, (*args, *comm.inputs))
    main = res[:n_out]
    return (main[0] if single else main), list(res[n_out:])


def _comm_call(job, *, name):
    nci, nco = len(job.inputs), len(job.out_shapes)

    def body(*refs):
        cin, refs = refs[:nci], refs[nci:]
        cout, csems = refs[:nco], refs[nco:]
        job.start(cin, cout, csems)
        job.middle(cin, cout, csems)
        job.finish(cin, cout, csems)

    return pl.pallas_call(
        body, name=name, out_shape=list(job.out_shapes), in_specs=[ANY_SPEC] * nci, out_specs=[ANY_SPEC] * nco,
        scratch_shapes=list(job.sems), input_output_aliases=dict(job.aliases))(*job.inputs)


def _ffn_up(xin, lnp, mod, w, *, seq, sc_idx, sh_idx, use_ln, name, comm=None):
    t, d = xin.shape
    f = w.shape[1] // 2
    tm = min(512, seq)
    tpb = seq // tm
    ch = min(COL_CHUNK, f)

    def body(x_ref, ln_ref, mod_ref, w_ref, h_ref, a_ref, gu_ref):
        x = x_ref[...]
        if use_ln:
            x = x * ln_ref[0:1, :] + ln_ref[1:2, :]
        h = x * (1.0 + mod_ref[0, sc_idx:sc_idx + 1, :]) + mod_ref[0, sh_idx:sh_idx + 1, :]
        hb = h.astype(BF16)
        h_ref[...] = hb
        for j in range(f // ch):
            g = jnp.dot(hb, w_ref[:, j * ch:(j + 1) * ch], preferred_element_type=F32)
            u = jnp.dot(hb, w_ref[:, f + j * ch:f + (j + 1) * ch], preferred_element_type=F32)
            s = _sigmoid(g)
            silu = g * s
            a_ref[:, j * ch:(j + 1) * ch] = (silu * u).astype(BF16)
            gu_ref[:, j * ch:(j + 1) * ch] = (u * (s + silu * (1.0 - s))).astype(BF16)
            gu_ref[:, f + j * ch:f + (j + 1) * ch] = silu.astype(BF16)

    return _pcall(
        body, name=name, grid=(t // tm,),
        in_specs=[pl.BlockSpec((tm, d), lambda i: (i, 0)), _full((2, d)),
                  pl.BlockSpec((1, N_MOD, d), lambda i: (i // tpb, 0, 0)), _resident((d, 2 * f))],
        out_specs=[pl.BlockSpec((tm, d), lambda i: (i, 0)), pl.BlockSpec((tm, f), lambda i: (i, 0)),
                   pl.BlockSpec((tm, 2 * f), lambda i: (i, 0))],
        out_shape=[jax.ShapeDtypeStruct((t, d), BF16), jax.ShapeDtypeStruct((t, f), BF16),
                   jax.ShapeDtypeStruct((t, 2 * f), BF16)],
        args=(xin, lnp, mod, w), comm=comm)


def _ffn_down_ln(a, wd, xin, lnp_in, mod, *, seq, gate_idx, use_ln, name, comm=None):
    t, f = a.shape
    d = wd.shape[1]
    tm = min(512, seq)
    tpb = seq // tm

    def body(a_ref, wd_ref, x_ref, ln_ref, mod_ref, f_ref, xhat_ref, rstd_ref, acc):
        av = a_ref[...]
        for j in range(d // COL_CHUNK):
            acc[:, j * COL_CHUNK:(j + 1) * COL_CHUNK] = jnp.dot(
                av, wd_ref[:, j * COL_CHUNK:(j + 1) * COL_CHUNK], preferred_element_type=F32)
        scale = 0.5 * (1.0 + mod_ref[0, gate_idx:gate_idx + 1, :])

        fo = acc[...]
        x = x_ref[...]
        if use_ln:
            x = x * ln_ref[0:1, :] + ln_ref[1:2, :]
        xhat, rstd = _ln_stats(DN_ALPHA * x + scale * fo)
        f_ref[...] = fo.astype(BF16)
        xhat_ref[...] = xhat
        rstd_ref[...] = rstd

    return _pcall(
        body, name=name, grid=(t // tm,),
        in_specs=[pl.BlockSpec((tm, f), lambda i: (i, 0)), _resident((f, d)),
                  pl.BlockSpec((tm, d), lambda i: (i, 0)), _full((2, d)),
                  pl.BlockSpec((1, N_MOD, d), lambda i: (i // tpb, 0, 0))],
        out_specs=[pl.BlockSpec((tm, d), lambda i: (i, 0)), pl.BlockSpec((tm, d), lambda i: (i, 0)),
                   pl.BlockSpec((tm, 1), lambda i: (i, 0))],
        out_shape=[jax.ShapeDtypeStruct((t, d), BF16), jax.ShapeDtypeStruct((t, d), F32),
                   jax.ShapeDtypeStruct((t, 1), F32)],
        scratch_shapes=[pltpu.VMEM((tm, d), F32)],
        args=(a, wd, xin, lnp_in, mod), comm=comm)


def _ffn_down_loss(a, wd, xhat_in, lnp_in, mod, lnp_out, tgt, *, seq, gate_idx, name):
    t, f = a.shape
    d = wd.shape[1]
    nb = t // seq
    tm = min(512, seq)
    tpb = seq // tm

    def body(a_ref, wd_ref, x_ref, lnin_ref, mod_ref, lnout_ref, tgt_ref,
             dr_ref, df_ref, loss_ref, dg_ref, db_ref, dgate_ref, acc):
        i = pl.program_id(0)
        av = a_ref[...]
        for j in range(d // COL_CHUNK):
            acc[:, j * COL_CHUNK:(j + 1) * COL_CHUNK] = jnp.dot(
                av, wd_ref[:, j * COL_CHUNK:(j + 1) * COL_CHUNK], preferred_element_type=F32)
        scale = 0.5 * (1.0 + mod_ref[0, gate_idx:gate_idx + 1, :])
        g_in, b_in = lnin_ref[0:1, :], lnin_ref[1:2, :]
        g_out, b_out = lnout_ref[0:1, :], lnout_ref[1:2, :]

        def chunk(rows, carry):
            s_loss, s_dg, s_db, s_gate = carry
            fo = acc[rows, :]
            xhat, rstd = _ln_stats(DN_ALPHA * (x_ref[rows, :] * g_in + b_in) + scale * fo)
            e = xhat * g_out + b_out - tgt_ref[rows, :]
            dy = e * (1.0 / d)
            dr = _ln_bwd(dy, xhat, rstd, g_out)
            dr_ref[rows, :] = dr
            df_ref[rows, :] = (scale * dr).astype(BF16)
            return (s_loss + _fold8(e * e), s_dg + _fold8(dy * xhat), s_db + _fold8(dy),
                    s_gate + _fold8(0.5 * fo * dr))

        zero = jnp.zeros((8, d), F32)
        s_loss, s_dg, s_db, s_gate = _row_chunk_loop(tm, chunk, (zero, zero, zero, zero))

        @pl.when(i == 0)
        def _():
            loss_ref[...] = jnp.zeros_like(loss_ref)
            dg_ref[...] = jnp.zeros_like(dg_ref)
            db_ref[...] = jnp.zeros_like(db_ref)

        @pl.when(i % tpb == 0)
        def _():
            dgate_ref[...] = jnp.zeros_like(dgate_ref)

        loss_ref[...] += _row_sum(s_loss)
        dg_ref[...] += _row_sum(s_dg)
        db_ref[...] += _row_sum(s_db)
        dgate_ref[0] += _row_sum(s_gate)

    return pl.pallas_call(
        body, name=name, grid=(t // tm,), scratch_shapes=[pltpu.VMEM((tm, d), F32)],
        in_specs=[pl.BlockSpec((tm, f), lambda i: (i, 0)), _resident((f, d)),
                  pl.BlockSpec((tm, d), lambda i: (i, 0)), _full((2, d)),
                  pl.BlockSpec((1, N_MOD, d), lambda i: (i // tpb, 0, 0)), _full((2, d)),
                  pl.BlockSpec((tm, d), lambda i: (i, 0))],
        out_specs=[pl.BlockSpec((tm, d), lambda i: (i, 0)), pl.BlockSpec((tm, d), lambda i: (i, 0)),
                   _full((1, d)), _full((1, d)), _full((1, d)),
                   pl.BlockSpec((1, 1, d), lambda i: (i // tpb, 0, 0))],
        out_shape=[jax.ShapeDtypeStruct((t, d), F32), jax.ShapeDtypeStruct((t, d), BF16),
                   jax.ShapeDtypeStruct((1, d), F32), jax.ShapeDtypeStruct((1, d), F32),
                   jax.ShapeDtypeStruct((1, d), F32), jax.ShapeDtypeStruct((nb, 1, d), F32)],
        compiler_params=_params(("arbitrary",)),
    )(a, wd, xhat_in, lnp_in, mod, lnp_out, tgt)


def _rope(v, cos, sa, sb):
    return v * cos + pltpu.roll(v, LANE - ROT_DIM // 2, 1) * sa + pltpu.roll(v, ROT_DIM // 2, 1) * sb


def _rope_t(dy, cos, sa, sb):
    return dy * cos + pltpu.roll(dy * sa, ROT_DIM // 2, 1) + pltpu.roll(dy * sb, LANE - ROT_DIM // 2, 1)


def _in_proj(xhat, lnp, mod, w_t, cos, sa, sb, *, seq, sc_idx, sh_idx, name, comm=None):
    t, d = xhat.shape
    tm = min(512, seq)
    tpb = seq // tm
    n_conv = 3 * CONV_WIDTH

    def body(x_ref, ln_ref, mod_ref, w_ref, cos_ref, sa_ref, sb_ref, h_ref, q_ref, k_ref, v_ref, ubc_ref):
        x = x_ref[...] * ln_ref[0:1, :] + ln_ref[1:2, :]
        h = x * (1.0 + mod_ref[0, sc_idx:sc_idx + 1, :]) + mod_ref[0, sh_idx:sh_idx + 1, :]
        hb = h.astype(BF16)
        h_ref[...] = hb
        cos_t, sa_t, sb_t = cos_ref[...], sa_ref[...], sb_ref[...]
        for j in range(ATTN_WIDTH // COL_CHUNK):
            p = _dot_nt(hb, w_ref[j * COL_CHUNK:(j + 1) * COL_CHUNK, :])
            for s in range(COL_CHUNK // LANE):
                q_ref[:, j * COL_CHUNK + s * LANE:j * COL_CHUNK + (s + 1) * LANE] = _rope(
                    p[:, s * LANE:(s + 1) * LANE], cos_t, sa_t, sb_t).astype(BF16)
        p = _dot_nt(hb, w_ref[ATTN_WIDTH:ATTN_WIDTH + 2 * KV_WIDTH, :])
        k_ref[...] = _rope(p[:, 0:KV_WIDTH], cos_t, sa_t, sb_t).astype(BF16)
        v_ref[...] = p[:, KV_WIDTH:].astype(BF16)
        base = ATTN_WIDTH + 2 * KV_WIDTH
        for j in range(n_conv // COL_CHUNK):
            ubc_ref[:, j * COL_CHUNK:(j + 1) * COL_CHUNK] = _dot_nt(
                hb, w_ref[base + j * COL_CHUNK:base + (j + 1) * COL_CHUNK, :])

    row = lambda w: pl.BlockSpec((tm, w), lambda i: (i, 0))
    return _pcall(
        body, name=name, grid=(t // tm,),
        in_specs=[row(d), _full((2, d)), pl.BlockSpec((1, N_MOD, d), lambda i: (i // tpb, 0, 0)),
                  _resident((IN_WIDTH, d)), row(LANE), row(LANE), row(LANE)],
        out_specs=[row(d), row(ATTN_WIDTH), row(KV_WIDTH), row(KV_WIDTH), row(n_conv)],
        out_shape=[jax.ShapeDtypeStruct((t, d), BF16), jax.ShapeDtypeStruct((t, ATTN_WIDTH), BF16),
                   jax.ShapeDtypeStruct((t, KV_WIDTH), BF16), jax.ShapeDtypeStruct((t, KV_WIDTH), BF16),
                   jax.ShapeDtypeStruct((t, n_conv), F32)],
        args=(xhat, lnp, mod, w_t, cos, sa, sb), comm=comm)


def _attn_group(q_ref, kp_ref, kc_ref, vp_ref, vc_ref, sink_ref, g, first):
    lo, hi = g * HEAD_DIM, (g + 1) * HEAD_DIM
    kk = jnp.concatenate([kp_ref[:, lo:hi], kc_ref[:, lo:hi]], axis=0)
    vv = jnp.concatenate([vp_ref[:, lo:hi], vc_ref[:, lo:hi]], axis=0)
    qs = jnp.concatenate([q_ref[:, (GQA_GROUP * g + j) * HEAD_DIM:(GQA_GROUP * g + j + 1) * HEAD_DIM]
                          for j in range(GQA_GROUP)], axis=0)
    rows = GQA_GROUP * BLOCK
    row = lax.broadcasted_iota(jnp.int32, (rows, 2 * BLOCK), 0)
    ki = lax.broadcasted_iota(jnp.int32, (rows, 2 * BLOCK), 1)
    diff = (row & (BLOCK - 1)) + BLOCK - ki
    valid = (diff >= 0) & (diff < WINDOW) & ((ki >= BLOCK) | jnp.logical_not(first))
    s = _dot_nt(qs, kk) * (HEAD_DIM ** -0.5)
    s = jnp.where(valid, s, -1e30)
    rcol = lax.broadcasted_iota(jnp.int32, (rows, 1), 0)
    sink = jnp.zeros((rows, 1), F32)
    for j in range(GQA_GROUP):
        sink = jnp.where(rcol // BLOCK == j, sink_ref[GQA_GROUP * g + j], sink)
    m = jnp.maximum(jnp.max(s, axis=1, keepdims=True), sink)
    p = jnp.exp(s - m)
    ps = jnp.exp(sink - m)
    inv = 1.0 / (jnp.sum(p, axis=1, keepdims=True) + ps)
    return qs, kk, vv, p * inv, ps * inv


def _attention(q, k, v, sinks, *, seq, name, comm=None):
    t = q.shape[0]
    nblk = seq // BLOCK

    def body(q_ref, kp_ref, kc_ref, vp_ref, vc_ref, sink_ref, o_ref):
        first = (pl.program_id(0) % nblk) == 0
        outs = []
        for g in range(N_KV_HEADS):
            _, _, vv, pn, _ = _attn_group(q_ref, kp_ref, kc_ref, vp_ref, vc_ref, sink_ref, g, first)
            o = jnp.dot(pn.astype(BF16), vv, preferred_element_type=F32)
            outs += [o[j * BLOCK:(j + 1) * BLOCK, :] for j in range(GQA_GROUP)]
        o_ref[...] = jnp.concatenate(outs, axis=1).astype(BF16)

    cur = lambda w: pl.BlockSpec((BLOCK, w), lambda n: (n, 0))
    prev = lambda w: pl.BlockSpec((BLOCK, w), lambda n: (jnp.maximum(n - 1, 0), 0))
    return _pcall(
        body, name=name, grid=(t // BLOCK,),
        in_specs=[cur(ATTN_WIDTH), prev(KV_WIDTH), cur(KV_WIDTH), prev(KV_WIDTH), cur(KV_WIDTH),
                  pl.BlockSpec(memory_space=pltpu.SMEM)],
        out_specs=cur(ATTN_WIDTH),
        out_shape=jax.ShapeDtypeStruct((t, ATTN_WIDTH), BF16),
        args=(q, k, k, v, v, sinks), comm=comm)


def _out_proj(attn, ubc, cw, wout, xhat_in, lnp_in, mod, *, seq, gate_idx, name, comm=None):
    t, d = xhat_in.shape
    tm = min(512, seq)
    tpb = seq // tm
    cwid = CONV_WIDTH

    def body(attn_ref, ubc_ref, halo_ref, cw_ref, w_ref, x_ref, ln_ref, mod_ref,
             mixin_ref, mix_ref, xhat_ref, rstd_ref, zbuf, acc):
        first = (pl.program_id(0) % tpb) == 0
        u, bg, cg = ubc_ref[:, 0:cwid], ubc_ref[:, cwid:2 * cwid], ubc_ref[:, 2 * cwid:3 * cwid]
        z = cg * u
        hz = halo_ref[:, 2 * cwid:3 * cwid] * halo_ref[:, 0:cwid]
        zbuf[0:8, :] = jnp.where(first, 0.0, hz)
        zbuf[8:8 + tm, :] = z
        y = cw_ref[0:1, :] * zbuf[6:6 + tm, :] + cw_ref[1:2, :] * zbuf[7:7 + tm, :] + cw_ref[2:3, :] * z
        mixin_ref[:, 0:ATTN_WIDTH] = attn_ref[...]
        mixin_ref[:, ATTN_WIDTH:] = (bg * y).astype(BF16)
        mv = mixin_ref[...]
        for j in range(d // COL_CHUNK):
            acc[:, j * COL_CHUNK:(j + 1) * COL_CHUNK] = jnp.dot(
                mv, w_ref[:, j * COL_CHUNK:(j + 1) * COL_CHUNK], preferred_element_type=F32)
        scale = 1.0 + mod_ref[0, gate_idx:gate_idx + 1, :]

        mix = acc[...]
        xhat, rstd = _ln_stats(DN_ALPHA * (x_ref[...] * ln_ref[0:1, :] + ln_ref[1:2, :]) + scale * mix)
        mix_ref[...] = mix.astype(BF16)
        xhat_ref[...] = xhat
        rstd_ref[...] = rstd

    row = lambda w: pl.BlockSpec((tm, w), lambda i: (i, 0))
    return _pcall(
        body, name=name, grid=(t // tm,),
        in_specs=[row(ATTN_WIDTH), row(3 * cwid),
                  pl.BlockSpec((8, 3 * cwid), lambda i: (jnp.maximum(i * (tm // 8) - 1, 0), 0)),
                  _full((8, cwid)), _resident((d, d)), row(d), _full((2, d)),
                  pl.BlockSpec((1, N_MOD, d), lambda i: (i // tpb, 0, 0))],
        out_specs=[row(d), row(d), row(d), row(1)],
        out_shape=[jax.ShapeDtypeStruct((t, d), BF16), jax.ShapeDtypeStruct((t, d), BF16),
                   jax.ShapeDtypeStruct((t, d), F32), jax.ShapeDtypeStruct((t, 1), F32)],
        scratch_shapes=[pltpu.VMEM((tm + 8, cwid), F32), pltpu.VMEM((tm, d), F32)],
        args=(attn, ubc, ubc, cw, wout, xhat_in, lnp_in, mod), comm=comm)


def _ffn_bwd_act(df, wd, gu, *, seq, name, comm=None):
    t, d = df.shape
    f = wd.shape[0]
    tm = min(512, seq)
    ch = min(COL_CHUNK, f)

    def body(df_ref, wd_ref, gu_ref, dgu_ref):
        dfv = df_ref[...]
        for j in range(f // ch):
            da = _dot_nt(dfv, wd_ref[j * ch:(j + 1) * ch, :])
            dgu_ref[:, j * ch:(j + 1) * ch] = (da * gu_ref[:, j * ch:(j + 1) * ch].astype(F32)).astype(BF16)
            dgu_ref[:, f + j * ch:f + (j + 1) * ch] = (
                da * gu_ref[:, f + j * ch:f + (j + 1) * ch].astype(F32)).astype(BF16)

    return _pcall(
        body, name=name, grid=(t // tm,),
        in_specs=[pl.BlockSpec((tm, d), lambda i: (i, 0)), _resident((f, d)),
                  pl.BlockSpec((tm, 2 * f), lambda i: (i, 0))],
        out_specs=pl.BlockSpec((tm, 2 * f), lambda i: (i, 0)),
        out_shape=jax.ShapeDtypeStruct((t, 2 * f), BF16),
        args=(df, wd, gu), comm=comm)


def _bwd_in(a, w, dr, xin, rstd_prev, lnp_prev, mod, branch_prev, *, seq, w_is_nt, sc_idx, gate_idx,
            branch_scale, final, name, comm=None):
    t, kdim = a.shape
    d = dr.shape[1]
    nb = t // seq
    tm = min(512, seq)
    tpb = seq // tm

    def body(*refs):
        if final:
            a_ref, w_ref, dr_ref, x_ref, mod_ref, dx_ref, dsc_ref, dsh_ref, acc = refs
        else:
            (a_ref, w_ref, dr_ref, x_ref, rstd_ref, ln_ref, mod_ref, br_ref,
             drp_ref, dbr_ref, dsc_ref, dsh_ref, dgate_ref, dg_ref, db_ref, acc) = refs
        i = pl.program_id(0)
        av = a_ref[...]
        for j in range(d // COL_CHUNK):
            cols = slice(j * COL_CHUNK, (j + 1) * COL_CHUNK)
            acc[:, cols] = (_dot_nt(av, w_ref[cols, :]) if w_is_nt
                            else jnp.dot(av, w_ref[:, cols], preferred_element_type=F32))
        sc1 = 1.0 + mod_ref[0, sc_idx:sc_idx + 1, :]
        if not final:
            g_prev, b_prev = ln_ref[0:1, :], ln_ref[1:2, :]
            bscale = branch_scale * (1.0 + mod_ref[0, gate_idx:gate_idx + 1, :])

        def chunk(rows, carry):
            dh = acc[rows, :]
            dx = DN_ALPHA * dr_ref[rows, :] + dh * sc1
            if final:
                dx_ref[rows, :] = dx
                return carry[0] + _fold8(dh * x_ref[rows, :]), carry[1] + _fold8(dh)
            xhat = x_ref[rows, :]
            drp = _ln_bwd(dx, xhat, rstd_ref[rows, :], g_prev)
            drp_ref[rows, :] = drp
            dbr_ref[rows, :] = (bscale * drp).astype(BF16)
            return (carry[0] + _fold8(dh * (xhat * g_prev + b_prev)), carry[1] + _fold8(dh),
                    carry[2] + _fold8(branch_scale * br_ref[rows, :].astype(F32) * drp),
                    carry[3] + _fold8(dx * xhat), carry[4] + _fold8(dx))

        zero = jnp.zeros((8, d), F32)
        sums = _row_chunk_loop(tm, chunk, (zero,) * (2 if final else 5))

        @pl.when((i % tpb) == 0)
        def _():
            dsc_ref[...] = jnp.zeros_like(dsc_ref)
            dsh_ref[...] = jnp.zeros_like(dsh_ref)
            if not final:
                dgate_ref[...] = jnp.zeros_like(dgate_ref)

        dsc_ref[0] += _row_sum(sums[0])
        dsh_ref[0] += _row_sum(sums[1])
        if not final:
            @pl.when(i == 0)
            def _():
                dg_ref[...] = jnp.zeros_like(dg_ref)
                db_ref[...] = jnp.zeros_like(db_ref)

            dgate_ref[0] += _row_sum(sums[2])
            dg_ref[...] += _row_sum(sums[3])
            db_ref[...] += _row_sum(sums[4])

    row = lambda w_: pl.BlockSpec((tm, w_), lambda i: (i, 0))
    vec = pl.BlockSpec((1, 1, d), lambda i: (i // tpb, 0, 0))
    mod_spec = pl.BlockSpec((1, N_MOD, d), lambda i: (i // tpb, 0, 0))
    vshape = jax.ShapeDtypeStruct((nb, 1, d), F32)
    if final:
        in_specs = [row(kdim), _resident(w.shape), row(d), row(d), mod_spec]
        args = (a, w, dr, xin, mod)
        out_specs = [row(d), vec, vec]
        out_shape = [jax.ShapeDtypeStruct((t, d), F32), vshape, vshape]
    else:
        in_specs = [row(kdim), _resident(w.shape), row(d), row(d), row(1), _full((2, d)), mod_spec, row(d)]
        args = (a, w, dr, xin, rstd_prev, lnp_prev, mod, branch_prev)
        out_specs = [row(d), row(d), vec, vec, vec, _full((1, d)), _full((1, d))]
        out_shape = [jax.ShapeDtypeStruct((t, d), F32), jax.ShapeDtypeStruct((t, d), BF16), vshape, vshape, vshape,
                     jax.ShapeDtypeStruct((1, d), F32), jax.ShapeDtypeStruct((1, d), F32)]
    return _pcall(
        body, name=name, grid=(t // tm,), in_specs=in_specs, out_specs=out_specs, out_shape=out_shape,
        scratch_shapes=[pltpu.VMEM((tm, d), F32)], args=args, comm=comm)


def _matmul_tn(a, b, *, tmm, tnn, name, comm=None):
    t, m = a.shape
    n = b.shape[1]
    tk = min(2048, t)

    def body(a_ref, b_ref, o_ref):
        @pl.when(pl.program_id(2) == 0)
        def _():
            o_ref[...] = jnp.zeros_like(o_ref)
        o_ref[...] += _dot_tn(a_ref[...], b_ref[...])

    return _pcall(
        body, name=name, grid=(m // tmm, n // tnn, t // tk),
        in_specs=[pl.BlockSpec((tk, tmm), lambda i, j, k: (k, i)), pl.BlockSpec((tk, tnn), lambda i, j, k: (k, j))],
        out_specs=pl.BlockSpec((tmm, tnn), lambda i, j, k: (i, j)),
        out_shape=jax.ShapeDtypeStruct((m, n), F32),
        args=(a, b), comm=comm)


def _matmul_nt_bf16(a, w, *, seq, name):
    t, kdim = a.shape
    n = w.shape[0]
    tm = min(512, seq)

    def body(a_ref, w_ref, o_ref):
        av = a_ref[...]
        for j in range(n // COL_CHUNK):
            o_ref[:, j * COL_CHUNK:(j + 1) * COL_CHUNK] = _dot_nt(
                av, w_ref[j * COL_CHUNK:(j + 1) * COL_CHUNK, :]).astype(BF16)

    return pl.pallas_call(
        body, name=name, grid=(t // tm,),
        in_specs=[pl.BlockSpec((tm, kdim), lambda i: (i, 0)), _resident((n, kdim))],
        out_specs=pl.BlockSpec((tm, n), lambda i: (i, 0)),
        out_shape=jax.ShapeDtypeStruct((t, n), BF16),
        compiler_params=_params(("arbitrary",)),
    )(a, w)


def _attention_bwd(q, k, v, dmixin, sinks, *, seq, name, comm=None):
    t = q.shape[0]
    nblk = seq // BLOCK

    def body(q_ref, kp_ref, kc_ref, vp_ref, vc_ref, do_ref, sink_ref,
             dq_ref, dkp_ref, dkc_ref, dvp_ref, dvc_ref, dsink_ref):
        n = pl.program_id(0)
        first = (n % nblk) == 0

        @pl.when(n == 0)
        def _():
            dsink_ref[...] = jnp.zeros_like(dsink_ref)

        dqs, dks, dvs = [], [], []
        srow = lax.broadcasted_iota(jnp.int32, (8, LANE), 0)
        dsink = jnp.zeros((8, LANE), F32)
        for g in range(N_KV_HEADS):
            qs, kk, vv, pn, psn = _attn_group(q_ref, kp_ref, kc_ref, vp_ref, vc_ref, sink_ref, g, first)
            dos = jnp.concatenate([do_ref[:, (GQA_GROUP * g + j) * HEAD_DIM:(GQA_GROUP * g + j + 1) * HEAD_DIM]
                                   for j in range(GQA_GROUP)], axis=0)
            dp = _dot_nt(dos, vv)
            delta = jnp.sum(pn * dp, axis=1, keepdims=True)
            ds = pn * (dp - delta)
            dsk = psn * delta
            for j in range(GQA_GROUP):
                tot = jnp.sum(dsk[j * BLOCK:(j + 1) * BLOCK, :], axis=0, keepdims=True)
                dsink = dsink - jnp.where(srow == GQA_GROUP * g + j, tot, 0.0)
            dsb = (ds * (HEAD_DIM ** -0.5)).astype(BF16)
            dqg = jnp.dot(dsb, kk, preferred_element_type=F32)
            dqs += [dqg[j * BLOCK:(j + 1) * BLOCK, :] for j in range(GQA_GROUP)]
            dks.append(_dot_tn(dsb, qs))
            dvs.append(_dot_tn(pn.astype(BF16), dos))
        dsink_ref[...] += dsink
        dq_ref[...] = jnp.concatenate(dqs, axis=1)
        dkp_ref[...] = jnp.concatenate([x[0:BLOCK, :] for x in dks], axis=1)
        dkc_ref[...] = jnp.concatenate([x[BLOCK:, :] for x in dks], axis=1)
        dvp_ref[...] = jnp.concatenate([x[0:BLOCK, :] for x in dvs], axis=1)
        dvc_ref[...] = jnp.concatenate([x[BLOCK:, :] for x in dvs], axis=1)

    cur = lambda w: pl.BlockSpec((BLOCK, w), lambda n: (n, 0))
    prev = lambda w: pl.BlockSpec((BLOCK, w), lambda n: (jnp.maximum(n - 1, 0), 0))
    kv = jax.ShapeDtypeStruct((t, KV_WIDTH), F32)
    return _pcall(
        body, name=name, grid=(t // BLOCK,),
        in_specs=[cur(ATTN_WIDTH), prev(KV_WIDTH), cur(KV_WIDTH), prev(KV_WIDTH), cur(KV_WIDTH), cur(ATTN_WIDTH),
                  pl.BlockSpec(memory_space=pltpu.SMEM)],
        out_specs=[cur(ATTN_WIDTH), cur(KV_WIDTH), cur(KV_WIDTH), cur(KV_WIDTH), cur(KV_WIDTH), _full((8, LANE))],
        out_shape=[jax.ShapeDtypeStruct((t, ATTN_WIDTH), F32), kv, kv, kv, kv, jax.ShapeDtypeStruct((8, LANE), F32)],
        args=(q, k, k, v, v, dmixin, sinks), comm=comm)


def _mix_bwd_assemble(dq, dkp, dkc, dvp, dvc, cos, sa, sb, dmixin, ubc, cw, *, seq, name, comm=None):
    t = dq.shape[0]
    nblk = seq // BLOCK
    ntile = t // BLOCK
    cwid = CONV_WIDTH
    tm = BLOCK

    def body(dq_ref, dkc_ref, dkp_ref, dvc_ref, dvp_ref, cos_ref, sa_ref, sb_ref, dco_ref, dcon_ref,
             ubc_ref, hprev_ref, hnext_ref, cw_ref, dproj_ref, dcw_ref, zbuf, dybuf):
        i = pl.program_id(0)
        first = (i % nblk) == 0
        last = (i % nblk) == nblk - 1
        glast = i == ntile - 1

        @pl.when(i == 0)
        def _():
            dcw_ref[...] = jnp.zeros_like(dcw_ref)

        cos_t, sa_t, sb_t = cos_ref[...], sa_ref[...], sb_ref[...]
        for j in range(ATTN_WIDTH // LANE):
            dproj_ref[:, j * LANE:(j + 1) * LANE] = _rope_t(
                dq_ref[:, j * LANE:(j + 1) * LANE], cos_t, sa_t, sb_t).astype(BF16)
        dk = dkc_ref[...] + jnp.where(glast, 0.0, dkp_ref[...])
        dproj_ref[:, ATTN_WIDTH:ATTN_WIDTH + KV_WIDTH] = _rope_t(dk, cos_t, sa_t, sb_t).astype(BF16)
        dv = dvc_ref[...] + jnp.where(glast, 0.0, dvp_ref[...])
        dproj_ref[:, ATTN_WIDTH + KV_WIDTH:ATTN_WIDTH + 2 * KV_WIDTH] = dv.astype(BF16)

        u, bg, cg = ubc_ref[:, 0:cwid], ubc_ref[:, cwid:2 * cwid], ubc_ref[:, 2 * cwid:3 * cwid]
        z = cg * u
        hz = hprev_ref[:, 2 * cwid:3 * cwid] * hprev_ref[:, 0:cwid]
        zbuf[0:8, :] = jnp.where(first, 0.0, hz)
        zbuf[8:8 + tm, :] = z
        z2, z1 = zbuf[6:6 + tm, :], zbuf[7:7 + tm, :]
        w0, w1, w2 = cw_ref[0:1, :], cw_ref[1:2, :], cw_ref[2:3, :]
        y = w0 * z2 + w1 * z1 + w2 * z
        dco = dco_ref[...].astype(F32)
        dyc = dco * bg
        dyn = dcon_ref[0:8, :].astype(F32) * hnext_ref[:, cwid:2 * cwid]
        dybuf[0:tm, :] = dyc
        dybuf[tm:tm + 8, :] = jnp.where(last, 0.0, dyn)
        dz = w2 * dyc + w1 * dybuf[1:1 + tm, :] + w0 * dybuf[2:2 + tm, :]
        srow = lax.broadcasted_iota(jnp.int32, (8, cwid), 0)
        dcw_ref[...] += (jnp.where(srow == 0, _row_sum(dyc * z2), 0.0) + jnp.where(srow == 1, _row_sum(dyc * z1), 0.0)
                         + jnp.where(srow == 2, _row_sum(dyc * z), 0.0))
        base = ATTN_WIDTH + 2 * KV_WIDTH
        dproj_ref[:, base:base + cwid] = (dz * cg).astype(BF16)
        dproj_ref[:, base + cwid:base + 2 * cwid] = (dco * y).astype(BF16)
        dproj_ref[:, base + 2 * cwid:base + 3 * cwid] = (dz * u).astype(BF16)

    cur = lambda w: pl.BlockSpec((tm, w), lambda i: (i, 0))
    nxt = lambda w: pl.BlockSpec((tm, w), lambda i: (jnp.minimum(i + 1, ntile - 1), 0))
    return _pcall(
        body, name=name, grid=(ntile,),
        in_specs=[cur(ATTN_WIDTH), cur(KV_WIDTH), nxt(KV_WIDTH), cur(KV_WIDTH), nxt(KV_WIDTH),
                  cur(LANE), cur(LANE), cur(LANE),
                  pl.BlockSpec((tm, cwid), lambda i: (i, 1)),
                  pl.BlockSpec((16, cwid), lambda i: (jnp.minimum((i + 1) * (tm // 16), t // 16 - 1), 1)),
                  cur(3 * cwid),
                  pl.BlockSpec((8, 3 * cwid), lambda i: (jnp.maximum(i * (tm // 8) - 1, 0), 0)),
                  pl.BlockSpec((8, 3 * cwid), lambda i: (jnp.minimum((i + 1) * (tm // 8), t // 8 - 1), 0)),
                  _full((8, cwid))],
        out_specs=[cur(IN_WIDTH), _full((8, cwid))],
        out_shape=[jax.ShapeDtypeStruct((t, IN_WIDTH), BF16), jax.ShapeDtypeStruct((8, cwid), F32)],
        scratch_shapes=[pltpu.VMEM((tm + 8, cwid), F32), pltpu.VMEM((tm + 8, cwid), F32)],
        args=(dq, dkc, dkp, dvc, dvp, cos, sa, sb, dmixin, dmixin, ubc, ubc, ubc, cw), comm=comm)


def _ada_fwd(c_all, w_ada, b_ada_shard, *, name):
    nb, d = c_all.shape
    n = w_ada.shape[1]
    tn = n // 2

    def body(c_ref, w_ref, b_ref, o_ref):
        cv = c_ref[...]
        cond = cv * _sigmoid(cv)
        o_ref[...] = jnp.dot(cond, w_ref[...], preferred_element_type=F32,
                             precision=lax.Precision.HIGHEST) + b_ref[...]

    return pl.pallas_call(
        body, name=name, grid=(n // tn,),
        in_specs=[_full((nb, d)), pl.BlockSpec((d, tn), lambda j: (0, j)), pl.BlockSpec((1, tn), lambda j: (0, j))],
        out_specs=pl.BlockSpec((nb, tn), lambda j: (0, j)),
        out_shape=jax.ShapeDtypeStruct((nb, n), F32),
        compiler_params=_params(("arbitrary",)),
    )(c_all, w_ada, b_ada_shard)


def _small_finish(gathered, dmod_all, dmod_shard, c_all_t, *, name):
    d = D_MODEL
    nb, n = dmod_shard.shape

    def body(g_ref, dm_ref, dms_ref, ct_ref, sum_ref, gw_ref, gb_ref):
        total = g_ref[0]
        for dev in range(1, N_DEV):
            total = total + g_ref[dev]
        sum_ref[...] = total
        gb_ref[...] = _row_sum(dm_ref[...])
        ctv = ct_ref[...]
        cond_t = ctv * _sigmoid(ctv)
        for jb in range(n // COL_CHUNK):
            gw_ref[:, jb * COL_CHUNK:(jb + 1) * COL_CHUNK] = jnp.dot(
                cond_t, dms_ref[:, jb * COL_CHUNK:(jb + 1) * COL_CHUNK], preferred_element_type=F32,
                precision=lax.Precision.HIGHEST)

    return pl.pallas_call(
        body, name=name, grid=(1,),
        in_specs=[_full((N_DEV, SMALL_ROWS, d)), _full((nb, N_MOD * d)), _full((nb, n)), _full((d, nb))],
        out_specs=[_full((SMALL_ROWS, d)), _full((d, n)), _full((1, N_MOD * d))],
        out_shape=[jax.ShapeDtypeStruct((SMALL_ROWS, d), F32), jax.ShapeDtypeStruct((d, n), F32),
                   jax.ShapeDtypeStruct((1, N_MOD * d), F32)],
        compiler_params=_params(("arbitrary",)),
    )(gathered, dmod_all, dmod_shard, c_all_t)


def _row_tile(r, c, budget=1 << 20):
    if r * c * 4 <= budget or r % 16:
        return r
    best = 16
    for tr in range(16, r + 1, 16):
        if r % tr == 0 and tr * c * 4 <= budget:
            best = tr
    return best


def _cast_into(w, chip, col_kind, *, name):
    r, c = w.shape
    tr = _row_tile(r, c)

    def body(chip_ref, w_ref, o_ref):
        o_ref[...] = w_ref[...].astype(BF16)

    if col_kind:
        out_spec = pl.BlockSpec((tr, c), lambda i, chip_ref: (i, chip_ref[0]))
        out_shape = jax.ShapeDtypeStruct((r, c * N_CHIPS), BF16)
    else:
        out_spec = pl.BlockSpec((tr, c), lambda i, chip_ref: (chip_ref[0] * (r // tr) + i, 0))
        out_shape = jax.ShapeDtypeStruct((r * N_CHIPS, c), BF16)
    return _pcall(body, name=name, grid=(r // tr,), in_specs=[pl.BlockSpec((tr, c), lambda i, chip_ref: (i, 0))],
                  out_specs=out_spec, out_shape=out_shape, args=(w,), prefetch=chip)


def _adamw(w, g, m, v, *, name, comm=None):
    r, c = w.shape
    tr = _row_tile(r, c)
    c1 = 1.0 - ADAM_B1 ** ADAM_STEP
    c2 = 1.0 - ADAM_B2 ** ADAM_STEP

    def body(w_ref, g_ref, m_ref, v_ref, d_ref, nm_ref, nv_ref):
        gv = g_ref[...]
        m2 = ADAM_B1 * m_ref[...] + (1.0 - ADAM_B1) * gv
        v2 = ADAM_B2 * v_ref[...] + (1.0 - ADAM_B2) * (gv * gv)
        d_ref[...] = -ADAM_LR * ((m2 / c1) / (jnp.sqrt(v2 / c2) + ADAM_EPS) + ADAM_WD * w_ref[...])
        nm_ref[...] = m2
        nv_ref[...] = v2

    spec = pl.BlockSpec((tr, c), lambda i: (i, 0))
    sh = jax.ShapeDtypeStruct((r, c), F32)
    return _pcall(body, name=name, grid=(r // tr,), in_specs=[spec] * 4, out_specs=[spec] * 3, out_shape=[sh] * 3,
                  args=(w, g, m, v), comm=comm)


def _sum_pair(pos, g3, r3, blk_of, *, name, comm=None):
    n, rows, cols = r3.shape
    tr = _row_tile(rows, cols)

    def body(pos_ref, g_ref, r_ref, s32_ref, s16_ref):
        s = g_ref[0] + r_ref[0]
        s32_ref[0] = s
        s16_ref[0] = s.astype(BF16)

    own = pl.BlockSpec((1, tr, cols), lambda p, i, pos: (blk_of(p, pos), i, 0))
    plain = pl.BlockSpec((1, tr, cols), lambda p, i, pos: (p, i, 0))
    return _pcall(
        body, name=name, grid=(n, rows // tr), in_specs=[own, plain], out_specs=[plain, plain],
        out_shape=[jax.ShapeDtypeStruct((n, rows, cols), F32), jax.ShapeDtypeStruct((n, rows, cols), BF16)],
        args=(g3, r3), prefetch=pos, comm=comm)


def _sum_final(pos, s32, recv, *, col_kind, n_shard, name, comm=None):
    if col_kind:
        rows, cols = s32.shape[1], n_shard
        own = lambda tr: pl.BlockSpec((1, tr, cols), lambda i, pos: (0, i, 2 * pos[0] + pos[1]))
    else:
        rows, cols = s32.shape[1], s32.shape[2]
        own = lambda tr: pl.BlockSpec((1, tr, cols), lambda i, pos: (2 * pos[0] + pos[1], i, 0))
    tr = _row_tile(rows, cols)

    def body(pos_ref, s_ref, r_ref, o_ref):
        o_ref[0] = ((s_ref[0] + r_ref[0].astype(F32)) + r_ref[1].astype(F32)) + r_ref[2].astype(F32)

    return _pcall(
        body, name=name, grid=(rows // tr,),
        in_specs=[own(tr), pl.BlockSpec((3, tr, cols), lambda i, pos: (0, i, 0))],
        out_specs=pl.BlockSpec((1, tr, cols), lambda i, pos: (pos[2], i, 0)),
        out_shape=jax.ShapeDtypeStruct((2, rows, cols), F32), args=(s32, recv), prefetch=pos, comm=comm)


def _position():
    return lax.axis_index("x"), lax.axis_index("y"), lax.axis_index("c")


def _allgather8(x_shard, *, name, comm=None):
    m_per, n = x_shard.shape
    nci, nco = (0, 0) if comm is None else (len(comm.inputs), len(comm.out_shapes))

    def body(*refs):
        x_ref, refs = refs[0], refs[1:]
        cin, refs = refs[:nci], refs[nci:]
        out_ref, refs = refs[0], refs[1:]
        cout, refs = refs[:nco], refs[nco:]
        (send_sems, recv_sems, local_sem), csems = refs[:3], refs[3:]
        x, y, c = _position()
        me, sibling = (x, y, c), (x, y, 1 - c)
        chips = [(1 - x, y), (x, 1 - y), (1 - x, 1 - y)]

        def rows(px, py, pc):
            return out_ref.at[pl.ds((4 * px + 2 * py + pc) * m_per, m_per), :]

        def copy(k, block, to, src=None):
            return pltpu.make_async_remote_copy(
                src_ref=rows(*block) if src is None else src, dst_ref=rows(*block),
                send_sem=send_sems.at[k], recv_sem=recv_sems.at[k], device_id=to, device_id_type=MESH)

        mine = pltpu.make_async_copy(x_ref, rows(*me), local_sem)
        mine.start()
        first = [copy(0, me, sibling, src=x_ref)]
        first += [copy(1 + j, me, (*chip, c), src=x_ref) for j, chip in enumerate(chips)]
        for cp in first:
            cp.start()
        if comm is not None:
            comm.start(cin, cout, csems)
        passed = [copy(4 + j, (*chip, c), sibling) for j, chip in enumerate(chips)]
        for j, chip in enumerate(chips):
            copy(1 + j, (*chip, c), me).wait_recv()
            passed[j].start()
        copy(0, sibling, me).wait_recv()
        for j, chip in enumerate(chips):
            copy(4 + j, (*chip, 1 - c), me).wait_recv()
        for cp in first + passed:
            cp.wait_send()
        mine.wait()
        if comm is not None:
            comm.middle(cin, cout, csems)
            comm.finish(cin, cout, csems)

    vmem = pl.BlockSpec(memory_space=pltpu.VMEM)
    sems = [pltpu.SemaphoreType.DMA((7,)), pltpu.SemaphoreType.DMA((7,)), pltpu.SemaphoreType.DMA]
    out = jax.ShapeDtypeStruct((N_DEV * m_per, n), x_shard.dtype)
    if comm is None:
        return pl.pallas_call(body, name=name, out_shape=out, in_specs=[vmem], out_specs=vmem,
                              scratch_shapes=sems)(x_shard)
    res = pl.pallas_call(
        body, name=name, out_shape=[out] + list(comm.out_shapes), in_specs=[vmem] + [ANY_SPEC] * nci,
        out_specs=[vmem] + [ANY_SPEC] * nco, scratch_shapes=sems + list(comm.sems),
        input_output_aliases={1 + i: 1 + o for i, o in comm.aliases.items()})(x_shard, *comm.inputs)
    return res[0], list(res[1:])


def _peer_chips(x, y):
    return [(1 - x, y), (x, 1 - y), (1 - x, 1 - y)]


class _GatherJob:
    def __init__(self, pieces):
        self.pieces = pieces
        n_p = len(pieces)
        self.inputs = [p[0] for p in pieces]
        self.out_shapes = [jax.ShapeDtypeStruct(p[0].shape, p[0].dtype) for p in pieces]
        for buf, col_kind, r0, nr in pieces:
            half_rows = buf.shape[0] // (2 if col_kind else 2 * N_CHIPS)
            assert r0 % 16 == 0 and nr % 16 == 0 and r0 + nr <= half_rows, (buf.shape, r0, nr)
        self.aliases = {p: p for p in range(n_p)}
        self.sems = [pltpu.SemaphoreType.DMA((3 * n_p,))] * 4

    def _region(self, cout, p, chip_idx, half):
        buf, col_kind, r0, nr = self.pieces[p]
        if col_kind:
            n = buf.shape[1] // N_CHIPS
            return cout[p].at[pl.ds(half * (buf.shape[0] // 2) + r0, nr), pl.ds(chip_idx * n, n)]
        n = buf.shape[0] // N_CHIPS
        return cout[p].at[pl.ds(chip_idx * n + half * (n // 2) + r0, nr), :]

    def _copies(self, cout, sems):
        send_sems, recv_sems, fsend_sems, frecv_sems = sems
        x, y, c = _position()
        k = 2 * x + y
        sibling = (x, y, 1 - c)
        sends, arrivals, fwds, fwd_arrivals = [], [], [], []

        def remote(region, ssem, rsem, to):
            return pltpu.make_async_remote_copy(src_ref=region, dst_ref=region, send_sem=ssem, recv_sem=rsem,
                                                device_id=to, device_id_type=MESH)

        for p in range(len(self.pieces)):
            for j, chip in enumerate(_peer_chips(x, y)):
                idx = 3 * p + j
                theirs = 2 * chip[0] + chip[1]
                sends.append(remote(self._region(cout, p, k, c), send_sems.at[idx], recv_sems.at[idx], (*chip, c)))
                arrivals.append(remote(self._region(cout, p, theirs, c), send_sems.at[idx], recv_sems.at[idx],
                                       (*chip, c)))
                fwds.append(remote(self._region(cout, p, theirs, c), fsend_sems.at[idx], frecv_sems.at[idx], sibling))
                fwd_arrivals.append(remote(self._region(cout, p, theirs, 1 - c), fsend_sems.at[idx],
                                           frecv_sems.at[idx], sibling))
        return sends, arrivals, fwds, fwd_arrivals

    def start(self, cin, cout, sems):
        for cp in self._copies(cout, sems)[0]:
            cp.start()

    def middle(self, cin, cout, sems):
        _, arrivals, fwds, _ = self._copies(cout, sems)
        for arrived, fw in zip(arrivals, fwds):
            arrived.wait_recv()
            fw.start()

    def finish(self, cin, cout, sems):
        sends, _, fwds, fwd_arrivals = self._copies(cout, sems)
        for arrived in fwd_arrivals:
            arrived.wait_recv()
        for cp in sends + fwds:
            cp.wait_send()


class _PairedJob:
    aliases = {}

    def start(self, cin, cout, sems):
        for cp in self._copies(cin, cout, sems):
            cp.start()

    def middle(self, cin, cout, sems):
        pass

    def finish(self, cin, cout, sems):
        copies = self._copies(cin, cout, sems)
        for cp in copies:
            cp.wait_recv()
        for cp in copies:
            cp.wait_send()


class _SwapJob(_PairedJob):
    def __init__(self, grads, kinds):
        self.inputs, self.kinds = list(grads), list(kinds)
        self.out_shapes, self.n_copies = [], []
        for g, kd in zip(grads, kinds):
            if kd:
                self.out_shapes.append(jax.ShapeDtypeStruct((1, g.shape[0] // 2, g.shape[1]), g.dtype))
                self.n_copies.append(1)
            else:
                n = g.shape[0] // N_CHIPS
                self.out_shapes.append(jax.ShapeDtypeStruct((N_CHIPS, n // 2, g.shape[1]), g.dtype))
                self.n_copies.append(N_CHIPS)
        total = sum(self.n_copies)
        self.sems = [pltpu.SemaphoreType.DMA((total,)), pltpu.SemaphoreType.DMA((total,))]

    def _copies(self, cin, cout, sems):
        send_sems, recv_sems = sems
        x, y, c = _position()
        copies = []
        for p, src_ref in enumerate(cin):
            for kk in range(self.n_copies[p]):
                if self.kinds[p]:
                    hr = src_ref.shape[0] // 2
                    src = src_ref.at[pl.ds((1 - c) * hr, hr), :]
                else:
                    n = src_ref.shape[0] // N_CHIPS
                    src = src_ref.at[pl.ds(kk * n + (1 - c) * (n // 2), n // 2), :]
                idx = len(copies)
                copies.append(pltpu.make_async_remote_copy(
                    src_ref=src, dst_ref=cout[p].at[kk], send_sem=send_sems.at[idx], recv_sem=recv_sems.at[idx],
                    device_id=(x, y, 1 - c), device_id_type=MESH))
        return copies


class _ExchangeJob(_PairedJob):
    def __init__(self, s16, kinds, sizes):
        self.inputs, self.kinds, self.sizes = list(s16), list(kinds), list(sizes)
        self.out_shapes = [jax.ShapeDtypeStruct((3, s.shape[1], n if kd else s.shape[2]), s.dtype)
                           for s, kd, n in zip(s16, kinds, sizes)]
        self.sems = [pltpu.SemaphoreType.DMA((3 * len(s16),)), pltpu.SemaphoreType.DMA((3 * len(s16),))]

    def _copies(self, cin, cout, sems):
        send_sems, recv_sems = sems
        x, y, c = _position()
        copies = []
        for p, src_ref in enumerate(cin):
            for j, chip in enumerate(_peer_chips(x, y)):
                kk = 2 * chip[0] + chip[1]
                n = self.sizes[p]
                src = src_ref.at[0, :, pl.ds(kk * n, n)] if self.kinds[p] else src_ref.at[kk]
                copies.append(pltpu.make_async_remote_copy(
                    src_ref=src, dst_ref=cout[p].at[j], send_sem=send_sems.at[3 * p + j],
                    recv_sem=recv_sems.at[3 * p + j], device_id=(*chip, c), device_id_type=MESH))
        return copies


class _ShareJob:
    def __init__(self, halves):
        self.inputs = list(halves)
        self.out_shapes = [jax.ShapeDtypeStruct(h.shape, h.dtype) for h in halves]
        self.aliases = {p: p for p in range(len(halves))}
        self.sems = [pltpu.SemaphoreType.DMA((len(halves),)), pltpu.SemaphoreType.DMA((len(halves),))]

    def _copies(self, cout, sems, half):
        send_sems, recv_sems = sems
        x, y, c = _position()
        h = c if half == "mine" else 1 - c
        return [pltpu.make_async_remote_copy(
            src_ref=o.at[h], dst_ref=o.at[h], send_sem=send_sems.at[p], recv_sem=recv_sems.at[p],
            device_id=(x, y, 1 - c), device_id_type=MESH) for p, o in enumerate(cout)]

    def start(self, cin, cout, sems):
        for cp in self._copies(cout, sems, "mine"):
            cp.start()

    def middle(self, cin, cout, sems):
        pass

    def finish(self, cin, cout, sems):
        for cp in self._copies(cout, sems, "theirs"):
            cp.wait_recv()
        for cp in self._copies(cout, sems, "mine"):
            cp.wait_send()


class _MultiJob:
    def __init__(self, jobs):
        self.jobs = jobs
        self.inputs = [a for j in jobs for a in j.inputs]
        self.out_shapes = [s for j in jobs for s in j.out_shapes]
        self.sems = [s for j in jobs for s in j.sems]
        self.aliases = {}
        i0 = o0 = 0
        for j in jobs:
            for i, o in j.aliases.items():
                self.aliases[i0 + i] = o0 + o
            i0 += len(j.inputs)
            o0 += len(j.out_shapes)

    def _parts(self, cin, cout, sems):
        i0 = o0 = s0 = 0
        for j in self.jobs:
            ni, no, ns = len(j.inputs), len(j.out_shapes), len(j.sems)
            yield j, cin[i0:i0 + ni], cout[o0:o0 + no], sems[s0:s0 + ns]
            i0, o0, s0 = i0 + ni, o0 + no, s0 + ns

    def start(self, cin, cout, sems):
        for j, a, b, s in self._parts(cin, cout, sems):
            j.start(a, b, s)

    def middle(self, cin, cout, sems):
        for j, a, b, s in self._parts(cin, cout, sems):
            j.middle(a, b, s)

    def finish(self, cin, cout, sems):
        for j, a, b, s in self._parts(cin, cout, sems):
            j.finish(a, b, s)


def _rope_tables(positions):
    half = ROT_DIM // 2
    inv_freq = jnp.power(jnp.float32(ROPE_THETA), -jnp.arange(0, ROT_DIM, 2, dtype=F32) / ROT_DIM)
    inv_head = jnp.concatenate([inv_freq, inv_freq, jnp.zeros((HEAD_DIM - ROT_DIM,), F32)])
    inv_lane = jnp.concatenate([inv_head] * (LANE // HEAD_DIM))
    ang = positions.astype(F32).reshape(-1)[:, None] * inv_lane[None, :]
    sin = jnp.sin(ang)
    dim = jnp.arange(LANE) % HEAD_DIM
    return jnp.cos(ang), jnp.where(dim < half, -sin, 0.0), jnp.where(dim >= half, sin, 0.0)


def kernel(x, c, positions, w_ada, b_ada, ffn1_w_gate_up, ffn1_w_down, ln1_g, ln1_b, w_in, conv_w, attn_sinks, w_out, ln2_g, ln2_b, ffn2_w_gate_up, ffn2_w_down, ln3_g, ln3_b, loss_target, m_w_ada, m_b_ada, m_ffn1_w_gate_up, m_ffn1_w_down, m_ln1_g, m_ln1_b, m_w_in, m_conv_w, m_attn_sinks, m_w_out, m_ln2_g, m_ln2_b, m_ffn2_w_gate_up, m_ffn2_w_down, m_ln3_g, m_ln3_b, v_w_ada, v_b_ada, v_ffn1_w_gate_up, v_ffn1_w_down, v_ln1_g, v_ln1_b, v_w_in, v_conv_w, v_attn_sinks, v_w_out, v_ln2_g, v_ln2_b, v_ffn2_w_gate_up, v_ffn2_w_down, v_ln3_g, v_ln3_b):
    d = D_MODEL
    nb, seq, _ = x.shape
    t = nb * seq
    f = ffn1_w_down.shape[1] * N_CHIPS
    ax, ay, ac = _position()
    chip = 2 * ax + ay
    dev = 2 * chip + ac
    pos = jnp.stack([ax, ay, ac]).astype(jnp.int32)

    x2 = x.reshape(t, d)
    tgt2 = loss_target.reshape(t, d)
    ln1 = jnp.concatenate([ln1_g, ln1_b], axis=0)
    ln2 = jnp.concatenate([ln2_g, ln2_b], axis=0)
    ln3 = jnp.concatenate([ln3_g, ln3_b], axis=0)
    sinks = attn_sinks.reshape(N_Q_HEADS)
    cos_t, sa_t, sb_t = _rope_tables(positions)

    n_ada = w_ada.shape[2]
    c_all = _allgather8(c.reshape(nb * d // LANE, LANE), name="gather_c").reshape(N_DEV * nb, d)
    b_shard = lax.dynamic_slice(b_ada, (0, chip * n_ada), (1, n_ada))
    mod_part = _ada_fwd(c_all, w_ada[0], b_shard, name="ada_fwd")
    conv_rows = jnp.pad(conv_w[0], ((0, 5), (0, n_ada - conv_w.shape[2])))
    part = jnp.concatenate([mod_part, conv_rows], axis=0)
    parts = _allgather8(part, name="gather_mod").reshape(N_DEV, N_DEV * nb + 8, n_ada)
    mod_all = jnp.concatenate([parts[2 * k, :N_DEV * nb, :] for k in range(N_CHIPS)], axis=1)
    mod = lax.dynamic_slice(mod_all, (dev * nb, 0), (nb, N_MOD * d)).reshape(nb, N_MOD, d)
    cw_full = jnp.concatenate([parts[2 * k, N_DEV * nb:, :conv_w.shape[2]] for k in range(N_CHIPS)], axis=1)

    chip_arr = jnp.reshape(chip, (1,)).astype(jnp.int32)
    b_gu1 = _cast_into(ffn1_w_gate_up[0], chip_arr, True, name="cast_gu1")
    b_d1 = _cast_into(ffn1_w_down[0], chip_arr, False, name="cast_d1")
    b_in = _cast_into(w_in[0].T, chip_arr, False, name="cast_in")
    b_out = _cast_into(w_out[0], chip_arr, False, name="cast_out")
    b_gu2 = _cast_into(ffn2_w_gate_up[0], chip_arr, True, name="cast_gu2")
    b_d2 = _cast_into(ffn2_w_down[0], chip_arr, False, name="cast_d2")
    n_gu, n_d, n_in, n_out = (ffn1_w_gate_up.shape[2], ffn1_w_down.shape[1], w_in.shape[2], w_out.shape[1])

    def whole(buf, col_kind):
        return (buf, col_kind, 0, buf.shape[0] // (2 if col_kind else 2 * N_CHIPS))

    gu_cuts = [0, 176, 352, d // 2]
    gu_part = lambda buf, s: (buf, True, gu_cuts[s], gu_cuts[s + 1] - gu_cuts[s])

    (wgu1,) = _comm_call(_GatherJob([whole(b_gu1, True)]), name="gather_gu1")
    (h1, a1, gu1), (wd1, wout) = _ffn_up(x2, ln1, mod, wgu1, seq=seq, sc_idx=1, sh_idx=0, use_ln=False,
                                         name="ffn1_up", comm=_GatherJob([whole(b_d1, False), whole(b_out, False)]))
    (f1, xhat1, rstd1), (win_t,) = _ffn_down_ln(a1, wd1, x2, ln1, mod, seq=seq, gate_idx=2, use_ln=False,
                                                name="ffn1_down", comm=_GatherJob([whole(b_in, False)]))
    (h2, q, k, v, ubc), (b_gu2,) = _in_proj(
        xhat1, ln1, mod, win_t, cos_t, sa_t, sb_t, seq=seq, sc_idx=4, sh_idx=3, name="in_proj",
        comm=_GatherJob([gu_part(b_gu2, 0)]))
    attn, (b_gu2,) = _attention(q, k, v, sinks, seq=seq, name="attention", comm=_GatherJob([gu_part(b_gu2, 1)]))
    (mixin, mix, xhat2, rstd2), (wgu2,) = _out_proj(
        attn, ubc, cw_full, wout, xhat1, ln1, mod, seq=seq, gate_idx=5, name="out_proj",
        comm=_GatherJob([gu_part(b_gu2, 2)]))
    (h3, a3, gu3), (wd2,) = _ffn_up(xhat2, ln2, mod, wgu2, seq=seq, sc_idx=7, sh_idx=6, use_ln=True, name="ffn2_up",
                                    comm=_GatherJob([whole(b_d2, False)]))
    dr3, df3, loss_cols, dln3g, dln3b, dgate3 = _ffn_down_loss(
        a3, wd2, xhat2, ln2, mod, ln3, tgt2, seq=seq, gate_idx=8, name="ffn2_down_loss")

    def pair_sum(g, r3, col_kind, name_, comm=None):
        if col_kind:
            g3 = g.reshape(2, g.shape[0] // 2, g.shape[1])
            blk_of = lambda p_, pos_: pos_[2]
        else:
            g3 = g.reshape(2 * N_CHIPS, g.shape[0] // (2 * N_CHIPS), g.shape[1])
            blk_of = lambda p_, pos_: 2 * p_ + pos_[2]
        return _sum_pair(pos, g3, r3, blk_of, name=name_, comm=comm)

    dgu3 = _ffn_bwd_act(df3, wd2, gu3, seq=seq, name="ffn2_bwd_act")
    g_wd2 = _matmul_tn(a3, df3, tmm=f // 2, tnn=d, name="grad_wd2")
    g_wgu2, (sib_d2,) = _matmul_tn(h3, dgu3, tmm=d, tnn=(2 * f) // 4, name="grad_wgu2",
                                   comm=_SwapJob([g_wd2], [False]))
    s32_d2, s16_d2 = pair_sum(g_wd2, sib_d2, False, "sum_pair_d2")
    (dr2, dmix, dsc3, dsh3, dgate2, dln2g, dln2b), (sib_gu2, recv_d2) = _bwd_in(
        dgu3, wgu2, dr3, xhat2, rstd2, ln2, mod, mix, seq=seq, w_is_nt=True, sc_idx=7, gate_idx=5,
        branch_scale=1.0, final=False, name="ffn2_bwd_in",
        comm=_MultiJob([_SwapJob([g_wgu2], [True]), _ExchangeJob([s16_d2], [False], [n_d])]))
    s32_gu2, s16_gu2 = pair_sum(g_wgu2, sib_gu2, True, "sum_pair_gu2")
    g_wout = _matmul_tn(mixin, dmix, tmm=d, tnn=d, name="grad_wout")
    dmixin = _matmul_nt_bf16(dmix, wout, seq=seq, name="out_proj_bwd")
    (dq, dkp, dkc, dvp, dvc, dsink), (recv_gu2, sib_out) = _attention_bwd(
        q, k, v, dmixin, sinks, seq=seq, name="attention_bwd",
        comm=_MultiJob([_ExchangeJob([s16_gu2], [True], [n_gu]), _SwapJob([g_wout], [False])]))
    s32_out, s16_out = pair_sum(g_wout, sib_out, False, "sum_pair_out")
    (dproj, dcw), (recv_out,) = _mix_bwd_assemble(
        dq, dkp, dkc, dvp, dvc, cos_t, sa_t, sb_t, dmixin, ubc, cw_full, seq=seq, name="mix_bwd",
        comm=_ExchangeJob([s16_out], [False], [n_out]))
    g_win_t = _matmul_tn(dproj, h2, tmm=IN_WIDTH // 2, tnn=d, name="grad_win")
    (dr1, df1, dsc2, dsh2, dgate1, dln1g, dln1b), (sib_in,) = _bwd_in(
        dproj, win_t, dr2, xhat1, rstd1, ln1, mod, f1, seq=seq, w_is_nt=False, sc_idx=4, gate_idx=2,
        branch_scale=0.5, final=False, name="in_proj_bwd", comm=_SwapJob([g_win_t], [False]))
    s32_in, s16_in = pair_sum(g_win_t, sib_in, False, "sum_pair_in")
    g_wd1, (recv_in,) = _matmul_tn(a1, df1, tmm=f // 2, tnn=d, name="grad_wd1",
                                   comm=_ExchangeJob([s16_in], [False], [n_in]))
    dgu1, (sib_d1,) = _ffn_bwd_act(df1, wd1, gu1, seq=seq, name="ffn1_bwd_act", comm=_SwapJob([g_wd1], [False]))
    s32_d1, s16_d1 = pair_sum(g_wd1, sib_d1, False, "sum_pair_d1")
    g_wgu1, (recv_d1,) = _matmul_tn(h1, dgu1, tmm=d, tnn=(2 * f) // 4, name="grad_wgu1",
                                    comm=_ExchangeJob([s16_d1], [False], [n_d]))

    def final_half(s32_, recv_, col_kind, n_shard, name_, comm=None):
        return _sum_final(pos, s32_, recv_, col_kind=col_kind, n_shard=n_shard, name=name_, comm=comm)

    half_gu2, (sib_gu1,) = final_half(s32_gu2, recv_gu2, True, n_gu, "sum_final_gu2", comm=_SwapJob([g_wgu1], [True]))
    early = [half_gu2,
             final_half(s32_d2, recv_d2, False, n_d, "sum_final_d2"),
             final_half(s32_out, recv_out, False, n_out, "sum_final_out"),
             final_half(s32_in, recv_in, False, n_in, "sum_final_in"),
             final_half(s32_d1, recv_d1, False, n_d, "sum_final_d1")]
    s32_gu1, s16_gu1 = pair_sum(g_wgu1, sib_gu1, True, "sum_pair_gu1")
    (grad_x, dsc1, dsh1), (recv_gu1, full_gu2, full_d2, full_out, full_in, full_d1) = _bwd_in(
        dgu1, wgu1, dr1, x2, None, None, mod, None, seq=seq, w_is_nt=True, sc_idx=1, gate_idx=None,
        branch_scale=None, final=True, name="ffn1_bwd_in",
        comm=_MultiJob([_ExchangeJob([s16_gu1], [True], [n_gu]), _ShareJob(early)]))
    late = [final_half(s32_gu1, recv_gu1, True, n_gu, "sum_final_gu1")]

    dmod = jnp.concatenate([dsh1, dsc1, dgate1, dsh2, dsc2, dgate2, dsh3, dsc3, dgate3], axis=1)
    loss_row = jnp.sum(loss_cols, axis=1, keepdims=True) * (0.5 / d)
    lane_row = lambda a: jnp.pad(a, ((0, 0), (0, d - a.shape[1])))
    block = jnp.concatenate(
        [dmod.reshape(nb * N_MOD, d), dln1g, dln1b, dln2g, dln2b, dln3g, dln3b,
         lane_row(dcw[0:3, :]), lane_row(dsink[:, 0:1].reshape(1, N_Q_HEADS)), lane_row(loss_row)], axis=0)
    block = jnp.pad(block, ((0, SMALL_ROWS - block.shape[0]), (0, 0)))
    gathered, (full_gu1,) = _allgather8(block, name="gather_small", comm=_ShareJob(late))
    gathered = gathered.reshape(N_DEV, SMALL_ROWS, d)
    dmod_all = gathered[:, :nb * N_MOD, :].reshape(N_DEV * nb, N_MOD * d)
    dmod_shard = lax.dynamic_slice(dmod_all, (0, chip * n_ada), (N_DEV * nb, n_ada))
    small, g_w_ada, g_b_ada = _small_finish(gathered, dmod_all, dmod_shard, c_all.T, name="small_finish")
    r0 = nb * N_MOD
    loss = small[r0 + 10, 0]
    g_ln = [small[r0 + i:r0 + i + 1, :] for i in range(6)]
    g_cw_full = small[r0 + 6:r0 + 9, :CONV_WIDTH]
    g_conv = lax.dynamic_slice(g_cw_full, (0, chip * conv_w.shape[2]), (3, conv_w.shape[2]))
    g_sinks = small[r0 + 9:r0 + 10, :N_Q_HEADS]

    def flat2(a):
        return a.reshape(-1, a.shape[-1])

    def unhalve(a):
        return a.reshape(2 * a.shape[1], a.shape[2])

    results = {}

    def adamw(name_, w_, g_, m_, v_):
        g2 = flat2(g_)
        dl, nm, nv = _adamw(flat2(w_), g2, flat2(m_), flat2(v_), name="adamw_" + name_)
        results[name_] = tuple(a.reshape(w_.shape) for a in (g2, dl, nm, nv))

    adamw("w_ada", w_ada, g_w_ada, m_w_ada, v_w_ada)
    adamw("ffn2_w_gate_up", ffn2_w_gate_up, unhalve(full_gu2), m_ffn2_w_gate_up, v_ffn2_w_gate_up)
    adamw("ffn2_w_down", ffn2_w_down, unhalve(full_d2), m_ffn2_w_down, v_ffn2_w_down)
    adamw("w_out", w_out, unhalve(full_out), m_w_out, v_w_out)
    adamw("w_in", w_in, unhalve(full_in).T, m_w_in, v_w_in)
    adamw("ffn1_w_gate_up", ffn1_w_gate_up, unhalve(full_gu1), m_ffn1_w_gate_up, v_ffn1_w_gate_up)
    adamw("ffn1_w_down", ffn1_w_down, unhalve(full_d1), m_ffn1_w_down, v_ffn1_w_down)
    adamw("b_ada", b_ada, g_b_ada, m_b_ada, v_b_ada)
    adamw("ln1_g", ln1_g, g_ln[0], m_ln1_g, v_ln1_g)
    adamw("ln1_b", ln1_b, g_ln[1], m_ln1_b, v_ln1_b)
    adamw("ln2_g", ln2_g, g_ln[2], m_ln2_g, v_ln2_g)
    adamw("ln2_b", ln2_b, g_ln[3], m_ln2_b, v_ln2_b)
    adamw("ln3_g", ln3_g, g_ln[4], m_ln3_g, v_ln3_g)
    adamw("ln3_b", ln3_b, g_ln[5], m_ln3_b, v_ln3_b)
    adamw("conv_w", conv_w, g_conv, m_conv_w, v_conv_w)
    adamw("attn_sinks", attn_sinks, g_sinks, m_attn_sinks, v_attn_sinks)
    order = ["w_ada", "b_ada", "ffn1_w_gate_up", "ffn1_w_down", "ln1_g", "ln1_b", "w_in", "conv_w", "attn_sinks",
             "w_out", "ln2_g", "ln2_b", "ffn2_w_gate_up", "ffn2_w_down", "ln3_g", "ln3_b"]
    return (loss, grad_x.reshape(x.shape), *[results[n_][0] for n_ in order], *[results[n_][1] for n_ in order],
            *[results[n_][2] for n_ in order], *[results[n_][3] for n_ in order])
```

```python
import jax
import jax.numpy as jnp
from jax import lax
from jax.experimental import pallas as pl
from jax.experimental.pallas import tpu as pltpu

F32 = jnp.float32
BF16 = jnp.bfloat16
MESH = pl.DeviceIdType.MESH

D_MODEL = 1024
HEAD_DIM = 64
ATTN_WIDTH = 512
CONV_WIDTH = 512
N_Q_HEADS = 8
N_KV_HEADS = 2
GQA_GROUP = 4
KV_WIDTH = 128
WINDOW = 128
BLOCK = 128
ROT_DIM = 16
ROPE_THETA = 500000.0
N_MOD = 9
LN_EPS = 1e-5
DN_ALPHA = 2.0 ** 0.25
IN_WIDTH = 2304
N_CHIPS = 4
N_DEV = 8
SMALL_ROWS = 32

ADAM_LR = 0.001
ADAM_B1 = 0.9
ADAM_B2 = 0.999
ADAM_EPS = 1e-08
ADAM_WD = 0.01
ADAM_STEP = 10

LANE = 128
COL_CHUNK = 256
VMEM_LIMIT = 56 * 1024 * 1024


def _params(sem=None, vmem=True):
    return pltpu.CompilerParams(dimension_semantics=sem, vmem_limit_bytes=VMEM_LIMIT if vmem else None)


def _sigmoid(g):
    return 0.5 * jnp.tanh(0.5 * g) + 0.5


def _row_sum(v):
    return jnp.sum(v, axis=0, keepdims=True)


ROW_CHUNK = 16
EPILOGUE_UNROLL = 8


def _fold8(v):
    return v[0:8, :] + v[8:16, :]


def _row_chunk_loop(n_rows, step, init):
    per_iter = ROW_CHUNK * EPILOGUE_UNROLL
    assert n_rows % per_iter == 0, n_rows

    def body(it, carry):
        for s in range(EPILOGUE_UNROLL):
            start = pl.multiple_of(it * per_iter + s * ROW_CHUNK, ROW_CHUNK)
            carry = step(pl.ds(start, ROW_CHUNK), carry)
        return carry

    return lax.fori_loop(0, n_rows // per_iter, body, init)


def _ln_stats(r):
    mu = jnp.mean(r, axis=-1, keepdims=True)
    rc = r - mu
    var = jnp.mean(rc * rc, axis=-1, keepdims=True)
    rstd = lax.rsqrt(var + LN_EPS)
    return rc * rstd, rstd


def _ln_bwd(dxo, xhat, rstd, g):
    dxhat = dxo * g
    m1 = jnp.mean(dxhat, axis=-1, keepdims=True)
    m2 = jnp.mean(dxhat * xhat, axis=-1, keepdims=True)
    return rstd * (dxhat - m1 - xhat * m2)


def _dot_nt(a, b):
    return lax.dot_general(a, b, (((1,), (1,)), ((), ())), preferred_element_type=F32)


def _dot_tn(a, b):
    return lax.dot_general(a, b, (((0,), (0,)), ((), ())), preferred_element_type=F32)


def _full(shape):
    nd = len(shape)
    return pl.BlockSpec(shape, lambda *_: (0,) * nd)


def _resident(shape):
    nd = len(shape)
    return pl.BlockSpec(shape, lambda *_: (0,) * nd, pipeline_mode=pl.Buffered(1))


ANY_SPEC = pl.BlockSpec(memory_space=pl.ANY)


def _pcall(body, *, name, grid, in_specs, out_specs, out_shape, args, scratch_shapes=(), comm=None, prefetch=None):
    single = not isinstance(out_shape, (list, tuple))
    out_specs = [out_specs] if single else list(out_specs)
    out_shape = [out_shape] if single else list(out_shape)
    in_specs = list(in_specs)
    scratch_shapes = list(scratch_shapes)
    sem = ("arbitrary",) * len(grid)
    n_pre = 0 if prefetch is None else 1
    pre_args = () if prefetch is None else (prefetch,)

    def call(fn, ins_, outs_, shapes_, scratch_, aliases_, operands):
        if prefetch is None:
            return pl.pallas_call(fn, name=name, grid=grid, in_specs=ins_, out_specs=outs_, out_shape=shapes_,
                                  scratch_shapes=scratch_, input_output_aliases=aliases_,
                                  compiler_params=_params(sem))(*operands)
        spec = pltpu.PrefetchScalarGridSpec(num_scalar_prefetch=1, grid=grid, in_specs=ins_, out_specs=outs_,
                                            scratch_shapes=scratch_)
        return pl.pallas_call(fn, name=name, grid_spec=spec, out_shape=shapes_,
                              input_output_aliases={n_pre + i: o for i, o in aliases_.items()},
                              compiler_params=_params(sem))(*pre_args, *operands)

    if comm is None:
        res = call(body, in_specs, out_specs, out_shape, scratch_shapes, {}, args)
        return res[0] if single else res
    n_in, n_out, n_scr = len(in_specs), len(out_specs), len(scratch_shapes)
    nci, nco = len(comm.inputs), len(comm.out_shapes)
    n_steps = 1
    for g in grid:
        n_steps *= g
    staged = n_steps >= 4
    middle_step = n_steps - 1 - max(1, n_steps // 8)

    def wrapped(*refs):
        pre, refs = refs[:n_pre], refs[n_pre:]
        ins, refs = refs[:n_in], refs[n_in:]
        cin, refs = refs[:nci], refs[nci:]
        outs, refs = refs[:n_out], refs[n_out:]
        cout, refs = refs[:nco], refs[nco:]
        scr, csems = refs[:n_scr], refs[n_scr:]
        step = pl.program_id(0)
        for ax in range(1, len(grid)):
            step = step * grid[ax] + pl.program_id(ax)

        @pl.when(step == 0)
        def _():
            comm.start(cin, cout, csems)

        body(*pre, *ins, *outs, *scr)

        if staged:
            @pl.when(step == middle_step)
            def _():
                comm.middle(cin, cout, csems)

        @pl.when(step == n_steps - 1)
        def _():
            if not staged:
                comm.middle(cin, cout, csems)
            comm.finish(cin, cout, csems)

    res = call(wrapped, in_specs + [ANY_SPEC] * nci, out_specs + [ANY_SPEC] * nco,
               out_shape + list(comm.out_shapes), scratch_shapes + list(comm.sems),
               {n_in + i: n_out + o for i, o in comm.aliases.items()}, (*args, *comm.inputs))
    main = res[:n_out]
    return (main[0] if single else main), list(res[n_out:])


def _comm_call(job, *, name):
    nci, nco = len(job.inputs), len(job.out_shapes)

    def body(*refs):
        cin, refs = refs[:nci], refs[nci:]
        cout, csems = refs[:nco], refs[nco:]
        job.start(cin, cout, csems)
        job.middle(cin, cout, csems)
        job.finish(cin, cout, csems)

    return pl.pallas_call(
        body, name=name, out_shape=list(job.out_shapes), in_specs=[ANY_SPEC] * nci, out_specs=[ANY_SPEC] * nco,
        scratch_shapes=list(job.sems), input_output_aliases=dict(job.aliases))(*job.inputs)


def _ffn_up(xin, lnp, mod, w, *, seq, sc_idx, sh_idx, use_ln, name, comm=None):
    t, d = xin.shape
    f = w.shape[1] // 2
    tm = min(512, seq)
    tpb = seq // tm
    ch = min(COL_CHUNK, f)

    def body(x_ref, ln_ref, mod_ref, w_ref, h_ref, a_ref, gu_ref):
        x = x_ref[...]
        if use_ln:
            x = x * ln_ref[0:1, :] + ln_ref[1:2, :]
        h = x * (1.0 + mod_ref[0, sc_idx:sc_idx + 1, :]) + mod_ref[0, sh_idx:sh_idx + 1, :]
        hb = h.astype(BF16)
        h_ref[...] = hb
        for j in range(f // ch):
            g = jnp.dot(hb, w_ref[:, j * ch:(j + 1) * ch], preferred_element_type=F32)
            u = jnp.dot(hb, w_ref[:, f + j * ch:f + (j + 1) * ch], preferred_element_type=F32)
            s = _sigmoid(g)
            silu = g * s
            a_ref[:, j * ch:(j + 1) * ch] = (silu * u).astype(BF16)
            gu_ref[:, j * ch:(j + 1) * ch] = (u * (s + silu * (1.0 - s))).astype(BF16)
            gu_ref[:, f + j * ch:f + (j + 1) * ch] = silu.astype(BF16)

    return _pcall(
        body, name=name, grid=(t // tm,),
        in_specs=[pl.BlockSpec((tm, d), lambda i: (i, 0)), _full((2, d)),
                  pl.BlockSpec((1, N_MOD, d), lambda i: (i // tpb, 0, 0)), _resident((d, 2 * f))],
        out_specs=[pl.BlockSpec((tm, d), lambda i: (i, 0)), pl.BlockSpec((tm, f), lambda i: (i, 0)),
                   pl.BlockSpec((tm, 2 * f), lambda i: (i, 0))],
        out_shape=[jax.ShapeDtypeStruct((t, d), BF16), jax.ShapeDtypeStruct((t, f), BF16),
                   jax.ShapeDtypeStruct((t, 2 * f), BF16)],
        args=(xin, lnp, mod, w), comm=comm)


def _ffn_down_ln(a, wd, xin, lnp_in, mod, *, seq, gate_idx, use_ln, name, comm=None):
    t, f = a.shape
    d = wd.shape[1]
    tm = min(512, seq)
    tpb = seq // tm

    def body(a_ref, wd_ref, x_ref, ln_ref, mod_ref, f_ref, xhat_ref, rstd_ref, acc):
        av = a_ref[...]
        for j in range(d // COL_CHUNK):
            acc[:, j * COL_CHUNK:(j + 1) * COL_CHUNK] = jnp.dot(
                av, wd_ref[:, j * COL_CHUNK:(j + 1) * COL_CHUNK], preferred_element_type=F32)
        scale = 0.5 * (1.0 + mod_ref[0, gate_idx:gate_idx + 1, :])

        fo = acc[...]
        x = x_ref[...]
        if use_ln:
            x = x * ln_ref[0:1, :] + ln_ref[1:2, :]
        xhat, rstd = _ln_stats(DN_ALPHA * x + scale * fo)
        f_ref[...] = fo.astype(BF16)
        xhat_ref[...] = xhat
        rstd_ref[...] = rstd

    return _pcall(
        body, name=name, grid=(t // tm,),
        in_specs=[pl.BlockSpec((tm, f), lambda i: (i, 0)), _resident((f, d)),
                  pl.BlockSpec((tm, d), lambda i: (i, 0)), _full((2, d)),
                  pl.BlockSpec((1, N_MOD, d), lambda i: (i // tpb, 0, 0))],
        out_specs=[pl.BlockSpec((tm, d), lambda i: (i, 0)), pl.BlockSpec((tm, d), lambda i: (i, 0)),
                   pl.BlockSpec((tm, 1), lambda i: (i, 0))],
        out_shape=[jax.ShapeDtypeStruct((t, d), BF16), jax.ShapeDtypeStruct((t, d), F32),
                   jax.ShapeDtypeStruct((t, 1), F32)],
        scratch_shapes=[pltpu.VMEM((tm, d), F32)],
        args=(a, wd, xin, lnp_in, mod), comm=comm)


def _ffn_down_loss(a, wd, xhat_in, lnp_in, mod, lnp_out, tgt, *, seq, gate_idx, name):
    t, f = a.shape
    d = wd.shape[1]
    nb = t // seq
    tm = min(512, seq)
    tpb = seq // tm

    def body(a_ref, wd_ref, x_ref, lnin_ref, mod_ref, lnout_ref, tgt_ref,
             dr_ref, df_ref, loss_ref, dg_ref, db_ref, dgate_ref, acc):
        i = pl.program_id(0)
        av = a_ref[...]
        for j in range(d // COL_CHUNK):
            acc[:, j * COL_CHUNK:(j + 1) * COL_CHUNK] = jnp.dot(
                av, wd_ref[:, j * COL_CHUNK:(j + 1) * COL_CHUNK], preferred_element_type=F32)
        scale = 0.5 * (1.0 + mod_ref[0, gate_idx:gate_idx + 1, :])
        g_in, b_in = lnin_ref[0:1, :], lnin_ref[1:2, :]
        g_out, b_out = lnout_ref[0:1, :], lnout_ref[1:2, :]

        def chunk(rows, carry):
            s_loss, s_dg, s_db, s_gate = carry
            fo = acc[rows, :]
            xhat, rstd = _ln_stats(DN_ALPHA * (x_ref[rows, :] * g_in + b_in) + scale * fo)
            e = xhat * g_out + b_out - tgt_ref[rows, :]
            dy = e * (1.0 / d)
            dr = _ln_bwd(dy, xhat, rstd, g_out)
            dr_ref[rows, :] = dr
            df_ref[rows, :] = (scale * dr).astype(BF16)
            return (s_loss + _fold8(e * e), s_dg + _fold8(dy * xhat), s_db + _fold8(dy),
                    s_gate + _fold8(0.5 * fo * dr))

        zero = jnp.zeros((8, d), F32)
        s_loss, s_dg, s_db, s_gate = _row_chunk_loop(tm, chunk, (zero, zero, zero, zero))

        @pl.when(i == 0)
        def _():
            loss_ref[...] = jnp.zeros_like(loss_ref)
            dg_ref[...] = jnp.zeros_like(dg_ref)
            db_ref[...] = jnp.zeros_like(db_ref)

        @pl.when(i % tpb == 0)
        def _():
            dgate_ref[...] = jnp.zeros_like(dgate_ref)

        loss_ref[...] += _row_sum(s_loss)
        dg_ref[...] += _row_sum(s_dg)
        db_ref[...] += _row_sum(s_db)
        dgate_ref[0] += _row_sum(s_gate)

    return pl.pallas_call(
        body, name=name, grid=(t // tm,), scratch_shapes=[pltpu.VMEM((tm, d), F32)],
        in_specs=[pl.BlockSpec((tm, f), lambda i: (i, 0)), _resident((f, d)),
                  pl.BlockSpec((tm, d), lambda i: (i, 0)), _full((2, d)),
                  pl.BlockSpec((1, N_MOD, d), lambda i: (i // tpb, 0, 0)), _full((2, d)),
                  pl.BlockSpec((tm, d), lambda i: (i, 0))],
        out_specs=[pl.BlockSpec((tm, d), lambda i: (i, 0)), pl.BlockSpec((tm, d), lambda i: (i, 0)),
                   _full((1, d)), _full((1, d)), _full((1, d)),
                   pl.BlockSpec((1, 1, d), lambda i: (i // tpb, 0, 0))],
        out_shape=[jax.ShapeDtypeStruct((t, d), F32), jax.ShapeDtypeStruct((t, d), BF16),
                   jax.ShapeDtypeStruct((1, d), F32), jax.ShapeDtypeStruct((1, d), F32),
                   jax.ShapeDtypeStruct((1, d), F32), jax.ShapeDtypeStruct((nb, 1, d), F32)],
        compiler_params=_params(("arbitrary",)),
    )(a, wd, xhat_in, lnp_in, mod, lnp_out, tgt)


def _rope(v, cos, sa, sb):
    return v * cos + pltpu.roll(v, LANE - ROT_DIM // 2, 1) * sa + pltpu.roll(v, ROT_DIM // 2, 1) * sb


def _rope_t(dy, cos, sa, sb):
    return dy * cos + pltpu.roll(dy * sa, ROT_DIM // 2, 1) + pltpu.roll(dy * sb, LANE - ROT_DIM // 2, 1)


def _in_proj(xhat, lnp, mod, w_t, cos, sa, sb, *, seq, sc_idx, sh_idx, name, comm=None):
    t, d = xhat.shape
    tm = min(512, seq)
    tpb = seq // tm
    n_conv = 3 * CONV_WIDTH

    def body(x_ref, ln_ref, mod_ref, w_ref, cos_ref, sa_ref, sb_ref, h_ref, q_ref, k_ref, v_ref, ubc_ref):
        x = x_ref[...] * ln_ref[0:1, :] + ln_ref[1:2, :]
        h = x * (1.0 + mod_ref[0, sc_idx:sc_idx + 1, :]) + mod_ref[0, sh_idx:sh_idx + 1, :]
        hb = h.astype(BF16)
        h_ref[...] = hb
        cos_t, sa_t, sb_t = cos_ref[...], sa_ref[...], sb_ref[...]
        for j in range(ATTN_WIDTH // COL_CHUNK):
            p = _dot_nt(hb, w_ref[j * COL_CHUNK:(j + 1) * COL_CHUNK, :])
            for s in range(COL_CHUNK // LANE):
                q_ref[:, j * COL_CHUNK + s * LANE:j * COL_CHUNK + (s + 1) * LANE] = _rope(
                    p[:, s * LANE:(s + 1) * LANE], cos_t, sa_t, sb_t).astype(BF16)
        p = _dot_nt(hb, w_ref[ATTN_WIDTH:ATTN_WIDTH + 2 * KV_WIDTH, :])
        k_ref[...] = _rope(p[:, 0:KV_WIDTH], cos_t, sa_t, sb_t).astype(BF16)
        v_ref[...] = p[:, KV_WIDTH:].astype(BF16)
        base = ATTN_WIDTH + 2 * KV_WIDTH
        for j in range(n_conv // COL_CHUNK):
            ubc_ref[:, j * COL_CHUNK:(j + 1) * COL_CHUNK] = _dot_nt(
                hb, w_ref[base + j * COL_CHUNK:base + (j + 1) * COL_CHUNK, :])

    row = lambda w: pl.BlockSpec((tm, w), lambda i: (i, 0))
    return _pcall(
        body, name=name, grid=(t // tm,),
        in_specs=[row(d), _full((2, d)), pl.BlockSpec((1, N_MOD, d), lambda i: (i // tpb, 0, 0)),
                  _resident((IN_WIDTH, d)), row(LANE), row(LANE), row(LANE)],
        out_specs=[row(d), row(ATTN_WIDTH), row(KV_WIDTH), row(KV_WIDTH), row(n_conv)],
        out_shape=[jax.ShapeDtypeStruct((t, d), BF16), jax.ShapeDtypeStruct((t, ATTN_WIDTH), BF16),
                   jax.ShapeDtypeStruct((t, KV_WIDTH), BF16), jax.ShapeDtypeStruct((t, KV_WIDTH), BF16),
                   jax.ShapeDtypeStruct((t, n_conv), F32)],
        args=(xhat, lnp, mod, w_t, cos, sa, sb), comm=comm)


def _attn_group(q_ref, kp_ref, kc_ref, vp_ref, vc_ref, sink_ref, g, first):
    lo, hi = g * HEAD_DIM, (g + 1) * HEAD_DIM
    kk = jnp.concatenate([kp_ref[:, lo:hi], kc_ref[:, lo:hi]], axis=0)
    vv = jnp.concatenate([vp_ref[:, lo:hi], vc_ref[:, lo:hi]], axis=0)
    qs = jnp.concatenate([q_ref[:, (GQA_GROUP * g + j) * HEAD_DIM:(GQA_GROUP * g + j + 1) * HEAD_DIM]
                          for j in range(GQA_GROUP)], axis=0)
    rows = GQA_GROUP * BLOCK
    row = lax.broadcasted_iota(jnp.int32, (rows, 2 * BLOCK), 0)
    ki = lax.broadcasted_iota(jnp.int32, (rows, 2 * BLOCK), 1)
    diff = (row & (BLOCK - 1)) + BLOCK - ki
    valid = (diff >= 0) & (diff < WINDOW) & ((ki >= BLOCK) | jnp.logical_not(first))
    s = _dot_nt(qs, kk) * (HEAD_DIM ** -0.5)
    s = jnp.where(valid, s, -1e30)
    rcol = lax.broadcasted_iota(jnp.int32, (rows, 1), 0)
    sink = jnp.zeros((rows, 1), F32)
    for j in range(GQA_GROUP):
        sink = jnp.where(rcol // BLOCK == j, sink_ref[GQA_GROUP * g + j], sink)
    m = jnp.maximum(jnp.max(s, axis=1, keepdims=True), sink)
    p = jnp.exp(s - m)
    ps = jnp.exp(sink - m)
    inv = 1.0 / (jnp.sum(p, axis=1, keepdims=True) + ps)
    return qs, kk, vv, p * inv, ps * inv


def _attention(q, k, v, sinks, *, seq, name, comm=None):
    t = q.shape[0]
    nblk = seq // BLOCK

    def body(q_ref, kp_ref, kc_ref, vp_ref, vc_ref, sink_ref, o_ref):
        first = (pl.program_id(0) % nblk) == 0
        outs = []
        for g in range(N_KV_HEADS):
            _, _, vv, pn, _ = _attn_group(q_ref, kp_ref, kc_ref, vp_ref, vc_ref, sink_ref, g, first)
            o = jnp.dot(pn.astype(BF16), vv, preferred_element_type=F32)
            outs += [o[j * BLOCK:(j + 1) * BLOCK, :] for j in range(GQA_GROUP)]
        o_ref[...] = jnp.concatenate(outs, axis=1).astype(BF16)

    cur = lambda w: pl.BlockSpec((BLOCK, w), lambda n: (n, 0))
    prev = lambda w: pl.BlockSpec((BLOCK, w), lambda n: (jnp.maximum(n - 1, 0), 0))
    return _pcall(
        body, name=name, grid=(t // BLOCK,),
        in_specs=[cur(ATTN_WIDTH), prev(KV_WIDTH), cur(KV_WIDTH), prev(KV_WIDTH), cur(KV_WIDTH),
                  pl.BlockSpec(memory_space=pltpu.SMEM)],
        out_specs=cur(ATTN_WIDTH),
        out_shape=jax.ShapeDtypeStruct((t, ATTN_WIDTH), BF16),
        args=(q, k, k, v, v, sinks), comm=comm)


def _out_proj(attn, ubc, cw, wout, xhat_in, lnp_in, mod, *, seq, gate_idx, name, comm=None):
    t, d = xhat_in.shape
    tm = min(512, seq)
    tpb = seq // tm
    cwid = CONV_WIDTH

    def body(attn_ref, ubc_ref, halo_ref, cw_ref, w_ref, x_ref, ln_ref, mod_ref,
             mixin_ref, mix_ref, xhat_ref, rstd_ref, zbuf, acc):
        first = (pl.program_id(0) % tpb) == 0
        u, bg, cg = ubc_ref[:, 0:cwid], ubc_ref[:, cwid:2 * cwid], ubc_ref[:, 2 * cwid:3 * cwid]
        z = cg * u
        hz = halo_ref[:, 2 * cwid:3 * cwid] * halo_ref[:, 0:cwid]
        zbuf[0:8, :] = jnp.where(first, 0.0, hz)
        zbuf[8:8 + tm, :] = z
        y = cw_ref[0:1, :] * zbuf[6:6 + tm, :] + cw_ref[1:2, :] * zbuf[7:7 + tm, :] + cw_ref[2:3, :] * z
        mixin_ref[:, 0:ATTN_WIDTH] = attn_ref[...]
        mixin_ref[:, ATTN_WIDTH:] = (bg * y).astype(BF16)
        mv = mixin_ref[...]
        for j in range(d // COL_CHUNK):
            acc[:, j * COL_CHUNK:(j + 1) * COL_CHUNK] = jnp.dot(
                mv, w_ref[:, j * COL_CHUNK:(j + 1) * COL_CHUNK], preferred_element_type=F32)
        scale = 1.0 + mod_ref[0, gate_idx:gate_idx + 1, :]

        mix = acc[...]
        xhat, rstd = _ln_stats(DN_ALPHA * (x_ref[...] * ln_ref[0:1, :] + ln_ref[1:2, :]) + scale * mix)
        mix_ref[...] = mix.astype(BF16)
        xhat_ref[...] = xhat
        rstd_ref[...] = rstd

    row = lambda w: pl.BlockSpec((tm, w), lambda i: (i, 0))
    return _pcall(
        body, name=name, grid=(t // tm,),
        in_specs=[row(ATTN_WIDTH), row(3 * cwid),
                  pl.BlockSpec((8, 3 * cwid), lambda i: (jnp.maximum(i * (tm // 8) - 1, 0), 0)),
                  _full((8, cwid)), _resident((d, d)), row(d), _full((2, d)),
                  pl.BlockSpec((1, N_MOD, d), lambda i: (i // tpb, 0, 0))],
        out_specs=[row(d), row(d), row(d), row(1)],
        out_shape=[jax.ShapeDtypeStruct((t, d), BF16), jax.ShapeDtypeStruct((t, d), BF16),
                   jax.ShapeDtypeStruct((t, d), F32), jax.ShapeDtypeStruct((t, 1), F32)],
        scratch_shapes=[pltpu.VMEM((tm + 8, cwid), F32), pltpu.VMEM((tm, d), F32)],
        args=(attn, ubc, ubc, cw, wout, xhat_in, lnp_in, mod), comm=comm)


def _ffn_bwd_act(df, wd, gu, *, seq, name, comm=None):
    t, d = df.shape
    f = wd.shape[0]
    tm = min(512, seq)
    ch = min(COL_CHUNK, f)

    def body(df_ref, wd_ref, gu_ref, dgu_ref):
        dfv = df_ref[...]
        for j in range(f // ch):
            da = _dot_nt(dfv, wd_ref[j * ch:(j + 1) * ch, :])
            dgu_ref[:, j * ch:(j + 1) * ch] = (da * gu_ref[:, j * ch:(j + 1) * ch].astype(F32)).astype(BF16)
            dgu_ref[:, f + j * ch:f + (j + 1) * ch] = (
                da * gu_ref[:, f + j * ch:f + (j + 1) * ch].astype(F32)).astype(BF16)

    return _pcall(
        body, name=name, grid=(t // tm,),
        in_specs=[pl.BlockSpec((tm, d), lambda i: (i, 0)), _resident((f, d)),
                  pl.BlockSpec((tm, 2 * f), lambda i: (i, 0))],
        out_specs=pl.BlockSpec((tm, 2 * f), lambda i: (i, 0)),
        out_shape=jax.ShapeDtypeStruct((t, 2 * f), BF16),
        args=(df, wd, gu), comm=comm)


def _bwd_in(a, w, dr, xin, rstd_prev, lnp_prev, mod, branch_prev, *, seq, w_is_nt, sc_idx, gate_idx,
            branch_scale, final, name, comm=None):
    t, kdim = a.shape
    d = dr.shape[1]
    nb = t // seq
    tm = min(512, seq)
    tpb = seq // tm

    def body(*refs):
        if final:
            a_ref, w_ref, dr_ref, x_ref, mod_ref, dx_ref, dsc_ref, dsh_ref, acc = refs
        else:
            (a_ref, w_ref, dr_ref, x_ref, rstd_ref, ln_ref, mod_ref, br_ref,
             drp_ref, dbr_ref, dsc_ref, dsh_ref, dgate_ref, dg_ref, db_ref, acc) = refs
        i = pl.program_id(0)
        av = a_ref[...]
        for j in range(d // COL_CHUNK):
            cols = slice(j * COL_CHUNK, (j + 1) * COL_CHUNK)
            acc[:, cols] = (_dot_nt(av, w_ref[cols, :]) if w_is_nt
                            else jnp.dot(av, w_ref[:, cols], preferred_element_type=F32))
        sc1 = 1.0 + mod_ref[0, sc_idx:sc_idx + 1, :]
        if not final:
            g_prev, b_prev = ln_ref[0:1, :], ln_ref[1:2, :]
            bscale = branch_scale * (1.0 + mod_ref[0, gate_idx:gate_idx + 1, :])

        def chunk(rows, carry):
            dh = acc[rows, :]
            dx = DN_ALPHA * dr_ref[rows, :] + dh * sc1
            if final:
                dx_ref[rows, :] = dx
                return carry[0] + _fold8(dh * x_ref[rows, :]), carry[1] + _fold8(dh)
            xhat = x_ref[rows, :]
            drp = _ln_bwd(dx, xhat, rstd_ref[rows, :], g_prev)
            drp_ref[rows, :] = drp
            dbr_ref[rows, :] = (bscale * drp).astype(BF16)
            return (carry[0] + _fold8(dh * (xhat * g_prev + b_prev)), carry[1] + _fold8(dh),
                    carry[2] + _fold8(branch_scale * br_ref[rows, :].astype(F32) * drp),
                    carry[3] + _fold8(dx * xhat), carry[4] + _fold8(dx))

        zero = jnp.zeros((8, d), F32)
        sums = _row_chunk_loop(tm, chunk, (zero,) * (2 if final else 5))

        @pl.when((i % tpb) == 0)
        def _():
            dsc_ref[...] = jnp.zeros_like(dsc_ref)
            dsh_ref[...] = jnp.zeros_like(dsh_ref)
            if not final:
                dgate_ref[...] = jnp.zeros_like(dgate_ref)

        dsc_ref[0] += _row_sum(sums[0])
        dsh_ref[0] += _row_sum(sums[1])
        if not final:
            @pl.when(i == 0)
            def _():
                dg_ref[...] = jnp.zeros_like(dg_ref)
                db_ref[...] = jnp.zeros_like(db_ref)

            dgate_ref[0] += _row_sum(sums[2])
            dg_ref[...] += _row_sum(sums[3])
            db_ref[...] += _row_sum(sums[4])

    row = lambda w_: pl.BlockSpec((tm, w_), lambda i: (i, 0))
    vec = pl.BlockSpec((1, 1, d), lambda i: (i // tpb, 0, 0))
    mod_spec = pl.BlockSpec((1, N_MOD, d), lambda i: (i // tpb, 0, 0))
    vshape = jax.ShapeDtypeStruct((nb, 1, d), F32)
    if final:
        in_specs = [row(kdim), _resident(w.shape), row(d), row(d), mod_spec]
        args = (a, w, dr, xin, mod)
        out_specs = [row(d), vec, vec]
        out_shape = [jax.ShapeDtypeStruct((t, d), F32), vshape, vshape]
    else:
        in_specs = [row(kdim), _resident(w.shape), row(d), row(d), row(1), _full((2, d)), mod_spec, row(d)]
        args = (a, w, dr, xin, rstd_prev, lnp_prev, mod, branch_prev)
        out_specs = [row(d), row(d), vec, vec, vec, _full((1, d)), _full((1, d))]
        out_shape = [jax.ShapeDtypeStruct((t, d), F32), jax.ShapeDtypeStruct((t, d), BF16), vshape, vshape, vshape,
                     jax.ShapeDtypeStruct((1, d), F32), jax.ShapeDtypeStruct((1, d), F32)]
    return _pcall(
        body, name=name, grid=(t // tm,), in_specs=in_specs, out_specs=out_specs, out_shape=out_shape,
        scratch_shapes=[pltpu.VMEM((tm, d), F32)], args=args, comm=comm)


def _matmul_tn(a, b, *, tmm, tnn, name, comm=None):
    t, m = a.shape
    n = b.shape[1]
    tk = min(2048, t)

    def body(a_ref, b_ref, o_ref):
        @pl.when(pl.program_id(2) == 0)
        def _():
            o_ref[...] = jnp.zeros_like(o_ref)
        o_ref[...] += _dot_tn(a_ref[...], b_ref[...])

    return _pcall(
        body, name=name, grid=(m // tmm, n // tnn, t // tk),
        in_specs=[pl.BlockSpec((tk, tmm), lambda i, j, k: (k, i)), pl.BlockSpec((tk, tnn), lambda i, j, k: (k, j))],
        out_specs=pl.BlockSpec((tmm, tnn), lambda i, j, k: (i, j)),
        out_shape=jax.ShapeDtypeStruct((m, n), F32),
        args=(a, b), comm=comm)


def _matmul_nt_bf16(a, w, *, seq, name):
    t, kdim = a.shape
    n = w.shape[0]
    tm = min(512, seq)

    def body(a_ref, w_ref, o_ref):
        av = a_ref[...]
        for j in range(n // COL_CHUNK):
            o_ref[:, j * COL_CHUNK:(j + 1) * COL_CHUNK] = _dot_nt(
                av, w_ref[j * COL_CHUNK:(j + 1) * COL_CHUNK, :]).astype(BF16)

    return pl.pallas_call(
        body, name=name, grid=(t // tm,),
        in_specs=[pl.BlockSpec((tm, kdim), lambda i: (i, 0)), _resident((n, kdim))],
        out_specs=pl.BlockSpec((tm, n), lambda i: (i, 0)),
        out_shape=jax.ShapeDtypeStruct((t, n), BF16),
        compiler_params=_params(("arbitrary",)),
    )(a, w)


def _attention_bwd(q, k, v, dmixin, sinks, *, seq, name, comm=None):
    t = q.shape[0]
    nblk = seq // BLOCK

    def body(q_ref, kp_ref, kc_ref, vp_ref, vc_ref, do_ref, sink_ref,
             dq_ref, dkp_ref, dkc_ref, dvp_ref, dvc_ref, dsink_ref):
        n = pl.program_id(0)
        first = (n % nblk) == 0

        @pl.when(n == 0)
        def _():
            dsink_ref[...] = jnp.zeros_like(dsink_ref)

        dqs, dks, dvs = [], [], []
        srow = lax.broadcasted_iota(jnp.int32, (8, LANE), 0)
        dsink = jnp.zeros((8, LANE), F32)
        for g in range(N_KV_HEADS):
            qs, kk, vv, pn, psn = _attn_group(q_ref, kp_ref, kc_ref, vp_ref, vc_ref, sink_ref, g, first)
            dos = jnp.concatenate([do_ref[:, (GQA_GROUP * g + j) * HEAD_DIM:(GQA_GROUP * g + j + 1) * HEAD_DIM]
                                   for j in range(GQA_GROUP)], axis=0)
            dp = _dot_nt(dos, vv)
            delta = jnp.sum(pn * dp, axis=1, keepdims=True)
            ds = pn * (dp - delta)
            dsk = psn * delta
            for j in range(GQA_GROUP):
                tot = jnp.sum(dsk[j * BLOCK:(j + 1) * BLOCK, :], axis=0, keepdims=True)
                dsink = dsink - jnp.where(srow == GQA_GROUP * g + j, tot, 0.0)
            dsb = (ds * (HEAD_DIM ** -0.5)).astype(BF16)
            dqg = jnp.dot(dsb, kk, preferred_element_type=F32)
            dqs += [dqg[j * BLOCK:(j + 1) * BLOCK, :] for j in range(GQA_GROUP)]
            dks.append(_dot_tn(dsb, qs))
            dvs.append(_dot_tn(pn.astype(BF16), dos))
        dsink_ref[...] += dsink
        dq_ref[...] = jnp.concatenate(dqs, axis=1)
        dkp_ref[...] = jnp.concatenate([x[0:BLOCK, :] for x in dks], axis=1)
        dkc_ref[...] = jnp.concatenate([x[BLOCK:, :] for x in dks], axis=1)
        dvp_ref[...] = jnp.concatenate([x[0:BLOCK, :] for x in dvs], axis=1)
        dvc_ref[...] = jnp.concatenate([x[BLOCK:, :] for x in dvs], axis=1)

    cur = lambda w: pl.BlockSpec((BLOCK, w), lambda n: (n, 0))
    prev = lambda w: pl.BlockSpec((BLOCK, w), lambda n: (jnp.maximum(n - 1, 0), 0))
    kv = jax.ShapeDtypeStruct((t, KV_WIDTH), F32)
    return _pcall(
        body, name=name, grid=(t // BLOCK,),
        in_specs=[cur(ATTN_WIDTH), prev(KV_WIDTH), cur(KV_WIDTH), prev(KV_WIDTH), cur(KV_WIDTH), cur(ATTN_WIDTH),
                  pl.BlockSpec(memory_space=pltpu.SMEM)],
        out_specs=[cur(ATTN_WIDTH), cur(KV_WIDTH), cur(KV_WIDTH), cur(KV_WIDTH), cur(KV_WIDTH), _full((8, LANE))],
        out_shape=[jax.ShapeDtypeStruct((t, ATTN_WIDTH), F32), kv, kv, kv, kv, jax.ShapeDtypeStruct((8, LANE), F32)],
        args=(q, k, k, v, v, dmixin, sinks), comm=comm)


def _mix_bwd_assemble(dq, dkp, dkc, dvp, dvc, cos, sa, sb, dmixin, ubc, cw, *, seq, name, comm=None):
    t = dq.shape[0]
    nblk = seq // BLOCK
    ntile = t // BLOCK
    cwid = CONV_WIDTH
    tm = BLOCK

    def body(dq_ref, dkc_ref, dkp_ref, dvc_ref, dvp_ref, cos_ref, sa_ref, sb_ref, dco_ref, dcon_ref,
             ubc_ref, hprev_ref, hnext_ref, cw_ref, dproj_ref, dcw_ref, zbuf, dybuf):
        i = pl.program_id(0)
        first = (i % nblk) == 0
        last = (i % nblk) == nblk - 1
        glast = i == ntile - 1

        @pl.when(i == 0)
        def _():
            dcw_ref[...] = jnp.zeros_like(dcw_ref)

        cos_t, sa_t, sb_t = cos_ref[...], sa_ref[...], sb_ref[...]
        for j in range(ATTN_WIDTH // LANE):
            dproj_ref[:, j * LANE:(j + 1) * LANE] = _rope_t(
                dq_ref[:, j * LANE:(j + 1) * LANE], cos_t, sa_t, sb_t).astype(BF16)
        dk = dkc_ref[...] + jnp.where(glast, 0.0, dkp_ref[...])
        dproj_ref[:, ATTN_WIDTH:ATTN_WIDTH + KV_WIDTH] = _rope_t(dk, cos_t, sa_t, sb_t).astype(BF16)
        dv = dvc_ref[...] + jnp.where(glast, 0.0, dvp_ref[...])
        dproj_ref[:, ATTN_WIDTH + KV_WIDTH:ATTN_WIDTH + 2 * KV_WIDTH] = dv.astype(BF16)

        u, bg, cg = ubc_ref[:, 0:cwid], ubc_ref[:, cwid:2 * cwid], ubc_ref[:, 2 * cwid:3 * cwid]
        z = cg * u
        hz = hprev_ref[:, 2 * cwid:3 * cwid] * hprev_ref[:, 0:cwid]
        zbuf[0:8, :] = jnp.where(first, 0.0, hz)
        zbuf[8:8 + tm, :] = z
        z2, z1 = zbuf[6:6 + tm, :], zbuf[7:7 + tm, :]
        w0, w1, w2 = cw_ref[0:1, :], cw_ref[1:2, :], cw_ref[2:3, :]
        y = w0 * z2 + w1 * z1 + w2 * z
        dco = dco_ref[...].astype(F32)
        dyc = dco * bg
        dyn = dcon_ref[0:8, :].astype(F32) * hnext_ref[:, cwid:2 * cwid]
        dybuf[0:tm, :] = dyc
        dybuf[tm:tm + 8, :] = jnp.where(last, 0.0, dyn)
        dz = w2 * dyc + w1 * dybuf[1:1 + tm, :] + w0 * dybuf[2:2 + tm, :]
        srow = lax.broadcasted_iota(jnp.int32, (8, cwid), 0)
        dcw_ref[...] += (jnp.where(srow == 0, _row_sum(dyc * z2), 0.0) + jnp.where(srow == 1, _row_sum(dyc * z1), 0.0)
                         + jnp.where(srow == 2, _row_sum(dyc * z), 0.0))
        base = ATTN_WIDTH + 2 * KV_WIDTH
        dproj_ref[:, base:base + cwid] = (dz * cg).astype(BF16)
        dproj_ref[:, base + cwid:base + 2 * cwid] = (dco * y).astype(BF16)
        dproj_ref[:, base + 2 * cwid:base + 3 * cwid] = (dz * u).astype(BF16)

    cur = lambda w: pl.BlockSpec((tm, w), lambda i: (i, 0))
    nxt = lambda w: pl.BlockSpec((tm, w), lambda i: (jnp.minimum(i + 1, ntile - 1), 0))
    return _pcall(
        body, name=name, grid=(ntile,),
        in_specs=[cur(ATTN_WIDTH), cur(KV_WIDTH), nxt(KV_WIDTH), cur(KV_WIDTH), nxt(KV_WIDTH),
                  cur(LANE), cur(LANE), cur(LANE),
                  pl.BlockSpec((tm, cwid), lambda i: (i, 1)),
                  pl.BlockSpec((16, cwid), lambda i: (jnp.minimum((i + 1) * (tm // 16), t // 16 - 1), 1)),
                  cur(3 * cwid),
                  pl.BlockSpec((8, 3 * cwid), lambda i: (jnp.maximum(i * (tm // 8) - 1, 0), 0)),
                  pl.BlockSpec((8, 3 * cwid), lambda i: (jnp.minimum((i + 1) * (tm // 8), t // 8 - 1), 0)),
                  _full((8, cwid))],
        out_specs=[cur(IN_WIDTH), _full((8, cwid))],
        out_shape=[jax.ShapeDtypeStruct((t, IN_WIDTH), BF16), jax.ShapeDtypeStruct((8, cwid), F32)],
        scratch_shapes=[pltpu.VMEM((tm + 8, cwid), F32), pltpu.VMEM((tm + 8, cwid), F32)],
        args=(dq, dkc, dkp, dvc, dvp, cos, sa, sb, dmixin, dmixin, ubc, ubc, ubc, cw), comm=comm)


def _ada_fwd(c_all, w_ada, b_ada_shard, *, name, comm=None):
    nb, d = c_all.shape
    n = w_ada.shape[1]
    tn = n // 2

    def body(c_ref, w_ref, b_ref, o_ref):
        cv = c_ref[...]
        cond = cv * _sigmoid(cv)
        o_ref[...] = jnp.dot(cond, w_ref[...], preferred_element_type=F32,
                             precision=lax.Precision.HIGHEST) + b_ref[...]

    return _pcall(
        body, name=name, grid=(n // tn,),
        in_specs=[_full((nb, d)), pl.BlockSpec((d, tn), lambda j: (0, j)), pl.BlockSpec((1, tn), lambda j: (0, j))],
        out_specs=pl.BlockSpec((nb, tn), lambda j: (0, j)),
        out_shape=jax.ShapeDtypeStruct((nb, n), F32), args=(c_all, w_ada, b_ada_shard), comm=comm)


def _small_finish(gathered, dmod_all, dmod_shard, c_all_t, *, name):
    d = D_MODEL
    nb, n = dmod_shard.shape

    def body(g_ref, dm_ref, dms_ref, ct_ref, sum_ref, gw_ref, gb_ref):
        total = g_ref[0]
        for dev in range(1, N_DEV):
            total = total + g_ref[dev]
        sum_ref[...] = total
        gb_ref[...] = _row_sum(dm_ref[...])
        ctv = ct_ref[...]
        cond_t = ctv * _sigmoid(ctv)
        for jb in range(n // COL_CHUNK):
            gw_ref[:, jb * COL_CHUNK:(jb + 1) * COL_CHUNK] = jnp.dot(
                cond_t, dms_ref[:, jb * COL_CHUNK:(jb + 1) * COL_CHUNK], preferred_element_type=F32,
                precision=lax.Precision.HIGHEST)

    return pl.pallas_call(
        body, name=name, grid=(1,),
        in_specs=[_full((N_DEV, SMALL_ROWS, d)), _full((nb, N_MOD * d)), _full((nb, n)), _full((d, nb))],
        out_specs=[_full((SMALL_ROWS, d)), _full((d, n)), _full((1, N_MOD * d))],
        out_shape=[jax.ShapeDtypeStruct((SMALL_ROWS, d), F32), jax.ShapeDtypeStruct((d, n), F32),
                   jax.ShapeDtypeStruct((1, N_MOD * d), F32)],
        compiler_params=_params(("arbitrary",)),
    )(gathered, dmod_all, dmod_shard, c_all_t)


def _row_tile(r, c, budget=1 << 21):
    if r * c * 4 <= budget or r % 16:
        return r
    best = 16
    for tr in range(16, r + 1, 16):
        if r % tr == 0 and tr * c * 4 <= budget:
            best = tr
    return best


def _cast_into(w, chip, col_kind, *, name):
    r, c = w.shape
    tr = _row_tile(r, c)

    def body(chip_ref, w_ref, o_ref):
        o_ref[...] = w_ref[...].astype(BF16)

    if col_kind:
        out_spec = pl.BlockSpec((tr, c), lambda i, chip_ref: (i, chip_ref[0]))
        out_shape = jax.ShapeDtypeStruct((r, c * N_CHIPS), BF16)
    else:
        out_spec = pl.BlockSpec((tr, c), lambda i, chip_ref: (chip_ref[0] * (r // tr) + i, 0))
        out_shape = jax.ShapeDtypeStruct((r * N_CHIPS, c), BF16)
    return _pcall(body, name=name, grid=(r // tr,), in_specs=[pl.BlockSpec((tr, c), lambda i, chip_ref: (i, 0))],
                  out_specs=out_spec, out_shape=out_shape, args=(w,), prefetch=chip)


def _adamw(w, g, m, v, *, name, comm=None):
    r, c = w.shape
    tr = _row_tile(r, c)
    c1 = 1.0 - ADAM_B1 ** ADAM_STEP
    c2 = 1.0 - ADAM_B2 ** ADAM_STEP

    def body(w_ref, g_ref, m_ref, v_ref, d_ref, nm_ref, nv_ref):
        gv = g_ref[...]
        m2 = ADAM_B1 * m_ref[...] + (1.0 - ADAM_B1) * gv
        v2 = ADAM_B2 * v_ref[...] + (1.0 - ADAM_B2) * (gv * gv)
        d_ref[...] = -ADAM_LR * ((m2 / c1) / (jnp.sqrt(v2 / c2) + ADAM_EPS) + ADAM_WD * w_ref[...])
        nm_ref[...] = m2
        nv_ref[...] = v2

    spec = pl.BlockSpec((tr, c), lambda i: (i, 0))
    sh = jax.ShapeDtypeStruct((r, c), F32)
    return _pcall(body, name=name, grid=(r // tr,), in_specs=[spec] * 4, out_specs=[spec] * 3, out_shape=[sh] * 3,
                  args=(w, g, m, v), comm=comm)


def _sum_pair(pos, g3, r3, blk_of, *, name, comm=None):
    n, rows, cols = r3.shape
    tr = _row_tile(rows, cols)

    def body(pos_ref, g_ref, r_ref, s32_ref, s16_ref):
        s = g_ref[0] + r_ref[0]
        s32_ref[0] = s
        s16_ref[0] = s.astype(BF16)

    own = pl.BlockSpec((1, tr, cols), lambda p, i, pos: (blk_of(p, pos), i, 0))
    plain = pl.BlockSpec((1, tr, cols), lambda p, i, pos: (p, i, 0))
    return _pcall(
        body, name=name, grid=(n, rows // tr), in_specs=[own, plain], out_specs=[plain, plain],
        out_shape=[jax.ShapeDtypeStruct((n, rows, cols), F32), jax.ShapeDtypeStruct((n, rows, cols), BF16)],
        args=(g3, r3), prefetch=pos, comm=comm)


def _sum_final(pos, s32, recv, *, col_kind, n_shard, name, comm=None):
    if col_kind:
        rows, cols = s32.shape[1], n_shard
        own = lambda tr: pl.BlockSpec((1, tr, cols), lambda i, pos: (0, i, 2 * pos[0] + pos[1]))
    else:
        rows, cols = s32.shape[1], s32.shape[2]
        own = lambda tr: pl.BlockSpec((1, tr, cols), lambda i, pos: (2 * pos[0] + pos[1], i, 0))
    tr = _row_tile(rows, cols)

    def body(pos_ref, s_ref, r_ref, o_ref):
        o_ref[0] = ((s_ref[0] + r_ref[0].astype(F32)) + r_ref[1].astype(F32)) + r_ref[2].astype(F32)

    return _pcall(
        body, name=name, grid=(rows // tr,),
        in_specs=[own(tr), pl.BlockSpec((3, tr, cols), lambda i, pos: (0, i, 0))],
        out_specs=pl.BlockSpec((1, tr, cols), lambda i, pos: (pos[2], i, 0)),
        out_shape=jax.ShapeDtypeStruct((2, rows, cols), F32), args=(s32, recv), prefetch=pos, comm=comm)


def _position():
    return lax.axis_index("x"), lax.axis_index("y"), lax.axis_index("c")


def _allgather8(x_shard, *, name, comm=None):
    m_per, n = x_shard.shape
    nci, nco = (0, 0) if comm is None else (len(comm.inputs), len(comm.out_shapes))

    def body(*refs):
        x_ref, refs = refs[0], refs[1:]
        cin, refs = refs[:nci], refs[nci:]
        out_ref, refs = refs[0], refs[1:]
        cout, refs = refs[:nco], refs[nco:]
        (send_sems, recv_sems, local_sem), csems = refs[:3], refs[3:]
        x, y, c = _position()
        me, sibling = (x, y, c), (x, y, 1 - c)
        chips = [(1 - x, y), (x, 1 - y), (1 - x, 1 - y)]

        def rows(px, py, pc):
            return out_ref.at[pl.ds((4 * px + 2 * py + pc) * m_per, m_per), :]

        def copy(k, block, to, src=None):
            return pltpu.make_async_remote_copy(
                src_ref=rows(*block) if src is None else src, dst_ref=rows(*block),
                send_sem=send_sems.at[k], recv_sem=recv_sems.at[k], device_id=to, device_id_type=MESH)

        mine = pltpu.make_async_copy(x_ref, rows(*me), local_sem)
        mine.start()
        first = [copy(0, me, sibling, src=x_ref)]
        first += [copy(1 + j, me, (*chip, c), src=x_ref) for j, chip in enumerate(chips)]
        for cp in first:
            cp.start()
        if comm is not None:
            comm.start(cin, cout, csems)
        passed = [copy(4 + j, (*chip, c), sibling) for j, chip in enumerate(chips)]
        for j, chip in enumerate(chips):
            copy(1 + j, (*chip, c), me).wait_recv()
            passed[j].start()
        copy(0, sibling, me).wait_recv()
        for j, chip in enumerate(chips):
            copy(4 + j, (*chip, 1 - c), me).wait_recv()
        for cp in first + passed:
            cp.wait_send()
        mine.wait()
        if comm is not None:
            comm.middle(cin, cout, csems)
            comm.finish(cin, cout, csems)

    vmem = pl.BlockSpec(memory_space=pltpu.VMEM)
    sems = [pltpu.SemaphoreType.DMA((7,)), pltpu.SemaphoreType.DMA((7,)), pltpu.SemaphoreType.DMA]
    out = jax.ShapeDtypeStruct((N_DEV * m_per, n), x_shard.dtype)
    if comm is None:
        return pl.pallas_call(body, name=name, out_shape=out, in_specs=[vmem], out_specs=vmem,
                              scratch_shapes=sems)(x_shard)
    res = pl.pallas_call(
        body, name=name, out_shape=[out] + list(comm.out_shapes), in_specs=[vmem] + [ANY_SPEC] * nci,
        out_specs=[vmem] + [ANY_SPEC] * nco, scratch_shapes=sems + list(comm.sems),
        input_output_aliases={1 + i: 1 + o for i, o in comm.aliases.items()})(x_shard, *comm.inputs)
    return res[0], list(res[1:])


def _peer_chips(x, y):
    return [(1 - x, y), (x, 1 - y), (1 - x, 1 - y)]


class _GatherJob:
    def __init__(self, pieces):
        self.pieces = pieces
        n_p = len(pieces)
        self.inputs = [p[0] for p in pieces]
        self.out_shapes = [jax.ShapeDtypeStruct(p[0].shape, p[0].dtype) for p in pieces]
        for buf, col_kind, r0, nr in pieces:
            half_rows = buf.shape[0] // (2 if col_kind else 2 * N_CHIPS)
            assert r0 % 16 == 0 and nr % 16 == 0 and r0 + nr <= half_rows, (buf.shape, r0, nr)
        self.aliases = {p: p for p in range(n_p)}
        self.sems = [pltpu.SemaphoreType.DMA((3 * n_p,))] * 4

    def _region(self, cout, p, chip_idx, half):
        buf, col_kind, r0, nr = self.pieces[p]
        if col_kind:
            n = buf.shape[1] // N_CHIPS
            return cout[p].at[pl.ds(half * (buf.shape[0] // 2) + r0, nr), pl.ds(chip_idx * n, n)]
        n = buf.shape[0] // N_CHIPS
        return cout[p].at[pl.ds(chip_idx * n + half * (n // 2) + r0, nr), :]

    def _copies(self, cout, sems):
        send_sems, recv_sems, fsend_sems, frecv_sems = sems
        x, y, c = _position()
        k = 2 * x + y
        sibling = (x, y, 1 - c)
        sends, arrivals, fwds, fwd_arrivals = [], [], [], []

        def remote(region, ssem, rsem, to):
            return pltpu.make_async_remote_copy(src_ref=region, dst_ref=region, send_sem=ssem, recv_sem=rsem,
                                                device_id=to, device_id_type=MESH)

        for p in range(len(self.pieces)):
            for j, chip in enumerate(_peer_chips(x, y)):
                idx = 3 * p + j
                theirs = 2 * chip[0] + chip[1]
                sends.append(remote(self._region(cout, p, k, c), send_sems.at[idx], recv_sems.at[idx], (*chip, c)))
                arrivals.append(remote(self._region(cout, p, theirs, c), send_sems.at[idx], recv_sems.at[idx],
                                       (*chip, c)))
                fwds.append(remote(self._region(cout, p, theirs, c), fsend_sems.at[idx], frecv_sems.at[idx], sibling))
                fwd_arrivals.append(remote(self._region(cout, p, theirs, 1 - c), fsend_sems.at[idx],
                                           frecv_sems.at[idx], sibling))
        return sends, arrivals, fwds, fwd_arrivals

    def start(self, cin, cout, sems):
        for cp in self._copies(cout, sems)[0]:
            cp.start()

    def middle(self, cin, cout, sems):
        _, arrivals, fwds, _ = self._copies(cout, sems)
        for arrived, fw in zip(arrivals, fwds):
            arrived.wait_recv()
            fw.start()

    def finish(self, cin, cout, sems):
        sends, _, fwds, fwd_arrivals = self._copies(cout, sems)
        for arrived in fwd_arrivals:
            arrived.wait_recv()
        for cp in sends + fwds:
            cp.wait_send()


class _PairedJob:
    aliases = {}

    def start(self, cin, cout, sems):
        for cp in self._copies(cin, cout, sems):
            cp.start()

    def middle(self, cin, cout, sems):
        pass

    def finish(self, cin, cout, sems):
        copies = self._copies(cin, cout, sems)
        for cp in copies:
            cp.wait_recv()
        for cp in copies:
            cp.wait_send()


class _SwapJob(_PairedJob):
    def __init__(self, grads, kinds):
        self.inputs, self.kinds = list(grads), list(kinds)
        self.out_shapes, self.n_copies = [], []
        for g, kd in zip(grads, kinds):
            if kd:
                self.out_shapes.append(jax.ShapeDtypeStruct((1, g.shape[0] // 2, g.shape[1]), g.dtype))
                self.n_copies.append(1)
            else:
                n = g.shape[0] // N_CHIPS
                self.out_shapes.append(jax.ShapeDtypeStruct((N_CHIPS, n // 2, g.shape[1]), g.dtype))
                self.n_copies.append(N_CHIPS)
        total = sum(self.n_copies)
        self.sems = [pltpu.SemaphoreType.DMA((total,)), pltpu.SemaphoreType.DMA((total,))]

    def _copies(self, cin, cout, sems):
        send_sems, recv_sems = sems
        x, y, c = _position()
        copies = []
        for p, src_ref in enumerate(cin):
            for kk in range(self.n_copies[p]):
                if self.kinds[p]:
                    hr = src_ref.shape[0] // 2
                    src = src_ref.at[pl.ds((1 - c) * hr, hr), :]
                else:
                    n = src_ref.shape[0] // N_CHIPS
                    src = src_ref.at[pl.ds(kk * n + (1 - c) * (n // 2), n // 2), :]
                idx = len(copies)
                copies.append(pltpu.make_async_remote_copy(
                    src_ref=src, dst_ref=cout[p].at[kk], send_sem=send_sems.at[idx], recv_sem=recv_sems.at[idx],
                    device_id=(x, y, 1 - c), device_id_type=MESH))
        return copies


class _ExchangeJob(_PairedJob):
    def __init__(self, s16, kinds, sizes):
        self.inputs, self.kinds, self.sizes = list(s16), list(kinds), list(sizes)
        self.out_shapes = [jax.ShapeDtypeStruct((3, s.shape[1], n if kd else s.shape[2]), s.dtype)
                           for s, kd, n in zip(s16, kinds, sizes)]
        self.sems = [pltpu.SemaphoreType.DMA((3 * len(s16),)), pltpu.SemaphoreType.DMA((3 * len(s16),))]

    def _copies(self, cin, cout, sems):
        send_sems, recv_sems = sems
        x, y, c = _position()
        copies = []
        for p, src_ref in enumerate(cin):
            for j, chip in enumerate(_peer_chips(x, y)):
                kk = 2 * chip[0] + chip[1]
                n = self.sizes[p]
                src = src_ref.at[0, :, pl.ds(kk * n, n)] if self.kinds[p] else src_ref.at[kk]
                copies.append(pltpu.make_async_remote_copy(
                    src_ref=src, dst_ref=cout[p].at[j], send_sem=send_sems.at[3 * p + j],
                    recv_sem=recv_sems.at[3 * p + j], device_id=(*chip, c), device_id_type=MESH))
        return copies


class _ShareJob:
    def __init__(self, halves):
        self.inputs = list(halves)
        self.out_shapes = [jax.ShapeDtypeStruct(h.shape, h.dtype) for h in halves]
        self.aliases = {p: p for p in range(len(halves))}
        self.sems = [pltpu.SemaphoreType.DMA((len(halves),)), pltpu.SemaphoreType.DMA((len(halves),))]

    def _copies(self, cout, sems, half):
        send_sems, recv_sems = sems
        x, y, c = _position()
        h = c if half == "mine" else 1 - c
        return [pltpu.make_async_remote_copy(
            src_ref=o.at[h], dst_ref=o.at[h], send_sem=send_sems.at[p], recv_sem=recv_sems.at[p],
            device_id=(x, y, 1 - c), device_id_type=MESH) for p, o in enumerate(cout)]

    def start(self, cin, cout, sems):
        for cp in self._copies(cout, sems, "mine"):
            cp.start()

    def middle(self, cin, cout, sems):
        pass

    def finish(self, cin, cout, sems):
        for cp in self._copies(cout, sems, "theirs"):
            cp.wait_recv()
        for cp in self._copies(cout, sems, "mine"):
            cp.wait_send()


class _MultiJob:
    def __init__(self, jobs):
        self.jobs = jobs
        self.inputs = [a for j in jobs for a in j.inputs]
        self.out_shapes = [s for j in jobs for s in j.out_shapes]
        self.sems = [s for j in jobs for s in j.sems]
        self.aliases = {}
        i0 = o0 = 0
        for j in jobs:
            for i, o in j.aliases.items():
                self.aliases[i0 + i] = o0 + o
            i0 += len(j.inputs)
            o0 += len(j.out_shapes)

    def _parts(self, cin, cout, sems):
        i0 = o0 = s0 = 0
        for j in self.jobs:
            ni, no, ns = len(j.inputs), len(j.out_shapes), len(j.sems)
            yield j, cin[i0:i0 + ni], cout[o0:o0 + no], sems[s0:s0 + ns]
            i0, o0, s0 = i0 + ni, o0 + no, s0 + ns

    def start(self, cin, cout, sems):
        for j, a, b, s in self._parts(cin, cout, sems):
            j.start(a, b, s)

    def middle(self, cin, cout, sems):
        for j, a, b, s in self._parts(cin, cout, sems):
            j.middle(a, b, s)

    def finish(self, cin, cout, sems):
        for j, a, b, s in self._parts(cin, cout, sems):
            j.finish(a, b, s)


def _rope_tables(positions):
    half = ROT_DIM // 2
    inv_freq = jnp.power(jnp.float32(ROPE_THETA), -jnp.arange(0, ROT_DIM, 2, dtype=F32) / ROT_DIM)
    inv_head = jnp.concatenate([inv_freq, inv_freq, jnp.zeros((HEAD_DIM - ROT_DIM,), F32)])
    inv_lane = jnp.concatenate([inv_head] * (LANE // HEAD_DIM))
    ang = positions.astype(F32).reshape(-1)[:, None] * inv_lane[None, :]
    sin = jnp.sin(ang)
    dim = jnp.arange(LANE) % HEAD_DIM
    return jnp.cos(ang), jnp.where(dim < half, -sin, 0.0), jnp.where(dim >= half, sin, 0.0)


def kernel(x, c, positions, w_ada, b_ada, ffn1_w_gate_up, ffn1_w_down, ln1_g, ln1_b, w_in, conv_w, attn_sinks, w_out, ln2_g, ln2_b, ffn2_w_gate_up, ffn2_w_down, ln3_g, ln3_b, loss_target, m_w_ada, m_b_ada, m_ffn1_w_gate_up, m_ffn1_w_down, m_ln1_g, m_ln1_b, m_w_in, m_conv_w, m_attn_sinks, m_w_out, m_ln2_g, m_ln2_b, m_ffn2_w_gate_up, m_ffn2_w_down, m_ln3_g, m_ln3_b, v_w_ada, v_b_ada, v_ffn1_w_gate_up, v_ffn1_w_down, v_ln1_g, v_ln1_b, v_w_in, v_conv_w, v_attn_sinks, v_w_out, v_ln2_g, v_ln2_b, v_ffn2_w_gate_up, v_ffn2_w_down, v_ln3_g, v_ln3_b):
    d = D_MODEL
    nb, seq, _ = x.shape
    t = nb * seq
    f = ffn1_w_down.shape[1] * N_CHIPS
    ax, ay, ac = _position()
    chip = 2 * ax + ay
    dev = 2 * chip + ac
    pos = jnp.stack([ax, ay, ac]).astype(jnp.int32)

    x2 = x.reshape(t, d)
    tgt2 = loss_target.reshape(t, d)
    ln1 = jnp.concatenate([ln1_g, ln1_b], axis=0)
    ln2 = jnp.concatenate([ln2_g, ln2_b], axis=0)
    ln3 = jnp.concatenate([ln3_g, ln3_b], axis=0)
    sinks = attn_sinks.reshape(N_Q_HEADS)
    cos_t, sa_t, sb_t = _rope_tables(positions)

    gu_cuts = [0, 176, 352, d // 2]
    gu_part = lambda buf, s: (buf, True, gu_cuts[s], gu_cuts[s + 1] - gu_cuts[s])
    chip_arr = jnp.reshape(chip, (1,)).astype(jnp.int32)
    b_gu1 = _cast_into(ffn1_w_gate_up[0], chip_arr, True, name="cast_gu1")

    n_ada = w_ada.shape[2]
    c_all, (b_gu1,) = _allgather8(c.reshape(nb * d // LANE, LANE), name="gather_c", comm=_GatherJob([gu_part(b_gu1, 0)]))
    c_all = c_all.reshape(N_DEV * nb, d)
    b_shard = lax.dynamic_slice(b_ada, (0, chip * n_ada), (1, n_ada))
    mod_part, (b_gu1,) = _ada_fwd(c_all, w_ada[0], b_shard, name="ada_fwd", comm=_GatherJob([gu_part(b_gu1, 1)]))
    conv_rows = jnp.pad(conv_w[0], ((0, 5), (0, n_ada - conv_w.shape[2])))
    part = jnp.concatenate([mod_part, conv_rows], axis=0)
    parts, (wgu1,) = _allgather8(part, name="gather_mod", comm=_GatherJob([gu_part(b_gu1, 2)]))
    parts = parts.reshape(N_DEV, N_DEV * nb + 8, n_ada)
    mod_all = jnp.concatenate([parts[2 * k, :N_DEV * nb, :] for k in range(N_CHIPS)], axis=1)
    mod = lax.dynamic_slice(mod_all, (dev * nb, 0), (nb, N_MOD * d)).reshape(nb, N_MOD, d)
    cw_full = jnp.concatenate([parts[2 * k, N_DEV * nb:, :conv_w.shape[2]] for k in range(N_CHIPS)], axis=1)

    b_d1 = _cast_into(ffn1_w_down[0], chip_arr, False, name="cast_d1")
    b_in = _cast_into(w_in[0].T, chip_arr, False, name="cast_in")
    b_out = _cast_into(w_out[0], chip_arr, False, name="cast_out")
    b_gu2 = _cast_into(ffn2_w_gate_up[0], chip_arr, True, name="cast_gu2")
    b_d2 = _cast_into(ffn2_w_down[0], chip_arr, False, name="cast_d2")
    n_gu, n_d, n_in, n_out = (ffn1_w_gate_up.shape[2], ffn1_w_down.shape[1], w_in.shape[2], w_out.shape[1])

    def whole(buf, col_kind):
        return (buf, col_kind, 0, buf.shape[0] // (2 if col_kind else 2 * N_CHIPS))

    (h1, a1, gu1), (wd1, wout) = _ffn_up(x2, ln1, mod, wgu1, seq=seq, sc_idx=1, sh_idx=0, use_ln=False,
                                         name="ffn1_up", comm=_GatherJob([whole(b_d1, False), whole(b_out, False)]))
    (f1, xhat1, rstd1), (win_t,) = _ffn_down_ln(a1, wd1, x2, ln1, mod, seq=seq, gate_idx=2, use_ln=False,
                                                name="ffn1_down", comm=_GatherJob([whole(b_in, False)]))
    (h2, q, k, v, ubc), (b_gu2,) = _in_proj(
        xhat1, ln1, mod, win_t, cos_t, sa_t, sb_t, seq=seq, sc_idx=4, sh_idx=3, name="in_proj",
        comm=_GatherJob([gu_part(b_gu2, 0)]))
    attn, (b_gu2,) = _attention(q, k, v, sinks, seq=seq, name="attention", comm=_GatherJob([gu_part(b_gu2, 1)]))
    (mixin, mix, xhat2, rstd2), (wgu2,) = _out_proj(
        attn, ubc, cw_full, wout, xhat1, ln1, mod, seq=seq, gate_idx=5, name="out_proj",
        comm=_GatherJob([gu_part(b_gu2, 2)]))
    (h3, a3, gu3), (wd2,) = _ffn_up(xhat2, ln2, mod, wgu2, seq=seq, sc_idx=7, sh_idx=6, use_ln=True, name="ffn2_up",
                                    comm=_GatherJob([whole(b_d2, False)]))
    dr3, df3, loss_cols, dln3g, dln3b, dgate3 = _ffn_down_loss(
        a3, wd2, xhat2, ln2, mod, ln3, tgt2, seq=seq, gate_idx=8, name="ffn2_down_loss")

    def pair_sum(g, r3, col_kind, name_, comm=None):
        if col_kind:
            g3 = g.reshape(2, g.shape[0] // 2, g.shape[1])
            blk_of = lambda p_, pos_: pos_[2]
        else:
            g3 = g.reshape(2 * N_CHIPS, g.shape[0] // (2 * N_CHIPS), g.shape[1])
            blk_of = lambda p_, pos_: 2 * p_ + pos_[2]
        return _sum_pair(pos, g3, r3, blk_of, name=name_, comm=comm)

    dgu3 = _ffn_bwd_act(df3, wd2, gu3, seq=seq, name="ffn2_bwd_act")
    g_wd2 = _matmul_tn(a3, df3, tmm=f // 2, tnn=d, name="grad_wd2")
    g_wgu2, (sib_d2,) = _matmul_tn(h3, dgu3, tmm=d, tnn=(2 * f) // 4, name="grad_wgu2",
                                   comm=_SwapJob([g_wd2], [False]))
    s32_d2, s16_d2 = pair_sum(g_wd2, sib_d2, False, "sum_pair_d2")
    (dr2, dmix, dsc3, dsh3, dgate2, dln2g, dln2b), (sib_gu2, recv_d2) = _bwd_in(
        dgu3, wgu2, dr3, xhat2, rstd2, ln2, mod, mix, seq=seq, w_is_nt=True, sc_idx=7, gate_idx=5,
        branch_scale=1.0, final=False, name="ffn2_bwd_in",
        comm=_MultiJob([_SwapJob([g_wgu2], [True]), _ExchangeJob([s16_d2], [False], [n_d])]))
    s32_gu2, s16_gu2 = pair_sum(g_wgu2, sib_gu2, True, "sum_pair_gu2")
    g_wout = _matmul_tn(mixin, dmix, tmm=d, tnn=d, name="grad_wout")
    dmixin = _matmul_nt_bf16(dmix, wout, seq=seq, name="out_proj_bwd")
    (dq, dkp, dkc, dvp, dvc, dsink), (recv_gu2, sib_out) = _attention_bwd(
        q, k, v, dmixin, sinks, seq=seq, name="attention_bwd",
        comm=_MultiJob([_ExchangeJob([s16_gu2], [True], [n_gu]), _SwapJob([g_wout], [False])]))
    s32_out, s16_out = pair_sum(g_wout, sib_out, False, "sum_pair_out")
    (dproj, dcw), (recv_out,) = _mix_bwd_assemble(
        dq, dkp, dkc, dvp, dvc, cos_t, sa_t, sb_t, dmixin, ubc, cw_full, seq=seq, name="mix_bwd",
        comm=_ExchangeJob([s16_out], [False], [n_out]))
    g_win_t = _matmul_tn(dproj, h2, tmm=IN_WIDTH // 2, tnn=d, name="grad_win")
    (dr1, df1, dsc2, dsh2, dgate1, dln1g, dln1b), (sib_in,) = _bwd_in(
        dproj, win_t, dr2, xhat1, rstd1, ln1, mod, f1, seq=seq, w_is_nt=False, sc_idx=4, gate_idx=2,
        branch_scale=0.5, final=False, name="in_proj_bwd", comm=_SwapJob([g_win_t], [False]))
    s32_in, s16_in = pair_sum(g_win_t, sib_in, False, "sum_pair_in")
    g_wd1, (recv_in,) = _matmul_tn(a1, df1, tmm=f // 2, tnn=d, name="grad_wd1",
                                   comm=_ExchangeJob([s16_in], [False], [n_in]))
    dgu1, (sib_d1,) = _ffn_bwd_act(df1, wd1, gu1, seq=seq, name="ffn1_bwd_act", comm=_SwapJob([g_wd1], [False]))
    s32_d1, s16_d1 = pair_sum(g_wd1, sib_d1, False, "sum_pair_d1")
    g_wgu1, (recv_d1,) = _matmul_tn(h1, dgu1, tmm=d, tnn=(2 * f) // 4, name="grad_wgu1",
                                    comm=_ExchangeJob([s16_d1], [False], [n_d]))

    def final_half(s32_, recv_, col_kind, n_shard, name_, comm=None):
        return _sum_final(pos, s32_, recv_, col_kind=col_kind, n_shard=n_shard, name=name_, comm=comm)

    half_gu2, (sib_gu1,) = final_half(s32_gu2, recv_gu2, True, n_gu, "sum_final_gu2", comm=_SwapJob([g_wgu1], [True]))
    early = [half_gu2,
             final_half(s32_d2, recv_d2, False, n_d, "sum_final_d2"),
             final_half(s32_out, recv_out, False, n_out, "sum_final_out"),
             final_half(s32_in, recv_in, False, n_in, "sum_final_in"),
             final_half(s32_d1, recv_d1, False, n_d, "sum_final_d1")]
    s32_gu1, s16_gu1 = pair_sum(g_wgu1, sib_gu1, True, "sum_pair_gu1")
    (grad_x, dsc1, dsh1), (recv_gu1, full_gu2, full_d2, full_out, full_in, full_d1) = _bwd_in(
        dgu1, wgu1, dr1, x2, None, None, mod, None, seq=seq, w_is_nt=True, sc_idx=1, gate_idx=None,
        branch_scale=None, final=True, name="ffn1_bwd_in",
        comm=_MultiJob([_ExchangeJob([s16_gu1], [True], [n_gu]), _ShareJob(early)]))
    late = [final_half(s32_gu1, recv_gu1, True, n_gu, "sum_final_gu1")]

    dmod = jnp.concatenate([dsh1, dsc1, dgate1, dsh2, dsc2, dgate2, dsh3, dsc3, dgate3], axis=1)
    loss_row = jnp.sum(loss_cols, axis=1, keepdims=True) * (0.5 / d)
    lane_row = lambda a: jnp.pad(a, ((0, 0), (0, d - a.shape[1])))
    block = jnp.concatenate(
        [dmod.reshape(nb * N_MOD, d), dln1g, dln1b, dln2g, dln2b, dln3g, dln3b,
         lane_row(dcw[0:3, :]), lane_row(dsink[:, 0:1].reshape(1, N_Q_HEADS)), lane_row(loss_row)], axis=0)
    block = jnp.pad(block, ((0, SMALL_ROWS - block.shape[0]), (0, 0)))
    gathered, (full_gu1,) = _allgather8(block, name="gather_small", comm=_ShareJob(late))
    gathered = gathered.reshape(N_DEV, SMALL_ROWS, d)
    dmod_all = gathered[:, :nb * N_MOD, :].reshape(N_DEV * nb, N_MOD * d)
    dmod_shard = lax.dynamic_slice(dmod_all, (0, chip * n_ada), (N_DEV * nb, n_ada))
    small, g_w_ada, g_b_ada = _small_finish(gathered, dmod_all, dmod_shard, c_all.T, name="small_finish")
    r0 = nb * N_MOD
    loss = small[r0 + 10, 0]
    g_ln = [small[r0 + i:r0 + i + 1, :] for i in range(6)]
    g_cw_full = small[r0 + 6:r0 + 9, :CONV_WIDTH]
    g_conv = lax.dynamic_slice(g_cw_full, (0, chip * conv_w.shape[2]), (3, conv_w.shape[2]))
    g_sinks = small[r0 + 9:r0 + 10, :N_Q_HEADS]

    def flat2(a):
        return a.reshape(-1, a.shape[-1])

    def unhalve(a):
        return a.reshape(2 * a.shape[1], a.shape[2])

    results = {}

    def adamw(name_, w_, g_, m_, v_):
        g2 = flat2(g_)
        dl, nm, nv = _adamw(flat2(w_), g2, flat2(m_), flat2(v_), name="adamw_" + name_)
        results[name_] = tuple(a.reshape(w_.shape) for a in (g2, dl, nm, nv))

    adamw("w_ada", w_ada, g_w_ada, m_w_ada, v_w_ada)
    adamw("ffn2_w_gate_up", ffn2_w_gate_up, unhalve(full_gu2), m_ffn2_w_gate_up, v_ffn2_w_gate_up)
    adamw("ffn2_w_down", ffn2_w_down, unhalve(full_d2), m_ffn2_w_down, v_ffn2_w_down)
    adamw("w_out", w_out, unhalve(full_out), m_w_out, v_w_out)
    adamw("w_in", w_in, unhalve(full_in).T, m_w_in, v_w_in)
    adamw("ffn1_w_gate_up", ffn1_w_gate_up, unhalve(full_gu1), m_ffn1_w_gate_up, v_ffn1_w_gate_up)
    adamw("ffn1_w_down", ffn1_w_down, unhalve(full_d1), m_ffn1_w_down, v_ffn1_w_down)
    adamw("b_ada", b_ada, g_b_ada, m_b_ada, v_b_ada)
    adamw("ln1_g", ln1_g, g_ln[0], m_ln1_g, v_ln1_g)
    adamw("ln1_b", ln1_b, g_ln[1], m_ln1_b, v_ln1_b)
    adamw("ln2_g", ln2_g, g_ln[2], m_ln2_g, v_ln2_g)
    adamw("ln2_b", ln2_b, g_ln[3], m_ln2_b, v_ln2_b)
    adamw("ln3_g", ln3_g, g_ln[4], m_ln3_g, v_ln3_g)
    adamw("ln3_b", ln3_b, g_ln[5], m_ln3_b, v_ln3_b)
    adamw("conv_w", conv_w, g_conv, m_conv_w, v_conv_w)
    adamw("attn_sinks", attn_sinks, g_sinks, m_attn_sinks, v_attn_sinks)
    order = ["w_ada", "b_ada", "ffn1_w_gate_up", "ffn1_w_down", "ln1_g", "ln1_b", "w_in", "conv_w", "attn_sinks",
             "w_out", "ln2_g", "ln2_b", "ffn2_w_gate_up", "ffn2_w_down", "ln3_g", "ln3_b"]
    return (loss, grad_x.reshape(x.shape), *[results[n_][0] for n_ in order], *[results[n_][1] for n_ in order],
            *[results[n_][2] for n_ in order], *[results[n_][3] for n_ in order])
```

```python
import jax
import jax.numpy as jnp
from jax import lax
from jax.experimental import pallas as pl
from jax.experimental.pallas import tpu as pltpu

F32 = jnp.float32
BF16 = jnp.bfloat16
MESH = pl.DeviceIdType.MESH

D_MODEL = 1024
HEAD_DIM = 64
ATTN_WIDTH = 512
CONV_WIDTH = 512
N_Q_HEADS = 8
N_KV_HEADS = 2
GQA_GROUP = 4
KV_WIDTH = 128
WINDOW = 128
BLOCK = 128
ROT_DIM = 16
ROPE_THETA = 500000.0
N_MOD = 9
LN_EPS = 1e-5
DN_ALPHA = 2.0 ** 0.25
IN_WIDTH = 2304
N_CHIPS = 4
N_DEV = 8
SMALL_ROWS = 32

ADAM_LR = 0.001
ADAM_B1 = 0.9
ADAM_B2 = 0.999
ADAM_EPS = 1e-08
ADAM_WD = 0.01
ADAM_STEP = 10

LANE = 128
COL_CHUNK = 256
VMEM_LIMIT = 56 * 1024 * 1024


def _params(sem=None, vmem=True):
    return pltpu.CompilerParams(dimension_semantics=sem, vmem_limit_bytes=VMEM_LIMIT if vmem else None)


def _sigmoid(g):
    return 0.5 * jnp.tanh(0.5 * g) + 0.5


def _row_sum(v):
    return jnp.sum(v, axis=0, keepdims=True)


ROW_CHUNK = 16
EPILOGUE_UNROLL = 8


def _fold8(v):
    return v[0:8, :] + v[8:16, :]


def _row_chunk_loop(n_rows, step, init):
    per_iter = ROW_CHUNK * EPILOGUE_UNROLL
    assert n_rows % per_iter == 0, n_rows

    def body(it, carry):
        for s in range(EPILOGUE_UNROLL):
            start = pl.multiple_of(it * per_iter + s * ROW_CHUNK, ROW_CHUNK)
            carry = step(pl.ds(start, ROW_CHUNK), carry)
        return carry

    return lax.fori_loop(0, n_rows // per_iter, body, init)


def _ln_stats(r):
    mu = jnp.mean(r, axis=-1, keepdims=True)
    rc = r - mu
    var = jnp.mean(rc * rc, axis=-1, keepdims=True)
    rstd = lax.rsqrt(var + LN_EPS)
    return rc * rstd, rstd


def _ln_bwd(dxo, xhat, rstd, g):
    dxhat = dxo * g
    m1 = jnp.mean(dxhat, axis=-1, keepdims=True)
    m2 = jnp.mean(dxhat * xhat, axis=-1, keepdims=True)
    return rstd * (dxhat - m1 - xhat * m2)


def _dot_nt(a, b):
    return lax.dot_general(a, b, (((1,), (1,)), ((), ())), preferred_element_type=F32)


def _dot_tn(a, b):
    return lax.dot_general(a, b, (((0,), (0,)), ((), ())), preferred_element_type=F32)


def _full(shape):
    nd = len(shape)
    return pl.BlockSpec(shape, lambda *_: (0,) * nd)


def _resident(shape):
    nd = len(shape)
    return pl.BlockSpec(shape, lambda *_: (0,) * nd, pipeline_mode=pl.Buffered(1))


ANY_SPEC = pl.BlockSpec(memory_space=pl.ANY)


def _pcall(body, *, name, grid, in_specs, out_specs, out_shape, args, scratch_shapes=(), comm=None, prefetch=None):
    single = not isinstance(out_shape, (list, tuple))
    out_specs = [out_specs] if single else list(out_specs)
    out_shape = [out_shape] if single else list(out_shape)
    in_specs = list(in_specs)
    scratch_shapes = list(scratch_shapes)
    sem = ("arbitrary",) * len(grid)
    n_pre = 0 if prefetch is None else 1
    pre_args = () if prefetch is None else (prefetch,)

    def call(fn, ins_, outs_, shapes_, scratch_, aliases_, operands):
        if prefetch is None:
            return pl.pallas_call(fn, name=name, grid=grid, in_specs=ins_, out_specs=outs_, out_shape=shapes_,
                                  scratch_shapes=scratch_, input_output_aliases=aliases_,
                                  compiler_params=_params(sem))(*operands)
        spec = pltpu.PrefetchScalarGridSpec(num_scalar_prefetch=1, grid=grid, in_specs=ins_, out_specs=outs_,
                                            scratch_shapes=scratch_)
        return pl.pallas_call(fn, name=name, grid_spec=spec, out_shape=shapes_,
                              input_output_aliases={n_pre + i: o for i, o in aliases_.items()},
                              compiler_params=_params(sem))(*pre_args, *operands)

    if comm is None:
        res = call(body, in_specs, out_specs, out_shape, scratch_shapes, {}, args)
        return res[0] if single else res
    n_in, n_out, n_scr = len(in_specs), len(out_specs), len(scratch_shapes)
    nci, nco = len(comm.inputs), len(comm.out_shapes)
    n_steps = 1
    for g in grid:
        n_steps *= g
    staged = n_steps >= 4
    middle_step = n_steps - 1 - max(1, n_steps // 8)

    def wrapped(*refs):
        pre, refs = refs[:n_pre], refs[n_pre:]
        ins, refs = refs[:n_in], refs[n_in:]
        cin, refs = refs[:nci], refs[nci:]
        outs, refs = refs[:n_out], refs[n_out:]
        cout, refs = refs[:nco], refs[nco:]
        scr, csems = refs[:n_scr], refs[n_scr:]
        step = pl.program_id(0)
        for ax in range(1, len(grid)):
            step = step * grid[ax] + pl.program_id(ax)

        @pl.when(step == 0)
        def _():
            comm.start(cin, cout, csems)

        body(*pre, *ins, *outs, *scr)

        if staged:
            @pl.when(step == middle_step)
            def _():
                comm.middle(cin, cout, csems)

        @pl.when(step == n_steps - 1)
        def _():
            if not staged:
                comm.middle(cin, cout, csems)
            comm.finish(cin, cout, csems)

    res = call(wrapped, in_specs + [ANY_SPEC] * nci, out_specs + [ANY_SPEC] * nco,
               out_shape + list(comm.out_shapes), scratch_shapes + list(comm.sems),
               {n_in + i: n_out + o for i, o in comm.aliases.items()}, (*args, *comm.inputs))
    main = res[:n_out]
    return (main[0] if single else main), list(res[n_out:])


def _comm_call(job, *, name):
    nci, nco = len(job.inputs), len(job.out_shapes)

    def body(*refs):
        cin, refs = refs[:nci], refs[nci:]
        cout, csems = refs[:nco], refs[nco:]
        job.start(cin, cout, csems)
        job.middle(cin, cout, csems)
        job.finish(cin, cout, csems)

    return pl.pallas_call(
        body, name=name, out_shape=list(job.out_shapes), in_specs=[ANY_SPEC] * nci, out_specs=[ANY_SPEC] * nco,
        scratch_shapes=list(job.sems), input_output_aliases=dict(job.aliases))(*job.inputs)


def _ffn_up(xin, lnp, mod, w, *, seq, sc_idx, sh_idx, use_ln, name, comm=None):
    t, d = xin.shape
    f = w.shape[1] // 2
    tm = min(512, seq)
    tpb = seq // tm
    ch = min(COL_CHUNK, f)

    def body(x_ref, ln_ref, mod_ref, w_ref, h_ref, a_ref, gu_ref):
        x = x_ref[...]
        if use_ln:
            x = x * ln_ref[0:1, :] + ln_ref[1:2, :]
        h = x * (1.0 + mod_ref[0, sc_idx:sc_idx + 1, :]) + mod_ref[0, sh_idx:sh_idx + 1, :]
        hb = h.astype(BF16)
        h_ref[...] = hb
        for j in range(f // ch):
            g = jnp.dot(hb, w_ref[:, j * ch:(j + 1) * ch], preferred_element_type=F32)
            u = jnp.dot(hb, w_ref[:, f + j * ch:f + (j + 1) * ch], preferred_element_type=F32)
            s = _sigmoid(g)
            silu = g * s
            a_ref[:, j * ch:(j + 1) * ch] = (silu * u).astype(BF16)
            gu_ref[:, j * ch:(j + 1) * ch] = (u * (s + silu * (1.0 - s))).astype(BF16)
            gu_ref[:, f + j * ch:f + (j + 1) * ch] = silu.astype(BF16)

    return _pcall(
        body, name=name, grid=(t // tm,),
        in_specs=[pl.BlockSpec((tm, d), lambda i: (i, 0)), _full((2, d)),
                  pl.BlockSpec((1, N_MOD, d), lambda i: (i // tpb, 0, 0)), _resident((d, 2 * f))],
        out_specs=[pl.BlockSpec((tm, d), lambda i: (i, 0)), pl.BlockSpec((tm, f), lambda i: (i, 0)),
                   pl.BlockSpec((tm, 2 * f), lambda i: (i, 0))],
        out_shape=[jax.ShapeDtypeStruct((t, d), BF16), jax.ShapeDtypeStruct((t, f), BF16),
                   jax.ShapeDtypeStruct((t, 2 * f), BF16)],
        args=(xin, lnp, mod, w), comm=comm)


def _ffn_down_ln(a, wd, xin, lnp_in, mod, *, seq, gate_idx, use_ln, name, comm=None):
    t, f = a.shape
    d = wd.shape[1]
    tm = min(512, seq)
    tpb = seq // tm

    def body(a_ref, wd_ref, x_ref, ln_ref, mod_ref, f_ref, xhat_ref, rstd_ref, acc):
        av = a_ref[...]
        for j in range(d // COL_CHUNK):
            acc[:, j * COL_CHUNK:(j + 1) * COL_CHUNK] = jnp.dot(
                av, wd_ref[:, j * COL_CHUNK:(j + 1) * COL_CHUNK], preferred_element_type=F32)
        scale = 0.5 * (1.0 + mod_ref[0, gate_idx:gate_idx + 1, :])

        fo = acc[...]
        x = x_ref[...]
        if use_ln:
            x = x * ln_ref[0:1, :] + ln_ref[1:2, :]
        xhat, rstd = _ln_stats(DN_ALPHA * x + scale * fo)
        f_ref[...] = fo.astype(BF16)
        xhat_ref[...] = xhat
        rstd_ref[...] = rstd

    return _pcall(
        body, name=name, grid=(t // tm,),
        in_specs=[pl.BlockSpec((tm, f), lambda i: (i, 0)), _resident((f, d)),
                  pl.BlockSpec((tm, d), lambda i: (i, 0)), _full((2, d)),
                  pl.BlockSpec((1, N_MOD, d), lambda i: (i // tpb, 0, 0))],
        out_specs=[pl.BlockSpec((tm, d), lambda i: (i, 0)), pl.BlockSpec((tm, d), lambda i: (i, 0)),
                   pl.BlockSpec((tm, 1), lambda i: (i, 0))],
        out_shape=[jax.ShapeDtypeStruct((t, d), BF16), jax.ShapeDtypeStruct((t, d), F32),
                   jax.ShapeDtypeStruct((t, 1), F32)],
        scratch_shapes=[pltpu.VMEM((tm, d), F32)],
        args=(a, wd, xin, lnp_in, mod), comm=comm)


def _ffn_down_loss(a, wd, xhat_in, lnp_in, mod, lnp_out, tgt, *, seq, gate_idx, name):
    t, f = a.shape
    d = wd.shape[1]
    nb = t // seq
    tm = min(512, seq)
    tpb = seq // tm

    def body(a_ref, wd_ref, x_ref, lnin_ref, mod_ref, lnout_ref, tgt_ref,
             dr_ref, df_ref, loss_ref, dg_ref, db_ref, dgate_ref, acc):
        i = pl.program_id(0)
        av = a_ref[...]
        for j in range(d // COL_CHUNK):
            acc[:, j * COL_CHUNK:(j + 1) * COL_CHUNK] = jnp.dot(
                av, wd_ref[:, j * COL_CHUNK:(j + 1) * COL_CHUNK], preferred_element_type=F32)
        scale = 0.5 * (1.0 + mod_ref[0, gate_idx:gate_idx + 1, :])
        g_in, b_in = lnin_ref[0:1, :], lnin_ref[1:2, :]
        g_out, b_out = lnout_ref[0:1, :], lnout_ref[1:2, :]

        def chunk(rows, carry):
            s_loss, s_dg, s_db, s_gate = carry
            fo = acc[rows, :]
            xhat, rstd = _ln_stats(DN_ALPHA * (x_ref[rows, :] * g_in + b_in) + scale * fo)
            e = xhat * g_out + b_out - tgt_ref[rows, :]
            dy = e * (1.0 / d)
            dr = _ln_bwd(dy, xhat, rstd, g_out)
            dr_ref[rows, :] = dr
            df_ref[rows, :] = (scale * dr).astype(BF16)
            return (s_loss + _fold8(e * e), s_dg + _fold8(dy * xhat), s_db + _fold8(dy),
                    s_gate + _fold8(0.5 * fo * dr))

        zero = jnp.zeros((8, d), F32)
        s_loss, s_dg, s_db, s_gate = _row_chunk_loop(tm, chunk, (zero, zero, zero, zero))

        @pl.when(i == 0)
        def _():
            loss_ref[...] = jnp.zeros_like(loss_ref)
            dg_ref[...] = jnp.zeros_like(dg_ref)
            db_ref[...] = jnp.zeros_like(db_ref)

        @pl.when(i % tpb == 0)
        def _():
            dgate_ref[...] = jnp.zeros_like(dgate_ref)

        loss_ref[...] += _row_sum(s_loss)
        dg_ref[...] += _row_sum(s_dg)
        db_ref[...] += _row_sum(s_db)
        dgate_ref[0] += _row_sum(s_gate)

    return pl.pallas_call(
        body, name=name, grid=(t // tm,), scratch_shapes=[pltpu.VMEM((tm, d), F32)],
        in_specs=[pl.BlockSpec((tm, f), lambda i: (i, 0)), _resident((f, d)),
                  pl.BlockSpec((tm, d), lambda i: (i, 0)), _full((2, d)),
                  pl.BlockSpec((1, N_MOD, d), lambda i: (i // tpb, 0, 0)), _full((2, d)),
                  pl.BlockSpec((tm, d), lambda i: (i, 0))],
        out_specs=[pl.BlockSpec((tm, d), lambda i: (i, 0)), pl.BlockSpec((tm, d), lambda i: (i, 0)),
                   _full((1, d)), _full((1, d)), _full((1, d)),
                   pl.BlockSpec((1, 1, d), lambda i: (i // tpb, 0, 0))],
        out_shape=[jax.ShapeDtypeStruct((t, d), F32), jax.ShapeDtypeStruct((t, d), BF16),
                   jax.ShapeDtypeStruct((1, d), F32), jax.ShapeDtypeStruct((1, d), F32),
                   jax.ShapeDtypeStruct((1, d), F32), jax.ShapeDtypeStruct((nb, 1, d), F32)],
        compiler_params=_params(("arbitrary",)),
    )(a, wd, xhat_in, lnp_in, mod, lnp_out, tgt)


def _rope(v, cos, sa, sb):
    return v * cos + pltpu.roll(v, LANE - ROT_DIM // 2, 1) * sa + pltpu.roll(v, ROT_DIM // 2, 1) * sb


def _rope_t(dy, cos, sa, sb):
    return dy * cos + pltpu.roll(dy * sa, ROT_DIM // 2, 1) + pltpu.roll(dy * sb, LANE - ROT_DIM // 2, 1)


def _in_proj(xhat, lnp, mod, w_t, cos, sa, sb, *, seq, sc_idx, sh_idx, name, comm=None):
    t, d = xhat.shape
    tm = min(512, seq)
    tpb = seq // tm
    n_conv = 3 * CONV_WIDTH

    def body(x_ref, ln_ref, mod_ref, w_ref, cos_ref, sa_ref, sb_ref, h_ref, q_ref, k_ref, v_ref, ubc_ref):
        x = x_ref[...] * ln_ref[0:1, :] + ln_ref[1:2, :]
        h = x * (1.0 + mod_ref[0, sc_idx:sc_idx + 1, :]) + mod_ref[0, sh_idx:sh_idx + 1, :]
        hb = h.astype(BF16)
        h_ref[...] = hb
        cos_t, sa_t, sb_t = cos_ref[...], sa_ref[...], sb_ref[...]
        for j in range(ATTN_WIDTH // COL_CHUNK):
            p = _dot_nt(hb, w_ref[j * COL_CHUNK:(j + 1) * COL_CHUNK, :])
            for s in range(COL_CHUNK // LANE):
                q_ref[:, j * COL_CHUNK + s * LANE:j * COL_CHUNK + (s + 1) * LANE] = _rope(
                    p[:, s * LANE:(s + 1) * LANE], cos_t, sa_t, sb_t).astype(BF16)
        p = _dot_nt(hb, w_ref[ATTN_WIDTH:ATTN_WIDTH + 2 * KV_WIDTH, :])
        k_ref[...] = _rope(p[:, 0:KV_WIDTH], cos_t, sa_t, sb_t).astype(BF16)
        v_ref[...] = p[:, KV_WIDTH:].astype(BF16)
        base = ATTN_WIDTH + 2 * KV_WIDTH
        for j in range(n_conv // COL_CHUNK):
            ubc_ref[:, j * COL_CHUNK:(j + 1) * COL_CHUNK] = _dot_nt(
                hb, w_ref[base + j * COL_CHUNK:base + (j + 1) * COL_CHUNK, :])

    row = lambda w: pl.BlockSpec((tm, w), lambda i: (i, 0))
    return _pcall(
        body, name=name, grid=(t // tm,),
        in_specs=[row(d), _full((2, d)), pl.BlockSpec((1, N_MOD, d), lambda i: (i // tpb, 0, 0)),
                  _resident((IN_WIDTH, d)), row(LANE), row(LANE), row(LANE)],
        out_specs=[row(d), row(ATTN_WIDTH), row(KV_WIDTH), row(KV_WIDTH), row(n_conv)],
        out_shape=[jax.ShapeDtypeStruct((t, d), BF16), jax.ShapeDtypeStruct((t, ATTN_WIDTH), BF16),
                   jax.ShapeDtypeStruct((t, KV_WIDTH), BF16), jax.ShapeDtypeStruct((t, KV_WIDTH), BF16),
                   jax.ShapeDtypeStruct((t, n_conv), F32)],
        args=(xhat, lnp, mod, w_t, cos, sa, sb), comm=comm)


def _attn_group(q_ref, kp_ref, kc_ref, vp_ref, vc_ref, sink_ref, g, first):
    lo, hi = g * HEAD_DIM, (g + 1) * HEAD_DIM
    kk = jnp.concatenate([kp_ref[:, lo:hi], kc_ref[:, lo:hi]], axis=0)
    vv = jnp.concatenate([vp_ref[:, lo:hi], vc_ref[:, lo:hi]], axis=0)
    qs = jnp.concatenate([q_ref[:, (GQA_GROUP * g + j) * HEAD_DIM:(GQA_GROUP * g + j + 1) * HEAD_DIM]
                          for j in range(GQA_GROUP)], axis=0)
    rows = GQA_GROUP * BLOCK
    row = lax.broadcasted_iota(jnp.int32, (rows, 2 * BLOCK), 0)
    ki = lax.broadcasted_iota(jnp.int32, (rows, 2 * BLOCK), 1)
    diff = (row & (BLOCK - 1)) + BLOCK - ki
    valid = (diff >= 0) & (diff < WINDOW) & ((ki >= BLOCK) | jnp.logical_not(first))
    s = _dot_nt(qs, kk) * (HEAD_DIM ** -0.5)
    s = jnp.where(valid, s, -1e30)
    rcol = lax.broadcasted_iota(jnp.int32, (rows, 1), 0)
    sink = jnp.zeros((rows, 1), F32)
    for j in range(GQA_GROUP):
        sink = jnp.where(rcol // BLOCK == j, sink_ref[GQA_GROUP * g + j], sink)
    m = jnp.maximum(jnp.max(s, axis=1, keepdims=True), sink)
    p = jnp.exp(s - m)
    ps = jnp.exp(sink - m)
    inv = 1.0 / (jnp.sum(p, axis=1, keepdims=True) + ps)
    return qs, kk, vv, p * inv, ps * inv


def _attention(q, k, v, sinks, *, seq, name, comm=None):
    t = q.shape[0]
    nblk = seq // BLOCK

    def body(q_ref, kp_ref, kc_ref, vp_ref, vc_ref, sink_ref, o_ref):
        first = (pl.program_id(0) % nblk) == 0
        outs = []
        for g in range(N_KV_HEADS):
            _, _, vv, pn, _ = _attn_group(q_ref, kp_ref, kc_ref, vp_ref, vc_ref, sink_ref, g, first)
            o = jnp.dot(pn.astype(BF16), vv, preferred_element_type=F32)
            outs += [o[j * BLOCK:(j + 1) * BLOCK, :] for j in range(GQA_GROUP)]
        o_ref[...] = jnp.concatenate(outs, axis=1).astype(BF16)

    cur = lambda w: pl.BlockSpec((BLOCK, w), lambda n: (n, 0))
    prev = lambda w: pl.BlockSpec((BLOCK, w), lambda n: (jnp.maximum(n - 1, 0), 0))
    return _pcall(
        body, name=name, grid=(t // BLOCK,),
        in_specs=[cur(ATTN_WIDTH), prev(KV_WIDTH), cur(KV_WIDTH), prev(KV_WIDTH), cur(KV_WIDTH),
                  pl.BlockSpec(memory_space=pltpu.SMEM)],
        out_specs=cur(ATTN_WIDTH),
        out_shape=jax.ShapeDtypeStruct((t, ATTN_WIDTH), BF16),
        args=(q, k, k, v, v, sinks), comm=comm)


def _out_proj(attn, ubc, cw, wout, xhat_in, lnp_in, mod, *, seq, gate_idx, name, comm=None):
    t, d = xhat_in.shape
    tm = min(512, seq)
    tpb = seq // tm
    cwid = CONV_WIDTH

    def body(attn_ref, ubc_ref, halo_ref, cw_ref, w_ref, x_ref, ln_ref, mod_ref,
             mixin_ref, mix_ref, xhat_ref, rstd_ref, zbuf, acc):
        first = (pl.program_id(0) % tpb) == 0
        u, bg, cg = ubc_ref[:, 0:cwid], ubc_ref[:, cwid:2 * cwid], ubc_ref[:, 2 * cwid:3 * cwid]
        z = cg * u
        hz = halo_ref[:, 2 * cwid:3 * cwid] * halo_ref[:, 0:cwid]
        zbuf[0:8, :] = jnp.where(first, 0.0, hz)
        zbuf[8:8 + tm, :] = z
        y = cw_ref[0:1, :] * zbuf[6:6 + tm, :] + cw_ref[1:2, :] * zbuf[7:7 + tm, :] + cw_ref[2:3, :] * z
        mixin_ref[:, 0:ATTN_WIDTH] = attn_ref[...]
        mixin_ref[:, ATTN_WIDTH:] = (bg * y).astype(BF16)
        mv = mixin_ref[...]
        for j in range(d // COL_CHUNK):
            acc[:, j * COL_CHUNK:(j + 1) * COL_CHUNK] = jnp.dot(
                mv, w_ref[:, j * COL_CHUNK:(j + 1) * COL_CHUNK], preferred_element_type=F32)
        scale = 1.0 + mod_ref[0, gate_idx:gate_idx + 1, :]

        mix = acc[...]
        xhat, rstd = _ln_stats(DN_ALPHA * (x_ref[...] * ln_ref[0:1, :] + ln_ref[1:2, :]) + scale * mix)
        mix_ref[...] = mix.astype(BF16)
        xhat_ref[...] = xhat
        rstd_ref[...] = rstd

    row = lambda w: pl.BlockSpec((tm, w), lambda i: (i, 0))
    return _pcall(
        body, name=name, grid=(t // tm,),
        in_specs=[row(ATTN_WIDTH), row(3 * cwid),
                  pl.BlockSpec((8, 3 * cwid), lambda i: (jnp.maximum(i * (tm // 8) - 1, 0), 0)),
                  _full((8, cwid)), _resident((d, d)), row(d), _full((2, d)),
                  pl.BlockSpec((1, N_MOD, d), lambda i: (i // tpb, 0, 0))],
        out_specs=[row(d), row(d), row(d), row(1)],
        out_shape=[jax.ShapeDtypeStruct((t, d), BF16), jax.ShapeDtypeStruct((t, d), BF16),
                   jax.ShapeDtypeStruct((t, d), F32), jax.ShapeDtypeStruct((t, 1), F32)],
        scratch_shapes=[pltpu.VMEM((tm + 8, cwid), F32), pltpu.VMEM((tm, d), F32)],
        args=(attn, ubc, ubc, cw, wout, xhat_in, lnp_in, mod), comm=comm)


def _ffn_bwd_act(df, wd, gu, *, seq, name, comm=None):
    t, d = df.shape
    f = wd.shape[0]
    tm = min(512, seq)
    ch = min(COL_CHUNK, f)

    def body(df_ref, wd_ref, gu_ref, dgu_ref):
        dfv = df_ref[...]
        for j in range(f // ch):
            da = _dot_nt(dfv, wd_ref[j * ch:(j + 1) * ch, :])
            dgu_ref[:, j * ch:(j + 1) * ch] = (da * gu_ref[:, j * ch:(j + 1) * ch].astype(F32)).astype(BF16)
            dgu_ref[:, f + j * ch:f + (j + 1) * ch] = (
                da * gu_ref[:, f + j * ch:f + (j + 1) * ch].astype(F32)).astype(BF16)

    return _pcall(
        body, name=name, grid=(t // tm,),
        in_specs=[pl.BlockSpec((tm, d), lambda i: (i, 0)), _resident((f, d)),
                  pl.BlockSpec((tm, 2 * f), lambda i: (i, 0))],
        out_specs=pl.BlockSpec((tm, 2 * f), lambda i: (i, 0)),
        out_shape=jax.ShapeDtypeStruct((t, 2 * f), BF16),
        args=(df, wd, gu), comm=comm)


def _bwd_in(a, w, dr, xin, rstd_prev, lnp_prev, mod, branch_prev, *, seq, w_is_nt, sc_idx, gate_idx,
            branch_scale, final, name, comm=None):
    t, kdim = a.shape
    d = dr.shape[1]
    nb = t // seq
    tm = min(512, seq)
    tpb = seq // tm

    def body(*refs):
        if final:
            a_ref, w_ref, dr_ref, x_ref, mod_ref, dx_ref, dsc_ref, dsh_ref, acc = refs
        else:
            (a_ref, w_ref, dr_ref, x_ref, rstd_ref, ln_ref, mod_ref, br_ref,
             drp_ref, dbr_ref, dsc_ref, dsh_ref, dgate_ref, dg_ref, db_ref, acc) = refs
        i = pl.program_id(0)
        av = a_ref[...]
        for j in range(d // COL_CHUNK):
            cols = slice(j * COL_CHUNK, (j + 1) * COL_CHUNK)
            acc[:, cols] = (_dot_nt(av, w_ref[cols, :]) if w_is_nt
                            else jnp.dot(av, w_ref[:, cols], preferred_element_type=F32))
        sc1 = 1.0 + mod_ref[0, sc_idx:sc_idx + 1, :]
        if not final:
            g_prev, b_prev = ln_ref[0:1, :], ln_ref[1:2, :]
            bscale = branch_scale * (1.0 + mod_ref[0, gate_idx:gate_idx + 1, :])

        def chunk(rows, carry):
            dh = acc[rows, :]
            dx = DN_ALPHA * dr_ref[rows, :] + dh * sc1
            if final:
                dx_ref[rows, :] = dx
                return carry[0] + _fold8(dh * x_ref[rows, :]), carry[1] + _fold8(dh)
            xhat = x_ref[rows, :]
            drp = _ln_bwd(dx, xhat, rstd_ref[rows, :], g_prev)
            drp_ref[rows, :] = drp
            dbr_ref[rows, :] = (bscale * drp).astype(BF16)
            return (carry[0] + _fold8(dh * (xhat * g_prev + b_prev)), carry[1] + _fold8(dh),
                    carry[2] + _fold8(branch_scale * br_ref[rows, :].astype(F32) * drp),
                    carry[3] + _fold8(dx * xhat), carry[4] + _fold8(dx))

        zero = jnp.zeros((8, d), F32)
        sums = _row_chunk_loop(tm, chunk, (zero,) * (2 if final else 5))

        @pl.when((i % tpb) == 0)
        def _():
            dsc_ref[...] = jnp.zeros_like(dsc_ref)
            dsh_ref[...] = jnp.zeros_like(dsh_ref)
            if not final:
                dgate_ref[...] = jnp.zeros_like(dgate_ref)

        dsc_ref[0] += _row_sum(sums[0])
        dsh_ref[0] += _row_sum(sums[1])
        if not final:
            @pl.when(i == 0)
            def _():
                dg_ref[...] = jnp.zeros_like(dg_ref)
                db_ref[...] = jnp.zeros_like(db_ref)

            dgate_ref[0] += _row_sum(sums[2])
            dg_ref[...] += _row_sum(sums[3])
            db_ref[...] += _row_sum(sums[4])

    row = lambda w_: pl.BlockSpec((tm, w_), lambda i: (i, 0))
    vec = pl.BlockSpec((1, 1, d), lambda i: (i // tpb, 0, 0))
    mod_spec = pl.BlockSpec((1, N_MOD, d), lambda i: (i // tpb, 0, 0))
    vshape = jax.ShapeDtypeStruct((nb, 1, d), F32)
    if final:
        in_specs = [row(kdim), _resident(w.shape), row(d), row(d), mod_spec]
        args = (a, w, dr, xin, mod)
        out_specs = [row(d), vec, vec]
        out_shape = [jax.ShapeDtypeStruct((t, d), F32), vshape, vshape]
    else:
        in_specs = [row(kdim), _resident(w.shape), row(d), row(d), row(1), _full((2, d)), mod_spec, row(d)]
        args = (a, w, dr, xin, rstd_prev, lnp_prev, mod, branch_prev)
        out_specs = [row(d), row(d), vec, vec, vec, _full((1, d)), _full((1, d))]
        out_shape = [jax.ShapeDtypeStruct((t, d), F32), jax.ShapeDtypeStruct((t, d), BF16), vshape, vshape, vshape,
                     jax.ShapeDtypeStruct((1, d), F32), jax.ShapeDtypeStruct((1, d), F32)]
    return _pcall(
        body, name=name, grid=(t // tm,), in_specs=in_specs, out_specs=out_specs, out_shape=out_shape,
        scratch_shapes=[pltpu.VMEM((tm, d), F32)], args=args, comm=comm)


def _matmul_tn(a, b, *, tmm, tnn, name, comm=None):
    t, m = a.shape
    n = b.shape[1]
    tk = min(2048, t)

    def body(a_ref, b_ref, o_ref):
        @pl.when(pl.program_id(2) == 0)
        def _():
            o_ref[...] = jnp.zeros_like(o_ref)
        o_ref[...] += _dot_tn(a_ref[...], b_ref[...])

    return _pcall(
        body, name=name, grid=(m // tmm, n // tnn, t // tk),
        in_specs=[pl.BlockSpec((tk, tmm), lambda i, j, k: (k, i)), pl.BlockSpec((tk, tnn), lambda i, j, k: (k, j))],
        out_specs=pl.BlockSpec((tmm, tnn), lambda i, j, k: (i, j)),
        out_shape=jax.ShapeDtypeStruct((m, n), F32),
        args=(a, b), comm=comm)


def _grad_chip_sum(pos, a, b, *, name, comm=None):
    t, m = a.shape
    n = b.shape[1]
    hm, tnn = m // 2, n // N_CHIPS
    tk = min(2048, t)
    nk = t // tk
    n_j = n // tnn

    def body(pos_ref, a_ref, b_ref, s32_ref, s16_ref, land_ref, acc, theirs, send_sems, recv_sems, copy_sem):
        p, j, k = pl.program_id(0), pl.program_id(1), pl.program_id(2)
        x, y, c = _position()

        def push(jj):
            return pltpu.make_async_remote_copy(
                src_ref=acc.at[jj], dst_ref=land_ref.at[jj], send_sem=send_sems.at[jj], recv_sem=recv_sems.at[jj],
                device_id=(x, y, 1 - c), device_id_type=MESH)

        fetch = pltpu.make_async_copy(land_ref.at[j], theirs, copy_sem)

        @pl.when(jnp.logical_and(p == 1, k == 0))
        def _():
            push(j).wait_send()
            push(j).wait_recv()
            fetch.start()

        part = _dot_tn(a_ref[...], b_ref[...])

        @pl.when(k == 0)
        def _():
            acc[j] = part

        @pl.when(k > 0)
        def _():
            acc[j] += part

        @pl.when(jnp.logical_and(p == 0, k == nk - 1))
        def _():
            push(j).start()

        @pl.when(jnp.logical_and(p == 1, k == nk - 1))
        def _():
            fetch.wait()
            s = acc[j] + theirs[...]
            s32_ref[0] = s
            s16_ref[0] = s.astype(BF16)

    half = lambda p, pos_ref: 1 - pos_ref[2] - p + 2 * p * pos_ref[2]
    out_tile = pl.BlockSpec((1, hm, tnn), lambda p, j, k, pos_ref: (0, 0, j * p))
    shape = lambda dt: jax.ShapeDtypeStruct((1, hm, n), dt)
    out = _pcall(
        body, name=name, grid=(2, n_j, nk),
        in_specs=[pl.BlockSpec((tk, hm), lambda p, j, k, pos_ref: (k, half(p, pos_ref))),
                  pl.BlockSpec((tk, tnn), lambda p, j, k, pos_ref: (k, j))],
        out_specs=[out_tile, out_tile, ANY_SPEC],
        out_shape=[shape(F32), shape(BF16), jax.ShapeDtypeStruct((n_j, hm, tnn), F32)],
        scratch_shapes=[pltpu.VMEM((n_j, hm, tnn), F32), pltpu.VMEM((hm, tnn), F32),
                        pltpu.SemaphoreType.DMA((n_j,)), pltpu.SemaphoreType.DMA((n_j,)), pltpu.SemaphoreType.DMA],
        args=(a, b), prefetch=pos, comm=comm)
    if comm is None:
        return out[0], out[1]
    (s32, s16, _), extra = out
    return (s32, s16), extra


def _matmul_nt_bf16(a, w, *, seq, name):
    t, kdim = a.shape
    n = w.shape[0]
    tm = min(512, seq)

    def body(a_ref, w_ref, o_ref):
        av = a_ref[...]
        for j in range(n // COL_CHUNK):
            o_ref[:, j * COL_CHUNK:(j + 1) * COL_CHUNK] = _dot_nt(
                av, w_ref[j * COL_CHUNK:(j + 1) * COL_CHUNK, :]).astype(BF16)

    return pl.pallas_call(
        body, name=name, grid=(t // tm,),
        in_specs=[pl.BlockSpec((tm, kdim), lambda i: (i, 0)), _resident((n, kdim))],
        out_specs=pl.BlockSpec((tm, n), lambda i: (i, 0)),
        out_shape=jax.ShapeDtypeStruct((t, n), BF16),
        compiler_params=_params(("arbitrary",)),
    )(a, w)


def _attention_bwd(q, k, v, dmixin, sinks, *, seq, name, comm=None):
    t = q.shape[0]
    nblk = seq // BLOCK

    def body(q_ref, kp_ref, kc_ref, vp_ref, vc_ref, do_ref, sink_ref,
             dq_ref, dkp_ref, dkc_ref, dvp_ref, dvc_ref, dsink_ref):
        n = pl.program_id(0)
        first = (n % nblk) == 0

        @pl.when(n == 0)
        def _():
            dsink_ref[...] = jnp.zeros_like(dsink_ref)

        dqs, dks, dvs = [], [], []
        srow = lax.broadcasted_iota(jnp.int32, (8, LANE), 0)
        dsink = jnp.zeros((8, LANE), F32)
        for g in range(N_KV_HEADS):
            qs, kk, vv, pn, psn = _attn_group(q_ref, kp_ref, kc_ref, vp_ref, vc_ref, sink_ref, g, first)
            dos = jnp.concatenate([do_ref[:, (GQA_GROUP * g + j) * HEAD_DIM:(GQA_GROUP * g + j + 1) * HEAD_DIM]
                                   for j in range(GQA_GROUP)], axis=0)
            dp = _dot_nt(dos, vv)
            delta = jnp.sum(pn * dp, axis=1, keepdims=True)
            ds = pn * (dp - delta)
            dsk = psn * delta
            for j in range(GQA_GROUP):
                tot = jnp.sum(dsk[j * BLOCK:(j + 1) * BLOCK, :], axis=0, keepdims=True)
                dsink = dsink - jnp.where(srow == GQA_GROUP * g + j, tot, 0.0)
            dsb = (ds * (HEAD_DIM ** -0.5)).astype(BF16)
            dqg = jnp.dot(dsb, kk, preferred_element_type=F32)
            dqs += [dqg[j * BLOCK:(j + 1) * BLOCK, :] for j in range(GQA_GROUP)]
            dks.append(_dot_tn(dsb, qs))
            dvs.append(_dot_tn(pn.astype(BF16), dos))
        dsink_ref[...] += dsink
        dq_ref[...] = jnp.concatenate(dqs, axis=1)
        dkp_ref[...] = jnp.concatenate([x[0:BLOCK, :] for x in dks], axis=1)
        dkc_ref[...] = jnp.concatenate([x[BLOCK:, :] for x in dks], axis=1)
        dvp_ref[...] = jnp.concatenate([x[0:BLOCK, :] for x in dvs], axis=1)
        dvc_ref[...] = jnp.concatenate([x[BLOCK:, :] for x in dvs], axis=1)

    cur = lambda w: pl.BlockSpec((BLOCK, w), lambda n: (n, 0))
    prev = lambda w: pl.BlockSpec((BLOCK, w), lambda n: (jnp.maximum(n - 1, 0), 0))
    kv = jax.ShapeDtypeStruct((t, KV_WIDTH), F32)
    return _pcall(
        body, name=name, grid=(t // BLOCK,),
        in_specs=[cur(ATTN_WIDTH), prev(KV_WIDTH), cur(KV_WIDTH), prev(KV_WIDTH), cur(KV_WIDTH), cur(ATTN_WIDTH),
                  pl.BlockSpec(memory_space=pltpu.SMEM)],
        out_specs=[cur(ATTN_WIDTH), cur(KV_WIDTH), cur(KV_WIDTH), cur(KV_WIDTH), cur(KV_WIDTH), _full((8, LANE))],
        out_shape=[jax.ShapeDtypeStruct((t, ATTN_WIDTH), F32), kv, kv, kv, kv, jax.ShapeDtypeStruct((8, LANE), F32)],
        args=(q, k, k, v, v, dmixin, sinks), comm=comm)


def _mix_bwd_assemble(dq, dkp, dkc, dvp, dvc, cos, sa, sb, dmixin, ubc, cw, *, seq, name, comm=None):
    t = dq.shape[0]
    nblk = seq // BLOCK
    ntile = t // BLOCK
    cwid = CONV_WIDTH
    tm = BLOCK

    def body(dq_ref, dkc_ref, dkp_ref, dvc_ref, dvp_ref, cos_ref, sa_ref, sb_ref, dco_ref, dcon_ref,
             ubc_ref, hprev_ref, hnext_ref, cw_ref, dproj_ref, dcw_ref, zbuf, dybuf):
        i = pl.program_id(0)
        first = (i % nblk) == 0
        last = (i % nblk) == nblk - 1
        glast = i == ntile - 1

        @pl.when(i == 0)
        def _():
            dcw_ref[...] = jnp.zeros_like(dcw_ref)

        cos_t, sa_t, sb_t = cos_ref[...], sa_ref[...], sb_ref[...]
        for j in range(ATTN_WIDTH // LANE):
            dproj_ref[:, j * LANE:(j + 1) * LANE] = _rope_t(
                dq_ref[:, j * LANE:(j + 1) * LANE], cos_t, sa_t, sb_t).astype(BF16)
        dk = dkc_ref[...] + jnp.where(glast, 0.0, dkp_ref[...])
        dproj_ref[:, ATTN_WIDTH:ATTN_WIDTH + KV_WIDTH] = _rope_t(dk, cos_t, sa_t, sb_t).astype(BF16)
        dv = dvc_ref[...] + jnp.where(glast, 0.0, dvp_ref[...])
        dproj_ref[:, ATTN_WIDTH + KV_WIDTH:ATTN_WIDTH + 2 * KV_WIDTH] = dv.astype(BF16)

        u, bg, cg = ubc_ref[:, 0:cwid], ubc_ref[:, cwid:2 * cwid], ubc_ref[:, 2 * cwid:3 * cwid]
        z = cg * u
        hz = hprev_ref[:, 2 * cwid:3 * cwid] * hprev_ref[:, 0:cwid]
        zbuf[0:8, :] = jnp.where(first, 0.0, hz)
        zbuf[8:8 + tm, :] = z
        z2, z1 = zbuf[6:6 + tm, :], zbuf[7:7 + tm, :]
        w0, w1, w2 = cw_ref[0:1, :], cw_ref[1:2, :], cw_ref[2:3, :]
        y = w0 * z2 + w1 * z1 + w2 * z
        dco = dco_ref[...].astype(F32)
        dyc = dco * bg
        dyn = dcon_ref[0:8, :].astype(F32) * hnext_ref[:, cwid:2 * cwid]
        dybuf[0:tm, :] = dyc
        dybuf[tm:tm + 8, :] = jnp.where(last, 0.0, dyn)
        dz = w2 * dyc + w1 * dybuf[1:1 + tm, :] + w0 * dybuf[2:2 + tm, :]
        srow = lax.broadcasted_iota(jnp.int32, (8, cwid), 0)
        dcw_ref[...] += (jnp.where(srow == 0, _row_sum(dyc * z2), 0.0) + jnp.where(srow == 1, _row_sum(dyc * z1), 0.0)
                         + jnp.where(srow == 2, _row_sum(dyc * z), 0.0))
        base = ATTN_WIDTH + 2 * KV_WIDTH
        dproj_ref[:, base:base + cwid] = (dz * cg).astype(BF16)
        dproj_ref[:, base + cwid:base + 2 * cwid] = (dco * y).astype(BF16)
        dproj_ref[:, base + 2 * cwid:base + 3 * cwid] = (dz * u).astype(BF16)

    cur = lambda w: pl.BlockSpec((tm, w), lambda i: (i, 0))
    nxt = lambda w: pl.BlockSpec((tm, w), lambda i: (jnp.minimum(i + 1, ntile - 1), 0))
    return _pcall(
        body, name=name, grid=(ntile,),
        in_specs=[cur(ATTN_WIDTH), cur(KV_WIDTH), nxt(KV_WIDTH), cur(KV_WIDTH), nxt(KV_WIDTH),
                  cur(LANE), cur(LANE), cur(LANE),
                  pl.BlockSpec((tm, cwid), lambda i: (i, 1)),
                  pl.BlockSpec((16, cwid), lambda i: (jnp.minimum((i + 1) * (tm // 16), t // 16 - 1), 1)),
                  cur(3 * cwid),
                  pl.BlockSpec((8, 3 * cwid), lambda i: (jnp.maximum(i * (tm // 8) - 1, 0), 0)),
                  pl.BlockSpec((8, 3 * cwid), lambda i: (jnp.minimum((i + 1) * (tm // 8), t // 8 - 1), 0)),
                  _full((8, cwid))],
        out_specs=[cur(IN_WIDTH), _full((8, cwid))],
        out_shape=[jax.ShapeDtypeStruct((t, IN_WIDTH), BF16), jax.ShapeDtypeStruct((8, cwid), F32)],
        scratch_shapes=[pltpu.VMEM((tm + 8, cwid), F32), pltpu.VMEM((tm + 8, cwid), F32)],
        args=(dq, dkc, dkp, dvc, dvp, cos, sa, sb, dmixin, dmixin, ubc, ubc, ubc, cw), comm=comm)


def _ada_fwd(c_all, w_ada, b_ada_shard, *, name, comm=None):
    nb, d = c_all.shape
    n = w_ada.shape[1]
    tn = n // 2

    def body(c_ref, w_ref, b_ref, o_ref):
        cv = c_ref[...]
        cond = cv * _sigmoid(cv)
        o_ref[...] = jnp.dot(cond, w_ref[...], preferred_element_type=F32,
                             precision=lax.Precision.HIGHEST) + b_ref[...]

    return _pcall(
        body, name=name, grid=(n // tn,),
        in_specs=[_full((nb, d)), pl.BlockSpec((d, tn), lambda j: (0, j)), pl.BlockSpec((1, tn), lambda j: (0, j))],
        out_specs=pl.BlockSpec((nb, tn), lambda j: (0, j)),
        out_shape=jax.ShapeDtypeStruct((nb, n), F32), args=(c_all, w_ada, b_ada_shard), comm=comm)


def _small_finish(gathered, dmod_all, dmod_shard, c_all_t, *, name):
    d = D_MODEL
    nb, n = dmod_shard.shape

    def body(g_ref, dm_ref, dms_ref, ct_ref, sum_ref, gw_ref, gb_ref):
        total = g_ref[0]
        for dev in range(1, N_DEV):
            total = total + g_ref[dev]
        sum_ref[...] = total
        gb_ref[...] = _row_sum(dm_ref[...])
        ctv = ct_ref[...]
        cond_t = ctv * _sigmoid(ctv)
        for jb in range(n // COL_CHUNK):
            gw_ref[:, jb * COL_CHUNK:(jb + 1) * COL_CHUNK] = jnp.dot(
                cond_t, dms_ref[:, jb * COL_CHUNK:(jb + 1) * COL_CHUNK], preferred_element_type=F32,
                precision=lax.Precision.HIGHEST)

    return pl.pallas_call(
        body, name=name, grid=(1,),
        in_specs=[_full((N_DEV, SMALL_ROWS, d)), _full((nb, N_MOD * d)), _full((nb, n)), _full((d, nb))],
        out_specs=[_full((SMALL_ROWS, d)), _full((d, n)), _full((1, N_MOD * d))],
        out_shape=[jax.ShapeDtypeStruct((SMALL_ROWS, d), F32), jax.ShapeDtypeStruct((d, n), F32),
                   jax.ShapeDtypeStruct((1, N_MOD * d), F32)],
        compiler_params=_params(("arbitrary",)),
    )(gathered, dmod_all, dmod_shard, c_all_t)


def _row_tile(r, c, budget=1 << 21):
    if r * c * 4 <= budget or r % 16:
        return r
    best = 16
    for tr in range(16, r + 1, 16):
        if r % tr == 0 and tr * c * 4 <= budget:
            best = tr
    return best


def _cast_into(w, chip, col_kind, *, name):
    r, c = w.shape
    tr = _row_tile(r, c)

    def body(chip_ref, w_ref, o_ref):
        o_ref[...] = w_ref[...].astype(BF16)

    if col_kind:
        out_spec = pl.BlockSpec((tr, c), lambda i, chip_ref: (i, chip_ref[0]))
        out_shape = jax.ShapeDtypeStruct((r, c * N_CHIPS), BF16)
    else:
        out_spec = pl.BlockSpec((tr, c), lambda i, chip_ref: (chip_ref[0] * (r // tr) + i, 0))
        out_shape = jax.ShapeDtypeStruct((r * N_CHIPS, c), BF16)
    return _pcall(body, name=name, grid=(r // tr,), in_specs=[pl.BlockSpec((tr, c), lambda i, chip_ref: (i, 0))],
                  out_specs=out_spec, out_shape=out_shape, args=(w,), prefetch=chip)


def _adamw(w, g, m, v, *, name, comm=None):
    r, c = w.shape
    tr = _row_tile(r, c)
    c1 = 1.0 - ADAM_B1 ** ADAM_STEP
    c2 = 1.0 - ADAM_B2 ** ADAM_STEP

    def body(w_ref, g_ref, m_ref, v_ref, d_ref, nm_ref, nv_ref):
        gv = g_ref[...]
        m2 = ADAM_B1 * m_ref[...] + (1.0 - ADAM_B1) * gv
        v2 = ADAM_B2 * v_ref[...] + (1.0 - ADAM_B2) * (gv * gv)
        d_ref[...] = -ADAM_LR * ((m2 / c1) / (jnp.sqrt(v2 / c2) + ADAM_EPS) + ADAM_WD * w_ref[...])
        nm_ref[...] = m2
        nv_ref[...] = v2

    spec = pl.BlockSpec((tr, c), lambda i: (i, 0))
    sh = jax.ShapeDtypeStruct((r, c), F32)
    return _pcall(body, name=name, grid=(r // tr,), in_specs=[spec] * 4, out_specs=[spec] * 3, out_shape=[sh] * 3,
                  args=(w, g, m, v), comm=comm)


def _sum_pair(pos, g3, r3, blk_of, *, name, comm=None):
    n, rows, cols = r3.shape
    tr = _row_tile(rows, cols)

    def body(pos_ref, g_ref, r_ref, s32_ref, s16_ref):
        s = g_ref[0] + r_ref[0]
        s32_ref[0] = s
        s16_ref[0] = s.astype(BF16)

    own = pl.BlockSpec((1, tr, cols), lambda p, i, pos: (blk_of(p, pos), i, 0))
    plain = pl.BlockSpec((1, tr, cols), lambda p, i, pos: (p, i, 0))
    return _pcall(
        body, name=name, grid=(n, rows // tr), in_specs=[own, plain], out_specs=[plain, plain],
        out_shape=[jax.ShapeDtypeStruct((n, rows, cols), F32), jax.ShapeDtypeStruct((n, rows, cols), BF16)],
        args=(g3, r3), prefetch=pos, comm=comm)


def _sum_final(pos, s32, recv, *, col_kind, n_shard, name, comm=None):
    if col_kind:
        rows, cols = s32.shape[1], n_shard
        own = lambda tr: pl.BlockSpec((1, tr, cols), lambda i, pos: (0, i, 2 * pos[0] + pos[1]))
    else:
        rows, cols = s32.shape[1], s32.shape[2]
        own = lambda tr: pl.BlockSpec((1, tr, cols), lambda i, pos: (2 * pos[0] + pos[1], i, 0))
    tr = _row_tile(rows, cols)

    def body(pos_ref, s_ref, r_ref, o_ref):
        o_ref[0] = ((s_ref[0] + r_ref[0].astype(F32)) + r_ref[1].astype(F32)) + r_ref[2].astype(F32)

    return _pcall(
        body, name=name, grid=(rows // tr,),
        in_specs=[own(tr), pl.BlockSpec((3, tr, cols), lambda i, pos: (0, i, 0))],
        out_specs=pl.BlockSpec((1, tr, cols), lambda i, pos: (pos[2], i, 0)),
        out_shape=jax.ShapeDtypeStruct((2, rows, cols), F32), args=(s32, recv), prefetch=pos, comm=comm)


def _position():
    return lax.axis_index("x"), lax.axis_index("y"), lax.axis_index("c")


def _allgather8(x_shard, *, name, comm=None):
    m_per, n = x_shard.shape
    nci, nco = (0, 0) if comm is None else (len(comm.inputs), len(comm.out_shapes))

    def body(*refs):
        x_ref, refs = refs[0], refs[1:]
        cin, refs = refs[:nci], refs[nci:]
        out_ref, refs = refs[0], refs[1:]
        cout, refs = refs[:nco], refs[nco:]
        (send_sems, recv_sems, local_sem), csems = refs[:3], refs[3:]
        x, y, c = _position()
        me, sibling = (x, y, c), (x, y, 1 - c)
        chips = [(1 - x, y), (x, 1 - y), (1 - x, 1 - y)]

        def rows(px, py, pc):
            return out_ref.at[pl.ds((4 * px + 2 * py + pc) * m_per, m_per), :]

        def copy(k, block, to, src=None):
            return pltpu.make_async_remote_copy(
                src_ref=rows(*block) if src is None else src, dst_ref=rows(*block),
                send_sem=send_sems.at[k], recv_sem=recv_sems.at[k], device_id=to, device_id_type=MESH)

        mine = pltpu.make_async_copy(x_ref, rows(*me), local_sem)
        mine.start()
        first = [copy(0, me, sibling, src=x_ref)]
        first += [copy(1 + j, me, (*chip, c), src=x_ref) for j, chip in enumerate(chips)]
        for cp in first:
            cp.start()
        if comm is not None:
            comm.start(cin, cout, csems)
        passed = [copy(4 + j, (*chip, c), sibling) for j, chip in enumerate(chips)]
        for j, chip in enumerate(chips):
            copy(1 + j, (*chip, c), me).wait_recv()
            passed[j].start()
        copy(0, sibling, me).wait_recv()
        for j, chip in enumerate(chips):
            copy(4 + j, (*chip, 1 - c), me).wait_recv()
        for cp in first + passed:
            cp.wait_send()
        mine.wait()
        if comm is not None:
            comm.middle(cin, cout, csems)
            comm.finish(cin, cout, csems)

    vmem = pl.BlockSpec(memory_space=pltpu.VMEM)
    sems = [pltpu.SemaphoreType.DMA((7,)), pltpu.SemaphoreType.DMA((7,)), pltpu.SemaphoreType.DMA]
    out = jax.ShapeDtypeStruct((N_DEV * m_per, n), x_shard.dtype)
    if comm is None:
        return pl.pallas_call(body, name=name, out_shape=out, in_specs=[vmem], out_specs=vmem,
                              scratch_shapes=sems)(x_shard)
    res = pl.pallas_call(
        body, name=name, out_shape=[out] + list(comm.out_shapes), in_specs=[vmem] + [ANY_SPEC] * nci,
        out_specs=[vmem] + [ANY_SPEC] * nco, scratch_shapes=sems + list(comm.sems),
        input_output_aliases={1 + i: 1 + o for i, o in comm.aliases.items()})(x_shard, *comm.inputs)
    return res[0], list(res[1:])


def _peer_chips(x, y):
    return [(1 - x, y), (x, 1 - y), (1 - x, 1 - y)]


class _GatherJob:
    def __init__(self, pieces):
        self.pieces = pieces
        n_p = len(pieces)
        self.inputs = [p[0] for p in pieces]
        self.out_shapes = [jax.ShapeDtypeStruct(p[0].shape, p[0].dtype) for p in pieces]
        for buf, col_kind, r0, nr in pieces:
            half_rows = buf.shape[0] // (2 if col_kind else 2 * N_CHIPS)
            assert r0 % 16 == 0 and nr % 16 == 0 and r0 + nr <= half_rows, (buf.shape, r0, nr)
        self.aliases = {p: p for p in range(n_p)}
        self.sems = [pltpu.SemaphoreType.DMA((3 * n_p,))] * 4

    def _region(self, cout, p, chip_idx, half):
        buf, col_kind, r0, nr = self.pieces[p]
        if col_kind:
            n = buf.shape[1] // N_CHIPS
            return cout[p].at[pl.ds(half * (buf.shape[0] // 2) + r0, nr), pl.ds(chip_idx * n, n)]
        n = buf.shape[0] // N_CHIPS
        return cout[p].at[pl.ds(chip_idx * n + half * (n // 2) + r0, nr), :]

    def _copies(self, cout, sems):
        send_sems, recv_sems, fsend_sems, frecv_sems = sems
        x, y, c = _position()
        k = 2 * x + y
        sibling = (x, y, 1 - c)
        sends, arrivals, fwds, fwd_arrivals = [], [], [], []

        def remote(region, ssem, rsem, to):
            return pltpu.make_async_remote_copy(src_ref=region, dst_ref=region, send_sem=ssem, recv_sem=rsem,
                                                device_id=to, device_id_type=MESH)

        for p in range(len(self.pieces)):
            for j, chip in enumerate(_peer_chips(x, y)):
                idx = 3 * p + j
                theirs = 2 * chip[0] + chip[1]
                sends.append(remote(self._region(cout, p, k, c), send_sems.at[idx], recv_sems.at[idx], (*chip, c)))
                arrivals.append(remote(self._region(cout, p, theirs, c), send_sems.at[idx], recv_sems.at[idx],
                                       (*chip, c)))
                fwds.append(remote(self._region(cout, p, theirs, c), fsend_sems.at[idx], frecv_sems.at[idx], sibling))
                fwd_arrivals.append(remote(self._region(cout, p, theirs, 1 - c), fsend_sems.at[idx],
                                           frecv_sems.at[idx], sibling))
        return sends, arrivals, fwds, fwd_arrivals

    def start(self, cin, cout, sems):
        for cp in self._copies(cout, sems)[0]:
            cp.start()

    def middle(self, cin, cout, sems):
        _, arrivals, fwds, _ = self._copies(cout, sems)
        for arrived, fw in zip(arrivals, fwds):
            arrived.wait_recv()
            fw.start()

    def finish(self, cin, cout, sems):
        sends, _, fwds, fwd_arrivals = self._copies(cout, sems)
        for arrived in fwd_arrivals:
            arrived.wait_recv()
        for cp in sends + fwds:
            cp.wait_send()


class _PairedJob:
    aliases = {}

    def start(self, cin, cout, sems):
        for cp in self._copies(cin, cout, sems):
            cp.start()

    def middle(self, cin, cout, sems):
        pass

    def finish(self, cin, cout, sems):
        copies = self._copies(cin, cout, sems)
        for cp in copies:
            cp.wait_recv()
        for cp in copies:
            cp.wait_send()


class _SwapJob(_PairedJob):
    def __init__(self, grads, kinds):
        self.inputs, self.kinds = list(grads), list(kinds)
        self.out_shapes, self.n_copies = [], []
        for g, kd in zip(grads, kinds):
            if kd:
                self.out_shapes.append(jax.ShapeDtypeStruct((1, g.shape[0] // 2, g.shape[1]), g.dtype))
                self.n_copies.append(1)
            else:
                n = g.shape[0] // N_CHIPS
                self.out_shapes.append(jax.ShapeDtypeStruct((N_CHIPS, n // 2, g.shape[1]), g.dtype))
                self.n_copies.append(N_CHIPS)
        total = sum(self.n_copies)
        self.sems = [pltpu.SemaphoreType.DMA((total,)), pltpu.SemaphoreType.DMA((total,))]

    def _copies(self, cin, cout, sems):
        send_sems, recv_sems = sems
        x, y, c = _position()
        copies = []
        for p, src_ref in enumerate(cin):
            for kk in range(self.n_copies[p]):
                if self.kinds[p]:
                    hr = src_ref.shape[0] // 2
                    src = src_ref.at[pl.ds((1 - c) * hr, hr), :]
                else:
                    n = src_ref.shape[0] // N_CHIPS
                    src = src_ref.at[pl.ds(kk * n + (1 - c) * (n // 2), n // 2), :]
                idx = len(copies)
                copies.append(pltpu.make_async_remote_copy(
                    src_ref=src, dst_ref=cout[p].at[kk], send_sem=send_sems.at[idx], recv_sem=recv_sems.at[idx],
                    device_id=(x, y, 1 - c), device_id_type=MESH))
        return copies


class _ExchangeJob(_PairedJob):
    def __init__(self, s16, kinds, sizes):
        self.inputs, self.kinds, self.sizes = list(s16), list(kinds), list(sizes)
        self.out_shapes = [jax.ShapeDtypeStruct((3, s.shape[1], n if kd else s.shape[2]), s.dtype)
                           for s, kd, n in zip(s16, kinds, sizes)]
        self.sems = [pltpu.SemaphoreType.DMA((3 * len(s16),)), pltpu.SemaphoreType.DMA((3 * len(s16),))]

    def _copies(self, cin, cout, sems):
        send_sems, recv_sems = sems
        x, y, c = _position()
        copies = []
        for p, src_ref in enumerate(cin):
            for j, chip in enumerate(_peer_chips(x, y)):
                kk = 2 * chip[0] + chip[1]
                n = self.sizes[p]
                src = src_ref.at[0, :, pl.ds(kk * n, n)] if self.kinds[p] else src_ref.at[kk]
                copies.append(pltpu.make_async_remote_copy(
                    src_ref=src, dst_ref=cout[p].at[j], send_sem=send_sems.at[3 * p + j],
                    recv_sem=recv_sems.at[3 * p + j], device_id=(*chip, c), device_id_type=MESH))
        return copies


class _ShareJob:
    def __init__(self, halves):
        self.inputs = list(halves)
        self.out_shapes = [jax.ShapeDtypeStruct(h.shape, h.dtype) for h in halves]
        self.aliases = {p: p for p in range(len(halves))}
        self.sems = [pltpu.SemaphoreType.DMA((len(halves),)), pltpu.SemaphoreType.DMA((len(halves),))]

    def _copies(self, cout, sems, half):
        send_sems, recv_sems = sems
        x, y, c = _position()
        h = c if half == "mine" else 1 - c
        return [pltpu.make_async_remote_copy(
            src_ref=o.at[h], dst_ref=o.at[h], send_sem=send_sems.at[p], recv_sem=recv_sems.at[p],
            device_id=(x, y, 1 - c), device_id_type=MESH) for p, o in enumerate(cout)]

    def start(self, cin, cout, sems):
        for cp in self._copies(cout, sems, "mine"):
            cp.start()

    def middle(self, cin, cout, sems):
        pass

    def finish(self, cin, cout, sems):
        for cp in self._copies(cout, sems, "theirs"):
            cp.wait_recv()
        for cp in self._copies(cout, sems, "mine"):
            cp.wait_send()


class _MultiJob:
    def __init__(self, jobs):
        self.jobs = jobs
        self.inputs = [a for j in jobs for a in j.inputs]
        self.out_shapes = [s for j in jobs for s in j.out_shapes]
        self.sems = [s for j in jobs for s in j.sems]
        self.aliases = {}
        i0 = o0 = 0
        for j in jobs:
            for i, o in j.aliases.items():
                self.aliases[i0 + i] = o0 + o
            i0 += len(j.inputs)
            o0 += len(j.out_shapes)

    def _parts(self, cin, cout, sems):
        i0 = o0 = s0 = 0
        for j in self.jobs:
            ni, no, ns = len(j.inputs), len(j.out_shapes), len(j.sems)
            yield j, cin[i0:i0 + ni], cout[o0:o0 + no], sems[s0:s0 + ns]
            i0, o0, s0 = i0 + ni, o0 + no, s0 + ns

    def start(self, cin, cout, sems):
        for j, a, b, s in self._parts(cin, cout, sems):
            j.start(a, b, s)

    def middle(self, cin, cout, sems):
        for j, a, b, s in self._parts(cin, cout, sems):
            j.middle(a, b, s)

    def finish(self, cin, cout, sems):
        for j, a, b, s in self._parts(cin, cout, sems):
            j.finish(a, b, s)


def _rope_tables(positions):
    half = ROT_DIM // 2
    inv_freq = jnp.power(jnp.float32(ROPE_THETA), -jnp.arange(0, ROT_DIM, 2, dtype=F32) / ROT_DIM)
    inv_head = jnp.concatenate([inv_freq, inv_freq, jnp.zeros((HEAD_DIM - ROT_DIM,), F32)])
    inv_lane = jnp.concatenate([inv_head] * (LANE // HEAD_DIM))
    ang = positions.astype(F32).reshape(-1)[:, None] * inv_lane[None, :]
    sin = jnp.sin(ang)
    dim = jnp.arange(LANE) % HEAD_DIM
    return jnp.cos(ang), jnp.where(dim < half, -sin, 0.0), jnp.where(dim >= half, sin, 0.0)


def kernel(x, c, positions, w_ada, b_ada, ffn1_w_gate_up, ffn1_w_down, ln1_g, ln1_b, w_in, conv_w, attn_sinks, w_out, ln2_g, ln2_b, ffn2_w_gate_up, ffn2_w_down, ln3_g, ln3_b, loss_target, m_w_ada, m_b_ada, m_ffn1_w_gate_up, m_ffn1_w_down, m_ln1_g, m_ln1_b, m_w_in, m_conv_w, m_attn_sinks, m_w_out, m_ln2_g, m_ln2_b, m_ffn2_w_gate_up, m_ffn2_w_down, m_ln3_g, m_ln3_b, v_w_ada, v_b_ada, v_ffn1_w_gate_up, v_ffn1_w_down, v_ln1_g, v_ln1_b, v_w_in, v_conv_w, v_attn_sinks, v_w_out, v_ln2_g, v_ln2_b, v_ffn2_w_gate_up, v_ffn2_w_down, v_ln3_g, v_ln3_b):
    d = D_MODEL
    nb, seq, _ = x.shape
    t = nb * seq
    f = ffn1_w_down.shape[1] * N_CHIPS
    ax, ay, ac = _position()
    chip = 2 * ax + ay
    dev = 2 * chip + ac
    pos = jnp.stack([ax, ay, ac]).astype(jnp.int32)

    x2 = x.reshape(t, d)
    tgt2 = loss_target.reshape(t, d)
    ln1 = jnp.concatenate([ln1_g, ln1_b], axis=0)
    ln2 = jnp.concatenate([ln2_g, ln2_b], axis=0)
    ln3 = jnp.concatenate([ln3_g, ln3_b], axis=0)
    sinks = attn_sinks.reshape(N_Q_HEADS)
    cos_t, sa_t, sb_t = _rope_tables(positions)

    gu_cuts = [0, 176, 352, d // 2]
    gu_part = lambda buf, s: (buf, True, gu_cuts[s], gu_cuts[s + 1] - gu_cuts[s])
    chip_arr = jnp.reshape(chip, (1,)).astype(jnp.int32)
    b_gu1 = _cast_into(ffn1_w_gate_up[0], chip_arr, True, name="cast_gu1")

    n_ada = w_ada.shape[2]
    c_all, (b_gu1,) = _allgather8(c.reshape(nb * d // LANE, LANE), name="gather_c", comm=_GatherJob([gu_part(b_gu1, 0)]))
    c_all = c_all.reshape(N_DEV * nb, d)
    b_shard = lax.dynamic_slice(b_ada, (0, chip * n_ada), (1, n_ada))
    mod_part, (b_gu1,) = _ada_fwd(c_all, w_ada[0], b_shard, name="ada_fwd", comm=_GatherJob([gu_part(b_gu1, 1)]))
    conv_rows = jnp.pad(conv_w[0], ((0, 5), (0, n_ada - conv_w.shape[2])))
    part = jnp.concatenate([mod_part, conv_rows], axis=0)
    parts, (wgu1,) = _allgather8(part, name="gather_mod", comm=_GatherJob([gu_part(b_gu1, 2)]))
    parts = parts.reshape(N_DEV, N_DEV * nb + 8, n_ada)
    mod_all = jnp.concatenate([parts[2 * k, :N_DEV * nb, :] for k in range(N_CHIPS)], axis=1)
    mod = lax.dynamic_slice(mod_all, (dev * nb, 0), (nb, N_MOD * d)).reshape(nb, N_MOD, d)
    cw_full = jnp.concatenate([parts[2 * k, N_DEV * nb:, :conv_w.shape[2]] for k in range(N_CHIPS)], axis=1)

    b_d1 = _cast_into(ffn1_w_down[0], chip_arr, False, name="cast_d1")
    b_in = _cast_into(w_in[0].T, chip_arr, False, name="cast_in")
    b_out = _cast_into(w_out[0], chip_arr, False, name="cast_out")
    b_gu2 = _cast_into(ffn2_w_gate_up[0], chip_arr, True, name="cast_gu2")
    b_d2 = _cast_into(ffn2_w_down[0], chip_arr, False, name="cast_d2")
    n_gu, n_d, n_in, n_out = (ffn1_w_gate_up.shape[2], ffn1_w_down.shape[1], w_in.shape[2], w_out.shape[1])

    def whole(buf, col_kind):
        return (buf, col_kind, 0, buf.shape[0] // (2 if col_kind else 2 * N_CHIPS))

    (h1, a1, gu1), (wd1, wout) = _ffn_up(x2, ln1, mod, wgu1, seq=seq, sc_idx=1, sh_idx=0, use_ln=False,
                                         name="ffn1_up", comm=_GatherJob([whole(b_d1, False), whole(b_out, False)]))
    (f1, xhat1, rstd1), (win_t,) = _ffn_down_ln(a1, wd1, x2, ln1, mod, seq=seq, gate_idx=2, use_ln=False,
                                                name="ffn1_down", comm=_GatherJob([whole(b_in, False)]))
    (h2, q, k, v, ubc), (b_gu2,) = _in_proj(
        xhat1, ln1, mod, win_t, cos_t, sa_t, sb_t, seq=seq, sc_idx=4, sh_idx=3, name="in_proj",
        comm=_GatherJob([gu_part(b_gu2, 0)]))
    attn, (b_gu2,) = _attention(q, k, v, sinks, seq=seq, name="attention", comm=_GatherJob([gu_part(b_gu2, 1)]))
    (mixin, mix, xhat2, rstd2), (wgu2,) = _out_proj(
        attn, ubc, cw_full, wout, xhat1, ln1, mod, seq=seq, gate_idx=5, name="out_proj",
        comm=_GatherJob([gu_part(b_gu2, 2)]))
    (h3, a3, gu3), (wd2,) = _ffn_up(xhat2, ln2, mod, wgu2, seq=seq, sc_idx=7, sh_idx=6, use_ln=True, name="ffn2_up",
                                    comm=_GatherJob([whole(b_d2, False)]))
    dr3, df3, loss_cols, dln3g, dln3b, dgate3 = _ffn_down_loss(
        a3, wd2, xhat2, ln2, mod, ln3, tgt2, seq=seq, gate_idx=8, name="ffn2_down_loss")

    def pair_sum(g, r3, col_kind, name_, comm=None):
        if col_kind:
            g3 = g.reshape(2, g.shape[0] // 2, g.shape[1])
            blk_of = lambda p_, pos_: pos_[2]
        else:
            g3 = g.reshape(2 * N_CHIPS, g.shape[0] // (2 * N_CHIPS), g.shape[1])
            blk_of = lambda p_, pos_: 2 * p_ + pos_[2]
        return _sum_pair(pos, g3, r3, blk_of, name=name_, comm=comm)

    dgu3 = _ffn_bwd_act(df3, wd2, gu3, seq=seq, name="ffn2_bwd_act")
    g_wd2 = _matmul_tn(a3, df3, tmm=f // 2, tnn=d, name="grad_wd2")
    (s32_gu2, s16_gu2), (sib_d2,) = _grad_chip_sum(pos, h3, dgu3, name="grad_wgu2", comm=_SwapJob([g_wd2], [False]))
    s32_d2, s16_d2 = pair_sum(g_wd2, sib_d2, False, "sum_pair_d2")
    (dr2, dmix, dsc3, dsh3, dgate2, dln2g, dln2b), (recv_d2,) = _bwd_in(
        dgu3, wgu2, dr3, xhat2, rstd2, ln2, mod, mix, seq=seq, w_is_nt=True, sc_idx=7, gate_idx=5,
        branch_scale=1.0, final=False, name="ffn2_bwd_in", comm=_ExchangeJob([s16_d2], [False], [n_d]))
    g_wout = _matmul_tn(mixin, dmix, tmm=d, tnn=d, name="grad_wout")
    dmixin = _matmul_nt_bf16(dmix, wout, seq=seq, name="out_proj_bwd")
    (dq, dkp, dkc, dvp, dvc, dsink), (recv_gu2, sib_out) = _attention_bwd(
        q, k, v, dmixin, sinks, seq=seq, name="attention_bwd",
        comm=_MultiJob([_ExchangeJob([s16_gu2], [True], [n_gu]), _SwapJob([g_wout], [False])]))
    s32_out, s16_out = pair_sum(g_wout, sib_out, False, "sum_pair_out")
    (dproj, dcw), (recv_out,) = _mix_bwd_assemble(
        dq, dkp, dkc, dvp, dvc, cos_t, sa_t, sb_t, dmixin, ubc, cw_full, seq=seq, name="mix_bwd",
        comm=_ExchangeJob([s16_out], [False], [n_out]))
    g_win_t = _matmul_tn(dproj, h2, tmm=IN_WIDTH // 2, tnn=d, name="grad_win")
    (dr1, df1, dsc2, dsh2, dgate1, dln1g, dln1b), (sib_in,) = _bwd_in(
        dproj, win_t, dr2, xhat1, rstd1, ln1, mod, f1, seq=seq, w_is_nt=False, sc_idx=4, gate_idx=2,
        branch_scale=0.5, final=False, name="in_proj_bwd", comm=_SwapJob([g_win_t], [False]))
    s32_in, s16_in = pair_sum(g_win_t, sib_in, False, "sum_pair_in")
    g_wd1, (recv_in,) = _matmul_tn(a1, df1, tmm=f // 2, tnn=d, name="grad_wd1",
                                   comm=_ExchangeJob([s16_in], [False], [n_in]))
    dgu1, (sib_d1,) = _ffn_bwd_act(df1, wd1, gu1, seq=seq, name="ffn1_bwd_act", comm=_SwapJob([g_wd1], [False]))
    s32_d1, s16_d1 = pair_sum(g_wd1, sib_d1, False, "sum_pair_d1")
    (s32_gu1, s16_gu1), (recv_d1,) = _grad_chip_sum(pos, h1, dgu1, name="grad_wgu1",
                                                    comm=_ExchangeJob([s16_d1], [False], [n_d]))

    def final_half(s32_, recv_, col_kind, n_shard, name_):
        return _sum_final(pos, s32_, recv_, col_kind=col_kind, n_shard=n_shard, name=name_)

    early = [final_half(s32_gu2, recv_gu2, True, n_gu, "sum_final_gu2"),
             final_half(s32_d2, recv_d2, False, n_d, "sum_final_d2"),
             final_half(s32_out, recv_out, False, n_out, "sum_final_out"),
             final_half(s32_in, recv_in, False, n_in, "sum_final_in"),
             final_half(s32_d1, recv_d1, False, n_d, "sum_final_d1")]
    (grad_x, dsc1, dsh1), (recv_gu1, full_gu2, full_d2, full_out, full_in, full_d1) = _bwd_in(
        dgu1, wgu1, dr1, x2, None, None, mod, None, seq=seq, w_is_nt=True, sc_idx=1, gate_idx=None,
        branch_scale=None, final=True, name="ffn1_bwd_in",
        comm=_MultiJob([_ExchangeJob([s16_gu1], [True], [n_gu]), _ShareJob(early)]))
    late = [final_half(s32_gu1, recv_gu1, True, n_gu, "sum_final_gu1")]

    dmod = jnp.concatenate([dsh1, dsc1, dgate1, dsh2, dsc2, dgate2, dsh3, dsc3, dgate3], axis=1)
    loss_row = jnp.sum(loss_cols, axis=1, keepdims=True) * (0.5 / d)
    lane_row = lambda a: jnp.pad(a, ((0, 0), (0, d - a.shape[1])))
    block = jnp.concatenate(
        [dmod.reshape(nb * N_MOD, d), dln1g, dln1b, dln2g, dln2b, dln3g, dln3b,
         lane_row(dcw[0:3, :]), lane_row(dsink[:, 0:1].reshape(1, N_Q_HEADS)), lane_row(loss_row)], axis=0)
    block = jnp.pad(block, ((0, SMALL_ROWS - block.shape[0]), (0, 0)))
    gathered, (full_gu1,) = _allgather8(block, name="gather_small", comm=_ShareJob(late))
    gathered = gathered.reshape(N_DEV, SMALL_ROWS, d)
    dmod_all = gathered[:, :nb * N_MOD, :].reshape(N_DEV * nb, N_MOD * d)
    dmod_shard = lax.dynamic_slice(dmod_all, (0, chip * n_ada), (N_DEV * nb, n_ada))
    small, g_w_ada, g_b_ada = _small_finish(gathered, dmod_all, dmod_shard, c_all.T, name="small_finish")
    r0 = nb * N_MOD
    loss = small[r0 + 10, 0]
    g_ln = [small[r0 + i:r0 + i + 1, :] for i in range(6)]
    g_cw_full = small[r0 + 6:r0 + 9, :CONV_WIDTH]
    g_conv = lax.dynamic_slice(g_cw_full, (0, chip * conv_w.shape[2]), (3, conv_w.shape[2]))
    g_sinks = small[r0 + 9:r0 + 10, :N_Q_HEADS]

    def flat2(a):
        return a.reshape(-1, a.shape[-1])

    def unhalve(a):
        return a.reshape(2 * a.shape[1], a.shape[2])

    results = {}

    def adamw(name_, w_, g_, m_, v_):
        g2 = flat2(g_)
        dl, nm, nv = _adamw(flat2(w_), g2, flat2(m_), flat2(v_), name="adamw_" + name_)
        results[name_] = tuple(a.reshape(w_.shape) for a in (g2, dl, nm, nv))

    adamw("w_ada", w_ada, g_w_ada, m_w_ada, v_w_ada)
    adamw("ffn2_w_gate_up", ffn2_w_gate_up, unhalve(full_gu2), m_ffn2_w_gate_up, v_ffn2_w_gate_up)
    adamw("ffn2_w_down", ffn2_w_down, unhalve(full_d2), m_ffn2_w_down, v_ffn2_w_down)
    adamw("w_out", w_out, unhalve(full_out), m_w_out, v_w_out)
    adamw("w_in", w_in, unhalve(full_in).T, m_w_in, v_w_in)
    adamw("ffn1_w_gate_up", ffn1_w_gate_up, unhalve(full_gu1), m_ffn1_w_gate_up, v_ffn1_w_gate_up)
    adamw("ffn1_w_down", ffn1_w_down, unhalve(full_d1), m_ffn1_w_down, v_ffn1_w_down)
    adamw("b_ada", b_ada, g_b_ada, m_b_ada, v_b_ada)
    adamw("ln1_g", ln1_g, g_ln[0], m_ln1_g, v_ln1_g)
    adamw("ln1_b", ln1_b, g_ln[1], m_ln1_b, v_ln1_b)
    adamw("ln2_g", ln2_g, g_ln[2], m_ln2_g, v_ln2_g)
    adamw("ln2_b", ln2_b, g_ln[3], m_ln2_b, v_ln2_b)
    adamw("ln3_g", ln3_g, g_ln[4], m_ln3_g, v_ln3_g)
    adamw("ln3_b", ln3_b, g_ln[5], m_ln3_b, v_ln3_b)
    adamw("conv_w", conv_w, g_conv, m_conv_w, v_conv_w)
    adamw("attn_sinks", attn_sinks, g_sinks, m_attn_sinks, v_attn_sinks)
    order = ["w_ada", "b_ada", "ffn1_w_gate_up", "ffn1_w_down", "ln1_g", "ln1_b", "w_in", "conv_w", "attn_sinks",
             "w_out", "ln2_g", "ln2_b", "ffn2_w_gate_up", "ffn2_w_down", "ln3_g", "ln3_b"]
    return (loss, grad_x.reshape(x.shape), *[results[n_][0] for n_ in order], *[results[n_][1] for n_ in order],
            *[results[n_][2] for n_ in order], *[results[n_][3] for n_ in order])
```

```python
import jax
import jax.numpy as jnp
from jax import lax
from jax.experimental import pallas as pl
from jax.experimental.pallas import tpu as pltpu

F32 = jnp.float32
BF16 = jnp.bfloat16
MESH = pl.DeviceIdType.MESH

D_MODEL = 1024
HEAD_DIM = 64
ATTN_WIDTH = 512
CONV_WIDTH = 512
N_Q_HEADS = 8
N_KV_HEADS = 2
GQA_GROUP = 4
KV_WIDTH = 128
WINDOW = 128
BLOCK = 128
ROT_DIM = 16
ROPE_THETA = 500000.0
N_MOD = 9
LN_EPS = 1e-5
DN_ALPHA = 2.0 ** 0.25
IN_WIDTH = 2304
N_CHIPS = 4
N_DEV = 8
SMALL_ROWS = 32

ADAM_LR = 0.001
ADAM_B1 = 0.9
ADAM_B2 = 0.999
ADAM_EPS = 1e-08
ADAM_WD = 0.01
ADAM_STEP = 10

LANE = 128
HALO = 16
COL_CHUNK = 256
VMEM_LIMIT = 56 * 1024 * 1024


def _params(sem=None, vmem=True):
    return pltpu.CompilerParams(dimension_semantics=sem, vmem_limit_bytes=VMEM_LIMIT if vmem else None)


def _sigmoid(g):
    return 0.5 * jnp.tanh(0.5 * g) + 0.5


def _row_sum(v):
    return jnp.sum(v, axis=0, keepdims=True)


ROW_CHUNK = 16
EPILOGUE_UNROLL = 8


def _fold8(v):
    return v[0:8, :] + v[8:16, :]


def _row_chunk_loop(n_rows, step, init):
    per_iter = ROW_CHUNK * EPILOGUE_UNROLL
    assert n_rows % per_iter == 0, n_rows

    def body(it, carry):
        for s in range(EPILOGUE_UNROLL):
            start = pl.multiple_of(it * per_iter + s * ROW_CHUNK, ROW_CHUNK)
            carry = step(pl.ds(start, ROW_CHUNK), carry)
        return carry

    return lax.fori_loop(0, n_rows // per_iter, body, init)


def _ln_stats(r):
    mu = jnp.mean(r, axis=-1, keepdims=True)
    rc = r - mu
    var = jnp.mean(rc * rc, axis=-1, keepdims=True)
    rstd = lax.rsqrt(var + LN_EPS)
    return rc * rstd, rstd


def _ln_bwd(dxo, xhat, rstd, g):
    dxhat = dxo * g
    m1 = jnp.mean(dxhat, axis=-1, keepdims=True)
    m2 = jnp.mean(dxhat * xhat, axis=-1, keepdims=True)
    return rstd * (dxhat - m1 - xhat * m2)


def _dot_nt(a, b):
    return lax.dot_general(a, b, (((1,), (1,)), ((), ())), preferred_element_type=F32)


def _dot_tn(a, b):
    return lax.dot_general(a, b, (((0,), (0,)), ((), ())), preferred_element_type=F32)


def _full(shape):
    nd = len(shape)
    return pl.BlockSpec(shape, lambda *_: (0,) * nd)


def _resident(shape):
    nd = len(shape)
    return pl.BlockSpec(shape, lambda *_: (0,) * nd, pipeline_mode=pl.Buffered(1))


ANY_SPEC = pl.BlockSpec(memory_space=pl.ANY)


def _pcall(body, *, name, grid, in_specs, out_specs, out_shape, args, scratch_shapes=(), comm=None, prefetch=None):
    single = not isinstance(out_shape, (list, tuple))
    out_specs = [out_specs] if single else list(out_specs)
    out_shape = [out_shape] if single else list(out_shape)
    in_specs = list(in_specs)
    scratch_shapes = list(scratch_shapes)
    sem = ("arbitrary",) * len(grid)
    n_pre = 0 if prefetch is None else 1
    pre_args = () if prefetch is None else (prefetch,)

    def call(fn, ins_, outs_, shapes_, scratch_, aliases_, operands):
        if prefetch is None:
            return pl.pallas_call(fn, name=name, grid=grid, in_specs=ins_, out_specs=outs_, out_shape=shapes_,
                                  scratch_shapes=scratch_, input_output_aliases=aliases_,
                                  compiler_params=_params(sem))(*operands)
        spec = pltpu.PrefetchScalarGridSpec(num_scalar_prefetch=1, grid=grid, in_specs=ins_, out_specs=outs_,
                                            scratch_shapes=scratch_)
        return pl.pallas_call(fn, name=name, grid_spec=spec, out_shape=shapes_,
                              input_output_aliases={n_pre + i: o for i, o in aliases_.items()},
                              compiler_params=_params(sem))(*pre_args, *operands)

    if comm is None:
        res = call(body, in_specs, out_specs, out_shape, scratch_shapes, {}, args)
        return res[0] if single else res
    n_in, n_out, n_scr = len(in_specs), len(out_specs), len(scratch_shapes)
    nci, nco = len(comm.inputs), len(comm.out_shapes)
    n_steps = 1
    for g in grid:
        n_steps *= g
    staged = n_steps >= 4
    middle_step = n_steps - 1 - max(1, n_steps // 8)

    def wrapped(*refs):
        pre, refs = refs[:n_pre], refs[n_pre:]
        ins, refs = refs[:n_in], refs[n_in:]
        cin, refs = refs[:nci], refs[nci:]
        outs, refs = refs[:n_out], refs[n_out:]
        cout, refs = refs[:nco], refs[nco:]
        scr, csems = refs[:n_scr], refs[n_scr:]
        step = pl.program_id(0)
        for ax in range(1, len(grid)):
            step = step * grid[ax] + pl.program_id(ax)

        @pl.when(step == 0)
        def _():
            comm.start(cin, cout, csems)

        body(*pre, *ins, *outs, *scr)

        if staged:
            @pl.when(step == middle_step)
            def _():
                comm.middle(cin, cout, csems)

        @pl.when(step == n_steps - 1)
        def _():
            if not staged:
                comm.middle(cin, cout, csems)
            comm.finish(cin, cout, csems)

    res = call(wrapped, in_specs + [ANY_SPEC] * nci, out_specs + [ANY_SPEC] * nco,
               out_shape + list(comm.out_shapes), scratch_shapes + list(comm.sems),
               {n_in + i: n_out + o for i, o in comm.aliases.items()}, (*args, *comm.inputs))
    main = res[:n_out]
    return (main[0] if single else main), list(res[n_out:])


def _comm_call(job, *, name):
    nci, nco = len(job.inputs), len(job.out_shapes)

    def body(*refs):
        cin, refs = refs[:nci], refs[nci:]
        cout, csems = refs[:nco], refs[nco:]
        job.start(cin, cout, csems)
        job.middle(cin, cout, csems)
        job.finish(cin, cout, csems)

    return pl.pallas_call(
        body, name=name, out_shape=list(job.out_shapes), in_specs=[ANY_SPEC] * nci, out_specs=[ANY_SPEC] * nco,
        scratch_shapes=list(job.sems), input_output_aliases=dict(job.aliases))(*job.inputs)


def _ffn_up(xin, lnp, mod, w, *, seq, sc_idx, sh_idx, use_ln, name, comm=None):
    t, d = xin.shape
    f = w.shape[1] // 2
    tm = min(512, seq)
    tpb = seq // tm
    ch = min(COL_CHUNK, f)

    def body(x_ref, ln_ref, mod_ref, w_ref, h_ref, a_ref, gu_ref):
        x = x_ref[...]
        if use_ln:
            x = x * ln_ref[0:1, :] + ln_ref[1:2, :]
        h = x * (1.0 + mod_ref[0, sc_idx:sc_idx + 1, :]) + mod_ref[0, sh_idx:sh_idx + 1, :]
        hb = h.astype(BF16)
        h_ref[...] = hb
        for j in range(f // ch):
            g = jnp.dot(hb, w_ref[:, j * ch:(j + 1) * ch], preferred_element_type=F32)
            u = jnp.dot(hb, w_ref[:, f + j * ch:f + (j + 1) * ch], preferred_element_type=F32)
            s = _sigmoid(g)
            silu = g * s
            a_ref[:, j * ch:(j + 1) * ch] = (silu * u).astype(BF16)
            gu_ref[:, j * ch:(j + 1) * ch] = (u * (s + silu * (1.0 - s))).astype(BF16)
            gu_ref[:, f + j * ch:f + (j + 1) * ch] = silu.astype(BF16)

    return _pcall(
        body, name=name, grid=(t // tm,),
        in_specs=[pl.BlockSpec((tm, d), lambda i: (i, 0)), _full((2, d)),
                  pl.BlockSpec((1, N_MOD, d), lambda i: (i // tpb, 0, 0)), _resident((d, 2 * f))],
        out_specs=[pl.BlockSpec((tm, d), lambda i: (i, 0)), pl.BlockSpec((tm, f), lambda i: (i, 0)),
                   pl.BlockSpec((tm, 2 * f), lambda i: (i, 0))],
        out_shape=[jax.ShapeDtypeStruct((t, d), BF16), jax.ShapeDtypeStruct((t, f), BF16),
                   jax.ShapeDtypeStruct((t, 2 * f), BF16)],
        args=(xin, lnp, mod, w), comm=comm)


def _ffn_down_ln(a, wd, xin, lnp_in, mod, *, seq, gate_idx, use_ln, name, comm=None):
    t, f = a.shape
    d = wd.shape[1]
    tm = min(512, seq)
    tpb = seq // tm

    def body(a_ref, wd_ref, x_ref, ln_ref, mod_ref, f_ref, xhat_ref, rstd_ref, acc):
        av = a_ref[...]
        for j in range(d // COL_CHUNK):
            acc[:, j * COL_CHUNK:(j + 1) * COL_CHUNK] = jnp.dot(
                av, wd_ref[:, j * COL_CHUNK:(j + 1) * COL_CHUNK], preferred_element_type=F32)
        scale = 0.5 * (1.0 + mod_ref[0, gate_idx:gate_idx + 1, :])

        fo = acc[...]
        x = x_ref[...]
        if use_ln:
            x = x * ln_ref[0:1, :] + ln_ref[1:2, :]
        xhat, rstd = _ln_stats(DN_ALPHA * x + scale * fo)
        f_ref[...] = fo.astype(BF16)
        xhat_ref[...] = xhat
        rstd_ref[...] = rstd

    return _pcall(
        body, name=name, grid=(t // tm,),
        in_specs=[pl.BlockSpec((tm, f), lambda i: (i, 0)), _resident((f, d)),
                  pl.BlockSpec((tm, d), lambda i: (i, 0)), _full((2, d)),
                  pl.BlockSpec((1, N_MOD, d), lambda i: (i // tpb, 0, 0))],
        out_specs=[pl.BlockSpec((tm, d), lambda i: (i, 0)), pl.BlockSpec((tm, d), lambda i: (i, 0)),
                   pl.BlockSpec((tm, 1), lambda i: (i, 0))],
        out_shape=[jax.ShapeDtypeStruct((t, d), BF16), jax.ShapeDtypeStruct((t, d), F32),
                   jax.ShapeDtypeStruct((t, 1), F32)],
        scratch_shapes=[pltpu.VMEM((tm, d), F32)],
        args=(a, wd, xin, lnp_in, mod), comm=comm)


def _ffn_down_loss(a, wd, xhat_in, lnp_in, mod, lnp_out, tgt, *, seq, gate_idx, name):
    t, f = a.shape
    d = wd.shape[1]
    nb = t // seq
    tm = min(512, seq)
    tpb = seq // tm

    def body(a_ref, wd_ref, x_ref, lnin_ref, mod_ref, lnout_ref, tgt_ref,
             dr_ref, df_ref, loss_ref, dg_ref, db_ref, dgate_ref, acc):
        i = pl.program_id(0)
        av = a_ref[...]
        for j in range(d // COL_CHUNK):
            acc[:, j * COL_CHUNK:(j + 1) * COL_CHUNK] = jnp.dot(
                av, wd_ref[:, j * COL_CHUNK:(j + 1) * COL_CHUNK], preferred_element_type=F32)
        scale = 0.5 * (1.0 + mod_ref[0, gate_idx:gate_idx + 1, :])
        g_in, b_in = lnin_ref[0:1, :], lnin_ref[1:2, :]
        g_out, b_out = lnout_ref[0:1, :], lnout_ref[1:2, :]

        def chunk(rows, carry):
            s_loss, s_dg, s_db, s_gate = carry
            fo = acc[rows, :]
            xhat, rstd = _ln_stats(DN_ALPHA * (x_ref[rows, :] * g_in + b_in) + scale * fo)
            e = xhat * g_out + b_out - tgt_ref[rows, :]
            dy = e * (1.0 / d)
            dr = _ln_bwd(dy, xhat, rstd, g_out)
            dr_ref[rows, :] = dr
            df_ref[rows, :] = (scale * dr).astype(BF16)
            return (s_loss + _fold8(e * e), s_dg + _fold8(dy * xhat), s_db + _fold8(dy),
                    s_gate + _fold8(0.5 * fo * dr))

        zero = jnp.zeros((8, d), F32)
        s_loss, s_dg, s_db, s_gate = _row_chunk_loop(tm, chunk, (zero, zero, zero, zero))

        @pl.when(i == 0)
        def _():
            loss_ref[...] = jnp.zeros_like(loss_ref)
            dg_ref[...] = jnp.zeros_like(dg_ref)
            db_ref[...] = jnp.zeros_like(db_ref)

        @pl.when(i % tpb == 0)
        def _():
            dgate_ref[...] = jnp.zeros_like(dgate_ref)

        loss_ref[...] += _row_sum(s_loss)
        dg_ref[...] += _row_sum(s_dg)
        db_ref[...] += _row_sum(s_db)
        dgate_ref[0] += _row_sum(s_gate)

    return pl.pallas_call(
        body, name=name, grid=(t // tm,), scratch_shapes=[pltpu.VMEM((tm, d), F32)],
        in_specs=[pl.BlockSpec((tm, f), lambda i: (i, 0)), _resident((f, d)),
                  pl.BlockSpec((tm, d), lambda i: (i, 0)), _full((2, d)),
                  pl.BlockSpec((1, N_MOD, d), lambda i: (i // tpb, 0, 0)), _full((2, d)),
                  pl.BlockSpec((tm, d), lambda i: (i, 0))],
        out_specs=[pl.BlockSpec((tm, d), lambda i: (i, 0)), pl.BlockSpec((tm, d), lambda i: (i, 0)),
                   _full((1, d)), _full((1, d)), _full((1, d)),
                   pl.BlockSpec((1, 1, d), lambda i: (i // tpb, 0, 0))],
        out_shape=[jax.ShapeDtypeStruct((t, d), F32), jax.ShapeDtypeStruct((t, d), BF16),
                   jax.ShapeDtypeStruct((1, d), F32), jax.ShapeDtypeStruct((1, d), F32),
                   jax.ShapeDtypeStruct((1, d), F32), jax.ShapeDtypeStruct((nb, 1, d), F32)],
        compiler_params=_params(("arbitrary",)),
    )(a, wd, xhat_in, lnp_in, mod, lnp_out, tgt)


def _rope(v, cos, sa, sb):
    return v * cos + pltpu.roll(v, LANE - ROT_DIM // 2, 1) * sa + pltpu.roll(v, ROT_DIM // 2, 1) * sb


def _rope_t(dy, cos, sa, sb):
    return dy * cos + pltpu.roll(dy * sa, ROT_DIM // 2, 1) + pltpu.roll(dy * sb, LANE - ROT_DIM // 2, 1)


def _in_proj(xhat, lnp, mod, w_t, cos, sa, sb, *, seq, sc_idx, sh_idx, name, comm=None):
    t, d = xhat.shape
    tm = min(512, seq)
    tpb = seq // tm
    n_conv = 3 * CONV_WIDTH

    def body(x_ref, ln_ref, mod_ref, w_ref, cos_ref, sa_ref, sb_ref, h_ref, q_ref, k_ref, v_ref, ubc_ref):
        x = x_ref[...] * ln_ref[0:1, :] + ln_ref[1:2, :]
        h = x * (1.0 + mod_ref[0, sc_idx:sc_idx + 1, :]) + mod_ref[0, sh_idx:sh_idx + 1, :]
        hb = h.astype(BF16)
        h_ref[...] = hb
        cos_t, sa_t, sb_t = cos_ref[...], sa_ref[...], sb_ref[...]
        for j in range(ATTN_WIDTH // COL_CHUNK):
            p = _dot_nt(hb, w_ref[j * COL_CHUNK:(j + 1) * COL_CHUNK, :])
            for s in range(COL_CHUNK // LANE):
                q_ref[:, j * COL_CHUNK + s * LANE:j * COL_CHUNK + (s + 1) * LANE] = _rope(
                    p[:, s * LANE:(s + 1) * LANE], cos_t, sa_t, sb_t).astype(BF16)
        p = _dot_nt(hb, w_ref[ATTN_WIDTH:ATTN_WIDTH + 2 * KV_WIDTH, :])
        k_ref[...] = _rope(p[:, 0:KV_WIDTH], cos_t, sa_t, sb_t).astype(BF16)
        v_ref[...] = p[:, KV_WIDTH:].astype(BF16)
        base = ATTN_WIDTH + 2 * KV_WIDTH
        for j in range(n_conv // COL_CHUNK):
            ubc_ref[:, j * COL_CHUNK:(j + 1) * COL_CHUNK] = _dot_nt(
                hb, w_ref[base + j * COL_CHUNK:base + (j + 1) * COL_CHUNK, :]).astype(BF16)

    row = lambda w: pl.BlockSpec((tm, w), lambda i: (i, 0))
    return _pcall(
        body, name=name, grid=(t // tm,),
        in_specs=[row(d), _full((2, d)), pl.BlockSpec((1, N_MOD, d), lambda i: (i // tpb, 0, 0)),
                  _resident((IN_WIDTH, d)), row(LANE), row(LANE), row(LANE)],
        out_specs=[row(d), row(ATTN_WIDTH), row(KV_WIDTH), row(KV_WIDTH), row(n_conv)],
        out_shape=[jax.ShapeDtypeStruct((t, d), BF16), jax.ShapeDtypeStruct((t, ATTN_WIDTH), BF16),
                   jax.ShapeDtypeStruct((t, KV_WIDTH), BF16), jax.ShapeDtypeStruct((t, KV_WIDTH), BF16),
                   jax.ShapeDtypeStruct((t, n_conv), BF16)],
        args=(xhat, lnp, mod, w_t, cos, sa, sb), comm=comm)


def _attn_group(q_ref, kp_ref, kc_ref, vp_ref, vc_ref, sink_ref, g, first):
    lo, hi = g * HEAD_DIM, (g + 1) * HEAD_DIM
    kk = jnp.concatenate([kp_ref[:, lo:hi], kc_ref[:, lo:hi]], axis=0)
    vv = jnp.concatenate([vp_ref[:, lo:hi], vc_ref[:, lo:hi]], axis=0)
    qs = jnp.concatenate([q_ref[:, (GQA_GROUP * g + j) * HEAD_DIM:(GQA_GROUP * g + j + 1) * HEAD_DIM]
                          for j in range(GQA_GROUP)], axis=0)
    rows = GQA_GROUP * BLOCK
    row = lax.broadcasted_iota(jnp.int32, (rows, 2 * BLOCK), 0)
    ki = lax.broadcasted_iota(jnp.int32, (rows, 2 * BLOCK), 1)
    diff = (row & (BLOCK - 1)) + BLOCK - ki
    valid = (diff >= 0) & (diff < WINDOW) & ((ki >= BLOCK) | jnp.logical_not(first))
    s = _dot_nt(qs, kk) * (HEAD_DIM ** -0.5)
    s = jnp.where(valid, s, -1e30)
    rcol = lax.broadcasted_iota(jnp.int32, (rows, 1), 0)
    sink = jnp.zeros((rows, 1), F32)
    for j in range(GQA_GROUP):
        sink = jnp.where(rcol // BLOCK == j, sink_ref[GQA_GROUP * g + j], sink)
    m = jnp.maximum(jnp.max(s, axis=1, keepdims=True), sink)
    p = jnp.exp(s - m)
    ps = jnp.exp(sink - m)
    inv = 1.0 / (jnp.sum(p, axis=1, keepdims=True) + ps)
    return qs, kk, vv, p * inv, ps * inv


def _attention(q, k, v, sinks, *, seq, name, comm=None):
    t = q.shape[0]
    nblk = seq // BLOCK

    def body(q_ref, kp_ref, kc_ref, vp_ref, vc_ref, sink_ref, o_ref):
        first = (pl.program_id(0) % nblk) == 0
        outs = []
        for g in range(N_KV_HEADS):
            _, _, vv, pn, _ = _attn_group(q_ref, kp_ref, kc_ref, vp_ref, vc_ref, sink_ref, g, first)
            o = jnp.dot(pn.astype(BF16), vv, preferred_element_type=F32)
            outs += [o[j * BLOCK:(j + 1) * BLOCK, :] for j in range(GQA_GROUP)]
        o_ref[...] = jnp.concatenate(outs, axis=1).astype(BF16)

    cur = lambda w: pl.BlockSpec((BLOCK, w), lambda n: (n, 0))
    prev = lambda w: pl.BlockSpec((BLOCK, w), lambda n: (jnp.maximum(n - 1, 0), 0))
    return _pcall(
        body, name=name, grid=(t // BLOCK,),
        in_specs=[cur(ATTN_WIDTH), prev(KV_WIDTH), cur(KV_WIDTH), prev(KV_WIDTH), cur(KV_WIDTH),
                  pl.BlockSpec(memory_space=pltpu.SMEM)],
        out_specs=cur(ATTN_WIDTH),
        out_shape=jax.ShapeDtypeStruct((t, ATTN_WIDTH), BF16),
        args=(q, k, k, v, v, sinks), comm=comm)


def _out_proj(attn, ubc, cw, wout, xhat_in, lnp_in, mod, *, seq, gate_idx, name, comm=None):
    t, d = xhat_in.shape
    tm = min(512, seq)
    tpb = seq // tm
    cwid = CONV_WIDTH

    def body(attn_ref, ubc_ref, halo_ref, cw_ref, w_ref, x_ref, ln_ref, mod_ref,
             mixin_ref, mix_ref, xhat_ref, rstd_ref, zbuf, acc):
        first = (pl.program_id(0) % tpb) == 0
        u, bg, cg = (ubc_ref[:, s * cwid:(s + 1) * cwid].astype(F32) for s in range(3))
        z = cg * u
        hz = halo_ref[:, 2 * cwid:3 * cwid].astype(F32) * halo_ref[:, 0:cwid].astype(F32)
        zbuf[0:HALO, :] = jnp.where(first, 0.0, hz)
        zbuf[HALO:HALO + tm, :] = z
        y = (cw_ref[0:1, :] * zbuf[HALO - 2:HALO - 2 + tm, :] + cw_ref[1:2, :] * zbuf[HALO - 1:HALO - 1 + tm, :]
             + cw_ref[2:3, :] * z)
        mixin_ref[:, 0:ATTN_WIDTH] = attn_ref[...]
        mixin_ref[:, ATTN_WIDTH:] = (bg * y).astype(BF16)
        mv = mixin_ref[...]
        for j in range(d // COL_CHUNK):
            acc[:, j * COL_CHUNK:(j + 1) * COL_CHUNK] = jnp.dot(
                mv, w_ref[:, j * COL_CHUNK:(j + 1) * COL_CHUNK], preferred_element_type=F32)
        scale = 1.0 + mod_ref[0, gate_idx:gate_idx + 1, :]

        mix = acc[...]
        xhat, rstd = _ln_stats(DN_ALPHA * (x_ref[...] * ln_ref[0:1, :] + ln_ref[1:2, :]) + scale * mix)
        mix_ref[...] = mix.astype(BF16)
        xhat_ref[...] = xhat
        rstd_ref[...] = rstd

    row = lambda w: pl.BlockSpec((tm, w), lambda i: (i, 0))
    return _pcall(
        body, name=name, grid=(t // tm,),
        in_specs=[row(ATTN_WIDTH), row(3 * cwid),
                  pl.BlockSpec((HALO, 3 * cwid), lambda i: (jnp.maximum(i * (tm // HALO) - 1, 0), 0)),
                  _full((8, cwid)), _resident((d, d)), row(d), _full((2, d)),
                  pl.BlockSpec((1, N_MOD, d), lambda i: (i // tpb, 0, 0))],
        out_specs=[row(d), row(d), row(d), row(1)],
        out_shape=[jax.ShapeDtypeStruct((t, d), BF16), jax.ShapeDtypeStruct((t, d), BF16),
                   jax.ShapeDtypeStruct((t, d), F32), jax.ShapeDtypeStruct((t, 1), F32)],
        scratch_shapes=[pltpu.VMEM((tm + HALO, cwid), F32), pltpu.VMEM((tm, d), F32)],
        args=(attn, ubc, ubc, cw, wout, xhat_in, lnp_in, mod), comm=comm)


def _ffn_bwd_act(df, wd, gu, *, seq, name, comm=None):
    t, d = df.shape
    f = wd.shape[0]
    tm = min(512, seq)
    ch = min(COL_CHUNK, f)

    def body(df_ref, wd_ref, gu_ref, dgu_ref):
        dfv = df_ref[...]
        for j in range(f // ch):
            da = _dot_nt(dfv, wd_ref[j * ch:(j + 1) * ch, :])
            dgu_ref[:, j * ch:(j + 1) * ch] = (da * gu_ref[:, j * ch:(j + 1) * ch].astype(F32)).astype(BF16)
            dgu_ref[:, f + j * ch:f + (j + 1) * ch] = (
                da * gu_ref[:, f + j * ch:f + (j + 1) * ch].astype(F32)).astype(BF16)

    return _pcall(
        body, name=name, grid=(t // tm,),
        in_specs=[pl.BlockSpec((tm, d), lambda i: (i, 0)), _resident((f, d)),
                  pl.BlockSpec((tm, 2 * f), lambda i: (i, 0))],
        out_specs=pl.BlockSpec((tm, 2 * f), lambda i: (i, 0)),
        out_shape=jax.ShapeDtypeStruct((t, 2 * f), BF16),
        args=(df, wd, gu), comm=comm)


def _bwd_in(a, w, dr, xin, rstd_prev, lnp_prev, mod, branch_prev, *, seq, w_is_nt, sc_idx, gate_idx,
            branch_scale, final, name, comm=None):
    t, kdim = a.shape
    d = dr.shape[1]
    nb = t // seq
    tm = min(512, seq)
    tpb = seq // tm

    def body(*refs):
        if final:
            a_ref, w_ref, dr_ref, x_ref, mod_ref, dx_ref, dsc_ref, dsh_ref, acc = refs
        else:
            (a_ref, w_ref, dr_ref, x_ref, rstd_ref, ln_ref, mod_ref, br_ref,
             drp_ref, dbr_ref, dsc_ref, dsh_ref, dgate_ref, dg_ref, db_ref, acc) = refs
        i = pl.program_id(0)
        av = a_ref[...]
        for j in range(d // COL_CHUNK):
            cols = slice(j * COL_CHUNK, (j + 1) * COL_CHUNK)
            acc[:, cols] = (_dot_nt(av, w_ref[cols, :]) if w_is_nt
                            else jnp.dot(av, w_ref[:, cols], preferred_element_type=F32))
        sc1 = 1.0 + mod_ref[0, sc_idx:sc_idx + 1, :]
        if not final:
            g_prev, b_prev = ln_ref[0:1, :], ln_ref[1:2, :]
            bscale = branch_scale * (1.0 + mod_ref[0, gate_idx:gate_idx + 1, :])

        def chunk(rows, carry):
            dh = acc[rows, :]
            dx = DN_ALPHA * dr_ref[rows, :] + dh * sc1
            if final:
                dx_ref[rows, :] = dx
                return carry[0] + _fold8(dh * x_ref[rows, :]), carry[1] + _fold8(dh)
            xhat = x_ref[rows, :]
            drp = _ln_bwd(dx, xhat, rstd_ref[rows, :], g_prev)
            drp_ref[rows, :] = drp
            dbr_ref[rows, :] = (bscale * drp).astype(BF16)
            return (carry[0] + _fold8(dh * (xhat * g_prev + b_prev)), carry[1] + _fold8(dh),
                    carry[2] + _fold8(branch_scale * br_ref[rows, :].astype(F32) * drp),
                    carry[3] + _fold8(dx * xhat), carry[4] + _fold8(dx))

        zero = jnp.zeros((8, d), F32)
        sums = _row_chunk_loop(tm, chunk, (zero,) * (2 if final else 5))

        @pl.when((i % tpb) == 0)
        def _():
            dsc_ref[...] = jnp.zeros_like(dsc_ref)
            dsh_ref[...] = jnp.zeros_like(dsh_ref)
            if not final:
                dgate_ref[...] = jnp.zeros_like(dgate_ref)

        dsc_ref[0] += _row_sum(sums[0])
        dsh_ref[0] += _row_sum(sums[1])
        if not final:
            @pl.when(i == 0)
            def _():
                dg_ref[...] = jnp.zeros_like(dg_ref)
                db_ref[...] = jnp.zeros_like(db_ref)

            dgate_ref[0] += _row_sum(sums[2])
            dg_ref[...] += _row_sum(sums[3])
            db_ref[...] += _row_sum(sums[4])

    row = lambda w_: pl.BlockSpec((tm, w_), lambda i: (i, 0))
    vec = pl.BlockSpec((1, 1, d), lambda i: (i // tpb, 0, 0))
    mod_spec = pl.BlockSpec((1, N_MOD, d), lambda i: (i // tpb, 0, 0))
    vshape = jax.ShapeDtypeStruct((nb, 1, d), F32)
    if final:
        in_specs = [row(kdim), _resident(w.shape), row(d), row(d), mod_spec]
        args = (a, w, dr, xin, mod)
        out_specs = [row(d), vec, vec]
        out_shape = [jax.ShapeDtypeStruct((t, d), F32), vshape, vshape]
    else:
        in_specs = [row(kdim), _resident(w.shape), row(d), row(d), row(1), _full((2, d)), mod_spec, row(d)]
        args = (a, w, dr, xin, rstd_prev, lnp_prev, mod, branch_prev)
        out_specs = [row(d), row(d), vec, vec, vec, _full((1, d)), _full((1, d))]
        out_shape = [jax.ShapeDtypeStruct((t, d), F32), jax.ShapeDtypeStruct((t, d), BF16), vshape, vshape, vshape,
                     jax.ShapeDtypeStruct((1, d), F32), jax.ShapeDtypeStruct((1, d), F32)]
    return _pcall(
        body, name=name, grid=(t // tm,), in_specs=in_specs, out_specs=out_specs, out_shape=out_shape,
        scratch_shapes=[pltpu.VMEM((tm, d), F32)], args=args, comm=comm)


def _matmul_tn(a, b, *, tmm, tnn, name, comm=None):
    t, m = a.shape
    n = b.shape[1]
    tk = min(2048, t)

    def body(a_ref, b_ref, o_ref):
        @pl.when(pl.program_id(2) == 0)
        def _():
            o_ref[...] = jnp.zeros_like(o_ref)
        o_ref[...] += _dot_tn(a_ref[...], b_ref[...])

    return _pcall(
        body, name=name, grid=(m // tmm, n // tnn, t // tk),
        in_specs=[pl.BlockSpec((tk, tmm), lambda i, j, k: (k, i)), pl.BlockSpec((tk, tnn), lambda i, j, k: (k, j))],
        out_specs=pl.BlockSpec((tmm, tnn), lambda i, j, k: (i, j)),
        out_shape=jax.ShapeDtypeStruct((m, n), F32),
        args=(a, b), comm=comm)


def _grad_chip_sum(pos, a, b, *, name, comm=None):
    t, m = a.shape
    n = b.shape[1]
    hm, tnn = m // 2, n // N_CHIPS
    tk = min(2048, t)
    nk = t // tk
    n_j = n // tnn

    def body(pos_ref, a_ref, b_ref, s32_ref, s16_ref, land_ref, acc, theirs, send_sems, recv_sems, copy_sem):
        p, j, k = pl.program_id(0), pl.program_id(1), pl.program_id(2)
        x, y, c = _position()

        def push(jj):
            return pltpu.make_async_remote_copy(
                src_ref=acc.at[jj], dst_ref=land_ref.at[jj], send_sem=send_sems.at[jj], recv_sem=recv_sems.at[jj],
                device_id=(x, y, 1 - c), device_id_type=MESH)

        fetch = pltpu.make_async_copy(land_ref.at[j], theirs, copy_sem)

        @pl.when(jnp.logical_and(p == 1, k == 0))
        def _():
            push(j).wait_send()
            push(j).wait_recv()
            fetch.start()

        part = _dot_tn(a_ref[...], b_ref[...])

        @pl.when(k == 0)
        def _():
            acc[j] = part

        @pl.when(k > 0)
        def _():
            acc[j] += part

        @pl.when(jnp.logical_and(p == 0, k == nk - 1))
        def _():
            push(j).start()

        @pl.when(jnp.logical_and(p == 1, k == nk - 1))
        def _():
            fetch.wait()
            s = acc[j] + theirs[...]
            s32_ref[0] = s
            s16_ref[0] = s.astype(BF16)

    half = lambda p, pos_ref: 1 - pos_ref[2] - p + 2 * p * pos_ref[2]
    out_tile = pl.BlockSpec((1, hm, tnn), lambda p, j, k, pos_ref: (0, 0, j * p))
    shape = lambda dt: jax.ShapeDtypeStruct((1, hm, n), dt)
    out = _pcall(
        body, name=name, grid=(2, n_j, nk),
        in_specs=[pl.BlockSpec((tk, hm), lambda p, j, k, pos_ref: (k, half(p, pos_ref))),
                  pl.BlockSpec((tk, tnn), lambda p, j, k, pos_ref: (k, j))],
        out_specs=[out_tile, out_tile, ANY_SPEC],
        out_shape=[shape(F32), shape(BF16), jax.ShapeDtypeStruct((n_j, hm, tnn), F32)],
        scratch_shapes=[pltpu.VMEM((n_j, hm, tnn), F32), pltpu.VMEM((hm, tnn), F32),
                        pltpu.SemaphoreType.DMA((n_j,)), pltpu.SemaphoreType.DMA((n_j,)), pltpu.SemaphoreType.DMA],
        args=(a, b), prefetch=pos, comm=comm)
    if comm is None:
        return out[0], out[1]
    (s32, s16, _), extra = out
    return (s32, s16), extra


def _matmul_nt_bf16(a, w, *, seq, name):
    t, kdim = a.shape
    n = w.shape[0]
    tm = min(512, seq)

    def body(a_ref, w_ref, o_ref):
        av = a_ref[...]
        for j in range(n // COL_CHUNK):
            o_ref[:, j * COL_CHUNK:(j + 1) * COL_CHUNK] = _dot_nt(
                av, w_ref[j * COL_CHUNK:(j + 1) * COL_CHUNK, :]).astype(BF16)

    return pl.pallas_call(
        body, name=name, grid=(t // tm,),
        in_specs=[pl.BlockSpec((tm, kdim), lambda i: (i, 0)), _resident((n, kdim))],
        out_specs=pl.BlockSpec((tm, n), lambda i: (i, 0)),
        out_shape=jax.ShapeDtypeStruct((t, n), BF16),
        compiler_params=_params(("arbitrary",)),
    )(a, w)


def _attention_bwd(q, k, v, dmixin, sinks, *, seq, name, comm=None):
    t = q.shape[0]
    nblk = seq // BLOCK

    def body(q_ref, kp_ref, kc_ref, vp_ref, vc_ref, do_ref, sink_ref,
             dq_ref, dkp_ref, dkc_ref, dvp_ref, dvc_ref, dsink_ref):
        n = pl.program_id(0)
        first = (n % nblk) == 0

        @pl.when(n == 0)
        def _():
            dsink_ref[...] = jnp.zeros_like(dsink_ref)

        dqs, dks, dvs = [], [], []
        srow = lax.broadcasted_iota(jnp.int32, (8, LANE), 0)
        dsink = jnp.zeros((8, LANE), F32)
        for g in range(N_KV_HEADS):
            qs, kk, vv, pn, psn = _attn_group(q_ref, kp_ref, kc_ref, vp_ref, vc_ref, sink_ref, g, first)
            dos = jnp.concatenate([do_ref[:, (GQA_GROUP * g + j) * HEAD_DIM:(GQA_GROUP * g + j + 1) * HEAD_DIM]
                                   for j in range(GQA_GROUP)], axis=0)
            dp = _dot_nt(dos, vv)
            delta = jnp.sum(pn * dp, axis=1, keepdims=True)
            ds = pn * (dp - delta)
            dsk = psn * delta
            for j in range(GQA_GROUP):
                tot = jnp.sum(dsk[j * BLOCK:(j + 1) * BLOCK, :], axis=0, keepdims=True)
                dsink = dsink - jnp.where(srow == GQA_GROUP * g + j, tot, 0.0)
            dsb = (ds * (HEAD_DIM ** -0.5)).astype(BF16)
            dqg = jnp.dot(dsb, kk, preferred_element_type=F32)
            dqs += [dqg[j * BLOCK:(j + 1) * BLOCK, :] for j in range(GQA_GROUP)]
            dks.append(_dot_tn(dsb, qs))
            dvs.append(_dot_tn(pn.astype(BF16), dos))
        dsink_ref[...] += dsink
        dq_ref[...] = jnp.concatenate(dqs, axis=1)
        dkp_ref[...] = jnp.concatenate([x[0:BLOCK, :] for x in dks], axis=1)
        dkc_ref[...] = jnp.concatenate([x[BLOCK:, :] for x in dks], axis=1)
        dvp_ref[...] = jnp.concatenate([x[0:BLOCK, :] for x in dvs], axis=1)
        dvc_ref[...] = jnp.concatenate([x[BLOCK:, :] for x in dvs], axis=1)

    cur = lambda w: pl.BlockSpec((BLOCK, w), lambda n: (n, 0))
    prev = lambda w: pl.BlockSpec((BLOCK, w), lambda n: (jnp.maximum(n - 1, 0), 0))
    kv = jax.ShapeDtypeStruct((t, KV_WIDTH), F32)
    return _pcall(
        body, name=name, grid=(t // BLOCK,),
        in_specs=[cur(ATTN_WIDTH), prev(KV_WIDTH), cur(KV_WIDTH), prev(KV_WIDTH), cur(KV_WIDTH), cur(ATTN_WIDTH),
                  pl.BlockSpec(memory_space=pltpu.SMEM)],
        out_specs=[cur(ATTN_WIDTH), cur(KV_WIDTH), cur(KV_WIDTH), cur(KV_WIDTH), cur(KV_WIDTH), _full((8, LANE))],
        out_shape=[jax.ShapeDtypeStruct((t, ATTN_WIDTH), F32), kv, kv, kv, kv, jax.ShapeDtypeStruct((8, LANE), F32)],
        args=(q, k, k, v, v, dmixin, sinks), comm=comm)


def _mix_bwd_assemble(dq, dkp, dkc, dvp, dvc, cos, sa, sb, dmixin, ubc, cw, *, seq, name, comm=None):
    t = dq.shape[0]
    cwid = CONV_WIDTH
    tm = min(2 * BLOCK, seq)
    tiles_per_seq = seq // tm
    ntile = t // tm
    nblk_all = t // BLOCK
    per_tile = tm // BLOCK

    def body(*refs):
        dq_ref, dkc_ref, dvc_ref = refs[0:3]
        dkp_refs, dvp_refs = refs[3:3 + per_tile], refs[3 + per_tile:3 + 2 * per_tile]
        (cos_ref, sa_ref, sb_ref, dco_ref, dcon_ref, ubc_ref, hprev_ref, hnext_ref, cw_ref,
         dproj_ref, dcw_ref, zbuf, dybuf) = refs[3 + 2 * per_tile:]
        i = pl.program_id(0)
        first = (i % tiles_per_seq) == 0
        last = (i % tiles_per_seq) == tiles_per_seq - 1
        glast = i == ntile - 1

        @pl.when(i == 0)
        def _():
            dcw_ref[...] = jnp.zeros_like(dcw_ref)

        def with_next_block(cur_ref, nxt_refs):
            nxt = [r[...] for r in nxt_refs]
            nxt[-1] = jnp.where(glast, 0.0, nxt[-1])
            return cur_ref[...] + jnp.concatenate(nxt, axis=0)

        cos_t, sa_t, sb_t = cos_ref[...], sa_ref[...], sb_ref[...]
        for j in range(ATTN_WIDTH // LANE):
            dproj_ref[:, j * LANE:(j + 1) * LANE] = _rope_t(
                dq_ref[:, j * LANE:(j + 1) * LANE], cos_t, sa_t, sb_t).astype(BF16)
        dk = with_next_block(dkc_ref, dkp_refs)
        dproj_ref[:, ATTN_WIDTH:ATTN_WIDTH + KV_WIDTH] = _rope_t(dk, cos_t, sa_t, sb_t).astype(BF16)
        dv = with_next_block(dvc_ref, dvp_refs)
        dproj_ref[:, ATTN_WIDTH + KV_WIDTH:ATTN_WIDTH + 2 * KV_WIDTH] = dv.astype(BF16)

        u, bg, cg = (ubc_ref[:, s * cwid:(s + 1) * cwid].astype(F32) for s in range(3))
        z = cg * u
        hz = hprev_ref[:, 2 * cwid:3 * cwid].astype(F32) * hprev_ref[:, 0:cwid].astype(F32)
        zbuf[0:HALO, :] = jnp.where(first, 0.0, hz)
        zbuf[HALO:HALO + tm, :] = z
        z2, z1 = zbuf[HALO - 2:HALO - 2 + tm, :], zbuf[HALO - 1:HALO - 1 + tm, :]
        w0, w1, w2 = cw_ref[0:1, :], cw_ref[1:2, :], cw_ref[2:3, :]
        y = w0 * z2 + w1 * z1 + w2 * z
        dco = dco_ref[...].astype(F32)
        dyc = dco * bg
        dyn = dcon_ref[...].astype(F32) * hnext_ref[:, cwid:2 * cwid].astype(F32)
        dybuf[0:tm, :] = dyc
        dybuf[tm:tm + HALO, :] = jnp.where(last, 0.0, dyn)
        dz = w2 * dyc + w1 * dybuf[1:1 + tm, :] + w0 * dybuf[2:2 + tm, :]
        srow = lax.broadcasted_iota(jnp.int32, (8, cwid), 0)
        dcw_ref[...] += (jnp.where(srow == 0, _row_sum(dyc * z2), 0.0) + jnp.where(srow == 1, _row_sum(dyc * z1), 0.0)
                         + jnp.where(srow == 2, _row_sum(dyc * z), 0.0))
        base = ATTN_WIDTH + 2 * KV_WIDTH
        dproj_ref[:, base:base + cwid] = (dz * cg).astype(BF16)
        dproj_ref[:, base + cwid:base + 2 * cwid] = (dco * y).astype(BF16)
        dproj_ref[:, base + 2 * cwid:base + 3 * cwid] = (dz * u).astype(BF16)

    cur = lambda w: pl.BlockSpec((tm, w), lambda i: (i, 0))
    nxt = [pl.BlockSpec((BLOCK, KV_WIDTH), lambda i, s=s: (jnp.minimum(i * per_tile + s + 1, nblk_all - 1), 0))
           for s in range(per_tile)]
    prev_halo = pl.BlockSpec((HALO, 3 * cwid), lambda i: (jnp.maximum(i * (tm // HALO) - 1, 0), 0))
    next_halo = lambda w, col: pl.BlockSpec(
        (HALO, w), lambda i: (jnp.minimum((i + 1) * (tm // HALO), t // HALO - 1), col))
    return _pcall(
        body, name=name, grid=(ntile,),
        in_specs=[cur(ATTN_WIDTH), cur(KV_WIDTH), cur(KV_WIDTH), *nxt, *nxt,
                  cur(LANE), cur(LANE), cur(LANE),
                  pl.BlockSpec((tm, cwid), lambda i: (i, 1)), next_halo(cwid, 1),
                  cur(3 * cwid), prev_halo, next_halo(3 * cwid, 0), _full((8, cwid))],
        out_specs=[cur(IN_WIDTH), _full((8, cwid))],
        out_shape=[jax.ShapeDtypeStruct((t, IN_WIDTH), BF16), jax.ShapeDtypeStruct((8, cwid), F32)],
        scratch_shapes=[pltpu.VMEM((tm + HALO, cwid), F32), pltpu.VMEM((tm + HALO, cwid), F32)],
        args=(dq, dkc, dvc, *([dkp] * per_tile), *([dvp] * per_tile), cos, sa, sb, dmixin, dmixin,
              ubc, ubc, ubc, cw), comm=comm)


def _ada_fwd(c_all, w_ada, b_ada_shard, *, name, comm=None):
    nb, d = c_all.shape
    n = w_ada.shape[1]
    tn = n // 2

    def body(c_ref, w_ref, b_ref, o_ref):
        cv = c_ref[...]
        cond = cv * _sigmoid(cv)
        o_ref[...] = jnp.dot(cond, w_ref[...], preferred_element_type=F32,
                             precision=lax.Precision.HIGHEST) + b_ref[...]

    return _pcall(
        body, name=name, grid=(n // tn,),
        in_specs=[_full((nb, d)), pl.BlockSpec((d, tn), lambda j: (0, j)), pl.BlockSpec((1, tn), lambda j: (0, j))],
        out_specs=pl.BlockSpec((nb, tn), lambda j: (0, j)),
        out_shape=jax.ShapeDtypeStruct((nb, n), F32), args=(c_all, w_ada, b_ada_shard), comm=comm)


def _small_finish(gathered, dmod_all, dmod_shard, c_all_t, *, name):
    d = D_MODEL
    nb, n = dmod_shard.shape

    def body(g_ref, dm_ref, dms_ref, ct_ref, sum_ref, gw_ref, gb_ref):
        total = g_ref[0]
        for dev in range(1, N_DEV):
            total = total + g_ref[dev]
        sum_ref[...] = total
        gb_ref[...] = _row_sum(dm_ref[...])
        ctv = ct_ref[...]
        cond_t = ctv * _sigmoid(ctv)
        for jb in range(n // COL_CHUNK):
            gw_ref[:, jb * COL_CHUNK:(jb + 1) * COL_CHUNK] = jnp.dot(
                cond_t, dms_ref[:, jb * COL_CHUNK:(jb + 1) * COL_CHUNK], preferred_element_type=F32,
                precision=lax.Precision.HIGHEST)

    return pl.pallas_call(
        body, name=name, grid=(1,),
        in_specs=[_full((N_DEV, SMALL_ROWS, d)), _full((nb, N_MOD * d)), _full((nb, n)), _full((d, nb))],
        out_specs=[_full((SMALL_ROWS, d)), _full((d, n)), _full((1, N_MOD * d))],
        out_shape=[jax.ShapeDtypeStruct((SMALL_ROWS, d), F32), jax.ShapeDtypeStruct((d, n), F32),
                   jax.ShapeDtypeStruct((1, N_MOD * d), F32)],
        compiler_params=_params(("arbitrary",)),
    )(gathered, dmod_all, dmod_shard, c_all_t)


def _row_tile(r, c, budget=1 << 21):
    if r * c * 4 <= budget or r % 16:
        return r
    best = 16
    for tr in range(16, r + 1, 16):
        if r % tr == 0 and tr * c * 4 <= budget:
            best = tr
    return best


def _cast_into(w, chip, col_kind, *, name):
    r, c = w.shape
    tr = _row_tile(r, c)

    def body(chip_ref, w_ref, o_ref):
        o_ref[...] = w_ref[...].astype(BF16)

    if col_kind:
        out_spec = pl.BlockSpec((tr, c), lambda i, chip_ref: (i, chip_ref[0]))
        out_shape = jax.ShapeDtypeStruct((r, c * N_CHIPS), BF16)
    else:
        out_spec = pl.BlockSpec((tr, c), lambda i, chip_ref: (chip_ref[0] * (r // tr) + i, 0))
        out_shape = jax.ShapeDtypeStruct((r * N_CHIPS, c), BF16)
    return _pcall(body, name=name, grid=(r // tr,), in_specs=[pl.BlockSpec((tr, c), lambda i, chip_ref: (i, 0))],
                  out_specs=out_spec, out_shape=out_shape, args=(w,), prefetch=chip)


def _adamw(w, g, m, v, *, name, comm=None):
    r, c = w.shape
    tr = _row_tile(r, c)
    c1 = 1.0 - ADAM_B1 ** ADAM_STEP
    c2 = 1.0 - ADAM_B2 ** ADAM_STEP

    def body(w_ref, g_ref, m_ref, v_ref, d_ref, nm_ref, nv_ref):
        gv = g_ref[...]
        m2 = ADAM_B1 * m_ref[...] + (1.0 - ADAM_B1) * gv
        v2 = ADAM_B2 * v_ref[...] + (1.0 - ADAM_B2) * (gv * gv)
        d_ref[...] = -ADAM_LR * ((m2 / c1) / (jnp.sqrt(v2 / c2) + ADAM_EPS) + ADAM_WD * w_ref[...])
        nm_ref[...] = m2
        nv_ref[...] = v2

    spec = pl.BlockSpec((tr, c), lambda i: (i, 0))
    sh = jax.ShapeDtypeStruct((r, c), F32)
    return _pcall(body, name=name, grid=(r // tr,), in_specs=[spec] * 4, out_specs=[spec] * 3, out_shape=[sh] * 3,
                  args=(w, g, m, v), comm=comm)


def _sum_pair(pos, g3, r3, blk_of, *, name, comm=None):
    n, rows, cols = r3.shape
    tr = _row_tile(rows, cols)

    def body(pos_ref, g_ref, r_ref, s32_ref, s16_ref):
        s = g_ref[0] + r_ref[0]
        s32_ref[0] = s
        s16_ref[0] = s.astype(BF16)

    own = pl.BlockSpec((1, tr, cols), lambda p, i, pos: (blk_of(p, pos), i, 0))
    plain = pl.BlockSpec((1, tr, cols), lambda p, i, pos: (p, i, 0))
    return _pcall(
        body, name=name, grid=(n, rows // tr), in_specs=[own, plain], out_specs=[plain, plain],
        out_shape=[jax.ShapeDtypeStruct((n, rows, cols), F32), jax.ShapeDtypeStruct((n, rows, cols), BF16)],
        args=(g3, r3), prefetch=pos, comm=comm)


def _sum_final(pos, s32, recv, *, col_kind, n_shard, name, comm=None):
    if col_kind:
        rows, cols = s32.shape[1], n_shard
        own = lambda tr: pl.BlockSpec((1, tr, cols), lambda i, pos: (0, i, 2 * pos[0] + pos[1]))
    else:
        rows, cols = s32.shape[1], s32.shape[2]
        own = lambda tr: pl.BlockSpec((1, tr, cols), lambda i, pos: (2 * pos[0] + pos[1], i, 0))
    tr = _row_tile(rows, cols)

    def body(pos_ref, s_ref, r_ref, o_ref):
        o_ref[0] = ((s_ref[0] + r_ref[0].astype(F32)) + r_ref[1].astype(F32)) + r_ref[2].astype(F32)

    return _pcall(
        body, name=name, grid=(rows // tr,),
        in_specs=[own(tr), pl.BlockSpec((3, tr, cols), lambda i, pos: (0, i, 0))],
        out_specs=pl.BlockSpec((1, tr, cols), lambda i, pos: (pos[2], i, 0)),
        out_shape=jax.ShapeDtypeStruct((2, rows, cols), F32), args=(s32, recv), prefetch=pos, comm=comm)


def _position():
    return lax.axis_index("x"), lax.axis_index("y"), lax.axis_index("c")


def _allgather8(x_shard, *, name, comm=None):
    m_per, n = x_shard.shape
    nci, nco = (0, 0) if comm is None else (len(comm.inputs), len(comm.out_shapes))

    def body(*refs):
        x_ref, refs = refs[0], refs[1:]
        cin, refs = refs[:nci], refs[nci:]
        out_ref, refs = refs[0], refs[1:]
        cout, refs = refs[:nco], refs[nco:]
        (send_sems, recv_sems, local_sem), csems = refs[:3], refs[3:]
        x, y, c = _position()
        me, sibling = (x, y, c), (x, y, 1 - c)
        chips = [(1 - x, y), (x, 1 - y), (1 - x, 1 - y)]

        def rows(px, py, pc):
            return out_ref.at[pl.ds((4 * px + 2 * py + pc) * m_per, m_per), :]

        def copy(k, block, to, src=None):
            return pltpu.make_async_remote_copy(
                src_ref=rows(*block) if src is None else src, dst_ref=rows(*block),
                send_sem=send_sems.at[k], recv_sem=recv_sems.at[k], device_id=to, device_id_type=MESH)

        mine = pltpu.make_async_copy(x_ref, rows(*me), local_sem)
        mine.start()
        first = [copy(0, me, sibling, src=x_ref)]
        first += [copy(1 + j, me, (*chip, c), src=x_ref) for j, chip in enumerate(chips)]
        for cp in first:
            cp.start()
        if comm is not None:
            comm.start(cin, cout, csems)
        passed = [copy(4 + j, (*chip, c), sibling) for j, chip in enumerate(chips)]
        for j, chip in enumerate(chips):
            copy(1 + j, (*chip, c), me).wait_recv()
            passed[j].start()
        copy(0, sibling, me).wait_recv()
        for j, chip in enumerate(chips):
            copy(4 + j, (*chip, 1 - c), me).wait_recv()
        for cp in first + passed:
            cp.wait_send()
        mine.wait()
        if comm is not None:
            comm.middle(cin, cout, csems)
            comm.finish(cin, cout, csems)

    vmem = pl.BlockSpec(memory_space=pltpu.VMEM)
    sems = [pltpu.SemaphoreType.DMA((7,)), pltpu.SemaphoreType.DMA((7,)), pltpu.SemaphoreType.DMA]
    out = jax.ShapeDtypeStruct((N_DEV * m_per, n), x_shard.dtype)
    if comm is None:
        return pl.pallas_call(body, name=name, out_shape=out, in_specs=[vmem], out_specs=vmem,
                              scratch_shapes=sems)(x_shard)
    res = pl.pallas_call(
        body, name=name, out_shape=[out] + list(comm.out_shapes), in_specs=[vmem] + [ANY_SPEC] * nci,
        out_specs=[vmem] + [ANY_SPEC] * nco, scratch_shapes=sems + list(comm.sems),
        input_output_aliases={1 + i: 1 + o for i, o in comm.aliases.items()})(x_shard, *comm.inputs)
    return res[0], list(res[1:])


def _peer_chips(x, y):
    return [(1 - x, y), (x, 1 - y), (1 - x, 1 - y)]


class _GatherJob:
    def __init__(self, pieces):
        self.pieces = pieces
        n_p = len(pieces)
        self.inputs = [p[0] for p in pieces]
        self.out_shapes = [jax.ShapeDtypeStruct(p[0].shape, p[0].dtype) for p in pieces]
        for buf, col_kind, r0, nr in pieces:
            half_rows = buf.shape[0] // (2 if col_kind else 2 * N_CHIPS)
            assert r0 % 16 == 0 and nr % 16 == 0 and r0 + nr <= half_rows, (buf.shape, r0, nr)
        self.aliases = {p: p for p in range(n_p)}
        self.sems = [pltpu.SemaphoreType.DMA((3 * n_p,))] * 4

    def _region(self, cout, p, chip_idx, half):
        buf, col_kind, r0, nr = self.pieces[p]
        if col_kind:
            n = buf.shape[1] // N_CHIPS
            return cout[p].at[pl.ds(half * (buf.shape[0] // 2) + r0, nr), pl.ds(chip_idx * n, n)]
        n = buf.shape[0] // N_CHIPS
        return cout[p].at[pl.ds(chip_idx * n + half * (n // 2) + r0, nr), :]

    def _copies(self, cout, sems):
        send_sems, recv_sems, fsend_sems, frecv_sems = sems
        x, y, c = _position()
        k = 2 * x + y
        sibling = (x, y, 1 - c)
        sends, arrivals, fwds, fwd_arrivals = [], [], [], []

        def remote(region, ssem, rsem, to):
            return pltpu.make_async_remote_copy(src_ref=region, dst_ref=region, send_sem=ssem, recv_sem=rsem,
                                                device_id=to, device_id_type=MESH)

        for p in range(len(self.pieces)):
            for j, chip in enumerate(_peer_chips(x, y)):
                idx = 3 * p + j
                theirs = 2 * chip[0] + chip[1]
                sends.append(remote(self._region(cout, p, k, c), send_sems.at[idx], recv_sems.at[idx], (*chip, c)))
                arrivals.append(remote(self._region(cout, p, theirs, c), send_sems.at[idx], recv_sems.at[idx],
                                       (*chip, c)))
                fwds.append(remote(self._region(cout, p, theirs, c), fsend_sems.at[idx], frecv_sems.at[idx], sibling))
                fwd_arrivals.append(remote(self._region(cout, p, theirs, 1 - c), fsend_sems.at[idx],
                                           frecv_sems.at[idx], sibling))
        return sends, arrivals, fwds, fwd_arrivals

    def start(self, cin, cout, sems):
        for cp in self._copies(cout, sems)[0]:
            cp.start()

    def middle(self, cin, cout, sems):
        _, arrivals, fwds, _ = self._copies(cout, sems)
        for arrived, fw in zip(arrivals, fwds):
            arrived.wait_recv()
            fw.start()

    def finish(self, cin, cout, sems):
        sends, _, fwds, fwd_arrivals = self._copies(cout, sems)
        for arrived in fwd_arrivals:
            arrived.wait_recv()
        for cp in sends + fwds:
            cp.wait_send()


class _PairedJob:
    aliases = {}

    def start(self, cin, cout, sems):
        for cp in self._copies(cin, cout, sems):
            cp.start()

    def middle(self, cin, cout, sems):
        pass

    def finish(self, cin, cout, sems):
        copies = self._copies(cin, cout, sems)
        for cp in copies:
            cp.wait_recv()
        for cp in copies:
            cp.wait_send()


class _SwapJob(_PairedJob):
    def __init__(self, grads, kinds):
        self.inputs, self.kinds = list(grads), list(kinds)
        self.out_shapes, self.n_copies = [], []
        for g, kd in zip(grads, kinds):
            if kd:
                self.out_shapes.append(jax.ShapeDtypeStruct((1, g.shape[0] // 2, g.shape[1]), g.dtype))
                self.n_copies.append(1)
            else:
                n = g.shape[0] // N_CHIPS
                self.out_shapes.append(jax.ShapeDtypeStruct((N_CHIPS, n // 2, g.shape[1]), g.dtype))
                self.n_copies.append(N_CHIPS)
        total = sum(self.n_copies)
        self.sems = [pltpu.SemaphoreType.DMA((total,)), pltpu.SemaphoreType.DMA((total,))]

    def _copies(self, cin, cout, sems):
        send_sems, recv_sems = sems
        x, y, c = _position()
        copies = []
        for p, src_ref in enumerate(cin):
            for kk in range(self.n_copies[p]):
                if self.kinds[p]:
                    hr = src_ref.shape[0] // 2
                    src = src_ref.at[pl.ds((1 - c) * hr, hr), :]
                else:
                    n = src_ref.shape[0] // N_CHIPS
                    src = src_ref.at[pl.ds(kk * n + (1 - c) * (n // 2), n // 2), :]
                idx = len(copies)
                copies.append(pltpu.make_async_remote_copy(
                    src_ref=src, dst_ref=cout[p].at[kk], send_sem=send_sems.at[idx], recv_sem=recv_sems.at[idx],
                    device_id=(x, y, 1 - c), device_id_type=MESH))
        return copies


class _ExchangeJob(_PairedJob):
    def __init__(self, s16, kinds, sizes):
        self.inputs, self.kinds, self.sizes = list(s16), list(kinds), list(sizes)
        self.out_shapes = [jax.ShapeDtypeStruct((3, s.shape[1], n if kd else s.shape[2]), s.dtype)
                           for s, kd, n in zip(s16, kinds, sizes)]
        self.sems = [pltpu.SemaphoreType.DMA((3 * len(s16),)), pltpu.SemaphoreType.DMA((3 * len(s16),))]

    def _copies(self, cin, cout, sems):
        send_sems, recv_sems = sems
        x, y, c = _position()
        copies = []
        for p, src_ref in enumerate(cin):
            for j, chip in enumerate(_peer_chips(x, y)):
                kk = 2 * chip[0] + chip[1]
                n = self.sizes[p]
                src = src_ref.at[0, :, pl.ds(kk * n, n)] if self.kinds[p] else src_ref.at[kk]
                copies.append(pltpu.make_async_remote_copy(
                    src_ref=src, dst_ref=cout[p].at[j], send_sem=send_sems.at[3 * p + j],
                    recv_sem=recv_sems.at[3 * p + j], device_id=(*chip, c), device_id_type=MESH))
        return copies


class _ShareJob:
    def __init__(self, halves):
        self.inputs = list(halves)
        self.out_shapes = [jax.ShapeDtypeStruct(h.shape, h.dtype) for h in halves]
        self.aliases = {p: p for p in range(len(halves))}
        self.sems = [pltpu.SemaphoreType.DMA((len(halves),)), pltpu.SemaphoreType.DMA((len(halves),))]

    def _copies(self, cout, sems, half):
        send_sems, recv_sems = sems
        x, y, c = _position()
        h = c if half == "mine" else 1 - c
        return [pltpu.make_async_remote_copy(
            src_ref=o.at[h], dst_ref=o.at[h], send_sem=send_sems.at[p], recv_sem=recv_sems.at[p],
            device_id=(x, y, 1 - c), device_id_type=MESH) for p, o in enumerate(cout)]

    def start(self, cin, cout, sems):
        for cp in self._copies(cout, sems, "mine"):
            cp.start()

    def middle(self, cin, cout, sems):
        pass

    def finish(self, cin, cout, sems):
        for cp in self._copies(cout, sems, "theirs"):
            cp.wait_recv()
        for cp in self._copies(cout, sems, "mine"):
            cp.wait_send()


class _MultiJob:
    def __init__(self, jobs):
        self.jobs = jobs
        self.inputs = [a for j in jobs for a in j.inputs]
        self.out_shapes = [s for j in jobs for s in j.out_shapes]
        self.sems = [s for j in jobs for s in j.sems]
        self.aliases = {}
        i0 = o0 = 0
        for j in jobs:
            for i, o in j.aliases.items():
                self.aliases[i0 + i] = o0 + o
            i0 += len(j.inputs)
            o0 += len(j.out_shapes)

    def _parts(self, cin, cout, sems):
        i0 = o0 = s0 = 0
        for j in self.jobs:
            ni, no, ns = len(j.inputs), len(j.out_shapes), len(j.sems)
            yield j, cin[i0:i0 + ni], cout[o0:o0 + no], sems[s0:s0 + ns]
            i0, o0, s0 = i0 + ni, o0 + no, s0 + ns

    def start(self, cin, cout, sems):
        for j, a, b, s in self._parts(cin, cout, sems):
            j.start(a, b, s)

    def middle(self, cin, cout, sems):
        for j, a, b, s in self._parts(cin, cout, sems):
            j.middle(a, b, s)

    def finish(self, cin, cout, sems):
        for j, a, b, s in self._parts(cin, cout, sems):
            j.finish(a, b, s)


def _rope_tables(positions):
    half = ROT_DIM // 2
    inv_freq = jnp.power(jnp.float32(ROPE_THETA), -jnp.arange(0, ROT_DIM, 2, dtype=F32) / ROT_DIM)
    inv_head = jnp.concatenate([inv_freq, inv_freq, jnp.zeros((HEAD_DIM - ROT_DIM,), F32)])
    inv_lane = jnp.concatenate([inv_head] * (LANE // HEAD_DIM))
    ang = positions.astype(F32).reshape(-1)[:, None] * inv_lane[None, :]
    sin = jnp.sin(ang)
    dim = jnp.arange(LANE) % HEAD_DIM
    return jnp.cos(ang), jnp.where(dim < half, -sin, 0.0), jnp.where(dim >= half, sin, 0.0)


def kernel(x, c, positions, w_ada, b_ada, ffn1_w_gate_up, ffn1_w_down, ln1_g, ln1_b, w_in, conv_w, attn_sinks, w_out, ln2_g, ln2_b, ffn2_w_gate_up, ffn2_w_down, ln3_g, ln3_b, loss_target, m_w_ada, m_b_ada, m_ffn1_w_gate_up, m_ffn1_w_down, m_ln1_g, m_ln1_b, m_w_in, m_conv_w, m_attn_sinks, m_w_out, m_ln2_g, m_ln2_b, m_ffn2_w_gate_up, m_ffn2_w_down, m_ln3_g, m_ln3_b, v_w_ada, v_b_ada, v_ffn1_w_gate_up, v_ffn1_w_down, v_ln1_g, v_ln1_b, v_w_in, v_conv_w, v_attn_sinks, v_w_out, v_ln2_g, v_ln2_b, v_ffn2_w_gate_up, v_ffn2_w_down, v_ln3_g, v_ln3_b):
    d = D_MODEL
    nb, seq, _ = x.shape
    t = nb * seq
    f = ffn1_w_down.shape[1] * N_CHIPS
    ax, ay, ac = _position()
    chip = 2 * ax + ay
    dev = 2 * chip + ac
    pos = jnp.stack([ax, ay, ac]).astype(jnp.int32)

    x2 = x.reshape(t, d)
    tgt2 = loss_target.reshape(t, d)
    ln1 = jnp.concatenate([ln1_g, ln1_b], axis=0)
    ln2 = jnp.concatenate([ln2_g, ln2_b], axis=0)
    ln3 = jnp.concatenate([ln3_g, ln3_b], axis=0)
    sinks = attn_sinks.reshape(N_Q_HEADS)
    cos_t, sa_t, sb_t = _rope_tables(positions)

    gu_cuts = [0, 176, 352, d // 2]
    gu_part = lambda buf, s: (buf, True, gu_cuts[s], gu_cuts[s + 1] - gu_cuts[s])
    chip_arr = jnp.reshape(chip, (1,)).astype(jnp.int32)
    b_gu1 = _cast_into(ffn1_w_gate_up[0], chip_arr, True, name="cast_gu1")

    n_ada = w_ada.shape[2]
    c_all, (b_gu1,) = _allgather8(c.reshape(nb * d // LANE, LANE), name="gather_c", comm=_GatherJob([gu_part(b_gu1, 0)]))
    c_all = c_all.reshape(N_DEV * nb, d)
    b_shard = lax.dynamic_slice(b_ada, (0, chip * n_ada), (1, n_ada))
    mod_part, (b_gu1,) = _ada_fwd(c_all, w_ada[0], b_shard, name="ada_fwd", comm=_GatherJob([gu_part(b_gu1, 1)]))
    conv_rows = jnp.pad(conv_w[0], ((0, 5), (0, n_ada - conv_w.shape[2])))
    part = jnp.concatenate([mod_part, conv_rows], axis=0)
    parts, (wgu1,) = _allgather8(part, name="gather_mod", comm=_GatherJob([gu_part(b_gu1, 2)]))
    parts = parts.reshape(N_DEV, N_DEV * nb + 8, n_ada)
    mod_all = jnp.concatenate([parts[2 * k, :N_DEV * nb, :] for k in range(N_CHIPS)], axis=1)
    mod = lax.dynamic_slice(mod_all, (dev * nb, 0), (nb, N_MOD * d)).reshape(nb, N_MOD, d)
    cw_full = jnp.concatenate([parts[2 * k, N_DEV * nb:, :conv_w.shape[2]] for k in range(N_CHIPS)], axis=1)

    b_d1 = _cast_into(ffn1_w_down[0], chip_arr, False, name="cast_d1")
    b_in = _cast_into(w_in[0].T, chip_arr, False, name="cast_in")
    b_out = _cast_into(w_out[0], chip_arr, False, name="cast_out")
    b_gu2 = _cast_into(ffn2_w_gate_up[0], chip_arr, True, name="cast_gu2")
    b_d2 = _cast_into(ffn2_w_down[0], chip_arr, False, name="cast_d2")
    n_gu, n_d, n_in, n_out = (ffn1_w_gate_up.shape[2], ffn1_w_down.shape[1], w_in.shape[2], w_out.shape[1])

    def whole(buf, col_kind):
        return (buf, col_kind, 0, buf.shape[0] // (2 if col_kind else 2 * N_CHIPS))

    (h1, a1, gu1), (wd1, wout) = _ffn_up(x2, ln1, mod, wgu1, seq=seq, sc_idx=1, sh_idx=0, use_ln=False,
                                         name="ffn1_up", comm=_GatherJob([whole(b_d1, False), whole(b_out, False)]))
    (f1, xhat1, rstd1), (win_t,) = _ffn_down_ln(a1, wd1, x2, ln1, mod, seq=seq, gate_idx=2, use_ln=False,
                                                name="ffn1_down", comm=_GatherJob([whole(b_in, False)]))
    (h2, q, k, v, ubc), (b_gu2,) = _in_proj(
        xhat1, ln1, mod, win_t, cos_t, sa_t, sb_t, seq=seq, sc_idx=4, sh_idx=3, name="in_proj",
        comm=_GatherJob([gu_part(b_gu2, 0)]))
    attn, (b_gu2,) = _attention(q, k, v, sinks, seq=seq, name="attention", comm=_GatherJob([gu_part(b_gu2, 1)]))
    (mixin, mix, xhat2, rstd2), (wgu2,) = _out_proj(
        attn, ubc, cw_full, wout, xhat1, ln1, mod, seq=seq, gate_idx=5, name="out_proj",
        comm=_GatherJob([gu_part(b_gu2, 2)]))
    (h3, a3, gu3), (wd2,) = _ffn_up(xhat2, ln2, mod, wgu2, seq=seq, sc_idx=7, sh_idx=6, use_ln=True, name="ffn2_up",
                                    comm=_GatherJob([whole(b_d2, False)]))
    dr3, df3, loss_cols, dln3g, dln3b, dgate3 = _ffn_down_loss(
        a3, wd2, xhat2, ln2, mod, ln3, tgt2, seq=seq, gate_idx=8, name="ffn2_down_loss")

    def pair_sum(g, r3, col_kind, name_, comm=None):
        if col_kind:
            g3 = g.reshape(2, g.shape[0] // 2, g.shape[1])
            blk_of = lambda p_, pos_: pos_[2]
        else:
            g3 = g.reshape(2 * N_CHIPS, g.shape[0] // (2 * N_CHIPS), g.shape[1])
            blk_of = lambda p_, pos_: 2 * p_ + pos_[2]
        return _sum_pair(pos, g3, r3, blk_of, name=name_, comm=comm)

    dgu3 = _ffn_bwd_act(df3, wd2, gu3, seq=seq, name="ffn2_bwd_act")
    g_wd2 = _matmul_tn(a3, df3, tmm=f // 2, tnn=d, name="grad_wd2")
    (s32_gu2, s16_gu2), (sib_d2,) = _grad_chip_sum(pos, h3, dgu3, name="grad_wgu2", comm=_SwapJob([g_wd2], [False]))
    s32_d2, s16_d2 = pair_sum(g_wd2, sib_d2, False, "sum_pair_d2")
    (dr2, dmix, dsc3, dsh3, dgate2, dln2g, dln2b), (recv_d2,) = _bwd_in(
        dgu3, wgu2, dr3, xhat2, rstd2, ln2, mod, mix, seq=seq, w_is_nt=True, sc_idx=7, gate_idx=5,
        branch_scale=1.0, final=False, name="ffn2_bwd_in", comm=_ExchangeJob([s16_d2], [False], [n_d]))
    g_wout = _matmul_tn(mixin, dmix, tmm=d, tnn=d, name="grad_wout")
    dmixin = _matmul_nt_bf16(dmix, wout, seq=seq, name="out_proj_bwd")
    (dq, dkp, dkc, dvp, dvc, dsink), (recv_gu2, sib_out) = _attention_bwd(
        q, k, v, dmixin, sinks, seq=seq, name="attention_bwd",
        comm=_MultiJob([_ExchangeJob([s16_gu2], [True], [n_gu]), _SwapJob([g_wout], [False])]))
    s32_out, s16_out = pair_sum(g_wout, sib_out, False, "sum_pair_out")
    (dproj, dcw), (recv_out,) = _mix_bwd_assemble(
        dq, dkp, dkc, dvp, dvc, cos_t, sa_t, sb_t, dmixin, ubc, cw_full, seq=seq, name="mix_bwd",
        comm=_ExchangeJob([s16_out], [False], [n_out]))
    g_win_t = _matmul_tn(dproj, h2, tmm=IN_WIDTH // 2, tnn=d, name="grad_win")
    (dr1, df1, dsc2, dsh2, dgate1, dln1g, dln1b), (sib_in,) = _bwd_in(
        dproj, win_t, dr2, xhat1, rstd1, ln1, mod, f1, seq=seq, w_is_nt=False, sc_idx=4, gate_idx=2,
        branch_scale=0.5, final=False, name="in_proj_bwd", comm=_SwapJob([g_win_t], [False]))
    s32_in, s16_in = pair_sum(g_win_t, sib_in, False, "sum_pair_in")
    g_wd1, (recv_in,) = _matmul_tn(a1, df1, tmm=f // 2, tnn=d, name="grad_wd1",
                                   comm=_ExchangeJob([s16_in], [False], [n_in]))
    dgu1, (sib_d1,) = _ffn_bwd_act(df1, wd1, gu1, seq=seq, name="ffn1_bwd_act", comm=_SwapJob([g_wd1], [False]))
    s32_d1, s16_d1 = pair_sum(g_wd1, sib_d1, False, "sum_pair_d1")
    (s32_gu1, s16_gu1), (recv_d1,) = _grad_chip_sum(pos, h1, dgu1, name="grad_wgu1",
                                                    comm=_ExchangeJob([s16_d1], [False], [n_d]))

    def final_half(s32_, recv_, col_kind, n_shard, name_):
        return _sum_final(pos, s32_, recv_, col_kind=col_kind, n_shard=n_shard, name=name_)

    early = [final_half(s32_gu2, recv_gu2, True, n_gu, "sum_final_gu2"),
             final_half(s32_d2, recv_d2, False, n_d, "sum_final_d2"),
             final_half(s32_out, recv_out, False, n_out, "sum_final_out"),
             final_half(s32_in, recv_in, False, n_in, "sum_final_in"),
             final_half(s32_d1, recv_d1, False, n_d, "sum_final_d1")]
    (grad_x, dsc1, dsh1), (recv_gu1, full_gu2, full_d2, full_out, full_in, full_d1) = _bwd_in(
        dgu1, wgu1, dr1, x2, None, None, mod, None, seq=seq, w_is_nt=True, sc_idx=1, gate_idx=None,
        branch_scale=None, final=True, name="ffn1_bwd_in",
        comm=_MultiJob([_ExchangeJob([s16_gu1], [True], [n_gu]), _ShareJob(early)]))
    late = [final_half(s32_gu1, recv_gu1, True, n_gu, "sum_final_gu1")]

    dmod = jnp.concatenate([dsh1, dsc1, dgate1, dsh2, dsc2, dgate2, dsh3, dsc3, dgate3], axis=1)
    loss_row = jnp.sum(loss_cols, axis=1, keepdims=True) * (0.5 / d)
    lane_row = lambda a: jnp.pad(a, ((0, 0), (0, d - a.shape[1])))
    block = jnp.concatenate(
        [dmod.reshape(nb * N_MOD, d), dln1g, dln1b, dln2g, dln2b, dln3g, dln3b,
         lane_row(dcw[0:3, :]), lane_row(dsink[:, 0:1].reshape(1, N_Q_HEADS)), lane_row(loss_row)], axis=0)
    block = jnp.pad(block, ((0, SMALL_ROWS - block.shape[0]), (0, 0)))
    gathered, (full_gu1,) = _allgather8(block, name="gather_small", comm=_ShareJob(late))
    gathered = gathered.reshape(N_DEV, SMALL_ROWS, d)
    dmod_all = gathered[:, :nb * N_MOD, :].reshape(N_DEV * nb, N_MOD * d)
    dmod_shard = lax.dynamic_slice(dmod_all, (0, chip * n_ada), (N_DEV * nb, n_ada))
    small, g_w_ada, g_b_ada = _small_finish(gathered, dmod_all, dmod_shard, c_all.T, name="small_finish")
    r0 = nb * N_MOD
    loss = small[r0 + 10, 0]
    g_ln = [small[r0 + i:r0 + i + 1, :] for i in range(6)]
    g_cw_full = small[r0 + 6:r0 + 9, :CONV_WIDTH]
    g_conv = lax.dynamic_slice(g_cw_full, (0, chip * conv_w.shape[2]), (3, conv_w.shape[2]))
    g_sinks = small[r0 + 9:r0 + 10, :N_Q_HEADS]

    def flat2(a):
        return a.reshape(-1, a.shape[-1])

    def unhalve(a):
        return a.reshape(2 * a.shape[1], a.shape[2])

    results = {}

    def adamw(name_, w_, g_, m_, v_):
        g2 = flat2(g_)
        dl, nm, nv = _adamw(flat2(w_), g2, flat2(m_), flat2(v_), name="adamw_" + name_)
        results[name_] = tuple(a.reshape(w_.shape) for a in (g2, dl, nm, nv))

    adamw("w_ada", w_ada, g_w_ada, m_w_ada, v_w_ada)
    adamw("ffn2_w_gate_up", ffn2_w_gate_up, unhalve(full_gu2), m_ffn2_w_gate_up, v_ffn2_w_gate_up)
    adamw("ffn2_w_down", ffn2_w_down, unhalve(full_d2), m_ffn2_w_down, v_ffn2_w_down)
    adamw("w_out", w_out, unhalve(full_out), m_w_out, v_w_out)
    adamw("w_in", w_in, unhalve(full_in).T, m_w_in, v_w_in)
    adamw("ffn1_w_gate_up", ffn1_w_gate_up, unhalve(full_gu1), m_ffn1_w_gate_up, v_ffn1_w_gate_up)
    adamw("ffn1_w_down", ffn1_w_down, unhalve(full_d1), m_ffn1_w_down, v_ffn1_w_down)
    adamw("b_ada", b_ada, g_b_ada, m_b_ada, v_b_ada)
    adamw("ln1_g", ln1_g, g_ln[0], m_ln1_g, v_ln1_g)
    adamw("ln1_b", ln1_b, g_ln[1], m_ln1_b, v_ln1_b)
    adamw("ln2_g", ln2_g, g_ln[2], m_ln2_g, v_ln2_g)
    adamw("ln2_b", ln2_b, g_ln[3], m_ln2_b, v_ln2_b)
    adamw("ln3_g", ln3_g, g_ln[4], m_ln3_g, v_ln3_g)
    adamw("ln3_b", ln3_b, g_ln[5], m_ln3_b, v_ln3_b)
    adamw("conv_w", conv_w, g_conv, m_conv_w, v_conv_w)
    adamw("attn_sinks", attn_sinks, g_sinks, m_attn_sinks, v_attn_sinks)
    order = ["w_ada", "b_ada", "ffn1_w_gate_up", "ffn1_w_down", "ln1_g", "ln1_b", "w_in", "conv_w", "attn_sinks",
             "w_out", "ln2_g", "ln2_b", "ffn2_w_gate_up", "ffn2_w_down", "ln3_g", "ln3_b"]
    return (loss, grad_x.reshape(x.shape), *[results[n_][0] for n_ in order], *[results[n_][1] for n_ in order],
            *[results[n_][2] for n_ in order], *[results[n_][3] for n_ in order])
```

```python
import jax
import jax.numpy as jnp
from jax import lax
from jax.experimental import pallas as pl
from jax.experimental.pallas import tpu as pltpu

F32 = jnp.float32
BF16 = jnp.bfloat16
MESH = pl.DeviceIdType.MESH

D_MODEL = 1024
HEAD_DIM = 64
ATTN_WIDTH = 512
CONV_WIDTH = 512
N_Q_HEADS = 8
N_KV_HEADS = 2
GQA_GROUP = 4
KV_WIDTH = 128
WINDOW = 128
BLOCK = 128
ROT_DIM = 16
ROPE_THETA = 500000.0
N_MOD = 9
LN_EPS = 1e-5
DN_ALPHA = 2.0 ** 0.25
IN_WIDTH = 2304
N_CHIPS = 4
N_DEV = 8
SMALL_ROWS = 32

ADAM_LR = 0.001
ADAM_B1 = 0.9
ADAM_B2 = 0.999
ADAM_EPS = 1e-08
ADAM_WD = 0.01
ADAM_STEP = 10

LANE = 128
HALO = 16
COL_CHUNK = 256
VMEM_LIMIT = 56 * 1024 * 1024


def _params(sem=None, vmem=True):
    return pltpu.CompilerParams(dimension_semantics=sem, vmem_limit_bytes=VMEM_LIMIT if vmem else None)


def _sigmoid(g):
    return 0.5 * jnp.tanh(0.5 * g) + 0.5


def _row_sum(v):
    return jnp.sum(v, axis=0, keepdims=True)


ROW_CHUNK = 16
EPILOGUE_UNROLL = 8


def _fold8(v):
    return v[0:8, :] + v[8:16, :]


def _row_chunk_loop(n_rows, step, init):
    per_iter = ROW_CHUNK * EPILOGUE_UNROLL
    assert n_rows % per_iter == 0, n_rows

    def body(it, carry):
        for s in range(EPILOGUE_UNROLL):
            start = pl.multiple_of(it * per_iter + s * ROW_CHUNK, ROW_CHUNK)
            carry = step(pl.ds(start, ROW_CHUNK), carry)
        return carry

    return lax.fori_loop(0, n_rows // per_iter, body, init)


def _ln_stats(r):
    mu = jnp.mean(r, axis=-1, keepdims=True)
    rc = r - mu
    var = jnp.mean(rc * rc, axis=-1, keepdims=True)
    rstd = lax.rsqrt(var + LN_EPS)
    return rc * rstd, rstd


def _ln_bwd(dxo, xhat, rstd, g):
    dxhat = dxo * g
    m1 = jnp.mean(dxhat, axis=-1, keepdims=True)
    m2 = jnp.mean(dxhat * xhat, axis=-1, keepdims=True)
    return rstd * (dxhat - m1 - xhat * m2)


def _dot_nt(a, b):
    return lax.dot_general(a, b, (((1,), (1,)), ((), ())), preferred_element_type=F32)


def _dot_tn(a, b):
    return lax.dot_general(a, b, (((0,), (0,)), ((), ())), preferred_element_type=F32)


def _full(shape):
    nd = len(shape)
    return pl.BlockSpec(shape, lambda *_: (0,) * nd)


def _resident(shape):
    nd = len(shape)
    return pl.BlockSpec(shape, lambda *_: (0,) * nd, pipeline_mode=pl.Buffered(1))


ANY_SPEC = pl.BlockSpec(memory_space=pl.ANY)


def _pcall(body, *, name, grid, in_specs, out_specs, out_shape, args, scratch_shapes=(), comm=None, prefetch=None):
    single = not isinstance(out_shape, (list, tuple))
    out_specs = [out_specs] if single else list(out_specs)
    out_shape = [out_shape] if single else list(out_shape)
    in_specs = list(in_specs)
    scratch_shapes = list(scratch_shapes)
    sem = ("arbitrary",) * len(grid)
    n_pre = 0 if prefetch is None else 1
    pre_args = () if prefetch is None else (prefetch,)

    def call(fn, ins_, outs_, shapes_, scratch_, aliases_, operands):
        if prefetch is None:
            return pl.pallas_call(fn, name=name, grid=grid, in_specs=ins_, out_specs=outs_, out_shape=shapes_,
                                  scratch_shapes=scratch_, input_output_aliases=aliases_,
                                  compiler_params=_params(sem))(*operands)
        spec = pltpu.PrefetchScalarGridSpec(num_scalar_prefetch=1, grid=grid, in_specs=ins_, out_specs=outs_,
                                            scratch_shapes=scratch_)
        return pl.pallas_call(fn, name=name, grid_spec=spec, out_shape=shapes_,
                              input_output_aliases={n_pre + i: o for i, o in aliases_.items()},
                              compiler_params=_params(sem))(*pre_args, *operands)

    if comm is None:
        res = call(body, in_specs, out_specs, out_shape, scratch_shapes, {}, args)
        return res[0] if single else res
    n_in, n_out, n_scr = len(in_specs), len(out_specs), len(scratch_shapes)
    nci, nco = len(comm.inputs), len(comm.out_shapes)
    n_steps = 1
    for g in grid:
        n_steps *= g
    staged = n_steps >= 4
    middle_step = n_steps - 1 - max(1, n_steps // 8)

    def wrapped(*refs):
        pre, refs = refs[:n_pre], refs[n_pre:]
        ins, refs = refs[:n_in], refs[n_in:]
        cin, refs = refs[:nci], refs[nci:]
        outs, refs = refs[:n_out], refs[n_out:]
        cout, refs = refs[:nco], refs[nco:]
        scr, csems = refs[:n_scr], refs[n_scr:]
        step = pl.program_id(0)
        for ax in range(1, len(grid)):
            step = step * grid[ax] + pl.program_id(ax)

        @pl.when(step == 0)
        def _():
            comm.start(cin, cout, csems)

        body(*pre, *ins, *outs, *scr)

        if staged:
            @pl.when(step == middle_step)
            def _():
                comm.middle(cin, cout, csems)

        @pl.when(step == n_steps - 1)
        def _():
            if not staged:
                comm.middle(cin, cout, csems)
            comm.finish(cin, cout, csems)

    res = call(wrapped, in_specs + [ANY_SPEC] * nci, out_specs + [ANY_SPEC] * nco,
               out_shape + list(comm.out_shapes), scratch_shapes + list(comm.sems),
               {n_in + i: n_out + o for i, o in comm.aliases.items()}, (*args, *comm.inputs))
    main = res[:n_out]
    return (main[0] if single else main), list(res[n_out:])


def _comm_call(job, *, name):
    nci, nco = len(job.inputs), len(job.out_shapes)

    def body(*refs):
        cin, refs = refs[:nci], refs[nci:]
        cout, csems = refs[:nco], refs[nco:]
        job.start(cin, cout, csems)
        job.middle(cin, cout, csems)
        job.finish(cin, cout, csems)

    return pl.pallas_call(
        body, name=name, out_shape=list(job.out_shapes), in_specs=[ANY_SPEC] * nci, out_specs=[ANY_SPEC] * nco,
        scratch_shapes=list(job.sems), input_output_aliases=dict(job.aliases))(*job.inputs)


def _ffn_up(xin, lnp, mod, w, *, seq, sc_idx, sh_idx, use_ln, name, comm=None):
    t, d = xin.shape
    f = w.shape[1] // 2
    tm = min(512, seq)
    tpb = seq // tm
    ch = min(COL_CHUNK, f)

    def body(x_ref, ln_ref, mod_ref, w_ref, h_ref, a_ref, gu_ref):
        x = x_ref[...]
        if use_ln:
            x = x * ln_ref[0:1, :] + ln_ref[1:2, :]
        h = x * (1.0 + mod_ref[0, sc_idx:sc_idx + 1, :]) + mod_ref[0, sh_idx:sh_idx + 1, :]
        hb = h.astype(BF16)
        h_ref[...] = hb
        for j in range(f // ch):
            g = jnp.dot(hb, w_ref[:, j * ch:(j + 1) * ch], preferred_element_type=F32)
            u = jnp.dot(hb, w_ref[:, f + j * ch:f + (j + 1) * ch], preferred_element_type=F32)
            s = _sigmoid(g)
            silu = g * s
            a_ref[:, j * ch:(j + 1) * ch] = (silu * u).astype(BF16)
            gu_ref[:, j * ch:(j + 1) * ch] = (u * (s + silu * (1.0 - s))).astype(BF16)
            gu_ref[:, f + j * ch:f + (j + 1) * ch] = silu.astype(BF16)

    return _pcall(
        body, name=name, grid=(t // tm,),
        in_specs=[pl.BlockSpec((tm, d), lambda i: (i, 0)), _full((2, d)),
                  pl.BlockSpec((1, N_MOD, d), lambda i: (i // tpb, 0, 0)), _resident((d, 2 * f))],
        out_specs=[pl.BlockSpec((tm, d), lambda i: (i, 0)), pl.BlockSpec((tm, f), lambda i: (i, 0)),
                   pl.BlockSpec((tm, 2 * f), lambda i: (i, 0))],
        out_shape=[jax.ShapeDtypeStruct((t, d), BF16), jax.ShapeDtypeStruct((t, f), BF16),
                   jax.ShapeDtypeStruct((t, 2 * f), BF16)],
        args=(xin, lnp, mod, w), comm=comm)


def _ffn_down_ln(a, wd, xin, lnp_in, mod, *, seq, gate_idx, use_ln, name, comm=None):
    t, f = a.shape
    d = wd.shape[1]
    tm = min(512, seq)
    tpb = seq // tm

    def body(a_ref, wd_ref, x_ref, ln_ref, mod_ref, f_ref, xhat_ref, rstd_ref, acc):
        av = a_ref[...]
        for j in range(d // COL_CHUNK):
            acc[:, j * COL_CHUNK:(j + 1) * COL_CHUNK] = jnp.dot(
                av, wd_ref[:, j * COL_CHUNK:(j + 1) * COL_CHUNK], preferred_element_type=F32)
        scale = 0.5 * (1.0 + mod_ref[0, gate_idx:gate_idx + 1, :])

        fo = acc[...]
        x = x_ref[...]
        if use_ln:
            x = x * ln_ref[0:1, :] + ln_ref[1:2, :]
        xhat, rstd = _ln_stats(DN_ALPHA * x + scale * fo)
        f_ref[...] = fo.astype(BF16)
        xhat_ref[...] = xhat
        rstd_ref[...] = rstd

    return _pcall(
        body, name=name, grid=(t // tm,),
        in_specs=[pl.BlockSpec((tm, f), lambda i: (i, 0)), _resident((f, d)),
                  pl.BlockSpec((tm, d), lambda i: (i, 0)), _full((2, d)),
                  pl.BlockSpec((1, N_MOD, d), lambda i: (i // tpb, 0, 0))],
        out_specs=[pl.BlockSpec((tm, d), lambda i: (i, 0)), pl.BlockSpec((tm, d), lambda i: (i, 0)),
                   pl.BlockSpec((tm, 1), lambda i: (i, 0))],
        out_shape=[jax.ShapeDtypeStruct((t, d), BF16), jax.ShapeDtypeStruct((t, d), F32),
                   jax.ShapeDtypeStruct((t, 1), F32)],
        scratch_shapes=[pltpu.VMEM((tm, d), F32)],
        args=(a, wd, xin, lnp_in, mod), comm=comm)


def _ffn_down_loss(a, wd, xhat_in, lnp_in, mod, lnp_out, tgt, *, seq, gate_idx, name):
    t, f = a.shape
    d = wd.shape[1]
    nb = t // seq
    tm = min(512, seq)
    tpb = seq // tm

    def body(a_ref, wd_ref, x_ref, lnin_ref, mod_ref, lnout_ref, tgt_ref,
             dr_ref, df_ref, loss_ref, dg_ref, db_ref, dgate_ref, acc):
        i = pl.program_id(0)
        av = a_ref[...]
        for j in range(d // COL_CHUNK):
            acc[:, j * COL_CHUNK:(j + 1) * COL_CHUNK] = jnp.dot(
                av, wd_ref[:, j * COL_CHUNK:(j + 1) * COL_CHUNK], preferred_element_type=F32)
        scale = 0.5 * (1.0 + mod_ref[0, gate_idx:gate_idx + 1, :])
        g_in, b_in = lnin_ref[0:1, :], lnin_ref[1:2, :]
        g_out, b_out = lnout_ref[0:1, :], lnout_ref[1:2, :]

        def chunk(rows, carry):
            s_loss, s_dg, s_db, s_gate = carry
            fo = acc[rows, :]
            xhat, rstd = _ln_stats(DN_ALPHA * (x_ref[rows, :] * g_in + b_in) + scale * fo)
            e = xhat * g_out + b_out - tgt_ref[rows, :]
            dy = e * (1.0 / d)
            dr = _ln_bwd(dy, xhat, rstd, g_out)
            dr_ref[rows, :] = dr
            df_ref[rows, :] = (scale * dr).astype(BF16)
            return (s_loss + _fold8(e * e), s_dg + _fold8(dy * xhat), s_db + _fold8(dy),
                    s_gate + _fold8(0.5 * fo * dr))

        zero = jnp.zeros((8, d), F32)
        s_loss, s_dg, s_db, s_gate = _row_chunk_loop(tm, chunk, (zero, zero, zero, zero))

        @pl.when(i == 0)
        def _():
            loss_ref[...] = jnp.zeros_like(loss_ref)
            dg_ref[...] = jnp.zeros_like(dg_ref)
            db_ref[...] = jnp.zeros_like(db_ref)

        @pl.when(i % tpb == 0)
        def _():
            dgate_ref[...] = jnp.zeros_like(dgate_ref)

        loss_ref[...] += _row_sum(s_loss)
        dg_ref[...] += _row_sum(s_dg)
        db_ref[...] += _row_sum(s_db)
        dgate_ref[0] += _row_sum(s_gate)

    return pl.pallas_call(
        body, name=name, grid=(t // tm,), scratch_shapes=[pltpu.VMEM((tm, d), F32)],
        in_specs=[pl.BlockSpec((tm, f), lambda i: (i, 0)), _resident((f, d)),
                  pl.BlockSpec((tm, d), lambda i: (i, 0)), _full((2, d)),
                  pl.BlockSpec((1, N_MOD, d), lambda i: (i // tpb, 0, 0)), _full((2, d)),
                  pl.BlockSpec((tm, d), lambda i: (i, 0))],
        out_specs=[pl.BlockSpec((tm, d), lambda i: (i, 0)), pl.BlockSpec((tm, d), lambda i: (i, 0)),
                   _full((1, d)), _full((1, d)), _full((1, d)),
                   pl.BlockSpec((1, 1, d), lambda i: (i // tpb, 0, 0))],
        out_shape=[jax.ShapeDtypeStruct((t, d), F32), jax.ShapeDtypeStruct((t, d), BF16),
                   jax.ShapeDtypeStruct((1, d), F32), jax.ShapeDtypeStruct((1, d), F32),
                   jax.ShapeDtypeStruct((1, d), F32), jax.ShapeDtypeStruct((nb, 1, d), F32)],
        compiler_params=_params(("arbitrary",)),
    )(a, wd, xhat_in, lnp_in, mod, lnp_out, tgt)


def _rope(v, cos, sa, sb):
    return v * cos + pltpu.roll(v, LANE - ROT_DIM // 2, 1) * sa + pltpu.roll(v, ROT_DIM // 2, 1) * sb


def _rope_t(dy, cos, sa, sb):
    return dy * cos + pltpu.roll(dy * sa, ROT_DIM // 2, 1) + pltpu.roll(dy * sb, LANE - ROT_DIM // 2, 1)


def _in_proj(xhat, lnp, mod, w_t, cos, sa, sb, *, seq, sc_idx, sh_idx, name, comm=None):
    t, d = xhat.shape
    tm = min(512, seq)
    tpb = seq // tm
    n_conv = 3 * CONV_WIDTH

    def body(x_ref, ln_ref, mod_ref, w_ref, cos_ref, sa_ref, sb_ref, h_ref, q_ref, k_ref, v_ref, ubc_ref):
        x = x_ref[...] * ln_ref[0:1, :] + ln_ref[1:2, :]
        h = x * (1.0 + mod_ref[0, sc_idx:sc_idx + 1, :]) + mod_ref[0, sh_idx:sh_idx + 1, :]
        hb = h.astype(BF16)
        h_ref[...] = hb
        cos_t, sa_t, sb_t = cos_ref[...], sa_ref[...], sb_ref[...]
        for j in range(ATTN_WIDTH // COL_CHUNK):
            p = _dot_nt(hb, w_ref[j * COL_CHUNK:(j + 1) * COL_CHUNK, :])
            for s in range(COL_CHUNK // LANE):
                q_ref[:, j * COL_CHUNK + s * LANE:j * COL_CHUNK + (s + 1) * LANE] = _rope(
                    p[:, s * LANE:(s + 1) * LANE], cos_t, sa_t, sb_t).astype(BF16)
        p = _dot_nt(hb, w_ref[ATTN_WIDTH:ATTN_WIDTH + 2 * KV_WIDTH, :])
        k_ref[...] = _rope(p[:, 0:KV_WIDTH], cos_t, sa_t, sb_t).astype(BF16)
        v_ref[...] = p[:, KV_WIDTH:].astype(BF16)
        base = ATTN_WIDTH + 2 * KV_WIDTH
        for j in range(n_conv // COL_CHUNK):
            ubc_ref[:, j * COL_CHUNK:(j + 1) * COL_CHUNK] = _dot_nt(
                hb, w_ref[base + j * COL_CHUNK:base + (j + 1) * COL_CHUNK, :]).astype(BF16)

    row = lambda w: pl.BlockSpec((tm, w), lambda i: (i, 0))
    return _pcall(
        body, name=name, grid=(t // tm,),
        in_specs=[row(d), _full((2, d)), pl.BlockSpec((1, N_MOD, d), lambda i: (i // tpb, 0, 0)),
                  _resident((IN_WIDTH, d)), row(LANE), row(LANE), row(LANE)],
        out_specs=[row(d), row(ATTN_WIDTH), row(KV_WIDTH), row(KV_WIDTH), row(n_conv)],
        out_shape=[jax.ShapeDtypeStruct((t, d), BF16), jax.ShapeDtypeStruct((t, ATTN_WIDTH), BF16),
                   jax.ShapeDtypeStruct((t, KV_WIDTH), BF16), jax.ShapeDtypeStruct((t, KV_WIDTH), BF16),
                   jax.ShapeDtypeStruct((t, n_conv), BF16)],
        args=(xhat, lnp, mod, w_t, cos, sa, sb), comm=comm)


ATTN_TILE_BLOCKS = 2


def _attn_sub_block(s, tile, nblk, kp_ref, kc_ref, vp_ref, vc_ref):
    rows = slice(s * BLOCK, (s + 1) * BLOCK)
    if s == 0:
        first = ((tile * ATTN_TILE_BLOCKS) % nblk) == 0
        return rows, (kp_ref, slice(0, BLOCK)), (kc_ref, rows), (vp_ref, slice(0, BLOCK)), (vc_ref, rows), first
    before = slice((s - 1) * BLOCK, s * BLOCK)
    return rows, (kc_ref, before), (kc_ref, rows), (vc_ref, before), (vc_ref, rows), False


def _attn_group(q_ref, rows, k_prev, k_cur, v_prev, v_cur, sink_ref, g, first):
    lo, hi = g * HEAD_DIM, (g + 1) * HEAD_DIM
    kk = jnp.concatenate([k_prev[0][k_prev[1], lo:hi], k_cur[0][k_cur[1], lo:hi]], axis=0)
    vv = jnp.concatenate([v_prev[0][v_prev[1], lo:hi], v_cur[0][v_cur[1], lo:hi]], axis=0)
    qs = jnp.concatenate([q_ref[rows, (GQA_GROUP * g + j) * HEAD_DIM:(GQA_GROUP * g + j + 1) * HEAD_DIM]
                          for j in range(GQA_GROUP)], axis=0)
    rows = GQA_GROUP * BLOCK
    row = lax.broadcasted_iota(jnp.int32, (rows, 2 * BLOCK), 0)
    ki = lax.broadcasted_iota(jnp.int32, (rows, 2 * BLOCK), 1)
    diff = (row & (BLOCK - 1)) + BLOCK - ki
    valid = (diff >= 0) & (diff < WINDOW) & ((ki >= BLOCK) | jnp.logical_not(first))
    s = _dot_nt(qs, kk) * (HEAD_DIM ** -0.5)
    s = jnp.where(valid, s, -1e30)
    rcol = lax.broadcasted_iota(jnp.int32, (rows, 1), 0)
    sink = jnp.zeros((rows, 1), F32)
    for j in range(GQA_GROUP):
        sink = jnp.where(rcol // BLOCK == j, sink_ref[GQA_GROUP * g + j], sink)
    m = jnp.maximum(jnp.max(s, axis=1, keepdims=True), sink)
    p = jnp.exp(s - m)
    ps = jnp.exp(sink - m)
    inv = 1.0 / (jnp.sum(p, axis=1, keepdims=True) + ps)
    return qs, kk, vv, p * inv, ps * inv


def _attention(q, k, v, sinks, *, seq, name, comm=None):
    t = q.shape[0]
    nblk = seq // BLOCK
    tile = ATTN_TILE_BLOCKS * BLOCK

    def body(q_ref, kp_ref, kc_ref, vp_ref, vc_ref, sink_ref, o_ref):
        for s in range(ATTN_TILE_BLOCKS):
            rows, k_prev, k_cur, v_prev, v_cur, first = _attn_sub_block(
                s, pl.program_id(0), nblk, kp_ref, kc_ref, vp_ref, vc_ref)
            outs = []
            for g in range(N_KV_HEADS):
                _, _, vv, pn, _ = _attn_group(q_ref, rows, k_prev, k_cur, v_prev, v_cur, sink_ref, g, first)
                o = jnp.dot(pn.astype(BF16), vv, preferred_element_type=F32)
                outs += [o[j * BLOCK:(j + 1) * BLOCK, :] for j in range(GQA_GROUP)]
            o_ref[rows, :] = jnp.concatenate(outs, axis=1).astype(BF16)

    cur = lambda w: pl.BlockSpec((tile, w), lambda n: (n, 0))
    prev = lambda w: pl.BlockSpec((BLOCK, w), lambda n: (jnp.maximum(n * ATTN_TILE_BLOCKS - 1, 0), 0))
    return _pcall(
        body, name=name, grid=(t // tile,),
        in_specs=[cur(ATTN_WIDTH), prev(KV_WIDTH), cur(KV_WIDTH), prev(KV_WIDTH), cur(KV_WIDTH),
                  pl.BlockSpec(memory_space=pltpu.SMEM)],
        out_specs=cur(ATTN_WIDTH),
        out_shape=jax.ShapeDtypeStruct((t, ATTN_WIDTH), BF16),
        args=(q, k, k, v, v, sinks), comm=comm)


def _out_proj(attn, ubc, cw, wout, xhat_in, lnp_in, mod, *, seq, gate_idx, name, comm=None):
    t, d = xhat_in.shape
    tm = min(512, seq)
    tpb = seq // tm
    cwid = CONV_WIDTH

    def body(attn_ref, ubc_ref, halo_ref, cw_ref, w_ref, x_ref, ln_ref, mod_ref,
             mixin_ref, mix_ref, xhat_ref, rstd_ref, zbuf, acc):
        first = (pl.program_id(0) % tpb) == 0
        u, bg, cg = (ubc_ref[:, s * cwid:(s + 1) * cwid].astype(F32) for s in range(3))
        z = cg * u
        hz = halo_ref[:, 2 * cwid:3 * cwid].astype(F32) * halo_ref[:, 0:cwid].astype(F32)
        zbuf[0:HALO, :] = jnp.where(first, 0.0, hz)
        zbuf[HALO:HALO + tm, :] = z
        y = (cw_ref[0:1, :] * zbuf[HALO - 2:HALO - 2 + tm, :] + cw_ref[1:2, :] * zbuf[HALO - 1:HALO - 1 + tm, :]
             + cw_ref[2:3, :] * z)
        mixin_ref[:, 0:ATTN_WIDTH] = attn_ref[...]
        mixin_ref[:, ATTN_WIDTH:] = (bg * y).astype(BF16)
        mv = mixin_ref[...]
        for j in range(d // COL_CHUNK):
            acc[:, j * COL_CHUNK:(j + 1) * COL_CHUNK] = jnp.dot(
                mv, w_ref[:, j * COL_CHUNK:(j + 1) * COL_CHUNK], preferred_element_type=F32)
        scale = 1.0 + mod_ref[0, gate_idx:gate_idx + 1, :]

        mix = acc[...]
        xhat, rstd = _ln_stats(DN_ALPHA * (x_ref[...] * ln_ref[0:1, :] + ln_ref[1:2, :]) + scale * mix)
        mix_ref[...] = mix.astype(BF16)
        xhat_ref[...] = xhat
        rstd_ref[...] = rstd

    row = lambda w: pl.BlockSpec((tm, w), lambda i: (i, 0))
    return _pcall(
        body, name=name, grid=(t // tm,),
        in_specs=[row(ATTN_WIDTH), row(3 * cwid),
                  pl.BlockSpec((HALO, 3 * cwid), lambda i: (jnp.maximum(i * (tm // HALO) - 1, 0), 0)),
                  _full((8, cwid)), _resident((d, d)), row(d), _full((2, d)),
                  pl.BlockSpec((1, N_MOD, d), lambda i: (i // tpb, 0, 0))],
        out_specs=[row(d), row(d), row(d), row(1)],
        out_shape=[jax.ShapeDtypeStruct((t, d), BF16), jax.ShapeDtypeStruct((t, d), BF16),
                   jax.ShapeDtypeStruct((t, d), F32), jax.ShapeDtypeStruct((t, 1), F32)],
        scratch_shapes=[pltpu.VMEM((tm + HALO, cwid), F32), pltpu.VMEM((tm, d), F32)],
        args=(attn, ubc, ubc, cw, wout, xhat_in, lnp_in, mod), comm=comm)


def _ffn_bwd_act(df, wd, gu, *, seq, name, comm=None):
    t, d = df.shape
    f = wd.shape[0]
    tm = min(512, seq)
    ch = min(COL_CHUNK, f)

    def body(df_ref, wd_ref, gu_ref, dgu_ref):
        dfv = df_ref[...]
        for j in range(f // ch):
            da = _dot_nt(dfv, wd_ref[j * ch:(j + 1) * ch, :])
            dgu_ref[:, j * ch:(j + 1) * ch] = (da * gu_ref[:, j * ch:(j + 1) * ch].astype(F32)).astype(BF16)
            dgu_ref[:, f + j * ch:f + (j + 1) * ch] = (
                da * gu_ref[:, f + j * ch:f + (j + 1) * ch].astype(F32)).astype(BF16)

    return _pcall(
        body, name=name, grid=(t // tm,),
        in_specs=[pl.BlockSpec((tm, d), lambda i: (i, 0)), _resident((f, d)),
                  pl.BlockSpec((tm, 2 * f), lambda i: (i, 0))],
        out_specs=pl.BlockSpec((tm, 2 * f), lambda i: (i, 0)),
        out_shape=jax.ShapeDtypeStruct((t, 2 * f), BF16),
        args=(df, wd, gu), comm=comm)


def _bwd_in(a, w, dr, xin, rstd_prev, lnp_prev, mod, branch_prev, *, seq, w_is_nt, sc_idx, gate_idx,
            branch_scale, final, name, comm=None):
    t, kdim = a.shape
    d = dr.shape[1]
    nb = t // seq
    tm = min(512, seq)
    tpb = seq // tm

    def body(*refs):
        if final:
            a_ref, w_ref, dr_ref, x_ref, mod_ref, dx_ref, dsc_ref, dsh_ref, acc = refs
        else:
            (a_ref, w_ref, dr_ref, x_ref, rstd_ref, ln_ref, mod_ref, br_ref,
             drp_ref, dbr_ref, dsc_ref, dsh_ref, dgate_ref, dg_ref, db_ref, acc) = refs
        i = pl.program_id(0)
        av = a_ref[...]
        for j in range(d // COL_CHUNK):
            cols = slice(j * COL_CHUNK, (j + 1) * COL_CHUNK)
            acc[:, cols] = (_dot_nt(av, w_ref[cols, :]) if w_is_nt
                            else jnp.dot(av, w_ref[:, cols], preferred_element_type=F32))
        sc1 = 1.0 + mod_ref[0, sc_idx:sc_idx + 1, :]
        if not final:
            g_prev, b_prev = ln_ref[0:1, :], ln_ref[1:2, :]
            bscale = branch_scale * (1.0 + mod_ref[0, gate_idx:gate_idx + 1, :])

        def chunk(rows, carry):
            dh = acc[rows, :]
            dx = DN_ALPHA * dr_ref[rows, :] + dh * sc1
            if final:
                dx_ref[rows, :] = dx
                return carry[0] + _fold8(dh * x_ref[rows, :]), carry[1] + _fold8(dh)
            xhat = x_ref[rows, :]
            drp = _ln_bwd(dx, xhat, rstd_ref[rows, :], g_prev)
            drp_ref[rows, :] = drp
            dbr_ref[rows, :] = (bscale * drp).astype(BF16)
            return (carry[0] + _fold8(dh * (xhat * g_prev + b_prev)), carry[1] + _fold8(dh),
                    carry[2] + _fold8(branch_scale * br_ref[rows, :].astype(F32) * drp),
                    carry[3] + _fold8(dx * xhat), carry[4] + _fold8(dx))

        zero = jnp.zeros((8, d), F32)
        sums = _row_chunk_loop(tm, chunk, (zero,) * (2 if final else 5))

        @pl.when((i % tpb) == 0)
        def _():
            dsc_ref[...] = jnp.zeros_like(dsc_ref)
            dsh_ref[...] = jnp.zeros_like(dsh_ref)
            if not final:
                dgate_ref[...] = jnp.zeros_like(dgate_ref)

        dsc_ref[0] += _row_sum(sums[0])
        dsh_ref[0] += _row_sum(sums[1])
        if not final:
            @pl.when(i == 0)
            def _():
                dg_ref[...] = jnp.zeros_like(dg_ref)
                db_ref[...] = jnp.zeros_like(db_ref)

            dgate_ref[0] += _row_sum(sums[2])
            dg_ref[...] += _row_sum(sums[3])
            db_ref[...] += _row_sum(sums[4])

    row = lambda w_: pl.BlockSpec((tm, w_), lambda i: (i, 0))
    vec = pl.BlockSpec((1, 1, d), lambda i: (i // tpb, 0, 0))
    mod_spec = pl.BlockSpec((1, N_MOD, d), lambda i: (i // tpb, 0, 0))
    vshape = jax.ShapeDtypeStruct((nb, 1, d), F32)
    if final:
        in_specs = [row(kdim), _resident(w.shape), row(d), row(d), mod_spec]
        args = (a, w, dr, xin, mod)
        out_specs = [row(d), vec, vec]
        out_shape = [jax.ShapeDtypeStruct((t, d), F32), vshape, vshape]
    else:
        in_specs = [row(kdim), _resident(w.shape), row(d), row(d), row(1), _full((2, d)), mod_spec, row(d)]
        args = (a, w, dr, xin, rstd_prev, lnp_prev, mod, branch_prev)
        out_specs = [row(d), row(d), vec, vec, vec, _full((1, d)), _full((1, d))]
        out_shape = [jax.ShapeDtypeStruct((t, d), F32), jax.ShapeDtypeStruct((t, d), BF16), vshape, vshape, vshape,
                     jax.ShapeDtypeStruct((1, d), F32), jax.ShapeDtypeStruct((1, d), F32)]
    return _pcall(
        body, name=name, grid=(t // tm,), in_specs=in_specs, out_specs=out_specs, out_shape=out_shape,
        scratch_shapes=[pltpu.VMEM((tm, d), F32)], args=args, comm=comm)


def _matmul_tn(a, b, *, tmm, tnn, name, comm=None):
    t, m = a.shape
    n = b.shape[1]
    tk = min(2048, t)

    def body(a_ref, b_ref, o_ref):
        @pl.when(pl.program_id(2) == 0)
        def _():
            o_ref[...] = jnp.zeros_like(o_ref)
        o_ref[...] += _dot_tn(a_ref[...], b_ref[...])

    return _pcall(
        body, name=name, grid=(m // tmm, n // tnn, t // tk),
        in_specs=[pl.BlockSpec((tk, tmm), lambda i, j, k: (k, i)), pl.BlockSpec((tk, tnn), lambda i, j, k: (k, j))],
        out_specs=pl.BlockSpec((tmm, tnn), lambda i, j, k: (i, j)),
        out_shape=jax.ShapeDtypeStruct((m, n), F32),
        args=(a, b), comm=comm)


def _grad_chip_sum(pos, a, b, *, name, comm=None):
    t, m = a.shape
    n = b.shape[1]
    hm, tnn = m // 2, n // N_CHIPS
    tk = min(2048, t)
    nk = t // tk
    n_j = n // tnn

    def body(pos_ref, a_ref, b_ref, s32_ref, s16_ref, land_ref, acc, theirs, send_sems, recv_sems, copy_sem):
        p, j, k = pl.program_id(0), pl.program_id(1), pl.program_id(2)
        x, y, c = _position()

        def push(jj):
            return pltpu.make_async_remote_copy(
                src_ref=acc.at[jj], dst_ref=land_ref.at[jj], send_sem=send_sems.at[jj], recv_sem=recv_sems.at[jj],
                device_id=(x, y, 1 - c), device_id_type=MESH)

        fetch = pltpu.make_async_copy(land_ref.at[j], theirs, copy_sem)

        @pl.when(jnp.logical_and(p == 1, k == 0))
        def _():
            push(j).wait_send()
            push(j).wait_recv()
            fetch.start()

        part = _dot_tn(a_ref[...], b_ref[...])

        @pl.when(k == 0)
        def _():
            acc[j] = part

        @pl.when(k > 0)
        def _():
            acc[j] += part

        @pl.when(jnp.logical_and(p == 0, k == nk - 1))
        def _():
            push(j).start()

        @pl.when(jnp.logical_and(p == 1, k == nk - 1))
        def _():
            fetch.wait()
            s = acc[j] + theirs[...]
            s32_ref[0] = s
            s16_ref[0] = s.astype(BF16)

    half = lambda p, pos_ref: 1 - pos_ref[2] - p + 2 * p * pos_ref[2]
    out_tile = pl.BlockSpec((1, hm, tnn), lambda p, j, k, pos_ref: (0, 0, j * p))
    shape = lambda dt: jax.ShapeDtypeStruct((1, hm, n), dt)
    out = _pcall(
        body, name=name, grid=(2, n_j, nk),
        in_specs=[pl.BlockSpec((tk, hm), lambda p, j, k, pos_ref: (k, half(p, pos_ref))),
                  pl.BlockSpec((tk, tnn), lambda p, j, k, pos_ref: (k, j))],
        out_specs=[out_tile, out_tile, ANY_SPEC],
        out_shape=[shape(F32), shape(BF16), jax.ShapeDtypeStruct((n_j, hm, tnn), F32)],
        scratch_shapes=[pltpu.VMEM((n_j, hm, tnn), F32), pltpu.VMEM((hm, tnn), F32),
                        pltpu.SemaphoreType.DMA((n_j,)), pltpu.SemaphoreType.DMA((n_j,)), pltpu.SemaphoreType.DMA],
        args=(a, b), prefetch=pos, comm=comm)
    if comm is None:
        return out[0], out[1]
    (s32, s16, _), extra = out
    return (s32, s16), extra


def _matmul_nt_bf16(a, w, *, seq, name):
    t, kdim = a.shape
    n = w.shape[0]
    tm = min(512, seq)

    def body(a_ref, w_ref, o_ref):
        av = a_ref[...]
        for j in range(n // COL_CHUNK):
            o_ref[:, j * COL_CHUNK:(j + 1) * COL_CHUNK] = _dot_nt(
                av, w_ref[j * COL_CHUNK:(j + 1) * COL_CHUNK, :]).astype(BF16)

    return pl.pallas_call(
        body, name=name, grid=(t // tm,),
        in_specs=[pl.BlockSpec((tm, kdim), lambda i: (i, 0)), _resident((n, kdim))],
        out_specs=pl.BlockSpec((tm, n), lambda i: (i, 0)),
        out_shape=jax.ShapeDtypeStruct((t, n), BF16),
        compiler_params=_params(("arbitrary",)),
    )(a, w)


def _attention_bwd(q, k, v, dmixin, sinks, *, seq, name, comm=None):
    t = q.shape[0]
    nblk = seq // BLOCK
    tile = ATTN_TILE_BLOCKS * BLOCK

    def body(q_ref, kp_ref, kc_ref, vp_ref, vc_ref, do_ref, sink_ref,
             dq_ref, dkp_ref, dkc_ref, dvp_ref, dvc_ref, dsink_ref):
        n = pl.program_id(0)

        @pl.when(n == 0)
        def _():
            dsink_ref[...] = jnp.zeros_like(dsink_ref)

        srow = lax.broadcasted_iota(jnp.int32, (8, LANE), 0)
        dsink = jnp.zeros((8, LANE), F32)
        for s in range(ATTN_TILE_BLOCKS):
            rows, k_prev, k_cur, v_prev, v_cur, first = _attn_sub_block(s, n, nblk, kp_ref, kc_ref, vp_ref, vc_ref)
            dqs, dks, dvs = [], [], []
            for g in range(N_KV_HEADS):
                qs, kk, vv, pn, psn = _attn_group(q_ref, rows, k_prev, k_cur, v_prev, v_cur, sink_ref, g, first)
                dos = jnp.concatenate(
                    [do_ref[rows, (GQA_GROUP * g + j) * HEAD_DIM:(GQA_GROUP * g + j + 1) * HEAD_DIM]
                     for j in range(GQA_GROUP)], axis=0)
                dp = _dot_nt(dos, vv)
                delta = jnp.sum(pn * dp, axis=1, keepdims=True)
                ds = pn * (dp - delta)
                dsk = psn * delta
                for j in range(GQA_GROUP):
                    tot = jnp.sum(dsk[j * BLOCK:(j + 1) * BLOCK, :], axis=0, keepdims=True)
                    dsink = dsink - jnp.where(srow == GQA_GROUP * g + j, tot, 0.0)
                dsb = (ds * (HEAD_DIM ** -0.5)).astype(BF16)
                dqg = jnp.dot(dsb, kk, preferred_element_type=F32)
                dqs += [dqg[j * BLOCK:(j + 1) * BLOCK, :] for j in range(GQA_GROUP)]
                dks.append(_dot_tn(dsb, qs))
                dvs.append(_dot_tn(pn.astype(BF16), dos))
            dq_ref[rows, :] = jnp.concatenate(dqs, axis=1)
            dkp_ref[rows, :] = jnp.concatenate([x[0:BLOCK, :] for x in dks], axis=1)
            dkc_ref[rows, :] = jnp.concatenate([x[BLOCK:, :] for x in dks], axis=1)
            dvp_ref[rows, :] = jnp.concatenate([x[0:BLOCK, :] for x in dvs], axis=1)
            dvc_ref[rows, :] = jnp.concatenate([x[BLOCK:, :] for x in dvs], axis=1)
        dsink_ref[...] += dsink

    cur = lambda w: pl.BlockSpec((tile, w), lambda n: (n, 0))
    prev = lambda w: pl.BlockSpec((BLOCK, w), lambda n: (jnp.maximum(n * ATTN_TILE_BLOCKS - 1, 0), 0))
    kv = jax.ShapeDtypeStruct((t, KV_WIDTH), F32)
    return _pcall(
        body, name=name, grid=(t // tile,),
        in_specs=[cur(ATTN_WIDTH), prev(KV_WIDTH), cur(KV_WIDTH), prev(KV_WIDTH), cur(KV_WIDTH), cur(ATTN_WIDTH),
                  pl.BlockSpec(memory_space=pltpu.SMEM)],
        out_specs=[cur(ATTN_WIDTH), cur(KV_WIDTH), cur(KV_WIDTH), cur(KV_WIDTH), cur(KV_WIDTH), _full((8, LANE))],
        out_shape=[jax.ShapeDtypeStruct((t, ATTN_WIDTH), F32), kv, kv, kv, kv, jax.ShapeDtypeStruct((8, LANE), F32)],
        args=(q, k, k, v, v, dmixin, sinks), comm=comm)


def _mix_bwd_assemble(dq, dkp, dkc, dvp, dvc, cos, sa, sb, dmixin, ubc, cw, *, seq, name, comm=None):
    t = dq.shape[0]
    cwid = CONV_WIDTH
    tm = min(2 * BLOCK, seq)
    tiles_per_seq = seq // tm
    ntile = t // tm
    nblk_all = t // BLOCK
    per_tile = tm // BLOCK

    def body(*refs):
        dq_ref, dkc_ref, dvc_ref = refs[0:3]
        dkp_refs, dvp_refs = refs[3:3 + per_tile], refs[3 + per_tile:3 + 2 * per_tile]
        (cos_ref, sa_ref, sb_ref, dco_ref, dcon_ref, ubc_ref, hprev_ref, hnext_ref, cw_ref,
         dproj_ref, dcw_ref, zbuf, dybuf) = refs[3 + 2 * per_tile:]
        i = pl.program_id(0)
        first = (i % tiles_per_seq) == 0
        last = (i % tiles_per_seq) == tiles_per_seq - 1
        glast = i == ntile - 1

        @pl.when(i == 0)
        def _():
            dcw_ref[...] = jnp.zeros_like(dcw_ref)

        def with_next_block(cur_ref, nxt_refs):
            nxt = [r[...] for r in nxt_refs]
            nxt[-1] = jnp.where(glast, 0.0, nxt[-1])
            return cur_ref[...] + jnp.concatenate(nxt, axis=0)

        cos_t, sa_t, sb_t = cos_ref[...], sa_ref[...], sb_ref[...]
        for j in range(ATTN_WIDTH // LANE):
            dproj_ref[:, j * LANE:(j + 1) * LANE] = _rope_t(
                dq_ref[:, j * LANE:(j + 1) * LANE], cos_t, sa_t, sb_t).astype(BF16)
        dk = with_next_block(dkc_ref, dkp_refs)
        dproj_ref[:, ATTN_WIDTH:ATTN_WIDTH + KV_WIDTH] = _rope_t(dk, cos_t, sa_t, sb_t).astype(BF16)
        dv = with_next_block(dvc_ref, dvp_refs)
        dproj_ref[:, ATTN_WIDTH + KV_WIDTH:ATTN_WIDTH + 2 * KV_WIDTH] = dv.astype(BF16)

        u, bg, cg = (ubc_ref[:, s * cwid:(s + 1) * cwid].astype(F32) for s in range(3))
        z = cg * u
        hz = hprev_ref[:, 2 * cwid:3 * cwid].astype(F32) * hprev_ref[:, 0:cwid].astype(F32)
        zbuf[0:HALO, :] = jnp.where(first, 0.0, hz)
        zbuf[HALO:HALO + tm, :] = z
        z2, z1 = zbuf[HALO - 2:HALO - 2 + tm, :], zbuf[HALO - 1:HALO - 1 + tm, :]
        w0, w1, w2 = cw_ref[0:1, :], cw_ref[1:2, :], cw_ref[2:3, :]
        y = w0 * z2 + w1 * z1 + w2 * z
        dco = dco_ref[...].astype(F32)
        dyc = dco * bg
        dyn = dcon_ref[...].astype(F32) * hnext_ref[:, cwid:2 * cwid].astype(F32)
        dybuf[0:tm, :] = dyc
        dybuf[tm:tm + HALO, :] = jnp.where(last, 0.0, dyn)
        dz = w2 * dyc + w1 * dybuf[1:1 + tm, :] + w0 * dybuf[2:2 + tm, :]
        srow = lax.broadcasted_iota(jnp.int32, (8, cwid), 0)
        dcw_ref[...] += (jnp.where(srow == 0, _row_sum(dyc * z2), 0.0) + jnp.where(srow == 1, _row_sum(dyc * z1), 0.0)
                         + jnp.where(srow == 2, _row_sum(dyc * z), 0.0))
        base = ATTN_WIDTH + 2 * KV_WIDTH
        dproj_ref[:, base:base + cwid] = (dz * cg).astype(BF16)
        dproj_ref[:, base + cwid:base + 2 * cwid] = (dco * y).astype(BF16)
        dproj_ref[:, base + 2 * cwid:base + 3 * cwid] = (dz * u).astype(BF16)

    cur = lambda w: pl.BlockSpec((tm, w), lambda i: (i, 0))
    nxt = [pl.BlockSpec((BLOCK, KV_WIDTH), lambda i, s=s: (jnp.minimum(i * per_tile + s + 1, nblk_all - 1), 0))
           for s in range(per_tile)]
    prev_halo = pl.BlockSpec((HALO, 3 * cwid), lambda i: (jnp.maximum(i * (tm // HALO) - 1, 0), 0))
    next_halo = lambda w, col: pl.BlockSpec(
        (HALO, w), lambda i: (jnp.minimum((i + 1) * (tm // HALO), t // HALO - 1), col))
    return _pcall(
        body, name=name, grid=(ntile,),
        in_specs=[cur(ATTN_WIDTH), cur(KV_WIDTH), cur(KV_WIDTH), *nxt, *nxt,
                  cur(LANE), cur(LANE), cur(LANE),
                  pl.BlockSpec((tm, cwid), lambda i: (i, 1)), next_halo(cwid, 1),
                  cur(3 * cwid), prev_halo, next_halo(3 * cwid, 0), _full((8, cwid))],
        out_specs=[cur(IN_WIDTH), _full((8, cwid))],
        out_shape=[jax.ShapeDtypeStruct((t, IN_WIDTH), BF16), jax.ShapeDtypeStruct((8, cwid), F32)],
        scratch_shapes=[pltpu.VMEM((tm + HALO, cwid), F32), pltpu.VMEM((tm + HALO, cwid), F32)],
        args=(dq, dkc, dvc, *([dkp] * per_tile), *([dvp] * per_tile), cos, sa, sb, dmixin, dmixin,
              ubc, ubc, ubc, cw), comm=comm)


def _ada_fwd(c_all, w_ada, b_ada_shard, *, name, comm=None):
    nb, d = c_all.shape
    n = w_ada.shape[1]
    tn = n // 2

    def body(c_ref, w_ref, b_ref, o_ref):
        cv = c_ref[...]
        cond = cv * _sigmoid(cv)
        o_ref[...] = jnp.dot(cond, w_ref[...], preferred_element_type=F32,
                             precision=lax.Precision.HIGHEST) + b_ref[...]

    return _pcall(
        body, name=name, grid=(n // tn,),
        in_specs=[_full((nb, d)), pl.BlockSpec((d, tn), lambda j: (0, j)), pl.BlockSpec((1, tn), lambda j: (0, j))],
        out_specs=pl.BlockSpec((nb, tn), lambda j: (0, j)),
        out_shape=jax.ShapeDtypeStruct((nb, n), F32), args=(c_all, w_ada, b_ada_shard), comm=comm)


def _small_finish(gathered, dmod_all, dmod_shard, c_all_t, *, name):
    d = D_MODEL
    nb, n = dmod_shard.shape

    def body(g_ref, dm_ref, dms_ref, ct_ref, sum_ref, gw_ref, gb_ref):
        total = g_ref[0]
        for dev in range(1, N_DEV):
            total = total + g_ref[dev]
        sum_ref[...] = total
        gb_ref[...] = _row_sum(dm_ref[...])
        ctv = ct_ref[...]
        cond_t = ctv * _sigmoid(ctv)
        for jb in range(n // COL_CHUNK):
            gw_ref[:, jb * COL_CHUNK:(jb + 1) * COL_CHUNK] = jnp.dot(
                cond_t, dms_ref[:, jb * COL_CHUNK:(jb + 1) * COL_CHUNK], preferred_element_type=F32,
                precision=lax.Precision.HIGHEST)

    return pl.pallas_call(
        body, name=name, grid=(1,),
        in_specs=[_full((N_DEV, SMALL_ROWS, d)), _full((nb, N_MOD * d)), _full((nb, n)), _full((d, nb))],
        out_specs=[_full((SMALL_ROWS, d)), _full((d, n)), _full((1, N_MOD * d))],
        out_shape=[jax.ShapeDtypeStruct((SMALL_ROWS, d), F32), jax.ShapeDtypeStruct((d, n), F32),
                   jax.ShapeDtypeStruct((1, N_MOD * d), F32)],
        compiler_params=_params(("arbitrary",)),
    )(gathered, dmod_all, dmod_shard, c_all_t)


def _row_tile(r, c, budget=1 << 21):
    if r * c * 4 <= budget or r % 16:
        return r
    best = 16
    for tr in range(16, r + 1, 16):
        if r % tr == 0 and tr * c * 4 <= budget:
            best = tr
    return best


def _cast_into(w, chip, col_kind, *, name):
    r, c = w.shape
    tr = _row_tile(r, c)

    def body(chip_ref, w_ref, o_ref):
        o_ref[...] = w_ref[...].astype(BF16)

    if col_kind:
        out_spec = pl.BlockSpec((tr, c), lambda i, chip_ref: (i, chip_ref[0]))
        out_shape = jax.ShapeDtypeStruct((r, c * N_CHIPS), BF16)
    else:
        out_spec = pl.BlockSpec((tr, c), lambda i, chip_ref: (chip_ref[0] * (r // tr) + i, 0))
        out_shape = jax.ShapeDtypeStruct((r * N_CHIPS, c), BF16)
    return _pcall(body, name=name, grid=(r // tr,), in_specs=[pl.BlockSpec((tr, c), lambda i, chip_ref: (i, 0))],
                  out_specs=out_spec, out_shape=out_shape, args=(w,), prefetch=chip)


def _adamw(w, g, m, v, *, name, comm=None):
    r, c = w.shape
    tr = _row_tile(r, c)
    c1 = 1.0 - ADAM_B1 ** ADAM_STEP
    c2 = 1.0 - ADAM_B2 ** ADAM_STEP

    def body(w_ref, g_ref, m_ref, v_ref, d_ref, nm_ref, nv_ref):
        gv = g_ref[...]
        m2 = ADAM_B1 * m_ref[...] + (1.0 - ADAM_B1) * gv
        v2 = ADAM_B2 * v_ref[...] + (1.0 - ADAM_B2) * (gv * gv)
        d_ref[...] = -ADAM_LR * ((m2 / c1) / (jnp.sqrt(v2 / c2) + ADAM_EPS) + ADAM_WD * w_ref[...])
        nm_ref[...] = m2
        nv_ref[...] = v2

    spec = pl.BlockSpec((tr, c), lambda i: (i, 0))
    sh = jax.ShapeDtypeStruct((r, c), F32)
    return _pcall(body, name=name, grid=(r // tr,), in_specs=[spec] * 4, out_specs=[spec] * 3, out_shape=[sh] * 3,
                  args=(w, g, m, v), comm=comm)


def _sum_pair(pos, g3, r3, blk_of, *, name, comm=None):
    n, rows, cols = r3.shape
    tr = _row_tile(rows, cols)

    def body(pos_ref, g_ref, r_ref, s32_ref, s16_ref):
        s = g_ref[0] + r_ref[0]
        s32_ref[0] = s
        s16_ref[0] = s.astype(BF16)

    own = pl.BlockSpec((1, tr, cols), lambda p, i, pos: (blk_of(p, pos), i, 0))
    plain = pl.BlockSpec((1, tr, cols), lambda p, i, pos: (p, i, 0))
    return _pcall(
        body, name=name, grid=(n, rows // tr), in_specs=[own, plain], out_specs=[plain, plain],
        out_shape=[jax.ShapeDtypeStruct((n, rows, cols), F32), jax.ShapeDtypeStruct((n, rows, cols), BF16)],
        args=(g3, r3), prefetch=pos, comm=comm)


def _sum_final(pos, s32, recv, *, col_kind, n_shard, name, comm=None):
    if col_kind:
        rows, cols = s32.shape[1], n_shard
        own = lambda tr: pl.BlockSpec((1, tr, cols), lambda i, pos: (0, i, 2 * pos[0] + pos[1]))
    else:
        rows, cols = s32.shape[1], s32.shape[2]
        own = lambda tr: pl.BlockSpec((1, tr, cols), lambda i, pos: (2 * pos[0] + pos[1], i, 0))
    tr = _row_tile(rows, cols)

    def body(pos_ref, s_ref, r_ref, o_ref):
        o_ref[0] = ((s_ref[0] + r_ref[0].astype(F32)) + r_ref[1].astype(F32)) + r_ref[2].astype(F32)

    return _pcall(
        body, name=name, grid=(rows // tr,),
        in_specs=[own(tr), pl.BlockSpec((3, tr, cols), lambda i, pos: (0, i, 0))],
        out_specs=pl.BlockSpec((1, tr, cols), lambda i, pos: (pos[2], i, 0)),
        out_shape=jax.ShapeDtypeStruct((2, rows, cols), F32), args=(s32, recv), prefetch=pos, comm=comm)


def _position():
    return lax.axis_index("x"), lax.axis_index("y"), lax.axis_index("c")


def _allgather8(x_shard, *, name, comm=None):
    m_per, n = x_shard.shape
    nci, nco = (0, 0) if comm is None else (len(comm.inputs), len(comm.out_shapes))

    def body(*refs):
        x_ref, refs = refs[0], refs[1:]
        cin, refs = refs[:nci], refs[nci:]
        out_ref, refs = refs[0], refs[1:]
        cout, refs = refs[:nco], refs[nco:]
        (send_sems, recv_sems, local_sem), csems = refs[:3], refs[3:]
        x, y, c = _position()
        me, sibling = (x, y, c), (x, y, 1 - c)
        chips = [(1 - x, y), (x, 1 - y), (1 - x, 1 - y)]

        def rows(px, py, pc):
            return out_ref.at[pl.ds((4 * px + 2 * py + pc) * m_per, m_per), :]

        def copy(k, block, to, src=None):
            return pltpu.make_async_remote_copy(
                src_ref=rows(*block) if src is None else src, dst_ref=rows(*block),
                send_sem=send_sems.at[k], recv_sem=recv_sems.at[k], device_id=to, device_id_type=MESH)

        mine = pltpu.make_async_copy(x_ref, rows(*me), local_sem)
        mine.start()
        first = [copy(0, me, sibling, src=x_ref)]
        first += [copy(1 + j, me, (*chip, c), src=x_ref) for j, chip in enumerate(chips)]
        for cp in first:
            cp.start()
        if comm is not None:
            comm.start(cin, cout, csems)
        passed = [copy(4 + j, (*chip, c), sibling) for j, chip in enumerate(chips)]
        for j, chip in enumerate(chips):
            copy(1 + j, (*chip, c), me).wait_recv()
            passed[j].start()
        copy(0, sibling, me).wait_recv()
        for j, chip in enumerate(chips):
            copy(4 + j, (*chip, 1 - c), me).wait_recv()
        for cp in first + passed:
            cp.wait_send()
        mine.wait()
        if comm is not None:
            comm.middle(cin, cout, csems)
            comm.finish(cin, cout, csems)

    vmem = pl.BlockSpec(memory_space=pltpu.VMEM)
    sems = [pltpu.SemaphoreType.DMA((7,)), pltpu.SemaphoreType.DMA((7,)), pltpu.SemaphoreType.DMA]
    out = jax.ShapeDtypeStruct((N_DEV * m_per, n), x_shard.dtype)
    if comm is None:
        return pl.pallas_call(body, name=name, out_shape=out, in_specs=[vmem], out_specs=vmem,
                              scratch_shapes=sems)(x_shard)
    res = pl.pallas_call(
        body, name=name, out_shape=[out] + list(comm.out_shapes), in_specs=[vmem] + [ANY_SPEC] * nci,
        out_specs=[vmem] + [ANY_SPEC] * nco, scratch_shapes=sems + list(comm.sems),
        input_output_aliases={1 + i: 1 + o for i, o in comm.aliases.items()})(x_shard, *comm.inputs)
    return res[0], list(res[1:])


def _peer_chips(x, y):
    return [(1 - x, y), (x, 1 - y), (1 - x, 1 - y)]


class _GatherJob:
    def __init__(self, pieces):
        self.pieces = pieces
        n_p = len(pieces)
        self.inputs = [p[0] for p in pieces]
        self.out_shapes = [jax.ShapeDtypeStruct(p[0].shape, p[0].dtype) for p in pieces]
        for buf, col_kind, r0, nr in pieces:
            half_rows = buf.shape[0] // (2 if col_kind else 2 * N_CHIPS)
            assert r0 % 16 == 0 and nr % 16 == 0 and r0 + nr <= half_rows, (buf.shape, r0, nr)
        self.aliases = {p: p for p in range(n_p)}
        self.sems = [pltpu.SemaphoreType.DMA((3 * n_p,))] * 4

    def _region(self, cout, p, chip_idx, half):
        buf, col_kind, r0, nr = self.pieces[p]
        if col_kind:
            n = buf.shape[1] // N_CHIPS
            return cout[p].at[pl.ds(half * (buf.shape[0] // 2) + r0, nr), pl.ds(chip_idx * n, n)]
        n = buf.shape[0] // N_CHIPS
        return cout[p].at[pl.ds(chip_idx * n + half * (n // 2) + r0, nr), :]

    def _copies(self, cout, sems):
        send_sems, recv_sems, fsend_sems, frecv_sems = sems
        x, y, c = _position()
        k = 2 * x + y
        sibling = (x, y, 1 - c)
        sends, arrivals, fwds, fwd_arrivals = [], [], [], []

        def remote(region, ssem, rsem, to):
            return pltpu.make_async_remote_copy(src_ref=region, dst_ref=region, send_sem=ssem, recv_sem=rsem,
                                                device_id=to, device_id_type=MESH)

        for p in range(len(self.pieces)):
            for j, chip in enumerate(_peer_chips(x, y)):
                idx = 3 * p + j
                theirs = 2 * chip[0] + chip[1]
                sends.append(remote(self._region(cout, p, k, c), send_sems.at[idx], recv_sems.at[idx], (*chip, c)))
                arrivals.append(remote(self._region(cout, p, theirs, c), send_sems.at[idx], recv_sems.at[idx],
                                       (*chip, c)))
                fwds.append(remote(self._region(cout, p, theirs, c), fsend_sems.at[idx], frecv_sems.at[idx], sibling))
                fwd_arrivals.append(remote(self._region(cout, p, theirs, 1 - c), fsend_sems.at[idx],
                                           frecv_sems.at[idx], sibling))
        return sends, arrivals, fwds, fwd_arrivals

    def start(self, cin, cout, sems):
        for cp in self._copies(cout, sems)[0]:
            cp.start()

    def middle(self, cin, cout, sems):
        _, arrivals, fwds, _ = self._copies(cout, sems)
        for arrived, fw in zip(arrivals, fwds):
            arrived.wait_recv()
            fw.start()

    def finish(self, cin, cout, sems):
        sends, _, fwds, fwd_arrivals = self._copies(cout, sems)
        for arrived in fwd_arrivals:
            arrived.wait_recv()
        for cp in sends + fwds:
            cp.wait_send()


class _PairedJob:
    aliases = {}

    def start(self, cin, cout, sems):
        for cp in self._copies(cin, cout, sems):
            cp.start()

    def middle(self, cin, cout, sems):
        pass

    def finish(self, cin, cout, sems):
        copies = self._copies(cin, cout, sems)
        for cp in copies:
            cp.wait_recv()
        for cp in copies:
            cp.wait_send()


class _SwapJob(_PairedJob):
    def __init__(self, grads, kinds):
        self.inputs, self.kinds = list(grads), list(kinds)
        self.out_shapes, self.n_copies = [], []
        for g, kd in zip(grads, kinds):
            if kd:
                self.out_shapes.append(jax.ShapeDtypeStruct((1, g.shape[0] // 2, g.shape[1]), g.dtype))
                self.n_copies.append(1)
            else:
                n = g.shape[0] // N_CHIPS
                self.out_shapes.append(jax.ShapeDtypeStruct((N_CHIPS, n // 2, g.shape[1]), g.dtype))
                self.n_copies.append(N_CHIPS)
        total = sum(self.n_copies)
        self.sems = [pltpu.SemaphoreType.DMA((total,)), pltpu.SemaphoreType.DMA((total,))]

    def _copies(self, cin, cout, sems):
        send_sems, recv_sems = sems
        x, y, c = _position()
        copies = []
        for p, src_ref in enumerate(cin):
            for kk in range(self.n_copies[p]):
                if self.kinds[p]:
                    hr = src_ref.shape[0] // 2
                    src = src_ref.at[pl.ds((1 - c) * hr, hr), :]
                else:
                    n = src_ref.shape[0] // N_CHIPS
                    src = src_ref.at[pl.ds(kk * n + (1 - c) * (n // 2), n // 2), :]
                idx = len(copies)
                copies.append(pltpu.make_async_remote_copy(
                    src_ref=src, dst_ref=cout[p].at[kk], send_sem=send_sems.at[idx], recv_sem=recv_sems.at[idx],
                    device_id=(x, y, 1 - c), device_id_type=MESH))
        return copies


class _ExchangeJob(_PairedJob):
    def __init__(self, s16, kinds, sizes):
        self.inputs, self.kinds, self.sizes = list(s16), list(kinds), list(sizes)
        self.out_shapes = [jax.ShapeDtypeStruct((3, s.shape[1], n if kd else s.shape[2]), s.dtype)
                           for s, kd, n in zip(s16, kinds, sizes)]
        self.sems = [pltpu.SemaphoreType.DMA((3 * len(s16),)), pltpu.SemaphoreType.DMA((3 * len(s16),))]

    def _copies(self, cin, cout, sems):
        send_sems, recv_sems = sems
        x, y, c = _position()
        copies = []
        for p, src_ref in enumerate(cin):
            for j, chip in enumerate(_peer_chips(x, y)):
                kk = 2 * chip[0] + chip[1]
                n = self.sizes[p]
                src = src_ref.at[0, :, pl.ds(kk * n, n)] if self.kinds[p] else src_ref.at[kk]
                copies.append(pltpu.make_async_remote_copy(
                    src_ref=src, dst_ref=cout[p].at[j], send_sem=send_sems.at[3 * p + j],
                    recv_sem=recv_sems.at[3 * p + j], device_id=(*chip, c), device_id_type=MESH))
        return copies


class _ShareJob:
    def __init__(self, halves):
        self.inputs = list(halves)
        self.out_shapes = [jax.ShapeDtypeStruct(h.shape, h.dtype) for h in halves]
        self.aliases = {p: p for p in range(len(halves))}
        self.sems = [pltpu.SemaphoreType.DMA((len(halves),)), pltpu.SemaphoreType.DMA((len(halves),))]

    def _copies(self, cout, sems, half):
        send_sems, recv_sems = sems
        x, y, c = _position()
        h = c if half == "mine" else 1 - c
        return [pltpu.make_async_remote_copy(
            src_ref=o.at[h], dst_ref=o.at[h], send_sem=send_sems.at[p], recv_sem=recv_sems.at[p],
            device_id=(x, y, 1 - c), device_id_type=MESH) for p, o in enumerate(cout)]

    def start(self, cin, cout, sems):
        for cp in self._copies(cout, sems, "mine"):
            cp.start()

    def middle(self, cin, cout, sems):
        pass

    def finish(self, cin, cout, sems):
        for cp in self._copies(cout, sems, "theirs"):
            cp.wait_recv()
        for cp in self._copies(cout, sems, "mine"):
            cp.wait_send()


class _MultiJob:
    def __init__(self, jobs):
        self.jobs = jobs
        self.inputs = [a for j in jobs for a in j.inputs]
        self.out_shapes = [s for j in jobs for s in j.out_shapes]
        self.sems = [s for j in jobs for s in j.sems]
        self.aliases = {}
        i0 = o0 = 0
        for j in jobs:
            for i, o in j.aliases.items():
                self.aliases[i0 + i] = o0 + o
            i0 += len(j.inputs)
            o0 += len(j.out_shapes)

    def _parts(self, cin, cout, sems):
        i0 = o0 = s0 = 0
        for j in self.jobs:
            ni, no, ns = len(j.inputs), len(j.out_shapes), len(j.sems)
            yield j, cin[i0:i0 + ni], cout[o0:o0 + no], sems[s0:s0 + ns]
            i0, o0, s0 = i0 + ni, o0 + no, s0 + ns

    def start(self, cin, cout, sems):
        for j, a, b, s in self._parts(cin, cout, sems):
            j.start(a, b, s)

    def middle(self, cin, cout, sems):
        for j, a, b, s in self._parts(cin, cout, sems):
            j.middle(a, b, s)

    def finish(self, cin, cout, sems):
        for j, a, b, s in self._parts(cin, cout, sems):
            j.finish(a, b, s)


def _rope_tables(positions):
    half = ROT_DIM // 2
    inv_freq = jnp.power(jnp.float32(ROPE_THETA), -jnp.arange(0, ROT_DIM, 2, dtype=F32) / ROT_DIM)
    inv_head = jnp.concatenate([inv_freq, inv_freq, jnp.zeros((HEAD_DIM - ROT_DIM,), F32)])
    inv_lane = jnp.concatenate([inv_head] * (LANE // HEAD_DIM))
    ang = positions.astype(F32).reshape(-1)[:, None] * inv_lane[None, :]
    sin = jnp.sin(ang)
    dim = jnp.arange(LANE) % HEAD_DIM
    return jnp.cos(ang), jnp.where(dim < half, -sin, 0.0), jnp.where(dim >= half, sin, 0.0)


def kernel(x, c, positions, w_ada, b_ada, ffn1_w_gate_up, ffn1_w_down, ln1_g, ln1_b, w_in, conv_w, attn_sinks, w_out, ln2_g, ln2_b, ffn2_w_gate_up, ffn2_w_down, ln3_g, ln3_b, loss_target, m_w_ada, m_b_ada, m_ffn1_w_gate_up, m_ffn1_w_down, m_ln1_g, m_ln1_b, m_w_in, m_conv_w, m_attn_sinks, m_w_out, m_ln2_g, m_ln2_b, m_ffn2_w_gate_up, m_ffn2_w_down, m_ln3_g, m_ln3_b, v_w_ada, v_b_ada, v_ffn1_w_gate_up, v_ffn1_w_down, v_ln1_g, v_ln1_b, v_w_in, v_conv_w, v_attn_sinks, v_w_out, v_ln2_g, v_ln2_b, v_ffn2_w_gate_up, v_ffn2_w_down, v_ln3_g, v_ln3_b):
    d = D_MODEL
    nb, seq, _ = x.shape
    t = nb * seq
    f = ffn1_w_down.shape[1] * N_CHIPS
    ax, ay, ac = _position()
    chip = 2 * ax + ay
    dev = 2 * chip + ac
    pos = jnp.stack([ax, ay, ac]).astype(jnp.int32)

    x2 = x.reshape(t, d)
    tgt2 = loss_target.reshape(t, d)
    ln1 = jnp.concatenate([ln1_g, ln1_b], axis=0)
    ln2 = jnp.concatenate([ln2_g, ln2_b], axis=0)
    ln3 = jnp.concatenate([ln3_g, ln3_b], axis=0)
    sinks = attn_sinks.reshape(N_Q_HEADS)
    cos_t, sa_t, sb_t = _rope_tables(positions)

    gu_cuts = [0, 176, 352, d // 2]
    gu_part = lambda buf, s: (buf, True, gu_cuts[s], gu_cuts[s + 1] - gu_cuts[s])
    chip_arr = jnp.reshape(chip, (1,)).astype(jnp.int32)
    b_gu1 = _cast_into(ffn1_w_gate_up[0], chip_arr, True, name="cast_gu1")

    n_ada = w_ada.shape[2]
    c_all, (b_gu1,) = _allgather8(c.reshape(nb * d // LANE, LANE), name="gather_c", comm=_GatherJob([gu_part(b_gu1, 0)]))
    c_all = c_all.reshape(N_DEV * nb, d)
    b_shard = lax.dynamic_slice(b_ada, (0, chip * n_ada), (1, n_ada))
    mod_part, (b_gu1,) = _ada_fwd(c_all, w_ada[0], b_shard, name="ada_fwd", comm=_GatherJob([gu_part(b_gu1, 1)]))
    conv_rows = jnp.pad(conv_w[0], ((0, 5), (0, n_ada - conv_w.shape[2])))
    part = jnp.concatenate([mod_part, conv_rows], axis=0)
    parts, (wgu1,) = _allgather8(part, name="gather_mod", comm=_GatherJob([gu_part(b_gu1, 2)]))
    parts = parts.reshape(N_DEV, N_DEV * nb + 8, n_ada)
    mod_all = jnp.concatenate([parts[2 * k, :N_DEV * nb, :] for k in range(N_CHIPS)], axis=1)
    mod = lax.dynamic_slice(mod_all, (dev * nb, 0), (nb, N_MOD * d)).reshape(nb, N_MOD, d)
    cw_full = jnp.concatenate([parts[2 * k, N_DEV * nb:, :conv_w.shape[2]] for k in range(N_CHIPS)], axis=1)

    b_d1 = _cast_into(ffn1_w_down[0], chip_arr, False, name="cast_d1")
    b_in = _cast_into(w_in[0].T, chip_arr, False, name="cast_in")
    b_out = _cast_into(w_out[0], chip_arr, False, name="cast_out")
    b_gu2 = _cast_into(ffn2_w_gate_up[0], chip_arr, True, name="cast_gu2")
    b_d2 = _cast_into(ffn2_w_down[0], chip_arr, False, name="cast_d2")
    n_gu, n_d, n_in, n_out = (ffn1_w_gate_up.shape[2], ffn1_w_down.shape[1], w_in.shape[2], w_out.shape[1])

    def whole(buf, col_kind):
        return (buf, col_kind, 0, buf.shape[0] // (2 if col_kind else 2 * N_CHIPS))

    (h1, a1, gu1), (wd1, wout) = _ffn_up(x2, ln1, mod, wgu1, seq=seq, sc_idx=1, sh_idx=0, use_ln=False,
                                         name="ffn1_up", comm=_GatherJob([whole(b_d1, False), whole(b_out, False)]))
    (f1, xhat1, rstd1), (win_t,) = _ffn_down_ln(a1, wd1, x2, ln1, mod, seq=seq, gate_idx=2, use_ln=False,
                                                name="ffn1_down", comm=_GatherJob([whole(b_in, False)]))
    (h2, q, k, v, ubc), (b_gu2,) = _in_proj(
        xhat1, ln1, mod, win_t, cos_t, sa_t, sb_t, seq=seq, sc_idx=4, sh_idx=3, name="in_proj",
        comm=_GatherJob([gu_part(b_gu2, 0)]))
    attn, (b_gu2,) = _attention(q, k, v, sinks, seq=seq, name="attention", comm=_GatherJob([gu_part(b_gu2, 1)]))
    (mixin, mix, xhat2, rstd2), (wgu2,) = _out_proj(
        attn, ubc, cw_full, wout, xhat1, ln1, mod, seq=seq, gate_idx=5, name="out_proj",
        comm=_GatherJob([gu_part(b_gu2, 2)]))
    (h3, a3, gu3), (wd2,) = _ffn_up(xhat2, ln2, mod, wgu2, seq=seq, sc_idx=7, sh_idx=6, use_ln=True, name="ffn2_up",
                                    comm=_GatherJob([whole(b_d2, False)]))
    dr3, df3, loss_cols, dln3g, dln3b, dgate3 = _ffn_down_loss(
        a3, wd2, xhat2, ln2, mod, ln3, tgt2, seq=seq, gate_idx=8, name="ffn2_down_loss")

    def pair_sum(g, r3, col_kind, name_, comm=None):
        if col_kind:
            g3 = g.reshape(2, g.shape[0] // 2, g.shape[1])
            blk_of = lambda p_, pos_: pos_[2]
        else:
            g3 = g.reshape(2 * N_CHIPS, g.shape[0] // (2 * N_CHIPS), g.shape[1])
            blk_of = lambda p_, pos_: 2 * p_ + pos_[2]
        return _sum_pair(pos, g3, r3, blk_of, name=name_, comm=comm)

    dgu3 = _ffn_bwd_act(df3, wd2, gu3, seq=seq, name="ffn2_bwd_act")
    g_wd2 = _matmul_tn(a3, df3, tmm=f // 2, tnn=d, name="grad_wd2")
    (s32_gu2, s16_gu2), (sib_d2,) = _grad_chip_sum(pos, h3, dgu3, name="grad_wgu2", comm=_SwapJob([g_wd2], [False]))
    s32_d2, s16_d2 = pair_sum(g_wd2, sib_d2, False, "sum_pair_d2")
    (dr2, dmix, dsc3, dsh3, dgate2, dln2g, dln2b), (recv_d2,) = _bwd_in(
        dgu3, wgu2, dr3, xhat2, rstd2, ln2, mod, mix, seq=seq, w_is_nt=True, sc_idx=7, gate_idx=5,
        branch_scale=1.0, final=False, name="ffn2_bwd_in", comm=_ExchangeJob([s16_d2], [False], [n_d]))
    g_wout = _matmul_tn(mixin, dmix, tmm=d, tnn=d, name="grad_wout")
    dmixin = _matmul_nt_bf16(dmix, wout, seq=seq, name="out_proj_bwd")
    (dq, dkp, dkc, dvp, dvc, dsink), (recv_gu2, sib_out) = _attention_bwd(
        q, k, v, dmixin, sinks, seq=seq, name="attention_bwd",
        comm=_MultiJob([_ExchangeJob([s16_gu2], [True], [n_gu]), _SwapJob([g_wout], [False])]))
    s32_out, s16_out = pair_sum(g_wout, sib_out, False, "sum_pair_out")
    (dproj, dcw), (recv_out,) = _mix_bwd_assemble(
        dq, dkp, dkc, dvp, dvc, cos_t, sa_t, sb_t, dmixin, ubc, cw_full, seq=seq, name="mix_bwd",
        comm=_ExchangeJob([s16_out], [False], [n_out]))
    g_win_t = _matmul_tn(dproj, h2, tmm=IN_WIDTH // 2, tnn=d, name="grad_win")
    (dr1, df1, dsc2, dsh2, dgate1, dln1g, dln1b), (sib_in,) = _bwd_in(
        dproj, win_t, dr2, xhat1, rstd1, ln1, mod, f1, seq=seq, w_is_nt=False, sc_idx=4, gate_idx=2,
        branch_scale=0.5, final=False, name="in_proj_bwd", comm=_SwapJob([g_win_t], [False]))
    s32_in, s16_in = pair_sum(g_win_t, sib_in, False, "sum_pair_in")
    g_wd1, (recv_in,) = _matmul_tn(a1, df1, tmm=f // 2, tnn=d, name="grad_wd1",
                                   comm=_ExchangeJob([s16_in], [False], [n_in]))
    dgu1, (sib_d1,) = _ffn_bwd_act(df1, wd1, gu1, seq=seq, name="ffn1_bwd_act", comm=_SwapJob([g_wd1], [False]))
    s32_d1, s16_d1 = pair_sum(g_wd1, sib_d1, False, "sum_pair_d1")
    (s32_gu1, s16_gu1), (recv_d1,) = _grad_chip_sum(pos, h1, dgu1, name="grad_wgu1",
                                                    comm=_ExchangeJob([s16_d1], [False], [n_d]))

    def final_half(s32_, recv_, col_kind, n_shard, name_):
        return _sum_final(pos, s32_, recv_, col_kind=col_kind, n_shard=n_shard, name=name_)

    early = [final_half(s32_gu2, recv_gu2, True, n_gu, "sum_final_gu2"),
             final_half(s32_d2, recv_d2, False, n_d, "sum_final_d2"),
             final_half(s32_out, recv_out, False, n_out, "sum_final_out"),
             final_half(s32_in, recv_in, False, n_in, "sum_final_in"),
             final_half(s32_d1, recv_d1, False, n_d, "sum_final_d1")]
    (grad_x, dsc1, dsh1), (recv_gu1, full_gu2, full_d2, full_out, full_in, full_d1) = _bwd_in(
        dgu1, wgu1, dr1, x2, None, None, mod, None, seq=seq, w_is_nt=True, sc_idx=1, gate_idx=None,
        branch_scale=None, final=True, name="ffn1_bwd_in",
        comm=_MultiJob([_ExchangeJob([s16_gu1], [True], [n_gu]), _ShareJob(early)]))
    late = [final_half(s32_gu1, recv_gu1, True, n_gu, "sum_final_gu1")]

    dmod = jnp.concatenate([dsh1, dsc1, dgate1, dsh2, dsc2, dgate2, dsh3, dsc3, dgate3], axis=1)
    loss_row = jnp.sum(loss_cols, axis=1, keepdims=True) * (0.5 / d)
    lane_row = lambda a: jnp.pad(a, ((0, 0), (0, d - a.shape[1])))
    block = jnp.concatenate(
        [dmod.reshape(nb * N_MOD, d), dln1g, dln1b, dln2g, dln2b, dln3g, dln3b,
         lane_row(dcw[0:3, :]), lane_row(dsink[:, 0:1].reshape(1, N_Q_HEADS)), lane_row(loss_row)], axis=0)
    block = jnp.pad(block, ((0, SMALL_ROWS - block.shape[0]), (0, 0)))
    gathered, (full_gu1,) = _allgather8(block, name="gather_small", comm=_ShareJob(late))
    gathered = gathered.reshape(N_DEV, SMALL_ROWS, d)
    dmod_all = gathered[:, :nb * N_MOD, :].reshape(N_DEV * nb, N_MOD * d)
    dmod_shard = lax.dynamic_slice(dmod_all, (0, chip * n_ada), (N_DEV * nb, n_ada))
    small, g_w_ada, g_b_ada = _small_finish(gathered, dmod_all, dmod_shard, c_all.T, name="small_finish")
    r0 = nb * N_MOD
    loss = small[r0 + 10, 0]
    g_ln = [small[r0 + i:r0 + i + 1, :] for i in range(6)]
    g_cw_full = small[r0 + 6:r0 + 9, :CONV_WIDTH]
    g_conv = lax.dynamic_slice(g_cw_full, (0, chip * conv_w.shape[2]), (3, conv_w.shape[2]))
    g_sinks = small[r0 + 9:r0 + 10, :N_Q_HEADS]

    def flat2(a):
        return a.reshape(-1, a.shape[-1])

    def unhalve(a):
        return a.reshape(2 * a.shape[1], a.shape[2])

    results = {}

    def adamw(name_, w_, g_, m_, v_):
        g2 = flat2(g_)
        dl, nm, nv = _adamw(flat2(w_), g2, flat2(m_), flat2(v_), name="adamw_" + name_)
        results[name_] = tuple(a.reshape(w_.shape) for a in (g2, dl, nm, nv))

    adamw("w_ada", w_ada, g_w_ada, m_w_ada, v_w_ada)
    adamw("ffn2_w_gate_up", ffn2_w_gate_up, unhalve(full_gu2), m_ffn2_w_gate_up, v_ffn2_w_gate_up)
    adamw("ffn2_w_down", ffn2_w_down, unhalve(full_d2), m_ffn2_w_down, v_ffn2_w_down)
    adamw("w_out", w_out, unhalve(full_out), m_w_out, v_w_out)
    adamw("w_in", w_in, unhalve(full_in).T, m_w_in, v_w_in)
    adamw("ffn1_w_gate_up", ffn1_w_gate_up, unhalve(full_gu1), m_ffn1_w_gate_up, v_ffn1_w_gate_up)
    adamw("ffn1_w_down", ffn1_w_down, unhalve(full_d1), m_ffn1_w_down, v_ffn1_w_down)
    adamw("b_ada", b_ada, g_b_ada, m_b_ada, v_b_ada)
    adamw("ln1_g", ln1_g, g_ln[0], m_ln1_g, v_ln1_g)
    adamw("ln1_b", ln1_b, g_ln[1], m_ln1_b, v_ln1_b)
    adamw("ln2_g", ln2_g, g_ln[2], m_ln2_g, v_ln2_g)
    adamw("ln2_b", ln2_b, g_ln[3], m_ln2_b, v_ln2_b)
    adamw("ln3_g", ln3_g, g_ln[4], m_ln3_g, v_ln3_g)
    adamw("ln3_b", ln3_b, g_ln[5], m_ln3_b, v_ln3_b)
    adamw("conv_w", conv_w, g_conv, m_conv_w, v_conv_w)
    adamw("attn_sinks", attn_sinks, g_sinks, m_attn_sinks, v_attn_sinks)
    order = ["w_ada", "b_ada", "ffn1_w_gate_up", "ffn1_w_down", "ln1_g", "ln1_b", "w_in", "conv_w", "attn_sinks",
             "w_out", "ln2_g", "ln2_b", "ffn2_w_gate_up", "ffn2_w_down", "ln3_g", "ln3_b"]
    return (loss, grad_x.reshape(x.shape), *[results[n_][0] for n_ in order], *[results[n_][1] for n_ in order],
            *[results[n_][2] for n_ in order], *[results[n_][3] for n_ in order])
```

```python
import jax
import jax.numpy as jnp
from jax import lax
from jax.experimental import pallas as pl
from jax.experimental.pallas import tpu as pltpu

F32 = jnp.float32
BF16 = jnp.bfloat16
MESH = pl.DeviceIdType.MESH

D_MODEL = 1024
HEAD_DIM = 64
ATTN_WIDTH = 512
CONV_WIDTH = 512
N_Q_HEADS = 8
N_KV_HEADS = 2
GQA_GROUP = 4
KV_WIDTH = 128
WINDOW = 128
BLOCK = 128
ROT_DIM = 16
ROPE_THETA = 500000.0
N_MOD = 9
LN_EPS = 1e-5
DN_ALPHA = 2.0 ** 0.25
IN_WIDTH = 2304
N_CHIPS = 4
N_DEV = 8
SMALL_ROWS = 32

ADAM_LR = 0.001
ADAM_B1 = 0.9
ADAM_B2 = 0.999
ADAM_EPS = 1e-08
ADAM_WD = 0.01
ADAM_STEP = 10

LANE = 128
HALO = 16
COL_CHUNK = 256
VMEM_LIMIT = 56 * 1024 * 1024


def _params(sem=None, vmem=True):
    return pltpu.CompilerParams(dimension_semantics=sem, vmem_limit_bytes=VMEM_LIMIT if vmem else None)


def _sigmoid(g):
    return 0.5 * jnp.tanh(0.5 * g) + 0.5


def _row_sum(v):
    return jnp.sum(v, axis=0, keepdims=True)


ROW_CHUNK = 16
EPILOGUE_UNROLL = 8


def _fold8(v):
    return v[0:8, :] + v[8:16, :]


def _row_chunk_loop(n_rows, step, init):
    per_iter = ROW_CHUNK * EPILOGUE_UNROLL
    assert n_rows % per_iter == 0, n_rows

    def body(it, carry):
        for s in range(EPILOGUE_UNROLL):
            start = pl.multiple_of(it * per_iter + s * ROW_CHUNK, ROW_CHUNK)
            carry = step(pl.ds(start, ROW_CHUNK), carry)
        return carry

    return lax.fori_loop(0, n_rows // per_iter, body, init)


def _ln_stats(r):
    mu = jnp.mean(r, axis=-1, keepdims=True)
    rc = r - mu
    var = jnp.mean(rc * rc, axis=-1, keepdims=True)
    rstd = lax.rsqrt(var + LN_EPS)
    return rc * rstd, rstd


def _ln_bwd(dxo, xhat, rstd, g):
    dxhat = dxo * g
    m1 = jnp.mean(dxhat, axis=-1, keepdims=True)
    m2 = jnp.mean(dxhat * xhat, axis=-1, keepdims=True)
    return rstd * (dxhat - m1 - xhat * m2)


def _dot_nt(a, b):
    return lax.dot_general(a, b, (((1,), (1,)), ((), ())), preferred_element_type=F32)


def _dot_tn(a, b):
    return lax.dot_general(a, b, (((0,), (0,)), ((), ())), preferred_element_type=F32)


def _full(shape):
    nd = len(shape)
    return pl.BlockSpec(shape, lambda *_: (0,) * nd)


def _resident(shape):
    nd = len(shape)
    return pl.BlockSpec(shape, lambda *_: (0,) * nd, pipeline_mode=pl.Buffered(1))


ANY_SPEC = pl.BlockSpec(memory_space=pl.ANY)


def _pcall(body, *, name, grid, in_specs, out_specs, out_shape, args, scratch_shapes=(), comm=None, prefetch=None):
    single = not isinstance(out_shape, (list, tuple))
    out_specs = [out_specs] if single else list(out_specs)
    out_shape = [out_shape] if single else list(out_shape)
    in_specs = list(in_specs)
    scratch_shapes = list(scratch_shapes)
    sem = ("arbitrary",) * len(grid)
    n_pre = 0 if prefetch is None else 1
    pre_args = () if prefetch is None else (prefetch,)

    def call(fn, ins_, outs_, shapes_, scratch_, aliases_, operands):
        if prefetch is None:
            return pl.pallas_call(fn, name=name, grid=grid, in_specs=ins_, out_specs=outs_, out_shape=shapes_,
                                  scratch_shapes=scratch_, input_output_aliases=aliases_,
                                  compiler_params=_params(sem))(*operands)
        spec = pltpu.PrefetchScalarGridSpec(num_scalar_prefetch=1, grid=grid, in_specs=ins_, out_specs=outs_,
                                            scratch_shapes=scratch_)
        return pl.pallas_call(fn, name=name, grid_spec=spec, out_shape=shapes_,
                              input_output_aliases={n_pre + i: o for i, o in aliases_.items()},
                              compiler_params=_params(sem))(*pre_args, *operands)

    if comm is None:
        res = call(body, in_specs, out_specs, out_shape, scratch_shapes, {}, args)
        return res[0] if single else res
    n_in, n_out, n_scr = len(in_specs), len(out_specs), len(scratch_shapes)
    nci, nco = len(comm.inputs), len(comm.out_shapes)
    n_steps = 1
    for g in grid:
        n_steps *= g
    staged = n_steps >= 4
    middle_step = n_steps - 1 - max(1, n_steps // 8)

    def wrapped(*refs):
        pre, refs = refs[:n_pre], refs[n_pre:]
        ins, refs = refs[:n_in], refs[n_in:]
        cin, refs = refs[:nci], refs[nci:]
        outs, refs = refs[:n_out], refs[n_out:]
        cout, refs = refs[:nco], refs[nco:]
        scr, csems = refs[:n_scr], refs[n_scr:]
        step = pl.program_id(0)
        for ax in range(1, len(grid)):
            step = step * grid[ax] + pl.program_id(ax)

        @pl.when(step == 0)
        def _():
            comm.start(cin, cout, csems)

        body(*pre, *ins, *outs, *scr)

        if staged:
            @pl.when(step == middle_step)
            def _():
                comm.middle(cin, cout, csems)

        @pl.when(step == n_steps - 1)
        def _():
            if not staged:
                comm.middle(cin, cout, csems)
            comm.finish(cin, cout, csems)

    res = call(wrapped, in_specs + [ANY_SPEC] * nci, out_specs + [ANY_SPEC] * nco,
               out_shape + list(comm.out_shapes), scratch_shapes + list(comm.sems),
               {n_in + i: n_out + o for i, o in comm.aliases.items()}, (*args, *comm.inputs))
    main = res[:n_out]
    return (main[0] if single else main), list(res[n_out:])


def _comm_call(job, *, name):
    nci, nco = len(job.inputs), len(job.out_shapes)

    def body(*refs):
        cin, refs = refs[:nci], refs[nci:]
        cout, csems = refs[:nco], refs[nco:]
        job.start(cin, cout, csems)
        job.middle(cin, cout, csems)
        job.finish(cin, cout, csems)

    return pl.pallas_call(
        body, name=name, out_shape=list(job.out_shapes), in_specs=[ANY_SPEC] * nci, out_specs=[ANY_SPEC] * nco,
        scratch_shapes=list(job.sems), input_output_aliases=dict(job.aliases))(*job.inputs)


def _ffn_up(xin, lnp, mod, w, *, seq, sc_idx, sh_idx, use_ln, name, comm=None):
    t, d = xin.shape
    f = w.shape[1] // 2
    tm = min(512, seq)
    tpb = seq // tm
    ch = min(COL_CHUNK, f)

    def body(x_ref, ln_ref, mod_ref, w_ref, h_ref, a_ref, gu_ref):
        x = x_ref[...]
        if use_ln:
            x = x * ln_ref[0:1, :] + ln_ref[1:2, :]
        h = x * (1.0 + mod_ref[0, sc_idx:sc_idx + 1, :]) + mod_ref[0, sh_idx:sh_idx + 1, :]
        hb = h.astype(BF16)
        h_ref[...] = hb
        for j in range(f // ch):
            g = jnp.dot(hb, w_ref[:, j * ch:(j + 1) * ch], preferred_element_type=F32)
            u = jnp.dot(hb, w_ref[:, f + j * ch:f + (j + 1) * ch], preferred_element_type=F32)
            s = _sigmoid(g)
            silu = g * s
            a_ref[:, j * ch:(j + 1) * ch] = (silu * u).astype(BF16)
            gu_ref[:, j * ch:(j + 1) * ch] = (u * (s + silu * (1.0 - s))).astype(BF16)
            gu_ref[:, f + j * ch:f + (j + 1) * ch] = silu.astype(BF16)

    return _pcall(
        body, name=name, grid=(t // tm,),
        in_specs=[pl.BlockSpec((tm, d), lambda i: (i, 0)), _full((2, d)),
                  pl.BlockSpec((1, N_MOD, d), lambda i: (i // tpb, 0, 0)), _resident((d, 2 * f))],
        out_specs=[pl.BlockSpec((tm, d), lambda i: (i, 0)), pl.BlockSpec((tm, f), lambda i: (i, 0)),
                   pl.BlockSpec((tm, 2 * f), lambda i: (i, 0))],
        out_shape=[jax.ShapeDtypeStruct((t, d), BF16), jax.ShapeDtypeStruct((t, f), BF16),
                   jax.ShapeDtypeStruct((t, 2 * f), BF16)],
        args=(xin, lnp, mod, w), comm=comm)


def _ffn_down_ln(a, wd, xin, lnp_in, mod, *, seq, gate_idx, use_ln, name, comm=None):
    t, f = a.shape
    d = wd.shape[1]
    tm = min(512, seq)
    tpb = seq // tm

    def body(a_ref, wd_ref, x_ref, ln_ref, mod_ref, f_ref, xhat_ref, rstd_ref, acc):
        av = a_ref[...]
        for j in range(d // COL_CHUNK):
            acc[:, j * COL_CHUNK:(j + 1) * COL_CHUNK] = jnp.dot(
                av, wd_ref[:, j * COL_CHUNK:(j + 1) * COL_CHUNK], preferred_element_type=F32)
        scale = 0.5 * (1.0 + mod_ref[0, gate_idx:gate_idx + 1, :])

        fo = acc[...]
        x = x_ref[...]
        if use_ln:
            x = x * ln_ref[0:1, :] + ln_ref[1:2, :]
        xhat, rstd = _ln_stats(DN_ALPHA * x + scale * fo)
        f_ref[...] = fo.astype(BF16)
        xhat_ref[...] = xhat
        rstd_ref[...] = rstd

    return _pcall(
        body, name=name, grid=(t // tm,),
        in_specs=[pl.BlockSpec((tm, f), lambda i: (i, 0)), _resident((f, d)),
                  pl.BlockSpec((tm, d), lambda i: (i, 0)), _full((2, d)),
                  pl.BlockSpec((1, N_MOD, d), lambda i: (i // tpb, 0, 0))],
        out_specs=[pl.BlockSpec((tm, d), lambda i: (i, 0)), pl.BlockSpec((tm, d), lambda i: (i, 0)),
                   pl.BlockSpec((tm, 1), lambda i: (i, 0))],
        out_shape=[jax.ShapeDtypeStruct((t, d), BF16), jax.ShapeDtypeStruct((t, d), F32),
                   jax.ShapeDtypeStruct((t, 1), F32)],
        scratch_shapes=[pltpu.VMEM((tm, d), F32)],
        args=(a, wd, xin, lnp_in, mod), comm=comm)


def _ffn_down_loss(a, wd, xhat_in, lnp_in, mod, lnp_out, tgt, *, seq, gate_idx, name):
    t, f = a.shape
    d = wd.shape[1]
    nb = t // seq
    tm = min(512, seq)
    tpb = seq // tm

    def body(a_ref, wd_ref, x_ref, lnin_ref, mod_ref, lnout_ref, tgt_ref,
             dr_ref, df_ref, loss_ref, dg_ref, db_ref, dgate_ref, acc):
        i = pl.program_id(0)
        av = a_ref[...]
        for j in range(d // COL_CHUNK):
            acc[:, j * COL_CHUNK:(j + 1) * COL_CHUNK] = jnp.dot(
                av, wd_ref[:, j * COL_CHUNK:(j + 1) * COL_CHUNK], preferred_element_type=F32)
        scale = 0.5 * (1.0 + mod_ref[0, gate_idx:gate_idx + 1, :])
        g_in, b_in = lnin_ref[0:1, :], lnin_ref[1:2, :]
        g_out, b_out = lnout_ref[0:1, :], lnout_ref[1:2, :]

        def chunk(rows, carry):
            s_loss, s_dg, s_db, s_gate = carry
            fo = acc[rows, :]
            xhat, rstd = _ln_stats(DN_ALPHA * (x_ref[rows, :] * g_in + b_in) + scale * fo)
            e = xhat * g_out + b_out - tgt_ref[rows, :]
            dy = e * (1.0 / d)
            dr = _ln_bwd(dy, xhat, rstd, g_out)
            dr_ref[rows, :] = dr
            df_ref[rows, :] = (scale * dr).astype(BF16)
            return (s_loss + _fold8(e * e), s_dg + _fold8(dy * xhat), s_db + _fold8(dy),
                    s_gate + _fold8(0.5 * fo * dr))

        zero = jnp.zeros((8, d), F32)
        s_loss, s_dg, s_db, s_gate = _row_chunk_loop(tm, chunk, (zero, zero, zero, zero))

        @pl.when(i == 0)
        def _():
            loss_ref[...] = jnp.zeros_like(loss_ref)
            dg_ref[...] = jnp.zeros_like(dg_ref)
            db_ref[...] = jnp.zeros_like(db_ref)

        @pl.when(i % tpb == 0)
        def _():
            dgate_ref[...] = jnp.zeros_like(dgate_ref)

        loss_ref[...] += _row_sum(s_loss)
        dg_ref[...] += _row_sum(s_dg)
        db_ref[...] += _row_sum(s_db)
        dgate_ref[0] += _row_sum(s_gate)

    return pl.pallas_call(
        body, name=name, grid=(t // tm,), scratch_shapes=[pltpu.VMEM((tm, d), F32)],
        in_specs=[pl.BlockSpec((tm, f), lambda i: (i, 0)), _resident((f, d)),
                  pl.BlockSpec((tm, d), lambda i: (i, 0)), _full((2, d)),
                  pl.BlockSpec((1, N_MOD, d), lambda i: (i // tpb, 0, 0)), _full((2, d)),
                  pl.BlockSpec((tm, d), lambda i: (i, 0))],
        out_specs=[pl.BlockSpec((tm, d), lambda i: (i, 0)), pl.BlockSpec((tm, d), lambda i: (i, 0)),
                   _full((1, d)), _full((1, d)), _full((1, d)),
                   pl.BlockSpec((1, 1, d), lambda i: (i // tpb, 0, 0))],
        out_shape=[jax.ShapeDtypeStruct((t, d), F32), jax.ShapeDtypeStruct((t, d), BF16),
                   jax.ShapeDtypeStruct((1, d), F32), jax.ShapeDtypeStruct((1, d), F32),
                   jax.ShapeDtypeStruct((1, d), F32), jax.ShapeDtypeStruct((nb, 1, d), F32)],
        compiler_params=_params(("arbitrary",)),
    )(a, wd, xhat_in, lnp_in, mod, lnp_out, tgt)


def _rope(v, cos, sa, sb):
    return v * cos + pltpu.roll(v, LANE - ROT_DIM // 2, 1) * sa + pltpu.roll(v, ROT_DIM // 2, 1) * sb


def _rope_t(dy, cos, sa, sb):
    return dy * cos + pltpu.roll(dy * sa, ROT_DIM // 2, 1) + pltpu.roll(dy * sb, LANE - ROT_DIM // 2, 1)


def _in_proj(xhat, lnp, mod, w_t, cos, sa, sb, *, seq, sc_idx, sh_idx, name, comm=None):
    t, d = xhat.shape
    tm = min(512, seq)
    tpb = seq // tm
    n_conv = 3 * CONV_WIDTH

    def body(x_ref, ln_ref, mod_ref, w_ref, cos_ref, sa_ref, sb_ref, h_ref, q_ref, k_ref, v_ref, ubc_ref):
        x = x_ref[...] * ln_ref[0:1, :] + ln_ref[1:2, :]
        h = x * (1.0 + mod_ref[0, sc_idx:sc_idx + 1, :]) + mod_ref[0, sh_idx:sh_idx + 1, :]
        hb = h.astype(BF16)
        h_ref[...] = hb
        cos_t, sa_t, sb_t = cos_ref[...], sa_ref[...], sb_ref[...]
        for j in range(ATTN_WIDTH // COL_CHUNK):
            p = _dot_nt(hb, w_ref[j * COL_CHUNK:(j + 1) * COL_CHUNK, :])
            for s in range(COL_CHUNK // LANE):
                q_ref[:, j * COL_CHUNK + s * LANE:j * COL_CHUNK + (s + 1) * LANE] = _rope(
                    p[:, s * LANE:(s + 1) * LANE], cos_t, sa_t, sb_t).astype(BF16)
        p = _dot_nt(hb, w_ref[ATTN_WIDTH:ATTN_WIDTH + 2 * KV_WIDTH, :])
        k_ref[...] = _rope(p[:, 0:KV_WIDTH], cos_t, sa_t, sb_t).astype(BF16)
        v_ref[...] = p[:, KV_WIDTH:].astype(BF16)
        base = ATTN_WIDTH + 2 * KV_WIDTH
        for j in range(n_conv // COL_CHUNK):
            ubc_ref[:, j * COL_CHUNK:(j + 1) * COL_CHUNK] = _dot_nt(
                hb, w_ref[base + j * COL_CHUNK:base + (j + 1) * COL_CHUNK, :]).astype(BF16)

    row = lambda w: pl.BlockSpec((tm, w), lambda i: (i, 0))
    return _pcall(
        body, name=name, grid=(t // tm,),
        in_specs=[row(d), _full((2, d)), pl.BlockSpec((1, N_MOD, d), lambda i: (i // tpb, 0, 0)),
                  _resident((IN_WIDTH, d)), row(LANE), row(LANE), row(LANE)],
        out_specs=[row(d), row(ATTN_WIDTH), row(KV_WIDTH), row(KV_WIDTH), row(n_conv)],
        out_shape=[jax.ShapeDtypeStruct((t, d), BF16), jax.ShapeDtypeStruct((t, ATTN_WIDTH), BF16),
                   jax.ShapeDtypeStruct((t, KV_WIDTH), BF16), jax.ShapeDtypeStruct((t, KV_WIDTH), BF16),
                   jax.ShapeDtypeStruct((t, n_conv), BF16)],
        args=(xhat, lnp, mod, w_t, cos, sa, sb), comm=comm)


ATTN_TILE_BLOCKS = 2


def _attn_sub_block(s, tile, nblk, kp_ref, kc_ref, vp_ref, vc_ref):
    rows = slice(s * BLOCK, (s + 1) * BLOCK)
    if s == 0:
        first = ((tile * ATTN_TILE_BLOCKS) % nblk) == 0
        return rows, (kp_ref, slice(0, BLOCK)), (kc_ref, rows), (vp_ref, slice(0, BLOCK)), (vc_ref, rows), first
    before = slice((s - 1) * BLOCK, s * BLOCK)
    return rows, (kc_ref, before), (kc_ref, rows), (vc_ref, before), (vc_ref, rows), False


def _attn_group(q_ref, rows, k_prev, k_cur, v_prev, v_cur, sink_ref, g, first):
    lo, hi = g * HEAD_DIM, (g + 1) * HEAD_DIM
    kk = jnp.concatenate([k_prev[0][k_prev[1], lo:hi], k_cur[0][k_cur[1], lo:hi]], axis=0)
    vv = jnp.concatenate([v_prev[0][v_prev[1], lo:hi], v_cur[0][v_cur[1], lo:hi]], axis=0)
    qs = jnp.concatenate([q_ref[rows, (GQA_GROUP * g + j) * HEAD_DIM:(GQA_GROUP * g + j + 1) * HEAD_DIM]
                          for j in range(GQA_GROUP)], axis=0)
    rows = GQA_GROUP * BLOCK
    row = lax.broadcasted_iota(jnp.int32, (rows, 2 * BLOCK), 0)
    ki = lax.broadcasted_iota(jnp.int32, (rows, 2 * BLOCK), 1)
    diff = (row & (BLOCK - 1)) + BLOCK - ki
    valid = (diff >= 0) & (diff < WINDOW) & ((ki >= BLOCK) | jnp.logical_not(first))
    s = _dot_nt(qs, kk) * (HEAD_DIM ** -0.5)
    s = jnp.where(valid, s, -1e30)
    rcol = lax.broadcasted_iota(jnp.int32, (rows, 1), 0)
    sink = jnp.zeros((rows, 1), F32)
    for j in range(GQA_GROUP):
        sink = jnp.where(rcol // BLOCK == j, sink_ref[GQA_GROUP * g + j], sink)
    m = jnp.maximum(jnp.max(s, axis=1, keepdims=True), sink)
    p = jnp.exp(s - m)
    ps = jnp.exp(sink - m)
    inv = 1.0 / (jnp.sum(p, axis=1, keepdims=True) + ps)
    return qs, kk, vv, p * inv, ps * inv


def _attention(q, k, v, sinks, *, seq, name, comm=None):
    t = q.shape[0]
    nblk = seq // BLOCK
    tile = ATTN_TILE_BLOCKS * BLOCK

    def body(q_ref, kp_ref, kc_ref, vp_ref, vc_ref, sink_ref, o_ref):
        for s in range(ATTN_TILE_BLOCKS):
            rows, k_prev, k_cur, v_prev, v_cur, first = _attn_sub_block(
                s, pl.program_id(0), nblk, kp_ref, kc_ref, vp_ref, vc_ref)
            outs = []
            for g in range(N_KV_HEADS):
                _, _, vv, pn, _ = _attn_group(q_ref, rows, k_prev, k_cur, v_prev, v_cur, sink_ref, g, first)
                o = jnp.dot(pn.astype(BF16), vv, preferred_element_type=F32)
                outs += [o[j * BLOCK:(j + 1) * BLOCK, :] for j in range(GQA_GROUP)]
            o_ref[rows, :] = jnp.concatenate(outs, axis=1).astype(BF16)

    cur = lambda w: pl.BlockSpec((tile, w), lambda n: (n, 0))
    prev = lambda w: pl.BlockSpec((BLOCK, w), lambda n: (jnp.maximum(n * ATTN_TILE_BLOCKS - 1, 0), 0))
    return _pcall(
        body, name=name, grid=(t // tile,),
        in_specs=[cur(ATTN_WIDTH), prev(KV_WIDTH), cur(KV_WIDTH), prev(KV_WIDTH), cur(KV_WIDTH),
                  pl.BlockSpec(memory_space=pltpu.SMEM)],
        out_specs=cur(ATTN_WIDTH),
        out_shape=jax.ShapeDtypeStruct((t, ATTN_WIDTH), BF16),
        args=(q, k, k, v, v, sinks), comm=comm)


def _out_proj(attn, ubc, cw, wout, xhat_in, lnp_in, mod, *, seq, gate_idx, name, comm=None):
    t, d = xhat_in.shape
    tm = min(512, seq)
    tpb = seq // tm
    cwid = CONV_WIDTH

    def body(attn_ref, ubc_ref, halo_ref, cw_ref, w_ref, x_ref, ln_ref, mod_ref,
             mixin_ref, mix_ref, xhat_ref, rstd_ref, zbuf, acc):
        first = (pl.program_id(0) % tpb) == 0
        u, bg, cg = (ubc_ref[:, s * cwid:(s + 1) * cwid].astype(F32) for s in range(3))
        z = cg * u
        hz = halo_ref[:, 2 * cwid:3 * cwid].astype(F32) * halo_ref[:, 0:cwid].astype(F32)
        zbuf[0:HALO, :] = jnp.where(first, 0.0, hz)
        zbuf[HALO:HALO + tm, :] = z
        y = (cw_ref[0:1, :] * zbuf[HALO - 2:HALO - 2 + tm, :] + cw_ref[1:2, :] * zbuf[HALO - 1:HALO - 1 + tm, :]
             + cw_ref[2:3, :] * z)
        mixin_ref[:, 0:ATTN_WIDTH] = attn_ref[...]
        mixin_ref[:, ATTN_WIDTH:] = (bg * y).astype(BF16)
        mv = mixin_ref[...]
        for j in range(d // COL_CHUNK):
            acc[:, j * COL_CHUNK:(j + 1) * COL_CHUNK] = jnp.dot(
                mv, w_ref[:, j * COL_CHUNK:(j + 1) * COL_CHUNK], preferred_element_type=F32)
        scale = 1.0 + mod_ref[0, gate_idx:gate_idx + 1, :]

        mix = acc[...]
        xhat, rstd = _ln_stats(DN_ALPHA * (x_ref[...] * ln_ref[0:1, :] + ln_ref[1:2, :]) + scale * mix)
        mix_ref[...] = mix.astype(BF16)
        xhat_ref[...] = xhat
        rstd_ref[...] = rstd

    row = lambda w: pl.BlockSpec((tm, w), lambda i: (i, 0))
    return _pcall(
        body, name=name, grid=(t // tm,),
        in_specs=[row(ATTN_WIDTH), row(3 * cwid),
                  pl.BlockSpec((HALO, 3 * cwid), lambda i: (jnp.maximum(i * (tm // HALO) - 1, 0), 0)),
                  _full((8, cwid)), _resident((d, d)), row(d), _full((2, d)),
                  pl.BlockSpec((1, N_MOD, d), lambda i: (i // tpb, 0, 0))],
        out_specs=[row(d), row(d), row(d), row(1)],
        out_shape=[jax.ShapeDtypeStruct((t, d), BF16), jax.ShapeDtypeStruct((t, d), BF16),
                   jax.ShapeDtypeStruct((t, d), F32), jax.ShapeDtypeStruct((t, 1), F32)],
        scratch_shapes=[pltpu.VMEM((tm + HALO, cwid), F32), pltpu.VMEM((tm, d), F32)],
        args=(attn, ubc, ubc, cw, wout, xhat_in, lnp_in, mod), comm=comm)


def _ffn_bwd_act(df, wd, gu, *, seq, name, comm=None):
    t, d = df.shape
    f = wd.shape[0]
    tm = min(512, seq)
    ch = min(COL_CHUNK, f)

    def body(df_ref, wd_ref, gu_ref, dgu_ref):
        dfv = df_ref[...]
        for j in range(f // ch):
            da = _dot_nt(dfv, wd_ref[j * ch:(j + 1) * ch, :])
            dgu_ref[:, j * ch:(j + 1) * ch] = (da * gu_ref[:, j * ch:(j + 1) * ch].astype(F32)).astype(BF16)
            dgu_ref[:, f + j * ch:f + (j + 1) * ch] = (
                da * gu_ref[:, f + j * ch:f + (j + 1) * ch].astype(F32)).astype(BF16)

    return _pcall(
        body, name=name, grid=(t // tm,),
        in_specs=[pl.BlockSpec((tm, d), lambda i: (i, 0)), _resident((f, d)),
                  pl.BlockSpec((tm, 2 * f), lambda i: (i, 0))],
        out_specs=pl.BlockSpec((tm, 2 * f), lambda i: (i, 0)),
        out_shape=jax.ShapeDtypeStruct((t, 2 * f), BF16),
        args=(df, wd, gu), comm=comm)


def _bwd_in(a, w, dr, xin, rstd_prev, lnp_prev, mod, branch_prev, *, seq, w_is_nt, sc_idx, gate_idx,
            branch_scale, final, name, comm=None):
    t, kdim = a.shape
    d = dr.shape[1]
    nb = t // seq
    tm = min(512, seq)
    tpb = seq // tm

    def body(*refs):
        if final:
            a_ref, w_ref, dr_ref, x_ref, mod_ref, dx_ref, dsc_ref, dsh_ref, acc = refs
        else:
            (a_ref, w_ref, dr_ref, x_ref, rstd_ref, ln_ref, mod_ref, br_ref,
             drp_ref, dbr_ref, dsc_ref, dsh_ref, dgate_ref, dg_ref, db_ref, acc) = refs
        i = pl.program_id(0)
        av = a_ref[...]
        for j in range(d // COL_CHUNK):
            cols = slice(j * COL_CHUNK, (j + 1) * COL_CHUNK)
            acc[:, cols] = (_dot_nt(av, w_ref[cols, :]) if w_is_nt
                            else jnp.dot(av, w_ref[:, cols], preferred_element_type=F32))
        sc1 = 1.0 + mod_ref[0, sc_idx:sc_idx + 1, :]
        if not final:
            g_prev, b_prev = ln_ref[0:1, :], ln_ref[1:2, :]
            bscale = branch_scale * (1.0 + mod_ref[0, gate_idx:gate_idx + 1, :])

        def chunk(rows, carry):
            dh = acc[rows, :]
            dx = DN_ALPHA * dr_ref[rows, :] + dh * sc1
            if final:
                dx_ref[rows, :] = dx
                return carry[0] + _fold8(dh * x_ref[rows, :]), carry[1] + _fold8(dh)
            xhat = x_ref[rows, :]
            drp = _ln_bwd(dx, xhat, rstd_ref[rows, :], g_prev)
            drp_ref[rows, :] = drp
            dbr_ref[rows, :] = (bscale * drp).astype(BF16)
            return (carry[0] + _fold8(dh * (xhat * g_prev + b_prev)), carry[1] + _fold8(dh),
                    carry[2] + _fold8(branch_scale * br_ref[rows, :].astype(F32) * drp),
                    carry[3] + _fold8(dx * xhat), carry[4] + _fold8(dx))

        zero = jnp.zeros((8, d), F32)
        sums = _row_chunk_loop(tm, chunk, (zero,) * (2 if final else 5))

        @pl.when((i % tpb) == 0)
        def _():
            dsc_ref[...] = jnp.zeros_like(dsc_ref)
            dsh_ref[...] = jnp.zeros_like(dsh_ref)
            if not final:
                dgate_ref[...] = jnp.zeros_like(dgate_ref)

        dsc_ref[0] += _row_sum(sums[0])
        dsh_ref[0] += _row_sum(sums[1])
        if not final:
            @pl.when(i == 0)
            def _():
                dg_ref[...] = jnp.zeros_like(dg_ref)
                db_ref[...] = jnp.zeros_like(db_ref)

            dgate_ref[0] += _row_sum(sums[2])
            dg_ref[...] += _row_sum(sums[3])
            db_ref[...] += _row_sum(sums[4])

    row = lambda w_: pl.BlockSpec((tm, w_), lambda i: (i, 0))
    vec = pl.BlockSpec((1, 1, d), lambda i: (i // tpb, 0, 0))
    mod_spec = pl.BlockSpec((1, N_MOD, d), lambda i: (i // tpb, 0, 0))
    vshape = jax.ShapeDtypeStruct((nb, 1, d), F32)
    if final:
        in_specs = [row(kdim), _resident(w.shape), row(d), row(d), mod_spec]
        args = (a, w, dr, xin, mod)
        out_specs = [row(d), vec, vec]
        out_shape = [jax.ShapeDtypeStruct((t, d), F32), vshape, vshape]
    else:
        in_specs = [row(kdim), _resident(w.shape), row(d), row(d), row(1), _full((2, d)), mod_spec, row(d)]
        args = (a, w, dr, xin, rstd_prev, lnp_prev, mod, branch_prev)
        out_specs = [row(d), row(d), vec, vec, vec, _full((1, d)), _full((1, d))]
        out_shape = [jax.ShapeDtypeStruct((t, d), F32), jax.ShapeDtypeStruct((t, d), BF16), vshape, vshape, vshape,
                     jax.ShapeDtypeStruct((1, d), F32), jax.ShapeDtypeStruct((1, d), F32)]
    return _pcall(
        body, name=name, grid=(t // tm,), in_specs=in_specs, out_specs=out_specs, out_shape=out_shape,
        scratch_shapes=[pltpu.VMEM((tm, d), F32)], args=args, comm=comm)


def _matmul_tn(a, b, *, tmm, tnn, name, comm=None):
    t, m = a.shape
    n = b.shape[1]
    tk = min(2048, t)

    def body(a_ref, b_ref, o_ref):
        @pl.when(pl.program_id(2) == 0)
        def _():
            o_ref[...] = jnp.zeros_like(o_ref)
        o_ref[...] += _dot_tn(a_ref[...], b_ref[...])

    return _pcall(
        body, name=name, grid=(m // tmm, n // tnn, t // tk),
        in_specs=[pl.BlockSpec((tk, tmm), lambda i, j, k: (k, i)), pl.BlockSpec((tk, tnn), lambda i, j, k: (k, j))],
        out_specs=pl.BlockSpec((tmm, tnn), lambda i, j, k: (i, j)),
        out_shape=jax.ShapeDtypeStruct((m, n), F32),
        args=(a, b), comm=comm)


def _grad_chip_sum(pos, a, b, *, name, comm=None):
    t, m = a.shape
    n = b.shape[1]
    hm, tnn = m // 2, n // N_CHIPS
    tk = min(2048, t)
    nk = t // tk
    n_j = n // tnn

    def body(pos_ref, a_ref, b_ref, s32_ref, s16_ref, land_ref, acc, theirs, send_sems, recv_sems, copy_sem):
        p, j, k = pl.program_id(0), pl.program_id(1), pl.program_id(2)
        x, y, c = _position()

        def push(jj):
            return pltpu.make_async_remote_copy(
                src_ref=acc.at[jj], dst_ref=land_ref.at[jj], send_sem=send_sems.at[jj], recv_sem=recv_sems.at[jj],
                device_id=(x, y, 1 - c), device_id_type=MESH)

        fetch = pltpu.make_async_copy(land_ref.at[j], theirs, copy_sem)

        @pl.when(jnp.logical_and(p == 1, k == 0))
        def _():
            push(j).wait_send()
            push(j).wait_recv()
            fetch.start()

        part = _dot_tn(a_ref[...], b_ref[...])

        @pl.when(k == 0)
        def _():
            acc[j] = part

        @pl.when(k > 0)
        def _():
            acc[j] += part

        @pl.when(jnp.logical_and(p == 0, k == nk - 1))
        def _():
            push(j).start()

        @pl.when(jnp.logical_and(p == 1, k == nk - 1))
        def _():
            fetch.wait()
            s = acc[j] + theirs[...]
            s32_ref[0] = s
            s16_ref[0] = s.astype(BF16)

    half = lambda p, pos_ref: 1 - pos_ref[2] - p + 2 * p * pos_ref[2]
    out_tile = pl.BlockSpec((1, hm, tnn), lambda p, j, k, pos_ref: (0, 0, j * p))
    shape = lambda dt: jax.ShapeDtypeStruct((1, hm, n), dt)
    out = _pcall(
        body, name=name, grid=(2, n_j, nk),
        in_specs=[pl.BlockSpec((tk, hm), lambda p, j, k, pos_ref: (k, half(p, pos_ref))),
                  pl.BlockSpec((tk, tnn), lambda p, j, k, pos_ref: (k, j))],
        out_specs=[out_tile, out_tile, ANY_SPEC],
        out_shape=[shape(F32), shape(BF16), jax.ShapeDtypeStruct((n_j, hm, tnn), F32)],
        scratch_shapes=[pltpu.VMEM((n_j, hm, tnn), F32), pltpu.VMEM((hm, tnn), F32),
                        pltpu.SemaphoreType.DMA((n_j,)), pltpu.SemaphoreType.DMA((n_j,)), pltpu.SemaphoreType.DMA],
        args=(a, b), prefetch=pos, comm=comm)
    if comm is None:
        return out[0], out[1]
    (s32, s16, _), extra = out
    return (s32, s16), extra


def _matmul_nt_bf16(a, w, *, seq, name):
    t, kdim = a.shape
    n = w.shape[0]
    tm = min(512, seq)

    def body(a_ref, w_ref, o_ref):
        av = a_ref[...]
        for j in range(n // COL_CHUNK):
            o_ref[:, j * COL_CHUNK:(j + 1) * COL_CHUNK] = _dot_nt(
                av, w_ref[j * COL_CHUNK:(j + 1) * COL_CHUNK, :]).astype(BF16)

    return pl.pallas_call(
        body, name=name, grid=(t // tm,),
        in_specs=[pl.BlockSpec((tm, kdim), lambda i: (i, 0)), _resident((n, kdim))],
        out_specs=pl.BlockSpec((tm, n), lambda i: (i, 0)),
        out_shape=jax.ShapeDtypeStruct((t, n), BF16),
        compiler_params=_params(("arbitrary",)),
    )(a, w)


def _attention_bwd(q, k, v, dmixin, sinks, *, seq, name, comm=None):
    t = q.shape[0]
    nblk = seq // BLOCK
    tile = ATTN_TILE_BLOCKS * BLOCK

    def body(q_ref, kp_ref, kc_ref, vp_ref, vc_ref, do_ref, sink_ref,
             dq_ref, dkp_ref, dkc_ref, dvp_ref, dvc_ref, dsink_ref):
        n = pl.program_id(0)

        @pl.when(n == 0)
        def _():
            dsink_ref[...] = jnp.zeros_like(dsink_ref)

        srow = lax.broadcasted_iota(jnp.int32, (8, LANE), 0)
        dsink = jnp.zeros((8, LANE), F32)
        for s in range(ATTN_TILE_BLOCKS):
            rows, k_prev, k_cur, v_prev, v_cur, first = _attn_sub_block(s, n, nblk, kp_ref, kc_ref, vp_ref, vc_ref)
            dqs, dks, dvs = [], [], []
            for g in range(N_KV_HEADS):
                qs, kk, vv, pn, psn = _attn_group(q_ref, rows, k_prev, k_cur, v_prev, v_cur, sink_ref, g, first)
                dos = jnp.concatenate(
                    [do_ref[rows, (GQA_GROUP * g + j) * HEAD_DIM:(GQA_GROUP * g + j + 1) * HEAD_DIM]
                     for j in range(GQA_GROUP)], axis=0)
                dp = _dot_nt(dos, vv)
                delta = jnp.sum(pn * dp, axis=1, keepdims=True)
                ds = pn * (dp - delta)
                dsk = psn * delta
                for j in range(GQA_GROUP):
                    tot = jnp.sum(dsk[j * BLOCK:(j + 1) * BLOCK, :], axis=0, keepdims=True)
                    dsink = dsink - jnp.where(srow == GQA_GROUP * g + j, tot, 0.0)
                dsb = (ds * (HEAD_DIM ** -0.5)).astype(BF16)
                dqg = jnp.dot(dsb, kk, preferred_element_type=F32)
                dqs += [dqg[j * BLOCK:(j + 1) * BLOCK, :] for j in range(GQA_GROUP)]
                dks.append(_dot_tn(dsb, qs))
                dvs.append(_dot_tn(pn.astype(BF16), dos))
            dq_ref[rows, :] = jnp.concatenate(dqs, axis=1)
            dkp_ref[rows, :] = jnp.concatenate([x[0:BLOCK, :] for x in dks], axis=1)
            dkc_ref[rows, :] = jnp.concatenate([x[BLOCK:, :] for x in dks], axis=1)
            dvp_ref[rows, :] = jnp.concatenate([x[0:BLOCK, :] for x in dvs], axis=1)
            dvc_ref[rows, :] = jnp.concatenate([x[BLOCK:, :] for x in dvs], axis=1)
        dsink_ref[...] += dsink

    cur = lambda w: pl.BlockSpec((tile, w), lambda n: (n, 0))
    prev = lambda w: pl.BlockSpec((BLOCK, w), lambda n: (jnp.maximum(n * ATTN_TILE_BLOCKS - 1, 0), 0))
    kv = jax.ShapeDtypeStruct((t, KV_WIDTH), F32)
    return _pcall(
        body, name=name, grid=(t // tile,),
        in_specs=[cur(ATTN_WIDTH), prev(KV_WIDTH), cur(KV_WIDTH), prev(KV_WIDTH), cur(KV_WIDTH), cur(ATTN_WIDTH),
                  pl.BlockSpec(memory_space=pltpu.SMEM)],
        out_specs=[cur(ATTN_WIDTH), cur(KV_WIDTH), cur(KV_WIDTH), cur(KV_WIDTH), cur(KV_WIDTH), _full((8, LANE))],
        out_shape=[jax.ShapeDtypeStruct((t, ATTN_WIDTH), F32), kv, kv, kv, kv, jax.ShapeDtypeStruct((8, LANE), F32)],
        args=(q, k, k, v, v, dmixin, sinks), comm=comm)


def _mix_bwd_assemble(dq, dkp, dkc, dvp, dvc, cos, sa, sb, dmixin, ubc, cw, *, seq, name, comm=None):
    t = dq.shape[0]
    cwid = CONV_WIDTH
    tm = min(2 * BLOCK, seq)
    tiles_per_seq = seq // tm
    ntile = t // tm
    nblk_all = t // BLOCK
    per_tile = tm // BLOCK

    def body(*refs):
        dq_ref, dkc_ref, dvc_ref = refs[0:3]
        dkp_refs, dvp_refs = refs[3:3 + per_tile], refs[3 + per_tile:3 + 2 * per_tile]
        (cos_ref, sa_ref, sb_ref, dco_ref, dcon_ref, ubc_ref, hprev_ref, hnext_ref, cw_ref,
         dproj_ref, dcw_ref, zbuf, dybuf) = refs[3 + 2 * per_tile:]
        i = pl.program_id(0)
        first = (i % tiles_per_seq) == 0
        last = (i % tiles_per_seq) == tiles_per_seq - 1
        glast = i == ntile - 1

        @pl.when(i == 0)
        def _():
            dcw_ref[...] = jnp.zeros_like(dcw_ref)

        def with_next_block(cur_ref, nxt_refs):
            nxt = [r[...] for r in nxt_refs]
            nxt[-1] = jnp.where(glast, 0.0, nxt[-1])
            return cur_ref[...] + jnp.concatenate(nxt, axis=0)

        cos_t, sa_t, sb_t = cos_ref[...], sa_ref[...], sb_ref[...]
        for j in range(ATTN_WIDTH // LANE):
            dproj_ref[:, j * LANE:(j + 1) * LANE] = _rope_t(
                dq_ref[:, j * LANE:(j + 1) * LANE], cos_t, sa_t, sb_t).astype(BF16)
        dk = with_next_block(dkc_ref, dkp_refs)
        dproj_ref[:, ATTN_WIDTH:ATTN_WIDTH + KV_WIDTH] = _rope_t(dk, cos_t, sa_t, sb_t).astype(BF16)
        dv = with_next_block(dvc_ref, dvp_refs)
        dproj_ref[:, ATTN_WIDTH + KV_WIDTH:ATTN_WIDTH + 2 * KV_WIDTH] = dv.astype(BF16)

        u, bg, cg = (ubc_ref[:, s * cwid:(s + 1) * cwid].astype(F32) for s in range(3))
        z = cg * u
        hz = hprev_ref[:, 2 * cwid:3 * cwid].astype(F32) * hprev_ref[:, 0:cwid].astype(F32)
        zbuf[0:HALO, :] = jnp.where(first, 0.0, hz)
        zbuf[HALO:HALO + tm, :] = z
        z2, z1 = zbuf[HALO - 2:HALO - 2 + tm, :], zbuf[HALO - 1:HALO - 1 + tm, :]
        w0, w1, w2 = cw_ref[0:1, :], cw_ref[1:2, :], cw_ref[2:3, :]
        y = w0 * z2 + w1 * z1 + w2 * z
        dco = dco_ref[...].astype(F32)
        dyc = dco * bg
        dyn = dcon_ref[...].astype(F32) * hnext_ref[:, cwid:2 * cwid].astype(F32)
        dybuf[0:tm, :] = dyc
        dybuf[tm:tm + HALO, :] = jnp.where(last, 0.0, dyn)
        dz = w2 * dyc + w1 * dybuf[1:1 + tm, :] + w0 * dybuf[2:2 + tm, :]
        srow = lax.broadcasted_iota(jnp.int32, (8, cwid), 0)
        dcw_ref[...] += (jnp.where(srow == 0, _row_sum(dyc * z2), 0.0) + jnp.where(srow == 1, _row_sum(dyc * z1), 0.0)
                         + jnp.where(srow == 2, _row_sum(dyc * z), 0.0))
        base = ATTN_WIDTH + 2 * KV_WIDTH
        dproj_ref[:, base:base + cwid] = (dz * cg).astype(BF16)
        dproj_ref[:, base + cwid:base + 2 * cwid] = (dco * y).astype(BF16)
        dproj_ref[:, base + 2 * cwid:base + 3 * cwid] = (dz * u).astype(BF16)

    cur = lambda w: pl.BlockSpec((tm, w), lambda i: (i, 0))
    nxt = [pl.BlockSpec((BLOCK, KV_WIDTH), lambda i, s=s: (jnp.minimum(i * per_tile + s + 1, nblk_all - 1), 0))
           for s in range(per_tile)]
    prev_halo = pl.BlockSpec((HALO, 3 * cwid), lambda i: (jnp.maximum(i * (tm // HALO) - 1, 0), 0))
    next_halo = lambda w, col: pl.BlockSpec(
        (HALO, w), lambda i: (jnp.minimum((i + 1) * (tm // HALO), t // HALO - 1), col))
    return _pcall(
        body, name=name, grid=(ntile,),
        in_specs=[cur(ATTN_WIDTH), cur(KV_WIDTH), cur(KV_WIDTH), *nxt, *nxt,
                  cur(LANE), cur(LANE), cur(LANE),
                  pl.BlockSpec((tm, cwid), lambda i: (i, 1)), next_halo(cwid, 1),
                  cur(3 * cwid), prev_halo, next_halo(3 * cwid, 0), _full((8, cwid))],
        out_specs=[cur(IN_WIDTH), _full((8, cwid))],
        out_shape=[jax.ShapeDtypeStruct((t, IN_WIDTH), BF16), jax.ShapeDtypeStruct((8, cwid), F32)],
        scratch_shapes=[pltpu.VMEM((tm + HALO, cwid), F32), pltpu.VMEM((tm + HALO, cwid), F32)],
        args=(dq, dkc, dvc, *([dkp] * per_tile), *([dvp] * per_tile), cos, sa, sb, dmixin, dmixin,
              ubc, ubc, ubc, cw), comm=comm)


def _ada_fwd(c_all, w_ada, b_ada_shard, *, name, comm=None):
    nb, d = c_all.shape
    n = w_ada.shape[1]
    tn = n // 2

    def body(c_ref, w_ref, b_ref, o_ref):
        cv = c_ref[...]
        cond = cv * _sigmoid(cv)
        o_ref[...] = jnp.dot(cond, w_ref[...], preferred_element_type=F32,
                             precision=lax.Precision.HIGHEST) + b_ref[...]

    return _pcall(
        body, name=name, grid=(n // tn,),
        in_specs=[_full((nb, d)), pl.BlockSpec((d, tn), lambda j: (0, j)), pl.BlockSpec((1, tn), lambda j: (0, j))],
        out_specs=pl.BlockSpec((nb, tn), lambda j: (0, j)),
        out_shape=jax.ShapeDtypeStruct((nb, n), F32), args=(c_all, w_ada, b_ada_shard), comm=comm)


def _small_finish(gathered, dmod_all, dmod_shard, c_all_t, *, name):
    d = D_MODEL
    nb, n = dmod_shard.shape

    def body(g_ref, dm_ref, dms_ref, ct_ref, sum_ref, gw_ref, gb_ref):
        total = g_ref[0]
        for dev in range(1, N_DEV):
            total = total + g_ref[dev]
        sum_ref[...] = total
        gb_ref[...] = _row_sum(dm_ref[...])
        ctv = ct_ref[...]
        cond_t = ctv * _sigmoid(ctv)
        for jb in range(n // COL_CHUNK):
            gw_ref[:, jb * COL_CHUNK:(jb + 1) * COL_CHUNK] = jnp.dot(
                cond_t, dms_ref[:, jb * COL_CHUNK:(jb + 1) * COL_CHUNK], preferred_element_type=F32,
                precision=lax.Precision.HIGHEST)

    return pl.pallas_call(
        body, name=name, grid=(1,),
        in_specs=[_full((N_DEV, SMALL_ROWS, d)), _full((nb, N_MOD * d)), _full((nb, n)), _full((d, nb))],
        out_specs=[_full((SMALL_ROWS, d)), _full((d, n)), _full((1, N_MOD * d))],
        out_shape=[jax.ShapeDtypeStruct((SMALL_ROWS, d), F32), jax.ShapeDtypeStruct((d, n), F32),
                   jax.ShapeDtypeStruct((1, N_MOD * d), F32)],
        compiler_params=_params(("arbitrary",)),
    )(gathered, dmod_all, dmod_shard, c_all_t)


def _row_tile(r, c, budget=1 << 21):
    if r * c * 4 <= budget or r % 16:
        return r
    best = 16
    for tr in range(16, r + 1, 16):
        if r % tr == 0 and tr * c * 4 <= budget:
            best = tr
    return best


def _cast_into(w, chip, col_kind, *, name):
    r, c = w.shape
    tr = _row_tile(r, c)

    def body(chip_ref, w_ref, o_ref):
        o_ref[...] = w_ref[...].astype(BF16)

    if col_kind:
        out_spec = pl.BlockSpec((tr, c), lambda i, chip_ref: (i, chip_ref[0]))
        out_shape = jax.ShapeDtypeStruct((r, c * N_CHIPS), BF16)
    else:
        out_spec = pl.BlockSpec((tr, c), lambda i, chip_ref: (chip_ref[0] * (r // tr) + i, 0))
        out_shape = jax.ShapeDtypeStruct((r * N_CHIPS, c), BF16)
    return _pcall(body, name=name, grid=(r // tr,), in_specs=[pl.BlockSpec((tr, c), lambda i, chip_ref: (i, 0))],
                  out_specs=out_spec, out_shape=out_shape, args=(w,), prefetch=chip)


def _adamw(w, g, m, v, *, name, comm=None):
    r, c = w.shape
    tr = _row_tile(r, c)
    c1 = 1.0 - ADAM_B1 ** ADAM_STEP
    c2 = 1.0 - ADAM_B2 ** ADAM_STEP

    def body(w_ref, g_ref, m_ref, v_ref, d_ref, nm_ref, nv_ref):
        gv = g_ref[...]
        m2 = ADAM_B1 * m_ref[...] + (1.0 - ADAM_B1) * gv
        v2 = ADAM_B2 * v_ref[...] + (1.0 - ADAM_B2) * (gv * gv)
        d_ref[...] = -ADAM_LR * ((m2 / c1) / (jnp.sqrt(v2 / c2) + ADAM_EPS) + ADAM_WD * w_ref[...])
        nm_ref[...] = m2
        nv_ref[...] = v2

    spec = pl.BlockSpec((tr, c), lambda i: (i, 0))
    sh = jax.ShapeDtypeStruct((r, c), F32)
    return _pcall(body, name=name, grid=(r // tr,), in_specs=[spec] * 4, out_specs=[spec] * 3, out_shape=[sh] * 3,
                  args=(w, g, m, v), comm=comm)


def _sum_pair(pos, g3, r3, blk_of, *, name, comm=None):
    n, rows, cols = r3.shape
    tr = _row_tile(rows, cols)

    def body(pos_ref, g_ref, r_ref, s32_ref, s16_ref):
        s = g_ref[0] + r_ref[0]
        s32_ref[0] = s
        s16_ref[0] = s.astype(BF16)

    own = pl.BlockSpec((1, tr, cols), lambda p, i, pos: (blk_of(p, pos), i, 0))
    plain = pl.BlockSpec((1, tr, cols), lambda p, i, pos: (p, i, 0))
    return _pcall(
        body, name=name, grid=(n, rows // tr), in_specs=[own, plain], out_specs=[plain, plain],
        out_shape=[jax.ShapeDtypeStruct((n, rows, cols), F32), jax.ShapeDtypeStruct((n, rows, cols), BF16)],
        args=(g3, r3), prefetch=pos, comm=comm)


def _sum_final(pos, s32, recv, *, col_kind, n_shard, name, comm=None):
    if col_kind:
        rows, cols = s32.shape[1], n_shard
        own = lambda tr: pl.BlockSpec((1, tr, cols), lambda i, pos: (0, i, 2 * pos[0] + pos[1]))
    else:
        rows, cols = s32.shape[1], s32.shape[2]
        own = lambda tr: pl.BlockSpec((1, tr, cols), lambda i, pos: (2 * pos[0] + pos[1], i, 0))
    tr = _row_tile(rows, cols)

    def body(pos_ref, s_ref, r_ref, o_ref):
        o_ref[0] = ((s_ref[0] + r_ref[0].astype(F32)) + r_ref[1].astype(F32)) + r_ref[2].astype(F32)

    return _pcall(
        body, name=name, grid=(rows // tr,),
        in_specs=[own(tr), pl.BlockSpec((3, tr, cols), lambda i, pos: (0, i, 0))],
        out_specs=pl.BlockSpec((1, tr, cols), lambda i, pos: (pos[2], i, 0)),
        out_shape=jax.ShapeDtypeStruct((2, rows, cols), F32), args=(s32, recv), prefetch=pos, comm=comm)


def _position():
    return lax.axis_index("x"), lax.axis_index("y"), lax.axis_index("c")


def _allgather8(x_shard, *, name, comm=None):
    m_per, n = x_shard.shape
    nci, nco = (0, 0) if comm is None else (len(comm.inputs), len(comm.out_shapes))

    def body(*refs):
        x_ref, refs = refs[0], refs[1:]
        cin, refs = refs[:nci], refs[nci:]
        out_ref, refs = refs[0], refs[1:]
        cout, refs = refs[:nco], refs[nco:]
        (send_sems, recv_sems, local_sem), csems = refs[:3], refs[3:]
        x, y, c = _position()
        me, sibling = (x, y, c), (x, y, 1 - c)
        chips = [(1 - x, y), (x, 1 - y), (1 - x, 1 - y)]

        def rows(px, py, pc):
            return out_ref.at[pl.ds((4 * px + 2 * py + pc) * m_per, m_per), :]

        def copy(k, block, to, src=None):
            return pltpu.make_async_remote_copy(
                src_ref=rows(*block) if src is None else src, dst_ref=rows(*block),
                send_sem=send_sems.at[k], recv_sem=recv_sems.at[k], device_id=to, device_id_type=MESH)

        mine = pltpu.make_async_copy(x_ref, rows(*me), local_sem)
        mine.start()
        first = [copy(0, me, sibling, src=x_ref)]
        first += [copy(1 + j, me, (*chip, c), src=x_ref) for j, chip in enumerate(chips)]
        for cp in first:
            cp.start()
        if comm is not None:
            comm.start(cin, cout, csems)
        passed = [copy(4 + j, (*chip, c), sibling) for j, chip in enumerate(chips)]
        for j, chip in enumerate(chips):
            copy(1 + j, (*chip, c), me).wait_recv()
            passed[j].start()
        copy(0, sibling, me).wait_recv()
        for j, chip in enumerate(chips):
            copy(4 + j, (*chip, 1 - c), me).wait_recv()
        for cp in first + passed:
            cp.wait_send()
        mine.wait()
        if comm is not None:
            comm.middle(cin, cout, csems)
            comm.finish(cin, cout, csems)

    vmem = pl.BlockSpec(memory_space=pltpu.VMEM)
    sems = [pltpu.SemaphoreType.DMA((7,)), pltpu.SemaphoreType.DMA((7,)), pltpu.SemaphoreType.DMA]
    out = jax.ShapeDtypeStruct((N_DEV * m_per, n), x_shard.dtype)
    if comm is None:
        return pl.pallas_call(body, name=name, out_shape=out, in_specs=[vmem], out_specs=vmem,
                              scratch_shapes=sems)(x_shard)
    res = pl.pallas_call(
        body, name=name, out_shape=[out] + list(comm.out_shapes), in_specs=[vmem] + [ANY_SPEC] * nci,
        out_specs=[vmem] + [ANY_SPEC] * nco, scratch_shapes=sems + list(comm.sems),
        input_output_aliases={1 + i: 1 + o for i, o in comm.aliases.items()})(x_shard, *comm.inputs)
    return res[0], list(res[1:])


def _peer_chips(x, y):
    return [(1 - x, y), (x, 1 - y), (1 - x, 1 - y)]


class _GatherJob:
    def __init__(self, pieces):
        self.pieces = pieces
        n_p = len(pieces)
        self.inputs = [p[0] for p in pieces]
        self.out_shapes = [jax.ShapeDtypeStruct(p[0].shape, p[0].dtype) for p in pieces]
        for buf, col_kind, r0, nr in pieces:
            half_rows = buf.shape[0] // (2 if col_kind else 2 * N_CHIPS)
            assert r0 % 16 == 0 and nr % 16 == 0 and r0 + nr <= half_rows, (buf.shape, r0, nr)
        self.aliases = {p: p for p in range(n_p)}
        self.sems = [pltpu.SemaphoreType.DMA((3 * n_p,))] * 4

    def _region(self, cout, p, chip_idx, half):
        buf, col_kind, r0, nr = self.pieces[p]
        if col_kind:
            n = buf.shape[1] // N_CHIPS
            return cout[p].at[pl.ds(half * (buf.shape[0] // 2) + r0, nr), pl.ds(chip_idx * n, n)]
        n = buf.shape[0] // N_CHIPS
        return cout[p].at[pl.ds(chip_idx * n + half * (n // 2) + r0, nr), :]

    def _copies(self, cout, sems):
        send_sems, recv_sems, fsend_sems, frecv_sems = sems
        x, y, c = _position()
        k = 2 * x + y
        sibling = (x, y, 1 - c)
        sends, arrivals, fwds, fwd_arrivals = [], [], [], []

        def remote(region, ssem, rsem, to):
            return pltpu.make_async_remote_copy(src_ref=region, dst_ref=region, send_sem=ssem, recv_sem=rsem,
                                                device_id=to, device_id_type=MESH)

        for p in range(len(self.pieces)):
            for j, chip in enumerate(_peer_chips(x, y)):
                idx = 3 * p + j
                theirs = 2 * chip[0] + chip[1]
                sends.append(remote(self._region(cout, p, k, c), send_sems.at[idx], recv_sems.at[idx], (*chip, c)))
                arrivals.append(remote(self._region(cout, p, theirs, c), send_sems.at[idx], recv_sems.at[idx],
                                       (*chip, c)))
                fwds.append(remote(self._region(cout, p, theirs, c), fsend_sems.at[idx], frecv_sems.at[idx], sibling))
                fwd_arrivals.append(remote(self._region(cout, p, theirs, 1 - c), fsend_sems.at[idx],
                                           frecv_sems.at[idx], sibling))
        return sends, arrivals, fwds, fwd_arrivals

    def start(self, cin, cout, sems):
        for cp in self._copies(cout, sems)[0]:
            cp.start()

    def middle(self, cin, cout, sems):
        _, arrivals, fwds, _ = self._copies(cout, sems)
        for arrived, fw in zip(arrivals, fwds):
            arrived.wait_recv()
            fw.start()

    def finish(self, cin, cout, sems):
        sends, _, fwds, fwd_arrivals = self._copies(cout, sems)
        for arrived in fwd_arrivals:
            arrived.wait_recv()
        for cp in sends + fwds:
            cp.wait_send()


class _PairedJob:
    aliases = {}

    def start(self, cin, cout, sems):
        for cp in self._copies(cin, cout, sems):
            cp.start()

    def middle(self, cin, cout, sems):
        pass

    def finish(self, cin, cout, sems):
        copies = self._copies(cin, cout, sems)
        for cp in copies:
            cp.wait_recv()
        for cp in copies:
            cp.wait_send()


class _SwapJob(_PairedJob):
    def __init__(self, grads, kinds):
        self.inputs, self.kinds = list(grads), list(kinds)
        self.out_shapes, self.n_copies = [], []
        for g, kd in zip(grads, kinds):
            if kd:
                self.out_shapes.append(jax.ShapeDtypeStruct((1, g.shape[0] // 2, g.shape[1]), g.dtype))
                self.n_copies.append(1)
            else:
                n = g.shape[0] // N_CHIPS
                self.out_shapes.append(jax.ShapeDtypeStruct((N_CHIPS, n // 2, g.shape[1]), g.dtype))
                self.n_copies.append(N_CHIPS)
        total = sum(self.n_copies)
        self.sems = [pltpu.SemaphoreType.DMA((total,)), pltpu.SemaphoreType.DMA((total,))]

    def _copies(self, cin, cout, sems):
        send_sems, recv_sems = sems
        x, y, c = _position()
        copies = []
        for p, src_ref in enumerate(cin):
            for kk in range(self.n_copies[p]):
                if self.kinds[p]:
                    hr = src_ref.shape[0] // 2
                    src = src_ref.at[pl.ds((1 - c) * hr, hr), :]
                else:
                    n = src_ref.shape[0] // N_CHIPS
                    src = src_ref.at[pl.ds(kk * n + (1 - c) * (n // 2), n // 2), :]
                idx = len(copies)
                copies.append(pltpu.make_async_remote_copy(
                    src_ref=src, dst_ref=cout[p].at[kk], send_sem=send_sems.at[idx], recv_sem=recv_sems.at[idx],
                    device_id=(x, y, 1 - c), device_id_type=MESH))
        return copies


class _ExchangeJob(_PairedJob):
    def __init__(self, s16, kinds, sizes):
        self.inputs, self.kinds, self.sizes = list(s16), list(kinds), list(sizes)
        self.out_shapes = [jax.ShapeDtypeStruct((3, s.shape[1], n if kd else s.shape[2]), s.dtype)
                           for s, kd, n in zip(s16, kinds, sizes)]
        self.sems = [pltpu.SemaphoreType.DMA((3 * len(s16),)), pltpu.SemaphoreType.DMA((3 * len(s16),))]

    def _copies(self, cin, cout, sems):
        send_sems, recv_sems = sems
        x, y, c = _position()
        copies = []
        for p, src_ref in enumerate(cin):
            for j, chip in enumerate(_peer_chips(x, y)):
                kk = 2 * chip[0] + chip[1]
                n = self.sizes[p]
                src = src_ref.at[0, :, pl.ds(kk * n, n)] if self.kinds[p] else src_ref.at[kk]
                copies.append(pltpu.make_async_remote_copy(
                    src_ref=src, dst_ref=cout[p].at[j], send_sem=send_sems.at[3 * p + j],
                    recv_sem=recv_sems.at[3 * p + j], device_id=(*chip, c), device_id_type=MESH))
        return copies


class _ShareJob:
    def __init__(self, halves):
        self.inputs = list(halves)
        self.out_shapes = [jax.ShapeDtypeStruct(h.shape, h.dtype) for h in halves]
        self.aliases = {p: p for p in range(len(halves))}
        self.sems = [pltpu.SemaphoreType.DMA((len(halves),)), pltpu.SemaphoreType.DMA((len(halves),))]

    def _copies(self, cout, sems, half):
        send_sems, recv_sems = sems
        x, y, c = _position()
        h = c if half == "mine" else 1 - c
        return [pltpu.make_async_remote_copy(
            src_ref=o.at[h], dst_ref=o.at[h], send_sem=send_sems.at[p], recv_sem=recv_sems.at[p],
            device_id=(x, y, 1 - c), device_id_type=MESH) for p, o in enumerate(cout)]

    def start(self, cin, cout, sems):
        for cp in self._copies(cout, sems, "mine"):
            cp.start()

    def middle(self, cin, cout, sems):
        pass

    def finish(self, cin, cout, sems):
        for cp in self._copies(cout, sems, "theirs"):
            cp.wait_recv()
        for cp in self._copies(cout, sems, "mine"):
            cp.wait_send()


class _MultiJob:
    def __init__(self, jobs):
        self.jobs = jobs
        self.inputs = [a for j in jobs for a in j.inputs]
        self.out_shapes = [s for j in jobs for s in j.out_shapes]
        self.sems = [s for j in jobs for s in j.sems]
        self.aliases = {}
        i0 = o0 = 0
        for j in jobs:
            for i, o in j.aliases.items():
                self.aliases[i0 + i] = o0 + o
            i0 += len(j.inputs)
            o0 += len(j.out_shapes)

    def _parts(self, cin, cout, sems):
        i0 = o0 = s0 = 0
        for j in self.jobs:
            ni, no, ns = len(j.inputs), len(j.out_shapes), len(j.sems)
            yield j, cin[i0:i0 + ni], cout[o0:o0 + no], sems[s0:s0 + ns]
            i0, o0, s0 = i0 + ni, o0 + no, s0 + ns

    def start(self, cin, cout, sems):
        for j, a, b, s in self._parts(cin, cout, sems):
            j.start(a, b, s)

    def middle(self, cin, cout, sems):
        for j, a, b, s in self._parts(cin, cout, sems):
            j.middle(a, b, s)

    def finish(self, cin, cout, sems):
        for j, a, b, s in self._parts(cin, cout, sems):
            j.finish(a, b, s)


def _rope_tables(positions):
    half = ROT_DIM // 2
    inv_freq = jnp.power(jnp.float32(ROPE_THETA), -jnp.arange(0, ROT_DIM, 2, dtype=F32) / ROT_DIM)
    inv_head = jnp.concatenate([inv_freq, inv_freq, jnp.zeros((HEAD_DIM - ROT_DIM,), F32)])
    inv_lane = jnp.concatenate([inv_head] * (LANE // HEAD_DIM))
    ang = positions.astype(F32).reshape(-1)[:, None] * inv_lane[None, :]
    sin = jnp.sin(ang)
    dim = jnp.arange(LANE) % HEAD_DIM
    return jnp.cos(ang), jnp.where(dim < half, -sin, 0.0), jnp.where(dim >= half, sin, 0.0)


def kernel(x, c, positions, w_ada, b_ada, ffn1_w_gate_up, ffn1_w_down, ln1_g, ln1_b, w_in, conv_w, attn_sinks, w_out, ln2_g, ln2_b, ffn2_w_gate_up, ffn2_w_down, ln3_g, ln3_b, loss_target, m_w_ada, m_b_ada, m_ffn1_w_gate_up, m_ffn1_w_down, m_ln1_g, m_ln1_b, m_w_in, m_conv_w, m_attn_sinks, m_w_out, m_ln2_g, m_ln2_b, m_ffn2_w_gate_up, m_ffn2_w_down, m_ln3_g, m_ln3_b, v_w_ada, v_b_ada, v_ffn1_w_gate_up, v_ffn1_w_down, v_ln1_g, v_ln1_b, v_w_in, v_conv_w, v_attn_sinks, v_w_out, v_ln2_g, v_ln2_b, v_ffn2_w_gate_up, v_ffn2_w_down, v_ln3_g, v_ln3_b):
    d = D_MODEL
    nb, seq, _ = x.shape
    t = nb * seq
    f = ffn1_w_down.shape[1] * N_CHIPS
    ax, ay, ac = _position()
    chip = 2 * ax + ay
    dev = 2 * chip + ac
    pos = jnp.stack([ax, ay, ac]).astype(jnp.int32)

    x2 = x.reshape(t, d)
    tgt2 = loss_target.reshape(t, d)
    ln1 = jnp.concatenate([ln1_g, ln1_b], axis=0)
    ln2 = jnp.concatenate([ln2_g, ln2_b], axis=0)
    ln3 = jnp.concatenate([ln3_g, ln3_b], axis=0)
    sinks = attn_sinks.reshape(N_Q_HEADS)
    cos_t, sa_t, sb_t = _rope_tables(positions)

    gu_cuts = [0, 176, 352, d // 2]
    gu_part = lambda buf, s: (buf, True, gu_cuts[s], gu_cuts[s + 1] - gu_cuts[s])
    chip_arr = jnp.reshape(chip, (1,)).astype(jnp.int32)
    b_gu1 = _cast_into(ffn1_w_gate_up[0], chip_arr, True, name="cast_gu1")

    n_ada = w_ada.shape[2]
    c_all, (b_gu1,) = _allgather8(c.reshape(nb * d // LANE, LANE), name="gather_c", comm=_GatherJob([gu_part(b_gu1, 0)]))
    c_all = c_all.reshape(N_DEV * nb, d)
    b_shard = lax.dynamic_slice(b_ada, (0, chip * n_ada), (1, n_ada))
    mod_part, (b_gu1,) = _ada_fwd(c_all, w_ada[0], b_shard, name="ada_fwd", comm=_GatherJob([gu_part(b_gu1, 1)]))
    conv_rows = jnp.pad(conv_w[0], ((0, 5), (0, n_ada - conv_w.shape[2])))
    part = jnp.concatenate([mod_part, conv_rows], axis=0)
    parts, (wgu1,) = _allgather8(part, name="gather_mod", comm=_GatherJob([gu_part(b_gu1, 2)]))
    parts = parts.reshape(N_DEV, N_DEV * nb + 8, n_ada)
    mod_all = jnp.concatenate([parts[2 * k, :N_DEV * nb, :] for k in range(N_CHIPS)], axis=1)
    mod = lax.dynamic_slice(mod_all, (dev * nb, 0), (nb, N_MOD * d)).reshape(nb, N_MOD, d)
    cw_full = jnp.concatenate([parts[2 * k, N_DEV * nb:, :conv_w.shape[2]] for k in range(N_CHIPS)], axis=1)

    b_d1 = _cast_into(ffn1_w_down[0], chip_arr, False, name="cast_d1")
    b_in = _cast_into(w_in[0].T, chip_arr, False, name="cast_in")
    b_out = _cast_into(w_out[0], chip_arr, False, name="cast_out")
    b_gu2 = _cast_into(ffn2_w_gate_up[0], chip_arr, True, name="cast_gu2")
    b_d2 = _cast_into(ffn2_w_down[0], chip_arr, False, name="cast_d2")
    n_gu, n_d, n_in, n_out = (ffn1_w_gate_up.shape[2], ffn1_w_down.shape[1], w_in.shape[2], w_out.shape[1])

    def whole(buf, col_kind):
        return (buf, col_kind, 0, buf.shape[0] // (2 if col_kind else 2 * N_CHIPS))

    (h1, a1, gu1), (wd1, wout) = _ffn_up(x2, ln1, mod, wgu1, seq=seq, sc_idx=1, sh_idx=0, use_ln=False,
                                         name="ffn1_up", comm=_GatherJob([whole(b_d1, False), whole(b_out, False)]))
    (f1, xhat1, rstd1), (win_t,) = _ffn_down_ln(a1, wd1, x2, ln1, mod, seq=seq, gate_idx=2, use_ln=False,
                                                name="ffn1_down", comm=_GatherJob([whole(b_in, False)]))
    (h2, q, k, v, ubc), (b_gu2,) = _in_proj(
        xhat1, ln1, mod, win_t, cos_t, sa_t, sb_t, seq=seq, sc_idx=4, sh_idx=3, name="in_proj",
        comm=_GatherJob([gu_part(b_gu2, 0)]))
    attn, (b_gu2,) = _attention(q, k, v, sinks, seq=seq, name="attention", comm=_GatherJob([gu_part(b_gu2, 1)]))
    (mixin, mix, xhat2, rstd2), (wgu2,) = _out_proj(
        attn, ubc, cw_full, wout, xhat1, ln1, mod, seq=seq, gate_idx=5, name="out_proj",
        comm=_GatherJob([gu_part(b_gu2, 2)]))
    (h3, a3, gu3), (wd2,) = _ffn_up(xhat2, ln2, mod, wgu2, seq=seq, sc_idx=7, sh_idx=6, use_ln=True, name="ffn2_up",
                                    comm=_GatherJob([whole(b_d2, False)]))
    dr3, df3, loss_cols, dln3g, dln3b, dgate3 = _ffn_down_loss(
        a3, wd2, xhat2, ln2, mod, ln3, tgt2, seq=seq, gate_idx=8, name="ffn2_down_loss")

    def pair_sum(g, r3, col_kind, name_, comm=None):
        if col_kind:
            g3 = g.reshape(2, g.shape[0] // 2, g.shape[1])
            blk_of = lambda p_, pos_: pos_[2]
        else:
            g3 = g.reshape(2 * N_CHIPS, g.shape[0] // (2 * N_CHIPS), g.shape[1])
            blk_of = lambda p_, pos_: 2 * p_ + pos_[2]
        return _sum_pair(pos, g3, r3, blk_of, name=name_, comm=comm)

    dgu3 = _ffn_bwd_act(df3, wd2, gu3, seq=seq, name="ffn2_bwd_act")
    g_wd2 = _matmul_tn(a3, df3, tmm=f // 2, tnn=d, name="grad_wd2")
    (s32_gu2, s16_gu2), (sib_d2,) = _grad_chip_sum(pos, h3, dgu3, name="grad_wgu2", comm=_SwapJob([g_wd2], [False]))
    s32_d2, s16_d2 = pair_sum(g_wd2, sib_d2, False, "sum_pair_d2")
    (dr2, dmix, dsc3, dsh3, dgate2, dln2g, dln2b), (recv_gu2,) = _bwd_in(
        dgu3, wgu2, dr3, xhat2, rstd2, ln2, mod, mix, seq=seq, w_is_nt=True, sc_idx=7, gate_idx=5,
        branch_scale=1.0, final=False, name="ffn2_bwd_in", comm=_ExchangeJob([s16_gu2], [True], [n_gu]))
    g_wout = _matmul_tn(mixin, dmix, tmm=d, tnn=d, name="grad_wout")
    dmixin = _matmul_nt_bf16(dmix, wout, seq=seq, name="out_proj_bwd")
    (dq, dkp, dkc, dvp, dvc, dsink), (recv_d2, sib_out) = _attention_bwd(
        q, k, v, dmixin, sinks, seq=seq, name="attention_bwd",
        comm=_MultiJob([_ExchangeJob([s16_d2], [False], [n_d]), _SwapJob([g_wout], [False])]))
    s32_out, s16_out = pair_sum(g_wout, sib_out, False, "sum_pair_out")
    (dproj, dcw), (recv_out,) = _mix_bwd_assemble(
        dq, dkp, dkc, dvp, dvc, cos_t, sa_t, sb_t, dmixin, ubc, cw_full, seq=seq, name="mix_bwd",
        comm=_ExchangeJob([s16_out], [False], [n_out]))
    g_win_t = _matmul_tn(dproj, h2, tmm=IN_WIDTH // 2, tnn=d, name="grad_win")
    (dr1, df1, dsc2, dsh2, dgate1, dln1g, dln1b), (sib_in,) = _bwd_in(
        dproj, win_t, dr2, xhat1, rstd1, ln1, mod, f1, seq=seq, w_is_nt=False, sc_idx=4, gate_idx=2,
        branch_scale=0.5, final=False, name="in_proj_bwd", comm=_SwapJob([g_win_t], [False]))
    s32_in, s16_in = pair_sum(g_win_t, sib_in, False, "sum_pair_in")
    g_wd1, (recv_in,) = _matmul_tn(a1, df1, tmm=f // 2, tnn=d, name="grad_wd1",
                                   comm=_ExchangeJob([s16_in], [False], [n_in]))
    dgu1, (sib_d1,) = _ffn_bwd_act(df1, wd1, gu1, seq=seq, name="ffn1_bwd_act", comm=_SwapJob([g_wd1], [False]))
    s32_d1, s16_d1 = pair_sum(g_wd1, sib_d1, False, "sum_pair_d1")
    (s32_gu1, s16_gu1), (recv_d1,) = _grad_chip_sum(pos, h1, dgu1, name="grad_wgu1",
                                                    comm=_ExchangeJob([s16_d1], [False], [n_d]))

    def final_half(s32_, recv_, col_kind, n_shard, name_):
        return _sum_final(pos, s32_, recv_, col_kind=col_kind, n_shard=n_shard, name=name_)

    early = [final_half(s32_gu2, recv_gu2, True, n_gu, "sum_final_gu2"),
             final_half(s32_d2, recv_d2, False, n_d, "sum_final_d2"),
             final_half(s32_out, recv_out, False, n_out, "sum_final_out"),
             final_half(s32_in, recv_in, False, n_in, "sum_final_in"),
             final_half(s32_d1, recv_d1, False, n_d, "sum_final_d1")]
    (grad_x, dsc1, dsh1), (recv_gu1, full_gu2, full_d2, full_out, full_in, full_d1) = _bwd_in(
        dgu1, wgu1, dr1, x2, None, None, mod, None, seq=seq, w_is_nt=True, sc_idx=1, gate_idx=None,
        branch_scale=None, final=True, name="ffn1_bwd_in",
        comm=_MultiJob([_ExchangeJob([s16_gu1], [True], [n_gu]), _ShareJob(early)]))
    late = [final_half(s32_gu1, recv_gu1, True, n_gu, "sum_final_gu1")]

    dmod = jnp.concatenate([dsh1, dsc1, dgate1, dsh2, dsc2, dgate2, dsh3, dsc3, dgate3], axis=1)
    loss_row = jnp.sum(loss_cols, axis=1, keepdims=True) * (0.5 / d)
    lane_row = lambda a: jnp.pad(a, ((0, 0), (0, d - a.shape[1])))
    block = jnp.concatenate(
        [dmod.reshape(nb * N_MOD, d), dln1g, dln1b, dln2g, dln2b, dln3g, dln3b,
         lane_row(dcw[0:3, :]), lane_row(dsink[:, 0:1].reshape(1, N_Q_HEADS)), lane_row(loss_row)], axis=0)
    block = jnp.pad(block, ((0, SMALL_ROWS - block.shape[0]), (0, 0)))
    gathered, (full_gu1,) = _allgather8(block, name="gather_small", comm=_ShareJob(late))
    gathered = gathered.reshape(N_DEV, SMALL_ROWS, d)
    dmod_all = gathered[:, :nb * N_MOD, :].reshape(N_DEV * nb, N_MOD * d)
    dmod_shard = lax.dynamic_slice(dmod_all, (0, chip * n_ada), (N_DEV * nb, n_ada))
    small, g_w_ada, g_b_ada = _small_finish(gathered, dmod_all, dmod_shard, c_all.T, name="small_finish")
    r0 = nb * N_MOD
    loss = small[r0 + 10, 0]
    g_ln = [small[r0 + i:r0 + i + 1, :] for i in range(6)]
    g_cw_full = small[r0 + 6:r0 + 9, :CONV_WIDTH]
    g_conv = lax.dynamic_slice(g_cw_full, (0, chip * conv_w.shape[2]), (3, conv_w.shape[2]))
    g_sinks = small[r0 + 9:r0 + 10, :N_Q_HEADS]

    def flat2(a):
        return a.reshape(-1, a.shape[-1])

    def unhalve(a):
        return a.reshape(2 * a.shape[1], a.shape[2])

    results = {}

    def adamw(name_, w_, g_, m_, v_):
        g2 = flat2(g_)
        dl, nm, nv = _adamw(flat2(w_), g2, flat2(m_), flat2(v_), name="adamw_" + name_)
        results[name_] = tuple(a.reshape(w_.shape) for a in (g2, dl, nm, nv))

    adamw("w_ada", w_ada, g_w_ada, m_w_ada, v_w_ada)
    adamw("ffn2_w_gate_up", ffn2_w_gate_up, unhalve(full_gu2), m_ffn2_w_gate_up, v_ffn2_w_gate_up)
    adamw("ffn2_w_down", ffn2_w_down, unhalve(full_d2), m_ffn2_w_down, v_ffn2_w_down)
    adamw("w_out", w_out, unhalve(full_out), m_w_out, v_w_out)
    adamw("w_in", w_in, unhalve(full_in).T, m_w_in, v_w_in)
    adamw("ffn1_w_gate_up", ffn1_w_gate_up, unhalve(full_gu1), m_ffn1_w_gate_up, v_ffn1_w_gate_up)
    adamw("ffn1_w_down", ffn1_w_down, unhalve(full_d1), m_ffn1_w_down, v_ffn1_w_down)
    adamw("b_ada", b_ada, g_b_ada, m_b_ada, v_b_ada)
    adamw("ln1_g", ln1_g, g_ln[0], m_ln1_g, v_ln1_g)
    adamw("ln1_b", ln1_b, g_ln[1], m_ln1_b, v_ln1_b)
    adamw("ln2_g", ln2_g, g_ln[2], m_ln2_g, v_ln2_g)
    adamw("ln2_b", ln2_b, g_ln[3], m_ln2_b, v_ln2_b)
    adamw("ln3_g", ln3_g, g_ln[4], m_ln3_g, v_ln3_g)
    adamw("ln3_b", ln3_b, g_ln[5], m_ln3_b, v_ln3_b)
    adamw("conv_w", conv_w, g_conv, m_conv_w, v_conv_w)
    adamw("attn_sinks", attn_sinks, g_sinks, m_attn_sinks, v_attn_sinks)
    order = ["w_ada", "b_ada", "ffn1_w_gate_up", "ffn1_w_down", "ln1_g", "ln1_b", "w_in", "conv_w", "attn_sinks",
             "w_out", "ln2_g", "ln2_b", "ffn2_w_gate_up", "ffn2_w_down", "ln3_g", "ln3_b"]
    return (loss, grad_x.reshape(x.shape), *[results[n_][0] for n_ in order], *[results[n_][1] for n_ in order],
            *[results[n_][2] for n_ in order], *[results[n_][3] for n_ in order])
```

```python
import jax
import jax.numpy as jnp
from jax import lax
from jax.experimental import pallas as pl
from jax.experimental.pallas import tpu as pltpu

F32 = jnp.float32
BF16 = jnp.bfloat16
MESH = pl.DeviceIdType.MESH

D_MODEL = 1024
HEAD_DIM = 64
ATTN_WIDTH = 512
CONV_WIDTH = 512
N_Q_HEADS = 8
N_KV_HEADS = 2
GQA_GROUP = 4
KV_WIDTH = 128
WINDOW = 128
BLOCK = 128
ROT_DIM = 16
ROPE_THETA = 500000.0
N_MOD = 9
LN_EPS = 1e-5
DN_ALPHA = 2.0 ** 0.25
IN_WIDTH = 2304
N_CHIPS = 4
N_DEV = 8
SMALL_ROWS = 32

ADAM_LR = 0.001
ADAM_B1 = 0.9
ADAM_B2 = 0.999
ADAM_EPS = 1e-08
ADAM_WD = 0.01
ADAM_STEP = 10

LANE = 128
HALO = 16
COL_CHUNK = 256
VMEM_LIMIT = 56 * 1024 * 1024


def _params(sem=None, vmem=True):
    return pltpu.CompilerParams(dimension_semantics=sem, vmem_limit_bytes=VMEM_LIMIT if vmem else None)


def _sigmoid(g):
    return 0.5 * jnp.tanh(0.5 * g) + 0.5


def _row_sum(v):
    return jnp.sum(v, axis=0, keepdims=True)


ROW_CHUNK = 16
EPILOGUE_UNROLL = 8


def _fold8(v):
    return v[0:8, :] + v[8:16, :]


def _row_chunk_loop(n_rows, step, init):
    per_iter = ROW_CHUNK * EPILOGUE_UNROLL
    assert n_rows % per_iter == 0, n_rows

    def body(it, carry):
        for s in range(EPILOGUE_UNROLL):
            start = pl.multiple_of(it * per_iter + s * ROW_CHUNK, ROW_CHUNK)
            carry = step(pl.ds(start, ROW_CHUNK), carry)
        return carry

    return lax.fori_loop(0, n_rows // per_iter, body, init)


def _ln_stats(r):
    mu = jnp.mean(r, axis=-1, keepdims=True)
    rc = r - mu
    var = jnp.mean(rc * rc, axis=-1, keepdims=True)
    rstd = lax.rsqrt(var + LN_EPS)
    return rc * rstd, rstd


def _ln_bwd(dxo, xhat, rstd, g):
    dxhat = dxo * g
    m1 = jnp.mean(dxhat, axis=-1, keepdims=True)
    m2 = jnp.mean(dxhat * xhat, axis=-1, keepdims=True)
    return rstd * (dxhat - m1 - xhat * m2)


def _dot_nt(a, b):
    return lax.dot_general(a, b, (((1,), (1,)), ((), ())), preferred_element_type=F32)


def _dot_tn(a, b):
    return lax.dot_general(a, b, (((0,), (0,)), ((), ())), preferred_element_type=F32)


def _full(shape):
    nd = len(shape)
    return pl.BlockSpec(shape, lambda *_: (0,) * nd)


def _resident(shape):
    nd = len(shape)
    return pl.BlockSpec(shape, lambda *_: (0,) * nd, pipeline_mode=pl.Buffered(1))


ANY_SPEC = pl.BlockSpec(memory_space=pl.ANY)


def _pcall(body, *, name, grid, in_specs, out_specs, out_shape, args, scratch_shapes=(), comm=None, prefetch=None):
    single = not isinstance(out_shape, (list, tuple))
    out_specs = [out_specs] if single else list(out_specs)
    out_shape = [out_shape] if single else list(out_shape)
    in_specs = list(in_specs)
    scratch_shapes = list(scratch_shapes)
    sem = ("arbitrary",) * len(grid)
    n_pre = 0 if prefetch is None else 1
    pre_args = () if prefetch is None else (prefetch,)

    def call(fn, ins_, outs_, shapes_, scratch_, aliases_, operands):
        if prefetch is None:
            return pl.pallas_call(fn, name=name, grid=grid, in_specs=ins_, out_specs=outs_, out_shape=shapes_,
                                  scratch_shapes=scratch_, input_output_aliases=aliases_,
                                  compiler_params=_params(sem))(*operands)
        spec = pltpu.PrefetchScalarGridSpec(num_scalar_prefetch=1, grid=grid, in_specs=ins_, out_specs=outs_,
                                            scratch_shapes=scratch_)
        return pl.pallas_call(fn, name=name, grid_spec=spec, out_shape=shapes_,
                              input_output_aliases={n_pre + i: o for i, o in aliases_.items()},
                              compiler_params=_params(sem))(*pre_args, *operands)

    if comm is None:
        res = call(body, in_specs, out_specs, out_shape, scratch_shapes, {}, args)
        return res[0] if single else res
    n_in, n_out, n_scr = len(in_specs), len(out_specs), len(scratch_shapes)
    nci, nco = len(comm.inputs), len(comm.out_shapes)
    n_steps = 1
    for g in grid:
        n_steps *= g
    staged = n_steps >= 4
    middle_step = n_steps - 1 - max(1, n_steps // 8)

    def wrapped(*refs):
        pre, refs = refs[:n_pre], refs[n_pre:]
        ins, refs = refs[:n_in], refs[n_in:]
        cin, refs = refs[:nci], refs[nci:]
        outs, refs = refs[:n_out], refs[n_out:]
        cout, refs = refs[:nco], refs[nco:]
        scr, csems = refs[:n_scr], refs[n_scr:]
        step = pl.program_id(0)
        for ax in range(1, len(grid)):
            step = step * grid[ax] + pl.program_id(ax)

        @pl.when(step == 0)
        def _():
            comm.start(cin, cout, csems)

        body(*pre, *ins, *outs, *scr)

        if staged:
            @pl.when(step == middle_step)
            def _():
                comm.middle(cin, cout, csems)

        @pl.when(step == n_steps - 1)
        def _():
            if not staged:
                comm.middle(cin, cout, csems)
            comm.finish(cin, cout, csems)

    res = call(wrapped, in_specs + [ANY_SPEC] * nci, out_specs + [ANY_SPEC] * nco,
               out_shape + list(comm.out_shapes), scratch_shapes + list(comm.sems),
               {n_in + i: n_out + o for i, o in comm.aliases.items()}, (*args, *comm.inputs))
    main = res[:n_out]
    return (main[0] if single else main), list(res[n_out:])


def _comm_call(job, *, name):
    nci, nco = len(job.inputs), len(job.out_shapes)

    def body(*refs):
        cin, refs = refs[:nci], refs[nci:]
        cout, csems = refs[:nco], refs[nco:]
        job.start(cin, cout, csems)
        job.middle(cin, cout, csems)
        job.finish(cin, cout, csems)

    return pl.pallas_call(
        body, name=name, out_shape=list(job.out_shapes), in_specs=[ANY_SPEC] * nci, out_specs=[ANY_SPEC] * nco,
        scratch_shapes=list(job.sems), input_output_aliases=dict(job.aliases))(*job.inputs)


def _ffn_up(xin, lnp, mod, w, *, seq, sc_idx, sh_idx, use_ln, name, comm=None):
    t, d = xin.shape
    f = w.shape[1] // 2
    tm = min(512, seq)
    tpb = seq // tm
    ch = min(COL_CHUNK, f)

    def body(x_ref, ln_ref, mod_ref, w_ref, h_ref, a_ref, gu_ref):
        x = x_ref[...]
        if use_ln:
            x = x * ln_ref[0:1, :] + ln_ref[1:2, :]
        h = x * (1.0 + mod_ref[0, sc_idx:sc_idx + 1, :]) + mod_ref[0, sh_idx:sh_idx + 1, :]
        hb = h.astype(BF16)
        h_ref[...] = hb
        for j in range(f // ch):
            g = jnp.dot(hb, w_ref[:, j * ch:(j + 1) * ch], preferred_element_type=F32)
            u = jnp.dot(hb, w_ref[:, f + j * ch:f + (j + 1) * ch], preferred_element_type=F32)
            s = _sigmoid(g)
            silu = g * s
            a_ref[:, j * ch:(j + 1) * ch] = (silu * u).astype(BF16)
            gu_ref[:, j * ch:(j + 1) * ch] = (u * (s + silu * (1.0 - s))).astype(BF16)
            gu_ref[:, f + j * ch:f + (j + 1) * ch] = silu.astype(BF16)

    return _pcall(
        body, name=name, grid=(t // tm,),
        in_specs=[pl.BlockSpec((tm, d), lambda i: (i, 0)), _full((2, d)),
                  pl.BlockSpec((1, N_MOD, d), lambda i: (i // tpb, 0, 0)), _resident((d, 2 * f))],
        out_specs=[pl.BlockSpec((tm, d), lambda i: (i, 0)), pl.BlockSpec((tm, f), lambda i: (i, 0)),
                   pl.BlockSpec((tm, 2 * f), lambda i: (i, 0))],
        out_shape=[jax.ShapeDtypeStruct((t, d), BF16), jax.ShapeDtypeStruct((t, f), BF16),
                   jax.ShapeDtypeStruct((t, 2 * f), BF16)],
        args=(xin, lnp, mod, w), comm=comm)


def _ffn_down_ln(a, wd, xin, lnp_in, mod, *, seq, gate_idx, use_ln, name, comm=None):
    t, f = a.shape
    d = wd.shape[1]
    tm = min(512, seq)
    tpb = seq // tm

    def body(a_ref, wd_ref, x_ref, ln_ref, mod_ref, f_ref, xhat_ref, rstd_ref, acc):
        av = a_ref[...]
        for j in range(d // COL_CHUNK):
            acc[:, j * COL_CHUNK:(j + 1) * COL_CHUNK] = jnp.dot(
                av, wd_ref[:, j * COL_CHUNK:(j + 1) * COL_CHUNK], preferred_element_type=F32)
        scale = 0.5 * (1.0 + mod_ref[0, gate_idx:gate_idx + 1, :])

        fo = acc[...]
        x = x_ref[...]
        if use_ln:
            x = x * ln_ref[0:1, :] + ln_ref[1:2, :]
        xhat, rstd = _ln_stats(DN_ALPHA * x + scale * fo)
        f_ref[...] = fo.astype(BF16)
        xhat_ref[...] = xhat
        rstd_ref[...] = rstd

    return _pcall(
        body, name=name, grid=(t // tm,),
        in_specs=[pl.BlockSpec((tm, f), lambda i: (i, 0)), _resident((f, d)),
                  pl.BlockSpec((tm, d), lambda i: (i, 0)), _full((2, d)),
                  pl.BlockSpec((1, N_MOD, d), lambda i: (i // tpb, 0, 0))],
        out_specs=[pl.BlockSpec((tm, d), lambda i: (i, 0)), pl.BlockSpec((tm, d), lambda i: (i, 0)),
                   pl.BlockSpec((tm, 1), lambda i: (i, 0))],
        out_shape=[jax.ShapeDtypeStruct((t, d), BF16), jax.ShapeDtypeStruct((t, d), F32),
                   jax.ShapeDtypeStruct((t, 1), F32)],
        scratch_shapes=[pltpu.VMEM((tm, d), F32)],
        args=(a, wd, xin, lnp_in, mod), comm=comm)


def _ffn_down_loss(a, wd, xhat_in, lnp_in, mod, lnp_out, tgt, *, seq, gate_idx, name):
    t, f = a.shape
    d = wd.shape[1]
    nb = t // seq
    tm = min(512, seq)
    tpb = seq // tm

    def body(a_ref, wd_ref, x_ref, lnin_ref, mod_ref, lnout_ref, tgt_ref,
             dr_ref, df_ref, loss_ref, dg_ref, db_ref, dgate_ref, acc):
        i = pl.program_id(0)
        av = a_ref[...]
        for j in range(d // COL_CHUNK):
            acc[:, j * COL_CHUNK:(j + 1) * COL_CHUNK] = jnp.dot(
                av, wd_ref[:, j * COL_CHUNK:(j + 1) * COL_CHUNK], preferred_element_type=F32)
        scale = 0.5 * (1.0 + mod_ref[0, gate_idx:gate_idx + 1, :])
        g_in, b_in = lnin_ref[0:1, :], lnin_ref[1:2, :]
        g_out, b_out = lnout_ref[0:1, :], lnout_ref[1:2, :]

        def chunk(rows, carry):
            s_loss, s_dg, s_db, s_gate = carry
            fo = acc[rows, :]
            xhat, rstd = _ln_stats(DN_ALPHA * (x_ref[rows, :] * g_in + b_in) + scale * fo)
            e = xhat * g_out + b_out - tgt_ref[rows, :]
            dy = e * (1.0 / d)
            dr = _ln_bwd(dy, xhat, rstd, g_out)
            dr_ref[rows, :] = dr
            df_ref[rows, :] = (scale * dr).astype(BF16)
            return (s_loss + _fold8(e * e), s_dg + _fold8(dy * xhat), s_db + _fold8(dy),
                    s_gate + _fold8(0.5 * fo * dr))

        zero = jnp.zeros((8, d), F32)
        s_loss, s_dg, s_db, s_gate = _row_chunk_loop(tm, chunk, (zero, zero, zero, zero))

        @pl.when(i == 0)
        def _():
            loss_ref[...] = jnp.zeros_like(loss_ref)
            dg_ref[...] = jnp.zeros_like(dg_ref)
            db_ref[...] = jnp.zeros_like(db_ref)

        @pl.when(i % tpb == 0)
        def _():
            dgate_ref[...] = jnp.zeros_like(dgate_ref)

        loss_ref[...] += _row_sum(s_loss)
        dg_ref[...] += _row_sum(s_dg)
        db_ref[...] += _row_sum(s_db)
        dgate_ref[0] += _row_sum(s_gate)

    return pl.pallas_call(
        body, name=name, grid=(t // tm,), scratch_shapes=[pltpu.VMEM((tm, d), F32)],
        in_specs=[pl.BlockSpec((tm, f), lambda i: (i, 0)), _resident((f, d)),
                  pl.BlockSpec((tm, d), lambda i: (i, 0)), _full((2, d)),
                  pl.BlockSpec((1, N_MOD, d), lambda i: (i // tpb, 0, 0)), _full((2, d)),
                  pl.BlockSpec((tm, d), lambda i: (i, 0))],
        out_specs=[pl.BlockSpec((tm, d), lambda i: (i, 0)), pl.BlockSpec((tm, d), lambda i: (i, 0)),
                   _full((1, d)), _full((1, d)), _full((1, d)),
                   pl.BlockSpec((1, 1, d), lambda i: (i // tpb, 0, 0))],
        out_shape=[jax.ShapeDtypeStruct((t, d), F32), jax.ShapeDtypeStruct((t, d), BF16),
                   jax.ShapeDtypeStruct((1, d), F32), jax.ShapeDtypeStruct((1, d), F32),
                   jax.ShapeDtypeStruct((1, d), F32), jax.ShapeDtypeStruct((nb, 1, d), F32)],
        compiler_params=_params(("arbitrary",)),
    )(a, wd, xhat_in, lnp_in, mod, lnp_out, tgt)


def _rope(v, cos, sa, sb):
    return v * cos + pltpu.roll(v, LANE - ROT_DIM // 2, 1) * sa + pltpu.roll(v, ROT_DIM // 2, 1) * sb


def _rope_t(dy, cos, sa, sb):
    return dy * cos + pltpu.roll(dy * sa, ROT_DIM // 2, 1) + pltpu.roll(dy * sb, LANE - ROT_DIM // 2, 1)


def _in_proj(xhat, lnp, mod, w_t, cos, sa, sb, *, seq, sc_idx, sh_idx, name, comm=None):
    t, d = xhat.shape
    tm = min(512, seq)
    tpb = seq // tm
    n_conv = 3 * CONV_WIDTH

    def body(x_ref, ln_ref, mod_ref, w_ref, cos_ref, sa_ref, sb_ref, h_ref, q_ref, k_ref, v_ref, ubc_ref):
        x = x_ref[...] * ln_ref[0:1, :] + ln_ref[1:2, :]
        h = x * (1.0 + mod_ref[0, sc_idx:sc_idx + 1, :]) + mod_ref[0, sh_idx:sh_idx + 1, :]
        hb = h.astype(BF16)
        h_ref[...] = hb
        cos_t, sa_t, sb_t = cos_ref[...], sa_ref[...], sb_ref[...]
        for j in range(ATTN_WIDTH // COL_CHUNK):
            p = _dot_nt(hb, w_ref[j * COL_CHUNK:(j + 1) * COL_CHUNK, :])
            for s in range(COL_CHUNK // LANE):
                q_ref[:, j * COL_CHUNK + s * LANE:j * COL_CHUNK + (s + 1) * LANE] = _rope(
                    p[:, s * LANE:(s + 1) * LANE], cos_t, sa_t, sb_t).astype(BF16)
        p = _dot_nt(hb, w_ref[ATTN_WIDTH:ATTN_WIDTH + 2 * KV_WIDTH, :])
        k_ref[...] = _rope(p[:, 0:KV_WIDTH], cos_t, sa_t, sb_t).astype(BF16)
        v_ref[...] = p[:, KV_WIDTH:].astype(BF16)
        base = ATTN_WIDTH + 2 * KV_WIDTH
        for j in range(n_conv // COL_CHUNK):
            ubc_ref[:, j * COL_CHUNK:(j + 1) * COL_CHUNK] = _dot_nt(
                hb, w_ref[base + j * COL_CHUNK:base + (j + 1) * COL_CHUNK, :]).astype(BF16)

    row = lambda w: pl.BlockSpec((tm, w), lambda i: (i, 0))
    return _pcall(
        body, name=name, grid=(t // tm,),
        in_specs=[row(d), _full((2, d)), pl.BlockSpec((1, N_MOD, d), lambda i: (i // tpb, 0, 0)),
                  _resident((IN_WIDTH, d)), row(LANE), row(LANE), row(LANE)],
        out_specs=[row(d), row(ATTN_WIDTH), row(KV_WIDTH), row(KV_WIDTH), row(n_conv)],
        out_shape=[jax.ShapeDtypeStruct((t, d), BF16), jax.ShapeDtypeStruct((t, ATTN_WIDTH), BF16),
                   jax.ShapeDtypeStruct((t, KV_WIDTH), BF16), jax.ShapeDtypeStruct((t, KV_WIDTH), BF16),
                   jax.ShapeDtypeStruct((t, n_conv), BF16)],
        args=(xhat, lnp, mod, w_t, cos, sa, sb), comm=comm)


ATTN_TILE_BLOCKS = 2


def _attn_sub_block(s, tile, nblk, kp_ref, kc_ref, vp_ref, vc_ref):
    rows = slice(s * BLOCK, (s + 1) * BLOCK)
    if s == 0:
        first = ((tile * ATTN_TILE_BLOCKS) % nblk) == 0
        return rows, (kp_ref, slice(0, BLOCK)), (kc_ref, rows), (vp_ref, slice(0, BLOCK)), (vc_ref, rows), first
    before = slice((s - 1) * BLOCK, s * BLOCK)
    return rows, (kc_ref, before), (kc_ref, rows), (vc_ref, before), (vc_ref, rows), False


def _attn_group(q_ref, rows, k_prev, k_cur, v_prev, v_cur, sink_ref, g, first):
    lo, hi = g * HEAD_DIM, (g + 1) * HEAD_DIM
    kk = jnp.concatenate([k_prev[0][k_prev[1], lo:hi], k_cur[0][k_cur[1], lo:hi]], axis=0)
    vv = jnp.concatenate([v_prev[0][v_prev[1], lo:hi], v_cur[0][v_cur[1], lo:hi]], axis=0)
    qs = jnp.concatenate([q_ref[rows, (GQA_GROUP * g + j) * HEAD_DIM:(GQA_GROUP * g + j + 1) * HEAD_DIM]
                          for j in range(GQA_GROUP)], axis=0)
    cols = GQA_GROUP * BLOCK
    ki = lax.broadcasted_iota(jnp.int32, (2 * BLOCK, cols), 0)
    col = lax.broadcasted_iota(jnp.int32, (2 * BLOCK, cols), 1)
    diff = (col & (BLOCK - 1)) + BLOCK - ki
    valid = (diff >= 0) & (diff < WINDOW) & ((ki >= BLOCK) | jnp.logical_not(first))
    s = _dot_nt(kk, qs) * (HEAD_DIM ** -0.5)
    s = jnp.where(valid, s, -1e30)
    hcol = lax.broadcasted_iota(jnp.int32, (1, cols), 1)
    sink = jnp.zeros((1, cols), F32)
    for j in range(GQA_GROUP):
        sink = jnp.where(hcol // BLOCK == j, sink_ref[GQA_GROUP * g + j], sink)
    m = jnp.maximum(jnp.max(s, axis=0, keepdims=True), sink)
    p = jnp.exp(s - m)
    ps = jnp.exp(sink - m)
    inv = 1.0 / (jnp.sum(p, axis=0, keepdims=True) + ps)
    return qs, kk, vv, p * inv, ps * inv


def _heads_to_lanes(x_t):
    return jnp.concatenate([x_t[:, j * BLOCK:(j + 1) * BLOCK].T for j in range(GQA_GROUP)], axis=1)


def _attention(q, k, v, sinks, *, seq, name, comm=None):
    t = q.shape[0]
    nblk = seq // BLOCK
    tile = ATTN_TILE_BLOCKS * BLOCK

    def body(q_ref, kp_ref, kc_ref, vp_ref, vc_ref, sink_ref, o_ref):
        for s in range(ATTN_TILE_BLOCKS):
            rows, k_prev, k_cur, v_prev, v_cur, first = _attn_sub_block(
                s, pl.program_id(0), nblk, kp_ref, kc_ref, vp_ref, vc_ref)
            outs = []
            for g in range(N_KV_HEADS):
                _, _, vv, pn, _ = _attn_group(q_ref, rows, k_prev, k_cur, v_prev, v_cur, sink_ref, g, first)
                outs.append(_heads_to_lanes(_dot_tn(vv, pn.astype(BF16))))
            o_ref[rows, :] = jnp.concatenate(outs, axis=1).astype(BF16)

    cur = lambda w: pl.BlockSpec((tile, w), lambda n: (n, 0))
    prev = lambda w: pl.BlockSpec((BLOCK, w), lambda n: (jnp.maximum(n * ATTN_TILE_BLOCKS - 1, 0), 0))
    return _pcall(
        body, name=name, grid=(t // tile,),
        in_specs=[cur(ATTN_WIDTH), prev(KV_WIDTH), cur(KV_WIDTH), prev(KV_WIDTH), cur(KV_WIDTH),
                  pl.BlockSpec(memory_space=pltpu.SMEM)],
        out_specs=cur(ATTN_WIDTH),
        out_shape=jax.ShapeDtypeStruct((t, ATTN_WIDTH), BF16),
        args=(q, k, k, v, v, sinks), comm=comm)


def _out_proj(attn, ubc, cw, wout, xhat_in, lnp_in, mod, *, seq, gate_idx, name, comm=None):
    t, d = xhat_in.shape
    tm = min(512, seq)
    tpb = seq // tm
    cwid = CONV_WIDTH

    def body(attn_ref, ubc_ref, halo_ref, cw_ref, w_ref, x_ref, ln_ref, mod_ref,
             mixin_ref, mix_ref, xhat_ref, rstd_ref, zbuf, acc):
        first = (pl.program_id(0) % tpb) == 0
        u, bg, cg = (ubc_ref[:, s * cwid:(s + 1) * cwid].astype(F32) for s in range(3))
        z = cg * u
        hz = halo_ref[:, 2 * cwid:3 * cwid].astype(F32) * halo_ref[:, 0:cwid].astype(F32)
        zbuf[0:HALO, :] = jnp.where(first, 0.0, hz)
        zbuf[HALO:HALO + tm, :] = z
        y = (cw_ref[0:1, :] * zbuf[HALO - 2:HALO - 2 + tm, :] + cw_ref[1:2, :] * zbuf[HALO - 1:HALO - 1 + tm, :]
             + cw_ref[2:3, :] * z)
        mixin_ref[:, 0:ATTN_WIDTH] = attn_ref[...]
        mixin_ref[:, ATTN_WIDTH:] = (bg * y).astype(BF16)
        mv = mixin_ref[...]
        for j in range(d // COL_CHUNK):
            acc[:, j * COL_CHUNK:(j + 1) * COL_CHUNK] = jnp.dot(
                mv, w_ref[:, j * COL_CHUNK:(j + 1) * COL_CHUNK], preferred_element_type=F32)
        scale = 1.0 + mod_ref[0, gate_idx:gate_idx + 1, :]

        mix = acc[...]
        xhat, rstd = _ln_stats(DN_ALPHA * (x_ref[...] * ln_ref[0:1, :] + ln_ref[1:2, :]) + scale * mix)
        mix_ref[...] = mix.astype(BF16)
        xhat_ref[...] = xhat
        rstd_ref[...] = rstd

    row = lambda w: pl.BlockSpec((tm, w), lambda i: (i, 0))
    return _pcall(
        body, name=name, grid=(t // tm,),
        in_specs=[row(ATTN_WIDTH), row(3 * cwid),
                  pl.BlockSpec((HALO, 3 * cwid), lambda i: (jnp.maximum(i * (tm // HALO) - 1, 0), 0)),
                  _full((8, cwid)), _resident((d, d)), row(d), _full((2, d)),
                  pl.BlockSpec((1, N_MOD, d), lambda i: (i // tpb, 0, 0))],
        out_specs=[row(d), row(d), row(d), row(1)],
        out_shape=[jax.ShapeDtypeStruct((t, d), BF16), jax.ShapeDtypeStruct((t, d), BF16),
                   jax.ShapeDtypeStruct((t, d), F32), jax.ShapeDtypeStruct((t, 1), F32)],
        scratch_shapes=[pltpu.VMEM((tm + HALO, cwid), F32), pltpu.VMEM((tm, d), F32)],
        args=(attn, ubc, ubc, cw, wout, xhat_in, lnp_in, mod), comm=comm)


def _ffn_bwd_act(df, wd, gu, *, seq, name, comm=None):
    t, d = df.shape
    f = wd.shape[0]
    tm = min(512, seq)
    ch = min(COL_CHUNK, f)

    def body(df_ref, wd_ref, gu_ref, dgu_ref):
        dfv = df_ref[...]
        for j in range(f // ch):
            da = _dot_nt(dfv, wd_ref[j * ch:(j + 1) * ch, :])
            dgu_ref[:, j * ch:(j + 1) * ch] = (da * gu_ref[:, j * ch:(j + 1) * ch].astype(F32)).astype(BF16)
            dgu_ref[:, f + j * ch:f + (j + 1) * ch] = (
                da * gu_ref[:, f + j * ch:f + (j + 1) * ch].astype(F32)).astype(BF16)

    return _pcall(
        body, name=name, grid=(t // tm,),
        in_specs=[pl.BlockSpec((tm, d), lambda i: (i, 0)), _resident((f, d)),
                  pl.BlockSpec((tm, 2 * f), lambda i: (i, 0))],
        out_specs=pl.BlockSpec((tm, 2 * f), lambda i: (i, 0)),
        out_shape=jax.ShapeDtypeStruct((t, 2 * f), BF16),
        args=(df, wd, gu), comm=comm)


def _bwd_in(a, w, dr, xin, rstd_prev, lnp_prev, mod, branch_prev, *, seq, w_is_nt, sc_idx, gate_idx,
            branch_scale, final, name, comm=None):
    t, kdim = a.shape
    d = dr.shape[1]
    nb = t // seq
    tm = min(512, seq)
    tpb = seq // tm

    def body(*refs):
        if final:
            a_ref, w_ref, dr_ref, x_ref, mod_ref, dx_ref, dsc_ref, dsh_ref, acc = refs
        else:
            (a_ref, w_ref, dr_ref, x_ref, rstd_ref, ln_ref, mod_ref, br_ref,
             drp_ref, dbr_ref, dsc_ref, dsh_ref, dgate_ref, dg_ref, db_ref, acc) = refs
        i = pl.program_id(0)
        av = a_ref[...]
        for j in range(d // COL_CHUNK):
            cols = slice(j * COL_CHUNK, (j + 1) * COL_CHUNK)
            acc[:, cols] = (_dot_nt(av, w_ref[cols, :]) if w_is_nt
                            else jnp.dot(av, w_ref[:, cols], preferred_element_type=F32))
        sc1 = 1.0 + mod_ref[0, sc_idx:sc_idx + 1, :]
        if not final:
            g_prev, b_prev = ln_ref[0:1, :], ln_ref[1:2, :]
            bscale = branch_scale * (1.0 + mod_ref[0, gate_idx:gate_idx + 1, :])

        def chunk(rows, carry):
            dh = acc[rows, :]
            dx = DN_ALPHA * dr_ref[rows, :] + dh * sc1
            if final:
                dx_ref[rows, :] = dx
                return carry[0] + _fold8(dh * x_ref[rows, :]), carry[1] + _fold8(dh)
            xhat = x_ref[rows, :]
            drp = _ln_bwd(dx, xhat, rstd_ref[rows, :], g_prev)
            drp_ref[rows, :] = drp
            dbr_ref[rows, :] = (bscale * drp).astype(BF16)
            return (carry[0] + _fold8(dh * (xhat * g_prev + b_prev)), carry[1] + _fold8(dh),
                    carry[2] + _fold8(branch_scale * br_ref[rows, :].astype(F32) * drp),
                    carry[3] + _fold8(dx * xhat), carry[4] + _fold8(dx))

        zero = jnp.zeros((8, d), F32)
        sums = _row_chunk_loop(tm, chunk, (zero,) * (2 if final else 5))

        @pl.when((i % tpb) == 0)
        def _():
            dsc_ref[...] = jnp.zeros_like(dsc_ref)
            dsh_ref[...] = jnp.zeros_like(dsh_ref)
            if not final:
                dgate_ref[...] = jnp.zeros_like(dgate_ref)

        dsc_ref[0] += _row_sum(sums[0])
        dsh_ref[0] += _row_sum(sums[1])
        if not final:
            @pl.when(i == 0)
            def _():
                dg_ref[...] = jnp.zeros_like(dg_ref)
                db_ref[...] = jnp.zeros_like(db_ref)

            dgate_ref[0] += _row_sum(sums[2])
            dg_ref[...] += _row_sum(sums[3])
            db_ref[...] += _row_sum(sums[4])

    row = lambda w_: pl.BlockSpec((tm, w_), lambda i: (i, 0))
    vec = pl.BlockSpec((1, 1, d), lambda i: (i // tpb, 0, 0))
    mod_spec = pl.BlockSpec((1, N_MOD, d), lambda i: (i // tpb, 0, 0))
    vshape = jax.ShapeDtypeStruct((nb, 1, d), F32)
    if final:
        in_specs = [row(kdim), _resident(w.shape), row(d), row(d), mod_spec]
        args = (a, w, dr, xin, mod)
        out_specs = [row(d), vec, vec]
        out_shape = [jax.ShapeDtypeStruct((t, d), F32), vshape, vshape]
    else:
        in_specs = [row(kdim), _resident(w.shape), row(d), row(d), row(1), _full((2, d)), mod_spec, row(d)]
        args = (a, w, dr, xin, rstd_prev, lnp_prev, mod, branch_prev)
        out_specs = [row(d), row(d), vec, vec, vec, _full((1, d)), _full((1, d))]
        out_shape = [jax.ShapeDtypeStruct((t, d), F32), jax.ShapeDtypeStruct((t, d), BF16), vshape, vshape, vshape,
                     jax.ShapeDtypeStruct((1, d), F32), jax.ShapeDtypeStruct((1, d), F32)]
    return _pcall(
        body, name=name, grid=(t // tm,), in_specs=in_specs, out_specs=out_specs, out_shape=out_shape,
        scratch_shapes=[pltpu.VMEM((tm, d), F32)], args=args, comm=comm)


def _matmul_tn(a, b, *, tmm, tnn, name, comm=None):
    t, m = a.shape
    n = b.shape[1]
    tk = min(2048, t)

    def body(a_ref, b_ref, o_ref):
        @pl.when(pl.program_id(2) == 0)
        def _():
            o_ref[...] = jnp.zeros_like(o_ref)
        o_ref[...] += _dot_tn(a_ref[...], b_ref[...])

    return _pcall(
        body, name=name, grid=(m // tmm, n // tnn, t // tk),
        in_specs=[pl.BlockSpec((tk, tmm), lambda i, j, k: (k, i)), pl.BlockSpec((tk, tnn), lambda i, j, k: (k, j))],
        out_specs=pl.BlockSpec((tmm, tnn), lambda i, j, k: (i, j)),
        out_shape=jax.ShapeDtypeStruct((m, n), F32),
        args=(a, b), comm=comm)


def _grad_chip_sum(pos, a, b, *, name, comm=None):
    t, m = a.shape
    n = b.shape[1]
    hm, tnn = m // 2, n // N_CHIPS
    tk = min(2048, t)
    nk = t // tk
    n_j = n // tnn

    def body(pos_ref, a_ref, b_ref, s32_ref, s16_ref, land_ref, acc, theirs, send_sems, recv_sems, copy_sem):
        p, j, k = pl.program_id(0), pl.program_id(1), pl.program_id(2)
        x, y, c = _position()

        def push(jj):
            return pltpu.make_async_remote_copy(
                src_ref=acc.at[jj], dst_ref=land_ref.at[jj], send_sem=send_sems.at[jj], recv_sem=recv_sems.at[jj],
                device_id=(x, y, 1 - c), device_id_type=MESH)

        fetch = pltpu.make_async_copy(land_ref.at[j], theirs, copy_sem)

        @pl.when(jnp.logical_and(p == 1, k == 0))
        def _():
            push(j).wait_send()
            push(j).wait_recv()
            fetch.start()

        part = _dot_tn(a_ref[...], b_ref[...])

        @pl.when(k == 0)
        def _():
            acc[j] = part

        @pl.when(k > 0)
        def _():
            acc[j] += part

        @pl.when(jnp.logical_and(p == 0, k == nk - 1))
        def _():
            push(j).start()

        @pl.when(jnp.logical_and(p == 1, k == nk - 1))
        def _():
            fetch.wait()
            s = acc[j] + theirs[...]
            s32_ref[0] = s
            s16_ref[0] = s.astype(BF16)

    half = lambda p, pos_ref: 1 - pos_ref[2] - p + 2 * p * pos_ref[2]
    out_tile = pl.BlockSpec((1, hm, tnn), lambda p, j, k, pos_ref: (0, 0, j * p))
    shape = lambda dt: jax.ShapeDtypeStruct((1, hm, n), dt)
    out = _pcall(
        body, name=name, grid=(2, n_j, nk),
        in_specs=[pl.BlockSpec((tk, hm), lambda p, j, k, pos_ref: (k, half(p, pos_ref))),
                  pl.BlockSpec((tk, tnn), lambda p, j, k, pos_ref: (k, j))],
        out_specs=[out_tile, out_tile, ANY_SPEC],
        out_shape=[shape(F32), shape(BF16), jax.ShapeDtypeStruct((n_j, hm, tnn), F32)],
        scratch_shapes=[pltpu.VMEM((n_j, hm, tnn), F32), pltpu.VMEM((hm, tnn), F32),
                        pltpu.SemaphoreType.DMA((n_j,)), pltpu.SemaphoreType.DMA((n_j,)), pltpu.SemaphoreType.DMA],
        args=(a, b), prefetch=pos, comm=comm)
    if comm is None:
        return out[0], out[1]
    (s32, s16, _), extra = out
    return (s32, s16), extra


def _matmul_nt_bf16(a, w, *, seq, name):
    t, kdim = a.shape
    n = w.shape[0]
    tm = min(512, seq)

    def body(a_ref, w_ref, o_ref):
        av = a_ref[...]
        for j in range(n // COL_CHUNK):
            o_ref[:, j * COL_CHUNK:(j + 1) * COL_CHUNK] = _dot_nt(
                av, w_ref[j * COL_CHUNK:(j + 1) * COL_CHUNK, :]).astype(BF16)

    return pl.pallas_call(
        body, name=name, grid=(t // tm,),
        in_specs=[pl.BlockSpec((tm, kdim), lambda i: (i, 0)), _resident((n, kdim))],
        out_specs=pl.BlockSpec((tm, n), lambda i: (i, 0)),
        out_shape=jax.ShapeDtypeStruct((t, n), BF16),
        compiler_params=_params(("arbitrary",)),
    )(a, w)


def _attention_bwd(q, k, v, dmixin, sinks, *, seq, name, comm=None):
    t = q.shape[0]
    nblk = seq // BLOCK
    tile = ATTN_TILE_BLOCKS * BLOCK

    def body(q_ref, kp_ref, kc_ref, vp_ref, vc_ref, do_ref, sink_ref,
             dq_ref, dkp_ref, dkc_ref, dvp_ref, dvc_ref, dsink_ref):
        n = pl.program_id(0)

        @pl.when(n == 0)
        def _():
            dsink_ref[...] = jnp.zeros_like(dsink_ref)

        srow = lax.broadcasted_iota(jnp.int32, (8, LANE), 0)
        dsink = jnp.zeros((8, LANE), F32)
        for s in range(ATTN_TILE_BLOCKS):
            rows, k_prev, k_cur, v_prev, v_cur, first = _attn_sub_block(s, n, nblk, kp_ref, kc_ref, vp_ref, vc_ref)
            dqs, dks, dvs = [], [], []
            for g in range(N_KV_HEADS):
                qs, kk, vv, pn, psn = _attn_group(q_ref, rows, k_prev, k_cur, v_prev, v_cur, sink_ref, g, first)
                dos = jnp.concatenate(
                    [do_ref[rows, (GQA_GROUP * g + j) * HEAD_DIM:(GQA_GROUP * g + j + 1) * HEAD_DIM]
                     for j in range(GQA_GROUP)], axis=0)
                dp = _dot_nt(vv, dos)
                delta = jnp.sum(pn * dp, axis=0, keepdims=True)
                ds = pn * (dp - delta)
                dsk = psn * delta
                for j in range(GQA_GROUP):
                    tot = jnp.sum(dsk[:, j * BLOCK:(j + 1) * BLOCK], axis=1, keepdims=True)
                    dsink = dsink - jnp.where(srow == GQA_GROUP * g + j, tot, 0.0)
                dsb = (ds * (HEAD_DIM ** -0.5)).astype(BF16)
                dqs.append(_heads_to_lanes(_dot_tn(kk, dsb)))
                dks.append(jnp.dot(dsb, qs, preferred_element_type=F32))
                dvs.append(jnp.dot(pn.astype(BF16), dos, preferred_element_type=F32))
            dq_ref[rows, :] = jnp.concatenate(dqs, axis=1)
            dkp_ref[rows, :] = jnp.concatenate([x[0:BLOCK, :] for x in dks], axis=1)
            dkc_ref[rows, :] = jnp.concatenate([x[BLOCK:, :] for x in dks], axis=1)
            dvp_ref[rows, :] = jnp.concatenate([x[0:BLOCK, :] for x in dvs], axis=1)
            dvc_ref[rows, :] = jnp.concatenate([x[BLOCK:, :] for x in dvs], axis=1)
        dsink_ref[...] += dsink

    cur = lambda w: pl.BlockSpec((tile, w), lambda n: (n, 0))
    prev = lambda w: pl.BlockSpec((BLOCK, w), lambda n: (jnp.maximum(n * ATTN_TILE_BLOCKS - 1, 0), 0))
    kv = jax.ShapeDtypeStruct((t, KV_WIDTH), F32)
    return _pcall(
        body, name=name, grid=(t // tile,),
        in_specs=[cur(ATTN_WIDTH), prev(KV_WIDTH), cur(KV_WIDTH), prev(KV_WIDTH), cur(KV_WIDTH), cur(ATTN_WIDTH),
                  pl.BlockSpec(memory_space=pltpu.SMEM)],
        out_specs=[cur(ATTN_WIDTH), cur(KV_WIDTH), cur(KV_WIDTH), cur(KV_WIDTH), cur(KV_WIDTH), _full((8, LANE))],
        out_shape=[jax.ShapeDtypeStruct((t, ATTN_WIDTH), F32), kv, kv, kv, kv, jax.ShapeDtypeStruct((8, LANE), F32)],
        args=(q, k, k, v, v, dmixin, sinks), comm=comm)


def _mix_bwd_assemble(dq, dkp, dkc, dvp, dvc, cos, sa, sb, dmixin, ubc, cw, *, seq, name, comm=None):
    t = dq.shape[0]
    cwid = CONV_WIDTH
    tm = min(2 * BLOCK, seq)
    tiles_per_seq = seq // tm
    ntile = t // tm
    nblk_all = t // BLOCK
    per_tile = tm // BLOCK

    def body(*refs):
        dq_ref, dkc_ref, dvc_ref = refs[0:3]
        dkp_refs, dvp_refs = refs[3:3 + per_tile], refs[3 + per_tile:3 + 2 * per_tile]
        (cos_ref, sa_ref, sb_ref, dco_ref, dcon_ref, ubc_ref, hprev_ref, hnext_ref, cw_ref,
         dproj_ref, dcw_ref, zbuf, dybuf) = refs[3 + 2 * per_tile:]
        i = pl.program_id(0)
        first = (i % tiles_per_seq) == 0
        last = (i % tiles_per_seq) == tiles_per_seq - 1
        glast = i == ntile - 1

        @pl.when(i == 0)
        def _():
            dcw_ref[...] = jnp.zeros_like(dcw_ref)

        def with_next_block(cur_ref, nxt_refs):
            nxt = [r[...] for r in nxt_refs]
            nxt[-1] = jnp.where(glast, 0.0, nxt[-1])
            return cur_ref[...] + jnp.concatenate(nxt, axis=0)

        cos_t, sa_t, sb_t = cos_ref[...], sa_ref[...], sb_ref[...]
        for j in range(ATTN_WIDTH // LANE):
            dproj_ref[:, j * LANE:(j + 1) * LANE] = _rope_t(
                dq_ref[:, j * LANE:(j + 1) * LANE], cos_t, sa_t, sb_t).astype(BF16)
        dk = with_next_block(dkc_ref, dkp_refs)
        dproj_ref[:, ATTN_WIDTH:ATTN_WIDTH + KV_WIDTH] = _rope_t(dk, cos_t, sa_t, sb_t).astype(BF16)
        dv = with_next_block(dvc_ref, dvp_refs)
        dproj_ref[:, ATTN_WIDTH + KV_WIDTH:ATTN_WIDTH + 2 * KV_WIDTH] = dv.astype(BF16)

        u, bg, cg = (ubc_ref[:, s * cwid:(s + 1) * cwid].astype(F32) for s in range(3))
        z = cg * u
        hz = hprev_ref[:, 2 * cwid:3 * cwid].astype(F32) * hprev_ref[:, 0:cwid].astype(F32)
        zbuf[0:HALO, :] = jnp.where(first, 0.0, hz)
        zbuf[HALO:HALO + tm, :] = z
        z2, z1 = zbuf[HALO - 2:HALO - 2 + tm, :], zbuf[HALO - 1:HALO - 1 + tm, :]
        w0, w1, w2 = cw_ref[0:1, :], cw_ref[1:2, :], cw_ref[2:3, :]
        y = w0 * z2 + w1 * z1 + w2 * z
        dco = dco_ref[...].astype(F32)
        dyc = dco * bg
        dyn = dcon_ref[...].astype(F32) * hnext_ref[:, cwid:2 * cwid].astype(F32)
        dybuf[0:tm, :] = dyc
        dybuf[tm:tm + HALO, :] = jnp.where(last, 0.0, dyn)
        dz = w2 * dyc + w1 * dybuf[1:1 + tm, :] + w0 * dybuf[2:2 + tm, :]
        srow = lax.broadcasted_iota(jnp.int32, (8, cwid), 0)
        dcw_ref[...] += (jnp.where(srow == 0, _row_sum(dyc * z2), 0.0) + jnp.where(srow == 1, _row_sum(dyc * z1), 0.0)
                         + jnp.where(srow == 2, _row_sum(dyc * z), 0.0))
        base = ATTN_WIDTH + 2 * KV_WIDTH
        dproj_ref[:, base:base + cwid] = (dz * cg).astype(BF16)
        dproj_ref[:, base + cwid:base + 2 * cwid] = (dco * y).astype(BF16)
        dproj_ref[:, base + 2 * cwid:base + 3 * cwid] = (dz * u).astype(BF16)

    cur = lambda w: pl.BlockSpec((tm, w), lambda i: (i, 0))
    nxt = [pl.BlockSpec((BLOCK, KV_WIDTH), lambda i, s=s: (jnp.minimum(i * per_tile + s + 1, nblk_all - 1), 0))
           for s in range(per_tile)]
    prev_halo = pl.BlockSpec((HALO, 3 * cwid), lambda i: (jnp.maximum(i * (tm // HALO) - 1, 0), 0))
    next_halo = lambda w, col: pl.BlockSpec(
        (HALO, w), lambda i: (jnp.minimum((i + 1) * (tm // HALO), t // HALO - 1), col))
    return _pcall(
        body, name=name, grid=(ntile,),
        in_specs=[cur(ATTN_WIDTH), cur(KV_WIDTH), cur(KV_WIDTH), *nxt, *nxt,
                  cur(LANE), cur(LANE), cur(LANE),
                  pl.BlockSpec((tm, cwid), lambda i: (i, 1)), next_halo(cwid, 1),
                  cur(3 * cwid), prev_halo, next_halo(3 * cwid, 0), _full((8, cwid))],
        out_specs=[cur(IN_WIDTH), _full((8, cwid))],
        out_shape=[jax.ShapeDtypeStruct((t, IN_WIDTH), BF16), jax.ShapeDtypeStruct((8, cwid), F32)],
        scratch_shapes=[pltpu.VMEM((tm + HALO, cwid), F32), pltpu.VMEM((tm + HALO, cwid), F32)],
        args=(dq, dkc, dvc, *([dkp] * per_tile), *([dvp] * per_tile), cos, sa, sb, dmixin, dmixin,
              ubc, ubc, ubc, cw), comm=comm)


def _ada_fwd(c_all, w_ada, b_ada_shard, *, name, comm=None):
    nb, d = c_all.shape
    n = w_ada.shape[1]
    tn = n // 2

    def body(c_ref, w_ref, b_ref, o_ref):
        cv = c_ref[...]
        cond = cv * _sigmoid(cv)
        o_ref[...] = jnp.dot(cond, w_ref[...], preferred_element_type=F32,
                             precision=lax.Precision.HIGHEST) + b_ref[...]

    return _pcall(
        body, name=name, grid=(n // tn,),
        in_specs=[_full((nb, d)), pl.BlockSpec((d, tn), lambda j: (0, j)), pl.BlockSpec((1, tn), lambda j: (0, j))],
        out_specs=pl.BlockSpec((nb, tn), lambda j: (0, j)),
        out_shape=jax.ShapeDtypeStruct((nb, n), F32), args=(c_all, w_ada, b_ada_shard), comm=comm)


def _small_finish(gathered, dmod_all, dmod_shard, c_all_t, *, name):
    d = D_MODEL
    nb, n = dmod_shard.shape

    def body(g_ref, dm_ref, dms_ref, ct_ref, sum_ref, gw_ref, gb_ref):
        total = g_ref[0]
        for dev in range(1, N_DEV):
            total = total + g_ref[dev]
        sum_ref[...] = total
        gb_ref[...] = _row_sum(dm_ref[...])
        ctv = ct_ref[...]
        cond_t = ctv * _sigmoid(ctv)
        for jb in range(n // COL_CHUNK):
            gw_ref[:, jb * COL_CHUNK:(jb + 1) * COL_CHUNK] = jnp.dot(
                cond_t, dms_ref[:, jb * COL_CHUNK:(jb + 1) * COL_CHUNK], preferred_element_type=F32,
                precision=lax.Precision.HIGHEST)

    return pl.pallas_call(
        body, name=name, grid=(1,),
        in_specs=[_full((N_DEV, SMALL_ROWS, d)), _full((nb, N_MOD * d)), _full((nb, n)), _full((d, nb))],
        out_specs=[_full((SMALL_ROWS, d)), _full((d, n)), _full((1, N_MOD * d))],
        out_shape=[jax.ShapeDtypeStruct((SMALL_ROWS, d), F32), jax.ShapeDtypeStruct((d, n), F32),
                   jax.ShapeDtypeStruct((1, N_MOD * d), F32)],
        compiler_params=_params(("arbitrary",)),
    )(gathered, dmod_all, dmod_shard, c_all_t)


def _row_tile(r, c, budget=1 << 21):
    if r * c * 4 <= budget or r % 16:
        return r
    best = 16
    for tr in range(16, r + 1, 16):
        if r % tr == 0 and tr * c * 4 <= budget:
            best = tr
    return best


def _cast_into(w, chip, col_kind, *, name):
    r, c = w.shape
    tr = _row_tile(r, c)

    def body(chip_ref, w_ref, o_ref):
        o_ref[...] = w_ref[...].astype(BF16)

    if col_kind:
        out_spec = pl.BlockSpec((tr, c), lambda i, chip_ref: (i, chip_ref[0]))
        out_shape = jax.ShapeDtypeStruct((r, c * N_CHIPS), BF16)
    else:
        out_spec = pl.BlockSpec((tr, c), lambda i, chip_ref: (chip_ref[0] * (r // tr) + i, 0))
        out_shape = jax.ShapeDtypeStruct((r * N_CHIPS, c), BF16)
    return _pcall(body, name=name, grid=(r // tr,), in_specs=[pl.BlockSpec((tr, c), lambda i, chip_ref: (i, 0))],
                  out_specs=out_spec, out_shape=out_shape, args=(w,), prefetch=chip)


def _adamw(w, g, m, v, *, name, comm=None):
    r, c = w.shape
    tr = _row_tile(r, c)
    c1 = 1.0 - ADAM_B1 ** ADAM_STEP
    c2 = 1.0 - ADAM_B2 ** ADAM_STEP

    def body(w_ref, g_ref, m_ref, v_ref, d_ref, nm_ref, nv_ref):
        gv = g_ref[...]
        m2 = ADAM_B1 * m_ref[...] + (1.0 - ADAM_B1) * gv
        v2 = ADAM_B2 * v_ref[...] + (1.0 - ADAM_B2) * (gv * gv)
        d_ref[...] = -ADAM_LR * ((m2 / c1) / (jnp.sqrt(v2 / c2) + ADAM_EPS) + ADAM_WD * w_ref[...])
        nm_ref[...] = m2
        nv_ref[...] = v2

    spec = pl.BlockSpec((tr, c), lambda i: (i, 0))
    sh = jax.ShapeDtypeStruct((r, c), F32)
    return _pcall(body, name=name, grid=(r // tr,), in_specs=[spec] * 4, out_specs=[spec] * 3, out_shape=[sh] * 3,
                  args=(w, g, m, v), comm=comm)


def _sum_pair(pos, g3, r3, blk_of, *, name, comm=None):
    n, rows, cols = r3.shape
    tr = _row_tile(rows, cols)

    def body(pos_ref, g_ref, r_ref, s32_ref, s16_ref):
        s = g_ref[0] + r_ref[0]
        s32_ref[0] = s
        s16_ref[0] = s.astype(BF16)

    own = pl.BlockSpec((1, tr, cols), lambda p, i, pos: (blk_of(p, pos), i, 0))
    plain = pl.BlockSpec((1, tr, cols), lambda p, i, pos: (p, i, 0))
    return _pcall(
        body, name=name, grid=(n, rows // tr), in_specs=[own, plain], out_specs=[plain, plain],
        out_shape=[jax.ShapeDtypeStruct((n, rows, cols), F32), jax.ShapeDtypeStruct((n, rows, cols), BF16)],
        args=(g3, r3), prefetch=pos, comm=comm)


def _sum_final(pos, s32, recv, *, col_kind, n_shard, name, comm=None):
    if col_kind:
        rows, cols = s32.shape[1], n_shard
        own = lambda tr: pl.BlockSpec((1, tr, cols), lambda i, pos: (0, i, 2 * pos[0] + pos[1]))
    else:
        rows, cols = s32.shape[1], s32.shape[2]
        own = lambda tr: pl.BlockSpec((1, tr, cols), lambda i, pos: (2 * pos[0] + pos[1], i, 0))
    tr = _row_tile(rows, cols)

    def body(pos_ref, s_ref, r_ref, o_ref):
        o_ref[0] = ((s_ref[0] + r_ref[0].astype(F32)) + r_ref[1].astype(F32)) + r_ref[2].astype(F32)

    return _pcall(
        body, name=name, grid=(rows // tr,),
        in_specs=[own(tr), pl.BlockSpec((3, tr, cols), lambda i, pos: (0, i, 0))],
        out_specs=pl.BlockSpec((1, tr, cols), lambda i, pos: (pos[2], i, 0)),
        out_shape=jax.ShapeDtypeStruct((2, rows, cols), F32), args=(s32, recv), prefetch=pos, comm=comm)


def _position():
    return lax.axis_index("x"), lax.axis_index("y"), lax.axis_index("c")


def _allgather8(x_shard, *, name, comm=None):
    m_per, n = x_shard.shape
    nci, nco = (0, 0) if comm is None else (len(comm.inputs), len(comm.out_shapes))

    def body(*refs):
        x_ref, refs = refs[0], refs[1:]
        cin, refs = refs[:nci], refs[nci:]
        out_ref, refs = refs[0], refs[1:]
        cout, refs = refs[:nco], refs[nco:]
        (send_sems, recv_sems, local_sem), csems = refs[:3], refs[3:]
        x, y, c = _position()
        me, sibling = (x, y, c), (x, y, 1 - c)
        chips = [(1 - x, y), (x, 1 - y), (1 - x, 1 - y)]

        def rows(px, py, pc):
            return out_ref.at[pl.ds((4 * px + 2 * py + pc) * m_per, m_per), :]

        def copy(k, block, to, src=None):
            return pltpu.make_async_remote_copy(
                src_ref=rows(*block) if src is None else src, dst_ref=rows(*block),
                send_sem=send_sems.at[k], recv_sem=recv_sems.at[k], device_id=to, device_id_type=MESH)

        mine = pltpu.make_async_copy(x_ref, rows(*me), local_sem)
        mine.start()
        first = [copy(0, me, sibling, src=x_ref)]
        first += [copy(1 + j, me, (*chip, c), src=x_ref) for j, chip in enumerate(chips)]
        for cp in first:
            cp.start()
        if comm is not None:
            comm.start(cin, cout, csems)
        passed = [copy(4 + j, (*chip, c), sibling) for j, chip in enumerate(chips)]
        for j, chip in enumerate(chips):
            copy(1 + j, (*chip, c), me).wait_recv()
            passed[j].start()
        copy(0, sibling, me).wait_recv()
        for j, chip in enumerate(chips):
            copy(4 + j, (*chip, 1 - c), me).wait_recv()
        for cp in first + passed:
            cp.wait_send()
        mine.wait()
        if comm is not None:
            comm.middle(cin, cout, csems)
            comm.finish(cin, cout, csems)

    vmem = pl.BlockSpec(memory_space=pltpu.VMEM)
    sems = [pltpu.SemaphoreType.DMA((7,)), pltpu.SemaphoreType.DMA((7,)), pltpu.SemaphoreType.DMA]
    out = jax.ShapeDtypeStruct((N_DEV * m_per, n), x_shard.dtype)
    if comm is None:
        return pl.pallas_call(body, name=name, out_shape=out, in_specs=[vmem], out_specs=vmem,
                              scratch_shapes=sems)(x_shard)
    res = pl.pallas_call(
        body, name=name, out_shape=[out] + list(comm.out_shapes), in_specs=[vmem] + [ANY_SPEC] * nci,
        out_specs=[vmem] + [ANY_SPEC] * nco, scratch_shapes=sems + list(comm.sems),
        input_output_aliases={1 + i: 1 + o for i, o in comm.aliases.items()})(x_shard, *comm.inputs)
    return res[0], list(res[1:])


def _peer_chips(x, y):
    return [(1 - x, y), (x, 1 - y), (1 - x, 1 - y)]


class _GatherJob:
    def __init__(self, pieces):
        self.pieces = pieces
        n_p = len(pieces)
        self.inputs = [p[0] for p in pieces]
        self.out_shapes = [jax.ShapeDtypeStruct(p[0].shape, p[0].dtype) for p in pieces]
        for buf, col_kind, r0, nr in pieces:
            half_rows = buf.shape[0] // (2 if col_kind else 2 * N_CHIPS)
            assert r0 % 16 == 0 and nr % 16 == 0 and r0 + nr <= half_rows, (buf.shape, r0, nr)
        self.aliases = {p: p for p in range(n_p)}
        self.sems = [pltpu.SemaphoreType.DMA((3 * n_p,))] * 4

    def _region(self, cout, p, chip_idx, half):
        buf, col_kind, r0, nr = self.pieces[p]
        if col_kind:
            n = buf.shape[1] // N_CHIPS
            return cout[p].at[pl.ds(half * (buf.shape[0] // 2) + r0, nr), pl.ds(chip_idx * n, n)]
        n = buf.shape[0] // N_CHIPS
        return cout[p].at[pl.ds(chip_idx * n + half * (n // 2) + r0, nr), :]

    def _copies(self, cout, sems):
        send_sems, recv_sems, fsend_sems, frecv_sems = sems
        x, y, c = _position()
        k = 2 * x + y
        sibling = (x, y, 1 - c)
        sends, arrivals, fwds, fwd_arrivals = [], [], [], []

        def remote(region, ssem, rsem, to):
            return pltpu.make_async_remote_copy(src_ref=region, dst_ref=region, send_sem=ssem, recv_sem=rsem,
                                                device_id=to, device_id_type=MESH)

        for p in range(len(self.pieces)):
            for j, chip in enumerate(_peer_chips(x, y)):
                idx = 3 * p + j
                theirs = 2 * chip[0] + chip[1]
                sends.append(remote(self._region(cout, p, k, c), send_sems.at[idx], recv_sems.at[idx], (*chip, c)))
                arrivals.append(remote(self._region(cout, p, theirs, c), send_sems.at[idx], recv_sems.at[idx],
                                       (*chip, c)))
                fwds.append(remote(self._region(cout, p, theirs, c), fsend_sems.at[idx], frecv_sems.at[idx], sibling))
                fwd_arrivals.append(remote(self._region(cout, p, theirs, 1 - c), fsend_sems.at[idx],
                                           frecv_sems.at[idx], sibling))
        return sends, arrivals, fwds, fwd_arrivals

    def start(self, cin, cout, sems):
        for cp in self._copies(cout, sems)[0]:
            cp.start()

    def middle(self, cin, cout, sems):
        _, arrivals, fwds, _ = self._copies(cout, sems)
        for arrived, fw in zip(arrivals, fwds):
            arrived.wait_recv()
            fw.start()

    def finish(self, cin, cout, sems):
        sends, _, fwds, fwd_arrivals = self._copies(cout, sems)
        for arrived in fwd_arrivals:
            arrived.wait_recv()
        for cp in sends + fwds:
            cp.wait_send()


class _PairedJob:
    aliases = {}

    def start(self, cin, cout, sems):
        for cp in self._copies(cin, cout, sems):
            cp.start()

    def middle(self, cin, cout, sems):
        pass

    def finish(self, cin, cout, sems):
        copies = self._copies(cin, cout, sems)
        for cp in copies:
            cp.wait_recv()
        for cp in copies:
            cp.wait_send()


class _SwapJob(_PairedJob):
    def __init__(self, grads, kinds):
        self.inputs, self.kinds = list(grads), list(kinds)
        self.out_shapes, self.n_copies = [], []
        for g, kd in zip(grads, kinds):
            if kd:
                self.out_shapes.append(jax.ShapeDtypeStruct((1, g.shape[0] // 2, g.shape[1]), g.dtype))
                self.n_copies.append(1)
            else:
                n = g.shape[0] // N_CHIPS
                self.out_shapes.append(jax.ShapeDtypeStruct((N_CHIPS, n // 2, g.shape[1]), g.dtype))
                self.n_copies.append(N_CHIPS)
        total = sum(self.n_copies)
        self.sems = [pltpu.SemaphoreType.DMA((total,)), pltpu.SemaphoreType.DMA((total,))]

    def _copies(self, cin, cout, sems):
        send_sems, recv_sems = sems
        x, y, c = _position()
        copies = []
        for p, src_ref in enumerate(cin):
            for kk in range(self.n_copies[p]):
                if self.kinds[p]:
                    hr = src_ref.shape[0] // 2
                    src = src_ref.at[pl.ds((1 - c) * hr, hr), :]
                else:
                    n = src_ref.shape[0] // N_CHIPS
                    src = src_ref.at[pl.ds(kk * n + (1 - c) * (n // 2), n // 2), :]
                idx = len(copies)
                copies.append(pltpu.make_async_remote_copy(
                    src_ref=src, dst_ref=cout[p].at[kk], send_sem=send_sems.at[idx], recv_sem=recv_sems.at[idx],
                    device_id=(x, y, 1 - c), device_id_type=MESH))
        return copies


class _ExchangeJob(_PairedJob):
    def __init__(self, s16, kinds, sizes):
        self.inputs, self.kinds, self.sizes = list(s16), list(kinds), list(sizes)
        self.out_shapes = [jax.ShapeDtypeStruct((3, s.shape[1], n if kd else s.shape[2]), s.dtype)
                           for s, kd, n in zip(s16, kinds, sizes)]
        self.sems = [pltpu.SemaphoreType.DMA((3 * len(s16),)), pltpu.SemaphoreType.DMA((3 * len(s16),))]

    def _copies(self, cin, cout, sems):
        send_sems, recv_sems = sems
        x, y, c = _position()
        copies = []
        for p, src_ref in enumerate(cin):
            for j, chip in enumerate(_peer_chips(x, y)):
                kk = 2 * chip[0] + chip[1]
                n = self.sizes[p]
                src = src_ref.at[0, :, pl.ds(kk * n, n)] if self.kinds[p] else src_ref.at[kk]
                copies.append(pltpu.make_async_remote_copy(
                    src_ref=src, dst_ref=cout[p].at[j], send_sem=send_sems.at[3 * p + j],
                    recv_sem=recv_sems.at[3 * p + j], device_id=(*chip, c), device_id_type=MESH))
        return copies


class _ShareJob:
    def __init__(self, halves):
        self.inputs = list(halves)
        self.out_shapes = [jax.ShapeDtypeStruct(h.shape, h.dtype) for h in halves]
        self.aliases = {p: p for p in range(len(halves))}
        self.sems = [pltpu.SemaphoreType.DMA((len(halves),)), pltpu.SemaphoreType.DMA((len(halves),))]

    def _copies(self, cout, sems, half):
        send_sems, recv_sems = sems
        x, y, c = _position()
        h = c if half == "mine" else 1 - c
        return [pltpu.make_async_remote_copy(
            src_ref=o.at[h], dst_ref=o.at[h], send_sem=send_sems.at[p], recv_sem=recv_sems.at[p],
            device_id=(x, y, 1 - c), device_id_type=MESH) for p, o in enumerate(cout)]

    def start(self, cin, cout, sems):
        for cp in self._copies(cout, sems, "mine"):
            cp.start()

    def middle(self, cin, cout, sems):
        pass

    def finish(self, cin, cout, sems):
        for cp in self._copies(cout, sems, "theirs"):
            cp.wait_recv()
        for cp in self._copies(cout, sems, "mine"):
            cp.wait_send()


class _MultiJob:
    def __init__(self, jobs):
        self.jobs = jobs
        self.inputs = [a for j in jobs for a in j.inputs]
        self.out_shapes = [s for j in jobs for s in j.out_shapes]
        self.sems = [s for j in jobs for s in j.sems]
        self.aliases = {}
        i0 = o0 = 0
        for j in jobs:
            for i, o in j.aliases.items():
                self.aliases[i0 + i] = o0 + o
            i0 += len(j.inputs)
            o0 += len(j.out_shapes)

    def _parts(self, cin, cout, sems):
        i0 = o0 = s0 = 0
        for j in self.jobs:
            ni, no, ns = len(j.inputs), len(j.out_shapes), len(j.sems)
            yield j, cin[i0:i0 + ni], cout[o0:o0 + no], sems[s0:s0 + ns]
            i0, o0, s0 = i0 + ni, o0 + no, s0 + ns

    def start(self, cin, cout, sems):
        for j, a, b, s in self._parts(cin, cout, sems):
            j.start(a, b, s)

    def middle(self, cin, cout, sems):
        for j, a, b, s in self._parts(cin, cout, sems):
            j.middle(a, b, s)

    def finish(self, cin, cout, sems):
        for j, a, b, s in self._parts(cin, cout, sems):
            j.finish(a, b, s)


def _rope_tables(positions):
    half = ROT_DIM // 2
    inv_freq = jnp.power(jnp.float32(ROPE_THETA), -jnp.arange(0, ROT_DIM, 2, dtype=F32) / ROT_DIM)
    inv_head = jnp.concatenate([inv_freq, inv_freq, jnp.zeros((HEAD_DIM - ROT_DIM,), F32)])
    inv_lane = jnp.concatenate([inv_head] * (LANE // HEAD_DIM))
    ang = positions.astype(F32).reshape(-1)[:, None] * inv_lane[None, :]
    sin = jnp.sin(ang)
    dim = jnp.arange(LANE) % HEAD_DIM
    return jnp.cos(ang), jnp.where(dim < half, -sin, 0.0), jnp.where(dim >= half, sin, 0.0)


def kernel(x, c, positions, w_ada, b_ada, ffn1_w_gate_up, ffn1_w_down, ln1_g, ln1_b, w_in, conv_w, attn_sinks, w_out, ln2_g, ln2_b, ffn2_w_gate_up, ffn2_w_down, ln3_g, ln3_b, loss_target, m_w_ada, m_b_ada, m_ffn1_w_gate_up, m_ffn1_w_down, m_ln1_g, m_ln1_b, m_w_in, m_conv_w, m_attn_sinks, m_w_out, m_ln2_g, m_ln2_b, m_ffn2_w_gate_up, m_ffn2_w_down, m_ln3_g, m_ln3_b, v_w_ada, v_b_ada, v_ffn1_w_gate_up, v_ffn1_w_down, v_ln1_g, v_ln1_b, v_w_in, v_conv_w, v_attn_sinks, v_w_out, v_ln2_g, v_ln2_b, v_ffn2_w_gate_up, v_ffn2_w_down, v_ln3_g, v_ln3_b):
    d = D_MODEL
    nb, seq, _ = x.shape
    t = nb * seq
    f = ffn1_w_down.shape[1] * N_CHIPS
    ax, ay, ac = _position()
    chip = 2 * ax + ay
    dev = 2 * chip + ac
    pos = jnp.stack([ax, ay, ac]).astype(jnp.int32)

    x2 = x.reshape(t, d)
    tgt2 = loss_target.reshape(t, d)
    ln1 = jnp.concatenate([ln1_g, ln1_b], axis=0)
    ln2 = jnp.concatenate([ln2_g, ln2_b], axis=0)
    ln3 = jnp.concatenate([ln3_g, ln3_b], axis=0)
    sinks = attn_sinks.reshape(N_Q_HEADS)
    cos_t, sa_t, sb_t = _rope_tables(positions)

    gu_cuts = [0, 176, 352, d // 2]
    gu_part = lambda buf, s: (buf, True, gu_cuts[s], gu_cuts[s + 1] - gu_cuts[s])
    chip_arr = jnp.reshape(chip, (1,)).astype(jnp.int32)
    b_gu1 = _cast_into(ffn1_w_gate_up[0], chip_arr, True, name="cast_gu1")

    n_ada = w_ada.shape[2]
    c_all, (b_gu1,) = _allgather8(c.reshape(nb * d // LANE, LANE), name="gather_c", comm=_GatherJob([gu_part(b_gu1, 0)]))
    c_all = c_all.reshape(N_DEV * nb, d)
    b_shard = lax.dynamic_slice(b_ada, (0, chip * n_ada), (1, n_ada))
    mod_part, (b_gu1,) = _ada_fwd(c_all, w_ada[0], b_shard, name="ada_fwd", comm=_GatherJob([gu_part(b_gu1, 1)]))
    conv_rows = jnp.pad(conv_w[0], ((0, 5), (0, n_ada - conv_w.shape[2])))
    part = jnp.concatenate([mod_part, conv_rows], axis=0)
    parts, (wgu1,) = _allgather8(part, name="gather_mod", comm=_GatherJob([gu_part(b_gu1, 2)]))
    parts = parts.reshape(N_DEV, N_DEV * nb + 8, n_ada)
    mod_all = jnp.concatenate([parts[2 * k, :N_DEV * nb, :] for k in range(N_CHIPS)], axis=1)
    mod = lax.dynamic_slice(mod_all, (dev * nb, 0), (nb, N_MOD * d)).reshape(nb, N_MOD, d)
    cw_full = jnp.concatenate([parts[2 * k, N_DEV * nb:, :conv_w.shape[2]] for k in range(N_CHIPS)], axis=1)

    b_d1 = _cast_into(ffn1_w_down[0], chip_arr, False, name="cast_d1")
    b_in = _cast_into(w_in[0].T, chip_arr, False, name="cast_in")
    b_out = _cast_into(w_out[0], chip_arr, False, name="cast_out")
    b_gu2 = _cast_into(ffn2_w_gate_up[0], chip_arr, True, name="cast_gu2")
    b_d2 = _cast_into(ffn2_w_down[0], chip_arr, False, name="cast_d2")
    n_gu, n_d, n_in, n_out = (ffn1_w_gate_up.shape[2], ffn1_w_down.shape[1], w_in.shape[2], w_out.shape[1])

    def whole(buf, col_kind):
        return (buf, col_kind, 0, buf.shape[0] // (2 if col_kind else 2 * N_CHIPS))

    (h1, a1, gu1), (wd1, wout) = _ffn_up(x2, ln1, mod, wgu1, seq=seq, sc_idx=1, sh_idx=0, use_ln=False,
                                         name="ffn1_up", comm=_GatherJob([whole(b_d1, False), whole(b_out, False)]))
    (f1, xhat1, rstd1), (win_t,) = _ffn_down_ln(a1, wd1, x2, ln1, mod, seq=seq, gate_idx=2, use_ln=False,
                                                name="ffn1_down", comm=_GatherJob([whole(b_in, False)]))
    (h2, q, k, v, ubc), (b_gu2,) = _in_proj(
        xhat1, ln1, mod, win_t, cos_t, sa_t, sb_t, seq=seq, sc_idx=4, sh_idx=3, name="in_proj",
        comm=_GatherJob([gu_part(b_gu2, 0)]))
    attn, (b_gu2,) = _attention(q, k, v, sinks, seq=seq, name="attention", comm=_GatherJob([gu_part(b_gu2, 1)]))
    (mixin, mix, xhat2, rstd2), (wgu2,) = _out_proj(
        attn, ubc, cw_full, wout, xhat1, ln1, mod, seq=seq, gate_idx=5, name="out_proj",
        comm=_GatherJob([gu_part(b_gu2, 2)]))
    (h3, a3, gu3), (wd2,) = _ffn_up(xhat2, ln2, mod, wgu2, seq=seq, sc_idx=7, sh_idx=6, use_ln=True, name="ffn2_up",
                                    comm=_GatherJob([whole(b_d2, False)]))
    dr3, df3, loss_cols, dln3g, dln3b, dgate3 = _ffn_down_loss(
        a3, wd2, xhat2, ln2, mod, ln3, tgt2, seq=seq, gate_idx=8, name="ffn2_down_loss")

    def pair_sum(g, r3, col_kind, name_, comm=None):
        if col_kind:
            g3 = g.reshape(2, g.shape[0] // 2, g.shape[1])
            blk_of = lambda p_, pos_: pos_[2]
        else:
            g3 = g.reshape(2 * N_CHIPS, g.shape[0] // (2 * N_CHIPS), g.shape[1])
            blk_of = lambda p_, pos_: 2 * p_ + pos_[2]
        return _sum_pair(pos, g3, r3, blk_of, name=name_, comm=comm)

    dgu3 = _ffn_bwd_act(df3, wd2, gu3, seq=seq, name="ffn2_bwd_act")
    g_wd2 = _matmul_tn(a3, df3, tmm=f // 2, tnn=d, name="grad_wd2")
    (s32_gu2, s16_gu2), (sib_d2,) = _grad_chip_sum(pos, h3, dgu3, name="grad_wgu2", comm=_SwapJob([g_wd2], [False]))
    s32_d2, s16_d2 = pair_sum(g_wd2, sib_d2, False, "sum_pair_d2")
    (dr2, dmix, dsc3, dsh3, dgate2, dln2g, dln2b), (recv_gu2,) = _bwd_in(
        dgu3, wgu2, dr3, xhat2, rstd2, ln2, mod, mix, seq=seq, w_is_nt=True, sc_idx=7, gate_idx=5,
        branch_scale=1.0, final=False, name="ffn2_bwd_in", comm=_ExchangeJob([s16_gu2], [True], [n_gu]))
    g_wout = _matmul_tn(mixin, dmix, tmm=d, tnn=d, name="grad_wout")
    dmixin = _matmul_nt_bf16(dmix, wout, seq=seq, name="out_proj_bwd")
    (dq, dkp, dkc, dvp, dvc, dsink), (recv_d2, sib_out) = _attention_bwd(
        q, k, v, dmixin, sinks, seq=seq, name="attention_bwd",
        comm=_MultiJob([_ExchangeJob([s16_d2], [False], [n_d]), _SwapJob([g_wout], [False])]))
    s32_out, s16_out = pair_sum(g_wout, sib_out, False, "sum_pair_out")
    (dproj, dcw), (recv_out,) = _mix_bwd_assemble(
        dq, dkp, dkc, dvp, dvc, cos_t, sa_t, sb_t, dmixin, ubc, cw_full, seq=seq, name="mix_bwd",
        comm=_ExchangeJob([s16_out], [False], [n_out]))
    g_win_t = _matmul_tn(dproj, h2, tmm=IN_WIDTH // 2, tnn=d, name="grad_win")
    (dr1, df1, dsc2, dsh2, dgate1, dln1g, dln1b), (sib_in,) = _bwd_in(
        dproj, win_t, dr2, xhat1, rstd1, ln1, mod, f1, seq=seq, w_is_nt=False, sc_idx=4, gate_idx=2,
        branch_scale=0.5, final=False, name="in_proj_bwd", comm=_SwapJob([g_win_t], [False]))
    s32_in, s16_in = pair_sum(g_win_t, sib_in, False, "sum_pair_in")
    g_wd1, (recv_in,) = _matmul_tn(a1, df1, tmm=f // 2, tnn=d, name="grad_wd1",
                                   comm=_ExchangeJob([s16_in], [False], [n_in]))
    dgu1, (sib_d1,) = _ffn_bwd_act(df1, wd1, gu1, seq=seq, name="ffn1_bwd_act", comm=_SwapJob([g_wd1], [False]))
    s32_d1, s16_d1 = pair_sum(g_wd1, sib_d1, False, "sum_pair_d1")
    (s32_gu1, s16_gu1), (recv_d1,) = _grad_chip_sum(pos, h1, dgu1, name="grad_wgu1",
                                                    comm=_ExchangeJob([s16_d1], [False], [n_d]))

    def final_half(s32_, recv_, col_kind, n_shard, name_):
        return _sum_final(pos, s32_, recv_, col_kind=col_kind, n_shard=n_shard, name=name_)

    early = [final_half(s32_gu2, recv_gu2, True, n_gu, "sum_final_gu2"),
             final_half(s32_d2, recv_d2, False, n_d, "sum_final_d2"),
             final_half(s32_out, recv_out, False, n_out, "sum_final_out"),
             final_half(s32_in, recv_in, False, n_in, "sum_final_in"),
             final_half(s32_d1, recv_d1, False, n_d, "sum_final_d1")]
    (grad_x, dsc1, dsh1), (recv_gu1, full_gu2, full_d2, full_out, full_in, full_d1) = _bwd_in(
        dgu1, wgu1, dr1, x2, None, None, mod, None, seq=seq, w_is_nt=True, sc_idx=1, gate_idx=None,
        branch_scale=None, final=True, name="ffn1_bwd_in",
        comm=_MultiJob([_ExchangeJob([s16_gu1], [True], [n_gu]), _ShareJob(early)]))
    late = [final_half(s32_gu1, recv_gu1, True, n_gu, "sum_final_gu1")]

    dmod = jnp.concatenate([dsh1, dsc1, dgate1, dsh2, dsc2, dgate2, dsh3, dsc3, dgate3], axis=1)
    loss_row = jnp.sum(loss_cols, axis=1, keepdims=True) * (0.5 / d)
    lane_row = lambda a: jnp.pad(a, ((0, 0), (0, d - a.shape[1])))
    block = jnp.concatenate(
        [dmod.reshape(nb * N_MOD, d), dln1g, dln1b, dln2g, dln2b, dln3g, dln3b,
         lane_row(dcw[0:3, :]), lane_row(dsink[:, 0:1].reshape(1, N_Q_HEADS)), lane_row(loss_row)], axis=0)
    block = jnp.pad(block, ((0, SMALL_ROWS - block.shape[0]), (0, 0)))
    gathered, (full_gu1,) = _allgather8(block, name="gather_small", comm=_ShareJob(late))
    gathered = gathered.reshape(N_DEV, SMALL_ROWS, d)
    dmod_all = gathered[:, :nb * N_MOD, :].reshape(N_DEV * nb, N_MOD * d)
    dmod_shard = lax.dynamic_slice(dmod_all, (0, chip * n_ada), (N_DEV * nb, n_ada))
    small, g_w_ada, g_b_ada = _small_finish(gathered, dmod_all, dmod_shard, c_all.T, name="small_finish")
    r0 = nb * N_MOD
    loss = small[r0 + 10, 0]
    g_ln = [small[r0 + i:r0 + i + 1, :] for i in range(6)]
    g_cw_full = small[r0 + 6:r0 + 9, :CONV_WIDTH]
    g_conv = lax.dynamic_slice(g_cw_full, (0, chip * conv_w.shape[2]), (3, conv_w.shape[2]))
    g_sinks = small[r0 + 9:r0 + 10, :N_Q_HEADS]

    def flat2(a):
        return a.reshape(-1, a.shape[-1])

    def unhalve(a):
        return a.reshape(2 * a.shape[1], a.shape[2])

    results = {}

    def adamw(name_, w_, g_, m_, v_):
        g2 = flat2(g_)
        dl, nm, nv = _adamw(flat2(w_), g2, flat2(m_), flat2(v_), name="adamw_" + name_)
        results[name_] = tuple(a.reshape(w_.shape) for a in (g2, dl, nm, nv))

    adamw("w_ada", w_ada, g_w_ada, m_w_ada, v_w_ada)
    adamw("ffn2_w_gate_up", ffn2_w_gate_up, unhalve(full_gu2), m_ffn2_w_gate_up, v_ffn2_w_gate_up)
    adamw("ffn2_w_down", ffn2_w_down, unhalve(full_d2), m_ffn2_w_down, v_ffn2_w_down)
    adamw("w_out", w_out, unhalve(full_out), m_w_out, v_w_out)
    adamw("w_in", w_in, unhalve(full_in).T, m_w_in, v_w_in)
    adamw("ffn1_w_gate_up", ffn1_w_gate_up, unhalve(full_gu1), m_ffn1_w_gate_up, v_ffn1_w_gate_up)
    adamw("ffn1_w_down", ffn1_w_down, unhalve(full_d1), m_ffn1_w_down, v_ffn1_w_down)
    adamw("b_ada", b_ada, g_b_ada, m_b_ada, v_b_ada)
    adamw("ln1_g", ln1_g, g_ln[0], m_ln1_g, v_ln1_g)
    adamw("ln1_b", ln1_b, g_ln[1], m_ln1_b, v_ln1_b)
    adamw("ln2_g", ln2_g, g_ln[2], m_ln2_g, v_ln2_g)
    adamw("ln2_b", ln2_b, g_ln[3], m_ln2_b, v_ln2_b)
    adamw("ln3_g", ln3_g, g_ln[4], m_ln3_g, v_ln3_g)
    adamw("ln3_b", ln3_b, g_ln[5], m_ln3_b, v_ln3_b)
    adamw("conv_w", conv_w, g_conv, m_conv_w, v_conv_w)
    adamw("attn_sinks", attn_sinks, g_sinks, m_attn_sinks, v_attn_sinks)
    order = ["w_ada", "b_ada", "ffn1_w_gate_up", "ffn1_w_down", "ln1_g", "ln1_b", "w_in", "conv_w", "attn_sinks",
             "w_out", "ln2_g", "ln2_b", "ffn2_w_gate_up", "ffn2_w_down", "ln3_g", "ln3_b"]
    return (loss, grad_x.reshape(x.shape), *[results[n_][0] for n_ in order], *[results[n_][1] for n_ in order],
            *[results[n_][2] for n_ in order], *[results[n_][3] for n_ in order])
```

```python
import jax
import jax.numpy as jnp
from jax import lax
from jax.experimental import pallas as pl
from jax.experimental.pallas import tpu as pltpu

F32 = jnp.float32
BF16 = jnp.bfloat16
MESH = pl.DeviceIdType.MESH

D_MODEL = 1024
HEAD_DIM = 64
ATTN_WIDTH = 512
CONV_WIDTH = 512
N_Q_HEADS = 8
N_KV_HEADS = 2
GQA_GROUP = 4
KV_WIDTH = 128
WINDOW = 128
BLOCK = 128
ROT_DIM = 16
ROPE_THETA = 500000.0
N_MOD = 9
LN_EPS = 1e-5
DN_ALPHA = 2.0 ** 0.25
IN_WIDTH = 2304
N_CHIPS = 4
N_DEV = 8
SMALL_ROWS = 32

ADAM_LR = 0.001
ADAM_B1 = 0.9
ADAM_B2 = 0.999
ADAM_EPS = 1e-08
ADAM_WD = 0.01
ADAM_STEP = 10

LANE = 128
HALO = 16
COL_CHUNK = 256
VMEM_LIMIT = 56 * 1024 * 1024


def _params(sem=None, vmem=True):
    return pltpu.CompilerParams(dimension_semantics=sem, vmem_limit_bytes=VMEM_LIMIT if vmem else None)


def _sigmoid(g):
    return 0.5 * jnp.tanh(0.5 * g) + 0.5


def _row_sum(v):
    return jnp.sum(v, axis=0, keepdims=True)


ROW_CHUNK = 16
EPILOGUE_UNROLL = 8


def _fold8(v):
    return v[0:8, :] + v[8:16, :]


def _row_chunk_loop(n_rows, step, init):
    per_iter = ROW_CHUNK * EPILOGUE_UNROLL
    assert n_rows % per_iter == 0, n_rows

    def body(it, carry):
        for s in range(EPILOGUE_UNROLL):
            start = pl.multiple_of(it * per_iter + s * ROW_CHUNK, ROW_CHUNK)
            carry = step(pl.ds(start, ROW_CHUNK), carry)
        return carry

    return lax.fori_loop(0, n_rows // per_iter, body, init)


def _ln_stats(r):
    mu = jnp.mean(r, axis=-1, keepdims=True)
    rc = r - mu
    var = jnp.mean(rc * rc, axis=-1, keepdims=True)
    rstd = lax.rsqrt(var + LN_EPS)
    return rc * rstd, rstd


def _ln_bwd(dxo, xhat, rstd, g):
    dxhat = dxo * g
    m1 = jnp.mean(dxhat, axis=-1, keepdims=True)
    m2 = jnp.mean(dxhat * xhat, axis=-1, keepdims=True)
    return rstd * (dxhat - m1 - xhat * m2)


def _dot_nt(a, b):
    return lax.dot_general(a, b, (((1,), (1,)), ((), ())), preferred_element_type=F32)


def _dot_tn(a, b):
    return lax.dot_general(a, b, (((0,), (0,)), ((), ())), preferred_element_type=F32)


def _full(shape):
    nd = len(shape)
    return pl.BlockSpec(shape, lambda *_: (0,) * nd)


def _resident(shape):
    nd = len(shape)
    return pl.BlockSpec(shape, lambda *_: (0,) * nd, pipeline_mode=pl.Buffered(1))


ANY_SPEC = pl.BlockSpec(memory_space=pl.ANY)


def _pcall(body, *, name, grid, in_specs, out_specs, out_shape, args, scratch_shapes=(), comm=None, prefetch=None):
    single = not isinstance(out_shape, (list, tuple))
    out_specs = [out_specs] if single else list(out_specs)
    out_shape = [out_shape] if single else list(out_shape)
    in_specs = list(in_specs)
    scratch_shapes = list(scratch_shapes)
    sem = ("arbitrary",) * len(grid)
    n_pre = 0 if prefetch is None else 1
    pre_args = () if prefetch is None else (prefetch,)

    def call(fn, ins_, outs_, shapes_, scratch_, aliases_, operands):
        if prefetch is None:
            return pl.pallas_call(fn, name=name, grid=grid, in_specs=ins_, out_specs=outs_, out_shape=shapes_,
                                  scratch_shapes=scratch_, input_output_aliases=aliases_,
                                  compiler_params=_params(sem))(*operands)
        spec = pltpu.PrefetchScalarGridSpec(num_scalar_prefetch=1, grid=grid, in_specs=ins_, out_specs=outs_,
                                            scratch_shapes=scratch_)
        return pl.pallas_call(fn, name=name, grid_spec=spec, out_shape=shapes_,
                              input_output_aliases={n_pre + i: o for i, o in aliases_.items()},
                              compiler_params=_params(sem))(*pre_args, *operands)

    if comm is None:
        res = call(body, in_specs, out_specs, out_shape, scratch_shapes, {}, args)
        return res[0] if single else res
    n_in, n_out, n_scr = len(in_specs), len(out_specs), len(scratch_shapes)
    nci, nco = len(comm.inputs), len(comm.out_shapes)
    n_steps = 1
    for g in grid:
        n_steps *= g
    staged = n_steps >= 8
    middle_step = (n_steps * 5) // 8 - 1
    late_step = n_steps - 1 - max(1, n_steps // 8)

    def wrapped(*refs):
        pre, refs = refs[:n_pre], refs[n_pre:]
        ins, refs = refs[:n_in], refs[n_in:]
        cin, refs = refs[:nci], refs[nci:]
        outs, refs = refs[:n_out], refs[n_out:]
        cout, refs = refs[:nco], refs[nco:]
        scr, csems = refs[:n_scr], refs[n_scr:]
        step = pl.program_id(0)
        for ax in range(1, len(grid)):
            step = step * grid[ax] + pl.program_id(ax)

        @pl.when(step == 0)
        def _():
            comm.start(cin, cout, csems)

        body(*pre, *ins, *outs, *scr)

        if staged:
            @pl.when(step == middle_step)
            def _():
                comm.middle(cin, cout, csems)

            @pl.when(step == late_step)
            def _():
                comm.late(cin, cout, csems)

        @pl.when(step == n_steps - 1)
        def _():
            if not staged:
                comm.middle(cin, cout, csems)
                comm.late(cin, cout, csems)
            comm.finish(cin, cout, csems)

    res = call(wrapped, in_specs + [ANY_SPEC] * nci, out_specs + [ANY_SPEC] * nco,
               out_shape + list(comm.out_shapes), scratch_shapes + list(comm.sems),
               {n_in + i: n_out + o for i, o in comm.aliases.items()}, (*args, *comm.inputs))
    main = res[:n_out]
    return (main[0] if single else main), list(res[n_out:])


def _comm_call(job, *, name):
    nci, nco = len(job.inputs), len(job.out_shapes)

    def body(*refs):
        cin, refs = refs[:nci], refs[nci:]
        cout, csems = refs[:nco], refs[nco:]
        job.start(cin, cout, csems)
        job.middle(cin, cout, csems)
        job.late(cin, cout, csems)
        job.finish(cin, cout, csems)

    return pl.pallas_call(
        body, name=name, out_shape=list(job.out_shapes), in_specs=[ANY_SPEC] * nci, out_specs=[ANY_SPEC] * nco,
        scratch_shapes=list(job.sems), input_output_aliases=dict(job.aliases))(*job.inputs)


def _ffn_up(xin, lnp, mod, w, *, seq, sc_idx, sh_idx, use_ln, name, comm=None):
    t, d = xin.shape
    f = w.shape[1] // 2
    tm = min(512, seq)
    tpb = seq // tm
    ch = min(COL_CHUNK, f)

    def body(x_ref, ln_ref, mod_ref, w_ref, h_ref, a_ref, gu_ref):
        x = x_ref[...]
        if use_ln:
            x = x * ln_ref[0:1, :] + ln_ref[1:2, :]
        h = x * (1.0 + mod_ref[0, sc_idx:sc_idx + 1, :]) + mod_ref[0, sh_idx:sh_idx + 1, :]
        hb = h.astype(BF16)
        h_ref[...] = hb
        for j in range(f // ch):
            g = jnp.dot(hb, w_ref[:, j * ch:(j + 1) * ch], preferred_element_type=F32)
            u = jnp.dot(hb, w_ref[:, f + j * ch:f + (j + 1) * ch], preferred_element_type=F32)
            s = _sigmoid(g)
            silu = g * s
            a_ref[:, j * ch:(j + 1) * ch] = (silu * u).astype(BF16)
            gu_ref[:, j * ch:(j + 1) * ch] = (u * (s + silu * (1.0 - s))).astype(BF16)
            gu_ref[:, f + j * ch:f + (j + 1) * ch] = silu.astype(BF16)

    return _pcall(
        body, name=name, grid=(t // tm,),
        in_specs=[pl.BlockSpec((tm, d), lambda i: (i, 0)), _full((2, d)),
                  pl.BlockSpec((1, N_MOD, d), lambda i: (i // tpb, 0, 0)), _resident((d, 2 * f))],
        out_specs=[pl.BlockSpec((tm, d), lambda i: (i, 0)), pl.BlockSpec((tm, f), lambda i: (i, 0)),
                   pl.BlockSpec((tm, 2 * f), lambda i: (i, 0))],
        out_shape=[jax.ShapeDtypeStruct((t, d), BF16), jax.ShapeDtypeStruct((t, f), BF16),
                   jax.ShapeDtypeStruct((t, 2 * f), BF16)],
        args=(xin, lnp, mod, w), comm=comm)


def _ffn_down_ln(a, wd, xin, lnp_in, mod, *, seq, gate_idx, use_ln, name, comm=None):
    t, f = a.shape
    d = wd.shape[1]
    tm = min(512, seq)
    tpb = seq // tm

    def body(a_ref, wd_ref, x_ref, ln_ref, mod_ref, f_ref, xhat_ref, rstd_ref, acc):
        av = a_ref[...]
        for j in range(d // COL_CHUNK):
            acc[:, j * COL_CHUNK:(j + 1) * COL_CHUNK] = jnp.dot(
                av, wd_ref[:, j * COL_CHUNK:(j + 1) * COL_CHUNK], preferred_element_type=F32)
        scale = 0.5 * (1.0 + mod_ref[0, gate_idx:gate_idx + 1, :])

        fo = acc[...]
        x = x_ref[...]
        if use_ln:
            x = x * ln_ref[0:1, :] + ln_ref[1:2, :]
        xhat, rstd = _ln_stats(DN_ALPHA * x + scale * fo)
        f_ref[...] = fo.astype(BF16)
        xhat_ref[...] = xhat
        rstd_ref[...] = rstd

    return _pcall(
        body, name=name, grid=(t // tm,),
        in_specs=[pl.BlockSpec((tm, f), lambda i: (i, 0)), _resident((f, d)),
                  pl.BlockSpec((tm, d), lambda i: (i, 0)), _full((2, d)),
                  pl.BlockSpec((1, N_MOD, d), lambda i: (i // tpb, 0, 0))],
        out_specs=[pl.BlockSpec((tm, d), lambda i: (i, 0)), pl.BlockSpec((tm, d), lambda i: (i, 0)),
                   pl.BlockSpec((tm, 1), lambda i: (i, 0))],
        out_shape=[jax.ShapeDtypeStruct((t, d), BF16), jax.ShapeDtypeStruct((t, d), F32),
                   jax.ShapeDtypeStruct((t, 1), F32)],
        scratch_shapes=[pltpu.VMEM((tm, d), F32)],
        args=(a, wd, xin, lnp_in, mod), comm=comm)


def _ffn_down_loss(a, wd, xhat_in, lnp_in, mod, lnp_out, tgt, *, seq, gate_idx, name):
    t, f = a.shape
    d = wd.shape[1]
    nb = t // seq
    tm = min(512, seq)
    tpb = seq // tm

    def body(a_ref, wd_ref, x_ref, lnin_ref, mod_ref, lnout_ref, tgt_ref,
             dr_ref, df_ref, loss_ref, dg_ref, db_ref, dgate_ref, acc):
        i = pl.program_id(0)
        av = a_ref[...]
        for j in range(d // COL_CHUNK):
            acc[:, j * COL_CHUNK:(j + 1) * COL_CHUNK] = jnp.dot(
                av, wd_ref[:, j * COL_CHUNK:(j + 1) * COL_CHUNK], preferred_element_type=F32)
        scale = 0.5 * (1.0 + mod_ref[0, gate_idx:gate_idx + 1, :])
        g_in, b_in = lnin_ref[0:1, :], lnin_ref[1:2, :]
        g_out, b_out = lnout_ref[0:1, :], lnout_ref[1:2, :]

        def chunk(rows, carry):
            s_loss, s_dg, s_db, s_gate = carry
            fo = acc[rows, :]
            xhat, rstd = _ln_stats(DN_ALPHA * (x_ref[rows, :] * g_in + b_in) + scale * fo)
            e = xhat * g_out + b_out - tgt_ref[rows, :]
            dy = e * (1.0 / d)
            dr = _ln_bwd(dy, xhat, rstd, g_out)
            dr_ref[rows, :] = dr
            df_ref[rows, :] = (scale * dr).astype(BF16)
            return (s_loss + _fold8(e * e), s_dg + _fold8(dy * xhat), s_db + _fold8(dy),
                    s_gate + _fold8(0.5 * fo * dr))

        zero = jnp.zeros((8, d), F32)
        s_loss, s_dg, s_db, s_gate = _row_chunk_loop(tm, chunk, (zero, zero, zero, zero))

        @pl.when(i == 0)
        def _():
            loss_ref[...] = jnp.zeros_like(loss_ref)
            dg_ref[...] = jnp.zeros_like(dg_ref)
            db_ref[...] = jnp.zeros_like(db_ref)

        @pl.when(i % tpb == 0)
        def _():
            dgate_ref[...] = jnp.zeros_like(dgate_ref)

        loss_ref[...] += _row_sum(s_loss)
        dg_ref[...] += _row_sum(s_dg)
        db_ref[...] += _row_sum(s_db)
        dgate_ref[0] += _row_sum(s_gate)

    return pl.pallas_call(
        body, name=name, grid=(t // tm,), scratch_shapes=[pltpu.VMEM((tm, d), F32)],
        in_specs=[pl.BlockSpec((tm, f), lambda i: (i, 0)), _resident((f, d)),
                  pl.BlockSpec((tm, d), lambda i: (i, 0)), _full((2, d)),
                  pl.BlockSpec((1, N_MOD, d), lambda i: (i // tpb, 0, 0)), _full((2, d)),
                  pl.BlockSpec((tm, d), lambda i: (i, 0))],
        out_specs=[pl.BlockSpec((tm, d), lambda i: (i, 0)), pl.BlockSpec((tm, d), lambda i: (i, 0)),
                   _full((1, d)), _full((1, d)), _full((1, d)),
                   pl.BlockSpec((1, 1, d), lambda i: (i // tpb, 0, 0))],
        out_shape=[jax.ShapeDtypeStruct((t, d), F32), jax.ShapeDtypeStruct((t, d), BF16),
                   jax.ShapeDtypeStruct((1, d), F32), jax.ShapeDtypeStruct((1, d), F32),
                   jax.ShapeDtypeStruct((1, d), F32), jax.ShapeDtypeStruct((nb, 1, d), F32)],
        compiler_params=_params(("arbitrary",)),
    )(a, wd, xhat_in, lnp_in, mod, lnp_out, tgt)


def _rope(v, cos, sa, sb):
    return v * cos + pltpu.roll(v, LANE - ROT_DIM // 2, 1) * sa + pltpu.roll(v, ROT_DIM // 2, 1) * sb


def _rope_t(dy, cos, sa, sb):
    return dy * cos + pltpu.roll(dy * sa, ROT_DIM // 2, 1) + pltpu.roll(dy * sb, LANE - ROT_DIM // 2, 1)


def _in_proj(xhat, lnp, mod, w_t, cos, sa, sb, *, seq, sc_idx, sh_idx, name, comm=None):
    t, d = xhat.shape
    tm = min(512, seq)
    tpb = seq // tm
    n_conv = 3 * CONV_WIDTH

    def body(x_ref, ln_ref, mod_ref, w_ref, cos_ref, sa_ref, sb_ref, h_ref, q_ref, k_ref, v_ref, ubc_ref):
        x = x_ref[...] * ln_ref[0:1, :] + ln_ref[1:2, :]
        h = x * (1.0 + mod_ref[0, sc_idx:sc_idx + 1, :]) + mod_ref[0, sh_idx:sh_idx + 1, :]
        hb = h.astype(BF16)
        h_ref[...] = hb
        cos_t, sa_t, sb_t = cos_ref[...], sa_ref[...], sb_ref[...]
        for j in range(ATTN_WIDTH // COL_CHUNK):
            p = _dot_nt(hb, w_ref[j * COL_CHUNK:(j + 1) * COL_CHUNK, :])
            for s in range(COL_CHUNK // LANE):
                q_ref[:, j * COL_CHUNK + s * LANE:j * COL_CHUNK + (s + 1) * LANE] = _rope(
                    p[:, s * LANE:(s + 1) * LANE], cos_t, sa_t, sb_t).astype(BF16)
        p = _dot_nt(hb, w_ref[ATTN_WIDTH:ATTN_WIDTH + 2 * KV_WIDTH, :])
        k_ref[...] = _rope(p[:, 0:KV_WIDTH], cos_t, sa_t, sb_t).astype(BF16)
        v_ref[...] = p[:, KV_WIDTH:].astype(BF16)
        base = ATTN_WIDTH + 2 * KV_WIDTH
        for j in range(n_conv // COL_CHUNK):
            ubc_ref[:, j * COL_CHUNK:(j + 1) * COL_CHUNK] = _dot_nt(
                hb, w_ref[base + j * COL_CHUNK:base + (j + 1) * COL_CHUNK, :]).astype(BF16)

    row = lambda w: pl.BlockSpec((tm, w), lambda i: (i, 0))
    return _pcall(
        body, name=name, grid=(t // tm,),
        in_specs=[row(d), _full((2, d)), pl.BlockSpec((1, N_MOD, d), lambda i: (i // tpb, 0, 0)),
                  _resident((IN_WIDTH, d)), row(LANE), row(LANE), row(LANE)],
        out_specs=[row(d), row(ATTN_WIDTH), row(KV_WIDTH), row(KV_WIDTH), row(n_conv)],
        out_shape=[jax.ShapeDtypeStruct((t, d), BF16), jax.ShapeDtypeStruct((t, ATTN_WIDTH), BF16),
                   jax.ShapeDtypeStruct((t, KV_WIDTH), BF16), jax.ShapeDtypeStruct((t, KV_WIDTH), BF16),
                   jax.ShapeDtypeStruct((t, n_conv), BF16)],
        args=(xhat, lnp, mod, w_t, cos, sa, sb), comm=comm)


ATTN_TILE_BLOCKS = 2


def _attn_sub_block(s, tile, nblk, kp_ref, kc_ref, vp_ref, vc_ref):
    rows = slice(s * BLOCK, (s + 1) * BLOCK)
    if s == 0:
        first = ((tile * ATTN_TILE_BLOCKS) % nblk) == 0
        return rows, (kp_ref, slice(0, BLOCK)), (kc_ref, rows), (vp_ref, slice(0, BLOCK)), (vc_ref, rows), first
    before = slice((s - 1) * BLOCK, s * BLOCK)
    return rows, (kc_ref, before), (kc_ref, rows), (vc_ref, before), (vc_ref, rows), False


def _attn_group(q_ref, rows, k_prev, k_cur, v_prev, v_cur, sink_ref, g, first):
    lo, hi = g * HEAD_DIM, (g + 1) * HEAD_DIM
    kk = jnp.concatenate([k_prev[0][k_prev[1], lo:hi], k_cur[0][k_cur[1], lo:hi]], axis=0)
    vv = jnp.concatenate([v_prev[0][v_prev[1], lo:hi], v_cur[0][v_cur[1], lo:hi]], axis=0)
    qs = jnp.concatenate([q_ref[rows, (GQA_GROUP * g + j) * HEAD_DIM:(GQA_GROUP * g + j + 1) * HEAD_DIM]
                          for j in range(GQA_GROUP)], axis=0)
    cols = GQA_GROUP * BLOCK
    ki = lax.broadcasted_iota(jnp.int32, (2 * BLOCK, cols), 0)
    col = lax.broadcasted_iota(jnp.int32, (2 * BLOCK, cols), 1)
    diff = (col & (BLOCK - 1)) + BLOCK - ki
    valid = (diff >= 0) & (diff < WINDOW) & ((ki >= BLOCK) | jnp.logical_not(first))
    s = _dot_nt(kk, qs) * (HEAD_DIM ** -0.5)
    s = jnp.where(valid, s, -1e30)
    hcol = lax.broadcasted_iota(jnp.int32, (1, cols), 1)
    sink = jnp.zeros((1, cols), F32)
    for j in range(GQA_GROUP):
        sink = jnp.where(hcol // BLOCK == j, sink_ref[GQA_GROUP * g + j], sink)
    m = jnp.maximum(jnp.max(s, axis=0, keepdims=True), sink)
    p = jnp.exp(s - m)
    ps = jnp.exp(sink - m)
    inv = 1.0 / (jnp.sum(p, axis=0, keepdims=True) + ps)
    return qs, kk, vv, p * inv, ps * inv


def _heads_to_lanes(x_t):
    return jnp.concatenate([x_t[:, j * BLOCK:(j + 1) * BLOCK].T for j in range(GQA_GROUP)], axis=1)


def _attention(q, k, v, sinks, *, seq, name, comm=None):
    t = q.shape[0]
    nblk = seq // BLOCK
    tile = ATTN_TILE_BLOCKS * BLOCK

    def body(q_ref, kp_ref, kc_ref, vp_ref, vc_ref, sink_ref, o_ref):
        for s in range(ATTN_TILE_BLOCKS):
            rows, k_prev, k_cur, v_prev, v_cur, first = _attn_sub_block(
                s, pl.program_id(0), nblk, kp_ref, kc_ref, vp_ref, vc_ref)
            outs = []
            for g in range(N_KV_HEADS):
                _, _, vv, pn, _ = _attn_group(q_ref, rows, k_prev, k_cur, v_prev, v_cur, sink_ref, g, first)
                outs.append(_heads_to_lanes(_dot_tn(vv, pn.astype(BF16))))
            o_ref[rows, :] = jnp.concatenate(outs, axis=1).astype(BF16)

    cur = lambda w: pl.BlockSpec((tile, w), lambda n: (n, 0))
    prev = lambda w: pl.BlockSpec((BLOCK, w), lambda n: (jnp.maximum(n * ATTN_TILE_BLOCKS - 1, 0), 0))
    return _pcall(
        body, name=name, grid=(t // tile,),
        in_specs=[cur(ATTN_WIDTH), prev(KV_WIDTH), cur(KV_WIDTH), prev(KV_WIDTH), cur(KV_WIDTH),
                  pl.BlockSpec(memory_space=pltpu.SMEM)],
        out_specs=cur(ATTN_WIDTH),
        out_shape=jax.ShapeDtypeStruct((t, ATTN_WIDTH), BF16),
        args=(q, k, k, v, v, sinks), comm=comm)


def _out_proj(attn, ubc, cw, wout, xhat_in, lnp_in, mod, *, seq, gate_idx, name, comm=None):
    t, d = xhat_in.shape
    tm = min(512, seq)
    tpb = seq // tm
    cwid = CONV_WIDTH

    def body(attn_ref, ubc_ref, halo_ref, cw_ref, w_ref, x_ref, ln_ref, mod_ref,
             mixin_ref, mix_ref, xhat_ref, rstd_ref, zbuf, acc):
        first = (pl.program_id(0) % tpb) == 0
        u, bg, cg = (ubc_ref[:, s * cwid:(s + 1) * cwid].astype(F32) for s in range(3))
        z = cg * u
        hz = halo_ref[:, 2 * cwid:3 * cwid].astype(F32) * halo_ref[:, 0:cwid].astype(F32)
        zbuf[0:HALO, :] = jnp.where(first, 0.0, hz)
        zbuf[HALO:HALO + tm, :] = z
        y = (cw_ref[0:1, :] * zbuf[HALO - 2:HALO - 2 + tm, :] + cw_ref[1:2, :] * zbuf[HALO - 1:HALO - 1 + tm, :]
             + cw_ref[2:3, :] * z)
        mixin_ref[:, 0:ATTN_WIDTH] = attn_ref[...]
        mixin_ref[:, ATTN_WIDTH:] = (bg * y).astype(BF16)
        mv = mixin_ref[...]
        for j in range(d // COL_CHUNK):
            acc[:, j * COL_CHUNK:(j + 1) * COL_CHUNK] = jnp.dot(
                mv, w_ref[:, j * COL_CHUNK:(j + 1) * COL_CHUNK], preferred_element_type=F32)
        scale = 1.0 + mod_ref[0, gate_idx:gate_idx + 1, :]

        mix = acc[...]
        xhat, rstd = _ln_stats(DN_ALPHA * (x_ref[...] * ln_ref[0:1, :] + ln_ref[1:2, :]) + scale * mix)
        mix_ref[...] = mix.astype(BF16)
        xhat_ref[...] = xhat
        rstd_ref[...] = rstd

    row = lambda w: pl.BlockSpec((tm, w), lambda i: (i, 0))
    return _pcall(
        body, name=name, grid=(t // tm,),
        in_specs=[row(ATTN_WIDTH), row(3 * cwid),
                  pl.BlockSpec((HALO, 3 * cwid), lambda i: (jnp.maximum(i * (tm // HALO) - 1, 0), 0)),
                  _full((8, cwid)), _resident((d, d)), row(d), _full((2, d)),
                  pl.BlockSpec((1, N_MOD, d), lambda i: (i // tpb, 0, 0))],
        out_specs=[row(d), row(d), row(d), row(1)],
        out_shape=[jax.ShapeDtypeStruct((t, d), BF16), jax.ShapeDtypeStruct((t, d), BF16),
                   jax.ShapeDtypeStruct((t, d), F32), jax.ShapeDtypeStruct((t, 1), F32)],
        scratch_shapes=[pltpu.VMEM((tm + HALO, cwid), F32), pltpu.VMEM((tm, d), F32)],
        args=(attn, ubc, ubc, cw, wout, xhat_in, lnp_in, mod), comm=comm)


def _ffn_bwd_act(df, wd, gu, *, seq, name, comm=None):
    t, d = df.shape
    f = wd.shape[0]
    tm = min(512, seq)
    ch = min(COL_CHUNK, f)

    def body(df_ref, wd_ref, gu_ref, dgu_ref):
        dfv = df_ref[...]
        for j in range(f // ch):
            da = _dot_nt(dfv, wd_ref[j * ch:(j + 1) * ch, :])
            dgu_ref[:, j * ch:(j + 1) * ch] = (da * gu_ref[:, j * ch:(j + 1) * ch].astype(F32)).astype(BF16)
            dgu_ref[:, f + j * ch:f + (j + 1) * ch] = (
                da * gu_ref[:, f + j * ch:f + (j + 1) * ch].astype(F32)).astype(BF16)

    return _pcall(
        body, name=name, grid=(t // tm,),
        in_specs=[pl.BlockSpec((tm, d), lambda i: (i, 0)), _resident((f, d)),
                  pl.BlockSpec((tm, 2 * f), lambda i: (i, 0))],
        out_specs=pl.BlockSpec((tm, 2 * f), lambda i: (i, 0)),
        out_shape=jax.ShapeDtypeStruct((t, 2 * f), BF16),
        args=(df, wd, gu), comm=comm)


def _bwd_in(a, w, dr, xin, rstd_prev, lnp_prev, mod, branch_prev, *, seq, w_is_nt, sc_idx, gate_idx,
            branch_scale, final, name, comm=None):
    t, kdim = a.shape
    d = dr.shape[1]
    nb = t // seq
    tm = min(512, seq)
    tpb = seq // tm

    def body(*refs):
        if final:
            a_ref, w_ref, dr_ref, x_ref, mod_ref, dx_ref, dsc_ref, dsh_ref, acc = refs
        else:
            (a_ref, w_ref, dr_ref, x_ref, rstd_ref, ln_ref, mod_ref, br_ref,
             drp_ref, dbr_ref, dsc_ref, dsh_ref, dgate_ref, dg_ref, db_ref, acc) = refs
        i = pl.program_id(0)
        av = a_ref[...]
        for j in range(d // COL_CHUNK):
            cols = slice(j * COL_CHUNK, (j + 1) * COL_CHUNK)
            acc[:, cols] = (_dot_nt(av, w_ref[cols, :]) if w_is_nt
                            else jnp.dot(av, w_ref[:, cols], preferred_element_type=F32))
        sc1 = 1.0 + mod_ref[0, sc_idx:sc_idx + 1, :]
        if not final:
            g_prev, b_prev = ln_ref[0:1, :], ln_ref[1:2, :]
            bscale = branch_scale * (1.0 + mod_ref[0, gate_idx:gate_idx + 1, :])

        def chunk(rows, carry):
            dh = acc[rows, :]
            dx = DN_ALPHA * dr_ref[rows, :] + dh * sc1
            if final:
                dx_ref[rows, :] = dx
                return carry[0] + _fold8(dh * x_ref[rows, :]), carry[1] + _fold8(dh)
            xhat = x_ref[rows, :]
            drp = _ln_bwd(dx, xhat, rstd_ref[rows, :], g_prev)
            drp_ref[rows, :] = drp
            dbr_ref[rows, :] = (bscale * drp).astype(BF16)
            return (carry[0] + _fold8(dh * (xhat * g_prev + b_prev)), carry[1] + _fold8(dh),
                    carry[2] + _fold8(branch_scale * br_ref[rows, :].astype(F32) * drp),
                    carry[3] + _fold8(dx * xhat), carry[4] + _fold8(dx))

        zero = jnp.zeros((8, d), F32)
        sums = _row_chunk_loop(tm, chunk, (zero,) * (2 if final else 5))

        @pl.when((i % tpb) == 0)
        def _():
            dsc_ref[...] = jnp.zeros_like(dsc_ref)
            dsh_ref[...] = jnp.zeros_like(dsh_ref)
            if not final:
                dgate_ref[...] = jnp.zeros_like(dgate_ref)

        dsc_ref[0] += _row_sum(sums[0])
        dsh_ref[0] += _row_sum(sums[1])
        if not final:
            @pl.when(i == 0)
            def _():
                dg_ref[...] = jnp.zeros_like(dg_ref)
                db_ref[...] = jnp.zeros_like(db_ref)

            dgate_ref[0] += _row_sum(sums[2])
            dg_ref[...] += _row_sum(sums[3])
            db_ref[...] += _row_sum(sums[4])

    row = lambda w_: pl.BlockSpec((tm, w_), lambda i: (i, 0))
    vec = pl.BlockSpec((1, 1, d), lambda i: (i // tpb, 0, 0))
    mod_spec = pl.BlockSpec((1, N_MOD, d), lambda i: (i // tpb, 0, 0))
    vshape = jax.ShapeDtypeStruct((nb, 1, d), F32)
    if final:
        in_specs = [row(kdim), _resident(w.shape), row(d), row(d), mod_spec]
        args = (a, w, dr, xin, mod)
        out_specs = [row(d), vec, vec]
        out_shape = [jax.ShapeDtypeStruct((t, d), F32), vshape, vshape]
    else:
        in_specs = [row(kdim), _resident(w.shape), row(d), row(d), row(1), _full((2, d)), mod_spec, row(d)]
        args = (a, w, dr, xin, rstd_prev, lnp_prev, mod, branch_prev)
        out_specs = [row(d), row(d), vec, vec, vec, _full((1, d)), _full((1, d))]
        out_shape = [jax.ShapeDtypeStruct((t, d), F32), jax.ShapeDtypeStruct((t, d), BF16), vshape, vshape, vshape,
                     jax.ShapeDtypeStruct((1, d), F32), jax.ShapeDtypeStruct((1, d), F32)]
    return _pcall(
        body, name=name, grid=(t // tm,), in_specs=in_specs, out_specs=out_specs, out_shape=out_shape,
        scratch_shapes=[pltpu.VMEM((tm, d), F32)], args=args, comm=comm)


def _matmul_tn(a, b, *, tmm, tnn, name, comm=None):
    t, m = a.shape
    n = b.shape[1]
    tk = min(2048, t)

    def body(a_ref, b_ref, o_ref):
        @pl.when(pl.program_id(2) == 0)
        def _():
            o_ref[...] = jnp.zeros_like(o_ref)
        o_ref[...] += _dot_tn(a_ref[...], b_ref[...])

    return _pcall(
        body, name=name, grid=(m // tmm, n // tnn, t // tk),
        in_specs=[pl.BlockSpec((tk, tmm), lambda i, j, k: (k, i)), pl.BlockSpec((tk, tnn), lambda i, j, k: (k, j))],
        out_specs=pl.BlockSpec((tmm, tnn), lambda i, j, k: (i, j)),
        out_shape=jax.ShapeDtypeStruct((m, n), F32),
        args=(a, b), comm=comm)


def _grad_chip_sum(pos, a, b, *, name, comm=None):
    t, m = a.shape
    n = b.shape[1]
    hm, tnn = m // 2, n // N_CHIPS
    tk = min(2048, t)
    nk = t // tk
    n_j = n // tnn

    def body(pos_ref, a_ref, b_ref, s32_ref, s16_ref, land_ref, acc, theirs, send_sems, recv_sems, copy_sem):
        p, j, k = pl.program_id(0), pl.program_id(1), pl.program_id(2)
        x, y, c = _position()

        def push(jj):
            return pltpu.make_async_remote_copy(
                src_ref=acc.at[jj], dst_ref=land_ref.at[jj], send_sem=send_sems.at[jj], recv_sem=recv_sems.at[jj],
                device_id=(x, y, 1 - c), device_id_type=MESH)

        fetch = pltpu.make_async_copy(land_ref.at[j], theirs, copy_sem)

        @pl.when(jnp.logical_and(p == 1, k == 0))
        def _():
            push(j).wait_send()
            push(j).wait_recv()
            fetch.start()

        part = _dot_tn(a_ref[...], b_ref[...])

        @pl.when(k == 0)
        def _():
            acc[j] = part

        @pl.when(k > 0)
        def _():
            acc[j] += part

        @pl.when(jnp.logical_and(p == 0, k == nk - 1))
        def _():
            push(j).start()

        @pl.when(jnp.logical_and(p == 1, k == nk - 1))
        def _():
            fetch.wait()
            s = acc[j] + theirs[...]
            s32_ref[0] = s
            s16_ref[0] = s.astype(BF16)

    half = lambda p, pos_ref: 1 - pos_ref[2] - p + 2 * p * pos_ref[2]
    out_tile = pl.BlockSpec((1, hm, tnn), lambda p, j, k, pos_ref: (0, 0, j * p))
    shape = lambda dt: jax.ShapeDtypeStruct((1, hm, n), dt)
    out = _pcall(
        body, name=name, grid=(2, n_j, nk),
        in_specs=[pl.BlockSpec((tk, hm), lambda p, j, k, pos_ref: (k, half(p, pos_ref))),
                  pl.BlockSpec((tk, tnn), lambda p, j, k, pos_ref: (k, j))],
        out_specs=[out_tile, out_tile, ANY_SPEC],
        out_shape=[shape(F32), shape(BF16), jax.ShapeDtypeStruct((n_j, hm, tnn), F32)],
        scratch_shapes=[pltpu.VMEM((n_j, hm, tnn), F32), pltpu.VMEM((hm, tnn), F32),
                        pltpu.SemaphoreType.DMA((n_j,)), pltpu.SemaphoreType.DMA((n_j,)), pltpu.SemaphoreType.DMA],
        args=(a, b), prefetch=pos, comm=comm)
    if comm is None:
        return out[0], out[1]
    (s32, s16, _), extra = out
    return (s32, s16), extra


def _matmul_nt_bf16(a, w, *, seq, name):
    t, kdim = a.shape
    n = w.shape[0]
    tm = min(512, seq)

    def body(a_ref, w_ref, o_ref):
        av = a_ref[...]
        for j in range(n // COL_CHUNK):
            o_ref[:, j * COL_CHUNK:(j + 1) * COL_CHUNK] = _dot_nt(
                av, w_ref[j * COL_CHUNK:(j + 1) * COL_CHUNK, :]).astype(BF16)

    return pl.pallas_call(
        body, name=name, grid=(t // tm,),
        in_specs=[pl.BlockSpec((tm, kdim), lambda i: (i, 0)), _resident((n, kdim))],
        out_specs=pl.BlockSpec((tm, n), lambda i: (i, 0)),
        out_shape=jax.ShapeDtypeStruct((t, n), BF16),
        compiler_params=_params(("arbitrary",)),
    )(a, w)


def _attention_bwd(q, k, v, dmixin, sinks, *, seq, name, comm=None):
    t = q.shape[0]
    nblk = seq // BLOCK
    tile = ATTN_TILE_BLOCKS * BLOCK

    def body(q_ref, kp_ref, kc_ref, vp_ref, vc_ref, do_ref, sink_ref,
             dq_ref, dkp_ref, dkc_ref, dvp_ref, dvc_ref, dsink_ref):
        n = pl.program_id(0)

        @pl.when(n == 0)
        def _():
            dsink_ref[...] = jnp.zeros_like(dsink_ref)

        srow = lax.broadcasted_iota(jnp.int32, (8, LANE), 0)
        dsink = jnp.zeros((8, LANE), F32)
        for s in range(ATTN_TILE_BLOCKS):
            rows, k_prev, k_cur, v_prev, v_cur, first = _attn_sub_block(s, n, nblk, kp_ref, kc_ref, vp_ref, vc_ref)
            dqs, dks, dvs = [], [], []
            for g in range(N_KV_HEADS):
                qs, kk, vv, pn, psn = _attn_group(q_ref, rows, k_prev, k_cur, v_prev, v_cur, sink_ref, g, first)
                dos = jnp.concatenate(
                    [do_ref[rows, (GQA_GROUP * g + j) * HEAD_DIM:(GQA_GROUP * g + j + 1) * HEAD_DIM]
                     for j in range(GQA_GROUP)], axis=0)
                dp = _dot_nt(vv, dos)
                delta = jnp.sum(pn * dp, axis=0, keepdims=True)
                ds = pn * (dp - delta)
                dsk = psn * delta
                for j in range(GQA_GROUP):
                    tot = jnp.sum(dsk[:, j * BLOCK:(j + 1) * BLOCK], axis=1, keepdims=True)
                    dsink = dsink - jnp.where(srow == GQA_GROUP * g + j, tot, 0.0)
                dsb = (ds * (HEAD_DIM ** -0.5)).astype(BF16)
                dqs.append(_heads_to_lanes(_dot_tn(kk, dsb)))
                dks.append(jnp.dot(dsb, qs, preferred_element_type=F32))
                dvs.append(jnp.dot(pn.astype(BF16), dos, preferred_element_type=F32))
            dq_ref[rows, :] = jnp.concatenate(dqs, axis=1)
            dkp_ref[rows, :] = jnp.concatenate([x[0:BLOCK, :] for x in dks], axis=1)
            dkc_ref[rows, :] = jnp.concatenate([x[BLOCK:, :] for x in dks], axis=1)
            dvp_ref[rows, :] = jnp.concatenate([x[0:BLOCK, :] for x in dvs], axis=1)
            dvc_ref[rows, :] = jnp.concatenate([x[BLOCK:, :] for x in dvs], axis=1)
        dsink_ref[...] += dsink

    cur = lambda w: pl.BlockSpec((tile, w), lambda n: (n, 0))
    prev = lambda w: pl.BlockSpec((BLOCK, w), lambda n: (jnp.maximum(n * ATTN_TILE_BLOCKS - 1, 0), 0))
    kv = jax.ShapeDtypeStruct((t, KV_WIDTH), F32)
    return _pcall(
        body, name=name, grid=(t // tile,),
        in_specs=[cur(ATTN_WIDTH), prev(KV_WIDTH), cur(KV_WIDTH), prev(KV_WIDTH), cur(KV_WIDTH), cur(ATTN_WIDTH),
                  pl.BlockSpec(memory_space=pltpu.SMEM)],
        out_specs=[cur(ATTN_WIDTH), cur(KV_WIDTH), cur(KV_WIDTH), cur(KV_WIDTH), cur(KV_WIDTH), _full((8, LANE))],
        out_shape=[jax.ShapeDtypeStruct((t, ATTN_WIDTH), F32), kv, kv, kv, kv, jax.ShapeDtypeStruct((8, LANE), F32)],
        args=(q, k, k, v, v, dmixin, sinks), comm=comm)


def _mix_bwd_assemble(dq, dkp, dkc, dvp, dvc, cos, sa, sb, dmixin, ubc, cw, *, seq, name, comm=None):
    t = dq.shape[0]
    cwid = CONV_WIDTH
    tm = min(2 * BLOCK, seq)
    tiles_per_seq = seq // tm
    ntile = t // tm
    nblk_all = t // BLOCK
    per_tile = tm // BLOCK

    def body(*refs):
        dq_ref, dkc_ref, dvc_ref = refs[0:3]
        dkp_refs, dvp_refs = refs[3:3 + per_tile], refs[3 + per_tile:3 + 2 * per_tile]
        (cos_ref, sa_ref, sb_ref, dco_ref, dcon_ref, ubc_ref, hprev_ref, hnext_ref, cw_ref,
         dproj_ref, dcw_ref, zbuf, dybuf) = refs[3 + 2 * per_tile:]
        i = pl.program_id(0)
        first = (i % tiles_per_seq) == 0
        last = (i % tiles_per_seq) == tiles_per_seq - 1
        glast = i == ntile - 1

        @pl.when(i == 0)
        def _():
            dcw_ref[...] = jnp.zeros_like(dcw_ref)

        def with_next_block(cur_ref, nxt_refs):
            nxt = [r[...] for r in nxt_refs]
            nxt[-1] = jnp.where(glast, 0.0, nxt[-1])
            return cur_ref[...] + jnp.concatenate(nxt, axis=0)

        cos_t, sa_t, sb_t = cos_ref[...], sa_ref[...], sb_ref[...]
        for j in range(ATTN_WIDTH // LANE):
            dproj_ref[:, j * LANE:(j + 1) * LANE] = _rope_t(
                dq_ref[:, j * LANE:(j + 1) * LANE], cos_t, sa_t, sb_t).astype(BF16)
        dk = with_next_block(dkc_ref, dkp_refs)
        dproj_ref[:, ATTN_WIDTH:ATTN_WIDTH + KV_WIDTH] = _rope_t(dk, cos_t, sa_t, sb_t).astype(BF16)
        dv = with_next_block(dvc_ref, dvp_refs)
        dproj_ref[:, ATTN_WIDTH + KV_WIDTH:ATTN_WIDTH + 2 * KV_WIDTH] = dv.astype(BF16)

        u, bg, cg = (ubc_ref[:, s * cwid:(s + 1) * cwid].astype(F32) for s in range(3))
        z = cg * u
        hz = hprev_ref[:, 2 * cwid:3 * cwid].astype(F32) * hprev_ref[:, 0:cwid].astype(F32)
        zbuf[0:HALO, :] = jnp.where(first, 0.0, hz)
        zbuf[HALO:HALO + tm, :] = z
        z2, z1 = zbuf[HALO - 2:HALO - 2 + tm, :], zbuf[HALO - 1:HALO - 1 + tm, :]
        w0, w1, w2 = cw_ref[0:1, :], cw_ref[1:2, :], cw_ref[2:3, :]
        y = w0 * z2 + w1 * z1 + w2 * z
        dco = dco_ref[...].astype(F32)
        dyc = dco * bg
        dyn = dcon_ref[...].astype(F32) * hnext_ref[:, cwid:2 * cwid].astype(F32)
        dybuf[0:tm, :] = dyc
        dybuf[tm:tm + HALO, :] = jnp.where(last, 0.0, dyn)
        dz = w2 * dyc + w1 * dybuf[1:1 + tm, :] + w0 * dybuf[2:2 + tm, :]
        srow = lax.broadcasted_iota(jnp.int32, (8, cwid), 0)
        dcw_ref[...] += (jnp.where(srow == 0, _row_sum(dyc * z2), 0.0) + jnp.where(srow == 1, _row_sum(dyc * z1), 0.0)
                         + jnp.where(srow == 2, _row_sum(dyc * z), 0.0))
        base = ATTN_WIDTH + 2 * KV_WIDTH
        dproj_ref[:, base:base + cwid] = (dz * cg).astype(BF16)
        dproj_ref[:, base + cwid:base + 2 * cwid] = (dco * y).astype(BF16)
        dproj_ref[:, base + 2 * cwid:base + 3 * cwid] = (dz * u).astype(BF16)

    cur = lambda w: pl.BlockSpec((tm, w), lambda i: (i, 0))
    nxt = [pl.BlockSpec((BLOCK, KV_WIDTH), lambda i, s=s: (jnp.minimum(i * per_tile + s + 1, nblk_all - 1), 0))
           for s in range(per_tile)]
    prev_halo = pl.BlockSpec((HALO, 3 * cwid), lambda i: (jnp.maximum(i * (tm // HALO) - 1, 0), 0))
    next_halo = lambda w, col: pl.BlockSpec(
        (HALO, w), lambda i: (jnp.minimum((i + 1) * (tm // HALO), t // HALO - 1), col))
    return _pcall(
        body, name=name, grid=(ntile,),
        in_specs=[cur(ATTN_WIDTH), cur(KV_WIDTH), cur(KV_WIDTH), *nxt, *nxt,
                  cur(LANE), cur(LANE), cur(LANE),
                  pl.BlockSpec((tm, cwid), lambda i: (i, 1)), next_halo(cwid, 1),
                  cur(3 * cwid), prev_halo, next_halo(3 * cwid, 0), _full((8, cwid))],
        out_specs=[cur(IN_WIDTH), _full((8, cwid))],
        out_shape=[jax.ShapeDtypeStruct((t, IN_WIDTH), BF16), jax.ShapeDtypeStruct((8, cwid), F32)],
        scratch_shapes=[pltpu.VMEM((tm + HALO, cwid), F32), pltpu.VMEM((tm + HALO, cwid), F32)],
        args=(dq, dkc, dvc, *([dkp] * per_tile), *([dvp] * per_tile), cos, sa, sb, dmixin, dmixin,
              ubc, ubc, ubc, cw), comm=comm)


def _ada_fwd(c_all, w_ada, b_ada_shard, *, name, comm=None):
    nb, d = c_all.shape
    n = w_ada.shape[1]
    tn = n // 2

    def body(c_ref, w_ref, b_ref, o_ref):
        cv = c_ref[...]
        cond = cv * _sigmoid(cv)
        o_ref[...] = jnp.dot(cond, w_ref[...], preferred_element_type=F32,
                             precision=lax.Precision.HIGHEST) + b_ref[...]

    return _pcall(
        body, name=name, grid=(n // tn,),
        in_specs=[_full((nb, d)), pl.BlockSpec((d, tn), lambda j: (0, j)), pl.BlockSpec((1, tn), lambda j: (0, j))],
        out_specs=pl.BlockSpec((nb, tn), lambda j: (0, j)),
        out_shape=jax.ShapeDtypeStruct((nb, n), F32), args=(c_all, w_ada, b_ada_shard), comm=comm)


def _small_finish(gathered, dmod_all, dmod_shard, c_all_t, *, name):
    d = D_MODEL
    nb, n = dmod_shard.shape

    def body(g_ref, dm_ref, dms_ref, ct_ref, sum_ref, gw_ref, gb_ref):
        total = g_ref[0]
        for dev in range(1, N_DEV):
            total = total + g_ref[dev]
        sum_ref[...] = total
        gb_ref[...] = _row_sum(dm_ref[...])
        ctv = ct_ref[...]
        cond_t = ctv * _sigmoid(ctv)
        for jb in range(n // COL_CHUNK):
            gw_ref[:, jb * COL_CHUNK:(jb + 1) * COL_CHUNK] = jnp.dot(
                cond_t, dms_ref[:, jb * COL_CHUNK:(jb + 1) * COL_CHUNK], preferred_element_type=F32,
                precision=lax.Precision.HIGHEST)

    return pl.pallas_call(
        body, name=name, grid=(1,),
        in_specs=[_full((N_DEV, SMALL_ROWS, d)), _full((nb, N_MOD * d)), _full((nb, n)), _full((d, nb))],
        out_specs=[_full((SMALL_ROWS, d)), _full((d, n)), _full((1, N_MOD * d))],
        out_shape=[jax.ShapeDtypeStruct((SMALL_ROWS, d), F32), jax.ShapeDtypeStruct((d, n), F32),
                   jax.ShapeDtypeStruct((1, N_MOD * d), F32)],
        compiler_params=_params(("arbitrary",)),
    )(gathered, dmod_all, dmod_shard, c_all_t)


def _row_tile(r, c, budget=1 << 21):
    if r * c * 4 <= budget or r % 16:
        return r
    best = 16
    for tr in range(16, r + 1, 16):
        if r % tr == 0 and tr * c * 4 <= budget:
            best = tr
    return best


def _cast_into(w, chip, col_kind, *, name):
    r, c = w.shape
    tr = _row_tile(r, c)

    def body(chip_ref, w_ref, o_ref):
        o_ref[...] = w_ref[...].astype(BF16)

    if col_kind:
        out_spec = pl.BlockSpec((tr, c), lambda i, chip_ref: (i, chip_ref[0]))
        out_shape = jax.ShapeDtypeStruct((r, c * N_CHIPS), BF16)
    else:
        out_spec = pl.BlockSpec((tr, c), lambda i, chip_ref: (chip_ref[0] * (r // tr) + i, 0))
        out_shape = jax.ShapeDtypeStruct((r * N_CHIPS, c), BF16)
    return _pcall(body, name=name, grid=(r // tr,), in_specs=[pl.BlockSpec((tr, c), lambda i, chip_ref: (i, 0))],
                  out_specs=out_spec, out_shape=out_shape, args=(w,), prefetch=chip)


def _adamw(w, g, m, v, *, name, comm=None):
    r, c = w.shape
    tr = _row_tile(r, c)
    c1 = 1.0 - ADAM_B1 ** ADAM_STEP
    c2 = 1.0 - ADAM_B2 ** ADAM_STEP

    def body(w_ref, g_ref, m_ref, v_ref, d_ref, nm_ref, nv_ref):
        gv = g_ref[...]
        m2 = ADAM_B1 * m_ref[...] + (1.0 - ADAM_B1) * gv
        v2 = ADAM_B2 * v_ref[...] + (1.0 - ADAM_B2) * (gv * gv)
        d_ref[...] = -ADAM_LR * ((m2 / c1) / (jnp.sqrt(v2 / c2) + ADAM_EPS) + ADAM_WD * w_ref[...])
        nm_ref[...] = m2
        nv_ref[...] = v2

    spec = pl.BlockSpec((tr, c), lambda i: (i, 0))
    sh = jax.ShapeDtypeStruct((r, c), F32)
    return _pcall(body, name=name, grid=(r // tr,), in_specs=[spec] * 4, out_specs=[spec] * 3, out_shape=[sh] * 3,
                  args=(w, g, m, v), comm=comm)


def _sum_pair(pos, g3, r3, blk_of, *, name, comm=None):
    n, rows, cols = r3.shape
    tr = _row_tile(rows, cols)

    def body(pos_ref, g_ref, r_ref, s32_ref, s16_ref):
        s = g_ref[0] + r_ref[0]
        s32_ref[0] = s
        s16_ref[0] = s.astype(BF16)

    own = pl.BlockSpec((1, tr, cols), lambda p, i, pos: (blk_of(p, pos), i, 0))
    plain = pl.BlockSpec((1, tr, cols), lambda p, i, pos: (p, i, 0))
    return _pcall(
        body, name=name, grid=(n, rows // tr), in_specs=[own, plain], out_specs=[plain, plain],
        out_shape=[jax.ShapeDtypeStruct((n, rows, cols), F32), jax.ShapeDtypeStruct((n, rows, cols), BF16)],
        args=(g3, r3), prefetch=pos, comm=comm)


def _sum_final(pos, s32, recv, *, col_kind, n_shard, name, comm=None):
    if col_kind:
        rows, cols = s32.shape[1], n_shard
        own = lambda tr: pl.BlockSpec((1, tr, cols), lambda i, pos: (0, i, 2 * pos[0] + pos[1]))
    else:
        rows, cols = s32.shape[1], s32.shape[2]
        own = lambda tr: pl.BlockSpec((1, tr, cols), lambda i, pos: (2 * pos[0] + pos[1], i, 0))
    tr = _row_tile(rows, cols)

    def body(pos_ref, s_ref, r_ref, o_ref):
        o_ref[0] = ((s_ref[0] + r_ref[0].astype(F32)) + r_ref[1].astype(F32)) + r_ref[2].astype(F32)

    return _pcall(
        body, name=name, grid=(rows // tr,),
        in_specs=[own(tr), pl.BlockSpec((3, tr, cols), lambda i, pos: (0, i, 0))],
        out_specs=pl.BlockSpec((1, tr, cols), lambda i, pos: (pos[2], i, 0)),
        out_shape=jax.ShapeDtypeStruct((2, rows, cols), F32), args=(s32, recv), prefetch=pos, comm=comm)


def _position():
    return lax.axis_index("x"), lax.axis_index("y"), lax.axis_index("c")


def _allgather8(x_shard, *, name, comm=None):
    m_per, n = x_shard.shape
    nci, nco = (0, 0) if comm is None else (len(comm.inputs), len(comm.out_shapes))

    def body(*refs):
        x_ref, refs = refs[0], refs[1:]
        cin, refs = refs[:nci], refs[nci:]
        out_ref, refs = refs[0], refs[1:]
        cout, refs = refs[:nco], refs[nco:]
        (send_sems, recv_sems, local_sem), csems = refs[:3], refs[3:]
        x, y, c = _position()
        me, sibling = (x, y, c), (x, y, 1 - c)
        chips = [(1 - x, y), (x, 1 - y), (1 - x, 1 - y)]

        def rows(px, py, pc):
            return out_ref.at[pl.ds((4 * px + 2 * py + pc) * m_per, m_per), :]

        def copy(k, block, to, src=None):
            return pltpu.make_async_remote_copy(
                src_ref=rows(*block) if src is None else src, dst_ref=rows(*block),
                send_sem=send_sems.at[k], recv_sem=recv_sems.at[k], device_id=to, device_id_type=MESH)

        mine = pltpu.make_async_copy(x_ref, rows(*me), local_sem)
        mine.start()
        first = [copy(0, me, sibling, src=x_ref)]
        first += [copy(1 + j, me, (*chip, c), src=x_ref) for j, chip in enumerate(chips)]
        for cp in first:
            cp.start()
        if comm is not None:
            comm.start(cin, cout, csems)
        passed = [copy(4 + j, (*chip, c), sibling) for j, chip in enumerate(chips)]
        for j, chip in enumerate(chips):
            copy(1 + j, (*chip, c), me).wait_recv()
            passed[j].start()
        copy(0, sibling, me).wait_recv()
        for j, chip in enumerate(chips):
            copy(4 + j, (*chip, 1 - c), me).wait_recv()
        for cp in first + passed:
            cp.wait_send()
        mine.wait()
        if comm is not None:
            comm.middle(cin, cout, csems)
            comm.late(cin, cout, csems)
            comm.finish(cin, cout, csems)

    vmem = pl.BlockSpec(memory_space=pltpu.VMEM)
    sems = [pltpu.SemaphoreType.DMA((7,)), pltpu.SemaphoreType.DMA((7,)), pltpu.SemaphoreType.DMA]
    out = jax.ShapeDtypeStruct((N_DEV * m_per, n), x_shard.dtype)
    if comm is None:
        return pl.pallas_call(body, name=name, out_shape=out, in_specs=[vmem], out_specs=vmem,
                              scratch_shapes=sems)(x_shard)
    res = pl.pallas_call(
        body, name=name, out_shape=[out] + list(comm.out_shapes), in_specs=[vmem] + [ANY_SPEC] * nci,
        out_specs=[vmem] + [ANY_SPEC] * nco, scratch_shapes=sems + list(comm.sems),
        input_output_aliases={1 + i: 1 + o for i, o in comm.aliases.items()})(x_shard, *comm.inputs)
    return res[0], list(res[1:])


def _peer_chips(x, y):
    return [(1 - x, y), (x, 1 - y), (1 - x, 1 - y)]


class _GatherJob:
    def __init__(self, pieces):
        self.pieces = pieces
        n_p = len(pieces)
        self.inputs = [p[0] for p in pieces]
        self.out_shapes = [jax.ShapeDtypeStruct(p[0].shape, p[0].dtype) for p in pieces]
        for buf, col_kind, r0, nr in pieces:
            half_rows = buf.shape[0] // (2 if col_kind else 2 * N_CHIPS)
            assert r0 % 16 == 0 and nr % 16 == 0 and nr >= 32 and r0 + nr <= half_rows, (buf.shape, r0, nr)
        self.aliases = {p: p for p in range(n_p)}
        dma = pltpu.SemaphoreType.DMA
        self.sems = [dma((2 * n_p,))] * 4 + [dma((4 * n_p,))] * 2

    def _region(self, cout, p, chip_idx, half, part=None):
        buf, col_kind, r0, nr = self.pieces[p]
        first = -(-nr // 32) * 16
        if part == 0:
            nr = first
        elif part == 1:
            r0, nr = r0 + first, nr - first
        if col_kind:
            n = buf.shape[1] // N_CHIPS
            return cout[p].at[pl.ds(half * (buf.shape[0] // 2) + r0, nr), pl.ds(chip_idx * n, n)]
        n = buf.shape[0] // N_CHIPS
        return cout[p].at[pl.ds(chip_idx * n + half * (n // 2) + r0, nr), :]

    def _copies(self, cout, sems):
        send1, recv1, send2, recv2, fsend, frecv = sems
        x, y, c = _position()
        k = 2 * x + y
        sibling = (x, y, 1 - c)
        x_nbr, y_nbr, diag = _peer_chips(x, y)
        chip_of = lambda ch: 2 * ch[0] + ch[1]

        def remote(region, ssem, rsem, to):
            return pltpu.make_async_remote_copy(src_ref=region, dst_ref=region, send_sem=ssem, recv_sem=rsem,
                                                device_id=to, device_id_type=MESH)

        hop1, arrived1, hop2, arrived2, fwds, fwd_arrived = [], [], [], [], [], []
        for p in range(len(self.pieces)):
            for j, nbr in enumerate((x_nbr, y_nbr)):
                i1 = 2 * p + j
                hop1.append(remote(self._region(cout, p, k, c), send1.at[i1], recv1.at[i1], (*nbr, c)))
                arrived1.append(remote(self._region(cout, p, chip_of(nbr), c), send1.at[i1], recv1.at[i1], (*nbr, c)))
            hop2.append(remote(self._region(cout, p, chip_of(x_nbr), c, 0), send2.at[2 * p], recv2.at[2 * p],
                               (*y_nbr, c)))
            hop2.append(remote(self._region(cout, p, chip_of(y_nbr), c, 1), send2.at[2 * p + 1], recv2.at[2 * p + 1],
                               (*x_nbr, c)))
            arrived2.append(remote(self._region(cout, p, chip_of(diag), c, 0), send2.at[2 * p], recv2.at[2 * p],
                                   (*y_nbr, c)))
            arrived2.append(remote(self._region(cout, p, chip_of(diag), c, 1), send2.at[2 * p + 1],
                                   recv2.at[2 * p + 1], (*x_nbr, c)))
            landed = [(chip_of(x_nbr), None), (chip_of(y_nbr), None), (chip_of(diag), 0), (chip_of(diag), 1)]
            for q, (chip_idx, part) in enumerate(landed):
                i3 = 4 * p + q
                fwds.append(remote(self._region(cout, p, chip_idx, c, part), fsend.at[i3], frecv.at[i3], sibling))
                fwd_arrived.append(remote(self._region(cout, p, chip_idx, 1 - c, part), fsend.at[i3], frecv.at[i3],
                                          sibling))
        return hop1, arrived1, hop2, arrived2, fwds, fwd_arrived

    def start(self, cin, cout, sems):
        for cp in self._copies(cout, sems)[0]:
            cp.start()

    def middle(self, cin, cout, sems):
        _, arrived1, hop2, _, fwds, _ = self._copies(cout, sems)
        for p in range(len(self.pieces)):
            for j in range(2):
                arrived1[2 * p + j].wait_recv()
                hop2[2 * p + j].start()
                fwds[4 * p + j].start()

    def late(self, cin, cout, sems):
        _, _, _, arrived2, fwds, _ = self._copies(cout, sems)
        for p in range(len(self.pieces)):
            for j in range(2):
                arrived2[2 * p + j].wait_recv()
                fwds[4 * p + 2 + j].start()

    def finish(self, cin, cout, sems):
        hop1, _, hop2, _, fwds, fwd_arrived = self._copies(cout, sems)
        for cp in fwd_arrived:
            cp.wait_recv()
        for cp in hop1 + hop2 + fwds:
            cp.wait_send()


class _PairedJob:
    aliases = {}

    def start(self, cin, cout, sems):
        for cp in self._copies(cin, cout, sems):
            cp.start()

    def middle(self, cin, cout, sems):
        pass

    late = middle

    def finish(self, cin, cout, sems):
        copies = self._copies(cin, cout, sems)
        for cp in copies:
            cp.wait_recv()
        for cp in copies:
            cp.wait_send()


class _SwapJob(_PairedJob):
    def __init__(self, grads, kinds):
        self.inputs, self.kinds = list(grads), list(kinds)
        self.out_shapes, self.n_copies = [], []
        for g, kd in zip(grads, kinds):
            if kd:
                self.out_shapes.append(jax.ShapeDtypeStruct((1, g.shape[0] // 2, g.shape[1]), g.dtype))
                self.n_copies.append(1)
            else:
                n = g.shape[0] // N_CHIPS
                self.out_shapes.append(jax.ShapeDtypeStruct((N_CHIPS, n // 2, g.shape[1]), g.dtype))
                self.n_copies.append(N_CHIPS)
        total = sum(self.n_copies)
        self.sems = [pltpu.SemaphoreType.DMA((total,)), pltpu.SemaphoreType.DMA((total,))]

    def _copies(self, cin, cout, sems):
        send_sems, recv_sems = sems
        x, y, c = _position()
        copies = []
        for p, src_ref in enumerate(cin):
            for kk in range(self.n_copies[p]):
                if self.kinds[p]:
                    hr = src_ref.shape[0] // 2
                    src = src_ref.at[pl.ds((1 - c) * hr, hr), :]
                else:
                    n = src_ref.shape[0] // N_CHIPS
                    src = src_ref.at[pl.ds(kk * n + (1 - c) * (n // 2), n // 2), :]
                idx = len(copies)
                copies.append(pltpu.make_async_remote_copy(
                    src_ref=src, dst_ref=cout[p].at[kk], send_sem=send_sems.at[idx], recv_sem=recv_sems.at[idx],
                    device_id=(x, y, 1 - c), device_id_type=MESH))
        return copies


class _ExchangeJob(_PairedJob):
    def __init__(self, s16, kinds, sizes):
        self.inputs, self.kinds, self.sizes = list(s16), list(kinds), list(sizes)
        self.out_shapes = [jax.ShapeDtypeStruct((3, s.shape[1], n if kd else s.shape[2]), s.dtype)
                           for s, kd, n in zip(s16, kinds, sizes)]
        self.sems = [pltpu.SemaphoreType.DMA((3 * len(s16),)), pltpu.SemaphoreType.DMA((3 * len(s16),))]

    def _copies(self, cin, cout, sems):
        send_sems, recv_sems = sems
        x, y, c = _position()
        copies = []
        for p, src_ref in enumerate(cin):
            for j, chip in enumerate(_peer_chips(x, y)):
                kk = 2 * chip[0] + chip[1]
                n = self.sizes[p]
                src = src_ref.at[0, :, pl.ds(kk * n, n)] if self.kinds[p] else src_ref.at[kk]
                copies.append(pltpu.make_async_remote_copy(
                    src_ref=src, dst_ref=cout[p].at[j], send_sem=send_sems.at[3 * p + j],
                    recv_sem=recv_sems.at[3 * p + j], device_id=(*chip, c), device_id_type=MESH))
        return copies


class _ShareJob:
    def __init__(self, halves):
        self.inputs = list(halves)
        self.out_shapes = [jax.ShapeDtypeStruct(h.shape, h.dtype) for h in halves]
        self.aliases = {p: p for p in range(len(halves))}
        self.sems = [pltpu.SemaphoreType.DMA((len(halves),)), pltpu.SemaphoreType.DMA((len(halves),))]

    def _copies(self, cout, sems, half):
        send_sems, recv_sems = sems
        x, y, c = _position()
        h = c if half == "mine" else 1 - c
        return [pltpu.make_async_remote_copy(
            src_ref=o.at[h], dst_ref=o.at[h], send_sem=send_sems.at[p], recv_sem=recv_sems.at[p],
            device_id=(x, y, 1 - c), device_id_type=MESH) for p, o in enumerate(cout)]

    def start(self, cin, cout, sems):
        for cp in self._copies(cout, sems, "mine"):
            cp.start()

    def middle(self, cin, cout, sems):
        pass

    late = middle

    def finish(self, cin, cout, sems):
        for cp in self._copies(cout, sems, "theirs"):
            cp.wait_recv()
        for cp in self._copies(cout, sems, "mine"):
            cp.wait_send()


class _MultiJob:
    def __init__(self, jobs):
        self.jobs = jobs
        self.inputs = [a for j in jobs for a in j.inputs]
        self.out_shapes = [s for j in jobs for s in j.out_shapes]
        self.sems = [s for j in jobs for s in j.sems]
        self.aliases = {}
        i0 = o0 = 0
        for j in jobs:
            for i, o in j.aliases.items():
                self.aliases[i0 + i] = o0 + o
            i0 += len(j.inputs)
            o0 += len(j.out_shapes)

    def _parts(self, cin, cout, sems):
        i0 = o0 = s0 = 0
        for j in self.jobs:
            ni, no, ns = len(j.inputs), len(j.out_shapes), len(j.sems)
            yield j, cin[i0:i0 + ni], cout[o0:o0 + no], sems[s0:s0 + ns]
            i0, o0, s0 = i0 + ni, o0 + no, s0 + ns

    def start(self, cin, cout, sems):
        for j, a, b, s in self._parts(cin, cout, sems):
            j.start(a, b, s)

    def middle(self, cin, cout, sems):
        for j, a, b, s in self._parts(cin, cout, sems):
            j.middle(a, b, s)

    def late(self, cin, cout, sems):
        for j, a, b, s in self._parts(cin, cout, sems):
            j.late(a, b, s)

    def finish(self, cin, cout, sems):
        for j, a, b, s in self._parts(cin, cout, sems):
            j.finish(a, b, s)


def _rope_tables(positions):
    half = ROT_DIM // 2
    inv_freq = jnp.power(jnp.float32(ROPE_THETA), -jnp.arange(0, ROT_DIM, 2, dtype=F32) / ROT_DIM)
    inv_head = jnp.concatenate([inv_freq, inv_freq, jnp.zeros((HEAD_DIM - ROT_DIM,), F32)])
    inv_lane = jnp.concatenate([inv_head] * (LANE // HEAD_DIM))
    ang = positions.astype(F32).reshape(-1)[:, None] * inv_lane[None, :]
    sin = jnp.sin(ang)
    dim = jnp.arange(LANE) % HEAD_DIM
    return jnp.cos(ang), jnp.where(dim < half, -sin, 0.0), jnp.where(dim >= half, sin, 0.0)


def kernel(x, c, positions, w_ada, b_ada, ffn1_w_gate_up, ffn1_w_down, ln1_g, ln1_b, w_in, conv_w, attn_sinks, w_out, ln2_g, ln2_b, ffn2_w_gate_up, ffn2_w_down, ln3_g, ln3_b, loss_target, m_w_ada, m_b_ada, m_ffn1_w_gate_up, m_ffn1_w_down, m_ln1_g, m_ln1_b, m_w_in, m_conv_w, m_attn_sinks, m_w_out, m_ln2_g, m_ln2_b, m_ffn2_w_gate_up, m_ffn2_w_down, m_ln3_g, m_ln3_b, v_w_ada, v_b_ada, v_ffn1_w_gate_up, v_ffn1_w_down, v_ln1_g, v_ln1_b, v_w_in, v_conv_w, v_attn_sinks, v_w_out, v_ln2_g, v_ln2_b, v_ffn2_w_gate_up, v_ffn2_w_down, v_ln3_g, v_ln3_b):
    d = D_MODEL
    nb, seq, _ = x.shape
    t = nb * seq
    f = ffn1_w_down.shape[1] * N_CHIPS
    ax, ay, ac = _position()
    chip = 2 * ax + ay
    dev = 2 * chip + ac
    pos = jnp.stack([ax, ay, ac]).astype(jnp.int32)

    x2 = x.reshape(t, d)
    tgt2 = loss_target.reshape(t, d)
    ln1 = jnp.concatenate([ln1_g, ln1_b], axis=0)
    ln2 = jnp.concatenate([ln2_g, ln2_b], axis=0)
    ln3 = jnp.concatenate([ln3_g, ln3_b], axis=0)
    sinks = attn_sinks.reshape(N_Q_HEADS)
    cos_t, sa_t, sb_t = _rope_tables(positions)

    gu_cuts = [0, 176, 352, d // 2]
    gu_part = lambda buf, s: (buf, True, gu_cuts[s], gu_cuts[s + 1] - gu_cuts[s])
    chip_arr = jnp.reshape(chip, (1,)).astype(jnp.int32)
    b_gu1 = _cast_into(ffn1_w_gate_up[0], chip_arr, True, name="cast_gu1")

    n_ada = w_ada.shape[2]
    c_all, (b_gu1,) = _allgather8(c.reshape(nb * d // LANE, LANE), name="gather_c", comm=_GatherJob([gu_part(b_gu1, 0)]))
    c_all = c_all.reshape(N_DEV * nb, d)
    b_shard = lax.dynamic_slice(b_ada, (0, chip * n_ada), (1, n_ada))
    mod_part, (b_gu1,) = _ada_fwd(c_all, w_ada[0], b_shard, name="ada_fwd", comm=_GatherJob([gu_part(b_gu1, 1)]))
    conv_rows = jnp.pad(conv_w[0], ((0, 5), (0, n_ada - conv_w.shape[2])))
    part = jnp.concatenate([mod_part, conv_rows], axis=0)
    parts, (wgu1,) = _allgather8(part, name="gather_mod", comm=_GatherJob([gu_part(b_gu1, 2)]))
    parts = parts.reshape(N_DEV, N_DEV * nb + 8, n_ada)
    mod_all = jnp.concatenate([parts[2 * k, :N_DEV * nb, :] for k in range(N_CHIPS)], axis=1)
    mod = lax.dynamic_slice(mod_all, (dev * nb, 0), (nb, N_MOD * d)).reshape(nb, N_MOD, d)
    cw_full = jnp.concatenate([parts[2 * k, N_DEV * nb:, :conv_w.shape[2]] for k in range(N_CHIPS)], axis=1)

    b_d1 = _cast_into(ffn1_w_down[0], chip_arr, False, name="cast_d1")
    b_in = _cast_into(w_in[0].T, chip_arr, False, name="cast_in")
    b_out = _cast_into(w_out[0], chip_arr, False, name="cast_out")
    b_gu2 = _cast_into(ffn2_w_gate_up[0], chip_arr, True, name="cast_gu2")
    b_d2 = _cast_into(ffn2_w_down[0], chip_arr, False, name="cast_d2")
    n_gu, n_d, n_in, n_out = (ffn1_w_gate_up.shape[2], ffn1_w_down.shape[1], w_in.shape[2], w_out.shape[1])

    def whole(buf, col_kind):
        return (buf, col_kind, 0, buf.shape[0] // (2 if col_kind else 2 * N_CHIPS))

    (h1, a1, gu1), (wd1, wout) = _ffn_up(x2, ln1, mod, wgu1, seq=seq, sc_idx=1, sh_idx=0, use_ln=False,
                                         name="ffn1_up", comm=_GatherJob([whole(b_d1, False), whole(b_out, False)]))
    (f1, xhat1, rstd1), (win_t,) = _ffn_down_ln(a1, wd1, x2, ln1, mod, seq=seq, gate_idx=2, use_ln=False,
                                                name="ffn1_down", comm=_GatherJob([whole(b_in, False)]))
    (h2, q, k, v, ubc), (b_gu2,) = _in_proj(
        xhat1, ln1, mod, win_t, cos_t, sa_t, sb_t, seq=seq, sc_idx=4, sh_idx=3, name="in_proj",
        comm=_GatherJob([gu_part(b_gu2, 0)]))
    attn, (b_gu2,) = _attention(q, k, v, sinks, seq=seq, name="attention", comm=_GatherJob([gu_part(b_gu2, 1)]))
    (mixin, mix, xhat2, rstd2), (wgu2,) = _out_proj(
        attn, ubc, cw_full, wout, xhat1, ln1, mod, seq=seq, gate_idx=5, name="out_proj",
        comm=_GatherJob([gu_part(b_gu2, 2)]))
    (h3, a3, gu3), (wd2,) = _ffn_up(xhat2, ln2, mod, wgu2, seq=seq, sc_idx=7, sh_idx=6, use_ln=True, name="ffn2_up",
                                    comm=_GatherJob([whole(b_d2, False)]))
    dr3, df3, loss_cols, dln3g, dln3b, dgate3 = _ffn_down_loss(
        a3, wd2, xhat2, ln2, mod, ln3, tgt2, seq=seq, gate_idx=8, name="ffn2_down_loss")

    def pair_sum(g, r3, col_kind, name_, comm=None):
        if col_kind:
            g3 = g.reshape(2, g.shape[0] // 2, g.shape[1])
            blk_of = lambda p_, pos_: pos_[2]
        else:
            g3 = g.reshape(2 * N_CHIPS, g.shape[0] // (2 * N_CHIPS), g.shape[1])
            blk_of = lambda p_, pos_: 2 * p_ + pos_[2]
        return _sum_pair(pos, g3, r3, blk_of, name=name_, comm=comm)

    dgu3 = _ffn_bwd_act(df3, wd2, gu3, seq=seq, name="ffn2_bwd_act")
    g_wd2 = _matmul_tn(a3, df3, tmm=f // 2, tnn=d, name="grad_wd2")
    (s32_gu2, s16_gu2), (sib_d2,) = _grad_chip_sum(pos, h3, dgu3, name="grad_wgu2", comm=_SwapJob([g_wd2], [False]))
    s32_d2, s16_d2 = pair_sum(g_wd2, sib_d2, False, "sum_pair_d2")
    (dr2, dmix, dsc3, dsh3, dgate2, dln2g, dln2b), (recv_gu2,) = _bwd_in(
        dgu3, wgu2, dr3, xhat2, rstd2, ln2, mod, mix, seq=seq, w_is_nt=True, sc_idx=7, gate_idx=5,
        branch_scale=1.0, final=False, name="ffn2_bwd_in", comm=_ExchangeJob([s16_gu2], [True], [n_gu]))
    g_wout = _matmul_tn(mixin, dmix, tmm=d, tnn=d, name="grad_wout")
    dmixin = _matmul_nt_bf16(dmix, wout, seq=seq, name="out_proj_bwd")
    (dq, dkp, dkc, dvp, dvc, dsink), (recv_d2, sib_out) = _attention_bwd(
        q, k, v, dmixin, sinks, seq=seq, name="attention_bwd",
        comm=_MultiJob([_ExchangeJob([s16_d2], [False], [n_d]), _SwapJob([g_wout], [False])]))
    s32_out, s16_out = pair_sum(g_wout, sib_out, False, "sum_pair_out")
    (dproj, dcw), (recv_out,) = _mix_bwd_assemble(
        dq, dkp, dkc, dvp, dvc, cos_t, sa_t, sb_t, dmixin, ubc, cw_full, seq=seq, name="mix_bwd",
        comm=_ExchangeJob([s16_out], [False], [n_out]))
    g_win_t = _matmul_tn(dproj, h2, tmm=IN_WIDTH // 2, tnn=d, name="grad_win")
    (dr1, df1, dsc2, dsh2, dgate1, dln1g, dln1b), (sib_in,) = _bwd_in(
        dproj, win_t, dr2, xhat1, rstd1, ln1, mod, f1, seq=seq, w_is_nt=False, sc_idx=4, gate_idx=2,
        branch_scale=0.5, final=False, name="in_proj_bwd", comm=_SwapJob([g_win_t], [False]))
    s32_in, s16_in = pair_sum(g_win_t, sib_in, False, "sum_pair_in")
    g_wd1, (recv_in,) = _matmul_tn(a1, df1, tmm=f // 2, tnn=d, name="grad_wd1",
                                   comm=_ExchangeJob([s16_in], [False], [n_in]))
    dgu1, (sib_d1,) = _ffn_bwd_act(df1, wd1, gu1, seq=seq, name="ffn1_bwd_act", comm=_SwapJob([g_wd1], [False]))
    s32_d1, s16_d1 = pair_sum(g_wd1, sib_d1, False, "sum_pair_d1")
    (s32_gu1, s16_gu1), (recv_d1,) = _grad_chip_sum(pos, h1, dgu1, name="grad_wgu1",
                                                    comm=_ExchangeJob([s16_d1], [False], [n_d]))

    def final_half(s32_, recv_, col_kind, n_shard, name_):
        return _sum_final(pos, s32_, recv_, col_kind=col_kind, n_shard=n_shard, name=name_)

    early = [final_half(s32_gu2, recv_gu2, True, n_gu, "sum_final_gu2"),
             final_half(s32_d2, recv_d2, False, n_d, "sum_final_d2"),
             final_half(s32_out, recv_out, False, n_out, "sum_final_out"),
             final_half(s32_in, recv_in, False, n_in, "sum_final_in"),
             final_half(s32_d1, recv_d1, False, n_d, "sum_final_d1")]
    (grad_x, dsc1, dsh1), (recv_gu1, full_gu2, full_d2, full_out, full_in, full_d1) = _bwd_in(
        dgu1, wgu1, dr1, x2, None, None, mod, None, seq=seq, w_is_nt=True, sc_idx=1, gate_idx=None,
        branch_scale=None, final=True, name="ffn1_bwd_in",
        comm=_MultiJob([_ExchangeJob([s16_gu1], [True], [n_gu]), _ShareJob(early)]))
    late = [final_half(s32_gu1, recv_gu1, True, n_gu, "sum_final_gu1")]

    dmod = jnp.concatenate([dsh1, dsc1, dgate1, dsh2, dsc2, dgate2, dsh3, dsc3, dgate3], axis=1)
    loss_row = jnp.sum(loss_cols, axis=1, keepdims=True) * (0.5 / d)
    lane_row = lambda a: jnp.pad(a, ((0, 0), (0, d - a.shape[1])))
    block = jnp.concatenate(
        [dmod.reshape(nb * N_MOD, d), dln1g, dln1b, dln2g, dln2b, dln3g, dln3b,
         lane_row(dcw[0:3, :]), lane_row(dsink[:, 0:1].reshape(1, N_Q_HEADS)), lane_row(loss_row)], axis=0)
    block = jnp.pad(block, ((0, SMALL_ROWS - block.shape[0]), (0, 0)))
    gathered, (full_gu1,) = _allgather8(block, name="gather_small", comm=_ShareJob(late))
    gathered = gathered.reshape(N_DEV, SMALL_ROWS, d)
    dmod_all = gathered[:, :nb * N_MOD, :].reshape(N_DEV * nb, N_MOD * d)
    dmod_shard = lax.dynamic_slice(dmod_all, (0, chip * n_ada), (N_DEV * nb, n_ada))
    small, g_w_ada, g_b_ada = _small_finish(gathered, dmod_all, dmod_shard, c_all.T, name="small_finish")
    r0 = nb * N_MOD
    loss = small[r0 + 10, 0]
    g_ln = [small[r0 + i:r0 + i + 1, :] for i in range(6)]
    g_cw_full = small[r0 + 6:r0 + 9, :CONV_WIDTH]
    g_conv = lax.dynamic_slice(g_cw_full, (0, chip * conv_w.shape[2]), (3, conv_w.shape[2]))
    g_sinks = small[r0 + 9:r0 + 10, :N_Q_HEADS]

    def flat2(a):
        return a.reshape(-1, a.shape[-1])

    def unhalve(a):
        return a.reshape(2 * a.shape[1], a.shape[2])

    results = {}

    def adamw(name_, w_, g_, m_, v_):
        g2 = flat2(g_)
        dl, nm, nv = _adamw(flat2(w_), g2, flat2(m_), flat2(v_), name="adamw_" + name_)
        results[name_] = tuple(a.reshape(w_.shape) for a in (g2, dl, nm, nv))

    adamw("w_ada", w_ada, g_w_ada, m_w_ada, v_w_ada)
    adamw("ffn2_w_gate_up", ffn2_w_gate_up, unhalve(full_gu2), m_ffn2_w_gate_up, v_ffn2_w_gate_up)
    adamw("ffn2_w_down", ffn2_w_down, unhalve(full_d2), m_ffn2_w_down, v_ffn2_w_down)
    adamw("w_out", w_out, unhalve(full_out), m_w_out, v_w_out)
    adamw("w_in", w_in, unhalve(full_in).T, m_w_in, v_w_in)
    adamw("ffn1_w_gate_up", ffn1_w_gate_up, unhalve(full_gu1), m_ffn1_w_gate_up, v_ffn1_w_gate_up)
    adamw("ffn1_w_down", ffn1_w_down, unhalve(full_d1), m_ffn1_w_down, v_ffn1_w_down)
    adamw("b_ada", b_ada, g_b_ada, m_b_ada, v_b_ada)
    adamw("ln1_g", ln1_g, g_ln[0], m_ln1_g, v_ln1_g)
    adamw("ln1_b", ln1_b, g_ln[1], m_ln1_b, v_ln1_b)
    adamw("ln2_g", ln2_g, g_ln[2], m_ln2_g, v_ln2_g)
    adamw("ln2_b", ln2_b, g_ln[3], m_ln2_b, v_ln2_b)
    adamw("ln3_g", ln3_g, g_ln[4], m_ln3_g, v_ln3_g)
    adamw("ln3_b", ln3_b, g_ln[5], m_ln3_b, v_ln3_b)
    adamw("conv_w", conv_w, g_conv, m_conv_w, v_conv_w)
    adamw("attn_sinks", attn_sinks, g_sinks, m_attn_sinks, v_attn_sinks)
    order = ["w_ada", "b_ada", "ffn1_w_gate_up", "ffn1_w_down", "ln1_g", "ln1_b", "w_in", "conv_w", "attn_sinks",
             "w_out", "ln2_g", "ln2_b", "ffn2_w_gate_up", "ffn2_w_down", "ln3_g", "ln3_b"]
    return (loss, grad_x.reshape(x.shape), *[results[n_][0] for n_ in order], *[results[n_][1] for n_ in order],
            *[results[n_][2] for n_ in order], *[results[n_][3] for n_ in order])
```

```python
import jax
import jax.numpy as jnp
from jax import lax
from jax.experimental import pallas as pl
from jax.experimental.pallas import tpu as pltpu

F32 = jnp.float32
BF16 = jnp.bfloat16
MESH = pl.DeviceIdType.MESH

D_MODEL = 1024
HEAD_DIM = 64
ATTN_WIDTH = 512
CONV_WIDTH = 512
N_Q_HEADS = 8
N_KV_HEADS = 2
GQA_GROUP = 4
KV_WIDTH = 128
WINDOW = 128
BLOCK = 128
ROT_DIM = 16
ROPE_THETA = 500000.0
N_MOD = 9
LN_EPS = 1e-5
DN_ALPHA = 2.0 ** 0.25
IN_WIDTH = 2304
N_CHIPS = 4
N_DEV = 8
SMALL_ROWS = 32

ADAM_LR = 0.001
ADAM_B1 = 0.9
ADAM_B2 = 0.999
ADAM_EPS = 1e-08
ADAM_WD = 0.01
ADAM_STEP = 10

LANE = 128
HALO = 16
COL_CHUNK = 256
VMEM_LIMIT = 56 * 1024 * 1024


def _params(sem=None, vmem=True):
    return pltpu.CompilerParams(dimension_semantics=sem, vmem_limit_bytes=VMEM_LIMIT if vmem else None)


def _sigmoid(g):
    return 0.5 * jnp.tanh(0.5 * g) + 0.5


def _row_sum(v):
    return jnp.sum(v, axis=0, keepdims=True)


ROW_CHUNK = 16
EPILOGUE_UNROLL = 8


def _fold8(v):
    return v[0:8, :] + v[8:16, :]


def _row_chunk_loop(n_rows, step, init):
    per_iter = ROW_CHUNK * EPILOGUE_UNROLL
    assert n_rows % per_iter == 0, n_rows

    def body(it, carry):
        for s in range(EPILOGUE_UNROLL):
            start = pl.multiple_of(it * per_iter + s * ROW_CHUNK, ROW_CHUNK)
            carry = step(pl.ds(start, ROW_CHUNK), carry)
        return carry

    return lax.fori_loop(0, n_rows // per_iter, body, init)


def _ln_stats(r):
    mu = jnp.mean(r, axis=-1, keepdims=True)
    rc = r - mu
    var = jnp.mean(rc * rc, axis=-1, keepdims=True)
    rstd = lax.rsqrt(var + LN_EPS)
    return rc * rstd, rstd


def _ln_bwd(dxo, xhat, rstd, g):
    dxhat = dxo * g
    m1 = jnp.mean(dxhat, axis=-1, keepdims=True)
    m2 = jnp.mean(dxhat * xhat, axis=-1, keepdims=True)
    return rstd * (dxhat - m1 - xhat * m2)


def _dot_nt(a, b):
    return lax.dot_general(a, b, (((1,), (1,)), ((), ())), preferred_element_type=F32)


def _dot_tn(a, b):
    return lax.dot_general(a, b, (((0,), (0,)), ((), ())), preferred_element_type=F32)


def _full(shape):
    nd = len(shape)
    return pl.BlockSpec(shape, lambda *_: (0,) * nd)


def _resident(shape):
    nd = len(shape)
    return pl.BlockSpec(shape, lambda *_: (0,) * nd, pipeline_mode=pl.Buffered(1))


ANY_SPEC = pl.BlockSpec(memory_space=pl.ANY)


def _pcall(body, *, name, grid, in_specs, out_specs, out_shape, args, scratch_shapes=(), comm=None, prefetch=None):
    single = not isinstance(out_shape, (list, tuple))
    out_specs = [out_specs] if single else list(out_specs)
    out_shape = [out_shape] if single else list(out_shape)
    in_specs = list(in_specs)
    scratch_shapes = list(scratch_shapes)
    sem = ("arbitrary",) * len(grid)
    n_pre = 0 if prefetch is None else 1
    pre_args = () if prefetch is None else (prefetch,)

    def call(fn, ins_, outs_, shapes_, scratch_, aliases_, operands):
        if prefetch is None:
            return pl.pallas_call(fn, name=name, grid=grid, in_specs=ins_, out_specs=outs_, out_shape=shapes_,
                                  scratch_shapes=scratch_, input_output_aliases=aliases_,
                                  compiler_params=_params(sem))(*operands)
        spec = pltpu.PrefetchScalarGridSpec(num_scalar_prefetch=1, grid=grid, in_specs=ins_, out_specs=outs_,
                                            scratch_shapes=scratch_)
        return pl.pallas_call(fn, name=name, grid_spec=spec, out_shape=shapes_,
                              input_output_aliases={n_pre + i: o for i, o in aliases_.items()},
                              compiler_params=_params(sem))(*pre_args, *operands)

    if comm is None:
        res = call(body, in_specs, out_specs, out_shape, scratch_shapes, {}, args)
        return res[0] if single else res
    n_in, n_out, n_scr = len(in_specs), len(out_specs), len(scratch_shapes)
    nci, nco = len(comm.inputs), len(comm.out_shapes)
    n_steps = 1
    for g in grid:
        n_steps *= g
    staged = n_steps >= 8
    middle_step = (n_steps * 5) // 8 - 1
    late_step = n_steps - 1 - max(1, n_steps // 8)

    def wrapped(*refs):
        pre, refs = refs[:n_pre], refs[n_pre:]
        ins, refs = refs[:n_in], refs[n_in:]
        cin, refs = refs[:nci], refs[nci:]
        outs, refs = refs[:n_out], refs[n_out:]
        cout, refs = refs[:nco], refs[nco:]
        scr, csems = refs[:n_scr], refs[n_scr:]
        step = pl.program_id(0)
        for ax in range(1, len(grid)):
            step = step * grid[ax] + pl.program_id(ax)

        @pl.when(step == 0)
        def _():
            comm.start(cin, cout, csems)

        body(*pre, *ins, *outs, *scr)

        if staged:
            @pl.when(step == middle_step)
            def _():
                comm.middle(cin, cout, csems)

            @pl.when(step == late_step)
            def _():
                comm.late(cin, cout, csems)

        @pl.when(step == n_steps - 1)
        def _():
            if not staged:
                comm.middle(cin, cout, csems)
                comm.late(cin, cout, csems)
            comm.finish(cin, cout, csems)

    res = call(wrapped, in_specs + [ANY_SPEC] * nci, out_specs + [ANY_SPEC] * nco,
               out_shape + list(comm.out_shapes), scratch_shapes + list(comm.sems),
               {n_in + i: n_out + o for i, o in comm.aliases.items()}, (*args, *comm.inputs))
    main = res[:n_out]
    return (main[0] if single else main), list(res[n_out:])


def _ffn_up(xin, lnp, mod, w, *, seq, sc_idx, sh_idx, use_ln, name, comm=None):
    t, d = xin.shape
    f = w.shape[1] // 2
    tm = min(512, seq)
    tpb = seq // tm
    ch = min(COL_CHUNK, f)

    def body(x_ref, ln_ref, mod_ref, w_ref, h_ref, a_ref, dact_ref):
        x = x_ref[...]
        if use_ln:
            x = x * ln_ref[0:1, :] + ln_ref[1:2, :]
        h = x * (1.0 + mod_ref[0, sc_idx:sc_idx + 1, :]) + mod_ref[0, sh_idx:sh_idx + 1, :]
        hb = h.astype(BF16)
        h_ref[...] = hb
        for j in range(f // ch):
            g = jnp.dot(hb, w_ref[:, j * ch:(j + 1) * ch], preferred_element_type=F32)
            u = jnp.dot(hb, w_ref[:, f + j * ch:f + (j + 1) * ch], preferred_element_type=F32)
            s = _sigmoid(g)
            silu = g * s
            a_ref[:, j * ch:(j + 1) * ch] = (silu * u).astype(BF16)
            dact_ref[:, j * ch:(j + 1) * ch] = (u * (s + silu * (1.0 - s))).astype(BF16)
            dact_ref[:, f + j * ch:f + (j + 1) * ch] = silu.astype(BF16)

    return _pcall(
        body, name=name, grid=(t // tm,),
        in_specs=[pl.BlockSpec((tm, d), lambda i: (i, 0)), _full((2, d)),
                  pl.BlockSpec((1, N_MOD, d), lambda i: (i // tpb, 0, 0)), _resident((d, 2 * f))],
        out_specs=[pl.BlockSpec((tm, d), lambda i: (i, 0)), pl.BlockSpec((tm, f), lambda i: (i, 0)),
                   pl.BlockSpec((tm, 2 * f), lambda i: (i, 0))],
        out_shape=[jax.ShapeDtypeStruct((t, d), BF16), jax.ShapeDtypeStruct((t, f), BF16),
                   jax.ShapeDtypeStruct((t, 2 * f), BF16)],
        args=(xin, lnp, mod, w), comm=comm)


def _ffn_down_ln(a, wd, xin, lnp_in, mod, *, seq, gate_idx, use_ln, name, comm=None):
    t, f = a.shape
    d = wd.shape[1]
    tm = min(512, seq)
    tpb = seq // tm

    def body(a_ref, wd_ref, x_ref, ln_ref, mod_ref, f_ref, xhat_ref, rstd_ref, acc):
        av = a_ref[...]
        for j in range(d // COL_CHUNK):
            acc[:, j * COL_CHUNK:(j + 1) * COL_CHUNK] = jnp.dot(
                av, wd_ref[:, j * COL_CHUNK:(j + 1) * COL_CHUNK], preferred_element_type=F32)
        scale = 0.5 * (1.0 + mod_ref[0, gate_idx:gate_idx + 1, :])

        fo = acc[...]
        x = x_ref[...]
        if use_ln:
            x = x * ln_ref[0:1, :] + ln_ref[1:2, :]
        xhat, rstd = _ln_stats(DN_ALPHA * x + scale * fo)
        f_ref[...] = fo.astype(BF16)
        xhat_ref[...] = xhat
        rstd_ref[...] = rstd

    return _pcall(
        body, name=name, grid=(t // tm,),
        in_specs=[pl.BlockSpec((tm, f), lambda i: (i, 0)), _resident((f, d)),
                  pl.BlockSpec((tm, d), lambda i: (i, 0)), _full((2, d)),
                  pl.BlockSpec((1, N_MOD, d), lambda i: (i // tpb, 0, 0))],
        out_specs=[pl.BlockSpec((tm, d), lambda i: (i, 0)), pl.BlockSpec((tm, d), lambda i: (i, 0)),
                   pl.BlockSpec((tm, 1), lambda i: (i, 0))],
        out_shape=[jax.ShapeDtypeStruct((t, d), BF16), jax.ShapeDtypeStruct((t, d), F32),
                   jax.ShapeDtypeStruct((t, 1), F32)],
        scratch_shapes=[pltpu.VMEM((tm, d), F32)],
        args=(a, wd, xin, lnp_in, mod), comm=comm)


def _ffn_down_loss(a, wd, xhat_in, lnp_in, mod, lnp_out, tgt, *, seq, gate_idx, name):
    t, f = a.shape
    d = wd.shape[1]
    nb = t // seq
    tm = min(512, seq)
    tpb = seq // tm

    def body(a_ref, wd_ref, x_ref, lnin_ref, mod_ref, lnout_ref, tgt_ref,
             dr_ref, df_ref, loss_ref, dg_ref, db_ref, dgate_ref, acc):
        i = pl.program_id(0)
        av = a_ref[...]
        for j in range(d // COL_CHUNK):
            acc[:, j * COL_CHUNK:(j + 1) * COL_CHUNK] = jnp.dot(
                av, wd_ref[:, j * COL_CHUNK:(j + 1) * COL_CHUNK], preferred_element_type=F32)
        scale = 0.5 * (1.0 + mod_ref[0, gate_idx:gate_idx + 1, :])
        ag_in, ab_in = DN_ALPHA * lnin_ref[0:1, :], DN_ALPHA * lnin_ref[1:2, :]
        g_out, b_out = lnout_ref[0:1, :], lnout_ref[1:2, :]
        g_over_d = g_out * (1.0 / d)

        def chunk(rows, carry):
            s_loss, s_dg, s_db, s_gate = carry
            fo = acc[rows, :]
            xhat, rstd = _ln_stats(x_ref[rows, :] * ag_in + ab_in + scale * fo)
            e = xhat * g_out + b_out - tgt_ref[rows, :]
            dr = _ln_bwd(e, xhat, rstd, g_over_d)
            dr_ref[rows, :] = dr
            df_ref[rows, :] = (scale * dr).astype(BF16)
            return s_loss + _fold8(e * e), s_dg + _fold8(e * xhat), s_db + _fold8(e), s_gate + _fold8(fo * dr)

        zero = jnp.zeros((8, d), F32)
        s_loss, s_dg, s_db, s_gate = _row_chunk_loop(tm, chunk, (zero, zero, zero, zero))
        s_dg, s_db, s_gate = s_dg * (1.0 / d), s_db * (1.0 / d), s_gate * 0.5

        @pl.when(i == 0)
        def _():
            loss_ref[...] = jnp.zeros_like(loss_ref)
            dg_ref[...] = jnp.zeros_like(dg_ref)
            db_ref[...] = jnp.zeros_like(db_ref)

        @pl.when(i % tpb == 0)
        def _():
            dgate_ref[...] = jnp.zeros_like(dgate_ref)

        loss_ref[...] += _row_sum(s_loss)
        dg_ref[...] += _row_sum(s_dg)
        db_ref[...] += _row_sum(s_db)
        dgate_ref[0] += _row_sum(s_gate)

    return pl.pallas_call(
        body, name=name, grid=(t // tm,), scratch_shapes=[pltpu.VMEM((tm, d), F32)],
        in_specs=[pl.BlockSpec((tm, f), lambda i: (i, 0)), _resident((f, d)),
                  pl.BlockSpec((tm, d), lambda i: (i, 0)), _full((2, d)),
                  pl.BlockSpec((1, N_MOD, d), lambda i: (i // tpb, 0, 0)), _full((2, d)),
                  pl.BlockSpec((tm, d), lambda i: (i, 0))],
        out_specs=[pl.BlockSpec((tm, d), lambda i: (i, 0)), pl.BlockSpec((tm, d), lambda i: (i, 0)),
                   _full((1, d)), _full((1, d)), _full((1, d)),
                   pl.BlockSpec((1, 1, d), lambda i: (i // tpb, 0, 0))],
        out_shape=[jax.ShapeDtypeStruct((t, d), F32), jax.ShapeDtypeStruct((t, d), BF16),
                   jax.ShapeDtypeStruct((1, d), F32), jax.ShapeDtypeStruct((1, d), F32),
                   jax.ShapeDtypeStruct((1, d), F32), jax.ShapeDtypeStruct((nb, 1, d), F32)],
        compiler_params=_params(("arbitrary",)),
    )(a, wd, xhat_in, lnp_in, mod, lnp_out, tgt)


def _rope(v, cos, sa, sb):
    return v * cos + pltpu.roll(v, LANE - ROT_DIM // 2, 1) * sa + pltpu.roll(v, ROT_DIM // 2, 1) * sb


def _rope_t(dy, cos, sa, sb):
    return dy * cos + pltpu.roll(dy * sa, ROT_DIM // 2, 1) + pltpu.roll(dy * sb, LANE - ROT_DIM // 2, 1)


def _in_proj(xhat, lnp, mod, w_t, cos, sa, sb, *, seq, sc_idx, sh_idx, name, comm=None):
    t, d = xhat.shape
    tm = min(512, seq)
    tpb = seq // tm
    n_conv = 3 * CONV_WIDTH

    def body(x_ref, ln_ref, mod_ref, w_ref, cos_ref, sa_ref, sb_ref, h_ref, q_ref, k_ref, v_ref, ubc_ref):
        x = x_ref[...] * ln_ref[0:1, :] + ln_ref[1:2, :]
        h = x * (1.0 + mod_ref[0, sc_idx:sc_idx + 1, :]) + mod_ref[0, sh_idx:sh_idx + 1, :]
        hb = h.astype(BF16)
        h_ref[...] = hb
        cos_t, sa_t, sb_t = cos_ref[...], sa_ref[...], sb_ref[...]
        for j in range(ATTN_WIDTH // COL_CHUNK):
            p = _dot_nt(hb, w_ref[j * COL_CHUNK:(j + 1) * COL_CHUNK, :])
            for s in range(COL_CHUNK // LANE):
                q_ref[:, j * COL_CHUNK + s * LANE:j * COL_CHUNK + (s + 1) * LANE] = _rope(
                    p[:, s * LANE:(s + 1) * LANE], cos_t, sa_t, sb_t).astype(BF16)
        p = _dot_nt(hb, w_ref[ATTN_WIDTH:ATTN_WIDTH + 2 * KV_WIDTH, :])
        k_ref[...] = _rope(p[:, 0:KV_WIDTH], cos_t, sa_t, sb_t).astype(BF16)
        v_ref[...] = p[:, KV_WIDTH:].astype(BF16)
        base = ATTN_WIDTH + 2 * KV_WIDTH
        for j in range(n_conv // COL_CHUNK):
            ubc_ref[:, j * COL_CHUNK:(j + 1) * COL_CHUNK] = _dot_nt(
                hb, w_ref[base + j * COL_CHUNK:base + (j + 1) * COL_CHUNK, :]).astype(BF16)

    row = lambda w: pl.BlockSpec((tm, w), lambda i: (i, 0))
    return _pcall(
        body, name=name, grid=(t // tm,),
        in_specs=[row(d), _full((2, d)), pl.BlockSpec((1, N_MOD, d), lambda i: (i // tpb, 0, 0)),
                  _resident((IN_WIDTH, d)), row(LANE), row(LANE), row(LANE)],
        out_specs=[row(d), row(ATTN_WIDTH), row(KV_WIDTH), row(KV_WIDTH), row(n_conv)],
        out_shape=[jax.ShapeDtypeStruct((t, d), BF16), jax.ShapeDtypeStruct((t, ATTN_WIDTH), BF16),
                   jax.ShapeDtypeStruct((t, KV_WIDTH), BF16), jax.ShapeDtypeStruct((t, KV_WIDTH), BF16),
                   jax.ShapeDtypeStruct((t, n_conv), BF16)],
        args=(xhat, lnp, mod, w_t, cos, sa, sb), comm=comm)


ATTN_TILE_BLOCKS = 2


def _attn_sub_block(s, tile, nblk, kp_ref, kc_ref, vp_ref, vc_ref):
    rows = slice(s * BLOCK, (s + 1) * BLOCK)
    if s == 0:
        first = ((tile * ATTN_TILE_BLOCKS) % nblk) == 0
        return rows, (kp_ref, slice(0, BLOCK)), (kc_ref, rows), (vp_ref, slice(0, BLOCK)), (vc_ref, rows), first
    before = slice((s - 1) * BLOCK, s * BLOCK)
    return rows, (kc_ref, before), (kc_ref, rows), (vc_ref, before), (vc_ref, rows), False


def _attn_group(q_ref, rows, k_prev, k_cur, v_prev, v_cur, sink_ref, g, first):
    lo, hi = g * HEAD_DIM, (g + 1) * HEAD_DIM
    kk = jnp.concatenate([k_prev[0][k_prev[1], lo:hi], k_cur[0][k_cur[1], lo:hi]], axis=0)
    vv = jnp.concatenate([v_prev[0][v_prev[1], lo:hi], v_cur[0][v_cur[1], lo:hi]], axis=0)
    qs = jnp.concatenate([q_ref[rows, (GQA_GROUP * g + j) * HEAD_DIM:(GQA_GROUP * g + j + 1) * HEAD_DIM]
                          for j in range(GQA_GROUP)], axis=0)
    cols = GQA_GROUP * BLOCK
    ki = lax.broadcasted_iota(jnp.int32, (2 * BLOCK, cols), 0)
    col = lax.broadcasted_iota(jnp.int32, (2 * BLOCK, cols), 1)
    diff = (col & (BLOCK - 1)) + BLOCK - ki
    valid = (diff >= 0) & (diff < WINDOW) & ((ki >= BLOCK) | jnp.logical_not(first))
    s = _dot_nt(kk, qs) * (HEAD_DIM ** -0.5)
    s = jnp.where(valid, s, -1e30)
    hcol = lax.broadcasted_iota(jnp.int32, (1, cols), 1)
    sink = jnp.zeros((1, cols), F32)
    for j in range(GQA_GROUP):
        sink = jnp.where(hcol // BLOCK == j, sink_ref[GQA_GROUP * g + j], sink)
    m = jnp.maximum(jnp.max(s, axis=0, keepdims=True), sink)
    p = jnp.exp(s - m)
    ps = jnp.exp(sink - m)
    inv = 1.0 / (jnp.sum(p, axis=0, keepdims=True) + ps)
    return qs, kk, vv, p * inv, ps * inv


def _heads_to_lanes(x_t):
    return jnp.concatenate([x_t[:, j * BLOCK:(j + 1) * BLOCK].T for j in range(GQA_GROUP)], axis=1)


def _attention(q, k, v, sinks, *, seq, name, comm=None):
    t = q.shape[0]
    nblk = seq // BLOCK
    tile = ATTN_TILE_BLOCKS * BLOCK

    def body(q_ref, kp_ref, kc_ref, vp_ref, vc_ref, sink_ref, o_ref):
        for s in range(ATTN_TILE_BLOCKS):
            rows, k_prev, k_cur, v_prev, v_cur, first = _attn_sub_block(
                s, pl.program_id(0), nblk, kp_ref, kc_ref, vp_ref, vc_ref)
            outs = []
            for g in range(N_KV_HEADS):
                _, _, vv, pn, _ = _attn_group(q_ref, rows, k_prev, k_cur, v_prev, v_cur, sink_ref, g, first)
                outs.append(_heads_to_lanes(_dot_tn(vv, pn.astype(BF16))))
            o_ref[rows, :] = jnp.concatenate(outs, axis=1).astype(BF16)

    cur = lambda w: pl.BlockSpec((tile, w), lambda n: (n, 0))
    prev = lambda w: pl.BlockSpec((BLOCK, w), lambda n: (jnp.maximum(n * ATTN_TILE_BLOCKS - 1, 0), 0))
    return _pcall(
        body, name=name, grid=(t // tile,),
        in_specs=[cur(ATTN_WIDTH), prev(KV_WIDTH), cur(KV_WIDTH), prev(KV_WIDTH), cur(KV_WIDTH),
                  pl.BlockSpec(memory_space=pltpu.SMEM)],
        out_specs=cur(ATTN_WIDTH),
        out_shape=jax.ShapeDtypeStruct((t, ATTN_WIDTH), BF16),
        args=(q, k, k, v, v, sinks), comm=comm)


def _out_proj(attn, ubc, cw, wout, xhat_in, lnp_in, mod, *, seq, gate_idx, name, comm=None):
    t, d = xhat_in.shape
    tm = min(512, seq)
    tpb = seq // tm
    cwid = CONV_WIDTH

    def body(attn_ref, ubc_ref, halo_ref, cw_ref, w_ref, x_ref, ln_ref, mod_ref,
             mixin_ref, mix_ref, xhat_ref, rstd_ref, zbuf, acc):
        first = (pl.program_id(0) % tpb) == 0
        u, bg, cg = (ubc_ref[:, s * cwid:(s + 1) * cwid].astype(F32) for s in range(3))
        z = cg * u
        hz = halo_ref[:, 2 * cwid:3 * cwid].astype(F32) * halo_ref[:, 0:cwid].astype(F32)
        zbuf[0:HALO, :] = jnp.where(first, 0.0, hz)
        zbuf[HALO:HALO + tm, :] = z
        y = (cw_ref[0:1, :] * zbuf[HALO - 2:HALO - 2 + tm, :] + cw_ref[1:2, :] * zbuf[HALO - 1:HALO - 1 + tm, :]
             + cw_ref[2:3, :] * z)
        mixin_ref[:, 0:ATTN_WIDTH] = attn_ref[...]
        mixin_ref[:, ATTN_WIDTH:] = (bg * y).astype(BF16)
        mv = mixin_ref[...]
        for j in range(d // COL_CHUNK):
            acc[:, j * COL_CHUNK:(j + 1) * COL_CHUNK] = jnp.dot(
                mv, w_ref[:, j * COL_CHUNK:(j + 1) * COL_CHUNK], preferred_element_type=F32)
        scale = 1.0 + mod_ref[0, gate_idx:gate_idx + 1, :]

        mix = acc[...]
        xhat, rstd = _ln_stats(DN_ALPHA * (x_ref[...] * ln_ref[0:1, :] + ln_ref[1:2, :]) + scale * mix)
        mix_ref[...] = mix.astype(BF16)
        xhat_ref[...] = xhat
        rstd_ref[...] = rstd

    row = lambda w: pl.BlockSpec((tm, w), lambda i: (i, 0))
    return _pcall(
        body, name=name, grid=(t // tm,),
        in_specs=[row(ATTN_WIDTH), row(3 * cwid),
                  pl.BlockSpec((HALO, 3 * cwid), lambda i: (jnp.maximum(i * (tm // HALO) - 1, 0), 0)),
                  _full((8, cwid)), _resident((d, d)), row(d), _full((2, d)),
                  pl.BlockSpec((1, N_MOD, d), lambda i: (i // tpb, 0, 0))],
        out_specs=[row(d), row(d), row(d), row(1)],
        out_shape=[jax.ShapeDtypeStruct((t, d), BF16), jax.ShapeDtypeStruct((t, d), BF16),
                   jax.ShapeDtypeStruct((t, d), F32), jax.ShapeDtypeStruct((t, 1), F32)],
        scratch_shapes=[pltpu.VMEM((tm + HALO, cwid), F32), pltpu.VMEM((tm, d), F32)],
        args=(attn, ubc, ubc, cw, wout, xhat_in, lnp_in, mod), comm=comm)


def _ffn_bwd_act(df, wd, dact, *, seq, name, comm=None):
    t, d = df.shape
    f = wd.shape[0]
    tm = min(512, seq)
    ch = min(COL_CHUNK, f)

    def body(df_ref, wd_ref, dact_ref, dgu_ref):
        dfv = df_ref[...]
        for j in range(f // ch):
            da = _dot_nt(dfv, wd_ref[j * ch:(j + 1) * ch, :])
            dgu_ref[:, j * ch:(j + 1) * ch] = (da * dact_ref[:, j * ch:(j + 1) * ch].astype(F32)).astype(BF16)
            dgu_ref[:, f + j * ch:f + (j + 1) * ch] = (
                da * dact_ref[:, f + j * ch:f + (j + 1) * ch].astype(F32)).astype(BF16)

    return _pcall(
        body, name=name, grid=(t // tm,),
        in_specs=[pl.BlockSpec((tm, d), lambda i: (i, 0)), _resident((f, d)),
                  pl.BlockSpec((tm, 2 * f), lambda i: (i, 0))],
        out_specs=pl.BlockSpec((tm, 2 * f), lambda i: (i, 0)),
        out_shape=jax.ShapeDtypeStruct((t, 2 * f), BF16),
        args=(df, wd, dact), comm=comm)


def _bwd_in(a, w, dr, xin, rstd_prev, lnp_prev, mod, branch_prev, *, seq, w_is_nt, sc_idx, gate_idx,
            branch_scale, final, name, comm=None):
    t, kdim = a.shape
    d = dr.shape[1]
    nb = t // seq
    tm = min(512, seq)
    tpb = seq // tm

    def body(*refs):
        if final:
            a_ref, w_ref, dr_ref, x_ref, mod_ref, dx_ref, dsc_ref, dsh_ref, acc = refs
        else:
            (a_ref, w_ref, dr_ref, x_ref, rstd_ref, ln_ref, mod_ref, br_ref,
             drp_ref, dbr_ref, dsc_ref, dsh_ref, dgate_ref, dg_ref, db_ref, acc) = refs
        i = pl.program_id(0)
        av = a_ref[...]
        for j in range(d // COL_CHUNK):
            cols = slice(j * COL_CHUNK, (j + 1) * COL_CHUNK)
            acc[:, cols] = (_dot_nt(av, w_ref[cols, :]) if w_is_nt
                            else jnp.dot(av, w_ref[:, cols], preferred_element_type=F32))
        sc1 = 1.0 + mod_ref[0, sc_idx:sc_idx + 1, :]
        if not final:
            g_prev, b_prev = ln_ref[0:1, :], ln_ref[1:2, :]
            bscale = branch_scale * (1.0 + mod_ref[0, gate_idx:gate_idx + 1, :])

        def chunk(rows, carry):
            dh = acc[rows, :]
            dx = DN_ALPHA * dr_ref[rows, :] + dh * sc1
            if final:
                dx_ref[rows, :] = dx
                return carry[0] + _fold8(dh * x_ref[rows, :]), carry[1] + _fold8(dh)
            xhat = x_ref[rows, :]
            drp = _ln_bwd(dx, xhat, rstd_ref[rows, :], g_prev)
            drp_ref[rows, :] = drp
            dbr_ref[rows, :] = (bscale * drp).astype(BF16)
            return (carry[0] + _fold8(dh * xhat), carry[1] + _fold8(dh),
                    carry[2] + _fold8(br_ref[rows, :].astype(F32) * drp),
                    carry[3] + _fold8(dx * xhat), carry[4] + _fold8(dx))

        zero = jnp.zeros((8, d), F32)
        sums = list(_row_chunk_loop(tm, chunk, (zero,) * (2 if final else 5)))
        if not final:
            sums[0] = sums[0] * g_prev + sums[1] * b_prev
            sums[2] = sums[2] * branch_scale

        @pl.when((i % tpb) == 0)
        def _():
            dsc_ref[...] = jnp.zeros_like(dsc_ref)
            dsh_ref[...] = jnp.zeros_like(dsh_ref)
            if not final:
                dgate_ref[...] = jnp.zeros_like(dgate_ref)

        dsc_ref[0] += _row_sum(sums[0])
        dsh_ref[0] += _row_sum(sums[1])
        if not final:
            @pl.when(i == 0)
            def _():
                dg_ref[...] = jnp.zeros_like(dg_ref)
                db_ref[...] = jnp.zeros_like(db_ref)

            dgate_ref[0] += _row_sum(sums[2])
            dg_ref[...] += _row_sum(sums[3])
            db_ref[...] += _row_sum(sums[4])

    row = lambda w_: pl.BlockSpec((tm, w_), lambda i: (i, 0))
    vec = pl.BlockSpec((1, 1, d), lambda i: (i // tpb, 0, 0))
    mod_spec = pl.BlockSpec((1, N_MOD, d), lambda i: (i // tpb, 0, 0))
    vshape = jax.ShapeDtypeStruct((nb, 1, d), F32)
    if final:
        in_specs = [row(kdim), _resident(w.shape), row(d), row(d), mod_spec]
        args = (a, w, dr, xin, mod)
        out_specs = [row(d), vec, vec]
        out_shape = [jax.ShapeDtypeStruct((t, d), F32), vshape, vshape]
    else:
        in_specs = [row(kdim), _resident(w.shape), row(d), row(d), row(1), _full((2, d)), mod_spec, row(d)]
        args = (a, w, dr, xin, rstd_prev, lnp_prev, mod, branch_prev)
        out_specs = [row(d), row(d), vec, vec, vec, _full((1, d)), _full((1, d))]
        out_shape = [jax.ShapeDtypeStruct((t, d), F32), jax.ShapeDtypeStruct((t, d), BF16), vshape, vshape, vshape,
                     jax.ShapeDtypeStruct((1, d), F32), jax.ShapeDtypeStruct((1, d), F32)]
    return _pcall(
        body, name=name, grid=(t // tm,), in_specs=in_specs, out_specs=out_specs, out_shape=out_shape,
        scratch_shapes=[pltpu.VMEM((tm, d), F32)], args=args, comm=comm)


def _matmul_tn(a, b, *, tmm, tnn, name, comm=None):
    t, m = a.shape
    n = b.shape[1]
    tk = min(2048, t)

    def body(a_ref, b_ref, o_ref):
        @pl.when(pl.program_id(2) == 0)
        def _():
            o_ref[...] = jnp.zeros_like(o_ref)
        o_ref[...] += _dot_tn(a_ref[...], b_ref[...])

    return _pcall(
        body, name=name, grid=(m // tmm, n // tnn, t // tk),
        in_specs=[pl.BlockSpec((tk, tmm), lambda i, j, k: (k, i)), pl.BlockSpec((tk, tnn), lambda i, j, k: (k, j))],
        out_specs=pl.BlockSpec((tmm, tnn), lambda i, j, k: (i, j)),
        out_shape=jax.ShapeDtypeStruct((m, n), F32),
        args=(a, b), comm=comm)


def _grad_chip_sum(pos, a, b, *, name, comm=None):
    t, m = a.shape
    n = b.shape[1]
    hm, tnn = m // 2, n // N_CHIPS
    tk = min(2048, t)
    nk = t // tk
    n_j = n // tnn

    def body(pos_ref, a_ref, b_ref, s32_ref, s16_ref, land_ref, acc, theirs, send_sems, recv_sems, copy_sem):
        p, j, k = pl.program_id(0), pl.program_id(1), pl.program_id(2)
        x, y, c = _position()

        def push(jj):
            return pltpu.make_async_remote_copy(
                src_ref=acc.at[jj], dst_ref=land_ref.at[jj], send_sem=send_sems.at[jj], recv_sem=recv_sems.at[jj],
                device_id=(x, y, 1 - c), device_id_type=MESH)

        fetch = pltpu.make_async_copy(land_ref.at[j], theirs, copy_sem)

        @pl.when(jnp.logical_and(p == 1, k == 0))
        def _():
            push(j).wait_send()
            push(j).wait_recv()
            fetch.start()

        part = _dot_tn(a_ref[...], b_ref[...])

        @pl.when(k == 0)
        def _():
            acc[j] = part

        @pl.when(k > 0)
        def _():
            acc[j] += part

        @pl.when(jnp.logical_and(p == 0, k == nk - 1))
        def _():
            push(j).start()

        @pl.when(jnp.logical_and(p == 1, k == nk - 1))
        def _():
            fetch.wait()
            s = acc[j] + theirs[...]
            s32_ref[0] = s
            s16_ref[0] = s.astype(BF16)

    half = lambda p, pos_ref: 1 - pos_ref[2] - p + 2 * p * pos_ref[2]
    out_tile = pl.BlockSpec((1, hm, tnn), lambda p, j, k, pos_ref: (0, 0, j * p))
    shape = lambda dt: jax.ShapeDtypeStruct((1, hm, n), dt)
    out = _pcall(
        body, name=name, grid=(2, n_j, nk),
        in_specs=[pl.BlockSpec((tk, hm), lambda p, j, k, pos_ref: (k, half(p, pos_ref))),
                  pl.BlockSpec((tk, tnn), lambda p, j, k, pos_ref: (k, j))],
        out_specs=[out_tile, out_tile, ANY_SPEC],
        out_shape=[shape(F32), shape(BF16), jax.ShapeDtypeStruct((n_j, hm, tnn), F32)],
        scratch_shapes=[pltpu.VMEM((n_j, hm, tnn), F32), pltpu.VMEM((hm, tnn), F32),
                        pltpu.SemaphoreType.DMA((n_j,)), pltpu.SemaphoreType.DMA((n_j,)), pltpu.SemaphoreType.DMA],
        args=(a, b), prefetch=pos, comm=comm)
    if comm is None:
        return out[0], out[1]
    (s32, s16, _), extra = out
    return (s32, s16), extra


def _matmul_nt_bf16(a, w, *, seq, name):
    t, kdim = a.shape
    n = w.shape[0]
    tm = min(512, seq)

    def body(a_ref, w_ref, o_ref):
        av = a_ref[...]
        for j in range(n // COL_CHUNK):
            o_ref[:, j * COL_CHUNK:(j + 1) * COL_CHUNK] = _dot_nt(
                av, w_ref[j * COL_CHUNK:(j + 1) * COL_CHUNK, :]).astype(BF16)

    return pl.pallas_call(
        body, name=name, grid=(t // tm,),
        in_specs=[pl.BlockSpec((tm, kdim), lambda i: (i, 0)), _resident((n, kdim))],
        out_specs=pl.BlockSpec((tm, n), lambda i: (i, 0)),
        out_shape=jax.ShapeDtypeStruct((t, n), BF16),
        compiler_params=_params(("arbitrary",)),
    )(a, w)


def _attention_bwd(q, k, v, dmixin, sinks, *, seq, name, comm=None):
    t = q.shape[0]
    nblk = seq // BLOCK
    tile = ATTN_TILE_BLOCKS * BLOCK

    def body(q_ref, kp_ref, kc_ref, vp_ref, vc_ref, do_ref, sink_ref,
             dq_ref, dkp_ref, dkc_ref, dvp_ref, dvc_ref, dsink_ref):
        n = pl.program_id(0)

        @pl.when(n == 0)
        def _():
            dsink_ref[...] = jnp.zeros_like(dsink_ref)

        srow = lax.broadcasted_iota(jnp.int32, (8, LANE), 0)
        dsink = jnp.zeros((8, LANE), F32)
        for s in range(ATTN_TILE_BLOCKS):
            rows, k_prev, k_cur, v_prev, v_cur, first = _attn_sub_block(s, n, nblk, kp_ref, kc_ref, vp_ref, vc_ref)
            dqs, dks, dvs = [], [], []
            for g in range(N_KV_HEADS):
                qs, kk, vv, pn, psn = _attn_group(q_ref, rows, k_prev, k_cur, v_prev, v_cur, sink_ref, g, first)
                dos = jnp.concatenate(
                    [do_ref[rows, (GQA_GROUP * g + j) * HEAD_DIM:(GQA_GROUP * g + j + 1) * HEAD_DIM]
                     for j in range(GQA_GROUP)], axis=0)
                dp = _dot_nt(vv, dos)
                delta = jnp.sum(pn * dp, axis=0, keepdims=True)
                ds = pn * (dp - delta)
                dsk = psn * delta
                for j in range(GQA_GROUP):
                    tot = jnp.sum(dsk[:, j * BLOCK:(j + 1) * BLOCK], axis=1, keepdims=True)
                    dsink = dsink - jnp.where(srow == GQA_GROUP * g + j, tot, 0.0)
                dsb = (ds * (HEAD_DIM ** -0.5)).astype(BF16)
                dqs.append(_heads_to_lanes(_dot_tn(kk, dsb)))
                dks.append(jnp.dot(dsb, qs, preferred_element_type=F32))
                dvs.append(jnp.dot(pn.astype(BF16), dos, preferred_element_type=F32))
            dq_ref[rows, :] = jnp.concatenate(dqs, axis=1)
            dkp_ref[rows, :] = jnp.concatenate([x[0:BLOCK, :] for x in dks], axis=1)
            dkc_ref[rows, :] = jnp.concatenate([x[BLOCK:, :] for x in dks], axis=1)
            dvp_ref[rows, :] = jnp.concatenate([x[0:BLOCK, :] for x in dvs], axis=1)
            dvc_ref[rows, :] = jnp.concatenate([x[BLOCK:, :] for x in dvs], axis=1)
        dsink_ref[...] += dsink

    cur = lambda w: pl.BlockSpec((tile, w), lambda n: (n, 0))
    prev = lambda w: pl.BlockSpec((BLOCK, w), lambda n: (jnp.maximum(n * ATTN_TILE_BLOCKS - 1, 0), 0))
    kv = jax.ShapeDtypeStruct((t, KV_WIDTH), F32)
    return _pcall(
        body, name=name, grid=(t // tile,),
        in_specs=[cur(ATTN_WIDTH), prev(KV_WIDTH), cur(KV_WIDTH), prev(KV_WIDTH), cur(KV_WIDTH), cur(ATTN_WIDTH),
                  pl.BlockSpec(memory_space=pltpu.SMEM)],
        out_specs=[cur(ATTN_WIDTH), cur(KV_WIDTH), cur(KV_WIDTH), cur(KV_WIDTH), cur(KV_WIDTH), _full((8, LANE))],
        out_shape=[jax.ShapeDtypeStruct((t, ATTN_WIDTH), F32), kv, kv, kv, kv, jax.ShapeDtypeStruct((8, LANE), F32)],
        args=(q, k, k, v, v, dmixin, sinks), comm=comm)


def _mix_bwd_assemble(dq, dkp, dkc, dvp, dvc, cos, sa, sb, dmixin, ubc, cw, *, seq, name, comm=None):
    t = dq.shape[0]
    cwid = CONV_WIDTH
    tm = min(2 * BLOCK, seq)
    tiles_per_seq = seq // tm
    ntile = t // tm
    nblk_all = t // BLOCK
    per_tile = tm // BLOCK

    def body(*refs):
        dq_ref, dkc_ref, dvc_ref = refs[0:3]
        dkp_refs, dvp_refs = refs[3:3 + per_tile], refs[3 + per_tile:3 + 2 * per_tile]
        (cos_ref, sa_ref, sb_ref, dco_ref, dcon_ref, ubc_ref, hprev_ref, hnext_ref, cw_ref,
         dproj_ref, dcw_ref, zbuf, dybuf) = refs[3 + 2 * per_tile:]
        i = pl.program_id(0)
        first = (i % tiles_per_seq) == 0
        last = (i % tiles_per_seq) == tiles_per_seq - 1
        glast = i == ntile - 1

        @pl.when(i == 0)
        def _():
            dcw_ref[...] = jnp.zeros_like(dcw_ref)

        def with_next_block(cur_ref, nxt_refs):
            nxt = [r[...] for r in nxt_refs]
            nxt[-1] = jnp.where(glast, 0.0, nxt[-1])
            return cur_ref[...] + jnp.concatenate(nxt, axis=0)

        cos_t, sa_t, sb_t = cos_ref[...], sa_ref[...], sb_ref[...]
        for j in range(ATTN_WIDTH // LANE):
            dproj_ref[:, j * LANE:(j + 1) * LANE] = _rope_t(
                dq_ref[:, j * LANE:(j + 1) * LANE], cos_t, sa_t, sb_t).astype(BF16)
        dk = with_next_block(dkc_ref, dkp_refs)
        dproj_ref[:, ATTN_WIDTH:ATTN_WIDTH + KV_WIDTH] = _rope_t(dk, cos_t, sa_t, sb_t).astype(BF16)
        dv = with_next_block(dvc_ref, dvp_refs)
        dproj_ref[:, ATTN_WIDTH + KV_WIDTH:ATTN_WIDTH + 2 * KV_WIDTH] = dv.astype(BF16)

        u, bg, cg = (ubc_ref[:, s * cwid:(s + 1) * cwid].astype(F32) for s in range(3))
        z = cg * u
        hz = hprev_ref[:, 2 * cwid:3 * cwid].astype(F32) * hprev_ref[:, 0:cwid].astype(F32)
        zbuf[0:HALO, :] = jnp.where(first, 0.0, hz)
        zbuf[HALO:HALO + tm, :] = z
        z2, z1 = zbuf[HALO - 2:HALO - 2 + tm, :], zbuf[HALO - 1:HALO - 1 + tm, :]
        w0, w1, w2 = cw_ref[0:1, :], cw_ref[1:2, :], cw_ref[2:3, :]
        y = w0 * z2 + w1 * z1 + w2 * z
        dco = dco_ref[...].astype(F32)
        dyc = dco * bg
        dyn = dcon_ref[...].astype(F32) * hnext_ref[:, cwid:2 * cwid].astype(F32)
        dybuf[0:tm, :] = dyc
        dybuf[tm:tm + HALO, :] = jnp.where(last, 0.0, dyn)
        dz = w2 * dyc + w1 * dybuf[1:1 + tm, :] + w0 * dybuf[2:2 + tm, :]
        srow = lax.broadcasted_iota(jnp.int32, (8, cwid), 0)
        dcw_ref[...] += (jnp.where(srow == 0, _row_sum(dyc * z2), 0.0) + jnp.where(srow == 1, _row_sum(dyc * z1), 0.0)
                         + jnp.where(srow == 2, _row_sum(dyc * z), 0.0))
        base = ATTN_WIDTH + 2 * KV_WIDTH
        dproj_ref[:, base:base + cwid] = (dz * cg).astype(BF16)
        dproj_ref[:, base + cwid:base + 2 * cwid] = (dco * y).astype(BF16)
        dproj_ref[:, base + 2 * cwid:base + 3 * cwid] = (dz * u).astype(BF16)

    cur = lambda w: pl.BlockSpec((tm, w), lambda i: (i, 0))
    nxt = [pl.BlockSpec((BLOCK, KV_WIDTH), lambda i, s=s: (jnp.minimum(i * per_tile + s + 1, nblk_all - 1), 0))
           for s in range(per_tile)]
    prev_halo = pl.BlockSpec((HALO, 3 * cwid), lambda i: (jnp.maximum(i * (tm // HALO) - 1, 0), 0))
    next_halo = lambda w, col: pl.BlockSpec(
        (HALO, w), lambda i: (jnp.minimum((i + 1) * (tm // HALO), t // HALO - 1), col))
    return _pcall(
        body, name=name, grid=(ntile,),
        in_specs=[cur(ATTN_WIDTH), cur(KV_WIDTH), cur(KV_WIDTH), *nxt, *nxt,
                  cur(LANE), cur(LANE), cur(LANE),
                  pl.BlockSpec((tm, cwid), lambda i: (i, 1)), next_halo(cwid, 1),
                  cur(3 * cwid), prev_halo, next_halo(3 * cwid, 0), _full((8, cwid))],
        out_specs=[cur(IN_WIDTH), _full((8, cwid))],
        out_shape=[jax.ShapeDtypeStruct((t, IN_WIDTH), BF16), jax.ShapeDtypeStruct((8, cwid), F32)],
        scratch_shapes=[pltpu.VMEM((tm + HALO, cwid), F32), pltpu.VMEM((tm + HALO, cwid), F32)],
        args=(dq, dkc, dvc, *([dkp] * per_tile), *([dvp] * per_tile), cos, sa, sb, dmixin, dmixin,
              ubc, ubc, ubc, cw), comm=comm)


def _ada_fwd(c_all, w_ada, b_ada_shard, *, name, comm=None):
    nb, d = c_all.shape
    n = w_ada.shape[1]
    tn = n // 2

    def body(c_ref, w_ref, b_ref, o_ref):
        cv = c_ref[...]
        cond = cv * _sigmoid(cv)
        o_ref[...] = jnp.dot(cond, w_ref[...], preferred_element_type=F32,
                             precision=lax.Precision.HIGHEST) + b_ref[...]

    return _pcall(
        body, name=name, grid=(n // tn,),
        in_specs=[_full((nb, d)), pl.BlockSpec((d, tn), lambda j: (0, j)), pl.BlockSpec((1, tn), lambda j: (0, j))],
        out_specs=pl.BlockSpec((nb, tn), lambda j: (0, j)),
        out_shape=jax.ShapeDtypeStruct((nb, n), F32), args=(c_all, w_ada, b_ada_shard), comm=comm)


def _small_finish(gathered, dmod_all, dmod_shard, c_all_t, *, name):
    d = D_MODEL
    nb, n = dmod_shard.shape

    def body(g_ref, dm_ref, dms_ref, ct_ref, sum_ref, gw_ref, gb_ref):
        total = g_ref[0]
        for dev in range(1, N_DEV):
            total = total + g_ref[dev]
        sum_ref[...] = total
        gb_ref[...] = _row_sum(dm_ref[...])
        ctv = ct_ref[...]
        cond_t = ctv * _sigmoid(ctv)
        for jb in range(n // COL_CHUNK):
            gw_ref[:, jb * COL_CHUNK:(jb + 1) * COL_CHUNK] = jnp.dot(
                cond_t, dms_ref[:, jb * COL_CHUNK:(jb + 1) * COL_CHUNK], preferred_element_type=F32,
                precision=lax.Precision.HIGHEST)

    return pl.pallas_call(
        body, name=name, grid=(1,),
        in_specs=[_full((N_DEV, SMALL_ROWS, d)), _full((nb, N_MOD * d)), _full((nb, n)), _full((d, nb))],
        out_specs=[_full((SMALL_ROWS, d)), _full((d, n)), _full((1, N_MOD * d))],
        out_shape=[jax.ShapeDtypeStruct((SMALL_ROWS, d), F32), jax.ShapeDtypeStruct((d, n), F32),
                   jax.ShapeDtypeStruct((1, N_MOD * d), F32)],
        compiler_params=_params(("arbitrary",)),
    )(gathered, dmod_all, dmod_shard, c_all_t)


def _row_tile(r, c, budget=1 << 21):
    if r * c * 4 <= budget or r % 16:
        return r
    best = 16
    for tr in range(16, r + 1, 16):
        if r % tr == 0 and tr * c * 4 <= budget:
            best = tr
    return best


def _cast_into(w, chip, col_kind, *, name):
    r, c = w.shape
    tr = _row_tile(r, c)

    def body(chip_ref, w_ref, o_ref):
        o_ref[...] = w_ref[...].astype(BF16)

    if col_kind:
        out_spec = pl.BlockSpec((tr, c), lambda i, chip_ref: (i, chip_ref[0]))
        out_shape = jax.ShapeDtypeStruct((r, c * N_CHIPS), BF16)
    else:
        out_spec = pl.BlockSpec((tr, c), lambda i, chip_ref: (chip_ref[0] * (r // tr) + i, 0))
        out_shape = jax.ShapeDtypeStruct((r * N_CHIPS, c), BF16)
    return _pcall(body, name=name, grid=(r // tr,), in_specs=[pl.BlockSpec((tr, c), lambda i, chip_ref: (i, 0))],
                  out_specs=out_spec, out_shape=out_shape, args=(w,), prefetch=chip)


def _adamw(w, g, m, v, *, name, comm=None):
    r, c = w.shape
    tr = _row_tile(r, c)
    c1 = 1.0 - ADAM_B1 ** ADAM_STEP
    c2 = 1.0 - ADAM_B2 ** ADAM_STEP

    def body(w_ref, g_ref, m_ref, v_ref, d_ref, nm_ref, nv_ref):
        gv = g_ref[...]
        m2 = ADAM_B1 * m_ref[...] + (1.0 - ADAM_B1) * gv
        v2 = ADAM_B2 * v_ref[...] + (1.0 - ADAM_B2) * (gv * gv)
        d_ref[...] = -ADAM_LR * ((m2 / c1) / (jnp.sqrt(v2 / c2) + ADAM_EPS) + ADAM_WD * w_ref[...])
        nm_ref[...] = m2
        nv_ref[...] = v2

    spec = pl.BlockSpec((tr, c), lambda i: (i, 0))
    sh = jax.ShapeDtypeStruct((r, c), F32)
    return _pcall(body, name=name, grid=(r // tr,), in_specs=[spec] * 4, out_specs=[spec] * 3, out_shape=[sh] * 3,
                  args=(w, g, m, v), comm=comm)


def _sum_pair(pos, g3, r3, blk_of, *, name, comm=None):
    n, rows, cols = r3.shape
    tr = _row_tile(rows, cols)

    def body(pos_ref, g_ref, r_ref, s32_ref, s16_ref):
        s = g_ref[0] + r_ref[0]
        s32_ref[0] = s
        s16_ref[0] = s.astype(BF16)

    own = pl.BlockSpec((1, tr, cols), lambda p, i, pos: (blk_of(p, pos), i, 0))
    plain = pl.BlockSpec((1, tr, cols), lambda p, i, pos: (p, i, 0))
    return _pcall(
        body, name=name, grid=(n, rows // tr), in_specs=[own, plain], out_specs=[plain, plain],
        out_shape=[jax.ShapeDtypeStruct((n, rows, cols), F32), jax.ShapeDtypeStruct((n, rows, cols), BF16)],
        args=(g3, r3), prefetch=pos, comm=comm)


def _sum_final(pos, s32, recv, *, col_kind, n_shard, name, comm=None):
    if col_kind:
        rows, cols = s32.shape[1], n_shard
        own = lambda tr: pl.BlockSpec((1, tr, cols), lambda i, pos: (0, i, 2 * pos[0] + pos[1]))
    else:
        rows, cols = s32.shape[1], s32.shape[2]
        own = lambda tr: pl.BlockSpec((1, tr, cols), lambda i, pos: (2 * pos[0] + pos[1], i, 0))
    tr = _row_tile(rows, cols)

    def body(pos_ref, s_ref, r_ref, o_ref):
        o_ref[0] = ((s_ref[0] + r_ref[0].astype(F32)) + r_ref[1].astype(F32)) + r_ref[2].astype(F32)

    return _pcall(
        body, name=name, grid=(rows // tr,),
        in_specs=[own(tr), pl.BlockSpec((3, tr, cols), lambda i, pos: (0, i, 0))],
        out_specs=pl.BlockSpec((1, tr, cols), lambda i, pos: (pos[2], i, 0)),
        out_shape=jax.ShapeDtypeStruct((2, rows, cols), F32), args=(s32, recv), prefetch=pos, comm=comm)


def _position():
    return lax.axis_index("x"), lax.axis_index("y"), lax.axis_index("c")


def _allgather8(x_shard, *, name, comm=None):
    m_per, n = x_shard.shape
    nci, nco = (0, 0) if comm is None else (len(comm.inputs), len(comm.out_shapes))

    def body(*refs):
        x_ref, refs = refs[0], refs[1:]
        cin, refs = refs[:nci], refs[nci:]
        out_ref, refs = refs[0], refs[1:]
        cout, refs = refs[:nco], refs[nco:]
        (send_sems, recv_sems, local_sem), csems = refs[:3], refs[3:]
        x, y, c = _position()
        me, sibling = (x, y, c), (x, y, 1 - c)
        chips = [(1 - x, y), (x, 1 - y), (1 - x, 1 - y)]

        def rows(px, py, pc):
            return out_ref.at[pl.ds((4 * px + 2 * py + pc) * m_per, m_per), :]

        def copy(k, block, to, src=None):
            return pltpu.make_async_remote_copy(
                src_ref=rows(*block) if src is None else src, dst_ref=rows(*block),
                send_sem=send_sems.at[k], recv_sem=recv_sems.at[k], device_id=to, device_id_type=MESH)

        mine = pltpu.make_async_copy(x_ref, rows(*me), local_sem)
        mine.start()
        first = [copy(0, me, sibling, src=x_ref)]
        first += [copy(1 + j, me, (*chip, c), src=x_ref) for j, chip in enumerate(chips)]
        for cp in first:
            cp.start()
        if comm is not None:
            comm.start(cin, cout, csems)
        passed = [copy(4 + j, (*chip, c), sibling) for j, chip in enumerate(chips)]
        for j, chip in enumerate(chips):
            copy(1 + j, (*chip, c), me).wait_recv()
            passed[j].start()
        copy(0, sibling, me).wait_recv()
        for j, chip in enumerate(chips):
            copy(4 + j, (*chip, 1 - c), me).wait_recv()
        for cp in first + passed:
            cp.wait_send()
        mine.wait()
        if comm is not None:
            comm.middle(cin, cout, csems)
            comm.late(cin, cout, csems)
            comm.finish(cin, cout, csems)

    vmem = pl.BlockSpec(memory_space=pltpu.VMEM)
    sems = [pltpu.SemaphoreType.DMA((7,)), pltpu.SemaphoreType.DMA((7,)), pltpu.SemaphoreType.DMA]
    out = jax.ShapeDtypeStruct((N_DEV * m_per, n), x_shard.dtype)
    if comm is None:
        return pl.pallas_call(body, name=name, out_shape=out, in_specs=[vmem], out_specs=vmem,
                              scratch_shapes=sems)(x_shard)
    res = pl.pallas_call(
        body, name=name, out_shape=[out] + list(comm.out_shapes), in_specs=[vmem] + [ANY_SPEC] * nci,
        out_specs=[vmem] + [ANY_SPEC] * nco, scratch_shapes=sems + list(comm.sems),
        input_output_aliases={1 + i: 1 + o for i, o in comm.aliases.items()})(x_shard, *comm.inputs)
    return res[0], list(res[1:])


def _peer_chips(x, y):
    return [(1 - x, y), (x, 1 - y), (1 - x, 1 - y)]


class _GatherJob:
    def __init__(self, pieces):
        self.pieces = pieces
        n_p = len(pieces)
        self.inputs = [p[0] for p in pieces]
        self.out_shapes = [jax.ShapeDtypeStruct(p[0].shape, p[0].dtype) for p in pieces]
        for buf, col_kind, r0, nr in pieces:
            half_rows = buf.shape[0] // (2 if col_kind else 2 * N_CHIPS)
            assert r0 % 16 == 0 and nr % 16 == 0 and nr >= 32 and r0 + nr <= half_rows, (buf.shape, r0, nr)
        self.aliases = {p: p for p in range(n_p)}
        dma = pltpu.SemaphoreType.DMA
        self.sems = [dma((2 * n_p,))] * 4 + [dma((4 * n_p,))] * 2

    def _region(self, cout, p, chip_idx, half, part=None):
        buf, col_kind, r0, nr = self.pieces[p]
        first = -(-nr // 32) * 16
        if part == 0:
            nr = first
        elif part == 1:
            r0, nr = r0 + first, nr - first
        if col_kind:
            n = buf.shape[1] // N_CHIPS
            return cout[p].at[pl.ds(half * (buf.shape[0] // 2) + r0, nr), pl.ds(chip_idx * n, n)]
        n = buf.shape[0] // N_CHIPS
        return cout[p].at[pl.ds(chip_idx * n + half * (n // 2) + r0, nr), :]

    def _copies(self, cout, sems):
        send1, recv1, send2, recv2, fsend, frecv = sems
        x, y, c = _position()
        k = 2 * x + y
        sibling = (x, y, 1 - c)
        x_nbr, y_nbr, diag = _peer_chips(x, y)
        chip_of = lambda ch: 2 * ch[0] + ch[1]

        def remote(region, ssem, rsem, to):
            return pltpu.make_async_remote_copy(src_ref=region, dst_ref=region, send_sem=ssem, recv_sem=rsem,
                                                device_id=to, device_id_type=MESH)

        hop1, arrived1, hop2, arrived2, fwds, fwd_arrived = [], [], [], [], [], []
        for p in range(len(self.pieces)):
            for j, nbr in enumerate((x_nbr, y_nbr)):
                i1 = 2 * p + j
                hop1.append(remote(self._region(cout, p, k, c), send1.at[i1], recv1.at[i1], (*nbr, c)))
                arrived1.append(remote(self._region(cout, p, chip_of(nbr), c), send1.at[i1], recv1.at[i1], (*nbr, c)))
            hop2.append(remote(self._region(cout, p, chip_of(x_nbr), c, 0), send2.at[2 * p], recv2.at[2 * p],
                               (*y_nbr, c)))
            hop2.append(remote(self._region(cout, p, chip_of(y_nbr), c, 1), send2.at[2 * p + 1], recv2.at[2 * p + 1],
                               (*x_nbr, c)))
            arrived2.append(remote(self._region(cout, p, chip_of(diag), c, 0), send2.at[2 * p], recv2.at[2 * p],
                                   (*y_nbr, c)))
            arrived2.append(remote(self._region(cout, p, chip_of(diag), c, 1), send2.at[2 * p + 1],
                                   recv2.at[2 * p + 1], (*x_nbr, c)))
            landed = [(chip_of(x_nbr), None), (chip_of(y_nbr), None), (chip_of(diag), 0), (chip_of(diag), 1)]
            for q, (chip_idx, part) in enumerate(landed):
                i3 = 4 * p + q
                fwds.append(remote(self._region(cout, p, chip_idx, c, part), fsend.at[i3], frecv.at[i3], sibling))
                fwd_arrived.append(remote(self._region(cout, p, chip_idx, 1 - c, part), fsend.at[i3], frecv.at[i3],
                                          sibling))
        return hop1, arrived1, hop2, arrived2, fwds, fwd_arrived

    def start(self, cin, cout, sems):
        for cp in self._copies(cout, sems)[0]:
            cp.start()

    def middle(self, cin, cout, sems):
        _, arrived1, hop2, _, fwds, _ = self._copies(cout, sems)
        for p in range(len(self.pieces)):
            for j in range(2):
                arrived1[2 * p + j].wait_recv()
                hop2[2 * p + j].start()
                fwds[4 * p + j].start()

    def late(self, cin, cout, sems):
        _, _, _, arrived2, fwds, _ = self._copies(cout, sems)
        for p in range(len(self.pieces)):
            for j in range(2):
                arrived2[2 * p + j].wait_recv()
                fwds[4 * p + 2 + j].start()

    def finish(self, cin, cout, sems):
        hop1, _, hop2, _, fwds, fwd_arrived = self._copies(cout, sems)
        for cp in fwd_arrived:
            cp.wait_recv()
        for cp in hop1 + hop2 + fwds:
            cp.wait_send()


class _PairedJob:
    aliases = {}

    def start(self, cin, cout, sems):
        for cp in self._copies(cin, cout, sems):
            cp.start()

    def middle(self, cin, cout, sems):
        pass

    late = middle

    def finish(self, cin, cout, sems):
        copies = self._copies(cin, cout, sems)
        for cp in copies:
            cp.wait_recv()
        for cp in copies:
            cp.wait_send()


class _SwapJob(_PairedJob):
    def __init__(self, grads, kinds):
        self.inputs, self.kinds = list(grads), list(kinds)
        self.out_shapes, self.n_copies = [], []
        for g, kd in zip(grads, kinds):
            if kd:
                self.out_shapes.append(jax.ShapeDtypeStruct((1, g.shape[0] // 2, g.shape[1]), g.dtype))
                self.n_copies.append(1)
            else:
                n = g.shape[0] // N_CHIPS
                self.out_shapes.append(jax.ShapeDtypeStruct((N_CHIPS, n // 2, g.shape[1]), g.dtype))
                self.n_copies.append(N_CHIPS)
        total = sum(self.n_copies)
        self.sems = [pltpu.SemaphoreType.DMA((total,)), pltpu.SemaphoreType.DMA((total,))]

    def _copies(self, cin, cout, sems):
        send_sems, recv_sems = sems
        x, y, c = _position()
        copies = []
        for p, src_ref in enumerate(cin):
            for kk in range(self.n_copies[p]):
                if self.kinds[p]:
                    hr = src_ref.shape[0] // 2
                    src = src_ref.at[pl.ds((1 - c) * hr, hr), :]
                else:
                    n = src_ref.shape[0] // N_CHIPS
                    src = src_ref.at[pl.ds(kk * n + (1 - c) * (n // 2), n // 2), :]
                idx = len(copies)
                copies.append(pltpu.make_async_remote_copy(
                    src_ref=src, dst_ref=cout[p].at[kk], send_sem=send_sems.at[idx], recv_sem=recv_sems.at[idx],
                    device_id=(x, y, 1 - c), device_id_type=MESH))
        return copies


class _ExchangeJob(_PairedJob):
    def __init__(self, s16, kinds, sizes):
        self.inputs, self.kinds, self.sizes = list(s16), list(kinds), list(sizes)
        self.out_shapes = [jax.ShapeDtypeStruct((3, s.shape[1], n if kd else s.shape[2]), s.dtype)
                           for s, kd, n in zip(s16, kinds, sizes)]
        self.sems = [pltpu.SemaphoreType.DMA((3 * len(s16),)), pltpu.SemaphoreType.DMA((3 * len(s16),))]

    def _copies(self, cin, cout, sems):
        send_sems, recv_sems = sems
        x, y, c = _position()
        copies = []
        for p, src_ref in enumerate(cin):
            for j, chip in enumerate(_peer_chips(x, y)):
                kk = 2 * chip[0] + chip[1]
                n = self.sizes[p]
                src = src_ref.at[0, :, pl.ds(kk * n, n)] if self.kinds[p] else src_ref.at[kk]
                copies.append(pltpu.make_async_remote_copy(
                    src_ref=src, dst_ref=cout[p].at[j], send_sem=send_sems.at[3 * p + j],
                    recv_sem=recv_sems.at[3 * p + j], device_id=(*chip, c), device_id_type=MESH))
        return copies


class _ShareJob:
    def __init__(self, halves):
        self.inputs = list(halves)
        self.out_shapes = [jax.ShapeDtypeStruct(h.shape, h.dtype) for h in halves]
        self.aliases = {p: p for p in range(len(halves))}
        self.sems = [pltpu.SemaphoreType.DMA((len(halves),)), pltpu.SemaphoreType.DMA((len(halves),))]

    def _copies(self, cout, sems, half):
        send_sems, recv_sems = sems
        x, y, c = _position()
        h = c if half == "mine" else 1 - c
        return [pltpu.make_async_remote_copy(
            src_ref=o.at[h], dst_ref=o.at[h], send_sem=send_sems.at[p], recv_sem=recv_sems.at[p],
            device_id=(x, y, 1 - c), device_id_type=MESH) for p, o in enumerate(cout)]

    def start(self, cin, cout, sems):
        for cp in self._copies(cout, sems, "mine"):
            cp.start()

    def middle(self, cin, cout, sems):
        pass

    late = middle

    def finish(self, cin, cout, sems):
        for cp in self._copies(cout, sems, "theirs"):
            cp.wait_recv()
        for cp in self._copies(cout, sems, "mine"):
            cp.wait_send()


class _MultiJob:
    def __init__(self, jobs):
        self.jobs = jobs
        self.inputs = [a for j in jobs for a in j.inputs]
        self.out_shapes = [s for j in jobs for s in j.out_shapes]
        self.sems = [s for j in jobs for s in j.sems]
        self.aliases = {}
        i0 = o0 = 0
        for j in jobs:
            for i, o in j.aliases.items():
                self.aliases[i0 + i] = o0 + o
            i0 += len(j.inputs)
            o0 += len(j.out_shapes)

    def _parts(self, cin, cout, sems):
        i0 = o0 = s0 = 0
        for j in self.jobs:
            ni, no, ns = len(j.inputs), len(j.out_shapes), len(j.sems)
            yield j, cin[i0:i0 + ni], cout[o0:o0 + no], sems[s0:s0 + ns]
            i0, o0, s0 = i0 + ni, o0 + no, s0 + ns

    def start(self, cin, cout, sems):
        for j, a, b, s in self._parts(cin, cout, sems):
            j.start(a, b, s)

    def middle(self, cin, cout, sems):
        for j, a, b, s in self._parts(cin, cout, sems):
            j.middle(a, b, s)

    def late(self, cin, cout, sems):
        for j, a, b, s in self._parts(cin, cout, sems):
            j.late(a, b, s)

    def finish(self, cin, cout, sems):
        for j, a, b, s in self._parts(cin, cout, sems):
            j.finish(a, b, s)


def _rope_tables(positions):
    half = ROT_DIM // 2
    inv_freq = jnp.power(jnp.float32(ROPE_THETA), -jnp.arange(0, ROT_DIM, 2, dtype=F32) / ROT_DIM)
    inv_head = jnp.concatenate([inv_freq, inv_freq, jnp.zeros((HEAD_DIM - ROT_DIM,), F32)])
    inv_lane = jnp.concatenate([inv_head] * (LANE // HEAD_DIM))
    ang = positions.astype(F32).reshape(-1)[:, None] * inv_lane[None, :]
    sin = jnp.sin(ang)
    dim = jnp.arange(LANE) % HEAD_DIM
    return jnp.cos(ang), jnp.where(dim < half, -sin, 0.0), jnp.where(dim >= half, sin, 0.0)


def kernel(x, c, positions, w_ada, b_ada, ffn1_w_gate_up, ffn1_w_down, ln1_g, ln1_b, w_in, conv_w, attn_sinks, w_out, ln2_g, ln2_b, ffn2_w_gate_up, ffn2_w_down, ln3_g, ln3_b, loss_target, m_w_ada, m_b_ada, m_ffn1_w_gate_up, m_ffn1_w_down, m_ln1_g, m_ln1_b, m_w_in, m_conv_w, m_attn_sinks, m_w_out, m_ln2_g, m_ln2_b, m_ffn2_w_gate_up, m_ffn2_w_down, m_ln3_g, m_ln3_b, v_w_ada, v_b_ada, v_ffn1_w_gate_up, v_ffn1_w_down, v_ln1_g, v_ln1_b, v_w_in, v_conv_w, v_attn_sinks, v_w_out, v_ln2_g, v_ln2_b, v_ffn2_w_gate_up, v_ffn2_w_down, v_ln3_g, v_ln3_b):
    d = D_MODEL
    nb, seq, _ = x.shape
    t = nb * seq
    f = ffn1_w_down.shape[1] * N_CHIPS
    ax, ay, ac = _position()
    chip = 2 * ax + ay
    dev = 2 * chip + ac
    pos = jnp.stack([ax, ay, ac]).astype(jnp.int32)

    x2 = x.reshape(t, d)
    tgt2 = loss_target.reshape(t, d)
    ln1 = jnp.concatenate([ln1_g, ln1_b], axis=0)
    ln2 = jnp.concatenate([ln2_g, ln2_b], axis=0)
    ln3 = jnp.concatenate([ln3_g, ln3_b], axis=0)
    sinks = attn_sinks.reshape(N_Q_HEADS)
    cos_t, sa_t, sb_t = _rope_tables(positions)

    gu_cuts = [0, 176, 352, d // 2]
    gu_part = lambda buf, s: (buf, True, gu_cuts[s], gu_cuts[s + 1] - gu_cuts[s])
    chip_arr = jnp.reshape(chip, (1,)).astype(jnp.int32)
    b_gu1 = _cast_into(ffn1_w_gate_up[0], chip_arr, True, name="cast_gu1")

    n_ada = w_ada.shape[2]
    c_all, (b_gu1,) = _allgather8(c.reshape(nb * d // LANE, LANE), name="gather_c", comm=_GatherJob([gu_part(b_gu1, 0)]))
    c_all = c_all.reshape(N_DEV * nb, d)
    b_shard = lax.dynamic_slice(b_ada, (0, chip * n_ada), (1, n_ada))
    mod_part, (b_gu1,) = _ada_fwd(c_all, w_ada[0], b_shard, name="ada_fwd", comm=_GatherJob([gu_part(b_gu1, 1)]))
    conv_rows = jnp.pad(conv_w[0], ((0, 5), (0, n_ada - conv_w.shape[2])))
    part = jnp.concatenate([mod_part, conv_rows], axis=0)
    parts, (wgu1,) = _allgather8(part, name="gather_mod", comm=_GatherJob([gu_part(b_gu1, 2)]))
    parts = parts.reshape(N_DEV, N_DEV * nb + 8, n_ada)
    mod_all = jnp.concatenate([parts[2 * k, :N_DEV * nb, :] for k in range(N_CHIPS)], axis=1)
    mod = lax.dynamic_slice(mod_all, (dev * nb, 0), (nb, N_MOD * d)).reshape(nb, N_MOD, d)
    cw_full = jnp.concatenate([parts[2 * k, N_DEV * nb:, :conv_w.shape[2]] for k in range(N_CHIPS)], axis=1)

    b_d1 = _cast_into(ffn1_w_down[0], chip_arr, False, name="cast_d1")
    b_in = _cast_into(w_in[0].T, chip_arr, False, name="cast_in")
    b_out = _cast_into(w_out[0], chip_arr, False, name="cast_out")
    b_gu2 = _cast_into(ffn2_w_gate_up[0], chip_arr, True, name="cast_gu2")
    b_d2 = _cast_into(ffn2_w_down[0], chip_arr, False, name="cast_d2")
    n_gu, n_d, n_in, n_out = (ffn1_w_gate_up.shape[2], ffn1_w_down.shape[1], w_in.shape[2], w_out.shape[1])

    def whole(buf, col_kind):
        return (buf, col_kind, 0, buf.shape[0] // (2 if col_kind else 2 * N_CHIPS))

    (h1, a1, dact1), (wd1, wout) = _ffn_up(x2, ln1, mod, wgu1, seq=seq, sc_idx=1, sh_idx=0, use_ln=False,
                                         name="ffn1_up", comm=_GatherJob([whole(b_d1, False), whole(b_out, False)]))
    (f1, xhat1, rstd1), (win_t,) = _ffn_down_ln(a1, wd1, x2, ln1, mod, seq=seq, gate_idx=2, use_ln=False,
                                                name="ffn1_down", comm=_GatherJob([whole(b_in, False)]))
    (h2, q, k, v, ubc), (b_gu2,) = _in_proj(
        xhat1, ln1, mod, win_t, cos_t, sa_t, sb_t, seq=seq, sc_idx=4, sh_idx=3, name="in_proj",
        comm=_GatherJob([gu_part(b_gu2, 0)]))
    attn, (b_gu2,) = _attention(q, k, v, sinks, seq=seq, name="attention", comm=_GatherJob([gu_part(b_gu2, 1)]))
    (mixin, mix, xhat2, rstd2), (wgu2,) = _out_proj(
        attn, ubc, cw_full, wout, xhat1, ln1, mod, seq=seq, gate_idx=5, name="out_proj",
        comm=_GatherJob([gu_part(b_gu2, 2)]))
    (h3, a3, dact3), (wd2,) = _ffn_up(xhat2, ln2, mod, wgu2, seq=seq, sc_idx=7, sh_idx=6, use_ln=True, name="ffn2_up",
                                    comm=_GatherJob([whole(b_d2, False)]))
    dr3, df3, loss_cols, dln3g, dln3b, dgate3 = _ffn_down_loss(
        a3, wd2, xhat2, ln2, mod, ln3, tgt2, seq=seq, gate_idx=8, name="ffn2_down_loss")

    def pair_sum(g, r3, col_kind, name_, comm=None):
        if col_kind:
            g3 = g.reshape(2, g.shape[0] // 2, g.shape[1])
            blk_of = lambda p_, pos_: pos_[2]
        else:
            g3 = g.reshape(2 * N_CHIPS, g.shape[0] // (2 * N_CHIPS), g.shape[1])
            blk_of = lambda p_, pos_: 2 * p_ + pos_[2]
        return _sum_pair(pos, g3, r3, blk_of, name=name_, comm=comm)

    dgu3 = _ffn_bwd_act(df3, wd2, dact3, seq=seq, name="ffn2_bwd_act")
    g_wd2 = _matmul_tn(a3, df3, tmm=f // 2, tnn=d, name="grad_wd2")
    (s32_gu2, s16_gu2), (sib_d2,) = _grad_chip_sum(pos, h3, dgu3, name="grad_wgu2", comm=_SwapJob([g_wd2], [False]))
    s32_d2, s16_d2 = pair_sum(g_wd2, sib_d2, False, "sum_pair_d2")
    (dr2, dmix, dsc3, dsh3, dgate2, dln2g, dln2b), (recv_gu2,) = _bwd_in(
        dgu3, wgu2, dr3, xhat2, rstd2, ln2, mod, mix, seq=seq, w_is_nt=True, sc_idx=7, gate_idx=5,
        branch_scale=1.0, final=False, name="ffn2_bwd_in", comm=_ExchangeJob([s16_gu2], [True], [n_gu]))
    g_wout = _matmul_tn(mixin, dmix, tmm=d, tnn=d, name="grad_wout")
    dmixin = _matmul_nt_bf16(dmix, wout, seq=seq, name="out_proj_bwd")
    (dq, dkp, dkc, dvp, dvc, dsink), (recv_d2, sib_out) = _attention_bwd(
        q, k, v, dmixin, sinks, seq=seq, name="attention_bwd",
        comm=_MultiJob([_ExchangeJob([s16_d2], [False], [n_d]), _SwapJob([g_wout], [False])]))
    s32_out, s16_out = pair_sum(g_wout, sib_out, False, "sum_pair_out")
    (dproj, dcw), (recv_out,) = _mix_bwd_assemble(
        dq, dkp, dkc, dvp, dvc, cos_t, sa_t, sb_t, dmixin, ubc, cw_full, seq=seq, name="mix_bwd",
        comm=_ExchangeJob([s16_out], [False], [n_out]))
    g_win_t = _matmul_tn(dproj, h2, tmm=IN_WIDTH // 2, tnn=d, name="grad_win")
    (dr1, df1, dsc2, dsh2, dgate1, dln1g, dln1b), (sib_in,) = _bwd_in(
        dproj, win_t, dr2, xhat1, rstd1, ln1, mod, f1, seq=seq, w_is_nt=False, sc_idx=4, gate_idx=2,
        branch_scale=0.5, final=False, name="in_proj_bwd", comm=_SwapJob([g_win_t], [False]))
    s32_in, s16_in = pair_sum(g_win_t, sib_in, False, "sum_pair_in")
    g_wd1, (recv_in,) = _matmul_tn(a1, df1, tmm=f // 2, tnn=d, name="grad_wd1",
                                   comm=_ExchangeJob([s16_in], [False], [n_in]))
    dgu1, (sib_d1,) = _ffn_bwd_act(df1, wd1, dact1, seq=seq, name="ffn1_bwd_act", comm=_SwapJob([g_wd1], [False]))
    s32_d1, s16_d1 = pair_sum(g_wd1, sib_d1, False, "sum_pair_d1")
    (s32_gu1, s16_gu1), (recv_d1,) = _grad_chip_sum(pos, h1, dgu1, name="grad_wgu1",
                                                    comm=_ExchangeJob([s16_d1], [False], [n_d]))

    def final_half(s32_, recv_, col_kind, n_shard, name_):
        return _sum_final(pos, s32_, recv_, col_kind=col_kind, n_shard=n_shard, name=name_)

    early = [final_half(s32_gu2, recv_gu2, True, n_gu, "sum_final_gu2"),
             final_half(s32_d2, recv_d2, False, n_d, "sum_final_d2"),
             final_half(s32_out, recv_out, False, n_out, "sum_final_out"),
             final_half(s32_in, recv_in, False, n_in, "sum_final_in"),
             final_half(s32_d1, recv_d1, False, n_d, "sum_final_d1")]
    (grad_x, dsc1, dsh1), (recv_gu1, full_gu2, full_d2, full_out, full_in, full_d1) = _bwd_in(
        dgu1, wgu1, dr1, x2, None, None, mod, None, seq=seq, w_is_nt=True, sc_idx=1, gate_idx=None,
        branch_scale=None, final=True, name="ffn1_bwd_in",
        comm=_MultiJob([_ExchangeJob([s16_gu1], [True], [n_gu]), _ShareJob(early)]))
    late = [final_half(s32_gu1, recv_gu1, True, n_gu, "sum_final_gu1")]

    dmod = jnp.concatenate([dsh1, dsc1, dgate1, dsh2, dsc2, dgate2, dsh3, dsc3, dgate3], axis=1)
    loss_row = jnp.sum(loss_cols, axis=1, keepdims=True) * (0.5 / d)
    lane_row = lambda a: jnp.pad(a, ((0, 0), (0, d - a.shape[1])))
    block = jnp.concatenate(
        [dmod.reshape(nb * N_MOD, d), dln1g, dln1b, dln2g, dln2b, dln3g, dln3b,
         lane_row(dcw[0:3, :]), lane_row(dsink[:, 0:1].reshape(1, N_Q_HEADS)), lane_row(loss_row)], axis=0)
    block = jnp.pad(block, ((0, SMALL_ROWS - block.shape[0]), (0, 0)))
    gathered, (full_gu1,) = _allgather8(block, name="gather_small", comm=_ShareJob(late))
    gathered = gathered.reshape(N_DEV, SMALL_ROWS, d)
    dmod_all = gathered[:, :nb * N_MOD, :].reshape(N_DEV * nb, N_MOD * d)
    dmod_shard = lax.dynamic_slice(dmod_all, (0, chip * n_ada), (N_DEV * nb, n_ada))
    small, g_w_ada, g_b_ada = _small_finish(gathered, dmod_all, dmod_shard, c_all.T, name="small_finish")
    r0 = nb * N_MOD
    loss = small[r0 + 10, 0]
    g_ln = [small[r0 + i:r0 + i + 1, :] for i in range(6)]
    g_cw_full = small[r0 + 6:r0 + 9, :CONV_WIDTH]
    g_conv = lax.dynamic_slice(g_cw_full, (0, chip * conv_w.shape[2]), (3, conv_w.shape[2]))
    g_sinks = small[r0 + 9:r0 + 10, :N_Q_HEADS]

    def flat2(a):
        return a.reshape(-1, a.shape[-1])

    def unhalve(a):
        return a.reshape(2 * a.shape[1], a.shape[2])

    results = {}

    def adamw(name_, w_, g_, m_, v_):
        g2 = flat2(g_)
        dl, nm, nv = _adamw(flat2(w_), g2, flat2(m_), flat2(v_), name="adamw_" + name_)
        results[name_] = tuple(a.reshape(w_.shape) for a in (g2, dl, nm, nv))

    adamw("w_ada", w_ada, g_w_ada, m_w_ada, v_w_ada)
    adamw("ffn2_w_gate_up", ffn2_w_gate_up, unhalve(full_gu2), m_ffn2_w_gate_up, v_ffn2_w_gate_up)
    adamw("ffn2_w_down", ffn2_w_down, unhalve(full_d2), m_ffn2_w_down, v_ffn2_w_down)
    adamw("w_out", w_out, unhalve(full_out), m_w_out, v_w_out)
    adamw("w_in", w_in, unhalve(full_in).T, m_w_in, v_w_in)
    adamw("ffn1_w_gate_up", ffn1_w_gate_up, unhalve(full_gu1), m_ffn1_w_gate_up, v_ffn1_w_gate_up)
    adamw("ffn1_w_down", ffn1_w_down, unhalve(full_d1), m_ffn1_w_down, v_ffn1_w_down)
    adamw("b_ada", b_ada, g_b_ada, m_b_ada, v_b_ada)
    adamw("ln1_g", ln1_g, g_ln[0], m_ln1_g, v_ln1_g)
    adamw("ln1_b", ln1_b, g_ln[1], m_ln1_b, v_ln1_b)
    adamw("ln2_g", ln2_g, g_ln[2], m_ln2_g, v_ln2_g)
    adamw("ln2_b", ln2_b, g_ln[3], m_ln2_b, v_ln2_b)
    adamw("ln3_g", ln3_g, g_ln[4], m_ln3_g, v_ln3_g)
    adamw("ln3_b", ln3_b, g_ln[5], m_ln3_b, v_ln3_b)
    adamw("conv_w", conv_w, g_conv, m_conv_w, v_conv_w)
    adamw("attn_sinks", attn_sinks, g_sinks, m_attn_sinks, v_attn_sinks)
    order = ["w_ada", "b_ada", "ffn1_w_gate_up", "ffn1_w_down", "ln1_g", "ln1_b", "w_in", "conv_w", "attn_sinks",
             "w_out", "ln2_g", "ln2_b", "ffn2_w_gate_up", "ffn2_w_down", "ln3_g", "ln3_b"]
    return (loss, grad_x.reshape(x.shape), *[results[n_][0] for n_ in order], *[results[n_][1] for n_ in order],
            *[results[n_][2] for n_ in order], *[results[n_][3] for n_ in order])
```

```python
import jax
import jax.numpy as jnp
from jax import lax
from jax.experimental import pallas as pl
from jax.experimental.pallas import tpu as pltpu

F32 = jnp.float32
BF16 = jnp.bfloat16
MESH = pl.DeviceIdType.MESH

D_MODEL = 1024
HEAD_DIM = 64
ATTN_WIDTH = 512
CONV_WIDTH = 512
N_Q_HEADS = 8
N_KV_HEADS = 2
GQA_GROUP = 4
KV_WIDTH = 128
WINDOW = 128
BLOCK = 128
ROT_DIM = 16
ROPE_THETA = 500000.0
N_MOD = 9
LN_EPS = 1e-5
DN_ALPHA = 2.0 ** 0.25
IN_WIDTH = 2304
N_CHIPS = 4
N_DEV = 8
SMALL_ROWS = 32

ADAM_LR = 0.001
ADAM_B1 = 0.9
ADAM_B2 = 0.999
ADAM_EPS = 1e-08
ADAM_WD = 0.01
ADAM_STEP = 10

LANE = 128
HALO = 16
COL_CHUNK = 256
VMEM_LIMIT = 56 * 1024 * 1024


def _params(sem=None, vmem=True):
    return pltpu.CompilerParams(dimension_semantics=sem, vmem_limit_bytes=VMEM_LIMIT if vmem else None)


def _sigmoid(g):
    return 0.5 * jnp.tanh(0.5 * g) + 0.5


def _row_sum(v):
    return jnp.sum(v, axis=0, keepdims=True)


ROW_CHUNK = 16
EPILOGUE_UNROLL = 8


def _fold8(v):
    return v[0:8, :] + v[8:16, :]


def _row_chunk_loop(n_rows, step, init):
    per_iter = ROW_CHUNK * EPILOGUE_UNROLL
    assert n_rows % per_iter == 0, n_rows

    def body(it, carry):
        for s in range(EPILOGUE_UNROLL):
            start = pl.multiple_of(it * per_iter + s * ROW_CHUNK, ROW_CHUNK)
            carry = step(pl.ds(start, ROW_CHUNK), carry)
        return carry

    return lax.fori_loop(0, n_rows // per_iter, body, init)


def _ln_stats(r):
    mu = jnp.mean(r, axis=-1, keepdims=True)
    rc = r - mu
    var = jnp.mean(rc * rc, axis=-1, keepdims=True)
    rstd = lax.rsqrt(var + LN_EPS)
    return rc * rstd, rstd


def _ln_bwd(dxo, xhat, rstd, g):
    dxhat = dxo * g
    m1 = jnp.mean(dxhat, axis=-1, keepdims=True)
    m2 = jnp.mean(dxhat * xhat, axis=-1, keepdims=True)
    return rstd * (dxhat - m1 - xhat * m2)


def _dot_nt(a, b):
    return lax.dot_general(a, b, (((1,), (1,)), ((), ())), preferred_element_type=F32)


def _dot_tn(a, b):
    return lax.dot_general(a, b, (((0,), (0,)), ((), ())), preferred_element_type=F32)


def _full(shape):
    nd = len(shape)
    return pl.BlockSpec(shape, lambda *_: (0,) * nd)


def _resident(shape):
    nd = len(shape)
    return pl.BlockSpec(shape, lambda *_: (0,) * nd, pipeline_mode=pl.Buffered(1))


ANY_SPEC = pl.BlockSpec(memory_space=pl.ANY)


def _pcall(body, *, name, grid, in_specs, out_specs, out_shape, args, scratch_shapes=(), comm=None, prefetch=None):
    single = not isinstance(out_shape, (list, tuple))
    out_specs = [out_specs] if single else list(out_specs)
    out_shape = [out_shape] if single else list(out_shape)
    in_specs = list(in_specs)
    scratch_shapes = list(scratch_shapes)
    sem = ("arbitrary",) * len(grid)
    n_pre = 0 if prefetch is None else 1
    pre_args = () if prefetch is None else (prefetch,)

    def call(fn, ins_, outs_, shapes_, scratch_, aliases_, operands):
        if prefetch is None:
            return pl.pallas_call(fn, name=name, grid=grid, in_specs=ins_, out_specs=outs_, out_shape=shapes_,
                                  scratch_shapes=scratch_, input_output_aliases=aliases_,
                                  compiler_params=_params(sem))(*operands)
        spec = pltpu.PrefetchScalarGridSpec(num_scalar_prefetch=1, grid=grid, in_specs=ins_, out_specs=outs_,
                                            scratch_shapes=scratch_)
        return pl.pallas_call(fn, name=name, grid_spec=spec, out_shape=shapes_,
                              input_output_aliases={n_pre + i: o for i, o in aliases_.items()},
                              compiler_params=_params(sem))(*pre_args, *operands)

    if comm is None:
        res = call(body, in_specs, out_specs, out_shape, scratch_shapes, {}, args)
        return res[0] if single else res
    n_in, n_out, n_scr = len(in_specs), len(out_specs), len(scratch_shapes)
    nci, nco = len(comm.inputs), len(comm.out_shapes)
    n_steps = 1
    for g in grid:
        n_steps *= g
    staged = n_steps >= 8
    middle_step = (n_steps * 5) // 8 - 1
    late_step = n_steps - 1 - max(1, n_steps // 8)

    def wrapped(*refs):
        pre, refs = refs[:n_pre], refs[n_pre:]
        ins, refs = refs[:n_in], refs[n_in:]
        cin, refs = refs[:nci], refs[nci:]
        outs, refs = refs[:n_out], refs[n_out:]
        cout, refs = refs[:nco], refs[nco:]
        scr, csems = refs[:n_scr], refs[n_scr:]
        step = pl.program_id(0)
        for ax in range(1, len(grid)):
            step = step * grid[ax] + pl.program_id(ax)

        @pl.when(step == 0)
        def _():
            comm.start(cin, cout, csems)

        body(*pre, *ins, *outs, *scr)

        if staged:
            @pl.when(step == middle_step)
            def _():
                comm.middle(cin, cout, csems)

            @pl.when(step == late_step)
            def _():
                comm.late(cin, cout, csems)

        @pl.when(step == n_steps - 1)
        def _():
            if not staged:
                comm.middle(cin, cout, csems)
                comm.late(cin, cout, csems)
            comm.finish(cin, cout, csems)

    res = call(wrapped, in_specs + [ANY_SPEC] * nci, out_specs + [ANY_SPEC] * nco,
               out_shape + list(comm.out_shapes), scratch_shapes + list(comm.sems),
               {n_in + i: n_out + o for i, o in comm.aliases.items()}, (*args, *comm.inputs))
    main = res[:n_out]
    return (main[0] if single else main), list(res[n_out:])


def _ffn_up(xin, lnp, mod, w, *, seq, sc_idx, sh_idx, use_ln, name, comm=None):
    t, d = xin.shape
    f = w.shape[1] // 2
    tm = min(512, seq)
    tpb = seq // tm
    ch = min(COL_CHUNK, f)

    def body(x_ref, ln_ref, mod_ref, w_ref, h_ref, a_ref, dact_ref):
        x = x_ref[...]
        if use_ln:
            x = x * ln_ref[0:1, :] + ln_ref[1:2, :]
        h = x * (1.0 + mod_ref[0, sc_idx:sc_idx + 1, :]) + mod_ref[0, sh_idx:sh_idx + 1, :]
        hb = h.astype(BF16)
        h_ref[...] = hb
        for j in range(f // ch):
            g = jnp.dot(hb, w_ref[:, j * ch:(j + 1) * ch], preferred_element_type=F32)
            u = jnp.dot(hb, w_ref[:, f + j * ch:f + (j + 1) * ch], preferred_element_type=F32)
            s = _sigmoid(g)
            silu = g * s
            a_ref[:, j * ch:(j + 1) * ch] = (silu * u).astype(BF16)
            dact_ref[:, j * ch:(j + 1) * ch] = (u * (s + silu * (1.0 - s))).astype(BF16)
            dact_ref[:, f + j * ch:f + (j + 1) * ch] = silu.astype(BF16)

    return _pcall(
        body, name=name, grid=(t // tm,),
        in_specs=[pl.BlockSpec((tm, d), lambda i: (i, 0)), _full((2, d)),
                  pl.BlockSpec((1, N_MOD, d), lambda i: (i // tpb, 0, 0)), _resident((d, 2 * f))],
        out_specs=[pl.BlockSpec((tm, d), lambda i: (i, 0)), pl.BlockSpec((tm, f), lambda i: (i, 0)),
                   pl.BlockSpec((tm, 2 * f), lambda i: (i, 0))],
        out_shape=[jax.ShapeDtypeStruct((t, d), BF16), jax.ShapeDtypeStruct((t, f), BF16),
                   jax.ShapeDtypeStruct((t, 2 * f), BF16)],
        args=(xin, lnp, mod, w), comm=comm)


def _ffn_down_ln(a, wd, xin, lnp_in, mod, *, seq, gate_idx, use_ln, name, comm=None):
    t, f = a.shape
    d = wd.shape[1]
    tm = min(512, seq)
    tpb = seq // tm

    def body(a_ref, wd_ref, x_ref, ln_ref, mod_ref, f_ref, xhat_ref, rstd_ref, acc):
        av = a_ref[...]
        for j in range(d // COL_CHUNK):
            acc[:, j * COL_CHUNK:(j + 1) * COL_CHUNK] = jnp.dot(
                av, wd_ref[:, j * COL_CHUNK:(j + 1) * COL_CHUNK], preferred_element_type=F32)
        scale = 0.5 * (1.0 + mod_ref[0, gate_idx:gate_idx + 1, :])

        fo = acc[...]
        x = x_ref[...]
        if use_ln:
            x = x * ln_ref[0:1, :] + ln_ref[1:2, :]
        xhat, rstd = _ln_stats(DN_ALPHA * x + scale * fo)
        f_ref[...] = fo.astype(BF16)
        xhat_ref[...] = xhat
        rstd_ref[...] = rstd

    return _pcall(
        body, name=name, grid=(t // tm,),
        in_specs=[pl.BlockSpec((tm, f), lambda i: (i, 0)), _resident((f, d)),
                  pl.BlockSpec((tm, d), lambda i: (i, 0)), _full((2, d)),
                  pl.BlockSpec((1, N_MOD, d), lambda i: (i // tpb, 0, 0))],
        out_specs=[pl.BlockSpec((tm, d), lambda i: (i, 0)), pl.BlockSpec((tm, d), lambda i: (i, 0)),
                   pl.BlockSpec((tm, 1), lambda i: (i, 0))],
        out_shape=[jax.ShapeDtypeStruct((t, d), BF16), jax.ShapeDtypeStruct((t, d), F32),
                   jax.ShapeDtypeStruct((t, 1), F32)],
        scratch_shapes=[pltpu.VMEM((tm, d), F32)],
        args=(a, wd, xin, lnp_in, mod), comm=comm)


def _ffn_down_loss(a, wd, xhat_in, lnp_in, mod, lnp_out, tgt, *, seq, gate_idx, name):
    t, f = a.shape
    d = wd.shape[1]
    nb = t // seq
    tm = min(512, seq)
    tpb = seq // tm

    def body(a_ref, wd_ref, x_ref, lnin_ref, mod_ref, lnout_ref, tgt_ref,
             dr_ref, df_ref, loss_ref, dg_ref, db_ref, dgate_ref, acc):
        i = pl.program_id(0)
        av = a_ref[...]
        for j in range(d // COL_CHUNK):
            acc[:, j * COL_CHUNK:(j + 1) * COL_CHUNK] = jnp.dot(
                av, wd_ref[:, j * COL_CHUNK:(j + 1) * COL_CHUNK], preferred_element_type=F32)
        scale = 0.5 * (1.0 + mod_ref[0, gate_idx:gate_idx + 1, :])
        ag_in, ab_in = DN_ALPHA * lnin_ref[0:1, :], DN_ALPHA * lnin_ref[1:2, :]
        g_out, b_out = lnout_ref[0:1, :], lnout_ref[1:2, :]
        g_over_d = g_out * (1.0 / d)

        def chunk(rows, carry):
            s_loss, s_dg, s_db, s_gate = carry
            fo = acc[rows, :]
            xhat, rstd = _ln_stats(x_ref[rows, :] * ag_in + ab_in + scale * fo)
            e = xhat * g_out + b_out - tgt_ref[rows, :]
            dr = _ln_bwd(e, xhat, rstd, g_over_d)
            dr_ref[rows, :] = dr
            df_ref[rows, :] = (scale * dr).astype(BF16)
            return s_loss + _fold8(e * e), s_dg + _fold8(e * xhat), s_db + _fold8(e), s_gate + _fold8(fo * dr)

        zero = jnp.zeros((8, d), F32)
        s_loss, s_dg, s_db, s_gate = _row_chunk_loop(tm, chunk, (zero, zero, zero, zero))
        s_dg, s_db, s_gate = s_dg * (1.0 / d), s_db * (1.0 / d), s_gate * 0.5

        @pl.when(i == 0)
        def _():
            loss_ref[...] = jnp.zeros_like(loss_ref)
            dg_ref[...] = jnp.zeros_like(dg_ref)
            db_ref[...] = jnp.zeros_like(db_ref)

        @pl.when(i % tpb == 0)
        def _():
            dgate_ref[...] = jnp.zeros_like(dgate_ref)

        loss_ref[...] += _row_sum(s_loss)
        dg_ref[...] += _row_sum(s_dg)
        db_ref[...] += _row_sum(s_db)
        dgate_ref[0] += _row_sum(s_gate)

    return pl.pallas_call(
        body, name=name, grid=(t // tm,), scratch_shapes=[pltpu.VMEM((tm, d), F32)],
        in_specs=[pl.BlockSpec((tm, f), lambda i: (i, 0)), _resident((f, d)),
                  pl.BlockSpec((tm, d), lambda i: (i, 0)), _full((2, d)),
                  pl.BlockSpec((1, N_MOD, d), lambda i: (i // tpb, 0, 0)), _full((2, d)),
                  pl.BlockSpec((tm, d), lambda i: (i, 0))],
        out_specs=[pl.BlockSpec((tm, d), lambda i: (i, 0)), pl.BlockSpec((tm, d), lambda i: (i, 0)),
                   _full((1, d)), _full((1, d)), _full((1, d)),
                   pl.BlockSpec((1, 1, d), lambda i: (i // tpb, 0, 0))],
        out_shape=[jax.ShapeDtypeStruct((t, d), F32), jax.ShapeDtypeStruct((t, d), BF16),
                   jax.ShapeDtypeStruct((1, d), F32), jax.ShapeDtypeStruct((1, d), F32),
                   jax.ShapeDtypeStruct((1, d), F32), jax.ShapeDtypeStruct((nb, 1, d), F32)],
        compiler_params=_params(("arbitrary",)),
    )(a, wd, xhat_in, lnp_in, mod, lnp_out, tgt)


def _rope(v, cos, sa, sb):
    return v * cos + pltpu.roll(v, LANE - ROT_DIM // 2, 1) * sa + pltpu.roll(v, ROT_DIM // 2, 1) * sb


def _rope_t(dy, cos, sa, sb):
    return dy * cos + pltpu.roll(dy * sa, ROT_DIM // 2, 1) + pltpu.roll(dy * sb, LANE - ROT_DIM // 2, 1)


def _in_proj(xhat, lnp, mod, w_t, cos, sa, sb, *, seq, sc_idx, sh_idx, name, comm=None):
    t, d = xhat.shape
    tm = min(512, seq)
    tpb = seq // tm
    n_conv = 3 * CONV_WIDTH

    def body(x_ref, ln_ref, mod_ref, w_ref, cos_ref, sa_ref, sb_ref, h_ref, q_ref, k_ref, v_ref, ubc_ref):
        x = x_ref[...] * ln_ref[0:1, :] + ln_ref[1:2, :]
        h = x * (1.0 + mod_ref[0, sc_idx:sc_idx + 1, :]) + mod_ref[0, sh_idx:sh_idx + 1, :]
        hb = h.astype(BF16)
        h_ref[...] = hb
        cos_t, sa_t, sb_t = cos_ref[...], sa_ref[...], sb_ref[...]
        for j in range(ATTN_WIDTH // COL_CHUNK):
            p = _dot_nt(hb, w_ref[j * COL_CHUNK:(j + 1) * COL_CHUNK, :])
            for s in range(COL_CHUNK // LANE):
                q_ref[:, j * COL_CHUNK + s * LANE:j * COL_CHUNK + (s + 1) * LANE] = _rope(
                    p[:, s * LANE:(s + 1) * LANE], cos_t, sa_t, sb_t).astype(BF16)
        p = _dot_nt(hb, w_ref[ATTN_WIDTH:ATTN_WIDTH + 2 * KV_WIDTH, :])
        k_ref[...] = _rope(p[:, 0:KV_WIDTH], cos_t, sa_t, sb_t).astype(BF16)
        v_ref[...] = p[:, KV_WIDTH:].astype(BF16)
        base = ATTN_WIDTH + 2 * KV_WIDTH
        for j in range(n_conv // COL_CHUNK):
            ubc_ref[:, j * COL_CHUNK:(j + 1) * COL_CHUNK] = _dot_nt(
                hb, w_ref[base + j * COL_CHUNK:base + (j + 1) * COL_CHUNK, :]).astype(BF16)

    row = lambda w: pl.BlockSpec((tm, w), lambda i: (i, 0))
    return _pcall(
        body, name=name, grid=(t // tm,),
        in_specs=[row(d), _full((2, d)), pl.BlockSpec((1, N_MOD, d), lambda i: (i // tpb, 0, 0)),
                  _resident((IN_WIDTH, d)), row(LANE), row(LANE), row(LANE)],
        out_specs=[row(d), row(ATTN_WIDTH), row(KV_WIDTH), row(KV_WIDTH), row(n_conv)],
        out_shape=[jax.ShapeDtypeStruct((t, d), BF16), jax.ShapeDtypeStruct((t, ATTN_WIDTH), BF16),
                   jax.ShapeDtypeStruct((t, KV_WIDTH), BF16), jax.ShapeDtypeStruct((t, KV_WIDTH), BF16),
                   jax.ShapeDtypeStruct((t, n_conv), BF16)],
        args=(xhat, lnp, mod, w_t, cos, sa, sb), comm=comm)


ATTN_TILE_BLOCKS = 2


def _attn_sub_block(s, tile, nblk, kp_ref, kc_ref, vp_ref, vc_ref):
    rows = slice(s * BLOCK, (s + 1) * BLOCK)
    if s == 0:
        first = ((tile * ATTN_TILE_BLOCKS) % nblk) == 0
        return rows, (kp_ref, slice(0, BLOCK)), (kc_ref, rows), (vp_ref, slice(0, BLOCK)), (vc_ref, rows), first
    before = slice((s - 1) * BLOCK, s * BLOCK)
    return rows, (kc_ref, before), (kc_ref, rows), (vc_ref, before), (vc_ref, rows), False


def _attn_group(q_ref, rows, k_prev, k_cur, v_prev, v_cur, sink_ref, g, first):
    lo, hi = g * HEAD_DIM, (g + 1) * HEAD_DIM
    kk = jnp.concatenate([k_prev[0][k_prev[1], lo:hi], k_cur[0][k_cur[1], lo:hi]], axis=0)
    vv = jnp.concatenate([v_prev[0][v_prev[1], lo:hi], v_cur[0][v_cur[1], lo:hi]], axis=0)
    qs = jnp.concatenate([q_ref[rows, (GQA_GROUP * g + j) * HEAD_DIM:(GQA_GROUP * g + j + 1) * HEAD_DIM]
                          for j in range(GQA_GROUP)], axis=0)
    cols = GQA_GROUP * BLOCK
    ki = lax.broadcasted_iota(jnp.int32, (2 * BLOCK, cols), 0)
    col = lax.broadcasted_iota(jnp.int32, (2 * BLOCK, cols), 1)
    diff = (col & (BLOCK - 1)) + BLOCK - ki
    valid = (diff >= 0) & (diff < WINDOW) & ((ki >= BLOCK) | jnp.logical_not(first))
    s = _dot_nt(kk, qs) * (HEAD_DIM ** -0.5)
    s = jnp.where(valid, s, -1e30)
    hcol = lax.broadcasted_iota(jnp.int32, (1, cols), 1)
    sink = jnp.zeros((1, cols), F32)
    for j in range(GQA_GROUP):
        sink = jnp.where(hcol // BLOCK == j, sink_ref[GQA_GROUP * g + j], sink)
    m = jnp.maximum(jnp.max(s, axis=0, keepdims=True), sink)
    p = jnp.exp(s - m)
    ps = jnp.exp(sink - m)
    inv = 1.0 / (jnp.sum(p, axis=0, keepdims=True) + ps)
    return qs, kk, vv, p * inv, ps * inv


def _heads_to_lanes(x_t):
    return jnp.concatenate([x_t[:, j * BLOCK:(j + 1) * BLOCK].T for j in range(GQA_GROUP)], axis=1)


def _attention(q, k, v, sinks, *, seq, name, comm=None):
    t = q.shape[0]
    nblk = seq // BLOCK
    tile = ATTN_TILE_BLOCKS * BLOCK

    def body(q_ref, kp_ref, kc_ref, vp_ref, vc_ref, sink_ref, o_ref):
        for s in range(ATTN_TILE_BLOCKS):
            rows, k_prev, k_cur, v_prev, v_cur, first = _attn_sub_block(
                s, pl.program_id(0), nblk, kp_ref, kc_ref, vp_ref, vc_ref)
            outs = []
            for g in range(N_KV_HEADS):
                _, _, vv, pn, _ = _attn_group(q_ref, rows, k_prev, k_cur, v_prev, v_cur, sink_ref, g, first)
                outs.append(_heads_to_lanes(_dot_tn(vv, pn.astype(BF16))))
            o_ref[rows, :] = jnp.concatenate(outs, axis=1).astype(BF16)

    cur = lambda w: pl.BlockSpec((tile, w), lambda n: (n, 0))
    prev = lambda w: pl.BlockSpec((BLOCK, w), lambda n: (jnp.maximum(n * ATTN_TILE_BLOCKS - 1, 0), 0))
    return _pcall(
        body, name=name, grid=(t // tile,),
        in_specs=[cur(ATTN_WIDTH), prev(KV_WIDTH), cur(KV_WIDTH), prev(KV_WIDTH), cur(KV_WIDTH),
                  pl.BlockSpec(memory_space=pltpu.SMEM)],
        out_specs=cur(ATTN_WIDTH),
        out_shape=jax.ShapeDtypeStruct((t, ATTN_WIDTH), BF16),
        args=(q, k, k, v, v, sinks), comm=comm)


def _out_proj(attn, ubc, cw, wout, xhat_in, lnp_in, mod, *, seq, gate_idx, name, comm=None):
    t, d = xhat_in.shape
    tm = min(512, seq)
    tpb = seq // tm
    cwid = CONV_WIDTH

    def body(attn_ref, ubc_ref, halo_ref, cw_ref, w_ref, x_ref, ln_ref, mod_ref,
             mixin_ref, mix_ref, xhat_ref, rstd_ref, zbuf, acc):
        first = (pl.program_id(0) % tpb) == 0
        u, bg, cg = (ubc_ref[:, s * cwid:(s + 1) * cwid].astype(F32) for s in range(3))
        z = cg * u
        hz = halo_ref[:, 2 * cwid:3 * cwid].astype(F32) * halo_ref[:, 0:cwid].astype(F32)
        zbuf[0:HALO, :] = jnp.where(first, 0.0, hz)
        zbuf[HALO:HALO + tm, :] = z
        y = (cw_ref[0:1, :] * zbuf[HALO - 2:HALO - 2 + tm, :] + cw_ref[1:2, :] * zbuf[HALO - 1:HALO - 1 + tm, :]
             + cw_ref[2:3, :] * z)
        mixin_ref[:, 0:ATTN_WIDTH] = attn_ref[...]
        mixin_ref[:, ATTN_WIDTH:] = (bg * y).astype(BF16)
        mv = mixin_ref[...]
        for j in range(d // COL_CHUNK):
            acc[:, j * COL_CHUNK:(j + 1) * COL_CHUNK] = jnp.dot(
                mv, w_ref[:, j * COL_CHUNK:(j + 1) * COL_CHUNK], preferred_element_type=F32)
        scale = 1.0 + mod_ref[0, gate_idx:gate_idx + 1, :]

        mix = acc[...]
        xhat, rstd = _ln_stats(DN_ALPHA * (x_ref[...] * ln_ref[0:1, :] + ln_ref[1:2, :]) + scale * mix)
        mix_ref[...] = mix.astype(BF16)
        xhat_ref[...] = xhat
        rstd_ref[...] = rstd

    row = lambda w: pl.BlockSpec((tm, w), lambda i: (i, 0))
    return _pcall(
        body, name=name, grid=(t // tm,),
        in_specs=[row(ATTN_WIDTH), row(3 * cwid),
                  pl.BlockSpec((HALO, 3 * cwid), lambda i: (jnp.maximum(i * (tm // HALO) - 1, 0), 0)),
                  _full((8, cwid)), _resident((d, d)), row(d), _full((2, d)),
                  pl.BlockSpec((1, N_MOD, d), lambda i: (i // tpb, 0, 0))],
        out_specs=[row(d), row(d), row(d), row(1)],
        out_shape=[jax.ShapeDtypeStruct((t, d), BF16), jax.ShapeDtypeStruct((t, d), BF16),
                   jax.ShapeDtypeStruct((t, d), F32), jax.ShapeDtypeStruct((t, 1), F32)],
        scratch_shapes=[pltpu.VMEM((tm + HALO, cwid), F32), pltpu.VMEM((tm, d), F32)],
        args=(attn, ubc, ubc, cw, wout, xhat_in, lnp_in, mod), comm=comm)


def _ffn_bwd_act(df, wd, dact, *, seq, name, comm=None):
    t, d = df.shape
    f = wd.shape[0]
    tm = min(512, seq)
    ch = min(COL_CHUNK, f)

    def body(df_ref, wd_ref, dact_ref, dgu_ref):
        dfv = df_ref[...]
        for j in range(f // ch):
            da = _dot_nt(dfv, wd_ref[j * ch:(j + 1) * ch, :])
            dgu_ref[:, j * ch:(j + 1) * ch] = (da * dact_ref[:, j * ch:(j + 1) * ch].astype(F32)).astype(BF16)
            dgu_ref[:, f + j * ch:f + (j + 1) * ch] = (
                da * dact_ref[:, f + j * ch:f + (j + 1) * ch].astype(F32)).astype(BF16)

    return _pcall(
        body, name=name, grid=(t // tm,),
        in_specs=[pl.BlockSpec((tm, d), lambda i: (i, 0)), _resident((f, d)),
                  pl.BlockSpec((tm, 2 * f), lambda i: (i, 0))],
        out_specs=pl.BlockSpec((tm, 2 * f), lambda i: (i, 0)),
        out_shape=jax.ShapeDtypeStruct((t, 2 * f), BF16),
        args=(df, wd, dact), comm=comm)


def _bwd_in(a, w, dr, xin, rstd_prev, lnp_prev, mod, branch_prev, *, seq, w_is_nt, sc_idx, gate_idx,
            branch_scale, final, name, comm=None):
    t, kdim = a.shape
    d = dr.shape[1]
    nb = t // seq
    tm = min(512, seq)
    tpb = seq // tm

    def body(*refs):
        if final:
            a_ref, w_ref, dr_ref, x_ref, mod_ref, dx_ref, dsc_ref, dsh_ref, acc = refs
        else:
            (a_ref, w_ref, dr_ref, x_ref, rstd_ref, ln_ref, mod_ref, br_ref,
             drp_ref, dbr_ref, dsc_ref, dsh_ref, dgate_ref, dg_ref, db_ref, acc) = refs
        i = pl.program_id(0)
        av = a_ref[...]
        for j in range(d // COL_CHUNK):
            cols = slice(j * COL_CHUNK, (j + 1) * COL_CHUNK)
            acc[:, cols] = (_dot_nt(av, w_ref[cols, :]) if w_is_nt
                            else jnp.dot(av, w_ref[:, cols], preferred_element_type=F32))
        sc1 = 1.0 + mod_ref[0, sc_idx:sc_idx + 1, :]
        if not final:
            g_prev, b_prev = ln_ref[0:1, :], ln_ref[1:2, :]
            bscale = branch_scale * (1.0 + mod_ref[0, gate_idx:gate_idx + 1, :])

        def chunk(rows, carry):
            dh = acc[rows, :]
            dx = DN_ALPHA * dr_ref[rows, :] + dh * sc1
            if final:
                dx_ref[rows, :] = dx
                return carry[0] + _fold8(dh * x_ref[rows, :]), carry[1] + _fold8(dh)
            xhat = x_ref[rows, :]
            drp = _ln_bwd(dx, xhat, rstd_ref[rows, :], g_prev)
            drp_ref[rows, :] = drp
            dbr_ref[rows, :] = (bscale * drp).astype(BF16)
            return (carry[0] + _fold8(dh * xhat), carry[1] + _fold8(dh),
                    carry[2] + _fold8(br_ref[rows, :].astype(F32) * drp),
                    carry[3] + _fold8(dx * xhat), carry[4] + _fold8(dx))

        zero = jnp.zeros((8, d), F32)
        sums = list(_row_chunk_loop(tm, chunk, (zero,) * (2 if final else 5)))
        if not final:
            sums[0] = sums[0] * g_prev + sums[1] * b_prev
            sums[2] = sums[2] * branch_scale

        @pl.when((i % tpb) == 0)
        def _():
            dsc_ref[...] = jnp.zeros_like(dsc_ref)
            dsh_ref[...] = jnp.zeros_like(dsh_ref)
            if not final:
                dgate_ref[...] = jnp.zeros_like(dgate_ref)

        dsc_ref[0] += _row_sum(sums[0])
        dsh_ref[0] += _row_sum(sums[1])
        if not final:
            @pl.when(i == 0)
            def _():
                dg_ref[...] = jnp.zeros_like(dg_ref)
                db_ref[...] = jnp.zeros_like(db_ref)

            dgate_ref[0] += _row_sum(sums[2])
            dg_ref[...] += _row_sum(sums[3])
            db_ref[...] += _row_sum(sums[4])

    row = lambda w_: pl.BlockSpec((tm, w_), lambda i: (i, 0))
    vec = pl.BlockSpec((1, 1, d), lambda i: (i // tpb, 0, 0))
    mod_spec = pl.BlockSpec((1, N_MOD, d), lambda i: (i // tpb, 0, 0))
    vshape = jax.ShapeDtypeStruct((nb, 1, d), F32)
    if final:
        in_specs = [row(kdim), _resident(w.shape), row(d), row(d), mod_spec]
        args = (a, w, dr, xin, mod)
        out_specs = [row(d), vec, vec]
        out_shape = [jax.ShapeDtypeStruct((t, d), F32), vshape, vshape]
    else:
        in_specs = [row(kdim), _resident(w.shape), row(d), row(d), row(1), _full((2, d)), mod_spec, row(d)]
        args = (a, w, dr, xin, rstd_prev, lnp_prev, mod, branch_prev)
        out_specs = [row(d), row(d), vec, vec, vec, _full((1, d)), _full((1, d))]
        out_shape = [jax.ShapeDtypeStruct((t, d), F32), jax.ShapeDtypeStruct((t, d), BF16), vshape, vshape, vshape,
                     jax.ShapeDtypeStruct((1, d), F32), jax.ShapeDtypeStruct((1, d), F32)]
    return _pcall(
        body, name=name, grid=(t // tm,), in_specs=in_specs, out_specs=out_specs, out_shape=out_shape,
        scratch_shapes=[pltpu.VMEM((tm, d), F32)], args=args, comm=comm)


def _matmul_tn(a, b, *, tmm, tnn, name, comm=None):
    t, m = a.shape
    n = b.shape[1]
    tk = min(2048, t)

    def body(a_ref, b_ref, o_ref):
        @pl.when(pl.program_id(2) == 0)
        def _():
            o_ref[...] = jnp.zeros_like(o_ref)
        o_ref[...] += _dot_tn(a_ref[...], b_ref[...])

    return _pcall(
        body, name=name, grid=(m // tmm, n // tnn, t // tk),
        in_specs=[pl.BlockSpec((tk, tmm), lambda i, j, k: (k, i)), pl.BlockSpec((tk, tnn), lambda i, j, k: (k, j))],
        out_specs=pl.BlockSpec((tmm, tnn), lambda i, j, k: (i, j)),
        out_shape=jax.ShapeDtypeStruct((m, n), F32),
        args=(a, b), comm=comm)


def _grad_chip_sum(pos, a, b, *, name, comm=None):
    t, m = a.shape
    n = b.shape[1]
    hm, tnn = m // 2, n // N_CHIPS
    tk = min(2048, t)
    nk = t // tk
    n_j = n // tnn

    def body(pos_ref, a_ref, b_ref, s32_ref, s16_ref, land_ref, acc, theirs, send_sems, recv_sems, copy_sem):
        p, j, k = pl.program_id(0), pl.program_id(1), pl.program_id(2)
        x, y, c = _position()

        def push(jj):
            return pltpu.make_async_remote_copy(
                src_ref=acc.at[jj], dst_ref=land_ref.at[jj], send_sem=send_sems.at[jj], recv_sem=recv_sems.at[jj],
                device_id=(x, y, 1 - c), device_id_type=MESH)

        fetch = pltpu.make_async_copy(land_ref.at[j], theirs, copy_sem)

        @pl.when(jnp.logical_and(p == 1, k == 0))
        def _():
            push(j).wait_send()
            push(j).wait_recv()
            fetch.start()

        part = _dot_tn(a_ref[...], b_ref[...])

        @pl.when(k == 0)
        def _():
            acc[j] = part

        @pl.when(k > 0)
        def _():
            acc[j] += part

        @pl.when(jnp.logical_and(p == 0, k == nk - 1))
        def _():
            push(j).start()

        @pl.when(jnp.logical_and(p == 1, k == nk - 1))
        def _():
            fetch.wait()
            s = acc[j] + theirs[...]
            s32_ref[0] = s
            s16_ref[0] = s.astype(BF16)

    half = lambda p, pos_ref: 1 - pos_ref[2] - p + 2 * p * pos_ref[2]
    out_tile = pl.BlockSpec((1, hm, tnn), lambda p, j, k, pos_ref: (0, 0, j * p))
    shape = lambda dt: jax.ShapeDtypeStruct((1, hm, n), dt)
    out = _pcall(
        body, name=name, grid=(2, n_j, nk),
        in_specs=[pl.BlockSpec((tk, hm), lambda p, j, k, pos_ref: (k, half(p, pos_ref))),
                  pl.BlockSpec((tk, tnn), lambda p, j, k, pos_ref: (k, j))],
        out_specs=[out_tile, out_tile, ANY_SPEC],
        out_shape=[shape(F32), shape(BF16), jax.ShapeDtypeStruct((n_j, hm, tnn), F32)],
        scratch_shapes=[pltpu.VMEM((n_j, hm, tnn), F32), pltpu.VMEM((hm, tnn), F32),
                        pltpu.SemaphoreType.DMA((n_j,)), pltpu.SemaphoreType.DMA((n_j,)), pltpu.SemaphoreType.DMA],
        args=(a, b), prefetch=pos, comm=comm)
    if comm is None:
        return out[0], out[1]
    (s32, s16, _), extra = out
    return (s32, s16), extra


def _matmul_nt_bf16(a, w, *, seq, name):
    t, kdim = a.shape
    n = w.shape[0]
    tm = min(512, seq)

    def body(a_ref, w_ref, o_ref):
        av = a_ref[...]
        for j in range(n // COL_CHUNK):
            o_ref[:, j * COL_CHUNK:(j + 1) * COL_CHUNK] = _dot_nt(
                av, w_ref[j * COL_CHUNK:(j + 1) * COL_CHUNK, :]).astype(BF16)

    return pl.pallas_call(
        body, name=name, grid=(t // tm,),
        in_specs=[pl.BlockSpec((tm, kdim), lambda i: (i, 0)), _resident((n, kdim))],
        out_specs=pl.BlockSpec((tm, n), lambda i: (i, 0)),
        out_shape=jax.ShapeDtypeStruct((t, n), BF16),
        compiler_params=_params(("arbitrary",)),
    )(a, w)


def _attention_bwd(q, k, v, dmixin, sinks, *, seq, name, comm=None):
    t = q.shape[0]
    nblk = seq // BLOCK
    tile = ATTN_TILE_BLOCKS * BLOCK

    def body(q_ref, kp_ref, kc_ref, vp_ref, vc_ref, do_ref, sink_ref,
             dq_ref, dkp_ref, dkc_ref, dvp_ref, dvc_ref, dsink_ref):
        n = pl.program_id(0)

        @pl.when(n == 0)
        def _():
            dsink_ref[...] = jnp.zeros_like(dsink_ref)

        srow = lax.broadcasted_iota(jnp.int32, (8, LANE), 0)
        dsink = jnp.zeros((8, LANE), F32)
        for s in range(ATTN_TILE_BLOCKS):
            rows, k_prev, k_cur, v_prev, v_cur, first = _attn_sub_block(s, n, nblk, kp_ref, kc_ref, vp_ref, vc_ref)
            dqs, dks, dvs = [], [], []
            for g in range(N_KV_HEADS):
                qs, kk, vv, pn, psn = _attn_group(q_ref, rows, k_prev, k_cur, v_prev, v_cur, sink_ref, g, first)
                dos = jnp.concatenate(
                    [do_ref[rows, (GQA_GROUP * g + j) * HEAD_DIM:(GQA_GROUP * g + j + 1) * HEAD_DIM]
                     for j in range(GQA_GROUP)], axis=0)
                dp = _dot_nt(vv, dos)
                delta = jnp.sum(pn * dp, axis=0, keepdims=True)
                ds = pn * (dp - delta)
                dsk = psn * delta
                for j in range(GQA_GROUP):
                    tot = jnp.sum(dsk[:, j * BLOCK:(j + 1) * BLOCK], axis=1, keepdims=True)
                    dsink = dsink - jnp.where(srow == GQA_GROUP * g + j, tot, 0.0)
                dsb = (ds * (HEAD_DIM ** -0.5)).astype(BF16)
                dqs.append(_heads_to_lanes(_dot_tn(kk, dsb)))
                dks.append(jnp.dot(dsb, qs, preferred_element_type=F32))
                dvs.append(jnp.dot(pn.astype(BF16), dos, preferred_element_type=F32))
            dq_ref[rows, :] = jnp.concatenate(dqs, axis=1)
            dkp_ref[rows, :] = jnp.concatenate([x[0:BLOCK, :] for x in dks], axis=1)
            dkc_ref[rows, :] = jnp.concatenate([x[BLOCK:, :] for x in dks], axis=1)
            dvp_ref[rows, :] = jnp.concatenate([x[0:BLOCK, :] for x in dvs], axis=1)
            dvc_ref[rows, :] = jnp.concatenate([x[BLOCK:, :] for x in dvs], axis=1)
        dsink_ref[...] += dsink

    cur = lambda w: pl.BlockSpec((tile, w), lambda n: (n, 0))
    prev = lambda w: pl.BlockSpec((BLOCK, w), lambda n: (jnp.maximum(n * ATTN_TILE_BLOCKS - 1, 0), 0))
    kv = jax.ShapeDtypeStruct((t, KV_WIDTH), F32)
    return _pcall(
        body, name=name, grid=(t // tile,),
        in_specs=[cur(ATTN_WIDTH), prev(KV_WIDTH), cur(KV_WIDTH), prev(KV_WIDTH), cur(KV_WIDTH), cur(ATTN_WIDTH),
                  pl.BlockSpec(memory_space=pltpu.SMEM)],
        out_specs=[cur(ATTN_WIDTH), cur(KV_WIDTH), cur(KV_WIDTH), cur(KV_WIDTH), cur(KV_WIDTH), _full((8, LANE))],
        out_shape=[jax.ShapeDtypeStruct((t, ATTN_WIDTH), F32), kv, kv, kv, kv, jax.ShapeDtypeStruct((8, LANE), F32)],
        args=(q, k, k, v, v, dmixin, sinks), comm=comm)


def _mix_bwd_assemble(dq, dkp, dkc, dvp, dvc, cos, sa, sb, dmixin, ubc, cw, *, seq, name, comm=None):
    t = dq.shape[0]
    cwid = CONV_WIDTH
    tm = min(2 * BLOCK, seq)
    tiles_per_seq = seq // tm
    ntile = t // tm
    nblk_all = t // BLOCK
    per_tile = tm // BLOCK

    def body(*refs):
        dq_ref, dkc_ref, dvc_ref = refs[0:3]
        dkp_refs, dvp_refs = refs[3:3 + per_tile], refs[3 + per_tile:3 + 2 * per_tile]
        (cos_ref, sa_ref, sb_ref, dco_ref, dcon_ref, ubc_ref, hprev_ref, hnext_ref, cw_ref,
         dproj_ref, dcw_ref, zbuf, dybuf) = refs[3 + 2 * per_tile:]
        i = pl.program_id(0)
        first = (i % tiles_per_seq) == 0
        last = (i % tiles_per_seq) == tiles_per_seq - 1
        glast = i == ntile - 1

        @pl.when(i == 0)
        def _():
            dcw_ref[...] = jnp.zeros_like(dcw_ref)

        def with_next_block(cur_ref, nxt_refs):
            nxt = [r[...] for r in nxt_refs]
            nxt[-1] = jnp.where(glast, 0.0, nxt[-1])
            return cur_ref[...] + jnp.concatenate(nxt, axis=0)

        cos_t, sa_t, sb_t = cos_ref[...], sa_ref[...], sb_ref[...]
        for j in range(ATTN_WIDTH // LANE):
            dproj_ref[:, j * LANE:(j + 1) * LANE] = _rope_t(
                dq_ref[:, j * LANE:(j + 1) * LANE], cos_t, sa_t, sb_t).astype(BF16)
        dk = with_next_block(dkc_ref, dkp_refs)
        dproj_ref[:, ATTN_WIDTH:ATTN_WIDTH + KV_WIDTH] = _rope_t(dk, cos_t, sa_t, sb_t).astype(BF16)
        dv = with_next_block(dvc_ref, dvp_refs)
        dproj_ref[:, ATTN_WIDTH + KV_WIDTH:ATTN_WIDTH + 2 * KV_WIDTH] = dv.astype(BF16)

        u, bg, cg = (ubc_ref[:, s * cwid:(s + 1) * cwid].astype(F32) for s in range(3))
        z = cg * u
        hz = hprev_ref[:, 2 * cwid:3 * cwid].astype(F32) * hprev_ref[:, 0:cwid].astype(F32)
        zbuf[0:HALO, :] = jnp.where(first, 0.0, hz)
        zbuf[HALO:HALO + tm, :] = z
        z2, z1 = zbuf[HALO - 2:HALO - 2 + tm, :], zbuf[HALO - 1:HALO - 1 + tm, :]
        w0, w1, w2 = cw_ref[0:1, :], cw_ref[1:2, :], cw_ref[2:3, :]
        y = w0 * z2 + w1 * z1 + w2 * z
        dco = dco_ref[...].astype(F32)
        dyc = dco * bg
        dyn = dcon_ref[...].astype(F32) * hnext_ref[:, cwid:2 * cwid].astype(F32)
        dybuf[0:tm, :] = dyc
        dybuf[tm:tm + HALO, :] = jnp.where(last, 0.0, dyn)
        dz = w2 * dyc + w1 * dybuf[1:1 + tm, :] + w0 * dybuf[2:2 + tm, :]
        srow = lax.broadcasted_iota(jnp.int32, (8, cwid), 0)
        dcw_ref[...] += (jnp.where(srow == 0, _row_sum(dyc * z2), 0.0) + jnp.where(srow == 1, _row_sum(dyc * z1), 0.0)
                         + jnp.where(srow == 2, _row_sum(dyc * z), 0.0))
        base = ATTN_WIDTH + 2 * KV_WIDTH
        dproj_ref[:, base:base + cwid] = (dz * cg).astype(BF16)
        dproj_ref[:, base + cwid:base + 2 * cwid] = (dco * y).astype(BF16)
        dproj_ref[:, base + 2 * cwid:base + 3 * cwid] = (dz * u).astype(BF16)

    cur = lambda w: pl.BlockSpec((tm, w), lambda i: (i, 0))
    nxt = [pl.BlockSpec((BLOCK, KV_WIDTH), lambda i, s=s: (jnp.minimum(i * per_tile + s + 1, nblk_all - 1), 0))
           for s in range(per_tile)]
    prev_halo = pl.BlockSpec((HALO, 3 * cwid), lambda i: (jnp.maximum(i * (tm // HALO) - 1, 0), 0))
    next_halo = lambda w, col: pl.BlockSpec(
        (HALO, w), lambda i: (jnp.minimum((i + 1) * (tm // HALO), t // HALO - 1), col))
    return _pcall(
        body, name=name, grid=(ntile,),
        in_specs=[cur(ATTN_WIDTH), cur(KV_WIDTH), cur(KV_WIDTH), *nxt, *nxt,
                  cur(LANE), cur(LANE), cur(LANE),
                  pl.BlockSpec((tm, cwid), lambda i: (i, 1)), next_halo(cwid, 1),
                  cur(3 * cwid), prev_halo, next_halo(3 * cwid, 0), _full((8, cwid))],
        out_specs=[cur(IN_WIDTH), _full((8, cwid))],
        out_shape=[jax.ShapeDtypeStruct((t, IN_WIDTH), BF16), jax.ShapeDtypeStruct((8, cwid), F32)],
        scratch_shapes=[pltpu.VMEM((tm + HALO, cwid), F32), pltpu.VMEM((tm + HALO, cwid), F32)],
        args=(dq, dkc, dvc, *([dkp] * per_tile), *([dvp] * per_tile), cos, sa, sb, dmixin, dmixin,
              ubc, ubc, ubc, cw), comm=comm)


def _ada_fwd(c_all, w_ada, b_ada_shard, *, name, comm=None):
    nb, d = c_all.shape
    n = w_ada.shape[1]
    tn = n // 2

    def body(c_ref, w_ref, b_ref, o_ref):
        cv = c_ref[...]
        cond = cv * _sigmoid(cv)
        o_ref[...] = jnp.dot(cond, w_ref[...], preferred_element_type=F32,
                             precision=lax.Precision.HIGHEST) + b_ref[...]

    return _pcall(
        body, name=name, grid=(n // tn,),
        in_specs=[_full((nb, d)), pl.BlockSpec((d, tn), lambda j: (0, j)), pl.BlockSpec((1, tn), lambda j: (0, j))],
        out_specs=pl.BlockSpec((nb, tn), lambda j: (0, j)),
        out_shape=jax.ShapeDtypeStruct((nb, n), F32), args=(c_all, w_ada, b_ada_shard), comm=comm)


def _small_finish(gathered, dmod_all, dmod_shard, c_all_t, *, name):
    d = D_MODEL
    nb, n = dmod_shard.shape

    def body(g_ref, dm_ref, dms_ref, ct_ref, sum_ref, gw_ref, gb_ref):
        total = g_ref[0]
        for dev in range(1, N_DEV):
            total = total + g_ref[dev]
        sum_ref[...] = total
        gb_ref[...] = _row_sum(dm_ref[...])
        ctv = ct_ref[...]
        cond_t = ctv * _sigmoid(ctv)
        for jb in range(n // COL_CHUNK):
            gw_ref[:, jb * COL_CHUNK:(jb + 1) * COL_CHUNK] = jnp.dot(
                cond_t, dms_ref[:, jb * COL_CHUNK:(jb + 1) * COL_CHUNK], preferred_element_type=F32,
                precision=lax.Precision.HIGHEST)

    return pl.pallas_call(
        body, name=name, grid=(1,),
        in_specs=[_full((N_DEV, SMALL_ROWS, d)), _full((nb, N_MOD * d)), _full((nb, n)), _full((d, nb))],
        out_specs=[_full((SMALL_ROWS, d)), _full((d, n)), _full((1, N_MOD * d))],
        out_shape=[jax.ShapeDtypeStruct((SMALL_ROWS, d), F32), jax.ShapeDtypeStruct((d, n), F32),
                   jax.ShapeDtypeStruct((1, N_MOD * d), F32)],
        compiler_params=_params(("arbitrary",)),
    )(gathered, dmod_all, dmod_shard, c_all_t)


def _row_tile(r, c, budget=1 << 21):
    if r * c * 4 <= budget or r % 16:
        return r
    best = 16
    for tr in range(16, r + 1, 16):
        if r % tr == 0 and tr * c * 4 <= budget:
            best = tr
    return best


def _cast_into(w, chip, col_kind, *, name):
    r, c = w.shape
    tr = _row_tile(r, c)

    def body(chip_ref, w_ref, o_ref):
        o_ref[...] = w_ref[...].astype(BF16)

    if col_kind:
        out_spec = pl.BlockSpec((tr, c), lambda i, chip_ref: (i, chip_ref[0]))
        out_shape = jax.ShapeDtypeStruct((r, c * N_CHIPS), BF16)
    else:
        out_spec = pl.BlockSpec((tr, c), lambda i, chip_ref: (chip_ref[0] * (r // tr) + i, 0))
        out_shape = jax.ShapeDtypeStruct((r * N_CHIPS, c), BF16)
    return _pcall(body, name=name, grid=(r // tr,), in_specs=[pl.BlockSpec((tr, c), lambda i, chip_ref: (i, 0))],
                  out_specs=out_spec, out_shape=out_shape, args=(w,), prefetch=chip)


def _adamw(w, g, m, v, *, name, comm=None):
    r, c = w.shape
    tr = _row_tile(r, c)
    c1 = 1.0 - ADAM_B1 ** ADAM_STEP
    c2 = 1.0 - ADAM_B2 ** ADAM_STEP

    def body(w_ref, g_ref, m_ref, v_ref, d_ref, nm_ref, nv_ref):
        gv = g_ref[...]
        m2 = ADAM_B1 * m_ref[...] + (1.0 - ADAM_B1) * gv
        v2 = ADAM_B2 * v_ref[...] + (1.0 - ADAM_B2) * (gv * gv)
        d_ref[...] = -ADAM_LR * ((m2 / c1) / (jnp.sqrt(v2 / c2) + ADAM_EPS) + ADAM_WD * w_ref[...])
        nm_ref[...] = m2
        nv_ref[...] = v2

    spec = pl.BlockSpec((tr, c), lambda i: (i, 0))
    sh = jax.ShapeDtypeStruct((r, c), F32)
    return _pcall(body, name=name, grid=(r // tr,), in_specs=[spec] * 4, out_specs=[spec] * 3, out_shape=[sh] * 3,
                  args=(w, g, m, v), comm=comm)


def _sum_pair(pos, g3, r3, blk_of, *, name, comm=None):
    n, rows, cols = r3.shape
    tr = _row_tile(rows, cols)

    def body(pos_ref, g_ref, r_ref, s32_ref, s16_ref):
        s = g_ref[0] + r_ref[0]
        s32_ref[0] = s
        s16_ref[0] = s.astype(BF16)

    own = pl.BlockSpec((1, tr, cols), lambda p, i, pos: (blk_of(p, pos), i, 0))
    plain = pl.BlockSpec((1, tr, cols), lambda p, i, pos: (p, i, 0))
    return _pcall(
        body, name=name, grid=(n, rows // tr), in_specs=[own, plain], out_specs=[plain, plain],
        out_shape=[jax.ShapeDtypeStruct((n, rows, cols), F32), jax.ShapeDtypeStruct((n, rows, cols), BF16)],
        args=(g3, r3), prefetch=pos, comm=comm)


def _sum_final(pos, s32, recv, *, col_kind, n_shard, name, comm=None):
    if col_kind:
        rows, cols = s32.shape[1], n_shard
        own = lambda tr: pl.BlockSpec((1, tr, cols), lambda i, pos: (0, i, 2 * pos[0] + pos[1]))
    else:
        rows, cols = s32.shape[1], s32.shape[2]
        own = lambda tr: pl.BlockSpec((1, tr, cols), lambda i, pos: (2 * pos[0] + pos[1], i, 0))
    tr = _row_tile(rows, cols)

    def body(pos_ref, s_ref, r_ref, o_ref):
        o_ref[0] = ((s_ref[0] + r_ref[0].astype(F32)) + r_ref[1].astype(F32)) + r_ref[2].astype(F32)

    return _pcall(
        body, name=name, grid=(rows // tr,),
        in_specs=[own(tr), pl.BlockSpec((3, tr, cols), lambda i, pos: (0, i, 0))],
        out_specs=pl.BlockSpec((1, tr, cols), lambda i, pos: (pos[2], i, 0)),
        out_shape=jax.ShapeDtypeStruct((2, rows, cols), F32), args=(s32, recv), prefetch=pos, comm=comm)


def _position():
    return lax.axis_index("x"), lax.axis_index("y"), lax.axis_index("c")


def _allgather8(x_shard, *, name, comm=None):
    m_per, n = x_shard.shape
    nci, nco = (0, 0) if comm is None else (len(comm.inputs), len(comm.out_shapes))

    def body(*refs):
        x_ref, refs = refs[0], refs[1:]
        cin, refs = refs[:nci], refs[nci:]
        out_ref, refs = refs[0], refs[1:]
        cout, refs = refs[:nco], refs[nco:]
        (send_sems, recv_sems, local_sem), csems = refs[:3], refs[3:]
        x, y, c = _position()
        me, sibling = (x, y, c), (x, y, 1 - c)
        chips = [(1 - x, y), (x, 1 - y), (1 - x, 1 - y)]

        def rows(px, py, pc):
            return out_ref.at[pl.ds((4 * px + 2 * py + pc) * m_per, m_per), :]

        def copy(k, block, to, src=None):
            return pltpu.make_async_remote_copy(
                src_ref=rows(*block) if src is None else src, dst_ref=rows(*block),
                send_sem=send_sems.at[k], recv_sem=recv_sems.at[k], device_id=to, device_id_type=MESH)

        mine = pltpu.make_async_copy(x_ref, rows(*me), local_sem)
        mine.start()
        first = [copy(0, me, sibling, src=x_ref)]
        first += [copy(1 + j, me, (*chip, c), src=x_ref) for j, chip in enumerate(chips)]
        for cp in first:
            cp.start()
        if comm is not None:
            comm.start(cin, cout, csems)
        passed = [copy(4 + j, (*chip, c), sibling) for j, chip in enumerate(chips)]
        for j, chip in enumerate(chips):
            copy(1 + j, (*chip, c), me).wait_recv()
            passed[j].start()
        copy(0, sibling, me).wait_recv()
        for j, chip in enumerate(chips):
            copy(4 + j, (*chip, 1 - c), me).wait_recv()
        for cp in first + passed:
            cp.wait_send()
        mine.wait()
        if comm is not None:
            comm.middle(cin, cout, csems)
            comm.late(cin, cout, csems)
            comm.finish(cin, cout, csems)

    vmem = pl.BlockSpec(memory_space=pltpu.VMEM)
    sems = [pltpu.SemaphoreType.DMA((7,)), pltpu.SemaphoreType.DMA((7,)), pltpu.SemaphoreType.DMA]
    out = jax.ShapeDtypeStruct((N_DEV * m_per, n), x_shard.dtype)
    if comm is None:
        return pl.pallas_call(body, name=name, out_shape=out, in_specs=[vmem], out_specs=vmem,
                              scratch_shapes=sems)(x_shard)
    res = pl.pallas_call(
        body, name=name, out_shape=[out] + list(comm.out_shapes), in_specs=[vmem] + [ANY_SPEC] * nci,
        out_specs=[vmem] + [ANY_SPEC] * nco, scratch_shapes=sems + list(comm.sems),
        input_output_aliases={1 + i: 1 + o for i, o in comm.aliases.items()})(x_shard, *comm.inputs)
    return res[0], list(res[1:])


def _peer_chips(x, y):
    return [(1 - x, y), (x, 1 - y), (1 - x, 1 - y)]


class _GatherJob:
    def __init__(self, pieces):
        self.pieces = pieces
        n_p = len(pieces)
        self.inputs = [p[0] for p in pieces]
        self.out_shapes = [jax.ShapeDtypeStruct(p[0].shape, p[0].dtype) for p in pieces]
        for buf, col_kind, r0, nr in pieces:
            half_rows = buf.shape[0] // (2 if col_kind else 2 * N_CHIPS)
            assert r0 % 16 == 0 and nr % 16 == 0 and nr >= 32 and r0 + nr <= half_rows, (buf.shape, r0, nr)
        self.aliases = {p: p for p in range(n_p)}
        dma = pltpu.SemaphoreType.DMA
        self.sems = [dma((2 * n_p,))] * 4 + [dma((4 * n_p,))] * 2

    def _region(self, cout, p, chip_idx, half, part=None):
        buf, col_kind, r0, nr = self.pieces[p]
        first = -(-nr // 32) * 16
        if part == 0:
            nr = first
        elif part == 1:
            r0, nr = r0 + first, nr - first
        if col_kind:
            n = buf.shape[1] // N_CHIPS
            return cout[p].at[pl.ds(half * (buf.shape[0] // 2) + r0, nr), pl.ds(chip_idx * n, n)]
        n = buf.shape[0] // N_CHIPS
        return cout[p].at[pl.ds(chip_idx * n + half * (n // 2) + r0, nr), :]

    def _copies(self, cout, sems):
        send1, recv1, send2, recv2, fsend, frecv = sems
        x, y, c = _position()
        k = 2 * x + y
        sibling = (x, y, 1 - c)
        x_nbr, y_nbr, diag = _peer_chips(x, y)
        chip_of = lambda ch: 2 * ch[0] + ch[1]

        def remote(region, ssem, rsem, to):
            return pltpu.make_async_remote_copy(src_ref=region, dst_ref=region, send_sem=ssem, recv_sem=rsem,
                                                device_id=to, device_id_type=MESH)

        hop1, arrived1, hop2, arrived2, fwds, fwd_arrived = [], [], [], [], [], []
        for p in range(len(self.pieces)):
            for j, nbr in enumerate((x_nbr, y_nbr)):
                i1 = 2 * p + j
                hop1.append(remote(self._region(cout, p, k, c), send1.at[i1], recv1.at[i1], (*nbr, c)))
                arrived1.append(remote(self._region(cout, p, chip_of(nbr), c), send1.at[i1], recv1.at[i1], (*nbr, c)))
            hop2.append(remote(self._region(cout, p, chip_of(x_nbr), c, 0), send2.at[2 * p], recv2.at[2 * p],
                               (*y_nbr, c)))
            hop2.append(remote(self._region(cout, p, chip_of(y_nbr), c, 1), send2.at[2 * p + 1], recv2.at[2 * p + 1],
                               (*x_nbr, c)))
            arrived2.append(remote(self._region(cout, p, chip_of(diag), c, 0), send2.at[2 * p], recv2.at[2 * p],
                                   (*y_nbr, c)))
            arrived2.append(remote(self._region(cout, p, chip_of(diag), c, 1), send2.at[2 * p + 1],
                                   recv2.at[2 * p + 1], (*x_nbr, c)))
            landed = [(chip_of(x_nbr), None), (chip_of(y_nbr), None), (chip_of(diag), 0), (chip_of(diag), 1)]
            for q, (chip_idx, part) in enumerate(landed):
                i3 = 4 * p + q
                fwds.append(remote(self._region(cout, p, chip_idx, c, part), fsend.at[i3], frecv.at[i3], sibling))
                fwd_arrived.append(remote(self._region(cout, p, chip_idx, 1 - c, part), fsend.at[i3], frecv.at[i3],
                                          sibling))
        return hop1, arrived1, hop2, arrived2, fwds, fwd_arrived

    def start(self, cin, cout, sems):
        for cp in self._copies(cout, sems)[0]:
            cp.start()

    def middle(self, cin, cout, sems):
        _, arrived1, hop2, _, fwds, _ = self._copies(cout, sems)
        for p in range(len(self.pieces)):
            for j in range(2):
                arrived1[2 * p + j].wait_recv()
                hop2[2 * p + j].start()
                fwds[4 * p + j].start()

    def late(self, cin, cout, sems):
        _, _, _, arrived2, fwds, _ = self._copies(cout, sems)
        for p in range(len(self.pieces)):
            for j in range(2):
                arrived2[2 * p + j].wait_recv()
                fwds[4 * p + 2 + j].start()

    def finish(self, cin, cout, sems):
        hop1, _, hop2, _, fwds, fwd_arrived = self._copies(cout, sems)
        for cp in fwd_arrived:
            cp.wait_recv()
        for cp in hop1 + hop2 + fwds:
            cp.wait_send()


class _PairedJob:
    aliases = {}

    def start(self, cin, cout, sems):
        for cp in self._copies(cin, cout, sems):
            cp.start()

    def middle(self, cin, cout, sems):
        pass

    late = middle

    def finish(self, cin, cout, sems):
        copies = self._copies(cin, cout, sems)
        for cp in copies:
            cp.wait_recv()
        for cp in copies:
            cp.wait_send()


class _SwapJob(_PairedJob):
    def __init__(self, grads, kinds):
        self.inputs, self.kinds = list(grads), list(kinds)
        self.out_shapes, self.n_copies = [], []
        for g, kd in zip(grads, kinds):
            if kd:
                self.out_shapes.append(jax.ShapeDtypeStruct((1, g.shape[0] // 2, g.shape[1]), g.dtype))
                self.n_copies.append(1)
            else:
                n = g.shape[0] // N_CHIPS
                self.out_shapes.append(jax.ShapeDtypeStruct((N_CHIPS, n // 2, g.shape[1]), g.dtype))
                self.n_copies.append(N_CHIPS)
        total = sum(self.n_copies)
        self.sems = [pltpu.SemaphoreType.DMA((total,)), pltpu.SemaphoreType.DMA((total,))]

    def _copies(self, cin, cout, sems):
        send_sems, recv_sems = sems
        x, y, c = _position()
        copies = []
        for p, src_ref in enumerate(cin):
            for kk in range(self.n_copies[p]):
                if self.kinds[p]:
                    hr = src_ref.shape[0] // 2
                    src = src_ref.at[pl.ds((1 - c) * hr, hr), :]
                else:
                    n = src_ref.shape[0] // N_CHIPS
                    src = src_ref.at[pl.ds(kk * n + (1 - c) * (n // 2), n // 2), :]
                idx = len(copies)
                copies.append(pltpu.make_async_remote_copy(
                    src_ref=src, dst_ref=cout[p].at[kk], send_sem=send_sems.at[idx], recv_sem=recv_sems.at[idx],
                    device_id=(x, y, 1 - c), device_id_type=MESH))
        return copies


class _ExchangeJob(_PairedJob):
    def __init__(self, s16, kinds, sizes):
        self.inputs, self.kinds, self.sizes = list(s16), list(kinds), list(sizes)
        self.out_shapes = [jax.ShapeDtypeStruct((3, s.shape[1], n if kd else s.shape[2]), s.dtype)
                           for s, kd, n in zip(s16, kinds, sizes)]
        self.sems = [pltpu.SemaphoreType.DMA((3 * len(s16),)), pltpu.SemaphoreType.DMA((3 * len(s16),))]

    def _copies(self, cin, cout, sems):
        send_sems, recv_sems = sems
        x, y, c = _position()
        copies = []
        for p, src_ref in enumerate(cin):
            for j, chip in enumerate(_peer_chips(x, y)):
                kk = 2 * chip[0] + chip[1]
                n = self.sizes[p]
                src = src_ref.at[0, :, pl.ds(kk * n, n)] if self.kinds[p] else src_ref.at[kk]
                copies.append(pltpu.make_async_remote_copy(
                    src_ref=src, dst_ref=cout[p].at[j], send_sem=send_sems.at[3 * p + j],
                    recv_sem=recv_sems.at[3 * p + j], device_id=(*chip, c), device_id_type=MESH))
        return copies


class _ShareJob:
    def __init__(self, halves):
        self.inputs = list(halves)
        self.out_shapes = [jax.ShapeDtypeStruct(h.shape, h.dtype) for h in halves]
        self.aliases = {p: p for p in range(len(halves))}
        self.sems = [pltpu.SemaphoreType.DMA((len(halves),)), pltpu.SemaphoreType.DMA((len(halves),))]

    def _copies(self, cout, sems, half):
        send_sems, recv_sems = sems
        x, y, c = _position()
        h = c if half == "mine" else 1 - c
        return [pltpu.make_async_remote_copy(
            src_ref=o.at[h], dst_ref=o.at[h], send_sem=send_sems.at[p], recv_sem=recv_sems.at[p],
            device_id=(x, y, 1 - c), device_id_type=MESH) for p, o in enumerate(cout)]

    def start(self, cin, cout, sems):
        for cp in self._copies(cout, sems, "mine"):
            cp.start()

    def middle(self, cin, cout, sems):
        pass

    late = middle

    def finish(self, cin, cout, sems):
        for cp in self._copies(cout, sems, "theirs"):
            cp.wait_recv()
        for cp in self._copies(cout, sems, "mine"):
            cp.wait_send()


class _MultiJob:
    def __init__(self, jobs):
        self.jobs = jobs
        self.inputs = [a for j in jobs for a in j.inputs]
        self.out_shapes = [s for j in jobs for s in j.out_shapes]
        self.sems = [s for j in jobs for s in j.sems]
        self.aliases = {}
        i0 = o0 = 0
        for j in jobs:
            for i, o in j.aliases.items():
                self.aliases[i0 + i] = o0 + o
            i0 += len(j.inputs)
            o0 += len(j.out_shapes)

    def _parts(self, cin, cout, sems):
        i0 = o0 = s0 = 0
        for j in self.jobs:
            ni, no, ns = len(j.inputs), len(j.out_shapes), len(j.sems)
            yield j, cin[i0:i0 + ni], cout[o0:o0 + no], sems[s0:s0 + ns]
            i0, o0, s0 = i0 + ni, o0 + no, s0 + ns

    def start(self, cin, cout, sems):
        for j, a, b, s in self._parts(cin, cout, sems):
            j.start(a, b, s)

    def middle(self, cin, cout, sems):
        for j, a, b, s in self._parts(cin, cout, sems):
            j.middle(a, b, s)

    def late(self, cin, cout, sems):
        for j, a, b, s in self._parts(cin, cout, sems):
            j.late(a, b, s)

    def finish(self, cin, cout, sems):
        for j, a, b, s in self._parts(cin, cout, sems):
            j.finish(a, b, s)


def _rope_tables(positions):
    half = ROT_DIM // 2
    inv_freq = jnp.power(jnp.float32(ROPE_THETA), -jnp.arange(0, ROT_DIM, 2, dtype=F32) / ROT_DIM)
    inv_head = jnp.concatenate([inv_freq, inv_freq, jnp.zeros((HEAD_DIM - ROT_DIM,), F32)])
    inv_lane = jnp.concatenate([inv_head] * (LANE // HEAD_DIM))
    ang = positions.astype(F32).reshape(-1)[:, None] * inv_lane[None, :]
    sin = jnp.sin(ang)
    dim = jnp.arange(LANE) % HEAD_DIM
    return jnp.cos(ang), jnp.where(dim < half, -sin, 0.0), jnp.where(dim >= half, sin, 0.0)


def kernel(x, c, positions, w_ada, b_ada, ffn1_w_gate_up, ffn1_w_down, ln1_g, ln1_b, w_in, conv_w, attn_sinks, w_out, ln2_g, ln2_b, ffn2_w_gate_up, ffn2_w_down, ln3_g, ln3_b, loss_target, m_w_ada, m_b_ada, m_ffn1_w_gate_up, m_ffn1_w_down, m_ln1_g, m_ln1_b, m_w_in, m_conv_w, m_attn_sinks, m_w_out, m_ln2_g, m_ln2_b, m_ffn2_w_gate_up, m_ffn2_w_down, m_ln3_g, m_ln3_b, v_w_ada, v_b_ada, v_ffn1_w_gate_up, v_ffn1_w_down, v_ln1_g, v_ln1_b, v_w_in, v_conv_w, v_attn_sinks, v_w_out, v_ln2_g, v_ln2_b, v_ffn2_w_gate_up, v_ffn2_w_down, v_ln3_g, v_ln3_b):
    d = D_MODEL
    nb, seq, _ = x.shape
    t = nb * seq
    f = ffn1_w_down.shape[1] * N_CHIPS
    ax, ay, ac = _position()
    chip = 2 * ax + ay
    dev = 2 * chip + ac
    pos = jnp.stack([ax, ay, ac]).astype(jnp.int32)

    x2 = x.reshape(t, d)
    tgt2 = loss_target.reshape(t, d)
    ln1 = jnp.concatenate([ln1_g, ln1_b], axis=0)
    ln2 = jnp.concatenate([ln2_g, ln2_b], axis=0)
    ln3 = jnp.concatenate([ln3_g, ln3_b], axis=0)
    sinks = attn_sinks.reshape(N_Q_HEADS)
    cos_t, sa_t, sb_t = _rope_tables(positions)

    gu_cuts = [0, 176, 352, d // 2]
    gu_part = lambda buf, s: (buf, True, gu_cuts[s], gu_cuts[s + 1] - gu_cuts[s])
    chip_arr = jnp.reshape(chip, (1,)).astype(jnp.int32)
    b_gu1 = _cast_into(ffn1_w_gate_up[0], chip_arr, True, name="cast_gu1")

    n_ada = w_ada.shape[2]
    half_rows = d // 4
    c_all = _allgather8(c.reshape(nb * d // LANE, LANE), name="gather_c").reshape(N_DEV * nb, d)
    b_shard = lax.dynamic_slice(b_ada, (0, chip * n_ada), (1, n_ada))
    mod_part, (b_gu1,) = _ada_fwd(c_all, w_ada[0], b_shard, name="ada_fwd",
                                  comm=_GatherJob([(b_gu1, True, 0, half_rows)]))
    conv_rows = jnp.pad(conv_w[0], ((0, 5), (0, n_ada - conv_w.shape[2])))
    part = jnp.concatenate([mod_part, conv_rows], axis=0)
    parts, (wgu1,) = _allgather8(part, name="gather_mod", comm=_GatherJob([(b_gu1, True, half_rows, half_rows)]))
    parts = parts.reshape(N_DEV, N_DEV * nb + 8, n_ada)
    mod_all = jnp.concatenate([parts[2 * k, :N_DEV * nb, :] for k in range(N_CHIPS)], axis=1)
    mod = lax.dynamic_slice(mod_all, (dev * nb, 0), (nb, N_MOD * d)).reshape(nb, N_MOD, d)
    cw_full = jnp.concatenate([parts[2 * k, N_DEV * nb:, :conv_w.shape[2]] for k in range(N_CHIPS)], axis=1)

    b_d1 = _cast_into(ffn1_w_down[0], chip_arr, False, name="cast_d1")
    b_in = _cast_into(w_in[0].T, chip_arr, False, name="cast_in")
    b_out = _cast_into(w_out[0], chip_arr, False, name="cast_out")
    b_gu2 = _cast_into(ffn2_w_gate_up[0], chip_arr, True, name="cast_gu2")
    b_d2 = _cast_into(ffn2_w_down[0], chip_arr, False, name="cast_d2")
    n_gu, n_d, n_in, n_out = (ffn1_w_gate_up.shape[2], ffn1_w_down.shape[1], w_in.shape[2], w_out.shape[1])

    def whole(buf, col_kind):
        return (buf, col_kind, 0, buf.shape[0] // (2 if col_kind else 2 * N_CHIPS))

    (h1, a1, dact1), (wd1, wout) = _ffn_up(x2, ln1, mod, wgu1, seq=seq, sc_idx=1, sh_idx=0, use_ln=False,
                                         name="ffn1_up", comm=_GatherJob([whole(b_d1, False), whole(b_out, False)]))
    (f1, xhat1, rstd1), (win_t,) = _ffn_down_ln(a1, wd1, x2, ln1, mod, seq=seq, gate_idx=2, use_ln=False,
                                                name="ffn1_down", comm=_GatherJob([whole(b_in, False)]))
    (h2, q, k, v, ubc), (b_gu2,) = _in_proj(
        xhat1, ln1, mod, win_t, cos_t, sa_t, sb_t, seq=seq, sc_idx=4, sh_idx=3, name="in_proj",
        comm=_GatherJob([gu_part(b_gu2, 0)]))
    attn, (b_gu2,) = _attention(q, k, v, sinks, seq=seq, name="attention", comm=_GatherJob([gu_part(b_gu2, 1)]))
    (mixin, mix, xhat2, rstd2), (wgu2,) = _out_proj(
        attn, ubc, cw_full, wout, xhat1, ln1, mod, seq=seq, gate_idx=5, name="out_proj",
        comm=_GatherJob([gu_part(b_gu2, 2)]))
    (h3, a3, dact3), (wd2,) = _ffn_up(xhat2, ln2, mod, wgu2, seq=seq, sc_idx=7, sh_idx=6, use_ln=True, name="ffn2_up",
                                    comm=_GatherJob([whole(b_d2, False)]))
    dr3, df3, loss_cols, dln3g, dln3b, dgate3 = _ffn_down_loss(
        a3, wd2, xhat2, ln2, mod, ln3, tgt2, seq=seq, gate_idx=8, name="ffn2_down_loss")

    def pair_sum(g, r3, col_kind, name_, comm=None):
        if col_kind:
            g3 = g.reshape(2, g.shape[0] // 2, g.shape[1])
            blk_of = lambda p_, pos_: pos_[2]
        else:
            g3 = g.reshape(2 * N_CHIPS, g.shape[0] // (2 * N_CHIPS), g.shape[1])
            blk_of = lambda p_, pos_: 2 * p_ + pos_[2]
        return _sum_pair(pos, g3, r3, blk_of, name=name_, comm=comm)

    dgu3 = _ffn_bwd_act(df3, wd2, dact3, seq=seq, name="ffn2_bwd_act")
    g_wd2 = _matmul_tn(a3, df3, tmm=f // 2, tnn=d, name="grad_wd2")
    (s32_gu2, s16_gu2), (sib_d2,) = _grad_chip_sum(pos, h3, dgu3, name="grad_wgu2", comm=_SwapJob([g_wd2], [False]))
    s32_d2, s16_d2 = pair_sum(g_wd2, sib_d2, False, "sum_pair_d2")
    (dr2, dmix, dsc3, dsh3, dgate2, dln2g, dln2b), (recv_gu2,) = _bwd_in(
        dgu3, wgu2, dr3, xhat2, rstd2, ln2, mod, mix, seq=seq, w_is_nt=True, sc_idx=7, gate_idx=5,
        branch_scale=1.0, final=False, name="ffn2_bwd_in", comm=_ExchangeJob([s16_gu2], [True], [n_gu]))
    g_wout = _matmul_tn(mixin, dmix, tmm=d, tnn=d, name="grad_wout")
    dmixin = _matmul_nt_bf16(dmix, wout, seq=seq, name="out_proj_bwd")
    (dq, dkp, dkc, dvp, dvc, dsink), (recv_d2, sib_out) = _attention_bwd(
        q, k, v, dmixin, sinks, seq=seq, name="attention_bwd",
        comm=_MultiJob([_ExchangeJob([s16_d2], [False], [n_d]), _SwapJob([g_wout], [False])]))
    s32_out, s16_out = pair_sum(g_wout, sib_out, False, "sum_pair_out")
    (dproj, dcw), (recv_out,) = _mix_bwd_assemble(
        dq, dkp, dkc, dvp, dvc, cos_t, sa_t, sb_t, dmixin, ubc, cw_full, seq=seq, name="mix_bwd",
        comm=_ExchangeJob([s16_out], [False], [n_out]))
    g_win_t = _matmul_tn(dproj, h2, tmm=IN_WIDTH // 2, tnn=d, name="grad_win")
    (dr1, df1, dsc2, dsh2, dgate1, dln1g, dln1b), (sib_in,) = _bwd_in(
        dproj, win_t, dr2, xhat1, rstd1, ln1, mod, f1, seq=seq, w_is_nt=False, sc_idx=4, gate_idx=2,
        branch_scale=0.5, final=False, name="in_proj_bwd", comm=_SwapJob([g_win_t], [False]))
    s32_in, s16_in = pair_sum(g_win_t, sib_in, False, "sum_pair_in")
    g_wd1, (recv_in,) = _matmul_tn(a1, df1, tmm=f // 2, tnn=d, name="grad_wd1",
                                   comm=_ExchangeJob([s16_in], [False], [n_in]))
    dgu1, (sib_d1,) = _ffn_bwd_act(df1, wd1, dact1, seq=seq, name="ffn1_bwd_act", comm=_SwapJob([g_wd1], [False]))
    s32_d1, s16_d1 = pair_sum(g_wd1, sib_d1, False, "sum_pair_d1")
    (s32_gu1, s16_gu1), (recv_d1,) = _grad_chip_sum(pos, h1, dgu1, name="grad_wgu1",
                                                    comm=_ExchangeJob([s16_d1], [False], [n_d]))

    def final_half(s32_, recv_, col_kind, n_shard, name_):
        return _sum_final(pos, s32_, recv_, col_kind=col_kind, n_shard=n_shard, name=name_)

    early = [final_half(s32_gu2, recv_gu2, True, n_gu, "sum_final_gu2"),
             final_half(s32_d2, recv_d2, False, n_d, "sum_final_d2"),
             final_half(s32_out, recv_out, False, n_out, "sum_final_out"),
             final_half(s32_in, recv_in, False, n_in, "sum_final_in"),
             final_half(s32_d1, recv_d1, False, n_d, "sum_final_d1")]
    (grad_x, dsc1, dsh1), (recv_gu1, full_gu2, full_d2, full_out, full_in, full_d1) = _bwd_in(
        dgu1, wgu1, dr1, x2, None, None, mod, None, seq=seq, w_is_nt=True, sc_idx=1, gate_idx=None,
        branch_scale=None, final=True, name="ffn1_bwd_in",
        comm=_MultiJob([_ExchangeJob([s16_gu1], [True], [n_gu]), _ShareJob(early)]))
    late = [final_half(s32_gu1, recv_gu1, True, n_gu, "sum_final_gu1")]

    dmod = jnp.concatenate([dsh1, dsc1, dgate1, dsh2, dsc2, dgate2, dsh3, dsc3, dgate3], axis=1)
    loss_row = jnp.sum(loss_cols, axis=1, keepdims=True) * (0.5 / d)
    lane_row = lambda a: jnp.pad(a, ((0, 0), (0, d - a.shape[1])))
    block = jnp.concatenate(
        [dmod.reshape(nb * N_MOD, d), dln1g, dln1b, dln2g, dln2b, dln3g, dln3b,
         lane_row(dcw[0:3, :]), lane_row(dsink[:, 0:1].reshape(1, N_Q_HEADS)), lane_row(loss_row)], axis=0)
    block = jnp.pad(block, ((0, SMALL_ROWS - block.shape[0]), (0, 0)))
    gathered, (full_gu1,) = _allgather8(block, name="gather_small", comm=_ShareJob(late))
    gathered = gathered.reshape(N_DEV, SMALL_ROWS, d)
    dmod_all = gathered[:, :nb * N_MOD, :].reshape(N_DEV * nb, N_MOD * d)
    dmod_shard = lax.dynamic_slice(dmod_all, (0, chip * n_ada), (N_DEV * nb, n_ada))
    small, g_w_ada, g_b_ada = _small_finish(gathered, dmod_all, dmod_shard, c_all.T, name="small_finish")
    r0 = nb * N_MOD
    loss = small[r0 + 10, 0]
    g_ln = [small[r0 + i:r0 + i + 1, :] for i in range(6)]
    g_cw_full = small[r0 + 6:r0 + 9, :CONV_WIDTH]
    g_conv = lax.dynamic_slice(g_cw_full, (0, chip * conv_w.shape[2]), (3, conv_w.shape[2]))
    g_sinks = small[r0 + 9:r0 + 10, :N_Q_HEADS]

    def flat2(a):
        return a.reshape(-1, a.shape[-1])

    def unhalve(a):
        return a.reshape(2 * a.shape[1], a.shape[2])

    results = {}

    def adamw(name_, w_, g_, m_, v_):
        g2 = flat2(g_)
        dl, nm, nv = _adamw(flat2(w_), g2, flat2(m_), flat2(v_), name="adamw_" + name_)
        results[name_] = tuple(a.reshape(w_.shape) for a in (g2, dl, nm, nv))

    adamw("w_ada", w_ada, g_w_ada, m_w_ada, v_w_ada)
    adamw("ffn2_w_gate_up", ffn2_w_gate_up, unhalve(full_gu2), m_ffn2_w_gate_up, v_ffn2_w_gate_up)
    adamw("ffn2_w_down", ffn2_w_down, unhalve(full_d2), m_ffn2_w_down, v_ffn2_w_down)
    adamw("w_out", w_out, unhalve(full_out), m_w_out, v_w_out)
    adamw("w_in", w_in, unhalve(full_in).T, m_w_in, v_w_in)
    adamw("ffn1_w_gate_up", ffn1_w_gate_up, unhalve(full_gu1), m_ffn1_w_gate_up, v_ffn1_w_gate_up)
    adamw("ffn1_w_down", ffn1_w_down, unhalve(full_d1), m_ffn1_w_down, v_ffn1_w_down)
    adamw("b_ada", b_ada, g_b_ada, m_b_ada, v_b_ada)
    adamw("ln1_g", ln1_g, g_ln[0], m_ln1_g, v_ln1_g)
    adamw("ln1_b", ln1_b, g_ln[1], m_ln1_b, v_ln1_b)
    adamw("ln2_g", ln2_g, g_ln[2], m_ln2_g, v_ln2_g)
    adamw("ln2_b", ln2_b, g_ln[3], m_ln2_b, v_ln2_b)
    adamw("ln3_g", ln3_g, g_ln[4], m_ln3_g, v_ln3_g)
    adamw("ln3_b", ln3_b, g_ln[5], m_ln3_b, v_ln3_b)
    adamw("conv_w", conv_w, g_conv, m_conv_w, v_conv_w)
    adamw("attn_sinks", attn_sinks, g_sinks, m_attn_sinks, v_attn_sinks)
    order = ["w_ada", "b_ada", "ffn1_w_gate_up", "ffn1_w_down", "ln1_g", "ln1_b", "w_in", "conv_w", "attn_sinks",
             "w_out", "ln2_g", "ln2_b", "ffn2_w_gate_up", "ffn2_w_down", "ln3_g", "ln3_b"]
    return (loss, grad_x.reshape(x.shape), *[results[n_][0] for n_ in order], *[results[n_][1] for n_ in order],
            *[results[n_][2] for n_ in order], *[results[n_][3] for n_ in order])
```

```python
import jax
import jax.numpy as jnp
from jax import lax
from jax.experimental import pallas as pl
from jax.experimental.pallas import tpu as pltpu

F32 = jnp.float32
BF16 = jnp.bfloat16
MESH = pl.DeviceIdType.MESH

D_MODEL = 1024
HEAD_DIM = 64
ATTN_WIDTH = 512
CONV_WIDTH = 512
N_Q_HEADS = 8
N_KV_HEADS = 2
GQA_GROUP = 4
KV_WIDTH = 128
WINDOW = 128
BLOCK = 128
ROT_DIM = 16
ROPE_THETA = 500000.0
N_MOD = 9
LN_EPS = 1e-5
DN_ALPHA = 2.0 ** 0.25
IN_WIDTH = 2304
N_CHIPS = 4
N_DEV = 8
SMALL_ROWS = 32

ADAM_LR = 0.001
ADAM_B1 = 0.9
ADAM_B2 = 0.999
ADAM_EPS = 1e-08
ADAM_WD = 0.01
ADAM_STEP = 10

LANE = 128
HALO = 16
COL_CHUNK = 256
VMEM_LIMIT = 56 * 1024 * 1024


def _params(sem=None, vmem=True):
    return pltpu.CompilerParams(dimension_semantics=sem, vmem_limit_bytes=VMEM_LIMIT if vmem else None)


def _sigmoid(g):
    return 0.5 * jnp.tanh(0.5 * g) + 0.5


def _row_sum(v):
    return jnp.sum(v, axis=0, keepdims=True)


ROW_CHUNK = 16
EPILOGUE_UNROLL = 8


def _fold8(v):
    return v[0:8, :] + v[8:16, :]


def _row_chunk_loop(n_rows, step, init):
    per_iter = ROW_CHUNK * EPILOGUE_UNROLL
    assert n_rows % per_iter == 0, n_rows

    def body(it, carry):
        for s in range(EPILOGUE_UNROLL):
            start = pl.multiple_of(it * per_iter + s * ROW_CHUNK, ROW_CHUNK)
            carry = step(pl.ds(start, ROW_CHUNK), carry)
        return carry

    return lax.fori_loop(0, n_rows // per_iter, body, init)


def _ln_stats(r):
    mu = jnp.mean(r, axis=-1, keepdims=True)
    rc = r - mu
    var = jnp.mean(rc * rc, axis=-1, keepdims=True)
    rstd = lax.rsqrt(var + LN_EPS)
    return rc * rstd, rstd


def _ln_bwd(dxo, xhat, rstd, g):
    dxhat = dxo * g
    m1 = jnp.mean(dxhat, axis=-1, keepdims=True)
    m2 = jnp.mean(dxhat * xhat, axis=-1, keepdims=True)
    return rstd * (dxhat - m1 - xhat * m2)


def _dot_nt(a, b):
    return lax.dot_general(a, b, (((1,), (1,)), ((), ())), preferred_element_type=F32)


def _dot_tn(a, b):
    return lax.dot_general(a, b, (((0,), (0,)), ((), ())), preferred_element_type=F32)


def _full(shape):
    nd = len(shape)
    return pl.BlockSpec(shape, lambda *_: (0,) * nd)


def _resident(shape):
    nd = len(shape)
    return pl.BlockSpec(shape, lambda *_: (0,) * nd, pipeline_mode=pl.Buffered(1))


ANY_SPEC = pl.BlockSpec(memory_space=pl.ANY)


def _pcall(body, *, name, grid, in_specs, out_specs, out_shape, args, scratch_shapes=(), comm=None, prefetch=None):
    single = not isinstance(out_shape, (list, tuple))
    out_specs = [out_specs] if single else list(out_specs)
    out_shape = [out_shape] if single else list(out_shape)
    in_specs = list(in_specs)
    scratch_shapes = list(scratch_shapes)
    sem = ("arbitrary",) * len(grid)
    n_pre = 0 if prefetch is None else 1
    pre_args = () if prefetch is None else (prefetch,)

    def call(fn, ins_, outs_, shapes_, scratch_, aliases_, operands):
        if prefetch is None:
            return pl.pallas_call(fn, name=name, grid=grid, in_specs=ins_, out_specs=outs_, out_shape=shapes_,
                                  scratch_shapes=scratch_, input_output_aliases=aliases_,
                                  compiler_params=_params(sem))(*operands)
        spec = pltpu.PrefetchScalarGridSpec(num_scalar_prefetch=1, grid=grid, in_specs=ins_, out_specs=outs_,
                                            scratch_shapes=scratch_)
        return pl.pallas_call(fn, name=name, grid_spec=spec, out_shape=shapes_,
                              input_output_aliases={n_pre + i: o for i, o in aliases_.items()},
                              compiler_params=_params(sem))(*pre_args, *operands)

    if comm is None:
        res = call(body, in_specs, out_specs, out_shape, scratch_shapes, {}, args)
        return res[0] if single else res
    n_in, n_out, n_scr = len(in_specs), len(out_specs), len(scratch_shapes)
    nci, nco = len(comm.inputs), len(comm.out_shapes)
    n_steps = 1
    for g in grid:
        n_steps *= g
    staged = n_steps >= 8
    middle_step = (n_steps * 5) // 8 - 1
    late_step = n_steps - 1 - max(1, n_steps // 8)

    def wrapped(*refs):
        pre, refs = refs[:n_pre], refs[n_pre:]
        ins, refs = refs[:n_in], refs[n_in:]
        cin, refs = refs[:nci], refs[nci:]
        outs, refs = refs[:n_out], refs[n_out:]
        cout, refs = refs[:nco], refs[nco:]
        scr, csems = refs[:n_scr], refs[n_scr:]
        step = pl.program_id(0)
        for ax in range(1, len(grid)):
            step = step * grid[ax] + pl.program_id(ax)

        @pl.when(step == 0)
        def _():
            comm.start(cin, cout, csems)

        body(*pre, *ins, *outs, *scr)

        if staged:
            @pl.when(step == middle_step)
            def _():
                comm.middle(cin, cout, csems)

            @pl.when(step == late_step)
            def _():
                comm.late(cin, cout, csems)

        @pl.when(step == n_steps - 1)
        def _():
            if not staged:
                comm.middle(cin, cout, csems)
                comm.late(cin, cout, csems)
            comm.finish(cin, cout, csems)

    res = call(wrapped, in_specs + [ANY_SPEC] * nci, out_specs + [ANY_SPEC] * nco,
               out_shape + list(comm.out_shapes), scratch_shapes + list(comm.sems),
               {n_in + i: n_out + o for i, o in comm.aliases.items()}, (*args, *comm.inputs))
    main = res[:n_out]
    return (main[0] if single else main), list(res[n_out:])


def _ffn_up(xin, lnp, mod, w, *, seq, sc_idx, sh_idx, use_ln, name, comm=None):
    t, d = xin.shape
    f = w.shape[1] // 2
    tm = min(512, seq)
    tpb = seq // tm
    ch = min(COL_CHUNK, f)

    def body(x_ref, ln_ref, mod_ref, w_ref, h_ref, a_ref, dact_ref):
        x = x_ref[...]
        if use_ln:
            x = x * ln_ref[0:1, :] + ln_ref[1:2, :]
        h = x * (1.0 + mod_ref[0, sc_idx:sc_idx + 1, :]) + mod_ref[0, sh_idx:sh_idx + 1, :]
        hb = h.astype(BF16)
        h_ref[...] = hb
        for j in range(f // ch):
            g = jnp.dot(hb, w_ref[:, j * ch:(j + 1) * ch], preferred_element_type=F32)
            u = jnp.dot(hb, w_ref[:, f + j * ch:f + (j + 1) * ch], preferred_element_type=F32)
            s = _sigmoid(g)
            silu = g * s
            a_ref[:, j * ch:(j + 1) * ch] = (silu * u).astype(BF16)
            dact_ref[:, j * ch:(j + 1) * ch] = (u * (s + silu * (1.0 - s))).astype(BF16)
            dact_ref[:, f + j * ch:f + (j + 1) * ch] = silu.astype(BF16)

    return _pcall(
        body, name=name, grid=(t // tm,),
        in_specs=[pl.BlockSpec((tm, d), lambda i: (i, 0)), _full((2, d)),
                  pl.BlockSpec((1, N_MOD, d), lambda i: (i // tpb, 0, 0)), _resident((d, 2 * f))],
        out_specs=[pl.BlockSpec((tm, d), lambda i: (i, 0)), pl.BlockSpec((tm, f), lambda i: (i, 0)),
                   pl.BlockSpec((tm, 2 * f), lambda i: (i, 0))],
        out_shape=[jax.ShapeDtypeStruct((t, d), BF16), jax.ShapeDtypeStruct((t, f), BF16),
                   jax.ShapeDtypeStruct((t, 2 * f), BF16)],
        args=(xin, lnp, mod, w), comm=comm)


def _ffn_down_ln(a, wd, xin, lnp_in, mod, *, seq, gate_idx, use_ln, name, comm=None):
    t, f = a.shape
    d = wd.shape[1]
    tm = min(512, seq)
    tpb = seq // tm

    def body(a_ref, wd_ref, x_ref, ln_ref, mod_ref, f_ref, xhat_ref, rstd_ref, acc):
        av = a_ref[...]
        for j in range(d // COL_CHUNK):
            acc[:, j * COL_CHUNK:(j + 1) * COL_CHUNK] = jnp.dot(
                av, wd_ref[:, j * COL_CHUNK:(j + 1) * COL_CHUNK], preferred_element_type=F32)
        scale = 0.5 * (1.0 + mod_ref[0, gate_idx:gate_idx + 1, :])

        fo = acc[...]
        x = x_ref[...]
        if use_ln:
            x = x * ln_ref[0:1, :] + ln_ref[1:2, :]
        xhat, rstd = _ln_stats(DN_ALPHA * x + scale * fo)
        f_ref[...] = fo.astype(BF16)
        xhat_ref[...] = xhat
        rstd_ref[...] = rstd

    return _pcall(
        body, name=name, grid=(t // tm,),
        in_specs=[pl.BlockSpec((tm, f), lambda i: (i, 0)), _resident((f, d)),
                  pl.BlockSpec((tm, d), lambda i: (i, 0)), _full((2, d)),
                  pl.BlockSpec((1, N_MOD, d), lambda i: (i // tpb, 0, 0))],
        out_specs=[pl.BlockSpec((tm, d), lambda i: (i, 0)), pl.BlockSpec((tm, d), lambda i: (i, 0)),
                   pl.BlockSpec((tm, 1), lambda i: (i, 0))],
        out_shape=[jax.ShapeDtypeStruct((t, d), BF16), jax.ShapeDtypeStruct((t, d), F32),
                   jax.ShapeDtypeStruct((t, 1), F32)],
        scratch_shapes=[pltpu.VMEM((tm, d), F32)],
        args=(a, wd, xin, lnp_in, mod), comm=comm)


def _ffn_down_loss(a, wd, xhat_in, lnp_in, mod, lnp_out, tgt, *, seq, gate_idx, name):
    t, f = a.shape
    d = wd.shape[1]
    nb = t // seq
    tm = min(512, seq)
    tpb = seq // tm

    def body(a_ref, wd_ref, x_ref, lnin_ref, mod_ref, lnout_ref, tgt_ref,
             dr_ref, df_ref, loss_ref, dg_ref, db_ref, dgate_ref, acc):
        i = pl.program_id(0)
        av = a_ref[...]
        for j in range(d // COL_CHUNK):
            acc[:, j * COL_CHUNK:(j + 1) * COL_CHUNK] = jnp.dot(
                av, wd_ref[:, j * COL_CHUNK:(j + 1) * COL_CHUNK], preferred_element_type=F32)
        scale = 0.5 * (1.0 + mod_ref[0, gate_idx:gate_idx + 1, :])
        ag_in, ab_in = DN_ALPHA * lnin_ref[0:1, :], DN_ALPHA * lnin_ref[1:2, :]
        g_out, b_out = lnout_ref[0:1, :], lnout_ref[1:2, :]
        g_over_d = g_out * (1.0 / d)

        def chunk(rows, carry):
            s_loss, s_dg, s_db, s_gate = carry
            fo = acc[rows, :]
            xhat, rstd = _ln_stats(x_ref[rows, :] * ag_in + ab_in + scale * fo)
            e = xhat * g_out + b_out - tgt_ref[rows, :]
            dr = _ln_bwd(e, xhat, rstd, g_over_d)
            dr_ref[rows, :] = dr
            df_ref[rows, :] = (scale * dr).astype(BF16)
            return s_loss + _fold8(e * e), s_dg + _fold8(e * xhat), s_db + _fold8(e), s_gate + _fold8(fo * dr)

        zero = jnp.zeros((8, d), F32)
        s_loss, s_dg, s_db, s_gate = _row_chunk_loop(tm, chunk, (zero, zero, zero, zero))
        s_dg, s_db, s_gate = s_dg * (1.0 / d), s_db * (1.0 / d), s_gate * 0.5

        @pl.when(i == 0)
        def _():
            loss_ref[...] = jnp.zeros_like(loss_ref)
            dg_ref[...] = jnp.zeros_like(dg_ref)
            db_ref[...] = jnp.zeros_like(db_ref)

        @pl.when(i % tpb == 0)
        def _():
            dgate_ref[...] = jnp.zeros_like(dgate_ref)

        loss_ref[...] += _row_sum(s_loss)
        dg_ref[...] += _row_sum(s_dg)
        db_ref[...] += _row_sum(s_db)
        dgate_ref[0] += _row_sum(s_gate)

    return pl.pallas_call(
        body, name=name, grid=(t // tm,), scratch_shapes=[pltpu.VMEM((tm, d), F32)],
        in_specs=[pl.BlockSpec((tm, f), lambda i: (i, 0)), _resident((f, d)),
                  pl.BlockSpec((tm, d), lambda i: (i, 0)), _full((2, d)),
                  pl.BlockSpec((1, N_MOD, d), lambda i: (i // tpb, 0, 0)), _full((2, d)),
                  pl.BlockSpec((tm, d), lambda i: (i, 0))],
        out_specs=[pl.BlockSpec((tm, d), lambda i: (i, 0)), pl.BlockSpec((tm, d), lambda i: (i, 0)),
                   _full((1, d)), _full((1, d)), _full((1, d)),
                   pl.BlockSpec((1, 1, d), lambda i: (i // tpb, 0, 0))],
        out_shape=[jax.ShapeDtypeStruct((t, d), F32), jax.ShapeDtypeStruct((t, d), BF16),
                   jax.ShapeDtypeStruct((1, d), F32), jax.ShapeDtypeStruct((1, d), F32),
                   jax.ShapeDtypeStruct((1, d), F32), jax.ShapeDtypeStruct((nb, 1, d), F32)],
        compiler_params=_params(("arbitrary",)),
    )(a, wd, xhat_in, lnp_in, mod, lnp_out, tgt)


def _rope(v, cos, sa, sb):
    return v * cos + pltpu.roll(v, LANE - ROT_DIM // 2, 1) * sa + pltpu.roll(v, ROT_DIM // 2, 1) * sb


def _rope_t(dy, cos, sa, sb):
    return dy * cos + pltpu.roll(dy * sa, ROT_DIM // 2, 1) + pltpu.roll(dy * sb, LANE - ROT_DIM // 2, 1)


def _in_proj(xhat, lnp, mod, w_t, cos, sa, sb, *, seq, sc_idx, sh_idx, name, comm=None):
    t, d = xhat.shape
    tm = min(512, seq)
    tpb = seq // tm
    n_conv = 3 * CONV_WIDTH

    def body(x_ref, ln_ref, mod_ref, w_ref, cos_ref, sa_ref, sb_ref, h_ref, q_ref, k_ref, v_ref, ubc_ref):
        x = x_ref[...] * ln_ref[0:1, :] + ln_ref[1:2, :]
        h = x * (1.0 + mod_ref[0, sc_idx:sc_idx + 1, :]) + mod_ref[0, sh_idx:sh_idx + 1, :]
        hb = h.astype(BF16)
        h_ref[...] = hb
        cos_t, sa_t, sb_t = cos_ref[...], sa_ref[...], sb_ref[...]
        for j in range(ATTN_WIDTH // COL_CHUNK):
            p = _dot_nt(hb, w_ref[j * COL_CHUNK:(j + 1) * COL_CHUNK, :])
            for s in range(COL_CHUNK // LANE):
                q_ref[:, j * COL_CHUNK + s * LANE:j * COL_CHUNK + (s + 1) * LANE] = _rope(
                    p[:, s * LANE:(s + 1) * LANE], cos_t, sa_t, sb_t).astype(BF16)
        p = _dot_nt(hb, w_ref[ATTN_WIDTH:ATTN_WIDTH + 2 * KV_WIDTH, :])
        k_ref[...] = _rope(p[:, 0:KV_WIDTH], cos_t, sa_t, sb_t).astype(BF16)
        v_ref[...] = p[:, KV_WIDTH:].astype(BF16)
        base = ATTN_WIDTH + 2 * KV_WIDTH
        for j in range(n_conv // COL_CHUNK):
            ubc_ref[:, j * COL_CHUNK:(j + 1) * COL_CHUNK] = _dot_nt(
                hb, w_ref[base + j * COL_CHUNK:base + (j + 1) * COL_CHUNK, :]).astype(BF16)

    row = lambda w: pl.BlockSpec((tm, w), lambda i: (i, 0))
    return _pcall(
        body, name=name, grid=(t // tm,),
        in_specs=[row(d), _full((2, d)), pl.BlockSpec((1, N_MOD, d), lambda i: (i // tpb, 0, 0)),
                  _resident((IN_WIDTH, d)), row(LANE), row(LANE), row(LANE)],
        out_specs=[row(d), row(ATTN_WIDTH), row(KV_WIDTH), row(KV_WIDTH), row(n_conv)],
        out_shape=[jax.ShapeDtypeStruct((t, d), BF16), jax.ShapeDtypeStruct((t, ATTN_WIDTH), BF16),
                   jax.ShapeDtypeStruct((t, KV_WIDTH), BF16), jax.ShapeDtypeStruct((t, KV_WIDTH), BF16),
                   jax.ShapeDtypeStruct((t, n_conv), BF16)],
        args=(xhat, lnp, mod, w_t, cos, sa, sb), comm=comm)


ATTN_TILE_BLOCKS = 2


def _attn_sub_block(s, tile, nblk, kp_ref, kc_ref, vp_ref, vc_ref):
    rows = slice(s * BLOCK, (s + 1) * BLOCK)
    if s == 0:
        first = ((tile * ATTN_TILE_BLOCKS) % nblk) == 0
        return rows, (kp_ref, slice(0, BLOCK)), (kc_ref, rows), (vp_ref, slice(0, BLOCK)), (vc_ref, rows), first
    before = slice((s - 1) * BLOCK, s * BLOCK)
    return rows, (kc_ref, before), (kc_ref, rows), (vc_ref, before), (vc_ref, rows), False


def _attn_group(q_ref, rows, k_prev, k_cur, v_prev, v_cur, sink_ref, g, first):
    lo, hi = g * HEAD_DIM, (g + 1) * HEAD_DIM
    kk = jnp.concatenate([k_prev[0][k_prev[1], lo:hi], k_cur[0][k_cur[1], lo:hi]], axis=0)
    vv = jnp.concatenate([v_prev[0][v_prev[1], lo:hi], v_cur[0][v_cur[1], lo:hi]], axis=0)
    qs = jnp.concatenate([q_ref[rows, (GQA_GROUP * g + j) * HEAD_DIM:(GQA_GROUP * g + j + 1) * HEAD_DIM]
                          for j in range(GQA_GROUP)], axis=0)
    cols = GQA_GROUP * BLOCK
    ki = lax.broadcasted_iota(jnp.int32, (2 * BLOCK, cols), 0)
    col = lax.broadcasted_iota(jnp.int32, (2 * BLOCK, cols), 1)
    diff = (col & (BLOCK - 1)) + BLOCK - ki
    valid = (diff >= 0) & (diff < WINDOW) & ((ki >= BLOCK) | jnp.logical_not(first))
    s = _dot_nt(kk, qs) * (HEAD_DIM ** -0.5)
    s = jnp.where(valid, s, -1e30)
    hcol = lax.broadcasted_iota(jnp.int32, (1, cols), 1)
    sink = jnp.zeros((1, cols), F32)
    for j in range(GQA_GROUP):
        sink = jnp.where(hcol // BLOCK == j, sink_ref[GQA_GROUP * g + j], sink)
    m = jnp.maximum(jnp.max(s, axis=0, keepdims=True), sink)
    p = jnp.exp(s - m)
    ps = jnp.exp(sink - m)
    inv = 1.0 / (jnp.sum(p, axis=0, keepdims=True) + ps)
    return qs, kk, vv, p * inv, ps * inv


def _heads_to_lanes(x_t):
    return jnp.concatenate([x_t[:, j * BLOCK:(j + 1) * BLOCK].T for j in range(GQA_GROUP)], axis=1)


def _attention(q, k, v, sinks, *, seq, name, comm=None):
    t = q.shape[0]
    nblk = seq // BLOCK
    tile = ATTN_TILE_BLOCKS * BLOCK

    def body(q_ref, kp_ref, kc_ref, vp_ref, vc_ref, sink_ref, o_ref):
        for s in range(ATTN_TILE_BLOCKS):
            rows, k_prev, k_cur, v_prev, v_cur, first = _attn_sub_block(
                s, pl.program_id(0), nblk, kp_ref, kc_ref, vp_ref, vc_ref)
            outs = []
            for g in range(N_KV_HEADS):
                _, _, vv, pn, _ = _attn_group(q_ref, rows, k_prev, k_cur, v_prev, v_cur, sink_ref, g, first)
                outs.append(_heads_to_lanes(_dot_tn(vv, pn.astype(BF16))))
            o_ref[rows, :] = jnp.concatenate(outs, axis=1).astype(BF16)

    cur = lambda w: pl.BlockSpec((tile, w), lambda n: (n, 0))
    prev = lambda w: pl.BlockSpec((BLOCK, w), lambda n: (jnp.maximum(n * ATTN_TILE_BLOCKS - 1, 0), 0))
    return _pcall(
        body, name=name, grid=(t // tile,),
        in_specs=[cur(ATTN_WIDTH), prev(KV_WIDTH), cur(KV_WIDTH), prev(KV_WIDTH), cur(KV_WIDTH),
                  pl.BlockSpec(memory_space=pltpu.SMEM)],
        out_specs=cur(ATTN_WIDTH),
        out_shape=jax.ShapeDtypeStruct((t, ATTN_WIDTH), BF16),
        args=(q, k, k, v, v, sinks), comm=comm)


def _out_proj(attn, ubc, cw, wout, xhat_in, lnp_in, mod, *, seq, gate_idx, name, comm=None):
    t, d = xhat_in.shape
    tm = min(512, seq)
    tpb = seq // tm
    cwid = CONV_WIDTH

    def body(attn_ref, ubc_ref, halo_ref, cw_ref, w_ref, x_ref, ln_ref, mod_ref,
             mixin_ref, mix_ref, xhat_ref, rstd_ref, zbuf, acc):
        first = (pl.program_id(0) % tpb) == 0
        u, bg, cg = (ubc_ref[:, s * cwid:(s + 1) * cwid].astype(F32) for s in range(3))
        z = cg * u
        hz = halo_ref[:, 2 * cwid:3 * cwid].astype(F32) * halo_ref[:, 0:cwid].astype(F32)
        zbuf[0:HALO, :] = jnp.where(first, 0.0, hz)
        zbuf[HALO:HALO + tm, :] = z
        y = (cw_ref[0:1, :] * zbuf[HALO - 2:HALO - 2 + tm, :] + cw_ref[1:2, :] * zbuf[HALO - 1:HALO - 1 + tm, :]
             + cw_ref[2:3, :] * z)
        mixin_ref[:, 0:ATTN_WIDTH] = attn_ref[...]
        mixin_ref[:, ATTN_WIDTH:] = (bg * y).astype(BF16)
        mv = mixin_ref[...]
        for j in range(d // COL_CHUNK):
            acc[:, j * COL_CHUNK:(j + 1) * COL_CHUNK] = jnp.dot(
                mv, w_ref[:, j * COL_CHUNK:(j + 1) * COL_CHUNK], preferred_element_type=F32)
        scale = 1.0 + mod_ref[0, gate_idx:gate_idx + 1, :]

        mix = acc[...]
        xhat, rstd = _ln_stats(DN_ALPHA * (x_ref[...] * ln_ref[0:1, :] + ln_ref[1:2, :]) + scale * mix)
        mix_ref[...] = mix.astype(BF16)
        xhat_ref[...] = xhat
        rstd_ref[...] = rstd

    row = lambda w: pl.BlockSpec((tm, w), lambda i: (i, 0))
    return _pcall(
        body, name=name, grid=(t // tm,),
        in_specs=[row(ATTN_WIDTH), row(3 * cwid),
                  pl.BlockSpec((HALO, 3 * cwid), lambda i: (jnp.maximum(i * (tm // HALO) - 1, 0), 0)),
                  _full((8, cwid)), _resident((d, d)), row(d), _full((2, d)),
                  pl.BlockSpec((1, N_MOD, d), lambda i: (i // tpb, 0, 0))],
        out_specs=[row(d), row(d), row(d), row(1)],
        out_shape=[jax.ShapeDtypeStruct((t, d), BF16), jax.ShapeDtypeStruct((t, d), BF16),
                   jax.ShapeDtypeStruct((t, d), F32), jax.ShapeDtypeStruct((t, 1), F32)],
        scratch_shapes=[pltpu.VMEM((tm + HALO, cwid), F32), pltpu.VMEM((tm, d), F32)],
        args=(attn, ubc, ubc, cw, wout, xhat_in, lnp_in, mod), comm=comm)


def _ffn_bwd_act(df, wd, dact, *, seq, name, comm=None):
    t, d = df.shape
    f = wd.shape[0]
    tm = min(512, seq)
    ch = min(COL_CHUNK, f)

    def body(df_ref, wd_ref, dact_ref, dgu_ref):
        dfv = df_ref[...]
        for j in range(f // ch):
            da = _dot_nt(dfv, wd_ref[j * ch:(j + 1) * ch, :])
            dgu_ref[:, j * ch:(j + 1) * ch] = (da * dact_ref[:, j * ch:(j + 1) * ch].astype(F32)).astype(BF16)
            dgu_ref[:, f + j * ch:f + (j + 1) * ch] = (
                da * dact_ref[:, f + j * ch:f + (j + 1) * ch].astype(F32)).astype(BF16)

    return _pcall(
        body, name=name, grid=(t // tm,),
        in_specs=[pl.BlockSpec((tm, d), lambda i: (i, 0)), _resident((f, d)),
                  pl.BlockSpec((tm, 2 * f), lambda i: (i, 0))],
        out_specs=pl.BlockSpec((tm, 2 * f), lambda i: (i, 0)),
        out_shape=jax.ShapeDtypeStruct((t, 2 * f), BF16),
        args=(df, wd, dact), comm=comm)


def _bwd_in(a, w, dr, xin, rstd_prev, lnp_prev, mod, branch_prev, *, seq, w_is_nt, sc_idx, gate_idx,
            branch_scale, final, name, comm=None):
    t, kdim = a.shape
    d = dr.shape[1]
    nb = t // seq
    tm = min(512, seq)
    tpb = seq // tm

    def body(*refs):
        if final:
            a_ref, w_ref, dr_ref, x_ref, mod_ref, dx_ref, dsc_ref, dsh_ref, acc = refs
        else:
            (a_ref, w_ref, dr_ref, x_ref, rstd_ref, ln_ref, mod_ref, br_ref,
             drp_ref, dbr_ref, dsc_ref, dsh_ref, dgate_ref, dg_ref, db_ref, acc) = refs
        i = pl.program_id(0)
        av = a_ref[...]
        for j in range(d // COL_CHUNK):
            cols = slice(j * COL_CHUNK, (j + 1) * COL_CHUNK)
            acc[:, cols] = (_dot_nt(av, w_ref[cols, :]) if w_is_nt
                            else jnp.dot(av, w_ref[:, cols], preferred_element_type=F32))
        sc1 = 1.0 + mod_ref[0, sc_idx:sc_idx + 1, :]
        if not final:
            g_prev, b_prev = ln_ref[0:1, :], ln_ref[1:2, :]
            bscale = branch_scale * (1.0 + mod_ref[0, gate_idx:gate_idx + 1, :])

        def chunk(rows, carry):
            dh = acc[rows, :]
            dx = DN_ALPHA * dr_ref[rows, :] + dh * sc1
            if final:
                dx_ref[rows, :] = dx
                return carry[0] + _fold8(dh * x_ref[rows, :]), carry[1] + _fold8(dh)
            xhat = x_ref[rows, :]
            drp = _ln_bwd(dx, xhat, rstd_ref[rows, :], g_prev)
            drp_ref[rows, :] = drp
            dbr_ref[rows, :] = (bscale * drp).astype(BF16)
            return (carry[0] + _fold8(dh * xhat), carry[1] + _fold8(dh),
                    carry[2] + _fold8(br_ref[rows, :].astype(F32) * drp),
                    carry[3] + _fold8(dx * xhat), carry[4] + _fold8(dx))

        zero = jnp.zeros((8, d), F32)
        sums = list(_row_chunk_loop(tm, chunk, (zero,) * (2 if final else 5)))
        if not final:
            sums[0] = sums[0] * g_prev + sums[1] * b_prev
            sums[2] = sums[2] * branch_scale

        @pl.when((i % tpb) == 0)
        def _():
            dsc_ref[...] = jnp.zeros_like(dsc_ref)
            dsh_ref[...] = jnp.zeros_like(dsh_ref)
            if not final:
                dgate_ref[...] = jnp.zeros_like(dgate_ref)

        dsc_ref[0] += _row_sum(sums[0])
        dsh_ref[0] += _row_sum(sums[1])
        if not final:
            @pl.when(i == 0)
            def _():
                dg_ref[...] = jnp.zeros_like(dg_ref)
                db_ref[...] = jnp.zeros_like(db_ref)

            dgate_ref[0] += _row_sum(sums[2])
            dg_ref[...] += _row_sum(sums[3])
            db_ref[...] += _row_sum(sums[4])

    row = lambda w_: pl.BlockSpec((tm, w_), lambda i: (i, 0))
    vec = pl.BlockSpec((1, 1, d), lambda i: (i // tpb, 0, 0))
    mod_spec = pl.BlockSpec((1, N_MOD, d), lambda i: (i // tpb, 0, 0))
    vshape = jax.ShapeDtypeStruct((nb, 1, d), F32)
    if final:
        in_specs = [row(kdim), _resident(w.shape), row(d), row(d), mod_spec]
        args = (a, w, dr, xin, mod)
        out_specs = [row(d), vec, vec]
        out_shape = [jax.ShapeDtypeStruct((t, d), F32), vshape, vshape]
    else:
        in_specs = [row(kdim), _resident(w.shape), row(d), row(d), row(1), _full((2, d)), mod_spec, row(d)]
        args = (a, w, dr, xin, rstd_prev, lnp_prev, mod, branch_prev)
        out_specs = [row(d), row(d), vec, vec, vec, _full((1, d)), _full((1, d))]
        out_shape = [jax.ShapeDtypeStruct((t, d), F32), jax.ShapeDtypeStruct((t, d), BF16), vshape, vshape, vshape,
                     jax.ShapeDtypeStruct((1, d), F32), jax.ShapeDtypeStruct((1, d), F32)]
    return _pcall(
        body, name=name, grid=(t // tm,), in_specs=in_specs, out_specs=out_specs, out_shape=out_shape,
        scratch_shapes=[pltpu.VMEM((tm, d), F32)], args=args, comm=comm)


def _grad_chip_sum(pos, a, b, *, half_on_rows, name, comm=None):
    t, m = a.shape
    n = b.shape[1]
    tk = min(2048, t)
    nk = t // tk
    half = lambda p, pos_ref: 1 - pos_ref[2] - p + 2 * p * pos_ref[2]
    if half_on_rows:
        n_j = N_CHIPS
        tile = (m // 2, n // n_j)
        a_spec = pl.BlockSpec((tk, tile[0]), lambda p, j, k, pos_ref: (k, half(p, pos_ref)))
        b_spec = pl.BlockSpec((tk, tile[1]), lambda p, j, k, pos_ref: (k, j))
        out_tile = pl.BlockSpec((1, *tile), lambda p, j, k, pos_ref: (0, 0, j * p))
        total = (1, m // 2, n)
    else:
        n_j = 2
        tile = (m // n_j, n // 2)
        a_spec = pl.BlockSpec((tk, tile[0]), lambda p, j, k, pos_ref: (k, j))
        b_spec = pl.BlockSpec((tk, tile[1]), lambda p, j, k, pos_ref: (k, half(p, pos_ref)))
        out_tile = pl.BlockSpec((1, *tile), lambda p, j, k, pos_ref: (0, j * p, 0))
        total = (1, m, n // 2)

    def body(pos_ref, a_ref, b_ref, s32_ref, s16_ref, land_ref, acc, theirs, send_sems, recv_sems, copy_sem):
        p, j, k = pl.program_id(0), pl.program_id(1), pl.program_id(2)
        x, y, c = _position()

        def push(jj):
            return pltpu.make_async_remote_copy(
                src_ref=acc.at[jj], dst_ref=land_ref.at[jj], send_sem=send_sems.at[jj], recv_sem=recv_sems.at[jj],
                device_id=(x, y, 1 - c), device_id_type=MESH)

        fetch = pltpu.make_async_copy(land_ref.at[j], theirs, copy_sem)

        @pl.when(jnp.logical_and(p == 1, k == 0))
        def _():
            push(j).wait_send()
            push(j).wait_recv()
            fetch.start()

        part = _dot_tn(a_ref[...], b_ref[...])

        @pl.when(k == 0)
        def _():
            acc[j] = part

        @pl.when(k > 0)
        def _():
            acc[j] += part

        @pl.when(jnp.logical_and(p == 0, k == nk - 1))
        def _():
            push(j).start()

        @pl.when(jnp.logical_and(p == 1, k == nk - 1))
        def _():
            fetch.wait()
            s = acc[j] + theirs[...]
            s32_ref[0] = s
            s16_ref[0] = s.astype(BF16)

    out = _pcall(
        body, name=name, grid=(2, n_j, nk), in_specs=[a_spec, b_spec], out_specs=[out_tile, out_tile, ANY_SPEC],
        out_shape=[jax.ShapeDtypeStruct(total, F32), jax.ShapeDtypeStruct(total, BF16),
                   jax.ShapeDtypeStruct((n_j, *tile), F32)],
        scratch_shapes=[pltpu.VMEM((n_j, *tile), F32), pltpu.VMEM(tile, F32),
                        pltpu.SemaphoreType.DMA((n_j,)), pltpu.SemaphoreType.DMA((n_j,)), pltpu.SemaphoreType.DMA],
        args=(a, b), prefetch=pos, comm=comm)
    if comm is None:
        return out[0], out[1]
    (s32, s16, _), extra = out
    return (s32, s16), extra


def _matmul_nt_bf16(a, w, *, seq, name):
    t, kdim = a.shape
    n = w.shape[0]
    tm = min(512, seq)

    def body(a_ref, w_ref, o_ref):
        av = a_ref[...]
        for j in range(n // COL_CHUNK):
            o_ref[:, j * COL_CHUNK:(j + 1) * COL_CHUNK] = _dot_nt(
                av, w_ref[j * COL_CHUNK:(j + 1) * COL_CHUNK, :]).astype(BF16)

    return pl.pallas_call(
        body, name=name, grid=(t // tm,),
        in_specs=[pl.BlockSpec((tm, kdim), lambda i: (i, 0)), _resident((n, kdim))],
        out_specs=pl.BlockSpec((tm, n), lambda i: (i, 0)),
        out_shape=jax.ShapeDtypeStruct((t, n), BF16),
        compiler_params=_params(("arbitrary",)),
    )(a, w)


def _attention_bwd(q, k, v, dmixin, sinks, *, seq, name, comm=None):
    t = q.shape[0]
    nblk = seq // BLOCK
    tile = ATTN_TILE_BLOCKS * BLOCK

    def body(q_ref, kp_ref, kc_ref, vp_ref, vc_ref, do_ref, sink_ref,
             dq_ref, dkp_ref, dkc_ref, dvp_ref, dvc_ref, dsink_ref):
        n = pl.program_id(0)

        @pl.when(n == 0)
        def _():
            dsink_ref[...] = jnp.zeros_like(dsink_ref)

        srow = lax.broadcasted_iota(jnp.int32, (8, LANE), 0)
        dsink = jnp.zeros((8, LANE), F32)
        for s in range(ATTN_TILE_BLOCKS):
            rows, k_prev, k_cur, v_prev, v_cur, first = _attn_sub_block(s, n, nblk, kp_ref, kc_ref, vp_ref, vc_ref)
            dqs, dks, dvs = [], [], []
            for g in range(N_KV_HEADS):
                qs, kk, vv, pn, psn = _attn_group(q_ref, rows, k_prev, k_cur, v_prev, v_cur, sink_ref, g, first)
                dos = jnp.concatenate(
                    [do_ref[rows, (GQA_GROUP * g + j) * HEAD_DIM:(GQA_GROUP * g + j + 1) * HEAD_DIM]
                     for j in range(GQA_GROUP)], axis=0)
                dp = _dot_nt(vv, dos)
                delta = jnp.sum(pn * dp, axis=0, keepdims=True)
                ds = pn * (dp - delta)
                dsk = psn * delta
                for j in range(GQA_GROUP):
                    tot = jnp.sum(dsk[:, j * BLOCK:(j + 1) * BLOCK], axis=1, keepdims=True)
                    dsink = dsink - jnp.where(srow == GQA_GROUP * g + j, tot, 0.0)
                dsb = (ds * (HEAD_DIM ** -0.5)).astype(BF16)
                dqs.append(_heads_to_lanes(_dot_tn(kk, dsb)))
                dks.append(jnp.dot(dsb, qs, preferred_element_type=F32))
                dvs.append(jnp.dot(pn.astype(BF16), dos, preferred_element_type=F32))
            dq_ref[rows, :] = jnp.concatenate(dqs, axis=1)
            dkp_ref[rows, :] = jnp.concatenate([x[0:BLOCK, :] for x in dks], axis=1)
            dkc_ref[rows, :] = jnp.concatenate([x[BLOCK:, :] for x in dks], axis=1)
            dvp_ref[rows, :] = jnp.concatenate([x[0:BLOCK, :] for x in dvs], axis=1)
            dvc_ref[rows, :] = jnp.concatenate([x[BLOCK:, :] for x in dvs], axis=1)
        dsink_ref[...] += dsink

    cur = lambda w: pl.BlockSpec((tile, w), lambda n: (n, 0))
    prev = lambda w: pl.BlockSpec((BLOCK, w), lambda n: (jnp.maximum(n * ATTN_TILE_BLOCKS - 1, 0), 0))
    kv = jax.ShapeDtypeStruct((t, KV_WIDTH), F32)
    return _pcall(
        body, name=name, grid=(t // tile,),
        in_specs=[cur(ATTN_WIDTH), prev(KV_WIDTH), cur(KV_WIDTH), prev(KV_WIDTH), cur(KV_WIDTH), cur(ATTN_WIDTH),
                  pl.BlockSpec(memory_space=pltpu.SMEM)],
        out_specs=[cur(ATTN_WIDTH), cur(KV_WIDTH), cur(KV_WIDTH), cur(KV_WIDTH), cur(KV_WIDTH), _full((8, LANE))],
        out_shape=[jax.ShapeDtypeStruct((t, ATTN_WIDTH), F32), kv, kv, kv, kv, jax.ShapeDtypeStruct((8, LANE), F32)],
        args=(q, k, k, v, v, dmixin, sinks), comm=comm)


def _mix_bwd_assemble(dq, dkp, dkc, dvp, dvc, cos, sa, sb, dmixin, ubc, cw, *, seq, name, comm=None):
    t = dq.shape[0]
    cwid = CONV_WIDTH
    tm = min(2 * BLOCK, seq)
    tiles_per_seq = seq // tm
    ntile = t // tm
    nblk_all = t // BLOCK
    per_tile = tm // BLOCK

    def body(*refs):
        dq_ref, dkc_ref, dvc_ref = refs[0:3]
        dkp_refs, dvp_refs = refs[3:3 + per_tile], refs[3 + per_tile:3 + 2 * per_tile]
        (cos_ref, sa_ref, sb_ref, dco_ref, dcon_ref, ubc_ref, hprev_ref, hnext_ref, cw_ref,
         dproj_ref, dcw_ref, zbuf, dybuf) = refs[3 + 2 * per_tile:]
        i = pl.program_id(0)
        first = (i % tiles_per_seq) == 0
        last = (i % tiles_per_seq) == tiles_per_seq - 1
        glast = i == ntile - 1

        @pl.when(i == 0)
        def _():
            dcw_ref[...] = jnp.zeros_like(dcw_ref)

        def with_next_block(cur_ref, nxt_refs):
            nxt = [r[...] for r in nxt_refs]
            nxt[-1] = jnp.where(glast, 0.0, nxt[-1])
            return cur_ref[...] + jnp.concatenate(nxt, axis=0)

        cos_t, sa_t, sb_t = cos_ref[...], sa_ref[...], sb_ref[...]
        for j in range(ATTN_WIDTH // LANE):
            dproj_ref[:, j * LANE:(j + 1) * LANE] = _rope_t(
                dq_ref[:, j * LANE:(j + 1) * LANE], cos_t, sa_t, sb_t).astype(BF16)
        dk = with_next_block(dkc_ref, dkp_refs)
        dproj_ref[:, ATTN_WIDTH:ATTN_WIDTH + KV_WIDTH] = _rope_t(dk, cos_t, sa_t, sb_t).astype(BF16)
        dv = with_next_block(dvc_ref, dvp_refs)
        dproj_ref[:, ATTN_WIDTH + KV_WIDTH:ATTN_WIDTH + 2 * KV_WIDTH] = dv.astype(BF16)

        u, bg, cg = (ubc_ref[:, s * cwid:(s + 1) * cwid].astype(F32) for s in range(3))
        z = cg * u
        hz = hprev_ref[:, 2 * cwid:3 * cwid].astype(F32) * hprev_ref[:, 0:cwid].astype(F32)
        zbuf[0:HALO, :] = jnp.where(first, 0.0, hz)
        zbuf[HALO:HALO + tm, :] = z
        z2, z1 = zbuf[HALO - 2:HALO - 2 + tm, :], zbuf[HALO - 1:HALO - 1 + tm, :]
        w0, w1, w2 = cw_ref[0:1, :], cw_ref[1:2, :], cw_ref[2:3, :]
        y = w0 * z2 + w1 * z1 + w2 * z
        dco = dco_ref[...].astype(F32)
        dyc = dco * bg
        dyn = dcon_ref[...].astype(F32) * hnext_ref[:, cwid:2 * cwid].astype(F32)
        dybuf[0:tm, :] = dyc
        dybuf[tm:tm + HALO, :] = jnp.where(last, 0.0, dyn)
        dz = w2 * dyc + w1 * dybuf[1:1 + tm, :] + w0 * dybuf[2:2 + tm, :]
        srow = lax.broadcasted_iota(jnp.int32, (8, cwid), 0)
        dcw_ref[...] += (jnp.where(srow == 0, _row_sum(dyc * z2), 0.0) + jnp.where(srow == 1, _row_sum(dyc * z1), 0.0)
                         + jnp.where(srow == 2, _row_sum(dyc * z), 0.0))
        base = ATTN_WIDTH + 2 * KV_WIDTH
        dproj_ref[:, base:base + cwid] = (dz * cg).astype(BF16)
        dproj_ref[:, base + cwid:base + 2 * cwid] = (dco * y).astype(BF16)
        dproj_ref[:, base + 2 * cwid:base + 3 * cwid] = (dz * u).astype(BF16)

    cur = lambda w: pl.BlockSpec((tm, w), lambda i: (i, 0))
    nxt = [pl.BlockSpec((BLOCK, KV_WIDTH), lambda i, s=s: (jnp.minimum(i * per_tile + s + 1, nblk_all - 1), 0))
           for s in range(per_tile)]
    prev_halo = pl.BlockSpec((HALO, 3 * cwid), lambda i: (jnp.maximum(i * (tm // HALO) - 1, 0), 0))
    next_halo = lambda w, col: pl.BlockSpec(
        (HALO, w), lambda i: (jnp.minimum((i + 1) * (tm // HALO), t // HALO - 1), col))
    return _pcall(
        body, name=name, grid=(ntile,),
        in_specs=[cur(ATTN_WIDTH), cur(KV_WIDTH), cur(KV_WIDTH), *nxt, *nxt,
                  cur(LANE), cur(LANE), cur(LANE),
                  pl.BlockSpec((tm, cwid), lambda i: (i, 1)), next_halo(cwid, 1),
                  cur(3 * cwid), prev_halo, next_halo(3 * cwid, 0), _full((8, cwid))],
        out_specs=[cur(IN_WIDTH), _full((8, cwid))],
        out_shape=[jax.ShapeDtypeStruct((t, IN_WIDTH), BF16), jax.ShapeDtypeStruct((8, cwid), F32)],
        scratch_shapes=[pltpu.VMEM((tm + HALO, cwid), F32), pltpu.VMEM((tm + HALO, cwid), F32)],
        args=(dq, dkc, dvc, *([dkp] * per_tile), *([dvp] * per_tile), cos, sa, sb, dmixin, dmixin,
              ubc, ubc, ubc, cw), comm=comm)


def _ada_fwd(c_all, w_ada, b_ada_shard, *, name, comm=None):
    nb, d = c_all.shape
    n = w_ada.shape[1]
    tn = n // 2

    def body(c_ref, w_ref, b_ref, o_ref):
        cv = c_ref[...]
        cond = cv * _sigmoid(cv)
        o_ref[...] = jnp.dot(cond, w_ref[...], preferred_element_type=F32,
                             precision=lax.Precision.HIGHEST) + b_ref[...]

    return _pcall(
        body, name=name, grid=(n // tn,),
        in_specs=[_full((nb, d)), pl.BlockSpec((d, tn), lambda j: (0, j)), pl.BlockSpec((1, tn), lambda j: (0, j))],
        out_specs=pl.BlockSpec((nb, tn), lambda j: (0, j)),
        out_shape=jax.ShapeDtypeStruct((nb, n), F32), args=(c_all, w_ada, b_ada_shard), comm=comm)


def _small_finish(gathered, dmod_all, dmod_shard, c_all_t, *, name):
    d = D_MODEL
    nb, n = dmod_shard.shape

    def body(g_ref, dm_ref, dms_ref, ct_ref, sum_ref, gw_ref, gb_ref):
        total = g_ref[0]
        for dev in range(1, N_DEV):
            total = total + g_ref[dev]
        sum_ref[...] = total
        gb_ref[...] = _row_sum(dm_ref[...])
        ctv = ct_ref[...]
        cond_t = ctv * _sigmoid(ctv)
        for jb in range(n // COL_CHUNK):
            gw_ref[:, jb * COL_CHUNK:(jb + 1) * COL_CHUNK] = jnp.dot(
                cond_t, dms_ref[:, jb * COL_CHUNK:(jb + 1) * COL_CHUNK], preferred_element_type=F32,
                precision=lax.Precision.HIGHEST)

    return pl.pallas_call(
        body, name=name, grid=(1,),
        in_specs=[_full((N_DEV, SMALL_ROWS, d)), _full((nb, N_MOD * d)), _full((nb, n)), _full((d, nb))],
        out_specs=[_full((SMALL_ROWS, d)), _full((d, n)), _full((1, N_MOD * d))],
        out_shape=[jax.ShapeDtypeStruct((SMALL_ROWS, d), F32), jax.ShapeDtypeStruct((d, n), F32),
                   jax.ShapeDtypeStruct((1, N_MOD * d), F32)],
        compiler_params=_params(("arbitrary",)),
    )(gathered, dmod_all, dmod_shard, c_all_t)


def _row_tile(r, c, budget=1 << 21):
    if r * c * 4 <= budget or r % 16:
        return r
    best = 16
    for tr in range(16, r + 1, 16):
        if r % tr == 0 and tr * c * 4 <= budget:
            best = tr
    return best


def _cast_into(w, chip, col_kind, *, name):
    r, c = w.shape
    tr = _row_tile(r, c)

    def body(chip_ref, w_ref, o_ref):
        o_ref[...] = w_ref[...].astype(BF16)

    if col_kind:
        out_spec = pl.BlockSpec((tr, c), lambda i, chip_ref: (i, chip_ref[0]))
        out_shape = jax.ShapeDtypeStruct((r, c * N_CHIPS), BF16)
    else:
        out_spec = pl.BlockSpec((tr, c), lambda i, chip_ref: (chip_ref[0] * (r // tr) + i, 0))
        out_shape = jax.ShapeDtypeStruct((r * N_CHIPS, c), BF16)
    return _pcall(body, name=name, grid=(r // tr,), in_specs=[pl.BlockSpec((tr, c), lambda i, chip_ref: (i, 0))],
                  out_specs=out_spec, out_shape=out_shape, args=(w,), prefetch=chip)


def _adamw(w, g, m, v, *, name, comm=None):
    r, c = w.shape
    tr = _row_tile(r, c)
    c1 = 1.0 - ADAM_B1 ** ADAM_STEP
    c2 = 1.0 - ADAM_B2 ** ADAM_STEP

    def body(w_ref, g_ref, m_ref, v_ref, d_ref, nm_ref, nv_ref):
        gv = g_ref[...]
        m2 = ADAM_B1 * m_ref[...] + (1.0 - ADAM_B1) * gv
        v2 = ADAM_B2 * v_ref[...] + (1.0 - ADAM_B2) * (gv * gv)
        d_ref[...] = -ADAM_LR * ((m2 / c1) / (jnp.sqrt(v2 / c2) + ADAM_EPS) + ADAM_WD * w_ref[...])
        nm_ref[...] = m2
        nv_ref[...] = v2

    spec = pl.BlockSpec((tr, c), lambda i: (i, 0))
    sh = jax.ShapeDtypeStruct((r, c), F32)
    return _pcall(body, name=name, grid=(r // tr,), in_specs=[spec] * 4, out_specs=[spec] * 3, out_shape=[sh] * 3,
                  args=(w, g, m, v), comm=comm)


def _sum_final(pos, s32, recv, *, col_kind, n_shard, name, comm=None):
    def body(pos_ref, s_ref, r_ref, o_ref):
        total = ((s_ref[0] + r_ref[0].astype(F32)) + r_ref[1].astype(F32)) + r_ref[2].astype(F32)
        if col_kind:
            o_ref[0] = total
        else:
            o_ref[...] = total

    if col_kind:
        rows, cols = s32.shape[1], n_shard
        tr = _row_tile(rows, cols)
        own = pl.BlockSpec((1, tr, cols), lambda i, pos: (0, i, 2 * pos[0] + pos[1]))
        out_spec = pl.BlockSpec((1, tr, cols), lambda i, pos: (pos[2], i, 0))
        out_shape = jax.ShapeDtypeStruct((2, rows, cols), F32)
    else:
        rows, cols = n_shard, s32.shape[2]
        tr = _row_tile(rows, cols)
        own = pl.BlockSpec((1, tr, cols), lambda i, pos: (0, (2 * pos[0] + pos[1]) * (rows // tr) + i, 0))
        out_spec = pl.BlockSpec((tr, cols), lambda i, pos: (i, pos[2]))
        out_shape = jax.ShapeDtypeStruct((rows, 2 * cols), F32)
    return _pcall(
        body, name=name, grid=(rows // tr,),
        in_specs=[own, pl.BlockSpec((3, tr, cols), lambda i, pos: (0, i, 0))], out_specs=out_spec,
        out_shape=out_shape, args=(s32, recv), prefetch=pos, comm=comm)


def _position():
    return lax.axis_index("x"), lax.axis_index("y"), lax.axis_index("c")


def _allgather8(x_shard, *, name, comm=None):
    m_per, n = x_shard.shape
    nci, nco = (0, 0) if comm is None else (len(comm.inputs), len(comm.out_shapes))

    def body(*refs):
        x_ref, refs = refs[0], refs[1:]
        cin, refs = refs[:nci], refs[nci:]
        out_ref, refs = refs[0], refs[1:]
        cout, refs = refs[:nco], refs[nco:]
        (send_sems, recv_sems, local_sem), csems = refs[:3], refs[3:]
        x, y, c = _position()
        me, sibling = (x, y, c), (x, y, 1 - c)
        chips = [(1 - x, y), (x, 1 - y), (1 - x, 1 - y)]

        def rows(px, py, pc):
            return out_ref.at[pl.ds((4 * px + 2 * py + pc) * m_per, m_per), :]

        def copy(k, block, to, src=None):
            return pltpu.make_async_remote_copy(
                src_ref=rows(*block) if src is None else src, dst_ref=rows(*block),
                send_sem=send_sems.at[k], recv_sem=recv_sems.at[k], device_id=to, device_id_type=MESH)

        mine = pltpu.make_async_copy(x_ref, rows(*me), local_sem)
        mine.start()
        first = [copy(0, me, sibling, src=x_ref)]
        first += [copy(1 + j, me, (*chip, c), src=x_ref) for j, chip in enumerate(chips)]
        for cp in first:
            cp.start()
        if comm is not None:
            comm.start(cin, cout, csems)
        passed = [copy(4 + j, (*chip, c), sibling) for j, chip in enumerate(chips)]
        for j, chip in enumerate(chips):
            copy(1 + j, (*chip, c), me).wait_recv()
            passed[j].start()
        copy(0, sibling, me).wait_recv()
        for j, chip in enumerate(chips):
            copy(4 + j, (*chip, 1 - c), me).wait_recv()
        for cp in first + passed:
            cp.wait_send()
        mine.wait()
        if comm is not None:
            comm.middle(cin, cout, csems)
            comm.late(cin, cout, csems)
            comm.finish(cin, cout, csems)

    vmem = pl.BlockSpec(memory_space=pltpu.VMEM)
    sems = [pltpu.SemaphoreType.DMA((7,)), pltpu.SemaphoreType.DMA((7,)), pltpu.SemaphoreType.DMA]
    out = jax.ShapeDtypeStruct((N_DEV * m_per, n), x_shard.dtype)
    if comm is None:
        return pl.pallas_call(body, name=name, out_shape=out, in_specs=[vmem], out_specs=vmem,
                              scratch_shapes=sems)(x_shard)
    res = pl.pallas_call(
        body, name=name, out_shape=[out] + list(comm.out_shapes), in_specs=[vmem] + [ANY_SPEC] * nci,
        out_specs=[vmem] + [ANY_SPEC] * nco, scratch_shapes=sems + list(comm.sems),
        input_output_aliases={1 + i: 1 + o for i, o in comm.aliases.items()})(x_shard, *comm.inputs)
    return res[0], list(res[1:])


def _peer_chips(x, y):
    return [(1 - x, y), (x, 1 - y), (1 - x, 1 - y)]


class _GatherJob:
    def __init__(self, pieces):
        self.pieces = pieces
        n_p = len(pieces)
        self.inputs = [p[0] for p in pieces]
        self.out_shapes = [jax.ShapeDtypeStruct(p[0].shape, p[0].dtype) for p in pieces]
        for buf, col_kind, r0, nr in pieces:
            half_rows = buf.shape[0] // (2 if col_kind else 2 * N_CHIPS)
            assert r0 % 16 == 0 and nr % 16 == 0 and nr >= 32 and r0 + nr <= half_rows, (buf.shape, r0, nr)
        self.aliases = {p: p for p in range(n_p)}
        dma = pltpu.SemaphoreType.DMA
        self.sems = [dma((2 * n_p,))] * 4 + [dma((4 * n_p,))] * 2

    def _region(self, cout, p, chip_idx, half, part=None):
        buf, col_kind, r0, nr = self.pieces[p]
        first = -(-nr // 32) * 16
        if part == 0:
            nr = first
        elif part == 1:
            r0, nr = r0 + first, nr - first
        if col_kind:
            n = buf.shape[1] // N_CHIPS
            return cout[p].at[pl.ds(half * (buf.shape[0] // 2) + r0, nr), pl.ds(chip_idx * n, n)]
        n = buf.shape[0] // N_CHIPS
        return cout[p].at[pl.ds(chip_idx * n + half * (n // 2) + r0, nr), :]

    def _copies(self, cout, sems):
        send1, recv1, send2, recv2, fsend, frecv = sems
        x, y, c = _position()
        k = 2 * x + y
        sibling = (x, y, 1 - c)
        x_nbr, y_nbr, diag = _peer_chips(x, y)
        chip_of = lambda ch: 2 * ch[0] + ch[1]

        def remote(region, ssem, rsem, to):
            return pltpu.make_async_remote_copy(src_ref=region, dst_ref=region, send_sem=ssem, recv_sem=rsem,
                                                device_id=to, device_id_type=MESH)

        hop1, arrived1, hop2, arrived2, fwds, fwd_arrived = [], [], [], [], [], []
        for p in range(len(self.pieces)):
            for j, nbr in enumerate((x_nbr, y_nbr)):
                i1 = 2 * p + j
                hop1.append(remote(self._region(cout, p, k, c), send1.at[i1], recv1.at[i1], (*nbr, c)))
                arrived1.append(remote(self._region(cout, p, chip_of(nbr), c), send1.at[i1], recv1.at[i1], (*nbr, c)))
            hop2.append(remote(self._region(cout, p, chip_of(x_nbr), c, 0), send2.at[2 * p], recv2.at[2 * p],
                               (*y_nbr, c)))
            hop2.append(remote(self._region(cout, p, chip_of(y_nbr), c, 1), send2.at[2 * p + 1], recv2.at[2 * p + 1],
                               (*x_nbr, c)))
            arrived2.append(remote(self._region(cout, p, chip_of(diag), c, 0), send2.at[2 * p], recv2.at[2 * p],
                                   (*y_nbr, c)))
            arrived2.append(remote(self._region(cout, p, chip_of(diag), c, 1), send2.at[2 * p + 1],
                                   recv2.at[2 * p + 1], (*x_nbr, c)))
            landed = [(chip_of(x_nbr), None), (chip_of(y_nbr), None), (chip_of(diag), 0), (chip_of(diag), 1)]
            for q, (chip_idx, part) in enumerate(landed):
                i3 = 4 * p + q
                fwds.append(remote(self._region(cout, p, chip_idx, c, part), fsend.at[i3], frecv.at[i3], sibling))
                fwd_arrived.append(remote(self._region(cout, p, chip_idx, 1 - c, part), fsend.at[i3], frecv.at[i3],
                                          sibling))
        return hop1, arrived1, hop2, arrived2, fwds, fwd_arrived

    def start(self, cin, cout, sems):
        for cp in self._copies(cout, sems)[0]:
            cp.start()

    def middle(self, cin, cout, sems):
        _, arrived1, hop2, _, fwds, _ = self._copies(cout, sems)
        for p in range(len(self.pieces)):
            for j in range(2):
                arrived1[2 * p + j].wait_recv()
                hop2[2 * p + j].start()
                fwds[4 * p + j].start()

    def late(self, cin, cout, sems):
        _, _, _, arrived2, fwds, _ = self._copies(cout, sems)
        for p in range(len(self.pieces)):
            for j in range(2):
                arrived2[2 * p + j].wait_recv()
                fwds[4 * p + 2 + j].start()

    def finish(self, cin, cout, sems):
        hop1, _, hop2, _, fwds, fwd_arrived = self._copies(cout, sems)
        for cp in fwd_arrived:
            cp.wait_recv()
        for cp in hop1 + hop2 + fwds:
            cp.wait_send()


class _PairedJob:
    aliases = {}

    def start(self, cin, cout, sems):
        for cp in self._copies(cin, cout, sems):
            cp.start()

    def middle(self, cin, cout, sems):
        pass

    late = middle

    def finish(self, cin, cout, sems):
        copies = self._copies(cin, cout, sems)
        for cp in copies:
            cp.wait_recv()
        for cp in copies:
            cp.wait_send()


class _ExchangeJob(_PairedJob):
    def __init__(self, s16, kinds, sizes):
        self.inputs, self.kinds, self.sizes = list(s16), list(kinds), list(sizes)
        self.out_shapes = [jax.ShapeDtypeStruct((3, s.shape[1], n) if kd else (3, n, s.shape[2]), s.dtype)
                           for s, kd, n in zip(s16, kinds, sizes)]
        self.sems = [pltpu.SemaphoreType.DMA((3 * len(s16),)), pltpu.SemaphoreType.DMA((3 * len(s16),))]

    def _copies(self, cin, cout, sems):
        send_sems, recv_sems = sems
        x, y, c = _position()
        copies = []
        for p, src_ref in enumerate(cin):
            for j, chip in enumerate(_peer_chips(x, y)):
                kk = 2 * chip[0] + chip[1]
                n = self.sizes[p]
                src = src_ref.at[0, :, pl.ds(kk * n, n)] if self.kinds[p] else src_ref.at[0, pl.ds(kk * n, n), :]
                copies.append(pltpu.make_async_remote_copy(
                    src_ref=src, dst_ref=cout[p].at[j], send_sem=send_sems.at[3 * p + j],
                    recv_sem=recv_sems.at[3 * p + j], device_id=(*chip, c), device_id_type=MESH))
        return copies


class _ShareJob:
    def __init__(self, halves):
        self.inputs = list(halves)
        self.out_shapes = [jax.ShapeDtypeStruct(h.shape, h.dtype) for h in halves]
        self.aliases = {p: p for p in range(len(halves))}
        self.sems = [pltpu.SemaphoreType.DMA((len(halves),)), pltpu.SemaphoreType.DMA((len(halves),))]

    def _copies(self, cout, sems, half):
        send_sems, recv_sems = sems
        x, y, c = _position()
        h = c if half == "mine" else 1 - c

        def region(o):
            if len(o.shape) == 3:
                return o.at[h]
            hc = o.shape[1] // 2
            return o.at[:, pl.ds(h * hc, hc)]

        return [pltpu.make_async_remote_copy(
            src_ref=region(o), dst_ref=region(o), send_sem=send_sems.at[p], recv_sem=recv_sems.at[p],
            device_id=(x, y, 1 - c), device_id_type=MESH) for p, o in enumerate(cout)]

    def start(self, cin, cout, sems):
        for cp in self._copies(cout, sems, "mine"):
            cp.start()

    def middle(self, cin, cout, sems):
        pass

    late = middle

    def finish(self, cin, cout, sems):
        for cp in self._copies(cout, sems, "theirs"):
            cp.wait_recv()
        for cp in self._copies(cout, sems, "mine"):
            cp.wait_send()


class _MultiJob:
    def __init__(self, jobs):
        self.jobs = jobs
        self.inputs = [a for j in jobs for a in j.inputs]
        self.out_shapes = [s for j in jobs for s in j.out_shapes]
        self.sems = [s for j in jobs for s in j.sems]
        self.aliases = {}
        i0 = o0 = 0
        for j in jobs:
            for i, o in j.aliases.items():
                self.aliases[i0 + i] = o0 + o
            i0 += len(j.inputs)
            o0 += len(j.out_shapes)

    def _parts(self, cin, cout, sems):
        i0 = o0 = s0 = 0
        for j in self.jobs:
            ni, no, ns = len(j.inputs), len(j.out_shapes), len(j.sems)
            yield j, cin[i0:i0 + ni], cout[o0:o0 + no], sems[s0:s0 + ns]
            i0, o0, s0 = i0 + ni, o0 + no, s0 + ns

    def start(self, cin, cout, sems):
        for j, a, b, s in self._parts(cin, cout, sems):
            j.start(a, b, s)

    def middle(self, cin, cout, sems):
        for j, a, b, s in self._parts(cin, cout, sems):
            j.middle(a, b, s)

    def late(self, cin, cout, sems):
        for j, a, b, s in self._parts(cin, cout, sems):
            j.late(a, b, s)

    def finish(self, cin, cout, sems):
        for j, a, b, s in self._parts(cin, cout, sems):
            j.finish(a, b, s)


def _rope_tables(positions):
    half = ROT_DIM // 2
    inv_freq = jnp.power(jnp.float32(ROPE_THETA), -jnp.arange(0, ROT_DIM, 2, dtype=F32) / ROT_DIM)
    inv_head = jnp.concatenate([inv_freq, inv_freq, jnp.zeros((HEAD_DIM - ROT_DIM,), F32)])
    inv_lane = jnp.concatenate([inv_head] * (LANE // HEAD_DIM))
    ang = positions.astype(F32).reshape(-1)[:, None] * inv_lane[None, :]
    sin = jnp.sin(ang)
    dim = jnp.arange(LANE) % HEAD_DIM
    return jnp.cos(ang), jnp.where(dim < half, -sin, 0.0), jnp.where(dim >= half, sin, 0.0)


def kernel(x, c, positions, w_ada, b_ada, ffn1_w_gate_up, ffn1_w_down, ln1_g, ln1_b, w_in, conv_w, attn_sinks, w_out, ln2_g, ln2_b, ffn2_w_gate_up, ffn2_w_down, ln3_g, ln3_b, loss_target, m_w_ada, m_b_ada, m_ffn1_w_gate_up, m_ffn1_w_down, m_ln1_g, m_ln1_b, m_w_in, m_conv_w, m_attn_sinks, m_w_out, m_ln2_g, m_ln2_b, m_ffn2_w_gate_up, m_ffn2_w_down, m_ln3_g, m_ln3_b, v_w_ada, v_b_ada, v_ffn1_w_gate_up, v_ffn1_w_down, v_ln1_g, v_ln1_b, v_w_in, v_conv_w, v_attn_sinks, v_w_out, v_ln2_g, v_ln2_b, v_ffn2_w_gate_up, v_ffn2_w_down, v_ln3_g, v_ln3_b):
    d = D_MODEL
    nb, seq, _ = x.shape
    t = nb * seq
    f = ffn1_w_down.shape[1] * N_CHIPS
    ax, ay, ac = _position()
    chip = 2 * ax + ay
    dev = 2 * chip + ac
    pos = jnp.stack([ax, ay, ac]).astype(jnp.int32)

    x2 = x.reshape(t, d)
    tgt2 = loss_target.reshape(t, d)
    ln1 = jnp.concatenate([ln1_g, ln1_b], axis=0)
    ln2 = jnp.concatenate([ln2_g, ln2_b], axis=0)
    ln3 = jnp.concatenate([ln3_g, ln3_b], axis=0)
    sinks = attn_sinks.reshape(N_Q_HEADS)
    cos_t, sa_t, sb_t = _rope_tables(positions)

    gu_cuts = [0, 176, 352, d // 2]
    gu_part = lambda buf, s: (buf, True, gu_cuts[s], gu_cuts[s + 1] - gu_cuts[s])
    chip_arr = jnp.reshape(chip, (1,)).astype(jnp.int32)
    b_gu1 = _cast_into(ffn1_w_gate_up[0], chip_arr, True, name="cast_gu1")

    n_ada = w_ada.shape[2]
    c_all, (b_gu1,) = _allgather8(c.reshape(nb * d // LANE, LANE), name="gather_c", comm=_GatherJob([gu_part(b_gu1, 0)]))
    c_all = c_all.reshape(N_DEV * nb, d)
    b_shard = lax.dynamic_slice(b_ada, (0, chip * n_ada), (1, n_ada))
    mod_part, (b_gu1,) = _ada_fwd(c_all, w_ada[0], b_shard, name="ada_fwd", comm=_GatherJob([gu_part(b_gu1, 1)]))
    conv_rows = jnp.pad(conv_w[0], ((0, 5), (0, n_ada - conv_w.shape[2])))
    part = jnp.concatenate([mod_part, conv_rows], axis=0)
    parts, (wgu1,) = _allgather8(part, name="gather_mod", comm=_GatherJob([gu_part(b_gu1, 2)]))
    parts = parts.reshape(N_DEV, N_DEV * nb + 8, n_ada)
    mod_all = jnp.concatenate([parts[2 * k, :N_DEV * nb, :] for k in range(N_CHIPS)], axis=1)
    mod = lax.dynamic_slice(mod_all, (dev * nb, 0), (nb, N_MOD * d)).reshape(nb, N_MOD, d)
    cw_full = jnp.concatenate([parts[2 * k, N_DEV * nb:, :conv_w.shape[2]] for k in range(N_CHIPS)], axis=1)

    b_d1 = _cast_into(ffn1_w_down[0], chip_arr, False, name="cast_d1")
    b_in = _cast_into(w_in[0].T, chip_arr, False, name="cast_in")
    b_out = _cast_into(w_out[0], chip_arr, False, name="cast_out")
    b_gu2 = _cast_into(ffn2_w_gate_up[0], chip_arr, True, name="cast_gu2")
    b_d2 = _cast_into(ffn2_w_down[0], chip_arr, False, name="cast_d2")
    n_gu, n_d, n_in, n_out = (ffn1_w_gate_up.shape[2], ffn1_w_down.shape[1], w_in.shape[2], w_out.shape[1])

    def whole(buf, col_kind):
        return (buf, col_kind, 0, buf.shape[0] // (2 if col_kind else 2 * N_CHIPS))

    (h1, a1, dact1), (wd1, wout) = _ffn_up(x2, ln1, mod, wgu1, seq=seq, sc_idx=1, sh_idx=0, use_ln=False,
                                         name="ffn1_up", comm=_GatherJob([whole(b_d1, False), whole(b_out, False)]))
    (f1, xhat1, rstd1), (win_t,) = _ffn_down_ln(a1, wd1, x2, ln1, mod, seq=seq, gate_idx=2, use_ln=False,
                                                name="ffn1_down", comm=_GatherJob([whole(b_in, False)]))
    (h2, q, k, v, ubc), (b_gu2,) = _in_proj(
        xhat1, ln1, mod, win_t, cos_t, sa_t, sb_t, seq=seq, sc_idx=4, sh_idx=3, name="in_proj",
        comm=_GatherJob([gu_part(b_gu2, 0)]))
    attn, (b_gu2,) = _attention(q, k, v, sinks, seq=seq, name="attention", comm=_GatherJob([gu_part(b_gu2, 1)]))
    (mixin, mix, xhat2, rstd2), (wgu2,) = _out_proj(
        attn, ubc, cw_full, wout, xhat1, ln1, mod, seq=seq, gate_idx=5, name="out_proj",
        comm=_GatherJob([gu_part(b_gu2, 2)]))
    (h3, a3, dact3), (wd2,) = _ffn_up(xhat2, ln2, mod, wgu2, seq=seq, sc_idx=7, sh_idx=6, use_ln=True, name="ffn2_up",
                                    comm=_GatherJob([whole(b_d2, False)]))
    dr3, df3, loss_cols, dln3g, dln3b, dgate3 = _ffn_down_loss(
        a3, wd2, xhat2, ln2, mod, ln3, tgt2, seq=seq, gate_idx=8, name="ffn2_down_loss")

    dgu3 = _ffn_bwd_act(df3, wd2, dact3, seq=seq, name="ffn2_bwd_act")
    s32_d2, s16_d2 = _grad_chip_sum(pos, a3, df3, half_on_rows=False, name="grad_wd2")
    (s32_gu2, s16_gu2), (recv_d2,) = _grad_chip_sum(pos, h3, dgu3, half_on_rows=True, name="grad_wgu2",
                                                    comm=_ExchangeJob([s16_d2], [False], [n_d]))
    (dr2, dmix, dsc3, dsh3, dgate2, dln2g, dln2b), (recv_gu2,) = _bwd_in(
        dgu3, wgu2, dr3, xhat2, rstd2, ln2, mod, mix, seq=seq, w_is_nt=True, sc_idx=7, gate_idx=5,
        branch_scale=1.0, final=False, name="ffn2_bwd_in", comm=_ExchangeJob([s16_gu2], [True], [n_gu]))
    s32_out, s16_out = _grad_chip_sum(pos, mixin, dmix, half_on_rows=False, name="grad_wout")
    dmixin = _matmul_nt_bf16(dmix, wout, seq=seq, name="out_proj_bwd")
    (dq, dkp, dkc, dvp, dvc, dsink), (recv_out,) = _attention_bwd(
        q, k, v, dmixin, sinks, seq=seq, name="attention_bwd", comm=_ExchangeJob([s16_out], [False], [n_out]))
    dproj, dcw = _mix_bwd_assemble(
        dq, dkp, dkc, dvp, dvc, cos_t, sa_t, sb_t, dmixin, ubc, cw_full, seq=seq, name="mix_bwd")
    s32_in, s16_in = _grad_chip_sum(pos, dproj, h2, half_on_rows=False, name="grad_win")
    (dr1, df1, dsc2, dsh2, dgate1, dln1g, dln1b), (recv_in,) = _bwd_in(
        dproj, win_t, dr2, xhat1, rstd1, ln1, mod, f1, seq=seq, w_is_nt=False, sc_idx=4, gate_idx=2,
        branch_scale=0.5, final=False, name="in_proj_bwd", comm=_ExchangeJob([s16_in], [False], [n_in]))
    s32_d1, s16_d1 = _grad_chip_sum(pos, a1, df1, half_on_rows=False, name="grad_wd1")
    dgu1, (recv_d1,) = _ffn_bwd_act(df1, wd1, dact1, seq=seq, name="ffn1_bwd_act",
                                    comm=_ExchangeJob([s16_d1], [False], [n_d]))
    s32_gu1, s16_gu1 = _grad_chip_sum(pos, h1, dgu1, half_on_rows=True, name="grad_wgu1")

    def final_half(s32_, recv_, col_kind, n_shard, name_):
        return _sum_final(pos, s32_, recv_, col_kind=col_kind, n_shard=n_shard, name=name_)

    early = [final_half(s32_gu2, recv_gu2, True, n_gu, "sum_final_gu2"),
             final_half(s32_d2, recv_d2, False, n_d, "sum_final_d2"),
             final_half(s32_out, recv_out, False, n_out, "sum_final_out"),
             final_half(s32_in, recv_in, False, n_in, "sum_final_in"),
             final_half(s32_d1, recv_d1, False, n_d, "sum_final_d1")]
    (grad_x, dsc1, dsh1), (recv_gu1, full_gu2, full_d2, full_out, full_in, full_d1) = _bwd_in(
        dgu1, wgu1, dr1, x2, None, None, mod, None, seq=seq, w_is_nt=True, sc_idx=1, gate_idx=None,
        branch_scale=None, final=True, name="ffn1_bwd_in",
        comm=_MultiJob([_ExchangeJob([s16_gu1], [True], [n_gu]), _ShareJob(early)]))
    late = [final_half(s32_gu1, recv_gu1, True, n_gu, "sum_final_gu1")]

    dmod = jnp.concatenate([dsh1, dsc1, dgate1, dsh2, dsc2, dgate2, dsh3, dsc3, dgate3], axis=1)
    loss_row = jnp.sum(loss_cols, axis=1, keepdims=True) * (0.5 / d)
    lane_row = lambda a: jnp.pad(a, ((0, 0), (0, d - a.shape[1])))
    block = jnp.concatenate(
        [dmod.reshape(nb * N_MOD, d), dln1g, dln1b, dln2g, dln2b, dln3g, dln3b,
         lane_row(dcw[0:3, :]), lane_row(dsink[:, 0:1].reshape(1, N_Q_HEADS)), lane_row(loss_row)], axis=0)
    block = jnp.pad(block, ((0, SMALL_ROWS - block.shape[0]), (0, 0)))
    gathered, (full_gu1,) = _allgather8(block, name="gather_small", comm=_ShareJob(late))
    gathered = gathered.reshape(N_DEV, SMALL_ROWS, d)
    dmod_all = gathered[:, :nb * N_MOD, :].reshape(N_DEV * nb, N_MOD * d)
    dmod_shard = lax.dynamic_slice(dmod_all, (0, chip * n_ada), (N_DEV * nb, n_ada))
    small, g_w_ada, g_b_ada = _small_finish(gathered, dmod_all, dmod_shard, c_all.T, name="small_finish")
    r0 = nb * N_MOD
    loss = small[r0 + 10, 0]
    g_ln = [small[r0 + i:r0 + i + 1, :] for i in range(6)]
    g_cw_full = small[r0 + 6:r0 + 9, :CONV_WIDTH]
    g_conv = lax.dynamic_slice(g_cw_full, (0, chip * conv_w.shape[2]), (3, conv_w.shape[2]))
    g_sinks = small[r0 + 9:r0 + 10, :N_Q_HEADS]

    def flat2(a):
        return a.reshape(-1, a.shape[-1])

    def unhalve(a):
        return a.reshape(2 * a.shape[1], a.shape[2])

    results = {}

    def adamw(name_, w_, g_, m_, v_):
        g2 = flat2(g_)
        dl, nm, nv = _adamw(flat2(w_), g2, flat2(m_), flat2(v_), name="adamw_" + name_)
        results[name_] = tuple(a.reshape(w_.shape) for a in (g2, dl, nm, nv))

    adamw("w_ada", w_ada, g_w_ada, m_w_ada, v_w_ada)
    adamw("ffn2_w_gate_up", ffn2_w_gate_up, unhalve(full_gu2), m_ffn2_w_gate_up, v_ffn2_w_gate_up)
    adamw("ffn2_w_down", ffn2_w_down, full_d2, m_ffn2_w_down, v_ffn2_w_down)
    adamw("w_out", w_out, full_out, m_w_out, v_w_out)
    adamw("w_in", w_in, full_in.T, m_w_in, v_w_in)
    adamw("ffn1_w_gate_up", ffn1_w_gate_up, unhalve(full_gu1), m_ffn1_w_gate_up, v_ffn1_w_gate_up)
    adamw("ffn1_w_down", ffn1_w_down, full_d1, m_ffn1_w_down, v_ffn1_w_down)
    adamw("b_ada", b_ada, g_b_ada, m_b_ada, v_b_ada)
    adamw("ln1_g", ln1_g, g_ln[0], m_ln1_g, v_ln1_g)
    adamw("ln1_b", ln1_b, g_ln[1], m_ln1_b, v_ln1_b)
    adamw("ln2_g", ln2_g, g_ln[2], m_ln2_g, v_ln2_g)
    adamw("ln2_b", ln2_b, g_ln[3], m_ln2_b, v_ln2_b)
    adamw("ln3_g", ln3_g, g_ln[4], m_ln3_g, v_ln3_g)
    adamw("ln3_b", ln3_b, g_ln[5], m_ln3_b, v_ln3_b)
    adamw("conv_w", conv_w, g_conv, m_conv_w, v_conv_w)
    adamw("attn_sinks", attn_sinks, g_sinks, m_attn_sinks, v_attn_sinks)
    order = ["w_ada", "b_ada", "ffn1_w_gate_up", "ffn1_w_down", "ln1_g", "ln1_b", "w_in", "conv_w", "attn_sinks",
             "w_out", "ln2_g", "ln2_b", "ffn2_w_gate_up", "ffn2_w_down", "ln3_g", "ln3_b"]
    return (loss, grad_x.reshape(x.shape), *[results[n_][0] for n_ in order], *[results[n_][1] for n_ in order],
            *[results[n_][2] for n_ in order], *[results[n_][3] for n_ in order])
```

```python
import jax
import jax.numpy as jnp
from jax import lax
from jax.experimental import pallas as pl
from jax.experimental.pallas import tpu as pltpu

F32 = jnp.float32
BF16 = jnp.bfloat16
MESH = pl.DeviceIdType.MESH

D_MODEL = 1024
HEAD_DIM = 64
ATTN_WIDTH = 512
CONV_WIDTH = 512
N_Q_HEADS = 8
N_KV_HEADS = 2
GQA_GROUP = 4
KV_WIDTH = 128
WINDOW = 128
BLOCK = 128
ROT_DIM = 16
ROPE_THETA = 500000.0
N_MOD = 9
LN_EPS = 1e-5
DN_ALPHA = 2.0 ** 0.25
IN_WIDTH = 2304
N_CHIPS = 4
N_DEV = 8
SMALL_ROWS = 32

ADAM_LR = 0.001
ADAM_B1 = 0.9
ADAM_B2 = 0.999
ADAM_EPS = 1e-08
ADAM_WD = 0.01
ADAM_STEP = 10

LANE = 128
HALO = 16
COL_CHUNK = 256
VMEM_LIMIT = 56 * 1024 * 1024


def _params(sem=None, vmem=True):
    return pltpu.CompilerParams(dimension_semantics=sem, vmem_limit_bytes=VMEM_LIMIT if vmem else None)


def _sigmoid(g):
    return 0.5 * jnp.tanh(0.5 * g) + 0.5


def _row_sum(v):
    return jnp.sum(v, axis=0, keepdims=True)


ROW_CHUNK = 16
EPILOGUE_UNROLL = 8


def _fold8(v):
    return v[0:8, :] + v[8:16, :]


def _row_chunk_loop(n_rows, step, init):
    per_iter = ROW_CHUNK * EPILOGUE_UNROLL
    assert n_rows % per_iter == 0, n_rows

    def body(it, carry):
        for s in range(EPILOGUE_UNROLL):
            start = pl.multiple_of(it * per_iter + s * ROW_CHUNK, ROW_CHUNK)
            carry = step(pl.ds(start, ROW_CHUNK), carry)
        return carry

    return lax.fori_loop(0, n_rows // per_iter, body, init)


def _ln_stats(r):
    mu = jnp.mean(r, axis=-1, keepdims=True)
    rc = r - mu
    var = jnp.mean(rc * rc, axis=-1, keepdims=True)
    rstd = lax.rsqrt(var + LN_EPS)
    return rc * rstd, rstd


def _ln_bwd(dxo, xhat, rstd, g):
    dxhat = dxo * g
    m1 = jnp.mean(dxhat, axis=-1, keepdims=True)
    m2 = jnp.mean(dxhat * xhat, axis=-1, keepdims=True)
    return rstd * (dxhat - m1 - xhat * m2)


def _dot_nt(a, b):
    return lax.dot_general(a, b, (((1,), (1,)), ((), ())), preferred_element_type=F32)


def _dot_tn(a, b):
    return lax.dot_general(a, b, (((0,), (0,)), ((), ())), preferred_element_type=F32)


def _full(shape):
    nd = len(shape)
    return pl.BlockSpec(shape, lambda *_: (0,) * nd)


def _resident(shape):
    nd = len(shape)
    return pl.BlockSpec(shape, lambda *_: (0,) * nd, pipeline_mode=pl.Buffered(1))


ANY_SPEC = pl.BlockSpec(memory_space=pl.ANY)


def _pcall(body, *, name, grid, in_specs, out_specs, out_shape, args, scratch_shapes=(), comm=None, prefetch=None):
    single = not isinstance(out_shape, (list, tuple))
    out_specs = [out_specs] if single else list(out_specs)
    out_shape = [out_shape] if single else list(out_shape)
    in_specs = list(in_specs)
    scratch_shapes = list(scratch_shapes)
    sem = ("arbitrary",) * len(grid)
    n_pre = 0 if prefetch is None else 1
    pre_args = () if prefetch is None else (prefetch,)

    def call(fn, ins_, outs_, shapes_, scratch_, aliases_, operands):
        if prefetch is None:
            return pl.pallas_call(fn, name=name, grid=grid, in_specs=ins_, out_specs=outs_, out_shape=shapes_,
                                  scratch_shapes=scratch_, input_output_aliases=aliases_,
                                  compiler_params=_params(sem))(*operands)
        spec = pltpu.PrefetchScalarGridSpec(num_scalar_prefetch=1, grid=grid, in_specs=ins_, out_specs=outs_,
                                            scratch_shapes=scratch_)
        return pl.pallas_call(fn, name=name, grid_spec=spec, out_shape=shapes_,
                              input_output_aliases={n_pre + i: o for i, o in aliases_.items()},
                              compiler_params=_params(sem))(*pre_args, *operands)

    if comm is None:
        res = call(body, in_specs, out_specs, out_shape, scratch_shapes, {}, args)
        return res[0] if single else res
    n_in, n_out, n_scr = len(in_specs), len(out_specs), len(scratch_shapes)
    nci, nco = len(comm.inputs), len(comm.out_shapes)
    n_steps = 1
    for g in grid:
        n_steps *= g
    staged = n_steps >= 8
    middle_step = (n_steps * 5) // 8 - 1
    late_step = n_steps - 1 - max(1, n_steps // 8)

    def wrapped(*refs):
        pre, refs = refs[:n_pre], refs[n_pre:]
        ins, refs = refs[:n_in], refs[n_in:]
        cin, refs = refs[:nci], refs[nci:]
        outs, refs = refs[:n_out], refs[n_out:]
        cout, refs = refs[:nco], refs[nco:]
        scr, csems = refs[:n_scr], refs[n_scr:]
        step = pl.program_id(0)
        for ax in range(1, len(grid)):
            step = step * grid[ax] + pl.program_id(ax)

        @pl.when(step == 0)
        def _():
            comm.start(cin, cout, csems)

        body(*pre, *ins, *outs, *scr)

        if staged:
            @pl.when(step == middle_step)
            def _():
                comm.middle(cin, cout, csems)

            @pl.when(step == late_step)
            def _():
                comm.late(cin, cout, csems)

        @pl.when(step == n_steps - 1)
        def _():
            if not staged:
                comm.middle(cin, cout, csems)
                comm.late(cin, cout, csems)
            comm.finish(cin, cout, csems)

    res = call(wrapped, in_specs + [ANY_SPEC] * nci, out_specs + [ANY_SPEC] * nco,
               out_shape + list(comm.out_shapes), scratch_shapes + list(comm.sems),
               {n_in + i: n_out + o for i, o in comm.aliases.items()}, (*args, *comm.inputs))
    main = res[:n_out]
    return (main[0] if single else main), list(res[n_out:])


def _ffn_up(xin, lnp, mod, w, *, seq, sc_idx, sh_idx, use_ln, name, comm=None):
    t, d = xin.shape
    f = w.shape[1] // 2
    tm = min(512, seq)
    tpb = seq // tm
    ch = min(COL_CHUNK, f)

    def body(x_ref, ln_ref, mod_ref, w_ref, h_ref, a_ref, dact_ref):
        x = x_ref[...]
        if use_ln:
            x = x * ln_ref[0:1, :] + ln_ref[1:2, :]
        h = x * (1.0 + mod_ref[0, sc_idx:sc_idx + 1, :]) + mod_ref[0, sh_idx:sh_idx + 1, :]
        hb = h.astype(BF16)
        h_ref[...] = hb
        for j in range(f // ch):
            g = jnp.dot(hb, w_ref[:, j * ch:(j + 1) * ch], preferred_element_type=F32)
            u = jnp.dot(hb, w_ref[:, f + j * ch:f + (j + 1) * ch], preferred_element_type=F32)
            s = _sigmoid(g)
            silu = g * s
            a_ref[:, j * ch:(j + 1) * ch] = (silu * u).astype(BF16)
            dact_ref[:, j * ch:(j + 1) * ch] = (u * (s + silu * (1.0 - s))).astype(BF16)
            dact_ref[:, f + j * ch:f + (j + 1) * ch] = silu.astype(BF16)

    return _pcall(
        body, name=name, grid=(t // tm,),
        in_specs=[pl.BlockSpec((tm, d), lambda i: (i, 0)), _full((2, d)),
                  pl.BlockSpec((1, N_MOD, d), lambda i: (i // tpb, 0, 0)), _resident((d, 2 * f))],
        out_specs=[pl.BlockSpec((tm, d), lambda i: (i, 0)), pl.BlockSpec((tm, f), lambda i: (i, 0)),
                   pl.BlockSpec((tm, 2 * f), lambda i: (i, 0))],
        out_shape=[jax.ShapeDtypeStruct((t, d), BF16), jax.ShapeDtypeStruct((t, f), BF16),
                   jax.ShapeDtypeStruct((t, 2 * f), BF16)],
        args=(xin, lnp, mod, w), comm=comm)


def _ffn_down_ln(a, wd, xin, lnp_in, mod, *, seq, gate_idx, use_ln, name, comm=None):
    t, f = a.shape
    d = wd.shape[1]
    tm = min(512, seq)
    tpb = seq // tm

    def body(a_ref, wd_ref, x_ref, ln_ref, mod_ref, f_ref, xhat_ref, rstd_ref, acc):
        av = a_ref[...]
        for j in range(d // COL_CHUNK):
            acc[:, j * COL_CHUNK:(j + 1) * COL_CHUNK] = jnp.dot(
                av, wd_ref[:, j * COL_CHUNK:(j + 1) * COL_CHUNK], preferred_element_type=F32)
        scale = 0.5 * (1.0 + mod_ref[0, gate_idx:gate_idx + 1, :])

        fo = acc[...]
        x = x_ref[...]
        if use_ln:
            x = x * ln_ref[0:1, :] + ln_ref[1:2, :]
        xhat, rstd = _ln_stats(DN_ALPHA * x + scale * fo)
        f_ref[...] = fo.astype(BF16)
        xhat_ref[...] = xhat
        rstd_ref[...] = rstd

    return _pcall(
        body, name=name, grid=(t // tm,),
        in_specs=[pl.BlockSpec((tm, f), lambda i: (i, 0)), _resident((f, d)),
                  pl.BlockSpec((tm, d), lambda i: (i, 0)), _full((2, d)),
                  pl.BlockSpec((1, N_MOD, d), lambda i: (i // tpb, 0, 0))],
        out_specs=[pl.BlockSpec((tm, d), lambda i: (i, 0)), pl.BlockSpec((tm, d), lambda i: (i, 0)),
                   pl.BlockSpec((tm, 1), lambda i: (i, 0))],
        out_shape=[jax.ShapeDtypeStruct((t, d), BF16), jax.ShapeDtypeStruct((t, d), F32),
                   jax.ShapeDtypeStruct((t, 1), F32)],
        scratch_shapes=[pltpu.VMEM((tm, d), F32)],
        args=(a, wd, xin, lnp_in, mod), comm=comm)


def _ffn_down_loss(a, wd, xhat_in, lnp_in, mod, lnp_out, tgt, *, seq, gate_idx, name):
    t, f = a.shape
    d = wd.shape[1]
    nb = t // seq
    tm = min(512, seq)
    tpb = seq // tm

    def body(a_ref, wd_ref, x_ref, lnin_ref, mod_ref, lnout_ref, tgt_ref,
             dr_ref, df_ref, loss_ref, dg_ref, db_ref, dgate_ref, acc):
        i = pl.program_id(0)
        av = a_ref[...]
        for j in range(d // COL_CHUNK):
            acc[:, j * COL_CHUNK:(j + 1) * COL_CHUNK] = jnp.dot(
                av, wd_ref[:, j * COL_CHUNK:(j + 1) * COL_CHUNK], preferred_element_type=F32)
        scale = 0.5 * (1.0 + mod_ref[0, gate_idx:gate_idx + 1, :])
        ag_in, ab_in = DN_ALPHA * lnin_ref[0:1, :], DN_ALPHA * lnin_ref[1:2, :]
        g_out, b_out = lnout_ref[0:1, :], lnout_ref[1:2, :]
        g_over_d = g_out * (1.0 / d)

        def chunk(rows, carry):
            s_loss, s_dg, s_db, s_gate = carry
            fo = acc[rows, :]
            xhat, rstd = _ln_stats(x_ref[rows, :] * ag_in + ab_in + scale * fo)
            e = xhat * g_out + b_out - tgt_ref[rows, :]
            dr = _ln_bwd(e, xhat, rstd, g_over_d)
            dr_ref[rows, :] = dr
            df_ref[rows, :] = (scale * dr).astype(BF16)
            return s_loss + _fold8(e * e), s_dg + _fold8(e * xhat), s_db + _fold8(e), s_gate + _fold8(fo * dr)

        zero = jnp.zeros((8, d), F32)
        s_loss, s_dg, s_db, s_gate = _row_chunk_loop(tm, chunk, (zero, zero, zero, zero))
        s_dg, s_db, s_gate = s_dg * (1.0 / d), s_db * (1.0 / d), s_gate * 0.5

        @pl.when(i == 0)
        def _():
            loss_ref[...] = jnp.zeros_like(loss_ref)
            dg_ref[...] = jnp.zeros_like(dg_ref)
            db_ref[...] = jnp.zeros_like(db_ref)

        @pl.when(i % tpb == 0)
        def _():
            dgate_ref[...] = jnp.zeros_like(dgate_ref)

        loss_ref[...] += _row_sum(s_loss)
        dg_ref[...] += _row_sum(s_dg)
        db_ref[...] += _row_sum(s_db)
        dgate_ref[0] += _row_sum(s_gate)

    return pl.pallas_call(
        body, name=name, grid=(t // tm,), scratch_shapes=[pltpu.VMEM((tm, d), F32)],
        in_specs=[pl.BlockSpec((tm, f), lambda i: (i, 0)), _resident((f, d)),
                  pl.BlockSpec((tm, d), lambda i: (i, 0)), _full((2, d)),
                  pl.BlockSpec((1, N_MOD, d), lambda i: (i // tpb, 0, 0)), _full((2, d)),
                  pl.BlockSpec((tm, d), lambda i: (i, 0))],
        out_specs=[pl.BlockSpec((tm, d), lambda i: (i, 0)), pl.BlockSpec((tm, d), lambda i: (i, 0)),
                   _full((1, d)), _full((1, d)), _full((1, d)),
                   pl.BlockSpec((1, 1, d), lambda i: (i // tpb, 0, 0))],
        out_shape=[jax.ShapeDtypeStruct((t, d), F32), jax.ShapeDtypeStruct((t, d), BF16),
                   jax.ShapeDtypeStruct((1, d), F32), jax.ShapeDtypeStruct((1, d), F32),
                   jax.ShapeDtypeStruct((1, d), F32), jax.ShapeDtypeStruct((nb, 1, d), F32)],
        compiler_params=_params(("arbitrary",)),
    )(a, wd, xhat_in, lnp_in, mod, lnp_out, tgt)


def _rope(v, cos, sa, sb):
    return v * cos + pltpu.roll(v, LANE - ROT_DIM // 2, 1) * sa + pltpu.roll(v, ROT_DIM // 2, 1) * sb


def _rope_t(dy, cos, sa, sb):
    return dy * cos + pltpu.roll(dy * sa, ROT_DIM // 2, 1) + pltpu.roll(dy * sb, LANE - ROT_DIM // 2, 1)


def _in_proj(xhat, lnp, mod, w_t, cos, sa, sb, *, seq, sc_idx, sh_idx, name, comm=None):
    t, d = xhat.shape
    tm = min(512, seq)
    tpb = seq // tm
    n_conv = 3 * CONV_WIDTH

    def body(x_ref, ln_ref, mod_ref, w_ref, cos_ref, sa_ref, sb_ref, h_ref, q_ref, k_ref, v_ref, ubc_ref):
        x = x_ref[...] * ln_ref[0:1, :] + ln_ref[1:2, :]
        h = x * (1.0 + mod_ref[0, sc_idx:sc_idx + 1, :]) + mod_ref[0, sh_idx:sh_idx + 1, :]
        hb = h.astype(BF16)
        h_ref[...] = hb
        cos_t, sa_t, sb_t = cos_ref[...], sa_ref[...], sb_ref[...]
        for j in range(ATTN_WIDTH // COL_CHUNK):
            p = _dot_nt(hb, w_ref[j * COL_CHUNK:(j + 1) * COL_CHUNK, :])
            for s in range(COL_CHUNK // LANE):
                q_ref[:, j * COL_CHUNK + s * LANE:j * COL_CHUNK + (s + 1) * LANE] = _rope(
                    p[:, s * LANE:(s + 1) * LANE], cos_t, sa_t, sb_t).astype(BF16)
        p = _dot_nt(hb, w_ref[ATTN_WIDTH:ATTN_WIDTH + 2 * KV_WIDTH, :])
        k_ref[...] = _rope(p[:, 0:KV_WIDTH], cos_t, sa_t, sb_t).astype(BF16)
        v_ref[...] = p[:, KV_WIDTH:].astype(BF16)
        base = ATTN_WIDTH + 2 * KV_WIDTH
        for j in range(n_conv // COL_CHUNK):
            ubc_ref[:, j * COL_CHUNK:(j + 1) * COL_CHUNK] = _dot_nt(
                hb, w_ref[base + j * COL_CHUNK:base + (j + 1) * COL_CHUNK, :]).astype(BF16)

    row = lambda w: pl.BlockSpec((tm, w), lambda i: (i, 0))
    return _pcall(
        body, name=name, grid=(t // tm,),
        in_specs=[row(d), _full((2, d)), pl.BlockSpec((1, N_MOD, d), lambda i: (i // tpb, 0, 0)),
                  _resident((IN_WIDTH, d)), row(LANE), row(LANE), row(LANE)],
        out_specs=[row(d), row(ATTN_WIDTH), row(KV_WIDTH), row(KV_WIDTH), row(n_conv)],
        out_shape=[jax.ShapeDtypeStruct((t, d), BF16), jax.ShapeDtypeStruct((t, ATTN_WIDTH), BF16),
                   jax.ShapeDtypeStruct((t, KV_WIDTH), BF16), jax.ShapeDtypeStruct((t, KV_WIDTH), BF16),
                   jax.ShapeDtypeStruct((t, n_conv), BF16)],
        args=(xhat, lnp, mod, w_t, cos, sa, sb), comm=comm)


ATTN_TILE_BLOCKS = 2


def _attn_sub_block(s, tile, nblk, kp_ref, kc_ref, vp_ref, vc_ref):
    rows = slice(s * BLOCK, (s + 1) * BLOCK)
    if s == 0:
        first = ((tile * ATTN_TILE_BLOCKS) % nblk) == 0
        return rows, (kp_ref, slice(0, BLOCK)), (kc_ref, rows), (vp_ref, slice(0, BLOCK)), (vc_ref, rows), first
    before = slice((s - 1) * BLOCK, s * BLOCK)
    return rows, (kc_ref, before), (kc_ref, rows), (vc_ref, before), (vc_ref, rows), False


def _attn_group(q_ref, rows, k_prev, k_cur, v_prev, v_cur, sink_ref, g, first):
    lo, hi = g * HEAD_DIM, (g + 1) * HEAD_DIM
    kk = jnp.concatenate([k_prev[0][k_prev[1], lo:hi], k_cur[0][k_cur[1], lo:hi]], axis=0)
    vv = jnp.concatenate([v_prev[0][v_prev[1], lo:hi], v_cur[0][v_cur[1], lo:hi]], axis=0)
    qs = jnp.concatenate([q_ref[rows, (GQA_GROUP * g + j) * HEAD_DIM:(GQA_GROUP * g + j + 1) * HEAD_DIM]
                          for j in range(GQA_GROUP)], axis=0)
    cols = GQA_GROUP * BLOCK
    ki = lax.broadcasted_iota(jnp.int32, (2 * BLOCK, cols), 0)
    col = lax.broadcasted_iota(jnp.int32, (2 * BLOCK, cols), 1)
    diff = (col & (BLOCK - 1)) + BLOCK - ki
    valid = (diff >= 0) & (diff < WINDOW) & ((ki >= BLOCK) | jnp.logical_not(first))
    s = _dot_nt(kk, qs) * (HEAD_DIM ** -0.5)
    s = jnp.where(valid, s, -1e30)
    hcol = lax.broadcasted_iota(jnp.int32, (1, cols), 1)
    sink = jnp.zeros((1, cols), F32)
    for j in range(GQA_GROUP):
        sink = jnp.where(hcol // BLOCK == j, sink_ref[GQA_GROUP * g + j], sink)
    m = jnp.maximum(jnp.max(s, axis=0, keepdims=True), sink)
    p = jnp.exp(s - m)
    ps = jnp.exp(sink - m)
    inv = 1.0 / (jnp.sum(p, axis=0, keepdims=True) + ps)
    return qs, kk, vv, p * inv, ps * inv


def _heads_to_lanes(x_t):
    return jnp.concatenate([x_t[:, j * BLOCK:(j + 1) * BLOCK].T for j in range(GQA_GROUP)], axis=1)


def _attention(q, k, v, sinks, *, seq, name, comm=None):
    t = q.shape[0]
    nblk = seq // BLOCK
    tile = ATTN_TILE_BLOCKS * BLOCK

    def body(q_ref, kp_ref, kc_ref, vp_ref, vc_ref, sink_ref, o_ref):
        for s in range(ATTN_TILE_BLOCKS):
            rows, k_prev, k_cur, v_prev, v_cur, first = _attn_sub_block(
                s, pl.program_id(0), nblk, kp_ref, kc_ref, vp_ref, vc_ref)
            outs = []
            for g in range(N_KV_HEADS):
                _, _, vv, pn, _ = _attn_group(q_ref, rows, k_prev, k_cur, v_prev, v_cur, sink_ref, g, first)
                outs.append(_heads_to_lanes(_dot_tn(vv, pn.astype(BF16))))
            o_ref[rows, :] = jnp.concatenate(outs, axis=1).astype(BF16)

    cur = lambda w: pl.BlockSpec((tile, w), lambda n: (n, 0))
    prev = lambda w: pl.BlockSpec((BLOCK, w), lambda n: (jnp.maximum(n * ATTN_TILE_BLOCKS - 1, 0), 0))
    return _pcall(
        body, name=name, grid=(t // tile,),
        in_specs=[cur(ATTN_WIDTH), prev(KV_WIDTH), cur(KV_WIDTH), prev(KV_WIDTH), cur(KV_WIDTH),
                  pl.BlockSpec(memory_space=pltpu.SMEM)],
        out_specs=cur(ATTN_WIDTH),
        out_shape=jax.ShapeDtypeStruct((t, ATTN_WIDTH), BF16),
        args=(q, k, k, v, v, sinks), comm=comm)


def _out_proj(attn, ubc, cw, wout, xhat_in, lnp_in, mod, *, seq, gate_idx, name, comm=None):
    t, d = xhat_in.shape
    tm = min(512, seq)
    tpb = seq // tm
    cwid = CONV_WIDTH

    def body(attn_ref, ubc_ref, halo_ref, cw_ref, w_ref, x_ref, ln_ref, mod_ref,
             mixin_ref, mix_ref, xhat_ref, rstd_ref, zbuf, acc):
        first = (pl.program_id(0) % tpb) == 0
        u, bg, cg = (ubc_ref[:, s * cwid:(s + 1) * cwid].astype(F32) for s in range(3))
        z = cg * u
        hz = halo_ref[:, 2 * cwid:3 * cwid].astype(F32) * halo_ref[:, 0:cwid].astype(F32)
        zbuf[0:HALO, :] = jnp.where(first, 0.0, hz)
        zbuf[HALO:HALO + tm, :] = z
        y = (cw_ref[0:1, :] * zbuf[HALO - 2:HALO - 2 + tm, :] + cw_ref[1:2, :] * zbuf[HALO - 1:HALO - 1 + tm, :]
             + cw_ref[2:3, :] * z)
        mixin_ref[:, 0:ATTN_WIDTH] = attn_ref[...]
        mixin_ref[:, ATTN_WIDTH:] = (bg * y).astype(BF16)
        mv = mixin_ref[...]
        for j in range(d // COL_CHUNK):
            acc[:, j * COL_CHUNK:(j + 1) * COL_CHUNK] = jnp.dot(
                mv, w_ref[:, j * COL_CHUNK:(j + 1) * COL_CHUNK], preferred_element_type=F32)
        scale = 1.0 + mod_ref[0, gate_idx:gate_idx + 1, :]

        mix = acc[...]
        xhat, rstd = _ln_stats(DN_ALPHA * (x_ref[...] * ln_ref[0:1, :] + ln_ref[1:2, :]) + scale * mix)
        mix_ref[...] = mix.astype(BF16)
        xhat_ref[...] = xhat
        rstd_ref[...] = rstd

    row = lambda w: pl.BlockSpec((tm, w), lambda i: (i, 0))
    return _pcall(
        body, name=name, grid=(t // tm,),
        in_specs=[row(ATTN_WIDTH), row(3 * cwid),
                  pl.BlockSpec((HALO, 3 * cwid), lambda i: (jnp.maximum(i * (tm // HALO) - 1, 0), 0)),
                  _full((8, cwid)), _resident((d, d)), row(d), _full((2, d)),
                  pl.BlockSpec((1, N_MOD, d), lambda i: (i // tpb, 0, 0))],
        out_specs=[row(d), row(d), row(d), row(1)],
        out_shape=[jax.ShapeDtypeStruct((t, d), BF16), jax.ShapeDtypeStruct((t, d), BF16),
                   jax.ShapeDtypeStruct((t, d), F32), jax.ShapeDtypeStruct((t, 1), F32)],
        scratch_shapes=[pltpu.VMEM((tm + HALO, cwid), F32), pltpu.VMEM((tm, d), F32)],
        args=(attn, ubc, ubc, cw, wout, xhat_in, lnp_in, mod), comm=comm)


def _ffn_bwd_act(df, wd, dact, *, seq, name, comm=None):
    t, d = df.shape
    f = wd.shape[0]
    tm = min(512, seq)
    ch = min(COL_CHUNK, f)

    def body(df_ref, wd_ref, dact_ref, dgu_ref):
        dfv = df_ref[...]
        for j in range(f // ch):
            da = _dot_nt(dfv, wd_ref[j * ch:(j + 1) * ch, :])
            dgu_ref[:, j * ch:(j + 1) * ch] = (da * dact_ref[:, j * ch:(j + 1) * ch].astype(F32)).astype(BF16)
            dgu_ref[:, f + j * ch:f + (j + 1) * ch] = (
                da * dact_ref[:, f + j * ch:f + (j + 1) * ch].astype(F32)).astype(BF16)

    return _pcall(
        body, name=name, grid=(t // tm,),
        in_specs=[pl.BlockSpec((tm, d), lambda i: (i, 0)), _resident((f, d)),
                  pl.BlockSpec((tm, 2 * f), lambda i: (i, 0))],
        out_specs=pl.BlockSpec((tm, 2 * f), lambda i: (i, 0)),
        out_shape=jax.ShapeDtypeStruct((t, 2 * f), BF16),
        args=(df, wd, dact), comm=comm)


def _bwd_in(a, w, dr, xin, rstd_prev, lnp_prev, mod, branch_prev, *, seq, w_is_nt, sc_idx, gate_idx,
            branch_scale, final, name, comm=None):
    t, kdim = a.shape
    d = dr.shape[1]
    nb = t // seq
    tm = min(512, seq)
    tpb = seq // tm

    def body(*refs):
        if final:
            a_ref, w_ref, dr_ref, x_ref, mod_ref, dx_ref, dsc_ref, dsh_ref, acc = refs
        else:
            (a_ref, w_ref, dr_ref, x_ref, rstd_ref, ln_ref, mod_ref, br_ref,
             drp_ref, dbr_ref, dsc_ref, dsh_ref, dgate_ref, dg_ref, db_ref, acc) = refs
        i = pl.program_id(0)
        av = a_ref[...]
        for j in range(d // COL_CHUNK):
            cols = slice(j * COL_CHUNK, (j + 1) * COL_CHUNK)
            acc[:, cols] = (_dot_nt(av, w_ref[cols, :]) if w_is_nt
                            else jnp.dot(av, w_ref[:, cols], preferred_element_type=F32))
        sc1 = 1.0 + mod_ref[0, sc_idx:sc_idx + 1, :]
        if not final:
            g_prev, b_prev = ln_ref[0:1, :], ln_ref[1:2, :]
            bscale = branch_scale * (1.0 + mod_ref[0, gate_idx:gate_idx + 1, :])

        def chunk(rows, carry):
            dh = acc[rows, :]
            dx = DN_ALPHA * dr_ref[rows, :] + dh * sc1
            if final:
                dx_ref[rows, :] = dx
                return carry[0] + _fold8(dh * x_ref[rows, :]), carry[1] + _fold8(dh)
            xhat = x_ref[rows, :]
            drp = _ln_bwd(dx, xhat, rstd_ref[rows, :], g_prev)
            drp_ref[rows, :] = drp
            dbr_ref[rows, :] = (bscale * drp).astype(BF16)
            return (carry[0] + _fold8(dh * xhat), carry[1] + _fold8(dh),
                    carry[2] + _fold8(br_ref[rows, :].astype(F32) * drp),
                    carry[3] + _fold8(dx * xhat), carry[4] + _fold8(dx))

        zero = jnp.zeros((8, d), F32)
        sums = list(_row_chunk_loop(tm, chunk, (zero,) * (2 if final else 5)))
        if not final:
            sums[0] = sums[0] * g_prev + sums[1] * b_prev
            sums[2] = sums[2] * branch_scale

        @pl.when((i % tpb) == 0)
        def _():
            dsc_ref[...] = jnp.zeros_like(dsc_ref)
            dsh_ref[...] = jnp.zeros_like(dsh_ref)
            if not final:
                dgate_ref[...] = jnp.zeros_like(dgate_ref)

        dsc_ref[0] += _row_sum(sums[0])
        dsh_ref[0] += _row_sum(sums[1])
        if not final:
            @pl.when(i == 0)
            def _():
                dg_ref[...] = jnp.zeros_like(dg_ref)
                db_ref[...] = jnp.zeros_like(db_ref)

            dgate_ref[0] += _row_sum(sums[2])
            dg_ref[...] += _row_sum(sums[3])
            db_ref[...] += _row_sum(sums[4])

    row = lambda w_: pl.BlockSpec((tm, w_), lambda i: (i, 0))
    vec = pl.BlockSpec((1, 1, d), lambda i: (i // tpb, 0, 0))
    mod_spec = pl.BlockSpec((1, N_MOD, d), lambda i: (i // tpb, 0, 0))
    vshape = jax.ShapeDtypeStruct((nb, 1, d), F32)
    if final:
        in_specs = [row(kdim), _resident(w.shape), row(d), row(d), mod_spec]
        args = (a, w, dr, xin, mod)
        out_specs = [row(d), vec, vec]
        out_shape = [jax.ShapeDtypeStruct((t, d), F32), vshape, vshape]
    else:
        in_specs = [row(kdim), _resident(w.shape), row(d), row(d), row(1), _full((2, d)), mod_spec, row(d)]
        args = (a, w, dr, xin, rstd_prev, lnp_prev, mod, branch_prev)
        out_specs = [row(d), row(d), vec, vec, vec, _full((1, d)), _full((1, d))]
        out_shape = [jax.ShapeDtypeStruct((t, d), F32), jax.ShapeDtypeStruct((t, d), BF16), vshape, vshape, vshape,
                     jax.ShapeDtypeStruct((1, d), F32), jax.ShapeDtypeStruct((1, d), F32)]
    return _pcall(
        body, name=name, grid=(t // tm,), in_specs=in_specs, out_specs=out_specs, out_shape=out_shape,
        scratch_shapes=[pltpu.VMEM((tm, d), F32)], args=args, comm=comm)


def _grad_chip_sum(pos, a, b, *, half_on_rows, name, comm=None):
    t, m = a.shape
    n = b.shape[1]
    tk = min(2048 if half_on_rows else 4096, t)
    nk = t // tk
    half = lambda p, pos_ref: 1 - pos_ref[2] - p + 2 * p * pos_ref[2]
    if half_on_rows:
        n_j = N_CHIPS
        tile = (m // 2, n // n_j)
        a_spec = pl.BlockSpec((tk, tile[0]), lambda p, j, k, pos_ref: (k, half(p, pos_ref)))
        b_spec = pl.BlockSpec((tk, tile[1]), lambda p, j, k, pos_ref: (k, j))
        out_tile = pl.BlockSpec((1, *tile), lambda p, j, k, pos_ref: (0, 0, j * p))
        total = (1, m // 2, n)
    else:
        n_j = 2
        tile = (m // n_j, n // 2)
        a_spec = pl.BlockSpec((tk, tile[0]), lambda p, j, k, pos_ref: (k, j))
        b_spec = pl.BlockSpec((tk, tile[1]), lambda p, j, k, pos_ref: (k, half(p, pos_ref)))
        out_tile = pl.BlockSpec((1, *tile), lambda p, j, k, pos_ref: (0, j * p, 0))
        total = (1, m, n // 2)

    def body(pos_ref, a_ref, b_ref, s32_ref, s16_ref, land_ref, acc, theirs, send_sems, recv_sems, copy_sem):
        p, j, k = pl.program_id(0), pl.program_id(1), pl.program_id(2)
        x, y, c = _position()

        def push(jj):
            return pltpu.make_async_remote_copy(
                src_ref=acc.at[jj], dst_ref=land_ref.at[jj], send_sem=send_sems.at[jj], recv_sem=recv_sems.at[jj],
                device_id=(x, y, 1 - c), device_id_type=MESH)

        fetch = pltpu.make_async_copy(land_ref.at[j], theirs, copy_sem)

        @pl.when(jnp.logical_and(p == 1, k == 0))
        def _():
            push(j).wait_send()
            push(j).wait_recv()
            fetch.start()

        part = _dot_tn(a_ref[...], b_ref[...])

        @pl.when(k == 0)
        def _():
            acc[j] = part

        @pl.when(k > 0)
        def _():
            acc[j] += part

        @pl.when(jnp.logical_and(p == 0, k == nk - 1))
        def _():
            push(j).start()

        @pl.when(jnp.logical_and(p == 1, k == nk - 1))
        def _():
            fetch.wait()
            s = acc[j] + theirs[...]
            s32_ref[0] = s
            s16_ref[0] = s.astype(BF16)

    out = _pcall(
        body, name=name, grid=(2, n_j, nk), in_specs=[a_spec, b_spec], out_specs=[out_tile, out_tile, ANY_SPEC],
        out_shape=[jax.ShapeDtypeStruct(total, F32), jax.ShapeDtypeStruct(total, BF16),
                   jax.ShapeDtypeStruct((n_j, *tile), F32)],
        scratch_shapes=[pltpu.VMEM((n_j, *tile), F32), pltpu.VMEM(tile, F32),
                        pltpu.SemaphoreType.DMA((n_j,)), pltpu.SemaphoreType.DMA((n_j,)), pltpu.SemaphoreType.DMA],
        args=(a, b), prefetch=pos, comm=comm)
    if comm is None:
        return out[0], out[1]
    (s32, s16, _), extra = out
    return (s32, s16), extra


def _matmul_nt_bf16(a, w, *, seq, name):
    t, kdim = a.shape
    n = w.shape[0]
    tm = min(512, seq)

    def body(a_ref, w_ref, o_ref):
        av = a_ref[...]
        for j in range(n // COL_CHUNK):
            o_ref[:, j * COL_CHUNK:(j + 1) * COL_CHUNK] = _dot_nt(
                av, w_ref[j * COL_CHUNK:(j + 1) * COL_CHUNK, :]).astype(BF16)

    return pl.pallas_call(
        body, name=name, grid=(t // tm,),
        in_specs=[pl.BlockSpec((tm, kdim), lambda i: (i, 0)), _resident((n, kdim))],
        out_specs=pl.BlockSpec((tm, n), lambda i: (i, 0)),
        out_shape=jax.ShapeDtypeStruct((t, n), BF16),
        compiler_params=_params(("arbitrary",)),
    )(a, w)


def _attention_bwd(q, k, v, dmixin, sinks, *, seq, name, comm=None):
    t = q.shape[0]
    nblk = seq // BLOCK
    tile = ATTN_TILE_BLOCKS * BLOCK

    def body(q_ref, kp_ref, kc_ref, vp_ref, vc_ref, do_ref, sink_ref,
             dq_ref, dkp_ref, dkc_ref, dvp_ref, dvc_ref, dsink_ref):
        n = pl.program_id(0)

        @pl.when(n == 0)
        def _():
            dsink_ref[...] = jnp.zeros_like(dsink_ref)

        srow = lax.broadcasted_iota(jnp.int32, (8, LANE), 0)
        dsink = jnp.zeros((8, LANE), F32)
        for s in range(ATTN_TILE_BLOCKS):
            rows, k_prev, k_cur, v_prev, v_cur, first = _attn_sub_block(s, n, nblk, kp_ref, kc_ref, vp_ref, vc_ref)
            dqs, dks, dvs = [], [], []
            for g in range(N_KV_HEADS):
                qs, kk, vv, pn, psn = _attn_group(q_ref, rows, k_prev, k_cur, v_prev, v_cur, sink_ref, g, first)
                dos = jnp.concatenate(
                    [do_ref[rows, (GQA_GROUP * g + j) * HEAD_DIM:(GQA_GROUP * g + j + 1) * HEAD_DIM]
                     for j in range(GQA_GROUP)], axis=0)
                dp = _dot_nt(vv, dos)
                delta = jnp.sum(pn * dp, axis=0, keepdims=True)
                ds = pn * (dp - delta)
                dsk = psn * delta
                for j in range(GQA_GROUP):
                    tot = jnp.sum(dsk[:, j * BLOCK:(j + 1) * BLOCK], axis=1, keepdims=True)
                    dsink = dsink - jnp.where(srow == GQA_GROUP * g + j, tot, 0.0)
                dsb = (ds * (HEAD_DIM ** -0.5)).astype(BF16)
                dqs.append(_heads_to_lanes(_dot_tn(kk, dsb)))
                dks.append(jnp.dot(dsb, qs, preferred_element_type=F32))
                dvs.append(jnp.dot(pn.astype(BF16), dos, preferred_element_type=F32))
            dq_ref[rows, :] = jnp.concatenate(dqs, axis=1)
            dkp_ref[rows, :] = jnp.concatenate([x[0:BLOCK, :] for x in dks], axis=1)
            dkc_ref[rows, :] = jnp.concatenate([x[BLOCK:, :] for x in dks], axis=1)
            dvp_ref[rows, :] = jnp.concatenate([x[0:BLOCK, :] for x in dvs], axis=1)
            dvc_ref[rows, :] = jnp.concatenate([x[BLOCK:, :] for x in dvs], axis=1)
        dsink_ref[...] += dsink

    cur = lambda w: pl.BlockSpec((tile, w), lambda n: (n, 0))
    prev = lambda w: pl.BlockSpec((BLOCK, w), lambda n: (jnp.maximum(n * ATTN_TILE_BLOCKS - 1, 0), 0))
    kv = jax.ShapeDtypeStruct((t, KV_WIDTH), F32)
    return _pcall(
        body, name=name, grid=(t // tile,),
        in_specs=[cur(ATTN_WIDTH), prev(KV_WIDTH), cur(KV_WIDTH), prev(KV_WIDTH), cur(KV_WIDTH), cur(ATTN_WIDTH),
                  pl.BlockSpec(memory_space=pltpu.SMEM)],
        out_specs=[cur(ATTN_WIDTH), cur(KV_WIDTH), cur(KV_WIDTH), cur(KV_WIDTH), cur(KV_WIDTH), _full((8, LANE))],
        out_shape=[jax.ShapeDtypeStruct((t, ATTN_WIDTH), F32), kv, kv, kv, kv, jax.ShapeDtypeStruct((8, LANE), F32)],
        args=(q, k, k, v, v, dmixin, sinks), comm=comm)


def _mix_bwd_assemble(dq, dkp, dkc, dvp, dvc, cos, sa, sb, dmixin, ubc, cw, *, seq, name, comm=None):
    t = dq.shape[0]
    cwid = CONV_WIDTH
    tm = min(2 * BLOCK, seq)
    tiles_per_seq = seq // tm
    ntile = t // tm
    nblk_all = t // BLOCK
    per_tile = tm // BLOCK

    def body(*refs):
        dq_ref, dkc_ref, dvc_ref = refs[0:3]
        dkp_refs, dvp_refs = refs[3:3 + per_tile], refs[3 + per_tile:3 + 2 * per_tile]
        (cos_ref, sa_ref, sb_ref, dco_ref, dcon_ref, ubc_ref, hprev_ref, hnext_ref, cw_ref,
         dproj_ref, dcw_ref, zbuf, dybuf) = refs[3 + 2 * per_tile:]
        i = pl.program_id(0)
        first = (i % tiles_per_seq) == 0
        last = (i % tiles_per_seq) == tiles_per_seq - 1
        glast = i == ntile - 1

        @pl.when(i == 0)
        def _():
            dcw_ref[...] = jnp.zeros_like(dcw_ref)

        def with_next_block(cur_ref, nxt_refs):
            nxt = [r[...] for r in nxt_refs]
            nxt[-1] = jnp.where(glast, 0.0, nxt[-1])
            return cur_ref[...] + jnp.concatenate(nxt, axis=0)

        cos_t, sa_t, sb_t = cos_ref[...], sa_ref[...], sb_ref[...]
        for j in range(ATTN_WIDTH // LANE):
            dproj_ref[:, j * LANE:(j + 1) * LANE] = _rope_t(
                dq_ref[:, j * LANE:(j + 1) * LANE], cos_t, sa_t, sb_t).astype(BF16)
        dk = with_next_block(dkc_ref, dkp_refs)
        dproj_ref[:, ATTN_WIDTH:ATTN_WIDTH + KV_WIDTH] = _rope_t(dk, cos_t, sa_t, sb_t).astype(BF16)
        dv = with_next_block(dvc_ref, dvp_refs)
        dproj_ref[:, ATTN_WIDTH + KV_WIDTH:ATTN_WIDTH + 2 * KV_WIDTH] = dv.astype(BF16)

        u, bg, cg = (ubc_ref[:, s * cwid:(s + 1) * cwid].astype(F32) for s in range(3))
        z = cg * u
        hz = hprev_ref[:, 2 * cwid:3 * cwid].astype(F32) * hprev_ref[:, 0:cwid].astype(F32)
        zbuf[0:HALO, :] = jnp.where(first, 0.0, hz)
        zbuf[HALO:HALO + tm, :] = z
        z2, z1 = zbuf[HALO - 2:HALO - 2 + tm, :], zbuf[HALO - 1:HALO - 1 + tm, :]
        w0, w1, w2 = cw_ref[0:1, :], cw_ref[1:2, :], cw_ref[2:3, :]
        y = w0 * z2 + w1 * z1 + w2 * z
        dco = dco_ref[...].astype(F32)
        dyc = dco * bg
        dyn = dcon_ref[...].astype(F32) * hnext_ref[:, cwid:2 * cwid].astype(F32)
        dybuf[0:tm, :] = dyc
        dybuf[tm:tm + HALO, :] = jnp.where(last, 0.0, dyn)
        dz = w2 * dyc + w1 * dybuf[1:1 + tm, :] + w0 * dybuf[2:2 + tm, :]
        srow = lax.broadcasted_iota(jnp.int32, (8, cwid), 0)
        dcw_ref[...] += (jnp.where(srow == 0, _row_sum(dyc * z2), 0.0) + jnp.where(srow == 1, _row_sum(dyc * z1), 0.0)
                         + jnp.where(srow == 2, _row_sum(dyc * z), 0.0))
        base = ATTN_WIDTH + 2 * KV_WIDTH
        dproj_ref[:, base:base + cwid] = (dz * cg).astype(BF16)
        dproj_ref[:, base + cwid:base + 2 * cwid] = (dco * y).astype(BF16)
        dproj_ref[:, base + 2 * cwid:base + 3 * cwid] = (dz * u).astype(BF16)

    cur = lambda w: pl.BlockSpec((tm, w), lambda i: (i, 0))
    nxt = [pl.BlockSpec((BLOCK, KV_WIDTH), lambda i, s=s: (jnp.minimum(i * per_tile + s + 1, nblk_all - 1), 0))
           for s in range(per_tile)]
    prev_halo = pl.BlockSpec((HALO, 3 * cwid), lambda i: (jnp.maximum(i * (tm // HALO) - 1, 0), 0))
    next_halo = lambda w, col: pl.BlockSpec(
        (HALO, w), lambda i: (jnp.minimum((i + 1) * (tm // HALO), t // HALO - 1), col))
    return _pcall(
        body, name=name, grid=(ntile,),
        in_specs=[cur(ATTN_WIDTH), cur(KV_WIDTH), cur(KV_WIDTH), *nxt, *nxt,
                  cur(LANE), cur(LANE), cur(LANE),
                  pl.BlockSpec((tm, cwid), lambda i: (i, 1)), next_halo(cwid, 1),
                  cur(3 * cwid), prev_halo, next_halo(3 * cwid, 0), _full((8, cwid))],
        out_specs=[cur(IN_WIDTH), _full((8, cwid))],
        out_shape=[jax.ShapeDtypeStruct((t, IN_WIDTH), BF16), jax.ShapeDtypeStruct((8, cwid), F32)],
        scratch_shapes=[pltpu.VMEM((tm + HALO, cwid), F32), pltpu.VMEM((tm + HALO, cwid), F32)],
        args=(dq, dkc, dvc, *([dkp] * per_tile), *([dvp] * per_tile), cos, sa, sb, dmixin, dmixin,
              ubc, ubc, ubc, cw), comm=comm)


def _ada_fwd(c_all, w_ada, b_ada_shard, *, name, comm=None):
    nb, d = c_all.shape
    n = w_ada.shape[1]
    tn = n // 2

    def body(c_ref, w_ref, b_ref, o_ref):
        cv = c_ref[...]
        cond = cv * _sigmoid(cv)
        o_ref[...] = jnp.dot(cond, w_ref[...], preferred_element_type=F32,
                             precision=lax.Precision.HIGHEST) + b_ref[...]

    return _pcall(
        body, name=name, grid=(n // tn,),
        in_specs=[_full((nb, d)), pl.BlockSpec((d, tn), lambda j: (0, j)), pl.BlockSpec((1, tn), lambda j: (0, j))],
        out_specs=pl.BlockSpec((nb, tn), lambda j: (0, j)),
        out_shape=jax.ShapeDtypeStruct((nb, n), F32), args=(c_all, w_ada, b_ada_shard), comm=comm)


def _small_finish(gathered, dmod_all, dmod_shard, c_all_t, *, name):
    d = D_MODEL
    nb, n = dmod_shard.shape

    def body(g_ref, dm_ref, dms_ref, ct_ref, sum_ref, gw_ref, gb_ref):
        total = g_ref[0]
        for dev in range(1, N_DEV):
            total = total + g_ref[dev]
        sum_ref[...] = total
        gb_ref[...] = _row_sum(dm_ref[...])
        ctv = ct_ref[...]
        cond_t = ctv * _sigmoid(ctv)
        for jb in range(n // COL_CHUNK):
            gw_ref[:, jb * COL_CHUNK:(jb + 1) * COL_CHUNK] = jnp.dot(
                cond_t, dms_ref[:, jb * COL_CHUNK:(jb + 1) * COL_CHUNK], preferred_element_type=F32,
                precision=lax.Precision.HIGHEST)

    return pl.pallas_call(
        body, name=name, grid=(1,),
        in_specs=[_full((N_DEV, SMALL_ROWS, d)), _full((nb, N_MOD * d)), _full((nb, n)), _full((d, nb))],
        out_specs=[_full((SMALL_ROWS, d)), _full((d, n)), _full((1, N_MOD * d))],
        out_shape=[jax.ShapeDtypeStruct((SMALL_ROWS, d), F32), jax.ShapeDtypeStruct((d, n), F32),
                   jax.ShapeDtypeStruct((1, N_MOD * d), F32)],
        compiler_params=_params(("arbitrary",)),
    )(gathered, dmod_all, dmod_shard, c_all_t)


def _row_tile(r, c, budget=1 << 21):
    if r * c * 4 <= budget or r % 16:
        return r
    best = 16
    for tr in range(16, r + 1, 16):
        if r % tr == 0 and tr * c * 4 <= budget:
            best = tr
    return best


def _cast_into(w, chip, col_kind, *, name):
    r, c = w.shape
    tr = _row_tile(r, c)

    def body(chip_ref, w_ref, o_ref):
        o_ref[...] = w_ref[...].astype(BF16)

    if col_kind:
        out_spec = pl.BlockSpec((tr, c), lambda i, chip_ref: (i, chip_ref[0]))
        out_shape = jax.ShapeDtypeStruct((r, c * N_CHIPS), BF16)
    else:
        out_spec = pl.BlockSpec((tr, c), lambda i, chip_ref: (chip_ref[0] * (r // tr) + i, 0))
        out_shape = jax.ShapeDtypeStruct((r * N_CHIPS, c), BF16)
    return _pcall(body, name=name, grid=(r // tr,), in_specs=[pl.BlockSpec((tr, c), lambda i, chip_ref: (i, 0))],
                  out_specs=out_spec, out_shape=out_shape, args=(w,), prefetch=chip)


def _adamw(w, g, m, v, *, name, comm=None):
    r, c = w.shape
    tr = _row_tile(r, c)
    c1 = 1.0 - ADAM_B1 ** ADAM_STEP
    c2 = 1.0 - ADAM_B2 ** ADAM_STEP

    def body(w_ref, g_ref, m_ref, v_ref, d_ref, nm_ref, nv_ref):
        gv = g_ref[...]
        m2 = ADAM_B1 * m_ref[...] + (1.0 - ADAM_B1) * gv
        v2 = ADAM_B2 * v_ref[...] + (1.0 - ADAM_B2) * (gv * gv)
        d_ref[...] = -ADAM_LR * ((m2 / c1) / (jnp.sqrt(v2 / c2) + ADAM_EPS) + ADAM_WD * w_ref[...])
        nm_ref[...] = m2
        nv_ref[...] = v2

    spec = pl.BlockSpec((tr, c), lambda i: (i, 0))
    sh = jax.ShapeDtypeStruct((r, c), F32)
    return _pcall(body, name=name, grid=(r // tr,), in_specs=[spec] * 4, out_specs=[spec] * 3, out_shape=[sh] * 3,
                  args=(w, g, m, v), comm=comm)


def _sum_final(pos, s32, recv, *, col_kind, n_shard, name, comm=None):
    def body(pos_ref, s_ref, r_ref, o_ref):
        total = ((s_ref[0] + r_ref[0].astype(F32)) + r_ref[1].astype(F32)) + r_ref[2].astype(F32)
        if col_kind:
            o_ref[0] = total
        else:
            o_ref[...] = total

    if col_kind:
        rows, cols = s32.shape[1], n_shard
        tr = _row_tile(rows, cols)
        own = pl.BlockSpec((1, tr, cols), lambda i, pos: (0, i, 2 * pos[0] + pos[1]))
        out_spec = pl.BlockSpec((1, tr, cols), lambda i, pos: (pos[2], i, 0))
        out_shape = jax.ShapeDtypeStruct((2, rows, cols), F32)
    else:
        rows, cols = n_shard, s32.shape[2]
        tr = _row_tile(rows, cols)
        own = pl.BlockSpec((1, tr, cols), lambda i, pos: (0, (2 * pos[0] + pos[1]) * (rows // tr) + i, 0))
        out_spec = pl.BlockSpec((tr, cols), lambda i, pos: (i, pos[2]))
        out_shape = jax.ShapeDtypeStruct((rows, 2 * cols), F32)
    return _pcall(
        body, name=name, grid=(rows // tr,),
        in_specs=[own, pl.BlockSpec((3, tr, cols), lambda i, pos: (0, i, 0))], out_specs=out_spec,
        out_shape=out_shape, args=(s32, recv), prefetch=pos, comm=comm)


def _position():
    return lax.axis_index("x"), lax.axis_index("y"), lax.axis_index("c")


def _allgather8(x_shard, *, name, comm=None):
    m_per, n = x_shard.shape
    nci, nco = (0, 0) if comm is None else (len(comm.inputs), len(comm.out_shapes))

    def body(*refs):
        x_ref, refs = refs[0], refs[1:]
        cin, refs = refs[:nci], refs[nci:]
        out_ref, refs = refs[0], refs[1:]
        cout, refs = refs[:nco], refs[nco:]
        (send_sems, recv_sems, local_sem), csems = refs[:3], refs[3:]
        x, y, c = _position()
        me, sibling = (x, y, c), (x, y, 1 - c)
        chips = [(1 - x, y), (x, 1 - y), (1 - x, 1 - y)]

        def rows(px, py, pc):
            return out_ref.at[pl.ds((4 * px + 2 * py + pc) * m_per, m_per), :]

        def copy(k, block, to, src=None):
            return pltpu.make_async_remote_copy(
                src_ref=rows(*block) if src is None else src, dst_ref=rows(*block),
                send_sem=send_sems.at[k], recv_sem=recv_sems.at[k], device_id=to, device_id_type=MESH)

        mine = pltpu.make_async_copy(x_ref, rows(*me), local_sem)
        mine.start()
        first = [copy(0, me, sibling, src=x_ref)]
        first += [copy(1 + j, me, (*chip, c), src=x_ref) for j, chip in enumerate(chips)]
        for cp in first:
            cp.start()
        if comm is not None:
            comm.start(cin, cout, csems)
        passed = [copy(4 + j, (*chip, c), sibling) for j, chip in enumerate(chips)]
        for j, chip in enumerate(chips):
            copy(1 + j, (*chip, c), me).wait_recv()
            passed[j].start()
        copy(0, sibling, me).wait_recv()
        for j, chip in enumerate(chips):
            copy(4 + j, (*chip, 1 - c), me).wait_recv()
        for cp in first + passed:
            cp.wait_send()
        mine.wait()
        if comm is not None:
            comm.middle(cin, cout, csems)
            comm.late(cin, cout, csems)
            comm.finish(cin, cout, csems)

    vmem = pl.BlockSpec(memory_space=pltpu.VMEM)
    sems = [pltpu.SemaphoreType.DMA((7,)), pltpu.SemaphoreType.DMA((7,)), pltpu.SemaphoreType.DMA]
    out = jax.ShapeDtypeStruct((N_DEV * m_per, n), x_shard.dtype)
    if comm is None:
        return pl.pallas_call(body, name=name, out_shape=out, in_specs=[vmem], out_specs=vmem,
                              scratch_shapes=sems)(x_shard)
    res = pl.pallas_call(
        body, name=name, out_shape=[out] + list(comm.out_shapes), in_specs=[vmem] + [ANY_SPEC] * nci,
        out_specs=[vmem] + [ANY_SPEC] * nco, scratch_shapes=sems + list(comm.sems),
        input_output_aliases={1 + i: 1 + o for i, o in comm.aliases.items()})(x_shard, *comm.inputs)
    return res[0], list(res[1:])


def _peer_chips(x, y):
    return [(1 - x, y), (x, 1 - y), (1 - x, 1 - y)]


class _GatherJob:
    def __init__(self, pieces):
        self.pieces = pieces
        n_p = len(pieces)
        self.inputs = [p[0] for p in pieces]
        self.out_shapes = [jax.ShapeDtypeStruct(p[0].shape, p[0].dtype) for p in pieces]
        for buf, col_kind, r0, nr in pieces:
            half_rows = buf.shape[0] // (2 if col_kind else 2 * N_CHIPS)
            assert r0 % 16 == 0 and nr % 16 == 0 and nr >= 32 and r0 + nr <= half_rows, (buf.shape, r0, nr)
        self.aliases = {p: p for p in range(n_p)}
        dma = pltpu.SemaphoreType.DMA
        self.sems = [dma((2 * n_p,))] * 4 + [dma((4 * n_p,))] * 2

    def _region(self, cout, p, chip_idx, half, part=None):
        buf, col_kind, r0, nr = self.pieces[p]
        first = -(-nr // 32) * 16
        if part == 0:
            nr = first
        elif part == 1:
            r0, nr = r0 + first, nr - first
        if col_kind:
            n = buf.shape[1] // N_CHIPS
            return cout[p].at[pl.ds(half * (buf.shape[0] // 2) + r0, nr), pl.ds(chip_idx * n, n)]
        n = buf.shape[0] // N_CHIPS
        return cout[p].at[pl.ds(chip_idx * n + half * (n // 2) + r0, nr), :]

    def _copies(self, cout, sems):
        send1, recv1, send2, recv2, fsend, frecv = sems
        x, y, c = _position()
        k = 2 * x + y
        sibling = (x, y, 1 - c)
        x_nbr, y_nbr, diag = _peer_chips(x, y)
        chip_of = lambda ch: 2 * ch[0] + ch[1]

        def remote(region, ssem, rsem, to):
            return pltpu.make_async_remote_copy(src_ref=region, dst_ref=region, send_sem=ssem, recv_sem=rsem,
                                                device_id=to, device_id_type=MESH)

        hop1, arrived1, hop2, arrived2, fwds, fwd_arrived = [], [], [], [], [], []
        for p in range(len(self.pieces)):
            for j, nbr in enumerate((x_nbr, y_nbr)):
                i1 = 2 * p + j
                hop1.append(remote(self._region(cout, p, k, c), send1.at[i1], recv1.at[i1], (*nbr, c)))
                arrived1.append(remote(self._region(cout, p, chip_of(nbr), c), send1.at[i1], recv1.at[i1], (*nbr, c)))
            hop2.append(remote(self._region(cout, p, chip_of(x_nbr), c, 0), send2.at[2 * p], recv2.at[2 * p],
                               (*y_nbr, c)))
            hop2.append(remote(self._region(cout, p, chip_of(y_nbr), c, 1), send2.at[2 * p + 1], recv2.at[2 * p + 1],
                               (*x_nbr, c)))
            arrived2.append(remote(self._region(cout, p, chip_of(diag), c, 0), send2.at[2 * p], recv2.at[2 * p],
                                   (*y_nbr, c)))
            arrived2.append(remote(self._region(cout, p, chip_of(diag), c, 1), send2.at[2 * p + 1],
                                   recv2.at[2 * p + 1], (*x_nbr, c)))
            landed = [(chip_of(x_nbr), None), (chip_of(y_nbr), None), (chip_of(diag), 0), (chip_of(diag), 1)]
            for q, (chip_idx, part) in enumerate(landed):
                i3 = 4 * p + q
                fwds.append(remote(self._region(cout, p, chip_idx, c, part), fsend.at[i3], frecv.at[i3], sibling))
                fwd_arrived.append(remote(self._region(cout, p, chip_idx, 1 - c, part), fsend.at[i3], frecv.at[i3],
                                          sibling))
        return hop1, arrived1, hop2, arrived2, fwds, fwd_arrived

    def start(self, cin, cout, sems):
        for cp in self._copies(cout, sems)[0]:
            cp.start()

    def middle(self, cin, cout, sems):
        _, arrived1, hop2, _, fwds, _ = self._copies(cout, sems)
        for p in range(len(self.pieces)):
            for j in range(2):
                arrived1[2 * p + j].wait_recv()
                hop2[2 * p + j].start()
                fwds[4 * p + j].start()

    def late(self, cin, cout, sems):
        _, _, _, arrived2, fwds, _ = self._copies(cout, sems)
        for p in range(len(self.pieces)):
            for j in range(2):
                arrived2[2 * p + j].wait_recv()
                fwds[4 * p + 2 + j].start()

    def finish(self, cin, cout, sems):
        hop1, _, hop2, _, fwds, fwd_arrived = self._copies(cout, sems)
        for cp in fwd_arrived:
            cp.wait_recv()
        for cp in hop1 + hop2 + fwds:
            cp.wait_send()


class _PairedJob:
    aliases = {}

    def start(self, cin, cout, sems):
        for cp in self._copies(cin, cout, sems):
            cp.start()

    def middle(self, cin, cout, sems):
        pass

    late = middle

    def finish(self, cin, cout, sems):
        copies = self._copies(cin, cout, sems)
        for cp in copies:
            cp.wait_recv()
        for cp in copies:
            cp.wait_send()


class _ExchangeJob(_PairedJob):
    def __init__(self, s16, kinds, sizes):
        self.inputs, self.kinds, self.sizes = list(s16), list(kinds), list(sizes)
        self.out_shapes = [jax.ShapeDtypeStruct((3, s.shape[1], n) if kd else (3, n, s.shape[2]), s.dtype)
                           for s, kd, n in zip(s16, kinds, sizes)]
        self.sems = [pltpu.SemaphoreType.DMA((3 * len(s16),)), pltpu.SemaphoreType.DMA((3 * len(s16),))]

    def _copies(self, cin, cout, sems):
        send_sems, recv_sems = sems
        x, y, c = _position()
        copies = []
        for p, src_ref in enumerate(cin):
            for j, chip in enumerate(_peer_chips(x, y)):
                kk = 2 * chip[0] + chip[1]
                n = self.sizes[p]
                src = src_ref.at[0, :, pl.ds(kk * n, n)] if self.kinds[p] else src_ref.at[0, pl.ds(kk * n, n), :]
                copies.append(pltpu.make_async_remote_copy(
                    src_ref=src, dst_ref=cout[p].at[j], send_sem=send_sems.at[3 * p + j],
                    recv_sem=recv_sems.at[3 * p + j], device_id=(*chip, c), device_id_type=MESH))
        return copies


class _ShareJob:
    def __init__(self, halves):
        self.inputs = list(halves)
        self.out_shapes = [jax.ShapeDtypeStruct(h.shape, h.dtype) for h in halves]
        self.aliases = {p: p for p in range(len(halves))}
        self.sems = [pltpu.SemaphoreType.DMA((len(halves),)), pltpu.SemaphoreType.DMA((len(halves),))]

    def _copies(self, cout, sems, half):
        send_sems, recv_sems = sems
        x, y, c = _position()
        h = c if half == "mine" else 1 - c

        def region(o):
            if len(o.shape) == 3:
                return o.at[h]
            hc = o.shape[1] // 2
            return o.at[:, pl.ds(h * hc, hc)]

        return [pltpu.make_async_remote_copy(
            src_ref=region(o), dst_ref=region(o), send_sem=send_sems.at[p], recv_sem=recv_sems.at[p],
            device_id=(x, y, 1 - c), device_id_type=MESH) for p, o in enumerate(cout)]

    def start(self, cin, cout, sems):
        for cp in self._copies(cout, sems, "mine"):
            cp.start()

    def middle(self, cin, cout, sems):
        pass

    late = middle

    def finish(self, cin, cout, sems):
        for cp in self._copies(cout, sems, "theirs"):
            cp.wait_recv()
        for cp in self._copies(cout, sems, "mine"):
            cp.wait_send()


class _MultiJob:
    def __init__(self, jobs):
        self.jobs = jobs
        self.inputs = [a for j in jobs for a in j.inputs]
        self.out_shapes = [s for j in jobs for s in j.out_shapes]
        self.sems = [s for j in jobs for s in j.sems]
        self.aliases = {}
        i0 = o0 = 0
        for j in jobs:
            for i, o in j.aliases.items():
                self.aliases[i0 + i] = o0 + o
            i0 += len(j.inputs)
            o0 += len(j.out_shapes)

    def _parts(self, cin, cout, sems):
        i0 = o0 = s0 = 0
        for j in self.jobs:
            ni, no, ns = len(j.inputs), len(j.out_shapes), len(j.sems)
            yield j, cin[i0:i0 + ni], cout[o0:o0 + no], sems[s0:s0 + ns]
            i0, o0, s0 = i0 + ni, o0 + no, s0 + ns

    def start(self, cin, cout, sems):
        for j, a, b, s in self._parts(cin, cout, sems):
            j.start(a, b, s)

    def middle(self, cin, cout, sems):
        for j, a, b, s in self._parts(cin, cout, sems):
            j.middle(a, b, s)

    def late(self, cin, cout, sems):
        for j, a, b, s in self._parts(cin, cout, sems):
            j.late(a, b, s)

    def finish(self, cin, cout, sems):
        for j, a, b, s in self._parts(cin, cout, sems):
            j.finish(a, b, s)


def _rope_tables(positions):
    half = ROT_DIM // 2
    inv_freq = jnp.power(jnp.float32(ROPE_THETA), -jnp.arange(0, ROT_DIM, 2, dtype=F32) / ROT_DIM)
    inv_head = jnp.concatenate([inv_freq, inv_freq, jnp.zeros((HEAD_DIM - ROT_DIM,), F32)])
    inv_lane = jnp.concatenate([inv_head] * (LANE // HEAD_DIM))
    ang = positions.astype(F32).reshape(-1)[:, None] * inv_lane[None, :]
    sin = jnp.sin(ang)
    dim = jnp.arange(LANE) % HEAD_DIM
    return jnp.cos(ang), jnp.where(dim < half, -sin, 0.0), jnp.where(dim >= half, sin, 0.0)


def kernel(x, c, positions, w_ada, b_ada, ffn1_w_gate_up, ffn1_w_down, ln1_g, ln1_b, w_in, conv_w, attn_sinks, w_out, ln2_g, ln2_b, ffn2_w_gate_up, ffn2_w_down, ln3_g, ln3_b, loss_target, m_w_ada, m_b_ada, m_ffn1_w_gate_up, m_ffn1_w_down, m_ln1_g, m_ln1_b, m_w_in, m_conv_w, m_attn_sinks, m_w_out, m_ln2_g, m_ln2_b, m_ffn2_w_gate_up, m_ffn2_w_down, m_ln3_g, m_ln3_b, v_w_ada, v_b_ada, v_ffn1_w_gate_up, v_ffn1_w_down, v_ln1_g, v_ln1_b, v_w_in, v_conv_w, v_attn_sinks, v_w_out, v_ln2_g, v_ln2_b, v_ffn2_w_gate_up, v_ffn2_w_down, v_ln3_g, v_ln3_b):
    d = D_MODEL
    nb, seq, _ = x.shape
    t = nb * seq
    f = ffn1_w_down.shape[1] * N_CHIPS
    ax, ay, ac = _position()
    chip = 2 * ax + ay
    dev = 2 * chip + ac
    pos = jnp.stack([ax, ay, ac]).astype(jnp.int32)

    x2 = x.reshape(t, d)
    tgt2 = loss_target.reshape(t, d)
    ln1 = jnp.concatenate([ln1_g, ln1_b], axis=0)
    ln2 = jnp.concatenate([ln2_g, ln2_b], axis=0)
    ln3 = jnp.concatenate([ln3_g, ln3_b], axis=0)
    sinks = attn_sinks.reshape(N_Q_HEADS)
    cos_t, sa_t, sb_t = _rope_tables(positions)

    gu_cuts = [0, 176, 352, d // 2]
    gu_part = lambda buf, s: (buf, True, gu_cuts[s], gu_cuts[s + 1] - gu_cuts[s])
    chip_arr = jnp.reshape(chip, (1,)).astype(jnp.int32)
    b_gu1 = _cast_into(ffn1_w_gate_up[0], chip_arr, True, name="cast_gu1")

    n_ada = w_ada.shape[2]
    c_all, (b_gu1,) = _allgather8(c.reshape(nb * d // LANE, LANE), name="gather_c", comm=_GatherJob([gu_part(b_gu1, 0)]))
    c_all = c_all.reshape(N_DEV * nb, d)
    b_shard = lax.dynamic_slice(b_ada, (0, chip * n_ada), (1, n_ada))
    mod_part, (b_gu1,) = _ada_fwd(c_all, w_ada[0], b_shard, name="ada_fwd", comm=_GatherJob([gu_part(b_gu1, 1)]))
    conv_rows = jnp.pad(conv_w[0], ((0, 5), (0, n_ada - conv_w.shape[2])))
    part = jnp.concatenate([mod_part, conv_rows], axis=0)
    parts, (wgu1,) = _allgather8(part, name="gather_mod", comm=_GatherJob([gu_part(b_gu1, 2)]))
    parts = parts.reshape(N_DEV, N_DEV * nb + 8, n_ada)
    mod_all = jnp.concatenate([parts[2 * k, :N_DEV * nb, :] for k in range(N_CHIPS)], axis=1)
    mod = lax.dynamic_slice(mod_all, (dev * nb, 0), (nb, N_MOD * d)).reshape(nb, N_MOD, d)
    cw_full = jnp.concatenate([parts[2 * k, N_DEV * nb:, :conv_w.shape[2]] for k in range(N_CHIPS)], axis=1)

    b_d1 = _cast_into(ffn1_w_down[0], chip_arr, False, name="cast_d1")
    b_in = _cast_into(w_in[0].T, chip_arr, False, name="cast_in")
    b_out = _cast_into(w_out[0], chip_arr, False, name="cast_out")
    b_gu2 = _cast_into(ffn2_w_gate_up[0], chip_arr, True, name="cast_gu2")
    b_d2 = _cast_into(ffn2_w_down[0], chip_arr, False, name="cast_d2")
    n_gu, n_d, n_in, n_out = (ffn1_w_gate_up.shape[2], ffn1_w_down.shape[1], w_in.shape[2], w_out.shape[1])

    def whole(buf, col_kind):
        return (buf, col_kind, 0, buf.shape[0] // (2 if col_kind else 2 * N_CHIPS))

    (h1, a1, dact1), (wd1, wout) = _ffn_up(x2, ln1, mod, wgu1, seq=seq, sc_idx=1, sh_idx=0, use_ln=False,
                                         name="ffn1_up", comm=_GatherJob([whole(b_d1, False), whole(b_out, False)]))
    (f1, xhat1, rstd1), (win_t,) = _ffn_down_ln(a1, wd1, x2, ln1, mod, seq=seq, gate_idx=2, use_ln=False,
                                                name="ffn1_down", comm=_GatherJob([whole(b_in, False)]))
    (h2, q, k, v, ubc), (b_gu2,) = _in_proj(
        xhat1, ln1, mod, win_t, cos_t, sa_t, sb_t, seq=seq, sc_idx=4, sh_idx=3, name="in_proj",
        comm=_GatherJob([gu_part(b_gu2, 0)]))
    attn, (b_gu2,) = _attention(q, k, v, sinks, seq=seq, name="attention", comm=_GatherJob([gu_part(b_gu2, 1)]))
    (mixin, mix, xhat2, rstd2), (wgu2,) = _out_proj(
        attn, ubc, cw_full, wout, xhat1, ln1, mod, seq=seq, gate_idx=5, name="out_proj",
        comm=_GatherJob([gu_part(b_gu2, 2)]))
    (h3, a3, dact3), (wd2,) = _ffn_up(xhat2, ln2, mod, wgu2, seq=seq, sc_idx=7, sh_idx=6, use_ln=True, name="ffn2_up",
                                    comm=_GatherJob([whole(b_d2, False)]))
    dr3, df3, loss_cols, dln3g, dln3b, dgate3 = _ffn_down_loss(
        a3, wd2, xhat2, ln2, mod, ln3, tgt2, seq=seq, gate_idx=8, name="ffn2_down_loss")

    dgu3 = _ffn_bwd_act(df3, wd2, dact3, seq=seq, name="ffn2_bwd_act")
    s32_d2, s16_d2 = _grad_chip_sum(pos, a3, df3, half_on_rows=False, name="grad_wd2")
    (s32_gu2, s16_gu2), (recv_d2,) = _grad_chip_sum(pos, h3, dgu3, half_on_rows=True, name="grad_wgu2",
                                                    comm=_ExchangeJob([s16_d2], [False], [n_d]))
    (dr2, dmix, dsc3, dsh3, dgate2, dln2g, dln2b), (recv_gu2,) = _bwd_in(
        dgu3, wgu2, dr3, xhat2, rstd2, ln2, mod, mix, seq=seq, w_is_nt=True, sc_idx=7, gate_idx=5,
        branch_scale=1.0, final=False, name="ffn2_bwd_in", comm=_ExchangeJob([s16_gu2], [True], [n_gu]))
    s32_out, s16_out = _grad_chip_sum(pos, mixin, dmix, half_on_rows=False, name="grad_wout")
    dmixin = _matmul_nt_bf16(dmix, wout, seq=seq, name="out_proj_bwd")
    (dq, dkp, dkc, dvp, dvc, dsink), (recv_out,) = _attention_bwd(
        q, k, v, dmixin, sinks, seq=seq, name="attention_bwd", comm=_ExchangeJob([s16_out], [False], [n_out]))
    dproj, dcw = _mix_bwd_assemble(
        dq, dkp, dkc, dvp, dvc, cos_t, sa_t, sb_t, dmixin, ubc, cw_full, seq=seq, name="mix_bwd")
    s32_in, s16_in = _grad_chip_sum(pos, dproj, h2, half_on_rows=False, name="grad_win")
    (dr1, df1, dsc2, dsh2, dgate1, dln1g, dln1b), (recv_in,) = _bwd_in(
        dproj, win_t, dr2, xhat1, rstd1, ln1, mod, f1, seq=seq, w_is_nt=False, sc_idx=4, gate_idx=2,
        branch_scale=0.5, final=False, name="in_proj_bwd", comm=_ExchangeJob([s16_in], [False], [n_in]))
    s32_d1, s16_d1 = _grad_chip_sum(pos, a1, df1, half_on_rows=False, name="grad_wd1")
    dgu1, (recv_d1,) = _ffn_bwd_act(df1, wd1, dact1, seq=seq, name="ffn1_bwd_act",
                                    comm=_ExchangeJob([s16_d1], [False], [n_d]))
    s32_gu1, s16_gu1 = _grad_chip_sum(pos, h1, dgu1, half_on_rows=True, name="grad_wgu1")

    def final_half(s32_, recv_, col_kind, n_shard, name_):
        return _sum_final(pos, s32_, recv_, col_kind=col_kind, n_shard=n_shard, name=name_)

    early = [final_half(s32_gu2, recv_gu2, True, n_gu, "sum_final_gu2"),
             final_half(s32_d2, recv_d2, False, n_d, "sum_final_d2"),
             final_half(s32_out, recv_out, False, n_out, "sum_final_out"),
             final_half(s32_in, recv_in, False, n_in, "sum_final_in"),
             final_half(s32_d1, recv_d1, False, n_d, "sum_final_d1")]
    (grad_x, dsc1, dsh1), (recv_gu1, full_gu2, full_d2, full_out, full_in, full_d1) = _bwd_in(
        dgu1, wgu1, dr1, x2, None, None, mod, None, seq=seq, w_is_nt=True, sc_idx=1, gate_idx=None,
        branch_scale=None, final=True, name="ffn1_bwd_in",
        comm=_MultiJob([_ExchangeJob([s16_gu1], [True], [n_gu]), _ShareJob(early)]))
    late = [final_half(s32_gu1, recv_gu1, True, n_gu, "sum_final_gu1")]

    dmod = jnp.concatenate([dsh1, dsc1, dgate1, dsh2, dsc2, dgate2, dsh3, dsc3, dgate3], axis=1)
    loss_row = jnp.sum(loss_cols, axis=1, keepdims=True) * (0.5 / d)
    lane_row = lambda a: jnp.pad(a, ((0, 0), (0, d - a.shape[1])))
    block = jnp.concatenate(
        [dmod.reshape(nb * N_MOD, d), dln1g, dln1b, dln2g, dln2b, dln3g, dln3b,
         lane_row(dcw[0:3, :]), lane_row(dsink[:, 0:1].reshape(1, N_Q_HEADS)), lane_row(loss_row)], axis=0)
    block = jnp.pad(block, ((0, SMALL_ROWS - block.shape[0]), (0, 0)))
    gathered, (full_gu1,) = _allgather8(block, name="gather_small", comm=_ShareJob(late))
    gathered = gathered.reshape(N_DEV, SMALL_ROWS, d)
    dmod_all = gathered[:, :nb * N_MOD, :].reshape(N_DEV * nb, N_MOD * d)
    dmod_shard = lax.dynamic_slice(dmod_all, (0, chip * n_ada), (N_DEV * nb, n_ada))
    small, g_w_ada, g_b_ada = _small_finish(gathered, dmod_all, dmod_shard, c_all.T, name="small_finish")
    r0 = nb * N_MOD
    loss = small[r0 + 10, 0]
    g_ln = [small[r0 + i:r0 + i + 1, :] for i in range(6)]
    g_cw_full = small[r0 + 6:r0 + 9, :CONV_WIDTH]
    g_conv = lax.dynamic_slice(g_cw_full, (0, chip * conv_w.shape[2]), (3, conv_w.shape[2]))
    g_sinks = small[r0 + 9:r0 + 10, :N_Q_HEADS]

    def flat2(a):
        return a.reshape(-1, a.shape[-1])

    def unhalve(a):
        return a.reshape(2 * a.shape[1], a.shape[2])

    results = {}

    def adamw(name_, w_, g_, m_, v_):
        g2 = flat2(g_)
        dl, nm, nv = _adamw(flat2(w_), g2, flat2(m_), flat2(v_), name="adamw_" + name_)
        results[name_] = tuple(a.reshape(w_.shape) for a in (g2, dl, nm, nv))

    adamw("w_ada", w_ada, g_w_ada, m_w_ada, v_w_ada)
    adamw("ffn2_w_gate_up", ffn2_w_gate_up, unhalve(full_gu2), m_ffn2_w_gate_up, v_ffn2_w_gate_up)
    adamw("ffn2_w_down", ffn2_w_down, full_d2, m_ffn2_w_down, v_ffn2_w_down)
    adamw("w_out", w_out, full_out, m_w_out, v_w_out)
    adamw("w_in", w_in, full_in.T, m_w_in, v_w_in)
    adamw("ffn1_w_gate_up", ffn1_w_gate_up, unhalve(full_gu1), m_ffn1_w_gate_up, v_ffn1_w_gate_up)
    adamw("ffn1_w_down", ffn1_w_down, full_d1, m_ffn1_w_down, v_ffn1_w_down)
    adamw("b_ada", b_ada, g_b_ada, m_b_ada, v_b_ada)
    adamw("ln1_g", ln1_g, g_ln[0], m_ln1_g, v_ln1_g)
    adamw("ln1_b", ln1_b, g_ln[1], m_ln1_b, v_ln1_b)
    adamw("ln2_g", ln2_g, g_ln[2], m_ln2_g, v_ln2_g)
    adamw("ln2_b", ln2_b, g_ln[3], m_ln2_b, v_ln2_b)
    adamw("ln3_g", ln3_g, g_ln[4], m_ln3_g, v_ln3_g)
    adamw("ln3_b", ln3_b, g_ln[5], m_ln3_b, v_ln3_b)
    adamw("conv_w", conv_w, g_conv, m_conv_w, v_conv_w)
    adamw("attn_sinks", attn_sinks, g_sinks, m_attn_sinks, v_attn_sinks)
    order = ["w_ada", "b_ada", "ffn1_w_gate_up", "ffn1_w_down", "ln1_g", "ln1_b", "w_in", "conv_w", "attn_sinks",
             "w_out", "ln2_g", "ln2_b", "ffn2_w_gate_up", "ffn2_w_down", "ln3_g", "ln3_b"]
    return (loss, grad_x.reshape(x.shape), *[results[n_][0] for n_ in order], *[results[n_][1] for n_ in order],
            *[results[n_][2] for n_ in order], *[results[n_][3] for n_ in order])
```

```python
import jax
import jax.numpy as jnp
from jax import lax
from jax.experimental import pallas as pl
from jax.experimental.pallas import tpu as pltpu

F32 = jnp.float32
BF16 = jnp.bfloat16
MESH = pl.DeviceIdType.MESH

D_MODEL = 1024
HEAD_DIM = 64
ATTN_WIDTH = 512
CONV_WIDTH = 512
N_Q_HEADS = 8
N_KV_HEADS = 2
GQA_GROUP = 4
KV_WIDTH = 128
WINDOW = 128
BLOCK = 128
ROT_DIM = 16
ROPE_THETA = 500000.0
N_MOD = 9
LN_EPS = 1e-5
DN_ALPHA = 2.0 ** 0.25
IN_WIDTH = 2304
N_CHIPS = 4
N_DEV = 8
SMALL_ROWS = 32

ADAM_LR = 0.001
ADAM_B1 = 0.9
ADAM_B2 = 0.999
ADAM_EPS = 1e-08
ADAM_WD = 0.01
ADAM_STEP = 10

LANE = 128
HALO = 16
COL_CHUNK = 256
VMEM_LIMIT = 56 * 1024 * 1024


def _params(sem=None, vmem=True):
    return pltpu.CompilerParams(dimension_semantics=sem, vmem_limit_bytes=VMEM_LIMIT if vmem else None)


def _sigmoid(g):
    return 0.5 * jnp.tanh(0.5 * g) + 0.5


def _row_sum(v):
    return jnp.sum(v, axis=0, keepdims=True)


ROW_CHUNK = 16
EPILOGUE_UNROLL = 8


def _fold8(v):
    return v[0:8, :] + v[8:16, :]


def _row_chunk_loop(n_rows, step, init):
    per_iter = ROW_CHUNK * EPILOGUE_UNROLL
    assert n_rows % per_iter == 0, n_rows

    def body(it, carry):
        for s in range(EPILOGUE_UNROLL):
            start = pl.multiple_of(it * per_iter + s * ROW_CHUNK, ROW_CHUNK)
            carry = step(pl.ds(start, ROW_CHUNK), carry)
        return carry

    return lax.fori_loop(0, n_rows // per_iter, body, init)


def _ln_stats(r):
    mu = jnp.mean(r, axis=-1, keepdims=True)
    rc = r - mu
    var = jnp.mean(rc * rc, axis=-1, keepdims=True)
    rstd = lax.rsqrt(var + LN_EPS)
    return rc * rstd, rstd


def _ln_bwd(dxo, xhat, rstd, g):
    dxhat = dxo * g
    m1 = jnp.mean(dxhat, axis=-1, keepdims=True)
    m2 = jnp.mean(dxhat * xhat, axis=-1, keepdims=True)
    return rstd * (dxhat - m1 - xhat * m2)


def _dot_nt(a, b):
    return lax.dot_general(a, b, (((1,), (1,)), ((), ())), preferred_element_type=F32)


def _dot_tn(a, b):
    return lax.dot_general(a, b, (((0,), (0,)), ((), ())), preferred_element_type=F32)


def _full(shape):
    nd = len(shape)
    return pl.BlockSpec(shape, lambda *_: (0,) * nd)


def _resident(shape):
    nd = len(shape)
    return pl.BlockSpec(shape, lambda *_: (0,) * nd, pipeline_mode=pl.Buffered(1))


ANY_SPEC = pl.BlockSpec(memory_space=pl.ANY)


def _pcall(body, *, name, grid, in_specs, out_specs, out_shape, args, scratch_shapes=(), comm=None, prefetch=None):
    single = not isinstance(out_shape, (list, tuple))
    out_specs = [out_specs] if single else list(out_specs)
    out_shape = [out_shape] if single else list(out_shape)
    in_specs = list(in_specs)
    scratch_shapes = list(scratch_shapes)
    sem = ("arbitrary",) * len(grid)
    n_pre = 0 if prefetch is None else 1
    pre_args = () if prefetch is None else (prefetch,)

    def call(fn, ins_, outs_, shapes_, scratch_, aliases_, operands):
        if prefetch is None:
            return pl.pallas_call(fn, name=name, grid=grid, in_specs=ins_, out_specs=outs_, out_shape=shapes_,
                                  scratch_shapes=scratch_, input_output_aliases=aliases_,
                                  compiler_params=_params(sem))(*operands)
        spec = pltpu.PrefetchScalarGridSpec(num_scalar_prefetch=1, grid=grid, in_specs=ins_, out_specs=outs_,
                                            scratch_shapes=scratch_)
        return pl.pallas_call(fn, name=name, grid_spec=spec, out_shape=shapes_,
                              input_output_aliases={n_pre + i: o for i, o in aliases_.items()},
                              compiler_params=_params(sem))(*pre_args, *operands)

    if comm is None:
        res = call(body, in_specs, out_specs, out_shape, scratch_shapes, {}, args)
        return res[0] if single else res
    n_in, n_out, n_scr = len(in_specs), len(out_specs), len(scratch_shapes)
    nci, nco = len(comm.inputs), len(comm.out_shapes)
    n_steps = 1
    for g in grid:
        n_steps *= g
    staged = n_steps >= 8
    middle_step = (n_steps * 5) // 8 - 1
    late_step = n_steps - 1 - max(1, n_steps // 8)

    def wrapped(*refs):
        pre, refs = refs[:n_pre], refs[n_pre:]
        ins, refs = refs[:n_in], refs[n_in:]
        cin, refs = refs[:nci], refs[nci:]
        outs, refs = refs[:n_out], refs[n_out:]
        cout, refs = refs[:nco], refs[nco:]
        scr, csems = refs[:n_scr], refs[n_scr:]
        step = pl.program_id(0)
        for ax in range(1, len(grid)):
            step = step * grid[ax] + pl.program_id(ax)

        @pl.when(step == 0)
        def _():
            comm.start(cin, cout, csems)

        body(*pre, *ins, *outs, *scr)

        if staged:
            @pl.when(step == middle_step)
            def _():
                comm.middle(cin, cout, csems)

            @pl.when(step == late_step)
            def _():
                comm.late(cin, cout, csems)

        @pl.when(step == n_steps - 1)
        def _():
            if not staged:
                comm.middle(cin, cout, csems)
                comm.late(cin, cout, csems)
            comm.finish(cin, cout, csems)

    res = call(wrapped, in_specs + [ANY_SPEC] * nci, out_specs + [ANY_SPEC] * nco,
               out_shape + list(comm.out_shapes), scratch_shapes + list(comm.sems),
               {n_in + i: n_out + o for i, o in comm.aliases.items()}, (*args, *comm.inputs))
    main = res[:n_out]
    return (main[0] if single else main), list(res[n_out:])


def _ffn_up(xin, lnp, mod, w, *, seq, sc_idx, sh_idx, use_ln, name, comm=None):
    t, d = xin.shape
    f = w.shape[1] // 2
    tm = min(512, seq)
    tpb = seq // tm
    ch = min(COL_CHUNK, f)

    def body(x_ref, ln_ref, mod_ref, w_ref, h_ref, a_ref, dact_ref):
        x = x_ref[...]
        if use_ln:
            x = x * ln_ref[0:1, :] + ln_ref[1:2, :]
        h = x * (1.0 + mod_ref[0, sc_idx:sc_idx + 1, :]) + mod_ref[0, sh_idx:sh_idx + 1, :]
        hb = h.astype(BF16)
        h_ref[...] = hb
        for j in range(f // ch):
            g = jnp.dot(hb, w_ref[:, j * ch:(j + 1) * ch], preferred_element_type=F32)
            u = jnp.dot(hb, w_ref[:, f + j * ch:f + (j + 1) * ch], preferred_element_type=F32)
            s = _sigmoid(g)
            silu = g * s
            a_ref[:, j * ch:(j + 1) * ch] = (silu * u).astype(BF16)
            dact_ref[:, j * ch:(j + 1) * ch] = (u * (s + silu * (1.0 - s))).astype(BF16)
            dact_ref[:, f + j * ch:f + (j + 1) * ch] = silu.astype(BF16)

    return _pcall(
        body, name=name, grid=(t // tm,),
        in_specs=[pl.BlockSpec((tm, d), lambda i: (i, 0)), _full((2, d)),
                  pl.BlockSpec((1, N_MOD, d), lambda i: (i // tpb, 0, 0)), _resident((d, 2 * f))],
        out_specs=[pl.BlockSpec((tm, d), lambda i: (i, 0)), pl.BlockSpec((tm, f), lambda i: (i, 0)),
                   pl.BlockSpec((tm, 2 * f), lambda i: (i, 0))],
        out_shape=[jax.ShapeDtypeStruct((t, d), BF16), jax.ShapeDtypeStruct((t, f), BF16),
                   jax.ShapeDtypeStruct((t, 2 * f), BF16)],
        args=(xin, lnp, mod, w), comm=comm)


def _ffn_down_ln(a, wd, xin, lnp_in, mod, *, seq, gate_idx, use_ln, name, comm=None):
    t, f = a.shape
    d = wd.shape[1]
    tm = min(512, seq)
    tpb = seq // tm

    def body(a_ref, wd_ref, x_ref, ln_ref, mod_ref, f_ref, xhat_ref, rstd_ref, acc):
        av = a_ref[...]
        for j in range(d // COL_CHUNK):
            acc[:, j * COL_CHUNK:(j + 1) * COL_CHUNK] = jnp.dot(
                av, wd_ref[:, j * COL_CHUNK:(j + 1) * COL_CHUNK], preferred_element_type=F32)
        scale = 0.5 * (1.0 + mod_ref[0, gate_idx:gate_idx + 1, :])

        fo = acc[...]
        x = x_ref[...]
        if use_ln:
            x = x * ln_ref[0:1, :] + ln_ref[1:2, :]
        xhat, rstd = _ln_stats(DN_ALPHA * x + scale * fo)
        f_ref[...] = fo.astype(BF16)
        xhat_ref[...] = xhat
        rstd_ref[...] = rstd

    return _pcall(
        body, name=name, grid=(t // tm,),
        in_specs=[pl.BlockSpec((tm, f), lambda i: (i, 0)), _resident((f, d)),
                  pl.BlockSpec((tm, d), lambda i: (i, 0)), _full((2, d)),
                  pl.BlockSpec((1, N_MOD, d), lambda i: (i // tpb, 0, 0))],
        out_specs=[pl.BlockSpec((tm, d), lambda i: (i, 0)), pl.BlockSpec((tm, d), lambda i: (i, 0)),
                   pl.BlockSpec((tm, 1), lambda i: (i, 0))],
        out_shape=[jax.ShapeDtypeStruct((t, d), BF16), jax.ShapeDtypeStruct((t, d), F32),
                   jax.ShapeDtypeStruct((t, 1), F32)],
        scratch_shapes=[pltpu.VMEM((tm, d), F32)],
        args=(a, wd, xin, lnp_in, mod), comm=comm)


def _ffn_down_loss(a, wd, xhat_in, lnp_in, mod, lnp_out, tgt, *, seq, gate_idx, name):
    t, f = a.shape
    d = wd.shape[1]
    nb = t // seq
    tm = min(512, seq)
    tpb = seq // tm

    def body(a_ref, wd_ref, x_ref, lnin_ref, mod_ref, lnout_ref, tgt_ref,
             dr_ref, df_ref, loss_ref, dg_ref, db_ref, dgate_ref, acc):
        i = pl.program_id(0)
        av = a_ref[...]
        for j in range(d // COL_CHUNK):
            acc[:, j * COL_CHUNK:(j + 1) * COL_CHUNK] = jnp.dot(
                av, wd_ref[:, j * COL_CHUNK:(j + 1) * COL_CHUNK], preferred_element_type=F32)
        scale = 0.5 * (1.0 + mod_ref[0, gate_idx:gate_idx + 1, :])
        ag_in, ab_in = DN_ALPHA * lnin_ref[0:1, :], DN_ALPHA * lnin_ref[1:2, :]
        g_out, b_out = lnout_ref[0:1, :], lnout_ref[1:2, :]
        g_over_d = g_out * (1.0 / d)

        def chunk(rows, carry):
            s_loss, s_dg, s_db, s_gate = carry
            fo = acc[rows, :]
            xhat, rstd = _ln_stats(x_ref[rows, :] * ag_in + ab_in + scale * fo)
            e = xhat * g_out + b_out - tgt_ref[rows, :]
            dr = _ln_bwd(e, xhat, rstd, g_over_d)
            dr_ref[rows, :] = dr
            df_ref[rows, :] = (scale * dr).astype(BF16)
            return s_loss + _fold8(e * e), s_dg + _fold8(e * xhat), s_db + _fold8(e), s_gate + _fold8(fo * dr)

        zero = jnp.zeros((8, d), F32)
        s_loss, s_dg, s_db, s_gate = _row_chunk_loop(tm, chunk, (zero, zero, zero, zero))
        s_dg, s_db, s_gate = s_dg * (1.0 / d), s_db * (1.0 / d), s_gate * 0.5

        @pl.when(i == 0)
        def _():
            loss_ref[...] = jnp.zeros_like(loss_ref)
            dg_ref[...] = jnp.zeros_like(dg_ref)
            db_ref[...] = jnp.zeros_like(db_ref)

        @pl.when(i % tpb == 0)
        def _():
            dgate_ref[...] = jnp.zeros_like(dgate_ref)

        loss_ref[...] += _row_sum(s_loss)
        dg_ref[...] += _row_sum(s_dg)
        db_ref[...] += _row_sum(s_db)
        dgate_ref[0] += _row_sum(s_gate)

    return pl.pallas_call(
        body, name=name, grid=(t // tm,), scratch_shapes=[pltpu.VMEM((tm, d), F32)],
        in_specs=[pl.BlockSpec((tm, f), lambda i: (i, 0)), _resident((f, d)),
                  pl.BlockSpec((tm, d), lambda i: (i, 0)), _full((2, d)),
                  pl.BlockSpec((1, N_MOD, d), lambda i: (i // tpb, 0, 0)), _full((2, d)),
                  pl.BlockSpec((tm, d), lambda i: (i, 0))],
        out_specs=[pl.BlockSpec((tm, d), lambda i: (i, 0)), pl.BlockSpec((tm, d), lambda i: (i, 0)),
                   _full((1, d)), _full((1, d)), _full((1, d)),
                   pl.BlockSpec((1, 1, d), lambda i: (i // tpb, 0, 0))],
        out_shape=[jax.ShapeDtypeStruct((t, d), F32), jax.ShapeDtypeStruct((t, d), BF16),
                   jax.ShapeDtypeStruct((1, d), F32), jax.ShapeDtypeStruct((1, d), F32),
                   jax.ShapeDtypeStruct((1, d), F32), jax.ShapeDtypeStruct((nb, 1, d), F32)],
        compiler_params=_params(("arbitrary",)),
    )(a, wd, xhat_in, lnp_in, mod, lnp_out, tgt)


def _rope(v, cos, sa, sb):
    return v * cos + pltpu.roll(v, LANE - ROT_DIM // 2, 1) * sa + pltpu.roll(v, ROT_DIM // 2, 1) * sb


def _rope_t(dy, cos, sa, sb):
    return dy * cos + pltpu.roll(dy * sa, ROT_DIM // 2, 1) + pltpu.roll(dy * sb, LANE - ROT_DIM // 2, 1)


def _in_proj(xhat, lnp, mod, w_t, cos, sa, sb, *, seq, sc_idx, sh_idx, name, comm=None):
    t, d = xhat.shape
    tm = min(512, seq)
    tpb = seq // tm
    n_conv = 3 * CONV_WIDTH

    def body(x_ref, ln_ref, mod_ref, w_ref, cos_ref, sa_ref, sb_ref, h_ref, q_ref, k_ref, v_ref, ubc_ref):
        x = x_ref[...] * ln_ref[0:1, :] + ln_ref[1:2, :]
        h = x * (1.0 + mod_ref[0, sc_idx:sc_idx + 1, :]) + mod_ref[0, sh_idx:sh_idx + 1, :]
        hb = h.astype(BF16)
        h_ref[...] = hb
        cos_t, sa_t, sb_t = cos_ref[...], sa_ref[...], sb_ref[...]
        for j in range(ATTN_WIDTH // COL_CHUNK):
            p = _dot_nt(hb, w_ref[j * COL_CHUNK:(j + 1) * COL_CHUNK, :])
            for s in range(COL_CHUNK // LANE):
                q_ref[:, j * COL_CHUNK + s * LANE:j * COL_CHUNK + (s + 1) * LANE] = _rope(
                    p[:, s * LANE:(s + 1) * LANE], cos_t, sa_t, sb_t).astype(BF16)
        p = _dot_nt(hb, w_ref[ATTN_WIDTH:ATTN_WIDTH + 2 * KV_WIDTH, :])
        k_ref[...] = _rope(p[:, 0:KV_WIDTH], cos_t, sa_t, sb_t).astype(BF16)
        v_ref[...] = p[:, KV_WIDTH:].astype(BF16)
        base = ATTN_WIDTH + 2 * KV_WIDTH
        for j in range(n_conv // COL_CHUNK):
            ubc_ref[:, j * COL_CHUNK:(j + 1) * COL_CHUNK] = _dot_nt(
                hb, w_ref[base + j * COL_CHUNK:base + (j + 1) * COL_CHUNK, :]).astype(BF16)

    row = lambda w: pl.BlockSpec((tm, w), lambda i: (i, 0))
    return _pcall(
        body, name=name, grid=(t // tm,),
        in_specs=[row(d), _full((2, d)), pl.BlockSpec((1, N_MOD, d), lambda i: (i // tpb, 0, 0)),
                  _resident((IN_WIDTH, d)), row(LANE), row(LANE), row(LANE)],
        out_specs=[row(d), row(ATTN_WIDTH), row(KV_WIDTH), row(KV_WIDTH), row(n_conv)],
        out_shape=[jax.ShapeDtypeStruct((t, d), BF16), jax.ShapeDtypeStruct((t, ATTN_WIDTH), BF16),
                   jax.ShapeDtypeStruct((t, KV_WIDTH), BF16), jax.ShapeDtypeStruct((t, KV_WIDTH), BF16),
                   jax.ShapeDtypeStruct((t, n_conv), BF16)],
        args=(xhat, lnp, mod, w_t, cos, sa, sb), comm=comm)


ATTN_TILE_BLOCKS = 2


def _attn_sub_block(s, tile, nblk, kp_ref, kc_ref, vp_ref, vc_ref):
    rows = slice(s * BLOCK, (s + 1) * BLOCK)
    if s == 0:
        first = ((tile * ATTN_TILE_BLOCKS) % nblk) == 0
        return rows, (kp_ref, slice(0, BLOCK)), (kc_ref, rows), (vp_ref, slice(0, BLOCK)), (vc_ref, rows), first
    before = slice((s - 1) * BLOCK, s * BLOCK)
    return rows, (kc_ref, before), (kc_ref, rows), (vc_ref, before), (vc_ref, rows), False


def _attn_group(q_ref, rows, k_prev, k_cur, v_prev, v_cur, sink_ref, g, first):
    lo, hi = g * HEAD_DIM, (g + 1) * HEAD_DIM
    kk = jnp.concatenate([k_prev[0][k_prev[1], lo:hi], k_cur[0][k_cur[1], lo:hi]], axis=0)
    vv = jnp.concatenate([v_prev[0][v_prev[1], lo:hi], v_cur[0][v_cur[1], lo:hi]], axis=0)
    qs = jnp.concatenate([q_ref[rows, (GQA_GROUP * g + j) * HEAD_DIM:(GQA_GROUP * g + j + 1) * HEAD_DIM]
                          for j in range(GQA_GROUP)], axis=0)
    cols = GQA_GROUP * BLOCK
    ki = lax.broadcasted_iota(jnp.int32, (2 * BLOCK, cols), 0)
    col = lax.broadcasted_iota(jnp.int32, (2 * BLOCK, cols), 1)
    diff = (col & (BLOCK - 1)) + BLOCK - ki
    valid = (diff >= 0) & (diff < WINDOW) & ((ki >= BLOCK) | jnp.logical_not(first))
    s = _dot_nt(kk, qs) * (HEAD_DIM ** -0.5)
    s = jnp.where(valid, s, -1e30)
    hcol = lax.broadcasted_iota(jnp.int32, (1, cols), 1)
    sink = jnp.zeros((1, cols), F32)
    for j in range(GQA_GROUP):
        sink = jnp.where(hcol // BLOCK == j, sink_ref[GQA_GROUP * g + j], sink)
    m = jnp.maximum(jnp.max(s, axis=0, keepdims=True), sink)
    p = jnp.exp(s - m)
    ps = jnp.exp(sink - m)
    inv = 1.0 / (jnp.sum(p, axis=0, keepdims=True) + ps)
    return qs, kk, vv, p * inv, ps * inv


def _heads_to_lanes(x_t):
    return jnp.concatenate([x_t[:, j * BLOCK:(j + 1) * BLOCK].T for j in range(GQA_GROUP)], axis=1)


def _attention(q, k, v, sinks, *, seq, name, comm=None):
    t = q.shape[0]
    nblk = seq // BLOCK
    tile = ATTN_TILE_BLOCKS * BLOCK

    def body(q_ref, kp_ref, kc_ref, vp_ref, vc_ref, sink_ref, o_ref):
        for s in range(ATTN_TILE_BLOCKS):
            rows, k_prev, k_cur, v_prev, v_cur, first = _attn_sub_block(
                s, pl.program_id(0), nblk, kp_ref, kc_ref, vp_ref, vc_ref)
            outs = []
            for g in range(N_KV_HEADS):
                _, _, vv, pn, _ = _attn_group(q_ref, rows, k_prev, k_cur, v_prev, v_cur, sink_ref, g, first)
                outs.append(_heads_to_lanes(_dot_tn(vv, pn.astype(BF16))))
            o_ref[rows, :] = jnp.concatenate(outs, axis=1).astype(BF16)

    cur = lambda w: pl.BlockSpec((tile, w), lambda n: (n, 0))
    prev = lambda w: pl.BlockSpec((BLOCK, w), lambda n: (jnp.maximum(n * ATTN_TILE_BLOCKS - 1, 0), 0))
    return _pcall(
        body, name=name, grid=(t // tile,),
        in_specs=[cur(ATTN_WIDTH), prev(KV_WIDTH), cur(KV_WIDTH), prev(KV_WIDTH), cur(KV_WIDTH),
                  pl.BlockSpec(memory_space=pltpu.SMEM)],
        out_specs=cur(ATTN_WIDTH),
        out_shape=jax.ShapeDtypeStruct((t, ATTN_WIDTH), BF16),
        args=(q, k, k, v, v, sinks), comm=comm)


def _out_proj(attn, ubc, cw, wout, xhat_in, lnp_in, mod, *, seq, gate_idx, name, comm=None):
    t, d = xhat_in.shape
    tm = min(512, seq)
    tpb = seq // tm
    cwid = CONV_WIDTH

    def body(attn_ref, ubc_ref, halo_ref, cw_ref, w_ref, x_ref, ln_ref, mod_ref,
             mixin_ref, mix_ref, xhat_ref, rstd_ref, zbuf, acc):
        first = (pl.program_id(0) % tpb) == 0
        u, bg, cg = (ubc_ref[:, s * cwid:(s + 1) * cwid].astype(F32) for s in range(3))
        z = cg * u
        hz = halo_ref[:, 2 * cwid:3 * cwid].astype(F32) * halo_ref[:, 0:cwid].astype(F32)
        zbuf[0:HALO, :] = jnp.where(first, 0.0, hz)
        zbuf[HALO:HALO + tm, :] = z
        y = (cw_ref[0:1, :] * zbuf[HALO - 2:HALO - 2 + tm, :] + cw_ref[1:2, :] * zbuf[HALO - 1:HALO - 1 + tm, :]
             + cw_ref[2:3, :] * z)
        mixin_ref[:, 0:ATTN_WIDTH] = attn_ref[...]
        mixin_ref[:, ATTN_WIDTH:] = (bg * y).astype(BF16)
        mv = mixin_ref[...]
        for j in range(d // COL_CHUNK):
            acc[:, j * COL_CHUNK:(j + 1) * COL_CHUNK] = jnp.dot(
                mv, w_ref[:, j * COL_CHUNK:(j + 1) * COL_CHUNK], preferred_element_type=F32)
        scale = 1.0 + mod_ref[0, gate_idx:gate_idx + 1, :]

        mix = acc[...]
        xhat, rstd = _ln_stats(DN_ALPHA * (x_ref[...] * ln_ref[0:1, :] + ln_ref[1:2, :]) + scale * mix)
        mix_ref[...] = mix.astype(BF16)
        xhat_ref[...] = xhat
        rstd_ref[...] = rstd

    row = lambda w: pl.BlockSpec((tm, w), lambda i: (i, 0))
    return _pcall(
        body, name=name, grid=(t // tm,),
        in_specs=[row(ATTN_WIDTH), row(3 * cwid),
                  pl.BlockSpec((HALO, 3 * cwid), lambda i: (jnp.maximum(i * (tm // HALO) - 1, 0), 0)),
                  _full((8, cwid)), _resident((d, d)), row(d), _full((2, d)),
                  pl.BlockSpec((1, N_MOD, d), lambda i: (i // tpb, 0, 0))],
        out_specs=[row(d), row(d), row(d), row(1)],
        out_shape=[jax.ShapeDtypeStruct((t, d), BF16), jax.ShapeDtypeStruct((t, d), BF16),
                   jax.ShapeDtypeStruct((t, d), F32), jax.ShapeDtypeStruct((t, 1), F32)],
        scratch_shapes=[pltpu.VMEM((tm + HALO, cwid), F32), pltpu.VMEM((tm, d), F32)],
        args=(attn, ubc, ubc, cw, wout, xhat_in, lnp_in, mod), comm=comm)


def _ffn_bwd_act(df, wd, dact, *, seq, name, comm=None):
    t, d = df.shape
    f = wd.shape[0]
    tm = min(512, seq)
    ch = min(COL_CHUNK, f)

    def body(df_ref, wd_ref, dact_ref, dgu_ref):
        dfv = df_ref[...]
        for j in range(f // ch):
            da = _dot_nt(dfv, wd_ref[j * ch:(j + 1) * ch, :])
            dgu_ref[:, j * ch:(j + 1) * ch] = (da * dact_ref[:, j * ch:(j + 1) * ch].astype(F32)).astype(BF16)
            dgu_ref[:, f + j * ch:f + (j + 1) * ch] = (
                da * dact_ref[:, f + j * ch:f + (j + 1) * ch].astype(F32)).astype(BF16)

    return _pcall(
        body, name=name, grid=(t // tm,),
        in_specs=[pl.BlockSpec((tm, d), lambda i: (i, 0)), _resident((f, d)),
                  pl.BlockSpec((tm, 2 * f), lambda i: (i, 0))],
        out_specs=pl.BlockSpec((tm, 2 * f), lambda i: (i, 0)),
        out_shape=jax.ShapeDtypeStruct((t, 2 * f), BF16),
        args=(df, wd, dact), comm=comm)


def _bwd_in(a, w, dr, xin, rstd_prev, lnp_prev, mod, branch_prev, *, seq, w_is_nt, sc_idx, gate_idx,
            branch_scale, final, name, comm=None):
    t, kdim = a.shape
    d = dr.shape[1]
    nb = t // seq
    tm = min(512, seq)
    tpb = seq // tm

    def body(*refs):
        if final:
            a_ref, w_ref, dr_ref, x_ref, mod_ref, dx_ref, dsc_ref, dsh_ref, acc = refs
        else:
            (a_ref, w_ref, dr_ref, x_ref, rstd_ref, ln_ref, mod_ref, br_ref,
             drp_ref, dbr_ref, dsc_ref, dsh_ref, dgate_ref, dg_ref, db_ref, acc) = refs
        i = pl.program_id(0)
        av = a_ref[...]
        for j in range(d // COL_CHUNK):
            cols = slice(j * COL_CHUNK, (j + 1) * COL_CHUNK)
            acc[:, cols] = (_dot_nt(av, w_ref[cols, :]) if w_is_nt
                            else jnp.dot(av, w_ref[:, cols], preferred_element_type=F32))
        sc1 = 1.0 + mod_ref[0, sc_idx:sc_idx + 1, :]
        if not final:
            g_prev, b_prev = ln_ref[0:1, :], ln_ref[1:2, :]
            bscale = branch_scale * (1.0 + mod_ref[0, gate_idx:gate_idx + 1, :])

        def chunk(rows, carry):
            dh = acc[rows, :]
            dx = DN_ALPHA * dr_ref[rows, :] + dh * sc1
            if final:
                dx_ref[rows, :] = dx
                return carry[0] + _fold8(dh * x_ref[rows, :]), carry[1] + _fold8(dh)
            xhat = x_ref[rows, :]
            drp = _ln_bwd(dx, xhat, rstd_ref[rows, :], g_prev)
            drp_ref[rows, :] = drp
            dbr_ref[rows, :] = (bscale * drp).astype(BF16)
            return (carry[0] + _fold8(dh * xhat), carry[1] + _fold8(dh),
                    carry[2] + _fold8(br_ref[rows, :].astype(F32) * drp),
                    carry[3] + _fold8(dx * xhat), carry[4] + _fold8(dx))

        zero = jnp.zeros((8, d), F32)
        sums = list(_row_chunk_loop(tm, chunk, (zero,) * (2 if final else 5)))
        if not final:
            sums[0] = sums[0] * g_prev + sums[1] * b_prev
            sums[2] = sums[2] * branch_scale

        @pl.when((i % tpb) == 0)
        def _():
            dsc_ref[...] = jnp.zeros_like(dsc_ref)
            dsh_ref[...] = jnp.zeros_like(dsh_ref)
            if not final:
                dgate_ref[...] = jnp.zeros_like(dgate_ref)

        dsc_ref[0] += _row_sum(sums[0])
        dsh_ref[0] += _row_sum(sums[1])
        if not final:
            @pl.when(i == 0)
            def _():
                dg_ref[...] = jnp.zeros_like(dg_ref)
                db_ref[...] = jnp.zeros_like(db_ref)

            dgate_ref[0] += _row_sum(sums[2])
            dg_ref[...] += _row_sum(sums[3])
            db_ref[...] += _row_sum(sums[4])

    row = lambda w_: pl.BlockSpec((tm, w_), lambda i: (i, 0))
    vec = pl.BlockSpec((1, 1, d), lambda i: (i // tpb, 0, 0))
    mod_spec = pl.BlockSpec((1, N_MOD, d), lambda i: (i // tpb, 0, 0))
    vshape = jax.ShapeDtypeStruct((nb, 1, d), F32)
    if final:
        in_specs = [row(kdim), _resident(w.shape), row(d), row(d), mod_spec]
        args = (a, w, dr, xin, mod)
        out_specs = [row(d), vec, vec]
        out_shape = [jax.ShapeDtypeStruct((t, d), F32), vshape, vshape]
    else:
        in_specs = [row(kdim), _resident(w.shape), row(d), row(d), row(1), _full((2, d)), mod_spec, row(d)]
        args = (a, w, dr, xin, rstd_prev, lnp_prev, mod, branch_prev)
        out_specs = [row(d), row(d), vec, vec, vec, _full((1, d)), _full((1, d))]
        out_shape = [jax.ShapeDtypeStruct((t, d), F32), jax.ShapeDtypeStruct((t, d), BF16), vshape, vshape, vshape,
                     jax.ShapeDtypeStruct((1, d), F32), jax.ShapeDtypeStruct((1, d), F32)]
    return _pcall(
        body, name=name, grid=(t // tm,), in_specs=in_specs, out_specs=out_specs, out_shape=out_shape,
        scratch_shapes=[pltpu.VMEM((tm, d), F32)], args=args, comm=comm)


def _grad_chip_sum(pos, a, b, *, half_on_rows, name, comm=None):
    t, m = a.shape
    n = b.shape[1]
    tk = min(2048, t)
    nk = t // tk
    half = lambda p, pos_ref: 1 - pos_ref[2] - p + 2 * p * pos_ref[2]
    if half_on_rows:
        n_j = N_CHIPS
        tile = (m // 2, n // n_j)
        a_spec = pl.BlockSpec((tk, tile[0]), lambda p, j, k, pos_ref: (k, half(p, pos_ref)))
        b_spec = pl.BlockSpec((tk, tile[1]), lambda p, j, k, pos_ref: (k, j))
        out_tile = pl.BlockSpec((1, *tile), lambda p, j, k, pos_ref: (0, 0, j * p))
        total = (1, m // 2, n)
    else:
        n_j = 2
        tile = (m // n_j, n // 2)
        a_spec = pl.BlockSpec((tk, tile[0]), lambda p, j, k, pos_ref: (k, j))
        b_spec = pl.BlockSpec((tk, tile[1]), lambda p, j, k, pos_ref: (k, half(p, pos_ref)))
        out_tile = pl.BlockSpec((1, *tile), lambda p, j, k, pos_ref: (0, j * p, 0))
        total = (1, m, n // 2)

    def body(pos_ref, a_ref, b_ref, s32_ref, s16_ref, land_ref, acc, theirs, send_sems, recv_sems, copy_sem):
        p, j, k = pl.program_id(0), pl.program_id(1), pl.program_id(2)
        x, y, c = _position()

        def push(jj):
            return pltpu.make_async_remote_copy(
                src_ref=acc.at[jj], dst_ref=land_ref.at[jj], send_sem=send_sems.at[jj], recv_sem=recv_sems.at[jj],
                device_id=(x, y, 1 - c), device_id_type=MESH)

        fetch = pltpu.make_async_copy(land_ref.at[j], theirs, copy_sem)

        @pl.when(jnp.logical_and(p == 1, k == 0))
        def _():
            push(j).wait_send()
            push(j).wait_recv()
            fetch.start()

        part = _dot_tn(a_ref[...], b_ref[...])

        @pl.when(k == 0)
        def _():
            acc[j] = part

        @pl.when(k > 0)
        def _():
            acc[j] += part

        @pl.when(jnp.logical_and(p == 0, k == nk - 1))
        def _():
            push(j).start()

        @pl.when(jnp.logical_and(p == 1, k == nk - 1))
        def _():
            fetch.wait()
            s = acc[j] + theirs[...]
            s32_ref[0] = s
            s16_ref[0] = s.astype(BF16)

    out = _pcall(
        body, name=name, grid=(2, n_j, nk), in_specs=[a_spec, b_spec], out_specs=[out_tile, out_tile, ANY_SPEC],
        out_shape=[jax.ShapeDtypeStruct(total, F32), jax.ShapeDtypeStruct(total, BF16),
                   jax.ShapeDtypeStruct((n_j, *tile), F32)],
        scratch_shapes=[pltpu.VMEM((n_j, *tile), F32), pltpu.VMEM(tile, F32),
                        pltpu.SemaphoreType.DMA((n_j,)), pltpu.SemaphoreType.DMA((n_j,)), pltpu.SemaphoreType.DMA],
        args=(a, b), prefetch=pos, comm=comm)
    if comm is None:
        return out[0], out[1]
    (s32, s16, _), extra = out
    return (s32, s16), extra


def _matmul_nt_bf16(a, w, *, seq, name):
    t, kdim = a.shape
    n = w.shape[0]
    tm = min(512, seq)

    def body(a_ref, w_ref, o_ref):
        av = a_ref[...]
        for j in range(n // COL_CHUNK):
            o_ref[:, j * COL_CHUNK:(j + 1) * COL_CHUNK] = _dot_nt(
                av, w_ref[j * COL_CHUNK:(j + 1) * COL_CHUNK, :]).astype(BF16)

    return pl.pallas_call(
        body, name=name, grid=(t // tm,),
        in_specs=[pl.BlockSpec((tm, kdim), lambda i: (i, 0)), _resident((n, kdim))],
        out_specs=pl.BlockSpec((tm, n), lambda i: (i, 0)),
        out_shape=jax.ShapeDtypeStruct((t, n), BF16),
        compiler_params=_params(("arbitrary",)),
    )(a, w)


def _attention_bwd(q, k, v, dmixin, sinks, *, seq, name, comm=None):
    t = q.shape[0]
    nblk = seq // BLOCK
    tile = ATTN_TILE_BLOCKS * BLOCK

    def body(q_ref, kp_ref, kc_ref, vp_ref, vc_ref, do_ref, sink_ref,
             dq_ref, dkp_ref, dkc_ref, dvp_ref, dvc_ref, dsink_ref):
        n = pl.program_id(0)

        @pl.when(n == 0)
        def _():
            dsink_ref[...] = jnp.zeros_like(dsink_ref)

        srow = lax.broadcasted_iota(jnp.int32, (8, LANE), 0)
        dsink = jnp.zeros((8, LANE), F32)
        for s in range(ATTN_TILE_BLOCKS):
            rows, k_prev, k_cur, v_prev, v_cur, first = _attn_sub_block(s, n, nblk, kp_ref, kc_ref, vp_ref, vc_ref)
            dqs, dks, dvs = [], [], []
            for g in range(N_KV_HEADS):
                qs, kk, vv, pn, psn = _attn_group(q_ref, rows, k_prev, k_cur, v_prev, v_cur, sink_ref, g, first)
                dos = jnp.concatenate(
                    [do_ref[rows, (GQA_GROUP * g + j) * HEAD_DIM:(GQA_GROUP * g + j + 1) * HEAD_DIM]
                     for j in range(GQA_GROUP)], axis=0)
                dp = _dot_nt(vv, dos)
                delta = jnp.sum(pn * dp, axis=0, keepdims=True)
                ds = pn * (dp - delta)
                dsk = psn * delta
                for j in range(GQA_GROUP):
                    tot = jnp.sum(dsk[:, j * BLOCK:(j + 1) * BLOCK], axis=1, keepdims=True)
                    dsink = dsink - jnp.where(srow == GQA_GROUP * g + j, tot, 0.0)
                dsb = (ds * (HEAD_DIM ** -0.5)).astype(BF16)
                dqs.append(_heads_to_lanes(_dot_tn(kk, dsb)))
                dks.append(jnp.dot(dsb, qs, preferred_element_type=F32))
                dvs.append(jnp.dot(pn.astype(BF16), dos, preferred_element_type=F32))
            dq_ref[rows, :] = jnp.concatenate(dqs, axis=1)
            dkp_ref[rows, :] = jnp.concatenate([x[0:BLOCK, :] for x in dks], axis=1)
            dkc_ref[rows, :] = jnp.concatenate([x[BLOCK:, :] for x in dks], axis=1)
            dvp_ref[rows, :] = jnp.concatenate([x[0:BLOCK, :] for x in dvs], axis=1)
            dvc_ref[rows, :] = jnp.concatenate([x[BLOCK:, :] for x in dvs], axis=1)
        dsink_ref[...] += dsink

    cur = lambda w: pl.BlockSpec((tile, w), lambda n: (n, 0))
    prev = lambda w: pl.BlockSpec((BLOCK, w), lambda n: (jnp.maximum(n * ATTN_TILE_BLOCKS - 1, 0), 0))
    kv = jax.ShapeDtypeStruct((t, KV_WIDTH), F32)
    return _pcall(
        body, name=name, grid=(t // tile,),
        in_specs=[cur(ATTN_WIDTH), prev(KV_WIDTH), cur(KV_WIDTH), prev(KV_WIDTH), cur(KV_WIDTH), cur(ATTN_WIDTH),
                  pl.BlockSpec(memory_space=pltpu.SMEM)],
        out_specs=[cur(ATTN_WIDTH), cur(KV_WIDTH), cur(KV_WIDTH), cur(KV_WIDTH), cur(KV_WIDTH), _full((8, LANE))],
        out_shape=[jax.ShapeDtypeStruct((t, ATTN_WIDTH), F32), kv, kv, kv, kv, jax.ShapeDtypeStruct((8, LANE), F32)],
        args=(q, k, k, v, v, dmixin, sinks), comm=comm)


def _mix_bwd_assemble(dq, dkp, dkc, dvp, dvc, cos, sa, sb, dmixin, ubc, cw, *, seq, name, comm=None):
    t = dq.shape[0]
    cwid = CONV_WIDTH
    tm = min(2 * BLOCK, seq)
    tiles_per_seq = seq // tm
    ntile = t // tm
    nblk_all = t // BLOCK
    per_tile = tm // BLOCK

    def body(*refs):
        dq_ref, dkc_ref, dvc_ref = refs[0:3]
        dkp_refs, dvp_refs = refs[3:3 + per_tile], refs[3 + per_tile:3 + 2 * per_tile]
        (cos_ref, sa_ref, sb_ref, dco_ref, dcon_ref, ubc_ref, hprev_ref, hnext_ref, cw_ref,
         dproj_ref, dcw_ref, zbuf, dybuf) = refs[3 + 2 * per_tile:]
        i = pl.program_id(0)
        first = (i % tiles_per_seq) == 0
        last = (i % tiles_per_seq) == tiles_per_seq - 1
        glast = i == ntile - 1

        @pl.when(i == 0)
        def _():
            dcw_ref[...] = jnp.zeros_like(dcw_ref)

        def with_next_block(cur_ref, nxt_refs):
            nxt = [r[...] for r in nxt_refs]
            nxt[-1] = jnp.where(glast, 0.0, nxt[-1])
            return cur_ref[...] + jnp.concatenate(nxt, axis=0)

        cos_t, sa_t, sb_t = cos_ref[...], sa_ref[...], sb_ref[...]
        for j in range(ATTN_WIDTH // LANE):
            dproj_ref[:, j * LANE:(j + 1) * LANE] = _rope_t(
                dq_ref[:, j * LANE:(j + 1) * LANE], cos_t, sa_t, sb_t).astype(BF16)
        dk = with_next_block(dkc_ref, dkp_refs)
        dproj_ref[:, ATTN_WIDTH:ATTN_WIDTH + KV_WIDTH] = _rope_t(dk, cos_t, sa_t, sb_t).astype(BF16)
        dv = with_next_block(dvc_ref, dvp_refs)
        dproj_ref[:, ATTN_WIDTH + KV_WIDTH:ATTN_WIDTH + 2 * KV_WIDTH] = dv.astype(BF16)

        u, bg, cg = (ubc_ref[:, s * cwid:(s + 1) * cwid].astype(F32) for s in range(3))
        z = cg * u
        hz = hprev_ref[:, 2 * cwid:3 * cwid].astype(F32) * hprev_ref[:, 0:cwid].astype(F32)
        zbuf[0:HALO, :] = jnp.where(first, 0.0, hz)
        zbuf[HALO:HALO + tm, :] = z
        z2, z1 = zbuf[HALO - 2:HALO - 2 + tm, :], zbuf[HALO - 1:HALO - 1 + tm, :]
        w0, w1, w2 = cw_ref[0:1, :], cw_ref[1:2, :], cw_ref[2:3, :]
        y = w0 * z2 + w1 * z1 + w2 * z
        dco = dco_ref[...].astype(F32)
        dyc = dco * bg
        dyn = dcon_ref[...].astype(F32) * hnext_ref[:, cwid:2 * cwid].astype(F32)
        dybuf[0:tm, :] = dyc
        dybuf[tm:tm + HALO, :] = jnp.where(last, 0.0, dyn)
        dz = w2 * dyc + w1 * dybuf[1:1 + tm, :] + w0 * dybuf[2:2 + tm, :]
        srow = lax.broadcasted_iota(jnp.int32, (8, cwid), 0)
        dcw_ref[...] += (jnp.where(srow == 0, _row_sum(dyc * z2), 0.0) + jnp.where(srow == 1, _row_sum(dyc * z1), 0.0)
                         + jnp.where(srow == 2, _row_sum(dyc * z), 0.0))
        base = ATTN_WIDTH + 2 * KV_WIDTH
        dproj_ref[:, base:base + cwid] = (dz * cg).astype(BF16)
        dproj_ref[:, base + cwid:base + 2 * cwid] = (dco * y).astype(BF16)
        dproj_ref[:, base + 2 * cwid:base + 3 * cwid] = (dz * u).astype(BF16)

    cur = lambda w: pl.BlockSpec((tm, w), lambda i: (i, 0))
    nxt = [pl.BlockSpec((BLOCK, KV_WIDTH), lambda i, s=s: (jnp.minimum(i * per_tile + s + 1, nblk_all - 1), 0))
           for s in range(per_tile)]
    prev_halo = pl.BlockSpec((HALO, 3 * cwid), lambda i: (jnp.maximum(i * (tm // HALO) - 1, 0), 0))
    next_halo = lambda w, col: pl.BlockSpec(
        (HALO, w), lambda i: (jnp.minimum((i + 1) * (tm // HALO), t // HALO - 1), col))
    return _pcall(
        body, name=name, grid=(ntile,),
        in_specs=[cur(ATTN_WIDTH), cur(KV_WIDTH), cur(KV_WIDTH), *nxt, *nxt,
                  cur(LANE), cur(LANE), cur(LANE),
                  pl.BlockSpec((tm, cwid), lambda i: (i, 1)), next_halo(cwid, 1),
                  cur(3 * cwid), prev_halo, next_halo(3 * cwid, 0), _full((8, cwid))],
        out_specs=[cur(IN_WIDTH), _full((8, cwid))],
        out_shape=[jax.ShapeDtypeStruct((t, IN_WIDTH), BF16), jax.ShapeDtypeStruct((8, cwid), F32)],
        scratch_shapes=[pltpu.VMEM((tm + HALO, cwid), F32), pltpu.VMEM((tm + HALO, cwid), F32)],
        args=(dq, dkc, dvc, *([dkp] * per_tile), *([dvp] * per_tile), cos, sa, sb, dmixin, dmixin,
              ubc, ubc, ubc, cw), comm=comm)


def _ada_fwd(c_all, w_ada, b_ada_shard, *, name, comm=None):
    nb, d = c_all.shape
    n = w_ada.shape[1]
    tn = n // 2

    def body(c_ref, w_ref, b_ref, o_ref):
        cv = c_ref[...]
        cond = cv * _sigmoid(cv)
        o_ref[...] = jnp.dot(cond, w_ref[...], preferred_element_type=F32,
                             precision=lax.Precision.HIGHEST) + b_ref[...]

    return _pcall(
        body, name=name, grid=(n // tn,),
        in_specs=[_full((nb, d)), pl.BlockSpec((d, tn), lambda j: (0, j)), pl.BlockSpec((1, tn), lambda j: (0, j))],
        out_specs=pl.BlockSpec((nb, tn), lambda j: (0, j)),
        out_shape=jax.ShapeDtypeStruct((nb, n), F32), args=(c_all, w_ada, b_ada_shard), comm=comm)


def _small_finish(gathered, dmod_all, dmod_shard, c_all_t, *, name):
    d = D_MODEL
    nb, n = dmod_shard.shape

    def body(g_ref, dm_ref, dms_ref, ct_ref, sum_ref, gw_ref, gb_ref):
        total = g_ref[0]
        for dev in range(1, N_DEV):
            total = total + g_ref[dev]
        sum_ref[...] = total
        gb_ref[...] = _row_sum(dm_ref[...])
        ctv = ct_ref[...]
        cond_t = (ctv * _sigmoid(ctv)).astype(BF16)
        for jb in range(n // COL_CHUNK):
            gw_ref[:, jb * COL_CHUNK:(jb + 1) * COL_CHUNK] = jnp.dot(
                cond_t, dms_ref[:, jb * COL_CHUNK:(jb + 1) * COL_CHUNK].astype(BF16), preferred_element_type=F32)

    return pl.pallas_call(
        body, name=name, grid=(1,),
        in_specs=[_full((N_DEV, SMALL_ROWS, d)), _full((nb, N_MOD * d)), _full((nb, n)), _full((d, nb))],
        out_specs=[_full((SMALL_ROWS, d)), _full((d, n)), _full((1, N_MOD * d))],
        out_shape=[jax.ShapeDtypeStruct((SMALL_ROWS, d), F32), jax.ShapeDtypeStruct((d, n), F32),
                   jax.ShapeDtypeStruct((1, N_MOD * d), F32)],
        compiler_params=_params(("arbitrary",)),
    )(gathered, dmod_all, dmod_shard, c_all_t)


def _row_tile(r, c, budget=1 << 21):
    if r * c * 4 <= budget or r % 16:
        return r
    best = 16
    for tr in range(16, r + 1, 16):
        if r % tr == 0 and tr * c * 4 <= budget:
            best = tr
    return best


def _cast_into(w, chip, col_kind, *, name):
    r, c = w.shape
    tr = _row_tile(r, c)

    def body(chip_ref, w_ref, o_ref):
        o_ref[...] = w_ref[...].astype(BF16)

    if col_kind:
        out_spec = pl.BlockSpec((tr, c), lambda i, chip_ref: (i, chip_ref[0]))
        out_shape = jax.ShapeDtypeStruct((r, c * N_CHIPS), BF16)
    else:
        out_spec = pl.BlockSpec((tr, c), lambda i, chip_ref: (chip_ref[0] * (r // tr) + i, 0))
        out_shape = jax.ShapeDtypeStruct((r * N_CHIPS, c), BF16)
    return _pcall(body, name=name, grid=(r // tr,), in_specs=[pl.BlockSpec((tr, c), lambda i, chip_ref: (i, 0))],
                  out_specs=out_spec, out_shape=out_shape, args=(w,), prefetch=chip)


def _adamw(w, g, m, v, *, name, comm=None):
    r, c = w.shape
    tr = _row_tile(r, c)
    c1 = 1.0 - ADAM_B1 ** ADAM_STEP
    c2 = 1.0 - ADAM_B2 ** ADAM_STEP

    def body(w_ref, g_ref, m_ref, v_ref, d_ref, nm_ref, nv_ref):
        gv = g_ref[...]
        m2 = ADAM_B1 * m_ref[...] + (1.0 - ADAM_B1) * gv
        v2 = ADAM_B2 * v_ref[...] + (1.0 - ADAM_B2) * (gv * gv)
        d_ref[...] = -ADAM_LR * ((m2 / c1) / (jnp.sqrt(v2 / c2) + ADAM_EPS) + ADAM_WD * w_ref[...])
        nm_ref[...] = m2
        nv_ref[...] = v2

    spec = pl.BlockSpec((tr, c), lambda i: (i, 0))
    sh = jax.ShapeDtypeStruct((r, c), F32)
    return _pcall(body, name=name, grid=(r // tr,), in_specs=[spec] * 4, out_specs=[spec] * 3, out_shape=[sh] * 3,
                  args=(w, g, m, v), comm=comm)


def _sum_final(pos, s32, recv, *, col_kind, n_shard, name, comm=None):
    def body(pos_ref, s_ref, r_ref, o_ref):
        total = ((s_ref[0] + r_ref[0].astype(F32)) + r_ref[1].astype(F32)) + r_ref[2].astype(F32)
        if col_kind:
            o_ref[0] = total
        else:
            o_ref[...] = total

    if col_kind:
        rows, cols = s32.shape[1], n_shard
        tr = _row_tile(rows, cols)
        own = pl.BlockSpec((1, tr, cols), lambda i, pos: (0, i, 2 * pos[0] + pos[1]))
        out_spec = pl.BlockSpec((1, tr, cols), lambda i, pos: (pos[2], i, 0))
        out_shape = jax.ShapeDtypeStruct((2, rows, cols), F32)
    else:
        rows, cols = n_shard, s32.shape[2]
        tr = _row_tile(rows, cols)
        own = pl.BlockSpec((1, tr, cols), lambda i, pos: (0, (2 * pos[0] + pos[1]) * (rows // tr) + i, 0))
        out_spec = pl.BlockSpec((tr, cols), lambda i, pos: (i, pos[2]))
        out_shape = jax.ShapeDtypeStruct((rows, 2 * cols), F32)
    return _pcall(
        body, name=name, grid=(rows // tr,),
        in_specs=[own, pl.BlockSpec((3, tr, cols), lambda i, pos: (0, i, 0))], out_specs=out_spec,
        out_shape=out_shape, args=(s32, recv), prefetch=pos, comm=comm)


def _position():
    return lax.axis_index("x"), lax.axis_index("y"), lax.axis_index("c")


def _allgather8(x_shard, *, name, comm=None):
    m_per, n = x_shard.shape
    nci, nco = (0, 0) if comm is None else (len(comm.inputs), len(comm.out_shapes))

    def body(*refs):
        x_ref, refs = refs[0], refs[1:]
        cin, refs = refs[:nci], refs[nci:]
        out_ref, refs = refs[0], refs[1:]
        cout, refs = refs[:nco], refs[nco:]
        (send_sems, recv_sems, local_sem), csems = refs[:3], refs[3:]
        x, y, c = _position()
        me, sibling = (x, y, c), (x, y, 1 - c)
        chips = [(1 - x, y), (x, 1 - y), (1 - x, 1 - y)]

        def rows(px, py, pc):
            return out_ref.at[pl.ds((4 * px + 2 * py + pc) * m_per, m_per), :]

        def copy(k, block, to, src=None):
            return pltpu.make_async_remote_copy(
                src_ref=rows(*block) if src is None else src, dst_ref=rows(*block),
                send_sem=send_sems.at[k], recv_sem=recv_sems.at[k], device_id=to, device_id_type=MESH)

        mine = pltpu.make_async_copy(x_ref, rows(*me), local_sem)
        mine.start()
        first = [copy(0, me, sibling, src=x_ref)]
        first += [copy(1 + j, me, (*chip, c), src=x_ref) for j, chip in enumerate(chips)]
        for cp in first:
            cp.start()
        if comm is not None:
            comm.start(cin, cout, csems)
        passed = [copy(4 + j, (*chip, c), sibling) for j, chip in enumerate(chips)]
        for j, chip in enumerate(chips):
            copy(1 + j, (*chip, c), me).wait_recv()
            passed[j].start()
        copy(0, sibling, me).wait_recv()
        for j, chip in enumerate(chips):
            copy(4 + j, (*chip, 1 - c), me).wait_recv()
        for cp in first + passed:
            cp.wait_send()
        mine.wait()
        if comm is not None:
            comm.middle(cin, cout, csems)
            comm.late(cin, cout, csems)
            comm.finish(cin, cout, csems)

    vmem = pl.BlockSpec(memory_space=pltpu.VMEM)
    sems = [pltpu.SemaphoreType.DMA((7,)), pltpu.SemaphoreType.DMA((7,)), pltpu.SemaphoreType.DMA]
    out = jax.ShapeDtypeStruct((N_DEV * m_per, n), x_shard.dtype)
    if comm is None:
        return pl.pallas_call(body, name=name, out_shape=out, in_specs=[vmem], out_specs=vmem,
                              scratch_shapes=sems)(x_shard)
    res = pl.pallas_call(
        body, name=name, out_shape=[out] + list(comm.out_shapes), in_specs=[vmem] + [ANY_SPEC] * nci,
        out_specs=[vmem] + [ANY_SPEC] * nco, scratch_shapes=sems + list(comm.sems),
        input_output_aliases={1 + i: 1 + o for i, o in comm.aliases.items()})(x_shard, *comm.inputs)
    return res[0], list(res[1:])


def _peer_chips(x, y):
    return [(1 - x, y), (x, 1 - y), (1 - x, 1 - y)]


class _GatherJob:
    def __init__(self, pieces):
        self.pieces = pieces
        n_p = len(pieces)
        self.inputs = [p[0] for p in pieces]
        self.out_shapes = [jax.ShapeDtypeStruct(p[0].shape, p[0].dtype) for p in pieces]
        for buf, col_kind, r0, nr in pieces:
            half_rows = buf.shape[0] // (2 if col_kind else 2 * N_CHIPS)
            assert r0 % 16 == 0 and nr % 16 == 0 and nr >= 32 and r0 + nr <= half_rows, (buf.shape, r0, nr)
        self.aliases = {p: p for p in range(n_p)}
        dma = pltpu.SemaphoreType.DMA
        self.sems = [dma((2 * n_p,))] * 4 + [dma((4 * n_p,))] * 2

    def _region(self, cout, p, chip_idx, half, part=None):
        buf, col_kind, r0, nr = self.pieces[p]
        first = -(-nr // 32) * 16
        if part == 0:
            nr = first
        elif part == 1:
            r0, nr = r0 + first, nr - first
        if col_kind:
            n = buf.shape[1] // N_CHIPS
            return cout[p].at[pl.ds(half * (buf.shape[0] // 2) + r0, nr), pl.ds(chip_idx * n, n)]
        n = buf.shape[0] // N_CHIPS
        return cout[p].at[pl.ds(chip_idx * n + half * (n // 2) + r0, nr), :]

    def _copies(self, cout, sems):
        send1, recv1, send2, recv2, fsend, frecv = sems
        x, y, c = _position()
        k = 2 * x + y
        sibling = (x, y, 1 - c)
        x_nbr, y_nbr, diag = _peer_chips(x, y)
        chip_of = lambda ch: 2 * ch[0] + ch[1]

        def remote(region, ssem, rsem, to):
            return pltpu.make_async_remote_copy(src_ref=region, dst_ref=region, send_sem=ssem, recv_sem=rsem,
                                                device_id=to, device_id_type=MESH)

        hop1, arrived1, hop2, arrived2, fwds, fwd_arrived = [], [], [], [], [], []
        for p in range(len(self.pieces)):
            for j, nbr in enumerate((x_nbr, y_nbr)):
                i1 = 2 * p + j
                hop1.append(remote(self._region(cout, p, k, c), send1.at[i1], recv1.at[i1], (*nbr, c)))
                arrived1.append(remote(self._region(cout, p, chip_of(nbr), c), send1.at[i1], recv1.at[i1], (*nbr, c)))
            hop2.append(remote(self._region(cout, p, chip_of(x_nbr), c, 0), send2.at[2 * p], recv2.at[2 * p],
                               (*y_nbr, c)))
            hop2.append(remote(self._region(cout, p, chip_of(y_nbr), c, 1), send2.at[2 * p + 1], recv2.at[2 * p + 1],
                               (*x_nbr, c)))
            arrived2.append(remote(self._region(cout, p, chip_of(diag), c, 0), send2.at[2 * p], recv2.at[2 * p],
                                   (*y_nbr, c)))
            arrived2.append(remote(self._region(cout, p, chip_of(diag), c, 1), send2.at[2 * p + 1],
                                   recv2.at[2 * p + 1], (*x_nbr, c)))
            landed = [(chip_of(x_nbr), None), (chip_of(y_nbr), None), (chip_of(diag), 0), (chip_of(diag), 1)]
            for q, (chip_idx, part) in enumerate(landed):
                i3 = 4 * p + q
                fwds.append(remote(self._region(cout, p, chip_idx, c, part), fsend.at[i3], frecv.at[i3], sibling))
                fwd_arrived.append(remote(self._region(cout, p, chip_idx, 1 - c, part), fsend.at[i3], frecv.at[i3],
                                          sibling))
        return hop1, arrived1, hop2, arrived2, fwds, fwd_arrived

    def start(self, cin, cout, sems):
        for cp in self._copies(cout, sems)[0]:
            cp.start()

    def middle(self, cin, cout, sems):
        _, arrived1, hop2, _, fwds, _ = self._copies(cout, sems)
        for p in range(len(self.pieces)):
            for j in range(2):
                arrived1[2 * p + j].wait_recv()
                hop2[2 * p + j].start()
                fwds[4 * p + j].start()

    def late(self, cin, cout, sems):
        _, _, _, arrived2, fwds, _ = self._copies(cout, sems)
        for p in range(len(self.pieces)):
            for j in range(2):
                arrived2[2 * p + j].wait_recv()
                fwds[4 * p + 2 + j].start()

    def finish(self, cin, cout, sems):
        hop1, _, hop2, _, fwds, fwd_arrived = self._copies(cout, sems)
        for cp in fwd_arrived:
            cp.wait_recv()
        for cp in hop1 + hop2 + fwds:
            cp.wait_send()


class _PairedJob:
    aliases = {}

    def start(self, cin, cout, sems):
        for cp in self._copies(cin, cout, sems):
            cp.start()

    def middle(self, cin, cout, sems):
        pass

    late = middle

    def finish(self, cin, cout, sems):
        copies = self._copies(cin, cout, sems)
        for cp in copies:
            cp.wait_recv()
        for cp in copies:
            cp.wait_send()


class _ExchangeJob(_PairedJob):
    def __init__(self, s16, kinds, sizes):
        self.inputs, self.kinds, self.sizes = list(s16), list(kinds), list(sizes)
        self.out_shapes = [jax.ShapeDtypeStruct((3, s.shape[1], n) if kd else (3, n, s.shape[2]), s.dtype)
                           for s, kd, n in zip(s16, kinds, sizes)]
        self.sems = [pltpu.SemaphoreType.DMA((3 * len(s16),)), pltpu.SemaphoreType.DMA((3 * len(s16),))]

    def _copies(self, cin, cout, sems):
        send_sems, recv_sems = sems
        x, y, c = _position()
        copies = []
        for p, src_ref in enumerate(cin):
            for j, chip in enumerate(_peer_chips(x, y)):
                kk = 2 * chip[0] + chip[1]
                n = self.sizes[p]
                src = src_ref.at[0, :, pl.ds(kk * n, n)] if self.kinds[p] else src_ref.at[0, pl.ds(kk * n, n), :]
                copies.append(pltpu.make_async_remote_copy(
                    src_ref=src, dst_ref=cout[p].at[j], send_sem=send_sems.at[3 * p + j],
                    recv_sem=recv_sems.at[3 * p + j], device_id=(*chip, c), device_id_type=MESH))
        return copies


class _ShareJob:
    def __init__(self, halves):
        self.inputs = list(halves)
        self.out_shapes = [jax.ShapeDtypeStruct(h.shape, h.dtype) for h in halves]
        self.aliases = {p: p for p in range(len(halves))}
        self.sems = [pltpu.SemaphoreType.DMA((len(halves),)), pltpu.SemaphoreType.DMA((len(halves),))]

    def _copies(self, cout, sems, half):
        send_sems, recv_sems = sems
        x, y, c = _position()
        h = c if half == "mine" else 1 - c

        def region(o):
            if len(o.shape) == 3:
                return o.at[h]
            hc = o.shape[1] // 2
            return o.at[:, pl.ds(h * hc, hc)]

        return [pltpu.make_async_remote_copy(
            src_ref=region(o), dst_ref=region(o), send_sem=send_sems.at[p], recv_sem=recv_sems.at[p],
            device_id=(x, y, 1 - c), device_id_type=MESH) for p, o in enumerate(cout)]

    def start(self, cin, cout, sems):
        for cp in self._copies(cout, sems, "mine"):
            cp.start()

    def middle(self, cin, cout, sems):
        pass

    late = middle

    def finish(self, cin, cout, sems):
        for cp in self._copies(cout, sems, "theirs"):
            cp.wait_recv()
        for cp in self._copies(cout, sems, "mine"):
            cp.wait_send()


class _MultiJob:
    def __init__(self, jobs):
        self.jobs = jobs
        self.inputs = [a for j in jobs for a in j.inputs]
        self.out_shapes = [s for j in jobs for s in j.out_shapes]
        self.sems = [s for j in jobs for s in j.sems]
        self.aliases = {}
        i0 = o0 = 0
        for j in jobs:
            for i, o in j.aliases.items():
                self.aliases[i0 + i] = o0 + o
            i0 += len(j.inputs)
            o0 += len(j.out_shapes)

    def _parts(self, cin, cout, sems):
        i0 = o0 = s0 = 0
        for j in self.jobs:
            ni, no, ns = len(j.inputs), len(j.out_shapes), len(j.sems)
            yield j, cin[i0:i0 + ni], cout[o0:o0 + no], sems[s0:s0 + ns]
            i0, o0, s0 = i0 + ni, o0 + no, s0 + ns

    def start(self, cin, cout, sems):
        for j, a, b, s in self._parts(cin, cout, sems):
            j.start(a, b, s)

    def middle(self, cin, cout, sems):
        for j, a, b, s in self._parts(cin, cout, sems):
            j.middle(a, b, s)

    def late(self, cin, cout, sems):
        for j, a, b, s in self._parts(cin, cout, sems):
            j.late(a, b, s)

    def finish(self, cin, cout, sems):
        for j, a, b, s in self._parts(cin, cout, sems):
            j.finish(a, b, s)


def _rope_tables(positions):
    half = ROT_DIM // 2
    inv_freq = jnp.power(jnp.float32(ROPE_THETA), -jnp.arange(0, ROT_DIM, 2, dtype=F32) / ROT_DIM)
    inv_head = jnp.concatenate([inv_freq, inv_freq, jnp.zeros((HEAD_DIM - ROT_DIM,), F32)])
    inv_lane = jnp.concatenate([inv_head] * (LANE // HEAD_DIM))
    ang = positions.astype(F32).reshape(-1)[:, None] * inv_lane[None, :]
    sin = jnp.sin(ang)
    dim = jnp.arange(LANE) % HEAD_DIM
    return jnp.cos(ang), jnp.where(dim < half, -sin, 0.0), jnp.where(dim >= half, sin, 0.0)


def kernel(x, c, positions, w_ada, b_ada, ffn1_w_gate_up, ffn1_w_down, ln1_g, ln1_b, w_in, conv_w, attn_sinks, w_out, ln2_g, ln2_b, ffn2_w_gate_up, ffn2_w_down, ln3_g, ln3_b, loss_target, m_w_ada, m_b_ada, m_ffn1_w_gate_up, m_ffn1_w_down, m_ln1_g, m_ln1_b, m_w_in, m_conv_w, m_attn_sinks, m_w_out, m_ln2_g, m_ln2_b, m_ffn2_w_gate_up, m_ffn2_w_down, m_ln3_g, m_ln3_b, v_w_ada, v_b_ada, v_ffn1_w_gate_up, v_ffn1_w_down, v_ln1_g, v_ln1_b, v_w_in, v_conv_w, v_attn_sinks, v_w_out, v_ln2_g, v_ln2_b, v_ffn2_w_gate_up, v_ffn2_w_down, v_ln3_g, v_ln3_b):
    d = D_MODEL
    nb, seq, _ = x.shape
    t = nb * seq
    f = ffn1_w_down.shape[1] * N_CHIPS
    ax, ay, ac = _position()
    chip = 2 * ax + ay
    dev = 2 * chip + ac
    pos = jnp.stack([ax, ay, ac]).astype(jnp.int32)

    x2 = x.reshape(t, d)
    tgt2 = loss_target.reshape(t, d)
    ln1 = jnp.concatenate([ln1_g, ln1_b], axis=0)
    ln2 = jnp.concatenate([ln2_g, ln2_b], axis=0)
    ln3 = jnp.concatenate([ln3_g, ln3_b], axis=0)
    sinks = attn_sinks.reshape(N_Q_HEADS)
    cos_t, sa_t, sb_t = _rope_tables(positions)

    gu_cuts = [0, 176, 352, d // 2]
    gu_part = lambda buf, s: (buf, True, gu_cuts[s], gu_cuts[s + 1] - gu_cuts[s])
    chip_arr = jnp.reshape(chip, (1,)).astype(jnp.int32)
    b_gu1 = _cast_into(ffn1_w_gate_up[0], chip_arr, True, name="cast_gu1")

    n_ada = w_ada.shape[2]
    c_all, (b_gu1,) = _allgather8(c.reshape(nb * d // LANE, LANE), name="gather_c", comm=_GatherJob([gu_part(b_gu1, 0)]))
    c_all = c_all.reshape(N_DEV * nb, d)
    b_shard = lax.dynamic_slice(b_ada, (0, chip * n_ada), (1, n_ada))
    mod_part, (b_gu1,) = _ada_fwd(c_all, w_ada[0], b_shard, name="ada_fwd", comm=_GatherJob([gu_part(b_gu1, 1)]))
    conv_rows = jnp.pad(conv_w[0], ((0, 5), (0, n_ada - conv_w.shape[2])))
    part = jnp.concatenate([mod_part, conv_rows], axis=0)
    parts, (wgu1,) = _allgather8(part, name="gather_mod", comm=_GatherJob([gu_part(b_gu1, 2)]))
    parts = parts.reshape(N_DEV, N_DEV * nb + 8, n_ada)
    mod_all = jnp.concatenate([parts[2 * k, :N_DEV * nb, :] for k in range(N_CHIPS)], axis=1)
    mod = lax.dynamic_slice(mod_all, (dev * nb, 0), (nb, N_MOD * d)).reshape(nb, N_MOD, d)
    cw_full = jnp.concatenate([parts[2 * k, N_DEV * nb:, :conv_w.shape[2]] for k in range(N_CHIPS)], axis=1)

    b_d1 = _cast_into(ffn1_w_down[0], chip_arr, False, name="cast_d1")
    b_in = _cast_into(w_in[0].T, chip_arr, False, name="cast_in")
    b_out = _cast_into(w_out[0], chip_arr, False, name="cast_out")
    b_gu2 = _cast_into(ffn2_w_gate_up[0], chip_arr, True, name="cast_gu2")
    b_d2 = _cast_into(ffn2_w_down[0], chip_arr, False, name="cast_d2")
    n_gu, n_d, n_in, n_out = (ffn1_w_gate_up.shape[2], ffn1_w_down.shape[1], w_in.shape[2], w_out.shape[1])

    def whole(buf, col_kind):
        return (buf, col_kind, 0, buf.shape[0] // (2 if col_kind else 2 * N_CHIPS))

    (h1, a1, dact1), (wd1, wout) = _ffn_up(x2, ln1, mod, wgu1, seq=seq, sc_idx=1, sh_idx=0, use_ln=False,
                                         name="ffn1_up", comm=_GatherJob([whole(b_d1, False), whole(b_out, False)]))
    (f1, xhat1, rstd1), (win_t,) = _ffn_down_ln(a1, wd1, x2, ln1, mod, seq=seq, gate_idx=2, use_ln=False,
                                                name="ffn1_down", comm=_GatherJob([whole(b_in, False)]))
    (h2, q, k, v, ubc), (b_gu2,) = _in_proj(
        xhat1, ln1, mod, win_t, cos_t, sa_t, sb_t, seq=seq, sc_idx=4, sh_idx=3, name="in_proj",
        comm=_GatherJob([gu_part(b_gu2, 0)]))
    attn, (b_gu2,) = _attention(q, k, v, sinks, seq=seq, name="attention", comm=_GatherJob([gu_part(b_gu2, 1)]))
    (mixin, mix, xhat2, rstd2), (wgu2,) = _out_proj(
        attn, ubc, cw_full, wout, xhat1, ln1, mod, seq=seq, gate_idx=5, name="out_proj",
        comm=_GatherJob([gu_part(b_gu2, 2)]))
    (h3, a3, dact3), (wd2,) = _ffn_up(xhat2, ln2, mod, wgu2, seq=seq, sc_idx=7, sh_idx=6, use_ln=True, name="ffn2_up",
                                    comm=_GatherJob([whole(b_d2, False)]))
    dr3, df3, loss_cols, dln3g, dln3b, dgate3 = _ffn_down_loss(
        a3, wd2, xhat2, ln2, mod, ln3, tgt2, seq=seq, gate_idx=8, name="ffn2_down_loss")

    dgu3 = _ffn_bwd_act(df3, wd2, dact3, seq=seq, name="ffn2_bwd_act")
    s32_d2, s16_d2 = _grad_chip_sum(pos, a3, df3, half_on_rows=False, name="grad_wd2")
    (s32_gu2, s16_gu2), (recv_d2,) = _grad_chip_sum(pos, h3, dgu3, half_on_rows=True, name="grad_wgu2",
                                                    comm=_ExchangeJob([s16_d2], [False], [n_d]))
    (dr2, dmix, dsc3, dsh3, dgate2, dln2g, dln2b), (recv_gu2,) = _bwd_in(
        dgu3, wgu2, dr3, xhat2, rstd2, ln2, mod, mix, seq=seq, w_is_nt=True, sc_idx=7, gate_idx=5,
        branch_scale=1.0, final=False, name="ffn2_bwd_in", comm=_ExchangeJob([s16_gu2], [True], [n_gu]))
    s32_out, s16_out = _grad_chip_sum(pos, mixin, dmix, half_on_rows=False, name="grad_wout")
    dmixin = _matmul_nt_bf16(dmix, wout, seq=seq, name="out_proj_bwd")
    (dq, dkp, dkc, dvp, dvc, dsink), (recv_out,) = _attention_bwd(
        q, k, v, dmixin, sinks, seq=seq, name="attention_bwd", comm=_ExchangeJob([s16_out], [False], [n_out]))
    dproj, dcw = _mix_bwd_assemble(
        dq, dkp, dkc, dvp, dvc, cos_t, sa_t, sb_t, dmixin, ubc, cw_full, seq=seq, name="mix_bwd")
    s32_in, s16_in = _grad_chip_sum(pos, dproj, h2, half_on_rows=False, name="grad_win")
    (dr1, df1, dsc2, dsh2, dgate1, dln1g, dln1b), (recv_in,) = _bwd_in(
        dproj, win_t, dr2, xhat1, rstd1, ln1, mod, f1, seq=seq, w_is_nt=False, sc_idx=4, gate_idx=2,
        branch_scale=0.5, final=False, name="in_proj_bwd", comm=_ExchangeJob([s16_in], [False], [n_in]))
    s32_d1, s16_d1 = _grad_chip_sum(pos, a1, df1, half_on_rows=False, name="grad_wd1")
    dgu1, (recv_d1,) = _ffn_bwd_act(df1, wd1, dact1, seq=seq, name="ffn1_bwd_act",
                                    comm=_ExchangeJob([s16_d1], [False], [n_d]))
    s32_gu1, s16_gu1 = _grad_chip_sum(pos, h1, dgu1, half_on_rows=True, name="grad_wgu1")

    def final_half(s32_, recv_, col_kind, n_shard, name_):
        return _sum_final(pos, s32_, recv_, col_kind=col_kind, n_shard=n_shard, name=name_)

    early = [final_half(s32_gu2, recv_gu2, True, n_gu, "sum_final_gu2"),
             final_half(s32_d2, recv_d2, False, n_d, "sum_final_d2"),
             final_half(s32_out, recv_out, False, n_out, "sum_final_out"),
             final_half(s32_in, recv_in, False, n_in, "sum_final_in"),
             final_half(s32_d1, recv_d1, False, n_d, "sum_final_d1")]
    (grad_x, dsc1, dsh1), (recv_gu1, full_gu2, full_d2, full_out, full_in, full_d1) = _bwd_in(
        dgu1, wgu1, dr1, x2, None, None, mod, None, seq=seq, w_is_nt=True, sc_idx=1, gate_idx=None,
        branch_scale=None, final=True, name="ffn1_bwd_in",
        comm=_MultiJob([_ExchangeJob([s16_gu1], [True], [n_gu]), _ShareJob(early)]))
    late = [final_half(s32_gu1, recv_gu1, True, n_gu, "sum_final_gu1")]

    dmod = jnp.concatenate([dsh1, dsc1, dgate1, dsh2, dsc2, dgate2, dsh3, dsc3, dgate3], axis=1)
    loss_row = jnp.sum(loss_cols, axis=1, keepdims=True) * (0.5 / d)
    lane_row = lambda a: jnp.pad(a, ((0, 0), (0, d - a.shape[1])))
    block = jnp.concatenate(
        [dmod.reshape(nb * N_MOD, d), dln1g, dln1b, dln2g, dln2b, dln3g, dln3b,
         lane_row(dcw[0:3, :]), lane_row(dsink[:, 0:1].reshape(1, N_Q_HEADS)), lane_row(loss_row)], axis=0)
    block = jnp.pad(block, ((0, SMALL_ROWS - block.shape[0]), (0, 0)))
    gathered, (full_gu1,) = _allgather8(block, name="gather_small", comm=_ShareJob(late))
    gathered = gathered.reshape(N_DEV, SMALL_ROWS, d)
    dmod_all = gathered[:, :nb * N_MOD, :].reshape(N_DEV * nb, N_MOD * d)
    dmod_shard = lax.dynamic_slice(dmod_all, (0, chip * n_ada), (N_DEV * nb, n_ada))
    small, g_w_ada, g_b_ada = _small_finish(gathered, dmod_all, dmod_shard, c_all.T, name="small_finish")
    r0 = nb * N_MOD
    loss = small[r0 + 10, 0]
    g_ln = [small[r0 + i:r0 + i + 1, :] for i in range(6)]
    g_cw_full = small[r0 + 6:r0 + 9, :CONV_WIDTH]
    g_conv = lax.dynamic_slice(g_cw_full, (0, chip * conv_w.shape[2]), (3, conv_w.shape[2]))
    g_sinks = small[r0 + 9:r0 + 10, :N_Q_HEADS]

    def flat2(a):
        return a.reshape(-1, a.shape[-1])

    def unhalve(a):
        return a.reshape(2 * a.shape[1], a.shape[2])

    results = {}

    def adamw(name_, w_, g_, m_, v_):
        g2 = flat2(g_)
        dl, nm, nv = _adamw(flat2(w_), g2, flat2(m_), flat2(v_), name="adamw_" + name_)
        results[name_] = tuple(a.reshape(w_.shape) for a in (g2, dl, nm, nv))

    adamw("w_ada", w_ada, g_w_ada, m_w_ada, v_w_ada)
    adamw("ffn2_w_gate_up", ffn2_w_gate_up, unhalve(full_gu2), m_ffn2_w_gate_up, v_ffn2_w_gate_up)
    adamw("ffn2_w_down", ffn2_w_down, full_d2, m_ffn2_w_down, v_ffn2_w_down)
    adamw("w_out", w_out, full_out, m_w_out, v_w_out)
    adamw("w_in", w_in, full_in.T, m_w_in, v_w_in)
    adamw("ffn1_w_gate_up", ffn1_w_gate_up, unhalve(full_gu1), m_ffn1_w_gate_up, v_ffn1_w_gate_up)
    adamw("ffn1_w_down", ffn1_w_down, full_d1, m_ffn1_w_down, v_ffn1_w_down)
    adamw("b_ada", b_ada, g_b_ada, m_b_ada, v_b_ada)
    adamw("ln1_g", ln1_g, g_ln[0], m_ln1_g, v_ln1_g)
    adamw("ln1_b", ln1_b, g_ln[1], m_ln1_b, v_ln1_b)
    adamw("ln2_g", ln2_g, g_ln[2], m_ln2_g, v_ln2_g)
    adamw("ln2_b", ln2_b, g_ln[3], m_ln2_b, v_ln2_b)
    adamw("ln3_g", ln3_g, g_ln[4], m_ln3_g, v_ln3_g)
    adamw("ln3_b", ln3_b, g_ln[5], m_ln3_b, v_ln3_b)
    adamw("conv_w", conv_w, g_conv, m_conv_w, v_conv_w)
    adamw("attn_sinks", attn_sinks, g_sinks, m_attn_sinks, v_attn_sinks)
    order = ["w_ada", "b_ada", "ffn1_w_gate_up", "ffn1_w_down", "ln1_g", "ln1_b", "w_in", "conv_w", "attn_sinks",
             "w_out", "ln2_g", "ln2_b", "ffn2_w_gate_up", "ffn2_w_down", "ln3_g", "ln3_b"]
    return (loss, grad_x.reshape(x.shape), *[results[n_][0] for n_ in order], *[results[n_][1] for n_ in order],
            *[results[n_][2] for n_ in order], *[results[n_][3] for n_ in order])
```

```python
import jax
import jax.numpy as jnp
from jax import lax
from jax.experimental import pallas as pl
from jax.experimental.pallas import tpu as pltpu

F32 = jnp.float32
BF16 = jnp.bfloat16
MESH = pl.DeviceIdType.MESH

D_MODEL = 1024
HEAD_DIM = 64
ATTN_WIDTH = 512
CONV_WIDTH = 512
N_Q_HEADS = 8
N_KV_HEADS = 2
GQA_GROUP = 4
KV_WIDTH = 128
WINDOW = 128
BLOCK = 128
ROT_DIM = 16
ROPE_THETA = 500000.0
N_MOD = 9
LN_EPS = 1e-5
DN_ALPHA = 2.0 ** 0.25
IN_WIDTH = 2304
N_CHIPS = 4
N_DEV = 8
SMALL_ROWS = 32

ADAM_LR = 0.001
ADAM_B1 = 0.9
ADAM_B2 = 0.999
ADAM_EPS = 1e-08
ADAM_WD = 0.01
ADAM_STEP = 10

LANE = 128
HALO = 16
COL_CHUNK = 256
VMEM_LIMIT = 56 * 1024 * 1024


def _params(sem=None, vmem=True):
    return pltpu.CompilerParams(dimension_semantics=sem, vmem_limit_bytes=VMEM_LIMIT if vmem else None)


def _sigmoid(g):
    return 0.5 * jnp.tanh(0.5 * g) + 0.5


def _row_sum(v):
    return jnp.sum(v, axis=0, keepdims=True)


ROW_CHUNK = 16
EPILOGUE_UNROLL = 8


def _fold8(v):
    return v[0:8, :] + v[8:16, :]


def _row_chunk_loop(n_rows, step, init):
    per_iter = ROW_CHUNK * EPILOGUE_UNROLL
    assert n_rows % per_iter == 0, n_rows

    def body(it, carry):
        for s in range(EPILOGUE_UNROLL):
            start = pl.multiple_of(it * per_iter + s * ROW_CHUNK, ROW_CHUNK)
            carry = step(pl.ds(start, ROW_CHUNK), carry)
        return carry

    return lax.fori_loop(0, n_rows // per_iter, body, init)


def _ln_stats(r):
    mu = jnp.mean(r, axis=-1, keepdims=True)
    rc = r - mu
    var = jnp.mean(rc * rc, axis=-1, keepdims=True)
    rstd = lax.rsqrt(var + LN_EPS)
    return rc * rstd, rstd


def _ln_bwd(dxo, xhat, rstd, g):
    dxhat = dxo * g
    m1 = jnp.mean(dxhat, axis=-1, keepdims=True)
    m2 = jnp.mean(dxhat * xhat, axis=-1, keepdims=True)
    return rstd * (dxhat - m1 - xhat * m2)


def _dot_nt(a, b):
    return lax.dot_general(a, b, (((1,), (1,)), ((), ())), preferred_element_type=F32)


def _dot_tn(a, b):
    return lax.dot_general(a, b, (((0,), (0,)), ((), ())), preferred_element_type=F32)


def _full(shape):
    nd = len(shape)
    return pl.BlockSpec(shape, lambda *_: (0,) * nd)


def _resident(shape):
    nd = len(shape)
    return pl.BlockSpec(shape, lambda *_: (0,) * nd, pipeline_mode=pl.Buffered(1))


ANY_SPEC = pl.BlockSpec(memory_space=pl.ANY)


def _pcall(body, *, name, grid, in_specs, out_specs, out_shape, args, scratch_shapes=(), comm=None, prefetch=None):
    single = not isinstance(out_shape, (list, tuple))
    out_specs = [out_specs] if single else list(out_specs)
    out_shape = [out_shape] if single else list(out_shape)
    in_specs = list(in_specs)
    scratch_shapes = list(scratch_shapes)
    sem = ("arbitrary",) * len(grid)
    n_pre = 0 if prefetch is None else 1
    pre_args = () if prefetch is None else (prefetch,)

    def call(fn, ins_, outs_, shapes_, scratch_, aliases_, operands):
        if prefetch is None:
            return pl.pallas_call(fn, name=name, grid=grid, in_specs=ins_, out_specs=outs_, out_shape=shapes_,
                                  scratch_shapes=scratch_, input_output_aliases=aliases_,
                                  compiler_params=_params(sem))(*operands)
        spec = pltpu.PrefetchScalarGridSpec(num_scalar_prefetch=1, grid=grid, in_specs=ins_, out_specs=outs_,
                                            scratch_shapes=scratch_)
        return pl.pallas_call(fn, name=name, grid_spec=spec, out_shape=shapes_,
                              input_output_aliases={n_pre + i: o for i, o in aliases_.items()},
                              compiler_params=_params(sem))(*pre_args, *operands)

    if comm is None:
        res = call(body, in_specs, out_specs, out_shape, scratch_shapes, {}, args)
        return res[0] if single else res
    n_in, n_out, n_scr = len(in_specs), len(out_specs), len(scratch_shapes)
    nci, nco = len(comm.inputs), len(comm.out_shapes)
    n_steps = 1
    for g in grid:
        n_steps *= g
    staged = n_steps >= 8
    middle_step = (n_steps * 5) // 8 - 1
    late_step = n_steps - 1 - max(1, n_steps // 8)

    def wrapped(*refs):
        pre, refs = refs[:n_pre], refs[n_pre:]
        ins, refs = refs[:n_in], refs[n_in:]
        cin, refs = refs[:nci], refs[nci:]
        outs, refs = refs[:n_out], refs[n_out:]
        cout, refs = refs[:nco], refs[nco:]
        scr, csems = refs[:n_scr], refs[n_scr:]
        step = pl.program_id(0)
        for ax in range(1, len(grid)):
            step = step * grid[ax] + pl.program_id(ax)

        @pl.when(step == 0)
        def _():
            comm.start(cin, cout, csems)

        body(*pre, *ins, *outs, *scr)

        if staged:
            @pl.when(step == middle_step)
            def _():
                comm.middle(cin, cout, csems)

            @pl.when(step == late_step)
            def _():
                comm.late(cin, cout, csems)

        @pl.when(step == n_steps - 1)
        def _():
            if not staged:
                comm.middle(cin, cout, csems)
                comm.late(cin, cout, csems)
            comm.finish(cin, cout, csems)

    res = call(wrapped, in_specs + [ANY_SPEC] * nci, out_specs + [ANY_SPEC] * nco,
               out_shape + list(comm.out_shapes), scratch_shapes + list(comm.sems),
               {n_in + i: n_out + o for i, o in comm.aliases.items()}, (*args, *comm.inputs))
    main = res[:n_out]
    return (main[0] if single else main), list(res[n_out:])


def _ffn_up(xin, lnp, mod, w, *, seq, sc_idx, sh_idx, use_ln, name, comm=None):
    t, d = xin.shape
    f = w.shape[1] // 2
    tm = min(512, seq)
    tpb = seq // tm
    ch = min(COL_CHUNK, f)

    def body(x_ref, ln_ref, mod_ref, w_ref, h_ref, a_ref, dact_ref):
        x = x_ref[...]
        if use_ln:
            x = x * ln_ref[0:1, :] + ln_ref[1:2, :]
        h = x * (1.0 + mod_ref[0, sc_idx:sc_idx + 1, :]) + mod_ref[0, sh_idx:sh_idx + 1, :]
        hb = h.astype(BF16)
        h_ref[...] = hb
        for j in range(f // ch):
            g = jnp.dot(hb, w_ref[:, j * ch:(j + 1) * ch], preferred_element_type=F32)
            u = jnp.dot(hb, w_ref[:, f + j * ch:f + (j + 1) * ch], preferred_element_type=F32)
            s = _sigmoid(g)
            silu = g * s
            a_ref[:, j * ch:(j + 1) * ch] = (silu * u).astype(BF16)
            dact_ref[:, j * ch:(j + 1) * ch] = (u * (s + silu * (1.0 - s))).astype(BF16)
            dact_ref[:, f + j * ch:f + (j + 1) * ch] = silu.astype(BF16)

    return _pcall(
        body, name=name, grid=(t // tm,),
        in_specs=[pl.BlockSpec((tm, d), lambda i: (i, 0)), _full((2, d)),
                  pl.BlockSpec((1, N_MOD, d), lambda i: (i // tpb, 0, 0)), _resident((d, 2 * f))],
        out_specs=[pl.BlockSpec((tm, d), lambda i: (i, 0)), pl.BlockSpec((tm, f), lambda i: (i, 0)),
                   pl.BlockSpec((tm, 2 * f), lambda i: (i, 0))],
        out_shape=[jax.ShapeDtypeStruct((t, d), BF16), jax.ShapeDtypeStruct((t, f), BF16),
                   jax.ShapeDtypeStruct((t, 2 * f), BF16)],
        args=(xin, lnp, mod, w), comm=comm)


def _ffn_down_ln(a, wd, xin, lnp_in, mod, *, seq, gate_idx, use_ln, name, comm=None):
    t, f = a.shape
    d = wd.shape[1]
    tm = min(512, seq)
    tpb = seq // tm

    def body(a_ref, wd_ref, x_ref, ln_ref, mod_ref, f_ref, xhat_ref, rstd_ref, acc):
        av = a_ref[...]
        for j in range(d // COL_CHUNK):
            acc[:, j * COL_CHUNK:(j + 1) * COL_CHUNK] = jnp.dot(
                av, wd_ref[:, j * COL_CHUNK:(j + 1) * COL_CHUNK], preferred_element_type=F32)
        scale = 0.5 * (1.0 + mod_ref[0, gate_idx:gate_idx + 1, :])

        fo = acc[...]
        x = x_ref[...]
        if use_ln:
            x = x * ln_ref[0:1, :] + ln_ref[1:2, :]
        xhat, rstd = _ln_stats(DN_ALPHA * x + scale * fo)
        f_ref[...] = fo.astype(BF16)
        xhat_ref[...] = xhat
        rstd_ref[...] = rstd

    return _pcall(
        body, name=name, grid=(t // tm,),
        in_specs=[pl.BlockSpec((tm, f), lambda i: (i, 0)), _resident((f, d)),
                  pl.BlockSpec((tm, d), lambda i: (i, 0)), _full((2, d)),
                  pl.BlockSpec((1, N_MOD, d), lambda i: (i // tpb, 0, 0))],
        out_specs=[pl.BlockSpec((tm, d), lambda i: (i, 0)), pl.BlockSpec((tm, d), lambda i: (i, 0)),
                   pl.BlockSpec((tm, 1), lambda i: (i, 0))],
        out_shape=[jax.ShapeDtypeStruct((t, d), BF16), jax.ShapeDtypeStruct((t, d), F32),
                   jax.ShapeDtypeStruct((t, 1), F32)],
        scratch_shapes=[pltpu.VMEM((tm, d), F32)],
        args=(a, wd, xin, lnp_in, mod), comm=comm)


def _ffn_down_loss(a, wd, xhat_in, lnp_in, mod, lnp_out, tgt, *, seq, gate_idx, name):
    t, f = a.shape
    d = wd.shape[1]
    nb = t // seq
    tm = min(512, seq)
    tpb = seq // tm

    def body(a_ref, wd_ref, x_ref, lnin_ref, mod_ref, lnout_ref, tgt_ref,
             dr_ref, df_ref, loss_ref, dg_ref, db_ref, dgate_ref, acc):
        i = pl.program_id(0)
        av = a_ref[...]
        for j in range(d // COL_CHUNK):
            acc[:, j * COL_CHUNK:(j + 1) * COL_CHUNK] = jnp.dot(
                av, wd_ref[:, j * COL_CHUNK:(j + 1) * COL_CHUNK], preferred_element_type=F32)
        scale = 0.5 * (1.0 + mod_ref[0, gate_idx:gate_idx + 1, :])
        ag_in, ab_in = DN_ALPHA * lnin_ref[0:1, :], DN_ALPHA * lnin_ref[1:2, :]
        g_out, b_out = lnout_ref[0:1, :], lnout_ref[1:2, :]
        g_over_d = g_out * (1.0 / d)

        def chunk(rows, carry):
            s_loss, s_dg, s_db, s_gate = carry
            fo = acc[rows, :]
            xhat, rstd = _ln_stats(x_ref[rows, :] * ag_in + ab_in + scale * fo)
            e = xhat * g_out + b_out - tgt_ref[rows, :]
            dr = _ln_bwd(e, xhat, rstd, g_over_d)
            dr_ref[rows, :] = dr
            df_ref[rows, :] = (scale * dr).astype(BF16)
            return s_loss + _fold8(e * e), s_dg + _fold8(e * xhat), s_db + _fold8(e), s_gate + _fold8(fo * dr)

        zero = jnp.zeros((8, d), F32)
        s_loss, s_dg, s_db, s_gate = _row_chunk_loop(tm, chunk, (zero, zero, zero, zero))
        s_dg, s_db, s_gate = s_dg * (1.0 / d), s_db * (1.0 / d), s_gate * 0.5

        @pl.when(i == 0)
        def _():
            loss_ref[...] = jnp.zeros_like(loss_ref)
            dg_ref[...] = jnp.zeros_like(dg_ref)
            db_ref[...] = jnp.zeros_like(db_ref)

        @pl.when(i % tpb == 0)
        def _():
            dgate_ref[...] = jnp.zeros_like(dgate_ref)

        loss_ref[...] += _row_sum(s_loss)
        dg_ref[...] += _row_sum(s_dg)
        db_ref[...] += _row_sum(s_db)
        dgate_ref[0] += _row_sum(s_gate)

    return pl.pallas_call(
        body, name=name, grid=(t // tm,), scratch_shapes=[pltpu.VMEM((tm, d), F32)],
        in_specs=[pl.BlockSpec((tm, f), lambda i: (i, 0)), _resident((f, d)),
                  pl.BlockSpec((tm, d), lambda i: (i, 0)), _full((2, d)),
                  pl.BlockSpec((1, N_MOD, d), lambda i: (i // tpb, 0, 0)), _full((2, d)),
                  pl.BlockSpec((tm, d), lambda i: (i, 0))],
        out_specs=[pl.BlockSpec((tm, d), lambda i: (i, 0)), pl.BlockSpec((tm, d), lambda i: (i, 0)),
                   _full((1, d)), _full((1, d)), _full((1, d)),
                   pl.BlockSpec((1, 1, d), lambda i: (i // tpb, 0, 0))],
        out_shape=[jax.ShapeDtypeStruct((t, d), F32), jax.ShapeDtypeStruct((t, d), BF16),
                   jax.ShapeDtypeStruct((1, d), F32), jax.ShapeDtypeStruct((1, d), F32),
                   jax.ShapeDtypeStruct((1, d), F32), jax.ShapeDtypeStruct((nb, 1, d), F32)],
        compiler_params=_params(("arbitrary",)),
    )(a, wd, xhat_in, lnp_in, mod, lnp_out, tgt)


def _rope(v, cos, sa, sb):
    return v * cos + pltpu.roll(v, LANE - ROT_DIM // 2, 1) * sa + pltpu.roll(v, ROT_DIM // 2, 1) * sb


def _rope_t(dy, cos, sa, sb):
    return dy * cos + pltpu.roll(dy * sa, ROT_DIM // 2, 1) + pltpu.roll(dy * sb, LANE - ROT_DIM // 2, 1)


def _in_proj(xhat, lnp, mod, w_t, cos, sa, sb, *, seq, sc_idx, sh_idx, name, comm=None):
    t, d = xhat.shape
    tm = min(512, seq)
    tpb = seq // tm
    n_conv = 3 * CONV_WIDTH

    def body(x_ref, ln_ref, mod_ref, w_ref, cos_ref, sa_ref, sb_ref, h_ref, q_ref, k_ref, v_ref, ubc_ref):
        x = x_ref[...] * ln_ref[0:1, :] + ln_ref[1:2, :]
        h = x * (1.0 + mod_ref[0, sc_idx:sc_idx + 1, :]) + mod_ref[0, sh_idx:sh_idx + 1, :]
        hb = h.astype(BF16)
        h_ref[...] = hb
        cos_t, sa_t, sb_t = cos_ref[...], sa_ref[...], sb_ref[...]
        for j in range(ATTN_WIDTH // COL_CHUNK):
            p = _dot_nt(hb, w_ref[j * COL_CHUNK:(j + 1) * COL_CHUNK, :])
            for s in range(COL_CHUNK // LANE):
                q_ref[:, j * COL_CHUNK + s * LANE:j * COL_CHUNK + (s + 1) * LANE] = _rope(
                    p[:, s * LANE:(s + 1) * LANE], cos_t, sa_t, sb_t).astype(BF16)
        p = _dot_nt(hb, w_ref[ATTN_WIDTH:ATTN_WIDTH + 2 * KV_WIDTH, :])
        k_ref[...] = _rope(p[:, 0:KV_WIDTH], cos_t, sa_t, sb_t).astype(BF16)
        v_ref[...] = p[:, KV_WIDTH:].astype(BF16)
        base = ATTN_WIDTH + 2 * KV_WIDTH
        for j in range(n_conv // COL_CHUNK):
            ubc_ref[:, j * COL_CHUNK:(j + 1) * COL_CHUNK] = _dot_nt(
                hb, w_ref[base + j * COL_CHUNK:base + (j + 1) * COL_CHUNK, :]).astype(BF16)

    row = lambda w: pl.BlockSpec((tm, w), lambda i: (i, 0))
    return _pcall(
        body, name=name, grid=(t // tm,),
        in_specs=[row(d), _full((2, d)), pl.BlockSpec((1, N_MOD, d), lambda i: (i // tpb, 0, 0)),
                  _resident((IN_WIDTH, d)), row(LANE), row(LANE), row(LANE)],
        out_specs=[row(d), row(ATTN_WIDTH), row(KV_WIDTH), row(KV_WIDTH), row(n_conv)],
        out_shape=[jax.ShapeDtypeStruct((t, d), BF16), jax.ShapeDtypeStruct((t, ATTN_WIDTH), BF16),
                   jax.ShapeDtypeStruct((t, KV_WIDTH), BF16), jax.ShapeDtypeStruct((t, KV_WIDTH), BF16),
                   jax.ShapeDtypeStruct((t, n_conv), BF16)],
        args=(xhat, lnp, mod, w_t, cos, sa, sb), comm=comm)


ATTN_TILE_BLOCKS = 2


def _attn_sub_block(s, tile, nblk, kp_ref, kc_ref, vp_ref, vc_ref):
    rows = slice(s * BLOCK, (s + 1) * BLOCK)
    if s == 0:
        first = ((tile * ATTN_TILE_BLOCKS) % nblk) == 0
        return rows, (kp_ref, slice(0, BLOCK)), (kc_ref, rows), (vp_ref, slice(0, BLOCK)), (vc_ref, rows), first
    before = slice((s - 1) * BLOCK, s * BLOCK)
    return rows, (kc_ref, before), (kc_ref, rows), (vc_ref, before), (vc_ref, rows), False


def _attn_group(q_ref, rows, k_prev, k_cur, v_prev, v_cur, sink_ref, g, first):
    lo, hi = g * HEAD_DIM, (g + 1) * HEAD_DIM
    kk = jnp.concatenate([k_prev[0][k_prev[1], lo:hi], k_cur[0][k_cur[1], lo:hi]], axis=0)
    vv = jnp.concatenate([v_prev[0][v_prev[1], lo:hi], v_cur[0][v_cur[1], lo:hi]], axis=0)
    qs = jnp.concatenate([q_ref[rows, (GQA_GROUP * g + j) * HEAD_DIM:(GQA_GROUP * g + j + 1) * HEAD_DIM]
                          for j in range(GQA_GROUP)], axis=0)
    cols = GQA_GROUP * BLOCK
    ki = lax.broadcasted_iota(jnp.int32, (2 * BLOCK, cols), 0)
    col = lax.broadcasted_iota(jnp.int32, (2 * BLOCK, cols), 1)
    diff = (col & (BLOCK - 1)) + BLOCK - ki
    valid = (diff >= 0) & (diff < WINDOW) & ((ki >= BLOCK) | jnp.logical_not(first))
    s = _dot_nt(kk, qs) * (HEAD_DIM ** -0.5)
    s = jnp.where(valid, s, -1e30)
    hcol = lax.broadcasted_iota(jnp.int32, (1, cols), 1)
    sink = jnp.zeros((1, cols), F32)
    for j in range(GQA_GROUP):
        sink = jnp.where(hcol // BLOCK == j, sink_ref[GQA_GROUP * g + j], sink)
    m = jnp.maximum(jnp.max(s, axis=0, keepdims=True), sink)
    p = jnp.exp(s - m)
    ps = jnp.exp(sink - m)
    inv = 1.0 / (jnp.sum(p, axis=0, keepdims=True) + ps)
    return qs, kk, vv, p * inv, ps * inv


def _heads_to_lanes(x_t):
    return jnp.concatenate([x_t[:, j * BLOCK:(j + 1) * BLOCK].T for j in range(GQA_GROUP)], axis=1)


def _attention(q, k, v, sinks, *, seq, name, comm=None):
    t = q.shape[0]
    nblk = seq // BLOCK
    tile = ATTN_TILE_BLOCKS * BLOCK

    def body(q_ref, kp_ref, kc_ref, vp_ref, vc_ref, sink_ref, o_ref):
        for s in range(ATTN_TILE_BLOCKS):
            rows, k_prev, k_cur, v_prev, v_cur, first = _attn_sub_block(
                s, pl.program_id(0), nblk, kp_ref, kc_ref, vp_ref, vc_ref)
            outs = []
            for g in range(N_KV_HEADS):
                _, _, vv, pn, _ = _attn_group(q_ref, rows, k_prev, k_cur, v_prev, v_cur, sink_ref, g, first)
                outs.append(_heads_to_lanes(_dot_tn(vv, pn.astype(BF16))))
            o_ref[rows, :] = jnp.concatenate(outs, axis=1).astype(BF16)

    cur = lambda w: pl.BlockSpec((tile, w), lambda n: (n, 0))
    prev = lambda w: pl.BlockSpec((BLOCK, w), lambda n: (jnp.maximum(n * ATTN_TILE_BLOCKS - 1, 0), 0))
    return _pcall(
        body, name=name, grid=(t // tile,),
        in_specs=[cur(ATTN_WIDTH), prev(KV_WIDTH), cur(KV_WIDTH), prev(KV_WIDTH), cur(KV_WIDTH),
                  pl.BlockSpec(memory_space=pltpu.SMEM)],
        out_specs=cur(ATTN_WIDTH),
        out_shape=jax.ShapeDtypeStruct((t, ATTN_WIDTH), BF16),
        args=(q, k, k, v, v, sinks), comm=comm)


def _out_proj(attn, ubc, cw, wout, xhat_in, lnp_in, mod, *, seq, gate_idx, name, comm=None):
    t, d = xhat_in.shape
    tm = min(512, seq)
    tpb = seq // tm
    cwid = CONV_WIDTH

    def body(attn_ref, ubc_ref, halo_ref, cw_ref, w_ref, x_ref, ln_ref, mod_ref,
             mixin_ref, mix_ref, xhat_ref, rstd_ref, zbuf, acc):
        first = (pl.program_id(0) % tpb) == 0
        u, bg, cg = (ubc_ref[:, s * cwid:(s + 1) * cwid].astype(F32) for s in range(3))
        z = cg * u
        hz = halo_ref[:, 2 * cwid:3 * cwid].astype(F32) * halo_ref[:, 0:cwid].astype(F32)
        zbuf[0:HALO, :] = jnp.where(first, 0.0, hz)
        zbuf[HALO:HALO + tm, :] = z
        y = (cw_ref[0:1, :] * zbuf[HALO - 2:HALO - 2 + tm, :] + cw_ref[1:2, :] * zbuf[HALO - 1:HALO - 1 + tm, :]
             + cw_ref[2:3, :] * z)
        mixin_ref[:, 0:ATTN_WIDTH] = attn_ref[...]
        mixin_ref[:, ATTN_WIDTH:] = (bg * y).astype(BF16)
        mv = mixin_ref[...]
        for j in range(d // COL_CHUNK):
            acc[:, j * COL_CHUNK:(j + 1) * COL_CHUNK] = jnp.dot(
                mv, w_ref[:, j * COL_CHUNK:(j + 1) * COL_CHUNK], preferred_element_type=F32)
        scale = 1.0 + mod_ref[0, gate_idx:gate_idx + 1, :]

        mix = acc[...]
        xhat, rstd = _ln_stats(DN_ALPHA * (x_ref[...] * ln_ref[0:1, :] + ln_ref[1:2, :]) + scale * mix)
        mix_ref[...] = mix.astype(BF16)
        xhat_ref[...] = xhat
        rstd_ref[...] = rstd

    row = lambda w: pl.BlockSpec((tm, w), lambda i: (i, 0))
    return _pcall(
        body, name=name, grid=(t // tm,),
        in_specs=[row(ATTN_WIDTH), row(3 * cwid),
                  pl.BlockSpec((HALO, 3 * cwid), lambda i: (jnp.maximum(i * (tm // HALO) - 1, 0), 0)),
                  _full((8, cwid)), _resident((d, d)), row(d), _full((2, d)),
                  pl.BlockSpec((1, N_MOD, d), lambda i: (i // tpb, 0, 0))],
        out_specs=[row(d), row(d), row(d), row(1)],
        out_shape=[jax.ShapeDtypeStruct((t, d), BF16), jax.ShapeDtypeStruct((t, d), BF16),
                   jax.ShapeDtypeStruct((t, d), F32), jax.ShapeDtypeStruct((t, 1), F32)],
        scratch_shapes=[pltpu.VMEM((tm + HALO, cwid), F32), pltpu.VMEM((tm, d), F32)],
        args=(attn, ubc, ubc, cw, wout, xhat_in, lnp_in, mod), comm=comm)


def _ffn_bwd_act(df, wd, dact, *, seq, name, comm=None):
    t, d = df.shape
    f = wd.shape[0]
    tm = min(512, seq)
    ch = min(COL_CHUNK, f)

    def body(df_ref, wd_ref, dact_ref, dgu_ref):
        dfv = df_ref[...]
        for j in range(f // ch):
            da = _dot_nt(dfv, wd_ref[j * ch:(j + 1) * ch, :])
            dgu_ref[:, j * ch:(j + 1) * ch] = (da * dact_ref[:, j * ch:(j + 1) * ch].astype(F32)).astype(BF16)
            dgu_ref[:, f + j * ch:f + (j + 1) * ch] = (
                da * dact_ref[:, f + j * ch:f + (j + 1) * ch].astype(F32)).astype(BF16)

    return _pcall(
        body, name=name, grid=(t // tm,),
        in_specs=[pl.BlockSpec((tm, d), lambda i: (i, 0)), _resident((f, d)),
                  pl.BlockSpec((tm, 2 * f), lambda i: (i, 0))],
        out_specs=pl.BlockSpec((tm, 2 * f), lambda i: (i, 0)),
        out_shape=jax.ShapeDtypeStruct((t, 2 * f), BF16),
        args=(df, wd, dact), comm=comm)


def _bwd_in(a, w, dr, xin, rstd_prev, lnp_prev, mod, branch_prev, *, seq, w_is_nt, sc_idx, gate_idx,
            branch_scale, final, name, comm=None):
    t, kdim = a.shape
    d = dr.shape[1]
    nb = t // seq
    tm = min(512, seq)
    tpb = seq // tm

    def body(*refs):
        if final:
            a_ref, w_ref, dr_ref, x_ref, mod_ref, dx_ref, dsc_ref, dsh_ref, acc = refs
        else:
            (a_ref, w_ref, dr_ref, x_ref, rstd_ref, ln_ref, mod_ref, br_ref,
             drp_ref, dbr_ref, dsc_ref, dsh_ref, dgate_ref, dg_ref, db_ref, acc) = refs
        i = pl.program_id(0)
        av = a_ref[...]
        for j in range(d // COL_CHUNK):
            cols = slice(j * COL_CHUNK, (j + 1) * COL_CHUNK)
            acc[:, cols] = (_dot_nt(av, w_ref[cols, :]) if w_is_nt
                            else jnp.dot(av, w_ref[:, cols], preferred_element_type=F32))
        sc1 = 1.0 + mod_ref[0, sc_idx:sc_idx + 1, :]
        if not final:
            g_prev, b_prev = ln_ref[0:1, :], ln_ref[1:2, :]
            bscale = branch_scale * (1.0 + mod_ref[0, gate_idx:gate_idx + 1, :])

        def chunk(rows, carry):
            dh = acc[rows, :]
            dx = DN_ALPHA * dr_ref[rows, :] + dh * sc1
            if final:
                dx_ref[rows, :] = dx
                return carry[0] + _fold8(dh * x_ref[rows, :]), carry[1] + _fold8(dh)
            xhat = x_ref[rows, :]
            drp = _ln_bwd(dx, xhat, rstd_ref[rows, :], g_prev)
            drp_ref[rows, :] = drp
            dbr_ref[rows, :] = (bscale * drp).astype(BF16)
            return (carry[0] + _fold8(dh * xhat), carry[1] + _fold8(dh),
                    carry[2] + _fold8(br_ref[rows, :].astype(F32) * drp),
                    carry[3] + _fold8(dx * xhat), carry[4] + _fold8(dx))

        zero = jnp.zeros((8, d), F32)
        sums = list(_row_chunk_loop(tm, chunk, (zero,) * (2 if final else 5)))
        if not final:
            sums[0] = sums[0] * g_prev + sums[1] * b_prev
            sums[2] = sums[2] * branch_scale

        @pl.when((i % tpb) == 0)
        def _():
            dsc_ref[...] = jnp.zeros_like(dsc_ref)
            dsh_ref[...] = jnp.zeros_like(dsh_ref)
            if not final:
                dgate_ref[...] = jnp.zeros_like(dgate_ref)

        dsc_ref[0] += _row_sum(sums[0])
        dsh_ref[0] += _row_sum(sums[1])
        if not final:
            @pl.when(i == 0)
            def _():
                dg_ref[...] = jnp.zeros_like(dg_ref)
                db_ref[...] = jnp.zeros_like(db_ref)

            dgate_ref[0] += _row_sum(sums[2])
            dg_ref[...] += _row_sum(sums[3])
            db_ref[...] += _row_sum(sums[4])

    row = lambda w_: pl.BlockSpec((tm, w_), lambda i: (i, 0))
    vec = pl.BlockSpec((1, 1, d), lambda i: (i // tpb, 0, 0))
    mod_spec = pl.BlockSpec((1, N_MOD, d), lambda i: (i // tpb, 0, 0))
    vshape = jax.ShapeDtypeStruct((nb, 1, d), F32)
    if final:
        in_specs = [row(kdim), _resident(w.shape), row(d), row(d), mod_spec]
        args = (a, w, dr, xin, mod)
        out_specs = [row(d), vec, vec]
        out_shape = [jax.ShapeDtypeStruct((t, d), F32), vshape, vshape]
    else:
        in_specs = [row(kdim), _resident(w.shape), row(d), row(d), row(1), _full((2, d)), mod_spec, row(d)]
        args = (a, w, dr, xin, rstd_prev, lnp_prev, mod, branch_prev)
        out_specs = [row(d), row(d), vec, vec, vec, _full((1, d)), _full((1, d))]
        out_shape = [jax.ShapeDtypeStruct((t, d), F32), jax.ShapeDtypeStruct((t, d), BF16), vshape, vshape, vshape,
                     jax.ShapeDtypeStruct((1, d), F32), jax.ShapeDtypeStruct((1, d), F32)]
    return _pcall(
        body, name=name, grid=(t // tm,), in_specs=in_specs, out_specs=out_specs, out_shape=out_shape,
        scratch_shapes=[pltpu.VMEM((tm, d), F32)], args=args, comm=comm)


def _grad_chip_sum(pos, a, b, *, half_on_rows, name, comm=None):
    t, m = a.shape
    n = b.shape[1]
    tk = min(2048, t)
    nk = t // tk
    half = lambda p, pos_ref: 1 - pos_ref[2] - p + 2 * p * pos_ref[2]
    if half_on_rows:
        n_j = N_CHIPS
        tile = (m // 2, n // n_j)
        a_spec = pl.BlockSpec((tk, tile[0]), lambda p, j, k, pos_ref: (k, half(p, pos_ref)))
        b_spec = pl.BlockSpec((tk, tile[1]), lambda p, j, k, pos_ref: (k, j))
        out_tile = pl.BlockSpec((1, *tile), lambda p, j, k, pos_ref: (0, 0, j * p))
        total = (1, m // 2, n)
    else:
        n_j = 2
        tile = (m // n_j, n // 2)
        a_spec = pl.BlockSpec((tk, tile[0]), lambda p, j, k, pos_ref: (k, j))
        b_spec = pl.BlockSpec((tk, tile[1]), lambda p, j, k, pos_ref: (k, half(p, pos_ref)))
        out_tile = pl.BlockSpec((1, *tile), lambda p, j, k, pos_ref: (0, j * p, 0))
        total = (1, m, n // 2)

    def body(pos_ref, a_ref, b_ref, s32_ref, s16_ref, land_ref, acc, theirs, send_sems, recv_sems, copy_sem):
        p, j, k = pl.program_id(0), pl.program_id(1), pl.program_id(2)
        x, y, c = _position()

        def push(jj):
            return pltpu.make_async_remote_copy(
                src_ref=acc.at[jj], dst_ref=land_ref.at[jj], send_sem=send_sems.at[jj], recv_sem=recv_sems.at[jj],
                device_id=(x, y, 1 - c), device_id_type=MESH)

        fetch = pltpu.make_async_copy(land_ref.at[j], theirs, copy_sem)

        @pl.when(jnp.logical_and(p == 1, k == 0))
        def _():
            push(j).wait_send()
            push(j).wait_recv()
            fetch.start()

        part = _dot_tn(a_ref[...], b_ref[...])

        @pl.when(k == 0)
        def _():
            acc[j] = part

        @pl.when(k > 0)
        def _():
            acc[j] += part

        @pl.when(jnp.logical_and(p == 0, k == nk - 1))
        def _():
            push(j).start()

        @pl.when(jnp.logical_and(p == 1, k == nk - 1))
        def _():
            fetch.wait()
            s = acc[j] + theirs[...]
            s32_ref[0] = s
            s16_ref[0] = s.astype(BF16)

    out = _pcall(
        body, name=name, grid=(2, n_j, nk), in_specs=[a_spec, b_spec], out_specs=[out_tile, out_tile, ANY_SPEC],
        out_shape=[jax.ShapeDtypeStruct(total, F32), jax.ShapeDtypeStruct(total, BF16),
                   jax.ShapeDtypeStruct((n_j, *tile), F32)],
        scratch_shapes=[pltpu.VMEM((n_j, *tile), F32), pltpu.VMEM(tile, F32),
                        pltpu.SemaphoreType.DMA((n_j,)), pltpu.SemaphoreType.DMA((n_j,)), pltpu.SemaphoreType.DMA],
        args=(a, b), prefetch=pos, comm=comm)
    if comm is None:
        return out[0], out[1]
    (s32, s16, _), extra = out
    return (s32, s16), extra


def _matmul_nt_bf16(a, w, *, seq, name):
    t, kdim = a.shape
    n = w.shape[0]
    tm = min(512, seq)

    def body(a_ref, w_ref, o_ref):
        av = a_ref[...]
        for j in range(n // COL_CHUNK):
            o_ref[:, j * COL_CHUNK:(j + 1) * COL_CHUNK] = _dot_nt(
                av, w_ref[j * COL_CHUNK:(j + 1) * COL_CHUNK, :]).astype(BF16)

    return pl.pallas_call(
        body, name=name, grid=(t // tm,),
        in_specs=[pl.BlockSpec((tm, kdim), lambda i: (i, 0)), _resident((n, kdim))],
        out_specs=pl.BlockSpec((tm, n), lambda i: (i, 0)),
        out_shape=jax.ShapeDtypeStruct((t, n), BF16),
        compiler_params=_params(("arbitrary",)),
    )(a, w)


def _attention_bwd(q, k, v, dmixin, sinks, *, seq, name, comm=None):
    t = q.shape[0]
    nblk = seq // BLOCK
    tile = ATTN_TILE_BLOCKS * BLOCK

    def body(q_ref, kp_ref, kc_ref, vp_ref, vc_ref, do_ref, sink_ref,
             dq_ref, dkp_ref, dkc_ref, dvp_ref, dvc_ref, dsink_ref):
        n = pl.program_id(0)

        @pl.when(n == 0)
        def _():
            dsink_ref[...] = jnp.zeros_like(dsink_ref)

        srow = lax.broadcasted_iota(jnp.int32, (8, LANE), 0)
        dsink = jnp.zeros((8, LANE), F32)
        for s in range(ATTN_TILE_BLOCKS):
            rows, k_prev, k_cur, v_prev, v_cur, first = _attn_sub_block(s, n, nblk, kp_ref, kc_ref, vp_ref, vc_ref)
            dqs, dks, dvs = [], [], []
            for g in range(N_KV_HEADS):
                qs, kk, vv, pn, psn = _attn_group(q_ref, rows, k_prev, k_cur, v_prev, v_cur, sink_ref, g, first)
                dos = jnp.concatenate(
                    [do_ref[rows, (GQA_GROUP * g + j) * HEAD_DIM:(GQA_GROUP * g + j + 1) * HEAD_DIM]
                     for j in range(GQA_GROUP)], axis=0)
                dp = _dot_nt(vv, dos)
                delta = jnp.sum(pn * dp, axis=0, keepdims=True)
                ds = pn * (dp - delta)
                dsk = psn * delta
                for j in range(GQA_GROUP):
                    tot = jnp.sum(dsk[:, j * BLOCK:(j + 1) * BLOCK], axis=1, keepdims=True)
                    dsink = dsink - jnp.where(srow == GQA_GROUP * g + j, tot, 0.0)
                dsb = (ds * (HEAD_DIM ** -0.5)).astype(BF16)
                dqs.append(_heads_to_lanes(_dot_tn(kk, dsb)))
                dks.append(jnp.dot(dsb, qs, preferred_element_type=F32))
                dvs.append(jnp.dot(pn.astype(BF16), dos, preferred_element_type=F32))
            dq_ref[rows, :] = jnp.concatenate(dqs, axis=1)
            dkp_ref[rows, :] = jnp.concatenate([x[0:BLOCK, :] for x in dks], axis=1)
            dkc_ref[rows, :] = jnp.concatenate([x[BLOCK:, :] for x in dks], axis=1)
            dvp_ref[rows, :] = jnp.concatenate([x[0:BLOCK, :] for x in dvs], axis=1)
            dvc_ref[rows, :] = jnp.concatenate([x[BLOCK:, :] for x in dvs], axis=1)
        dsink_ref[...] += dsink

    cur = lambda w: pl.BlockSpec((tile, w), lambda n: (n, 0))
    prev = lambda w: pl.BlockSpec((BLOCK, w), lambda n: (jnp.maximum(n * ATTN_TILE_BLOCKS - 1, 0), 0))
    kv = jax.ShapeDtypeStruct((t, KV_WIDTH), F32)
    return _pcall(
        body, name=name, grid=(t // tile,),
        in_specs=[cur(ATTN_WIDTH), prev(KV_WIDTH), cur(KV_WIDTH), prev(KV_WIDTH), cur(KV_WIDTH), cur(ATTN_WIDTH),
                  pl.BlockSpec(memory_space=pltpu.SMEM)],
        out_specs=[cur(ATTN_WIDTH), cur(KV_WIDTH), cur(KV_WIDTH), cur(KV_WIDTH), cur(KV_WIDTH), _full((8, LANE))],
        out_shape=[jax.ShapeDtypeStruct((t, ATTN_WIDTH), F32), kv, kv, kv, kv, jax.ShapeDtypeStruct((8, LANE), F32)],
        args=(q, k, k, v, v, dmixin, sinks), comm=comm)


def _mix_bwd_assemble(dq, dkp, dkc, dvp, dvc, cos, sa, sb, dmixin, ubc, cw, *, seq, name, comm=None):
    t = dq.shape[0]
    cwid = CONV_WIDTH
    tm = min(2 * BLOCK, seq)
    tiles_per_seq = seq // tm
    ntile = t // tm
    nblk_all = t // BLOCK
    per_tile = tm // BLOCK

    def body(*refs):
        dq_ref, dkc_ref, dvc_ref = refs[0:3]
        dkp_refs, dvp_refs = refs[3:3 + per_tile], refs[3 + per_tile:3 + 2 * per_tile]
        (cos_ref, sa_ref, sb_ref, dco_ref, dcon_ref, ubc_ref, hprev_ref, hnext_ref, cw_ref,
         dproj_ref, dcw_ref, zbuf, dybuf) = refs[3 + 2 * per_tile:]
        i = pl.program_id(0)
        first = (i % tiles_per_seq) == 0
        last = (i % tiles_per_seq) == tiles_per_seq - 1
        glast = i == ntile - 1

        @pl.when(i == 0)
        def _():
            dcw_ref[...] = jnp.zeros_like(dcw_ref)

        def with_next_block(cur_ref, nxt_refs):
            nxt = [r[...] for r in nxt_refs]
            nxt[-1] = jnp.where(glast, 0.0, nxt[-1])
            return cur_ref[...] + jnp.concatenate(nxt, axis=0)

        cos_t, sa_t, sb_t = cos_ref[...], sa_ref[...], sb_ref[...]
        for j in range(ATTN_WIDTH // LANE):
            dproj_ref[:, j * LANE:(j + 1) * LANE] = _rope_t(
                dq_ref[:, j * LANE:(j + 1) * LANE], cos_t, sa_t, sb_t).astype(BF16)
        dk = with_next_block(dkc_ref, dkp_refs)
        dproj_ref[:, ATTN_WIDTH:ATTN_WIDTH + KV_WIDTH] = _rope_t(dk, cos_t, sa_t, sb_t).astype(BF16)
        dv = with_next_block(dvc_ref, dvp_refs)
        dproj_ref[:, ATTN_WIDTH + KV_WIDTH:ATTN_WIDTH + 2 * KV_WIDTH] = dv.astype(BF16)

        u, bg, cg = (ubc_ref[:, s * cwid:(s + 1) * cwid].astype(F32) for s in range(3))
        z = cg * u
        hz = hprev_ref[:, 2 * cwid:3 * cwid].astype(F32) * hprev_ref[:, 0:cwid].astype(F32)
        zbuf[0:HALO, :] = jnp.where(first, 0.0, hz)
        zbuf[HALO:HALO + tm, :] = z
        z2, z1 = zbuf[HALO - 2:HALO - 2 + tm, :], zbuf[HALO - 1:HALO - 1 + tm, :]
        w0, w1, w2 = cw_ref[0:1, :], cw_ref[1:2, :], cw_ref[2:3, :]
        y = w0 * z2 + w1 * z1 + w2 * z
        dco = dco_ref[...].astype(F32)
        dyc = dco * bg
        dyn = dcon_ref[...].astype(F32) * hnext_ref[:, cwid:2 * cwid].astype(F32)
        dybuf[0:tm, :] = dyc
        dybuf[tm:tm + HALO, :] = jnp.where(last, 0.0, dyn)
        dz = w2 * dyc + w1 * dybuf[1:1 + tm, :] + w0 * dybuf[2:2 + tm, :]
        srow = lax.broadcasted_iota(jnp.int32, (8, cwid), 0)
        dcw_ref[...] += (jnp.where(srow == 0, _row_sum(dyc * z2), 0.0) + jnp.where(srow == 1, _row_sum(dyc * z1), 0.0)
                         + jnp.where(srow == 2, _row_sum(dyc * z), 0.0))
        base = ATTN_WIDTH + 2 * KV_WIDTH
        dproj_ref[:, base:base + cwid] = (dz * cg).astype(BF16)
        dproj_ref[:, base + cwid:base + 2 * cwid] = (dco * y).astype(BF16)
        dproj_ref[:, base + 2 * cwid:base + 3 * cwid] = (dz * u).astype(BF16)

    cur = lambda w: pl.BlockSpec((tm, w), lambda i: (i, 0))
    nxt = [pl.BlockSpec((BLOCK, KV_WIDTH), lambda i, s=s: (jnp.minimum(i * per_tile + s + 1, nblk_all - 1), 0))
           for s in range(per_tile)]
    prev_halo = pl.BlockSpec((HALO, 3 * cwid), lambda i: (jnp.maximum(i * (tm // HALO) - 1, 0), 0))
    next_halo = lambda w, col: pl.BlockSpec(
        (HALO, w), lambda i: (jnp.minimum((i + 1) * (tm // HALO), t // HALO - 1), col))
    return _pcall(
        body, name=name, grid=(ntile,),
        in_specs=[cur(ATTN_WIDTH), cur(KV_WIDTH), cur(KV_WIDTH), *nxt, *nxt,
                  cur(LANE), cur(LANE), cur(LANE),
                  pl.BlockSpec((tm, cwid), lambda i: (i, 1)), next_halo(cwid, 1),
                  cur(3 * cwid), prev_halo, next_halo(3 * cwid, 0), _full((8, cwid))],
        out_specs=[cur(IN_WIDTH), _full((8, cwid))],
        out_shape=[jax.ShapeDtypeStruct((t, IN_WIDTH), BF16), jax.ShapeDtypeStruct((8, cwid), F32)],
        scratch_shapes=[pltpu.VMEM((tm + HALO, cwid), F32), pltpu.VMEM((tm + HALO, cwid), F32)],
        args=(dq, dkc, dvc, *([dkp] * per_tile), *([dvp] * per_tile), cos, sa, sb, dmixin, dmixin,
              ubc, ubc, ubc, cw), comm=comm)


def _ada_fwd(c_all, w_ada, b_ada_shard, chip, casts, *, name, comm=None):
    nb, d = c_all.shape
    n = w_ada.shape[1]
    steps = 2
    tn = n // steps
    n_cast = len(casts)

    def body(chip_ref, c_ref, w_ref, b_ref, *refs):
        cast_in, o_ref, cast_out = refs[:n_cast], refs[n_cast], refs[n_cast + 1:]
        cv = c_ref[...]
        cond = cv * _sigmoid(cv)
        o_ref[...] = jnp.dot(cond, w_ref[...], preferred_element_type=F32,
                             precision=lax.Precision.HIGHEST) + b_ref[...]
        for src, dst in zip(cast_in, cast_out):
            dst[...] = src[...].astype(BF16)

    in_specs = [_full((nb, d)), pl.BlockSpec((d, tn), lambda j, chip_ref: (0, j)),
                pl.BlockSpec((1, tn), lambda j, chip_ref: (0, j))]
    out_specs = [pl.BlockSpec((nb, tn), lambda j, chip_ref: (0, j))]
    out_shape = [jax.ShapeDtypeStruct((nb, n), F32)]
    for w, col_kind in casts:
        r, c = w.shape
        tr = r // steps
        in_specs.append(pl.BlockSpec((tr, c), lambda j, chip_ref: (j, 0)))
        if col_kind:
            out_specs.append(pl.BlockSpec((tr, c), lambda j, chip_ref: (j, chip_ref[0])))
            out_shape.append(jax.ShapeDtypeStruct((r, c * N_CHIPS), BF16))
        else:
            out_specs.append(pl.BlockSpec((tr, c), lambda j, chip_ref: (chip_ref[0] * steps + j, 0)))
            out_shape.append(jax.ShapeDtypeStruct((r * N_CHIPS, c), BF16))
    out = _pcall(body, name=name, grid=(steps,), in_specs=in_specs, out_specs=out_specs, out_shape=out_shape,
                 args=(c_all, w_ada, b_ada_shard, *[w for w, _ in casts]), prefetch=chip, comm=comm)
    res, extra = out if comm is not None else (out, None)
    return res[0], list(res[1:]), extra


def _small_finish(gathered, dmod_all, dmod_shard, c_all_t, *, name):
    d = D_MODEL
    nb, n = dmod_shard.shape

    def body(g_ref, dm_ref, dms_ref, ct_ref, sum_ref, gw_ref, gb_ref):
        total = g_ref[0]
        for dev in range(1, N_DEV):
            total = total + g_ref[dev]
        sum_ref[...] = total
        gb_ref[...] = _row_sum(dm_ref[...])
        ctv = ct_ref[...]
        cond_t = (ctv * _sigmoid(ctv)).astype(BF16)
        for jb in range(n // COL_CHUNK):
            gw_ref[:, jb * COL_CHUNK:(jb + 1) * COL_CHUNK] = jnp.dot(
                cond_t, dms_ref[:, jb * COL_CHUNK:(jb + 1) * COL_CHUNK].astype(BF16), preferred_element_type=F32)

    return pl.pallas_call(
        body, name=name, grid=(1,),
        in_specs=[_full((N_DEV, SMALL_ROWS, d)), _full((nb, N_MOD * d)), _full((nb, n)), _full((d, nb))],
        out_specs=[_full((SMALL_ROWS, d)), _full((d, n)), _full((1, N_MOD * d))],
        out_shape=[jax.ShapeDtypeStruct((SMALL_ROWS, d), F32), jax.ShapeDtypeStruct((d, n), F32),
                   jax.ShapeDtypeStruct((1, N_MOD * d), F32)],
        compiler_params=_params(("arbitrary",)),
    )(gathered, dmod_all, dmod_shard, c_all_t)


def _row_tile(r, c, budget=1 << 21):
    if r * c * 4 <= budget or r % 16:
        return r
    best = 16
    for tr in range(16, r + 1, 16):
        if r % tr == 0 and tr * c * 4 <= budget:
            best = tr
    return best


def _cast_into(w, chip, col_kind, *, name):
    r, c = w.shape
    tr = _row_tile(r, c)

    def body(chip_ref, w_ref, o_ref):
        o_ref[...] = w_ref[...].astype(BF16)

    if col_kind:
        out_spec = pl.BlockSpec((tr, c), lambda i, chip_ref: (i, chip_ref[0]))
        out_shape = jax.ShapeDtypeStruct((r, c * N_CHIPS), BF16)
    else:
        out_spec = pl.BlockSpec((tr, c), lambda i, chip_ref: (chip_ref[0] * (r // tr) + i, 0))
        out_shape = jax.ShapeDtypeStruct((r * N_CHIPS, c), BF16)
    return _pcall(body, name=name, grid=(r // tr,), in_specs=[pl.BlockSpec((tr, c), lambda i, chip_ref: (i, 0))],
                  out_specs=out_spec, out_shape=out_shape, args=(w,), prefetch=chip)


def _adamw(w, g, m, v, *, name, comm=None):
    r, c = w.shape
    tr = _row_tile(r, c)
    c1 = 1.0 - ADAM_B1 ** ADAM_STEP
    c2 = 1.0 - ADAM_B2 ** ADAM_STEP

    def body(w_ref, g_ref, m_ref, v_ref, d_ref, nm_ref, nv_ref):
        gv = g_ref[...]
        m2 = ADAM_B1 * m_ref[...] + (1.0 - ADAM_B1) * gv
        v2 = ADAM_B2 * v_ref[...] + (1.0 - ADAM_B2) * (gv * gv)
        d_ref[...] = -ADAM_LR * ((m2 / c1) / (jnp.sqrt(v2 / c2) + ADAM_EPS) + ADAM_WD * w_ref[...])
        nm_ref[...] = m2
        nv_ref[...] = v2

    spec = pl.BlockSpec((tr, c), lambda i: (i, 0))
    sh = jax.ShapeDtypeStruct((r, c), F32)
    return _pcall(body, name=name, grid=(r // tr,), in_specs=[spec] * 4, out_specs=[spec] * 3, out_shape=[sh] * 3,
                  args=(w, g, m, v), comm=comm)


def _sum_final(pos, s32, recv, *, col_kind, n_shard, name, comm=None):
    def body(pos_ref, s_ref, r_ref, o_ref):
        total = ((s_ref[0] + r_ref[0].astype(F32)) + r_ref[1].astype(F32)) + r_ref[2].astype(F32)
        if col_kind:
            o_ref[0] = total
        else:
            o_ref[...] = total

    if col_kind:
        rows, cols = s32.shape[1], n_shard
        tr = _row_tile(rows, cols)
        own = pl.BlockSpec((1, tr, cols), lambda i, pos: (0, i, 2 * pos[0] + pos[1]))
        out_spec = pl.BlockSpec((1, tr, cols), lambda i, pos: (pos[2], i, 0))
        out_shape = jax.ShapeDtypeStruct((2, rows, cols), F32)
    else:
        rows, cols = n_shard, s32.shape[2]
        tr = _row_tile(rows, cols)
        own = pl.BlockSpec((1, tr, cols), lambda i, pos: (0, (2 * pos[0] + pos[1]) * (rows // tr) + i, 0))
        out_spec = pl.BlockSpec((tr, cols), lambda i, pos: (i, pos[2]))
        out_shape = jax.ShapeDtypeStruct((rows, 2 * cols), F32)
    return _pcall(
        body, name=name, grid=(rows // tr,),
        in_specs=[own, pl.BlockSpec((3, tr, cols), lambda i, pos: (0, i, 0))], out_specs=out_spec,
        out_shape=out_shape, args=(s32, recv), prefetch=pos, comm=comm)


def _position():
    return lax.axis_index("x"), lax.axis_index("y"), lax.axis_index("c")


def _allgather8(x_shard, *, name, comm=None):
    m_per, n = x_shard.shape
    nci, nco = (0, 0) if comm is None else (len(comm.inputs), len(comm.out_shapes))

    def body(*refs):
        x_ref, refs = refs[0], refs[1:]
        cin, refs = refs[:nci], refs[nci:]
        out_ref, refs = refs[0], refs[1:]
        cout, refs = refs[:nco], refs[nco:]
        (send_sems, recv_sems, local_sem), csems = refs[:3], refs[3:]
        x, y, c = _position()
        me, sibling = (x, y, c), (x, y, 1 - c)
        chips = [(1 - x, y), (x, 1 - y), (1 - x, 1 - y)]

        def rows(px, py, pc):
            return out_ref.at[pl.ds((4 * px + 2 * py + pc) * m_per, m_per), :]

        def copy(k, block, to, src=None):
            return pltpu.make_async_remote_copy(
                src_ref=rows(*block) if src is None else src, dst_ref=rows(*block),
                send_sem=send_sems.at[k], recv_sem=recv_sems.at[k], device_id=to, device_id_type=MESH)

        mine = pltpu.make_async_copy(x_ref, rows(*me), local_sem)
        mine.start()
        first = [copy(0, me, sibling, src=x_ref)]
        first += [copy(1 + j, me, (*chip, c), src=x_ref) for j, chip in enumerate(chips)]
        for cp in first:
            cp.start()
        if comm is not None:
            comm.start(cin, cout, csems)
        passed = [copy(4 + j, (*chip, c), sibling) for j, chip in enumerate(chips)]
        for j, chip in enumerate(chips):
            copy(1 + j, (*chip, c), me).wait_recv()
            passed[j].start()
        copy(0, sibling, me).wait_recv()
        for j, chip in enumerate(chips):
            copy(4 + j, (*chip, 1 - c), me).wait_recv()
        for cp in first + passed:
            cp.wait_send()
        mine.wait()
        if comm is not None:
            comm.middle(cin, cout, csems)
            comm.late(cin, cout, csems)
            comm.finish(cin, cout, csems)

    vmem = pl.BlockSpec(memory_space=pltpu.VMEM)
    sems = [pltpu.SemaphoreType.DMA((7,)), pltpu.SemaphoreType.DMA((7,)), pltpu.SemaphoreType.DMA]
    out = jax.ShapeDtypeStruct((N_DEV * m_per, n), x_shard.dtype)
    if comm is None:
        return pl.pallas_call(body, name=name, out_shape=out, in_specs=[vmem], out_specs=vmem,
                              scratch_shapes=sems)(x_shard)
    res = pl.pallas_call(
        body, name=name, out_shape=[out] + list(comm.out_shapes), in_specs=[vmem] + [ANY_SPEC] * nci,
        out_specs=[vmem] + [ANY_SPEC] * nco, scratch_shapes=sems + list(comm.sems),
        input_output_aliases={1 + i: 1 + o for i, o in comm.aliases.items()})(x_shard, *comm.inputs)
    return res[0], list(res[1:])


def _peer_chips(x, y):
    return [(1 - x, y), (x, 1 - y), (1 - x, 1 - y)]


class _GatherJob:
    def __init__(self, pieces):
        self.pieces = pieces
        n_p = len(pieces)
        self.inputs = [p[0] for p in pieces]
        self.out_shapes = [jax.ShapeDtypeStruct(p[0].shape, p[0].dtype) for p in pieces]
        for buf, col_kind, r0, nr in pieces:
            half_rows = buf.shape[0] // (2 if col_kind else 2 * N_CHIPS)
            assert r0 % 16 == 0 and nr % 16 == 0 and nr >= 32 and r0 + nr <= half_rows, (buf.shape, r0, nr)
        self.aliases = {p: p for p in range(n_p)}
        dma = pltpu.SemaphoreType.DMA
        self.sems = [dma((2 * n_p,))] * 4 + [dma((4 * n_p,))] * 2

    def _region(self, cout, p, chip_idx, half, part=None):
        buf, col_kind, r0, nr = self.pieces[p]
        first = -(-nr // 32) * 16
        if part == 0:
            nr = first
        elif part == 1:
            r0, nr = r0 + first, nr - first
        if col_kind:
            n = buf.shape[1] // N_CHIPS
            return cout[p].at[pl.ds(half * (buf.shape[0] // 2) + r0, nr), pl.ds(chip_idx * n, n)]
        n = buf.shape[0] // N_CHIPS
        return cout[p].at[pl.ds(chip_idx * n + half * (n // 2) + r0, nr), :]

    def _copies(self, cout, sems):
        send1, recv1, send2, recv2, fsend, frecv = sems
        x, y, c = _position()
        k = 2 * x + y
        sibling = (x, y, 1 - c)
        x_nbr, y_nbr, diag = _peer_chips(x, y)
        chip_of = lambda ch: 2 * ch[0] + ch[1]

        def remote(region, ssem, rsem, to):
            return pltpu.make_async_remote_copy(src_ref=region, dst_ref=region, send_sem=ssem, recv_sem=rsem,
                                                device_id=to, device_id_type=MESH)

        hop1, arrived1, hop2, arrived2, fwds, fwd_arrived = [], [], [], [], [], []
        for p in range(len(self.pieces)):
            for j, nbr in enumerate((x_nbr, y_nbr)):
                i1 = 2 * p + j
                hop1.append(remote(self._region(cout, p, k, c), send1.at[i1], recv1.at[i1], (*nbr, c)))
                arrived1.append(remote(self._region(cout, p, chip_of(nbr), c), send1.at[i1], recv1.at[i1], (*nbr, c)))
            hop2.append(remote(self._region(cout, p, chip_of(x_nbr), c, 0), send2.at[2 * p], recv2.at[2 * p],
                               (*y_nbr, c)))
            hop2.append(remote(self._region(cout, p, chip_of(y_nbr), c, 1), send2.at[2 * p + 1], recv2.at[2 * p + 1],
                               (*x_nbr, c)))
            arrived2.append(remote(self._region(cout, p, chip_of(diag), c, 0), send2.at[2 * p], recv2.at[2 * p],
                                   (*y_nbr, c)))
            arrived2.append(remote(self._region(cout, p, chip_of(diag), c, 1), send2.at[2 * p + 1],
                                   recv2.at[2 * p + 1], (*x_nbr, c)))
            landed = [(chip_of(x_nbr), None), (chip_of(y_nbr), None), (chip_of(diag), 0), (chip_of(diag), 1)]
            for q, (chip_idx, part) in enumerate(landed):
                i3 = 4 * p + q
                fwds.append(remote(self._region(cout, p, chip_idx, c, part), fsend.at[i3], frecv.at[i3], sibling))
                fwd_arrived.append(remote(self._region(cout, p, chip_idx, 1 - c, part), fsend.at[i3], frecv.at[i3],
                                          sibling))
        return hop1, arrived1, hop2, arrived2, fwds, fwd_arrived

    def start(self, cin, cout, sems):
        for cp in self._copies(cout, sems)[0]:
            cp.start()

    def middle(self, cin, cout, sems):
        _, arrived1, hop2, _, fwds, _ = self._copies(cout, sems)
        for p in range(len(self.pieces)):
            for j in range(2):
                arrived1[2 * p + j].wait_recv()
                hop2[2 * p + j].start()
                fwds[4 * p + j].start()

    def late(self, cin, cout, sems):
        _, _, _, arrived2, fwds, _ = self._copies(cout, sems)
        for p in range(len(self.pieces)):
            for j in range(2):
                arrived2[2 * p + j].wait_recv()
                fwds[4 * p + 2 + j].start()

    def finish(self, cin, cout, sems):
        hop1, _, hop2, _, fwds, fwd_arrived = self._copies(cout, sems)
        for cp in fwd_arrived:
            cp.wait_recv()
        for cp in hop1 + hop2 + fwds:
            cp.wait_send()


class _PairedJob:
    aliases = {}

    def start(self, cin, cout, sems):
        for cp in self._copies(cin, cout, sems):
            cp.start()

    def middle(self, cin, cout, sems):
        pass

    late = middle

    def finish(self, cin, cout, sems):
        copies = self._copies(cin, cout, sems)
        for cp in copies:
            cp.wait_recv()
        for cp in copies:
            cp.wait_send()


class _ExchangeJob(_PairedJob):
    def __init__(self, s16, kinds, sizes):
        self.inputs, self.kinds, self.sizes = list(s16), list(kinds), list(sizes)
        self.out_shapes = [jax.ShapeDtypeStruct((3, s.shape[1], n) if kd else (3, n, s.shape[2]), s.dtype)
                           for s, kd, n in zip(s16, kinds, sizes)]
        self.sems = [pltpu.SemaphoreType.DMA((3 * len(s16),)), pltpu.SemaphoreType.DMA((3 * len(s16),))]

    def _copies(self, cin, cout, sems):
        send_sems, recv_sems = sems
        x, y, c = _position()
        copies = []
        for p, src_ref in enumerate(cin):
            for j, chip in enumerate(_peer_chips(x, y)):
                kk = 2 * chip[0] + chip[1]
                n = self.sizes[p]
                src = src_ref.at[0, :, pl.ds(kk * n, n)] if self.kinds[p] else src_ref.at[0, pl.ds(kk * n, n), :]
                copies.append(pltpu.make_async_remote_copy(
                    src_ref=src, dst_ref=cout[p].at[j], send_sem=send_sems.at[3 * p + j],
                    recv_sem=recv_sems.at[3 * p + j], device_id=(*chip, c), device_id_type=MESH))
        return copies


class _ShareJob:
    def __init__(self, halves):
        self.inputs = list(halves)
        self.out_shapes = [jax.ShapeDtypeStruct(h.shape, h.dtype) for h in halves]
        self.aliases = {p: p for p in range(len(halves))}
        self.sems = [pltpu.SemaphoreType.DMA((len(halves),)), pltpu.SemaphoreType.DMA((len(halves),))]

    def _copies(self, cout, sems, half):
        send_sems, recv_sems = sems
        x, y, c = _position()
        h = c if half == "mine" else 1 - c

        def region(o):
            if len(o.shape) == 3:
                return o.at[h]
            hc = o.shape[1] // 2
            return o.at[:, pl.ds(h * hc, hc)]

        return [pltpu.make_async_remote_copy(
            src_ref=region(o), dst_ref=region(o), send_sem=send_sems.at[p], recv_sem=recv_sems.at[p],
            device_id=(x, y, 1 - c), device_id_type=MESH) for p, o in enumerate(cout)]

    def start(self, cin, cout, sems):
        for cp in self._copies(cout, sems, "mine"):
            cp.start()

    def middle(self, cin, cout, sems):
        pass

    late = middle

    def finish(self, cin, cout, sems):
        for cp in self._copies(cout, sems, "theirs"):
            cp.wait_recv()
        for cp in self._copies(cout, sems, "mine"):
            cp.wait_send()


class _MultiJob:
    def __init__(self, jobs):
        self.jobs = jobs
        self.inputs = [a for j in jobs for a in j.inputs]
        self.out_shapes = [s for j in jobs for s in j.out_shapes]
        self.sems = [s for j in jobs for s in j.sems]
        self.aliases = {}
        i0 = o0 = 0
        for j in jobs:
            for i, o in j.aliases.items():
                self.aliases[i0 + i] = o0 + o
            i0 += len(j.inputs)
            o0 += len(j.out_shapes)

    def _parts(self, cin, cout, sems):
        i0 = o0 = s0 = 0
        for j in self.jobs:
            ni, no, ns = len(j.inputs), len(j.out_shapes), len(j.sems)
            yield j, cin[i0:i0 + ni], cout[o0:o0 + no], sems[s0:s0 + ns]
            i0, o0, s0 = i0 + ni, o0 + no, s0 + ns

    def start(self, cin, cout, sems):
        for j, a, b, s in self._parts(cin, cout, sems):
            j.start(a, b, s)

    def middle(self, cin, cout, sems):
        for j, a, b, s in self._parts(cin, cout, sems):
            j.middle(a, b, s)

    def late(self, cin, cout, sems):
        for j, a, b, s in self._parts(cin, cout, sems):
            j.late(a, b, s)

    def finish(self, cin, cout, sems):
        for j, a, b, s in self._parts(cin, cout, sems):
            j.finish(a, b, s)


def _rope_tables(positions):
    half = ROT_DIM // 2
    inv_freq = jnp.power(jnp.float32(ROPE_THETA), -jnp.arange(0, ROT_DIM, 2, dtype=F32) / ROT_DIM)
    inv_head = jnp.concatenate([inv_freq, inv_freq, jnp.zeros((HEAD_DIM - ROT_DIM,), F32)])
    inv_lane = jnp.concatenate([inv_head] * (LANE // HEAD_DIM))
    ang = positions.astype(F32).reshape(-1)[:, None] * inv_lane[None, :]
    sin = jnp.sin(ang)
    dim = jnp.arange(LANE) % HEAD_DIM
    return jnp.cos(ang), jnp.where(dim < half, -sin, 0.0), jnp.where(dim >= half, sin, 0.0)


def kernel(x, c, positions, w_ada, b_ada, ffn1_w_gate_up, ffn1_w_down, ln1_g, ln1_b, w_in, conv_w, attn_sinks, w_out, ln2_g, ln2_b, ffn2_w_gate_up, ffn2_w_down, ln3_g, ln3_b, loss_target, m_w_ada, m_b_ada, m_ffn1_w_gate_up, m_ffn1_w_down, m_ln1_g, m_ln1_b, m_w_in, m_conv_w, m_attn_sinks, m_w_out, m_ln2_g, m_ln2_b, m_ffn2_w_gate_up, m_ffn2_w_down, m_ln3_g, m_ln3_b, v_w_ada, v_b_ada, v_ffn1_w_gate_up, v_ffn1_w_down, v_ln1_g, v_ln1_b, v_w_in, v_conv_w, v_attn_sinks, v_w_out, v_ln2_g, v_ln2_b, v_ffn2_w_gate_up, v_ffn2_w_down, v_ln3_g, v_ln3_b):
    d = D_MODEL
    nb, seq, _ = x.shape
    t = nb * seq
    f = ffn1_w_down.shape[1] * N_CHIPS
    ax, ay, ac = _position()
    chip = 2 * ax + ay
    dev = 2 * chip + ac
    pos = jnp.stack([ax, ay, ac]).astype(jnp.int32)

    x2 = x.reshape(t, d)
    tgt2 = loss_target.reshape(t, d)
    ln1 = jnp.concatenate([ln1_g, ln1_b], axis=0)
    ln2 = jnp.concatenate([ln2_g, ln2_b], axis=0)
    ln3 = jnp.concatenate([ln3_g, ln3_b], axis=0)
    sinks = attn_sinks.reshape(N_Q_HEADS)
    cos_t, sa_t, sb_t = _rope_tables(positions)

    gu_cuts = [0, 176, 352, d // 2]
    gu_part = lambda buf, s: (buf, True, gu_cuts[s], gu_cuts[s + 1] - gu_cuts[s])
    chip_arr = jnp.reshape(chip, (1,)).astype(jnp.int32)
    b_gu1 = _cast_into(ffn1_w_gate_up[0], chip_arr, True, name="cast_gu1")

    n_ada = w_ada.shape[2]
    c_all, (b_gu1,) = _allgather8(c.reshape(nb * d // LANE, LANE), name="gather_c", comm=_GatherJob([gu_part(b_gu1, 0)]))
    c_all = c_all.reshape(N_DEV * nb, d)
    b_shard = lax.dynamic_slice(b_ada, (0, chip * n_ada), (1, n_ada))
    later_shards = [(ffn1_w_down[0], False), (w_in[0].T, False), (w_out[0], False), (ffn2_w_gate_up[0], True),
                    (ffn2_w_down[0], False)]
    mod_part, (b_d1, b_in, b_out, b_gu2, b_d2), (b_gu1,) = _ada_fwd(
        c_all, w_ada[0], b_shard, chip_arr, later_shards, name="ada_fwd", comm=_GatherJob([gu_part(b_gu1, 1)]))
    conv_rows = jnp.pad(conv_w[0], ((0, 5), (0, n_ada - conv_w.shape[2])))
    part = jnp.concatenate([mod_part, conv_rows], axis=0)
    parts, (wgu1,) = _allgather8(part, name="gather_mod", comm=_GatherJob([gu_part(b_gu1, 2)]))
    parts = parts.reshape(N_DEV, N_DEV * nb + 8, n_ada)
    mod_all = jnp.concatenate([parts[2 * k, :N_DEV * nb, :] for k in range(N_CHIPS)], axis=1)
    mod = lax.dynamic_slice(mod_all, (dev * nb, 0), (nb, N_MOD * d)).reshape(nb, N_MOD, d)
    cw_full = jnp.concatenate([parts[2 * k, N_DEV * nb:, :conv_w.shape[2]] for k in range(N_CHIPS)], axis=1)

    n_gu, n_d, n_in, n_out = (ffn1_w_gate_up.shape[2], ffn1_w_down.shape[1], w_in.shape[2], w_out.shape[1])

    def whole(buf, col_kind):
        return (buf, col_kind, 0, buf.shape[0] // (2 if col_kind else 2 * N_CHIPS))

    (h1, a1, dact1), (wd1, wout) = _ffn_up(x2, ln1, mod, wgu1, seq=seq, sc_idx=1, sh_idx=0, use_ln=False,
                                         name="ffn1_up", comm=_GatherJob([whole(b_d1, False), whole(b_out, False)]))
    (f1, xhat1, rstd1), (win_t,) = _ffn_down_ln(a1, wd1, x2, ln1, mod, seq=seq, gate_idx=2, use_ln=False,
                                                name="ffn1_down", comm=_GatherJob([whole(b_in, False)]))
    (h2, q, k, v, ubc), (b_gu2,) = _in_proj(
        xhat1, ln1, mod, win_t, cos_t, sa_t, sb_t, seq=seq, sc_idx=4, sh_idx=3, name="in_proj",
        comm=_GatherJob([gu_part(b_gu2, 0)]))
    attn, (b_gu2,) = _attention(q, k, v, sinks, seq=seq, name="attention", comm=_GatherJob([gu_part(b_gu2, 1)]))
    (mixin, mix, xhat2, rstd2), (wgu2,) = _out_proj(
        attn, ubc, cw_full, wout, xhat1, ln1, mod, seq=seq, gate_idx=5, name="out_proj",
        comm=_GatherJob([gu_part(b_gu2, 2)]))
    (h3, a3, dact3), (wd2,) = _ffn_up(xhat2, ln2, mod, wgu2, seq=seq, sc_idx=7, sh_idx=6, use_ln=True, name="ffn2_up",
                                    comm=_GatherJob([whole(b_d2, False)]))
    dr3, df3, loss_cols, dln3g, dln3b, dgate3 = _ffn_down_loss(
        a3, wd2, xhat2, ln2, mod, ln3, tgt2, seq=seq, gate_idx=8, name="ffn2_down_loss")

    dgu3 = _ffn_bwd_act(df3, wd2, dact3, seq=seq, name="ffn2_bwd_act")
    s32_d2, s16_d2 = _grad_chip_sum(pos, a3, df3, half_on_rows=False, name="grad_wd2")
    (s32_gu2, s16_gu2), (recv_d2,) = _grad_chip_sum(pos, h3, dgu3, half_on_rows=True, name="grad_wgu2",
                                                    comm=_ExchangeJob([s16_d2], [False], [n_d]))
    (dr2, dmix, dsc3, dsh3, dgate2, dln2g, dln2b), (recv_gu2,) = _bwd_in(
        dgu3, wgu2, dr3, xhat2, rstd2, ln2, mod, mix, seq=seq, w_is_nt=True, sc_idx=7, gate_idx=5,
        branch_scale=1.0, final=False, name="ffn2_bwd_in", comm=_ExchangeJob([s16_gu2], [True], [n_gu]))
    s32_out, s16_out = _grad_chip_sum(pos, mixin, dmix, half_on_rows=False, name="grad_wout")
    dmixin = _matmul_nt_bf16(dmix, wout, seq=seq, name="out_proj_bwd")
    (dq, dkp, dkc, dvp, dvc, dsink), (recv_out,) = _attention_bwd(
        q, k, v, dmixin, sinks, seq=seq, name="attention_bwd", comm=_ExchangeJob([s16_out], [False], [n_out]))
    dproj, dcw = _mix_bwd_assemble(
        dq, dkp, dkc, dvp, dvc, cos_t, sa_t, sb_t, dmixin, ubc, cw_full, seq=seq, name="mix_bwd")
    s32_in, s16_in = _grad_chip_sum(pos, dproj, h2, half_on_rows=False, name="grad_win")
    (dr1, df1, dsc2, dsh2, dgate1, dln1g, dln1b), (recv_in,) = _bwd_in(
        dproj, win_t, dr2, xhat1, rstd1, ln1, mod, f1, seq=seq, w_is_nt=False, sc_idx=4, gate_idx=2,
        branch_scale=0.5, final=False, name="in_proj_bwd", comm=_ExchangeJob([s16_in], [False], [n_in]))
    s32_d1, s16_d1 = _grad_chip_sum(pos, a1, df1, half_on_rows=False, name="grad_wd1")
    dgu1, (recv_d1,) = _ffn_bwd_act(df1, wd1, dact1, seq=seq, name="ffn1_bwd_act",
                                    comm=_ExchangeJob([s16_d1], [False], [n_d]))
    s32_gu1, s16_gu1 = _grad_chip_sum(pos, h1, dgu1, half_on_rows=True, name="grad_wgu1")

    def final_half(s32_, recv_, col_kind, n_shard, name_):
        return _sum_final(pos, s32_, recv_, col_kind=col_kind, n_shard=n_shard, name=name_)

    early = [final_half(s32_gu2, recv_gu2, True, n_gu, "sum_final_gu2"),
             final_half(s32_d2, recv_d2, False, n_d, "sum_final_d2"),
             final_half(s32_out, recv_out, False, n_out, "sum_final_out"),
             final_half(s32_in, recv_in, False, n_in, "sum_final_in"),
             final_half(s32_d1, recv_d1, False, n_d, "sum_final_d1")]
    (grad_x, dsc1, dsh1), (recv_gu1, full_gu2, full_d2, full_out, full_in, full_d1) = _bwd_in(
        dgu1, wgu1, dr1, x2, None, None, mod, None, seq=seq, w_is_nt=True, sc_idx=1, gate_idx=None,
        branch_scale=None, final=True, name="ffn1_bwd_in",
        comm=_MultiJob([_ExchangeJob([s16_gu1], [True], [n_gu]), _ShareJob(early)]))
    late = [final_half(s32_gu1, recv_gu1, True, n_gu, "sum_final_gu1")]

    dmod = jnp.concatenate([dsh1, dsc1, dgate1, dsh2, dsc2, dgate2, dsh3, dsc3, dgate3], axis=1)
    loss_row = jnp.sum(loss_cols, axis=1, keepdims=True) * (0.5 / d)
    lane_row = lambda a: jnp.pad(a, ((0, 0), (0, d - a.shape[1])))
    block = jnp.concatenate(
        [dmod.reshape(nb * N_MOD, d), dln1g, dln1b, dln2g, dln2b, dln3g, dln3b,
         lane_row(dcw[0:3, :]), lane_row(dsink[:, 0:1].reshape(1, N_Q_HEADS)), lane_row(loss_row)], axis=0)
    block = jnp.pad(block, ((0, SMALL_ROWS - block.shape[0]), (0, 0)))
    gathered, (full_gu1,) = _allgather8(block, name="gather_small", comm=_ShareJob(late))
    gathered = gathered.reshape(N_DEV, SMALL_ROWS, d)
    dmod_all = gathered[:, :nb * N_MOD, :].reshape(N_DEV * nb, N_MOD * d)
    dmod_shard = lax.dynamic_slice(dmod_all, (0, chip * n_ada), (N_DEV * nb, n_ada))
    small, g_w_ada, g_b_ada = _small_finish(gathered, dmod_all, dmod_shard, c_all.T, name="small_finish")
    r0 = nb * N_MOD
    loss = small[r0 + 10, 0]
    g_ln = [small[r0 + i:r0 + i + 1, :] for i in range(6)]
    g_cw_full = small[r0 + 6:r0 + 9, :CONV_WIDTH]
    g_conv = lax.dynamic_slice(g_cw_full, (0, chip * conv_w.shape[2]), (3, conv_w.shape[2]))
    g_sinks = small[r0 + 9:r0 + 10, :N_Q_HEADS]

    def flat2(a):
        return a.reshape(-1, a.shape[-1])

    def unhalve(a):
        return a.reshape(2 * a.shape[1], a.shape[2])

    results = {}

    def adamw(name_, w_, g_, m_, v_):
        g2 = flat2(g_)
        dl, nm, nv = _adamw(flat2(w_), g2, flat2(m_), flat2(v_), name="adamw_" + name_)
        results[name_] = tuple(a.reshape(w_.shape) for a in (g2, dl, nm, nv))

    adamw("w_ada", w_ada, g_w_ada, m_w_ada, v_w_ada)
    adamw("ffn2_w_gate_up", ffn2_w_gate_up, unhalve(full_gu2), m_ffn2_w_gate_up, v_ffn2_w_gate_up)
    adamw("ffn2_w_down", ffn2_w_down, full_d2, m_ffn2_w_down, v_ffn2_w_down)
    adamw("w_out", w_out, full_out, m_w_out, v_w_out)
    adamw("w_in", w_in, full_in.T, m_w_in, v_w_in)
    adamw("ffn1_w_gate_up", ffn1_w_gate_up, unhalve(full_gu1), m_ffn1_w_gate_up, v_ffn1_w_gate_up)
    adamw("ffn1_w_down", ffn1_w_down, full_d1, m_ffn1_w_down, v_ffn1_w_down)
    adamw("b_ada", b_ada, g_b_ada, m_b_ada, v_b_ada)
    adamw("ln1_g", ln1_g, g_ln[0], m_ln1_g, v_ln1_g)
    adamw("ln1_b", ln1_b, g_ln[1], m_ln1_b, v_ln1_b)
    adamw("ln2_g", ln2_g, g_ln[2], m_ln2_g, v_ln2_g)
    adamw("ln2_b", ln2_b, g_ln[3], m_ln2_b, v_ln2_b)
    adamw("ln3_g", ln3_g, g_ln[4], m_ln3_g, v_ln3_g)
    adamw("ln3_b", ln3_b, g_ln[5], m_ln3_b, v_ln3_b)
    adamw("conv_w", conv_w, g_conv, m_conv_w, v_conv_w)
    adamw("attn_sinks", attn_sinks, g_sinks, m_attn_sinks, v_attn_sinks)
    order = ["w_ada", "b_ada", "ffn1_w_gate_up", "ffn1_w_down", "ln1_g", "ln1_b", "w_in", "conv_w", "attn_sinks",
             "w_out", "ln2_g", "ln2_b", "ffn2_w_gate_up", "ffn2_w_down", "ln3_g", "ln3_b"]
    return (loss, grad_x.reshape(x.shape), *[results[n_][0] for n_ in order], *[results[n_][1] for n_ in order],
            *[results[n_][2] for n_ in order], *[results[n_][3] for n_ in order])
```

```python
import jax
import jax.numpy as jnp
from jax import lax
from jax.experimental import pallas as pl
from jax.experimental.pallas import tpu as pltpu

F32 = jnp.float32
BF16 = jnp.bfloat16
MESH = pl.DeviceIdType.MESH

D_MODEL = 1024
HEAD_DIM = 64
ATTN_WIDTH = 512
CONV_WIDTH = 512
N_Q_HEADS = 8
N_KV_HEADS = 2
GQA_GROUP = 4
KV_WIDTH = 128
WINDOW = 128
BLOCK = 128
ROT_DIM = 16
ROPE_THETA = 500000.0
N_MOD = 9
LN_EPS = 1e-5
DN_ALPHA = 2.0 ** 0.25
IN_WIDTH = 2304
N_CHIPS = 4
N_DEV = 8
SMALL_ROWS = 32

ADAM_LR = 0.001
ADAM_B1 = 0.9
ADAM_B2 = 0.999
ADAM_EPS = 1e-08
ADAM_WD = 0.01
ADAM_STEP = 10

LANE = 128
HALO = 16
COL_CHUNK = 256
VMEM_LIMIT = 56 * 1024 * 1024


def _params(sem=None, vmem=True):
    return pltpu.CompilerParams(dimension_semantics=sem, vmem_limit_bytes=VMEM_LIMIT if vmem else None)


def _sigmoid(g):
    return 0.5 * jnp.tanh(0.5 * g) + 0.5


def _row_sum(v):
    return jnp.sum(v, axis=0, keepdims=True)


ROW_CHUNK = 16
EPILOGUE_UNROLL = 8


def _fold8(v):
    return v[0:8, :] + v[8:16, :]


def _row_chunk_loop(n_rows, step, init):
    per_iter = ROW_CHUNK * EPILOGUE_UNROLL
    assert n_rows % per_iter == 0, n_rows

    def body(it, carry):
        for s in range(EPILOGUE_UNROLL):
            start = pl.multiple_of(it * per_iter + s * ROW_CHUNK, ROW_CHUNK)
            carry = step(pl.ds(start, ROW_CHUNK), carry)
        return carry

    return lax.fori_loop(0, n_rows // per_iter, body, init)


def _ln_stats(r):
    mu = jnp.mean(r, axis=-1, keepdims=True)
    rc = r - mu
    var = jnp.mean(rc * rc, axis=-1, keepdims=True)
    rstd = lax.rsqrt(var + LN_EPS)
    return rc * rstd, rstd


def _ln_bwd(dxo, xhat, rstd, g):
    dxhat = dxo * g
    m1 = jnp.mean(dxhat, axis=-1, keepdims=True)
    m2 = jnp.mean(dxhat * xhat, axis=-1, keepdims=True)
    return rstd * (dxhat - m1 - xhat * m2)


def _dot_nt(a, b):
    return lax.dot_general(a, b, (((1,), (1,)), ((), ())), preferred_element_type=F32)


def _dot_tn(a, b):
    return lax.dot_general(a, b, (((0,), (0,)), ((), ())), preferred_element_type=F32)


def _full(shape):
    nd = len(shape)
    return pl.BlockSpec(shape, lambda *_: (0,) * nd)


def _resident(shape):
    nd = len(shape)
    return pl.BlockSpec(shape, lambda *_: (0,) * nd, pipeline_mode=pl.Buffered(1))


ANY_SPEC = pl.BlockSpec(memory_space=pl.ANY)


def _pcall(body, *, name, grid, in_specs, out_specs, out_shape, args, scratch_shapes=(), comm=None, prefetch=None):
    single = not isinstance(out_shape, (list, tuple))
    out_specs = [out_specs] if single else list(out_specs)
    out_shape = [out_shape] if single else list(out_shape)
    in_specs = list(in_specs)
    scratch_shapes = list(scratch_shapes)
    sem = ("arbitrary",) * len(grid)
    n_pre = 0 if prefetch is None else 1
    pre_args = () if prefetch is None else (prefetch,)

    def call(fn, ins_, outs_, shapes_, scratch_, aliases_, operands):
        if prefetch is None:
            return pl.pallas_call(fn, name=name, grid=grid, in_specs=ins_, out_specs=outs_, out_shape=shapes_,
                                  scratch_shapes=scratch_, input_output_aliases=aliases_,
                                  compiler_params=_params(sem))(*operands)
        spec = pltpu.PrefetchScalarGridSpec(num_scalar_prefetch=1, grid=grid, in_specs=ins_, out_specs=outs_,
                                            scratch_shapes=scratch_)
        return pl.pallas_call(fn, name=name, grid_spec=spec, out_shape=shapes_,
                              input_output_aliases={n_pre + i: o for i, o in aliases_.items()},
                              compiler_params=_params(sem))(*pre_args, *operands)

    if comm is None:
        res = call(body, in_specs, out_specs, out_shape, scratch_shapes, {}, args)
        return res[0] if single else res
    n_in, n_out, n_scr = len(in_specs), len(out_specs), len(scratch_shapes)
    nci, nco = len(comm.inputs), len(comm.out_shapes)
    n_steps = 1
    for g in grid:
        n_steps *= g
    staged = n_steps >= 8
    middle_step = (n_steps * 5) // 8 - 1
    late_step = n_steps - 1 - max(1, n_steps // 8)

    def wrapped(*refs):
        pre, refs = refs[:n_pre], refs[n_pre:]
        ins, refs = refs[:n_in], refs[n_in:]
        cin, refs = refs[:nci], refs[nci:]
        outs, refs = refs[:n_out], refs[n_out:]
        cout, refs = refs[:nco], refs[nco:]
        scr, csems = refs[:n_scr], refs[n_scr:]
        step = pl.program_id(0)
        for ax in range(1, len(grid)):
            step = step * grid[ax] + pl.program_id(ax)

        @pl.when(step == 0)
        def _():
            comm.start(cin, cout, csems)

        body(*pre, *ins, *outs, *scr)

        if staged:
            @pl.when(step == middle_step)
            def _():
                comm.middle(cin, cout, csems)

            @pl.when(step == late_step)
            def _():
                comm.late(cin, cout, csems)

        @pl.when(step == n_steps - 1)
        def _():
            if not staged:
                comm.middle(cin, cout, csems)
                comm.late(cin, cout, csems)
            comm.finish(cin, cout, csems)

    res = call(wrapped, in_specs + [ANY_SPEC] * nci, out_specs + [ANY_SPEC] * nco,
               out_shape + list(comm.out_shapes), scratch_shapes + list(comm.sems),
               {n_in + i: n_out + o for i, o in comm.aliases.items()}, (*args, *comm.inputs))
    main = res[:n_out]
    return (main[0] if single else main), list(res[n_out:])


def _ffn_up(xin, lnp, mod, w, *, seq, sc_idx, sh_idx, use_ln, name, comm=None):
    t, d = xin.shape
    f = w.shape[1] // 2
    tm = min(512, seq)
    tpb = seq // tm
    ch = min(COL_CHUNK, f)

    def body(x_ref, ln_ref, mod_ref, w_ref, h_ref, a_ref, dact_ref):
        x = x_ref[...]
        if use_ln:
            x = x * ln_ref[0:1, :] + ln_ref[1:2, :]
        h = x * (1.0 + mod_ref[0, sc_idx:sc_idx + 1, :]) + mod_ref[0, sh_idx:sh_idx + 1, :]
        hb = h.astype(BF16)
        h_ref[...] = hb
        for j in range(f // ch):
            g = jnp.dot(hb, w_ref[:, j * ch:(j + 1) * ch], preferred_element_type=F32)
            u = jnp.dot(hb, w_ref[:, f + j * ch:f + (j + 1) * ch], preferred_element_type=F32)
            s = _sigmoid(g)
            silu = g * s
            a_ref[:, j * ch:(j + 1) * ch] = (silu * u).astype(BF16)
            dact_ref[:, j * ch:(j + 1) * ch] = (u * (s + silu * (1.0 - s))).astype(BF16)
            dact_ref[:, f + j * ch:f + (j + 1) * ch] = silu.astype(BF16)

    return _pcall(
        body, name=name, grid=(t // tm,),
        in_specs=[pl.BlockSpec((tm, d), lambda i: (i, 0)), _full((2, d)),
                  pl.BlockSpec((1, N_MOD, d), lambda i: (i // tpb, 0, 0)), _resident((d, 2 * f))],
        out_specs=[pl.BlockSpec((tm, d), lambda i: (i, 0)), pl.BlockSpec((tm, f), lambda i: (i, 0)),
                   pl.BlockSpec((tm, 2 * f), lambda i: (i, 0))],
        out_shape=[jax.ShapeDtypeStruct((t, d), BF16), jax.ShapeDtypeStruct((t, f), BF16),
                   jax.ShapeDtypeStruct((t, 2 * f), BF16)],
        args=(xin, lnp, mod, w), comm=comm)


def _ffn_down_ln(a, wd, xin, lnp_in, mod, *, seq, gate_idx, use_ln, name, comm=None):
    t, f = a.shape
    d = wd.shape[1]
    tm = min(512, seq)
    tpb = seq // tm

    def body(a_ref, wd_ref, x_ref, ln_ref, mod_ref, f_ref, xhat_ref, rstd_ref, acc):
        av = a_ref[...]
        for j in range(d // COL_CHUNK):
            acc[:, j * COL_CHUNK:(j + 1) * COL_CHUNK] = jnp.dot(
                av, wd_ref[:, j * COL_CHUNK:(j + 1) * COL_CHUNK], preferred_element_type=F32)
        scale = 0.5 * (1.0 + mod_ref[0, gate_idx:gate_idx + 1, :])

        fo = acc[...]
        x = x_ref[...]
        if use_ln:
            x = x * ln_ref[0:1, :] + ln_ref[1:2, :]
        xhat, rstd = _ln_stats(DN_ALPHA * x + scale * fo)
        f_ref[...] = fo.astype(BF16)
        xhat_ref[...] = xhat
        rstd_ref[...] = rstd

    return _pcall(
        body, name=name, grid=(t // tm,),
        in_specs=[pl.BlockSpec((tm, f), lambda i: (i, 0)), _resident((f, d)),
                  pl.BlockSpec((tm, d), lambda i: (i, 0)), _full((2, d)),
                  pl.BlockSpec((1, N_MOD, d), lambda i: (i // tpb, 0, 0))],
        out_specs=[pl.BlockSpec((tm, d), lambda i: (i, 0)), pl.BlockSpec((tm, d), lambda i: (i, 0)),
                   pl.BlockSpec((tm, 1), lambda i: (i, 0))],
        out_shape=[jax.ShapeDtypeStruct((t, d), BF16), jax.ShapeDtypeStruct((t, d), F32),
                   jax.ShapeDtypeStruct((t, 1), F32)],
        scratch_shapes=[pltpu.VMEM((tm, d), F32)],
        args=(a, wd, xin, lnp_in, mod), comm=comm)


def _ffn_down_loss(a, wd, xhat_in, lnp_in, mod, lnp_out, tgt, *, seq, gate_idx, name):
    t, f = a.shape
    d = wd.shape[1]
    nb = t // seq
    tm = min(512, seq)
    tpb = seq // tm

    def body(a_ref, wd_ref, x_ref, lnin_ref, mod_ref, lnout_ref, tgt_ref,
             dr_ref, df_ref, loss_ref, dg_ref, db_ref, dgate_ref, acc):
        i = pl.program_id(0)
        av = a_ref[...]
        for j in range(d // COL_CHUNK):
            acc[:, j * COL_CHUNK:(j + 1) * COL_CHUNK] = jnp.dot(
                av, wd_ref[:, j * COL_CHUNK:(j + 1) * COL_CHUNK], preferred_element_type=F32)
        scale = 0.5 * (1.0 + mod_ref[0, gate_idx:gate_idx + 1, :])
        ag_in, ab_in = DN_ALPHA * lnin_ref[0:1, :], DN_ALPHA * lnin_ref[1:2, :]
        g_out, b_out = lnout_ref[0:1, :], lnout_ref[1:2, :]
        g_over_d = g_out * (1.0 / d)

        def chunk(rows, carry):
            s_loss, s_dg, s_db, s_gate = carry
            fo = acc[rows, :]
            xhat, rstd = _ln_stats(x_ref[rows, :] * ag_in + ab_in + scale * fo)
            e = xhat * g_out + b_out - tgt_ref[rows, :]
            dr = _ln_bwd(e, xhat, rstd, g_over_d)
            dr_ref[rows, :] = dr
            df_ref[rows, :] = (scale * dr).astype(BF16)
            return s_loss + _fold8(e * e), s_dg + _fold8(e * xhat), s_db + _fold8(e), s_gate + _fold8(fo * dr)

        zero = jnp.zeros((8, d), F32)
        s_loss, s_dg, s_db, s_gate = _row_chunk_loop(tm, chunk, (zero, zero, zero, zero))
        s_dg, s_db, s_gate = s_dg * (1.0 / d), s_db * (1.0 / d), s_gate * 0.5

        @pl.when(i == 0)
        def _():
            loss_ref[...] = jnp.zeros_like(loss_ref)
            dg_ref[...] = jnp.zeros_like(dg_ref)
            db_ref[...] = jnp.zeros_like(db_ref)

        @pl.when(i % tpb == 0)
        def _():
            dgate_ref[...] = jnp.zeros_like(dgate_ref)

        loss_ref[...] += _row_sum(s_loss)
        dg_ref[...] += _row_sum(s_dg)
        db_ref[...] += _row_sum(s_db)
        dgate_ref[0] += _row_sum(s_gate)

    return pl.pallas_call(
        body, name=name, grid=(t // tm,), scratch_shapes=[pltpu.VMEM((tm, d), F32)],
        in_specs=[pl.BlockSpec((tm, f), lambda i: (i, 0)), _resident((f, d)),
                  pl.BlockSpec((tm, d), lambda i: (i, 0)), _full((2, d)),
                  pl.BlockSpec((1, N_MOD, d), lambda i: (i // tpb, 0, 0)), _full((2, d)),
                  pl.BlockSpec((tm, d), lambda i: (i, 0))],
        out_specs=[pl.BlockSpec((tm, d), lambda i: (i, 0)), pl.BlockSpec((tm, d), lambda i: (i, 0)),
                   _full((1, d)), _full((1, d)), _full((1, d)),
                   pl.BlockSpec((1, 1, d), lambda i: (i // tpb, 0, 0))],
        out_shape=[jax.ShapeDtypeStruct((t, d), F32), jax.ShapeDtypeStruct((t, d), BF16),
                   jax.ShapeDtypeStruct((1, d), F32), jax.ShapeDtypeStruct((1, d), F32),
                   jax.ShapeDtypeStruct((1, d), F32), jax.ShapeDtypeStruct((nb, 1, d), F32)],
        compiler_params=_params(("arbitrary",)),
    )(a, wd, xhat_in, lnp_in, mod, lnp_out, tgt)


def _rope(v, cos, sa, sb):
    return v * cos + pltpu.roll(v, LANE - ROT_DIM // 2, 1) * sa + pltpu.roll(v, ROT_DIM // 2, 1) * sb


def _rope_t(dy, cos, sa, sb):
    return dy * cos + pltpu.roll(dy * sa, ROT_DIM // 2, 1) + pltpu.roll(dy * sb, LANE - ROT_DIM // 2, 1)


def _in_proj(xhat, lnp, mod, w_t, cos, sa, sb, *, seq, sc_idx, sh_idx, name, comm=None):
    t, d = xhat.shape
    tm = min(512, seq)
    tpb = seq // tm
    n_conv = 3 * CONV_WIDTH

    def body(x_ref, ln_ref, mod_ref, w_ref, cos_ref, sa_ref, sb_ref, h_ref, q_ref, k_ref, v_ref, ubc_ref):
        x = x_ref[...] * ln_ref[0:1, :] + ln_ref[1:2, :]
        h = x * (1.0 + mod_ref[0, sc_idx:sc_idx + 1, :]) + mod_ref[0, sh_idx:sh_idx + 1, :]
        hb = h.astype(BF16)
        h_ref[...] = hb
        cos_t, sa_t, sb_t = cos_ref[...], sa_ref[...], sb_ref[...]
        for j in range(ATTN_WIDTH // COL_CHUNK):
            p = _dot_nt(hb, w_ref[j * COL_CHUNK:(j + 1) * COL_CHUNK, :])
            for s in range(COL_CHUNK // LANE):
                q_ref[:, j * COL_CHUNK + s * LANE:j * COL_CHUNK + (s + 1) * LANE] = _rope(
                    p[:, s * LANE:(s + 1) * LANE], cos_t, sa_t, sb_t).astype(BF16)
        p = _dot_nt(hb, w_ref[ATTN_WIDTH:ATTN_WIDTH + 2 * KV_WIDTH, :])
        k_ref[...] = _rope(p[:, 0:KV_WIDTH], cos_t, sa_t, sb_t).astype(BF16)
        v_ref[...] = p[:, KV_WIDTH:].astype(BF16)
        base = ATTN_WIDTH + 2 * KV_WIDTH
        for j in range(n_conv // COL_CHUNK):
            ubc_ref[:, j * COL_CHUNK:(j + 1) * COL_CHUNK] = _dot_nt(
                hb, w_ref[base + j * COL_CHUNK:base + (j + 1) * COL_CHUNK, :]).astype(BF16)

    row = lambda w: pl.BlockSpec((tm, w), lambda i: (i, 0))
    return _pcall(
        body, name=name, grid=(t // tm,),
        in_specs=[row(d), _full((2, d)), pl.BlockSpec((1, N_MOD, d), lambda i: (i // tpb, 0, 0)),
                  _resident((IN_WIDTH, d)), row(LANE), row(LANE), row(LANE)],
        out_specs=[row(d), row(ATTN_WIDTH), row(KV_WIDTH), row(KV_WIDTH), row(n_conv)],
        out_shape=[jax.ShapeDtypeStruct((t, d), BF16), jax.ShapeDtypeStruct((t, ATTN_WIDTH), BF16),
                   jax.ShapeDtypeStruct((t, KV_WIDTH), BF16), jax.ShapeDtypeStruct((t, KV_WIDTH), BF16),
                   jax.ShapeDtypeStruct((t, n_conv), BF16)],
        args=(xhat, lnp, mod, w_t, cos, sa, sb), comm=comm)


ATTN_TILE_BLOCKS = 2


def _attn_sub_block(s, tile, nblk, kp_ref, kc_ref, vp_ref, vc_ref):
    rows = slice(s * BLOCK, (s + 1) * BLOCK)
    if s == 0:
        first = ((tile * ATTN_TILE_BLOCKS) % nblk) == 0
        return rows, (kp_ref, slice(0, BLOCK)), (kc_ref, rows), (vp_ref, slice(0, BLOCK)), (vc_ref, rows), first
    before = slice((s - 1) * BLOCK, s * BLOCK)
    return rows, (kc_ref, before), (kc_ref, rows), (vc_ref, before), (vc_ref, rows), False


def _attn_group(q_ref, rows, k_prev, k_cur, v_prev, v_cur, sink_ref, g, first):
    lo, hi = g * HEAD_DIM, (g + 1) * HEAD_DIM
    kk = jnp.concatenate([k_prev[0][k_prev[1], lo:hi], k_cur[0][k_cur[1], lo:hi]], axis=0)
    vv = jnp.concatenate([v_prev[0][v_prev[1], lo:hi], v_cur[0][v_cur[1], lo:hi]], axis=0)
    qs = jnp.concatenate([q_ref[rows, (GQA_GROUP * g + j) * HEAD_DIM:(GQA_GROUP * g + j + 1) * HEAD_DIM]
                          for j in range(GQA_GROUP)], axis=0)
    cols = GQA_GROUP * BLOCK
    ki = lax.broadcasted_iota(jnp.int32, (2 * BLOCK, cols), 0)
    col = lax.broadcasted_iota(jnp.int32, (2 * BLOCK, cols), 1)
    diff = (col & (BLOCK - 1)) + BLOCK - ki
    valid = (diff >= 0) & (diff < WINDOW) & ((ki >= BLOCK) | jnp.logical_not(first))
    s = _dot_nt(kk, qs) * (HEAD_DIM ** -0.5)
    s = jnp.where(valid, s, -1e30)
    hcol = lax.broadcasted_iota(jnp.int32, (1, cols), 1)
    sink = jnp.zeros((1, cols), F32)
    for j in range(GQA_GROUP):
        sink = jnp.where(hcol // BLOCK == j, sink_ref[GQA_GROUP * g + j], sink)
    m = jnp.maximum(jnp.max(s, axis=0, keepdims=True), sink)
    p = jnp.exp(s - m)
    ps = jnp.exp(sink - m)
    inv = 1.0 / (jnp.sum(p, axis=0, keepdims=True) + ps)
    return qs, kk, vv, p * inv, ps * inv


def _heads_to_lanes(x_t):
    return jnp.concatenate([x_t[:, j * BLOCK:(j + 1) * BLOCK].T for j in range(GQA_GROUP)], axis=1)


def _attention(q, k, v, sinks, *, seq, name, comm=None):
    t = q.shape[0]
    nblk = seq // BLOCK
    tile = ATTN_TILE_BLOCKS * BLOCK

    def body(q_ref, kp_ref, kc_ref, vp_ref, vc_ref, sink_ref, o_ref):
        for s in range(ATTN_TILE_BLOCKS):
            rows, k_prev, k_cur, v_prev, v_cur, first = _attn_sub_block(
                s, pl.program_id(0), nblk, kp_ref, kc_ref, vp_ref, vc_ref)
            outs = []
            for g in range(N_KV_HEADS):
                _, _, vv, pn, _ = _attn_group(q_ref, rows, k_prev, k_cur, v_prev, v_cur, sink_ref, g, first)
                outs.append(_heads_to_lanes(_dot_tn(vv, pn.astype(BF16))))
            o_ref[rows, :] = jnp.concatenate(outs, axis=1).astype(BF16)

    cur = lambda w: pl.BlockSpec((tile, w), lambda n: (n, 0))
    prev = lambda w: pl.BlockSpec((BLOCK, w), lambda n: (jnp.maximum(n * ATTN_TILE_BLOCKS - 1, 0), 0))
    return _pcall(
        body, name=name, grid=(t // tile,),
        in_specs=[cur(ATTN_WIDTH), prev(KV_WIDTH), cur(KV_WIDTH), prev(KV_WIDTH), cur(KV_WIDTH),
                  pl.BlockSpec(memory_space=pltpu.SMEM)],
        out_specs=cur(ATTN_WIDTH),
        out_shape=jax.ShapeDtypeStruct((t, ATTN_WIDTH), BF16),
        args=(q, k, k, v, v, sinks), comm=comm)


def _out_proj(attn, ubc, cw, wout, xhat_in, lnp_in, mod, *, seq, gate_idx, name, comm=None):
    t, d = xhat_in.shape
    tm = min(512, seq)
    tpb = seq // tm
    cwid = CONV_WIDTH

    def body(attn_ref, ubc_ref, halo_ref, cw_ref, w_ref, x_ref, ln_ref, mod_ref,
             mixin_ref, mix_ref, xhat_ref, rstd_ref, zbuf, acc):
        first = (pl.program_id(0) % tpb) == 0
        u, bg, cg = (ubc_ref[:, s * cwid:(s + 1) * cwid].astype(F32) for s in range(3))
        z = cg * u
        hz = halo_ref[:, 2 * cwid:3 * cwid].astype(F32) * halo_ref[:, 0:cwid].astype(F32)
        zbuf[0:HALO, :] = jnp.where(first, 0.0, hz)
        zbuf[HALO:HALO + tm, :] = z
        y = (cw_ref[0:1, :] * zbuf[HALO - 2:HALO - 2 + tm, :] + cw_ref[1:2, :] * zbuf[HALO - 1:HALO - 1 + tm, :]
             + cw_ref[2:3, :] * z)
        mixin_ref[:, 0:ATTN_WIDTH] = attn_ref[...]
        mixin_ref[:, ATTN_WIDTH:] = (bg * y).astype(BF16)
        mv = mixin_ref[...]
        for j in range(d // COL_CHUNK):
            acc[:, j * COL_CHUNK:(j + 1) * COL_CHUNK] = jnp.dot(
                mv, w_ref[:, j * COL_CHUNK:(j + 1) * COL_CHUNK], preferred_element_type=F32)
        scale = 1.0 + mod_ref[0, gate_idx:gate_idx + 1, :]

        mix = acc[...]
        xhat, rstd = _ln_stats(DN_ALPHA * (x_ref[...] * ln_ref[0:1, :] + ln_ref[1:2, :]) + scale * mix)
        mix_ref[...] = mix.astype(BF16)
        xhat_ref[...] = xhat
        rstd_ref[...] = rstd

    row = lambda w: pl.BlockSpec((tm, w), lambda i: (i, 0))
    return _pcall(
        body, name=name, grid=(t // tm,),
        in_specs=[row(ATTN_WIDTH), row(3 * cwid),
                  pl.BlockSpec((HALO, 3 * cwid), lambda i: (jnp.maximum(i * (tm // HALO) - 1, 0), 0)),
                  _full((8, cwid)), _resident((d, d)), row(d), _full((2, d)),
                  pl.BlockSpec((1, N_MOD, d), lambda i: (i // tpb, 0, 0))],
        out_specs=[row(d), row(d), row(d), row(1)],
        out_shape=[jax.ShapeDtypeStruct((t, d), BF16), jax.ShapeDtypeStruct((t, d), BF16),
                   jax.ShapeDtypeStruct((t, d), F32), jax.ShapeDtypeStruct((t, 1), F32)],
        scratch_shapes=[pltpu.VMEM((tm + HALO, cwid), F32), pltpu.VMEM((tm, d), F32)],
        args=(attn, ubc, ubc, cw, wout, xhat_in, lnp_in, mod), comm=comm)


def _ffn_bwd_act(df, wd, dact, *, seq, name, comm=None):
    t, d = df.shape
    f = wd.shape[0]
    tm = min(512, seq)
    ch = min(COL_CHUNK, f)

    def body(df_ref, wd_ref, dact_ref, dgu_ref):
        dfv = df_ref[...]
        for j in range(f // ch):
            da = _dot_nt(dfv, wd_ref[j * ch:(j + 1) * ch, :])
            dgu_ref[:, j * ch:(j + 1) * ch] = (da * dact_ref[:, j * ch:(j + 1) * ch].astype(F32)).astype(BF16)
            dgu_ref[:, f + j * ch:f + (j + 1) * ch] = (
                da * dact_ref[:, f + j * ch:f + (j + 1) * ch].astype(F32)).astype(BF16)

    return _pcall(
        body, name=name, grid=(t // tm,),
        in_specs=[pl.BlockSpec((tm, d), lambda i: (i, 0)), _resident((f, d)),
                  pl.BlockSpec((tm, 2 * f), lambda i: (i, 0))],
        out_specs=pl.BlockSpec((tm, 2 * f), lambda i: (i, 0)),
        out_shape=jax.ShapeDtypeStruct((t, 2 * f), BF16),
        args=(df, wd, dact), comm=comm)


def _bwd_in(a, w, dr, xin, rstd_prev, lnp_prev, mod, branch_prev, *, seq, w_is_nt, sc_idx, gate_idx,
            branch_scale, final, name, comm=None):
    t, kdim = a.shape
    d = dr.shape[1]
    nb = t // seq
    tm = min(512, seq)
    tpb = seq // tm

    def body(*refs):
        if final:
            a_ref, w_ref, dr_ref, x_ref, mod_ref, dx_ref, dsc_ref, dsh_ref, acc = refs
        else:
            (a_ref, w_ref, dr_ref, x_ref, rstd_ref, ln_ref, mod_ref, br_ref,
             drp_ref, dbr_ref, dsc_ref, dsh_ref, dgate_ref, dg_ref, db_ref, acc) = refs
        i = pl.program_id(0)
        av = a_ref[...]
        for j in range(d // COL_CHUNK):
            cols = slice(j * COL_CHUNK, (j + 1) * COL_CHUNK)
            acc[:, cols] = (_dot_nt(av, w_ref[cols, :]) if w_is_nt
                            else jnp.dot(av, w_ref[:, cols], preferred_element_type=F32))
        sc1 = 1.0 + mod_ref[0, sc_idx:sc_idx + 1, :]
        if not final:
            g_prev, b_prev = ln_ref[0:1, :], ln_ref[1:2, :]
            bscale = branch_scale * (1.0 + mod_ref[0, gate_idx:gate_idx + 1, :])

        def chunk(rows, carry):
            dh = acc[rows, :]
            dx = DN_ALPHA * dr_ref[rows, :] + dh * sc1
            if final:
                dx_ref[rows, :] = dx
                return carry[0] + _fold8(dh * x_ref[rows, :]), carry[1] + _fold8(dh)
            xhat = x_ref[rows, :]
            drp = _ln_bwd(dx, xhat, rstd_ref[rows, :], g_prev)
            drp_ref[rows, :] = drp
            dbr_ref[rows, :] = (bscale * drp).astype(BF16)
            return (carry[0] + _fold8(dh * xhat), carry[1] + _fold8(dh),
                    carry[2] + _fold8(br_ref[rows, :].astype(F32) * drp),
                    carry[3] + _fold8(dx * xhat), carry[4] + _fold8(dx))

        zero = jnp.zeros((8, d), F32)
        sums = list(_row_chunk_loop(tm, chunk, (zero,) * (2 if final else 5)))
        if not final:
            sums[0] = sums[0] * g_prev + sums[1] * b_prev
            sums[2] = sums[2] * branch_scale

        @pl.when((i % tpb) == 0)
        def _():
            dsc_ref[...] = jnp.zeros_like(dsc_ref)
            dsh_ref[...] = jnp.zeros_like(dsh_ref)
            if not final:
                dgate_ref[...] = jnp.zeros_like(dgate_ref)

        dsc_ref[0] += _row_sum(sums[0])
        dsh_ref[0] += _row_sum(sums[1])
        if not final:
            @pl.when(i == 0)
            def _():
                dg_ref[...] = jnp.zeros_like(dg_ref)
                db_ref[...] = jnp.zeros_like(db_ref)

            dgate_ref[0] += _row_sum(sums[2])
            dg_ref[...] += _row_sum(sums[3])
            db_ref[...] += _row_sum(sums[4])

    row = lambda w_: pl.BlockSpec((tm, w_), lambda i: (i, 0))
    vec = pl.BlockSpec((1, 1, d), lambda i: (i // tpb, 0, 0))
    mod_spec = pl.BlockSpec((1, N_MOD, d), lambda i: (i // tpb, 0, 0))
    vshape = jax.ShapeDtypeStruct((nb, 1, d), F32)
    if final:
        in_specs = [row(kdim), _resident(w.shape), row(d), row(d), mod_spec]
        args = (a, w, dr, xin, mod)
        out_specs = [row(d), vec, vec]
        out_shape = [jax.ShapeDtypeStruct((t, d), F32), vshape, vshape]
    else:
        in_specs = [row(kdim), _resident(w.shape), row(d), row(d), row(1), _full((2, d)), mod_spec, row(d)]
        args = (a, w, dr, xin, rstd_prev, lnp_prev, mod, branch_prev)
        out_specs = [row(d), row(d), vec, vec, vec, _full((1, d)), _full((1, d))]
        out_shape = [jax.ShapeDtypeStruct((t, d), F32), jax.ShapeDtypeStruct((t, d), BF16), vshape, vshape, vshape,
                     jax.ShapeDtypeStruct((1, d), F32), jax.ShapeDtypeStruct((1, d), F32)]
    return _pcall(
        body, name=name, grid=(t // tm,), in_specs=in_specs, out_specs=out_specs, out_shape=out_shape,
        scratch_shapes=[pltpu.VMEM((tm, d), F32)], args=args, comm=comm)


def _grad_chip_sum(pos, a, b, *, half_on_rows, name, comm=None):
    t, m = a.shape
    n = b.shape[1]
    tk = min(2048, t)
    nk = t // tk
    half = lambda p, pos_ref: 1 - pos_ref[2] - p + 2 * p * pos_ref[2]
    if half_on_rows:
        n_j = N_CHIPS
        tile = (m // 2, n // n_j)
        a_spec = pl.BlockSpec((tk, tile[0]), lambda p, j, k, pos_ref: (k, half(p, pos_ref)))
        b_spec = pl.BlockSpec((tk, tile[1]), lambda p, j, k, pos_ref: (k, j))
        out_tile = pl.BlockSpec((1, *tile), lambda p, j, k, pos_ref: (0, 0, j * p))
        total = (1, m // 2, n)
    else:
        n_j = 2
        tile = (m // n_j, n // 2)
        a_spec = pl.BlockSpec((tk, tile[0]), lambda p, j, k, pos_ref: (k, j))
        b_spec = pl.BlockSpec((tk, tile[1]), lambda p, j, k, pos_ref: (k, half(p, pos_ref)))
        out_tile = pl.BlockSpec((1, *tile), lambda p, j, k, pos_ref: (0, j * p, 0))
        total = (1, m, n // 2)

    def body(pos_ref, a_ref, b_ref, s32_ref, s16_ref, land_ref, acc, theirs, send_sems, recv_sems, copy_sem):
        p, j, k = pl.program_id(0), pl.program_id(1), pl.program_id(2)
        x, y, c = _position()

        def push(jj):
            return pltpu.make_async_remote_copy(
                src_ref=acc.at[jj], dst_ref=land_ref.at[jj], send_sem=send_sems.at[jj], recv_sem=recv_sems.at[jj],
                device_id=(x, y, 1 - c), device_id_type=MESH)

        fetch = pltpu.make_async_copy(land_ref.at[j], theirs, copy_sem)

        @pl.when(jnp.logical_and(p == 1, k == 0))
        def _():
            push(j).wait_send()
            push(j).wait_recv()
            fetch.start()

        part = _dot_tn(a_ref[...], b_ref[...])

        @pl.when(k == 0)
        def _():
            acc[j] = part

        @pl.when(k > 0)
        def _():
            acc[j] += part

        @pl.when(jnp.logical_and(p == 0, k == nk - 1))
        def _():
            push(j).start()

        @pl.when(jnp.logical_and(p == 1, k == nk - 1))
        def _():
            fetch.wait()
            s = acc[j] + theirs[...]
            s32_ref[0] = s
            s16_ref[0] = s.astype(BF16)

    out = _pcall(
        body, name=name, grid=(2, n_j, nk), in_specs=[a_spec, b_spec], out_specs=[out_tile, out_tile, ANY_SPEC],
        out_shape=[jax.ShapeDtypeStruct(total, F32), jax.ShapeDtypeStruct(total, BF16),
                   jax.ShapeDtypeStruct((n_j, *tile), F32)],
        scratch_shapes=[pltpu.VMEM((n_j, *tile), F32), pltpu.VMEM(tile, F32),
                        pltpu.SemaphoreType.DMA((n_j,)), pltpu.SemaphoreType.DMA((n_j,)), pltpu.SemaphoreType.DMA],
        args=(a, b), prefetch=pos, comm=comm)
    if comm is None:
        return out[0], out[1]
    (s32, s16, _), extra = out
    return (s32, s16), extra


def _matmul_nt_bf16(a, w, *, seq, name):
    t, kdim = a.shape
    n = w.shape[0]
    tm = min(512, seq)

    def body(a_ref, w_ref, o_ref):
        av = a_ref[...]
        for j in range(n // COL_CHUNK):
            o_ref[:, j * COL_CHUNK:(j + 1) * COL_CHUNK] = _dot_nt(
                av, w_ref[j * COL_CHUNK:(j + 1) * COL_CHUNK, :]).astype(BF16)

    return pl.pallas_call(
        body, name=name, grid=(t // tm,),
        in_specs=[pl.BlockSpec((tm, kdim), lambda i: (i, 0)), _resident((n, kdim))],
        out_specs=pl.BlockSpec((tm, n), lambda i: (i, 0)),
        out_shape=jax.ShapeDtypeStruct((t, n), BF16),
        compiler_params=_params(("arbitrary",)),
    )(a, w)


def _attention_bwd(q, k, v, dmixin, sinks, *, seq, name, comm=None):
    t = q.shape[0]
    nblk = seq // BLOCK
    tile = ATTN_TILE_BLOCKS * BLOCK

    def body(q_ref, kp_ref, kc_ref, vp_ref, vc_ref, do_ref, sink_ref,
             dq_ref, dkp_ref, dkc_ref, dvp_ref, dvc_ref, dsink_ref):
        n = pl.program_id(0)

        @pl.when(n == 0)
        def _():
            dsink_ref[...] = jnp.zeros_like(dsink_ref)

        srow = lax.broadcasted_iota(jnp.int32, (8, LANE), 0)
        dsink = jnp.zeros((8, LANE), F32)
        for s in range(ATTN_TILE_BLOCKS):
            rows, k_prev, k_cur, v_prev, v_cur, first = _attn_sub_block(s, n, nblk, kp_ref, kc_ref, vp_ref, vc_ref)
            dqs, dks, dvs = [], [], []
            for g in range(N_KV_HEADS):
                qs, kk, vv, pn, psn = _attn_group(q_ref, rows, k_prev, k_cur, v_prev, v_cur, sink_ref, g, first)
                dos = jnp.concatenate(
                    [do_ref[rows, (GQA_GROUP * g + j) * HEAD_DIM:(GQA_GROUP * g + j + 1) * HEAD_DIM]
                     for j in range(GQA_GROUP)], axis=0)
                dp = _dot_nt(vv, dos)
                delta = jnp.sum(pn * dp, axis=0, keepdims=True)
                ds = pn * (dp - delta)
                dsk = psn * delta
                for j in range(GQA_GROUP):
                    tot = jnp.sum(dsk[:, j * BLOCK:(j + 1) * BLOCK], axis=1, keepdims=True)
                    dsink = dsink - jnp.where(srow == GQA_GROUP * g + j, tot, 0.0)
                dsb = (ds * (HEAD_DIM ** -0.5)).astype(BF16)
                dqs.append(_heads_to_lanes(_dot_tn(kk, dsb)))
                dks.append(jnp.dot(dsb, qs, preferred_element_type=F32))
                dvs.append(jnp.dot(pn.astype(BF16), dos, preferred_element_type=F32))
            dq_ref[rows, :] = jnp.concatenate(dqs, axis=1)
            dkp_ref[rows, :] = jnp.concatenate([x[0:BLOCK, :] for x in dks], axis=1)
            dkc_ref[rows, :] = jnp.concatenate([x[BLOCK:, :] for x in dks], axis=1)
            dvp_ref[rows, :] = jnp.concatenate([x[0:BLOCK, :] for x in dvs], axis=1)
            dvc_ref[rows, :] = jnp.concatenate([x[BLOCK:, :] for x in dvs], axis=1)
        dsink_ref[...] += dsink

    cur = lambda w: pl.BlockSpec((tile, w), lambda n: (n, 0))
    prev = lambda w: pl.BlockSpec((BLOCK, w), lambda n: (jnp.maximum(n * ATTN_TILE_BLOCKS - 1, 0), 0))
    kv = jax.ShapeDtypeStruct((t, KV_WIDTH), F32)
    return _pcall(
        body, name=name, grid=(t // tile,),
        in_specs=[cur(ATTN_WIDTH), prev(KV_WIDTH), cur(KV_WIDTH), prev(KV_WIDTH), cur(KV_WIDTH), cur(ATTN_WIDTH),
                  pl.BlockSpec(memory_space=pltpu.SMEM)],
        out_specs=[cur(ATTN_WIDTH), cur(KV_WIDTH), cur(KV_WIDTH), cur(KV_WIDTH), cur(KV_WIDTH), _full((8, LANE))],
        out_shape=[jax.ShapeDtypeStruct((t, ATTN_WIDTH), F32), kv, kv, kv, kv, jax.ShapeDtypeStruct((8, LANE), F32)],
        args=(q, k, k, v, v, dmixin, sinks), comm=comm)


def _mix_bwd_assemble(dq, dkp, dkc, dvp, dvc, cos, sa, sb, dmixin, ubc, cw, *, seq, name, comm=None):
    t = dq.shape[0]
    cwid = CONV_WIDTH
    tm = min(2 * BLOCK, seq)
    tiles_per_seq = seq // tm
    ntile = t // tm
    nblk_all = t // BLOCK
    per_tile = tm // BLOCK

    def body(*refs):
        dq_ref, dkc_ref, dvc_ref = refs[0:3]
        dkp_refs, dvp_refs = refs[3:3 + per_tile], refs[3 + per_tile:3 + 2 * per_tile]
        (cos_ref, sa_ref, sb_ref, dco_ref, dcon_ref, ubc_ref, hprev_ref, hnext_ref, cw_ref,
         dproj_ref, dcw_ref, zbuf, dybuf) = refs[3 + 2 * per_tile:]
        i = pl.program_id(0)
        first = (i % tiles_per_seq) == 0
        last = (i % tiles_per_seq) == tiles_per_seq - 1
        glast = i == ntile - 1

        @pl.when(i == 0)
        def _():
            dcw_ref[...] = jnp.zeros_like(dcw_ref)

        def with_next_block(cur_ref, nxt_refs):
            nxt = [r[...] for r in nxt_refs]
            nxt[-1] = jnp.where(glast, 0.0, nxt[-1])
            return cur_ref[...] + jnp.concatenate(nxt, axis=0)

        cos_t, sa_t, sb_t = cos_ref[...], sa_ref[...], sb_ref[...]
        for j in range(ATTN_WIDTH // LANE):
            dproj_ref[:, j * LANE:(j + 1) * LANE] = _rope_t(
                dq_ref[:, j * LANE:(j + 1) * LANE], cos_t, sa_t, sb_t).astype(BF16)
        dk = with_next_block(dkc_ref, dkp_refs)
        dproj_ref[:, ATTN_WIDTH:ATTN_WIDTH + KV_WIDTH] = _rope_t(dk, cos_t, sa_t, sb_t).astype(BF16)
        dv = with_next_block(dvc_ref, dvp_refs)
        dproj_ref[:, ATTN_WIDTH + KV_WIDTH:ATTN_WIDTH + 2 * KV_WIDTH] = dv.astype(BF16)

        u, bg, cg = (ubc_ref[:, s * cwid:(s + 1) * cwid].astype(F32) for s in range(3))
        z = cg * u
        hz = hprev_ref[:, 2 * cwid:3 * cwid].astype(F32) * hprev_ref[:, 0:cwid].astype(F32)
        zbuf[0:HALO, :] = jnp.where(first, 0.0, hz)
        zbuf[HALO:HALO + tm, :] = z
        z2, z1 = zbuf[HALO - 2:HALO - 2 + tm, :], zbuf[HALO - 1:HALO - 1 + tm, :]
        w0, w1, w2 = cw_ref[0:1, :], cw_ref[1:2, :], cw_ref[2:3, :]
        y = w0 * z2 + w1 * z1 + w2 * z
        dco = dco_ref[...].astype(F32)
        dyc = dco * bg
        dyn = dcon_ref[...].astype(F32) * hnext_ref[:, cwid:2 * cwid].astype(F32)
        dybuf[0:tm, :] = dyc
        dybuf[tm:tm + HALO, :] = jnp.where(last, 0.0, dyn)
        dz = w2 * dyc + w1 * dybuf[1:1 + tm, :] + w0 * dybuf[2:2 + tm, :]
        srow = lax.broadcasted_iota(jnp.int32, (8, cwid), 0)
        dcw_ref[...] += (jnp.where(srow == 0, _row_sum(dyc * z2), 0.0) + jnp.where(srow == 1, _row_sum(dyc * z1), 0.0)
                         + jnp.where(srow == 2, _row_sum(dyc * z), 0.0))
        base = ATTN_WIDTH + 2 * KV_WIDTH
        dproj_ref[:, base:base + cwid] = (dz * cg).astype(BF16)
        dproj_ref[:, base + cwid:base + 2 * cwid] = (dco * y).astype(BF16)
        dproj_ref[:, base + 2 * cwid:base + 3 * cwid] = (dz * u).astype(BF16)

    cur = lambda w: pl.BlockSpec((tm, w), lambda i: (i, 0))
    nxt = [pl.BlockSpec((BLOCK, KV_WIDTH), lambda i, s=s: (jnp.minimum(i * per_tile + s + 1, nblk_all - 1), 0))
           for s in range(per_tile)]
    prev_halo = pl.BlockSpec((HALO, 3 * cwid), lambda i: (jnp.maximum(i * (tm // HALO) - 1, 0), 0))
    next_halo = lambda w, col: pl.BlockSpec(
        (HALO, w), lambda i: (jnp.minimum((i + 1) * (tm // HALO), t // HALO - 1), col))
    return _pcall(
        body, name=name, grid=(ntile,),
        in_specs=[cur(ATTN_WIDTH), cur(KV_WIDTH), cur(KV_WIDTH), *nxt, *nxt,
                  cur(LANE), cur(LANE), cur(LANE),
                  pl.BlockSpec((tm, cwid), lambda i: (i, 1)), next_halo(cwid, 1),
                  cur(3 * cwid), prev_halo, next_halo(3 * cwid, 0), _full((8, cwid))],
        out_specs=[cur(IN_WIDTH), _full((8, cwid))],
        out_shape=[jax.ShapeDtypeStruct((t, IN_WIDTH), BF16), jax.ShapeDtypeStruct((8, cwid), F32)],
        scratch_shapes=[pltpu.VMEM((tm + HALO, cwid), F32), pltpu.VMEM((tm + HALO, cwid), F32)],
        args=(dq, dkc, dvc, *([dkp] * per_tile), *([dvp] * per_tile), cos, sa, sb, dmixin, dmixin,
              ubc, ubc, ubc, cw), comm=comm)


def _ada_fwd(c_all, w_ada, b_ada_shard, chip, casts, pos_col, inv_lane, *, name, comm=None):
    nb, d = c_all.shape
    n = w_ada.shape[1]
    t = pos_col.shape[0]
    steps = 2
    tn = n // steps
    n_cast = len(casts)

    def body(chip_ref, c_ref, w_ref, b_ref, pos_ref, inv_ref, *refs):
        cast_in = refs[:n_cast]
        o_ref, cos_ref, sa_ref, sb_ref = refs[n_cast:n_cast + 4]
        cast_out = refs[n_cast + 4:]
        cv = c_ref[...]
        cond = cv * _sigmoid(cv)
        o_ref[...] = jnp.dot(cond, w_ref[...], preferred_element_type=F32,
                             precision=lax.Precision.HIGHEST) + b_ref[...]
        ang = pos_ref[...].astype(F32) * inv_ref[...]
        sin = jnp.sin(ang)
        dim = lax.broadcasted_iota(jnp.int32, ang.shape, 1) % HEAD_DIM
        cos_ref[...] = jnp.cos(ang)
        sa_ref[...] = jnp.where(dim < ROT_DIM // 2, -sin, 0.0)
        sb_ref[...] = jnp.where(dim >= ROT_DIM // 2, sin, 0.0)
        for src, dst in zip(cast_in, cast_out):
            dst[...] = src[...].astype(BF16)

    table = pl.BlockSpec((t // steps, LANE), lambda j, chip_ref: (j, 0))
    in_specs = [_full((nb, d)), pl.BlockSpec((d, tn), lambda j, chip_ref: (0, j)),
                pl.BlockSpec((1, tn), lambda j, chip_ref: (0, j)),
                pl.BlockSpec((t // steps, 1), lambda j, chip_ref: (j, 0)), _full((1, LANE))]
    out_specs = [pl.BlockSpec((nb, tn), lambda j, chip_ref: (0, j)), table, table, table]
    out_shape = [jax.ShapeDtypeStruct((nb, n), F32)] + [jax.ShapeDtypeStruct((t, LANE), F32)] * 3
    for w, col_kind in casts:
        r, c = w.shape
        tr = r // steps
        in_specs.append(pl.BlockSpec((tr, c), lambda j, chip_ref: (j, 0)))
        if col_kind:
            out_specs.append(pl.BlockSpec((tr, c), lambda j, chip_ref: (j, chip_ref[0])))
            out_shape.append(jax.ShapeDtypeStruct((r, c * N_CHIPS), BF16))
        else:
            out_specs.append(pl.BlockSpec((tr, c), lambda j, chip_ref: (chip_ref[0] * steps + j, 0)))
            out_shape.append(jax.ShapeDtypeStruct((r * N_CHIPS, c), BF16))
    out = _pcall(body, name=name, grid=(steps,), in_specs=in_specs, out_specs=out_specs, out_shape=out_shape,
                 args=(c_all, w_ada, b_ada_shard, pos_col, inv_lane, *[w for w, _ in casts]), prefetch=chip, comm=comm)
    res, extra = out if comm is not None else (out, None)
    return res[0], tuple(res[1:4]), list(res[4:]), extra


def _small_finish(gathered, dmod_all, dmod_shard, c_all_t, *, name):
    d = D_MODEL
    nb, n = dmod_shard.shape

    def body(g_ref, dm_ref, dms_ref, ct_ref, sum_ref, gw_ref, gb_ref):
        total = g_ref[0]
        for dev in range(1, N_DEV):
            total = total + g_ref[dev]
        sum_ref[...] = total
        gb_ref[...] = _row_sum(dm_ref[...])
        ctv = ct_ref[...]
        cond_t = (ctv * _sigmoid(ctv)).astype(BF16)
        for jb in range(n // COL_CHUNK):
            gw_ref[:, jb * COL_CHUNK:(jb + 1) * COL_CHUNK] = jnp.dot(
                cond_t, dms_ref[:, jb * COL_CHUNK:(jb + 1) * COL_CHUNK].astype(BF16), preferred_element_type=F32)

    return pl.pallas_call(
        body, name=name, grid=(1,),
        in_specs=[_full((N_DEV, SMALL_ROWS, d)), _full((nb, N_MOD * d)), _full((nb, n)), _full((d, nb))],
        out_specs=[_full((SMALL_ROWS, d)), _full((d, n)), _full((1, N_MOD * d))],
        out_shape=[jax.ShapeDtypeStruct((SMALL_ROWS, d), F32), jax.ShapeDtypeStruct((d, n), F32),
                   jax.ShapeDtypeStruct((1, N_MOD * d), F32)],
        compiler_params=_params(("arbitrary",)),
    )(gathered, dmod_all, dmod_shard, c_all_t)


def _row_tile(r, c, budget=1 << 21):
    if r * c * 4 <= budget or r % 16:
        return r
    best = 16
    for tr in range(16, r + 1, 16):
        if r % tr == 0 and tr * c * 4 <= budget:
            best = tr
    return best


def _cast_into(w, chip, col_kind, *, name):
    r, c = w.shape
    tr = _row_tile(r, c)

    def body(chip_ref, w_ref, o_ref):
        o_ref[...] = w_ref[...].astype(BF16)

    if col_kind:
        out_spec = pl.BlockSpec((tr, c), lambda i, chip_ref: (i, chip_ref[0]))
        out_shape = jax.ShapeDtypeStruct((r, c * N_CHIPS), BF16)
    else:
        out_spec = pl.BlockSpec((tr, c), lambda i, chip_ref: (chip_ref[0] * (r // tr) + i, 0))
        out_shape = jax.ShapeDtypeStruct((r * N_CHIPS, c), BF16)
    return _pcall(body, name=name, grid=(r // tr,), in_specs=[pl.BlockSpec((tr, c), lambda i, chip_ref: (i, 0))],
                  out_specs=out_spec, out_shape=out_shape, args=(w,), prefetch=chip)


def _adamw(w, g, m, v, *, name, comm=None):
    r, c = w.shape
    tr = _row_tile(r, c)
    c1 = 1.0 - ADAM_B1 ** ADAM_STEP
    c2 = 1.0 - ADAM_B2 ** ADAM_STEP

    def body(w_ref, g_ref, m_ref, v_ref, d_ref, nm_ref, nv_ref):
        gv = g_ref[...]
        m2 = ADAM_B1 * m_ref[...] + (1.0 - ADAM_B1) * gv
        v2 = ADAM_B2 * v_ref[...] + (1.0 - ADAM_B2) * (gv * gv)
        d_ref[...] = -ADAM_LR * ((m2 / c1) / (jnp.sqrt(v2 / c2) + ADAM_EPS) + ADAM_WD * w_ref[...])
        nm_ref[...] = m2
        nv_ref[...] = v2

    spec = pl.BlockSpec((tr, c), lambda i: (i, 0))
    sh = jax.ShapeDtypeStruct((r, c), F32)
    return _pcall(body, name=name, grid=(r // tr,), in_specs=[spec] * 4, out_specs=[spec] * 3, out_shape=[sh] * 3,
                  args=(w, g, m, v), comm=comm)


def _sum_final(pos, s32, recv, *, col_kind, n_shard, name, comm=None):
    def body(pos_ref, s_ref, r_ref, o_ref):
        total = ((s_ref[0] + r_ref[0].astype(F32)) + r_ref[1].astype(F32)) + r_ref[2].astype(F32)
        if col_kind:
            o_ref[0] = total
        else:
            o_ref[...] = total

    if col_kind:
        rows, cols = s32.shape[1], n_shard
        tr = _row_tile(rows, cols)
        own = pl.BlockSpec((1, tr, cols), lambda i, pos: (0, i, 2 * pos[0] + pos[1]))
        out_spec = pl.BlockSpec((1, tr, cols), lambda i, pos: (pos[2], i, 0))
        out_shape = jax.ShapeDtypeStruct((2, rows, cols), F32)
    else:
        rows, cols = n_shard, s32.shape[2]
        tr = _row_tile(rows, cols)
        own = pl.BlockSpec((1, tr, cols), lambda i, pos: (0, (2 * pos[0] + pos[1]) * (rows // tr) + i, 0))
        out_spec = pl.BlockSpec((tr, cols), lambda i, pos: (i, pos[2]))
        out_shape = jax.ShapeDtypeStruct((rows, 2 * cols), F32)
    return _pcall(
        body, name=name, grid=(rows // tr,),
        in_specs=[own, pl.BlockSpec((3, tr, cols), lambda i, pos: (0, i, 0))], out_specs=out_spec,
        out_shape=out_shape, args=(s32, recv), prefetch=pos, comm=comm)


def _position():
    return lax.axis_index("x"), lax.axis_index("y"), lax.axis_index("c")


def _allgather8(x_shard, *, name, comm=None):
    m_per, n = x_shard.shape
    nci, nco = (0, 0) if comm is None else (len(comm.inputs), len(comm.out_shapes))

    def body(*refs):
        x_ref, refs = refs[0], refs[1:]
        cin, refs = refs[:nci], refs[nci:]
        out_ref, refs = refs[0], refs[1:]
        cout, refs = refs[:nco], refs[nco:]
        (send_sems, recv_sems, local_sem), csems = refs[:3], refs[3:]
        x, y, c = _position()
        me, sibling = (x, y, c), (x, y, 1 - c)
        chips = [(1 - x, y), (x, 1 - y), (1 - x, 1 - y)]

        def rows(px, py, pc):
            return out_ref.at[pl.ds((4 * px + 2 * py + pc) * m_per, m_per), :]

        def copy(k, block, to, src=None):
            return pltpu.make_async_remote_copy(
                src_ref=rows(*block) if src is None else src, dst_ref=rows(*block),
                send_sem=send_sems.at[k], recv_sem=recv_sems.at[k], device_id=to, device_id_type=MESH)

        mine = pltpu.make_async_copy(x_ref, rows(*me), local_sem)
        mine.start()
        first = [copy(0, me, sibling, src=x_ref)]
        first += [copy(1 + j, me, (*chip, c), src=x_ref) for j, chip in enumerate(chips)]
        for cp in first:
            cp.start()
        if comm is not None:
            comm.start(cin, cout, csems)
        passed = [copy(4 + j, (*chip, c), sibling) for j, chip in enumerate(chips)]
        for j, chip in enumerate(chips):
            copy(1 + j, (*chip, c), me).wait_recv()
            passed[j].start()
        copy(0, sibling, me).wait_recv()
        for j, chip in enumerate(chips):
            copy(4 + j, (*chip, 1 - c), me).wait_recv()
        for cp in first + passed:
            cp.wait_send()
        mine.wait()
        if comm is not None:
            comm.middle(cin, cout, csems)
            comm.late(cin, cout, csems)
            comm.finish(cin, cout, csems)

    vmem = pl.BlockSpec(memory_space=pltpu.VMEM)
    sems = [pltpu.SemaphoreType.DMA((7,)), pltpu.SemaphoreType.DMA((7,)), pltpu.SemaphoreType.DMA]
    out = jax.ShapeDtypeStruct((N_DEV * m_per, n), x_shard.dtype)
    if comm is None:
        return pl.pallas_call(body, name=name, out_shape=out, in_specs=[vmem], out_specs=vmem,
                              scratch_shapes=sems)(x_shard)
    res = pl.pallas_call(
        body, name=name, out_shape=[out] + list(comm.out_shapes), in_specs=[vmem] + [ANY_SPEC] * nci,
        out_specs=[vmem] + [ANY_SPEC] * nco, scratch_shapes=sems + list(comm.sems),
        input_output_aliases={1 + i: 1 + o for i, o in comm.aliases.items()})(x_shard, *comm.inputs)
    return res[0], list(res[1:])


def _peer_chips(x, y):
    return [(1 - x, y), (x, 1 - y), (1 - x, 1 - y)]


class _GatherJob:
    def __init__(self, pieces):
        self.pieces = pieces
        n_p = len(pieces)
        self.inputs = [p[0] for p in pieces]
        self.out_shapes = [jax.ShapeDtypeStruct(p[0].shape, p[0].dtype) for p in pieces]
        for buf, col_kind, r0, nr in pieces:
            half_rows = buf.shape[0] // (2 if col_kind else 2 * N_CHIPS)
            assert r0 % 16 == 0 and nr % 16 == 0 and nr >= 32 and r0 + nr <= half_rows, (buf.shape, r0, nr)
        self.aliases = {p: p for p in range(n_p)}
        dma = pltpu.SemaphoreType.DMA
        self.sems = [dma((2 * n_p,))] * 4 + [dma((4 * n_p,))] * 2

    def _region(self, cout, p, chip_idx, half, part=None):
        buf, col_kind, r0, nr = self.pieces[p]
        first = -(-nr // 32) * 16
        if part == 0:
            nr = first
        elif part == 1:
            r0, nr = r0 + first, nr - first
        if col_kind:
            n = buf.shape[1] // N_CHIPS
            return cout[p].at[pl.ds(half * (buf.shape[0] // 2) + r0, nr), pl.ds(chip_idx * n, n)]
        n = buf.shape[0] // N_CHIPS
        return cout[p].at[pl.ds(chip_idx * n + half * (n // 2) + r0, nr), :]

    def _copies(self, cout, sems):
        send1, recv1, send2, recv2, fsend, frecv = sems
        x, y, c = _position()
        k = 2 * x + y
        sibling = (x, y, 1 - c)
        x_nbr, y_nbr, diag = _peer_chips(x, y)
        chip_of = lambda ch: 2 * ch[0] + ch[1]

        def remote(region, ssem, rsem, to):
            return pltpu.make_async_remote_copy(src_ref=region, dst_ref=region, send_sem=ssem, recv_sem=rsem,
                                                device_id=to, device_id_type=MESH)

        hop1, arrived1, hop2, arrived2, fwds, fwd_arrived = [], [], [], [], [], []
        for p in range(len(self.pieces)):
            for j, nbr in enumerate((x_nbr, y_nbr)):
                i1 = 2 * p + j
                hop1.append(remote(self._region(cout, p, k, c), send1.at[i1], recv1.at[i1], (*nbr, c)))
                arrived1.append(remote(self._region(cout, p, chip_of(nbr), c), send1.at[i1], recv1.at[i1], (*nbr, c)))
            hop2.append(remote(self._region(cout, p, chip_of(x_nbr), c, 0), send2.at[2 * p], recv2.at[2 * p],
                               (*y_nbr, c)))
            hop2.append(remote(self._region(cout, p, chip_of(y_nbr), c, 1), send2.at[2 * p + 1], recv2.at[2 * p + 1],
                               (*x_nbr, c)))
            arrived2.append(remote(self._region(cout, p, chip_of(diag), c, 0), send2.at[2 * p], recv2.at[2 * p],
                                   (*y_nbr, c)))
            arrived2.append(remote(self._region(cout, p, chip_of(diag), c, 1), send2.at[2 * p + 1],
                                   recv2.at[2 * p + 1], (*x_nbr, c)))
            landed = [(chip_of(x_nbr), None), (chip_of(y_nbr), None), (chip_of(diag), 0), (chip_of(diag), 1)]
            for q, (chip_idx, part) in enumerate(landed):
                i3 = 4 * p + q
                fwds.append(remote(self._region(cout, p, chip_idx, c, part), fsend.at[i3], frecv.at[i3], sibling))
                fwd_arrived.append(remote(self._region(cout, p, chip_idx, 1 - c, part), fsend.at[i3], frecv.at[i3],
                                          sibling))
        return hop1, arrived1, hop2, arrived2, fwds, fwd_arrived

    def start(self, cin, cout, sems):
        for cp in self._copies(cout, sems)[0]:
            cp.start()

    def middle(self, cin, cout, sems):
        _, arrived1, hop2, _, fwds, _ = self._copies(cout, sems)
        for p in range(len(self.pieces)):
            for j in range(2):
                arrived1[2 * p + j].wait_recv()
                hop2[2 * p + j].start()
                fwds[4 * p + j].start()

    def late(self, cin, cout, sems):
        _, _, _, arrived2, fwds, _ = self._copies(cout, sems)
        for p in range(len(self.pieces)):
            for j in range(2):
                arrived2[2 * p + j].wait_recv()
                fwds[4 * p + 2 + j].start()

    def finish(self, cin, cout, sems):
        hop1, _, hop2, _, fwds, fwd_arrived = self._copies(cout, sems)
        for cp in fwd_arrived:
            cp.wait_recv()
        for cp in hop1 + hop2 + fwds:
            cp.wait_send()


class _PairedJob:
    aliases = {}

    def start(self, cin, cout, sems):
        for cp in self._copies(cin, cout, sems):
            cp.start()

    def middle(self, cin, cout, sems):
        pass

    late = middle

    def finish(self, cin, cout, sems):
        copies = self._copies(cin, cout, sems)
        for cp in copies:
            cp.wait_recv()
        for cp in copies:
            cp.wait_send()


class _ExchangeJob(_PairedJob):
    def __init__(self, s16, kinds, sizes):
        self.inputs, self.kinds, self.sizes = list(s16), list(kinds), list(sizes)
        self.out_shapes = [jax.ShapeDtypeStruct((3, s.shape[1], n) if kd else (3, n, s.shape[2]), s.dtype)
                           for s, kd, n in zip(s16, kinds, sizes)]
        self.sems = [pltpu.SemaphoreType.DMA((3 * len(s16),)), pltpu.SemaphoreType.DMA((3 * len(s16),))]

    def _copies(self, cin, cout, sems):
        send_sems, recv_sems = sems
        x, y, c = _position()
        copies = []
        for p, src_ref in enumerate(cin):
            for j, chip in enumerate(_peer_chips(x, y)):
                kk = 2 * chip[0] + chip[1]
                n = self.sizes[p]
                src = src_ref.at[0, :, pl.ds(kk * n, n)] if self.kinds[p] else src_ref.at[0, pl.ds(kk * n, n), :]
                copies.append(pltpu.make_async_remote_copy(
                    src_ref=src, dst_ref=cout[p].at[j], send_sem=send_sems.at[3 * p + j],
                    recv_sem=recv_sems.at[3 * p + j], device_id=(*chip, c), device_id_type=MESH))
        return copies


class _ShareJob:
    def __init__(self, halves):
        self.inputs = list(halves)
        self.out_shapes = [jax.ShapeDtypeStruct(h.shape, h.dtype) for h in halves]
        self.aliases = {p: p for p in range(len(halves))}
        self.sems = [pltpu.SemaphoreType.DMA((len(halves),)), pltpu.SemaphoreType.DMA((len(halves),))]

    def _copies(self, cout, sems, half):
        send_sems, recv_sems = sems
        x, y, c = _position()
        h = c if half == "mine" else 1 - c

        def region(o):
            if len(o.shape) == 3:
                return o.at[h]
            hc = o.shape[1] // 2
            return o.at[:, pl.ds(h * hc, hc)]

        return [pltpu.make_async_remote_copy(
            src_ref=region(o), dst_ref=region(o), send_sem=send_sems.at[p], recv_sem=recv_sems.at[p],
            device_id=(x, y, 1 - c), device_id_type=MESH) for p, o in enumerate(cout)]

    def start(self, cin, cout, sems):
        for cp in self._copies(cout, sems, "mine"):
            cp.start()

    def middle(self, cin, cout, sems):
        pass

    late = middle

    def finish(self, cin, cout, sems):
        for cp in self._copies(cout, sems, "theirs"):
            cp.wait_recv()
        for cp in self._copies(cout, sems, "mine"):
            cp.wait_send()


class _MultiJob:
    def __init__(self, jobs):
        self.jobs = jobs
        self.inputs = [a for j in jobs for a in j.inputs]
        self.out_shapes = [s for j in jobs for s in j.out_shapes]
        self.sems = [s for j in jobs for s in j.sems]
        self.aliases = {}
        i0 = o0 = 0
        for j in jobs:
            for i, o in j.aliases.items():
                self.aliases[i0 + i] = o0 + o
            i0 += len(j.inputs)
            o0 += len(j.out_shapes)

    def _parts(self, cin, cout, sems):
        i0 = o0 = s0 = 0
        for j in self.jobs:
            ni, no, ns = len(j.inputs), len(j.out_shapes), len(j.sems)
            yield j, cin[i0:i0 + ni], cout[o0:o0 + no], sems[s0:s0 + ns]
            i0, o0, s0 = i0 + ni, o0 + no, s0 + ns

    def start(self, cin, cout, sems):
        for j, a, b, s in self._parts(cin, cout, sems):
            j.start(a, b, s)

    def middle(self, cin, cout, sems):
        for j, a, b, s in self._parts(cin, cout, sems):
            j.middle(a, b, s)

    def late(self, cin, cout, sems):
        for j, a, b, s in self._parts(cin, cout, sems):
            j.late(a, b, s)

    def finish(self, cin, cout, sems):
        for j, a, b, s in self._parts(cin, cout, sems):
            j.finish(a, b, s)


def _rope_inv_lane():
    inv_freq = jnp.power(jnp.float32(ROPE_THETA), -jnp.arange(0, ROT_DIM, 2, dtype=F32) / ROT_DIM)
    inv_head = jnp.concatenate([inv_freq, inv_freq, jnp.zeros((HEAD_DIM - ROT_DIM,), F32)])
    return jnp.concatenate([inv_head] * (LANE // HEAD_DIM))[None, :]


def kernel(x, c, positions, w_ada, b_ada, ffn1_w_gate_up, ffn1_w_down, ln1_g, ln1_b, w_in, conv_w, attn_sinks, w_out, ln2_g, ln2_b, ffn2_w_gate_up, ffn2_w_down, ln3_g, ln3_b, loss_target, m_w_ada, m_b_ada, m_ffn1_w_gate_up, m_ffn1_w_down, m_ln1_g, m_ln1_b, m_w_in, m_conv_w, m_attn_sinks, m_w_out, m_ln2_g, m_ln2_b, m_ffn2_w_gate_up, m_ffn2_w_down, m_ln3_g, m_ln3_b, v_w_ada, v_b_ada, v_ffn1_w_gate_up, v_ffn1_w_down, v_ln1_g, v_ln1_b, v_w_in, v_conv_w, v_attn_sinks, v_w_out, v_ln2_g, v_ln2_b, v_ffn2_w_gate_up, v_ffn2_w_down, v_ln3_g, v_ln3_b):
    d = D_MODEL
    nb, seq, _ = x.shape
    t = nb * seq
    f = ffn1_w_down.shape[1] * N_CHIPS
    ax, ay, ac = _position()
    chip = 2 * ax + ay
    dev = 2 * chip + ac
    pos = jnp.stack([ax, ay, ac]).astype(jnp.int32)

    x2 = x.reshape(t, d)
    tgt2 = loss_target.reshape(t, d)
    ln1 = jnp.concatenate([ln1_g, ln1_b], axis=0)
    ln2 = jnp.concatenate([ln2_g, ln2_b], axis=0)
    ln3 = jnp.concatenate([ln3_g, ln3_b], axis=0)
    sinks = attn_sinks.reshape(N_Q_HEADS)

    gu_cuts = [0, 176, 352, d // 2]
    gu_part = lambda buf, s: (buf, True, gu_cuts[s], gu_cuts[s + 1] - gu_cuts[s])
    chip_arr = jnp.reshape(chip, (1,)).astype(jnp.int32)
    b_gu1 = _cast_into(ffn1_w_gate_up[0], chip_arr, True, name="cast_gu1")

    n_ada = w_ada.shape[2]
    c_all, (b_gu1,) = _allgather8(c.reshape(nb * d // LANE, LANE), name="gather_c", comm=_GatherJob([gu_part(b_gu1, 0)]))
    c_all = c_all.reshape(N_DEV * nb, d)
    b_shard = lax.dynamic_slice(b_ada, (0, chip * n_ada), (1, n_ada))
    later_shards = [(ffn1_w_down[0], False), (w_in[0].T, False), (w_out[0], False), (ffn2_w_gate_up[0], True),
                    (ffn2_w_down[0], False)]
    mod_part, (cos_t, sa_t, sb_t), (b_d1, b_in, b_out, b_gu2, b_d2), (b_gu1,) = _ada_fwd(
        c_all, w_ada[0], b_shard, chip_arr, later_shards, positions.reshape(t, 1), _rope_inv_lane(), name="ada_fwd",
        comm=_GatherJob([gu_part(b_gu1, 1)]))
    conv_rows = jnp.pad(conv_w[0], ((0, 5), (0, n_ada - conv_w.shape[2])))
    part = jnp.concatenate([mod_part, conv_rows], axis=0)
    parts, (wgu1,) = _allgather8(part, name="gather_mod", comm=_GatherJob([gu_part(b_gu1, 2)]))
    parts = parts.reshape(N_DEV, N_DEV * nb + 8, n_ada)
    mod_all = jnp.concatenate([parts[2 * k, :N_DEV * nb, :] for k in range(N_CHIPS)], axis=1)
    mod = lax.dynamic_slice(mod_all, (dev * nb, 0), (nb, N_MOD * d)).reshape(nb, N_MOD, d)
    cw_full = jnp.concatenate([parts[2 * k, N_DEV * nb:, :conv_w.shape[2]] for k in range(N_CHIPS)], axis=1)

    n_gu, n_d, n_in, n_out = (ffn1_w_gate_up.shape[2], ffn1_w_down.shape[1], w_in.shape[2], w_out.shape[1])

    def whole(buf, col_kind):
        return (buf, col_kind, 0, buf.shape[0] // (2 if col_kind else 2 * N_CHIPS))

    (h1, a1, dact1), (wd1, wout) = _ffn_up(x2, ln1, mod, wgu1, seq=seq, sc_idx=1, sh_idx=0, use_ln=False,
                                         name="ffn1_up", comm=_GatherJob([whole(b_d1, False), whole(b_out, False)]))
    (f1, xhat1, rstd1), (win_t,) = _ffn_down_ln(a1, wd1, x2, ln1, mod, seq=seq, gate_idx=2, use_ln=False,
                                                name="ffn1_down", comm=_GatherJob([whole(b_in, False)]))
    (h2, q, k, v, ubc), (b_gu2,) = _in_proj(
        xhat1, ln1, mod, win_t, cos_t, sa_t, sb_t, seq=seq, sc_idx=4, sh_idx=3, name="in_proj",
        comm=_GatherJob([gu_part(b_gu2, 0)]))
    attn, (b_gu2,) = _attention(q, k, v, sinks, seq=seq, name="attention", comm=_GatherJob([gu_part(b_gu2, 1)]))
    (mixin, mix, xhat2, rstd2), (wgu2,) = _out_proj(
        attn, ubc, cw_full, wout, xhat1, ln1, mod, seq=seq, gate_idx=5, name="out_proj",
        comm=_GatherJob([gu_part(b_gu2, 2)]))
    (h3, a3, dact3), (wd2,) = _ffn_up(xhat2, ln2, mod, wgu2, seq=seq, sc_idx=7, sh_idx=6, use_ln=True, name="ffn2_up",
                                    comm=_GatherJob([whole(b_d2, False)]))
    dr3, df3, loss_cols, dln3g, dln3b, dgate3 = _ffn_down_loss(
        a3, wd2, xhat2, ln2, mod, ln3, tgt2, seq=seq, gate_idx=8, name="ffn2_down_loss")

    dgu3 = _ffn_bwd_act(df3, wd2, dact3, seq=seq, name="ffn2_bwd_act")
    s32_d2, s16_d2 = _grad_chip_sum(pos, a3, df3, half_on_rows=False, name="grad_wd2")
    (s32_gu2, s16_gu2), (recv_d2,) = _grad_chip_sum(pos, h3, dgu3, half_on_rows=True, name="grad_wgu2",
                                                    comm=_ExchangeJob([s16_d2], [False], [n_d]))
    (dr2, dmix, dsc3, dsh3, dgate2, dln2g, dln2b), (recv_gu2,) = _bwd_in(
        dgu3, wgu2, dr3, xhat2, rstd2, ln2, mod, mix, seq=seq, w_is_nt=True, sc_idx=7, gate_idx=5,
        branch_scale=1.0, final=False, name="ffn2_bwd_in", comm=_ExchangeJob([s16_gu2], [True], [n_gu]))
    s32_out, s16_out = _grad_chip_sum(pos, mixin, dmix, half_on_rows=False, name="grad_wout")
    dmixin = _matmul_nt_bf16(dmix, wout, seq=seq, name="out_proj_bwd")
    (dq, dkp, dkc, dvp, dvc, dsink), (recv_out,) = _attention_bwd(
        q, k, v, dmixin, sinks, seq=seq, name="attention_bwd", comm=_ExchangeJob([s16_out], [False], [n_out]))
    dproj, dcw = _mix_bwd_assemble(
        dq, dkp, dkc, dvp, dvc, cos_t, sa_t, sb_t, dmixin, ubc, cw_full, seq=seq, name="mix_bwd")
    s32_in, s16_in = _grad_chip_sum(pos, dproj, h2, half_on_rows=False, name="grad_win")
    (dr1, df1, dsc2, dsh2, dgate1, dln1g, dln1b), (recv_in,) = _bwd_in(
        dproj, win_t, dr2, xhat1, rstd1, ln1, mod, f1, seq=seq, w_is_nt=False, sc_idx=4, gate_idx=2,
        branch_scale=0.5, final=False, name="in_proj_bwd", comm=_ExchangeJob([s16_in], [False], [n_in]))
    s32_d1, s16_d1 = _grad_chip_sum(pos, a1, df1, half_on_rows=False, name="grad_wd1")
    dgu1, (recv_d1,) = _ffn_bwd_act(df1, wd1, dact1, seq=seq, name="ffn1_bwd_act",
                                    comm=_ExchangeJob([s16_d1], [False], [n_d]))
    s32_gu1, s16_gu1 = _grad_chip_sum(pos, h1, dgu1, half_on_rows=True, name="grad_wgu1")

    def final_half(s32_, recv_, col_kind, n_shard, name_):
        return _sum_final(pos, s32_, recv_, col_kind=col_kind, n_shard=n_shard, name=name_)

    early = [final_half(s32_gu2, recv_gu2, True, n_gu, "sum_final_gu2"),
             final_half(s32_d2, recv_d2, False, n_d, "sum_final_d2"),
             final_half(s32_out, recv_out, False, n_out, "sum_final_out"),
             final_half(s32_in, recv_in, False, n_in, "sum_final_in"),
             final_half(s32_d1, recv_d1, False, n_d, "sum_final_d1")]
    (grad_x, dsc1, dsh1), (recv_gu1, full_gu2, full_d2, full_out, full_in, full_d1) = _bwd_in(
        dgu1, wgu1, dr1, x2, None, None, mod, None, seq=seq, w_is_nt=True, sc_idx=1, gate_idx=None,
        branch_scale=None, final=True, name="ffn1_bwd_in",
        comm=_MultiJob([_ExchangeJob([s16_gu1], [True], [n_gu]), _ShareJob(early)]))
    late = [final_half(s32_gu1, recv_gu1, True, n_gu, "sum_final_gu1")]

    dmod = jnp.concatenate([dsh1, dsc1, dgate1, dsh2, dsc2, dgate2, dsh3, dsc3, dgate3], axis=1)
    loss_row = jnp.sum(loss_cols, axis=1, keepdims=True) * (0.5 / d)
    lane_row = lambda a: jnp.pad(a, ((0, 0), (0, d - a.shape[1])))
    block = jnp.concatenate(
        [dmod.reshape(nb * N_MOD, d), dln1g, dln1b, dln2g, dln2b, dln3g, dln3b,
         lane_row(dcw[0:3, :]), lane_row(dsink[:, 0:1].reshape(1, N_Q_HEADS)), lane_row(loss_row)], axis=0)
    block = jnp.pad(block, ((0, SMALL_ROWS - block.shape[0]), (0, 0)))
    gathered, (full_gu1,) = _allgather8(block, name="gather_small", comm=_ShareJob(late))
    gathered = gathered.reshape(N_DEV, SMALL_ROWS, d)
    dmod_all = gathered[:, :nb * N_MOD, :].reshape(N_DEV * nb, N_MOD * d)
    dmod_shard = lax.dynamic_slice(dmod_all, (0, chip * n_ada), (N_DEV * nb, n_ada))
    small, g_w_ada, g_b_ada = _small_finish(gathered, dmod_all, dmod_shard, c_all.T, name="small_finish")
    r0 = nb * N_MOD
    loss = small[r0 + 10, 0]
    g_ln = [small[r0 + i:r0 + i + 1, :] for i in range(6)]
    g_cw_full = small[r0 + 6:r0 + 9, :CONV_WIDTH]
    g_conv = lax.dynamic_slice(g_cw_full, (0, chip * conv_w.shape[2]), (3, conv_w.shape[2]))
    g_sinks = small[r0 + 9:r0 + 10, :N_Q_HEADS]

    def flat2(a):
        return a.reshape(-1, a.shape[-1])

    def unhalve(a):
        return a.reshape(2 * a.shape[1], a.shape[2])

    results = {}

    def adamw(name_, w_, g_, m_, v_):
        g2 = flat2(g_)
        dl, nm, nv = _adamw(flat2(w_), g2, flat2(m_), flat2(v_), name="adamw_" + name_)
        results[name_] = tuple(a.reshape(w_.shape) for a in (g2, dl, nm, nv))

    adamw("w_ada", w_ada, g_w_ada, m_w_ada, v_w_ada)
    adamw("ffn2_w_gate_up", ffn2_w_gate_up, unhalve(full_gu2), m_ffn2_w_gate_up, v_ffn2_w_gate_up)
    adamw("ffn2_w_down", ffn2_w_down, full_d2, m_ffn2_w_down, v_ffn2_w_down)
    adamw("w_out", w_out, full_out, m_w_out, v_w_out)
    adamw("w_in", w_in, full_in.T, m_w_in, v_w_in)
    adamw("ffn1_w_gate_up", ffn1_w_gate_up, unhalve(full_gu1), m_ffn1_w_gate_up, v_ffn1_w_gate_up)
    adamw("ffn1_w_down", ffn1_w_down, full_d1, m_ffn1_w_down, v_ffn1_w_down)
    adamw("b_ada", b_ada, g_b_ada, m_b_ada, v_b_ada)
    adamw("ln1_g", ln1_g, g_ln[0], m_ln1_g, v_ln1_g)
    adamw("ln1_b", ln1_b, g_ln[1], m_ln1_b, v_ln1_b)
    adamw("ln2_g", ln2_g, g_ln[2], m_ln2_g, v_ln2_g)
    adamw("ln2_b", ln2_b, g_ln[3], m_ln2_b, v_ln2_b)
    adamw("ln3_g", ln3_g, g_ln[4], m_ln3_g, v_ln3_g)
    adamw("ln3_b", ln3_b, g_ln[5], m_ln3_b, v_ln3_b)
    adamw("conv_w", conv_w, g_conv, m_conv_w, v_conv_w)
    adamw("attn_sinks", attn_sinks, g_sinks, m_attn_sinks, v_attn_sinks)
    order = ["w_ada", "b_ada", "ffn1_w_gate_up", "ffn1_w_down", "ln1_g", "ln1_b", "w_in", "conv_w", "attn_sinks",
             "w_out", "ln2_g", "ln2_b", "ffn2_w_gate_up", "ffn2_w_down", "ln3_g", "ln3_b"]
    return (loss, grad_x.reshape(x.shape), *[results[n_][0] for n_ in order], *[results[n_][1] for n_ in order],
            *[results[n_][2] for n_ in order], *[results[n_][3] for n_ in order])
```

```python
import jax
import jax.numpy as jnp
from jax import lax
from jax.experimental import pallas as pl
from jax.experimental.pallas import tpu as pltpu

F32 = jnp.float32
BF16 = jnp.bfloat16
MESH = pl.DeviceIdType.MESH

D_MODEL = 1024
HEAD_DIM = 64
ATTN_WIDTH = 512
CONV_WIDTH = 512
N_Q_HEADS = 8
N_KV_HEADS = 2
GQA_GROUP = 4
KV_WIDTH = 128
WINDOW = 128
BLOCK = 128
ROT_DIM = 16
ROPE_THETA = 500000.0
N_MOD = 9
LN_EPS = 1e-5
DN_ALPHA = 2.0 ** 0.25
IN_WIDTH = 2304
N_CHIPS = 4
N_DEV = 8
SMALL_ROWS = 32

ADAM_LR = 0.001
ADAM_B1 = 0.9
ADAM_B2 = 0.999
ADAM_EPS = 1e-08
ADAM_WD = 0.01
ADAM_STEP = 10

LANE = 128
HALO = 16
COL_CHUNK = 256
VMEM_LIMIT = 56 * 1024 * 1024


def _params(sem=None, vmem=True):
    return pltpu.CompilerParams(dimension_semantics=sem, vmem_limit_bytes=VMEM_LIMIT if vmem else None)


def _sigmoid(g):
    return 0.5 * jnp.tanh(0.5 * g) + 0.5


def _row_sum(v):
    return jnp.sum(v, axis=0, keepdims=True)


ROW_CHUNK = 16
EPILOGUE_UNROLL = 8


def _fold8(v):
    return v[0:8, :] + v[8:16, :]


def _row_chunk_loop(n_rows, step, init):
    per_iter = ROW_CHUNK * EPILOGUE_UNROLL
    assert n_rows % per_iter == 0, n_rows

    def body(it, carry):
        for s in range(EPILOGUE_UNROLL):
            start = pl.multiple_of(it * per_iter + s * ROW_CHUNK, ROW_CHUNK)
            carry = step(pl.ds(start, ROW_CHUNK), carry)
        return carry

    return lax.fori_loop(0, n_rows // per_iter, body, init)


def _ln_stats(r):
    mu = jnp.mean(r, axis=-1, keepdims=True)
    rc = r - mu
    var = jnp.mean(rc * rc, axis=-1, keepdims=True)
    rstd = lax.rsqrt(var + LN_EPS)
    return rc * rstd, rstd


def _ln_bwd(dxo, xhat, rstd, g):
    dxhat = dxo * g
    m1 = jnp.mean(dxhat, axis=-1, keepdims=True)
    m2 = jnp.mean(dxhat * xhat, axis=-1, keepdims=True)
    return rstd * (dxhat - m1 - xhat * m2)


def _dot_nt(a, b):
    return lax.dot_general(a, b, (((1,), (1,)), ((), ())), preferred_element_type=F32)


def _dot_tn(a, b):
    return lax.dot_general(a, b, (((0,), (0,)), ((), ())), preferred_element_type=F32)


def _full(shape):
    nd = len(shape)
    return pl.BlockSpec(shape, lambda *_: (0,) * nd)


def _resident(shape):
    nd = len(shape)
    return pl.BlockSpec(shape, lambda *_: (0,) * nd, pipeline_mode=pl.Buffered(1))


ANY_SPEC = pl.BlockSpec(memory_space=pl.ANY)


def _pcall(body, *, name, grid, in_specs, out_specs, out_shape, args, scratch_shapes=(), comm=None, prefetch=None):
    single = not isinstance(out_shape, (list, tuple))
    out_specs = [out_specs] if single else list(out_specs)
    out_shape = [out_shape] if single else list(out_shape)
    in_specs = list(in_specs)
    scratch_shapes = list(scratch_shapes)
    sem = ("arbitrary",) * len(grid)
    n_pre = 0 if prefetch is None else 1
    pre_args = () if prefetch is None else (prefetch,)

    def call(fn, ins_, outs_, shapes_, scratch_, aliases_, operands):
        if prefetch is None:
            return pl.pallas_call(fn, name=name, grid=grid, in_specs=ins_, out_specs=outs_, out_shape=shapes_,
                                  scratch_shapes=scratch_, input_output_aliases=aliases_,
                                  compiler_params=_params(sem))(*operands)
        spec = pltpu.PrefetchScalarGridSpec(num_scalar_prefetch=1, grid=grid, in_specs=ins_, out_specs=outs_,
                                            scratch_shapes=scratch_)
        return pl.pallas_call(fn, name=name, grid_spec=spec, out_shape=shapes_,
                              input_output_aliases={n_pre + i: o for i, o in aliases_.items()},
                              compiler_params=_params(sem))(*pre_args, *operands)

    if comm is None:
        res = call(body, in_specs, out_specs, out_shape, scratch_shapes, {}, args)
        return res[0] if single else res
    n_in, n_out, n_scr = len(in_specs), len(out_specs), len(scratch_shapes)
    nci, nco = len(comm.inputs), len(comm.out_shapes)
    n_steps = 1
    for g in grid:
        n_steps *= g
    staged = n_steps >= 8
    middle_step = (n_steps * 5) // 8 - 1
    late_step = n_steps - 1 - max(1, n_steps // 8)

    def wrapped(*refs):
        pre, refs = refs[:n_pre], refs[n_pre:]
        ins, refs = refs[:n_in], refs[n_in:]
        cin, refs = refs[:nci], refs[nci:]
        outs, refs = refs[:n_out], refs[n_out:]
        cout, refs = refs[:nco], refs[nco:]
        scr, csems = refs[:n_scr], refs[n_scr:]
        step = pl.program_id(0)
        for ax in range(1, len(grid)):
            step = step * grid[ax] + pl.program_id(ax)

        @pl.when(step == 0)
        def _():
            comm.start(cin, cout, csems)

        body(*pre, *ins, *outs, *scr)

        if staged:
            @pl.when(step == middle_step)
            def _():
                comm.middle(cin, cout, csems)

            @pl.when(step == late_step)
            def _():
                comm.late(cin, cout, csems)

        @pl.when(step == n_steps - 1)
        def _():
            if not staged:
                comm.middle(cin, cout, csems)
                comm.late(cin, cout, csems)
            comm.finish(cin, cout, csems)

    res = call(wrapped, in_specs + [ANY_SPEC] * nci, out_specs + [ANY_SPEC] * nco,
               out_shape + list(comm.out_shapes), scratch_shapes + list(comm.sems),
               {n_in + i: n_out + o for i, o in comm.aliases.items()}, (*args, *comm.inputs))
    main = res[:n_out]
    return (main[0] if single else main), list(res[n_out:])


def _ffn_up(xin, lnp, mod, w, *, seq, sc_idx, sh_idx, use_ln, name, comm=None):
    t, d = xin.shape
    f = w.shape[1] // 2
    tm = min(512, seq)
    tpb = seq // tm
    ch = min(COL_CHUNK, f)

    def body(x_ref, ln_ref, mod_ref, w_ref, h_ref, a_ref, dact_ref):
        x = x_ref[...]
        if use_ln:
            x = x * ln_ref[0:1, :] + ln_ref[1:2, :]
        h = x * (1.0 + mod_ref[0, sc_idx:sc_idx + 1, :]) + mod_ref[0, sh_idx:sh_idx + 1, :]
        hb = h.astype(BF16)
        h_ref[...] = hb
        for j in range(f // ch):
            g = jnp.dot(hb, w_ref[:, j * ch:(j + 1) * ch], preferred_element_type=F32)
            u = jnp.dot(hb, w_ref[:, f + j * ch:f + (j + 1) * ch], preferred_element_type=F32)
            s = _sigmoid(g)
            silu = g * s
            a_ref[:, j * ch:(j + 1) * ch] = (silu * u).astype(BF16)
            dact_ref[:, j * ch:(j + 1) * ch] = (u * (s + silu * (1.0 - s))).astype(BF16)
            dact_ref[:, f + j * ch:f + (j + 1) * ch] = silu.astype(BF16)

    return _pcall(
        body, name=name, grid=(t // tm,),
        in_specs=[pl.BlockSpec((tm, d), lambda i: (i, 0)), _full((2, d)),
                  pl.BlockSpec((1, N_MOD, d), lambda i: (i // tpb, 0, 0)), _resident((d, 2 * f))],
        out_specs=[pl.BlockSpec((tm, d), lambda i: (i, 0)), pl.BlockSpec((tm, f), lambda i: (i, 0)),
                   pl.BlockSpec((tm, 2 * f), lambda i: (i, 0))],
        out_shape=[jax.ShapeDtypeStruct((t, d), BF16), jax.ShapeDtypeStruct((t, f), BF16),
                   jax.ShapeDtypeStruct((t, 2 * f), BF16)],
        args=(xin, lnp, mod, w), comm=comm)


def _ffn_down_ln(a, wd, xin, lnp_in, mod, *, seq, gate_idx, use_ln, name, comm=None):
    t, f = a.shape
    d = wd.shape[1]
    tm = min(512, seq)
    tpb = seq // tm

    def body(a_ref, wd_ref, x_ref, ln_ref, mod_ref, f_ref, xhat_ref, rstd_ref, acc):
        av = a_ref[...]
        for j in range(d // COL_CHUNK):
            acc[:, j * COL_CHUNK:(j + 1) * COL_CHUNK] = jnp.dot(
                av, wd_ref[:, j * COL_CHUNK:(j + 1) * COL_CHUNK], preferred_element_type=F32)
        scale = 0.5 * (1.0 + mod_ref[0, gate_idx:gate_idx + 1, :])

        fo = acc[...]
        x = x_ref[...]
        if use_ln:
            x = x * ln_ref[0:1, :] + ln_ref[1:2, :]
        xhat, rstd = _ln_stats(DN_ALPHA * x + scale * fo)
        f_ref[...] = fo.astype(BF16)
        xhat_ref[...] = xhat
        rstd_ref[...] = rstd

    return _pcall(
        body, name=name, grid=(t // tm,),
        in_specs=[pl.BlockSpec((tm, f), lambda i: (i, 0)), _resident((f, d)),
                  pl.BlockSpec((tm, d), lambda i: (i, 0)), _full((2, d)),
                  pl.BlockSpec((1, N_MOD, d), lambda i: (i // tpb, 0, 0))],
        out_specs=[pl.BlockSpec((tm, d), lambda i: (i, 0)), pl.BlockSpec((tm, d), lambda i: (i, 0)),
                   pl.BlockSpec((tm, 1), lambda i: (i, 0))],
        out_shape=[jax.ShapeDtypeStruct((t, d), BF16), jax.ShapeDtypeStruct((t, d), F32),
                   jax.ShapeDtypeStruct((t, 1), F32)],
        scratch_shapes=[pltpu.VMEM((tm, d), F32)],
        args=(a, wd, xin, lnp_in, mod), comm=comm)


def _ffn_down_loss(a, wd, xhat_in, lnp_in, mod, lnp_out, tgt, *, seq, gate_idx, name):
    t, f = a.shape
    d = wd.shape[1]
    nb = t // seq
    tm = min(512, seq)
    tpb = seq // tm

    def body(a_ref, wd_ref, x_ref, lnin_ref, mod_ref, lnout_ref, tgt_ref,
             dr_ref, df_ref, loss_ref, dg_ref, db_ref, dgate_ref, acc):
        i = pl.program_id(0)
        av = a_ref[...]
        for j in range(d // COL_CHUNK):
            acc[:, j * COL_CHUNK:(j + 1) * COL_CHUNK] = jnp.dot(
                av, wd_ref[:, j * COL_CHUNK:(j + 1) * COL_CHUNK], preferred_element_type=F32)
        scale = 0.5 * (1.0 + mod_ref[0, gate_idx:gate_idx + 1, :])
        ag_in, ab_in = DN_ALPHA * lnin_ref[0:1, :], DN_ALPHA * lnin_ref[1:2, :]
        g_out, b_out = lnout_ref[0:1, :], lnout_ref[1:2, :]
        g_over_d = g_out * (1.0 / d)

        def chunk(rows, carry):
            s_loss, s_dg, s_db, s_gate = carry
            fo = acc[rows, :]
            xhat, rstd = _ln_stats(x_ref[rows, :] * ag_in + ab_in + scale * fo)
            e = xhat * g_out + b_out - tgt_ref[rows, :]
            dr = _ln_bwd(e, xhat, rstd, g_over_d)
            dr_ref[rows, :] = dr
            df_ref[rows, :] = (scale * dr).astype(BF16)
            return s_loss + _fold8(e * e), s_dg + _fold8(e * xhat), s_db + _fold8(e), s_gate + _fold8(fo * dr)

        zero = jnp.zeros((8, d), F32)
        s_loss, s_dg, s_db, s_gate = _row_chunk_loop(tm, chunk, (zero, zero, zero, zero))
        s_dg, s_db, s_gate = s_dg * (1.0 / d), s_db * (1.0 / d), s_gate * 0.5

        @pl.when(i == 0)
        def _():
            loss_ref[...] = jnp.zeros_like(loss_ref)
            dg_ref[...] = jnp.zeros_like(dg_ref)
            db_ref[...] = jnp.zeros_like(db_ref)

        @pl.when(i % tpb == 0)
        def _():
            dgate_ref[...] = jnp.zeros_like(dgate_ref)

        loss_ref[...] += _row_sum(s_loss)
        dg_ref[...] += _row_sum(s_dg)
        db_ref[...] += _row_sum(s_db)
        dgate_ref[0] += _row_sum(s_gate)

    return pl.pallas_call(
        body, name=name, grid=(t // tm,), scratch_shapes=[pltpu.VMEM((tm, d), F32)],
        in_specs=[pl.BlockSpec((tm, f), lambda i: (i, 0)), _resident((f, d)),
                  pl.BlockSpec((tm, d), lambda i: (i, 0)), _full((2, d)),
                  pl.BlockSpec((1, N_MOD, d), lambda i: (i // tpb, 0, 0)), _full((2, d)),
                  pl.BlockSpec((tm, d), lambda i: (i, 0))],
        out_specs=[pl.BlockSpec((tm, d), lambda i: (i, 0)), pl.BlockSpec((tm, d), lambda i: (i, 0)),
                   _full((1, d)), _full((1, d)), _full((1, d)),
                   pl.BlockSpec((1, 1, d), lambda i: (i // tpb, 0, 0))],
        out_shape=[jax.ShapeDtypeStruct((t, d), F32), jax.ShapeDtypeStruct((t, d), BF16),
                   jax.ShapeDtypeStruct((1, d), F32), jax.ShapeDtypeStruct((1, d), F32),
                   jax.ShapeDtypeStruct((1, d), F32), jax.ShapeDtypeStruct((nb, 1, d), F32)],
        compiler_params=_params(("arbitrary",)),
    )(a, wd, xhat_in, lnp_in, mod, lnp_out, tgt)


def _rope(v, cos, sa, sb):
    return v * cos + pltpu.roll(v, LANE - ROT_DIM // 2, 1) * sa + pltpu.roll(v, ROT_DIM // 2, 1) * sb


def _rope_t(dy, cos, sa, sb):
    return dy * cos + pltpu.roll(dy * sa, ROT_DIM // 2, 1) + pltpu.roll(dy * sb, LANE - ROT_DIM // 2, 1)


def _in_proj(xhat, lnp, mod, w_t, cos, sa, sb, *, seq, sc_idx, sh_idx, name, comm=None):
    t, d = xhat.shape
    tm = min(512, seq)
    tpb = seq // tm
    n_conv = 3 * CONV_WIDTH

    def body(x_ref, ln_ref, mod_ref, w_ref, cos_ref, sa_ref, sb_ref, h_ref, q_ref, k_ref, v_ref, ubc_ref):
        x = x_ref[...] * ln_ref[0:1, :] + ln_ref[1:2, :]
        h = x * (1.0 + mod_ref[0, sc_idx:sc_idx + 1, :]) + mod_ref[0, sh_idx:sh_idx + 1, :]
        hb = h.astype(BF16)
        h_ref[...] = hb
        cos_t, sa_t, sb_t = cos_ref[...], sa_ref[...], sb_ref[...]
        for j in range(ATTN_WIDTH // COL_CHUNK):
            p = _dot_nt(hb, w_ref[j * COL_CHUNK:(j + 1) * COL_CHUNK, :])
            for s in range(COL_CHUNK // LANE):
                q_ref[:, j * COL_CHUNK + s * LANE:j * COL_CHUNK + (s + 1) * LANE] = _rope(
                    p[:, s * LANE:(s + 1) * LANE], cos_t, sa_t, sb_t).astype(BF16)
        p = _dot_nt(hb, w_ref[ATTN_WIDTH:ATTN_WIDTH + 2 * KV_WIDTH, :])
        k_ref[...] = _rope(p[:, 0:KV_WIDTH], cos_t, sa_t, sb_t).astype(BF16)
        v_ref[...] = p[:, KV_WIDTH:].astype(BF16)
        base = ATTN_WIDTH + 2 * KV_WIDTH
        for j in range(n_conv // COL_CHUNK):
            ubc_ref[:, j * COL_CHUNK:(j + 1) * COL_CHUNK] = _dot_nt(
                hb, w_ref[base + j * COL_CHUNK:base + (j + 1) * COL_CHUNK, :]).astype(BF16)

    row = lambda w: pl.BlockSpec((tm, w), lambda i: (i, 0))
    return _pcall(
        body, name=name, grid=(t // tm,),
        in_specs=[row(d), _full((2, d)), pl.BlockSpec((1, N_MOD, d), lambda i: (i // tpb, 0, 0)),
                  _resident((IN_WIDTH, d)), row(LANE), row(LANE), row(LANE)],
        out_specs=[row(d), row(ATTN_WIDTH), row(KV_WIDTH), row(KV_WIDTH), row(n_conv)],
        out_shape=[jax.ShapeDtypeStruct((t, d), BF16), jax.ShapeDtypeStruct((t, ATTN_WIDTH), BF16),
                   jax.ShapeDtypeStruct((t, KV_WIDTH), BF16), jax.ShapeDtypeStruct((t, KV_WIDTH), BF16),
                   jax.ShapeDtypeStruct((t, n_conv), BF16)],
        args=(xhat, lnp, mod, w_t, cos, sa, sb), comm=comm)


ATTN_TILE_BLOCKS = 2


def _attn_sub_block(s, tile, nblk, kp_ref, kc_ref, vp_ref, vc_ref):
    rows = slice(s * BLOCK, (s + 1) * BLOCK)
    if s == 0:
        first = ((tile * ATTN_TILE_BLOCKS) % nblk) == 0
        return rows, (kp_ref, slice(0, BLOCK)), (kc_ref, rows), (vp_ref, slice(0, BLOCK)), (vc_ref, rows), first
    before = slice((s - 1) * BLOCK, s * BLOCK)
    return rows, (kc_ref, before), (kc_ref, rows), (vc_ref, before), (vc_ref, rows), False


def _attn_group(q_ref, rows, k_prev, k_cur, v_prev, v_cur, sink_ref, g, first):
    lo, hi = g * HEAD_DIM, (g + 1) * HEAD_DIM
    kk = jnp.concatenate([k_prev[0][k_prev[1], lo:hi], k_cur[0][k_cur[1], lo:hi]], axis=0)
    vv = jnp.concatenate([v_prev[0][v_prev[1], lo:hi], v_cur[0][v_cur[1], lo:hi]], axis=0)
    qs = jnp.concatenate([q_ref[rows, (GQA_GROUP * g + j) * HEAD_DIM:(GQA_GROUP * g + j + 1) * HEAD_DIM]
                          for j in range(GQA_GROUP)], axis=0)
    cols = GQA_GROUP * BLOCK
    ki = lax.broadcasted_iota(jnp.int32, (2 * BLOCK, cols), 0)
    col = lax.broadcasted_iota(jnp.int32, (2 * BLOCK, cols), 1)
    diff = (col & (BLOCK - 1)) + BLOCK - ki
    valid = (diff >= 0) & (diff < WINDOW) & ((ki >= BLOCK) | jnp.logical_not(first))
    s = _dot_nt(kk, qs) * (HEAD_DIM ** -0.5)
    s = jnp.where(valid, s, -1e30)
    hcol = lax.broadcasted_iota(jnp.int32, (1, cols), 1)
    sink = jnp.zeros((1, cols), F32)
    for j in range(GQA_GROUP):
        sink = jnp.where(hcol // BLOCK == j, sink_ref[GQA_GROUP * g + j], sink)
    m = jnp.maximum(jnp.max(s, axis=0, keepdims=True), sink)
    p = jnp.exp(s - m)
    ps = jnp.exp(sink - m)
    inv = 1.0 / (jnp.sum(p, axis=0, keepdims=True) + ps)
    return qs, kk, vv, p * inv, ps * inv


def _heads_to_lanes(x_t):
    return jnp.concatenate([x_t[:, j * BLOCK:(j + 1) * BLOCK].T for j in range(GQA_GROUP)], axis=1)


def _attention(q, k, v, sinks, *, seq, name, comm=None):
    t = q.shape[0]
    nblk = seq // BLOCK
    tile = ATTN_TILE_BLOCKS * BLOCK

    def body(q_ref, kp_ref, kc_ref, vp_ref, vc_ref, sink_ref, o_ref):
        for s in range(ATTN_TILE_BLOCKS):
            rows, k_prev, k_cur, v_prev, v_cur, first = _attn_sub_block(
                s, pl.program_id(0), nblk, kp_ref, kc_ref, vp_ref, vc_ref)
            outs = []
            for g in range(N_KV_HEADS):
                _, _, vv, pn, _ = _attn_group(q_ref, rows, k_prev, k_cur, v_prev, v_cur, sink_ref, g, first)
                outs.append(_heads_to_lanes(_dot_tn(vv, pn.astype(BF16))))
            o_ref[rows, :] = jnp.concatenate(outs, axis=1).astype(BF16)

    cur = lambda w: pl.BlockSpec((tile, w), lambda n: (n, 0))
    prev = lambda w: pl.BlockSpec((BLOCK, w), lambda n: (jnp.maximum(n * ATTN_TILE_BLOCKS - 1, 0), 0))
    return _pcall(
        body, name=name, grid=(t // tile,),
        in_specs=[cur(ATTN_WIDTH), prev(KV_WIDTH), cur(KV_WIDTH), prev(KV_WIDTH), cur(KV_WIDTH),
                  pl.BlockSpec(memory_space=pltpu.SMEM)],
        out_specs=cur(ATTN_WIDTH),
        out_shape=jax.ShapeDtypeStruct((t, ATTN_WIDTH), BF16),
        args=(q, k, k, v, v, sinks), comm=comm)


def _out_proj(attn, ubc, cw, wout, xhat_in, lnp_in, mod, *, seq, gate_idx, name, comm=None):
    t, d = xhat_in.shape
    tm = min(512, seq)
    tpb = seq // tm
    cwid = CONV_WIDTH

    def body(attn_ref, ubc_ref, halo_ref, cw_ref, w_ref, x_ref, ln_ref, mod_ref,
             mixin_ref, mix_ref, xhat_ref, rstd_ref, zbuf, acc):
        first = (pl.program_id(0) % tpb) == 0
        u, bg, cg = (ubc_ref[:, s * cwid:(s + 1) * cwid].astype(F32) for s in range(3))
        z = cg * u
        hz = halo_ref[:, 2 * cwid:3 * cwid].astype(F32) * halo_ref[:, 0:cwid].astype(F32)
        zbuf[0:HALO, :] = jnp.where(first, 0.0, hz)
        zbuf[HALO:HALO + tm, :] = z
        y = (cw_ref[0:1, :] * zbuf[HALO - 2:HALO - 2 + tm, :] + cw_ref[1:2, :] * zbuf[HALO - 1:HALO - 1 + tm, :]
             + cw_ref[2:3, :] * z)
        mixin_ref[:, 0:ATTN_WIDTH] = attn_ref[...]
        mixin_ref[:, ATTN_WIDTH:] = (bg * y).astype(BF16)
        mv = mixin_ref[...]
        for j in range(d // COL_CHUNK):
            acc[:, j * COL_CHUNK:(j + 1) * COL_CHUNK] = jnp.dot(
                mv, w_ref[:, j * COL_CHUNK:(j + 1) * COL_CHUNK], preferred_element_type=F32)
        scale = 1.0 + mod_ref[0, gate_idx:gate_idx + 1, :]

        mix = acc[...]
        xhat, rstd = _ln_stats(DN_ALPHA * (x_ref[...] * ln_ref[0:1, :] + ln_ref[1:2, :]) + scale * mix)
        mix_ref[...] = mix.astype(BF16)
        xhat_ref[...] = xhat
        rstd_ref[...] = rstd

    row = lambda w: pl.BlockSpec((tm, w), lambda i: (i, 0))
    return _pcall(
        body, name=name, grid=(t // tm,),
        in_specs=[row(ATTN_WIDTH), row(3 * cwid),
                  pl.BlockSpec((HALO, 3 * cwid), lambda i: (jnp.maximum(i * (tm // HALO) - 1, 0), 0)),
                  _full((8, cwid)), _resident((d, d)), row(d), _full((2, d)),
                  pl.BlockSpec((1, N_MOD, d), lambda i: (i // tpb, 0, 0))],
        out_specs=[row(d), row(d), row(d), row(1)],
        out_shape=[jax.ShapeDtypeStruct((t, d), BF16), jax.ShapeDtypeStruct((t, d), BF16),
                   jax.ShapeDtypeStruct((t, d), F32), jax.ShapeDtypeStruct((t, 1), F32)],
        scratch_shapes=[pltpu.VMEM((tm + HALO, cwid), F32), pltpu.VMEM((tm, d), F32)],
        args=(attn, ubc, ubc, cw, wout, xhat_in, lnp_in, mod), comm=comm)


def _ffn_bwd_act(df, wd, dact, *, seq, name, comm=None):
    t, d = df.shape
    f = wd.shape[0]
    tm = min(512, seq)
    ch = min(COL_CHUNK, f)

    def body(df_ref, wd_ref, dact_ref, dgu_ref):
        dfv = df_ref[...]
        for j in range(f // ch):
            da = _dot_nt(dfv, wd_ref[j * ch:(j + 1) * ch, :])
            dgu_ref[:, j * ch:(j + 1) * ch] = (da * dact_ref[:, j * ch:(j + 1) * ch].astype(F32)).astype(BF16)
            dgu_ref[:, f + j * ch:f + (j + 1) * ch] = (
                da * dact_ref[:, f + j * ch:f + (j + 1) * ch].astype(F32)).astype(BF16)

    return _pcall(
        body, name=name, grid=(t // tm,),
        in_specs=[pl.BlockSpec((tm, d), lambda i: (i, 0)), _resident((f, d)),
                  pl.BlockSpec((tm, 2 * f), lambda i: (i, 0))],
        out_specs=pl.BlockSpec((tm, 2 * f), lambda i: (i, 0)),
        out_shape=jax.ShapeDtypeStruct((t, 2 * f), BF16),
        args=(df, wd, dact), comm=comm)


def _bwd_in(a, w, dr, xin, rstd_prev, lnp_prev, mod, branch_prev, *, seq, w_is_nt, sc_idx, gate_idx,
            branch_scale, final, name, comm=None):
    t, kdim = a.shape
    d = dr.shape[1]
    nb = t // seq
    tm = min(512, seq)
    tpb = seq // tm

    def body(*refs):
        if final:
            a_ref, w_ref, dr_ref, x_ref, mod_ref, dx_ref, dsc_ref, dsh_ref, acc = refs
        else:
            (a_ref, w_ref, dr_ref, x_ref, rstd_ref, ln_ref, mod_ref, br_ref,
             drp_ref, dbr_ref, dsc_ref, dsh_ref, dgate_ref, dg_ref, db_ref, acc) = refs
        i = pl.program_id(0)
        av = a_ref[...]
        for j in range(d // COL_CHUNK):
            cols = slice(j * COL_CHUNK, (j + 1) * COL_CHUNK)
            acc[:, cols] = (_dot_nt(av, w_ref[cols, :]) if w_is_nt
                            else jnp.dot(av, w_ref[:, cols], preferred_element_type=F32))
        sc1 = 1.0 + mod_ref[0, sc_idx:sc_idx + 1, :]
        if not final:
            g_prev, b_prev = ln_ref[0:1, :], ln_ref[1:2, :]
            bscale = branch_scale * (1.0 + mod_ref[0, gate_idx:gate_idx + 1, :])

        def chunk(rows, carry):
            dh = acc[rows, :]
            dx = DN_ALPHA * dr_ref[rows, :] + dh * sc1
            if final:
                dx_ref[rows, :] = dx
                return carry[0] + _fold8(dh * x_ref[rows, :]), carry[1] + _fold8(dh)
            xhat = x_ref[rows, :]
            drp = _ln_bwd(dx, xhat, rstd_ref[rows, :], g_prev)
            drp_ref[rows, :] = drp
            dbr_ref[rows, :] = (bscale * drp).astype(BF16)
            return (carry[0] + _fold8(dh * xhat), carry[1] + _fold8(dh),
                    carry[2] + _fold8(br_ref[rows, :].astype(F32) * drp),
                    carry[3] + _fold8(dx * xhat), carry[4] + _fold8(dx))

        zero = jnp.zeros((8, d), F32)
        sums = list(_row_chunk_loop(tm, chunk, (zero,) * (2 if final else 5)))
        if not final:
            sums[0] = sums[0] * g_prev + sums[1] * b_prev
            sums[2] = sums[2] * branch_scale

        @pl.when((i % tpb) == 0)
        def _():
            dsc_ref[...] = jnp.zeros_like(dsc_ref)
            dsh_ref[...] = jnp.zeros_like(dsh_ref)
            if not final:
                dgate_ref[...] = jnp.zeros_like(dgate_ref)

        dsc_ref[0] += _row_sum(sums[0])
        dsh_ref[0] += _row_sum(sums[1])
        if not final:
            @pl.when(i == 0)
            def _():
                dg_ref[...] = jnp.zeros_like(dg_ref)
                db_ref[...] = jnp.zeros_like(db_ref)

            dgate_ref[0] += _row_sum(sums[2])
            dg_ref[...] += _row_sum(sums[3])
            db_ref[...] += _row_sum(sums[4])

    row = lambda w_: pl.BlockSpec((tm, w_), lambda i: (i, 0))
    vec = pl.BlockSpec((1, 1, d), lambda i: (i // tpb, 0, 0))
    mod_spec = pl.BlockSpec((1, N_MOD, d), lambda i: (i // tpb, 0, 0))
    vshape = jax.ShapeDtypeStruct((nb, 1, d), F32)
    if final:
        in_specs = [row(kdim), _resident(w.shape), row(d), row(d), mod_spec]
        args = (a, w, dr, xin, mod)
        out_specs = [row(d), vec, vec]
        out_shape = [jax.ShapeDtypeStruct((t, d), F32), vshape, vshape]
    else:
        in_specs = [row(kdim), _resident(w.shape), row(d), row(d), row(1), _full((2, d)), mod_spec, row(d)]
        args = (a, w, dr, xin, rstd_prev, lnp_prev, mod, branch_prev)
        out_specs = [row(d), row(d), vec, vec, vec, _full((1, d)), _full((1, d))]
        out_shape = [jax.ShapeDtypeStruct((t, d), F32), jax.ShapeDtypeStruct((t, d), BF16), vshape, vshape, vshape,
                     jax.ShapeDtypeStruct((1, d), F32), jax.ShapeDtypeStruct((1, d), F32)]
    return _pcall(
        body, name=name, grid=(t // tm,), in_specs=in_specs, out_specs=out_specs, out_shape=out_shape,
        scratch_shapes=[pltpu.VMEM((tm, d), F32)], args=args, comm=comm)


def _grad_chip_sum(pos, a, b, *, half_on_rows, name, comm=None):
    t, m = a.shape
    n = b.shape[1]
    tk = min(2048, t)
    nk = t // tk
    half = lambda p, pos_ref: 1 - pos_ref[2] - p + 2 * p * pos_ref[2]
    if half_on_rows:
        n_j = N_CHIPS
        tile = (m // 2, n // n_j)
        a_spec = pl.BlockSpec((tk, tile[0]), lambda p, j, k, pos_ref: (k, half(p, pos_ref)))
        b_spec = pl.BlockSpec((tk, tile[1]), lambda p, j, k, pos_ref: (k, j))
        out_tile = pl.BlockSpec((1, *tile), lambda p, j, k, pos_ref: (0, 0, j * p))
        total = (1, m // 2, n)
    else:
        n_j = 2
        tile = (m // n_j, n // 2)
        a_spec = pl.BlockSpec((tk, tile[0]), lambda p, j, k, pos_ref: (k, j))
        b_spec = pl.BlockSpec((tk, tile[1]), lambda p, j, k, pos_ref: (k, half(p, pos_ref)))
        out_tile = pl.BlockSpec((1, *tile), lambda p, j, k, pos_ref: (0, j * p, 0))
        total = (1, m, n // 2)

    def body(pos_ref, a_ref, b_ref, s32_ref, s16_ref, land_ref, acc, theirs, send_sems, recv_sems, copy_sem):
        p, j, k = pl.program_id(0), pl.program_id(1), pl.program_id(2)
        x, y, c = _position()

        def push(jj):
            return pltpu.make_async_remote_copy(
                src_ref=acc.at[jj], dst_ref=land_ref.at[jj], send_sem=send_sems.at[jj], recv_sem=recv_sems.at[jj],
                device_id=(x, y, 1 - c), device_id_type=MESH)

        fetch = pltpu.make_async_copy(land_ref.at[j], theirs, copy_sem)

        @pl.when(jnp.logical_and(p == 1, k == 0))
        def _():
            push(j).wait_send()
            push(j).wait_recv()
            fetch.start()

        part = _dot_tn(a_ref[...], b_ref[...])

        @pl.when(k == 0)
        def _():
            acc[j] = part

        @pl.when(k > 0)
        def _():
            acc[j] += part

        @pl.when(jnp.logical_and(p == 0, k == nk - 1))
        def _():
            push(j).start()

        @pl.when(jnp.logical_and(p == 1, k == nk - 1))
        def _():
            fetch.wait()
            s = acc[j] + theirs[...]
            s32_ref[0] = s
            s16_ref[0] = s.astype(BF16)

    out = _pcall(
        body, name=name, grid=(2, n_j, nk), in_specs=[a_spec, b_spec], out_specs=[out_tile, out_tile, ANY_SPEC],
        out_shape=[jax.ShapeDtypeStruct(total, F32), jax.ShapeDtypeStruct(total, BF16),
                   jax.ShapeDtypeStruct((n_j, *tile), F32)],
        scratch_shapes=[pltpu.VMEM((n_j, *tile), F32), pltpu.VMEM(tile, F32),
                        pltpu.SemaphoreType.DMA((n_j,)), pltpu.SemaphoreType.DMA((n_j,)), pltpu.SemaphoreType.DMA],
        args=(a, b), prefetch=pos, comm=comm)
    if comm is None:
        return out[0], out[1]
    (s32, s16, _), extra = out
    return (s32, s16), extra


def _matmul_nt_bf16(a, w, *, seq, name):
    t, kdim = a.shape
    n = w.shape[0]
    tm = min(512, seq)

    def body(a_ref, w_ref, o_ref):
        av = a_ref[...]
        for j in range(n // COL_CHUNK):
            o_ref[:, j * COL_CHUNK:(j + 1) * COL_CHUNK] = _dot_nt(
                av, w_ref[j * COL_CHUNK:(j + 1) * COL_CHUNK, :]).astype(BF16)

    return pl.pallas_call(
        body, name=name, grid=(t // tm,),
        in_specs=[pl.BlockSpec((tm, kdim), lambda i: (i, 0)), _resident((n, kdim))],
        out_specs=pl.BlockSpec((tm, n), lambda i: (i, 0)),
        out_shape=jax.ShapeDtypeStruct((t, n), BF16),
        compiler_params=_params(("arbitrary",)),
    )(a, w)


def _attention_bwd(q, k, v, dmixin, sinks, *, seq, name, comm=None):
    t = q.shape[0]
    nblk = seq // BLOCK
    tile = ATTN_TILE_BLOCKS * BLOCK

    def body(q_ref, kp_ref, kc_ref, vp_ref, vc_ref, do_ref, sink_ref,
             dq_ref, dkp_ref, dkc_ref, dvp_ref, dvc_ref, dsink_ref):
        n = pl.program_id(0)

        @pl.when(n == 0)
        def _():
            dsink_ref[...] = jnp.zeros_like(dsink_ref)

        srow = lax.broadcasted_iota(jnp.int32, (8, LANE), 0)
        dsink = jnp.zeros((8, LANE), F32)
        for s in range(ATTN_TILE_BLOCKS):
            rows, k_prev, k_cur, v_prev, v_cur, first = _attn_sub_block(s, n, nblk, kp_ref, kc_ref, vp_ref, vc_ref)
            dqs, dks, dvs = [], [], []
            for g in range(N_KV_HEADS):
                qs, kk, vv, pn, psn = _attn_group(q_ref, rows, k_prev, k_cur, v_prev, v_cur, sink_ref, g, first)
                dos = jnp.concatenate(
                    [do_ref[rows, (GQA_GROUP * g + j) * HEAD_DIM:(GQA_GROUP * g + j + 1) * HEAD_DIM]
                     for j in range(GQA_GROUP)], axis=0)
                dp = _dot_nt(vv, dos)
                delta = jnp.sum(pn * dp, axis=0, keepdims=True)
                ds = pn * (dp - delta)
                dsk = psn * delta
                for j in range(GQA_GROUP):
                    tot = jnp.sum(dsk[:, j * BLOCK:(j + 1) * BLOCK], axis=1, keepdims=True)
                    dsink = dsink - jnp.where(srow == GQA_GROUP * g + j, tot, 0.0)
                dsb = (ds * (HEAD_DIM ** -0.5)).astype(BF16)
                dqs.append(_heads_to_lanes(_dot_tn(kk, dsb)))
                dks.append(jnp.dot(dsb, qs, preferred_element_type=F32))
                dvs.append(jnp.dot(pn.astype(BF16), dos, preferred_element_type=F32))
            dq_ref[rows, :] = jnp.concatenate(dqs, axis=1)
            dkp_ref[rows, :] = jnp.concatenate([x[0:BLOCK, :] for x in dks], axis=1)
            dkc_ref[rows, :] = jnp.concatenate([x[BLOCK:, :] for x in dks], axis=1)
            dvp_ref[rows, :] = jnp.concatenate([x[0:BLOCK, :] for x in dvs], axis=1)
            dvc_ref[rows, :] = jnp.concatenate([x[BLOCK:, :] for x in dvs], axis=1)
        dsink_ref[...] += dsink

    cur = lambda w: pl.BlockSpec((tile, w), lambda n: (n, 0))
    prev = lambda w: pl.BlockSpec((BLOCK, w), lambda n: (jnp.maximum(n * ATTN_TILE_BLOCKS - 1, 0), 0))
    kv = jax.ShapeDtypeStruct((t, KV_WIDTH), F32)
    return _pcall(
        body, name=name, grid=(t // tile,),
        in_specs=[cur(ATTN_WIDTH), prev(KV_WIDTH), cur(KV_WIDTH), prev(KV_WIDTH), cur(KV_WIDTH), cur(ATTN_WIDTH),
                  pl.BlockSpec(memory_space=pltpu.SMEM)],
        out_specs=[cur(ATTN_WIDTH), cur(KV_WIDTH), cur(KV_WIDTH), cur(KV_WIDTH), cur(KV_WIDTH), _full((8, LANE))],
        out_shape=[jax.ShapeDtypeStruct((t, ATTN_WIDTH), F32), kv, kv, kv, kv, jax.ShapeDtypeStruct((8, LANE), F32)],
        args=(q, k, k, v, v, dmixin, sinks), comm=comm)


def _mix_bwd_assemble(dq, dkp, dkc, dvp, dvc, cos, sa, sb, dmixin, ubc, cw, *, seq, name, comm=None):
    t = dq.shape[0]
    cwid = CONV_WIDTH
    tm = min(2 * BLOCK, seq)
    tiles_per_seq = seq // tm
    ntile = t // tm
    nblk_all = t // BLOCK
    per_tile = tm // BLOCK

    def body(*refs):
        dq_ref, dkc_ref, dvc_ref = refs[0:3]
        dkp_refs, dvp_refs = refs[3:3 + per_tile], refs[3 + per_tile:3 + 2 * per_tile]
        (cos_ref, sa_ref, sb_ref, dco_ref, dcon_ref, ubc_ref, hprev_ref, hnext_ref, cw_ref,
         dproj_ref, dcw_ref, zbuf, dybuf) = refs[3 + 2 * per_tile:]
        i = pl.program_id(0)
        first = (i % tiles_per_seq) == 0
        last = (i % tiles_per_seq) == tiles_per_seq - 1
        glast = i == ntile - 1

        @pl.when(i == 0)
        def _():
            dcw_ref[...] = jnp.zeros_like(dcw_ref)

        def with_next_block(cur_ref, nxt_refs):
            nxt = [r[...] for r in nxt_refs]
            nxt[-1] = jnp.where(glast, 0.0, nxt[-1])
            return cur_ref[...] + jnp.concatenate(nxt, axis=0)

        cos_t, sa_t, sb_t = cos_ref[...], sa_ref[...], sb_ref[...]
        for j in range(ATTN_WIDTH // LANE):
            dproj_ref[:, j * LANE:(j + 1) * LANE] = _rope_t(
                dq_ref[:, j * LANE:(j + 1) * LANE], cos_t, sa_t, sb_t).astype(BF16)
        dk = with_next_block(dkc_ref, dkp_refs)
        dproj_ref[:, ATTN_WIDTH:ATTN_WIDTH + KV_WIDTH] = _rope_t(dk, cos_t, sa_t, sb_t).astype(BF16)
        dv = with_next_block(dvc_ref, dvp_refs)
        dproj_ref[:, ATTN_WIDTH + KV_WIDTH:ATTN_WIDTH + 2 * KV_WIDTH] = dv.astype(BF16)

        u, bg, cg = (ubc_ref[:, s * cwid:(s + 1) * cwid].astype(F32) for s in range(3))
        z = cg * u
        hz = hprev_ref[:, 2 * cwid:3 * cwid].astype(F32) * hprev_ref[:, 0:cwid].astype(F32)
        zbuf[0:HALO, :] = jnp.where(first, 0.0, hz)
        zbuf[HALO:HALO + tm, :] = z
        z2, z1 = zbuf[HALO - 2:HALO - 2 + tm, :], zbuf[HALO - 1:HALO - 1 + tm, :]
        w0, w1, w2 = cw_ref[0:1, :], cw_ref[1:2, :], cw_ref[2:3, :]
        y = w0 * z2 + w1 * z1 + w2 * z
        dco = dco_ref[...].astype(F32)
        dyc = dco * bg
        dyn = dcon_ref[...].astype(F32) * hnext_ref[:, cwid:2 * cwid].astype(F32)
        dybuf[0:tm, :] = dyc
        dybuf[tm:tm + HALO, :] = jnp.where(last, 0.0, dyn)
        dz = w2 * dyc + w1 * dybuf[1:1 + tm, :] + w0 * dybuf[2:2 + tm, :]
        srow = lax.broadcasted_iota(jnp.int32, (8, cwid), 0)
        dcw_ref[...] += (jnp.where(srow == 0, _row_sum(dyc * z2), 0.0) + jnp.where(srow == 1, _row_sum(dyc * z1), 0.0)
                         + jnp.where(srow == 2, _row_sum(dyc * z), 0.0))
        base = ATTN_WIDTH + 2 * KV_WIDTH
        dproj_ref[:, base:base + cwid] = (dz * cg).astype(BF16)
        dproj_ref[:, base + cwid:base + 2 * cwid] = (dco * y).astype(BF16)
        dproj_ref[:, base + 2 * cwid:base + 3 * cwid] = (dz * u).astype(BF16)

    cur = lambda w: pl.BlockSpec((tm, w), lambda i: (i, 0))
    nxt = [pl.BlockSpec((BLOCK, KV_WIDTH), lambda i, s=s: (jnp.minimum(i * per_tile + s + 1, nblk_all - 1), 0))
           for s in range(per_tile)]
    prev_halo = pl.BlockSpec((HALO, 3 * cwid), lambda i: (jnp.maximum(i * (tm // HALO) - 1, 0), 0))
    next_halo = lambda w, col: pl.BlockSpec(
        (HALO, w), lambda i: (jnp.minimum((i + 1) * (tm // HALO), t // HALO - 1), col))
    return _pcall(
        body, name=name, grid=(ntile,),
        in_specs=[cur(ATTN_WIDTH), cur(KV_WIDTH), cur(KV_WIDTH), *nxt, *nxt,
                  cur(LANE), cur(LANE), cur(LANE),
                  pl.BlockSpec((tm, cwid), lambda i: (i, 1)), next_halo(cwid, 1),
                  cur(3 * cwid), prev_halo, next_halo(3 * cwid, 0), _full((8, cwid))],
        out_specs=[cur(IN_WIDTH), _full((8, cwid))],
        out_shape=[jax.ShapeDtypeStruct((t, IN_WIDTH), BF16), jax.ShapeDtypeStruct((8, cwid), F32)],
        scratch_shapes=[pltpu.VMEM((tm + HALO, cwid), F32), pltpu.VMEM((tm + HALO, cwid), F32)],
        args=(dq, dkc, dvc, *([dkp] * per_tile), *([dvp] * per_tile), cos, sa, sb, dmixin, dmixin,
              ubc, ubc, ubc, cw), comm=comm)


def _ada_fwd(c_all, w_ada, b_ada_shard, chip, casts, *, name, comm=None):
    nb, d = c_all.shape
    n = w_ada.shape[1]
    steps = 2
    tn = n // steps
    n_cast = len(casts)

    def body(chip_ref, c_ref, w_ref, b_ref, *refs):
        cast_in, o_ref, cast_out = refs[:n_cast], refs[n_cast], refs[n_cast + 1:]
        cv = c_ref[...]
        cond = cv * _sigmoid(cv)
        o_ref[...] = jnp.dot(cond, w_ref[...], preferred_element_type=F32,
                             precision=lax.Precision.HIGHEST) + b_ref[...]
        for src, dst in zip(cast_in, cast_out):
            dst[...] = src[...].astype(BF16)

    in_specs = [_full((nb, d)), pl.BlockSpec((d, tn), lambda j, chip_ref: (0, j)),
                pl.BlockSpec((1, tn), lambda j, chip_ref: (0, j))]
    out_specs = [pl.BlockSpec((nb, tn), lambda j, chip_ref: (0, j))]
    out_shape = [jax.ShapeDtypeStruct((nb, n), F32)]
    for w, col_kind in casts:
        r, c = w.shape
        tr = r // steps
        in_specs.append(pl.BlockSpec((tr, c), lambda j, chip_ref: (j, 0)))
        if col_kind:
            out_specs.append(pl.BlockSpec((tr, c), lambda j, chip_ref: (j, chip_ref[0])))
            out_shape.append(jax.ShapeDtypeStruct((r, c * N_CHIPS), BF16))
        else:
            out_specs.append(pl.BlockSpec((tr, c), lambda j, chip_ref: (chip_ref[0] * steps + j, 0)))
            out_shape.append(jax.ShapeDtypeStruct((r * N_CHIPS, c), BF16))
    out = _pcall(body, name=name, grid=(steps,), in_specs=in_specs, out_specs=out_specs, out_shape=out_shape,
                 args=(c_all, w_ada, b_ada_shard, *[w for w, _ in casts]), prefetch=chip, comm=comm)
    res, extra = out if comm is not None else (out, None)
    return res[0], list(res[1:]), extra


def _small_finish(gathered, dmod_all, dmod_shard, c_all_t, *, name):
    d = D_MODEL
    nb, n = dmod_shard.shape

    def body(g_ref, dm_ref, dms_ref, ct_ref, sum_ref, gw_ref, gb_ref):
        total = g_ref[0]
        for dev in range(1, N_DEV):
            total = total + g_ref[dev]
        sum_ref[...] = total
        gb_ref[...] = _row_sum(dm_ref[...])
        ctv = ct_ref[...]
        cond_t = (ctv * _sigmoid(ctv)).astype(BF16)
        for jb in range(n // COL_CHUNK):
            gw_ref[:, jb * COL_CHUNK:(jb + 1) * COL_CHUNK] = jnp.dot(
                cond_t, dms_ref[:, jb * COL_CHUNK:(jb + 1) * COL_CHUNK].astype(BF16), preferred_element_type=F32)

    return pl.pallas_call(
        body, name=name, grid=(1,),
        in_specs=[_full((N_DEV, SMALL_ROWS, d)), _full((nb, N_MOD * d)), _full((nb, n)), _full((d, nb))],
        out_specs=[_full((SMALL_ROWS, d)), _full((d, n)), _full((1, N_MOD * d))],
        out_shape=[jax.ShapeDtypeStruct((SMALL_ROWS, d), F32), jax.ShapeDtypeStruct((d, n), F32),
                   jax.ShapeDtypeStruct((1, N_MOD * d), F32)],
        compiler_params=_params(("arbitrary",)),
    )(gathered, dmod_all, dmod_shard, c_all_t)


def _row_tile(r, c, budget=1 << 21):
    if r * c * 4 <= budget or r % 16:
        return r
    best = 16
    for tr in range(16, r + 1, 16):
        if r % tr == 0 and tr * c * 4 <= budget:
            best = tr
    return best


def _cast_into(w, chip, col_kind, *, name):
    r, c = w.shape
    tr = _row_tile(r, c)

    def body(chip_ref, w_ref, o_ref):
        o_ref[...] = w_ref[...].astype(BF16)

    if col_kind:
        out_spec = pl.BlockSpec((tr, c), lambda i, chip_ref: (i, chip_ref[0]))
        out_shape = jax.ShapeDtypeStruct((r, c * N_CHIPS), BF16)
    else:
        out_spec = pl.BlockSpec((tr, c), lambda i, chip_ref: (chip_ref[0] * (r // tr) + i, 0))
        out_shape = jax.ShapeDtypeStruct((r * N_CHIPS, c), BF16)
    return _pcall(body, name=name, grid=(r // tr,), in_specs=[pl.BlockSpec((tr, c), lambda i, chip_ref: (i, 0))],
                  out_specs=out_spec, out_shape=out_shape, args=(w,), prefetch=chip)


def _adamw(w, g, m, v, *, name, comm=None):
    r, c = w.shape
    tr = _row_tile(r, c)
    c1 = 1.0 - ADAM_B1 ** ADAM_STEP
    c2 = 1.0 - ADAM_B2 ** ADAM_STEP

    def body(w_ref, g_ref, m_ref, v_ref, d_ref, nm_ref, nv_ref):
        gv = g_ref[...]
        m2 = ADAM_B1 * m_ref[...] + (1.0 - ADAM_B1) * gv
        v2 = ADAM_B2 * v_ref[...] + (1.0 - ADAM_B2) * (gv * gv)
        d_ref[...] = -ADAM_LR * ((m2 / c1) / (jnp.sqrt(v2 / c2) + ADAM_EPS) + ADAM_WD * w_ref[...])
        nm_ref[...] = m2
        nv_ref[...] = v2

    spec = pl.BlockSpec((tr, c), lambda i: (i, 0))
    sh = jax.ShapeDtypeStruct((r, c), F32)
    return _pcall(body, name=name, grid=(r // tr,), in_specs=[spec] * 4, out_specs=[spec] * 3, out_shape=[sh] * 3,
                  args=(w, g, m, v), comm=comm)


def _sum_final(pos, s32, recv, *, col_kind, n_shard, name, comm=None):
    def body(pos_ref, s_ref, r_ref, o_ref):
        total = ((s_ref[0] + r_ref[0].astype(F32)) + r_ref[1].astype(F32)) + r_ref[2].astype(F32)
        if col_kind:
            o_ref[0] = total
        else:
            o_ref[...] = total

    if col_kind:
        rows, cols = s32.shape[1], n_shard
        tr = _row_tile(rows, cols)
        own = pl.BlockSpec((1, tr, cols), lambda i, pos: (0, i, 2 * pos[0] + pos[1]))
        out_spec = pl.BlockSpec((1, tr, cols), lambda i, pos: (pos[2], i, 0))
        out_shape = jax.ShapeDtypeStruct((2, rows, cols), F32)
    else:
        rows, cols = n_shard, s32.shape[2]
        tr = _row_tile(rows, cols)
        own = pl.BlockSpec((1, tr, cols), lambda i, pos: (0, (2 * pos[0] + pos[1]) * (rows // tr) + i, 0))
        out_spec = pl.BlockSpec((tr, cols), lambda i, pos: (i, pos[2]))
        out_shape = jax.ShapeDtypeStruct((rows, 2 * cols), F32)
    return _pcall(
        body, name=name, grid=(rows // tr,),
        in_specs=[own, pl.BlockSpec((3, tr, cols), lambda i, pos: (0, i, 0))], out_specs=out_spec,
        out_shape=out_shape, args=(s32, recv), prefetch=pos, comm=comm)


def _position():
    return lax.axis_index("x"), lax.axis_index("y"), lax.axis_index("c")


def _allgather8(x_shard, *, name, comm=None):
    m_per, n = x_shard.shape
    nci, nco = (0, 0) if comm is None else (len(comm.inputs), len(comm.out_shapes))

    def body(*refs):
        x_ref, refs = refs[0], refs[1:]
        cin, refs = refs[:nci], refs[nci:]
        out_ref, refs = refs[0], refs[1:]
        cout, refs = refs[:nco], refs[nco:]
        (send_sems, recv_sems, local_sem), csems = refs[:3], refs[3:]
        x, y, c = _position()
        me, sibling = (x, y, c), (x, y, 1 - c)
        chips = [(1 - x, y), (x, 1 - y), (1 - x, 1 - y)]

        def rows(px, py, pc):
            return out_ref.at[pl.ds((4 * px + 2 * py + pc) * m_per, m_per), :]

        def copy(k, block, to, src=None):
            return pltpu.make_async_remote_copy(
                src_ref=rows(*block) if src is None else src, dst_ref=rows(*block),
                send_sem=send_sems.at[k], recv_sem=recv_sems.at[k], device_id=to, device_id_type=MESH)

        mine = pltpu.make_async_copy(x_ref, rows(*me), local_sem)
        mine.start()
        first = [copy(0, me, sibling, src=x_ref)]
        first += [copy(1 + j, me, (*chip, c), src=x_ref) for j, chip in enumerate(chips)]
        for cp in first:
            cp.start()
        if comm is not None:
            comm.start(cin, cout, csems)
        passed = [copy(4 + j, (*chip, c), sibling) for j, chip in enumerate(chips)]
        for j, chip in enumerate(chips):
            copy(1 + j, (*chip, c), me).wait_recv()
            passed[j].start()
        copy(0, sibling, me).wait_recv()
        for j, chip in enumerate(chips):
            copy(4 + j, (*chip, 1 - c), me).wait_recv()
        for cp in first + passed:
            cp.wait_send()
        mine.wait()
        if comm is not None:
            comm.middle(cin, cout, csems)
            comm.late(cin, cout, csems)
            comm.finish(cin, cout, csems)

    vmem = pl.BlockSpec(memory_space=pltpu.VMEM)
    sems = [pltpu.SemaphoreType.DMA((7,)), pltpu.SemaphoreType.DMA((7,)), pltpu.SemaphoreType.DMA]
    out = jax.ShapeDtypeStruct((N_DEV * m_per, n), x_shard.dtype)
    if comm is None:
        return pl.pallas_call(body, name=name, out_shape=out, in_specs=[vmem], out_specs=vmem,
                              scratch_shapes=sems)(x_shard)
    res = pl.pallas_call(
        body, name=name, out_shape=[out] + list(comm.out_shapes), in_specs=[vmem] + [ANY_SPEC] * nci,
        out_specs=[vmem] + [ANY_SPEC] * nco, scratch_shapes=sems + list(comm.sems),
        input_output_aliases={1 + i: 1 + o for i, o in comm.aliases.items()})(x_shard, *comm.inputs)
    return res[0], list(res[1:])


def _peer_chips(x, y):
    return [(1 - x, y), (x, 1 - y), (1 - x, 1 - y)]


class _GatherJob:
    def __init__(self, pieces):
        self.pieces = pieces
        n_p = len(pieces)
        self.inputs = [p[0] for p in pieces]
        self.out_shapes = [jax.ShapeDtypeStruct(p[0].shape, p[0].dtype) for p in pieces]
        for buf, col_kind, r0, nr in pieces:
            half_rows = buf.shape[0] // (2 if col_kind else 2 * N_CHIPS)
            assert r0 % 16 == 0 and nr % 16 == 0 and nr >= 32 and r0 + nr <= half_rows, (buf.shape, r0, nr)
        self.aliases = {p: p for p in range(n_p)}
        dma = pltpu.SemaphoreType.DMA
        self.sems = [dma((2 * n_p,))] * 4 + [dma((4 * n_p,))] * 2

    def _region(self, cout, p, chip_idx, half, part=None):
        buf, col_kind, r0, nr = self.pieces[p]
        first = -(-nr // 32) * 16
        if part == 0:
            nr = first
        elif part == 1:
            r0, nr = r0 + first, nr - first
        if col_kind:
            n = buf.shape[1] // N_CHIPS
            return cout[p].at[pl.ds(half * (buf.shape[0] // 2) + r0, nr), pl.ds(chip_idx * n, n)]
        n = buf.shape[0] // N_CHIPS
        return cout[p].at[pl.ds(chip_idx * n + half * (n // 2) + r0, nr), :]

    def _copies(self, cout, sems):
        send1, recv1, send2, recv2, fsend, frecv = sems
        x, y, c = _position()
        k = 2 * x + y
        sibling = (x, y, 1 - c)
        x_nbr, y_nbr, diag = _peer_chips(x, y)
        chip_of = lambda ch: 2 * ch[0] + ch[1]

        def remote(region, ssem, rsem, to):
            return pltpu.make_async_remote_copy(src_ref=region, dst_ref=region, send_sem=ssem, recv_sem=rsem,
                                                device_id=to, device_id_type=MESH)

        hop1, arrived1, hop2, arrived2, fwds, fwd_arrived = [], [], [], [], [], []
        for p in range(len(self.pieces)):
            for j, nbr in enumerate((x_nbr, y_nbr)):
                i1 = 2 * p + j
                hop1.append(remote(self._region(cout, p, k, c), send1.at[i1], recv1.at[i1], (*nbr, c)))
                arrived1.append(remote(self._region(cout, p, chip_of(nbr), c), send1.at[i1], recv1.at[i1], (*nbr, c)))
            hop2.append(remote(self._region(cout, p, chip_of(x_nbr), c, 0), send2.at[2 * p], recv2.at[2 * p],
                               (*y_nbr, c)))
            hop2.append(remote(self._region(cout, p, chip_of(y_nbr), c, 1), send2.at[2 * p + 1], recv2.at[2 * p + 1],
                               (*x_nbr, c)))
            arrived2.append(remote(self._region(cout, p, chip_of(diag), c, 0), send2.at[2 * p], recv2.at[2 * p],
                                   (*y_nbr, c)))
            arrived2.append(remote(self._region(cout, p, chip_of(diag), c, 1), send2.at[2 * p + 1],
                                   recv2.at[2 * p + 1], (*x_nbr, c)))
            landed = [(chip_of(x_nbr), None), (chip_of(y_nbr), None), (chip_of(diag), 0), (chip_of(diag), 1)]
            for q, (chip_idx, part) in enumerate(landed):
                i3 = 4 * p + q
                fwds.append(remote(self._region(cout, p, chip_idx, c, part), fsend.at[i3], frecv.at[i3], sibling))
                fwd_arrived.append(remote(self._region(cout, p, chip_idx, 1 - c, part), fsend.at[i3], frecv.at[i3],
                                          sibling))
        return hop1, arrived1, hop2, arrived2, fwds, fwd_arrived

    def start(self, cin, cout, sems):
        for cp in self._copies(cout, sems)[0]:
            cp.start()

    def middle(self, cin, cout, sems):
        _, arrived1, hop2, _, fwds, _ = self._copies(cout, sems)
        for p in range(len(self.pieces)):
            for j in range(2):
                arrived1[2 * p + j].wait_recv()
                hop2[2 * p + j].start()
                fwds[4 * p + j].start()

    def late(self, cin, cout, sems):
        _, _, _, arrived2, fwds, _ = self._copies(cout, sems)
        for p in range(len(self.pieces)):
            for j in range(2):
                arrived2[2 * p + j].wait_recv()
                fwds[4 * p + 2 + j].start()

    def finish(self, cin, cout, sems):
        hop1, _, hop2, _, fwds, fwd_arrived = self._copies(cout, sems)
        for cp in fwd_arrived:
            cp.wait_recv()
        for cp in hop1 + hop2 + fwds:
            cp.wait_send()


class _PairedJob:
    aliases = {}

    def start(self, cin, cout, sems):
        for cp in self._copies(cin, cout, sems):
            cp.start()

    def middle(self, cin, cout, sems):
        pass

    late = middle

    def finish(self, cin, cout, sems):
        copies = self._copies(cin, cout, sems)
        for cp in copies:
            cp.wait_recv()
        for cp in copies:
            cp.wait_send()


class _ExchangeJob(_PairedJob):
    def __init__(self, s16, kinds, sizes):
        self.inputs, self.kinds, self.sizes = list(s16), list(kinds), list(sizes)
        self.out_shapes = [jax.ShapeDtypeStruct((3, s.shape[1], n) if kd else (3, n, s.shape[2]), s.dtype)
                           for s, kd, n in zip(s16, kinds, sizes)]
        self.sems = [pltpu.SemaphoreType.DMA((3 * len(s16),)), pltpu.SemaphoreType.DMA((3 * len(s16),))]

    def _copies(self, cin, cout, sems):
        send_sems, recv_sems = sems
        x, y, c = _position()
        copies = []
        for p, src_ref in enumerate(cin):
            for j, chip in enumerate(_peer_chips(x, y)):
                kk = 2 * chip[0] + chip[1]
                n = self.sizes[p]
                src = src_ref.at[0, :, pl.ds(kk * n, n)] if self.kinds[p] else src_ref.at[0, pl.ds(kk * n, n), :]
                copies.append(pltpu.make_async_remote_copy(
                    src_ref=src, dst_ref=cout[p].at[j], send_sem=send_sems.at[3 * p + j],
                    recv_sem=recv_sems.at[3 * p + j], device_id=(*chip, c), device_id_type=MESH))
        return copies


class _ShareJob:
    def __init__(self, halves):
        self.inputs = list(halves)
        self.out_shapes = [jax.ShapeDtypeStruct(h.shape, h.dtype) for h in halves]
        self.aliases = {p: p for p in range(len(halves))}
        self.sems = [pltpu.SemaphoreType.DMA((len(halves),)), pltpu.SemaphoreType.DMA((len(halves),))]

    def _copies(self, cout, sems, half):
        send_sems, recv_sems = sems
        x, y, c = _position()
        h = c if half == "mine" else 1 - c

        def region(o):
            if len(o.shape) == 3:
                return o.at[h]
            hc = o.shape[1] // 2
            return o.at[:, pl.ds(h * hc, hc)]

        return [pltpu.make_async_remote_copy(
            src_ref=region(o), dst_ref=region(o), send_sem=send_sems.at[p], recv_sem=recv_sems.at[p],
            device_id=(x, y, 1 - c), device_id_type=MESH) for p, o in enumerate(cout)]

    def start(self, cin, cout, sems):
        for cp in self._copies(cout, sems, "mine"):
            cp.start()

    def middle(self, cin, cout, sems):
        pass

    late = middle

    def finish(self, cin, cout, sems):
        for cp in self._copies(cout, sems, "theirs"):
            cp.wait_recv()
        for cp in self._copies(cout, sems, "mine"):
            cp.wait_send()


class _MultiJob:
    def __init__(self, jobs):
        self.jobs = jobs
        self.inputs = [a for j in jobs for a in j.inputs]
        self.out_shapes = [s for j in jobs for s in j.out_shapes]
        self.sems = [s for j in jobs for s in j.sems]
        self.aliases = {}
        i0 = o0 = 0
        for j in jobs:
            for i, o in j.aliases.items():
                self.aliases[i0 + i] = o0 + o
            i0 += len(j.inputs)
            o0 += len(j.out_shapes)

    def _parts(self, cin, cout, sems):
        i0 = o0 = s0 = 0
        for j in self.jobs:
            ni, no, ns = len(j.inputs), len(j.out_shapes), len(j.sems)
            yield j, cin[i0:i0 + ni], cout[o0:o0 + no], sems[s0:s0 + ns]
            i0, o0, s0 = i0 + ni, o0 + no, s0 + ns

    def start(self, cin, cout, sems):
        for j, a, b, s in self._parts(cin, cout, sems):
            j.start(a, b, s)

    def middle(self, cin, cout, sems):
        for j, a, b, s in self._parts(cin, cout, sems):
            j.middle(a, b, s)

    def late(self, cin, cout, sems):
        for j, a, b, s in self._parts(cin, cout, sems):
            j.late(a, b, s)

    def finish(self, cin, cout, sems):
        for j, a, b, s in self._parts(cin, cout, sems):
            j.finish(a, b, s)


def _rope_tables(positions):
    half = ROT_DIM // 2
    inv_freq = jnp.power(jnp.float32(ROPE_THETA), -jnp.arange(0, ROT_DIM, 2, dtype=F32) / ROT_DIM)
    inv_head = jnp.concatenate([inv_freq, inv_freq, jnp.zeros((HEAD_DIM - ROT_DIM,), F32)])
    inv_lane = jnp.concatenate([inv_head] * (LANE // HEAD_DIM))
    ang = positions.astype(F32).reshape(-1)[:, None] * inv_lane[None, :]
    sin = jnp.sin(ang)
    dim = jnp.arange(LANE) % HEAD_DIM
    return jnp.cos(ang), jnp.where(dim < half, -sin, 0.0), jnp.where(dim >= half, sin, 0.0)


def kernel(x, c, positions, w_ada, b_ada, ffn1_w_gate_up, ffn1_w_down, ln1_g, ln1_b, w_in, conv_w, attn_sinks, w_out, ln2_g, ln2_b, ffn2_w_gate_up, ffn2_w_down, ln3_g, ln3_b, loss_target, m_w_ada, m_b_ada, m_ffn1_w_gate_up, m_ffn1_w_down, m_ln1_g, m_ln1_b, m_w_in, m_conv_w, m_attn_sinks, m_w_out, m_ln2_g, m_ln2_b, m_ffn2_w_gate_up, m_ffn2_w_down, m_ln3_g, m_ln3_b, v_w_ada, v_b_ada, v_ffn1_w_gate_up, v_ffn1_w_down, v_ln1_g, v_ln1_b, v_w_in, v_conv_w, v_attn_sinks, v_w_out, v_ln2_g, v_ln2_b, v_ffn2_w_gate_up, v_ffn2_w_down, v_ln3_g, v_ln3_b):
    d = D_MODEL
    nb, seq, _ = x.shape
    t = nb * seq
    f = ffn1_w_down.shape[1] * N_CHIPS
    ax, ay, ac = _position()
    chip = 2 * ax + ay
    dev = 2 * chip + ac
    pos = jnp.stack([ax, ay, ac]).astype(jnp.int32)

    x2 = x.reshape(t, d)
    tgt2 = loss_target.reshape(t, d)
    ln1 = jnp.concatenate([ln1_g, ln1_b], axis=0)
    ln2 = jnp.concatenate([ln2_g, ln2_b], axis=0)
    ln3 = jnp.concatenate([ln3_g, ln3_b], axis=0)
    sinks = attn_sinks.reshape(N_Q_HEADS)
    cos_t, sa_t, sb_t = _rope_tables(positions)

    gu_cuts = [0, 176, 352, d // 2]
    gu_part = lambda buf, s: (buf, True, gu_cuts[s], gu_cuts[s + 1] - gu_cuts[s])
    chip_arr = jnp.reshape(chip, (1,)).astype(jnp.int32)
    b_gu1 = _cast_into(ffn1_w_gate_up[0], chip_arr, True, name="cast_gu1")

    n_ada = w_ada.shape[2]
    c_all, (b_gu1,) = _allgather8(c.reshape(nb * d // LANE, LANE), name="gather_c", comm=_GatherJob([gu_part(b_gu1, 0)]))
    c_all = c_all.reshape(N_DEV * nb, d)
    b_shard = lax.dynamic_slice(b_ada, (0, chip * n_ada), (1, n_ada))
    later_shards = [(ffn1_w_down[0], False), (w_in[0].T, False), (w_out[0], False), (ffn2_w_gate_up[0], True),
                    (ffn2_w_down[0], False)]
    mod_part, (b_d1, b_in, b_out, b_gu2, b_d2), (b_gu1,) = _ada_fwd(
        c_all, w_ada[0], b_shard, chip_arr, later_shards, name="ada_fwd", comm=_GatherJob([gu_part(b_gu1, 1)]))
    conv_rows = jnp.pad(conv_w[0], ((0, 5), (0, n_ada - conv_w.shape[2])))
    part = jnp.concatenate([mod_part, conv_rows], axis=0)
    parts, (wgu1,) = _allgather8(part, name="gather_mod", comm=_GatherJob([gu_part(b_gu1, 2)]))
    parts = parts.reshape(N_DEV, N_DEV * nb + 8, n_ada)
    mod_all = jnp.concatenate([parts[2 * k, :N_DEV * nb, :] for k in range(N_CHIPS)], axis=1)
    mod = lax.dynamic_slice(mod_all, (dev * nb, 0), (nb, N_MOD * d)).reshape(nb, N_MOD, d)
    cw_full = jnp.concatenate([parts[2 * k, N_DEV * nb:, :conv_w.shape[2]] for k in range(N_CHIPS)], axis=1)

    n_gu, n_d, n_in, n_out = (ffn1_w_gate_up.shape[2], ffn1_w_down.shape[1], w_in.shape[2], w_out.shape[1])

    def whole(buf, col_kind):
        return (buf, col_kind, 0, buf.shape[0] // (2 if col_kind else 2 * N_CHIPS))

    (h1, a1, dact1), (wd1, wout) = _ffn_up(x2, ln1, mod, wgu1, seq=seq, sc_idx=1, sh_idx=0, use_ln=False,
                                         name="ffn1_up", comm=_GatherJob([whole(b_d1, False), whole(b_out, False)]))
    (f1, xhat1, rstd1), (win_t,) = _ffn_down_ln(a1, wd1, x2, ln1, mod, seq=seq, gate_idx=2, use_ln=False,
                                                name="ffn1_down", comm=_GatherJob([whole(b_in, False)]))
    (h2, q, k, v, ubc), (b_gu2,) = _in_proj(
        xhat1, ln1, mod, win_t, cos_t, sa_t, sb_t, seq=seq, sc_idx=4, sh_idx=3, name="in_proj",
        comm=_GatherJob([gu_part(b_gu2, 0)]))
    attn, (b_gu2,) = _attention(q, k, v, sinks, seq=seq, name="attention", comm=_GatherJob([gu_part(b_gu2, 1)]))
    (mixin, mix, xhat2, rstd2), (wgu2,) = _out_proj(
        attn, ubc, cw_full, wout, xhat1, ln1, mod, seq=seq, gate_idx=5, name="out_proj",
        comm=_GatherJob([gu_part(b_gu2, 2)]))
    (h3, a3, dact3), (wd2,) = _ffn_up(xhat2, ln2, mod, wgu2, seq=seq, sc_idx=7, sh_idx=6, use_ln=True, name="ffn2_up",
                                    comm=_GatherJob([whole(b_d2, False)]))
    dr3, df3, loss_cols, dln3g, dln3b, dgate3 = _ffn_down_loss(
        a3, wd2, xhat2, ln2, mod, ln3, tgt2, seq=seq, gate_idx=8, name="ffn2_down_loss")

    dgu3 = _ffn_bwd_act(df3, wd2, dact3, seq=seq, name="ffn2_bwd_act")
    s32_d2, s16_d2 = _grad_chip_sum(pos, a3, df3, half_on_rows=False, name="grad_wd2")
    (s32_gu2, s16_gu2), (recv_d2,) = _grad_chip_sum(pos, h3, dgu3, half_on_rows=True, name="grad_wgu2",
                                                    comm=_ExchangeJob([s16_d2], [False], [n_d]))
    (dr2, dmix, dsc3, dsh3, dgate2, dln2g, dln2b), (recv_gu2,) = _bwd_in(
        dgu3, wgu2, dr3, xhat2, rstd2, ln2, mod, mix, seq=seq, w_is_nt=True, sc_idx=7, gate_idx=5,
        branch_scale=1.0, final=False, name="ffn2_bwd_in", comm=_ExchangeJob([s16_gu2], [True], [n_gu]))
    s32_out, s16_out = _grad_chip_sum(pos, mixin, dmix, half_on_rows=False, name="grad_wout")
    dmixin = _matmul_nt_bf16(dmix, wout, seq=seq, name="out_proj_bwd")
    (dq, dkp, dkc, dvp, dvc, dsink), (recv_out,) = _attention_bwd(
        q, k, v, dmixin, sinks, seq=seq, name="attention_bwd", comm=_ExchangeJob([s16_out], [False], [n_out]))
    dproj, dcw = _mix_bwd_assemble(
        dq, dkp, dkc, dvp, dvc, cos_t, sa_t, sb_t, dmixin, ubc, cw_full, seq=seq, name="mix_bwd")
    s32_in, s16_in = _grad_chip_sum(pos, dproj, h2, half_on_rows=False, name="grad_win")
    (dr1, df1, dsc2, dsh2, dgate1, dln1g, dln1b), (recv_in,) = _bwd_in(
        dproj, win_t, dr2, xhat1, rstd1, ln1, mod, f1, seq=seq, w_is_nt=False, sc_idx=4, gate_idx=2,
        branch_scale=0.5, final=False, name="in_proj_bwd", comm=_ExchangeJob([s16_in], [False], [n_in]))
    s32_d1, s16_d1 = _grad_chip_sum(pos, a1, df1, half_on_rows=False, name="grad_wd1")
    dgu1, (recv_d1,) = _ffn_bwd_act(df1, wd1, dact1, seq=seq, name="ffn1_bwd_act",
                                    comm=_ExchangeJob([s16_d1], [False], [n_d]))

    def final_half(s32_, recv_, col_kind, n_shard, name_):
        return _sum_final(pos, s32_, recv_, col_kind=col_kind, n_shard=n_shard, name=name_)

    early = [final_half(s32_gu2, recv_gu2, True, n_gu, "sum_final_gu2"),
             final_half(s32_d2, recv_d2, False, n_d, "sum_final_d2"),
             final_half(s32_out, recv_out, False, n_out, "sum_final_out"),
             final_half(s32_in, recv_in, False, n_in, "sum_final_in"),
             final_half(s32_d1, recv_d1, False, n_d, "sum_final_d1")]
    (s32_gu1, s16_gu1), (full_gu2, full_d2, full_out, full_in, full_d1) = _grad_chip_sum(
        pos, h1, dgu1, half_on_rows=True, name="grad_wgu1", comm=_ShareJob(early))
    (grad_x, dsc1, dsh1), (recv_gu1,) = _bwd_in(
        dgu1, wgu1, dr1, x2, None, None, mod, None, seq=seq, w_is_nt=True, sc_idx=1, gate_idx=None,
        branch_scale=None, final=True, name="ffn1_bwd_in", comm=_ExchangeJob([s16_gu1], [True], [n_gu]))
    late = [final_half(s32_gu1, recv_gu1, True, n_gu, "sum_final_gu1")]

    dmod = jnp.concatenate([dsh1, dsc1, dgate1, dsh2, dsc2, dgate2, dsh3, dsc3, dgate3], axis=1)
    loss_row = jnp.sum(loss_cols, axis=1, keepdims=True) * (0.5 / d)
    lane_row = lambda a: jnp.pad(a, ((0, 0), (0, d - a.shape[1])))
    block = jnp.concatenate(
        [dmod.reshape(nb * N_MOD, d), dln1g, dln1b, dln2g, dln2b, dln3g, dln3b,
         lane_row(dcw[0:3, :]), lane_row(dsink[:, 0:1].reshape(1, N_Q_HEADS)), lane_row(loss_row)], axis=0)
    block = jnp.pad(block, ((0, SMALL_ROWS - block.shape[0]), (0, 0)))
    gathered, (full_gu1,) = _allgather8(block, name="gather_small", comm=_ShareJob(late))
    gathered = gathered.reshape(N_DEV, SMALL_ROWS, d)
    dmod_all = gathered[:, :nb * N_MOD, :].reshape(N_DEV * nb, N_MOD * d)
    dmod_shard = lax.dynamic_slice(dmod_all, (0, chip * n_ada), (N_DEV * nb, n_ada))
    small, g_w_ada, g_b_ada = _small_finish(gathered, dmod_all, dmod_shard, c_all.T, name="small_finish")
    r0 = nb * N_MOD
    loss = small[r0 + 10, 0]
    g_ln = [small[r0 + i:r0 + i + 1, :] for i in range(6)]
    g_cw_full = small[r0 + 6:r0 + 9, :CONV_WIDTH]
    g_conv = lax.dynamic_slice(g_cw_full, (0, chip * conv_w.shape[2]), (3, conv_w.shape[2]))
    g_sinks = small[r0 + 9:r0 + 10, :N_Q_HEADS]

    def flat2(a):
        return a.reshape(-1, a.shape[-1])

    def unhalve(a):
        return a.reshape(2 * a.shape[1], a.shape[2])

    results = {}

    def adamw(name_, w_, g_, m_, v_):
        g2 = flat2(g_)
        dl, nm, nv = _adamw(flat2(w_), g2, flat2(m_), flat2(v_), name="adamw_" + name_)
        results[name_] = tuple(a.reshape(w_.shape) for a in (g2, dl, nm, nv))

    adamw("w_ada", w_ada, g_w_ada, m_w_ada, v_w_ada)
    adamw("ffn2_w_gate_up", ffn2_w_gate_up, unhalve(full_gu2), m_ffn2_w_gate_up, v_ffn2_w_gate_up)
    adamw("ffn2_w_down", ffn2_w_down, full_d2, m_ffn2_w_down, v_ffn2_w_down)
    adamw("w_out", w_out, full_out, m_w_out, v_w_out)
    adamw("w_in", w_in, full_in.T, m_w_in, v_w_in)
    adamw("ffn1_w_gate_up", ffn1_w_gate_up, unhalve(full_gu1), m_ffn1_w_gate_up, v_ffn1_w_gate_up)
    adamw("ffn1_w_down", ffn1_w_down, full_d1, m_ffn1_w_down, v_ffn1_w_down)
    adamw("b_ada", b_ada, g_b_ada, m_b_ada, v_b_ada)
    adamw("ln1_g", ln1_g, g_ln[0], m_ln1_g, v_ln1_g)
    adamw("ln1_b", ln1_b, g_ln[1], m_ln1_b, v_ln1_b)
    adamw("ln2_g", ln2_g, g_ln[2], m_ln2_g, v_ln2_g)
    adamw("ln2_b", ln2_b, g_ln[3], m_ln2_b, v_ln2_b)
    adamw("ln3_g", ln3_g, g_ln[4], m_ln3_g, v_ln3_g)
    adamw("ln3_b", ln3_b, g_ln[5], m_ln3_b, v_ln3_b)
    adamw("conv_w", conv_w, g_conv, m_conv_w, v_conv_w)
    adamw("attn_sinks", attn_sinks, g_sinks, m_attn_sinks, v_attn_sinks)
    order = ["w_ada", "b_ada", "ffn1_w_gate_up", "ffn1_w_down", "ln1_g", "ln1_b", "w_in", "conv_w", "attn_sinks",
             "w_out", "ln2_g", "ln2_b", "ffn2_w_gate_up", "ffn2_w_down", "ln3_g", "ln3_b"]
    return (loss, grad_x.reshape(x.shape), *[results[n_][0] for n_ in order], *[results[n_][1] for n_ in order],
            *[results[n_][2] for n_ in order], *[results[n_][3] for n_ in order])
```

```python
import jax
import jax.numpy as jnp
from jax import lax
from jax.experimental import pallas as pl
from jax.experimental.pallas import tpu as pltpu

F32 = jnp.float32
BF16 = jnp.bfloat16
MESH = pl.DeviceIdType.MESH

D_MODEL = 1024
HEAD_DIM = 64
ATTN_WIDTH = 512
CONV_WIDTH = 512
N_Q_HEADS = 8
N_KV_HEADS = 2
GQA_GROUP = 4
KV_WIDTH = 128
WINDOW = 128
BLOCK = 128
ROT_DIM = 16
ROPE_THETA = 500000.0
N_MOD = 9
LN_EPS = 1e-5
DN_ALPHA = 2.0 ** 0.25
IN_WIDTH = 2304
N_CHIPS = 4
N_DEV = 8
SMALL_ROWS = 32

ADAM_LR = 0.001
ADAM_B1 = 0.9
ADAM_B2 = 0.999
ADAM_EPS = 1e-08
ADAM_WD = 0.01
ADAM_STEP = 10

LANE = 128
HALO = 16
COL_CHUNK = 256
VMEM_LIMIT = 56 * 1024 * 1024


def _params(sem=None, vmem=True):
    return pltpu.CompilerParams(dimension_semantics=sem, vmem_limit_bytes=VMEM_LIMIT if vmem else None)


def _sigmoid(g):
    return 0.5 * jnp.tanh(0.5 * g) + 0.5


def _row_sum(v):
    return jnp.sum(v, axis=0, keepdims=True)


ROW_CHUNK = 16
EPILOGUE_UNROLL = 8


def _fold8(v):
    return v[0:8, :] + v[8:16, :]


def _row_chunk_loop(n_rows, step, init):
    per_iter = ROW_CHUNK * EPILOGUE_UNROLL
    assert n_rows % per_iter == 0, n_rows

    def body(it, carry):
        for s in range(EPILOGUE_UNROLL):
            start = pl.multiple_of(it * per_iter + s * ROW_CHUNK, ROW_CHUNK)
            carry = step(pl.ds(start, ROW_CHUNK), carry)
        return carry

    return lax.fori_loop(0, n_rows // per_iter, body, init)


def _ln_stats(r):
    mu = jnp.mean(r, axis=-1, keepdims=True)
    rc = r - mu
    var = jnp.mean(rc * rc, axis=-1, keepdims=True)
    rstd = lax.rsqrt(var + LN_EPS)
    return rc * rstd, rstd


def _ln_bwd(dxo, xhat, rstd, g):
    dxhat = dxo * g
    m1 = jnp.mean(dxhat, axis=-1, keepdims=True)
    m2 = jnp.mean(dxhat * xhat, axis=-1, keepdims=True)
    return rstd * (dxhat - m1 - xhat * m2)


def _dot_nt(a, b):
    return lax.dot_general(a, b, (((1,), (1,)), ((), ())), preferred_element_type=F32)


def _dot_tn(a, b):
    return lax.dot_general(a, b, (((0,), (0,)), ((), ())), preferred_element_type=F32)


def _full(shape):
    nd = len(shape)
    return pl.BlockSpec(shape, lambda *_: (0,) * nd)


def _resident(shape):
    nd = len(shape)
    return pl.BlockSpec(shape, lambda *_: (0,) * nd, pipeline_mode=pl.Buffered(1))


ANY_SPEC = pl.BlockSpec(memory_space=pl.ANY)


def _pcall(body, *, name, grid, in_specs, out_specs, out_shape, args, scratch_shapes=(), comm=None, prefetch=None):
    single = not isinstance(out_shape, (list, tuple))
    out_specs = [out_specs] if single else list(out_specs)
    out_shape = [out_shape] if single else list(out_shape)
    in_specs = list(in_specs)
    scratch_shapes = list(scratch_shapes)
    sem = ("arbitrary",) * len(grid)
    n_pre = 0 if prefetch is None else 1
    pre_args = () if prefetch is None else (prefetch,)

    def call(fn, ins_, outs_, shapes_, scratch_, aliases_, operands):
        if prefetch is None:
            return pl.pallas_call(fn, name=name, grid=grid, in_specs=ins_, out_specs=outs_, out_shape=shapes_,
                                  scratch_shapes=scratch_, input_output_aliases=aliases_,
                                  compiler_params=_params(sem))(*operands)
        spec = pltpu.PrefetchScalarGridSpec(num_scalar_prefetch=1, grid=grid, in_specs=ins_, out_specs=outs_,
                                            scratch_shapes=scratch_)
        return pl.pallas_call(fn, name=name, grid_spec=spec, out_shape=shapes_,
                              input_output_aliases={n_pre + i: o for i, o in aliases_.items()},
                              compiler_params=_params(sem))(*pre_args, *operands)

    if comm is None:
        res = call(body, in_specs, out_specs, out_shape, scratch_shapes, {}, args)
        return res[0] if single else res
    n_in, n_out, n_scr = len(in_specs), len(out_specs), len(scratch_shapes)
    nci, nco = len(comm.inputs), len(comm.out_shapes)
    n_steps = 1
    for g in grid:
        n_steps *= g
    staged = n_steps >= 8
    middle_step = (n_steps * 5) // 8 - 1
    late_step = n_steps - 1 - max(1, n_steps // 8)

    def wrapped(*refs):
        pre, refs = refs[:n_pre], refs[n_pre:]
        ins, refs = refs[:n_in], refs[n_in:]
        cin, refs = refs[:nci], refs[nci:]
        outs, refs = refs[:n_out], refs[n_out:]
        cout, refs = refs[:nco], refs[nco:]
        scr, csems = refs[:n_scr], refs[n_scr:]
        step = pl.program_id(0)
        for ax in range(1, len(grid)):
            step = step * grid[ax] + pl.program_id(ax)

        @pl.when(step == 0)
        def _():
            comm.start(cin, cout, csems)

        body(*pre, *ins, *outs, *scr)

        if staged:
            @pl.when(step == middle_step)
            def _():
                comm.middle(cin, cout, csems)

            @pl.when(step == late_step)
            def _():
                comm.late(cin, cout, csems)

        @pl.when(step == n_steps - 1)
        def _():
            if not staged:
                comm.middle(cin, cout, csems)
                comm.late(cin, cout, csems)
            comm.finish(cin, cout, csems)

    res = call(wrapped, in_specs + [ANY_SPEC] * nci, out_specs + [ANY_SPEC] * nco,
               out_shape + list(comm.out_shapes), scratch_shapes + list(comm.sems),
               {n_in + i: n_out + o for i, o in comm.aliases.items()}, (*args, *comm.inputs))
    main = res[:n_out]
    return (main[0] if single else main), list(res[n_out:])


def _ffn_up(xin, lnp, mod, w, *, seq, sc_idx, sh_idx, use_ln, name, comm=None):
    t, d = xin.shape
    f = w.shape[1] // 2
    tm = min(512, seq)
    tpb = seq // tm
    ch = min(COL_CHUNK, f)

    def body(x_ref, ln_ref, mod_ref, w_ref, h_ref, a_ref, dact_ref):
        x = x_ref[...]
        if use_ln:
            x = x * ln_ref[0:1, :] + ln_ref[1:2, :]
        h = x * (1.0 + mod_ref[0, sc_idx:sc_idx + 1, :]) + mod_ref[0, sh_idx:sh_idx + 1, :]
        hb = h.astype(BF16)
        h_ref[...] = hb
        for j in range(f // ch):
            g = jnp.dot(hb, w_ref[:, j * ch:(j + 1) * ch], preferred_element_type=F32)
            u = jnp.dot(hb, w_ref[:, f + j * ch:f + (j + 1) * ch], preferred_element_type=F32)
            s = _sigmoid(g)
            silu = g * s
            a_ref[:, j * ch:(j + 1) * ch] = (silu * u).astype(BF16)
            dact_ref[:, j * ch:(j + 1) * ch] = (u * (s + silu * (1.0 - s))).astype(BF16)
            dact_ref[:, f + j * ch:f + (j + 1) * ch] = silu.astype(BF16)

    return _pcall(
        body, name=name, grid=(t // tm,),
        in_specs=[pl.BlockSpec((tm, d), lambda i: (i, 0)), _full((2, d)),
                  pl.BlockSpec((1, N_MOD, d), lambda i: (i // tpb, 0, 0)), _resident((d, 2 * f))],
        out_specs=[pl.BlockSpec((tm, d), lambda i: (i, 0)), pl.BlockSpec((tm, f), lambda i: (i, 0)),
                   pl.BlockSpec((tm, 2 * f), lambda i: (i, 0))],
        out_shape=[jax.ShapeDtypeStruct((t, d), BF16), jax.ShapeDtypeStruct((t, f), BF16),
                   jax.ShapeDtypeStruct((t, 2 * f), BF16)],
        args=(xin, lnp, mod, w), comm=comm)


def _ffn_down_ln(a, wd, xin, lnp_in, mod, *, seq, gate_idx, use_ln, name, comm=None):
    t, f = a.shape
    d = wd.shape[1]
    tm = min(512, seq)
    tpb = seq // tm

    def body(a_ref, wd_ref, x_ref, ln_ref, mod_ref, f_ref, xhat_ref, rstd_ref, acc):
        av = a_ref[...]
        for j in range(d // COL_CHUNK):
            acc[:, j * COL_CHUNK:(j + 1) * COL_CHUNK] = jnp.dot(
                av, wd_ref[:, j * COL_CHUNK:(j + 1) * COL_CHUNK], preferred_element_type=F32)
        scale = 0.5 * (1.0 + mod_ref[0, gate_idx:gate_idx + 1, :])

        fo = acc[...]
        x = x_ref[...]
        if use_ln:
            x = x * ln_ref[0:1, :] + ln_ref[1:2, :]
        xhat, rstd = _ln_stats(DN_ALPHA * x + scale * fo)
        f_ref[...] = fo.astype(BF16)
        xhat_ref[...] = xhat
        rstd_ref[...] = rstd

    return _pcall(
        body, name=name, grid=(t // tm,),
        in_specs=[pl.BlockSpec((tm, f), lambda i: (i, 0)), _resident((f, d)),
                  pl.BlockSpec((tm, d), lambda i: (i, 0)), _full((2, d)),
                  pl.BlockSpec((1, N_MOD, d), lambda i: (i // tpb, 0, 0))],
        out_specs=[pl.BlockSpec((tm, d), lambda i: (i, 0)), pl.BlockSpec((tm, d), lambda i: (i, 0)),
                   pl.BlockSpec((tm, 1), lambda i: (i, 0))],
        out_shape=[jax.ShapeDtypeStruct((t, d), BF16), jax.ShapeDtypeStruct((t, d), F32),
                   jax.ShapeDtypeStruct((t, 1), F32)],
        scratch_shapes=[pltpu.VMEM((tm, d), F32)],
        args=(a, wd, xin, lnp_in, mod), comm=comm)


def _ffn_down_loss(a, wd, xhat_in, lnp_in, mod, lnp_out, tgt, *, seq, gate_idx, name):
    t, f = a.shape
    d = wd.shape[1]
    nb = t // seq
    tm = min(512, seq)
    tpb = seq // tm

    def body(a_ref, wd_ref, x_ref, lnin_ref, mod_ref, lnout_ref, tgt_ref,
             dr_ref, df_ref, loss_ref, dg_ref, db_ref, dgate_ref, acc):
        i = pl.program_id(0)
        av = a_ref[...]
        for j in range(d // COL_CHUNK):
            acc[:, j * COL_CHUNK:(j + 1) * COL_CHUNK] = jnp.dot(
                av, wd_ref[:, j * COL_CHUNK:(j + 1) * COL_CHUNK], preferred_element_type=F32)
        scale = 0.5 * (1.0 + mod_ref[0, gate_idx:gate_idx + 1, :])
        ag_in, ab_in = DN_ALPHA * lnin_ref[0:1, :], DN_ALPHA * lnin_ref[1:2, :]
        g_out, b_out = lnout_ref[0:1, :], lnout_ref[1:2, :]
        g_over_d = g_out * (1.0 / d)

        def chunk(rows, carry):
            s_loss, s_dg, s_db, s_gate = carry
            fo = acc[rows, :]
            xhat, rstd = _ln_stats(x_ref[rows, :] * ag_in + ab_in + scale * fo)
            e = xhat * g_out + b_out - tgt_ref[rows, :]
            dr = _ln_bwd(e, xhat, rstd, g_over_d)
            dr_ref[rows, :] = dr
            df_ref[rows, :] = (scale * dr).astype(BF16)
            return s_loss + _fold8(e * e), s_dg + _fold8(e * xhat), s_db + _fold8(e), s_gate + _fold8(fo * dr)

        zero = jnp.zeros((8, d), F32)
        s_loss, s_dg, s_db, s_gate = _row_chunk_loop(tm, chunk, (zero, zero, zero, zero))
        s_dg, s_db, s_gate = s_dg * (1.0 / d), s_db * (1.0 / d), s_gate * 0.5

        @pl.when(i == 0)
        def _():
            loss_ref[...] = jnp.zeros_like(loss_ref)
            dg_ref[...] = jnp.zeros_like(dg_ref)
            db_ref[...] = jnp.zeros_like(db_ref)

        @pl.when(i % tpb == 0)
        def _():
            dgate_ref[...] = jnp.zeros_like(dgate_ref)

        loss_ref[...] += _row_sum(s_loss)
        dg_ref[...] += _row_sum(s_dg)
        db_ref[...] += _row_sum(s_db)
        dgate_ref[0] += _row_sum(s_gate)

    return pl.pallas_call(
        body, name=name, grid=(t // tm,), scratch_shapes=[pltpu.VMEM((tm, d), F32)],
        in_specs=[pl.BlockSpec((tm, f), lambda i: (i, 0)), _resident((f, d)),
                  pl.BlockSpec((tm, d), lambda i: (i, 0)), _full((2, d)),
                  pl.BlockSpec((1, N_MOD, d), lambda i: (i // tpb, 0, 0)), _full((2, d)),
                  pl.BlockSpec((tm, d), lambda i: (i, 0))],
        out_specs=[pl.BlockSpec((tm, d), lambda i: (i, 0)), pl.BlockSpec((tm, d), lambda i: (i, 0)),
                   _full((1, d)), _full((1, d)), _full((1, d)),
                   pl.BlockSpec((1, 1, d), lambda i: (i // tpb, 0, 0))],
        out_shape=[jax.ShapeDtypeStruct((t, d), F32), jax.ShapeDtypeStruct((t, d), BF16),
                   jax.ShapeDtypeStruct((1, d), F32), jax.ShapeDtypeStruct((1, d), F32),
                   jax.ShapeDtypeStruct((1, d), F32), jax.ShapeDtypeStruct((nb, 1, d), F32)],
        compiler_params=_params(("arbitrary",)),
    )(a, wd, xhat_in, lnp_in, mod, lnp_out, tgt)


def _rope(v, cos, sa, sb):
    return v * cos + pltpu.roll(v, LANE - ROT_DIM // 2, 1) * sa + pltpu.roll(v, ROT_DIM // 2, 1) * sb


def _rope_t(dy, cos, sa, sb):
    return dy * cos + pltpu.roll(dy * sa, ROT_DIM // 2, 1) + pltpu.roll(dy * sb, LANE - ROT_DIM // 2, 1)


def _in_proj(xhat, lnp, mod, w_t, cos, sa, sb, *, seq, sc_idx, sh_idx, name, comm=None):
    t, d = xhat.shape
    tm = min(512, seq)
    tpb = seq // tm
    n_conv = 3 * CONV_WIDTH

    def body(x_ref, ln_ref, mod_ref, w_ref, cos_ref, sa_ref, sb_ref, h_ref, q_ref, k_ref, v_ref, ubc_ref):
        x = x_ref[...] * ln_ref[0:1, :] + ln_ref[1:2, :]
        h = x * (1.0 + mod_ref[0, sc_idx:sc_idx + 1, :]) + mod_ref[0, sh_idx:sh_idx + 1, :]
        hb = h.astype(BF16)
        h_ref[...] = hb
        cos_t, sa_t, sb_t = cos_ref[...], sa_ref[...], sb_ref[...]
        for j in range(ATTN_WIDTH // COL_CHUNK):
            p = _dot_nt(hb, w_ref[j * COL_CHUNK:(j + 1) * COL_CHUNK, :])
            for s in range(COL_CHUNK // LANE):
                q_ref[:, j * COL_CHUNK + s * LANE:j * COL_CHUNK + (s + 1) * LANE] = _rope(
                    p[:, s * LANE:(s + 1) * LANE], cos_t, sa_t, sb_t).astype(BF16)
        p = _dot_nt(hb, w_ref[ATTN_WIDTH:ATTN_WIDTH + 2 * KV_WIDTH, :])
        k_ref[...] = _rope(p[:, 0:KV_WIDTH], cos_t, sa_t, sb_t).astype(BF16)
        v_ref[...] = p[:, KV_WIDTH:].astype(BF16)
        base = ATTN_WIDTH + 2 * KV_WIDTH
        for j in range(n_conv // COL_CHUNK):
            ubc_ref[:, j * COL_CHUNK:(j + 1) * COL_CHUNK] = _dot_nt(
                hb, w_ref[base + j * COL_CHUNK:base + (j + 1) * COL_CHUNK, :]).astype(BF16)

    row = lambda w: pl.BlockSpec((tm, w), lambda i: (i, 0))
    return _pcall(
        body, name=name, grid=(t // tm,),
        in_specs=[row(d), _full((2, d)), pl.BlockSpec((1, N_MOD, d), lambda i: (i // tpb, 0, 0)),
                  _resident((IN_WIDTH, d)), row(LANE), row(LANE), row(LANE)],
        out_specs=[row(d), row(ATTN_WIDTH), row(KV_WIDTH), row(KV_WIDTH), row(n_conv)],
        out_shape=[jax.ShapeDtypeStruct((t, d), BF16), jax.ShapeDtypeStruct((t, ATTN_WIDTH), BF16),
                   jax.ShapeDtypeStruct((t, KV_WIDTH), BF16), jax.ShapeDtypeStruct((t, KV_WIDTH), BF16),
                   jax.ShapeDtypeStruct((t, n_conv), BF16)],
        args=(xhat, lnp, mod, w_t, cos, sa, sb), comm=comm)


ATTN_TILE_BLOCKS = 2


def _attn_sub_block(s, tile, nblk, kp_ref, kc_ref, vp_ref, vc_ref):
    rows = slice(s * BLOCK, (s + 1) * BLOCK)
    if s == 0:
        first = ((tile * ATTN_TILE_BLOCKS) % nblk) == 0
        return rows, (kp_ref, slice(0, BLOCK)), (kc_ref, rows), (vp_ref, slice(0, BLOCK)), (vc_ref, rows), first
    before = slice((s - 1) * BLOCK, s * BLOCK)
    return rows, (kc_ref, before), (kc_ref, rows), (vc_ref, before), (vc_ref, rows), False


def _attn_group(q_ref, rows, k_prev, k_cur, v_prev, v_cur, sink_ref, g, first):
    lo, hi = g * HEAD_DIM, (g + 1) * HEAD_DIM
    kk = jnp.concatenate([k_prev[0][k_prev[1], lo:hi], k_cur[0][k_cur[1], lo:hi]], axis=0)
    vv = jnp.concatenate([v_prev[0][v_prev[1], lo:hi], v_cur[0][v_cur[1], lo:hi]], axis=0)
    qs = jnp.concatenate([q_ref[rows, (GQA_GROUP * g + j) * HEAD_DIM:(GQA_GROUP * g + j + 1) * HEAD_DIM]
                          for j in range(GQA_GROUP)], axis=0)
    cols = GQA_GROUP * BLOCK
    ki = lax.broadcasted_iota(jnp.int32, (2 * BLOCK, cols), 0)
    col = lax.broadcasted_iota(jnp.int32, (2 * BLOCK, cols), 1)
    diff = (col & (BLOCK - 1)) + BLOCK - ki
    valid = (diff >= 0) & (diff < WINDOW) & ((ki >= BLOCK) | jnp.logical_not(first))
    s = _dot_nt(kk, qs) * (HEAD_DIM ** -0.5)
    s = jnp.where(valid, s, -1e30)
    hcol = lax.broadcasted_iota(jnp.int32, (1, cols), 1)
    sink = jnp.zeros((1, cols), F32)
    for j in range(GQA_GROUP):
        sink = jnp.where(hcol // BLOCK == j, sink_ref[GQA_GROUP * g + j], sink)
    m = jnp.maximum(jnp.max(s, axis=0, keepdims=True), sink)
    p = jnp.exp(s - m)
    ps = jnp.exp(sink - m)
    inv = 1.0 / (jnp.sum(p, axis=0, keepdims=True) + ps)
    return qs, kk, vv, p * inv, ps * inv


def _heads_to_lanes(x_t):
    return jnp.concatenate([x_t[:, j * BLOCK:(j + 1) * BLOCK].T for j in range(GQA_GROUP)], axis=1)


def _attention(q, k, v, sinks, *, seq, name, comm=None):
    t = q.shape[0]
    nblk = seq // BLOCK
    tile = ATTN_TILE_BLOCKS * BLOCK

    def body(q_ref, kp_ref, kc_ref, vp_ref, vc_ref, sink_ref, o_ref):
        for s in range(ATTN_TILE_BLOCKS):
            rows, k_prev, k_cur, v_prev, v_cur, first = _attn_sub_block(
                s, pl.program_id(0), nblk, kp_ref, kc_ref, vp_ref, vc_ref)
            outs = []
            for g in range(N_KV_HEADS):
                _, _, vv, pn, _ = _attn_group(q_ref, rows, k_prev, k_cur, v_prev, v_cur, sink_ref, g, first)
                outs.append(_heads_to_lanes(_dot_tn(vv, pn.astype(BF16))))
            o_ref[rows, :] = jnp.concatenate(outs, axis=1).astype(BF16)

    cur = lambda w: pl.BlockSpec((tile, w), lambda n: (n, 0))
    prev = lambda w: pl.BlockSpec((BLOCK, w), lambda n: (jnp.maximum(n * ATTN_TILE_BLOCKS - 1, 0), 0))
    return _pcall(
        body, name=name, grid=(t // tile,),
        in_specs=[cur(ATTN_WIDTH), prev(KV_WIDTH), cur(KV_WIDTH), prev(KV_WIDTH), cur(KV_WIDTH),
                  pl.BlockSpec(memory_space=pltpu.SMEM)],
        out_specs=cur(ATTN_WIDTH),
        out_shape=jax.ShapeDtypeStruct((t, ATTN_WIDTH), BF16),
        args=(q, k, k, v, v, sinks), comm=comm)


def _out_proj(attn, ubc, cw, wout, xhat_in, lnp_in, mod, *, seq, gate_idx, name, comm=None):
    t, d = xhat_in.shape
    tm = min(512, seq)
    tpb = seq // tm
    cwid = CONV_WIDTH

    def body(attn_ref, ubc_ref, halo_ref, cw_ref, w_ref, x_ref, ln_ref, mod_ref,
             mixin_ref, mix_ref, xhat_ref, rstd_ref, zbuf, acc):
        first = (pl.program_id(0) % tpb) == 0
        u, bg, cg = (ubc_ref[:, s * cwid:(s + 1) * cwid].astype(F32) for s in range(3))
        z = cg * u
        hz = halo_ref[:, 2 * cwid:3 * cwid].astype(F32) * halo_ref[:, 0:cwid].astype(F32)
        zbuf[0:HALO, :] = jnp.where(first, 0.0, hz)
        zbuf[HALO:HALO + tm, :] = z
        y = (cw_ref[0:1, :] * zbuf[HALO - 2:HALO - 2 + tm, :] + cw_ref[1:2, :] * zbuf[HALO - 1:HALO - 1 + tm, :]
             + cw_ref[2:3, :] * z)
        mixin_ref[:, 0:ATTN_WIDTH] = attn_ref[...]
        mixin_ref[:, ATTN_WIDTH:] = (bg * y).astype(BF16)
        mv = mixin_ref[...]
        for j in range(d // COL_CHUNK):
            acc[:, j * COL_CHUNK:(j + 1) * COL_CHUNK] = jnp.dot(
                mv, w_ref[:, j * COL_CHUNK:(j + 1) * COL_CHUNK], preferred_element_type=F32)
        scale = 1.0 + mod_ref[0, gate_idx:gate_idx + 1, :]

        mix = acc[...]
        xhat, rstd = _ln_stats(DN_ALPHA * (x_ref[...] * ln_ref[0:1, :] + ln_ref[1:2, :]) + scale * mix)
        mix_ref[...] = mix.astype(BF16)
        xhat_ref[...] = xhat
        rstd_ref[...] = rstd

    row = lambda w: pl.BlockSpec((tm, w), lambda i: (i, 0))
    return _pcall(
        body, name=name, grid=(t // tm,),
        in_specs=[row(ATTN_WIDTH), row(3 * cwid),
                  pl.BlockSpec((HALO, 3 * cwid), lambda i: (jnp.maximum(i * (tm // HALO) - 1, 0), 0)),
                  _full((8, cwid)), _resident((d, d)), row(d), _full((2, d)),
                  pl.BlockSpec((1, N_MOD, d), lambda i: (i // tpb, 0, 0))],
        out_specs=[row(d), row(d), row(d), row(1)],
        out_shape=[jax.ShapeDtypeStruct((t, d), BF16), jax.ShapeDtypeStruct((t, d), BF16),
                   jax.ShapeDtypeStruct((t, d), F32), jax.ShapeDtypeStruct((t, 1), F32)],
        scratch_shapes=[pltpu.VMEM((tm + HALO, cwid), F32), pltpu.VMEM((tm, d), F32)],
        args=(attn, ubc, ubc, cw, wout, xhat_in, lnp_in, mod), comm=comm)


def _ffn_bwd_act(df, wd, dact, *, seq, name, comm=None):
    t, d = df.shape
    f = wd.shape[0]
    tm = min(512, seq)
    ch = min(COL_CHUNK, f)

    def body(df_ref, wd_ref, dact_ref, dgu_ref):
        dfv = df_ref[...]
        for j in range(f // ch):
            da = _dot_nt(dfv, wd_ref[j * ch:(j + 1) * ch, :])
            dgu_ref[:, j * ch:(j + 1) * ch] = (da * dact_ref[:, j * ch:(j + 1) * ch].astype(F32)).astype(BF16)
            dgu_ref[:, f + j * ch:f + (j + 1) * ch] = (
                da * dact_ref[:, f + j * ch:f + (j + 1) * ch].astype(F32)).astype(BF16)

    return _pcall(
        body, name=name, grid=(t // tm,),
        in_specs=[pl.BlockSpec((tm, d), lambda i: (i, 0)), _resident((f, d)),
                  pl.BlockSpec((tm, 2 * f), lambda i: (i, 0))],
        out_specs=pl.BlockSpec((tm, 2 * f), lambda i: (i, 0)),
        out_shape=jax.ShapeDtypeStruct((t, 2 * f), BF16),
        args=(df, wd, dact), comm=comm)


def _bwd_in(a, w, dr, xin, rstd_prev, lnp_prev, mod, branch_prev, *, seq, w_is_nt, sc_idx, gate_idx,
            branch_scale, final, name, comm=None):
    t, kdim = a.shape
    d = dr.shape[1]
    nb = t // seq
    tm = min(512, seq)
    tpb = seq // tm

    def body(*refs):
        if final:
            a_ref, w_ref, dr_ref, x_ref, mod_ref, dx_ref, dsc_ref, dsh_ref, acc = refs
        else:
            (a_ref, w_ref, dr_ref, x_ref, rstd_ref, ln_ref, mod_ref, br_ref,
             drp_ref, dbr_ref, dsc_ref, dsh_ref, dgate_ref, dg_ref, db_ref, acc) = refs
        i = pl.program_id(0)
        av = a_ref[...]
        for j in range(d // COL_CHUNK):
            cols = slice(j * COL_CHUNK, (j + 1) * COL_CHUNK)
            acc[:, cols] = (_dot_nt(av, w_ref[cols, :]) if w_is_nt
                            else jnp.dot(av, w_ref[:, cols], preferred_element_type=F32))
        sc1 = 1.0 + mod_ref[0, sc_idx:sc_idx + 1, :]
        if not final:
            g_prev, b_prev = ln_ref[0:1, :], ln_ref[1:2, :]
            bscale = branch_scale * (1.0 + mod_ref[0, gate_idx:gate_idx + 1, :])

        def chunk(rows, carry):
            dh = acc[rows, :]
            dx = DN_ALPHA * dr_ref[rows, :] + dh * sc1
            if final:
                dx_ref[rows, :] = dx
                return carry[0] + _fold8(dh * x_ref[rows, :]), carry[1] + _fold8(dh)
            xhat = x_ref[rows, :]
            drp = _ln_bwd(dx, xhat, rstd_ref[rows, :], g_prev)
            drp_ref[rows, :] = drp
            dbr_ref[rows, :] = (bscale * drp).astype(BF16)
            return (carry[0] + _fold8(dh * xhat), carry[1] + _fold8(dh),
                    carry[2] + _fold8(br_ref[rows, :].astype(F32) * drp),
                    carry[3] + _fold8(dx * xhat), carry[4] + _fold8(dx))

        zero = jnp.zeros((8, d), F32)
        sums = list(_row_chunk_loop(tm, chunk, (zero,) * (2 if final else 5)))
        if not final:
            sums[0] = sums[0] * g_prev + sums[1] * b_prev
            sums[2] = sums[2] * branch_scale

        @pl.when((i % tpb) == 0)
        def _():
            dsc_ref[...] = jnp.zeros_like(dsc_ref)
            dsh_ref[...] = jnp.zeros_like(dsh_ref)
            if not final:
                dgate_ref[...] = jnp.zeros_like(dgate_ref)

        dsc_ref[0] += _row_sum(sums[0])
        dsh_ref[0] += _row_sum(sums[1])
        if not final:
            @pl.when(i == 0)
            def _():
                dg_ref[...] = jnp.zeros_like(dg_ref)
                db_ref[...] = jnp.zeros_like(db_ref)

            dgate_ref[0] += _row_sum(sums[2])
            dg_ref[...] += _row_sum(sums[3])
            db_ref[...] += _row_sum(sums[4])

    row = lambda w_: pl.BlockSpec((tm, w_), lambda i: (i, 0))
    vec = pl.BlockSpec((1, 1, d), lambda i: (i // tpb, 0, 0))
    mod_spec = pl.BlockSpec((1, N_MOD, d), lambda i: (i // tpb, 0, 0))
    vshape = jax.ShapeDtypeStruct((nb, 1, d), F32)
    if final:
        in_specs = [row(kdim), _resident(w.shape), row(d), row(d), mod_spec]
        args = (a, w, dr, xin, mod)
        out_specs = [row(d), vec, vec]
        out_shape = [jax.ShapeDtypeStruct((t, d), F32), vshape, vshape]
    else:
        in_specs = [row(kdim), _resident(w.shape), row(d), row(d), row(1), _full((2, d)), mod_spec, row(d)]
        args = (a, w, dr, xin, rstd_prev, lnp_prev, mod, branch_prev)
        out_specs = [row(d), row(d), vec, vec, vec, _full((1, d)), _full((1, d))]
        out_shape = [jax.ShapeDtypeStruct((t, d), F32), jax.ShapeDtypeStruct((t, d), BF16), vshape, vshape, vshape,
                     jax.ShapeDtypeStruct((1, d), F32), jax.ShapeDtypeStruct((1, d), F32)]
    return _pcall(
        body, name=name, grid=(t // tm,), in_specs=in_specs, out_specs=out_specs, out_shape=out_shape,
        scratch_shapes=[pltpu.VMEM((tm, d), F32)], args=args, comm=comm)


def _grad_chip_sum(pos, a, b, *, half_on_rows, name, comm=None):
    t, m = a.shape
    n = b.shape[1]
    tk = min(2048, t)
    nk = t // tk
    half = lambda p, pos_ref: 1 - pos_ref[2] - p + 2 * p * pos_ref[2]
    if half_on_rows:
        n_j = N_CHIPS
        tile = (m // 2, n // n_j)
        a_spec = pl.BlockSpec((tk, tile[0]), lambda p, j, k, pos_ref: (k, half(p, pos_ref)))
        b_spec = pl.BlockSpec((tk, tile[1]), lambda p, j, k, pos_ref: (k, j))
        out_tile = pl.BlockSpec((1, *tile), lambda p, j, k, pos_ref: (0, 0, j * p))
        total = (1, m // 2, n)
    else:
        n_j = 2
        tile = (m // n_j, n // 2)
        a_spec = pl.BlockSpec((tk, tile[0]), lambda p, j, k, pos_ref: (k, j))
        b_spec = pl.BlockSpec((tk, tile[1]), lambda p, j, k, pos_ref: (k, half(p, pos_ref)))
        out_tile = pl.BlockSpec((1, *tile), lambda p, j, k, pos_ref: (0, j * p, 0))
        total = (1, m, n // 2)

    def body(pos_ref, a_ref, b_ref, s32_ref, s16_ref, land_ref, acc, theirs, send_sems, recv_sems, copy_sem):
        p, j, k = pl.program_id(0), pl.program_id(1), pl.program_id(2)
        x, y, c = _position()

        def push(jj):
            return pltpu.make_async_remote_copy(
                src_ref=acc.at[jj], dst_ref=land_ref.at[jj], send_sem=send_sems.at[jj], recv_sem=recv_sems.at[jj],
                device_id=(x, y, 1 - c), device_id_type=MESH)

        fetch = pltpu.make_async_copy(land_ref.at[j], theirs, copy_sem)

        @pl.when(jnp.logical_and(p == 1, k == 0))
        def _():
            push(j).wait_send()
            push(j).wait_recv()
            fetch.start()

        part = _dot_tn(a_ref[...], b_ref[...])

        @pl.when(k == 0)
        def _():
            acc[j] = part

        @pl.when(k > 0)
        def _():
            acc[j] += part

        @pl.when(jnp.logical_and(p == 0, k == nk - 1))
        def _():
            push(j).start()

        @pl.when(jnp.logical_and(p == 1, k == nk - 1))
        def _():
            fetch.wait()
            s = acc[j] + theirs[...]
            s32_ref[0] = s
            s16_ref[0] = s.astype(BF16)

    out = _pcall(
        body, name=name, grid=(2, n_j, nk), in_specs=[a_spec, b_spec], out_specs=[out_tile, out_tile, ANY_SPEC],
        out_shape=[jax.ShapeDtypeStruct(total, F32), jax.ShapeDtypeStruct(total, BF16),
                   jax.ShapeDtypeStruct((n_j, *tile), F32)],
        scratch_shapes=[pltpu.VMEM((n_j, *tile), F32), pltpu.VMEM(tile, F32),
                        pltpu.SemaphoreType.DMA((n_j,)), pltpu.SemaphoreType.DMA((n_j,)), pltpu.SemaphoreType.DMA],
        args=(a, b), prefetch=pos, comm=comm)
    if comm is None:
        return out[0], out[1]
    (s32, s16, _), extra = out
    return (s32, s16), extra


def _matmul_nt_bf16(a, w, *, seq, name):
    t, kdim = a.shape
    n = w.shape[0]
    tm = min(512, seq)

    def body(a_ref, w_ref, o_ref):
        av = a_ref[...]
        for j in range(n // COL_CHUNK):
            o_ref[:, j * COL_CHUNK:(j + 1) * COL_CHUNK] = _dot_nt(
                av, w_ref[j * COL_CHUNK:(j + 1) * COL_CHUNK, :]).astype(BF16)

    return pl.pallas_call(
        body, name=name, grid=(t // tm,),
        in_specs=[pl.BlockSpec((tm, kdim), lambda i: (i, 0)), _resident((n, kdim))],
        out_specs=pl.BlockSpec((tm, n), lambda i: (i, 0)),
        out_shape=jax.ShapeDtypeStruct((t, n), BF16),
        compiler_params=_params(("arbitrary",)),
    )(a, w)


def _attention_bwd(q, k, v, dmixin, sinks, *, seq, name, comm=None):
    t = q.shape[0]
    nblk = seq // BLOCK
    tile = ATTN_TILE_BLOCKS * BLOCK

    def body(q_ref, kp_ref, kc_ref, vp_ref, vc_ref, do_ref, sink_ref,
             dq_ref, dkp_ref, dkc_ref, dvp_ref, dvc_ref, dsink_ref):
        n = pl.program_id(0)

        @pl.when(n == 0)
        def _():
            dsink_ref[...] = jnp.zeros_like(dsink_ref)

        srow = lax.broadcasted_iota(jnp.int32, (8, LANE), 0)
        dsink = jnp.zeros((8, LANE), F32)
        for s in range(ATTN_TILE_BLOCKS):
            rows, k_prev, k_cur, v_prev, v_cur, first = _attn_sub_block(s, n, nblk, kp_ref, kc_ref, vp_ref, vc_ref)
            dqs, dks, dvs = [], [], []
            for g in range(N_KV_HEADS):
                qs, kk, vv, pn, psn = _attn_group(q_ref, rows, k_prev, k_cur, v_prev, v_cur, sink_ref, g, first)
                dos = jnp.concatenate(
                    [do_ref[rows, (GQA_GROUP * g + j) * HEAD_DIM:(GQA_GROUP * g + j + 1) * HEAD_DIM]
                     for j in range(GQA_GROUP)], axis=0)
                dp = _dot_nt(vv, dos)
                delta = jnp.sum(pn * dp, axis=0, keepdims=True)
                ds = pn * (dp - delta)
                dsk = psn * delta
                for j in range(GQA_GROUP):
                    tot = jnp.sum(dsk[:, j * BLOCK:(j + 1) * BLOCK], axis=1, keepdims=True)
                    dsink = dsink - jnp.where(srow == GQA_GROUP * g + j, tot, 0.0)
                dsb = (ds * (HEAD_DIM ** -0.5)).astype(BF16)
                dqs.append(_heads_to_lanes(_dot_tn(kk, dsb)))
                dks.append(jnp.dot(dsb, qs, preferred_element_type=F32))
                dvs.append(jnp.dot(pn.astype(BF16), dos, preferred_element_type=F32))
            dq_ref[rows, :] = jnp.concatenate(dqs, axis=1)
            dkp_ref[rows, :] = jnp.concatenate([x[0:BLOCK, :] for x in dks], axis=1)
            dkc_ref[rows, :] = jnp.concatenate([x[BLOCK:, :] for x in dks], axis=1)
            dvp_ref[rows, :] = jnp.concatenate([x[0:BLOCK, :] for x in dvs], axis=1)
            dvc_ref[rows, :] = jnp.concatenate([x[BLOCK:, :] for x in dvs], axis=1)
        dsink_ref[...] += dsink

    cur = lambda w: pl.BlockSpec((tile, w), lambda n: (n, 0))
    prev = lambda w: pl.BlockSpec((BLOCK, w), lambda n: (jnp.maximum(n * ATTN_TILE_BLOCKS - 1, 0), 0))
    kv = jax.ShapeDtypeStruct((t, KV_WIDTH), F32)
    return _pcall(
        body, name=name, grid=(t // tile,),
        in_specs=[cur(ATTN_WIDTH), prev(KV_WIDTH), cur(KV_WIDTH), prev(KV_WIDTH), cur(KV_WIDTH), cur(ATTN_WIDTH),
                  pl.BlockSpec(memory_space=pltpu.SMEM)],
        out_specs=[cur(ATTN_WIDTH), cur(KV_WIDTH), cur(KV_WIDTH), cur(KV_WIDTH), cur(KV_WIDTH), _full((8, LANE))],
        out_shape=[jax.ShapeDtypeStruct((t, ATTN_WIDTH), F32), kv, kv, kv, kv, jax.ShapeDtypeStruct((8, LANE), F32)],
        args=(q, k, k, v, v, dmixin, sinks), comm=comm)


def _mix_bwd_assemble(dq, dkp, dkc, dvp, dvc, cos, sa, sb, dmixin, ubc, cw, *, seq, name, comm=None):
    t = dq.shape[0]
    cwid = CONV_WIDTH
    tm = min(2 * BLOCK, seq)
    tiles_per_seq = seq // tm
    ntile = t // tm
    nblk_all = t // BLOCK
    per_tile = tm // BLOCK

    def body(*refs):
        dq_ref, dkc_ref, dvc_ref = refs[0:3]
        dkp_refs, dvp_refs = refs[3:3 + per_tile], refs[3 + per_tile:3 + 2 * per_tile]
        (cos_ref, sa_ref, sb_ref, dco_ref, dcon_ref, ubc_ref, hprev_ref, hnext_ref, cw_ref,
         dproj_ref, dcw_ref, zbuf, dybuf) = refs[3 + 2 * per_tile:]
        i = pl.program_id(0)
        first = (i % tiles_per_seq) == 0
        last = (i % tiles_per_seq) == tiles_per_seq - 1
        glast = i == ntile - 1

        @pl.when(i == 0)
        def _():
            dcw_ref[...] = jnp.zeros_like(dcw_ref)

        def with_next_block(cur_ref, nxt_refs):
            nxt = [r[...] for r in nxt_refs]
            nxt[-1] = jnp.where(glast, 0.0, nxt[-1])
            return cur_ref[...] + jnp.concatenate(nxt, axis=0)

        cos_t, sa_t, sb_t = cos_ref[...], sa_ref[...], sb_ref[...]
        for j in range(ATTN_WIDTH // LANE):
            dproj_ref[:, j * LANE:(j + 1) * LANE] = _rope_t(
                dq_ref[:, j * LANE:(j + 1) * LANE], cos_t, sa_t, sb_t).astype(BF16)
        dk = with_next_block(dkc_ref, dkp_refs)
        dproj_ref[:, ATTN_WIDTH:ATTN_WIDTH + KV_WIDTH] = _rope_t(dk, cos_t, sa_t, sb_t).astype(BF16)
        dv = with_next_block(dvc_ref, dvp_refs)
        dproj_ref[:, ATTN_WIDTH + KV_WIDTH:ATTN_WIDTH + 2 * KV_WIDTH] = dv.astype(BF16)

        u, bg, cg = (ubc_ref[:, s * cwid:(s + 1) * cwid].astype(F32) for s in range(3))
        z = cg * u
        hz = hprev_ref[:, 2 * cwid:3 * cwid].astype(F32) * hprev_ref[:, 0:cwid].astype(F32)
        zbuf[0:HALO, :] = jnp.where(first, 0.0, hz)
        zbuf[HALO:HALO + tm, :] = z
        z2, z1 = zbuf[HALO - 2:HALO - 2 + tm, :], zbuf[HALO - 1:HALO - 1 + tm, :]
        w0, w1, w2 = cw_ref[0:1, :], cw_ref[1:2, :], cw_ref[2:3, :]
        y = w0 * z2 + w1 * z1 + w2 * z
        dco = dco_ref[...].astype(F32)
        dyc = dco * bg
        dyn = dcon_ref[...].astype(F32) * hnext_ref[:, cwid:2 * cwid].astype(F32)
        dybuf[0:tm, :] = dyc
        dybuf[tm:tm + HALO, :] = jnp.where(last, 0.0, dyn)
        dz = w2 * dyc + w1 * dybuf[1:1 + tm, :] + w0 * dybuf[2:2 + tm, :]
        srow = lax.broadcasted_iota(jnp.int32, (8, cwid), 0)
        dcw_ref[...] += (jnp.where(srow == 0, _row_sum(dyc * z2), 0.0) + jnp.where(srow == 1, _row_sum(dyc * z1), 0.0)
                         + jnp.where(srow == 2, _row_sum(dyc * z), 0.0))
        base = ATTN_WIDTH + 2 * KV_WIDTH
        dproj_ref[:, base:base + cwid] = (dz * cg).astype(BF16)
        dproj_ref[:, base + cwid:base + 2 * cwid] = (dco * y).astype(BF16)
        dproj_ref[:, base + 2 * cwid:base + 3 * cwid] = (dz * u).astype(BF16)

    cur = lambda w: pl.BlockSpec((tm, w), lambda i: (i, 0))
    nxt = [pl.BlockSpec((BLOCK, KV_WIDTH), lambda i, s=s: (jnp.minimum(i * per_tile + s + 1, nblk_all - 1), 0))
           for s in range(per_tile)]
    prev_halo = pl.BlockSpec((HALO, 3 * cwid), lambda i: (jnp.maximum(i * (tm // HALO) - 1, 0), 0))
    next_halo = lambda w, col: pl.BlockSpec(
        (HALO, w), lambda i: (jnp.minimum((i + 1) * (tm // HALO), t // HALO - 1), col))
    return _pcall(
        body, name=name, grid=(ntile,),
        in_specs=[cur(ATTN_WIDTH), cur(KV_WIDTH), cur(KV_WIDTH), *nxt, *nxt,
                  cur(LANE), cur(LANE), cur(LANE),
                  pl.BlockSpec((tm, cwid), lambda i: (i, 1)), next_halo(cwid, 1),
                  cur(3 * cwid), prev_halo, next_halo(3 * cwid, 0), _full((8, cwid))],
        out_specs=[cur(IN_WIDTH), _full((8, cwid))],
        out_shape=[jax.ShapeDtypeStruct((t, IN_WIDTH), BF16), jax.ShapeDtypeStruct((8, cwid), F32)],
        scratch_shapes=[pltpu.VMEM((tm + HALO, cwid), F32), pltpu.VMEM((tm + HALO, cwid), F32)],
        args=(dq, dkc, dvc, *([dkp] * per_tile), *([dvp] * per_tile), cos, sa, sb, dmixin, dmixin,
              ubc, ubc, ubc, cw), comm=comm)


def _ada_fwd(c_all, w_ada, b_ada_shard, chip, casts, *, name, comm=None):
    nb, d = c_all.shape
    n = w_ada.shape[1]
    steps = 2
    tn = n // steps
    n_cast = len(casts)

    def body(chip_ref, c_ref, w_ref, b_ref, *refs):
        cast_in, o_ref, cast_out = refs[:n_cast], refs[n_cast], refs[n_cast + 1:]
        cv = c_ref[...]
        cond = cv * _sigmoid(cv)
        o_ref[...] = jnp.dot(cond, w_ref[...], preferred_element_type=F32,
                             precision=lax.Precision.HIGHEST) + b_ref[...]
        for src, dst in zip(cast_in, cast_out):
            dst[...] = src[...].astype(BF16)

    in_specs = [_full((nb, d)), pl.BlockSpec((d, tn), lambda j, chip_ref: (0, j)),
                pl.BlockSpec((1, tn), lambda j, chip_ref: (0, j))]
    out_specs = [pl.BlockSpec((nb, tn), lambda j, chip_ref: (0, j))]
    out_shape = [jax.ShapeDtypeStruct((nb, n), F32)]
    for w, col_kind in casts:
        r, c = w.shape
        tr = r // steps
        in_specs.append(pl.BlockSpec((tr, c), lambda j, chip_ref: (j, 0)))
        if col_kind:
            out_specs.append(pl.BlockSpec((tr, c), lambda j, chip_ref: (j, chip_ref[0])))
            out_shape.append(jax.ShapeDtypeStruct((r, c * N_CHIPS), BF16))
        else:
            out_specs.append(pl.BlockSpec((tr, c), lambda j, chip_ref: (chip_ref[0] * steps + j, 0)))
            out_shape.append(jax.ShapeDtypeStruct((r * N_CHIPS, c), BF16))
    out = _pcall(body, name=name, grid=(steps,), in_specs=in_specs, out_specs=out_specs, out_shape=out_shape,
                 args=(c_all, w_ada, b_ada_shard, *[w for w, _ in casts]), prefetch=chip, comm=comm)
    res, extra = out if comm is not None else (out, None)
    return res[0], list(res[1:]), extra


def _small_finish(gathered, dmod_all, dmod_shard, c_all_t, *, name):
    d = D_MODEL
    nb, n = dmod_shard.shape

    def body(g_ref, dm_ref, dms_ref, ct_ref, sum_ref, gw_ref, gb_ref):
        total = g_ref[0]
        for dev in range(1, N_DEV):
            total = total + g_ref[dev]
        sum_ref[...] = total
        gb_ref[...] = _row_sum(dm_ref[...])
        ctv = ct_ref[...]
        cond_t = (ctv * _sigmoid(ctv)).astype(BF16)
        for jb in range(n // COL_CHUNK):
            gw_ref[:, jb * COL_CHUNK:(jb + 1) * COL_CHUNK] = jnp.dot(
                cond_t, dms_ref[:, jb * COL_CHUNK:(jb + 1) * COL_CHUNK].astype(BF16), preferred_element_type=F32)

    return pl.pallas_call(
        body, name=name, grid=(1,),
        in_specs=[_full((N_DEV, SMALL_ROWS, d)), _full((nb, N_MOD * d)), _full((nb, n)), _full((d, nb))],
        out_specs=[_full((SMALL_ROWS, d)), _full((d, n)), _full((1, N_MOD * d))],
        out_shape=[jax.ShapeDtypeStruct((SMALL_ROWS, d), F32), jax.ShapeDtypeStruct((d, n), F32),
                   jax.ShapeDtypeStruct((1, N_MOD * d), F32)],
        compiler_params=_params(("arbitrary",)),
    )(gathered, dmod_all, dmod_shard, c_all_t)


def _row_tile(r, c, budget=1 << 21):
    if r * c * 4 <= budget or r % 16:
        return r
    best = 16
    for tr in range(16, r + 1, 16):
        if r % tr == 0 and tr * c * 4 <= budget:
            best = tr
    return best


def _cast_into(w, chip, col_kind, *, name):
    r, c = w.shape
    tr = _row_tile(r, c)

    def body(chip_ref, w_ref, o_ref):
        o_ref[...] = w_ref[...].astype(BF16)

    if col_kind:
        out_spec = pl.BlockSpec((tr, c), lambda i, chip_ref: (i, chip_ref[0]))
        out_shape = jax.ShapeDtypeStruct((r, c * N_CHIPS), BF16)
    else:
        out_spec = pl.BlockSpec((tr, c), lambda i, chip_ref: (chip_ref[0] * (r // tr) + i, 0))
        out_shape = jax.ShapeDtypeStruct((r * N_CHIPS, c), BF16)
    return _pcall(body, name=name, grid=(r // tr,), in_specs=[pl.BlockSpec((tr, c), lambda i, chip_ref: (i, 0))],
                  out_specs=out_spec, out_shape=out_shape, args=(w,), prefetch=chip)


def _adamw(w, g, m, v, *, name, comm=None):
    r, c = w.shape
    tr = _row_tile(r, c)
    c1 = 1.0 - ADAM_B1 ** ADAM_STEP
    c2 = 1.0 - ADAM_B2 ** ADAM_STEP

    def body(w_ref, g_ref, m_ref, v_ref, d_ref, nm_ref, nv_ref):
        gv = g_ref[...]
        m2 = ADAM_B1 * m_ref[...] + (1.0 - ADAM_B1) * gv
        v2 = ADAM_B2 * v_ref[...] + (1.0 - ADAM_B2) * (gv * gv)
        d_ref[...] = -ADAM_LR * ((m2 / c1) / (jnp.sqrt(v2 / c2) + ADAM_EPS) + ADAM_WD * w_ref[...])
        nm_ref[...] = m2
        nv_ref[...] = v2

    spec = pl.BlockSpec((tr, c), lambda i: (i, 0))
    sh = jax.ShapeDtypeStruct((r, c), F32)
    return _pcall(body, name=name, grid=(r // tr,), in_specs=[spec] * 4, out_specs=[spec] * 3, out_shape=[sh] * 3,
                  args=(w, g, m, v), comm=comm)


def _adamw_small(ws, gs, ms, vs, *, name):
    n = len(ws)
    c1 = 1.0 - ADAM_B1 ** ADAM_STEP
    c2 = 1.0 - ADAM_B2 ** ADAM_STEP

    def body(*refs):
        w_refs, g_refs, m_refs, v_refs = (refs[k * n:(k + 1) * n] for k in range(4))
        out = refs[4 * n:]
        for p in range(n):
            gv = g_refs[p][...]
            m2 = ADAM_B1 * m_refs[p][...] + (1.0 - ADAM_B1) * gv
            v2 = ADAM_B2 * v_refs[p][...] + (1.0 - ADAM_B2) * (gv * gv)
            out[3 * p][...] = -ADAM_LR * ((m2 / c1) / (jnp.sqrt(v2 / c2) + ADAM_EPS) + ADAM_WD * w_refs[p][...])
            out[3 * p + 1][...] = m2
            out[3 * p + 2][...] = v2

    specs = [_full(a.shape) for a in ws]
    res = pl.pallas_call(
        body, name=name, grid=(1,), in_specs=specs * 4,
        out_specs=[s for s in specs for _ in range(3)],
        out_shape=[jax.ShapeDtypeStruct(a.shape, F32) for a in ws for _ in range(3)],
        compiler_params=_params(("arbitrary",)),
    )(*ws, *gs, *ms, *vs)
    return [tuple(res[3 * p:3 * p + 3]) for p in range(n)]


def _sum_final(pos, s32, recv, *, col_kind, n_shard, name, comm=None):
    def body(pos_ref, s_ref, r_ref, o_ref):
        total = ((s_ref[0] + r_ref[0].astype(F32)) + r_ref[1].astype(F32)) + r_ref[2].astype(F32)
        if col_kind:
            o_ref[0] = total
        else:
            o_ref[...] = total

    if col_kind:
        rows, cols = s32.shape[1], n_shard
        tr = _row_tile(rows, cols)
        own = pl.BlockSpec((1, tr, cols), lambda i, pos: (0, i, 2 * pos[0] + pos[1]))
        out_spec = pl.BlockSpec((1, tr, cols), lambda i, pos: (pos[2], i, 0))
        out_shape = jax.ShapeDtypeStruct((2, rows, cols), F32)
    else:
        rows, cols = n_shard, s32.shape[2]
        tr = _row_tile(rows, cols)
        own = pl.BlockSpec((1, tr, cols), lambda i, pos: (0, (2 * pos[0] + pos[1]) * (rows // tr) + i, 0))
        out_spec = pl.BlockSpec((tr, cols), lambda i, pos: (i, pos[2]))
        out_shape = jax.ShapeDtypeStruct((rows, 2 * cols), F32)
    return _pcall(
        body, name=name, grid=(rows // tr,),
        in_specs=[own, pl.BlockSpec((3, tr, cols), lambda i, pos: (0, i, 0))], out_specs=out_spec,
        out_shape=out_shape, args=(s32, recv), prefetch=pos, comm=comm)


def _position():
    return lax.axis_index("x"), lax.axis_index("y"), lax.axis_index("c")


def _allgather8(x_shard, *, name, comm=None):
    m_per, n = x_shard.shape
    nci, nco = (0, 0) if comm is None else (len(comm.inputs), len(comm.out_shapes))

    def body(*refs):
        x_ref, refs = refs[0], refs[1:]
        cin, refs = refs[:nci], refs[nci:]
        out_ref, refs = refs[0], refs[1:]
        cout, refs = refs[:nco], refs[nco:]
        (send_sems, recv_sems, local_sem), csems = refs[:3], refs[3:]
        x, y, c = _position()
        me, sibling = (x, y, c), (x, y, 1 - c)
        chips = [(1 - x, y), (x, 1 - y), (1 - x, 1 - y)]

        def rows(px, py, pc):
            return out_ref.at[pl.ds((4 * px + 2 * py + pc) * m_per, m_per), :]

        def copy(k, block, to, src=None):
            return pltpu.make_async_remote_copy(
                src_ref=rows(*block) if src is None else src, dst_ref=rows(*block),
                send_sem=send_sems.at[k], recv_sem=recv_sems.at[k], device_id=to, device_id_type=MESH)

        mine = pltpu.make_async_copy(x_ref, rows(*me), local_sem)
        mine.start()
        first = [copy(0, me, sibling, src=x_ref)]
        first += [copy(1 + j, me, (*chip, c), src=x_ref) for j, chip in enumerate(chips)]
        for cp in first:
            cp.start()
        if comm is not None:
            comm.start(cin, cout, csems)
        passed = [copy(4 + j, (*chip, c), sibling) for j, chip in enumerate(chips)]
        for j, chip in enumerate(chips):
            copy(1 + j, (*chip, c), me).wait_recv()
            passed[j].start()
        copy(0, sibling, me).wait_recv()
        for j, chip in enumerate(chips):
            copy(4 + j, (*chip, 1 - c), me).wait_recv()
        for cp in first + passed:
            cp.wait_send()
        mine.wait()
        if comm is not None:
            comm.middle(cin, cout, csems)
            comm.late(cin, cout, csems)
            comm.finish(cin, cout, csems)

    vmem = pl.BlockSpec(memory_space=pltpu.VMEM)
    sems = [pltpu.SemaphoreType.DMA((7,)), pltpu.SemaphoreType.DMA((7,)), pltpu.SemaphoreType.DMA]
    out = jax.ShapeDtypeStruct((N_DEV * m_per, n), x_shard.dtype)
    if comm is None:
        return pl.pallas_call(body, name=name, out_shape=out, in_specs=[vmem], out_specs=vmem,
                              scratch_shapes=sems)(x_shard)
    res = pl.pallas_call(
        body, name=name, out_shape=[out] + list(comm.out_shapes), in_specs=[vmem] + [ANY_SPEC] * nci,
        out_specs=[vmem] + [ANY_SPEC] * nco, scratch_shapes=sems + list(comm.sems),
        input_output_aliases={1 + i: 1 + o for i, o in comm.aliases.items()})(x_shard, *comm.inputs)
    return res[0], list(res[1:])


def _peer_chips(x, y):
    return [(1 - x, y), (x, 1 - y), (1 - x, 1 - y)]


class _GatherJob:
    def __init__(self, pieces):
        self.pieces = pieces
        n_p = len(pieces)
        self.inputs = [p[0] for p in pieces]
        self.out_shapes = [jax.ShapeDtypeStruct(p[0].shape, p[0].dtype) for p in pieces]
        for buf, col_kind, r0, nr in pieces:
            half_rows = buf.shape[0] // (2 if col_kind else 2 * N_CHIPS)
            assert r0 % 16 == 0 and nr % 16 == 0 and nr >= 32 and r0 + nr <= half_rows, (buf.shape, r0, nr)
        self.aliases = {p: p for p in range(n_p)}
        dma = pltpu.SemaphoreType.DMA
        self.sems = [dma((2 * n_p,))] * 4 + [dma((4 * n_p,))] * 2

    def _region(self, cout, p, chip_idx, half, part=None):
        buf, col_kind, r0, nr = self.pieces[p]
        first = -(-nr // 32) * 16
        if part == 0:
            nr = first
        elif part == 1:
            r0, nr = r0 + first, nr - first
        if col_kind:
            n = buf.shape[1] // N_CHIPS
            return cout[p].at[pl.ds(half * (buf.shape[0] // 2) + r0, nr), pl.ds(chip_idx * n, n)]
        n = buf.shape[0] // N_CHIPS
        return cout[p].at[pl.ds(chip_idx * n + half * (n // 2) + r0, nr), :]

    def _copies(self, cout, sems):
        send1, recv1, send2, recv2, fsend, frecv = sems
        x, y, c = _position()
        k = 2 * x + y
        sibling = (x, y, 1 - c)
        x_nbr, y_nbr, diag = _peer_chips(x, y)
        chip_of = lambda ch: 2 * ch[0] + ch[1]

        def remote(region, ssem, rsem, to):
            return pltpu.make_async_remote_copy(src_ref=region, dst_ref=region, send_sem=ssem, recv_sem=rsem,
                                                device_id=to, device_id_type=MESH)

        hop1, arrived1, hop2, arrived2, fwds, fwd_arrived = [], [], [], [], [], []
        for p in range(len(self.pieces)):
            for j, nbr in enumerate((x_nbr, y_nbr)):
                i1 = 2 * p + j
                hop1.append(remote(self._region(cout, p, k, c), send1.at[i1], recv1.at[i1], (*nbr, c)))
                arrived1.append(remote(self._region(cout, p, chip_of(nbr), c), send1.at[i1], recv1.at[i1], (*nbr, c)))
            hop2.append(remote(self._region(cout, p, chip_of(x_nbr), c, 0), send2.at[2 * p], recv2.at[2 * p],
                               (*y_nbr, c)))
            hop2.append(remote(self._region(cout, p, chip_of(y_nbr), c, 1), send2.at[2 * p + 1], recv2.at[2 * p + 1],
                               (*x_nbr, c)))
            arrived2.append(remote(self._region(cout, p, chip_of(diag), c, 0), send2.at[2 * p], recv2.at[2 * p],
                                   (*y_nbr, c)))
            arrived2.append(remote(self._region(cout, p, chip_of(diag), c, 1), send2.at[2 * p + 1],
                                   recv2.at[2 * p + 1], (*x_nbr, c)))
            landed = [(chip_of(x_nbr), None), (chip_of(y_nbr), None), (chip_of(diag), 0), (chip_of(diag), 1)]
            for q, (chip_idx, part) in enumerate(landed):
                i3 = 4 * p + q
                fwds.append(remote(self._region(cout, p, chip_idx, c, part), fsend.at[i3], frecv.at[i3], sibling))
                fwd_arrived.append(remote(self._region(cout, p, chip_idx, 1 - c, part), fsend.at[i3], frecv.at[i3],
                                          sibling))
        return hop1, arrived1, hop2, arrived2, fwds, fwd_arrived

    def start(self, cin, cout, sems):
        for cp in self._copies(cout, sems)[0]:
            cp.start()

    def middle(self, cin, cout, sems):
        _, arrived1, hop2, _, fwds, _ = self._copies(cout, sems)
        for p in range(len(self.pieces)):
            for j in range(2):
                arrived1[2 * p + j].wait_recv()
                hop2[2 * p + j].start()
                fwds[4 * p + j].start()

    def late(self, cin, cout, sems):
        _, _, _, arrived2, fwds, _ = self._copies(cout, sems)
        for p in range(len(self.pieces)):
            for j in range(2):
                arrived2[2 * p + j].wait_recv()
                fwds[4 * p + 2 + j].start()

    def finish(self, cin, cout, sems):
        hop1, _, hop2, _, fwds, fwd_arrived = self._copies(cout, sems)
        for cp in fwd_arrived:
            cp.wait_recv()
        for cp in hop1 + hop2 + fwds:
            cp.wait_send()


class _PairedJob:
    aliases = {}

    def start(self, cin, cout, sems):
        for cp in self._copies(cin, cout, sems):
            cp.start()

    def middle(self, cin, cout, sems):
        pass

    late = middle

    def finish(self, cin, cout, sems):
        copies = self._copies(cin, cout, sems)
        for cp in copies:
            cp.wait_recv()
        for cp in copies:
            cp.wait_send()


class _ExchangeJob(_PairedJob):
    def __init__(self, s16, kinds, sizes):
        self.inputs, self.kinds, self.sizes = list(s16), list(kinds), list(sizes)
        self.out_shapes = [jax.ShapeDtypeStruct((3, s.shape[1], n) if kd else (3, n, s.shape[2]), s.dtype)
                           for s, kd, n in zip(s16, kinds, sizes)]
        self.sems = [pltpu.SemaphoreType.DMA((3 * len(s16),)), pltpu.SemaphoreType.DMA((3 * len(s16),))]

    def _copies(self, cin, cout, sems):
        send_sems, recv_sems = sems
        x, y, c = _position()
        copies = []
        for p, src_ref in enumerate(cin):
            for j, chip in enumerate(_peer_chips(x, y)):
                kk = 2 * chip[0] + chip[1]
                n = self.sizes[p]
                src = src_ref.at[0, :, pl.ds(kk * n, n)] if self.kinds[p] else src_ref.at[0, pl.ds(kk * n, n), :]
                copies.append(pltpu.make_async_remote_copy(
                    src_ref=src, dst_ref=cout[p].at[j], send_sem=send_sems.at[3 * p + j],
                    recv_sem=recv_sems.at[3 * p + j], device_id=(*chip, c), device_id_type=MESH))
        return copies


class _ShareJob:
    def __init__(self, halves):
        self.inputs = list(halves)
        self.out_shapes = [jax.ShapeDtypeStruct(h.shape, h.dtype) for h in halves]
        self.aliases = {p: p for p in range(len(halves))}
        self.sems = [pltpu.SemaphoreType.DMA((len(halves),)), pltpu.SemaphoreType.DMA((len(halves),))]

    def _copies(self, cout, sems, half):
        send_sems, recv_sems = sems
        x, y, c = _position()
        h = c if half == "mine" else 1 - c

        def region(o):
            if len(o.shape) == 3:
                return o.at[h]
            hc = o.shape[1] // 2
            return o.at[:, pl.ds(h * hc, hc)]

        return [pltpu.make_async_remote_copy(
            src_ref=region(o), dst_ref=region(o), send_sem=send_sems.at[p], recv_sem=recv_sems.at[p],
            device_id=(x, y, 1 - c), device_id_type=MESH) for p, o in enumerate(cout)]

    def start(self, cin, cout, sems):
        for cp in self._copies(cout, sems, "mine"):
            cp.start()

    def middle(self, cin, cout, sems):
        pass

    late = middle

    def finish(self, cin, cout, sems):
        for cp in self._copies(cout, sems, "theirs"):
            cp.wait_recv()
        for cp in self._copies(cout, sems, "mine"):
            cp.wait_send()


class _MultiJob:
    def __init__(self, jobs):
        self.jobs = jobs
        self.inputs = [a for j in jobs for a in j.inputs]
        self.out_shapes = [s for j in jobs for s in j.out_shapes]
        self.sems = [s for j in jobs for s in j.sems]
        self.aliases = {}
        i0 = o0 = 0
        for j in jobs:
            for i, o in j.aliases.items():
                self.aliases[i0 + i] = o0 + o
            i0 += len(j.inputs)
            o0 += len(j.out_shapes)

    def _parts(self, cin, cout, sems):
        i0 = o0 = s0 = 0
        for j in self.jobs:
            ni, no, ns = len(j.inputs), len(j.out_shapes), len(j.sems)
            yield j, cin[i0:i0 + ni], cout[o0:o0 + no], sems[s0:s0 + ns]
            i0, o0, s0 = i0 + ni, o0 + no, s0 + ns

    def start(self, cin, cout, sems):
        for j, a, b, s in self._parts(cin, cout, sems):
            j.start(a, b, s)

    def middle(self, cin, cout, sems):
        for j, a, b, s in self._parts(cin, cout, sems):
            j.middle(a, b, s)

    def late(self, cin, cout, sems):
        for j, a, b, s in self._parts(cin, cout, sems):
            j.late(a, b, s)

    def finish(self, cin, cout, sems):
        for j, a, b, s in self._parts(cin, cout, sems):
            j.finish(a, b, s)


def _rope_tables(positions):
    half = ROT_DIM // 2
    inv_freq = jnp.power(jnp.float32(ROPE_THETA), -jnp.arange(0, ROT_DIM, 2, dtype=F32) / ROT_DIM)
    inv_head = jnp.concatenate([inv_freq, inv_freq, jnp.zeros((HEAD_DIM - ROT_DIM,), F32)])
    inv_lane = jnp.concatenate([inv_head] * (LANE // HEAD_DIM))
    ang = positions.astype(F32).reshape(-1)[:, None] * inv_lane[None, :]
    sin = jnp.sin(ang)
    dim = jnp.arange(LANE) % HEAD_DIM
    return jnp.cos(ang), jnp.where(dim < half, -sin, 0.0), jnp.where(dim >= half, sin, 0.0)


def kernel(x, c, positions, w_ada, b_ada, ffn1_w_gate_up, ffn1_w_down, ln1_g, ln1_b, w_in, conv_w, attn_sinks, w_out, ln2_g, ln2_b, ffn2_w_gate_up, ffn2_w_down, ln3_g, ln3_b, loss_target, m_w_ada, m_b_ada, m_ffn1_w_gate_up, m_ffn1_w_down, m_ln1_g, m_ln1_b, m_w_in, m_conv_w, m_attn_sinks, m_w_out, m_ln2_g, m_ln2_b, m_ffn2_w_gate_up, m_ffn2_w_down, m_ln3_g, m_ln3_b, v_w_ada, v_b_ada, v_ffn1_w_gate_up, v_ffn1_w_down, v_ln1_g, v_ln1_b, v_w_in, v_conv_w, v_attn_sinks, v_w_out, v_ln2_g, v_ln2_b, v_ffn2_w_gate_up, v_ffn2_w_down, v_ln3_g, v_ln3_b):
    d = D_MODEL
    nb, seq, _ = x.shape
    t = nb * seq
    f = ffn1_w_down.shape[1] * N_CHIPS
    ax, ay, ac = _position()
    chip = 2 * ax + ay
    dev = 2 * chip + ac
    pos = jnp.stack([ax, ay, ac]).astype(jnp.int32)

    x2 = x.reshape(t, d)
    tgt2 = loss_target.reshape(t, d)
    ln1 = jnp.concatenate([ln1_g, ln1_b], axis=0)
    ln2 = jnp.concatenate([ln2_g, ln2_b], axis=0)
    ln3 = jnp.concatenate([ln3_g, ln3_b], axis=0)
    sinks = attn_sinks.reshape(N_Q_HEADS)
    cos_t, sa_t, sb_t = _rope_tables(positions)

    gu_cuts = [0, 176, 352, d // 2]
    gu_part = lambda buf, s: (buf, True, gu_cuts[s], gu_cuts[s + 1] - gu_cuts[s])
    chip_arr = jnp.reshape(chip, (1,)).astype(jnp.int32)
    b_gu1 = _cast_into(ffn1_w_gate_up[0], chip_arr, True, name="cast_gu1")

    n_ada = w_ada.shape[2]
    c_all, (b_gu1,) = _allgather8(c.reshape(nb * d // LANE, LANE), name="gather_c", comm=_GatherJob([gu_part(b_gu1, 0)]))
    c_all = c_all.reshape(N_DEV * nb, d)
    b_shard = lax.dynamic_slice(b_ada, (0, chip * n_ada), (1, n_ada))
    later_shards = [(ffn1_w_down[0], False), (w_in[0].T, False), (w_out[0], False), (ffn2_w_gate_up[0], True),
                    (ffn2_w_down[0], False)]
    mod_part, (b_d1, b_in, b_out, b_gu2, b_d2), (b_gu1,) = _ada_fwd(
        c_all, w_ada[0], b_shard, chip_arr, later_shards, name="ada_fwd", comm=_GatherJob([gu_part(b_gu1, 1)]))
    conv_rows = jnp.pad(conv_w[0], ((0, 5), (0, n_ada - conv_w.shape[2])))
    part = jnp.concatenate([mod_part, conv_rows], axis=0)
    parts, (wgu1,) = _allgather8(part, name="gather_mod", comm=_GatherJob([gu_part(b_gu1, 2)]))
    parts = parts.reshape(N_DEV, N_DEV * nb + 8, n_ada)
    mod_all = jnp.concatenate([parts[2 * k, :N_DEV * nb, :] for k in range(N_CHIPS)], axis=1)
    mod = lax.dynamic_slice(mod_all, (dev * nb, 0), (nb, N_MOD * d)).reshape(nb, N_MOD, d)
    cw_full = jnp.concatenate([parts[2 * k, N_DEV * nb:, :conv_w.shape[2]] for k in range(N_CHIPS)], axis=1)

    n_gu, n_d, n_in, n_out = (ffn1_w_gate_up.shape[2], ffn1_w_down.shape[1], w_in.shape[2], w_out.shape[1])

    def whole(buf, col_kind):
        return (buf, col_kind, 0, buf.shape[0] // (2 if col_kind else 2 * N_CHIPS))

    (h1, a1, dact1), (wd1, wout) = _ffn_up(x2, ln1, mod, wgu1, seq=seq, sc_idx=1, sh_idx=0, use_ln=False,
                                         name="ffn1_up", comm=_GatherJob([whole(b_d1, False), whole(b_out, False)]))
    (f1, xhat1, rstd1), (win_t,) = _ffn_down_ln(a1, wd1, x2, ln1, mod, seq=seq, gate_idx=2, use_ln=False,
                                                name="ffn1_down", comm=_GatherJob([whole(b_in, False)]))
    (h2, q, k, v, ubc), (b_gu2,) = _in_proj(
        xhat1, ln1, mod, win_t, cos_t, sa_t, sb_t, seq=seq, sc_idx=4, sh_idx=3, name="in_proj",
        comm=_GatherJob([gu_part(b_gu2, 0)]))
    attn, (b_gu2,) = _attention(q, k, v, sinks, seq=seq, name="attention", comm=_GatherJob([gu_part(b_gu2, 1)]))
    (mixin, mix, xhat2, rstd2), (wgu2,) = _out_proj(
        attn, ubc, cw_full, wout, xhat1, ln1, mod, seq=seq, gate_idx=5, name="out_proj",
        comm=_GatherJob([gu_part(b_gu2, 2)]))
    (h3, a3, dact3), (wd2,) = _ffn_up(xhat2, ln2, mod, wgu2, seq=seq, sc_idx=7, sh_idx=6, use_ln=True, name="ffn2_up",
                                    comm=_GatherJob([whole(b_d2, False)]))
    dr3, df3, loss_cols, dln3g, dln3b, dgate3 = _ffn_down_loss(
        a3, wd2, xhat2, ln2, mod, ln3, tgt2, seq=seq, gate_idx=8, name="ffn2_down_loss")

    dgu3 = _ffn_bwd_act(df3, wd2, dact3, seq=seq, name="ffn2_bwd_act")
    s32_d2, s16_d2 = _grad_chip_sum(pos, a3, df3, half_on_rows=False, name="grad_wd2")
    (s32_gu2, s16_gu2), (recv_d2,) = _grad_chip_sum(pos, h3, dgu3, half_on_rows=True, name="grad_wgu2",
                                                    comm=_ExchangeJob([s16_d2], [False], [n_d]))
    (dr2, dmix, dsc3, dsh3, dgate2, dln2g, dln2b), (recv_gu2,) = _bwd_in(
        dgu3, wgu2, dr3, xhat2, rstd2, ln2, mod, mix, seq=seq, w_is_nt=True, sc_idx=7, gate_idx=5,
        branch_scale=1.0, final=False, name="ffn2_bwd_in", comm=_ExchangeJob([s16_gu2], [True], [n_gu]))
    s32_out, s16_out = _grad_chip_sum(pos, mixin, dmix, half_on_rows=False, name="grad_wout")
    dmixin = _matmul_nt_bf16(dmix, wout, seq=seq, name="out_proj_bwd")
    (dq, dkp, dkc, dvp, dvc, dsink), (recv_out,) = _attention_bwd(
        q, k, v, dmixin, sinks, seq=seq, name="attention_bwd", comm=_ExchangeJob([s16_out], [False], [n_out]))
    dproj, dcw = _mix_bwd_assemble(
        dq, dkp, dkc, dvp, dvc, cos_t, sa_t, sb_t, dmixin, ubc, cw_full, seq=seq, name="mix_bwd")
    s32_in, s16_in = _grad_chip_sum(pos, dproj, h2, half_on_rows=False, name="grad_win")
    (dr1, df1, dsc2, dsh2, dgate1, dln1g, dln1b), (recv_in,) = _bwd_in(
        dproj, win_t, dr2, xhat1, rstd1, ln1, mod, f1, seq=seq, w_is_nt=False, sc_idx=4, gate_idx=2,
        branch_scale=0.5, final=False, name="in_proj_bwd", comm=_ExchangeJob([s16_in], [False], [n_in]))
    s32_d1, s16_d1 = _grad_chip_sum(pos, a1, df1, half_on_rows=False, name="grad_wd1")
    dgu1, (recv_d1,) = _ffn_bwd_act(df1, wd1, dact1, seq=seq, name="ffn1_bwd_act",
                                    comm=_ExchangeJob([s16_d1], [False], [n_d]))
    s32_gu1, s16_gu1 = _grad_chip_sum(pos, h1, dgu1, half_on_rows=True, name="grad_wgu1")

    def final_half(s32_, recv_, col_kind, n_shard, name_):
        return _sum_final(pos, s32_, recv_, col_kind=col_kind, n_shard=n_shard, name=name_)

    early = [final_half(s32_gu2, recv_gu2, True, n_gu, "sum_final_gu2"),
             final_half(s32_d2, recv_d2, False, n_d, "sum_final_d2"),
             final_half(s32_out, recv_out, False, n_out, "sum_final_out"),
             final_half(s32_in, recv_in, False, n_in, "sum_final_in"),
             final_half(s32_d1, recv_d1, False, n_d, "sum_final_d1")]
    (grad_x, dsc1, dsh1), (recv_gu1, full_gu2, full_d2, full_out, full_in, full_d1) = _bwd_in(
        dgu1, wgu1, dr1, x2, None, None, mod, None, seq=seq, w_is_nt=True, sc_idx=1, gate_idx=None,
        branch_scale=None, final=True, name="ffn1_bwd_in",
        comm=_MultiJob([_ExchangeJob([s16_gu1], [True], [n_gu]), _ShareJob(early)]))
    late = [final_half(s32_gu1, recv_gu1, True, n_gu, "sum_final_gu1")]

    dmod = jnp.concatenate([dsh1, dsc1, dgate1, dsh2, dsc2, dgate2, dsh3, dsc3, dgate3], axis=1)
    loss_row = jnp.sum(loss_cols, axis=1, keepdims=True) * (0.5 / d)
    lane_row = lambda a: jnp.pad(a, ((0, 0), (0, d - a.shape[1])))
    block = jnp.concatenate(
        [dmod.reshape(nb * N_MOD, d), dln1g, dln1b, dln2g, dln2b, dln3g, dln3b,
         lane_row(dcw[0:3, :]), lane_row(dsink[:, 0:1].reshape(1, N_Q_HEADS)), lane_row(loss_row)], axis=0)
    block = jnp.pad(block, ((0, SMALL_ROWS - block.shape[0]), (0, 0)))
    gathered, (full_gu1,) = _allgather8(block, name="gather_small", comm=_ShareJob(late))
    gathered = gathered.reshape(N_DEV, SMALL_ROWS, d)
    dmod_all = gathered[:, :nb * N_MOD, :].reshape(N_DEV * nb, N_MOD * d)
    dmod_shard = lax.dynamic_slice(dmod_all, (0, chip * n_ada), (N_DEV * nb, n_ada))
    small, g_w_ada, g_b_ada = _small_finish(gathered, dmod_all, dmod_shard, c_all.T, name="small_finish")
    r0 = nb * N_MOD
    loss = small[r0 + 10, 0]
    g_ln = [small[r0 + i:r0 + i + 1, :] for i in range(6)]
    g_cw_full = small[r0 + 6:r0 + 9, :CONV_WIDTH]
    g_conv = lax.dynamic_slice(g_cw_full, (0, chip * conv_w.shape[2]), (3, conv_w.shape[2]))
    g_sinks = small[r0 + 9:r0 + 10, :N_Q_HEADS]

    def flat2(a):
        return a.reshape(-1, a.shape[-1])

    def unhalve(a):
        return a.reshape(2 * a.shape[1], a.shape[2])

    results = {}

    def adamw(name_, w_, g_, m_, v_):
        g2 = flat2(g_)
        dl, nm, nv = _adamw(flat2(w_), g2, flat2(m_), flat2(v_), name="adamw_" + name_)
        results[name_] = tuple(a.reshape(w_.shape) for a in (g2, dl, nm, nv))

    adamw("w_ada", w_ada, g_w_ada, m_w_ada, v_w_ada)
    adamw("ffn2_w_gate_up", ffn2_w_gate_up, unhalve(full_gu2), m_ffn2_w_gate_up, v_ffn2_w_gate_up)
    adamw("ffn2_w_down", ffn2_w_down, full_d2, m_ffn2_w_down, v_ffn2_w_down)
    adamw("w_out", w_out, full_out, m_w_out, v_w_out)
    adamw("w_in", w_in, full_in.T, m_w_in, v_w_in)
    adamw("ffn1_w_gate_up", ffn1_w_gate_up, unhalve(full_gu1), m_ffn1_w_gate_up, v_ffn1_w_gate_up)
    adamw("ffn1_w_down", ffn1_w_down, full_d1, m_ffn1_w_down, v_ffn1_w_down)
    small_params = [("b_ada", b_ada, g_b_ada, m_b_ada, v_b_ada),
                    ("ln1_g", ln1_g, g_ln[0], m_ln1_g, v_ln1_g), ("ln1_b", ln1_b, g_ln[1], m_ln1_b, v_ln1_b),
                    ("ln2_g", ln2_g, g_ln[2], m_ln2_g, v_ln2_g), ("ln2_b", ln2_b, g_ln[3], m_ln2_b, v_ln2_b),
                    ("ln3_g", ln3_g, g_ln[4], m_ln3_g, v_ln3_g), ("ln3_b", ln3_b, g_ln[5], m_ln3_b, v_ln3_b),
                    ("conv_w", conv_w, g_conv, m_conv_w, v_conv_w),
                    ("attn_sinks", attn_sinks, g_sinks, m_attn_sinks, v_attn_sinks)]
    small_g = [flat2(g_) for _, _, g_, _, _ in small_params]
    small_res = _adamw_small([flat2(w_) for _, w_, _, _, _ in small_params], small_g,
                             [flat2(m_) for _, _, _, m_, _ in small_params],
                             [flat2(v_) for _, _, _, _, v_ in small_params], name="adamw_small")
    for (name_, w_, _, _, _), g2, res in zip(small_params, small_g, small_res):
        results[name_] = tuple(a.reshape(w_.shape) for a in (g2, *res))
    order = ["w_ada", "b_ada", "ffn1_w_gate_up", "ffn1_w_down", "ln1_g", "ln1_b", "w_in", "conv_w", "attn_sinks",
             "w_out", "ln2_g", "ln2_b", "ffn2_w_gate_up", "ffn2_w_down", "ln3_g", "ln3_b"]
    return (loss, grad_x.reshape(x.shape), *[results[n_][0] for n_ in order], *[results[n_][1] for n_ in order],
            *[results[n_][2] for n_ in order], *[results[n_][3] for n_ in order])
```

```python
import jax
import jax.numpy as jnp
from jax import lax
from jax.experimental import pallas as pl
from jax.experimental.pallas import tpu as pltpu

F32 = jnp.float32
BF16 = jnp.bfloat16
MESH = pl.DeviceIdType.MESH

D_MODEL = 1024
HEAD_DIM = 64
ATTN_WIDTH = 512
CONV_WIDTH = 512
N_Q_HEADS = 8
N_KV_HEADS = 2
GQA_GROUP = 4
KV_WIDTH = 128
WINDOW = 128
BLOCK = 128
ROT_DIM = 16
ROPE_THETA = 500000.0
N_MOD = 9
LN_EPS = 1e-5
DN_ALPHA = 2.0 ** 0.25
IN_WIDTH = 2304
N_CHIPS = 4
N_DEV = 8
SMALL_ROWS = 32

ADAM_LR = 0.001
ADAM_B1 = 0.9
ADAM_B2 = 0.999
ADAM_EPS = 1e-08
ADAM_WD = 0.01
ADAM_STEP = 10

LANE = 128
HALO = 16
COL_CHUNK = 256
VMEM_LIMIT = 56 * 1024 * 1024


def _params(sem=None, vmem=True):
    return pltpu.CompilerParams(dimension_semantics=sem, vmem_limit_bytes=VMEM_LIMIT if vmem else None)


def _sigmoid(g):
    return 0.5 * jnp.tanh(0.5 * g) + 0.5


def _row_sum(v):
    return jnp.sum(v, axis=0, keepdims=True)


ROW_CHUNK = 16
EPILOGUE_UNROLL = 8


def _fold8(v):
    return v[0:8, :] + v[8:16, :]


def _row_chunk_loop(n_rows, step, init):
    per_iter = ROW_CHUNK * EPILOGUE_UNROLL
    assert n_rows % per_iter == 0, n_rows

    def body(it, carry):
        for s in range(EPILOGUE_UNROLL):
            start = pl.multiple_of(it * per_iter + s * ROW_CHUNK, ROW_CHUNK)
            carry = step(pl.ds(start, ROW_CHUNK), carry)
        return carry

    return lax.fori_loop(0, n_rows // per_iter, body, init)


def _ln_stats(r):
    mu = jnp.mean(r, axis=-1, keepdims=True)
    rc = r - mu
    var = jnp.mean(rc * rc, axis=-1, keepdims=True)
    rstd = lax.rsqrt(var + LN_EPS)
    return rc * rstd, rstd


def _ln_bwd(dxo, xhat, rstd, g):
    dxhat = dxo * g
    m1 = jnp.mean(dxhat, axis=-1, keepdims=True)
    m2 = jnp.mean(dxhat * xhat, axis=-1, keepdims=True)
    return rstd * (dxhat - m1 - xhat * m2)


def _dot_nt(a, b):
    return lax.dot_general(a, b, (((1,), (1,)), ((), ())), preferred_element_type=F32)


def _dot_tn(a, b):
    return lax.dot_general(a, b, (((0,), (0,)), ((), ())), preferred_element_type=F32)


def _full(shape):
    nd = len(shape)
    return pl.BlockSpec(shape, lambda *_: (0,) * nd)


def _resident(shape):
    nd = len(shape)
    return pl.BlockSpec(shape, lambda *_: (0,) * nd, pipeline_mode=pl.Buffered(1))


ANY_SPEC = pl.BlockSpec(memory_space=pl.ANY)


def _pcall(body, *, name, grid, in_specs, out_specs, out_shape, args, scratch_shapes=(), comm=None, prefetch=None):
    single = not isinstance(out_shape, (list, tuple))
    out_specs = [out_specs] if single else list(out_specs)
    out_shape = [out_shape] if single else list(out_shape)
    in_specs = list(in_specs)
    scratch_shapes = list(scratch_shapes)
    sem = ("arbitrary",) * len(grid)
    n_pre = 0 if prefetch is None else 1
    pre_args = () if prefetch is None else (prefetch,)

    def call(fn, ins_, outs_, shapes_, scratch_, aliases_, operands):
        if prefetch is None:
            return pl.pallas_call(fn, name=name, grid=grid, in_specs=ins_, out_specs=outs_, out_shape=shapes_,
                                  scratch_shapes=scratch_, input_output_aliases=aliases_,
                                  compiler_params=_params(sem))(*operands)
        spec = pltpu.PrefetchScalarGridSpec(num_scalar_prefetch=1, grid=grid, in_specs=ins_, out_specs=outs_,
                                            scratch_shapes=scratch_)
        return pl.pallas_call(fn, name=name, grid_spec=spec, out_shape=shapes_,
                              input_output_aliases={n_pre + i: o for i, o in aliases_.items()},
                              compiler_params=_params(sem))(*pre_args, *operands)

    if comm is None:
        res = call(body, in_specs, out_specs, out_shape, scratch_shapes, {}, args)
        return res[0] if single else res
    n_in, n_out, n_scr = len(in_specs), len(out_specs), len(scratch_shapes)
    nci, nco = len(comm.inputs), len(comm.out_shapes)
    n_steps = 1
    for g in grid:
        n_steps *= g
    staged = n_steps >= 8
    middle_step = (n_steps * 5) // 8 - 1
    late_step = n_steps - 1 - max(1, n_steps // 8)

    def wrapped(*refs):
        pre, refs = refs[:n_pre], refs[n_pre:]
        ins, refs = refs[:n_in], refs[n_in:]
        cin, refs = refs[:nci], refs[nci:]
        outs, refs = refs[:n_out], refs[n_out:]
        cout, refs = refs[:nco], refs[nco:]
        scr, csems = refs[:n_scr], refs[n_scr:]
        step = pl.program_id(0)
        for ax in range(1, len(grid)):
            step = step * grid[ax] + pl.program_id(ax)

        @pl.when(step == 0)
        def _():
            comm.start(cin, cout, csems)

        body(*pre, *ins, *outs, *scr)

        if staged:
            @pl.when(step == middle_step)
            def _():
                comm.middle(cin, cout, csems)

            @pl.when(step == late_step)
            def _():
                comm.late(cin, cout, csems)

        @pl.when(step == n_steps - 1)
        def _():
            if not staged:
                comm.middle(cin, cout, csems)
                comm.late(cin, cout, csems)
            comm.finish(cin, cout, csems)

    res = call(wrapped, in_specs + [ANY_SPEC] * nci, out_specs + [ANY_SPEC] * nco,
               out_shape + list(comm.out_shapes), scratch_shapes + list(comm.sems),
               {n_in + i: n_out + o for i, o in comm.aliases.items()}, (*args, *comm.inputs))
    main = res[:n_out]
    return (main[0] if single else main), list(res[n_out:])


def _ffn_up(xin, lnp, mod, w, *, seq, sc_idx, sh_idx, use_ln, name, comm=None):
    t, d = xin.shape
    f = w.shape[1] // 2
    tm = min(512, seq)
    tpb = seq // tm
    ch = min(COL_CHUNK, f)

    def body(x_ref, ln_ref, mod_ref, w_ref, h_ref, a_ref, dact_ref):
        x = x_ref[...]
        if use_ln:
            x = x * ln_ref[0:1, :] + ln_ref[1:2, :]
        h = x * (1.0 + mod_ref[0, sc_idx:sc_idx + 1, :]) + mod_ref[0, sh_idx:sh_idx + 1, :]
        hb = h.astype(BF16)
        h_ref[...] = hb
        for j in range(f // ch):
            g = jnp.dot(hb, w_ref[:, j * ch:(j + 1) * ch], preferred_element_type=F32)
            u = jnp.dot(hb, w_ref[:, f + j * ch:f + (j + 1) * ch], preferred_element_type=F32)
            s = _sigmoid(g)
            silu = g * s
            a_ref[:, j * ch:(j + 1) * ch] = (silu * u).astype(BF16)
            dact_ref[:, j * ch:(j + 1) * ch] = (u * (s + silu * (1.0 - s))).astype(BF16)
            dact_ref[:, f + j * ch:f + (j + 1) * ch] = silu.astype(BF16)

    return _pcall(
        body, name=name, grid=(t // tm,),
        in_specs=[pl.BlockSpec((tm, d), lambda i: (i, 0)), _full((2, d)),
                  pl.BlockSpec((1, N_MOD, d), lambda i: (i // tpb, 0, 0)), _resident((d, 2 * f))],
        out_specs=[pl.BlockSpec((tm, d), lambda i: (i, 0)), pl.BlockSpec((tm, f), lambda i: (i, 0)),
                   pl.BlockSpec((tm, 2 * f), lambda i: (i, 0))],
        out_shape=[jax.ShapeDtypeStruct((t, d), BF16), jax.ShapeDtypeStruct((t, f), BF16),
                   jax.ShapeDtypeStruct((t, 2 * f), BF16)],
        args=(xin, lnp, mod, w), comm=comm)


def _ffn_down_ln(a, wd, xin, lnp_in, mod, *, seq, gate_idx, use_ln, name, comm=None):
    t, f = a.shape
    d = wd.shape[1]
    tm = min(512, seq)
    tpb = seq // tm

    def body(a_ref, wd_ref, x_ref, ln_ref, mod_ref, f_ref, xhat_ref, rstd_ref, acc):
        av = a_ref[...]
        for j in range(d // COL_CHUNK):
            acc[:, j * COL_CHUNK:(j + 1) * COL_CHUNK] = jnp.dot(
                av, wd_ref[:, j * COL_CHUNK:(j + 1) * COL_CHUNK], preferred_element_type=F32)
        scale = 0.5 * (1.0 + mod_ref[0, gate_idx:gate_idx + 1, :])

        fo = acc[...]
        x = x_ref[...]
        if use_ln:
            x = x * ln_ref[0:1, :] + ln_ref[1:2, :]
        xhat, rstd = _ln_stats(DN_ALPHA * x + scale * fo)
        f_ref[...] = fo.astype(BF16)
        xhat_ref[...] = xhat
        rstd_ref[...] = rstd

    return _pcall(
        body, name=name, grid=(t // tm,),
        in_specs=[pl.BlockSpec((tm, f), lambda i: (i, 0)), _resident((f, d)),
                  pl.BlockSpec((tm, d), lambda i: (i, 0)), _full((2, d)),
                  pl.BlockSpec((1, N_MOD, d), lambda i: (i // tpb, 0, 0))],
        out_specs=[pl.BlockSpec((tm, d), lambda i: (i, 0)), pl.BlockSpec((tm, d), lambda i: (i, 0)),
                   pl.BlockSpec((tm, 1), lambda i: (i, 0))],
        out_shape=[jax.ShapeDtypeStruct((t, d), BF16), jax.ShapeDtypeStruct((t, d), F32),
                   jax.ShapeDtypeStruct((t, 1), F32)],
        scratch_shapes=[pltpu.VMEM((tm, d), F32)],
        args=(a, wd, xin, lnp_in, mod), comm=comm)


def _ffn_down_loss(a, wd, xhat_in, lnp_in, mod, lnp_out, tgt, *, seq, gate_idx, name):
    t, f = a.shape
    d = wd.shape[1]
    nb = t // seq
    tm = min(512, seq)
    tpb = seq // tm

    def body(a_ref, wd_ref, x_ref, lnin_ref, mod_ref, lnout_ref, tgt_ref,
             dr_ref, df_ref, loss_ref, dg_ref, db_ref, dgate_ref, acc):
        i = pl.program_id(0)
        av = a_ref[...]
        for j in range(d // COL_CHUNK):
            acc[:, j * COL_CHUNK:(j + 1) * COL_CHUNK] = jnp.dot(
                av, wd_ref[:, j * COL_CHUNK:(j + 1) * COL_CHUNK], preferred_element_type=F32)
        scale = 0.5 * (1.0 + mod_ref[0, gate_idx:gate_idx + 1, :])
        ag_in, ab_in = DN_ALPHA * lnin_ref[0:1, :], DN_ALPHA * lnin_ref[1:2, :]
        g_out, b_out = lnout_ref[0:1, :], lnout_ref[1:2, :]
        g_over_d = g_out * (1.0 / d)

        def chunk(rows, carry):
            s_loss, s_dg, s_db, s_gate = carry
            fo = acc[rows, :]
            xhat, rstd = _ln_stats(x_ref[rows, :] * ag_in + ab_in + scale * fo)
            e = xhat * g_out + b_out - tgt_ref[rows, :]
            dr = _ln_bwd(e, xhat, rstd, g_over_d)
            dr_ref[rows, :] = dr
            df_ref[rows, :] = (scale * dr).astype(BF16)
            return s_loss + _fold8(e * e), s_dg + _fold8(e * xhat), s_db + _fold8(e), s_gate + _fold8(fo * dr)

        zero = jnp.zeros((8, d), F32)
        s_loss, s_dg, s_db, s_gate = _row_chunk_loop(tm, chunk, (zero, zero, zero, zero))
        s_dg, s_db, s_gate = s_dg * (1.0 / d), s_db * (1.0 / d), s_gate * 0.5

        @pl.when(i == 0)
        def _():
            loss_ref[...] = jnp.zeros_like(loss_ref)
            dg_ref[...] = jnp.zeros_like(dg_ref)
            db_ref[...] = jnp.zeros_like(db_ref)

        @pl.when(i % tpb == 0)
        def _():
            dgate_ref[...] = jnp.zeros_like(dgate_ref)

        loss_ref[...] += _row_sum(s_loss)
        dg_ref[...] += _row_sum(s_dg)
        db_ref[...] += _row_sum(s_db)
        dgate_ref[0] += _row_sum(s_gate)

    return pl.pallas_call(
        body, name=name, grid=(t // tm,), scratch_shapes=[pltpu.VMEM((tm, d), F32)],
        in_specs=[pl.BlockSpec((tm, f), lambda i: (i, 0)), _resident((f, d)),
                  pl.BlockSpec((tm, d), lambda i: (i, 0)), _full((2, d)),
                  pl.BlockSpec((1, N_MOD, d), lambda i: (i // tpb, 0, 0)), _full((2, d)),
                  pl.BlockSpec((tm, d), lambda i: (i, 0))],
        out_specs=[pl.BlockSpec((tm, d), lambda i: (i, 0)), pl.BlockSpec((tm, d), lambda i: (i, 0)),
                   _full((1, d)), _full((1, d)), _full((1, d)),
                   pl.BlockSpec((1, 1, d), lambda i: (i // tpb, 0, 0))],
        out_shape=[jax.ShapeDtypeStruct((t, d), F32), jax.ShapeDtypeStruct((t, d), BF16),
                   jax.ShapeDtypeStruct((1, d), F32), jax.ShapeDtypeStruct((1, d), F32),
                   jax.ShapeDtypeStruct((1, d), F32), jax.ShapeDtypeStruct((nb, 1, d), F32)],
        compiler_params=_params(("arbitrary",)),
    )(a, wd, xhat_in, lnp_in, mod, lnp_out, tgt)


def _rope(v, cos, sa, sb):
    return v * cos + pltpu.roll(v, LANE - ROT_DIM // 2, 1) * sa + pltpu.roll(v, ROT_DIM // 2, 1) * sb


def _rope_t(dy, cos, sa, sb):
    return dy * cos + pltpu.roll(dy * sa, ROT_DIM // 2, 1) + pltpu.roll(dy * sb, LANE - ROT_DIM // 2, 1)


def _in_proj(xhat, lnp, mod, w_t, cos, sa, sb, *, seq, sc_idx, sh_idx, name, comm=None):
    t, d = xhat.shape
    tm = min(512, seq)
    tpb = seq // tm
    n_conv = 3 * CONV_WIDTH

    def body(x_ref, ln_ref, mod_ref, w_ref, cos_ref, sa_ref, sb_ref, h_ref, q_ref, k_ref, v_ref, ubc_ref):
        x = x_ref[...] * ln_ref[0:1, :] + ln_ref[1:2, :]
        h = x * (1.0 + mod_ref[0, sc_idx:sc_idx + 1, :]) + mod_ref[0, sh_idx:sh_idx + 1, :]
        hb = h.astype(BF16)
        h_ref[...] = hb
        cos_t, sa_t, sb_t = cos_ref[...], sa_ref[...], sb_ref[...]
        for j in range(ATTN_WIDTH // COL_CHUNK):
            p = _dot_nt(hb, w_ref[j * COL_CHUNK:(j + 1) * COL_CHUNK, :])
            for s in range(COL_CHUNK // LANE):
                q_ref[:, j * COL_CHUNK + s * LANE:j * COL_CHUNK + (s + 1) * LANE] = _rope(
                    p[:, s * LANE:(s + 1) * LANE], cos_t, sa_t, sb_t).astype(BF16)
        p = _dot_nt(hb, w_ref[ATTN_WIDTH:ATTN_WIDTH + 2 * KV_WIDTH, :])
        k_ref[...] = _rope(p[:, 0:KV_WIDTH], cos_t, sa_t, sb_t).astype(BF16)
        v_ref[...] = p[:, KV_WIDTH:].astype(BF16)
        base = ATTN_WIDTH + 2 * KV_WIDTH
        for j in range(n_conv // COL_CHUNK):
            ubc_ref[:, j * COL_CHUNK:(j + 1) * COL_CHUNK] = _dot_nt(
                hb, w_ref[base + j * COL_CHUNK:base + (j + 1) * COL_CHUNK, :]).astype(BF16)

    row = lambda w: pl.BlockSpec((tm, w), lambda i: (i, 0))
    return _pcall(
        body, name=name, grid=(t // tm,),
        in_specs=[row(d), _full((2, d)), pl.BlockSpec((1, N_MOD, d), lambda i: (i // tpb, 0, 0)),
                  _resident((IN_WIDTH, d)), row(LANE), row(LANE), row(LANE)],
        out_specs=[row(d), row(ATTN_WIDTH), row(KV_WIDTH), row(KV_WIDTH), row(n_conv)],
        out_shape=[jax.ShapeDtypeStruct((t, d), BF16), jax.ShapeDtypeStruct((t, ATTN_WIDTH), BF16),
                   jax.ShapeDtypeStruct((t, KV_WIDTH), BF16), jax.ShapeDtypeStruct((t, KV_WIDTH), BF16),
                   jax.ShapeDtypeStruct((t, n_conv), BF16)],
        args=(xhat, lnp, mod, w_t, cos, sa, sb), comm=comm)


ATTN_TILE_BLOCKS = 2


def _attn_sub_block(s, tile, nblk, kp_ref, kc_ref, vp_ref, vc_ref):
    rows = slice(s * BLOCK, (s + 1) * BLOCK)
    if s == 0:
        first = ((tile * ATTN_TILE_BLOCKS) % nblk) == 0
        return rows, (kp_ref, slice(0, BLOCK)), (kc_ref, rows), (vp_ref, slice(0, BLOCK)), (vc_ref, rows), first
    before = slice((s - 1) * BLOCK, s * BLOCK)
    return rows, (kc_ref, before), (kc_ref, rows), (vc_ref, before), (vc_ref, rows), False


def _attn_group(q_ref, rows, k_prev, k_cur, v_prev, v_cur, sink_ref, g, first):
    lo, hi = g * HEAD_DIM, (g + 1) * HEAD_DIM
    kk = jnp.concatenate([k_prev[0][k_prev[1], lo:hi], k_cur[0][k_cur[1], lo:hi]], axis=0)
    vv = jnp.concatenate([v_prev[0][v_prev[1], lo:hi], v_cur[0][v_cur[1], lo:hi]], axis=0)
    qs = jnp.concatenate([q_ref[rows, (GQA_GROUP * g + j) * HEAD_DIM:(GQA_GROUP * g + j + 1) * HEAD_DIM]
                          for j in range(GQA_GROUP)], axis=0)
    cols = GQA_GROUP * BLOCK
    ki = lax.broadcasted_iota(jnp.int32, (2 * BLOCK, cols), 0)
    col = lax.broadcasted_iota(jnp.int32, (2 * BLOCK, cols), 1)
    diff = (col & (BLOCK - 1)) + BLOCK - ki
    valid = (diff >= 0) & (diff < WINDOW) & ((ki >= BLOCK) | jnp.logical_not(first))
    s = _dot_nt(kk, qs) * (HEAD_DIM ** -0.5)
    s = jnp.where(valid, s, -1e30)
    hcol = lax.broadcasted_iota(jnp.int32, (1, cols), 1)
    sink = jnp.zeros((1, cols), F32)
    for j in range(GQA_GROUP):
        sink = jnp.where(hcol // BLOCK == j, sink_ref[GQA_GROUP * g + j], sink)
    m = jnp.maximum(jnp.max(s, axis=0, keepdims=True), sink)
    p = jnp.exp(s - m)
    ps = jnp.exp(sink - m)
    inv = 1.0 / (jnp.sum(p, axis=0, keepdims=True) + ps)
    return qs, kk, vv, p * inv, ps * inv


def _heads_to_lanes(x_t):
    return jnp.concatenate([x_t[:, j * BLOCK:(j + 1) * BLOCK].T for j in range(GQA_GROUP)], axis=1)


def _attention(q, k, v, sinks, *, seq, name, comm=None):
    t = q.shape[0]
    nblk = seq // BLOCK
    tile = ATTN_TILE_BLOCKS * BLOCK

    def body(q_ref, kp_ref, kc_ref, vp_ref, vc_ref, sink_ref, o_ref):
        for s in range(ATTN_TILE_BLOCKS):
            rows, k_prev, k_cur, v_prev, v_cur, first = _attn_sub_block(
                s, pl.program_id(0), nblk, kp_ref, kc_ref, vp_ref, vc_ref)
            outs = []
            for g in range(N_KV_HEADS):
                _, _, vv, pn, _ = _attn_group(q_ref, rows, k_prev, k_cur, v_prev, v_cur, sink_ref, g, first)
                outs.append(_heads_to_lanes(_dot_tn(vv, pn.astype(BF16))))
            o_ref[rows, :] = jnp.concatenate(outs, axis=1).astype(BF16)

    cur = lambda w: pl.BlockSpec((tile, w), lambda n: (n, 0))
    prev = lambda w: pl.BlockSpec((BLOCK, w), lambda n: (jnp.maximum(n * ATTN_TILE_BLOCKS - 1, 0), 0))
    return _pcall(
        body, name=name, grid=(t // tile,),
        in_specs=[cur(ATTN_WIDTH), prev(KV_WIDTH), cur(KV_WIDTH), prev(KV_WIDTH), cur(KV_WIDTH),
                  pl.BlockSpec(memory_space=pltpu.SMEM)],
        out_specs=cur(ATTN_WIDTH),
        out_shape=jax.ShapeDtypeStruct((t, ATTN_WIDTH), BF16),
        args=(q, k, k, v, v, sinks), comm=comm)


def _out_proj(attn, ubc, cw, wout, xhat_in, lnp_in, mod, *, seq, gate_idx, name, comm=None):
    t, d = xhat_in.shape
    tm = min(512, seq)
    tpb = seq // tm
    cwid = CONV_WIDTH

    def body(attn_ref, ubc_ref, halo_ref, cw_ref, w_ref, x_ref, ln_ref, mod_ref,
             mixin_ref, mix_ref, xhat_ref, rstd_ref, zbuf, acc):
        first = (pl.program_id(0) % tpb) == 0
        u, bg, cg = (ubc_ref[:, s * cwid:(s + 1) * cwid].astype(F32) for s in range(3))
        z = cg * u
        hz = halo_ref[:, 2 * cwid:3 * cwid].astype(F32) * halo_ref[:, 0:cwid].astype(F32)
        zbuf[0:HALO, :] = jnp.where(first, 0.0, hz)
        zbuf[HALO:HALO + tm, :] = z
        y = (cw_ref[0:1, :] * zbuf[HALO - 2:HALO - 2 + tm, :] + cw_ref[1:2, :] * zbuf[HALO - 1:HALO - 1 + tm, :]
             + cw_ref[2:3, :] * z)
        mixin_ref[:, 0:ATTN_WIDTH] = attn_ref[...]
        mixin_ref[:, ATTN_WIDTH:] = (bg * y).astype(BF16)
        mv = mixin_ref[...]
        for j in range(d // COL_CHUNK):
            acc[:, j * COL_CHUNK:(j + 1) * COL_CHUNK] = jnp.dot(
                mv, w_ref[:, j * COL_CHUNK:(j + 1) * COL_CHUNK], preferred_element_type=F32)
        scale = 1.0 + mod_ref[0, gate_idx:gate_idx + 1, :]

        mix = acc[...]
        xhat, rstd = _ln_stats(DN_ALPHA * (x_ref[...] * ln_ref[0:1, :] + ln_ref[1:2, :]) + scale * mix)
        mix_ref[...] = mix.astype(BF16)
        xhat_ref[...] = xhat
        rstd_ref[...] = rstd

    row = lambda w: pl.BlockSpec((tm, w), lambda i: (i, 0))
    return _pcall(
        body, name=name, grid=(t // tm,),
        in_specs=[row(ATTN_WIDTH), row(3 * cwid),
                  pl.BlockSpec((HALO, 3 * cwid), lambda i: (jnp.maximum(i * (tm // HALO) - 1, 0), 0)),
                  _full((8, cwid)), _resident((d, d)), row(d), _full((2, d)),
                  pl.BlockSpec((1, N_MOD, d), lambda i: (i // tpb, 0, 0))],
        out_specs=[row(d), row(d), row(d), row(1)],
        out_shape=[jax.ShapeDtypeStruct((t, d), BF16), jax.ShapeDtypeStruct((t, d), BF16),
                   jax.ShapeDtypeStruct((t, d), F32), jax.ShapeDtypeStruct((t, 1), F32)],
        scratch_shapes=[pltpu.VMEM((tm + HALO, cwid), F32), pltpu.VMEM((tm, d), F32)],
        args=(attn, ubc, ubc, cw, wout, xhat_in, lnp_in, mod), comm=comm)


def _ffn_bwd_act(df, wd, dact, *, seq, name, comm=None):
    t, d = df.shape
    f = wd.shape[0]
    tm = min(512, seq)
    ch = min(COL_CHUNK, f)

    def body(df_ref, wd_ref, dact_ref, dgu_ref):
        dfv = df_ref[...]
        for j in range(f // ch):
            da = _dot_nt(dfv, wd_ref[j * ch:(j + 1) * ch, :])
            dgu_ref[:, j * ch:(j + 1) * ch] = (da * dact_ref[:, j * ch:(j + 1) * ch].astype(F32)).astype(BF16)
            dgu_ref[:, f + j * ch:f + (j + 1) * ch] = (
                da * dact_ref[:, f + j * ch:f + (j + 1) * ch].astype(F32)).astype(BF16)

    return _pcall(
        body, name=name, grid=(t // tm,),
        in_specs=[pl.BlockSpec((tm, d), lambda i: (i, 0)), _resident((f, d)),
                  pl.BlockSpec((tm, 2 * f), lambda i: (i, 0))],
        out_specs=pl.BlockSpec((tm, 2 * f), lambda i: (i, 0)),
        out_shape=jax.ShapeDtypeStruct((t, 2 * f), BF16),
        args=(df, wd, dact), comm=comm)


def _bwd_in(a, w, dr, xin, rstd_prev, lnp_prev, mod, branch_prev, *, seq, w_is_nt, sc_idx, gate_idx,
            branch_scale, final, name, comm=None):
    t, kdim = a.shape
    d = dr.shape[1]
    nb = t // seq
    tm = min(512, seq)
    tpb = seq // tm

    def body(*refs):
        if final:
            a_ref, w_ref, dr_ref, x_ref, mod_ref, dx_ref, dsc_ref, dsh_ref, acc = refs
        else:
            (a_ref, w_ref, dr_ref, x_ref, rstd_ref, ln_ref, mod_ref, br_ref,
             drp_ref, dbr_ref, dsc_ref, dsh_ref, dgate_ref, dg_ref, db_ref, acc) = refs
        i = pl.program_id(0)
        av = a_ref[...]
        for j in range(d // COL_CHUNK):
            cols = slice(j * COL_CHUNK, (j + 1) * COL_CHUNK)
            acc[:, cols] = (_dot_nt(av, w_ref[cols, :]) if w_is_nt
                            else jnp.dot(av, w_ref[:, cols], preferred_element_type=F32))
        sc1 = 1.0 + mod_ref[0, sc_idx:sc_idx + 1, :]
        if not final:
            g_prev, b_prev = ln_ref[0:1, :], ln_ref[1:2, :]
            bscale = branch_scale * (1.0 + mod_ref[0, gate_idx:gate_idx + 1, :])

        def chunk(rows, carry):
            dh = acc[rows, :]
            dx = DN_ALPHA * dr_ref[rows, :] + dh * sc1
            if final:
                dx_ref[rows, :] = dx
                return carry[0] + _fold8(dh * x_ref[rows, :]), carry[1] + _fold8(dh)
            xhat = x_ref[rows, :]
            drp = _ln_bwd(dx, xhat, rstd_ref[rows, :], g_prev)
            drp_ref[rows, :] = drp
            dbr_ref[rows, :] = (bscale * drp).astype(BF16)
            return (carry[0] + _fold8(dh * xhat), carry[1] + _fold8(dh),
                    carry[2] + _fold8(br_ref[rows, :].astype(F32) * drp),
                    carry[3] + _fold8(dx * xhat), carry[4] + _fold8(dx))

        zero = jnp.zeros((8, d), F32)
        sums = list(_row_chunk_loop(tm, chunk, (zero,) * (2 if final else 5)))
        if not final:
            sums[0] = sums[0] * g_prev + sums[1] * b_prev
            sums[2] = sums[2] * branch_scale

        @pl.when((i % tpb) == 0)
        def _():
            dsc_ref[...] = jnp.zeros_like(dsc_ref)
            dsh_ref[...] = jnp.zeros_like(dsh_ref)
            if not final:
                dgate_ref[...] = jnp.zeros_like(dgate_ref)

        dsc_ref[0] += _row_sum(sums[0])
        dsh_ref[0] += _row_sum(sums[1])
        if not final:
            @pl.when(i == 0)
            def _():
                dg_ref[...] = jnp.zeros_like(dg_ref)
                db_ref[...] = jnp.zeros_like(db_ref)

            dgate_ref[0] += _row_sum(sums[2])
            dg_ref[...] += _row_sum(sums[3])
            db_ref[...] += _row_sum(sums[4])

    row = lambda w_: pl.BlockSpec((tm, w_), lambda i: (i, 0))
    vec = pl.BlockSpec((1, 1, d), lambda i: (i // tpb, 0, 0))
    mod_spec = pl.BlockSpec((1, N_MOD, d), lambda i: (i // tpb, 0, 0))
    vshape = jax.ShapeDtypeStruct((nb, 1, d), F32)
    if final:
        in_specs = [row(kdim), _resident(w.shape), row(d), row(d), mod_spec]
        args = (a, w, dr, xin, mod)
        out_specs = [row(d), vec, vec]
        out_shape = [jax.ShapeDtypeStruct((t, d), F32), vshape, vshape]
    else:
        in_specs = [row(kdim), _resident(w.shape), row(d), row(d), row(1), _full((2, d)), mod_spec, row(d)]
        args = (a, w, dr, xin, rstd_prev, lnp_prev, mod, branch_prev)
        out_specs = [row(d), row(d), vec, vec, vec, _full((1, d)), _full((1, d))]
        out_shape = [jax.ShapeDtypeStruct((t, d), F32), jax.ShapeDtypeStruct((t, d), BF16), vshape, vshape, vshape,
                     jax.ShapeDtypeStruct((1, d), F32), jax.ShapeDtypeStruct((1, d), F32)]
    return _pcall(
        body, name=name, grid=(t // tm,), in_specs=in_specs, out_specs=out_specs, out_shape=out_shape,
        scratch_shapes=[pltpu.VMEM((tm, d), F32)], args=args, comm=comm)


def _grad_chip_sum(pos, a, b, *, half_on_rows, name, comm=None):
    t, m = a.shape
    n = b.shape[1]
    tk = min(2048, t)
    nk = t // tk
    half = lambda p, pos_ref: 1 - pos_ref[2] - p + 2 * p * pos_ref[2]
    if half_on_rows:
        n_j = N_CHIPS
        tile = (m // 2, n // n_j)
        a_spec = pl.BlockSpec((tk, tile[0]), lambda p, j, k, pos_ref: (k, half(p, pos_ref)))
        b_spec = pl.BlockSpec((tk, tile[1]), lambda p, j, k, pos_ref: (k, j))
        out_tile = pl.BlockSpec((1, *tile), lambda p, j, k, pos_ref: (0, 0, j * p))
        total = (1, m // 2, n)
    else:
        n_j = 2
        tile = (m // n_j, n // 2)
        a_spec = pl.BlockSpec((tk, tile[0]), lambda p, j, k, pos_ref: (k, j))
        b_spec = pl.BlockSpec((tk, tile[1]), lambda p, j, k, pos_ref: (k, half(p, pos_ref)))
        out_tile = pl.BlockSpec((1, *tile), lambda p, j, k, pos_ref: (0, j * p, 0))
        total = (1, m, n // 2)

    def body(pos_ref, a_ref, b_ref, s32_ref, s16_ref, land_ref, acc, theirs, send_sems, recv_sems, copy_sem):
        p, j, k = pl.program_id(0), pl.program_id(1), pl.program_id(2)
        x, y, c = _position()

        def push(jj):
            return pltpu.make_async_remote_copy(
                src_ref=acc.at[jj], dst_ref=land_ref.at[jj], send_sem=send_sems.at[jj], recv_sem=recv_sems.at[jj],
                device_id=(x, y, 1 - c), device_id_type=MESH)

        fetch = pltpu.make_async_copy(land_ref.at[j], theirs, copy_sem)

        @pl.when(jnp.logical_and(p == 1, k == 0))
        def _():
            push(j).wait_send()
            push(j).wait_recv()
            fetch.start()

        part = _dot_tn(a_ref[...], b_ref[...])

        @pl.when(k == 0)
        def _():
            acc[j] = part

        @pl.when(k > 0)
        def _():
            acc[j] += part

        @pl.when(jnp.logical_and(p == 0, k == nk - 1))
        def _():
            push(j).start()

        @pl.when(jnp.logical_and(p == 1, k == nk - 1))
        def _():
            fetch.wait()
            s = acc[j] + theirs[...]
            s32_ref[0] = s
            s16_ref[0] = s.astype(BF16)

    out = _pcall(
        body, name=name, grid=(2, n_j, nk), in_specs=[a_spec, b_spec], out_specs=[out_tile, out_tile, ANY_SPEC],
        out_shape=[jax.ShapeDtypeStruct(total, F32), jax.ShapeDtypeStruct(total, BF16),
                   jax.ShapeDtypeStruct((n_j, *tile), F32)],
        scratch_shapes=[pltpu.VMEM((n_j, *tile), F32), pltpu.VMEM(tile, F32),
                        pltpu.SemaphoreType.DMA((n_j,)), pltpu.SemaphoreType.DMA((n_j,)), pltpu.SemaphoreType.DMA],
        args=(a, b), prefetch=pos, comm=comm)
    if comm is None:
        return out[0], out[1]
    (s32, s16, _), extra = out
    return (s32, s16), extra


def _matmul_nt_bf16(a, w, *, seq, name):
    t, kdim = a.shape
    n = w.shape[0]
    tm = min(512, seq)

    def body(a_ref, w_ref, o_ref):
        av = a_ref[...]
        for j in range(n // COL_CHUNK):
            o_ref[:, j * COL_CHUNK:(j + 1) * COL_CHUNK] = _dot_nt(
                av, w_ref[j * COL_CHUNK:(j + 1) * COL_CHUNK, :]).astype(BF16)

    return pl.pallas_call(
        body, name=name, grid=(t // tm,),
        in_specs=[pl.BlockSpec((tm, kdim), lambda i: (i, 0)), _resident((n, kdim))],
        out_specs=pl.BlockSpec((tm, n), lambda i: (i, 0)),
        out_shape=jax.ShapeDtypeStruct((t, n), BF16),
        compiler_params=_params(("arbitrary",)),
    )(a, w)


def _attention_bwd(q, k, v, dmixin, sinks, *, seq, name, comm=None):
    t = q.shape[0]
    nblk = seq // BLOCK
    tile = ATTN_TILE_BLOCKS * BLOCK

    def body(q_ref, kp_ref, kc_ref, vp_ref, vc_ref, do_ref, sink_ref,
             dq_ref, dkp_ref, dkc_ref, dvp_ref, dvc_ref, dsink_ref):
        n = pl.program_id(0)

        @pl.when(n == 0)
        def _():
            dsink_ref[...] = jnp.zeros_like(dsink_ref)

        srow = lax.broadcasted_iota(jnp.int32, (8, LANE), 0)
        dsink = jnp.zeros((8, LANE), F32)
        for s in range(ATTN_TILE_BLOCKS):
            rows, k_prev, k_cur, v_prev, v_cur, first = _attn_sub_block(s, n, nblk, kp_ref, kc_ref, vp_ref, vc_ref)
            dqs, dks, dvs = [], [], []
            for g in range(N_KV_HEADS):
                qs, kk, vv, pn, psn = _attn_group(q_ref, rows, k_prev, k_cur, v_prev, v_cur, sink_ref, g, first)
                dos = jnp.concatenate(
                    [do_ref[rows, (GQA_GROUP * g + j) * HEAD_DIM:(GQA_GROUP * g + j + 1) * HEAD_DIM]
                     for j in range(GQA_GROUP)], axis=0)
                dp = _dot_nt(vv, dos)
                delta = jnp.sum(pn * dp, axis=0, keepdims=True)
                ds = pn * (dp - delta)
                dsk = psn * delta
                for j in range(GQA_GROUP):
                    tot = jnp.sum(dsk[:, j * BLOCK:(j + 1) * BLOCK], axis=1, keepdims=True)
                    dsink = dsink - jnp.where(srow == GQA_GROUP * g + j, tot, 0.0)
                dsb = (ds * (HEAD_DIM ** -0.5)).astype(BF16)
                dqs.append(_heads_to_lanes(_dot_tn(kk, dsb)))
                dks.append(jnp.dot(dsb, qs, preferred_element_type=F32))
                dvs.append(jnp.dot(pn.astype(BF16), dos, preferred_element_type=F32))
            dq_ref[rows, :] = jnp.concatenate(dqs, axis=1)
            dkp_ref[rows, :] = jnp.concatenate([x[0:BLOCK, :] for x in dks], axis=1)
            dkc_ref[rows, :] = jnp.concatenate([x[BLOCK:, :] for x in dks], axis=1)
            dvp_ref[rows, :] = jnp.concatenate([x[0:BLOCK, :] for x in dvs], axis=1)
            dvc_ref[rows, :] = jnp.concatenate([x[BLOCK:, :] for x in dvs], axis=1)
        dsink_ref[...] += dsink

    cur = lambda w: pl.BlockSpec((tile, w), lambda n: (n, 0))
    prev = lambda w: pl.BlockSpec((BLOCK, w), lambda n: (jnp.maximum(n * ATTN_TILE_BLOCKS - 1, 0), 0))
    kv = jax.ShapeDtypeStruct((t, KV_WIDTH), F32)
    return _pcall(
        body, name=name, grid=(t // tile,),
        in_specs=[cur(ATTN_WIDTH), prev(KV_WIDTH), cur(KV_WIDTH), prev(KV_WIDTH), cur(KV_WIDTH), cur(ATTN_WIDTH),
                  pl.BlockSpec(memory_space=pltpu.SMEM)],
        out_specs=[cur(ATTN_WIDTH), cur(KV_WIDTH), cur(KV_WIDTH), cur(KV_WIDTH), cur(KV_WIDTH), _full((8, LANE))],
        out_shape=[jax.ShapeDtypeStruct((t, ATTN_WIDTH), F32), kv, kv, kv, kv, jax.ShapeDtypeStruct((8, LANE), F32)],
        args=(q, k, k, v, v, dmixin, sinks), comm=comm)


def _mix_bwd_assemble(dq, dkp, dkc, dvp, dvc, cos, sa, sb, dmixin, ubc, cw, *, seq, name, comm=None):
    t = dq.shape[0]
    cwid = CONV_WIDTH
    tm = min(2 * BLOCK, seq)
    tiles_per_seq = seq // tm
    ntile = t // tm
    nblk_all = t // BLOCK
    per_tile = tm // BLOCK

    def body(*refs):
        dq_ref, dkc_ref, dvc_ref = refs[0:3]
        dkp_refs, dvp_refs = refs[3:3 + per_tile], refs[3 + per_tile:3 + 2 * per_tile]
        (cos_ref, sa_ref, sb_ref, dco_ref, dcon_ref, ubc_ref, hprev_ref, hnext_ref, cw_ref,
         dproj_ref, dcw_ref, zbuf, dybuf) = refs[3 + 2 * per_tile:]
        i = pl.program_id(0)
        first = (i % tiles_per_seq) == 0
        last = (i % tiles_per_seq) == tiles_per_seq - 1
        glast = i == ntile - 1

        @pl.when(i == 0)
        def _():
            dcw_ref[...] = jnp.zeros_like(dcw_ref)

        def with_next_block(cur_ref, nxt_refs):
            nxt = [r[...] for r in nxt_refs]
            nxt[-1] = jnp.where(glast, 0.0, nxt[-1])
            return cur_ref[...] + jnp.concatenate(nxt, axis=0)

        cos_t, sa_t, sb_t = cos_ref[...], sa_ref[...], sb_ref[...]
        for j in range(ATTN_WIDTH // LANE):
            dproj_ref[:, j * LANE:(j + 1) * LANE] = _rope_t(
                dq_ref[:, j * LANE:(j + 1) * LANE], cos_t, sa_t, sb_t).astype(BF16)
        dk = with_next_block(dkc_ref, dkp_refs)
        dproj_ref[:, ATTN_WIDTH:ATTN_WIDTH + KV_WIDTH] = _rope_t(dk, cos_t, sa_t, sb_t).astype(BF16)
        dv = with_next_block(dvc_ref, dvp_refs)
        dproj_ref[:, ATTN_WIDTH + KV_WIDTH:ATTN_WIDTH + 2 * KV_WIDTH] = dv.astype(BF16)

        u, bg, cg = (ubc_ref[:, s * cwid:(s + 1) * cwid].astype(F32) for s in range(3))
        z = cg * u
        hz = hprev_ref[:, 2 * cwid:3 * cwid].astype(F32) * hprev_ref[:, 0:cwid].astype(F32)
        zbuf[0:HALO, :] = jnp.where(first, 0.0, hz)
        zbuf[HALO:HALO + tm, :] = z
        z2, z1 = zbuf[HALO - 2:HALO - 2 + tm, :], zbuf[HALO - 1:HALO - 1 + tm, :]
        w0, w1, w2 = cw_ref[0:1, :], cw_ref[1:2, :], cw_ref[2:3, :]
        y = w0 * z2 + w1 * z1 + w2 * z
        dco = dco_ref[...].astype(F32)
        dyc = dco * bg
        dyn = dcon_ref[...].astype(F32) * hnext_ref[:, cwid:2 * cwid].astype(F32)
        dybuf[0:tm, :] = dyc
        dybuf[tm:tm + HALO, :] = jnp.where(last, 0.0, dyn)
        dz = w2 * dyc + w1 * dybuf[1:1 + tm, :] + w0 * dybuf[2:2 + tm, :]
        srow = lax.broadcasted_iota(jnp.int32, (8, cwid), 0)
        dcw_ref[...] += (jnp.where(srow == 0, _row_sum(dyc * z2), 0.0) + jnp.where(srow == 1, _row_sum(dyc * z1), 0.0)
                         + jnp.where(srow == 2, _row_sum(dyc * z), 0.0))
        base = ATTN_WIDTH + 2 * KV_WIDTH
        dproj_ref[:, base:base + cwid] = (dz * cg).astype(BF16)
        dproj_ref[:, base + cwid:base + 2 * cwid] = (dco * y).astype(BF16)
        dproj_ref[:, base + 2 * cwid:base + 3 * cwid] = (dz * u).astype(BF16)

    cur = lambda w: pl.BlockSpec((tm, w), lambda i: (i, 0))
    nxt = [pl.BlockSpec((BLOCK, KV_WIDTH), lambda i, s=s: (jnp.minimum(i * per_tile + s + 1, nblk_all - 1), 0))
           for s in range(per_tile)]
    prev_halo = pl.BlockSpec((HALO, 3 * cwid), lambda i: (jnp.maximum(i * (tm // HALO) - 1, 0), 0))
    next_halo = lambda w, col: pl.BlockSpec(
        (HALO, w), lambda i: (jnp.minimum((i + 1) * (tm // HALO), t // HALO - 1), col))
    return _pcall(
        body, name=name, grid=(ntile,),
        in_specs=[cur(ATTN_WIDTH), cur(KV_WIDTH), cur(KV_WIDTH), *nxt, *nxt,
                  cur(LANE), cur(LANE), cur(LANE),
                  pl.BlockSpec((tm, cwid), lambda i: (i, 1)), next_halo(cwid, 1),
                  cur(3 * cwid), prev_halo, next_halo(3 * cwid, 0), _full((8, cwid))],
        out_specs=[cur(IN_WIDTH), _full((8, cwid))],
        out_shape=[jax.ShapeDtypeStruct((t, IN_WIDTH), BF16), jax.ShapeDtypeStruct((8, cwid), F32)],
        scratch_shapes=[pltpu.VMEM((tm + HALO, cwid), F32), pltpu.VMEM((tm + HALO, cwid), F32)],
        args=(dq, dkc, dvc, *([dkp] * per_tile), *([dvp] * per_tile), cos, sa, sb, dmixin, dmixin,
              ubc, ubc, ubc, cw), comm=comm)


def _ada_fwd(c_all, w_ada, b_ada_shard, chip, casts, *, name, comm=None):
    nb, d = c_all.shape
    n = w_ada.shape[1]
    steps = 2
    tn = n // steps
    n_cast = len(casts)

    def body(chip_ref, c_ref, w_ref, b_ref, *refs):
        cast_in, o_ref, cast_out = refs[:n_cast], refs[n_cast], refs[n_cast + 1:]
        cv = c_ref[...]
        cond = cv * _sigmoid(cv)
        o_ref[...] = jnp.dot(cond, w_ref[...], preferred_element_type=F32,
                             precision=lax.Precision.HIGHEST) + b_ref[...]
        for src, dst in zip(cast_in, cast_out):
            dst[...] = src[...].astype(BF16)

    in_specs = [_full((nb, d)), pl.BlockSpec((d, tn), lambda j, chip_ref: (0, j)),
                pl.BlockSpec((1, tn), lambda j, chip_ref: (0, j))]
    out_specs = [pl.BlockSpec((nb, tn), lambda j, chip_ref: (0, j))]
    out_shape = [jax.ShapeDtypeStruct((nb, n), F32)]
    for w, col_kind in casts:
        r, c = w.shape
        tr = r // steps
        in_specs.append(pl.BlockSpec((tr, c), lambda j, chip_ref: (j, 0)))
        if col_kind:
            out_specs.append(pl.BlockSpec((tr, c), lambda j, chip_ref: (j, chip_ref[0])))
            out_shape.append(jax.ShapeDtypeStruct((r, c * N_CHIPS), BF16))
        else:
            out_specs.append(pl.BlockSpec((tr, c), lambda j, chip_ref: (chip_ref[0] * steps + j, 0)))
            out_shape.append(jax.ShapeDtypeStruct((r * N_CHIPS, c), BF16))
    out = _pcall(body, name=name, grid=(steps,), in_specs=in_specs, out_specs=out_specs, out_shape=out_shape,
                 args=(c_all, w_ada, b_ada_shard, *[w for w, _ in casts]), prefetch=chip, comm=comm)
    res, extra = out if comm is not None else (out, None)
    return res[0], list(res[1:]), extra


def _small_finish(gathered, dmod_all, dmod_shard, c_all_t, *, name):
    d = D_MODEL
    nb, n = dmod_shard.shape

    def body(g_ref, dm_ref, dms_ref, ct_ref, sum_ref, gw_ref, gb_ref):
        total = g_ref[0]
        for dev in range(1, N_DEV):
            total = total + g_ref[dev]
        sum_ref[...] = total
        gb_ref[...] = _row_sum(dm_ref[...])
        ctv = ct_ref[...]
        cond_t = (ctv * _sigmoid(ctv)).astype(BF16)
        for jb in range(n // COL_CHUNK):
            gw_ref[:, jb * COL_CHUNK:(jb + 1) * COL_CHUNK] = jnp.dot(
                cond_t, dms_ref[:, jb * COL_CHUNK:(jb + 1) * COL_CHUNK].astype(BF16), preferred_element_type=F32)

    return pl.pallas_call(
        body, name=name, grid=(1,),
        in_specs=[_full((N_DEV, SMALL_ROWS, d)), _full((nb, N_MOD * d)), _full((nb, n)), _full((d, nb))],
        out_specs=[_full((SMALL_ROWS, d)), _full((d, n)), _full((1, N_MOD * d))],
        out_shape=[jax.ShapeDtypeStruct((SMALL_ROWS, d), F32), jax.ShapeDtypeStruct((d, n), F32),
                   jax.ShapeDtypeStruct((1, N_MOD * d), F32)],
        compiler_params=_params(("arbitrary",)),
    )(gathered, dmod_all, dmod_shard, c_all_t)


def _row_tile(r, c, budget=1 << 21):
    if r * c * 4 <= budget or r % 16:
        return r
    best = 16
    for tr in range(16, r + 1, 16):
        if r % tr == 0 and tr * c * 4 <= budget:
            best = tr
    return best


def _cast_into(w, chip, col_kind, *, name):
    r, c = w.shape
    tr = _row_tile(r, c)

    def body(chip_ref, w_ref, o_ref):
        o_ref[...] = w_ref[...].astype(BF16)

    if col_kind:
        out_spec = pl.BlockSpec((tr, c), lambda i, chip_ref: (i, chip_ref[0]))
        out_shape = jax.ShapeDtypeStruct((r, c * N_CHIPS), BF16)
    else:
        out_spec = pl.BlockSpec((tr, c), lambda i, chip_ref: (chip_ref[0] * (r // tr) + i, 0))
        out_shape = jax.ShapeDtypeStruct((r * N_CHIPS, c), BF16)
    return _pcall(body, name=name, grid=(r // tr,), in_specs=[pl.BlockSpec((tr, c), lambda i, chip_ref: (i, 0))],
                  out_specs=out_spec, out_shape=out_shape, args=(w,), prefetch=chip)


def _adamw(w, g, m, v, *, name, return_grad=False, comm=None):
    r, c = w.shape
    tr = _row_tile(r, c)
    c1 = 1.0 - ADAM_B1 ** ADAM_STEP
    c2 = 1.0 - ADAM_B2 ** ADAM_STEP
    n_out = 4 if return_grad else 3

    def body(w_ref, g_ref, m_ref, v_ref, d_ref, nm_ref, nv_ref, *g_out):
        gv = g_ref[...]
        m2 = ADAM_B1 * m_ref[...] + (1.0 - ADAM_B1) * gv
        v2 = ADAM_B2 * v_ref[...] + (1.0 - ADAM_B2) * (gv * gv)
        d_ref[...] = -ADAM_LR * ((m2 / c1) / (jnp.sqrt(v2 / c2) + ADAM_EPS) + ADAM_WD * w_ref[...])
        nm_ref[...] = m2
        nv_ref[...] = v2
        for o in g_out:
            o[...] = gv

    spec = pl.BlockSpec((tr, c), lambda i: (i, 0))
    sh = jax.ShapeDtypeStruct((r, c), F32)
    return _pcall(body, name=name, grid=(r // tr,), in_specs=[spec] * 4, out_specs=[spec] * n_out,
                  out_shape=[sh] * n_out, args=(w, g, m, v), comm=comm)


def _adamw_small(ws, gs, ms, vs, *, name):
    n = len(ws)
    c1 = 1.0 - ADAM_B1 ** ADAM_STEP
    c2 = 1.0 - ADAM_B2 ** ADAM_STEP

    def body(*refs):
        w_refs, g_refs, m_refs, v_refs = (refs[k * n:(k + 1) * n] for k in range(4))
        out = refs[4 * n:]
        for p in range(n):
            gv = g_refs[p][...]
            m2 = ADAM_B1 * m_refs[p][...] + (1.0 - ADAM_B1) * gv
            v2 = ADAM_B2 * v_refs[p][...] + (1.0 - ADAM_B2) * (gv * gv)
            out[3 * p][...] = -ADAM_LR * ((m2 / c1) / (jnp.sqrt(v2 / c2) + ADAM_EPS) + ADAM_WD * w_refs[p][...])
            out[3 * p + 1][...] = m2
            out[3 * p + 2][...] = v2

    specs = [_full(a.shape) for a in ws]
    res = pl.pallas_call(
        body, name=name, grid=(1,), in_specs=specs * 4,
        out_specs=[s for s in specs for _ in range(3)],
        out_shape=[jax.ShapeDtypeStruct(a.shape, F32) for a in ws for _ in range(3)],
        compiler_params=_params(("arbitrary",)),
    )(*ws, *gs, *ms, *vs)
    return [tuple(res[3 * p:3 * p + 3]) for p in range(n)]


def _sum_final(pos, s32, recv, *, col_kind, n_shard, name, comm=None):
    def body(pos_ref, s_ref, r_ref, o_ref):
        total = ((s_ref[0] + r_ref[0].astype(F32)) + r_ref[1].astype(F32)) + r_ref[2].astype(F32)
        if col_kind:
            o_ref[0] = total
        else:
            o_ref[...] = total

    if col_kind:
        rows, cols = s32.shape[1], n_shard
        tr = _row_tile(rows, cols)
        own = pl.BlockSpec((1, tr, cols), lambda i, pos: (0, i, 2 * pos[0] + pos[1]))
        out_spec = pl.BlockSpec((1, tr, cols), lambda i, pos: (pos[2], i, 0))
        out_shape = jax.ShapeDtypeStruct((2, rows, cols), F32)
    else:
        rows, cols = n_shard, s32.shape[2]
        tr = _row_tile(rows, cols)
        own = pl.BlockSpec((1, tr, cols), lambda i, pos: (0, (2 * pos[0] + pos[1]) * (rows // tr) + i, 0))
        out_spec = pl.BlockSpec((tr, cols), lambda i, pos: (i, pos[2]))
        out_shape = jax.ShapeDtypeStruct((rows, 2 * cols), F32)
    return _pcall(
        body, name=name, grid=(rows // tr,),
        in_specs=[own, pl.BlockSpec((3, tr, cols), lambda i, pos: (0, i, 0))], out_specs=out_spec,
        out_shape=out_shape, args=(s32, recv), prefetch=pos, comm=comm)


def _position():
    return lax.axis_index("x"), lax.axis_index("y"), lax.axis_index("c")


def _allgather8(x_shard, *, name, comm=None):
    m_per, n = x_shard.shape
    nci, nco = (0, 0) if comm is None else (len(comm.inputs), len(comm.out_shapes))

    def body(*refs):
        x_ref, refs = refs[0], refs[1:]
        cin, refs = refs[:nci], refs[nci:]
        out_ref, refs = refs[0], refs[1:]
        cout, refs = refs[:nco], refs[nco:]
        (send_sems, recv_sems, local_sem), csems = refs[:3], refs[3:]
        x, y, c = _position()
        me, sibling = (x, y, c), (x, y, 1 - c)
        chips = [(1 - x, y), (x, 1 - y), (1 - x, 1 - y)]

        def rows(px, py, pc):
            return out_ref.at[pl.ds((4 * px + 2 * py + pc) * m_per, m_per), :]

        def copy(k, block, to, src=None):
            return pltpu.make_async_remote_copy(
                src_ref=rows(*block) if src is None else src, dst_ref=rows(*block),
                send_sem=send_sems.at[k], recv_sem=recv_sems.at[k], device_id=to, device_id_type=MESH)

        mine = pltpu.make_async_copy(x_ref, rows(*me), local_sem)
        mine.start()
        first = [copy(0, me, sibling, src=x_ref)]
        first += [copy(1 + j, me, (*chip, c), src=x_ref) for j, chip in enumerate(chips)]
        for cp in first:
            cp.start()
        if comm is not None:
            comm.start(cin, cout, csems)
        passed = [copy(4 + j, (*chip, c), sibling) for j, chip in enumerate(chips)]
        for j, chip in enumerate(chips):
            copy(1 + j, (*chip, c), me).wait_recv()
            passed[j].start()
        copy(0, sibling, me).wait_recv()
        for j, chip in enumerate(chips):
            copy(4 + j, (*chip, 1 - c), me).wait_recv()
        for cp in first + passed:
            cp.wait_send()
        mine.wait()
        if comm is not None:
            comm.middle(cin, cout, csems)
            comm.late(cin, cout, csems)
            comm.finish(cin, cout, csems)

    vmem = pl.BlockSpec(memory_space=pltpu.VMEM)
    sems = [pltpu.SemaphoreType.DMA((7,)), pltpu.SemaphoreType.DMA((7,)), pltpu.SemaphoreType.DMA]
    out = jax.ShapeDtypeStruct((N_DEV * m_per, n), x_shard.dtype)
    if comm is None:
        return pl.pallas_call(body, name=name, out_shape=out, in_specs=[vmem], out_specs=vmem,
                              scratch_shapes=sems)(x_shard)
    res = pl.pallas_call(
        body, name=name, out_shape=[out] + list(comm.out_shapes), in_specs=[vmem] + [ANY_SPEC] * nci,
        out_specs=[vmem] + [ANY_SPEC] * nco, scratch_shapes=sems + list(comm.sems),
        input_output_aliases={1 + i: 1 + o for i, o in comm.aliases.items()})(x_shard, *comm.inputs)
    return res[0], list(res[1:])


def _peer_chips(x, y):
    return [(1 - x, y), (x, 1 - y), (1 - x, 1 - y)]


class _GatherJob:
    def __init__(self, pieces):
        self.pieces = pieces
        n_p = len(pieces)
        self.inputs = [p[0] for p in pieces]
        self.out_shapes = [jax.ShapeDtypeStruct(p[0].shape, p[0].dtype) for p in pieces]
        for buf, col_kind, r0, nr in pieces:
            half_rows = buf.shape[0] // (2 if col_kind else 2 * N_CHIPS)
            assert r0 % 16 == 0 and nr % 16 == 0 and nr >= 32 and r0 + nr <= half_rows, (buf.shape, r0, nr)
        self.aliases = {p: p for p in range(n_p)}
        dma = pltpu.SemaphoreType.DMA
        self.sems = [dma((2 * n_p,))] * 4 + [dma((4 * n_p,))] * 2

    def _region(self, cout, p, chip_idx, half, part=None):
        buf, col_kind, r0, nr = self.pieces[p]
        first = -(-nr // 32) * 16
        if part == 0:
            nr = first
        elif part == 1:
            r0, nr = r0 + first, nr - first
        if col_kind:
            n = buf.shape[1] // N_CHIPS
            return cout[p].at[pl.ds(half * (buf.shape[0] // 2) + r0, nr), pl.ds(chip_idx * n, n)]
        n = buf.shape[0] // N_CHIPS
        return cout[p].at[pl.ds(chip_idx * n + half * (n // 2) + r0, nr), :]

    def _copies(self, cout, sems):
        send1, recv1, send2, recv2, fsend, frecv = sems
        x, y, c = _position()
        k = 2 * x + y
        sibling = (x, y, 1 - c)
        x_nbr, y_nbr, diag = _peer_chips(x, y)
        chip_of = lambda ch: 2 * ch[0] + ch[1]

        def remote(region, ssem, rsem, to):
            return pltpu.make_async_remote_copy(src_ref=region, dst_ref=region, send_sem=ssem, recv_sem=rsem,
                                                device_id=to, device_id_type=MESH)

        hop1, arrived1, hop2, arrived2, fwds, fwd_arrived = [], [], [], [], [], []
        for p in range(len(self.pieces)):
            for j, nbr in enumerate((x_nbr, y_nbr)):
                i1 = 2 * p + j
                hop1.append(remote(self._region(cout, p, k, c), send1.at[i1], recv1.at[i1], (*nbr, c)))
                arrived1.append(remote(self._region(cout, p, chip_of(nbr), c), send1.at[i1], recv1.at[i1], (*nbr, c)))
            hop2.append(remote(self._region(cout, p, chip_of(x_nbr), c, 0), send2.at[2 * p], recv2.at[2 * p],
                               (*y_nbr, c)))
            hop2.append(remote(self._region(cout, p, chip_of(y_nbr), c, 1), send2.at[2 * p + 1], recv2.at[2 * p + 1],
                               (*x_nbr, c)))
            arrived2.append(remote(self._region(cout, p, chip_of(diag), c, 0), send2.at[2 * p], recv2.at[2 * p],
                                   (*y_nbr, c)))
            arrived2.append(remote(self._region(cout, p, chip_of(diag), c, 1), send2.at[2 * p + 1],
                                   recv2.at[2 * p + 1], (*x_nbr, c)))
            landed = [(chip_of(x_nbr), None), (chip_of(y_nbr), None), (chip_of(diag), 0), (chip_of(diag), 1)]
            for q, (chip_idx, part) in enumerate(landed):
                i3 = 4 * p + q
                fwds.append(remote(self._region(cout, p, chip_idx, c, part), fsend.at[i3], frecv.at[i3], sibling))
                fwd_arrived.append(remote(self._region(cout, p, chip_idx, 1 - c, part), fsend.at[i3], frecv.at[i3],
                                          sibling))
        return hop1, arrived1, hop2, arrived2, fwds, fwd_arrived

    def start(self, cin, cout, sems):
        for cp in self._copies(cout, sems)[0]:
            cp.start()

    def middle(self, cin, cout, sems):
        _, arrived1, hop2, _, fwds, _ = self._copies(cout, sems)
        for p in range(len(self.pieces)):
            for j in range(2):
                arrived1[2 * p + j].wait_recv()
                hop2[2 * p + j].start()
                fwds[4 * p + j].start()

    def late(self, cin, cout, sems):
        _, _, _, arrived2, fwds, _ = self._copies(cout, sems)
        for p in range(len(self.pieces)):
            for j in range(2):
                arrived2[2 * p + j].wait_recv()
                fwds[4 * p + 2 + j].start()

    def finish(self, cin, cout, sems):
        hop1, _, hop2, _, fwds, fwd_arrived = self._copies(cout, sems)
        for cp in fwd_arrived:
            cp.wait_recv()
        for cp in hop1 + hop2 + fwds:
            cp.wait_send()


class _PairedJob:
    aliases = {}

    def start(self, cin, cout, sems):
        for cp in self._copies(cin, cout, sems):
            cp.start()

    def middle(self, cin, cout, sems):
        pass

    late = middle

    def finish(self, cin, cout, sems):
        copies = self._copies(cin, cout, sems)
        for cp in copies:
            cp.wait_recv()
        for cp in copies:
            cp.wait_send()


class _ExchangeJob(_PairedJob):
    def __init__(self, s16, kinds, sizes):
        self.inputs, self.kinds, self.sizes = list(s16), list(kinds), list(sizes)
        self.out_shapes = [jax.ShapeDtypeStruct((3, s.shape[1], n) if kd else (3, n, s.shape[2]), s.dtype)
                           for s, kd, n in zip(s16, kinds, sizes)]
        self.sems = [pltpu.SemaphoreType.DMA((3 * len(s16),)), pltpu.SemaphoreType.DMA((3 * len(s16),))]

    def _copies(self, cin, cout, sems):
        send_sems, recv_sems = sems
        x, y, c = _position()
        copies = []
        for p, src_ref in enumerate(cin):
            for j, chip in enumerate(_peer_chips(x, y)):
                kk = 2 * chip[0] + chip[1]
                n = self.sizes[p]
                src = src_ref.at[0, :, pl.ds(kk * n, n)] if self.kinds[p] else src_ref.at[0, pl.ds(kk * n, n), :]
                copies.append(pltpu.make_async_remote_copy(
                    src_ref=src, dst_ref=cout[p].at[j], send_sem=send_sems.at[3 * p + j],
                    recv_sem=recv_sems.at[3 * p + j], device_id=(*chip, c), device_id_type=MESH))
        return copies


class _ShareJob:
    def __init__(self, halves):
        self.inputs = list(halves)
        self.out_shapes = [jax.ShapeDtypeStruct(h.shape, h.dtype) for h in halves]
        self.aliases = {p: p for p in range(len(halves))}
        self.sems = [pltpu.SemaphoreType.DMA((len(halves),)), pltpu.SemaphoreType.DMA((len(halves),))]

    def _copies(self, cout, sems, half):
        send_sems, recv_sems = sems
        x, y, c = _position()
        h = c if half == "mine" else 1 - c

        def region(o):
            if len(o.shape) == 3:
                return o.at[h]
            hc = o.shape[1] // 2
            return o.at[:, pl.ds(h * hc, hc)]

        return [pltpu.make_async_remote_copy(
            src_ref=region(o), dst_ref=region(o), send_sem=send_sems.at[p], recv_sem=recv_sems.at[p],
            device_id=(x, y, 1 - c), device_id_type=MESH) for p, o in enumerate(cout)]

    def start(self, cin, cout, sems):
        for cp in self._copies(cout, sems, "mine"):
            cp.start()

    def middle(self, cin, cout, sems):
        pass

    late = middle

    def finish(self, cin, cout, sems):
        for cp in self._copies(cout, sems, "theirs"):
            cp.wait_recv()
        for cp in self._copies(cout, sems, "mine"):
            cp.wait_send()


class _MultiJob:
    def __init__(self, jobs):
        self.jobs = jobs
        self.inputs = [a for j in jobs for a in j.inputs]
        self.out_shapes = [s for j in jobs for s in j.out_shapes]
        self.sems = [s for j in jobs for s in j.sems]
        self.aliases = {}
        i0 = o0 = 0
        for j in jobs:
            for i, o in j.aliases.items():
                self.aliases[i0 + i] = o0 + o
            i0 += len(j.inputs)
            o0 += len(j.out_shapes)

    def _parts(self, cin, cout, sems):
        i0 = o0 = s0 = 0
        for j in self.jobs:
            ni, no, ns = len(j.inputs), len(j.out_shapes), len(j.sems)
            yield j, cin[i0:i0 + ni], cout[o0:o0 + no], sems[s0:s0 + ns]
            i0, o0, s0 = i0 + ni, o0 + no, s0 + ns

    def start(self, cin, cout, sems):
        for j, a, b, s in self._parts(cin, cout, sems):
            j.start(a, b, s)

    def middle(self, cin, cout, sems):
        for j, a, b, s in self._parts(cin, cout, sems):
            j.middle(a, b, s)

    def late(self, cin, cout, sems):
        for j, a, b, s in self._parts(cin, cout, sems):
            j.late(a, b, s)

    def finish(self, cin, cout, sems):
        for j, a, b, s in self._parts(cin, cout, sems):
            j.finish(a, b, s)


def _rope_tables(positions):
    half = ROT_DIM // 2
    inv_freq = jnp.power(jnp.float32(ROPE_THETA), -jnp.arange(0, ROT_DIM, 2, dtype=F32) / ROT_DIM)
    inv_head = jnp.concatenate([inv_freq, inv_freq, jnp.zeros((HEAD_DIM - ROT_DIM,), F32)])
    inv_lane = jnp.concatenate([inv_head] * (LANE // HEAD_DIM))
    ang = positions.astype(F32).reshape(-1)[:, None] * inv_lane[None, :]
    sin = jnp.sin(ang)
    dim = jnp.arange(LANE) % HEAD_DIM
    return jnp.cos(ang), jnp.where(dim < half, -sin, 0.0), jnp.where(dim >= half, sin, 0.0)


def kernel(x, c, positions, w_ada, b_ada, ffn1_w_gate_up, ffn1_w_down, ln1_g, ln1_b, w_in, conv_w, attn_sinks, w_out, ln2_g, ln2_b, ffn2_w_gate_up, ffn2_w_down, ln3_g, ln3_b, loss_target, m_w_ada, m_b_ada, m_ffn1_w_gate_up, m_ffn1_w_down, m_ln1_g, m_ln1_b, m_w_in, m_conv_w, m_attn_sinks, m_w_out, m_ln2_g, m_ln2_b, m_ffn2_w_gate_up, m_ffn2_w_down, m_ln3_g, m_ln3_b, v_w_ada, v_b_ada, v_ffn1_w_gate_up, v_ffn1_w_down, v_ln1_g, v_ln1_b, v_w_in, v_conv_w, v_attn_sinks, v_w_out, v_ln2_g, v_ln2_b, v_ffn2_w_gate_up, v_ffn2_w_down, v_ln3_g, v_ln3_b):
    d = D_MODEL
    nb, seq, _ = x.shape
    t = nb * seq
    f = ffn1_w_down.shape[1] * N_CHIPS
    ax, ay, ac = _position()
    chip = 2 * ax + ay
    dev = 2 * chip + ac
    pos = jnp.stack([ax, ay, ac]).astype(jnp.int32)

    x2 = x.reshape(t, d)
    tgt2 = loss_target.reshape(t, d)
    ln1 = jnp.concatenate([ln1_g, ln1_b], axis=0)
    ln2 = jnp.concatenate([ln2_g, ln2_b], axis=0)
    ln3 = jnp.concatenate([ln3_g, ln3_b], axis=0)
    sinks = attn_sinks.reshape(N_Q_HEADS)
    cos_t, sa_t, sb_t = _rope_tables(positions)

    gu_cuts = [0, 176, 352, d // 2]
    gu_part = lambda buf, s: (buf, True, gu_cuts[s], gu_cuts[s + 1] - gu_cuts[s])
    chip_arr = jnp.reshape(chip, (1,)).astype(jnp.int32)
    b_gu1 = _cast_into(ffn1_w_gate_up[0], chip_arr, True, name="cast_gu1")

    n_ada = w_ada.shape[2]
    c_all, (b_gu1,) = _allgather8(c.reshape(nb * d // LANE, LANE), name="gather_c", comm=_GatherJob([gu_part(b_gu1, 0)]))
    c_all = c_all.reshape(N_DEV * nb, d)
    b_shard = lax.dynamic_slice(b_ada, (0, chip * n_ada), (1, n_ada))
    later_shards = [(ffn1_w_down[0], False), (w_in[0].T, False), (w_out[0], False), (ffn2_w_gate_up[0], True),
                    (ffn2_w_down[0], False)]
    mod_part, (b_d1, b_in, b_out, b_gu2, b_d2), (b_gu1,) = _ada_fwd(
        c_all, w_ada[0], b_shard, chip_arr, later_shards, name="ada_fwd", comm=_GatherJob([gu_part(b_gu1, 1)]))
    conv_rows = jnp.pad(conv_w[0], ((0, 5), (0, n_ada - conv_w.shape[2])))
    part = jnp.concatenate([mod_part, conv_rows], axis=0)
    parts, (wgu1,) = _allgather8(part, name="gather_mod", comm=_GatherJob([gu_part(b_gu1, 2)]))
    parts = parts.reshape(N_DEV, N_DEV * nb + 8, n_ada)
    mod_all = jnp.concatenate([parts[2 * k, :N_DEV * nb, :] for k in range(N_CHIPS)], axis=1)
    mod = lax.dynamic_slice(mod_all, (dev * nb, 0), (nb, N_MOD * d)).reshape(nb, N_MOD, d)
    cw_full = jnp.concatenate([parts[2 * k, N_DEV * nb:, :conv_w.shape[2]] for k in range(N_CHIPS)], axis=1)

    n_gu, n_d, n_in, n_out = (ffn1_w_gate_up.shape[2], ffn1_w_down.shape[1], w_in.shape[2], w_out.shape[1])

    def whole(buf, col_kind):
        return (buf, col_kind, 0, buf.shape[0] // (2 if col_kind else 2 * N_CHIPS))

    (h1, a1, dact1), (wd1, wout) = _ffn_up(x2, ln1, mod, wgu1, seq=seq, sc_idx=1, sh_idx=0, use_ln=False,
                                         name="ffn1_up", comm=_GatherJob([whole(b_d1, False), whole(b_out, False)]))
    (f1, xhat1, rstd1), (win_t,) = _ffn_down_ln(a1, wd1, x2, ln1, mod, seq=seq, gate_idx=2, use_ln=False,
                                                name="ffn1_down", comm=_GatherJob([whole(b_in, False)]))
    (h2, q, k, v, ubc), (b_gu2,) = _in_proj(
        xhat1, ln1, mod, win_t, cos_t, sa_t, sb_t, seq=seq, sc_idx=4, sh_idx=3, name="in_proj",
        comm=_GatherJob([gu_part(b_gu2, 0)]))
    attn, (b_gu2,) = _attention(q, k, v, sinks, seq=seq, name="attention", comm=_GatherJob([gu_part(b_gu2, 1)]))
    (mixin, mix, xhat2, rstd2), (wgu2,) = _out_proj(
        attn, ubc, cw_full, wout, xhat1, ln1, mod, seq=seq, gate_idx=5, name="out_proj",
        comm=_GatherJob([gu_part(b_gu2, 2)]))
    (h3, a3, dact3), (wd2,) = _ffn_up(xhat2, ln2, mod, wgu2, seq=seq, sc_idx=7, sh_idx=6, use_ln=True, name="ffn2_up",
                                    comm=_GatherJob([whole(b_d2, False)]))
    dr3, df3, loss_cols, dln3g, dln3b, dgate3 = _ffn_down_loss(
        a3, wd2, xhat2, ln2, mod, ln3, tgt2, seq=seq, gate_idx=8, name="ffn2_down_loss")

    dgu3 = _ffn_bwd_act(df3, wd2, dact3, seq=seq, name="ffn2_bwd_act")
    s32_d2, s16_d2 = _grad_chip_sum(pos, a3, df3, half_on_rows=False, name="grad_wd2")
    (s32_gu2, s16_gu2), (recv_d2,) = _grad_chip_sum(pos, h3, dgu3, half_on_rows=True, name="grad_wgu2",
                                                    comm=_ExchangeJob([s16_d2], [False], [n_d]))
    (dr2, dmix, dsc3, dsh3, dgate2, dln2g, dln2b), (recv_gu2,) = _bwd_in(
        dgu3, wgu2, dr3, xhat2, rstd2, ln2, mod, mix, seq=seq, w_is_nt=True, sc_idx=7, gate_idx=5,
        branch_scale=1.0, final=False, name="ffn2_bwd_in", comm=_ExchangeJob([s16_gu2], [True], [n_gu]))
    s32_out, s16_out = _grad_chip_sum(pos, mixin, dmix, half_on_rows=False, name="grad_wout")
    dmixin = _matmul_nt_bf16(dmix, wout, seq=seq, name="out_proj_bwd")
    (dq, dkp, dkc, dvp, dvc, dsink), (recv_out,) = _attention_bwd(
        q, k, v, dmixin, sinks, seq=seq, name="attention_bwd", comm=_ExchangeJob([s16_out], [False], [n_out]))
    dproj, dcw = _mix_bwd_assemble(
        dq, dkp, dkc, dvp, dvc, cos_t, sa_t, sb_t, dmixin, ubc, cw_full, seq=seq, name="mix_bwd")
    s32_in, s16_in = _grad_chip_sum(pos, dproj, h2, half_on_rows=False, name="grad_win")
    (dr1, df1, dsc2, dsh2, dgate1, dln1g, dln1b), (recv_in,) = _bwd_in(
        dproj, win_t, dr2, xhat1, rstd1, ln1, mod, f1, seq=seq, w_is_nt=False, sc_idx=4, gate_idx=2,
        branch_scale=0.5, final=False, name="in_proj_bwd", comm=_ExchangeJob([s16_in], [False], [n_in]))
    s32_d1, s16_d1 = _grad_chip_sum(pos, a1, df1, half_on_rows=False, name="grad_wd1")
    dgu1, (recv_d1,) = _ffn_bwd_act(df1, wd1, dact1, seq=seq, name="ffn1_bwd_act",
                                    comm=_ExchangeJob([s16_d1], [False], [n_d]))
    s32_gu1, s16_gu1 = _grad_chip_sum(pos, h1, dgu1, half_on_rows=True, name="grad_wgu1")

    def final_half(s32_, recv_, col_kind, n_shard, name_):
        return _sum_final(pos, s32_, recv_, col_kind=col_kind, n_shard=n_shard, name=name_)

    early = [final_half(s32_gu2, recv_gu2, True, n_gu, "sum_final_gu2"),
             final_half(s32_d2, recv_d2, False, n_d, "sum_final_d2"),
             final_half(s32_out, recv_out, False, n_out, "sum_final_out"),
             final_half(s32_in, recv_in, False, n_in, "sum_final_in"),
             final_half(s32_d1, recv_d1, False, n_d, "sum_final_d1")]
    (grad_x, dsc1, dsh1), (recv_gu1, full_gu2, full_d2, full_out, full_in, full_d1) = _bwd_in(
        dgu1, wgu1, dr1, x2, None, None, mod, None, seq=seq, w_is_nt=True, sc_idx=1, gate_idx=None,
        branch_scale=None, final=True, name="ffn1_bwd_in",
        comm=_MultiJob([_ExchangeJob([s16_gu1], [True], [n_gu]), _ShareJob(early)]))
    late = [final_half(s32_gu1, recv_gu1, True, n_gu, "sum_final_gu1")]

    dmod = jnp.concatenate([dsh1, dsc1, dgate1, dsh2, dsc2, dgate2, dsh3, dsc3, dgate3], axis=1)
    loss_row = jnp.sum(loss_cols, axis=1, keepdims=True) * (0.5 / d)
    lane_row = lambda a: jnp.pad(a, ((0, 0), (0, d - a.shape[1])))
    block = jnp.concatenate(
        [dmod.reshape(nb * N_MOD, d), dln1g, dln1b, dln2g, dln2b, dln3g, dln3b,
         lane_row(dcw[0:3, :]), lane_row(dsink[:, 0:1].reshape(1, N_Q_HEADS)), lane_row(loss_row)], axis=0)
    block = jnp.pad(block, ((0, SMALL_ROWS - block.shape[0]), (0, 0)))
    gathered, (full_gu1,) = _allgather8(block, name="gather_small", comm=_ShareJob(late))
    gathered = gathered.reshape(N_DEV, SMALL_ROWS, d)
    dmod_all = gathered[:, :nb * N_MOD, :].reshape(N_DEV * nb, N_MOD * d)
    dmod_shard = lax.dynamic_slice(dmod_all, (0, chip * n_ada), (N_DEV * nb, n_ada))
    small, g_w_ada, g_b_ada = _small_finish(gathered, dmod_all, dmod_shard, c_all.T, name="small_finish")
    r0 = nb * N_MOD
    loss = small[r0 + 10, 0]
    g_ln = [small[r0 + i:r0 + i + 1, :] for i in range(6)]
    g_cw_full = small[r0 + 6:r0 + 9, :CONV_WIDTH]
    g_conv = lax.dynamic_slice(g_cw_full, (0, chip * conv_w.shape[2]), (3, conv_w.shape[2]))
    g_sinks = small[r0 + 9:r0 + 10, :N_Q_HEADS]

    def flat2(a):
        return a.reshape(-1, a.shape[-1])

    def unhalve(a):
        return a.reshape(2 * a.shape[1], a.shape[2])

    results = {}

    def adamw(name_, w_, g_, m_, v_, shared=False):
        g2 = flat2(g_)
        dl, nm, nv, *g_out = _adamw(flat2(w_), g2, flat2(m_), flat2(v_), name="adamw_" + name_, return_grad=shared)
        results[name_] = tuple(a.reshape(w_.shape) for a in (*g_out, g2)[:1] + (dl, nm, nv))

    adamw("w_ada", w_ada, g_w_ada, m_w_ada, v_w_ada)
    adamw("ffn2_w_gate_up", ffn2_w_gate_up, unhalve(full_gu2), m_ffn2_w_gate_up, v_ffn2_w_gate_up, shared=True)
    adamw("ffn2_w_down", ffn2_w_down, full_d2, m_ffn2_w_down, v_ffn2_w_down, shared=True)
    adamw("w_out", w_out, full_out, m_w_out, v_w_out, shared=True)
    adamw("w_in", w_in, full_in.T, m_w_in, v_w_in)
    adamw("ffn1_w_gate_up", ffn1_w_gate_up, unhalve(full_gu1), m_ffn1_w_gate_up, v_ffn1_w_gate_up, shared=True)
    adamw("ffn1_w_down", ffn1_w_down, full_d1, m_ffn1_w_down, v_ffn1_w_down, shared=True)
    small_params = [("b_ada", b_ada, g_b_ada, m_b_ada, v_b_ada),
                    ("ln1_g", ln1_g, g_ln[0], m_ln1_g, v_ln1_g), ("ln1_b", ln1_b, g_ln[1], m_ln1_b, v_ln1_b),
                    ("ln2_g", ln2_g, g_ln[2], m_ln2_g, v_ln2_g), ("ln2_b", ln2_b, g_ln[3], m_ln2_b, v_ln2_b),
                    ("ln3_g", ln3_g, g_ln[4], m_ln3_g, v_ln3_g), ("ln3_b", ln3_b, g_ln[5], m_ln3_b, v_ln3_b),
                    ("conv_w", conv_w, g_conv, m_conv_w, v_conv_w),
                    ("attn_sinks", attn_sinks, g_sinks, m_attn_sinks, v_attn_sinks)]
    small_g = [flat2(g_) for _, _, g_, _, _ in small_params]
    small_res = _adamw_small([flat2(w_) for _, w_, _, _, _ in small_params], small_g,
                             [flat2(m_) for _, _, _, m_, _ in small_params],
                             [flat2(v_) for _, _, _, _, v_ in small_params], name="adamw_small")
    for (name_, w_, _, _, _), g2, res in zip(small_params, small_g, small_res):
        results[name_] = tuple(a.reshape(w_.shape) for a in (g2, *res))
    order = ["w_ada", "b_ada", "ffn1_w_gate_up", "ffn1_w_down", "ln1_g", "ln1_b", "w_in", "conv_w", "attn_sinks",
             "w_out", "ln2_g", "ln2_b", "ffn2_w_gate_up", "ffn2_w_down", "ln3_g", "ln3_b"]
    return (loss, grad_x.reshape(x.shape), *[results[n_][0] for n_ in order], *[results[n_][1] for n_ in order],
            *[results[n_][2] for n_ in order], *[results[n_][3] for n_ in order])
```

```python
import jax
import jax.numpy as jnp
from jax import lax
from jax.experimental import pallas as pl
from jax.experimental.pallas import tpu as pltpu

F32 = jnp.float32
BF16 = jnp.bfloat16
MESH = pl.DeviceIdType.MESH

D_MODEL = 1024
HEAD_DIM = 64
ATTN_WIDTH = 512
CONV_WIDTH = 512
N_Q_HEADS = 8
N_KV_HEADS = 2
GQA_GROUP = 4
KV_WIDTH = 128
WINDOW = 128
BLOCK = 128
ROT_DIM = 16
ROPE_THETA = 500000.0
N_MOD = 9
LN_EPS = 1e-5
DN_ALPHA = 2.0 ** 0.25
IN_WIDTH = 2304
N_CHIPS = 4
N_DEV = 8
SMALL_ROWS = 32

ADAM_LR = 0.001
ADAM_B1 = 0.9
ADAM_B2 = 0.999
ADAM_EPS = 1e-08
ADAM_WD = 0.01
ADAM_STEP = 10

LANE = 128
HALO = 16
COL_CHUNK = 256
VMEM_LIMIT = 56 * 1024 * 1024


def _params(sem=None, vmem=True):
    return pltpu.CompilerParams(dimension_semantics=sem, vmem_limit_bytes=VMEM_LIMIT if vmem else None)


def _sigmoid(g):
    return 0.5 * jnp.tanh(0.5 * g) + 0.5


def _row_sum(v):
    return jnp.sum(v, axis=0, keepdims=True)


ROW_CHUNK = 16
EPILOGUE_UNROLL = 8


def _fold8(v):
    return v[0:8, :] + v[8:16, :]


def _row_chunk_loop(n_rows, step, init):
    per_iter = ROW_CHUNK * EPILOGUE_UNROLL
    assert n_rows % per_iter == 0, n_rows

    def body(it, carry):
        for s in range(EPILOGUE_UNROLL):
            start = pl.multiple_of(it * per_iter + s * ROW_CHUNK, ROW_CHUNK)
            carry = step(pl.ds(start, ROW_CHUNK), carry)
        return carry

    return lax.fori_loop(0, n_rows // per_iter, body, init)


def _ln_stats(r):
    mu = jnp.mean(r, axis=-1, keepdims=True)
    rc = r - mu
    var = jnp.mean(rc * rc, axis=-1, keepdims=True)
    rstd = lax.rsqrt(var + LN_EPS)
    return rc * rstd, rstd


def _ln_bwd(dxo, xhat, rstd, g):
    dxhat = dxo * g
    m1 = jnp.mean(dxhat, axis=-1, keepdims=True)
    m2 = jnp.mean(dxhat * xhat, axis=-1, keepdims=True)
    return rstd * (dxhat - m1 - xhat * m2)


def _dot_nt(a, b):
    return lax.dot_general(a, b, (((1,), (1,)), ((), ())), preferred_element_type=F32)


def _dot_tn(a, b):
    return lax.dot_general(a, b, (((0,), (0,)), ((), ())), preferred_element_type=F32)


def _full(shape):
    nd = len(shape)
    return pl.BlockSpec(shape, lambda *_: (0,) * nd)


def _resident(shape):
    nd = len(shape)
    return pl.BlockSpec(shape, lambda *_: (0,) * nd, pipeline_mode=pl.Buffered(1))


ANY_SPEC = pl.BlockSpec(memory_space=pl.ANY)


def _pcall(body, *, name, grid, in_specs, out_specs, out_shape, args, scratch_shapes=(), comm=None, prefetch=None):
    single = not isinstance(out_shape, (list, tuple))
    out_specs = [out_specs] if single else list(out_specs)
    out_shape = [out_shape] if single else list(out_shape)
    in_specs = list(in_specs)
    scratch_shapes = list(scratch_shapes)
    sem = ("arbitrary",) * len(grid)
    n_pre = 0 if prefetch is None else 1
    pre_args = () if prefetch is None else (prefetch,)

    def call(fn, ins_, outs_, shapes_, scratch_, aliases_, operands):
        if prefetch is None:
            return pl.pallas_call(fn, name=name, grid=grid, in_specs=ins_, out_specs=outs_, out_shape=shapes_,
                                  scratch_shapes=scratch_, input_output_aliases=aliases_,
                                  compiler_params=_params(sem))(*operands)
        spec = pltpu.PrefetchScalarGridSpec(num_scalar_prefetch=1, grid=grid, in_specs=ins_, out_specs=outs_,
                                            scratch_shapes=scratch_)
        return pl.pallas_call(fn, name=name, grid_spec=spec, out_shape=shapes_,
                              input_output_aliases={n_pre + i: o for i, o in aliases_.items()},
                              compiler_params=_params(sem))(*pre_args, *operands)

    if comm is None:
        res = call(body, in_specs, out_specs, out_shape, scratch_shapes, {}, args)
        return res[0] if single else res
    n_in, n_out, n_scr = len(in_specs), len(out_specs), len(scratch_shapes)
    nci, nco = len(comm.inputs), len(comm.out_shapes)
    n_steps = 1
    for g in grid:
        n_steps *= g
    staged = n_steps >= 8
    middle_step = (n_steps * 5) // 8 - 1
    late_step = n_steps - 1 - max(1, n_steps // 8)

    def wrapped(*refs):
        pre, refs = refs[:n_pre], refs[n_pre:]
        ins, refs = refs[:n_in], refs[n_in:]
        cin, refs = refs[:nci], refs[nci:]
        outs, refs = refs[:n_out], refs[n_out:]
        cout, refs = refs[:nco], refs[nco:]
        scr, csems = refs[:n_scr], refs[n_scr:]
        step = pl.program_id(0)
        for ax in range(1, len(grid)):
            step = step * grid[ax] + pl.program_id(ax)

        @pl.when(step == 0)
        def _():
            comm.start(cin, cout, csems)

        body(*pre, *ins, *outs, *scr)

        if staged:
            @pl.when(step == middle_step)
            def _():
                comm.middle(cin, cout, csems)

            @pl.when(step == late_step)
            def _():
                comm.late(cin, cout, csems)

        @pl.when(step == n_steps - 1)
        def _():
            if not staged:
                comm.middle(cin, cout, csems)
                comm.late(cin, cout, csems)
            comm.finish(cin, cout, csems)

    res = call(wrapped, in_specs + [ANY_SPEC] * nci, out_specs + [ANY_SPEC] * nco,
               out_shape + list(comm.out_shapes), scratch_shapes + list(comm.sems),
               {n_in + i: n_out + o for i, o in comm.aliases.items()}, (*args, *comm.inputs))
    main = res[:n_out]
    return (main[0] if single else main), list(res[n_out:])


def _ffn_up(xin, lnp, mod, w, *, seq, sc_idx, sh_idx, use_ln, name, comm=None):
    t, d = xin.shape
    f = w.shape[1] // 2
    tm = min(512, seq)
    tpb = seq // tm
    ch = min(COL_CHUNK, f)

    def body(x_ref, ln_ref, mod_ref, w_ref, h_ref, a_ref, dact_ref):
        x = x_ref[...]
        if use_ln:
            x = x * ln_ref[0:1, :] + ln_ref[1:2, :]
        h = x * (1.0 + mod_ref[0, sc_idx:sc_idx + 1, :]) + mod_ref[0, sh_idx:sh_idx + 1, :]
        hb = h.astype(BF16)
        h_ref[...] = hb
        for j in range(f // ch):
            g = jnp.dot(hb, w_ref[:, j * ch:(j + 1) * ch], preferred_element_type=F32)
            u = jnp.dot(hb, w_ref[:, f + j * ch:f + (j + 1) * ch], preferred_element_type=F32)
            s = _sigmoid(g)
            silu = g * s
            a_ref[:, j * ch:(j + 1) * ch] = (silu * u).astype(BF16)
            dact_ref[:, j * ch:(j + 1) * ch] = (u * (s + silu * (1.0 - s))).astype(BF16)
            dact_ref[:, f + j * ch:f + (j + 1) * ch] = silu.astype(BF16)

    return _pcall(
        body, name=name, grid=(t // tm,),
        in_specs=[pl.BlockSpec((tm, d), lambda i: (i, 0)), _full((2, d)),
                  pl.BlockSpec((1, N_MOD, d), lambda i: (i // tpb, 0, 0)), _resident((d, 2 * f))],
        out_specs=[pl.BlockSpec((tm, d), lambda i: (i, 0)), pl.BlockSpec((tm, f), lambda i: (i, 0)),
                   pl.BlockSpec((tm, 2 * f), lambda i: (i, 0))],
        out_shape=[jax.ShapeDtypeStruct((t, d), BF16), jax.ShapeDtypeStruct((t, f), BF16),
                   jax.ShapeDtypeStruct((t, 2 * f), BF16)],
        args=(xin, lnp, mod, w), comm=comm)


def _ffn_down_ln(a, wd, xin, lnp_in, mod, *, seq, gate_idx, use_ln, name, comm=None):
    t, f = a.shape
    d = wd.shape[1]
    tm = min(512, seq)
    tpb = seq // tm

    def body(a_ref, wd_ref, x_ref, ln_ref, mod_ref, f_ref, xhat_ref, rstd_ref, acc):
        av = a_ref[...]
        for j in range(d // COL_CHUNK):
            acc[:, j * COL_CHUNK:(j + 1) * COL_CHUNK] = jnp.dot(
                av, wd_ref[:, j * COL_CHUNK:(j + 1) * COL_CHUNK], preferred_element_type=F32)
        scale = 0.5 * (1.0 + mod_ref[0, gate_idx:gate_idx + 1, :])

        fo = acc[...]
        x = x_ref[...]
        if use_ln:
            x = x * ln_ref[0:1, :] + ln_ref[1:2, :]
        xhat, rstd = _ln_stats(DN_ALPHA * x + scale * fo)
        f_ref[...] = fo.astype(BF16)
        xhat_ref[...] = xhat
        rstd_ref[...] = rstd

    return _pcall(
        body, name=name, grid=(t // tm,),
        in_specs=[pl.BlockSpec((tm, f), lambda i: (i, 0)), _resident((f, d)),
                  pl.BlockSpec((tm, d), lambda i: (i, 0)), _full((2, d)),
                  pl.BlockSpec((1, N_MOD, d), lambda i: (i // tpb, 0, 0))],
        out_specs=[pl.BlockSpec((tm, d), lambda i: (i, 0)), pl.BlockSpec((tm, d), lambda i: (i, 0)),
                   pl.BlockSpec((tm, 1), lambda i: (i, 0))],
        out_shape=[jax.ShapeDtypeStruct((t, d), BF16), jax.ShapeDtypeStruct((t, d), F32),
                   jax.ShapeDtypeStruct((t, 1), F32)],
        scratch_shapes=[pltpu.VMEM((tm, d), F32)],
        args=(a, wd, xin, lnp_in, mod), comm=comm)


def _ffn_down_loss(a, wd, xhat_in, lnp_in, mod, lnp_out, tgt, *, seq, gate_idx, name):
    t, f = a.shape
    d = wd.shape[1]
    nb = t // seq
    tm = min(512, seq)
    tpb = seq // tm

    def body(a_ref, wd_ref, x_ref, lnin_ref, mod_ref, lnout_ref, tgt_ref,
             dr_ref, df_ref, loss_ref, dg_ref, db_ref, dgate_ref, acc):
        i = pl.program_id(0)
        av = a_ref[...]
        for j in range(d // COL_CHUNK):
            acc[:, j * COL_CHUNK:(j + 1) * COL_CHUNK] = jnp.dot(
                av, wd_ref[:, j * COL_CHUNK:(j + 1) * COL_CHUNK], preferred_element_type=F32)
        scale = 0.5 * (1.0 + mod_ref[0, gate_idx:gate_idx + 1, :])
        ag_in, ab_in = DN_ALPHA * lnin_ref[0:1, :], DN_ALPHA * lnin_ref[1:2, :]
        g_out, b_out = lnout_ref[0:1, :], lnout_ref[1:2, :]
        g_over_d = g_out * (1.0 / d)

        def chunk(rows, carry):
            s_loss, s_dg, s_db, s_gate = carry
            fo = acc[rows, :]
            xhat, rstd = _ln_stats(x_ref[rows, :] * ag_in + ab_in + scale * fo)
            e = xhat * g_out + b_out - tgt_ref[rows, :]
            dr = _ln_bwd(e, xhat, rstd, g_over_d)
            dr_ref[rows, :] = dr
            df_ref[rows, :] = (scale * dr).astype(BF16)
            return s_loss + _fold8(e * e), s_dg + _fold8(e * xhat), s_db + _fold8(e), s_gate + _fold8(fo * dr)

        zero = jnp.zeros((8, d), F32)
        s_loss, s_dg, s_db, s_gate = _row_chunk_loop(tm, chunk, (zero, zero, zero, zero))
        s_dg, s_db, s_gate = s_dg * (1.0 / d), s_db * (1.0 / d), s_gate * 0.5

        @pl.when(i == 0)
        def _():
            loss_ref[...] = jnp.zeros_like(loss_ref)
            dg_ref[...] = jnp.zeros_like(dg_ref)
            db_ref[...] = jnp.zeros_like(db_ref)

        @pl.when(i % tpb == 0)
        def _():
            dgate_ref[...] = jnp.zeros_like(dgate_ref)

        loss_ref[...] += _row_sum(s_loss)
        dg_ref[...] += _row_sum(s_dg)
        db_ref[...] += _row_sum(s_db)
        dgate_ref[0] += _row_sum(s_gate)

    return pl.pallas_call(
        body, name=name, grid=(t // tm,), scratch_shapes=[pltpu.VMEM((tm, d), F32)],
        in_specs=[pl.BlockSpec((tm, f), lambda i: (i, 0)), _resident((f, d)),
                  pl.BlockSpec((tm, d), lambda i: (i, 0)), _full((2, d)),
                  pl.BlockSpec((1, N_MOD, d), lambda i: (i // tpb, 0, 0)), _full((2, d)),
                  pl.BlockSpec((tm, d), lambda i: (i, 0))],
        out_specs=[pl.BlockSpec((tm, d), lambda i: (i, 0)), pl.BlockSpec((tm, d), lambda i: (i, 0)),
                   _full((1, d)), _full((1, d)), _full((1, d)),
                   pl.BlockSpec((1, 1, d), lambda i: (i // tpb, 0, 0))],
        out_shape=[jax.ShapeDtypeStruct((t, d), F32), jax.ShapeDtypeStruct((t, d), BF16),
                   jax.ShapeDtypeStruct((1, d), F32), jax.ShapeDtypeStruct((1, d), F32),
                   jax.ShapeDtypeStruct((1, d), F32), jax.ShapeDtypeStruct((nb, 1, d), F32)],
        compiler_params=_params(("arbitrary",)),
    )(a, wd, xhat_in, lnp_in, mod, lnp_out, tgt)


def _rope(v, cos, sa, sb):
    return v * cos + pltpu.roll(v, LANE - ROT_DIM // 2, 1) * sa + pltpu.roll(v, ROT_DIM // 2, 1) * sb


def _rope_t(dy, cos, sa, sb):
    return dy * cos + pltpu.roll(dy * sa, ROT_DIM // 2, 1) + pltpu.roll(dy * sb, LANE - ROT_DIM // 2, 1)


def _in_proj(xhat, lnp, mod, w_t, cos, sa, sb, *, seq, sc_idx, sh_idx, name, comm=None):
    t, d = xhat.shape
    tm = min(512, seq)
    tpb = seq // tm
    n_conv = 3 * CONV_WIDTH

    def body(x_ref, ln_ref, mod_ref, w_ref, cos_ref, sa_ref, sb_ref, h_ref, q_ref, k_ref, v_ref, ubc_ref):
        x = x_ref[...] * ln_ref[0:1, :] + ln_ref[1:2, :]
        h = x * (1.0 + mod_ref[0, sc_idx:sc_idx + 1, :]) + mod_ref[0, sh_idx:sh_idx + 1, :]
        hb = h.astype(BF16)
        h_ref[...] = hb
        cos_t, sa_t, sb_t = cos_ref[...], sa_ref[...], sb_ref[...]
        for j in range(ATTN_WIDTH // COL_CHUNK):
            p = _dot_nt(hb, w_ref[j * COL_CHUNK:(j + 1) * COL_CHUNK, :])
            for s in range(COL_CHUNK // LANE):
                q_ref[:, j * COL_CHUNK + s * LANE:j * COL_CHUNK + (s + 1) * LANE] = _rope(
                    p[:, s * LANE:(s + 1) * LANE], cos_t, sa_t, sb_t).astype(BF16)
        p = _dot_nt(hb, w_ref[ATTN_WIDTH:ATTN_WIDTH + 2 * KV_WIDTH, :])
        k_ref[...] = _rope(p[:, 0:KV_WIDTH], cos_t, sa_t, sb_t).astype(BF16)
        v_ref[...] = p[:, KV_WIDTH:].astype(BF16)
        base = ATTN_WIDTH + 2 * KV_WIDTH
        for j in range(n_conv // COL_CHUNK):
            ubc_ref[:, j * COL_CHUNK:(j + 1) * COL_CHUNK] = _dot_nt(
                hb, w_ref[base + j * COL_CHUNK:base + (j + 1) * COL_CHUNK, :]).astype(BF16)

    row = lambda w: pl.BlockSpec((tm, w), lambda i: (i, 0))
    return _pcall(
        body, name=name, grid=(t // tm,),
        in_specs=[row(d), _full((2, d)), pl.BlockSpec((1, N_MOD, d), lambda i: (i // tpb, 0, 0)),
                  _resident((IN_WIDTH, d)), row(LANE), row(LANE), row(LANE)],
        out_specs=[row(d), row(ATTN_WIDTH), row(KV_WIDTH), row(KV_WIDTH), row(n_conv)],
        out_shape=[jax.ShapeDtypeStruct((t, d), BF16), jax.ShapeDtypeStruct((t, ATTN_WIDTH), BF16),
                   jax.ShapeDtypeStruct((t, KV_WIDTH), BF16), jax.ShapeDtypeStruct((t, KV_WIDTH), BF16),
                   jax.ShapeDtypeStruct((t, n_conv), BF16)],
        args=(xhat, lnp, mod, w_t, cos, sa, sb), comm=comm)


ATTN_TILE_BLOCKS = 2


def _attn_sub_block(s, tile, nblk, kp_ref, kc_ref, vp_ref, vc_ref):
    rows = slice(s * BLOCK, (s + 1) * BLOCK)
    if s == 0:
        first = ((tile * ATTN_TILE_BLOCKS) % nblk) == 0
        return rows, (kp_ref, slice(0, BLOCK)), (kc_ref, rows), (vp_ref, slice(0, BLOCK)), (vc_ref, rows), first
    before = slice((s - 1) * BLOCK, s * BLOCK)
    return rows, (kc_ref, before), (kc_ref, rows), (vc_ref, before), (vc_ref, rows), False


def _attn_group(q_ref, rows, k_prev, k_cur, v_prev, v_cur, sink_ref, g, first):
    lo, hi = g * HEAD_DIM, (g + 1) * HEAD_DIM
    kk = jnp.concatenate([k_prev[0][k_prev[1], lo:hi], k_cur[0][k_cur[1], lo:hi]], axis=0)
    vv = jnp.concatenate([v_prev[0][v_prev[1], lo:hi], v_cur[0][v_cur[1], lo:hi]], axis=0)
    qs = jnp.concatenate([q_ref[rows, (GQA_GROUP * g + j) * HEAD_DIM:(GQA_GROUP * g + j + 1) * HEAD_DIM]
                          for j in range(GQA_GROUP)], axis=0)
    cols = GQA_GROUP * BLOCK
    ki = lax.broadcasted_iota(jnp.int32, (2 * BLOCK, cols), 0)
    col = lax.broadcasted_iota(jnp.int32, (2 * BLOCK, cols), 1)
    diff = (col & (BLOCK - 1)) + BLOCK - ki
    valid = (diff >= 0) & (diff < WINDOW) & ((ki >= BLOCK) | jnp.logical_not(first))
    s = _dot_nt(kk, qs) * (HEAD_DIM ** -0.5)
    s = jnp.where(valid, s, -1e30)
    hcol = lax.broadcasted_iota(jnp.int32, (1, cols), 1)
    sink = jnp.zeros((1, cols), F32)
    for j in range(GQA_GROUP):
        sink = jnp.where(hcol // BLOCK == j, sink_ref[GQA_GROUP * g + j], sink)
    m = jnp.maximum(jnp.max(s, axis=0, keepdims=True), sink)
    p = jnp.exp(s - m)
    ps = jnp.exp(sink - m)
    inv = 1.0 / (jnp.sum(p, axis=0, keepdims=True) + ps)
    return qs, kk, vv, p * inv, ps * inv


def _heads_to_lanes(x_t):
    return jnp.concatenate([x_t[:, j * BLOCK:(j + 1) * BLOCK].T for j in range(GQA_GROUP)], axis=1)


def _attention(q, k, v, sinks, *, seq, name, comm=None):
    t = q.shape[0]
    nblk = seq // BLOCK
    tile = ATTN_TILE_BLOCKS * BLOCK

    def body(q_ref, kp_ref, kc_ref, vp_ref, vc_ref, sink_ref, o_ref):
        for s in range(ATTN_TILE_BLOCKS):
            rows, k_prev, k_cur, v_prev, v_cur, first = _attn_sub_block(
                s, pl.program_id(0), nblk, kp_ref, kc_ref, vp_ref, vc_ref)
            outs = []
            for g in range(N_KV_HEADS):
                _, _, vv, pn, _ = _attn_group(q_ref, rows, k_prev, k_cur, v_prev, v_cur, sink_ref, g, first)
                outs.append(_heads_to_lanes(_dot_tn(vv, pn.astype(BF16))))
            o_ref[rows, :] = jnp.concatenate(outs, axis=1).astype(BF16)

    cur = lambda w: pl.BlockSpec((tile, w), lambda n: (n, 0))
    prev = lambda w: pl.BlockSpec((BLOCK, w), lambda n: (jnp.maximum(n * ATTN_TILE_BLOCKS - 1, 0), 0))
    return _pcall(
        body, name=name, grid=(t // tile,),
        in_specs=[cur(ATTN_WIDTH), prev(KV_WIDTH), cur(KV_WIDTH), prev(KV_WIDTH), cur(KV_WIDTH),
                  pl.BlockSpec(memory_space=pltpu.SMEM)],
        out_specs=cur(ATTN_WIDTH),
        out_shape=jax.ShapeDtypeStruct((t, ATTN_WIDTH), BF16),
        args=(q, k, k, v, v, sinks), comm=comm)


def _out_proj(attn, ubc, cw, wout, xhat_in, lnp_in, mod, *, seq, gate_idx, name, comm=None):
    t, d = xhat_in.shape
    tm = min(512, seq)
    tpb = seq // tm
    cwid = CONV_WIDTH

    def body(attn_ref, ubc_ref, halo_ref, cw_ref, w_ref, x_ref, ln_ref, mod_ref,
             mixin_ref, mix_ref, xhat_ref, rstd_ref, zbuf, acc):
        first = (pl.program_id(0) % tpb) == 0
        u, bg, cg = (ubc_ref[:, s * cwid:(s + 1) * cwid].astype(F32) for s in range(3))
        z = cg * u
        hz = halo_ref[:, 2 * cwid:3 * cwid].astype(F32) * halo_ref[:, 0:cwid].astype(F32)
        zbuf[0:HALO, :] = jnp.where(first, 0.0, hz)
        zbuf[HALO:HALO + tm, :] = z
        y = (cw_ref[0:1, :] * zbuf[HALO - 2:HALO - 2 + tm, :] + cw_ref[1:2, :] * zbuf[HALO - 1:HALO - 1 + tm, :]
             + cw_ref[2:3, :] * z)
        mixin_ref[:, 0:ATTN_WIDTH] = attn_ref[...]
        mixin_ref[:, ATTN_WIDTH:] = (bg * y).astype(BF16)
        mv = mixin_ref[...]
        for j in range(d // COL_CHUNK):
            acc[:, j * COL_CHUNK:(j + 1) * COL_CHUNK] = jnp.dot(
                mv, w_ref[:, j * COL_CHUNK:(j + 1) * COL_CHUNK], preferred_element_type=F32)
        scale = 1.0 + mod_ref[0, gate_idx:gate_idx + 1, :]

        mix = acc[...]
        xhat, rstd = _ln_stats(DN_ALPHA * (x_ref[...] * ln_ref[0:1, :] + ln_ref[1:2, :]) + scale * mix)
        mix_ref[...] = mix.astype(BF16)
        xhat_ref[...] = xhat
        rstd_ref[...] = rstd

    row = lambda w: pl.BlockSpec((tm, w), lambda i: (i, 0))
    return _pcall(
        body, name=name, grid=(t // tm,),
        in_specs=[row(ATTN_WIDTH), row(3 * cwid),
                  pl.BlockSpec((HALO, 3 * cwid), lambda i: (jnp.maximum(i * (tm // HALO) - 1, 0), 0)),
                  _full((8, cwid)), _resident((d, d)), row(d), _full((2, d)),
                  pl.BlockSpec((1, N_MOD, d), lambda i: (i // tpb, 0, 0))],
        out_specs=[row(d), row(d), row(d), row(1)],
        out_shape=[jax.ShapeDtypeStruct((t, d), BF16), jax.ShapeDtypeStruct((t, d), BF16),
                   jax.ShapeDtypeStruct((t, d), F32), jax.ShapeDtypeStruct((t, 1), F32)],
        scratch_shapes=[pltpu.VMEM((tm + HALO, cwid), F32), pltpu.VMEM((tm, d), F32)],
        args=(attn, ubc, ubc, cw, wout, xhat_in, lnp_in, mod), comm=comm)


def _ffn_bwd_act(df, wd, dact, *, seq, name, comm=None):
    t, d = df.shape
    f = wd.shape[0]
    tm = min(512, seq)
    ch = min(COL_CHUNK, f)

    def body(df_ref, wd_ref, dact_ref, dgu_ref):
        dfv = df_ref[...]
        for j in range(f // ch):
            da = _dot_nt(dfv, wd_ref[j * ch:(j + 1) * ch, :])
            dgu_ref[:, j * ch:(j + 1) * ch] = (da * dact_ref[:, j * ch:(j + 1) * ch].astype(F32)).astype(BF16)
            dgu_ref[:, f + j * ch:f + (j + 1) * ch] = (
                da * dact_ref[:, f + j * ch:f + (j + 1) * ch].astype(F32)).astype(BF16)

    return _pcall(
        body, name=name, grid=(t // tm,),
        in_specs=[pl.BlockSpec((tm, d), lambda i: (i, 0)), _resident((f, d)),
                  pl.BlockSpec((tm, 2 * f), lambda i: (i, 0))],
        out_specs=pl.BlockSpec((tm, 2 * f), lambda i: (i, 0)),
        out_shape=jax.ShapeDtypeStruct((t, 2 * f), BF16),
        args=(df, wd, dact), comm=comm)


def _bwd_in(a, w, dr, xin, rstd_prev, lnp_prev, mod, branch_prev, *, seq, w_is_nt, sc_idx, gate_idx,
            branch_scale, final, name, comm=None):
    t, kdim = a.shape
    d = dr.shape[1]
    nb = t // seq
    tm = min(512, seq)
    tpb = seq // tm

    def body(*refs):
        if final:
            a_ref, w_ref, dr_ref, x_ref, mod_ref, dx_ref, dsc_ref, dsh_ref, acc = refs
        else:
            (a_ref, w_ref, dr_ref, x_ref, rstd_ref, ln_ref, mod_ref, br_ref,
             drp_ref, dbr_ref, dsc_ref, dsh_ref, dgate_ref, dg_ref, db_ref, acc) = refs
        i = pl.program_id(0)
        av = a_ref[...]
        for j in range(d // COL_CHUNK):
            cols = slice(j * COL_CHUNK, (j + 1) * COL_CHUNK)
            acc[:, cols] = (_dot_nt(av, w_ref[cols, :]) if w_is_nt
                            else jnp.dot(av, w_ref[:, cols], preferred_element_type=F32))
        sc1 = 1.0 + mod_ref[0, sc_idx:sc_idx + 1, :]
        if not final:
            g_prev, b_prev = ln_ref[0:1, :], ln_ref[1:2, :]
            bscale = branch_scale * (1.0 + mod_ref[0, gate_idx:gate_idx + 1, :])

        def chunk(rows, carry):
            dh = acc[rows, :]
            dx = DN_ALPHA * dr_ref[rows, :] + dh * sc1
            if final:
                dx_ref[rows, :] = dx
                return carry[0] + _fold8(dh * x_ref[rows, :]), carry[1] + _fold8(dh)
            xhat = x_ref[rows, :]
            drp = _ln_bwd(dx, xhat, rstd_ref[rows, :], g_prev)
            drp_ref[rows, :] = drp
            dbr_ref[rows, :] = (bscale * drp).astype(BF16)
            return (carry[0] + _fold8(dh * xhat), carry[1] + _fold8(dh),
                    carry[2] + _fold8(br_ref[rows, :].astype(F32) * drp),
                    carry[3] + _fold8(dx * xhat), carry[4] + _fold8(dx))

        zero = jnp.zeros((8, d), F32)
        sums = list(_row_chunk_loop(tm, chunk, (zero,) * (2 if final else 5)))
        if not final:
            sums[0] = sums[0] * g_prev + sums[1] * b_prev
            sums[2] = sums[2] * branch_scale

        @pl.when((i % tpb) == 0)
        def _():
            dsc_ref[...] = jnp.zeros_like(dsc_ref)
            dsh_ref[...] = jnp.zeros_like(dsh_ref)
            if not final:
                dgate_ref[...] = jnp.zeros_like(dgate_ref)

        dsc_ref[0] += _row_sum(sums[0])
        dsh_ref[0] += _row_sum(sums[1])
        if not final:
            @pl.when(i == 0)
            def _():
                dg_ref[...] = jnp.zeros_like(dg_ref)
                db_ref[...] = jnp.zeros_like(db_ref)

            dgate_ref[0] += _row_sum(sums[2])
            dg_ref[...] += _row_sum(sums[3])
            db_ref[...] += _row_sum(sums[4])

    row = lambda w_: pl.BlockSpec((tm, w_), lambda i: (i, 0))
    vec = pl.BlockSpec((1, 1, d), lambda i: (i // tpb, 0, 0))
    mod_spec = pl.BlockSpec((1, N_MOD, d), lambda i: (i // tpb, 0, 0))
    vshape = jax.ShapeDtypeStruct((nb, 1, d), F32)
    if final:
        in_specs = [row(kdim), _resident(w.shape), row(d), row(d), mod_spec]
        args = (a, w, dr, xin, mod)
        out_specs = [row(d), vec, vec]
        out_shape = [jax.ShapeDtypeStruct((t, d), F32), vshape, vshape]
    else:
        in_specs = [row(kdim), _resident(w.shape), row(d), row(d), row(1), _full((2, d)), mod_spec, row(d)]
        args = (a, w, dr, xin, rstd_prev, lnp_prev, mod, branch_prev)
        out_specs = [row(d), row(d), vec, vec, vec, _full((1, d)), _full((1, d))]
        out_shape = [jax.ShapeDtypeStruct((t, d), F32), jax.ShapeDtypeStruct((t, d), BF16), vshape, vshape, vshape,
                     jax.ShapeDtypeStruct((1, d), F32), jax.ShapeDtypeStruct((1, d), F32)]
    return _pcall(
        body, name=name, grid=(t // tm,), in_specs=in_specs, out_specs=out_specs, out_shape=out_shape,
        scratch_shapes=[pltpu.VMEM((tm, d), F32)], args=args, comm=comm)


def _grad_chip_sum(pos, a, b, *, half_on_rows, name, comm=None):
    t, m = a.shape
    n = b.shape[1]
    tk = min(2048, t)
    nk = t // tk
    half = lambda p, pos_ref: 1 - pos_ref[2] - p + 2 * p * pos_ref[2]
    if half_on_rows:
        n_j = N_CHIPS
        tile = (m // 2, n // n_j)
        a_spec = pl.BlockSpec((tk, tile[0]), lambda p, j, k, pos_ref: (k, half(p, pos_ref)))
        b_spec = pl.BlockSpec((tk, tile[1]), lambda p, j, k, pos_ref: (k, j))
        out_tile = pl.BlockSpec((1, *tile), lambda p, j, k, pos_ref: (0, 0, j * p))
        total = (1, m // 2, n)
    else:
        n_j = 2
        tile = (m // n_j, n // 2)
        a_spec = pl.BlockSpec((tk, tile[0]), lambda p, j, k, pos_ref: (k, j))
        b_spec = pl.BlockSpec((tk, tile[1]), lambda p, j, k, pos_ref: (k, half(p, pos_ref)))
        out_tile = pl.BlockSpec((1, *tile), lambda p, j, k, pos_ref: (0, j * p, 0))
        total = (1, m, n // 2)

    def body(pos_ref, a_ref, b_ref, s32_ref, s16_ref, land_ref, acc, theirs, send_sems, recv_sems, copy_sem):
        p, j, k = pl.program_id(0), pl.program_id(1), pl.program_id(2)
        x, y, c = _position()

        def push(jj):
            return pltpu.make_async_remote_copy(
                src_ref=acc.at[jj], dst_ref=land_ref.at[jj], send_sem=send_sems.at[jj], recv_sem=recv_sems.at[jj],
                device_id=(x, y, 1 - c), device_id_type=MESH)

        fetch = pltpu.make_async_copy(land_ref.at[j], theirs, copy_sem)

        @pl.when(jnp.logical_and(p == 1, k == 0))
        def _():
            push(j).wait_send()
            push(j).wait_recv()
            fetch.start()

        part = _dot_tn(a_ref[...], b_ref[...])

        @pl.when(k == 0)
        def _():
            acc[j] = part

        @pl.when(k > 0)
        def _():
            acc[j] += part

        @pl.when(jnp.logical_and(p == 0, k == nk - 1))
        def _():
            push(j).start()

        @pl.when(jnp.logical_and(p == 1, k == nk - 1))
        def _():
            fetch.wait()
            s = acc[j] + theirs[...]
            s32_ref[0] = s
            s16_ref[0] = s.astype(BF16)

    out = _pcall(
        body, name=name, grid=(2, n_j, nk), in_specs=[a_spec, b_spec], out_specs=[out_tile, out_tile, ANY_SPEC],
        out_shape=[jax.ShapeDtypeStruct(total, F32), jax.ShapeDtypeStruct(total, BF16),
                   jax.ShapeDtypeStruct((n_j, *tile), F32)],
        scratch_shapes=[pltpu.VMEM((n_j, *tile), F32), pltpu.VMEM(tile, F32),
                        pltpu.SemaphoreType.DMA((n_j,)), pltpu.SemaphoreType.DMA((n_j,)), pltpu.SemaphoreType.DMA],
        args=(a, b), prefetch=pos, comm=comm)
    if comm is None:
        return out[0], out[1]
    (s32, s16, _), extra = out
    return (s32, s16), extra


def _matmul_nt_bf16(a, w, *, seq, name):
    t, kdim = a.shape
    n = w.shape[0]
    tm = min(512, seq)

    def body(a_ref, w_ref, o_ref):
        av = a_ref[...]
        for j in range(n // COL_CHUNK):
            o_ref[:, j * COL_CHUNK:(j + 1) * COL_CHUNK] = _dot_nt(
                av, w_ref[j * COL_CHUNK:(j + 1) * COL_CHUNK, :]).astype(BF16)

    return pl.pallas_call(
        body, name=name, grid=(t // tm,),
        in_specs=[pl.BlockSpec((tm, kdim), lambda i: (i, 0)), _resident((n, kdim))],
        out_specs=pl.BlockSpec((tm, n), lambda i: (i, 0)),
        out_shape=jax.ShapeDtypeStruct((t, n), BF16),
        compiler_params=_params(("arbitrary",)),
    )(a, w)


def _attention_bwd(q, k, v, dmixin, sinks, *, seq, name, comm=None):
    t = q.shape[0]
    nblk = seq // BLOCK
    tile = ATTN_TILE_BLOCKS * BLOCK

    def body(q_ref, kp_ref, kc_ref, vp_ref, vc_ref, do_ref, sink_ref,
             dq_ref, dkp_ref, dkc_ref, dvp_ref, dvc_ref, dsink_ref):
        n = pl.program_id(0)

        @pl.when(n == 0)
        def _():
            dsink_ref[...] = jnp.zeros_like(dsink_ref)

        srow = lax.broadcasted_iota(jnp.int32, (8, LANE), 0)
        dsink = jnp.zeros((8, LANE), F32)
        for s in range(ATTN_TILE_BLOCKS):
            rows, k_prev, k_cur, v_prev, v_cur, first = _attn_sub_block(s, n, nblk, kp_ref, kc_ref, vp_ref, vc_ref)
            dqs, dks, dvs = [], [], []
            for g in range(N_KV_HEADS):
                qs, kk, vv, pn, psn = _attn_group(q_ref, rows, k_prev, k_cur, v_prev, v_cur, sink_ref, g, first)
                dos = jnp.concatenate(
                    [do_ref[rows, (GQA_GROUP * g + j) * HEAD_DIM:(GQA_GROUP * g + j + 1) * HEAD_DIM]
                     for j in range(GQA_GROUP)], axis=0)
                dp = _dot_nt(vv, dos)
                delta = jnp.sum(pn * dp, axis=0, keepdims=True)
                ds = pn * (dp - delta)
                dsk = psn * delta
                for j in range(GQA_GROUP):
                    tot = jnp.sum(dsk[:, j * BLOCK:(j + 1) * BLOCK], axis=1, keepdims=True)
                    dsink = dsink - jnp.where(srow == GQA_GROUP * g + j, tot, 0.0)
                dsb = (ds * (HEAD_DIM ** -0.5)).astype(BF16)
                dqs.append(_heads_to_lanes(_dot_tn(kk, dsb)))
                dks.append(jnp.dot(dsb, qs, preferred_element_type=F32))
                dvs.append(jnp.dot(pn.astype(BF16), dos, preferred_element_type=F32))
            dq_ref[rows, :] = jnp.concatenate(dqs, axis=1)
            dkp_ref[rows, :] = jnp.concatenate([x[0:BLOCK, :] for x in dks], axis=1)
            dkc_ref[rows, :] = jnp.concatenate([x[BLOCK:, :] for x in dks], axis=1)
            dvp_ref[rows, :] = jnp.concatenate([x[0:BLOCK, :] for x in dvs], axis=1)
            dvc_ref[rows, :] = jnp.concatenate([x[BLOCK:, :] for x in dvs], axis=1)
        dsink_ref[...] += dsink

    cur = lambda w: pl.BlockSpec((tile, w), lambda n: (n, 0))
    prev = lambda w: pl.BlockSpec((BLOCK, w), lambda n: (jnp.maximum(n * ATTN_TILE_BLOCKS - 1, 0), 0))
    kv = jax.ShapeDtypeStruct((t, KV_WIDTH), F32)
    return _pcall(
        body, name=name, grid=(t // tile,),
        in_specs=[cur(ATTN_WIDTH), prev(KV_WIDTH), cur(KV_WIDTH), prev(KV_WIDTH), cur(KV_WIDTH), cur(ATTN_WIDTH),
                  pl.BlockSpec(memory_space=pltpu.SMEM)],
        out_specs=[cur(ATTN_WIDTH), cur(KV_WIDTH), cur(KV_WIDTH), cur(KV_WIDTH), cur(KV_WIDTH), _full((8, LANE))],
        out_shape=[jax.ShapeDtypeStruct((t, ATTN_WIDTH), F32), kv, kv, kv, kv, jax.ShapeDtypeStruct((8, LANE), F32)],
        args=(q, k, k, v, v, dmixin, sinks), comm=comm)


def _mix_bwd_assemble(dq, dkp, dkc, dvp, dvc, cos, sa, sb, dmixin, ubc, cw, *, seq, name, comm=None):
    t = dq.shape[0]
    cwid = CONV_WIDTH
    tm = min(2 * BLOCK, seq)
    tiles_per_seq = seq // tm
    ntile = t // tm
    nblk_all = t // BLOCK
    per_tile = tm // BLOCK

    def body(*refs):
        dq_ref, dkc_ref, dvc_ref = refs[0:3]
        dkp_refs, dvp_refs = refs[3:3 + per_tile], refs[3 + per_tile:3 + 2 * per_tile]
        (cos_ref, sa_ref, sb_ref, dco_ref, dcon_ref, ubc_ref, hprev_ref, hnext_ref, cw_ref,
         dproj_ref, dcw_ref, zbuf, dybuf) = refs[3 + 2 * per_tile:]
        i = pl.program_id(0)
        first = (i % tiles_per_seq) == 0
        last = (i % tiles_per_seq) == tiles_per_seq - 1
        glast = i == ntile - 1

        @pl.when(i == 0)
        def _():
            dcw_ref[...] = jnp.zeros_like(dcw_ref)

        def with_next_block(cur_ref, nxt_refs):
            nxt = [r[...] for r in nxt_refs]
            nxt[-1] = jnp.where(glast, 0.0, nxt[-1])
            return cur_ref[...] + jnp.concatenate(nxt, axis=0)

        cos_t, sa_t, sb_t = cos_ref[...], sa_ref[...], sb_ref[...]
        for j in range(ATTN_WIDTH // LANE):
            dproj_ref[:, j * LANE:(j + 1) * LANE] = _rope_t(
                dq_ref[:, j * LANE:(j + 1) * LANE], cos_t, sa_t, sb_t).astype(BF16)
        dk = with_next_block(dkc_ref, dkp_refs)
        dproj_ref[:, ATTN_WIDTH:ATTN_WIDTH + KV_WIDTH] = _rope_t(dk, cos_t, sa_t, sb_t).astype(BF16)
        dv = with_next_block(dvc_ref, dvp_refs)
        dproj_ref[:, ATTN_WIDTH + KV_WIDTH:ATTN_WIDTH + 2 * KV_WIDTH] = dv.astype(BF16)

        u, bg, cg = (ubc_ref[:, s * cwid:(s + 1) * cwid].astype(F32) for s in range(3))
        z = cg * u
        hz = hprev_ref[:, 2 * cwid:3 * cwid].astype(F32) * hprev_ref[:, 0:cwid].astype(F32)
        zbuf[0:HALO, :] = jnp.where(first, 0.0, hz)
        zbuf[HALO:HALO + tm, :] = z
        z2, z1 = zbuf[HALO - 2:HALO - 2 + tm, :], zbuf[HALO - 1:HALO - 1 + tm, :]
        w0, w1, w2 = cw_ref[0:1, :], cw_ref[1:2, :], cw_ref[2:3, :]
        y = w0 * z2 + w1 * z1 + w2 * z
        dco = dco_ref[...].astype(F32)
        dyc = dco * bg
        dyn = dcon_ref[...].astype(F32) * hnext_ref[:, cwid:2 * cwid].astype(F32)
        dybuf[0:tm, :] = dyc
        dybuf[tm:tm + HALO, :] = jnp.where(last, 0.0, dyn)
        dz = w2 * dyc + w1 * dybuf[1:1 + tm, :] + w0 * dybuf[2:2 + tm, :]
        srow = lax.broadcasted_iota(jnp.int32, (8, cwid), 0)
        dcw_ref[...] += (jnp.where(srow == 0, _row_sum(dyc * z2), 0.0) + jnp.where(srow == 1, _row_sum(dyc * z1), 0.0)
                         + jnp.where(srow == 2, _row_sum(dyc * z), 0.0))
        base = ATTN_WIDTH + 2 * KV_WIDTH
        dproj_ref[:, base:base + cwid] = (dz * cg).astype(BF16)
        dproj_ref[:, base + cwid:base + 2 * cwid] = (dco * y).astype(BF16)
        dproj_ref[:, base + 2 * cwid:base + 3 * cwid] = (dz * u).astype(BF16)

    cur = lambda w: pl.BlockSpec((tm, w), lambda i: (i, 0))
    nxt = [pl.BlockSpec((BLOCK, KV_WIDTH), lambda i, s=s: (jnp.minimum(i * per_tile + s + 1, nblk_all - 1), 0))
           for s in range(per_tile)]
    prev_halo = pl.BlockSpec((HALO, 3 * cwid), lambda i: (jnp.maximum(i * (tm // HALO) - 1, 0), 0))
    next_halo = lambda w, col: pl.BlockSpec(
        (HALO, w), lambda i: (jnp.minimum((i + 1) * (tm // HALO), t // HALO - 1), col))
    return _pcall(
        body, name=name, grid=(ntile,),
        in_specs=[cur(ATTN_WIDTH), cur(KV_WIDTH), cur(KV_WIDTH), *nxt, *nxt,
                  cur(LANE), cur(LANE), cur(LANE),
                  pl.BlockSpec((tm, cwid), lambda i: (i, 1)), next_halo(cwid, 1),
                  cur(3 * cwid), prev_halo, next_halo(3 * cwid, 0), _full((8, cwid))],
        out_specs=[cur(IN_WIDTH), _full((8, cwid))],
        out_shape=[jax.ShapeDtypeStruct((t, IN_WIDTH), BF16), jax.ShapeDtypeStruct((8, cwid), F32)],
        scratch_shapes=[pltpu.VMEM((tm + HALO, cwid), F32), pltpu.VMEM((tm + HALO, cwid), F32)],
        args=(dq, dkc, dvc, *([dkp] * per_tile), *([dvp] * per_tile), cos, sa, sb, dmixin, dmixin,
              ubc, ubc, ubc, cw), comm=comm)


def _ada_fwd(c_all, w_ada, b_ada_shard, chip, casts, *, name, comm=None):
    nb, d = c_all.shape
    n = w_ada.shape[1]
    steps = 2
    tn = n // steps
    n_cast = len(casts)

    def body(chip_ref, c_ref, w_ref, b_ref, *refs):
        cast_in, o_ref, cast_out = refs[:n_cast], refs[n_cast], refs[n_cast + 1:]
        cv = c_ref[...]
        cond = cv * _sigmoid(cv)
        o_ref[...] = jnp.dot(cond, w_ref[...], preferred_element_type=F32,
                             precision=lax.Precision.HIGHEST) + b_ref[...]
        for src, dst in zip(cast_in, cast_out):
            dst[...] = src[...].astype(BF16)

    in_specs = [_full((nb, d)), pl.BlockSpec((d, tn), lambda j, chip_ref: (0, j)),
                pl.BlockSpec((1, tn), lambda j, chip_ref: (0, j))]
    out_specs = [pl.BlockSpec((nb, tn), lambda j, chip_ref: (0, j))]
    out_shape = [jax.ShapeDtypeStruct((nb, n), F32)]
    for w, col_kind in casts:
        r, c = w.shape
        tr = r // steps
        in_specs.append(pl.BlockSpec((tr, c), lambda j, chip_ref: (j, 0)))
        if col_kind:
            out_specs.append(pl.BlockSpec((tr, c), lambda j, chip_ref: (j, chip_ref[0])))
            out_shape.append(jax.ShapeDtypeStruct((r, c * N_CHIPS), BF16))
        else:
            out_specs.append(pl.BlockSpec((tr, c), lambda j, chip_ref: (chip_ref[0] * steps + j, 0)))
            out_shape.append(jax.ShapeDtypeStruct((r * N_CHIPS, c), BF16))
    out = _pcall(body, name=name, grid=(steps,), in_specs=in_specs, out_specs=out_specs, out_shape=out_shape,
                 args=(c_all, w_ada, b_ada_shard, *[w for w, _ in casts]), prefetch=chip, comm=comm)
    res, extra = out if comm is not None else (out, None)
    return res[0], list(res[1:]), extra


def _small_finish(gathered, dmod_all, dmod_shard, c_all_t, *, name):
    d = D_MODEL
    nb, n = dmod_shard.shape

    def body(g_ref, dm_ref, dms_ref, ct_ref, sum_ref, gw_ref, gb_ref):
        total = g_ref[0]
        for dev in range(1, N_DEV):
            total = total + g_ref[dev]
        sum_ref[...] = total
        gb_ref[...] = _row_sum(dm_ref[...])
        ctv = ct_ref[...]
        cond_t = (ctv * _sigmoid(ctv)).astype(BF16)
        for jb in range(n // COL_CHUNK):
            gw_ref[:, jb * COL_CHUNK:(jb + 1) * COL_CHUNK] = jnp.dot(
                cond_t, dms_ref[:, jb * COL_CHUNK:(jb + 1) * COL_CHUNK].astype(BF16), preferred_element_type=F32)

    return pl.pallas_call(
        body, name=name, grid=(1,),
        in_specs=[_full((N_DEV, SMALL_ROWS, d)), _full((nb, N_MOD * d)), _full((nb, n)), _full((d, nb))],
        out_specs=[_full((SMALL_ROWS, d)), _full((d, n)), _full((1, N_MOD * d))],
        out_shape=[jax.ShapeDtypeStruct((SMALL_ROWS, d), F32), jax.ShapeDtypeStruct((d, n), F32),
                   jax.ShapeDtypeStruct((1, N_MOD * d), F32)],
        compiler_params=_params(("arbitrary",)),
    )(gathered, dmod_all, dmod_shard, c_all_t)


def _row_tile(r, c, budget=1 << 21):
    if r * c * 4 <= budget or r % 16:
        return r
    best = 16
    for tr in range(16, r + 1, 16):
        if r % tr == 0 and tr * c * 4 <= budget:
            best = tr
    return best


def _cast_into(w, chip, col_kind, *, name):
    r, c = w.shape
    tr = _row_tile(r, c)

    def body(chip_ref, w_ref, o_ref):
        o_ref[...] = w_ref[...].astype(BF16)

    if col_kind:
        out_spec = pl.BlockSpec((tr, c), lambda i, chip_ref: (i, chip_ref[0]))
        out_shape = jax.ShapeDtypeStruct((r, c * N_CHIPS), BF16)
    else:
        out_spec = pl.BlockSpec((tr, c), lambda i, chip_ref: (chip_ref[0] * (r // tr) + i, 0))
        out_shape = jax.ShapeDtypeStruct((r * N_CHIPS, c), BF16)
    return _pcall(body, name=name, grid=(r // tr,), in_specs=[pl.BlockSpec((tr, c), lambda i, chip_ref: (i, 0))],
                  out_specs=out_spec, out_shape=out_shape, args=(w,), prefetch=chip)


def _adamw(w, g, m, v, *, name, return_grad=False, comm=None):
    r, c = w.shape
    tr = _row_tile(r, c)
    c1 = 1.0 - ADAM_B1 ** ADAM_STEP
    c2 = 1.0 - ADAM_B2 ** ADAM_STEP
    n_out = 4 if return_grad else 3

    def body(w_ref, g_ref, m_ref, v_ref, d_ref, nm_ref, nv_ref, *g_out):
        gv = g_ref[...]
        m2 = ADAM_B1 * m_ref[...] + (1.0 - ADAM_B1) * gv
        v2 = ADAM_B2 * v_ref[...] + (1.0 - ADAM_B2) * (gv * gv)
        d_ref[...] = -ADAM_LR * ((m2 / c1) / (jnp.sqrt(v2 / c2) + ADAM_EPS) + ADAM_WD * w_ref[...])
        nm_ref[...] = m2
        nv_ref[...] = v2
        for o in g_out:
            o[...] = gv

    spec = pl.BlockSpec((tr, c), lambda i: (i, 0))
    sh = jax.ShapeDtypeStruct((r, c), F32)
    return _pcall(body, name=name, grid=(r // tr,), in_specs=[spec] * 4, out_specs=[spec] * n_out,
                  out_shape=[sh] * n_out, args=(w, g, m, v), comm=comm)


def _adamw_small(ws, gs, ms, vs, *, name):
    n = len(ws)
    c1 = 1.0 - ADAM_B1 ** ADAM_STEP
    c2 = 1.0 - ADAM_B2 ** ADAM_STEP

    def body(*refs):
        w_refs, g_refs, m_refs, v_refs = (refs[k * n:(k + 1) * n] for k in range(4))
        out = refs[4 * n:]
        for p in range(n):
            gv = g_refs[p][...]
            m2 = ADAM_B1 * m_refs[p][...] + (1.0 - ADAM_B1) * gv
            v2 = ADAM_B2 * v_refs[p][...] + (1.0 - ADAM_B2) * (gv * gv)
            out[3 * p][...] = -ADAM_LR * ((m2 / c1) / (jnp.sqrt(v2 / c2) + ADAM_EPS) + ADAM_WD * w_refs[p][...])
            out[3 * p + 1][...] = m2
            out[3 * p + 2][...] = v2

    specs = [_full(a.shape) for a in ws]
    res = pl.pallas_call(
        body, name=name, grid=(1,), in_specs=specs * 4,
        out_specs=[s for s in specs for _ in range(3)],
        out_shape=[jax.ShapeDtypeStruct(a.shape, F32) for a in ws for _ in range(3)],
        compiler_params=_params(("arbitrary",)),
    )(*ws, *gs, *ms, *vs)
    return [tuple(res[3 * p:3 * p + 3]) for p in range(n)]


def _sum_final(pos, s32, recv, *, col_kind, n_shard, name, comm=None):
    def body(pos_ref, s_ref, r_ref, o_ref):
        total = ((s_ref[0] + r_ref[0].astype(F32)) + r_ref[1].astype(F32)) + r_ref[2].astype(F32)
        if col_kind:
            o_ref[0] = total
        else:
            o_ref[...] = total

    if col_kind:
        rows, cols = s32.shape[1], n_shard
        tr = _row_tile(rows, cols)
        own = pl.BlockSpec((1, tr, cols), lambda i, pos: (0, i, 2 * pos[0] + pos[1]))
        out_spec = pl.BlockSpec((1, tr, cols), lambda i, pos: (pos[2], i, 0))
        out_shape = jax.ShapeDtypeStruct((2, rows, cols), F32)
    else:
        rows, cols = n_shard, s32.shape[2]
        tr = _row_tile(rows, cols)
        own = pl.BlockSpec((1, tr, cols), lambda i, pos: (0, (2 * pos[0] + pos[1]) * (rows // tr) + i, 0))
        out_spec = pl.BlockSpec((tr, cols), lambda i, pos: (i, pos[2]))
        out_shape = jax.ShapeDtypeStruct((rows, 2 * cols), F32)
    return _pcall(
        body, name=name, grid=(rows // tr,),
        in_specs=[own, pl.BlockSpec((3, tr, cols), lambda i, pos: (0, i, 0))], out_specs=out_spec,
        out_shape=out_shape, args=(s32, recv), prefetch=pos, comm=comm)


def _position():
    return lax.axis_index("x"), lax.axis_index("y"), lax.axis_index("c")


def _allgather8(x_shard, *, name, comm=None):
    m_per, n = x_shard.shape
    nci, nco = (0, 0) if comm is None else (len(comm.inputs), len(comm.out_shapes))

    def body(*refs):
        x_ref, refs = refs[0], refs[1:]
        cin, refs = refs[:nci], refs[nci:]
        out_ref, refs = refs[0], refs[1:]
        cout, refs = refs[:nco], refs[nco:]
        (send_sems, recv_sems, local_sem), csems = refs[:3], refs[3:]
        x, y, c = _position()
        me, sibling = (x, y, c), (x, y, 1 - c)
        chips = [(1 - x, y), (x, 1 - y), (1 - x, 1 - y)]

        def rows(px, py, pc):
            return out_ref.at[pl.ds((4 * px + 2 * py + pc) * m_per, m_per), :]

        def copy(k, block, to, src=None):
            return pltpu.make_async_remote_copy(
                src_ref=rows(*block) if src is None else src, dst_ref=rows(*block),
                send_sem=send_sems.at[k], recv_sem=recv_sems.at[k], device_id=to, device_id_type=MESH)

        mine = pltpu.make_async_copy(x_ref, rows(*me), local_sem)
        mine.start()
        first = [copy(0, me, sibling, src=x_ref)]
        first += [copy(1 + j, me, (*chip, c), src=x_ref) for j, chip in enumerate(chips)]
        for cp in first:
            cp.start()
        if comm is not None:
            comm.start(cin, cout, csems)
        passed = [copy(4 + j, (*chip, c), sibling) for j, chip in enumerate(chips)]
        for j, chip in enumerate(chips):
            copy(1 + j, (*chip, c), me).wait_recv()
            passed[j].start()
        copy(0, sibling, me).wait_recv()
        for j, chip in enumerate(chips):
            copy(4 + j, (*chip, 1 - c), me).wait_recv()
        for cp in first + passed:
            cp.wait_send()
        mine.wait()
        if comm is not None:
            comm.middle(cin, cout, csems)
            comm.late(cin, cout, csems)
            comm.finish(cin, cout, csems)

    vmem = pl.BlockSpec(memory_space=pltpu.VMEM)
    sems = [pltpu.SemaphoreType.DMA((7,)), pltpu.SemaphoreType.DMA((7,)), pltpu.SemaphoreType.DMA]
    out = jax.ShapeDtypeStruct((N_DEV * m_per, n), x_shard.dtype)
    if comm is None:
        return pl.pallas_call(body, name=name, out_shape=out, in_specs=[vmem], out_specs=vmem,
                              scratch_shapes=sems)(x_shard)
    res = pl.pallas_call(
        body, name=name, out_shape=[out] + list(comm.out_shapes), in_specs=[vmem] + [ANY_SPEC] * nci,
        out_specs=[vmem] + [ANY_SPEC] * nco, scratch_shapes=sems + list(comm.sems),
        input_output_aliases={1 + i: 1 + o for i, o in comm.aliases.items()})(x_shard, *comm.inputs)
    return res[0], list(res[1:])


def _peer_chips(x, y):
    return [(1 - x, y), (x, 1 - y), (1 - x, 1 - y)]


class _GatherJob:
    def __init__(self, pieces):
        self.pieces = pieces
        n_p = len(pieces)
        self.inputs = [p[0] for p in pieces]
        self.out_shapes = [jax.ShapeDtypeStruct(p[0].shape, p[0].dtype) for p in pieces]
        for buf, col_kind, r0, nr in pieces:
            half_rows = buf.shape[0] // (2 if col_kind else 2 * N_CHIPS)
            assert r0 % 16 == 0 and nr % 16 == 0 and nr >= 32 and r0 + nr <= half_rows, (buf.shape, r0, nr)
        self.aliases = {p: p for p in range(n_p)}
        dma = pltpu.SemaphoreType.DMA
        self.sems = [dma((2 * n_p,))] * 4 + [dma((4 * n_p,))] * 2

    def _region(self, cout, p, chip_idx, half, part=None):
        buf, col_kind, r0, nr = self.pieces[p]
        first = -(-nr // 32) * 16
        if part == 0:
            nr = first
        elif part == 1:
            r0, nr = r0 + first, nr - first
        if col_kind:
            n = buf.shape[1] // N_CHIPS
            return cout[p].at[pl.ds(half * (buf.shape[0] // 2) + r0, nr), pl.ds(chip_idx * n, n)]
        n = buf.shape[0] // N_CHIPS
        return cout[p].at[pl.ds(chip_idx * n + half * (n // 2) + r0, nr), :]

    def _copies(self, cout, sems):
        send1, recv1, send2, recv2, fsend, frecv = sems
        x, y, c = _position()
        k = 2 * x + y
        sibling = (x, y, 1 - c)
        x_nbr, y_nbr, diag = _peer_chips(x, y)
        chip_of = lambda ch: 2 * ch[0] + ch[1]

        def remote(region, ssem, rsem, to):
            return pltpu.make_async_remote_copy(src_ref=region, dst_ref=region, send_sem=ssem, recv_sem=rsem,
                                                device_id=to, device_id_type=MESH)

        hop1, arrived1, hop2, arrived2, fwds, fwd_arrived = [], [], [], [], [], []
        for p in range(len(self.pieces)):
            for j, nbr in enumerate((x_nbr, y_nbr)):
                i1 = 2 * p + j
                hop1.append(remote(self._region(cout, p, k, c), send1.at[i1], recv1.at[i1], (*nbr, c)))
                arrived1.append(remote(self._region(cout, p, chip_of(nbr), c), send1.at[i1], recv1.at[i1], (*nbr, c)))
            hop2.append(remote(self._region(cout, p, chip_of(x_nbr), c, 0), send2.at[2 * p], recv2.at[2 * p],
                               (*y_nbr, c)))
            hop2.append(remote(self._region(cout, p, chip_of(y_nbr), c, 1), send2.at[2 * p + 1], recv2.at[2 * p + 1],
                               (*x_nbr, c)))
            arrived2.append(remote(self._region(cout, p, chip_of(diag), c, 0), send2.at[2 * p], recv2.at[2 * p],
                                   (*y_nbr, c)))
            arrived2.append(remote(self._region(cout, p, chip_of(diag), c, 1), send2.at[2 * p + 1],
                                   recv2.at[2 * p + 1], (*x_nbr, c)))
            landed = [(chip_of(x_nbr), None), (chip_of(y_nbr), None), (chip_of(diag), 0), (chip_of(diag), 1)]
            for q, (chip_idx, part) in enumerate(landed):
                i3 = 4 * p + q
                fwds.append(remote(self._region(cout, p, chip_idx, c, part), fsend.at[i3], frecv.at[i3], sibling))
                fwd_arrived.append(remote(self._region(cout, p, chip_idx, 1 - c, part), fsend.at[i3], frecv.at[i3],
                                          sibling))
        return hop1, arrived1, hop2, arrived2, fwds, fwd_arrived

    def start(self, cin, cout, sems):
        for cp in self._copies(cout, sems)[0]:
            cp.start()

    def middle(self, cin, cout, sems):
        _, arrived1, hop2, _, fwds, _ = self._copies(cout, sems)
        for p in range(len(self.pieces)):
            for j in range(2):
                arrived1[2 * p + j].wait_recv()
                hop2[2 * p + j].start()
                fwds[4 * p + j].start()

    def late(self, cin, cout, sems):
        _, _, _, arrived2, fwds, _ = self._copies(cout, sems)
        for p in range(len(self.pieces)):
            for j in range(2):
                arrived2[2 * p + j].wait_recv()
                fwds[4 * p + 2 + j].start()

    def finish(self, cin, cout, sems):
        hop1, _, hop2, _, fwds, fwd_arrived = self._copies(cout, sems)
        for cp in fwd_arrived:
            cp.wait_recv()
        for cp in hop1 + hop2 + fwds:
            cp.wait_send()


class _PairedJob:
    aliases = {}

    def start(self, cin, cout, sems):
        for cp in self._copies(cin, cout, sems):
            cp.start()

    def middle(self, cin, cout, sems):
        pass

    late = middle

    def finish(self, cin, cout, sems):
        copies = self._copies(cin, cout, sems)
        for cp in copies:
            cp.wait_recv()
        for cp in copies:
            cp.wait_send()


class _ExchangeJob(_PairedJob):
    def __init__(self, s16, kinds, sizes):
        self.inputs, self.kinds, self.sizes = list(s16), list(kinds), list(sizes)
        self.out_shapes = [jax.ShapeDtypeStruct((3, s.shape[1], n) if kd else (3, n, s.shape[2]), s.dtype)
                           for s, kd, n in zip(s16, kinds, sizes)]
        self.sems = [pltpu.SemaphoreType.DMA((3 * len(s16),)), pltpu.SemaphoreType.DMA((3 * len(s16),))]

    def _copies(self, cin, cout, sems):
        send_sems, recv_sems = sems
        x, y, c = _position()
        copies = []
        for p, src_ref in enumerate(cin):
            for j, chip in enumerate(_peer_chips(x, y)):
                kk = 2 * chip[0] + chip[1]
                n = self.sizes[p]
                src = src_ref.at[0, :, pl.ds(kk * n, n)] if self.kinds[p] else src_ref.at[0, pl.ds(kk * n, n), :]
                copies.append(pltpu.make_async_remote_copy(
                    src_ref=src, dst_ref=cout[p].at[j], send_sem=send_sems.at[3 * p + j],
                    recv_sem=recv_sems.at[3 * p + j], device_id=(*chip, c), device_id_type=MESH))
        return copies


class _ShareJob:
    def __init__(self, halves):
        self.inputs = list(halves)
        self.out_shapes = [jax.ShapeDtypeStruct(h.shape, h.dtype) for h in halves]
        self.aliases = {p: p for p in range(len(halves))}
        self.sems = [pltpu.SemaphoreType.DMA((len(halves),)), pltpu.SemaphoreType.DMA((len(halves),))]

    def _copies(self, cout, sems, half):
        send_sems, recv_sems = sems
        x, y, c = _position()
        h = c if half == "mine" else 1 - c

        def region(o):
            if len(o.shape) == 3:
                return o.at[h]
            hc = o.shape[1] // 2
            return o.at[:, pl.ds(h * hc, hc)]

        return [pltpu.make_async_remote_copy(
            src_ref=region(o), dst_ref=region(o), send_sem=send_sems.at[p], recv_sem=recv_sems.at[p],
            device_id=(x, y, 1 - c), device_id_type=MESH) for p, o in enumerate(cout)]

    def start(self, cin, cout, sems):
        for cp in self._copies(cout, sems, "mine"):
            cp.start()

    def middle(self, cin, cout, sems):
        pass

    late = middle

    def finish(self, cin, cout, sems):
        for cp in self._copies(cout, sems, "theirs"):
            cp.wait_recv()
        for cp in self._copies(cout, sems, "mine"):
            cp.wait_send()


class _MultiJob:
    def __init__(self, jobs):
        self.jobs = jobs
        self.inputs = [a for j in jobs for a in j.inputs]
        self.out_shapes = [s for j in jobs for s in j.out_shapes]
        self.sems = [s for j in jobs for s in j.sems]
        self.aliases = {}
        i0 = o0 = 0
        for j in jobs:
            for i, o in j.aliases.items():
                self.aliases[i0 + i] = o0 + o
            i0 += len(j.inputs)
            o0 += len(j.out_shapes)

    def _parts(self, cin, cout, sems):
        i0 = o0 = s0 = 0
        for j in self.jobs:
            ni, no, ns = len(j.inputs), len(j.out_shapes), len(j.sems)
            yield j, cin[i0:i0 + ni], cout[o0:o0 + no], sems[s0:s0 + ns]
            i0, o0, s0 = i0 + ni, o0 + no, s0 + ns

    def start(self, cin, cout, sems):
        for j, a, b, s in self._parts(cin, cout, sems):
            j.start(a, b, s)

    def middle(self, cin, cout, sems):
        for j, a, b, s in self._parts(cin, cout, sems):
            j.middle(a, b, s)

    def late(self, cin, cout, sems):
        for j, a, b, s in self._parts(cin, cout, sems):
            j.late(a, b, s)

    def finish(self, cin, cout, sems):
        for j, a, b, s in self._parts(cin, cout, sems):
            j.finish(a, b, s)


def _rope_tables(positions):
    half = ROT_DIM // 2
    inv_freq = jnp.power(jnp.float32(ROPE_THETA), -jnp.arange(0, ROT_DIM, 2, dtype=F32) / ROT_DIM)
    inv_head = jnp.concatenate([inv_freq, inv_freq, jnp.zeros((HEAD_DIM - ROT_DIM,), F32)])
    inv_lane = jnp.concatenate([inv_head] * (LANE // HEAD_DIM))
    ang = positions.astype(F32).reshape(-1)[:, None] * inv_lane[None, :]
    sin = jnp.sin(ang)
    dim = jnp.arange(LANE) % HEAD_DIM
    return jnp.cos(ang), jnp.where(dim < half, -sin, 0.0), jnp.where(dim >= half, sin, 0.0)


def kernel(x, c, positions, w_ada, b_ada, ffn1_w_gate_up, ffn1_w_down, ln1_g, ln1_b, w_in, conv_w, attn_sinks, w_out, ln2_g, ln2_b, ffn2_w_gate_up, ffn2_w_down, ln3_g, ln3_b, loss_target, m_w_ada, m_b_ada, m_ffn1_w_gate_up, m_ffn1_w_down, m_ln1_g, m_ln1_b, m_w_in, m_conv_w, m_attn_sinks, m_w_out, m_ln2_g, m_ln2_b, m_ffn2_w_gate_up, m_ffn2_w_down, m_ln3_g, m_ln3_b, v_w_ada, v_b_ada, v_ffn1_w_gate_up, v_ffn1_w_down, v_ln1_g, v_ln1_b, v_w_in, v_conv_w, v_attn_sinks, v_w_out, v_ln2_g, v_ln2_b, v_ffn2_w_gate_up, v_ffn2_w_down, v_ln3_g, v_ln3_b):
    d = D_MODEL
    nb, seq, _ = x.shape
    t = nb * seq
    f = ffn1_w_down.shape[1] * N_CHIPS
    ax, ay, ac = _position()
    chip = 2 * ax + ay
    dev = 2 * chip + ac
    pos = jnp.stack([ax, ay, ac]).astype(jnp.int32)

    x2 = x.reshape(t, d)
    tgt2 = loss_target.reshape(t, d)
    ln1 = jnp.concatenate([ln1_g, ln1_b], axis=0)
    ln2 = jnp.concatenate([ln2_g, ln2_b], axis=0)
    ln3 = jnp.concatenate([ln3_g, ln3_b], axis=0)
    sinks = attn_sinks.reshape(N_Q_HEADS)
    cos_t, sa_t, sb_t = _rope_tables(positions)

    gu_cuts = [0, 176, 352, d // 2]
    gu_part = lambda buf, s: (buf, True, gu_cuts[s], gu_cuts[s + 1] - gu_cuts[s])
    chip_arr = jnp.reshape(chip, (1,)).astype(jnp.int32)
    b_gu1 = _cast_into(ffn1_w_gate_up[0], chip_arr, True, name="cast_gu1")

    n_ada = w_ada.shape[2]
    c_all, (b_gu1,) = _allgather8(c.reshape(nb * d // LANE, LANE), name="gather_c", comm=_GatherJob([gu_part(b_gu1, 0)]))
    c_all = c_all.reshape(N_DEV * nb, d)
    b_shard = lax.dynamic_slice(b_ada, (0, chip * n_ada), (1, n_ada))
    later_shards = [(ffn1_w_down[0], False), (w_in[0].T, False), (w_out[0], False), (ffn2_w_gate_up[0], True),
                    (ffn2_w_down[0], False)]
    mod_part, (b_d1, b_in, b_out, b_gu2, b_d2), (b_gu1,) = _ada_fwd(
        c_all, w_ada[0], b_shard, chip_arr, later_shards, name="ada_fwd", comm=_GatherJob([gu_part(b_gu1, 1)]))
    conv_rows = jnp.pad(conv_w[0], ((0, 5), (0, n_ada - conv_w.shape[2])))
    part = jnp.concatenate([mod_part, conv_rows], axis=0)
    parts, (wgu1,) = _allgather8(part, name="gather_mod", comm=_GatherJob([gu_part(b_gu1, 2)]))
    parts = parts.reshape(N_DEV, N_DEV * nb + 8, n_ada)
    mod_all = jnp.concatenate([parts[2 * k, :N_DEV * nb, :] for k in range(N_CHIPS)], axis=1)
    mod = lax.dynamic_slice(mod_all, (dev * nb, 0), (nb, N_MOD * d)).reshape(nb, N_MOD, d)
    cw_full = jnp.concatenate([parts[2 * k, N_DEV * nb:, :conv_w.shape[2]] for k in range(N_CHIPS)], axis=1)

    n_gu, n_d, n_in, n_out = (ffn1_w_gate_up.shape[2], ffn1_w_down.shape[1], w_in.shape[2], w_out.shape[1])

    def whole(buf, col_kind):
        return (buf, col_kind, 0, buf.shape[0] // (2 if col_kind else 2 * N_CHIPS))

    (h1, a1, dact1), (wd1, wout) = _ffn_up(x2, ln1, mod, wgu1, seq=seq, sc_idx=1, sh_idx=0, use_ln=False,
                                         name="ffn1_up", comm=_GatherJob([whole(b_d1, False), whole(b_out, False)]))
    (f1, xhat1, rstd1), (win_t,) = _ffn_down_ln(a1, wd1, x2, ln1, mod, seq=seq, gate_idx=2, use_ln=False,
                                                name="ffn1_down", comm=_GatherJob([whole(b_in, False)]))
    (h2, q, k, v, ubc), (b_gu2,) = _in_proj(
        xhat1, ln1, mod, win_t, cos_t, sa_t, sb_t, seq=seq, sc_idx=4, sh_idx=3, name="in_proj",
        comm=_GatherJob([gu_part(b_gu2, 0)]))
    attn, (b_gu2,) = _attention(q, k, v, sinks, seq=seq, name="attention", comm=_GatherJob([gu_part(b_gu2, 1)]))
    (mixin, mix, xhat2, rstd2), (wgu2,) = _out_proj(
        attn, ubc, cw_full, wout, xhat1, ln1, mod, seq=seq, gate_idx=5, name="out_proj",
        comm=_GatherJob([gu_part(b_gu2, 2)]))
    (h3, a3, dact3), (wd2,) = _ffn_up(xhat2, ln2, mod, wgu2, seq=seq, sc_idx=7, sh_idx=6, use_ln=True, name="ffn2_up",
                                    comm=_GatherJob([whole(b_d2, False)]))
    dr3, df3, loss_cols, dln3g, dln3b, dgate3 = _ffn_down_loss(
        a3, wd2, xhat2, ln2, mod, ln3, tgt2, seq=seq, gate_idx=8, name="ffn2_down_loss")

    dgu3 = _ffn_bwd_act(df3, wd2, dact3, seq=seq, name="ffn2_bwd_act")
    s32_d2, s16_d2 = _grad_chip_sum(pos, a3, df3, half_on_rows=False, name="grad_wd2")
    (s32_gu2, s16_gu2), (recv_d2,) = _grad_chip_sum(pos, h3, dgu3, half_on_rows=True, name="grad_wgu2",
                                                    comm=_ExchangeJob([s16_d2], [False], [n_d]))
    (dr2, dmix, dsc3, dsh3, dgate2, dln2g, dln2b), (recv_gu2,) = _bwd_in(
        dgu3, wgu2, dr3, xhat2, rstd2, ln2, mod, mix, seq=seq, w_is_nt=True, sc_idx=7, gate_idx=5,
        branch_scale=1.0, final=False, name="ffn2_bwd_in", comm=_ExchangeJob([s16_gu2], [True], [n_gu]))
    s32_out, s16_out = _grad_chip_sum(pos, mixin, dmix, half_on_rows=False, name="grad_wout")
    dmixin = _matmul_nt_bf16(dmix, wout, seq=seq, name="out_proj_bwd")
    (dq, dkp, dkc, dvp, dvc, dsink), (recv_out,) = _attention_bwd(
        q, k, v, dmixin, sinks, seq=seq, name="attention_bwd", comm=_ExchangeJob([s16_out], [False], [n_out]))
    dproj, dcw = _mix_bwd_assemble(
        dq, dkp, dkc, dvp, dvc, cos_t, sa_t, sb_t, dmixin, ubc, cw_full, seq=seq, name="mix_bwd")
    s32_in, s16_in = _grad_chip_sum(pos, dproj, h2, half_on_rows=False, name="grad_win")
    (dr1, df1, dsc2, dsh2, dgate1, dln1g, dln1b), (recv_in,) = _bwd_in(
        dproj, win_t, dr2, xhat1, rstd1, ln1, mod, f1, seq=seq, w_is_nt=False, sc_idx=4, gate_idx=2,
        branch_scale=0.5, final=False, name="in_proj_bwd", comm=_ExchangeJob([s16_in], [False], [n_in]))
    s32_d1, s16_d1 = _grad_chip_sum(pos, a1, df1, half_on_rows=False, name="grad_wd1")
    dgu1, (recv_d1,) = _ffn_bwd_act(df1, wd1, dact1, seq=seq, name="ffn1_bwd_act",
                                    comm=_ExchangeJob([s16_d1], [False], [n_d]))
    s32_gu1, s16_gu1 = _grad_chip_sum(pos, h1, dgu1, half_on_rows=True, name="grad_wgu1")

    def final_half(s32_, recv_, col_kind, n_shard, name_):
        return _sum_final(pos, s32_, recv_, col_kind=col_kind, n_shard=n_shard, name=name_)

    early = [final_half(s32_gu2, recv_gu2, True, n_gu, "sum_final_gu2"),
             final_half(s32_d2, recv_d2, False, n_d, "sum_final_d2"),
             final_half(s32_out, recv_out, False, n_out, "sum_final_out"),
             final_half(s32_in, recv_in, False, n_in, "sum_final_in"),
             final_half(s32_d1, recv_d1, False, n_d, "sum_final_d1")]
    (grad_x, dsc1, dsh1), (recv_gu1, full_gu2, full_d2, full_out, full_in, full_d1) = _bwd_in(
        dgu1, wgu1, dr1, x2, None, None, mod, None, seq=seq, w_is_nt=True, sc_idx=1, gate_idx=None,
        branch_scale=None, final=True, name="ffn1_bwd_in",
        comm=_MultiJob([_ExchangeJob([s16_gu1], [True], [n_gu]), _ShareJob(early)]))
    late = [final_half(s32_gu1, recv_gu1, True, n_gu, "sum_final_gu1")]

    dmod = jnp.concatenate([dsh1, dsc1, dgate1, dsh2, dsc2, dgate2, dsh3, dsc3, dgate3], axis=1)
    loss_row = jnp.sum(loss_cols, axis=1, keepdims=True) * (0.5 / d)
    lane_row = lambda a: jnp.pad(a, ((0, 0), (0, d - a.shape[1])))
    block = jnp.concatenate(
        [dmod.reshape(nb * N_MOD, d), dln1g, dln1b, dln2g, dln2b, dln3g, dln3b,
         lane_row(dcw[0:3, :]), lane_row(dsink[:, 0:1].reshape(1, N_Q_HEADS)), lane_row(loss_row)], axis=0)
    block = jnp.pad(block, ((0, SMALL_ROWS - block.shape[0]), (0, 0)))
    gathered, (full_gu1,) = _allgather8(block, name="gather_small", comm=_ShareJob(late))
    gathered = gathered.reshape(N_DEV, SMALL_ROWS, d)
    dmod_all = gathered[:, :nb * N_MOD, :].reshape(N_DEV * nb, N_MOD * d)
    dmod_shard = lax.dynamic_slice(dmod_all, (0, chip * n_ada), (N_DEV * nb, n_ada))
    small, g_w_ada, g_b_ada = _small_finish(gathered, dmod_all, dmod_shard, c_all.T, name="small_finish")
    r0 = nb * N_MOD
    loss = small[r0 + 10, 0]
    g_ln = [small[r0 + i:r0 + i + 1, :] for i in range(6)]
    g_cw_full = small[r0 + 6:r0 + 9, :CONV_WIDTH]
    g_conv = lax.dynamic_slice(g_cw_full, (0, chip * conv_w.shape[2]), (3, conv_w.shape[2]))
    g_sinks = small[r0 + 9:r0 + 10, :N_Q_HEADS]

    def flat2(a):
        return a.reshape(-1, a.shape[-1])

    def unhalve(a):
        return a.reshape(2 * a.shape[1], a.shape[2])

    results = {}

    def adamw(name_, w_, g_, m_, v_, shared=False):
        g2 = flat2(g_)
        dl, nm, nv, *g_out = _adamw(flat2(w_), g2, flat2(m_), flat2(v_), name="adamw_" + name_, return_grad=shared)
        results[name_] = tuple(a.reshape(w_.shape) for a in (*g_out, g2)[:1] + (dl, nm, nv))

    adamw("w_ada", w_ada, g_w_ada, m_w_ada, v_w_ada)
    adamw("ffn2_w_gate_up", ffn2_w_gate_up, unhalve(full_gu2), m_ffn2_w_gate_up, v_ffn2_w_gate_up, shared=True)
    adamw("ffn2_w_down", ffn2_w_down, full_d2, m_ffn2_w_down, v_ffn2_w_down, shared=True)
    adamw("w_out", w_out, full_out, m_w_out, v_w_out, shared=True)
    in_t = _adamw(flat2(w_in).T, full_in, flat2(m_w_in).T, flat2(v_w_in).T, name="adamw_w_in")
    results["w_in"] = tuple(a.T.reshape(w_in.shape) for a in (full_in, *in_t))
    adamw("ffn1_w_gate_up", ffn1_w_gate_up, unhalve(full_gu1), m_ffn1_w_gate_up, v_ffn1_w_gate_up, shared=True)
    adamw("ffn1_w_down", ffn1_w_down, full_d1, m_ffn1_w_down, v_ffn1_w_down, shared=True)
    small_params = [("b_ada", b_ada, g_b_ada, m_b_ada, v_b_ada),
                    ("ln1_g", ln1_g, g_ln[0], m_ln1_g, v_ln1_g), ("ln1_b", ln1_b, g_ln[1], m_ln1_b, v_ln1_b),
                    ("ln2_g", ln2_g, g_ln[2], m_ln2_g, v_ln2_g), ("ln2_b", ln2_b, g_ln[3], m_ln2_b, v_ln2_b),
                    ("ln3_g", ln3_g, g_ln[4], m_ln3_g, v_ln3_g), ("ln3_b", ln3_b, g_ln[5], m_ln3_b, v_ln3_b),
                    ("conv_w", conv_w, g_conv, m_conv_w, v_conv_w),
                    ("attn_sinks", attn_sinks, g_sinks, m_attn_sinks, v_attn_sinks)]
    small_g = [flat2(g_) for _, _, g_, _, _ in small_params]
    small_res = _adamw_small([flat2(w_) for _, w_, _, _, _ in small_params], small_g,
                             [flat2(m_) for _, _, _, m_, _ in small_params],
                             [flat2(v_) for _, _, _, _, v_ in small_params], name="adamw_small")
    for (name_, w_, _, _, _), g2, res in zip(small_params, small_g, small_res):
        results[name_] = tuple(a.reshape(w_.shape) for a in (g2, *res))
    order = ["w_ada", "b_ada", "ffn1_w_gate_up", "ffn1_w_down", "ln1_g", "ln1_b", "w_in", "conv_w", "attn_sinks",
             "w_out", "ln2_g", "ln2_b", "ffn2_w_gate_up", "ffn2_w_down", "ln3_g", "ln3_b"]
    return (loss, grad_x.reshape(x.shape), *[results[n_][0] for n_ in order], *[results[n_][1] for n_ in order],
            *[results[n_][2] for n_ in order], *[results[n_][3] for n_ in order])
```

```python
import jax
import jax.numpy as jnp
from jax import lax
from jax.experimental import pallas as pl
from jax.experimental.pallas import tpu as pltpu

F32 = jnp.float32
BF16 = jnp.bfloat16
MESH = pl.DeviceIdType.MESH

D_MODEL = 1024
HEAD_DIM = 64
ATTN_WIDTH = 512
CONV_WIDTH = 512
N_Q_HEADS = 8
N_KV_HEADS = 2
GQA_GROUP = 4
KV_WIDTH = 128
WINDOW = 128
BLOCK = 128
ROT_DIM = 16
ROPE_THETA = 500000.0
N_MOD = 9
LN_EPS = 1e-5
DN_ALPHA = 2.0 ** 0.25
IN_WIDTH = 2304
N_CHIPS = 4
N_DEV = 8
SMALL_ROWS = 32

ADAM_LR = 0.001
ADAM_B1 = 0.9
ADAM_B2 = 0.999
ADAM_EPS = 1e-08
ADAM_WD = 0.01
ADAM_STEP = 10

LANE = 128
HALO = 16
COL_CHUNK = 256
VMEM_LIMIT = 56 * 1024 * 1024


def _params(sem=None, vmem=True):
    return pltpu.CompilerParams(dimension_semantics=sem, vmem_limit_bytes=VMEM_LIMIT if vmem else None)


def _sigmoid(g):
    return 0.5 * jnp.tanh(0.5 * g) + 0.5


def _row_sum(v):
    return jnp.sum(v, axis=0, keepdims=True)


ROW_CHUNK = 16
EPILOGUE_UNROLL = 8


def _fold8(v):
    return v[0:8, :] + v[8:16, :]


def _row_chunk_loop(n_rows, step, init):
    per_iter = ROW_CHUNK * EPILOGUE_UNROLL
    assert n_rows % per_iter == 0, n_rows

    def body(it, carry):
        for s in range(EPILOGUE_UNROLL):
            start = pl.multiple_of(it * per_iter + s * ROW_CHUNK, ROW_CHUNK)
            carry = step(pl.ds(start, ROW_CHUNK), carry)
        return carry

    return lax.fori_loop(0, n_rows // per_iter, body, init)


def _ln_stats(r):
    mu = jnp.mean(r, axis=-1, keepdims=True)
    rc = r - mu
    var = jnp.mean(rc * rc, axis=-1, keepdims=True)
    rstd = lax.rsqrt(var + LN_EPS)
    return rc * rstd, rstd


def _ln_bwd(dxo, xhat, rstd, g):
    dxhat = dxo * g
    m1 = jnp.mean(dxhat, axis=-1, keepdims=True)
    m2 = jnp.mean(dxhat * xhat, axis=-1, keepdims=True)
    return rstd * (dxhat - m1 - xhat * m2)


def _dot_nt(a, b):
    return lax.dot_general(a, b, (((1,), (1,)), ((), ())), preferred_element_type=F32)


def _dot_tn(a, b):
    return lax.dot_general(a, b, (((0,), (0,)), ((), ())), preferred_element_type=F32)


def _full(shape):
    nd = len(shape)
    return pl.BlockSpec(shape, lambda *_: (0,) * nd)


def _resident(shape):
    nd = len(shape)
    return pl.BlockSpec(shape, lambda *_: (0,) * nd, pipeline_mode=pl.Buffered(1))


ANY_SPEC = pl.BlockSpec(memory_space=pl.ANY)


def _pcall(body, *, name, grid, in_specs, out_specs, out_shape, args, scratch_shapes=(), comm=None, prefetch=None):
    single = not isinstance(out_shape, (list, tuple))
    out_specs = [out_specs] if single else list(out_specs)
    out_shape = [out_shape] if single else list(out_shape)
    in_specs = list(in_specs)
    scratch_shapes = list(scratch_shapes)
    sem = ("arbitrary",) * len(grid)
    n_pre = 0 if prefetch is None else 1
    pre_args = () if prefetch is None else (prefetch,)

    def call(fn, ins_, outs_, shapes_, scratch_, aliases_, operands):
        if prefetch is None:
            return pl.pallas_call(fn, name=name, grid=grid, in_specs=ins_, out_specs=outs_, out_shape=shapes_,
                                  scratch_shapes=scratch_, input_output_aliases=aliases_,
                                  compiler_params=_params(sem))(*operands)
        spec = pltpu.PrefetchScalarGridSpec(num_scalar_prefetch=1, grid=grid, in_specs=ins_, out_specs=outs_,
                                            scratch_shapes=scratch_)
        return pl.pallas_call(fn, name=name, grid_spec=spec, out_shape=shapes_,
                              input_output_aliases={n_pre + i: o for i, o in aliases_.items()},
                              compiler_params=_params(sem))(*pre_args, *operands)

    if comm is None:
        res = call(body, in_specs, out_specs, out_shape, scratch_shapes, {}, args)
        return res[0] if single else res
    n_in, n_out, n_scr = len(in_specs), len(out_specs), len(scratch_shapes)
    nci, nco = len(comm.inputs), len(comm.out_shapes)
    n_steps = 1
    for g in grid:
        n_steps *= g
    staged = n_steps >= 8
    middle_step = (n_steps * 5) // 8 - 1
    late_step = n_steps - 1 - max(1, n_steps // 8)

    def wrapped(*refs):
        pre, refs = refs[:n_pre], refs[n_pre:]
        ins, refs = refs[:n_in], refs[n_in:]
        cin, refs = refs[:nci], refs[nci:]
        outs, refs = refs[:n_out], refs[n_out:]
        cout, refs = refs[:nco], refs[nco:]
        scr, csems = refs[:n_scr], refs[n_scr:]
        step = pl.program_id(0)
        for ax in range(1, len(grid)):
            step = step * grid[ax] + pl.program_id(ax)

        @pl.when(step == 0)
        def _():
            comm.start(cin, cout, csems)

        body(*pre, *ins, *outs, *scr)

        if staged:
            @pl.when(step == middle_step)
            def _():
                comm.middle(cin, cout, csems)

            @pl.when(step == late_step)
            def _():
                comm.late(cin, cout, csems)

        @pl.when(step == n_steps - 1)
        def _():
            if not staged:
                comm.middle(cin, cout, csems)
                comm.late(cin, cout, csems)
            comm.finish(cin, cout, csems)

    res = call(wrapped, in_specs + [ANY_SPEC] * nci, out_specs + [ANY_SPEC] * nco,
               out_shape + list(comm.out_shapes), scratch_shapes + list(comm.sems),
               {n_in + i: n_out + o for i, o in comm.aliases.items()}, (*args, *comm.inputs))
    main = res[:n_out]
    return (main[0] if single else main), list(res[n_out:])


def _ffn_up(xin, lnp, mod, w, *, seq, sc_idx, sh_idx, use_ln, name, comm=None):
    t, d = xin.shape
    f = w.shape[1] // 2
    tm = min(512, seq)
    tpb = seq // tm
    ch = min(COL_CHUNK, f)

    def body(x_ref, ln_ref, mod_ref, w_ref, h_ref, a_ref, dact_ref):
        x = x_ref[...]
        if use_ln:
            x = x * ln_ref[0:1, :] + ln_ref[1:2, :]
        h = x * (1.0 + mod_ref[0, sc_idx:sc_idx + 1, :]) + mod_ref[0, sh_idx:sh_idx + 1, :]
        hb = h.astype(BF16)
        h_ref[...] = hb
        for j in range(f // ch):
            g = jnp.dot(hb, w_ref[:, j * ch:(j + 1) * ch], preferred_element_type=F32)
            u = jnp.dot(hb, w_ref[:, f + j * ch:f + (j + 1) * ch], preferred_element_type=F32)
            s = _sigmoid(g)
            silu = g * s
            a_ref[:, j * ch:(j + 1) * ch] = (silu * u).astype(BF16)
            dact_ref[:, j * ch:(j + 1) * ch] = (u * (s + silu * (1.0 - s))).astype(BF16)
            dact_ref[:, f + j * ch:f + (j + 1) * ch] = silu.astype(BF16)

    return _pcall(
        body, name=name, grid=(t // tm,),
        in_specs=[pl.BlockSpec((tm, d), lambda i: (i, 0)), _full((2, d)),
                  pl.BlockSpec((1, N_MOD, d), lambda i: (i // tpb, 0, 0)), _resident((d, 2 * f))],
        out_specs=[pl.BlockSpec((tm, d), lambda i: (i, 0)), pl.BlockSpec((tm, f), lambda i: (i, 0)),
                   pl.BlockSpec((tm, 2 * f), lambda i: (i, 0))],
        out_shape=[jax.ShapeDtypeStruct((t, d), BF16), jax.ShapeDtypeStruct((t, f), BF16),
                   jax.ShapeDtypeStruct((t, 2 * f), BF16)],
        args=(xin, lnp, mod, w), comm=comm)


def _ffn_down_ln(a, wd, xin, lnp_in, mod, *, seq, gate_idx, use_ln, name, comm=None):
    t, f = a.shape
    d = wd.shape[1]
    tm = min(512, seq)
    tpb = seq // tm

    def body(a_ref, wd_ref, x_ref, ln_ref, mod_ref, f_ref, xhat_ref, rstd_ref, acc):
        av = a_ref[...]
        for j in range(d // COL_CHUNK):
            acc[:, j * COL_CHUNK:(j + 1) * COL_CHUNK] = jnp.dot(
                av, wd_ref[:, j * COL_CHUNK:(j + 1) * COL_CHUNK], preferred_element_type=F32)
        scale = 0.5 * (1.0 + mod_ref[0, gate_idx:gate_idx + 1, :])

        fo = acc[...]
        x = x_ref[...]
        if use_ln:
            x = x * ln_ref[0:1, :] + ln_ref[1:2, :]
        xhat, rstd = _ln_stats(DN_ALPHA * x + scale * fo)
        f_ref[...] = fo.astype(BF16)
        xhat_ref[...] = xhat
        rstd_ref[...] = rstd

    return _pcall(
        body, name=name, grid=(t // tm,),
        in_specs=[pl.BlockSpec((tm, f), lambda i: (i, 0)), _resident((f, d)),
                  pl.BlockSpec((tm, d), lambda i: (i, 0)), _full((2, d)),
                  pl.BlockSpec((1, N_MOD, d), lambda i: (i // tpb, 0, 0))],
        out_specs=[pl.BlockSpec((tm, d), lambda i: (i, 0)), pl.BlockSpec((tm, d), lambda i: (i, 0)),
                   pl.BlockSpec((tm, 1), lambda i: (i, 0))],
        out_shape=[jax.ShapeDtypeStruct((t, d), BF16), jax.ShapeDtypeStruct((t, d), F32),
                   jax.ShapeDtypeStruct((t, 1), F32)],
        scratch_shapes=[pltpu.VMEM((tm, d), F32)],
        args=(a, wd, xin, lnp_in, mod), comm=comm)


def _ffn_down_loss(a, wd, xhat_in, lnp_in, mod, lnp_out, tgt, *, seq, gate_idx, name):
    t, f = a.shape
    d = wd.shape[1]
    nb = t // seq
    tm = min(512, seq)
    tpb = seq // tm

    def body(a_ref, wd_ref, x_ref, lnin_ref, mod_ref, lnout_ref, tgt_ref,
             dr_ref, df_ref, loss_ref, dg_ref, db_ref, dgate_ref, acc):
        i = pl.program_id(0)
        av = a_ref[...]
        for j in range(d // COL_CHUNK):
            acc[:, j * COL_CHUNK:(j + 1) * COL_CHUNK] = jnp.dot(
                av, wd_ref[:, j * COL_CHUNK:(j + 1) * COL_CHUNK], preferred_element_type=F32)
        scale = 0.5 * (1.0 + mod_ref[0, gate_idx:gate_idx + 1, :])
        ag_in, ab_in = DN_ALPHA * lnin_ref[0:1, :], DN_ALPHA * lnin_ref[1:2, :]
        g_out, b_out = lnout_ref[0:1, :], lnout_ref[1:2, :]
        g_over_d = g_out * (1.0 / d)

        def chunk(rows, carry):
            s_loss, s_dg, s_db, s_gate = carry
            fo = acc[rows, :]
            xhat, rstd = _ln_stats(x_ref[rows, :] * ag_in + ab_in + scale * fo)
            e = xhat * g_out + b_out - tgt_ref[rows, :]
            dr = _ln_bwd(e, xhat, rstd, g_over_d)
            dr_ref[rows, :] = dr
            df_ref[rows, :] = (scale * dr).astype(BF16)
            return s_loss + _fold8(e * e), s_dg + _fold8(e * xhat), s_db + _fold8(e), s_gate + _fold8(fo * dr)

        zero = jnp.zeros((8, d), F32)
        s_loss, s_dg, s_db, s_gate = _row_chunk_loop(tm, chunk, (zero, zero, zero, zero))
        s_dg, s_db, s_gate = s_dg * (1.0 / d), s_db * (1.0 / d), s_gate * 0.5

        @pl.when(i == 0)
        def _():
            loss_ref[...] = jnp.zeros_like(loss_ref)
            dg_ref[...] = jnp.zeros_like(dg_ref)
            db_ref[...] = jnp.zeros_like(db_ref)

        @pl.when(i % tpb == 0)
        def _():
            dgate_ref[...] = jnp.zeros_like(dgate_ref)

        loss_ref[...] += _row_sum(s_loss)
        dg_ref[...] += _row_sum(s_dg)
        db_ref[...] += _row_sum(s_db)
        dgate_ref[0] += _row_sum(s_gate)

    return pl.pallas_call(
        body, name=name, grid=(t // tm,), scratch_shapes=[pltpu.VMEM((tm, d), F32)],
        in_specs=[pl.BlockSpec((tm, f), lambda i: (i, 0)), _resident((f, d)),
                  pl.BlockSpec((tm, d), lambda i: (i, 0)), _full((2, d)),
                  pl.BlockSpec((1, N_MOD, d), lambda i: (i // tpb, 0, 0)), _full((2, d)),
                  pl.BlockSpec((tm, d), lambda i: (i, 0))],
        out_specs=[pl.BlockSpec((tm, d), lambda i: (i, 0)), pl.BlockSpec((tm, d), lambda i: (i, 0)),
                   _full((1, d)), _full((1, d)), _full((1, d)),
                   pl.BlockSpec((1, 1, d), lambda i: (i // tpb, 0, 0))],
        out_shape=[jax.ShapeDtypeStruct((t, d), F32), jax.ShapeDtypeStruct((t, d), BF16),
                   jax.ShapeDtypeStruct((1, d), F32), jax.ShapeDtypeStruct((1, d), F32),
                   jax.ShapeDtypeStruct((1, d), F32), jax.ShapeDtypeStruct((nb, 1, d), F32)],
        compiler_params=_params(("arbitrary",)),
    )(a, wd, xhat_in, lnp_in, mod, lnp_out, tgt)


def _rope(v, cos, sa, sb):
    return v * cos + pltpu.roll(v, LANE - ROT_DIM // 2, 1) * sa + pltpu.roll(v, ROT_DIM // 2, 1) * sb


def _rope_t(dy, cos, sa, sb):
    return dy * cos + pltpu.roll(dy * sa, ROT_DIM // 2, 1) + pltpu.roll(dy * sb, LANE - ROT_DIM // 2, 1)


def _in_proj(xhat, lnp, mod, w_t, cos, sa, sb, *, seq, sc_idx, sh_idx, name, comm=None):
    t, d = xhat.shape
    tm = min(512, seq)
    tpb = seq // tm
    n_conv = 3 * CONV_WIDTH

    def body(x_ref, ln_ref, mod_ref, w_ref, cos_ref, sa_ref, sb_ref, h_ref, q_ref, k_ref, v_ref, ubc_ref):
        x = x_ref[...] * ln_ref[0:1, :] + ln_ref[1:2, :]
        h = x * (1.0 + mod_ref[0, sc_idx:sc_idx + 1, :]) + mod_ref[0, sh_idx:sh_idx + 1, :]
        hb = h.astype(BF16)
        h_ref[...] = hb
        cos_t, sa_t, sb_t = cos_ref[...], sa_ref[...], sb_ref[...]
        for j in range(ATTN_WIDTH // COL_CHUNK):
            p = _dot_nt(hb, w_ref[j * COL_CHUNK:(j + 1) * COL_CHUNK, :])
            for s in range(COL_CHUNK // LANE):
                q_ref[:, j * COL_CHUNK + s * LANE:j * COL_CHUNK + (s + 1) * LANE] = _rope(
                    p[:, s * LANE:(s + 1) * LANE], cos_t, sa_t, sb_t).astype(BF16)
        p = _dot_nt(hb, w_ref[ATTN_WIDTH:ATTN_WIDTH + 2 * KV_WIDTH, :])
        k_ref[...] = _rope(p[:, 0:KV_WIDTH], cos_t, sa_t, sb_t).astype(BF16)
        v_ref[...] = p[:, KV_WIDTH:].astype(BF16)
        base = ATTN_WIDTH + 2 * KV_WIDTH
        for j in range(n_conv // COL_CHUNK):
            ubc_ref[:, j * COL_CHUNK:(j + 1) * COL_CHUNK] = _dot_nt(
                hb, w_ref[base + j * COL_CHUNK:base + (j + 1) * COL_CHUNK, :]).astype(BF16)

    row = lambda w: pl.BlockSpec((tm, w), lambda i: (i, 0))
    return _pcall(
        body, name=name, grid=(t // tm,),
        in_specs=[row(d), _full((2, d)), pl.BlockSpec((1, N_MOD, d), lambda i: (i // tpb, 0, 0)),
                  _resident((IN_WIDTH, d)), row(LANE), row(LANE), row(LANE)],
        out_specs=[row(d), row(ATTN_WIDTH), row(KV_WIDTH), row(KV_WIDTH), row(n_conv)],
        out_shape=[jax.ShapeDtypeStruct((t, d), BF16), jax.ShapeDtypeStruct((t, ATTN_WIDTH), BF16),
                   jax.ShapeDtypeStruct((t, KV_WIDTH), BF16), jax.ShapeDtypeStruct((t, KV_WIDTH), BF16),
                   jax.ShapeDtypeStruct((t, n_conv), BF16)],
        args=(xhat, lnp, mod, w_t, cos, sa, sb), comm=comm)


ATTN_TILE_BLOCKS = 2


def _attn_sub_block(s, tile, nblk, kp_ref, kc_ref, vp_ref, vc_ref):
    rows = slice(s * BLOCK, (s + 1) * BLOCK)
    if s == 0:
        first = ((tile * ATTN_TILE_BLOCKS) % nblk) == 0
        return rows, (kp_ref, slice(0, BLOCK)), (kc_ref, rows), (vp_ref, slice(0, BLOCK)), (vc_ref, rows), first
    before = slice((s - 1) * BLOCK, s * BLOCK)
    return rows, (kc_ref, before), (kc_ref, rows), (vc_ref, before), (vc_ref, rows), False


def _attn_group(q_ref, rows, k_prev, k_cur, v_prev, v_cur, sink_ref, g, first):
    lo, hi = g * HEAD_DIM, (g + 1) * HEAD_DIM
    kk = jnp.concatenate([k_prev[0][k_prev[1], lo:hi], k_cur[0][k_cur[1], lo:hi]], axis=0)
    vv = jnp.concatenate([v_prev[0][v_prev[1], lo:hi], v_cur[0][v_cur[1], lo:hi]], axis=0)
    qs = jnp.concatenate([q_ref[rows, (GQA_GROUP * g + j) * HEAD_DIM:(GQA_GROUP * g + j + 1) * HEAD_DIM]
                          for j in range(GQA_GROUP)], axis=0)
    cols = GQA_GROUP * BLOCK
    ki = lax.broadcasted_iota(jnp.int32, (2 * BLOCK, cols), 0)
    col = lax.broadcasted_iota(jnp.int32, (2 * BLOCK, cols), 1)
    diff = (col & (BLOCK - 1)) + BLOCK - ki
    valid = (diff >= 0) & (diff < WINDOW) & ((ki >= BLOCK) | jnp.logical_not(first))
    s = _dot_nt(kk, qs) * (HEAD_DIM ** -0.5)
    s = jnp.where(valid, s, -1e30)
    hcol = lax.broadcasted_iota(jnp.int32, (1, cols), 1)
    sink = jnp.zeros((1, cols), F32)
    for j in range(GQA_GROUP):
        sink = jnp.where(hcol // BLOCK == j, sink_ref[GQA_GROUP * g + j], sink)
    m = jnp.maximum(jnp.max(s, axis=0, keepdims=True), sink)
    p = jnp.exp(s - m)
    ps = jnp.exp(sink - m)
    inv = 1.0 / (jnp.sum(p, axis=0, keepdims=True) + ps)
    return qs, kk, vv, p * inv, ps * inv


def _heads_to_lanes(x_t):
    return jnp.concatenate([x_t[:, j * BLOCK:(j + 1) * BLOCK].T for j in range(GQA_GROUP)], axis=1)


def _attention(q, k, v, sinks, *, seq, name, comm=None):
    t = q.shape[0]
    nblk = seq // BLOCK
    tile = ATTN_TILE_BLOCKS * BLOCK

    def body(q_ref, kp_ref, kc_ref, vp_ref, vc_ref, sink_ref, o_ref):
        for s in range(ATTN_TILE_BLOCKS):
            rows, k_prev, k_cur, v_prev, v_cur, first = _attn_sub_block(
                s, pl.program_id(0), nblk, kp_ref, kc_ref, vp_ref, vc_ref)
            outs = []
            for g in range(N_KV_HEADS):
                _, _, vv, pn, _ = _attn_group(q_ref, rows, k_prev, k_cur, v_prev, v_cur, sink_ref, g, first)
                outs.append(_heads_to_lanes(_dot_tn(vv, pn.astype(BF16))))
            o_ref[rows, :] = jnp.concatenate(outs, axis=1).astype(BF16)

    cur = lambda w: pl.BlockSpec((tile, w), lambda n: (n, 0))
    prev = lambda w: pl.BlockSpec((BLOCK, w), lambda n: (jnp.maximum(n * ATTN_TILE_BLOCKS - 1, 0), 0))
    return _pcall(
        body, name=name, grid=(t // tile,),
        in_specs=[cur(ATTN_WIDTH), prev(KV_WIDTH), cur(KV_WIDTH), prev(KV_WIDTH), cur(KV_WIDTH),
                  pl.BlockSpec(memory_space=pltpu.SMEM)],
        out_specs=cur(ATTN_WIDTH),
        out_shape=jax.ShapeDtypeStruct((t, ATTN_WIDTH), BF16),
        args=(q, k, k, v, v, sinks), comm=comm)


def _out_proj(attn, ubc, cw, wout, xhat_in, lnp_in, mod, *, seq, gate_idx, name, comm=None):
    t, d = xhat_in.shape
    tm = min(512, seq)
    tpb = seq // tm
    cwid = CONV_WIDTH

    def body(attn_ref, ubc_ref, halo_ref, cw_ref, w_ref, x_ref, ln_ref, mod_ref,
             mixin_ref, mix_ref, xhat_ref, rstd_ref, zbuf, acc):
        first = (pl.program_id(0) % tpb) == 0
        u, bg, cg = (ubc_ref[:, s * cwid:(s + 1) * cwid].astype(F32) for s in range(3))
        z = cg * u
        hz = halo_ref[:, 2 * cwid:3 * cwid].astype(F32) * halo_ref[:, 0:cwid].astype(F32)
        zbuf[0:HALO, :] = jnp.where(first, 0.0, hz)
        zbuf[HALO:HALO + tm, :] = z
        y = (cw_ref[0:1, :] * zbuf[HALO - 2:HALO - 2 + tm, :] + cw_ref[1:2, :] * zbuf[HALO - 1:HALO - 1 + tm, :]
             + cw_ref[2:3, :] * z)
        mixin_ref[:, 0:ATTN_WIDTH] = attn_ref[...]
        mixin_ref[:, ATTN_WIDTH:] = (bg * y).astype(BF16)
        mv = mixin_ref[...]
        for j in range(d // COL_CHUNK):
            acc[:, j * COL_CHUNK:(j + 1) * COL_CHUNK] = jnp.dot(
                mv, w_ref[:, j * COL_CHUNK:(j + 1) * COL_CHUNK], preferred_element_type=F32)
        scale = 1.0 + mod_ref[0, gate_idx:gate_idx + 1, :]

        mix = acc[...]
        xhat, rstd = _ln_stats(DN_ALPHA * (x_ref[...] * ln_ref[0:1, :] + ln_ref[1:2, :]) + scale * mix)
        mix_ref[...] = mix.astype(BF16)
        xhat_ref[...] = xhat
        rstd_ref[...] = rstd

    row = lambda w: pl.BlockSpec((tm, w), lambda i: (i, 0))
    return _pcall(
        body, name=name, grid=(t // tm,),
        in_specs=[row(ATTN_WIDTH), row(3 * cwid),
                  pl.BlockSpec((HALO, 3 * cwid), lambda i: (jnp.maximum(i * (tm // HALO) - 1, 0), 0)),
                  _full((8, cwid)), _resident((d, d)), row(d), _full((2, d)),
                  pl.BlockSpec((1, N_MOD, d), lambda i: (i // tpb, 0, 0))],
        out_specs=[row(d), row(d), row(d), row(1)],
        out_shape=[jax.ShapeDtypeStruct((t, d), BF16), jax.ShapeDtypeStruct((t, d), BF16),
                   jax.ShapeDtypeStruct((t, d), F32), jax.ShapeDtypeStruct((t, 1), F32)],
        scratch_shapes=[pltpu.VMEM((tm + HALO, cwid), F32), pltpu.VMEM((tm, d), F32)],
        args=(attn, ubc, ubc, cw, wout, xhat_in, lnp_in, mod), comm=comm)


def _ffn_bwd_act(df, wd, dact, *, seq, name, comm=None):
    t, d = df.shape
    f = wd.shape[0]
    tm = min(512, seq)
    ch = min(COL_CHUNK, f)

    def body(df_ref, wd_ref, dact_ref, dgu_ref):
        dfv = df_ref[...]
        for j in range(f // ch):
            da = _dot_nt(dfv, wd_ref[j * ch:(j + 1) * ch, :])
            dgu_ref[:, j * ch:(j + 1) * ch] = (da * dact_ref[:, j * ch:(j + 1) * ch].astype(F32)).astype(BF16)
            dgu_ref[:, f + j * ch:f + (j + 1) * ch] = (
                da * dact_ref[:, f + j * ch:f + (j + 1) * ch].astype(F32)).astype(BF16)

    return _pcall(
        body, name=name, grid=(t // tm,),
        in_specs=[pl.BlockSpec((tm, d), lambda i: (i, 0)), _resident((f, d)),
                  pl.BlockSpec((tm, 2 * f), lambda i: (i, 0))],
        out_specs=pl.BlockSpec((tm, 2 * f), lambda i: (i, 0)),
        out_shape=jax.ShapeDtypeStruct((t, 2 * f), BF16),
        args=(df, wd, dact), comm=comm)


def _bwd_in(a, w, dr, xin, rstd_prev, lnp_prev, mod, branch_prev, *, seq, w_is_nt, sc_idx, gate_idx,
            branch_scale, final, name, comm=None):
    t, kdim = a.shape
    d = dr.shape[1]
    nb = t // seq
    tm = min(512, seq)
    tpb = seq // tm

    def body(*refs):
        if final:
            a_ref, w_ref, dr_ref, x_ref, mod_ref, dx_ref, dsc_ref, dsh_ref, acc = refs
        else:
            (a_ref, w_ref, dr_ref, x_ref, rstd_ref, ln_ref, mod_ref, br_ref,
             drp_ref, dbr_ref, dsc_ref, dsh_ref, dgate_ref, dg_ref, db_ref, acc) = refs
        i = pl.program_id(0)
        av = a_ref[...]
        for j in range(d // COL_CHUNK):
            cols = slice(j * COL_CHUNK, (j + 1) * COL_CHUNK)
            acc[:, cols] = (_dot_nt(av, w_ref[cols, :]) if w_is_nt
                            else jnp.dot(av, w_ref[:, cols], preferred_element_type=F32))
        sc1 = 1.0 + mod_ref[0, sc_idx:sc_idx + 1, :]
        if not final:
            g_prev, b_prev = ln_ref[0:1, :], ln_ref[1:2, :]
            bscale = branch_scale * (1.0 + mod_ref[0, gate_idx:gate_idx + 1, :])

        def chunk(rows, carry):
            dh = acc[rows, :]
            dx = DN_ALPHA * dr_ref[rows, :] + dh * sc1
            if final:
                dx_ref[rows, :] = dx
                return carry[0] + _fold8(dh * x_ref[rows, :]), carry[1] + _fold8(dh)
            xhat = x_ref[rows, :]
            drp = _ln_bwd(dx, xhat, rstd_ref[rows, :], g_prev)
            drp_ref[rows, :] = drp
            dbr_ref[rows, :] = (bscale * drp).astype(BF16)
            return (carry[0] + _fold8(dh * xhat), carry[1] + _fold8(dh),
                    carry[2] + _fold8(br_ref[rows, :].astype(F32) * drp),
                    carry[3] + _fold8(dx * xhat), carry[4] + _fold8(dx))

        zero = jnp.zeros((8, d), F32)
        sums = list(_row_chunk_loop(tm, chunk, (zero,) * (2 if final else 5)))
        if not final:
            sums[0] = sums[0] * g_prev + sums[1] * b_prev
            sums[2] = sums[2] * branch_scale

        @pl.when((i % tpb) == 0)
        def _():
            dsc_ref[...] = jnp.zeros_like(dsc_ref)
            dsh_ref[...] = jnp.zeros_like(dsh_ref)
            if not final:
                dgate_ref[...] = jnp.zeros_like(dgate_ref)

        dsc_ref[0] += _row_sum(sums[0])
        dsh_ref[0] += _row_sum(sums[1])
        if not final:
            @pl.when(i == 0)
            def _():
                dg_ref[...] = jnp.zeros_like(dg_ref)
                db_ref[...] = jnp.zeros_like(db_ref)

            dgate_ref[0] += _row_sum(sums[2])
            dg_ref[...] += _row_sum(sums[3])
            db_ref[...] += _row_sum(sums[4])

    row = lambda w_: pl.BlockSpec((tm, w_), lambda i: (i, 0))
    vec = pl.BlockSpec((1, 1, d), lambda i: (i // tpb, 0, 0))
    mod_spec = pl.BlockSpec((1, N_MOD, d), lambda i: (i // tpb, 0, 0))
    vshape = jax.ShapeDtypeStruct((nb, 1, d), F32)
    if final:
        in_specs = [row(kdim), _resident(w.shape), row(d), row(d), mod_spec]
        args = (a, w, dr, xin, mod)
        out_specs = [row(d), vec, vec]
        out_shape = [jax.ShapeDtypeStruct((t, d), F32), vshape, vshape]
    else:
        in_specs = [row(kdim), _resident(w.shape), row(d), row(d), row(1), _full((2, d)), mod_spec, row(d)]
        args = (a, w, dr, xin, rstd_prev, lnp_prev, mod, branch_prev)
        out_specs = [row(d), row(d), vec, vec, vec, _full((1, d)), _full((1, d))]
        out_shape = [jax.ShapeDtypeStruct((t, d), F32), jax.ShapeDtypeStruct((t, d), BF16), vshape, vshape, vshape,
                     jax.ShapeDtypeStruct((1, d), F32), jax.ShapeDtypeStruct((1, d), F32)]
    return _pcall(
        body, name=name, grid=(t // tm,), in_specs=in_specs, out_specs=out_specs, out_shape=out_shape,
        scratch_shapes=[pltpu.VMEM((tm, d), F32)], args=args, comm=comm)


def _grad_chip_sum(pos, a, b, *, half_on_rows, name, comm=None):
    t, m = a.shape
    n = b.shape[1]
    tk = min(2048, t)
    nk = t // tk
    half = lambda p, pos_ref: 1 - pos_ref[2] - p + 2 * p * pos_ref[2]
    if half_on_rows:
        n_j = N_CHIPS
        tile = (m // 2, n // n_j)
        a_spec = pl.BlockSpec((tk, tile[0]), lambda p, j, k, pos_ref: (k, half(p, pos_ref)))
        b_spec = pl.BlockSpec((tk, tile[1]), lambda p, j, k, pos_ref: (k, j))
        out_tile = pl.BlockSpec((1, *tile), lambda p, j, k, pos_ref: (0, 0, j * p))
        total = (1, m // 2, n)
    else:
        n_j = 2
        tile = (m // n_j, n // 2)
        a_spec = pl.BlockSpec((tk, tile[0]), lambda p, j, k, pos_ref: (k, j))
        b_spec = pl.BlockSpec((tk, tile[1]), lambda p, j, k, pos_ref: (k, half(p, pos_ref)))
        out_tile = pl.BlockSpec((1, *tile), lambda p, j, k, pos_ref: (0, j * p, 0))
        total = (1, m, n // 2)

    def body(pos_ref, a_ref, b_ref, s32_ref, s16_ref, land_ref, acc, theirs, send_sems, recv_sems, copy_sem):
        p, j, k = pl.program_id(0), pl.program_id(1), pl.program_id(2)
        x, y, c = _position()

        def push(jj):
            return pltpu.make_async_remote_copy(
                src_ref=acc.at[jj], dst_ref=land_ref.at[jj], send_sem=send_sems.at[jj], recv_sem=recv_sems.at[jj],
                device_id=(x, y, 1 - c), device_id_type=MESH)

        fetch = pltpu.make_async_copy(land_ref.at[j], theirs, copy_sem)

        @pl.when(jnp.logical_and(p == 1, k == 0))
        def _():
            push(j).wait_send()
            push(j).wait_recv()
            fetch.start()

        part = _dot_tn(a_ref[...], b_ref[...])

        @pl.when(k == 0)
        def _():
            acc[j] = part

        @pl.when(k > 0)
        def _():
            acc[j] += part

        @pl.when(jnp.logical_and(p == 0, k == nk - 1))
        def _():
            push(j).start()

        @pl.when(jnp.logical_and(p == 1, k == nk - 1))
        def _():
            fetch.wait()
            s = acc[j] + theirs[...]
            s32_ref[0] = s
            s16_ref[0] = s.astype(BF16)

    out = _pcall(
        body, name=name, grid=(2, n_j, nk), in_specs=[a_spec, b_spec], out_specs=[out_tile, out_tile, ANY_SPEC],
        out_shape=[jax.ShapeDtypeStruct(total, F32), jax.ShapeDtypeStruct(total, BF16),
                   jax.ShapeDtypeStruct((n_j, *tile), F32)],
        scratch_shapes=[pltpu.VMEM((n_j, *tile), F32), pltpu.VMEM(tile, F32),
                        pltpu.SemaphoreType.DMA((n_j,)), pltpu.SemaphoreType.DMA((n_j,)), pltpu.SemaphoreType.DMA],
        args=(a, b), prefetch=pos, comm=comm)
    if comm is None:
        return out[0], out[1]
    (s32, s16, _), extra = out
    return (s32, s16), extra


def _matmul_nt_bf16(a, w, *, seq, name):
    t, kdim = a.shape
    n = w.shape[0]
    tm = min(512, seq)

    def body(a_ref, w_ref, o_ref):
        av = a_ref[...]
        for j in range(n // COL_CHUNK):
            o_ref[:, j * COL_CHUNK:(j + 1) * COL_CHUNK] = _dot_nt(
                av, w_ref[j * COL_CHUNK:(j + 1) * COL_CHUNK, :]).astype(BF16)

    return pl.pallas_call(
        body, name=name, grid=(t // tm,),
        in_specs=[pl.BlockSpec((tm, kdim), lambda i: (i, 0)), _resident((n, kdim))],
        out_specs=pl.BlockSpec((tm, n), lambda i: (i, 0)),
        out_shape=jax.ShapeDtypeStruct((t, n), BF16),
        compiler_params=_params(("arbitrary",)),
    )(a, w)


def _attention_bwd(q, k, v, dmixin, sinks, *, seq, name, comm=None):
    t = q.shape[0]
    nblk = seq // BLOCK
    tile = ATTN_TILE_BLOCKS * BLOCK

    def body(q_ref, kp_ref, kc_ref, vp_ref, vc_ref, do_ref, sink_ref,
             dq_ref, dkp_ref, dkc_ref, dvp_ref, dvc_ref, dsink_ref):
        n = pl.program_id(0)

        @pl.when(n == 0)
        def _():
            dsink_ref[...] = jnp.zeros_like(dsink_ref)

        srow = lax.broadcasted_iota(jnp.int32, (8, LANE), 0)
        dsink = jnp.zeros((8, LANE), F32)
        for s in range(ATTN_TILE_BLOCKS):
            rows, k_prev, k_cur, v_prev, v_cur, first = _attn_sub_block(s, n, nblk, kp_ref, kc_ref, vp_ref, vc_ref)
            dqs, dks, dvs = [], [], []
            for g in range(N_KV_HEADS):
                qs, kk, vv, pn, psn = _attn_group(q_ref, rows, k_prev, k_cur, v_prev, v_cur, sink_ref, g, first)
                dos = jnp.concatenate(
                    [do_ref[rows, (GQA_GROUP * g + j) * HEAD_DIM:(GQA_GROUP * g + j + 1) * HEAD_DIM]
                     for j in range(GQA_GROUP)], axis=0)
                dp = _dot_nt(vv, dos)
                delta = jnp.sum(pn * dp, axis=0, keepdims=True)
                ds = pn * (dp - delta)
                dsk = psn * delta
                for j in range(GQA_GROUP):
                    tot = jnp.sum(dsk[:, j * BLOCK:(j + 1) * BLOCK], axis=1, keepdims=True)
                    dsink = dsink - jnp.where(srow == GQA_GROUP * g + j, tot, 0.0)
                dsb = (ds * (HEAD_DIM ** -0.5)).astype(BF16)
                dqs.append(_heads_to_lanes(_dot_tn(kk, dsb)))
                dks.append(jnp.dot(dsb, qs, preferred_element_type=F32))
                dvs.append(jnp.dot(pn.astype(BF16), dos, preferred_element_type=F32))
            dq_ref[rows, :] = jnp.concatenate(dqs, axis=1)
            dkp_ref[rows, :] = jnp.concatenate([x[0:BLOCK, :] for x in dks], axis=1)
            dkc_ref[rows, :] = jnp.concatenate([x[BLOCK:, :] for x in dks], axis=1)
            dvp_ref[rows, :] = jnp.concatenate([x[0:BLOCK, :] for x in dvs], axis=1)
            dvc_ref[rows, :] = jnp.concatenate([x[BLOCK:, :] for x in dvs], axis=1)
        dsink_ref[...] += dsink

    cur = lambda w: pl.BlockSpec((tile, w), lambda n: (n, 0))
    prev = lambda w: pl.BlockSpec((BLOCK, w), lambda n: (jnp.maximum(n * ATTN_TILE_BLOCKS - 1, 0), 0))
    kv = jax.ShapeDtypeStruct((t, KV_WIDTH), F32)
    return _pcall(
        body, name=name, grid=(t // tile,),
        in_specs=[cur(ATTN_WIDTH), prev(KV_WIDTH), cur(KV_WIDTH), prev(KV_WIDTH), cur(KV_WIDTH), cur(ATTN_WIDTH),
                  pl.BlockSpec(memory_space=pltpu.SMEM)],
        out_specs=[cur(ATTN_WIDTH), cur(KV_WIDTH), cur(KV_WIDTH), cur(KV_WIDTH), cur(KV_WIDTH), _full((8, LANE))],
        out_shape=[jax.ShapeDtypeStruct((t, ATTN_WIDTH), F32), kv, kv, kv, kv, jax.ShapeDtypeStruct((8, LANE), F32)],
        args=(q, k, k, v, v, dmixin, sinks), comm=comm)


def _mix_bwd_assemble(dq, dkp, dkc, dvp, dvc, cos, sa, sb, dmixin, ubc, cw, *, seq, name, comm=None):
    t = dq.shape[0]
    cwid = CONV_WIDTH
    tm = min(2 * BLOCK, seq)
    tiles_per_seq = seq // tm
    ntile = t // tm
    nblk_all = t // BLOCK
    per_tile = tm // BLOCK

    def body(*refs):
        dq_ref, dkc_ref, dvc_ref = refs[0:3]
        dkp_refs, dvp_refs = refs[3:3 + per_tile], refs[3 + per_tile:3 + 2 * per_tile]
        (cos_ref, sa_ref, sb_ref, dco_ref, dcon_ref, ubc_ref, hprev_ref, hnext_ref, cw_ref,
         dproj_ref, dcw_ref, zbuf, dybuf) = refs[3 + 2 * per_tile:]
        i = pl.program_id(0)
        first = (i % tiles_per_seq) == 0
        last = (i % tiles_per_seq) == tiles_per_seq - 1
        glast = i == ntile - 1

        @pl.when(i == 0)
        def _():
            dcw_ref[...] = jnp.zeros_like(dcw_ref)

        def with_next_block(cur_ref, nxt_refs):
            nxt = [r[...] for r in nxt_refs]
            nxt[-1] = jnp.where(glast, 0.0, nxt[-1])
            return cur_ref[...] + jnp.concatenate(nxt, axis=0)

        cos_t, sa_t, sb_t = cos_ref[...], sa_ref[...], sb_ref[...]
        for j in range(ATTN_WIDTH // LANE):
            dproj_ref[:, j * LANE:(j + 1) * LANE] = _rope_t(
                dq_ref[:, j * LANE:(j + 1) * LANE], cos_t, sa_t, sb_t).astype(BF16)
        dk = with_next_block(dkc_ref, dkp_refs)
        dproj_ref[:, ATTN_WIDTH:ATTN_WIDTH + KV_WIDTH] = _rope_t(dk, cos_t, sa_t, sb_t).astype(BF16)
        dv = with_next_block(dvc_ref, dvp_refs)
        dproj_ref[:, ATTN_WIDTH + KV_WIDTH:ATTN_WIDTH + 2 * KV_WIDTH] = dv.astype(BF16)

        u, bg, cg = (ubc_ref[:, s * cwid:(s + 1) * cwid].astype(F32) for s in range(3))
        z = cg * u
        hz = hprev_ref[:, 2 * cwid:3 * cwid].astype(F32) * hprev_ref[:, 0:cwid].astype(F32)
        zbuf[0:HALO, :] = jnp.where(first, 0.0, hz)
        zbuf[HALO:HALO + tm, :] = z
        z2, z1 = zbuf[HALO - 2:HALO - 2 + tm, :], zbuf[HALO - 1:HALO - 1 + tm, :]
        w0, w1, w2 = cw_ref[0:1, :], cw_ref[1:2, :], cw_ref[2:3, :]
        y = w0 * z2 + w1 * z1 + w2 * z
        dco = dco_ref[...].astype(F32)
        dyc = dco * bg
        dyn = dcon_ref[...].astype(F32) * hnext_ref[:, cwid:2 * cwid].astype(F32)
        dybuf[0:tm, :] = dyc
        dybuf[tm:tm + HALO, :] = jnp.where(last, 0.0, dyn)
        dz = w2 * dyc + w1 * dybuf[1:1 + tm, :] + w0 * dybuf[2:2 + tm, :]
        srow = lax.broadcasted_iota(jnp.int32, (8, cwid), 0)
        dcw_ref[...] += (jnp.where(srow == 0, _row_sum(dyc * z2), 0.0) + jnp.where(srow == 1, _row_sum(dyc * z1), 0.0)
                         + jnp.where(srow == 2, _row_sum(dyc * z), 0.0))
        base = ATTN_WIDTH + 2 * KV_WIDTH
        dproj_ref[:, base:base + cwid] = (dz * cg).astype(BF16)
        dproj_ref[:, base + cwid:base + 2 * cwid] = (dco * y).astype(BF16)
        dproj_ref[:, base + 2 * cwid:base + 3 * cwid] = (dz * u).astype(BF16)

    cur = lambda w: pl.BlockSpec((tm, w), lambda i: (i, 0))
    nxt = [pl.BlockSpec((BLOCK, KV_WIDTH), lambda i, s=s: (jnp.minimum(i * per_tile + s + 1, nblk_all - 1), 0))
           for s in range(per_tile)]
    prev_halo = pl.BlockSpec((HALO, 3 * cwid), lambda i: (jnp.maximum(i * (tm // HALO) - 1, 0), 0))
    next_halo = lambda w, col: pl.BlockSpec(
        (HALO, w), lambda i: (jnp.minimum((i + 1) * (tm // HALO), t // HALO - 1), col))
    return _pcall(
        body, name=name, grid=(ntile,),
        in_specs=[cur(ATTN_WIDTH), cur(KV_WIDTH), cur(KV_WIDTH), *nxt, *nxt,
                  cur(LANE), cur(LANE), cur(LANE),
                  pl.BlockSpec((tm, cwid), lambda i: (i, 1)), next_halo(cwid, 1),
                  cur(3 * cwid), prev_halo, next_halo(3 * cwid, 0), _full((8, cwid))],
        out_specs=[cur(IN_WIDTH), _full((8, cwid))],
        out_shape=[jax.ShapeDtypeStruct((t, IN_WIDTH), BF16), jax.ShapeDtypeStruct((8, cwid), F32)],
        scratch_shapes=[pltpu.VMEM((tm + HALO, cwid), F32), pltpu.VMEM((tm + HALO, cwid), F32)],
        args=(dq, dkc, dvc, *([dkp] * per_tile), *([dvp] * per_tile), cos, sa, sb, dmixin, dmixin,
              ubc, ubc, ubc, cw), comm=comm)


def _ada_fwd(c_all, w_ada, b_ada_shard, chip, casts, *, name, comm=None):
    nb, d = c_all.shape
    n = w_ada.shape[1]
    steps = 2
    tn = n // steps
    n_cast = len(casts)

    def body(chip_ref, c_ref, w_ref, b_ref, *refs):
        cast_in, o_ref, cast_out = refs[:n_cast], refs[n_cast], refs[n_cast + 1:]
        cv = c_ref[...]
        cond = cv * _sigmoid(cv)
        o_ref[...] = jnp.dot(cond, w_ref[...], preferred_element_type=F32,
                             precision=lax.Precision.HIGHEST) + b_ref[...]
        for src, dst in zip(cast_in, cast_out):
            dst[...] = src[...].astype(BF16)

    in_specs = [_full((nb, d)), pl.BlockSpec((d, tn), lambda j, chip_ref: (0, j)),
                pl.BlockSpec((1, tn), lambda j, chip_ref: (0, j))]
    out_specs = [pl.BlockSpec((nb, tn), lambda j, chip_ref: (0, j))]
    out_shape = [jax.ShapeDtypeStruct((nb, n), F32)]
    for w, col_kind in casts:
        r, c = w.shape
        tr = r // steps
        in_specs.append(pl.BlockSpec((tr, c), lambda j, chip_ref: (j, 0)))
        if col_kind:
            out_specs.append(pl.BlockSpec((tr, c), lambda j, chip_ref: (j, chip_ref[0])))
            out_shape.append(jax.ShapeDtypeStruct((r, c * N_CHIPS), BF16))
        else:
            out_specs.append(pl.BlockSpec((tr, c), lambda j, chip_ref: (chip_ref[0] * steps + j, 0)))
            out_shape.append(jax.ShapeDtypeStruct((r * N_CHIPS, c), BF16))
    out = _pcall(body, name=name, grid=(steps,), in_specs=in_specs, out_specs=out_specs, out_shape=out_shape,
                 args=(c_all, w_ada, b_ada_shard, *[w for w, _ in casts]), prefetch=chip, comm=comm)
    res, extra = out if comm is not None else (out, None)
    return res[0], list(res[1:]), extra


def _small_finish(gathered, dmod_all, dmod_shard, c_all_t, *, name):
    d = D_MODEL
    nb, n = dmod_shard.shape

    def body(g_ref, dm_ref, dms_ref, ct_ref, sum_ref, gw_ref, gb_ref):
        total = g_ref[0]
        for dev in range(1, N_DEV):
            total = total + g_ref[dev]
        sum_ref[...] = total
        gb_ref[...] = _row_sum(dm_ref[...])
        ctv = ct_ref[...]
        cond_t = (ctv * _sigmoid(ctv)).astype(BF16)
        for jb in range(n // COL_CHUNK):
            gw_ref[:, jb * COL_CHUNK:(jb + 1) * COL_CHUNK] = jnp.dot(
                cond_t, dms_ref[:, jb * COL_CHUNK:(jb + 1) * COL_CHUNK].astype(BF16), preferred_element_type=F32)

    return pl.pallas_call(
        body, name=name, grid=(1,),
        in_specs=[_full((N_DEV, SMALL_ROWS, d)), _full((nb, N_MOD * d)), _full((nb, n)), _full((d, nb))],
        out_specs=[_full((SMALL_ROWS, d)), _full((d, n)), _full((1, N_MOD * d))],
        out_shape=[jax.ShapeDtypeStruct((SMALL_ROWS, d), F32), jax.ShapeDtypeStruct((d, n), F32),
                   jax.ShapeDtypeStruct((1, N_MOD * d), F32)],
        compiler_params=_params(("arbitrary",)),
    )(gathered, dmod_all, dmod_shard, c_all_t)


def _row_tile(r, c, budget=1 << 21):
    if r * c * 4 <= budget or r % 16:
        return r
    best = 16
    for tr in range(16, r + 1, 16):
        if r % tr == 0 and tr * c * 4 <= budget:
            best = tr
    return best


def _cast_into(w, chip, col_kind, *, name):
    r, c = w.shape
    tr = _row_tile(r, c)

    def body(chip_ref, w_ref, o_ref):
        o_ref[...] = w_ref[...].astype(BF16)

    if col_kind:
        out_spec = pl.BlockSpec((tr, c), lambda i, chip_ref: (i, chip_ref[0]))
        out_shape = jax.ShapeDtypeStruct((r, c * N_CHIPS), BF16)
    else:
        out_spec = pl.BlockSpec((tr, c), lambda i, chip_ref: (chip_ref[0] * (r // tr) + i, 0))
        out_shape = jax.ShapeDtypeStruct((r * N_CHIPS, c), BF16)
    return _pcall(body, name=name, grid=(r // tr,), in_specs=[pl.BlockSpec((tr, c), lambda i, chip_ref: (i, 0))],
                  out_specs=out_spec, out_shape=out_shape, args=(w,), prefetch=chip)


def _adamw(w, g, m, v, *, name, return_grad=False, comm=None):
    r, c = w.shape
    tr = _row_tile(r, c)
    c1 = 1.0 - ADAM_B1 ** ADAM_STEP
    c2 = 1.0 - ADAM_B2 ** ADAM_STEP
    n_out = 4 if return_grad else 3

    def body(w_ref, g_ref, m_ref, v_ref, d_ref, nm_ref, nv_ref, *g_out):
        gv = g_ref[...]
        m2 = ADAM_B1 * m_ref[...] + (1.0 - ADAM_B1) * gv
        v2 = ADAM_B2 * v_ref[...] + (1.0 - ADAM_B2) * (gv * gv)
        d_ref[...] = -ADAM_LR * ((m2 / c1) / (jnp.sqrt(v2 / c2) + ADAM_EPS) + ADAM_WD * w_ref[...])
        nm_ref[...] = m2
        nv_ref[...] = v2
        for o in g_out:
            o[...] = gv

    spec = pl.BlockSpec((tr, c), lambda i: (i, 0))
    sh = jax.ShapeDtypeStruct((r, c), F32)
    return _pcall(body, name=name, grid=(r // tr,), in_specs=[spec] * 4, out_specs=[spec] * n_out,
                  out_shape=[sh] * n_out, args=(w, g, m, v), comm=comm)


def _adamw_small(ws, gs, ms, vs, *, name):
    n = len(ws)
    c1 = 1.0 - ADAM_B1 ** ADAM_STEP
    c2 = 1.0 - ADAM_B2 ** ADAM_STEP

    def body(*refs):
        w_refs, g_refs, m_refs, v_refs = (refs[k * n:(k + 1) * n] for k in range(4))
        out = refs[4 * n:]
        for p in range(n):
            gv = g_refs[p][...]
            m2 = ADAM_B1 * m_refs[p][...] + (1.0 - ADAM_B1) * gv
            v2 = ADAM_B2 * v_refs[p][...] + (1.0 - ADAM_B2) * (gv * gv)
            out[3 * p][...] = -ADAM_LR * ((m2 / c1) / (jnp.sqrt(v2 / c2) + ADAM_EPS) + ADAM_WD * w_refs[p][...])
            out[3 * p + 1][...] = m2
            out[3 * p + 2][...] = v2

    specs = [_full(a.shape) for a in ws]
    res = pl.pallas_call(
        body, name=name, grid=(1,), in_specs=specs * 4,
        out_specs=[s for s in specs for _ in range(3)],
        out_shape=[jax.ShapeDtypeStruct(a.shape, F32) for a in ws for _ in range(3)],
        compiler_params=_params(("arbitrary",)),
    )(*ws, *gs, *ms, *vs)
    return [tuple(res[3 * p:3 * p + 3]) for p in range(n)]


def _sum_final(pos, s32, recv, *, col_kind, n_shard, name, comm=None):
    def body(pos_ref, s_ref, r_ref, o_ref):
        total = ((s_ref[0] + r_ref[0].astype(F32)) + r_ref[1].astype(F32)) + r_ref[2].astype(F32)
        if col_kind:
            o_ref[0] = total
        else:
            o_ref[...] = total

    if col_kind:
        rows, cols = s32.shape[1], n_shard
        tr = _row_tile(rows, cols)
        own = pl.BlockSpec((1, tr, cols), lambda i, pos: (0, i, 2 * pos[0] + pos[1]))
        out_spec = pl.BlockSpec((1, tr, cols), lambda i, pos: (pos[2], i, 0))
        out_shape = jax.ShapeDtypeStruct((2, rows, cols), F32)
    else:
        rows, cols = n_shard, s32.shape[2]
        tr = _row_tile(rows, cols)
        own = pl.BlockSpec((1, tr, cols), lambda i, pos: (0, (2 * pos[0] + pos[1]) * (rows // tr) + i, 0))
        out_spec = pl.BlockSpec((tr, cols), lambda i, pos: (i, pos[2]))
        out_shape = jax.ShapeDtypeStruct((rows, 2 * cols), F32)
    return _pcall(
        body, name=name, grid=(rows // tr,),
        in_specs=[own, pl.BlockSpec((3, tr, cols), lambda i, pos: (0, i, 0))], out_specs=out_spec,
        out_shape=out_shape, args=(s32, recv), prefetch=pos, comm=comm)


def _position():
    return lax.axis_index("x"), lax.axis_index("y"), lax.axis_index("c")


def _allgather8(x_shard, *, name, comm=None):
    m_per, n = x_shard.shape
    nci, nco = (0, 0) if comm is None else (len(comm.inputs), len(comm.out_shapes))

    def body(*refs):
        x_ref, refs = refs[0], refs[1:]
        cin, refs = refs[:nci], refs[nci:]
        out_ref, refs = refs[0], refs[1:]
        cout, refs = refs[:nco], refs[nco:]
        (send_sems, recv_sems, local_sem), csems = refs[:3], refs[3:]
        x, y, c = _position()
        me, sibling = (x, y, c), (x, y, 1 - c)
        chips = [(1 - x, y), (x, 1 - y), (1 - x, 1 - y)]

        def rows(px, py, pc):
            return out_ref.at[pl.ds((4 * px + 2 * py + pc) * m_per, m_per), :]

        def copy(k, block, to, src=None):
            return pltpu.make_async_remote_copy(
                src_ref=rows(*block) if src is None else src, dst_ref=rows(*block),
                send_sem=send_sems.at[k], recv_sem=recv_sems.at[k], device_id=to, device_id_type=MESH)

        mine = pltpu.make_async_copy(x_ref, rows(*me), local_sem)
        mine.start()
        first = [copy(0, me, sibling, src=x_ref)]
        first += [copy(1 + j, me, (*chip, c), src=x_ref) for j, chip in enumerate(chips)]
        for cp in first:
            cp.start()
        if comm is not None:
            comm.start(cin, cout, csems)
        passed = [copy(4 + j, (*chip, c), sibling) for j, chip in enumerate(chips)]
        for j, chip in enumerate(chips):
            copy(1 + j, (*chip, c), me).wait_recv()
            passed[j].start()
        copy(0, sibling, me).wait_recv()
        for j, chip in enumerate(chips):
            copy(4 + j, (*chip, 1 - c), me).wait_recv()
        for cp in first + passed:
            cp.wait_send()
        mine.wait()
        if comm is not None:
            comm.middle(cin, cout, csems)
            comm.late(cin, cout, csems)
            comm.finish(cin, cout, csems)

    vmem = pl.BlockSpec(memory_space=pltpu.VMEM)
    sems = [pltpu.SemaphoreType.DMA((7,)), pltpu.SemaphoreType.DMA((7,)), pltpu.SemaphoreType.DMA]
    out = jax.ShapeDtypeStruct((N_DEV * m_per, n), x_shard.dtype)
    if comm is None:
        return pl.pallas_call(body, name=name, out_shape=out, in_specs=[vmem], out_specs=vmem,
                              scratch_shapes=sems)(x_shard)
    res = pl.pallas_call(
        body, name=name, out_shape=[out] + list(comm.out_shapes), in_specs=[vmem] + [ANY_SPEC] * nci,
        out_specs=[vmem] + [ANY_SPEC] * nco, scratch_shapes=sems + list(comm.sems),
        input_output_aliases={1 + i: 1 + o for i, o in comm.aliases.items()})(x_shard, *comm.inputs)
    return res[0], list(res[1:])


def _peer_chips(x, y):
    return [(1 - x, y), (x, 1 - y), (1 - x, 1 - y)]


class _GatherJob:
    def __init__(self, pieces):
        self.pieces = pieces
        n_p = len(pieces)
        self.inputs = [p[0] for p in pieces]
        self.out_shapes = [jax.ShapeDtypeStruct(p[0].shape, p[0].dtype) for p in pieces]
        for buf, col_kind, r0, nr in pieces:
            half_rows = buf.shape[0] // (2 if col_kind else 2 * N_CHIPS)
            assert r0 % 16 == 0 and nr % 16 == 0 and nr >= 32 and r0 + nr <= half_rows, (buf.shape, r0, nr)
        self.aliases = {p: p for p in range(n_p)}
        dma = pltpu.SemaphoreType.DMA
        self.sems = [dma((2 * n_p,))] * 4 + [dma((4 * n_p,))] * 2

    def _region(self, cout, p, chip_idx, half, part=None):
        buf, col_kind, r0, nr = self.pieces[p]
        first = -(-nr // 32) * 16
        if part == 0:
            nr = first
        elif part == 1:
            r0, nr = r0 + first, nr - first
        if col_kind:
            n = buf.shape[1] // N_CHIPS
            return cout[p].at[pl.ds(half * (buf.shape[0] // 2) + r0, nr), pl.ds(chip_idx * n, n)]
        n = buf.shape[0] // N_CHIPS
        return cout[p].at[pl.ds(chip_idx * n + half * (n // 2) + r0, nr), :]

    def _copies(self, cout, sems):
        send1, recv1, send2, recv2, fsend, frecv = sems
        x, y, c = _position()
        k = 2 * x + y
        sibling = (x, y, 1 - c)
        x_nbr, y_nbr, diag = _peer_chips(x, y)
        chip_of = lambda ch: 2 * ch[0] + ch[1]

        def remote(region, ssem, rsem, to):
            return pltpu.make_async_remote_copy(src_ref=region, dst_ref=region, send_sem=ssem, recv_sem=rsem,
                                                device_id=to, device_id_type=MESH)

        hop1, arrived1, hop2, arrived2, fwds, fwd_arrived = [], [], [], [], [], []
        for p in range(len(self.pieces)):
            for j, nbr in enumerate((x_nbr, y_nbr)):
                i1 = 2 * p + j
                hop1.append(remote(self._region(cout, p, k, c), send1.at[i1], recv1.at[i1], (*nbr, c)))
                arrived1.append(remote(self._region(cout, p, chip_of(nbr), c), send1.at[i1], recv1.at[i1], (*nbr, c)))
            hop2.append(remote(self._region(cout, p, chip_of(x_nbr), c, 0), send2.at[2 * p], recv2.at[2 * p],
                               (*y_nbr, c)))
            hop2.append(remote(self._region(cout, p, chip_of(y_nbr), c, 1), send2.at[2 * p + 1], recv2.at[2 * p + 1],
                               (*x_nbr, c)))
            arrived2.append(remote(self._region(cout, p, chip_of(diag), c, 0), send2.at[2 * p], recv2.at[2 * p],
                                   (*y_nbr, c)))
            arrived2.append(remote(self._region(cout, p, chip_of(diag), c, 1), send2.at[2 * p + 1],
                                   recv2.at[2 * p + 1], (*x_nbr, c)))
            landed = [(chip_of(x_nbr), None), (chip_of(y_nbr), None), (chip_of(diag), 0), (chip_of(diag), 1)]
            for q, (chip_idx, part) in enumerate(landed):
                i3 = 4 * p + q
                fwds.append(remote(self._region(cout, p, chip_idx, c, part), fsend.at[i3], frecv.at[i3], sibling))
                fwd_arrived.append(remote(self._region(cout, p, chip_idx, 1 - c, part), fsend.at[i3], frecv.at[i3],
                                          sibling))
        return hop1, arrived1, hop2, arrived2, fwds, fwd_arrived

    def start(self, cin, cout, sems):
        for cp in self._copies(cout, sems)[0]:
            cp.start()

    def middle(self, cin, cout, sems):
        _, arrived1, hop2, _, fwds, _ = self._copies(cout, sems)
        for p in range(len(self.pieces)):
            for j in range(2):
                arrived1[2 * p + j].wait_recv()
                hop2[2 * p + j].start()
                fwds[4 * p + j].start()

    def late(self, cin, cout, sems):
        _, _, _, arrived2, fwds, _ = self._copies(cout, sems)
        for p in range(len(self.pieces)):
            for j in range(2):
                arrived2[2 * p + j].wait_recv()
                fwds[4 * p + 2 + j].start()

    def finish(self, cin, cout, sems):
        hop1, _, hop2, _, fwds, fwd_arrived = self._copies(cout, sems)
        for cp in fwd_arrived:
            cp.wait_recv()
        for cp in hop1 + hop2 + fwds:
            cp.wait_send()


class _PairedJob:
    aliases = {}

    def start(self, cin, cout, sems):
        for cp in self._copies(cin, cout, sems):
            cp.start()

    def middle(self, cin, cout, sems):
        pass

    late = middle

    def finish(self, cin, cout, sems):
        copies = self._copies(cin, cout, sems)
        for cp in copies:
            cp.wait_recv()
        for cp in copies:
            cp.wait_send()


class _ExchangeJob(_PairedJob):
    def __init__(self, s16, kinds, sizes):
        self.inputs, self.kinds, self.sizes = list(s16), list(kinds), list(sizes)
        self.out_shapes = [jax.ShapeDtypeStruct((3, s.shape[1], n) if kd else (3, n, s.shape[2]), s.dtype)
                           for s, kd, n in zip(s16, kinds, sizes)]
        self.sems = [pltpu.SemaphoreType.DMA((3 * len(s16),)), pltpu.SemaphoreType.DMA((3 * len(s16),))]

    def _copies(self, cin, cout, sems):
        send_sems, recv_sems = sems
        x, y, c = _position()
        copies = []
        for p, src_ref in enumerate(cin):
            for j, chip in enumerate(_peer_chips(x, y)):
                kk = 2 * chip[0] + chip[1]
                n = self.sizes[p]
                src = src_ref.at[0, :, pl.ds(kk * n, n)] if self.kinds[p] else src_ref.at[0, pl.ds(kk * n, n), :]
                copies.append(pltpu.make_async_remote_copy(
                    src_ref=src, dst_ref=cout[p].at[j], send_sem=send_sems.at[3 * p + j],
                    recv_sem=recv_sems.at[3 * p + j], device_id=(*chip, c), device_id_type=MESH))
        return copies


class _ShareJob:
    def __init__(self, halves):
        self.inputs = list(halves)
        self.out_shapes = [jax.ShapeDtypeStruct(h.shape, h.dtype) for h in halves]
        self.aliases = {p: p for p in range(len(halves))}
        self.sems = [pltpu.SemaphoreType.DMA((len(halves),)), pltpu.SemaphoreType.DMA((len(halves),))]

    def _copies(self, cout, sems, half):
        send_sems, recv_sems = sems
        x, y, c = _position()
        h = c if half == "mine" else 1 - c

        def region(o):
            if len(o.shape) == 3:
                return o.at[h]
            hc = o.shape[1] // 2
            return o.at[:, pl.ds(h * hc, hc)]

        return [pltpu.make_async_remote_copy(
            src_ref=region(o), dst_ref=region(o), send_sem=send_sems.at[p], recv_sem=recv_sems.at[p],
            device_id=(x, y, 1 - c), device_id_type=MESH) for p, o in enumerate(cout)]

    def start(self, cin, cout, sems):
        for cp in self._copies(cout, sems, "mine"):
            cp.start()

    def middle(self, cin, cout, sems):
        pass

    late = middle

    def finish(self, cin, cout, sems):
        for cp in self._copies(cout, sems, "theirs"):
            cp.wait_recv()
        for cp in self._copies(cout, sems, "mine"):
            cp.wait_send()


class _MultiJob:
    def __init__(self, jobs):
        self.jobs = jobs
        self.inputs = [a for j in jobs for a in j.inputs]
        self.out_shapes = [s for j in jobs for s in j.out_shapes]
        self.sems = [s for j in jobs for s in j.sems]
        self.aliases = {}
        i0 = o0 = 0
        for j in jobs:
            for i, o in j.aliases.items():
                self.aliases[i0 + i] = o0 + o
            i0 += len(j.inputs)
            o0 += len(j.out_shapes)

    def _parts(self, cin, cout, sems):
        i0 = o0 = s0 = 0
        for j in self.jobs:
            ni, no, ns = len(j.inputs), len(j.out_shapes), len(j.sems)
            yield j, cin[i0:i0 + ni], cout[o0:o0 + no], sems[s0:s0 + ns]
            i0, o0, s0 = i0 + ni, o0 + no, s0 + ns

    def start(self, cin, cout, sems):
        for j, a, b, s in self._parts(cin, cout, sems):
            j.start(a, b, s)

    def middle(self, cin, cout, sems):
        for j, a, b, s in self._parts(cin, cout, sems):
            j.middle(a, b, s)

    def late(self, cin, cout, sems):
        for j, a, b, s in self._parts(cin, cout, sems):
            j.late(a, b, s)

    def finish(self, cin, cout, sems):
        for j, a, b, s in self._parts(cin, cout, sems):
            j.finish(a, b, s)


def _rope_tables(positions):
    half = ROT_DIM // 2
    inv_freq = jnp.power(jnp.float32(ROPE_THETA), -jnp.arange(0, ROT_DIM, 2, dtype=F32) / ROT_DIM)
    inv_head = jnp.concatenate([inv_freq, inv_freq, jnp.zeros((HEAD_DIM - ROT_DIM,), F32)])
    inv_lane = jnp.concatenate([inv_head] * (LANE // HEAD_DIM))
    ang = positions.astype(F32).reshape(-1)[:, None] * inv_lane[None, :]
    sin = jnp.sin(ang)
    dim = jnp.arange(LANE) % HEAD_DIM
    return jnp.cos(ang), jnp.where(dim < half, -sin, 0.0), jnp.where(dim >= half, sin, 0.0)


def kernel(x, c, positions, w_ada, b_ada, ffn1_w_gate_up, ffn1_w_down, ln1_g, ln1_b, w_in, conv_w, attn_sinks, w_out, ln2_g, ln2_b, ffn2_w_gate_up, ffn2_w_down, ln3_g, ln3_b, loss_target, m_w_ada, m_b_ada, m_ffn1_w_gate_up, m_ffn1_w_down, m_ln1_g, m_ln1_b, m_w_in, m_conv_w, m_attn_sinks, m_w_out, m_ln2_g, m_ln2_b, m_ffn2_w_gate_up, m_ffn2_w_down, m_ln3_g, m_ln3_b, v_w_ada, v_b_ada, v_ffn1_w_gate_up, v_ffn1_w_down, v_ln1_g, v_ln1_b, v_w_in, v_conv_w, v_attn_sinks, v_w_out, v_ln2_g, v_ln2_b, v_ffn2_w_gate_up, v_ffn2_w_down, v_ln3_g, v_ln3_b):
    d = D_MODEL
    nb, seq, _ = x.shape
    t = nb * seq
    f = ffn1_w_down.shape[1] * N_CHIPS
    ax, ay, ac = _position()
    chip = 2 * ax + ay
    dev = 2 * chip + ac
    pos = jnp.stack([ax, ay, ac]).astype(jnp.int32)

    x2 = x.reshape(t, d)
    tgt2 = loss_target.reshape(t, d)
    ln1 = jnp.concatenate([ln1_g, ln1_b], axis=0)
    ln2 = jnp.concatenate([ln2_g, ln2_b], axis=0)
    ln3 = jnp.concatenate([ln3_g, ln3_b], axis=0)
    sinks = attn_sinks.reshape(N_Q_HEADS)
    cos_t, sa_t, sb_t = _rope_tables(positions)

    gu_cuts = [0, 176, 352, d // 2]
    gu_part = lambda buf, s: (buf, True, gu_cuts[s], gu_cuts[s + 1] - gu_cuts[s])
    chip_arr = jnp.reshape(chip, (1,)).astype(jnp.int32)
    b_gu1 = _cast_into(ffn1_w_gate_up[0], chip_arr, True, name="cast_gu1")

    n_ada = w_ada.shape[2]
    c_all, (b_gu1,) = _allgather8(c.reshape(nb * d // LANE, LANE), name="gather_c", comm=_GatherJob([gu_part(b_gu1, 0)]))
    c_all = c_all.reshape(N_DEV * nb, d)
    b_shard = lax.dynamic_slice(b_ada, (0, chip * n_ada), (1, n_ada))
    later_shards = [(ffn1_w_down[0], False), (w_in[0].T, False), (w_out[0], False), (ffn2_w_gate_up[0], True),
                    (ffn2_w_down[0], False)]
    mod_part, (b_d1, b_in, b_out, b_gu2, b_d2), (b_gu1,) = _ada_fwd(
        c_all, w_ada[0], b_shard, chip_arr, later_shards, name="ada_fwd", comm=_GatherJob([gu_part(b_gu1, 1)]))
    conv_rows = jnp.pad(conv_w[0], ((0, 5), (0, n_ada - conv_w.shape[2])))
    part = jnp.concatenate([mod_part, conv_rows], axis=0)
    parts, (wgu1,) = _allgather8(part, name="gather_mod", comm=_GatherJob([gu_part(b_gu1, 2)]))
    parts = parts.reshape(N_DEV, N_DEV * nb + 8, n_ada)
    mod_all = jnp.concatenate([parts[2 * k, :N_DEV * nb, :] for k in range(N_CHIPS)], axis=1)
    mod = lax.dynamic_slice(mod_all, (dev * nb, 0), (nb, N_MOD * d)).reshape(nb, N_MOD, d)
    cw_full = jnp.concatenate([parts[2 * k, N_DEV * nb:, :conv_w.shape[2]] for k in range(N_CHIPS)], axis=1)

    n_gu, n_d, n_in, n_out = (ffn1_w_gate_up.shape[2], ffn1_w_down.shape[1], w_in.shape[2], w_out.shape[1])

    def whole(buf, col_kind):
        return (buf, col_kind, 0, buf.shape[0] // (2 if col_kind else 2 * N_CHIPS))

    (h1, a1, dact1), (wd1, wout) = _ffn_up(x2, ln1, mod, wgu1, seq=seq, sc_idx=1, sh_idx=0, use_ln=False,
                                         name="ffn1_up", comm=_GatherJob([whole(b_d1, False), whole(b_out, False)]))
    (f1, xhat1, rstd1), (win_t,) = _ffn_down_ln(a1, wd1, x2, ln1, mod, seq=seq, gate_idx=2, use_ln=False,
                                                name="ffn1_down", comm=_GatherJob([whole(b_in, False)]))
    (h2, q, k, v, ubc), (b_gu2,) = _in_proj(
        xhat1, ln1, mod, win_t, cos_t, sa_t, sb_t, seq=seq, sc_idx=4, sh_idx=3, name="in_proj",
        comm=_GatherJob([gu_part(b_gu2, 0)]))
    attn, (b_gu2,) = _attention(q, k, v, sinks, seq=seq, name="attention", comm=_GatherJob([gu_part(b_gu2, 1)]))
    (mixin, mix, xhat2, rstd2), (wgu2,) = _out_proj(
        attn, ubc, cw_full, wout, xhat1, ln1, mod, seq=seq, gate_idx=5, name="out_proj",
        comm=_GatherJob([gu_part(b_gu2, 2)]))
    (h3, a3, dact3), (wd2,) = _ffn_up(xhat2, ln2, mod, wgu2, seq=seq, sc_idx=7, sh_idx=6, use_ln=True, name="ffn2_up",
                                    comm=_GatherJob([whole(b_d2, False)]))
    dr3, df3, loss_cols, dln3g, dln3b, dgate3 = _ffn_down_loss(
        a3, wd2, xhat2, ln2, mod, ln3, tgt2, seq=seq, gate_idx=8, name="ffn2_down_loss")

    dgu3 = _ffn_bwd_act(df3, wd2, dact3, seq=seq, name="ffn2_bwd_act")
    s32_d2, s16_d2 = _grad_chip_sum(pos, a3, df3, half_on_rows=False, name="grad_wd2")
    (s32_gu2, s16_gu2), (recv_d2,) = _grad_chip_sum(pos, h3, dgu3, half_on_rows=True, name="grad_wgu2",
                                                    comm=_ExchangeJob([s16_d2], [False], [n_d]))
    (dr2, dmix, dsc3, dsh3, dgate2, dln2g, dln2b), (recv_gu2,) = _bwd_in(
        dgu3, wgu2, dr3, xhat2, rstd2, ln2, mod, mix, seq=seq, w_is_nt=True, sc_idx=7, gate_idx=5,
        branch_scale=1.0, final=False, name="ffn2_bwd_in", comm=_ExchangeJob([s16_gu2], [True], [n_gu]))
    s32_out, s16_out = _grad_chip_sum(pos, mixin, dmix, half_on_rows=False, name="grad_wout")
    dmixin = _matmul_nt_bf16(dmix, wout, seq=seq, name="out_proj_bwd")
    (dq, dkp, dkc, dvp, dvc, dsink), (recv_out,) = _attention_bwd(
        q, k, v, dmixin, sinks, seq=seq, name="attention_bwd", comm=_ExchangeJob([s16_out], [False], [n_out]))
    dproj, dcw = _mix_bwd_assemble(
        dq, dkp, dkc, dvp, dvc, cos_t, sa_t, sb_t, dmixin, ubc, cw_full, seq=seq, name="mix_bwd")
    s32_in, s16_in = _grad_chip_sum(pos, dproj, h2, half_on_rows=False, name="grad_win")
    (dr1, df1, dsc2, dsh2, dgate1, dln1g, dln1b), (recv_in,) = _bwd_in(
        dproj, win_t, dr2, xhat1, rstd1, ln1, mod, f1, seq=seq, w_is_nt=False, sc_idx=4, gate_idx=2,
        branch_scale=0.5, final=False, name="in_proj_bwd", comm=_ExchangeJob([s16_in], [False], [n_in]))
    s32_d1, s16_d1 = _grad_chip_sum(pos, a1, df1, half_on_rows=False, name="grad_wd1")
    dgu1, (recv_d1,) = _ffn_bwd_act(df1, wd1, dact1, seq=seq, name="ffn1_bwd_act",
                                    comm=_ExchangeJob([s16_d1], [False], [n_d]))
    s32_gu1, s16_gu1 = _grad_chip_sum(pos, h1, dgu1, half_on_rows=True, name="grad_wgu1")

    def final_half(s32_, recv_, col_kind, n_shard, name_):
        return _sum_final(pos, s32_, recv_, col_kind=col_kind, n_shard=n_shard, name=name_)

    early = [final_half(s32_gu2, recv_gu2, True, n_gu, "sum_final_gu2"),
             final_half(s32_d2, recv_d2, False, n_d, "sum_final_d2"),
             final_half(s32_out, recv_out, False, n_out, "sum_final_out"),
             final_half(s32_in, recv_in, False, n_in, "sum_final_in"),
             final_half(s32_d1, recv_d1, False, n_d, "sum_final_d1")]
    (grad_x, dsc1, dsh1), (recv_gu1, full_gu2, full_d2, full_out, full_in, full_d1) = _bwd_in(
        dgu1, wgu1, dr1, x2, None, None, mod, None, seq=seq, w_is_nt=True, sc_idx=1, gate_idx=None,
        branch_scale=None, final=True, name="ffn1_bwd_in",
        comm=_MultiJob([_ExchangeJob([s16_gu1], [True], [n_gu]), _ShareJob(early)]))
    late = [final_half(s32_gu1, recv_gu1, True, n_gu, "sum_final_gu1")]

    dmod = jnp.concatenate([dsh1, dsc1, dgate1, dsh2, dsc2, dgate2, dsh3, dsc3, dgate3], axis=1)
    loss_row = jnp.sum(loss_cols, axis=1, keepdims=True) * (0.5 / d)
    lane_row = lambda a: jnp.pad(a, ((0, 0), (0, d - a.shape[1])))
    block = jnp.concatenate(
        [dmod.reshape(nb * N_MOD, d), dln1g, dln1b, dln2g, dln2b, dln3g, dln3b,
         lane_row(dcw[0:3, :]), lane_row(dsink[:, 0:1].reshape(1, N_Q_HEADS)), lane_row(loss_row)], axis=0)
    block = jnp.pad(block, ((0, SMALL_ROWS - block.shape[0]), (0, 0)))
    gathered, (full_gu1,) = _allgather8(block, name="gather_small", comm=_ShareJob(late))
    gathered = gathered.reshape(N_DEV, SMALL_ROWS, d)
    dmod_all = gathered[:, :nb * N_MOD, :].reshape(N_DEV * nb, N_MOD * d)
    dmod_shard = lax.dynamic_slice(dmod_all, (0, chip * n_ada), (N_DEV * nb, n_ada))
    small, g_w_ada, g_b_ada = _small_finish(gathered, dmod_all, dmod_shard, c_all.T, name="small_finish")
    r0 = nb * N_MOD
    loss = small[r0 + 10, 0]
    g_ln = [small[r0 + i:r0 + i + 1, :] for i in range(6)]
    g_cw_full = small[r0 + 6:r0 + 9, :CONV_WIDTH]
    g_conv = lax.dynamic_slice(g_cw_full, (0, chip * conv_w.shape[2]), (3, conv_w.shape[2]))
    g_sinks = small[r0 + 9:r0 + 10, :N_Q_HEADS]

    def flat2(a):
        return a.reshape(-1, a.shape[-1])

    def unhalve(a):
        return a.reshape(2 * a.shape[1], a.shape[2])

    results = {}

    def adamw(name_, w_, g_, m_, v_, shared=False):
        g2 = flat2(g_)
        dl, nm, nv, *g_out = _adamw(flat2(w_), g2, flat2(m_), flat2(v_), name="adamw_" + name_, return_grad=shared)
        results[name_] = tuple(a.reshape(w_.shape) for a in (*g_out, g2)[:1] + (dl, nm, nv))

    adamw("w_ada", w_ada, g_w_ada, m_w_ada, v_w_ada)
    adamw("ffn2_w_gate_up", ffn2_w_gate_up, unhalve(full_gu2), m_ffn2_w_gate_up, v_ffn2_w_gate_up, shared=True)
    adamw("ffn2_w_down", ffn2_w_down, full_d2, m_ffn2_w_down, v_ffn2_w_down, shared=True)
    adamw("w_out", w_out, full_out, m_w_out, v_w_out, shared=True)
    *in_t, g_in_t = _adamw(flat2(w_in).T, full_in, flat2(m_w_in).T, flat2(v_w_in).T, name="adamw_w_in",
                           return_grad=True)
    results["w_in"] = tuple(a.T.reshape(w_in.shape) for a in (g_in_t, *in_t))
    adamw("ffn1_w_gate_up", ffn1_w_gate_up, unhalve(full_gu1), m_ffn1_w_gate_up, v_ffn1_w_gate_up, shared=True)
    adamw("ffn1_w_down", ffn1_w_down, full_d1, m_ffn1_w_down, v_ffn1_w_down, shared=True)
    small_params = [("b_ada", b_ada, g_b_ada, m_b_ada, v_b_ada),
                    ("ln1_g", ln1_g, g_ln[0], m_ln1_g, v_ln1_g), ("ln1_b", ln1_b, g_ln[1], m_ln1_b, v_ln1_b),
                    ("ln2_g", ln2_g, g_ln[2], m_ln2_g, v_ln2_g), ("ln2_b", ln2_b, g_ln[3], m_ln2_b, v_ln2_b),
                    ("ln3_g", ln3_g, g_ln[4], m_ln3_g, v_ln3_g), ("ln3_b", ln3_b, g_ln[5], m_ln3_b, v_ln3_b),
                    ("conv_w", conv_w, g_conv, m_conv_w, v_conv_w),
                    ("attn_sinks", attn_sinks, g_sinks, m_attn_sinks, v_attn_sinks)]
    small_g = [flat2(g_) for _, _, g_, _, _ in small_params]
    small_res = _adamw_small([flat2(w_) for _, w_, _, _, _ in small_params], small_g,
                             [flat2(m_) for _, _, _, m_, _ in small_params],
                             [flat2(v_) for _, _, _, _, v_ in small_params], name="adamw_small")
    for (name_, w_, _, _, _), g2, res in zip(small_params, small_g, small_res):
        results[name_] = tuple(a.reshape(w_.shape) for a in (g2, *res))
    order = ["w_ada", "b_ada", "ffn1_w_gate_up", "ffn1_w_down", "ln1_g", "ln1_b", "w_in", "conv_w", "attn_sinks",
             "w_out", "ln2_g", "ln2_b", "ffn2_w_gate_up", "ffn2_w_down", "ln3_g", "ln3_b"]
    return (loss, grad_x.reshape(x.shape), *[results[n_][0] for n_ in order], *[results[n_][1] for n_ in order],
            *[results[n_][2] for n_ in order], *[results[n_][3] for n_ in order])
```
